```python
import math
import jax, jax.numpy as jnp
from jax import lax
import numpy as np

D_MODEL = 1024
BATCH = 8
SEQ = 4096
DEPTH = 1

ATT_PATTERNS = ((128, 1), (512, 4), (2048, 16))
ATT_GROUPS = len(ATT_PATTERNS)
ATT_HEADS = 8
ATT_HEAD_DIM = 64
ATT_WIDTH = ATT_HEADS * ATT_HEAD_DIM
ATT_BLOCK = 128
RWKV_WIDTH = D_MODEL
RWKV_HEAD_DIM = 64
RWKV_HEADS = RWKV_WIDTH // RWKV_HEAD_DIM
DECAY_LORA = 64
AAA_LORA = 64
GATE_LORA = 160
D_FF = 2816
CONV_WIDTH = 3
N_BRANCHES = 2
ATT_IN = ATT_GROUPS * 3 * ATT_WIDTH
RWKV_IN = 3 * RWKV_WIDTH + DECAY_LORA + AAA_LORA + GATE_LORA
GATE_IN = N_BRANCHES * D_MODEL
N_IN = ATT_IN + RWKV_IN + GATE_IN
RMS_EPS = 1e-6
GN_EPS = 64e-5

kernel_name = 'hybrid_dilated_attn_rwkv7_convffn_adaln'


def rms_norm(x, w):
    xf = x.astype(jnp.float32)
    y = xf * lax.rsqrt(jnp.mean(xf * xf, axis=-1, keepdims=True) + RMS_EPS)
    return (y * w).astype(x.dtype)


def dilated_window_attention(q, k, v, window, dilation):
    b, s, h, e = q.shape
    back = window // dilation
    sub_len = -(-s // dilation)
    n_blk = -(-sub_len // ATT_BLOCK)
    s_pad = n_blk * ATT_BLOCK * dilation

    def to_blocks(t):
        t = jnp.pad(t, ((0, 0), (0, s_pad - s), (0, 0), (0, 0)))
        return t.reshape(b, n_blk, ATT_BLOCK, dilation, h, e)

    def with_prev(t):
        prev = jnp.concatenate([jnp.zeros_like(t[:, :1]), t[:, :-1]], axis=1)
        return jnp.concatenate([prev, t], axis=2)

    qb = to_blocks(q)
    kc = with_prev(to_blocks(k))
    vc = with_prev(to_blocks(v))
    scores = jnp.einsum('bnqrhe,bnkrhe->bnrhqk', qb, kc).astype(jnp.float32) * (e ** -0.5)
    qi = jnp.arange(ATT_BLOCK)[:, None]
    kj = jnp.arange(2 * ATT_BLOCK)[None, :]
    dist = qi + ATT_BLOCK - kj
    kpos = jnp.arange(n_blk)[:, None] * ATT_BLOCK + kj - ATT_BLOCK
    valid = ((dist >= 0) & (dist <= back))[None] & (kpos >= 0)[:, None, :]
    scores = jnp.where(valid[None, :, None, None], scores, -jnp.inf)
    m = jnp.max(scores, axis=-1, keepdims=True)
    p = jnp.exp(scores - m)
    den = jnp.sum(p, axis=-1)
    num = jnp.einsum('bnrhqk,bnkrhe->bnqrhe', p, vc.astype(jnp.float32))
    den_q = jnp.transpose(den, (0, 1, 4, 2, 3))
    out = (num / den_q[..., None]).reshape(b, s_pad, h, e)[:, :s]
    lse = (jnp.transpose(m[..., 0], (0, 1, 4, 2, 3)) + jnp.log(den_q)).reshape(b, s_pad, h)[:, :s]
    return out, lse


def attention_mixer(z):
    b, s, _ = z.shape
    qkv = z.reshape(b, s, ATT_GROUPS, 3, ATT_HEADS, ATT_HEAD_DIM)
    outs, lses = [], []
    for g, (window, dilation) in enumerate(ATT_PATTERNS):
        o, l = dilated_window_attention(qkv[:, :, g, 0], qkv[:, :, g, 1], qkv[:, :, g, 2], window, dilation)
        outs.append(o)
        lses.append(l)
    wts = jax.nn.softmax(jnp.stack(lses), axis=0)
    out = jnp.sum(wts[..., None] * jnp.stack(outs), axis=0)
    return out.reshape(b, s, ATT_WIDTH).astype(z.dtype)


def rwkv7_mixer(z, mu, w0, w2, a0, a2, g2, k_k, k_a, r_k, lnx_w, lnx_b):
    f32 = jnp.float32
    b, s, _ = z.shape
    z_prev = jnp.pad(z, ((0, 0), (1, 0), (0, 0)))[:, :s]
    z = z + (z_prev - z) * mu
    c = RWKV_WIDTH
    r, k, v, w_low, a_low, g_low = jnp.split(
        z, [c, 2 * c, 3 * c, 3 * c + DECAY_LORA, 3 * c + DECAY_LORA + AAA_LORA], axis=-1)
    w_log = -jax.nn.softplus(-(w0 + jnp.tanh(w_low) @ w2).astype(f32)) - 0.5
    decay = jnp.exp(-jnp.exp(w_log))
    a = jax.nn.sigmoid((a0 + a_low @ a2).astype(f32))
    g = jax.nn.sigmoid(g_low) @ g2
    k_mod = k.astype(f32) * (1.0 + (a - 1.0) * k_a)

    def heads(t):
        return t.astype(f32).reshape(b, s, RWKV_HEADS, RWKV_HEAD_DIM)

    kk = heads(k * k_k)
    kk = kk / jnp.maximum(jnp.sqrt(jnp.sum(kk * kk, axis=-1, keepdims=True)), 1e-12)
    r_h, k_h, v_h, w_h, a_h = heads(r), heads(k_mod), heads(v), heads(decay), heads(a)

    def step(state, inp):
        r_t, w_t, k_t, v_t, aa_t, bb_t = inp
        sa = jnp.einsum('bhvk,bhk->bhv', state, aa_t)
        state = state * w_t[:, :, None, :] + sa[..., None] * bb_t[:, :, None, :] + v_t[..., None] * k_t[:, :, None, :]
        return state, jnp.einsum('bhvk,bhk->bhv', state, r_t)

    tm = lambda t: jnp.swapaxes(t, 0, 1)
    state0 = jnp.zeros((b, RWKV_HEADS, RWKV_HEAD_DIM, RWKV_HEAD_DIM), f32)
    _, y = lax.scan(step, state0, (tm(r_h), tm(w_h), tm(k_h), tm(v_h), tm(-kk), tm(kk * a_h)))
    y = tm(y)
    mean = jnp.mean(y, axis=-1, keepdims=True)
    var = jnp.mean(jnp.square(y - mean), axis=-1, keepdims=True)
    y = ((y - mean) * lax.rsqrt(var + GN_EPS)).reshape(b, s, c) * lnx_w + lnx_b
    bonus = (jnp.sum(r_h * k_h * r_k, axis=-1, keepdims=True) * v_h).reshape(b, s, c)
    return ((y + bonus) * g).astype(z.dtype)


def conv_ffn(h, w_up, conv_w, conv_b, w_down):
    s = h.shape[1]
    u = h @ w_up
    up = jnp.pad(u, ((0, 0), (CONV_WIDTH - 1, 0), (0, 0)))
    u = conv_b + sum(conv_w[j] * up[:, j:j + s] for j in range(CONV_WIDTH))
    gate, val = jnp.split(u, 2, axis=-1)
    return (jax.nn.silu(gate) * val) @ w_down


def _fwd_setup_inputs(seed: int = 0) -> dict:
    key = jax.random.key(seed)
    ks = iter(jax.random.split(key, 32))
    f32 = jnp.float32
    L, D, C = DEPTH, D_MODEL, RWKV_WIDTH

    def nrm(shape, scale):
        return jax.random.normal(next(ks), shape, f32) * scale

    ramp = (jnp.arange(C, dtype=f32) / (C - 1)) ** 0.85
    inputs = {}
    inputs['x'] = nrm((BATCH, SEQ, D), 1.0)
    inputs['c'] = nrm((BATCH, D), 1.0)
    inputs['w_ada'] = nrm((L, D, 6 * D), 0.3 * D ** -0.5)
    inputs['b_ada'] = nrm((L, 6 * D), 0.02)
    inputs['norm1_w'] = 1.0 + nrm((L, D), 0.05)
    inputs['w_in'] = nrm((L, D, N_IN), D ** -0.5)
    inputs['b_gate'] = nrm((L, GATE_IN), 0.1)
    inputs['mu_shift'] = jax.random.uniform(next(ks), (L, RWKV_IN), f32)
    inputs['w0'] = -6.5 + 5.0 * ramp + nrm((L, C), 0.1)
    inputs['w2'] = nrm((L, DECAY_LORA, C), 0.1 * DECAY_LORA ** -0.5)
    inputs['a0'] = nrm((L, C), 0.1)
    inputs['a2'] = nrm((L, AAA_LORA, C), AAA_LORA ** -0.5)
    inputs['g2'] = nrm((L, GATE_LORA, C), GATE_LORA ** -0.5)
    inputs['k_k'] = 0.85 + nrm((L, C), 0.05)
    inputs['k_a'] = 1.0 + nrm((L, C), 0.05)
    inputs['r_k'] = nrm((L, RWKV_HEADS, RWKV_HEAD_DIM), 0.1)
    inputs['lnx_w'] = 1.0 + nrm((L, C), 0.05)
    inputs['lnx_b'] = nrm((L, C), 0.02)
    inputs['w_att_out'] = nrm((L, ATT_WIDTH, D), ATT_WIDTH ** -0.5)
    inputs['w_rwkv_out'] = nrm((L, C, D), C ** -0.5)
    inputs['w_o'] = nrm((L, D, D), D ** -0.5)
    inputs['norm2_w'] = 1.0 + nrm((L, D), 0.05)
    inputs['w_up'] = nrm((L, D, 2 * D_FF), D ** -0.5)
    inputs['conv_w'] = nrm((L, CONV_WIDTH, 2 * D_FF), CONV_WIDTH ** -0.5)
    inputs['conv_b'] = nrm((L, 2 * D_FF), 0.02)
    inputs['w_down'] = nrm((L, D_FF, D), D_FF ** -0.5)
    inputs['norm_f_w'] = 1.0 + nrm((D,), 0.05)
    return inputs


def _fwd_reference(x, c, w_ada, b_ada, norm1_w, w_in, b_gate, mu_shift, w0, w2, a0, a2, g2, k_k, k_a, r_k,
              lnx_w, lnx_b, w_att_out, w_rwkv_out, w_o, norm2_w, w_up, conv_w, conv_b, w_down, norm_f_w):
    for l in range(DEPTH):
        ada = (c @ w_ada[l] + b_ada[l])[:, None, :]
        sh1, sc1, gt1, sh2, sc2, gt2 = jnp.split(ada, 6, axis=-1)
        h = rms_norm(x, norm1_w[l]) * (1.0 + sc1) + sh1
        proj = h @ w_in[l]
        att_in, rwkv_in, gate_in = jnp.split(proj, [ATT_IN, ATT_IN + RWKV_IN], axis=-1)
        y_att = attention_mixer(att_in) @ w_att_out[l]
        y_rwkv = rwkv7_mixer(rwkv_in, mu_shift[l], w0[l], w2[l], a0[l], a2[l], g2[l], k_k[l], k_a[l],
                             r_k[l], lnx_w[l], lnx_b[l]) @ w_rwkv_out[l]
        g_att, g_rwkv = jnp.split(jax.nn.sigmoid(gate_in + b_gate[l]), N_BRANCHES, axis=-1)
        x = x + gt1 * ((g_att * y_att + g_rwkv * y_rwkv) @ w_o[l])
        h = rms_norm(x, norm2_w[l]) * (1.0 + sc2) + sh2
        x = x + gt2 * conv_ffn(h, w_up[l], conv_w[l], conv_b[l], w_down[l])
    return rms_norm(x, norm_f_w)


import jax as _jax
import jax.numpy as _jnp

TWIN_FORMAT = 'train_step'
FWD_PARAMS = ['x', 'c', 'w_ada', 'b_ada', 'norm1_w', 'w_in', 'b_gate', 'mu_shift', 'w0', 'w2', 'a0', 'a2', 'g2', 'k_k', 'k_a', 'r_k', 'lnx_w', 'lnx_b', 'w_att_out', 'w_rwkv_out', 'w_o', 'norm2_w', 'w_up', 'conv_w', 'conv_b', 'w_down', 'norm_f_w']
TWIN_WEIGHTS = ['w_ada', 'b_ada', 'norm1_w', 'w_in', 'b_gate', 'mu_shift', 'w0', 'w2', 'a0', 'a2', 'g2', 'k_k', 'k_a', 'r_k', 'lnx_w', 'lnx_b', 'w_att_out', 'w_rwkv_out', 'w_o', 'norm2_w', 'w_up', 'conv_w', 'conv_b', 'w_down', 'norm_f_w']
TWIN_DIFF_INPUT = 'x'
TWIN_INPUTS = ['x', 'c', 'w_ada', 'b_ada', 'norm1_w', 'w_in', 'b_gate', 'mu_shift', 'w0', 'w2', 'a0', 'a2', 'g2', 'k_k', 'k_a', 'r_k', 'lnx_w', 'lnx_b', 'w_att_out', 'w_rwkv_out', 'w_o', 'norm2_w', 'w_up', 'conv_w', 'conv_b', 'w_down', 'norm_f_w', 'loss_target', 'm_w_ada', 'm_b_ada', 'm_norm1_w', 'm_w_in', 'm_b_gate', 'm_mu_shift', 'm_w0', 'm_w2', 'm_a0', 'm_a2', 'm_g2', 'm_k_k', 'm_k_a', 'm_r_k', 'm_lnx_w', 'm_lnx_b', 'm_w_att_out', 'm_w_rwkv_out', 'm_w_o', 'm_norm2_w', 'm_w_up', 'm_conv_w', 'm_conv_b', 'm_w_down', 'm_norm_f_w', 'v_w_ada', 'v_b_ada', 'v_norm1_w', 'v_w_in', 'v_b_gate', 'v_mu_shift', 'v_w0', 'v_w2', 'v_a0', 'v_a2', 'v_g2', 'v_k_k', 'v_k_a', 'v_r_k', 'v_lnx_w', 'v_lnx_b', 'v_w_att_out', 'v_w_rwkv_out', 'v_w_o', 'v_norm2_w', 'v_w_up', 'v_conv_w', 'v_conv_b', 'v_w_down', 'v_norm_f_w']
TWIN_OUTPUTS = ['loss', 'grad_x', 'grad_w_ada', 'grad_b_ada', 'grad_norm1_w', 'grad_w_in', 'grad_b_gate', 'grad_mu_shift', 'grad_w0', 'grad_w2', 'grad_a0', 'grad_a2', 'grad_g2', 'grad_k_k', 'grad_k_a', 'grad_r_k', 'grad_lnx_w', 'grad_lnx_b', 'grad_w_att_out', 'grad_w_rwkv_out', 'grad_w_o', 'grad_norm2_w', 'grad_w_up', 'grad_conv_w', 'grad_conv_b', 'grad_w_down', 'grad_norm_f_w', 'delta_w_ada', 'delta_b_ada', 'delta_norm1_w', 'delta_w_in', 'delta_b_gate', 'delta_mu_shift', 'delta_w0', 'delta_w2', 'delta_a0', 'delta_a2', 'delta_g2', 'delta_k_k', 'delta_k_a', 'delta_r_k', 'delta_lnx_w', 'delta_lnx_b', 'delta_w_att_out', 'delta_w_rwkv_out', 'delta_w_o', 'delta_norm2_w', 'delta_w_up', 'delta_conv_w', 'delta_conv_b', 'delta_w_down', 'delta_norm_f_w', 'new_m_w_ada', 'new_m_b_ada', 'new_m_norm1_w', 'new_m_w_in', 'new_m_b_gate', 'new_m_mu_shift', 'new_m_w0', 'new_m_w2', 'new_m_a0', 'new_m_a2', 'new_m_g2', 'new_m_k_k', 'new_m_k_a', 'new_m_r_k', 'new_m_lnx_w', 'new_m_lnx_b', 'new_m_w_att_out', 'new_m_w_rwkv_out', 'new_m_w_o', 'new_m_norm2_w', 'new_m_w_up', 'new_m_conv_w', 'new_m_conv_b', 'new_m_w_down', 'new_m_norm_f_w', 'new_v_w_ada', 'new_v_b_ada', 'new_v_norm1_w', 'new_v_w_in', 'new_v_b_gate', 'new_v_mu_shift', 'new_v_w0', 'new_v_w2', 'new_v_a0', 'new_v_a2', 'new_v_g2', 'new_v_k_k', 'new_v_k_a', 'new_v_r_k', 'new_v_lnx_w', 'new_v_lnx_b', 'new_v_w_att_out', 'new_v_w_rwkv_out', 'new_v_w_o', 'new_v_norm2_w', 'new_v_w_up', 'new_v_conv_w', 'new_v_conv_b', 'new_v_w_down', 'new_v_norm_f_w']
TWIN_LEAF_KINDS = {'loss': 'loss', 'grad_x': 'grad_x', 'grad_w_ada': 'grad_w', 'grad_b_ada': 'grad_w', 'grad_norm1_w': 'grad_w', 'grad_w_in': 'grad_w', 'grad_b_gate': 'grad_w', 'grad_mu_shift': 'grad_w', 'grad_w0': 'grad_w', 'grad_w2': 'grad_w', 'grad_a0': 'grad_w', 'grad_a2': 'grad_w', 'grad_g2': 'grad_w', 'grad_k_k': 'grad_w', 'grad_k_a': 'grad_w', 'grad_r_k': 'grad_w', 'grad_lnx_w': 'grad_w', 'grad_lnx_b': 'grad_w', 'grad_w_att_out': 'grad_w', 'grad_w_rwkv_out': 'grad_w', 'grad_w_o': 'grad_w', 'grad_norm2_w': 'grad_w', 'grad_w_up': 'grad_w', 'grad_conv_w': 'grad_w', 'grad_conv_b': 'grad_w', 'grad_w_down': 'grad_w', 'grad_norm_f_w': 'grad_w', 'delta_w_ada': 'delta_w', 'delta_b_ada': 'delta_w', 'delta_norm1_w': 'delta_w', 'delta_w_in': 'delta_w', 'delta_b_gate': 'delta_w', 'delta_mu_shift': 'delta_w', 'delta_w0': 'delta_w', 'delta_w2': 'delta_w', 'delta_a0': 'delta_w', 'delta_a2': 'delta_w', 'delta_g2': 'delta_w', 'delta_k_k': 'delta_w', 'delta_k_a': 'delta_w', 'delta_r_k': 'delta_w', 'delta_lnx_w': 'delta_w', 'delta_lnx_b': 'delta_w', 'delta_w_att_out': 'delta_w', 'delta_w_rwkv_out': 'delta_w', 'delta_w_o': 'delta_w', 'delta_norm2_w': 'delta_w', 'delta_w_up': 'delta_w', 'delta_conv_w': 'delta_w', 'delta_conv_b': 'delta_w', 'delta_w_down': 'delta_w', 'delta_norm_f_w': 'delta_w', 'new_m_w_ada': 'new_m', 'new_m_b_ada': 'new_m', 'new_m_norm1_w': 'new_m', 'new_m_w_in': 'new_m', 'new_m_b_gate': 'new_m', 'new_m_mu_shift': 'new_m', 'new_m_w0': 'new_m', 'new_m_w2': 'new_m', 'new_m_a0': 'new_m', 'new_m_a2': 'new_m', 'new_m_g2': 'new_m', 'new_m_k_k': 'new_m', 'new_m_k_a': 'new_m', 'new_m_r_k': 'new_m', 'new_m_lnx_w': 'new_m', 'new_m_lnx_b': 'new_m', 'new_m_w_att_out': 'new_m', 'new_m_w_rwkv_out': 'new_m', 'new_m_w_o': 'new_m', 'new_m_norm2_w': 'new_m', 'new_m_w_up': 'new_m', 'new_m_conv_w': 'new_m', 'new_m_conv_b': 'new_m', 'new_m_w_down': 'new_m', 'new_m_norm_f_w': 'new_m', 'new_v_w_ada': 'new_v', 'new_v_b_ada': 'new_v', 'new_v_norm1_w': 'new_v', 'new_v_w_in': 'new_v', 'new_v_b_gate': 'new_v', 'new_v_mu_shift': 'new_v', 'new_v_w0': 'new_v', 'new_v_w2': 'new_v', 'new_v_a0': 'new_v', 'new_v_a2': 'new_v', 'new_v_g2': 'new_v', 'new_v_k_k': 'new_v', 'new_v_k_a': 'new_v', 'new_v_r_k': 'new_v', 'new_v_lnx_w': 'new_v', 'new_v_lnx_b': 'new_v', 'new_v_w_att_out': 'new_v', 'new_v_w_rwkv_out': 'new_v', 'new_v_w_o': 'new_v', 'new_v_norm2_w': 'new_v', 'new_v_w_up': 'new_v', 'new_v_conv_w': 'new_v', 'new_v_conv_b': 'new_v', 'new_v_w_down': 'new_v', 'new_v_norm_f_w': 'new_v'}


def _forward(args):
    return _fwd_reference(*[args[k] for k in FWD_PARAMS])


def _output_shape():
    out = _jax.eval_shape(lambda: _forward(_fwd_setup_inputs(0)))
    return out.shape, out.dtype

N_MICROBATCH = 1
ADAM_LR = 0.001
ADAM_B1 = 0.9
ADAM_B2 = 0.999
ADAM_EPS = 1e-08
ADAM_WD = 0.01
ADAM_STEP = 10
PER_EXAMPLE_BATCH_AXIS = {'x': 0, 'c': 0, 'loss_target': 0}
SHARED_INPUTS = []
_WEIGHT_DTYPES = {'w_ada': _jnp.float32, 'b_ada': _jnp.float32, 'norm1_w': _jnp.float32, 'w_in': _jnp.float32, 'b_gate': _jnp.float32, 'mu_shift': _jnp.float32, 'w0': _jnp.float32, 'w2': _jnp.float32, 'a0': _jnp.float32, 'a2': _jnp.float32, 'g2': _jnp.float32, 'k_k': _jnp.float32, 'k_a': _jnp.float32, 'r_k': _jnp.float32, 'lnx_w': _jnp.float32, 'lnx_b': _jnp.float32, 'w_att_out': _jnp.float32, 'w_rwkv_out': _jnp.float32, 'w_o': _jnp.float32, 'norm2_w': _jnp.float32, 'w_up': _jnp.float32, 'conv_w': _jnp.float32, 'conv_b': _jnp.float32, 'w_down': _jnp.float32, 'norm_f_w': _jnp.float32}
MOMENT_SCALE = {'w_ada': 9.341117e-02, 'b_ada': 1.002065e-01, 'norm1_w': 3.671224e-02, 'w_in': 1.197897e-02, 'b_gate': 5.451807e-03, 'mu_shift': 3.162792e-02, 'w0': 1.160874e-02, 'w2': 2.295053e-03, 'a0': 7.018198e-03, 'a2': 6.806741e-03, 'g2': 1.725931e-02, 'k_k': 3.254307e-02, 'k_a': 2.904764e-02, 'r_k': 5.735245e-02, 'lnx_w': 1.652477e-02, 'lnx_b': 2.347615e-02, 'w_att_out': 9.755699e-03, 'w_rwkv_out': 1.763202e-02, 'w_o': 2.021554e-02, 'norm2_w': 5.641170e-02, 'w_up': 2.425293e-02, 'conv_w': 2.422406e-02, 'conv_b': 2.432368e-02, 'w_down': 3.977109e-02, 'norm_f_w': 3.202379e+01}


def _to_microbatches(a, axis):
    t = _jnp.moveaxis(a, axis, 0)
    t = t.reshape((N_MICROBATCH, t.shape[0] // N_MICROBATCH) + t.shape[1:])
    return _jnp.moveaxis(t, 1, axis + 1)


def setup_inputs(seed: int = 0) -> dict:
    inp = _fwd_setup_inputs(seed)
    key = _jax.random.fold_in(_jax.random.key(seed), 7919)
    shape, _ = _output_shape()
    out = dict(inp)
    out["loss_target"] = _jax.random.normal(_jax.random.fold_in(key, 0), shape, _jnp.float32)
    for i, name in enumerate(TWIN_WEIGHTS):
        w = inp[name].astype(_jnp.float32)
        if MOMENT_SCALE is None:
            s = _jnp.sqrt(_jnp.mean(_jnp.square(w)) + 1e-30)
        else:
            s = MOMENT_SCALE[name]
        km, kv = _jax.random.split(_jax.random.fold_in(key, i + 1))
        out[name] = w
        out["m_" + name] = s * _jax.random.normal(km, w.shape, _jnp.float32)
        out["v_" + name] = (s * s) * _jax.random.uniform(kv, w.shape, _jnp.float32, 0.5, 1.5)
    if N_MICROBATCH > 1:
        for name, axis in PER_EXAMPLE_BATCH_AXIS.items():
            out[name] = _to_microbatches(out[name], axis)
    return {'x': out['x'], 'c': out['c'], 'w_ada': out['w_ada'], 'b_ada': out['b_ada'], 'norm1_w': out['norm1_w'], 'w_in': out['w_in'], 'b_gate': out['b_gate'], 'mu_shift': out['mu_shift'], 'w0': out['w0'], 'w2': out['w2'], 'a0': out['a0'], 'a2': out['a2'], 'g2': out['g2'], 'k_k': out['k_k'], 'k_a': out['k_a'], 'r_k': out['r_k'], 'lnx_w': out['lnx_w'], 'lnx_b': out['lnx_b'], 'w_att_out': out['w_att_out'], 'w_rwkv_out': out['w_rwkv_out'], 'w_o': out['w_o'], 'norm2_w': out['norm2_w'], 'w_up': out['w_up'], 'conv_w': out['conv_w'], 'conv_b': out['conv_b'], 'w_down': out['w_down'], 'norm_f_w': out['norm_f_w'], 'loss_target': out['loss_target'], 'm_w_ada': out['m_w_ada'], 'm_b_ada': out['m_b_ada'], 'm_norm1_w': out['m_norm1_w'], 'm_w_in': out['m_w_in'], 'm_b_gate': out['m_b_gate'], 'm_mu_shift': out['m_mu_shift'], 'm_w0': out['m_w0'], 'm_w2': out['m_w2'], 'm_a0': out['m_a0'], 'm_a2': out['m_a2'], 'm_g2': out['m_g2'], 'm_k_k': out['m_k_k'], 'm_k_a': out['m_k_a'], 'm_r_k': out['m_r_k'], 'm_lnx_w': out['m_lnx_w'], 'm_lnx_b': out['m_lnx_b'], 'm_w_att_out': out['m_w_att_out'], 'm_w_rwkv_out': out['m_w_rwkv_out'], 'm_w_o': out['m_w_o'], 'm_norm2_w': out['m_norm2_w'], 'm_w_up': out['m_w_up'], 'm_conv_w': out['m_conv_w'], 'm_conv_b': out['m_conv_b'], 'm_w_down': out['m_w_down'], 'm_norm_f_w': out['m_norm_f_w'], 'v_w_ada': out['v_w_ada'], 'v_b_ada': out['v_b_ada'], 'v_norm1_w': out['v_norm1_w'], 'v_w_in': out['v_w_in'], 'v_b_gate': out['v_b_gate'], 'v_mu_shift': out['v_mu_shift'], 'v_w0': out['v_w0'], 'v_w2': out['v_w2'], 'v_a0': out['v_a0'], 'v_a2': out['v_a2'], 'v_g2': out['v_g2'], 'v_k_k': out['v_k_k'], 'v_k_a': out['v_k_a'], 'v_r_k': out['v_r_k'], 'v_lnx_w': out['v_lnx_w'], 'v_lnx_b': out['v_lnx_b'], 'v_w_att_out': out['v_w_att_out'], 'v_w_rwkv_out': out['v_w_rwkv_out'], 'v_w_o': out['v_w_o'], 'v_norm2_w': out['v_norm2_w'], 'v_w_up': out['v_w_up'], 'v_conv_w': out['v_conv_w'], 'v_conv_b': out['v_conv_b'], 'v_w_down': out['v_w_down'], 'v_norm_f_w': out['v_norm_f_w']}


def _loss(weights, diff, rest, loss_target):
    with _jax.named_scope("forward"):
        args = {**rest, TWIN_DIFF_INPUT: diff, **{k: w.astype(_WEIGHT_DTYPES[k]) for k, w in weights.items()}}
        y = _forward(args)
    with _jax.named_scope("loss_head"):
        err = _jnp.square(y.astype(_jnp.float32) - loss_target)
        return 0.5 * _jnp.sum(_jnp.mean(err, axis=-1)) if err.ndim else 0.5 * err


def _adamw(w, g, m, v):
    m = ADAM_B1 * m + (1.0 - ADAM_B1) * g
    v = ADAM_B2 * v + (1.0 - ADAM_B2) * _jnp.square(g)
    m_hat = m / (1.0 - ADAM_B1 ** ADAM_STEP)
    v_hat = v / (1.0 - ADAM_B2 ** ADAM_STEP)
    delta = -ADAM_LR * (m_hat / (_jnp.sqrt(v_hat) + ADAM_EPS) + ADAM_WD * w)
    return delta, m, v


def reference(x, c, w_ada, b_ada, norm1_w, w_in, b_gate, mu_shift, w0, w2, a0, a2, g2, k_k, k_a, r_k, lnx_w, lnx_b, w_att_out, w_rwkv_out, w_o, norm2_w, w_up, conv_w, conv_b, w_down, norm_f_w, loss_target, m_w_ada, m_b_ada, m_norm1_w, m_w_in, m_b_gate, m_mu_shift, m_w0, m_w2, m_a0, m_a2, m_g2, m_k_k, m_k_a, m_r_k, m_lnx_w, m_lnx_b, m_w_att_out, m_w_rwkv_out, m_w_o, m_norm2_w, m_w_up, m_conv_w, m_conv_b, m_w_down, m_norm_f_w, v_w_ada, v_b_ada, v_norm1_w, v_w_in, v_b_gate, v_mu_shift, v_w0, v_w2, v_a0, v_a2, v_g2, v_k_k, v_k_a, v_r_k, v_lnx_w, v_lnx_b, v_w_att_out, v_w_rwkv_out, v_w_o, v_norm2_w, v_w_up, v_conv_w, v_conv_b, v_w_down, v_norm_f_w):
    given = dict(x=x, c=c, w_ada=w_ada, b_ada=b_ada, norm1_w=norm1_w, w_in=w_in, b_gate=b_gate, mu_shift=mu_shift, w0=w0, w2=w2, a0=a0, a2=a2, g2=g2, k_k=k_k, k_a=k_a, r_k=r_k, lnx_w=lnx_w, lnx_b=lnx_b, w_att_out=w_att_out, w_rwkv_out=w_rwkv_out, w_o=w_o, norm2_w=norm2_w, w_up=w_up, conv_w=conv_w, conv_b=conv_b, w_down=w_down, norm_f_w=norm_f_w, loss_target=loss_target, m_w_ada=m_w_ada, m_b_ada=m_b_ada, m_norm1_w=m_norm1_w, m_w_in=m_w_in, m_b_gate=m_b_gate, m_mu_shift=m_mu_shift, m_w0=m_w0, m_w2=m_w2, m_a0=m_a0, m_a2=m_a2, m_g2=m_g2, m_k_k=m_k_k, m_k_a=m_k_a, m_r_k=m_r_k, m_lnx_w=m_lnx_w, m_lnx_b=m_lnx_b, m_w_att_out=m_w_att_out, m_w_rwkv_out=m_w_rwkv_out, m_w_o=m_w_o, m_norm2_w=m_norm2_w, m_w_up=m_w_up, m_conv_w=m_conv_w, m_conv_b=m_conv_b, m_w_down=m_w_down, m_norm_f_w=m_norm_f_w, v_w_ada=v_w_ada, v_b_ada=v_b_ada, v_norm1_w=v_norm1_w, v_w_in=v_w_in, v_b_gate=v_b_gate, v_mu_shift=v_mu_shift, v_w0=v_w0, v_w2=v_w2, v_a0=v_a0, v_a2=v_a2, v_g2=v_g2, v_k_k=v_k_k, v_k_a=v_k_a, v_r_k=v_r_k, v_lnx_w=v_lnx_w, v_lnx_b=v_lnx_b, v_w_att_out=v_w_att_out, v_w_rwkv_out=v_w_rwkv_out, v_w_o=v_w_o, v_norm2_w=v_norm2_w, v_w_up=v_w_up, v_conv_w=v_conv_w, v_conv_b=v_conv_b, v_w_down=v_w_down, v_norm_f_w=v_norm_f_w)
    weights = {n: given[n] for n in TWIN_WEIGHTS}
    shared = {n: given[n] for n in SHARED_INPUTS}
    per_example = {n: given[n] for n in ['x', 'c']}
    grad_fn = _jax.value_and_grad(_loss, argnums=(0, 1))

    def one_microbatch(ex, loss_target):
        ex = dict(ex)
        diff = ex.pop(TWIN_DIFF_INPUT)
        return grad_fn(weights, diff, {**shared, **ex}, loss_target)

    if N_MICROBATCH == 1:
        loss, (grad_w, grad_x) = one_microbatch(per_example, given["loss_target"])
    else:
        def body(carry, xs):
            loss_sum, grad_sum = carry
            l_k, (gw_k, gx_k) = one_microbatch(xs[0], xs[1])
            with _jax.named_scope("update"):
                return (loss_sum + l_k, _jax.tree.map(_jnp.add, grad_sum, gw_k)), gx_k

        init = (_jnp.zeros((), _jnp.float32), _jax.tree.map(_jnp.zeros_like, weights))
        (loss, grad_w), grad_x = _jax.lax.scan(body, init, (per_example, given["loss_target"]))
    with _jax.named_scope("update"):
        delta_w, new_m, new_v = {}, {}, {}
        for n in TWIN_WEIGHTS:
            delta_w[n], new_m[n], new_v[n] = _adamw(weights[n], grad_w[n], given["m_" + n], given["v_" + n])
    return (loss, grad_x, *[grad_w[n] for n in TWIN_WEIGHTS], *[delta_w[n] for n in TWIN_WEIGHTS],
            *[new_m[n] for n in TWIN_WEIGHTS], *[new_v[n] for n in TWIN_WEIGHTS])
```

```python
import functools
import math

import jax
import jax.numpy as jnp
from jax import lax
from jax.experimental import pallas as pl
from jax.experimental.pallas import tpu as pltpu

F32 = jnp.float32
BF16 = jnp.bfloat16

D = 1024
HEAD = 64
ATT_PATTERNS = ((128, 1), (512, 4), (2048, 16))
ATT_HEADS = 8
ATT_W = ATT_HEADS * HEAD
ATT_IN = 3 * 3 * ATT_W
QBLK = 128
N_HEADS = D // HEAD
LORA_W, LORA_A, LORA_G = 64, 64, 160
RWKV_IN = 3 * D + LORA_W + LORA_A + LORA_G
N_IN = ATT_IN + RWKV_IN + 2 * D
D_FF = 2816
RMS_EPS = 1e-6
GN_EPS = 64e-5
N_DEV = 8
LANES = 128
SUBLANES = 8

C_R, C_K, C_V, C_GA, C_GR = 0, 1024, 2048, 3072, 4096
C_ATT = 5120
C_LORA = C_ATT + ATT_IN
LORA_PAD = 512
G_PAD = 256
N_PAD = C_LORA + LORA_PAD

ADAM_LR, ADAM_B1, ADAM_B2, ADAM_EPS, ADAM_WD, ADAM_STEP = 0.001, 0.9, 0.999, 1e-08, 0.01, 10

SCAN_TB = 128
VMEM_LIMIT = 56 * 1024 * 1024

_MESH = pl.DeviceIdType.MESH


def _cparams(sem):
    return pltpu.CompilerParams(dimension_semantics=sem, vmem_limit_bytes=VMEM_LIMIT)


def _tile(dim, pref):
    if dim <= pref:
        return dim
    best = None
    for t in range(LANES, pref + 1, LANES):
        if dim % t == 0:
            best = t
    assert best is not None, dim
    return best


def _mm(a, b, mode, out_dtype, name, tm=512, tn=1024, tk=512):
    if mode == "nn":
        (M, K), (K2, N) = a.shape, b.shape
    elif mode == "nt":
        (M, K), (N, K2) = a.shape, b.shape
    else:
        (K, M), (K2, N) = a.shape, b.shape
    assert K == K2, (a.shape, b.shape, mode)
    tm, tn, tk = _tile(M, tm), _tile(N, tn), _tile(K, tk)
    nk = K // tk
    dims = {"nn": (((1,), (0,)), ((), ())), "nt": (((1,), (1,)), ((), ())), "tn": (((0,), (0,)), ((), ()))}[mode]

    def body(a_ref, b_ref, o_ref, acc_ref):
        k = pl.program_id(2)
        part = lax.dot_general(a_ref[...].astype(BF16), b_ref[...].astype(BF16), dims,
                               preferred_element_type=F32)

        @pl.when(k == 0)
        def _():
            acc_ref[...] = part

        @pl.when(k > 0)
        def _():
            acc_ref[...] += part

        @pl.when(k == nk - 1)
        def _():
            o_ref[...] = acc_ref[...].astype(o_ref.dtype)

    a_spec = pl.BlockSpec((tk, tm), lambda i, j, k: (k, i)) if mode == "tn" else pl.BlockSpec((tm, tk), lambda i, j, k: (i, k))
    b_spec = pl.BlockSpec((tn, tk), lambda i, j, k: (j, k)) if mode == "nt" else pl.BlockSpec((tk, tn), lambda i, j, k: (k, j))
    return pl.pallas_call(
        body, name=name, grid=(M // tm, N // tn, nk),
        in_specs=[a_spec, b_spec],
        out_specs=pl.BlockSpec((tm, tn), lambda i, j, k: (i, j)),
        out_shape=jax.ShapeDtypeStruct((M, N), out_dtype),
        scratch_shapes=[pltpu.VMEM((tm, tn), F32)],
        compiler_params=_cparams(("parallel", "parallel", "arbitrary")),
    )(a, b)


def _rows(tm, w, col=0):
    return pl.BlockSpec((tm, w), lambda i: (i, col))


def _full(shape):
    return pl.BlockSpec(shape, lambda i: (0,) * len(shape))


def _prev8(tm, w, col=0):
    return pl.BlockSpec((SUBLANES, w), lambda i: (jnp.maximum(i * (tm // SUBLANES) - 1, 0), col))


def _next8(tm, w, n_rows, col=0):
    last = n_rows // SUBLANES - 1
    return pl.BlockSpec((SUBLANES, w), lambda i: (jnp.minimum((i + 1) * (tm // SUBLANES), last), col))


def _shift_down(x, halo, k, first):
    rolled = pltpu.roll(x, k, 0)
    row = lax.broadcasted_iota(jnp.int32, x.shape, 0)
    out = rolled
    for j in range(k):
        h = jnp.where(first, 0.0, halo[SUBLANES - k + j:SUBLANES - k + j + 1, :])
        out = jnp.where(row == j, h, out)
    return out


def _shift_up(x, halo, k, last):
    n = x.shape[0]
    rolled = pltpu.roll(x, n - k, 0)
    row = lax.broadcasted_iota(jnp.int32, x.shape, 0)
    out = rolled
    for j in range(k):
        h = jnp.where(last, 0.0, halo[j:j + 1, :])
        out = jnp.where(row == n - k + j, h, out)
    return out


def _acc(ref, val, first):
    @pl.when(first)
    def _():
        ref[...] = val

    @pl.when(jnp.logical_not(first))
    def _():
        ref[...] += val


def _colsum(x):
    return jnp.sum(x, axis=0, keepdims=True)


def _norm_fwd(x, mo, gt, nw, sc, sh, name, tm=256):
    S = x.shape[0]
    has_res = mo is not None

    def body(*refs):
        if has_res:
            x_ref, mo_ref, gt_ref, nw_ref, sc_ref, sh_ref, x2_ref, h_ref, rs_ref = refs
            x2 = x_ref[...] + gt_ref[...] * mo_ref[...]
            x2_ref[...] = x2
        else:
            x_ref, nw_ref, sc_ref, sh_ref, h_ref, rs_ref = refs
            x2 = x_ref[...]
        rstd = lax.rsqrt(jnp.mean(x2 * x2, axis=-1, keepdims=True) + RMS_EPS)
        rs_ref[...] = rstd
        h_ref[...] = ((x2 * rstd * nw_ref[...]) * (1.0 + sc_ref[...]) + sh_ref[...]).astype(BF16)

    vec = _full((1, D))
    ins = [x, mo, gt, nw, sc, sh] if has_res else [x, nw, sc, sh]
    in_specs = [_rows(tm, D), _rows(tm, D), vec, vec, vec, vec] if has_res else [_rows(tm, D), vec, vec, vec]
    outs = [jax.ShapeDtypeStruct((S, D), BF16), jax.ShapeDtypeStruct((S, 1), F32)]
    out_specs = [_rows(tm, D), _rows(tm, 1)]
    if has_res:
        outs = [jax.ShapeDtypeStruct((S, D), F32)] + outs
        out_specs = [_rows(tm, D)] + out_specs
    return pl.pallas_call(body, name=name, grid=(S // tm,), in_specs=in_specs, out_specs=out_specs,
                          out_shape=outs, compiler_params=_cparams(("parallel",)))(*ins)


def _norm_bwd(dh, xin, rstd, nw, sc, dres, mo, gt, name, tm=256):
    S = xin.shape[0]
    has_res = mo is not None

    def body(*refs):
        if has_res:
            dh_ref, x_ref, rs_ref, nw_ref, sc_ref, dres_ref, mo_ref, gt_ref, dx_ref, dsh_ref, dsc_ref, dnw_ref, dmo_ref, dgt_ref = refs
        else:
            dh_ref, x_ref, rs_ref, nw_ref, sc_ref, dres_ref, dx_ref, dsh_ref, dsc_ref, dnw_ref = refs
        first = pl.program_id(0) == 0
        dh = dh_ref[...]
        rstd = rs_ref[...]
        n = x_ref[...] * rstd
        w = nw_ref[...]
        _acc(dsh_ref, _colsum(dh), first)
        _acc(dsc_ref, _colsum(dh * (n * w)), first)
        dnw = dh * (1.0 + sc_ref[...])
        _acc(dnw_ref, _colsum(dnw * n), first)
        dn = dnw * w
        dx = dres_ref[...] + rstd * (dn - n * jnp.mean(dn * n, axis=-1, keepdims=True))
        dx_ref[...] = dx
        if has_res:
            dmo_ref[...] = (dx * gt_ref[...]).astype(BF16)
            _acc(dgt_ref, _colsum(dx * mo_ref[...]), first)

    vec = _full((1, D))
    vshape = jax.ShapeDtypeStruct((1, D), F32)
    ins = [dh, xin, rstd, nw, sc, dres] + ([mo, gt] if has_res else [])
    in_specs = [_rows(tm, D), _rows(tm, D), _rows(tm, 1), vec, vec, _rows(tm, D)] + ([_rows(tm, D), vec] if has_res else [])
    outs = [jax.ShapeDtypeStruct((S, D), F32), vshape, vshape, vshape]
    out_specs = [_rows(tm, D), vec, vec, vec]
    if has_res:
        outs += [jax.ShapeDtypeStruct((S, D), BF16), vshape]
        out_specs += [_rows(tm, D), vec]
    return pl.pallas_call(body, name=name, grid=(S // tm,), in_specs=in_specs, out_specs=out_specs,
                          out_shape=outs, compiler_params=_cparams(("arbitrary",)))(*ins)


def _final(x2, f, gt2, nfw, target, tm=256):
    S = x2.shape[0]

    def body(x2_ref, f_ref, gt_ref, w_ref, t_ref, loss_ref, dx_ref, df_ref, dgt_ref, dw_ref):
        first = pl.program_id(0) == 0
        f = f_ref[...]
        gt = gt_ref[...]
        w = w_ref[...]
        x3 = x2_ref[...] + gt * f
        rstd = lax.rsqrt(jnp.mean(x3 * x3, axis=-1, keepdims=True) + RMS_EPS)
        n = x3 * rstd
        e = n * w - t_ref[...]
        part = 0.5 * jnp.sum(jnp.mean(e * e, axis=-1, keepdims=True), axis=0, keepdims=True)
        _acc(loss_ref, jnp.broadcast_to(part, (SUBLANES, LANES)), first)
        dy = e * (1.0 / D)
        _acc(dw_ref, _colsum(dy * n), first)
        dn = dy * w
        dx = rstd * (dn - n * jnp.mean(dn * n, axis=-1, keepdims=True))
        dx_ref[...] = dx
        df_ref[...] = (dx * gt).astype(BF16)
        _acc(dgt_ref, _colsum(dx * f), first)

    vec = _full((1, D))
    vshape = jax.ShapeDtypeStruct((1, D), F32)
    return pl.pallas_call(
        body, name="final_loss", grid=(S // tm,),
        in_specs=[_rows(tm, D), _rows(tm, D), vec, vec, _rows(tm, D)],
        out_specs=[_full((SUBLANES, LANES)), _rows(tm, D), _rows(tm, D), vec, vec],
        out_shape=[jax.ShapeDtypeStruct((SUBLANES, LANES), F32), jax.ShapeDtypeStruct((S, D), F32),
                   jax.ShapeDtypeStruct((S, D), BF16), vshape, vshape],
        compiler_params=_cparams(("arbitrary",)))(x2, f, gt2, nfw, target)


def _gate_fwd(P, bga, bgr, y_att, y_rwkv, tm=256):
    S = P.shape[0]

    def body(pa_ref, pr_ref, ba_ref, br_ref, ya_ref, yr_ref, mix_ref):
        ga = jax.nn.sigmoid(pa_ref[...] + ba_ref[...])
        gr = jax.nn.sigmoid(pr_ref[...] + br_ref[...])
        mix_ref[...] = (ga * ya_ref[...] + gr * yr_ref[...]).astype(BF16)

    vec = _full((1, D))
    return pl.pallas_call(
        body, name="gate_fwd", grid=(S // tm,),
        in_specs=[_rows(tm, D, C_GA // D), _rows(tm, D, C_GR // D), vec, vec, _rows(tm, D), _rows(tm, D)],
        out_specs=_rows(tm, D), out_shape=jax.ShapeDtypeStruct((S, D), BF16),
        compiler_params=_cparams(("parallel",)))(P, P, bga, bgr, y_att, y_rwkv)


def _gate_bwd(dmix, P, bga, bgr, y_att, y_rwkv, tm=256):
    S = P.shape[0]

    def body(dm_ref, pa_ref, pr_ref, ba_ref, br_ref, ya_ref, yr_ref, dya_ref, dyr_ref, dpa_ref, dpr_ref, dba_ref, dbr_ref):
        first = pl.program_id(0) == 0
        dm = dm_ref[...]
        ga = jax.nn.sigmoid(pa_ref[...] + ba_ref[...])
        gr = jax.nn.sigmoid(pr_ref[...] + br_ref[...])
        dya_ref[...] = (dm * ga).astype(BF16)
        dyr_ref[...] = (dm * gr).astype(BF16)
        dpa = dm * ya_ref[...] * ga * (1.0 - ga)
        dpr = dm * yr_ref[...] * gr * (1.0 - gr)
        dpa_ref[...] = dpa.astype(BF16)
        dpr_ref[...] = dpr.astype(BF16)
        _acc(dba_ref, _colsum(dpa), first)
        _acc(dbr_ref, _colsum(dpr), first)

    vec = _full((1, D))
    row = _rows(tm, D)
    rshape = jax.ShapeDtypeStruct((S, D), BF16)
    vshape = jax.ShapeDtypeStruct((1, D), F32)
    return pl.pallas_call(
        body, name="gate_bwd", grid=(S // tm,),
        in_specs=[row, _rows(tm, D, C_GA // D), _rows(tm, D, C_GR // D), vec, vec, row, row],
        out_specs=[row, row, row, row, vec, vec],
        out_shape=[rshape, rshape, rshape, rshape, vshape, vshape],
        compiler_params=_cparams(("arbitrary",)))(dmix, P, P, bga, bgr, y_att, y_rwkv)


def _conv_fwd(u, conv_w8, conv_b, tm=256, tn=256):
    S = u.shape[0]
    nj = D_FF // tn

    def conv(u_ref, h_ref, w_ref, b_ref, first):
        u = u_ref[...]
        h = h_ref[...]
        w = w_ref[...]
        return b_ref[...] + w[0:1] * _shift_down(u, h, 2, first) + w[1:2] * _shift_down(u, h, 1, first) + w[2:3] * u

    def body(ug_ref, hg_ref, uv_ref, hv_ref, wg_ref, wv_ref, bg_ref, bv_ref, act_ref):
        first = pl.program_id(0) == 0
        g = conv(ug_ref, hg_ref, wg_ref, bg_ref, first)
        v = conv(uv_ref, hv_ref, wv_ref, bv_ref, first)
        act_ref[...] = (g * jax.nn.sigmoid(g) * v).astype(BF16)

    blk = lambda off: pl.BlockSpec((tm, tn), lambda i, j: (i, j + off))
    halo = lambda off: pl.BlockSpec((SUBLANES, tn), lambda i, j: (jnp.maximum(i * (tm // SUBLANES) - 1, 0), j + off))
    wsp = lambda off: pl.BlockSpec((SUBLANES, tn), lambda i, j: (0, j + off))
    bsp = lambda off: pl.BlockSpec((1, tn), lambda i, j: (0, j + off))
    return pl.pallas_call(
        body, name="conv_fwd", grid=(S // tm, nj),
        in_specs=[blk(0), halo(0), blk(nj), halo(nj), wsp(0), wsp(nj), bsp(0), bsp(nj)],
        out_specs=pl.BlockSpec((tm, tn), lambda i, j: (i, j)),
        out_shape=jax.ShapeDtypeStruct((S, D_FF), BF16),
        compiler_params=_cparams(("parallel", "parallel")))(u, u, u, u, conv_w8, conv_w8, conv_b, conv_b)


def _conv_bwd_a(dact, u, conv_w8, conv_b, tm=256, tn=256):
    S = u.shape[0]
    nj = D_FF // tn

    def half(u_ref, h_ref, w_ref, b_ref, first):
        u = u_ref[...]
        h = h_ref[...]
        w = w_ref[...]
        u2, u1 = _shift_down(u, h, 2, first), _shift_down(u, h, 1, first)
        return b_ref[...] + w[0:1] * u2 + w[1:2] * u1 + w[2:3] * u, (u2, u1, u)

    def wgrad(d, taps):
        z = jnp.zeros((SUBLANES - 3, d.shape[1]), F32)
        return jnp.concatenate([_colsum(d * taps[0]), _colsum(d * taps[1]), _colsum(d * taps[2]), z], axis=0)

    def body(da_ref, ug_ref, hg_ref, uv_ref, hv_ref, wg_ref, wv_ref, bg_ref, bv_ref,
             dg_ref, dv_ref, dwg_ref, dwv_ref, dbg_ref, dbv_ref):
        first = pl.program_id(1) == 0
        g, tg = half(ug_ref, hg_ref, wg_ref, bg_ref, first)
        v, tv = half(uv_ref, hv_ref, wv_ref, bv_ref, first)
        da = da_ref[...].astype(F32)
        sg = jax.nn.sigmoid(g)
        dg = da * v * (sg * (1.0 + g * (1.0 - sg)))
        dv = da * (g * sg)
        dg_ref[...] = dg
        dv_ref[...] = dv
        _acc(dwg_ref, wgrad(dg, tg), first)
        _acc(dwv_ref, wgrad(dv, tv), first)
        _acc(dbg_ref, _colsum(dg), first)
        _acc(dbv_ref, _colsum(dv), first)

    blk = lambda off: pl.BlockSpec((tm, tn), lambda j, i: (i, j + off))
    halo = lambda off: pl.BlockSpec((SUBLANES, tn), lambda j, i: (jnp.maximum(i * (tm // SUBLANES) - 1, 0), j + off))
    wsp = lambda off: pl.BlockSpec((SUBLANES, tn), lambda j, i: (0, j + off))
    bsp = lambda off: pl.BlockSpec((1, tn), lambda j, i: (0, j + off))
    f = jax.ShapeDtypeStruct
    outs = pl.pallas_call(
        body, name="conv_bwd_a", grid=(nj, S // tm),
        in_specs=[pl.BlockSpec((tm, tn), lambda j, i: (i, j)), blk(0), halo(0), blk(nj), halo(nj), wsp(0), wsp(nj), bsp(0), bsp(nj)],
        out_specs=[pl.BlockSpec((tm, tn), lambda j, i: (i, j)), pl.BlockSpec((tm, tn), lambda j, i: (i, j)),
                   pl.BlockSpec((SUBLANES, tn), lambda j, i: (0, j)), pl.BlockSpec((SUBLANES, tn), lambda j, i: (0, j)),
                   pl.BlockSpec((1, tn), lambda j, i: (0, j)), pl.BlockSpec((1, tn), lambda j, i: (0, j))],
        out_shape=[f((S, D_FF), F32), f((S, D_FF), F32), f((SUBLANES, D_FF), F32), f((SUBLANES, D_FF), F32),
                   f((1, D_FF), F32), f((1, D_FF), F32)],
        compiler_params=_cparams(("parallel", "arbitrary")))(dact, u, u, u, u, conv_w8, conv_w8, conv_b, conv_b)
    return outs


def _conv_bwd_b(duc, conv_w8, tm=256, tn=256):
    S, W = duc.shape

    def body(d_ref, h_ref, w_ref, o_ref):
        last = pl.program_id(0) == pl.num_programs(0) - 1
        d = d_ref[...]
        h = h_ref[...]
        w = w_ref[...]
        o_ref[...] = (w[2:3] * d + w[1:2] * _shift_up(d, h, 1, last) + w[0:1] * _shift_up(d, h, 2, last)).astype(BF16)

    last_tile = S // SUBLANES - 1
    return pl.pallas_call(
        body, name="conv_bwd_b", grid=(S // tm, W // tn),
        in_specs=[pl.BlockSpec((tm, tn), lambda i, j: (i, j)),
                  pl.BlockSpec((SUBLANES, tn), lambda i, j: (jnp.minimum((i + 1) * (tm // SUBLANES), last_tile), j)),
                  pl.BlockSpec((SUBLANES, tn), lambda i, j: (0, j))],
        out_specs=pl.BlockSpec((tm, tn), lambda i, j: (i, j)),
        out_shape=jax.ShapeDtypeStruct((S, W), BF16),
        compiler_params=_cparams(("parallel", "parallel")))(duc, duc, conv_w8)


ATT_SCALE = HEAD ** -0.5
NEG = -1e30


def _att_blocks_per_seq(S):
    return [S // (QBLK * d) for (_, d) in ATT_PATTERNS]


def _att_scores(q, kc, kp, has_prev):
    qi = lax.broadcasted_iota(jnp.int32, (QBLK, QBLK), 0)
    kj = lax.broadcasted_iota(jnp.int32, (QBLK, QBLK), 1)
    nt = (((1,), (1,)), ((), ()))
    s_c = lax.dot_general(q, kc, nt, preferred_element_type=F32) * ATT_SCALE
    s_p = lax.dot_general(q, kp, nt, preferred_element_type=F32) * ATT_SCALE
    s_c = jnp.where(kj <= qi, s_c, NEG)
    s_p = jnp.where(jnp.logical_and(kj >= qi, has_prev), s_p, NEG)
    return s_c, s_p


def _att_fwd(q, k, v):
    GH, S, _ = q.shape
    nb = S // QBLK
    bps = _att_blocks_per_seq(S)

    def body(q_ref, k_ref, v_ref, o_ref, l_ref):
        g = pl.program_id(0) // ATT_HEADS
        per = jnp.where(g == 0, bps[0], jnp.where(g == 1, bps[1], bps[2]))

        def blk(n, carry):
            cur = pl.ds(pl.multiple_of(n * QBLK, QBLK), QBLK)
            prv = pl.ds(pl.multiple_of(jnp.maximum(n - 1, 0) * QBLK, QBLK), QBLK)
            s_c, s_p = _att_scores(q_ref[cur, :], k_ref[cur, :], k_ref[prv, :], (n % per) != 0)
            m = jnp.maximum(jnp.max(s_c, axis=1, keepdims=True), jnp.max(s_p, axis=1, keepdims=True))
            p_c = jnp.exp(s_c - m)
            p_p = jnp.exp(s_p - m)
            den = jnp.sum(p_c, axis=1, keepdims=True) + jnp.sum(p_p, axis=1, keepdims=True)
            num = (jnp.dot(p_c.astype(BF16), v_ref[cur, :], preferred_element_type=F32)
                   + jnp.dot(p_p.astype(BF16), v_ref[prv, :], preferred_element_type=F32))
            o_ref[cur, :] = num / den
            l_ref[cur, :] = jnp.broadcast_to(m + jnp.log(den), (QBLK, HEAD))
            return carry

        lax.fori_loop(0, nb, blk, 0)

    spec = pl.BlockSpec((None, S, HEAD), lambda i: (i, 0, 0))
    shp = jax.ShapeDtypeStruct((GH, S, HEAD), F32)
    return pl.pallas_call(body, name="att_fwd", grid=(GH,), in_specs=[spec, spec, spec], out_specs=[spec, spec],
                          out_shape=[shp, shp], compiler_params=_cparams(("parallel",)))(q, k, v)


def _att_bwd(q, k, v, o, l, do, dl):
    GH, S, _ = q.shape
    nb = S // QBLK
    bps = _att_blocks_per_seq(S)
    tn = (((0,), (0,)), ((), ()))
    nt = (((1,), (1,)), ((), ()))

    def body(q_ref, k_ref, v_ref, o_ref, l_ref, do_ref, dl_ref, dq_ref, dk_ref, dv_ref):
        g = pl.program_id(0) // ATT_HEADS
        per = jnp.where(g == 0, bps[0], jnp.where(g == 1, bps[1], bps[2]))
        dk_ref[...] = jnp.zeros_like(dk_ref)
        dv_ref[...] = jnp.zeros_like(dv_ref)

        def blk(n, carry):
            cur = pl.ds(pl.multiple_of(n * QBLK, QBLK), QBLK)
            prv = pl.ds(pl.multiple_of(jnp.maximum(n - 1, 0) * QBLK, QBLK), QBLK)
            qb = q_ref[cur, :]
            kc, kp, vc, vp = k_ref[cur, :], k_ref[prv, :], v_ref[cur, :], v_ref[prv, :]
            s_c, s_p = _att_scores(qb, kc, kp, (n % per) != 0)
            lse = l_ref[cur, :][:, 0:1]
            p_c = jnp.exp(s_c - lse)
            p_p = jnp.exp(s_p - lse)
            dob = do_ref[cur, :]
            delta = jnp.sum(dob * o_ref[cur, :] - dl_ref[cur, :], axis=1, keepdims=True)
            dob16 = dob.astype(BF16)
            dp_c = lax.dot_general(dob16, vc, nt, preferred_element_type=F32)
            dp_p = lax.dot_general(dob16, vp, nt, preferred_element_type=F32)
            ds_c = (p_c * (dp_c - delta) * ATT_SCALE).astype(BF16)
            ds_p = (p_p * (dp_p - delta) * ATT_SCALE).astype(BF16)
            dq_ref[cur, :] = (jnp.dot(ds_c, kc, preferred_element_type=F32) + jnp.dot(ds_p, kp, preferred_element_type=F32))
            dk_ref[cur, :] += lax.dot_general(ds_c, qb, tn, preferred_element_type=F32)
            dv_ref[cur, :] += lax.dot_general(p_c.astype(BF16), dob16, tn, preferred_element_type=F32)
            dk_ref[prv, :] += lax.dot_general(ds_p, qb, tn, preferred_element_type=F32)
            dv_ref[prv, :] += lax.dot_general(p_p.astype(BF16), dob16, tn, preferred_element_type=F32)
            return carry

        lax.fori_loop(0, nb, blk, 0)

    spec = pl.BlockSpec((None, S, HEAD), lambda i: (i, 0, 0))
    shp = jax.ShapeDtypeStruct((GH, S, HEAD), F32)
    return pl.pallas_call(body, name="att_bwd", grid=(GH,), in_specs=[spec] * 7, out_specs=[spec] * 3,
                          out_shape=[shp] * 3, compiler_params=_cparams(("parallel",)))(q, k, v, o, l, do, dl)


def _att_combine_fwd(o3, l3, tm=512):
    _, R, _ = o3.shape

    def body(o_ref, l_ref, a_ref):
        l0, l1, l2 = l_ref[0], l_ref[1], l_ref[2]
        m = jnp.maximum(jnp.maximum(l0, l1), l2)
        e0, e1, e2 = jnp.exp(l0 - m), jnp.exp(l1 - m), jnp.exp(l2 - m)
        a_ref[...] = (e0 * o_ref[0] + e1 * o_ref[1] + e2 * o_ref[2]) / (e0 + e1 + e2)

    spec3 = pl.BlockSpec((3, tm, LANES), lambda i: (0, i, 0))
    return pl.pallas_call(body, name="att_combine_fwd", grid=(R // tm,), in_specs=[spec3, spec3],
                          out_specs=_rows(tm, LANES), out_shape=jax.ShapeDtypeStruct((R, LANES), F32),
                          compiler_params=_cparams(("parallel",)))(o3, l3)


def _att_combine_bwd(da, o3, l3, tm=512):
    _, R, _ = o3.shape

    def body(da_ref, o_ref, l_ref, do_ref, dl_ref):
        da = da_ref[...]
        l0, l1, l2 = l_ref[0], l_ref[1], l_ref[2]
        m = jnp.maximum(jnp.maximum(l0, l1), l2)
        e0, e1, e2 = jnp.exp(l0 - m), jnp.exp(l1 - m), jnp.exp(l2 - m)
        inv = 1.0 / (e0 + e1 + e2)
        w = (e0 * inv, e1 * inv, e2 * inv)
        dw = (da * o_ref[0], da * o_ref[1], da * o_ref[2])
        mean = w[0] * dw[0] + w[1] * dw[1] + w[2] * dw[2]
        for g in range(3):
            do_ref[g] = w[g] * da
            dl_ref[g] = w[g] * (dw[g] - mean)

    spec3 = pl.BlockSpec((3, tm, LANES), lambda i: (0, i, 0))
    shp = jax.ShapeDtypeStruct((3, R, LANES), F32)
    return pl.pallas_call(body, name="att_combine_bwd", grid=(R // tm,), in_specs=[_rows(tm, LANES), spec3, spec3],
                          out_specs=[spec3, spec3], out_shape=[shp, shp],
                          compiler_params=_cparams(("parallel",)))(da, o3, l3)


@jax.custom_vjp
def _bdot(a, b):
    return jnp.dot(a.astype(BF16), b.astype(BF16), preferred_element_type=F32)


def _bdot_fwd(a, b):
    return _bdot(a, b), (a, b)


def _bdot_bwd(res, ct):
    a, b = res
    ct16 = ct.astype(BF16)
    da = lax.dot_general(ct16, b.astype(BF16), (((1,), (1,)), ((), ())), preferred_element_type=F32)
    db = lax.dot_general(a.astype(BF16), ct16, (((0,), (0,)), ((), ())), preferred_element_type=F32)
    return da, db


_bdot.defvjp(_bdot_fwd, _bdot_bwd)


def _head_sum(x):
    hi = lax.Precision.HIGHEST
    sel = (lax.broadcasted_iota(jnp.int32, (D, LANES), 0) // HEAD == lax.broadcasted_iota(jnp.int32, (D, LANES), 1)).astype(F32)
    sel_t = (lax.broadcasted_iota(jnp.int32, (LANES, D), 1) // HEAD == lax.broadcasted_iota(jnp.int32, (LANES, D), 0)).astype(F32)
    return jnp.dot(jnp.dot(x, sel, precision=hi, preferred_element_type=F32), sel_t, precision=hi, preferred_element_type=F32)


def _softplus(z):
    return jnp.maximum(z, 0.0) + jnp.log(1.0 + jnp.exp(-jnp.abs(z)))


def _rwkv_prep_fn(zr, zrp, zk, zkp, zv, zvp, zl, zlp, mu_r, mu_k, mu_v, mu_l, w0, a0, k_k, k_a, w2, a2, g2p):
    r = zr + (zrp - zr) * mu_r
    k = zk + (zkp - zk) * mu_k
    v = zv + (zvp - zv) * mu_v
    lo = zl + (zlp - zl) * mu_l
    w_low, a_low, g_low = lo[:, 0:LORA_W], lo[:, LORA_W:LORA_W + LORA_A], lo[:, LANES:LANES + G_PAD]
    w_log = -_softplus(-(w0 + _bdot(jnp.tanh(w_low), w2))) - 0.5
    decay = jnp.exp(-jnp.exp(w_log))
    a = jax.nn.sigmoid(a0 + _bdot(a_low, a2))
    g = _bdot(jax.nn.sigmoid(g_low), g2p)
    kmod = k * (1.0 + (a - 1.0) * k_a)
    kk = k * k_k
    kk = kk / jnp.maximum(jnp.sqrt(_head_sum(kk * kk)), 1e-12)
    return r, decay, kmod, v, -kk, kk * a, g


def _rwkv_prep_specs(tm):
    vec = _full((1, D))
    slabs = []
    for col in (C_R // D, C_K // D, C_V // D):
        slabs += [_rows(tm, D, col), _prev8(tm, D, col)]
    slabs += [_rows(tm, LORA_PAD, C_LORA // LORA_PAD), _prev8(tm, LORA_PAD, C_LORA // LORA_PAD)]
    params = [vec, vec, vec, _full((1, LORA_PAD)), vec, vec, vec, vec,
              _full((LORA_W, D)), _full((LORA_A, D)), _full((G_PAD, D))]
    return slabs, params


def _prep_inputs(refs, first):
    vals = []
    for s in range(4):
        z = refs[2 * s][...]
        vals += [z, _shift_down(z, refs[2 * s + 1][...], 1, first)]
    return vals + [r[...] for r in refs[8:19]]


def _rwkv_prep(P, params, tm=256):
    S = P.shape[0]
    slabs, pspecs = _rwkv_prep_specs(tm)

    def body(*refs):
        outs = _rwkv_prep_fn(*_prep_inputs(refs, pl.program_id(0) == 0))
        for o_ref, val in zip(refs[19:], outs):
            o_ref[...] = val

    shp = jax.ShapeDtypeStruct((S, D), F32)
    return pl.pallas_call(body, name="rwkv_prep", grid=(S // tm,), in_specs=slabs + pspecs,
                          out_specs=[_rows(tm, D)] * 7, out_shape=[shp] * 7,
                          compiler_params=_cparams(("parallel",)))(*([P] * 8), *params)


def _rwkv_prep_bwd(P, params, cts_a, cts_b, tm=128):
    S = P.shape[0]
    slabs, pspecs = _rwkv_prep_specs(tm)
    has_b = [c is not None for c in cts_b]
    n_ct = 7 + sum(has_b)

    def body(*refs):
        first = pl.program_id(0) == 0
        ins = _prep_inputs(refs, first)
        ct_refs = refs[19:19 + n_ct]
        out_refs = refs[19 + n_ct:]
        cts, pos = [], 7
        for i in range(7):
            c = ct_refs[i][...]
            if has_b[i]:
                c = c + ct_refs[pos][...]
                pos += 1
            cts.append(c)
        _, vjp = jax.vjp(_rwkv_prep_fn, *ins)
        grads = vjp(tuple(cts))
        for s in range(4):
            out_refs[s][...] = grads[2 * s]
            out_refs[4 + s][...] = grads[2 * s + 1]
        for i in range(11):
            _acc(out_refs[8 + i], grads[8 + i], first)

    ct_in = list(cts_a) + [c for c in cts_b if c is not None]
    row, lrow = _rows(tm, D), _rows(tm, LORA_PAD)
    f = jax.ShapeDtypeStruct
    zshapes = [f((S, D), F32)] * 3 + [f((S, LORA_PAD), F32)]
    pshapes = [f((1, D), F32)] * 3 + [f((1, LORA_PAD), F32)] + [f((1, D), F32)] * 4 + [f((LORA_W, D), F32), f((LORA_A, D), F32), f((G_PAD, D), F32)]
    return pl.pallas_call(
        body, name="rwkv_prep_bwd", grid=(S // tm,),
        in_specs=slabs + pspecs + [row] * n_ct,
        out_specs=[row, row, row, lrow] * 2 + pspecs,
        out_shape=zshapes * 2 + pshapes,
        compiler_params=_cparams(("arbitrary",)))(*([P] * 8), *params, *ct_in)


def _shift_add(a, b, tm=256):
    S, W = a.shape

    def body(a_ref, b_ref, h_ref, o_ref):
        last = pl.program_id(0) == pl.num_programs(0) - 1
        o_ref[...] = (a_ref[...] + _shift_up(b_ref[...], h_ref[...], 1, last)).astype(BF16)

    return pl.pallas_call(body, name="shift_add", grid=(S // tm,),
                          in_specs=[_rows(tm, W), _rows(tm, W), _next8(tm, W, S)],
                          out_specs=_rows(tm, W), out_shape=jax.ShapeDtypeStruct((S, W), BF16),
                          compiler_params=_cparams(("parallel",)))(a, b, b)


def _rwkv_post_fn(y, r, kmod, v, g, lnx_w, lnx_b, r_k):
    mean = _head_sum(y) * (1.0 / HEAD)
    yc = y - mean
    var = _head_sum(yc * yc) * (1.0 / HEAD)
    yn = yc * lax.rsqrt(var + GN_EPS) * lnx_w + lnx_b
    bonus = _head_sum(r * kmod * r_k) * v
    return (yn + bonus) * g


def _rwkv_post(y, r, kmod, v, g, lnx_w, lnx_b, r_k, tm=256):
    S = y.shape[0]

    def body(y_ref, r_ref, k_ref, v_ref, g_ref, w_ref, b_ref, rk_ref, o_ref):
        o_ref[...] = _rwkv_post_fn(y_ref[...], r_ref[...], k_ref[...], v_ref[...], g_ref[...],
                                   w_ref[...], b_ref[...], rk_ref[...]).astype(BF16)

    row, vec = _rows(tm, D), _full((1, D))
    return pl.pallas_call(body, name="rwkv_post", grid=(S // tm,), in_specs=[row] * 5 + [vec] * 3, out_specs=row,
                          out_shape=jax.ShapeDtypeStruct((S, D), BF16),
                          compiler_params=_cparams(("parallel",)))(y, r, kmod, v, g, lnx_w, lnx_b, r_k)


def _rwkv_post_bwd(drw, y, r, kmod, v, g, lnx_w, lnx_b, r_k, tm=256):
    S = y.shape[0]

    def body(d_ref, y_ref, r_ref, k_ref, v_ref, g_ref, w_ref, b_ref, rk_ref, *out_refs):
        first = pl.program_id(0) == 0
        _, vjp = jax.vjp(_rwkv_post_fn, y_ref[...], r_ref[...], k_ref[...], v_ref[...], g_ref[...],
                         w_ref[...], b_ref[...], rk_ref[...])
        grads = vjp(d_ref[...])
        for i in range(5):
            out_refs[i][...] = grads[i]
        for i in range(5, 8):
            _acc(out_refs[i], grads[i], first)

    row, vec = _rows(tm, D), _full((1, D))
    f = jax.ShapeDtypeStruct
    return pl.pallas_call(body, name="rwkv_post_bwd", grid=(S // tm,), in_specs=[row] * 6 + [vec] * 3,
                          out_specs=[row] * 5 + [vec] * 3, out_shape=[f((S, D), F32)] * 5 + [f((1, D), F32)] * 3,
                          compiler_params=_cparams(("arbitrary",)))(drw, y, r, kmod, v, g, lnx_w, lnx_b, r_k)


def _col_tiles(t_ref, n_ops, j):
    return [pltpu.roll(t_ref[o], (LANES - SUBLANES * j) % LANES, 1) for o in range(n_ops)]


def _col(tile, ii, lane_lo):
    a = jnp.broadcast_to(tile[0:HEAD, ii:ii + 1], (HEAD, LANES))
    b = jnp.broadcast_to(tile[HEAD:2 * HEAD, ii:ii + 1], (HEAD, LANES))
    return jnp.where(lane_lo, a, b)


def _scan_fwd(r, w, k, v, a, b):
    S = r.shape[0]
    nblk = S // SCAN_TB
    npair = N_HEADS // 2

    def body(r_ref, w_ref, k_ref, v_ref, a_ref, b_ref, y_ref, ck_ref, s_ref, t_ref):
        @pl.when(pl.program_id(1) == 0)
        def _():
            s_ref[...] = jnp.zeros_like(s_ref)

        ck_ref[...] = s_ref[...]
        for o, ref in enumerate((r_ref, w_ref, k_ref, a_ref, b_ref)):
            t_ref[o] = ref[...].T
        lane_lo = lax.broadcasted_iota(jnp.int32, (HEAD, LANES), 1) < HEAD

        def group(j, st):
            tiles = _col_tiles(t_ref, 5, j)
            for ii in range(SUBLANES):
                t = j * SUBLANES + ii
                rc, wc, kc, ac, bc = [_col(tl, ii, lane_lo) for tl in tiles]
                vrow = v_ref[pl.ds(t, 1), :]
                sa = jnp.sum(st * ac, axis=0, keepdims=True)
                st = st * wc + bc * sa + kc * vrow
                y_ref[pl.ds(t, 1), :] = jnp.sum(st * rc, axis=0, keepdims=True)
            return st

        s_ref[...] = lax.fori_loop(0, SCAN_TB // SUBLANES, group, s_ref[...])

    blk = pl.BlockSpec((SCAN_TB, LANES), lambda p, i: (i, p))
    return pl.pallas_call(
        body, name="scan_fwd", grid=(npair, nblk), in_specs=[blk] * 6,
        out_specs=[blk, pl.BlockSpec((None, None, HEAD, LANES), lambda p, i: (i, p, 0, 0))],
        out_shape=[jax.ShapeDtypeStruct((S, D), F32), jax.ShapeDtypeStruct((nblk, npair, HEAD, LANES), F32)],
        scratch_shapes=[pltpu.VMEM((HEAD, LANES), F32), pltpu.VMEM((5, LANES, SCAN_TB), F32)],
        compiler_params=_cparams(("parallel", "arbitrary")))(r, w, k, v, a, b)


def _half_sums(z, lane_lo):
    lo = jnp.sum(jnp.where(lane_lo, z, 0.0), axis=1, keepdims=True)
    return lo, jnp.sum(z, axis=1, keepdims=True) - lo


def _scan_bwd(r, w, k, v, a, b, ckpt, dy):
    S = r.shape[0]
    nblk = S // SCAN_TB
    npair = N_HEADS // 2
    NG = SCAN_TB // SUBLANES

    def body(r_ref, w_ref, k_ref, v_ref, a_ref, b_ref, ck_ref, dy_ref,
             dr_ref, dw_ref, dk_ref, dv_ref, da_ref, db_ref, ds_ref, t_ref, sall_ref, c_ref):
        @pl.when(pl.program_id(1) == 0)
        def _():
            ds_ref[...] = jnp.zeros_like(ds_ref)

        for o, ref in enumerate((r_ref, w_ref, k_ref, a_ref, b_ref)):
            t_ref[o] = ref[...].T
        lane_lo = lax.broadcasted_iota(jnp.int32, (HEAD, LANES), 1) < HEAD
        lane = lax.broadcasted_iota(jnp.int32, (HEAD, LANES), 1)

        def replay(j, st):
            tiles = _col_tiles(t_ref, 5, j)
            for ii in range(SUBLANES):
                t = j * SUBLANES + ii
                _, wc, kc, ac, bc = [_col(tl, ii, lane_lo) for tl in tiles]
                sall_ref[t] = st
                sa = jnp.sum(st * ac, axis=0, keepdims=True)
                st = st * wc + bc * sa + kc * v_ref[pl.ds(t, 1), :]
            return st

        lax.fori_loop(0, NG, replay, ck_ref[...])
        c_ref[...] = jnp.zeros_like(c_ref)

        def group(jj, dst):
            j = NG - 1 - jj
            tiles = _col_tiles(t_ref, 5, j)
            acc = [jnp.zeros((HEAD, LANES), F32) for _ in range(10)]
            for ii in reversed(range(SUBLANES)):
                t = j * SUBLANES + ii
                rc, wc, kc, ac, bc = [_col(tl, ii, lane_lo) for tl in tiles]
                vrow = v_ref[pl.ds(t, 1), :]
                dyrow = dy_ref[pl.ds(t, 1), :]
                sp = sall_ref[t]
                sa = jnp.sum(sp * ac, axis=0, keepdims=True)
                sn = sp * wc + bc * sa + kc * vrow
                dsn = dst + rc * dyrow
                dv_ref[pl.ds(t, 1), :] = jnp.sum(dsn * kc, axis=0, keepdims=True)
                dsa = jnp.sum(dsn * bc, axis=0, keepdims=True)
                cols = (_half_sums(sn * dyrow, lane_lo) + _half_sums(dsn * sp, lane_lo) + _half_sums(dsn * vrow, lane_lo)
                        + _half_sums(sp * dsa, lane_lo) + _half_sums(dsn * sa, lane_lo))
                acc = [jnp.where(lane == ii, c, x) for c, x in zip(cols, acc)]
                dst = dsn * wc + ac * dsa
            shift = (SUBLANES * j) % LANES
            for q in range(5):
                c_ref[q, 0:HEAD, :] += pltpu.roll(acc[2 * q], shift, 1)
                c_ref[q, HEAD:2 * HEAD, :] += pltpu.roll(acc[2 * q + 1], shift, 1)
            return dst

        ds_ref[...] = lax.fori_loop(0, NG, group, ds_ref[...])
        for q, ref in enumerate((dr_ref, dw_ref, dk_ref, da_ref, db_ref)):
            ref[...] = c_ref[q].T

    blk = pl.BlockSpec((SCAN_TB, LANES), lambda p, i: (nblk - 1 - i, p))
    shp = jax.ShapeDtypeStruct((S, D), F32)
    return pl.pallas_call(
        body, name="scan_bwd", grid=(npair, nblk),
        in_specs=[blk] * 6 + [pl.BlockSpec((None, None, HEAD, LANES), lambda p, i: (nblk - 1 - i, p, 0, 0)), blk],
        out_specs=[blk] * 6, out_shape=[shp] * 6,
        scratch_shapes=[pltpu.VMEM((HEAD, LANES), F32), pltpu.VMEM((5, LANES, SCAN_TB), F32),
                        pltpu.VMEM((SCAN_TB, HEAD, LANES), F32), pltpu.VMEM((5, LANES, SCAN_TB), F32)],
        compiler_params=_cparams(("parallel", "arbitrary")))(r, w, k, v, a, b, ckpt, dy)


def _ada_fwd(c8, w_ada, b_ada):
    def body(c_ref, w_ref, b_ref, o_ref):
        o_ref[...] = jnp.dot(c_ref[...].astype(BF16), w_ref[...], preferred_element_type=F32) + b_ref[...]

    tn = 1536
    return pl.pallas_call(body, name="ada_fwd", grid=(6 * D // tn,),
                          in_specs=[_full((SUBLANES, D)), pl.BlockSpec((D, tn), lambda j: (0, j)), pl.BlockSpec((1, tn), lambda j: (0, j))],
                          out_specs=pl.BlockSpec((SUBLANES, tn), lambda j: (0, j)),
                          out_shape=jax.ShapeDtypeStruct((SUBLANES, 6 * D), F32),
                          compiler_params=_cparams(("parallel",)))(c8, w_ada, b_ada)


def _outer(col, row):
    N = row.shape[1]
    tn = 1536

    def body(c_ref, r_ref, o_ref):
        o_ref[...] = c_ref[...] * r_ref[...]

    return pl.pallas_call(body, name="ada_wgrad", grid=(N // tn,),
                          in_specs=[_full((D, 1)), pl.BlockSpec((1, tn), lambda j: (0, j))],
                          out_specs=pl.BlockSpec((D, tn), lambda j: (0, j)),
                          out_shape=jax.ShapeDtypeStruct((D, N), F32),
                          compiler_params=_cparams(("parallel",)))(col, row)


def _exchange(src, broadcast, name):
    rows = src.shape[-2]
    out_shape = jax.ShapeDtypeStruct((N_DEV, rows, LANES), src.dtype)

    def body(src_ref, out_ref, send_sems, recv_sems, local_sem):
        x, y, c = lax.axis_index("x"), lax.axis_index("y"), lax.axis_index("c")
        me = 4 * x + 2 * y + c

        def block(j):
            return src_ref if broadcast else src_ref.at[j]

        local = pltpu.make_async_copy(block(me), out_ref.at[me], local_sem)
        local.start()
        copies = []
        for d in range(1, N_DEV):
            px, py, pc = x ^ (d >> 2), y ^ ((d >> 1) & 1), c ^ (d & 1)
            peer = 4 * px + 2 * py + pc
            copies.append(pltpu.make_async_remote_copy(
                src_ref=block(peer), dst_ref=out_ref.at[me], send_sem=send_sems.at[d], recv_sem=recv_sems.at[d],
                device_id=(px, py, pc), device_id_type=_MESH))
        for cp in copies:
            cp.start()
        for d, cp in zip(range(1, N_DEV), copies):
            px, py, pc = x ^ (d >> 2), y ^ ((d >> 1) & 1), c ^ (d & 1)
            peer = 4 * px + 2 * py + pc
            pltpu.make_async_remote_copy(
                src_ref=block(me), dst_ref=out_ref.at[peer], send_sem=send_sems.at[d], recv_sem=recv_sems.at[d],
                device_id=(px, py, pc), device_id_type=_MESH).wait_recv()
        for cp in copies:
            cp.wait_send()
        local.wait()

    return pl.pallas_call(
        body, name=name, out_shape=out_shape,
        in_specs=[pl.BlockSpec(memory_space=pl.ANY)], out_specs=pl.BlockSpec(memory_space=pl.ANY),
        scratch_shapes=[pltpu.SemaphoreType.DMA((N_DEV,)), pltpu.SemaphoreType.DMA((N_DEV,)), pltpu.SemaphoreType.DMA],
        compiler_params=pltpu.CompilerParams(has_side_effects=True),
    )(src)


def _sum_adam(parts, w, m, v, name, tm=512):
    _, R, _ = parts.shape
    tm = _tile_rows(R, tm)
    c1 = 1.0 / (1.0 - ADAM_B1 ** ADAM_STEP)
    c2 = 1.0 / (1.0 - ADAM_B2 ** ADAM_STEP)

    def body(p_ref, w_ref, m_ref, v_ref, g_ref, d_ref, nm_ref, nv_ref):
        g = p_ref[0].astype(F32)
        for j in range(1, N_DEV):
            g = g + p_ref[j].astype(F32)
        nm = ADAM_B1 * m_ref[...] + (1.0 - ADAM_B1) * g
        nv = ADAM_B2 * v_ref[...] + (1.0 - ADAM_B2) * (g * g)
        g_ref[...] = g
        nm_ref[...] = nm
        nv_ref[...] = nv
        d_ref[...] = -ADAM_LR * ((nm * c1) / (jnp.sqrt(nv * c2) + ADAM_EPS) + ADAM_WD * w_ref[...])

    row = _rows(tm, LANES)
    shp = jax.ShapeDtypeStruct((R, LANES), F32)
    return pl.pallas_call(body, name=name, grid=(R // tm,),
                          in_specs=[pl.BlockSpec((N_DEV, tm, LANES), lambda i: (0, i, 0)), row, row, row],
                          out_specs=[row] * 4, out_shape=[shp] * 4,
                          compiler_params=_cparams(("parallel",)))(parts, w, m, v)


def _tile_rows(R, pref):
    best = 16
    for t in range(16, pref + 1, 16):
        if R % t == 0:
            best = t
    return best


PACK_ALIGN = 16 * LANES
PACK_ROWS = 512 * LANES

SHARDED = (("w_ada", 1), ("w_in", 1), ("w2", 1), ("a2", 1), ("g2", 1), ("w_att_out", 1), ("w_rwkv_out", 0),
           ("w_o", 0), ("w_up", 1), ("conv_w", 1), ("w_down", 0))
REPLICATED = ("b_ada", "norm1_w", "b_gate", "mu_shift", "w0", "a0", "k_k", "k_a", "r_k", "lnx_w", "lnx_b",
              "norm2_w", "conv_b", "norm_f_w")
WEIGHTS = ("w_ada", "b_ada", "norm1_w", "w_in", "b_gate", "mu_shift", "w0", "w2", "a0", "a2", "g2", "k_k", "k_a", "r_k",
           "lnx_w", "lnx_b", "w_att_out", "w_rwkv_out", "w_o", "norm2_w", "w_up", "conv_w", "conv_b", "w_down", "norm_f_w")


def _pack(arrays):
    flat, layout, off = [], [], 0
    for a in arrays:
        n = a.size
        pad = (-n) % PACK_ALIGN
        flat.append(jnp.pad(a.reshape(-1), (0, pad)))
        layout.append((off, n, a.shape))
        off += n + pad
    tail = (-off) % PACK_ROWS
    if tail:
        flat.append(jnp.zeros((tail,), flat[0].dtype))
    return jnp.concatenate(flat).reshape(-1, LANES), layout


def _unpack(buf, layout):
    flat = buf.reshape(-1)
    return [flat[off:off + n].reshape(shape) for off, n, shape in layout]


def _regroup(t, d):
    S = t.shape[0]
    return t.reshape((S // d, d) + t.shape[1:]).swapaxes(0, 1).reshape(t.shape)


def _ungroup(t, d):
    S = t.shape[0]
    return t.reshape((d, S // d) + t.shape[1:]).swapaxes(0, 1).reshape(t.shape)


def _att_in(P):
    S = P.shape[0]
    z = P[:, C_ATT:C_ATT + ATT_IN].astype(BF16).reshape(S, 3, 3, ATT_HEADS, HEAD)
    per_group = [_regroup(z[:, g], d) for g, (_, d) in enumerate(ATT_PATTERNS)]
    z = jnp.stack(per_group, axis=0)
    z = z.transpose(2, 0, 3, 1, 4).reshape(3, 3 * ATT_HEADS, S, HEAD)
    return z[0], z[1], z[2]


def _att_natural(t):
    S = t.shape[1]
    t = t.reshape(3, ATT_HEADS, S, HEAD)
    return jnp.stack([_ungroup(t[g].swapaxes(0, 1), d).swapaxes(0, 1) for g, (_, d) in enumerate(ATT_PATTERNS)], axis=0)


def _att_regrouped(t):
    S = t.shape[2]
    out = jnp.stack([_regroup(t[g].swapaxes(0, 1), d).swapaxes(0, 1) for g, (_, d) in enumerate(ATT_PATTERNS)], axis=0)
    return out.reshape(3 * ATT_HEADS, S, HEAD)


def _att_grad_cols(dq, dk, dv):
    S = dq.shape[1]
    z = jnp.stack([dq, dk, dv], axis=0).astype(BF16).reshape(3, 3, ATT_HEADS, S, HEAD)
    per_group = [_ungroup(z[:, g].transpose(2, 0, 1, 3), d) for g, (_, d) in enumerate(ATT_PATTERNS)]
    return jnp.stack(per_group, axis=1).reshape(S, ATT_IN)


def _pad_w_in(w_in):
    rkv = w_in[:, ATT_IN:ATT_IN + 3 * D]
    lora = w_in[:, ATT_IN + 3 * D:ATT_IN + RWKV_IN]
    gates = w_in[:, ATT_IN + RWKV_IN:]
    att = w_in[:, :ATT_IN]
    lw, la, lg = lora[:, :LORA_W], lora[:, LORA_W:LORA_W + LORA_A], lora[:, LORA_W + LORA_A:]
    zeros = jnp.zeros((w_in.shape[0], LORA_PAD - LANES - LORA_G), w_in.dtype)
    return jnp.concatenate([rkv, gates, att, lw, la, lg, zeros], axis=1)


def _unpad_w_in(g):
    att = g[:, C_ATT:C_ATT + ATT_IN]
    rkv = g[:, C_R:C_R + 3 * D]
    lora = jnp.concatenate([g[:, C_LORA:C_LORA + LORA_W + LORA_A], g[:, C_LORA + LANES:C_LORA + LANES + LORA_G]], axis=1)
    gates = g[:, C_GA:C_GA + 2 * D]
    return jnp.concatenate([att, rkv, lora, gates], axis=1)


def _pad_mu(mu):
    lo = mu[:, 3 * D:]
    mu_l = jnp.concatenate([lo[:, :LORA_W + LORA_A], lo[:, LORA_W + LORA_A:], jnp.zeros((1, LORA_PAD - LANES - LORA_G), mu.dtype)], axis=1)
    return mu[:, :D], mu[:, D:2 * D], mu[:, 2 * D:3 * D], mu_l


def _local_step(x, c, W, target):
    S = x.shape[0]
    G = {}
    c8 = jnp.pad(c, ((0, SUBLANES - 1), (0, 0)))
    ada = _ada_fwd(c8, W["w_ada"], W["b_ada"])[0:1]
    sh1, sc1, gt1, sh2, sc2, gt2 = [ada[:, i * D:(i + 1) * D] for i in range(6)]
    h1, rstd1 = _norm_fwd(x, None, None, W["norm1_w"], sc1, sh1, "norm1_fwd")
    w_in_p = _pad_w_in(W["w_in"])
    P = _mm(h1, w_in_p, "nn", F32, "proj_in")

    q, k, v = _att_in(P)
    o_g, l_g = _att_fwd(q, k, v)
    o_nat = _att_natural(o_g).reshape(3, -1, LANES)
    l_nat = _att_natural(l_g).reshape(3, -1, LANES)
    att_hm = _att_combine_fwd(o_nat, l_nat)
    att = att_hm.reshape(ATT_HEADS, S, HEAD).swapaxes(0, 1).reshape(S, ATT_W).astype(BF16)
    y_att = _mm(att, W["w_att_out"], "nn", F32, "att_out")

    mu_r, mu_k, mu_v, mu_l = _pad_mu(W["mu_shift"])
    g2p = jnp.pad(W["g2"], ((0, G_PAD - LORA_G), (0, 0)))
    prep_params = [mu_r, mu_k, mu_v, mu_l, W["w0"], W["a0"], W["k_k"], W["k_a"], W["w2"], W["a2"], g2p]
    r_, dec, kmod, v_, aa, bb, gg = _rwkv_prep(P, prep_params)
    y_scan, ckpt = _scan_fwd(r_, dec, kmod, v_, aa, bb)
    r_k = W["r_k"].reshape(1, D)
    rw = _rwkv_post(y_scan, r_, kmod, v_, gg, W["lnx_w"], W["lnx_b"], r_k)
    y_rwkv = _mm(rw, W["w_rwkv_out"], "nn", F32, "rwkv_out")

    bga, bgr = W["b_gate"][:, :D], W["b_gate"][:, D:]
    mix = _gate_fwd(P, bga, bgr, y_att, y_rwkv)
    mo = _mm(mix, W["w_o"], "nn", F32, "mix_out")
    x2, h2, rstd2 = _norm_fwd(x, mo, gt1, W["norm2_w"], sc2, sh2, "norm2_fwd")
    u = _mm(h2, W["w_up"], "nn", F32, "ffn_up")
    conv_w8 = jnp.pad(W["conv_w"], ((0, SUBLANES - 3), (0, 0)))
    act = _conv_fwd(u, conv_w8, W["conv_b"])
    f = _mm(act, W["w_down"], "nn", F32, "ffn_down")
    loss_blk, dx3, df, dgt2, G["norm_f_w"] = _final(x2, f, gt2, W["norm_f_w"], target)
    loss = loss_blk[0, 0]

    dact = _mm(df, W["w_down"], "nt", BF16, "ffn_down_dx")
    G["w_down"] = _mm(act, df, "tn", F32, "ffn_down_dw")
    dug, duv, dwg, dwv, dbg, dbv = _conv_bwd_a(dact, u, conv_w8, W["conv_b"])
    G["conv_w"] = jnp.concatenate([dwg[0:3], dwv[0:3]], axis=1)
    G["conv_b"] = jnp.concatenate([dbg, dbv], axis=1)
    du = jnp.concatenate([_conv_bwd_b(dug, conv_w8[:, :D_FF]), _conv_bwd_b(duv, conv_w8[:, D_FF:])], axis=1)
    dh2 = _mm(du, W["w_up"], "nt", F32, "ffn_up_dx")
    G["w_up"] = _mm(h2, du, "tn", F32, "ffn_up_dw")
    dx2, dsh2, dsc2, G["norm2_w"], dmo, dgt1 = _norm_bwd(dh2, x2, rstd2, W["norm2_w"], sc2, dx3, mo, gt1, "norm2_bwd")
    dmix = _mm(dmo, W["w_o"], "nt", F32, "mix_out_dx")
    G["w_o"] = _mm(mix, dmo, "tn", F32, "mix_out_dw")
    dy_att, dy_rwkv, dpga, dpgr, dbga, dbgr = _gate_bwd(dmix, P, bga, bgr, y_att, y_rwkv)
    G["b_gate"] = jnp.concatenate([dbga, dbgr], axis=1)

    datt = _mm(dy_att, W["w_att_out"], "nt", F32, "att_out_dx")
    G["w_att_out"] = _mm(att, dy_att, "tn", F32, "att_out_dw")
    datt_hm = datt.reshape(S, ATT_HEADS, HEAD).swapaxes(0, 1).reshape(-1, LANES)
    do_nat, dl_nat = _att_combine_bwd(datt_hm, o_nat, l_nat)
    do_g = _att_regrouped(do_nat.reshape(3, ATT_HEADS, S, HEAD))
    dl_g = _att_regrouped(dl_nat.reshape(3, ATT_HEADS, S, HEAD))
    dq, dk, dv = _att_bwd(q, k, v, o_g, l_g, do_g, dl_g)
    dp_att = _att_grad_cols(dq, dk, dv)

    drw = _mm(dy_rwkv, W["w_rwkv_out"], "nt", F32, "rwkv_out_dx")
    G["w_rwkv_out"] = _mm(rw, dy_rwkv, "tn", F32, "rwkv_out_dw")
    dy_scan, dr1, dk1, dv1, dgg, G["lnx_w"], G["lnx_b"], drk = _rwkv_post_bwd(drw, y_scan, r_, kmod, v_, gg, W["lnx_w"], W["lnx_b"], r_k)
    G["r_k"] = drk.reshape(W["r_k"].shape)
    dr2, ddec, dk2, dv2, daa, dbb = _scan_bwd(r_, dec, kmod, v_, aa, bb, ckpt, dy_scan)
    pb = _rwkv_prep_bwd(P, prep_params, [dr2, ddec, dk2, dv2, daa, dbb, dgg], [dr1, None, dk1, dv1, None, None, None])
    dz, dzp, dpar = pb[0:4], pb[4:8], pb[8:]
    dp_rkv = [_shift_add(dz[i], dzp[i]) for i in range(3)]
    dp_lora = _shift_add(dz[3], dzp[3])
    dmu_r, dmu_k, dmu_v, dmu_l, G["w0"], G["a0"], G["k_k"], G["k_a"], G["w2"], G["a2"], dg2p = dpar
    G["g2"] = dg2p[0:LORA_G]
    G["mu_shift"] = jnp.concatenate([dmu_r, dmu_k, dmu_v, dmu_l[:, :LORA_W + LORA_A], dmu_l[:, LANES:LANES + LORA_G]], axis=1)

    dP = jnp.concatenate(dp_rkv + [dpga, dpgr, dp_att, dp_lora], axis=1)
    dh1 = _mm(dP, w_in_p, "nt", F32, "proj_in_dx")
    G["w_in"] = _unpad_w_in(_mm(h1, dP, "tn", F32, "proj_in_dw"))
    grad_x, dsh1, dsc1, G["norm1_w"] = _norm_bwd(dh1, x, rstd1, W["norm1_w"], sc1, dx2, None, None, "norm1_bwd")
    dada = jnp.concatenate([dsh1, dsc1, dgt1, dsh2, dsc2, dgt2], axis=1)
    G["b_ada"] = dada
    G["w_ada"] = _outer(c.reshape(D, 1), dada)
    return loss, grad_x, G


def _gather_full(bufs, layout):
    per_dev = [_unpack(bufs[d], layout) for d in range(N_DEV)]
    full = {}
    for i, (name, axis) in enumerate(SHARDED):
        full[name] = jnp.concatenate([per_dev[d][i] for d in range(N_DEV)], axis=axis)
    return full


def _owner_blocks(G):
    blocks = []
    for j in range(N_DEV):
        shards = []
        for name, axis in SHARDED:
            g = G[name]
            n = g.shape[axis] // N_DEV
            shards.append(lax.slice_in_dim(g, j * n, (j + 1) * n, axis=axis).astype(BF16))
        buf, layout = _pack(shards)
        blocks.append(buf)
    return jnp.stack(blocks, axis=0), layout


def kernel(x, c, w_ada, b_ada, norm1_w, w_in, b_gate, mu_shift, w0, w2, a0, a2, g2, k_k, k_a, r_k, lnx_w, lnx_b, w_att_out, w_rwkv_out, w_o, norm2_w, w_up, conv_w, conv_b, w_down, norm_f_w, loss_target, m_w_ada, m_b_ada, m_norm1_w, m_w_in, m_b_gate, m_mu_shift, m_w0, m_w2, m_a0, m_a2, m_g2, m_k_k, m_k_a, m_r_k, m_lnx_w, m_lnx_b, m_w_att_out, m_w_rwkv_out, m_w_o, m_norm2_w, m_w_up, m_conv_w, m_conv_b, m_w_down, m_norm_f_w, v_w_ada, v_b_ada, v_norm1_w, v_w_in, v_b_gate, v_mu_shift, v_w0, v_w2, v_a0, v_a2, v_g2, v_k_k, v_k_a, v_r_k, v_lnx_w, v_lnx_b, v_w_att_out, v_w_rwkv_out, v_w_o, v_norm2_w, v_w_up, v_conv_w, v_conv_b, v_w_down, v_norm_f_w):
    env = dict(locals())
    w_shard = {n: env[n] for n in WEIGHTS}
    m_shard = {n: env["m_" + n] for n in WEIGHTS}
    v_shard = {n: env["v_" + n] for n in WEIGHTS}

    mine, layout = _pack([w_shard[n][0].astype(BF16) for n, _ in SHARDED])
    gathered = _exchange(mine, True, "gather_weights")
    W = _gather_full(gathered, layout)
    for n in REPLICATED:
        W[n] = w_shard[n].reshape(1, -1) if n != "r_k" else w_shard[n][0]

    loss, grad_x, G = _local_step(x[0], c, W, loss_target[0])
    loss = lax.psum(loss, ("x", "y", "c"))

    blocks, glayout = _owner_blocks(G)
    parts = _exchange(blocks, False, "scatter_grads")
    pw, _ = _pack([w_shard[n][0] for n, _ in SHARDED])
    pm, _ = _pack([m_shard[n][0] for n, _ in SHARDED])
    pv, _ = _pack([v_shard[n][0] for n, _ in SHARDED])
    res = _sum_adam(parts, pw, pm, pv, "adam_sharded")
    out = {}
    for kind, buf in zip(("grad", "delta", "new_m", "new_v"), res):
        for (n, _), a in zip(SHARDED, _unpack(buf, glayout)):
            out[kind, n] = a.reshape(w_shard[n].shape)

    small, slayout = _pack([G[n].reshape(-1) for n in REPLICATED])
    sparts = _exchange(small, True, "gather_small_grads")
    sw, _ = _pack([w_shard[n].reshape(-1) for n in REPLICATED])
    sm, _ = _pack([m_shard[n].reshape(-1) for n in REPLICATED])
    sv, _ = _pack([v_shard[n].reshape(-1) for n in REPLICATED])
    res = _sum_adam(sparts, sw, sm, sv, "adam_replicated")
    for kind, buf in zip(("grad", "delta", "new_m", "new_v"), res):
        for n, a in zip(REPLICATED, _unpack(buf, slayout)):
            out[kind, n] = a.reshape(w_shard[n].shape)

    return (loss, grad_x[None], *[out[kind, n] for kind in ("grad", "delta", "new_m", "new_v") for n in WEIGHTS])
```

```python
import functools
import math

import jax
import jax.numpy as jnp
from jax import lax
from jax.experimental import pallas as pl
from jax.experimental.pallas import tpu as pltpu

F32 = jnp.float32
BF16 = jnp.bfloat16

D = 1024
HEAD = 64
ATT_PATTERNS = ((128, 1), (512, 4), (2048, 16))
ATT_HEADS = 8
ATT_W = ATT_HEADS * HEAD
ATT_IN = 3 * 3 * ATT_W
QBLK = 128
N_HEADS = D // HEAD
LORA_W, LORA_A, LORA_G = 64, 64, 160
RWKV_IN = 3 * D + LORA_W + LORA_A + LORA_G
N_IN = ATT_IN + RWKV_IN + 2 * D
D_FF = 2816
RMS_EPS = 1e-6
GN_EPS = 64e-5
N_DEV = 8
LANES = 128
SUBLANES = 8

C_R, C_K, C_V, C_GA, C_GR = 0, 1024, 2048, 3072, 4096
C_ATT = 5120
C_LORA = C_ATT + ATT_IN
LORA_PAD = 512
G_PAD = 256
N_PAD = C_LORA + LORA_PAD

ADAM_LR, ADAM_B1, ADAM_B2, ADAM_EPS, ADAM_WD, ADAM_STEP = 0.001, 0.9, 0.999, 1e-08, 0.01, 10

SCAN_TB = 128
VMEM_LIMIT = 56 * 1024 * 1024

_MESH = pl.DeviceIdType.MESH


def _cparams(sem):
    return pltpu.CompilerParams(dimension_semantics=sem, vmem_limit_bytes=VMEM_LIMIT)


def _tile(dim, pref):
    if dim <= pref:
        return dim
    best = None
    for t in range(LANES, pref + 1, LANES):
        if dim % t == 0:
            best = t
    assert best is not None, dim
    return best


def _mm(a, b, mode, out_dtype, name, tm=512, tn=1024, tk=512):
    if mode == "nn":
        (M, K), (K2, N) = a.shape, b.shape
    elif mode == "nt":
        (M, K), (N, K2) = a.shape, b.shape
    else:
        (K, M), (K2, N) = a.shape, b.shape
    assert K == K2, (a.shape, b.shape, mode)
    tm, tn, tk = _tile(M, tm), _tile(N, tn), _tile(K, tk)
    nk = K // tk
    dims = {"nn": (((1,), (0,)), ((), ())), "nt": (((1,), (1,)), ((), ())), "tn": (((0,), (0,)), ((), ()))}[mode]

    def body(a_ref, b_ref, o_ref, acc_ref):
        k = pl.program_id(2)
        part = lax.dot_general(a_ref[...].astype(BF16), b_ref[...].astype(BF16), dims,
                               preferred_element_type=F32)

        @pl.when(k == 0)
        def _():
            acc_ref[...] = part

        @pl.when(k > 0)
        def _():
            acc_ref[...] += part

        @pl.when(k == nk - 1)
        def _():
            o_ref[...] = acc_ref[...].astype(o_ref.dtype)

    a_spec = pl.BlockSpec((tk, tm), lambda i, j, k: (k, i)) if mode == "tn" else pl.BlockSpec((tm, tk), lambda i, j, k: (i, k))
    b_spec = pl.BlockSpec((tn, tk), lambda i, j, k: (j, k)) if mode == "nt" else pl.BlockSpec((tk, tn), lambda i, j, k: (k, j))
    return pl.pallas_call(
        body, name=name, grid=(M // tm, N // tn, nk),
        in_specs=[a_spec, b_spec],
        out_specs=pl.BlockSpec((tm, tn), lambda i, j, k: (i, j)),
        out_shape=jax.ShapeDtypeStruct((M, N), out_dtype),
        scratch_shapes=[pltpu.VMEM((tm, tn), F32)],
        compiler_params=_cparams(("parallel", "parallel", "arbitrary")),
    )(a, b)


def _rows(tm, w, col=0):
    return pl.BlockSpec((tm, w), lambda i: (i, col))


def _full(shape):
    return pl.BlockSpec(shape, lambda i: (0,) * len(shape))


def _prev8(tm, w, col=0):
    return pl.BlockSpec((SUBLANES, w), lambda i: (jnp.maximum(i * (tm // SUBLANES) - 1, 0), col))


def _next8(tm, w, n_rows, col=0):
    last = n_rows // SUBLANES - 1
    return pl.BlockSpec((SUBLANES, w), lambda i: (jnp.minimum((i + 1) * (tm // SUBLANES), last), col))


def _shift_down(x, halo, k, first):
    rolled = pltpu.roll(x, k, 0)
    row = lax.broadcasted_iota(jnp.int32, x.shape, 0)
    out = rolled
    for j in range(k):
        h = jnp.where(first, 0.0, halo[SUBLANES - k + j:SUBLANES - k + j + 1, :])
        out = jnp.where(row == j, h, out)
    return out


def _shift_up(x, halo, k, last):
    n = x.shape[0]
    rolled = pltpu.roll(x, n - k, 0)
    row = lax.broadcasted_iota(jnp.int32, x.shape, 0)
    out = rolled
    for j in range(k):
        h = jnp.where(last, 0.0, halo[j:j + 1, :])
        out = jnp.where(row == n - k + j, h, out)
    return out


def _acc(ref, val, first):
    @pl.when(first)
    def _():
        ref[...] = val

    @pl.when(jnp.logical_not(first))
    def _():
        ref[...] += val


def _colsum(x):
    return jnp.sum(x, axis=0, keepdims=True)


def _norm_fwd(x, mo, gt, nw, sc, sh, name, tm=256):
    S = x.shape[0]
    has_res = mo is not None

    def body(*refs):
        if has_res:
            x_ref, mo_ref, gt_ref, nw_ref, sc_ref, sh_ref, x2_ref, h_ref, rs_ref = refs
            x2 = x_ref[...] + gt_ref[...] * mo_ref[...]
            x2_ref[...] = x2
        else:
            x_ref, nw_ref, sc_ref, sh_ref, h_ref, rs_ref = refs
            x2 = x_ref[...]
        rstd = lax.rsqrt(jnp.mean(x2 * x2, axis=-1, keepdims=True) + RMS_EPS)
        rs_ref[...] = rstd
        h_ref[...] = ((x2 * rstd * nw_ref[...]) * (1.0 + sc_ref[...]) + sh_ref[...]).astype(BF16)

    vec = _full((1, D))
    ins = [x, mo, gt, nw, sc, sh] if has_res else [x, nw, sc, sh]
    in_specs = [_rows(tm, D), _rows(tm, D), vec, vec, vec, vec] if has_res else [_rows(tm, D), vec, vec, vec]
    outs = [jax.ShapeDtypeStruct((S, D), BF16), jax.ShapeDtypeStruct((S, 1), F32)]
    out_specs = [_rows(tm, D), _rows(tm, 1)]
    if has_res:
        outs = [jax.ShapeDtypeStruct((S, D), F32)] + outs
        out_specs = [_rows(tm, D)] + out_specs
    return pl.pallas_call(body, name=name, grid=(S // tm,), in_specs=in_specs, out_specs=out_specs,
                          out_shape=outs, compiler_params=_cparams(("parallel",)))(*ins)


def _norm_bwd(dh, xin, rstd, nw, sc, dres, mo, gt, name, tm=256):
    S = xin.shape[0]
    has_res = mo is not None

    def body(*refs):
        if has_res:
            dh_ref, x_ref, rs_ref, nw_ref, sc_ref, dres_ref, mo_ref, gt_ref, dx_ref, dsh_ref, dsc_ref, dnw_ref, dmo_ref, dgt_ref = refs
        else:
            dh_ref, x_ref, rs_ref, nw_ref, sc_ref, dres_ref, dx_ref, dsh_ref, dsc_ref, dnw_ref = refs
        first = pl.program_id(0) == 0
        dh = dh_ref[...]
        rstd = rs_ref[...]
        n = x_ref[...] * rstd
        w = nw_ref[...]
        _acc(dsh_ref, _colsum(dh), first)
        _acc(dsc_ref, _colsum(dh * (n * w)), first)
        dnw = dh * (1.0 + sc_ref[...])
        _acc(dnw_ref, _colsum(dnw * n), first)
        dn = dnw * w
        dx = dres_ref[...] + rstd * (dn - n * jnp.mean(dn * n, axis=-1, keepdims=True))
        dx_ref[...] = dx
        if has_res:
            dmo_ref[...] = (dx * gt_ref[...]).astype(BF16)
            _acc(dgt_ref, _colsum(dx * mo_ref[...]), first)

    vec = _full((1, D))
    vshape = jax.ShapeDtypeStruct((1, D), F32)
    ins = [dh, xin, rstd, nw, sc, dres] + ([mo, gt] if has_res else [])
    in_specs = [_rows(tm, D), _rows(tm, D), _rows(tm, 1), vec, vec, _rows(tm, D)] + ([_rows(tm, D), vec] if has_res else [])
    outs = [jax.ShapeDtypeStruct((S, D), F32), vshape, vshape, vshape]
    out_specs = [_rows(tm, D), vec, vec, vec]
    if has_res:
        outs += [jax.ShapeDtypeStruct((S, D), BF16), vshape]
        out_specs += [_rows(tm, D), vec]
    return pl.pallas_call(body, name=name, grid=(S // tm,), in_specs=in_specs, out_specs=out_specs,
                          out_shape=outs, compiler_params=_cparams(("arbitrary",)))(*ins)


def _final(x2, f, gt2, nfw, target, tm=256):
    S = x2.shape[0]

    def body(x2_ref, f_ref, gt_ref, w_ref, t_ref, loss_ref, dx_ref, df_ref, dgt_ref, dw_ref):
        first = pl.program_id(0) == 0
        f = f_ref[...]
        gt = gt_ref[...]
        w = w_ref[...]
        x3 = x2_ref[...] + gt * f
        rstd = lax.rsqrt(jnp.mean(x3 * x3, axis=-1, keepdims=True) + RMS_EPS)
        n = x3 * rstd
        e = n * w - t_ref[...]
        part = 0.5 * jnp.sum(jnp.mean(e * e, axis=-1, keepdims=True), axis=0, keepdims=True)
        _acc(loss_ref, jnp.broadcast_to(part, (SUBLANES, LANES)), first)
        dy = e * (1.0 / D)
        _acc(dw_ref, _colsum(dy * n), first)
        dn = dy * w
        dx = rstd * (dn - n * jnp.mean(dn * n, axis=-1, keepdims=True))
        dx_ref[...] = dx
        df_ref[...] = (dx * gt).astype(BF16)
        _acc(dgt_ref, _colsum(dx * f), first)

    vec = _full((1, D))
    vshape = jax.ShapeDtypeStruct((1, D), F32)
    return pl.pallas_call(
        body, name="final_loss", grid=(S // tm,),
        in_specs=[_rows(tm, D), _rows(tm, D), vec, vec, _rows(tm, D)],
        out_specs=[_full((SUBLANES, LANES)), _rows(tm, D), _rows(tm, D), vec, vec],
        out_shape=[jax.ShapeDtypeStruct((SUBLANES, LANES), F32), jax.ShapeDtypeStruct((S, D), F32),
                   jax.ShapeDtypeStruct((S, D), BF16), vshape, vshape],
        compiler_params=_cparams(("arbitrary",)))(x2, f, gt2, nfw, target)


def _gate_fwd(P, bga, bgr, y_att, y_rwkv, tm=256):
    S = P.shape[0]

    def body(pa_ref, pr_ref, ba_ref, br_ref, ya_ref, yr_ref, mix_ref):
        ga = jax.nn.sigmoid(pa_ref[...] + ba_ref[...])
        gr = jax.nn.sigmoid(pr_ref[...] + br_ref[...])
        mix_ref[...] = (ga * ya_ref[...] + gr * yr_ref[...]).astype(BF16)

    vec = _full((1, D))
    return pl.pallas_call(
        body, name="gate_fwd", grid=(S // tm,),
        in_specs=[_rows(tm, D, C_GA // D), _rows(tm, D, C_GR // D), vec, vec, _rows(tm, D), _rows(tm, D)],
        out_specs=_rows(tm, D), out_shape=jax.ShapeDtypeStruct((S, D), BF16),
        compiler_params=_cparams(("parallel",)))(P, P, bga, bgr, y_att, y_rwkv)


def _gate_bwd(dmix, P, bga, bgr, y_att, y_rwkv, tm=256):
    S = P.shape[0]

    def body(dm_ref, pa_ref, pr_ref, ba_ref, br_ref, ya_ref, yr_ref, dya_ref, dyr_ref, dpa_ref, dpr_ref, dba_ref, dbr_ref):
        first = pl.program_id(0) == 0
        dm = dm_ref[...]
        ga = jax.nn.sigmoid(pa_ref[...] + ba_ref[...])
        gr = jax.nn.sigmoid(pr_ref[...] + br_ref[...])
        dya_ref[...] = (dm * ga).astype(BF16)
        dyr_ref[...] = (dm * gr).astype(BF16)
        dpa = dm * ya_ref[...] * ga * (1.0 - ga)
        dpr = dm * yr_ref[...] * gr * (1.0 - gr)
        dpa_ref[...] = dpa.astype(BF16)
        dpr_ref[...] = dpr.astype(BF16)
        _acc(dba_ref, _colsum(dpa), first)
        _acc(dbr_ref, _colsum(dpr), first)

    vec = _full((1, D))
    row = _rows(tm, D)
    rshape = jax.ShapeDtypeStruct((S, D), BF16)
    vshape = jax.ShapeDtypeStruct((1, D), F32)
    return pl.pallas_call(
        body, name="gate_bwd", grid=(S // tm,),
        in_specs=[row, _rows(tm, D, C_GA // D), _rows(tm, D, C_GR // D), vec, vec, row, row],
        out_specs=[row, row, row, row, vec, vec],
        out_shape=[rshape, rshape, rshape, rshape, vshape, vshape],
        compiler_params=_cparams(("arbitrary",)))(dmix, P, P, bga, bgr, y_att, y_rwkv)


def _conv_fwd(u, conv_w8, conv_b, tm=256, tn=256):
    S = u.shape[0]
    nj = D_FF // tn

    def conv(u_ref, h_ref, w_ref, b_ref, first):
        u = u_ref[...]
        h = h_ref[...]
        w = w_ref[...]
        return b_ref[...] + w[0:1] * _shift_down(u, h, 2, first) + w[1:2] * _shift_down(u, h, 1, first) + w[2:3] * u

    def body(ug_ref, hg_ref, uv_ref, hv_ref, wg_ref, wv_ref, bg_ref, bv_ref, act_ref):
        first = pl.program_id(0) == 0
        g = conv(ug_ref, hg_ref, wg_ref, bg_ref, first)
        v = conv(uv_ref, hv_ref, wv_ref, bv_ref, first)
        act_ref[...] = (g * jax.nn.sigmoid(g) * v).astype(BF16)

    blk = lambda off: pl.BlockSpec((tm, tn), lambda i, j: (i, j + off))
    halo = lambda off: pl.BlockSpec((SUBLANES, tn), lambda i, j: (jnp.maximum(i * (tm // SUBLANES) - 1, 0), j + off))
    wsp = lambda off: pl.BlockSpec((SUBLANES, tn), lambda i, j: (0, j + off))
    bsp = lambda off: pl.BlockSpec((1, tn), lambda i, j: (0, j + off))
    return pl.pallas_call(
        body, name="conv_fwd", grid=(S // tm, nj),
        in_specs=[blk(0), halo(0), blk(nj), halo(nj), wsp(0), wsp(nj), bsp(0), bsp(nj)],
        out_specs=pl.BlockSpec((tm, tn), lambda i, j: (i, j)),
        out_shape=jax.ShapeDtypeStruct((S, D_FF), BF16),
        compiler_params=_cparams(("parallel", "parallel")))(u, u, u, u, conv_w8, conv_w8, conv_b, conv_b)


def _conv_bwd_a(dact, u, conv_w8, conv_b, tm=256, tn=256):
    S = u.shape[0]
    nj = D_FF // tn

    def half(u_ref, h_ref, w_ref, b_ref, first):
        u = u_ref[...]
        h = h_ref[...]
        w = w_ref[...]
        u2, u1 = _shift_down(u, h, 2, first), _shift_down(u, h, 1, first)
        return b_ref[...] + w[0:1] * u2 + w[1:2] * u1 + w[2:3] * u, (u2, u1, u)

    def wgrad(d, taps):
        z = jnp.zeros((SUBLANES - 3, d.shape[1]), F32)
        return jnp.concatenate([_colsum(d * taps[0]), _colsum(d * taps[1]), _colsum(d * taps[2]), z], axis=0)

    def body(da_ref, ug_ref, hg_ref, uv_ref, hv_ref, wg_ref, wv_ref, bg_ref, bv_ref,
             dg_ref, dv_ref, dwg_ref, dwv_ref, dbg_ref, dbv_ref):
        first = pl.program_id(1) == 0
        g, tg = half(ug_ref, hg_ref, wg_ref, bg_ref, first)
        v, tv = half(uv_ref, hv_ref, wv_ref, bv_ref, first)
        da = da_ref[...].astype(F32)
        sg = jax.nn.sigmoid(g)
        dg = da * v * (sg * (1.0 + g * (1.0 - sg)))
        dv = da * (g * sg)
        dg_ref[...] = dg
        dv_ref[...] = dv
        _acc(dwg_ref, wgrad(dg, tg), first)
        _acc(dwv_ref, wgrad(dv, tv), first)
        _acc(dbg_ref, _colsum(dg), first)
        _acc(dbv_ref, _colsum(dv), first)

    blk = lambda off: pl.BlockSpec((tm, tn), lambda j, i: (i, j + off))
    halo = lambda off: pl.BlockSpec((SUBLANES, tn), lambda j, i: (jnp.maximum(i * (tm // SUBLANES) - 1, 0), j + off))
    wsp = lambda off: pl.BlockSpec((SUBLANES, tn), lambda j, i: (0, j + off))
    bsp = lambda off: pl.BlockSpec((1, tn), lambda j, i: (0, j + off))
    f = jax.ShapeDtypeStruct
    outs = pl.pallas_call(
        body, name="conv_bwd_a", grid=(nj, S // tm),
        in_specs=[pl.BlockSpec((tm, tn), lambda j, i: (i, j)), blk(0), halo(0), blk(nj), halo(nj), wsp(0), wsp(nj), bsp(0), bsp(nj)],
        out_specs=[pl.BlockSpec((tm, tn), lambda j, i: (i, j)), pl.BlockSpec((tm, tn), lambda j, i: (i, j)),
                   pl.BlockSpec((SUBLANES, tn), lambda j, i: (0, j)), pl.BlockSpec((SUBLANES, tn), lambda j, i: (0, j)),
                   pl.BlockSpec((1, tn), lambda j, i: (0, j)), pl.BlockSpec((1, tn), lambda j, i: (0, j))],
        out_shape=[f((S, D_FF), F32), f((S, D_FF), F32), f((SUBLANES, D_FF), F32), f((SUBLANES, D_FF), F32),
                   f((1, D_FF), F32), f((1, D_FF), F32)],
        compiler_params=_cparams(("parallel", "arbitrary")))(dact, u, u, u, u, conv_w8, conv_w8, conv_b, conv_b)
    return outs


def _conv_bwd_b(duc, conv_w8, tm=256, tn=256):
    S, W = duc.shape

    def body(d_ref, h_ref, w_ref, o_ref):
        last = pl.program_id(0) == pl.num_programs(0) - 1
        d = d_ref[...]
        h = h_ref[...]
        w = w_ref[...]
        o_ref[...] = (w[2:3] * d + w[1:2] * _shift_up(d, h, 1, last) + w[0:1] * _shift_up(d, h, 2, last)).astype(BF16)

    last_tile = S // SUBLANES - 1
    return pl.pallas_call(
        body, name="conv_bwd_b", grid=(S // tm, W // tn),
        in_specs=[pl.BlockSpec((tm, tn), lambda i, j: (i, j)),
                  pl.BlockSpec((SUBLANES, tn), lambda i, j: (jnp.minimum((i + 1) * (tm // SUBLANES), last_tile), j)),
                  pl.BlockSpec((SUBLANES, tn), lambda i, j: (0, j))],
        out_specs=pl.BlockSpec((tm, tn), lambda i, j: (i, j)),
        out_shape=jax.ShapeDtypeStruct((S, W), BF16),
        compiler_params=_cparams(("parallel", "parallel")))(duc, duc, conv_w8)


ATT_SCALE = HEAD ** -0.5
NEG = -1e30


def _att_blocks_per_seq(S):
    return [S // (QBLK * d) for (_, d) in ATT_PATTERNS]


def _att_scores(q, kc, kp, has_prev):
    qi = lax.broadcasted_iota(jnp.int32, (QBLK, QBLK), 0)
    kj = lax.broadcasted_iota(jnp.int32, (QBLK, QBLK), 1)
    nt = (((1,), (1,)), ((), ()))
    s_c = lax.dot_general(q, kc, nt, preferred_element_type=F32) * ATT_SCALE
    s_p = lax.dot_general(q, kp, nt, preferred_element_type=F32) * ATT_SCALE
    s_c = jnp.where(kj <= qi, s_c, NEG)
    s_p = jnp.where(jnp.logical_and(kj >= qi, has_prev), s_p, NEG)
    return s_c, s_p


def _att_fwd(q, k, v):
    GH, S, _ = q.shape
    nb = S // QBLK
    bps = _att_blocks_per_seq(S)

    def body(q_ref, k_ref, v_ref, o_ref, l_ref):
        g = pl.program_id(0) // ATT_HEADS
        per = jnp.where(g == 0, bps[0], jnp.where(g == 1, bps[1], bps[2]))

        def blk(n, carry):
            cur = pl.ds(pl.multiple_of(n * QBLK, QBLK), QBLK)
            prv = pl.ds(pl.multiple_of(jnp.maximum(n - 1, 0) * QBLK, QBLK), QBLK)
            s_c, s_p = _att_scores(q_ref[cur, :], k_ref[cur, :], k_ref[prv, :], (n % per) != 0)
            m = jnp.maximum(jnp.max(s_c, axis=1, keepdims=True), jnp.max(s_p, axis=1, keepdims=True))
            p_c = jnp.exp(s_c - m)
            p_p = jnp.exp(s_p - m)
            den = jnp.sum(p_c, axis=1, keepdims=True) + jnp.sum(p_p, axis=1, keepdims=True)
            num = (jnp.dot(p_c.astype(BF16), v_ref[cur, :], preferred_element_type=F32)
                   + jnp.dot(p_p.astype(BF16), v_ref[prv, :], preferred_element_type=F32))
            o_ref[cur, :] = num / den
            l_ref[cur, :] = jnp.broadcast_to(m + jnp.log(den), (QBLK, HEAD))
            return carry

        lax.fori_loop(0, nb, blk, 0)

    spec = pl.BlockSpec((None, S, HEAD), lambda i: (i, 0, 0))
    shp = jax.ShapeDtypeStruct((GH, S, HEAD), F32)
    return pl.pallas_call(body, name="att_fwd", grid=(GH,), in_specs=[spec, spec, spec], out_specs=[spec, spec],
                          out_shape=[shp, shp], compiler_params=_cparams(("parallel",)))(q, k, v)


def _att_bwd(q, k, v, o, l, do, dl):
    GH, S, _ = q.shape
    nb = S // QBLK
    bps = _att_blocks_per_seq(S)
    tn = (((0,), (0,)), ((), ()))
    nt = (((1,), (1,)), ((), ()))

    def body(q_ref, k_ref, v_ref, o_ref, l_ref, do_ref, dl_ref, dq_ref, dk_ref, dv_ref):
        g = pl.program_id(0) // ATT_HEADS
        per = jnp.where(g == 0, bps[0], jnp.where(g == 1, bps[1], bps[2]))
        dk_ref[...] = jnp.zeros_like(dk_ref)
        dv_ref[...] = jnp.zeros_like(dv_ref)

        def blk(n, carry):
            cur = pl.ds(pl.multiple_of(n * QBLK, QBLK), QBLK)
            prv = pl.ds(pl.multiple_of(jnp.maximum(n - 1, 0) * QBLK, QBLK), QBLK)
            qb = q_ref[cur, :]
            kc, kp, vc, vp = k_ref[cur, :], k_ref[prv, :], v_ref[cur, :], v_ref[prv, :]
            s_c, s_p = _att_scores(qb, kc, kp, (n % per) != 0)
            lse = l_ref[cur, :][:, 0:1]
            p_c = jnp.exp(s_c - lse)
            p_p = jnp.exp(s_p - lse)
            dob = do_ref[cur, :]
            delta = jnp.sum(dob * o_ref[cur, :] - dl_ref[cur, :], axis=1, keepdims=True)
            dob16 = dob.astype(BF16)
            dp_c = lax.dot_general(dob16, vc, nt, preferred_element_type=F32)
            dp_p = lax.dot_general(dob16, vp, nt, preferred_element_type=F32)
            ds_c = (p_c * (dp_c - delta) * ATT_SCALE).astype(BF16)
            ds_p = (p_p * (dp_p - delta) * ATT_SCALE).astype(BF16)
            dq_ref[cur, :] = (jnp.dot(ds_c, kc, preferred_element_type=F32) + jnp.dot(ds_p, kp, preferred_element_type=F32))
            dk_ref[cur, :] += lax.dot_general(ds_c, qb, tn, preferred_element_type=F32)
            dv_ref[cur, :] += lax.dot_general(p_c.astype(BF16), dob16, tn, preferred_element_type=F32)
            dk_ref[prv, :] += lax.dot_general(ds_p, qb, tn, preferred_element_type=F32)
            dv_ref[prv, :] += lax.dot_general(p_p.astype(BF16), dob16, tn, preferred_element_type=F32)
            return carry

        lax.fori_loop(0, nb, blk, 0)

    spec = pl.BlockSpec((None, S, HEAD), lambda i: (i, 0, 0))
    shp = jax.ShapeDtypeStruct((GH, S, HEAD), F32)
    return pl.pallas_call(body, name="att_bwd", grid=(GH,), in_specs=[spec] * 7, out_specs=[spec] * 3,
                          out_shape=[shp] * 3, compiler_params=_cparams(("parallel",)))(q, k, v, o, l, do, dl)


def _att_combine_fwd(o3, l3, tm=512):
    _, R, _ = o3.shape

    def body(o_ref, l_ref, a_ref):
        l0, l1, l2 = l_ref[0], l_ref[1], l_ref[2]
        m = jnp.maximum(jnp.maximum(l0, l1), l2)
        e0, e1, e2 = jnp.exp(l0 - m), jnp.exp(l1 - m), jnp.exp(l2 - m)
        a_ref[...] = (e0 * o_ref[0] + e1 * o_ref[1] + e2 * o_ref[2]) / (e0 + e1 + e2)

    spec3 = pl.BlockSpec((3, tm, LANES), lambda i: (0, i, 0))
    return pl.pallas_call(body, name="att_combine_fwd", grid=(R // tm,), in_specs=[spec3, spec3],
                          out_specs=_rows(tm, LANES), out_shape=jax.ShapeDtypeStruct((R, LANES), F32),
                          compiler_params=_cparams(("parallel",)))(o3, l3)


def _att_combine_bwd(da, o3, l3, tm=512):
    _, R, _ = o3.shape

    def body(da_ref, o_ref, l_ref, do_ref, dl_ref):
        da = da_ref[...]
        l0, l1, l2 = l_ref[0], l_ref[1], l_ref[2]
        m = jnp.maximum(jnp.maximum(l0, l1), l2)
        e0, e1, e2 = jnp.exp(l0 - m), jnp.exp(l1 - m), jnp.exp(l2 - m)
        inv = 1.0 / (e0 + e1 + e2)
        w = (e0 * inv, e1 * inv, e2 * inv)
        dw = (da * o_ref[0], da * o_ref[1], da * o_ref[2])
        mean = w[0] * dw[0] + w[1] * dw[1] + w[2] * dw[2]
        for g in range(3):
            do_ref[g] = w[g] * da
            dl_ref[g] = w[g] * (dw[g] - mean)

    spec3 = pl.BlockSpec((3, tm, LANES), lambda i: (0, i, 0))
    shp = jax.ShapeDtypeStruct((3, R, LANES), F32)
    return pl.pallas_call(body, name="att_combine_bwd", grid=(R // tm,), in_specs=[_rows(tm, LANES), spec3, spec3],
                          out_specs=[spec3, spec3], out_shape=[shp, shp],
                          compiler_params=_cparams(("parallel",)))(da, o3, l3)


@jax.custom_vjp
def _bdot(a, b):
    return jnp.dot(a.astype(BF16), b.astype(BF16), preferred_element_type=F32)


def _bdot_fwd(a, b):
    return _bdot(a, b), (a, b)


def _bdot_bwd(res, ct):
    a, b = res
    ct16 = ct.astype(BF16)
    da = lax.dot_general(ct16, b.astype(BF16), (((1,), (1,)), ((), ())), preferred_element_type=F32)
    db = lax.dot_general(a.astype(BF16), ct16, (((0,), (0,)), ((), ())), preferred_element_type=F32)
    return da, db


_bdot.defvjp(_bdot_fwd, _bdot_bwd)


def _head_sum(x):
    hi = lax.Precision.HIGHEST
    sel = (lax.broadcasted_iota(jnp.int32, (D, LANES), 0) // HEAD == lax.broadcasted_iota(jnp.int32, (D, LANES), 1)).astype(F32)
    sel_t = (lax.broadcasted_iota(jnp.int32, (LANES, D), 1) // HEAD == lax.broadcasted_iota(jnp.int32, (LANES, D), 0)).astype(F32)
    return jnp.dot(jnp.dot(x, sel, precision=hi, preferred_element_type=F32), sel_t, precision=hi, preferred_element_type=F32)


def _softplus(z):
    return jnp.maximum(z, 0.0) + jnp.log(1.0 + jnp.exp(-jnp.abs(z)))


def _rwkv_prep_fn(zr, zrp, zk, zkp, zv, zvp, zl, zlp, mu_r, mu_k, mu_v, mu_l, w0, a0, k_k, k_a, w2, a2, g2p):
    r = zr + (zrp - zr) * mu_r
    k = zk + (zkp - zk) * mu_k
    v = zv + (zvp - zv) * mu_v
    lo = zl + (zlp - zl) * mu_l
    w_low, a_low, g_low = lo[:, 0:LORA_W], lo[:, LORA_W:LORA_W + LORA_A], lo[:, LANES:LANES + G_PAD]
    w_log = -_softplus(-(w0 + _bdot(jnp.tanh(w_low), w2))) - 0.5
    decay = jnp.exp(-jnp.exp(w_log))
    a = jax.nn.sigmoid(a0 + _bdot(a_low, a2))
    g = _bdot(jax.nn.sigmoid(g_low), g2p)
    kmod = k * (1.0 + (a - 1.0) * k_a)
    kk = k * k_k
    kk = kk / jnp.maximum(jnp.sqrt(_head_sum(kk * kk)), 1e-12)
    return r, decay, kmod, v, -kk, kk * a, g


def _rwkv_prep_specs(tm):
    vec = _full((1, D))
    slabs = []
    for col in (C_R // D, C_K // D, C_V // D):
        slabs += [_rows(tm, D, col), _prev8(tm, D, col)]
    slabs += [_rows(tm, LORA_PAD, C_LORA // LORA_PAD), _prev8(tm, LORA_PAD, C_LORA // LORA_PAD)]
    params = [vec, vec, vec, _full((1, LORA_PAD)), vec, vec, vec, vec,
              _full((LORA_W, D)), _full((LORA_A, D)), _full((G_PAD, D))]
    return slabs, params


def _prep_inputs(refs, first):
    vals = []
    for s in range(4):
        z = refs[2 * s][...]
        vals += [z, _shift_down(z, refs[2 * s + 1][...], 1, first)]
    return vals + [r[...] for r in refs[8:19]]


def _rwkv_prep(P, params, tm=256):
    S = P.shape[0]
    slabs, pspecs = _rwkv_prep_specs(tm)

    def body(*refs):
        outs = _rwkv_prep_fn(*_prep_inputs(refs, pl.program_id(0) == 0))
        for o_ref, val in zip(refs[19:], outs):
            o_ref[...] = val

    shp = jax.ShapeDtypeStruct((S, D), F32)
    return pl.pallas_call(body, name="rwkv_prep", grid=(S // tm,), in_specs=slabs + pspecs,
                          out_specs=[_rows(tm, D)] * 7, out_shape=[shp] * 7,
                          compiler_params=_cparams(("parallel",)))(*([P] * 8), *params)


def _rwkv_prep_bwd(P, params, cts_a, cts_b, tm=128):
    S = P.shape[0]
    slabs, pspecs = _rwkv_prep_specs(tm)
    has_b = [c is not None for c in cts_b]
    n_ct = 7 + sum(has_b)

    def body(*refs):
        first = pl.program_id(0) == 0
        ins = _prep_inputs(refs, first)
        ct_refs = refs[19:19 + n_ct]
        out_refs = refs[19 + n_ct:]
        cts, pos = [], 7
        for i in range(7):
            c = ct_refs[i][...]
            if has_b[i]:
                c = c + ct_refs[pos][...]
                pos += 1
            cts.append(c)
        _, vjp = jax.vjp(_rwkv_prep_fn, *ins)
        grads = vjp(tuple(cts))
        for s in range(4):
            out_refs[s][...] = grads[2 * s]
            out_refs[4 + s][...] = grads[2 * s + 1]
        for i in range(11):
            _acc(out_refs[8 + i], grads[8 + i], first)

    ct_in = list(cts_a) + [c for c in cts_b if c is not None]
    row, lrow = _rows(tm, D), _rows(tm, LORA_PAD)
    f = jax.ShapeDtypeStruct
    zshapes = [f((S, D), F32)] * 3 + [f((S, LORA_PAD), F32)]
    pshapes = [f((1, D), F32)] * 3 + [f((1, LORA_PAD), F32)] + [f((1, D), F32)] * 4 + [f((LORA_W, D), F32), f((LORA_A, D), F32), f((G_PAD, D), F32)]
    return pl.pallas_call(
        body, name="rwkv_prep_bwd", grid=(S // tm,),
        in_specs=slabs + pspecs + [row] * n_ct,
        out_specs=[row, row, row, lrow] * 2 + pspecs,
        out_shape=zshapes * 2 + pshapes,
        compiler_params=_cparams(("arbitrary",)))(*([P] * 8), *params, *ct_in)


def _shift_add(a, b, tm=256):
    S, W = a.shape

    def body(a_ref, b_ref, h_ref, o_ref):
        last = pl.program_id(0) == pl.num_programs(0) - 1
        o_ref[...] = (a_ref[...] + _shift_up(b_ref[...], h_ref[...], 1, last)).astype(BF16)

    return pl.pallas_call(body, name="shift_add", grid=(S // tm,),
                          in_specs=[_rows(tm, W), _rows(tm, W), _next8(tm, W, S)],
                          out_specs=_rows(tm, W), out_shape=jax.ShapeDtypeStruct((S, W), BF16),
                          compiler_params=_cparams(("parallel",)))(a, b, b)


def _rwkv_post_fn(y, r, kmod, v, g, lnx_w, lnx_b, r_k):
    mean = _head_sum(y) * (1.0 / HEAD)
    yc = y - mean
    var = _head_sum(yc * yc) * (1.0 / HEAD)
    yn = yc * lax.rsqrt(var + GN_EPS) * lnx_w + lnx_b
    bonus = _head_sum(r * kmod * r_k) * v
    return (yn + bonus) * g


def _rwkv_post(y, r, kmod, v, g, lnx_w, lnx_b, r_k, tm=256):
    S = y.shape[0]

    def body(y_ref, r_ref, k_ref, v_ref, g_ref, w_ref, b_ref, rk_ref, o_ref):
        o_ref[...] = _rwkv_post_fn(y_ref[...], r_ref[...], k_ref[...], v_ref[...], g_ref[...],
                                   w_ref[...], b_ref[...], rk_ref[...]).astype(BF16)

    row, vec = _rows(tm, D), _full((1, D))
    return pl.pallas_call(body, name="rwkv_post", grid=(S // tm,), in_specs=[row] * 5 + [vec] * 3, out_specs=row,
                          out_shape=jax.ShapeDtypeStruct((S, D), BF16),
                          compiler_params=_cparams(("parallel",)))(y, r, kmod, v, g, lnx_w, lnx_b, r_k)


def _rwkv_post_bwd(drw, y, r, kmod, v, g, lnx_w, lnx_b, r_k, tm=256):
    S = y.shape[0]

    def body(d_ref, y_ref, r_ref, k_ref, v_ref, g_ref, w_ref, b_ref, rk_ref, *out_refs):
        first = pl.program_id(0) == 0
        _, vjp = jax.vjp(_rwkv_post_fn, y_ref[...], r_ref[...], k_ref[...], v_ref[...], g_ref[...],
                         w_ref[...], b_ref[...], rk_ref[...])
        grads = vjp(d_ref[...])
        for i in range(5):
            out_refs[i][...] = grads[i]
        for i in range(5, 8):
            _acc(out_refs[i], grads[i], first)

    row, vec = _rows(tm, D), _full((1, D))
    f = jax.ShapeDtypeStruct
    return pl.pallas_call(body, name="rwkv_post_bwd", grid=(S // tm,), in_specs=[row] * 6 + [vec] * 3,
                          out_specs=[row] * 5 + [vec] * 3, out_shape=[f((S, D), F32)] * 5 + [f((1, D), F32)] * 3,
                          compiler_params=_cparams(("arbitrary",)))(drw, y, r, kmod, v, g, lnx_w, lnx_b, r_k)


def _col(tile, ii, lane_lo):
    a = jnp.broadcast_to(tile[0:HEAD, ii:ii + 1], (HEAD, LANES))
    b = jnp.broadcast_to(tile[HEAD:2 * HEAD, ii:ii + 1], (HEAD, LANES))
    return jnp.where(lane_lo, a, b)


SCAN_NG = SCAN_TB // SUBLANES
N_PIECES = 2


def _scan_sources(lane_refs, mxu_refs, t_ref, p_ref):
    for o, ref in enumerate(lane_refs):
        t_ref[o] = ref[...].T
    for o, ref in enumerate(mxu_refs):
        rest = ref[...]
        for p in range(N_PIECES):
            piece = rest.astype(BF16).astype(F32)
            rest = rest - piece
            p_ref[o, p] = piece


def _gen_tiles(buf_ref, g, n_lane, n_mxu, t_ref, p_ref):
    lane_lo = lax.broadcasted_iota(jnp.int32, (HEAD, LANES), 1) < HEAD
    tiles = [pltpu.roll(t_ref[o], (LANES - SUBLANES * g) % LANES, 1) for o in range(n_lane)]
    for ii in range(SUBLANES):
        for o in range(n_lane):
            buf_ref[ii, o] = _col(tiles[o], ii, lane_lo)
    if n_mxu == 0:
        return
    diag = (lax.broadcasted_iota(jnp.int32, (HEAD, LANES), 1) % HEAD
            == lax.broadcasted_iota(jnp.int32, (HEAD, LANES), 0)).astype(BF16)
    ones = (lax.broadcasted_iota(jnp.int32, (LANES, LANES), 0) // HEAD
            == lax.broadcasted_iota(jnp.int32, (LANES, LANES), 1) // HEAD).astype(BF16)
    start = pl.multiple_of(g * SUBLANES, SUBLANES)
    for o in range(n_mxu):
        cols = None
        for p in range(N_PIECES):
            rows = p_ref[o, p, pl.ds(start, SUBLANES), :].astype(BF16)
            lhs = jnp.concatenate([jnp.broadcast_to(rows[ii:ii + 1], (HEAD, LANES)) * diag for ii in range(SUBLANES)], axis=0)
            part = jnp.dot(lhs, ones, preferred_element_type=F32)
            cols = part if cols is None else cols + part
        for ii in range(SUBLANES):
            buf_ref[ii, n_lane + o] = cols[ii * HEAD:(ii + 1) * HEAD]


def _scan_scratch(n_lane, n_mxu):
    tiles = pltpu.VMEM((SUBLANES, n_lane + n_mxu, HEAD, LANES), F32)
    return [pltpu.VMEM((HEAD, LANES), F32), pltpu.VMEM((max(n_lane, 1), LANES, SCAN_TB), F32),
            pltpu.VMEM((max(n_mxu, 1), N_PIECES, SCAN_TB, LANES), F32), tiles, tiles]


def _scan_fwd(r, w, k, v, a, b):
    S = r.shape[0]
    nblk = S // SCAN_TB
    npair = N_HEADS // 2

    def body(r_ref, w_ref, k_ref, v_ref, a_ref, b_ref, y_ref, sall_ref, s_ref, t_ref, p_ref, buf0, buf1):
        @pl.when(pl.program_id(1) == 0)
        def _():
            s_ref[...] = jnp.zeros_like(s_ref)

        _scan_sources((w_ref,), (r_ref, k_ref, a_ref, b_ref), t_ref, p_ref)
        gen = functools.partial(_gen_tiles, n_lane=1, n_mxu=4, t_ref=t_ref, p_ref=p_ref)

        def steps(buf, g, st):
            for ii in range(SUBLANES):
                t = g * SUBLANES + ii
                wc, rc, kc, ac, bc = [buf[ii, o] for o in range(5)]
                sall_ref[t] = st
                sa = jnp.sum(st * ac, axis=0, keepdims=True)
                st = st * wc + bc * sa + kc * v_ref[pl.ds(t, 1), :]
                y_ref[pl.ds(t, 1), :] = jnp.sum(st * rc, axis=0, keepdims=True)
            return st

        gen(buf0, 0)

        def two_groups(i, st):
            g = 2 * i
            gen(buf1, g + 1)
            st = steps(buf0, g, st)
            gen(buf0, jnp.minimum(g + 2, SCAN_NG - 1))
            return steps(buf1, g + 1, st)

        s_ref[...] = lax.fori_loop(0, SCAN_NG // 2, two_groups, s_ref[...])

    blk = pl.BlockSpec((SCAN_TB, LANES), lambda p, i: (i, p))
    return pl.pallas_call(
        body, name="scan_fwd", grid=(npair, nblk), in_specs=[blk] * 6,
        out_specs=[blk, pl.BlockSpec((SCAN_TB, None, HEAD, LANES), lambda p, i: (i, p, 0, 0))],
        out_shape=[jax.ShapeDtypeStruct((S, D), F32), jax.ShapeDtypeStruct((S, npair, HEAD, LANES), F32)],
        scratch_shapes=_scan_scratch(1, 4),
        compiler_params=_cparams(("parallel", "arbitrary")))(r, w, k, v, a, b)


def _scan_bwd(r, w, k, v, a, b, sall, dy):
    S = r.shape[0]
    nblk = S // SCAN_TB
    npair = N_HEADS // 2
    NG = SCAN_TB // SUBLANES
    nt = (((1,), (1,)), ((), ()))

    def body(r_ref, w_ref, k_ref, v_ref, a_ref, b_ref, sall_ref, dy_ref,
             dr_ref, dw_ref, dk_ref, dv_ref, da_ref, db_ref, ds_ref, t_ref, p_ref, buf0, buf1):
        @pl.when(pl.program_id(1) == 0)
        def _():
            ds_ref[...] = jnp.zeros_like(ds_ref)

        _scan_sources((w_ref, r_ref, k_ref), (a_ref, b_ref), t_ref, p_ref)
        gen = functools.partial(_gen_tiles, n_lane=3, n_mxu=2, t_ref=t_ref, p_ref=p_ref)
        half_sel = (lax.broadcasted_iota(jnp.int32, (SUBLANES, LANES), 0)
                    == lax.broadcasted_iota(jnp.int32, (SUBLANES, LANES), 1) // HEAD).astype(BF16)

        def key_grad(ref, t, z):
            res = lax.dot_general(half_sel, z.astype(BF16), nt, preferred_element_type=F32)
            ref[pl.ds(t, 1), 0:HEAD] = res[0:1]
            ref[pl.ds(t, 1), HEAD:2 * HEAD] = res[1:2]

        def steps(buf, g, dst):
            for ii in reversed(range(SUBLANES)):
                t = g * SUBLANES + ii
                wc, rc, kc, ac, bc = [buf[ii, o] for o in range(5)]
                vrow = v_ref[pl.ds(t, 1), :]
                dyrow = dy_ref[pl.ds(t, 1), :]
                sp = sall_ref[t]
                sa = jnp.sum(sp * ac, axis=0, keepdims=True)
                sn = sp * wc + bc * sa + kc * vrow
                dsn = dst + rc * dyrow
                dv_ref[pl.ds(t, 1), :] = jnp.sum(dsn * kc, axis=0, keepdims=True)
                dsa = jnp.sum(dsn * bc, axis=0, keepdims=True)
                key_grad(dr_ref, t, sn * dyrow)
                key_grad(dw_ref, t, dsn * sp)
                key_grad(dk_ref, t, dsn * vrow)
                key_grad(da_ref, t, sp * dsa)
                key_grad(db_ref, t, dsn * sa)
                dst = dsn * wc + ac * dsa
            return dst

        gen(buf0, SCAN_NG - 1)

        def two_groups(i, dst):
            g = SCAN_NG - 1 - 2 * i
            gen(buf1, g - 1)
            dst = steps(buf0, g, dst)
            gen(buf0, jnp.maximum(g - 2, 0))
            return steps(buf1, g - 1, dst)

        ds_ref[...] = lax.fori_loop(0, SCAN_NG // 2, two_groups, ds_ref[...])

    blk = pl.BlockSpec((SCAN_TB, LANES), lambda p, i: (nblk - 1 - i, p))
    shp = jax.ShapeDtypeStruct((S, D), F32)
    return pl.pallas_call(
        body, name="scan_bwd", grid=(npair, nblk),
        in_specs=[blk] * 6 + [pl.BlockSpec((SCAN_TB, None, HEAD, LANES), lambda p, i: (nblk - 1 - i, p, 0, 0)), blk],
        out_specs=[blk] * 6, out_shape=[shp] * 6,
        scratch_shapes=_scan_scratch(3, 2),
        compiler_params=_cparams(("parallel", "arbitrary")))(r, w, k, v, a, b, sall, dy)


def _ada_fwd(c8, w_ada, b_ada):
    def body(c_ref, w_ref, b_ref, o_ref):
        o_ref[...] = jnp.dot(c_ref[...].astype(BF16), w_ref[...], preferred_element_type=F32) + b_ref[...]

    tn = 1536
    return pl.pallas_call(body, name="ada_fwd", grid=(6 * D // tn,),
                          in_specs=[_full((SUBLANES, D)), pl.BlockSpec((D, tn), lambda j: (0, j)), pl.BlockSpec((1, tn), lambda j: (0, j))],
                          out_specs=pl.BlockSpec((SUBLANES, tn), lambda j: (0, j)),
                          out_shape=jax.ShapeDtypeStruct((SUBLANES, 6 * D), F32),
                          compiler_params=_cparams(("parallel",)))(c8, w_ada, b_ada)


def _outer(col, row):
    N = row.shape[1]
    tn = 1536

    def body(c_ref, r_ref, o_ref):
        o_ref[...] = c_ref[...] * r_ref[...]

    return pl.pallas_call(body, name="ada_wgrad", grid=(N // tn,),
                          in_specs=[_full((D, 1)), pl.BlockSpec((1, tn), lambda j: (0, j))],
                          out_specs=pl.BlockSpec((D, tn), lambda j: (0, j)),
                          out_shape=jax.ShapeDtypeStruct((D, N), F32),
                          compiler_params=_cparams(("parallel",)))(col, row)


def _exchange(src, broadcast, name):
    rows = src.shape[-2]
    out_shape = jax.ShapeDtypeStruct((N_DEV, rows, LANES), src.dtype)

    def body(src_ref, out_ref, send_sems, recv_sems, local_sem):
        x, y, c = lax.axis_index("x"), lax.axis_index("y"), lax.axis_index("c")
        me = 4 * x + 2 * y + c

        def block(j):
            return src_ref if broadcast else src_ref.at[j]

        local = pltpu.make_async_copy(block(me), out_ref.at[me], local_sem)
        local.start()
        copies = []
        for d in range(1, N_DEV):
            px, py, pc = x ^ (d >> 2), y ^ ((d >> 1) & 1), c ^ (d & 1)
            peer = 4 * px + 2 * py + pc
            copies.append(pltpu.make_async_remote_copy(
                src_ref=block(peer), dst_ref=out_ref.at[me], send_sem=send_sems.at[d], recv_sem=recv_sems.at[d],
                device_id=(px, py, pc), device_id_type=_MESH))
        for cp in copies:
            cp.start()
        for d, cp in zip(range(1, N_DEV), copies):
            px, py, pc = x ^ (d >> 2), y ^ ((d >> 1) & 1), c ^ (d & 1)
            peer = 4 * px + 2 * py + pc
            pltpu.make_async_remote_copy(
                src_ref=block(me), dst_ref=out_ref.at[peer], send_sem=send_sems.at[d], recv_sem=recv_sems.at[d],
                device_id=(px, py, pc), device_id_type=_MESH).wait_recv()
        for cp in copies:
            cp.wait_send()
        local.wait()

    return pl.pallas_call(
        body, name=name, out_shape=out_shape,
        in_specs=[pl.BlockSpec(memory_space=pl.ANY)], out_specs=pl.BlockSpec(memory_space=pl.ANY),
        scratch_shapes=[pltpu.SemaphoreType.DMA((N_DEV,)), pltpu.SemaphoreType.DMA((N_DEV,)), pltpu.SemaphoreType.DMA],
        compiler_params=pltpu.CompilerParams(has_side_effects=True),
    )(src)


def _sum_adam(parts, w, m, v, name, tm=512):
    _, R, _ = parts.shape
    tm = _tile_rows(R, tm)
    c1 = 1.0 / (1.0 - ADAM_B1 ** ADAM_STEP)
    c2 = 1.0 / (1.0 - ADAM_B2 ** ADAM_STEP)

    def body(p_ref, w_ref, m_ref, v_ref, g_ref, d_ref, nm_ref, nv_ref):
        g = p_ref[0].astype(F32)
        for j in range(1, N_DEV):
            g = g + p_ref[j].astype(F32)
        nm = ADAM_B1 * m_ref[...] + (1.0 - ADAM_B1) * g
        nv = ADAM_B2 * v_ref[...] + (1.0 - ADAM_B2) * (g * g)
        g_ref[...] = g
        nm_ref[...] = nm
        nv_ref[...] = nv
        d_ref[...] = -ADAM_LR * ((nm * c1) / (jnp.sqrt(nv * c2) + ADAM_EPS) + ADAM_WD * w_ref[...])

    row = _rows(tm, LANES)
    shp = jax.ShapeDtypeStruct((R, LANES), F32)
    return pl.pallas_call(body, name=name, grid=(R // tm,),
                          in_specs=[pl.BlockSpec((N_DEV, tm, LANES), lambda i: (0, i, 0)), row, row, row],
                          out_specs=[row] * 4, out_shape=[shp] * 4,
                          compiler_params=_cparams(("parallel",)))(parts, w, m, v)


def _tile_rows(R, pref):
    best = 16
    for t in range(16, pref + 1, 16):
        if R % t == 0:
            best = t
    return best


PACK_ALIGN = 16 * LANES
PACK_ROWS = 512 * LANES

SHARDED = (("w_ada", 1), ("w_in", 1), ("w2", 1), ("a2", 1), ("g2", 1), ("w_att_out", 1), ("w_rwkv_out", 0),
           ("w_o", 0), ("w_up", 1), ("conv_w", 1), ("w_down", 0))
REPLICATED = ("b_ada", "norm1_w", "b_gate", "mu_shift", "w0", "a0", "k_k", "k_a", "r_k", "lnx_w", "lnx_b",
              "norm2_w", "conv_b", "norm_f_w")
WEIGHTS = ("w_ada", "b_ada", "norm1_w", "w_in", "b_gate", "mu_shift", "w0", "w2", "a0", "a2", "g2", "k_k", "k_a", "r_k",
           "lnx_w", "lnx_b", "w_att_out", "w_rwkv_out", "w_o", "norm2_w", "w_up", "conv_w", "conv_b", "w_down", "norm_f_w")


def _pack(arrays):
    flat, layout, off = [], [], 0
    for i, a in enumerate(arrays):
        n = a.size
        pad = (-n) % PACK_ALIGN if i + 1 < len(arrays) else (-(off + n)) % PACK_ROWS
        flat.append(a.reshape(-1))
        if pad:
            flat.append(jnp.zeros((pad,), a.dtype))
        layout.append((off, n, a.shape))
        off += n + pad
    return jnp.concatenate(flat).reshape(-1, LANES), layout


def _unpack(buf, layout):
    flat = buf.reshape(-1)
    return [flat[off:off + n].reshape(shape) for off, n, shape in layout]


def _regroup(t, d):
    S = t.shape[0]
    return t.reshape((S // d, d) + t.shape[1:]).swapaxes(0, 1).reshape(t.shape)


def _ungroup(t, d):
    S = t.shape[0]
    return t.reshape((d, S // d) + t.shape[1:]).swapaxes(0, 1).reshape(t.shape)


def _att_in(P):
    S = P.shape[0]
    z = P[:, C_ATT:C_ATT + ATT_IN].astype(BF16).reshape(S, 3, 3, ATT_HEADS, HEAD)
    per_group = [_regroup(z[:, g], d) for g, (_, d) in enumerate(ATT_PATTERNS)]
    z = jnp.stack(per_group, axis=0)
    z = z.transpose(2, 0, 3, 1, 4).reshape(3, 3 * ATT_HEADS, S, HEAD)
    return z[0], z[1], z[2]


def _att_natural(t):
    S = t.shape[1]
    t = t.reshape(3, ATT_HEADS, S, HEAD)
    return jnp.stack([_ungroup(t[g].swapaxes(0, 1), d).swapaxes(0, 1) for g, (_, d) in enumerate(ATT_PATTERNS)], axis=0)


def _att_regrouped(t):
    S = t.shape[2]
    out = jnp.stack([_regroup(t[g].swapaxes(0, 1), d).swapaxes(0, 1) for g, (_, d) in enumerate(ATT_PATTERNS)], axis=0)
    return out.reshape(3 * ATT_HEADS, S, HEAD)


def _att_grad_cols(dq, dk, dv):
    S = dq.shape[1]
    z = jnp.stack([dq, dk, dv], axis=0).astype(BF16).reshape(3, 3, ATT_HEADS, S, HEAD)
    per_group = [_ungroup(z[:, g].transpose(2, 0, 1, 3), d) for g, (_, d) in enumerate(ATT_PATTERNS)]
    return jnp.stack(per_group, axis=1).reshape(S, ATT_IN)


def _pad_w_in(w_in):
    rkv = w_in[:, ATT_IN:ATT_IN + 3 * D]
    lora = w_in[:, ATT_IN + 3 * D:ATT_IN + RWKV_IN]
    gates = w_in[:, ATT_IN + RWKV_IN:]
    att = w_in[:, :ATT_IN]
    lw, la, lg = lora[:, :LORA_W], lora[:, LORA_W:LORA_W + LORA_A], lora[:, LORA_W + LORA_A:]
    zeros = jnp.zeros((w_in.shape[0], LORA_PAD - LANES - LORA_G), w_in.dtype)
    return jnp.concatenate([rkv, gates, att, lw, la, lg, zeros], axis=1)


def _unpad_w_in(g):
    att = g[:, C_ATT:C_ATT + ATT_IN]
    rkv = g[:, C_R:C_R + 3 * D]
    lora = jnp.concatenate([g[:, C_LORA:C_LORA + LORA_W + LORA_A], g[:, C_LORA + LANES:C_LORA + LANES + LORA_G]], axis=1)
    gates = g[:, C_GA:C_GA + 2 * D]
    return jnp.concatenate([att, rkv, lora, gates], axis=1)


def _pad_mu(mu):
    lo = mu[:, 3 * D:]
    mu_l = jnp.concatenate([lo[:, :LORA_W + LORA_A], lo[:, LORA_W + LORA_A:], jnp.zeros((1, LORA_PAD - LANES - LORA_G), mu.dtype)], axis=1)
    return mu[:, :D], mu[:, D:2 * D], mu[:, 2 * D:3 * D], mu_l


def _local_step(x, c, W, target):
    S = x.shape[0]
    G = {}
    c8 = jnp.pad(c, ((0, SUBLANES - 1), (0, 0)))
    ada = _ada_fwd(c8, W["w_ada"], W["b_ada"])[0:1]
    sh1, sc1, gt1, sh2, sc2, gt2 = [ada[:, i * D:(i + 1) * D] for i in range(6)]
    h1, rstd1 = _norm_fwd(x, None, None, W["norm1_w"], sc1, sh1, "norm1_fwd")
    w_in_p = _pad_w_in(W["w_in"])
    P = _mm(h1, w_in_p, "nn", F32, "proj_in")

    q, k, v = _att_in(P)
    o_g, l_g = _att_fwd(q, k, v)
    o_nat = _att_natural(o_g).reshape(3, -1, LANES)
    l_nat = _att_natural(l_g).reshape(3, -1, LANES)
    att_hm = _att_combine_fwd(o_nat, l_nat)
    att = att_hm.reshape(ATT_HEADS, S, HEAD).swapaxes(0, 1).reshape(S, ATT_W).astype(BF16)
    y_att = _mm(att, W["w_att_out"], "nn", F32, "att_out")

    mu_r, mu_k, mu_v, mu_l = _pad_mu(W["mu_shift"])
    g2p = jnp.pad(W["g2"], ((0, G_PAD - LORA_G), (0, 0)))
    prep_params = [mu_r, mu_k, mu_v, mu_l, W["w0"], W["a0"], W["k_k"], W["k_a"], W["w2"], W["a2"], g2p]
    r_, dec, kmod, v_, aa, bb, gg = _rwkv_prep(P, prep_params)
    y_scan, states = _scan_fwd(r_, dec, kmod, v_, aa, bb)
    r_k = W["r_k"].reshape(1, D)
    rw = _rwkv_post(y_scan, r_, kmod, v_, gg, W["lnx_w"], W["lnx_b"], r_k)
    y_rwkv = _mm(rw, W["w_rwkv_out"], "nn", F32, "rwkv_out")

    bga, bgr = W["b_gate"][:, :D], W["b_gate"][:, D:]
    mix = _gate_fwd(P, bga, bgr, y_att, y_rwkv)
    mo = _mm(mix, W["w_o"], "nn", F32, "mix_out")
    x2, h2, rstd2 = _norm_fwd(x, mo, gt1, W["norm2_w"], sc2, sh2, "norm2_fwd")
    u = _mm(h2, W["w_up"], "nn", F32, "ffn_up")
    conv_w8 = jnp.pad(W["conv_w"], ((0, SUBLANES - 3), (0, 0)))
    act = _conv_fwd(u, conv_w8, W["conv_b"])
    f = _mm(act, W["w_down"], "nn", F32, "ffn_down")
    loss_blk, dx3, df, dgt2, G["norm_f_w"] = _final(x2, f, gt2, W["norm_f_w"], target)
    loss = loss_blk[0, 0]

    dact = _mm(df, W["w_down"], "nt", BF16, "ffn_down_dx")
    G["w_down"] = _mm(act, df, "tn", F32, "ffn_down_dw")
    dug, duv, dwg, dwv, dbg, dbv = _conv_bwd_a(dact, u, conv_w8, W["conv_b"])
    G["conv_w"] = jnp.concatenate([dwg[0:3], dwv[0:3]], axis=1)
    G["conv_b"] = jnp.concatenate([dbg, dbv], axis=1)
    du = jnp.concatenate([_conv_bwd_b(dug, conv_w8[:, :D_FF]), _conv_bwd_b(duv, conv_w8[:, D_FF:])], axis=1)
    dh2 = _mm(du, W["w_up"], "nt", F32, "ffn_up_dx")
    G["w_up"] = _mm(h2, du, "tn", F32, "ffn_up_dw")
    dx2, dsh2, dsc2, G["norm2_w"], dmo, dgt1 = _norm_bwd(dh2, x2, rstd2, W["norm2_w"], sc2, dx3, mo, gt1, "norm2_bwd")
    dmix = _mm(dmo, W["w_o"], "nt", F32, "mix_out_dx")
    G["w_o"] = _mm(mix, dmo, "tn", F32, "mix_out_dw")
    dy_att, dy_rwkv, dpga, dpgr, dbga, dbgr = _gate_bwd(dmix, P, bga, bgr, y_att, y_rwkv)
    G["b_gate"] = jnp.concatenate([dbga, dbgr], axis=1)

    datt = _mm(dy_att, W["w_att_out"], "nt", F32, "att_out_dx")
    G["w_att_out"] = _mm(att, dy_att, "tn", F32, "att_out_dw")
    datt_hm = datt.reshape(S, ATT_HEADS, HEAD).swapaxes(0, 1).reshape(-1, LANES)
    do_nat, dl_nat = _att_combine_bwd(datt_hm, o_nat, l_nat)
    do_g = _att_regrouped(do_nat.reshape(3, ATT_HEADS, S, HEAD))
    dl_g = _att_regrouped(dl_nat.reshape(3, ATT_HEADS, S, HEAD))
    dq, dk, dv = _att_bwd(q, k, v, o_g, l_g, do_g, dl_g)
    dp_att = _att_grad_cols(dq, dk, dv)

    drw = _mm(dy_rwkv, W["w_rwkv_out"], "nt", F32, "rwkv_out_dx")
    G["w_rwkv_out"] = _mm(rw, dy_rwkv, "tn", F32, "rwkv_out_dw")
    dy_scan, dr1, dk1, dv1, dgg, G["lnx_w"], G["lnx_b"], drk = _rwkv_post_bwd(drw, y_scan, r_, kmod, v_, gg, W["lnx_w"], W["lnx_b"], r_k)
    G["r_k"] = drk.reshape(W["r_k"].shape)
    dr2, ddec, dk2, dv2, daa, dbb = _scan_bwd(r_, dec, kmod, v_, aa, bb, states, dy_scan)
    pb = _rwkv_prep_bwd(P, prep_params, [dr2, ddec, dk2, dv2, daa, dbb, dgg], [dr1, None, dk1, dv1, None, None, None])
    dz, dzp, dpar = pb[0:4], pb[4:8], pb[8:]
    dp_rkv = [_shift_add(dz[i], dzp[i]) for i in range(3)]
    dp_lora = _shift_add(dz[3], dzp[3])
    dmu_r, dmu_k, dmu_v, dmu_l, G["w0"], G["a0"], G["k_k"], G["k_a"], G["w2"], G["a2"], dg2p = dpar
    G["g2"] = dg2p[0:LORA_G]
    G["mu_shift"] = jnp.concatenate([dmu_r, dmu_k, dmu_v, dmu_l[:, :LORA_W + LORA_A], dmu_l[:, LANES:LANES + LORA_G]], axis=1)

    dP = jnp.concatenate(dp_rkv + [dpga, dpgr, dp_att, dp_lora], axis=1)
    dh1 = _mm(dP, w_in_p, "nt", F32, "proj_in_dx")
    G["w_in"] = _unpad_w_in(_mm(h1, dP, "tn", F32, "proj_in_dw"))
    grad_x, dsh1, dsc1, G["norm1_w"] = _norm_bwd(dh1, x, rstd1, W["norm1_w"], sc1, dx2, None, None, "norm1_bwd")
    dada = jnp.concatenate([dsh1, dsc1, dgt1, dsh2, dsc2, dgt2], axis=1)
    G["b_ada"] = dada
    G["w_ada"] = _outer(c.reshape(D, 1), dada)
    return loss, grad_x, G


def _gather_full(bufs, layout):
    flat = bufs.reshape(N_DEV, -1)
    full = {}
    for (name, axis), (off, n, shape) in zip(SHARDED, layout):
        part = flat[:, off:off + n].reshape((N_DEV,) + shape)
        if axis == 1:
            part = part.transpose(1, 0, 2)
        full[name] = part.reshape((N_DEV * shape[0], shape[1]) if axis == 0 else (shape[0], N_DEV * shape[1]))
    return full


def _owner_blocks(G, layout):
    rows, end = [], 0
    for (name, axis), (off, n, shape) in zip(SHARDED, layout):
        g = G[name].astype(BF16)
        g = g.reshape((N_DEV,) + shape) if axis == 0 else g.reshape(shape[0], N_DEV, shape[1]).transpose(1, 0, 2)
        rows += [jnp.zeros((N_DEV, off - end), BF16), g.reshape(N_DEV, n)]
        end = off + n
    total = -(-end // PACK_ROWS) * PACK_ROWS
    rows.append(jnp.zeros((N_DEV, total - end), BF16))
    return jnp.concatenate([r for r in rows if r.shape[1]], axis=1).reshape(N_DEV, -1, LANES)


def kernel(x, c, w_ada, b_ada, norm1_w, w_in, b_gate, mu_shift, w0, w2, a0, a2, g2, k_k, k_a, r_k, lnx_w, lnx_b, w_att_out, w_rwkv_out, w_o, norm2_w, w_up, conv_w, conv_b, w_down, norm_f_w, loss_target, m_w_ada, m_b_ada, m_norm1_w, m_w_in, m_b_gate, m_mu_shift, m_w0, m_w2, m_a0, m_a2, m_g2, m_k_k, m_k_a, m_r_k, m_lnx_w, m_lnx_b, m_w_att_out, m_w_rwkv_out, m_w_o, m_norm2_w, m_w_up, m_conv_w, m_conv_b, m_w_down, m_norm_f_w, v_w_ada, v_b_ada, v_norm1_w, v_w_in, v_b_gate, v_mu_shift, v_w0, v_w2, v_a0, v_a2, v_g2, v_k_k, v_k_a, v_r_k, v_lnx_w, v_lnx_b, v_w_att_out, v_w_rwkv_out, v_w_o, v_norm2_w, v_w_up, v_conv_w, v_conv_b, v_w_down, v_norm_f_w):
    env = dict(locals())
    w_shard = {n: env[n] for n in WEIGHTS}
    m_shard = {n: env["m_" + n] for n in WEIGHTS}
    v_shard = {n: env["v_" + n] for n in WEIGHTS}

    mine, layout = _pack([w_shard[n][0].astype(BF16) for n, _ in SHARDED])
    gathered = _exchange(mine, True, "gather_weights")
    W = _gather_full(gathered, layout)
    for n in REPLICATED:
        W[n] = w_shard[n].reshape(1, -1) if n != "r_k" else w_shard[n][0]

    loss, grad_x, G = _local_step(x[0], c, W, loss_target[0])
    loss = lax.psum(loss, ("x", "y", "c"))

    glayout = layout
    parts = _exchange(_owner_blocks(G, layout), False, "scatter_grads")
    pw, _ = _pack([w_shard[n][0] for n, _ in SHARDED])
    pm, _ = _pack([m_shard[n][0] for n, _ in SHARDED])
    pv, _ = _pack([v_shard[n][0] for n, _ in SHARDED])
    res = _sum_adam(parts, pw, pm, pv, "adam_sharded")
    out = {}
    for kind, buf in zip(("grad", "delta", "new_m", "new_v"), res):
        for (n, _), a in zip(SHARDED, _unpack(buf, glayout)):
            out[kind, n] = a.reshape(w_shard[n].shape)

    small, slayout = _pack([G[n].reshape(-1) for n in REPLICATED])
    sparts = _exchange(small, True, "gather_small_grads")
    sw, _ = _pack([w_shard[n].reshape(-1) for n in REPLICATED])
    sm, _ = _pack([m_shard[n].reshape(-1) for n in REPLICATED])
    sv, _ = _pack([v_shard[n].reshape(-1) for n in REPLICATED])
    res = _sum_adam(sparts, sw, sm, sv, "adam_replicated")
    for kind, buf in zip(("grad", "delta", "new_m", "new_v"), res):
        for n, a in zip(REPLICATED, _unpack(buf, slayout)):
            out[kind, n] = a.reshape(w_shard[n].shape)

    return (loss, grad_x[None], *[out[kind, n] for kind in ("grad", "delta", "new_m", "new_v") for n in WEIGHTS])
```

```python
import functools
import math

import jax
import jax.numpy as jnp
from jax import lax
from jax.experimental import pallas as pl
from jax.experimental.pallas import tpu as pltpu

F32 = jnp.float32
BF16 = jnp.bfloat16

D = 1024
HEAD = 64
ATT_PATTERNS = ((128, 1), (512, 4), (2048, 16))
ATT_HEADS = 8
ATT_W = ATT_HEADS * HEAD
ATT_IN = 3 * 3 * ATT_W
QBLK = 128
N_HEADS = D // HEAD
LORA_W, LORA_A, LORA_G = 64, 64, 160
RWKV_IN = 3 * D + LORA_W + LORA_A + LORA_G
N_IN = ATT_IN + RWKV_IN + 2 * D
D_FF = 2816
RMS_EPS = 1e-6
GN_EPS = 64e-5
N_DEV = 8
LANES = 128
SUBLANES = 8

C_R, C_K, C_V, C_GA, C_GR = 0, 1024, 2048, 3072, 4096
C_ATT = 5120
C_LORA = C_ATT + ATT_IN
LORA_PAD = 512
G_PAD = 256
N_PAD = C_LORA + LORA_PAD

ADAM_LR, ADAM_B1, ADAM_B2, ADAM_EPS, ADAM_WD, ADAM_STEP = 0.001, 0.9, 0.999, 1e-08, 0.01, 10

SCAN_TB = 128
VMEM_LIMIT = 56 * 1024 * 1024

_MESH = pl.DeviceIdType.MESH


def _cparams(sem):
    return pltpu.CompilerParams(dimension_semantics=sem, vmem_limit_bytes=VMEM_LIMIT)


def _tile(dim, pref):
    if dim <= pref:
        return dim
    best = None
    for t in range(LANES, pref + 1, LANES):
        if dim % t == 0:
            best = t
    assert best is not None, dim
    return best


def _mm(a, b, mode, out_dtype, name, tm=512, tn=1024, tk=512):
    if mode == "nn":
        (M, K), (K2, N) = a.shape, b.shape
    elif mode == "nt":
        (M, K), (N, K2) = a.shape, b.shape
    else:
        (K, M), (K2, N) = a.shape, b.shape
    assert K == K2, (a.shape, b.shape, mode)
    tm, tn, tk = _tile(M, tm), _tile(N, tn), _tile(K, tk)
    nk = K // tk
    dims = {"nn": (((1,), (0,)), ((), ())), "nt": (((1,), (1,)), ((), ())), "tn": (((0,), (0,)), ((), ()))}[mode]

    def body(a_ref, b_ref, o_ref, acc_ref):
        k = pl.program_id(2)
        part = lax.dot_general(a_ref[...].astype(BF16), b_ref[...].astype(BF16), dims,
                               preferred_element_type=F32)

        @pl.when(k == 0)
        def _():
            acc_ref[...] = part

        @pl.when(k > 0)
        def _():
            acc_ref[...] += part

        @pl.when(k == nk - 1)
        def _():
            o_ref[...] = acc_ref[...].astype(o_ref.dtype)

    a_spec = pl.BlockSpec((tk, tm), lambda i, j, k: (k, i)) if mode == "tn" else pl.BlockSpec((tm, tk), lambda i, j, k: (i, k))
    b_spec = pl.BlockSpec((tn, tk), lambda i, j, k: (j, k)) if mode == "nt" else pl.BlockSpec((tk, tn), lambda i, j, k: (k, j))
    return pl.pallas_call(
        body, name=name, grid=(M // tm, N // tn, nk),
        in_specs=[a_spec, b_spec],
        out_specs=pl.BlockSpec((tm, tn), lambda i, j, k: (i, j)),
        out_shape=jax.ShapeDtypeStruct((M, N), out_dtype),
        scratch_shapes=[pltpu.VMEM((tm, tn), F32)],
        compiler_params=_cparams(("parallel", "parallel", "arbitrary")),
    )(a, b)


def _rows(tm, w, col=0):
    return pl.BlockSpec((tm, w), lambda i: (i, col))


def _full(shape):
    return pl.BlockSpec(shape, lambda i: (0,) * len(shape))


def _prev8(tm, w, col=0):
    return pl.BlockSpec((SUBLANES, w), lambda i: (jnp.maximum(i * (tm // SUBLANES) - 1, 0), col))


def _next8(tm, w, n_rows, col=0):
    last = n_rows // SUBLANES - 1
    return pl.BlockSpec((SUBLANES, w), lambda i: (jnp.minimum((i + 1) * (tm // SUBLANES), last), col))


def _shift_down(x, halo, k, first):
    rolled = pltpu.roll(x, k, 0)
    row = lax.broadcasted_iota(jnp.int32, x.shape, 0)
    out = rolled
    for j in range(k):
        h = jnp.where(first, 0.0, halo[SUBLANES - k + j:SUBLANES - k + j + 1, :])
        out = jnp.where(row == j, h, out)
    return out


def _shift_up(x, halo, k, last):
    n = x.shape[0]
    rolled = pltpu.roll(x, n - k, 0)
    row = lax.broadcasted_iota(jnp.int32, x.shape, 0)
    out = rolled
    for j in range(k):
        h = jnp.where(last, 0.0, halo[j:j + 1, :])
        out = jnp.where(row == n - k + j, h, out)
    return out


def _acc(ref, val, first):
    @pl.when(first)
    def _():
        ref[...] = val

    @pl.when(jnp.logical_not(first))
    def _():
        ref[...] += val


def _colsum(x):
    return jnp.sum(x, axis=0, keepdims=True)


def _norm_fwd(x, mo, gt, nw, sc, sh, name, tm=256):
    S = x.shape[0]
    has_res = mo is not None

    def body(*refs):
        if has_res:
            x_ref, mo_ref, gt_ref, nw_ref, sc_ref, sh_ref, x2_ref, h_ref, rs_ref = refs
            x2 = x_ref[...] + gt_ref[...] * mo_ref[...]
            x2_ref[...] = x2
        else:
            x_ref, nw_ref, sc_ref, sh_ref, h_ref, rs_ref = refs
            x2 = x_ref[...]
        rstd = lax.rsqrt(jnp.mean(x2 * x2, axis=-1, keepdims=True) + RMS_EPS)
        rs_ref[...] = rstd
        h_ref[...] = ((x2 * rstd * nw_ref[...]) * (1.0 + sc_ref[...]) + sh_ref[...]).astype(BF16)

    vec = _full((1, D))
    ins = [x, mo, gt, nw, sc, sh] if has_res else [x, nw, sc, sh]
    in_specs = [_rows(tm, D), _rows(tm, D), vec, vec, vec, vec] if has_res else [_rows(tm, D), vec, vec, vec]
    outs = [jax.ShapeDtypeStruct((S, D), BF16), jax.ShapeDtypeStruct((S, 1), F32)]
    out_specs = [_rows(tm, D), _rows(tm, 1)]
    if has_res:
        outs = [jax.ShapeDtypeStruct((S, D), F32)] + outs
        out_specs = [_rows(tm, D)] + out_specs
    return pl.pallas_call(body, name=name, grid=(S // tm,), in_specs=in_specs, out_specs=out_specs,
                          out_shape=outs, compiler_params=_cparams(("parallel",)))(*ins)


def _norm_bwd(dh, xin, rstd, nw, sc, dres, mo, gt, name, tm=256):
    S = xin.shape[0]
    has_res = mo is not None

    def body(*refs):
        if has_res:
            dh_ref, x_ref, rs_ref, nw_ref, sc_ref, dres_ref, mo_ref, gt_ref, dx_ref, dsh_ref, dsc_ref, dnw_ref, dmo_ref, dgt_ref = refs
        else:
            dh_ref, x_ref, rs_ref, nw_ref, sc_ref, dres_ref, dx_ref, dsh_ref, dsc_ref, dnw_ref = refs
        first = pl.program_id(0) == 0
        dh = dh_ref[...]
        rstd = rs_ref[...]
        n = x_ref[...] * rstd
        w = nw_ref[...]
        _acc(dsh_ref, _colsum(dh), first)
        _acc(dsc_ref, _colsum(dh * (n * w)), first)
        dnw = dh * (1.0 + sc_ref[...])
        _acc(dnw_ref, _colsum(dnw * n), first)
        dn = dnw * w
        dx = dres_ref[...] + rstd * (dn - n * jnp.mean(dn * n, axis=-1, keepdims=True))
        dx_ref[...] = dx
        if has_res:
            dmo_ref[...] = (dx * gt_ref[...]).astype(BF16)
            _acc(dgt_ref, _colsum(dx * mo_ref[...]), first)

    vec = _full((1, D))
    vshape = jax.ShapeDtypeStruct((1, D), F32)
    ins = [dh, xin, rstd, nw, sc, dres] + ([mo, gt] if has_res else [])
    in_specs = [_rows(tm, D), _rows(tm, D), _rows(tm, 1), vec, vec, _rows(tm, D)] + ([_rows(tm, D), vec] if has_res else [])
    outs = [jax.ShapeDtypeStruct((S, D), F32), vshape, vshape, vshape]
    out_specs = [_rows(tm, D), vec, vec, vec]
    if has_res:
        outs += [jax.ShapeDtypeStruct((S, D), BF16), vshape]
        out_specs += [_rows(tm, D), vec]
    return pl.pallas_call(body, name=name, grid=(S // tm,), in_specs=in_specs, out_specs=out_specs,
                          out_shape=outs, compiler_params=_cparams(("arbitrary",)))(*ins)


def _final(x2, f, gt2, nfw, target, tm=256):
    S = x2.shape[0]

    def body(x2_ref, f_ref, gt_ref, w_ref, t_ref, loss_ref, dx_ref, df_ref, dgt_ref, dw_ref):
        first = pl.program_id(0) == 0
        f = f_ref[...]
        gt = gt_ref[...]
        w = w_ref[...]
        x3 = x2_ref[...] + gt * f
        rstd = lax.rsqrt(jnp.mean(x3 * x3, axis=-1, keepdims=True) + RMS_EPS)
        n = x3 * rstd
        e = n * w - t_ref[...]
        part = 0.5 * jnp.sum(jnp.mean(e * e, axis=-1, keepdims=True), axis=0, keepdims=True)
        _acc(loss_ref, jnp.broadcast_to(part, (SUBLANES, LANES)), first)
        dy = e * (1.0 / D)
        _acc(dw_ref, _colsum(dy * n), first)
        dn = dy * w
        dx = rstd * (dn - n * jnp.mean(dn * n, axis=-1, keepdims=True))
        dx_ref[...] = dx
        df_ref[...] = (dx * gt).astype(BF16)
        _acc(dgt_ref, _colsum(dx * f), first)

    vec = _full((1, D))
    vshape = jax.ShapeDtypeStruct((1, D), F32)
    return pl.pallas_call(
        body, name="final_loss", grid=(S // tm,),
        in_specs=[_rows(tm, D), _rows(tm, D), vec, vec, _rows(tm, D)],
        out_specs=[_full((SUBLANES, LANES)), _rows(tm, D), _rows(tm, D), vec, vec],
        out_shape=[jax.ShapeDtypeStruct((SUBLANES, LANES), F32), jax.ShapeDtypeStruct((S, D), F32),
                   jax.ShapeDtypeStruct((S, D), BF16), vshape, vshape],
        compiler_params=_cparams(("arbitrary",)))(x2, f, gt2, nfw, target)


def _gate_fwd(P, bga, bgr, y_att, y_rwkv, tm=256):
    S = P.shape[0]

    def body(pa_ref, pr_ref, ba_ref, br_ref, ya_ref, yr_ref, mix_ref):
        ga = jax.nn.sigmoid(pa_ref[...] + ba_ref[...])
        gr = jax.nn.sigmoid(pr_ref[...] + br_ref[...])
        mix_ref[...] = (ga * ya_ref[...] + gr * yr_ref[...]).astype(BF16)

    vec = _full((1, D))
    return pl.pallas_call(
        body, name="gate_fwd", grid=(S // tm,),
        in_specs=[_rows(tm, D, C_GA // D), _rows(tm, D, C_GR // D), vec, vec, _rows(tm, D), _rows(tm, D)],
        out_specs=_rows(tm, D), out_shape=jax.ShapeDtypeStruct((S, D), BF16),
        compiler_params=_cparams(("parallel",)))(P, P, bga, bgr, y_att, y_rwkv)


def _gate_bwd(dmix, P, bga, bgr, y_att, y_rwkv, tm=256):
    S = P.shape[0]

    def body(dm_ref, pa_ref, pr_ref, ba_ref, br_ref, ya_ref, yr_ref, dya_ref, dyr_ref, dpa_ref, dpr_ref, dba_ref, dbr_ref):
        first = pl.program_id(0) == 0
        dm = dm_ref[...]
        ga = jax.nn.sigmoid(pa_ref[...] + ba_ref[...])
        gr = jax.nn.sigmoid(pr_ref[...] + br_ref[...])
        dya_ref[...] = (dm * ga).astype(BF16)
        dyr_ref[...] = (dm * gr).astype(BF16)
        dpa = dm * ya_ref[...] * ga * (1.0 - ga)
        dpr = dm * yr_ref[...] * gr * (1.0 - gr)
        dpa_ref[...] = dpa.astype(BF16)
        dpr_ref[...] = dpr.astype(BF16)
        _acc(dba_ref, _colsum(dpa), first)
        _acc(dbr_ref, _colsum(dpr), first)

    vec = _full((1, D))
    row = _rows(tm, D)
    rshape = jax.ShapeDtypeStruct((S, D), BF16)
    vshape = jax.ShapeDtypeStruct((1, D), F32)
    return pl.pallas_call(
        body, name="gate_bwd", grid=(S // tm,),
        in_specs=[row, _rows(tm, D, C_GA // D), _rows(tm, D, C_GR // D), vec, vec, row, row],
        out_specs=[row, row, row, row, vec, vec],
        out_shape=[rshape, rshape, rshape, rshape, vshape, vshape],
        compiler_params=_cparams(("arbitrary",)))(dmix, P, P, bga, bgr, y_att, y_rwkv)


def _conv_fwd(u, conv_w8, conv_b, tm=256, tn=256):
    S = u.shape[0]
    nj = D_FF // tn

    def conv(u_ref, h_ref, w_ref, b_ref, first):
        u = u_ref[...]
        h = h_ref[...]
        w = w_ref[...]
        return b_ref[...] + w[0:1] * _shift_down(u, h, 2, first) + w[1:2] * _shift_down(u, h, 1, first) + w[2:3] * u

    def body(ug_ref, hg_ref, uv_ref, hv_ref, wg_ref, wv_ref, bg_ref, bv_ref, act_ref):
        first = pl.program_id(0) == 0
        g = conv(ug_ref, hg_ref, wg_ref, bg_ref, first)
        v = conv(uv_ref, hv_ref, wv_ref, bv_ref, first)
        act_ref[...] = (g * jax.nn.sigmoid(g) * v).astype(BF16)

    blk = lambda off: pl.BlockSpec((tm, tn), lambda i, j: (i, j + off))
    halo = lambda off: pl.BlockSpec((SUBLANES, tn), lambda i, j: (jnp.maximum(i * (tm // SUBLANES) - 1, 0), j + off))
    wsp = lambda off: pl.BlockSpec((SUBLANES, tn), lambda i, j: (0, j + off))
    bsp = lambda off: pl.BlockSpec((1, tn), lambda i, j: (0, j + off))
    return pl.pallas_call(
        body, name="conv_fwd", grid=(S // tm, nj),
        in_specs=[blk(0), halo(0), blk(nj), halo(nj), wsp(0), wsp(nj), bsp(0), bsp(nj)],
        out_specs=pl.BlockSpec((tm, tn), lambda i, j: (i, j)),
        out_shape=jax.ShapeDtypeStruct((S, D_FF), BF16),
        compiler_params=_cparams(("parallel", "parallel")))(u, u, u, u, conv_w8, conv_w8, conv_b, conv_b)


def _conv_bwd_a(dact, u, conv_w8, conv_b, tm=256, tn=256):
    S = u.shape[0]
    nj = D_FF // tn

    def half(u_ref, h_ref, w_ref, b_ref, first):
        u = u_ref[...]
        h = h_ref[...]
        w = w_ref[...]
        u2, u1 = _shift_down(u, h, 2, first), _shift_down(u, h, 1, first)
        return b_ref[...] + w[0:1] * u2 + w[1:2] * u1 + w[2:3] * u, (u2, u1, u)

    def wgrad(d, taps):
        z = jnp.zeros((SUBLANES - 3, d.shape[1]), F32)
        return jnp.concatenate([_colsum(d * taps[0]), _colsum(d * taps[1]), _colsum(d * taps[2]), z], axis=0)

    def body(da_ref, ug_ref, hg_ref, uv_ref, hv_ref, wg_ref, wv_ref, bg_ref, bv_ref,
             dg_ref, dv_ref, dwg_ref, dwv_ref, dbg_ref, dbv_ref):
        first = pl.program_id(1) == 0
        g, tg = half(ug_ref, hg_ref, wg_ref, bg_ref, first)
        v, tv = half(uv_ref, hv_ref, wv_ref, bv_ref, first)
        da = da_ref[...].astype(F32)
        sg = jax.nn.sigmoid(g)
        dg = da * v * (sg * (1.0 + g * (1.0 - sg)))
        dv = da * (g * sg)
        dg_ref[...] = dg
        dv_ref[...] = dv
        _acc(dwg_ref, wgrad(dg, tg), first)
        _acc(dwv_ref, wgrad(dv, tv), first)
        _acc(dbg_ref, _colsum(dg), first)
        _acc(dbv_ref, _colsum(dv), first)

    blk = lambda off: pl.BlockSpec((tm, tn), lambda j, i: (i, j + off))
    halo = lambda off: pl.BlockSpec((SUBLANES, tn), lambda j, i: (jnp.maximum(i * (tm // SUBLANES) - 1, 0), j + off))
    wsp = lambda off: pl.BlockSpec((SUBLANES, tn), lambda j, i: (0, j + off))
    bsp = lambda off: pl.BlockSpec((1, tn), lambda j, i: (0, j + off))
    f = jax.ShapeDtypeStruct
    outs = pl.pallas_call(
        body, name="conv_bwd_a", grid=(nj, S // tm),
        in_specs=[pl.BlockSpec((tm, tn), lambda j, i: (i, j)), blk(0), halo(0), blk(nj), halo(nj), wsp(0), wsp(nj), bsp(0), bsp(nj)],
        out_specs=[pl.BlockSpec((tm, tn), lambda j, i: (i, j)), pl.BlockSpec((tm, tn), lambda j, i: (i, j)),
                   pl.BlockSpec((SUBLANES, tn), lambda j, i: (0, j)), pl.BlockSpec((SUBLANES, tn), lambda j, i: (0, j)),
                   pl.BlockSpec((1, tn), lambda j, i: (0, j)), pl.BlockSpec((1, tn), lambda j, i: (0, j))],
        out_shape=[f((S, D_FF), F32), f((S, D_FF), F32), f((SUBLANES, D_FF), F32), f((SUBLANES, D_FF), F32),
                   f((1, D_FF), F32), f((1, D_FF), F32)],
        compiler_params=_cparams(("parallel", "arbitrary")))(dact, u, u, u, u, conv_w8, conv_w8, conv_b, conv_b)
    return outs


def _conv_bwd_b(duc, conv_w8, tm=256, tn=256):
    S, W = duc.shape

    def body(d_ref, h_ref, w_ref, o_ref):
        last = pl.program_id(0) == pl.num_programs(0) - 1
        d = d_ref[...]
        h = h_ref[...]
        w = w_ref[...]
        o_ref[...] = (w[2:3] * d + w[1:2] * _shift_up(d, h, 1, last) + w[0:1] * _shift_up(d, h, 2, last)).astype(BF16)

    last_tile = S // SUBLANES - 1
    return pl.pallas_call(
        body, name="conv_bwd_b", grid=(S // tm, W // tn),
        in_specs=[pl.BlockSpec((tm, tn), lambda i, j: (i, j)),
                  pl.BlockSpec((SUBLANES, tn), lambda i, j: (jnp.minimum((i + 1) * (tm // SUBLANES), last_tile), j)),
                  pl.BlockSpec((SUBLANES, tn), lambda i, j: (0, j))],
        out_specs=pl.BlockSpec((tm, tn), lambda i, j: (i, j)),
        out_shape=jax.ShapeDtypeStruct((S, W), BF16),
        compiler_params=_cparams(("parallel", "parallel")))(duc, duc, conv_w8)


ATT_SCALE = HEAD ** -0.5
NEG = -1e30


def _att_blocks_per_seq(S):
    return [S // (QBLK * d) for (_, d) in ATT_PATTERNS]


def _att_scores(q, kc, kp, has_prev):
    qi = lax.broadcasted_iota(jnp.int32, (QBLK, QBLK), 0)
    kj = lax.broadcasted_iota(jnp.int32, (QBLK, QBLK), 1)
    nt = (((1,), (1,)), ((), ()))
    s_c = lax.dot_general(q, kc, nt, preferred_element_type=F32) * ATT_SCALE
    s_p = lax.dot_general(q, kp, nt, preferred_element_type=F32) * ATT_SCALE
    s_c = jnp.where(kj <= qi, s_c, NEG)
    s_p = jnp.where(jnp.logical_and(kj >= qi, has_prev), s_p, NEG)
    return s_c, s_p


def _att_fwd(q, k, v):
    GH, S, _ = q.shape
    nb = S // QBLK
    bps = _att_blocks_per_seq(S)

    def body(q_ref, k_ref, v_ref, o_ref, l_ref):
        g = pl.program_id(0) // ATT_HEADS
        per = jnp.where(g == 0, bps[0], jnp.where(g == 1, bps[1], bps[2]))

        def blk(n, carry):
            cur = pl.ds(pl.multiple_of(n * QBLK, QBLK), QBLK)
            prv = pl.ds(pl.multiple_of(jnp.maximum(n - 1, 0) * QBLK, QBLK), QBLK)
            s_c, s_p = _att_scores(q_ref[cur, :], k_ref[cur, :], k_ref[prv, :], (n % per) != 0)
            m = jnp.maximum(jnp.max(s_c, axis=1, keepdims=True), jnp.max(s_p, axis=1, keepdims=True))
            p_c = jnp.exp(s_c - m)
            p_p = jnp.exp(s_p - m)
            den = jnp.sum(p_c, axis=1, keepdims=True) + jnp.sum(p_p, axis=1, keepdims=True)
            num = (jnp.dot(p_c.astype(BF16), v_ref[cur, :], preferred_element_type=F32)
                   + jnp.dot(p_p.astype(BF16), v_ref[prv, :], preferred_element_type=F32))
            o_ref[cur, :] = num / den
            l_ref[cur, :] = jnp.broadcast_to(m + jnp.log(den), (QBLK, HEAD))
            return carry

        lax.fori_loop(0, nb, blk, 0)

    spec = pl.BlockSpec((None, S, HEAD), lambda i: (i, 0, 0))
    shp = jax.ShapeDtypeStruct((GH, S, HEAD), F32)
    return pl.pallas_call(body, name="att_fwd", grid=(GH,), in_specs=[spec, spec, spec], out_specs=[spec, spec],
                          out_shape=[shp, shp], compiler_params=_cparams(("parallel",)))(q, k, v)


def _att_bwd(q, k, v, o, l, do, dl):
    GH, S, _ = q.shape
    nb = S // QBLK
    bps = _att_blocks_per_seq(S)
    tn = (((0,), (0,)), ((), ()))
    nt = (((1,), (1,)), ((), ()))

    def body(q_ref, k_ref, v_ref, o_ref, l_ref, do_ref, dl_ref, dq_ref, dk_ref, dv_ref):
        g = pl.program_id(0) // ATT_HEADS
        per = jnp.where(g == 0, bps[0], jnp.where(g == 1, bps[1], bps[2]))
        dk_ref[...] = jnp.zeros_like(dk_ref)
        dv_ref[...] = jnp.zeros_like(dv_ref)

        def blk(n, carry):
            cur = pl.ds(pl.multiple_of(n * QBLK, QBLK), QBLK)
            prv = pl.ds(pl.multiple_of(jnp.maximum(n - 1, 0) * QBLK, QBLK), QBLK)
            qb = q_ref[cur, :]
            kc, kp, vc, vp = k_ref[cur, :], k_ref[prv, :], v_ref[cur, :], v_ref[prv, :]
            s_c, s_p = _att_scores(qb, kc, kp, (n % per) != 0)
            lse = l_ref[cur, :][:, 0:1]
            p_c = jnp.exp(s_c - lse)
            p_p = jnp.exp(s_p - lse)
            dob = do_ref[cur, :]
            delta = jnp.sum(dob * o_ref[cur, :] - dl_ref[cur, :], axis=1, keepdims=True)
            dob16 = dob.astype(BF16)
            dp_c = lax.dot_general(dob16, vc, nt, preferred_element_type=F32)
            dp_p = lax.dot_general(dob16, vp, nt, preferred_element_type=F32)
            ds_c = (p_c * (dp_c - delta) * ATT_SCALE).astype(BF16)
            ds_p = (p_p * (dp_p - delta) * ATT_SCALE).astype(BF16)
            dq_ref[cur, :] = (jnp.dot(ds_c, kc, preferred_element_type=F32) + jnp.dot(ds_p, kp, preferred_element_type=F32))
            dk_ref[cur, :] += lax.dot_general(ds_c, qb, tn, preferred_element_type=F32)
            dv_ref[cur, :] += lax.dot_general(p_c.astype(BF16), dob16, tn, preferred_element_type=F32)
            dk_ref[prv, :] += lax.dot_general(ds_p, qb, tn, preferred_element_type=F32)
            dv_ref[prv, :] += lax.dot_general(p_p.astype(BF16), dob16, tn, preferred_element_type=F32)
            return carry

        lax.fori_loop(0, nb, blk, 0)

    spec = pl.BlockSpec((None, S, HEAD), lambda i: (i, 0, 0))
    shp = jax.ShapeDtypeStruct((GH, S, HEAD), F32)
    return pl.pallas_call(body, name="att_bwd", grid=(GH,), in_specs=[spec] * 7, out_specs=[spec] * 3,
                          out_shape=[shp] * 3, compiler_params=_cparams(("parallel",)))(q, k, v, o, l, do, dl)


def _att_combine_fwd(o3, l3, tm=512):
    _, R, _ = o3.shape

    def body(o_ref, l_ref, a_ref):
        l0, l1, l2 = l_ref[0], l_ref[1], l_ref[2]
        m = jnp.maximum(jnp.maximum(l0, l1), l2)
        e0, e1, e2 = jnp.exp(l0 - m), jnp.exp(l1 - m), jnp.exp(l2 - m)
        a_ref[...] = (e0 * o_ref[0] + e1 * o_ref[1] + e2 * o_ref[2]) / (e0 + e1 + e2)

    spec3 = pl.BlockSpec((3, tm, LANES), lambda i: (0, i, 0))
    return pl.pallas_call(body, name="att_combine_fwd", grid=(R // tm,), in_specs=[spec3, spec3],
                          out_specs=_rows(tm, LANES), out_shape=jax.ShapeDtypeStruct((R, LANES), F32),
                          compiler_params=_cparams(("parallel",)))(o3, l3)


def _att_combine_bwd(da, o3, l3, tm=512):
    _, R, _ = o3.shape

    def body(da_ref, o_ref, l_ref, do_ref, dl_ref):
        da = da_ref[...]
        l0, l1, l2 = l_ref[0], l_ref[1], l_ref[2]
        m = jnp.maximum(jnp.maximum(l0, l1), l2)
        e0, e1, e2 = jnp.exp(l0 - m), jnp.exp(l1 - m), jnp.exp(l2 - m)
        inv = 1.0 / (e0 + e1 + e2)
        w = (e0 * inv, e1 * inv, e2 * inv)
        dw = (da * o_ref[0], da * o_ref[1], da * o_ref[2])
        mean = w[0] * dw[0] + w[1] * dw[1] + w[2] * dw[2]
        for g in range(3):
            do_ref[g] = w[g] * da
            dl_ref[g] = w[g] * (dw[g] - mean)

    spec3 = pl.BlockSpec((3, tm, LANES), lambda i: (0, i, 0))
    shp = jax.ShapeDtypeStruct((3, R, LANES), F32)
    return pl.pallas_call(body, name="att_combine_bwd", grid=(R // tm,), in_specs=[_rows(tm, LANES), spec3, spec3],
                          out_specs=[spec3, spec3], out_shape=[shp, shp],
                          compiler_params=_cparams(("parallel",)))(da, o3, l3)


@jax.custom_vjp
def _bdot(a, b):
    return jnp.dot(a.astype(BF16), b.astype(BF16), preferred_element_type=F32)


def _bdot_fwd(a, b):
    return _bdot(a, b), (a, b)


def _bdot_bwd(res, ct):
    a, b = res
    ct16 = ct.astype(BF16)
    da = lax.dot_general(ct16, b.astype(BF16), (((1,), (1,)), ((), ())), preferred_element_type=F32)
    db = lax.dot_general(a.astype(BF16), ct16, (((0,), (0,)), ((), ())), preferred_element_type=F32)
    return da, db


_bdot.defvjp(_bdot_fwd, _bdot_bwd)


def _head_sum(x):
    hi = lax.Precision.HIGHEST
    sel = (lax.broadcasted_iota(jnp.int32, (D, LANES), 0) // HEAD == lax.broadcasted_iota(jnp.int32, (D, LANES), 1)).astype(F32)
    sel_t = (lax.broadcasted_iota(jnp.int32, (LANES, D), 1) // HEAD == lax.broadcasted_iota(jnp.int32, (LANES, D), 0)).astype(F32)
    return jnp.dot(jnp.dot(x, sel, precision=hi, preferred_element_type=F32), sel_t, precision=hi, preferred_element_type=F32)


def _softplus(z):
    return jnp.maximum(z, 0.0) + jnp.log(1.0 + jnp.exp(-jnp.abs(z)))


def _rwkv_prep_fn(zr, zrp, zk, zkp, zv, zvp, zl, zlp, mu_r, mu_k, mu_v, mu_l, w0, a0, k_k, k_a, w2, a2, g2p):
    r = zr + (zrp - zr) * mu_r
    k = zk + (zkp - zk) * mu_k
    v = zv + (zvp - zv) * mu_v
    lo = zl + (zlp - zl) * mu_l
    w_low, a_low, g_low = lo[:, 0:LORA_W], lo[:, LORA_W:LORA_W + LORA_A], lo[:, LANES:LANES + G_PAD]
    w_log = -_softplus(-(w0 + _bdot(jnp.tanh(w_low), w2))) - 0.5
    decay = jnp.exp(-jnp.exp(w_log))
    a = jax.nn.sigmoid(a0 + _bdot(a_low, a2))
    g = _bdot(jax.nn.sigmoid(g_low), g2p)
    kmod = k * (1.0 + (a - 1.0) * k_a)
    kk = k * k_k
    kk = kk / jnp.maximum(jnp.sqrt(_head_sum(kk * kk)), 1e-12)
    return r, decay, kmod, v, -kk, kk * a, g


def _rwkv_prep_specs(tm):
    vec = _full((1, D))
    slabs = []
    for col in (C_R // D, C_K // D, C_V // D):
        slabs += [_rows(tm, D, col), _prev8(tm, D, col)]
    slabs += [_rows(tm, LORA_PAD, C_LORA // LORA_PAD), _prev8(tm, LORA_PAD, C_LORA // LORA_PAD)]
    params = [vec, vec, vec, _full((1, LORA_PAD)), vec, vec, vec, vec,
              _full((LORA_W, D)), _full((LORA_A, D)), _full((G_PAD, D))]
    return slabs, params


def _prep_inputs(refs, first):
    vals = []
    for s in range(4):
        z = refs[2 * s][...]
        vals += [z, _shift_down(z, refs[2 * s + 1][...], 1, first)]
    return vals + [r[...] for r in refs[8:19]]


def _rwkv_prep(P, params, tm=256):
    S = P.shape[0]
    slabs, pspecs = _rwkv_prep_specs(tm)

    def body(*refs):
        outs = _rwkv_prep_fn(*_prep_inputs(refs, pl.program_id(0) == 0))
        for o_ref, val in zip(refs[19:], outs):
            o_ref[...] = val

    shp = jax.ShapeDtypeStruct((S, D), F32)
    return pl.pallas_call(body, name="rwkv_prep", grid=(S // tm,), in_specs=slabs + pspecs,
                          out_specs=[_rows(tm, D)] * 7, out_shape=[shp] * 7,
                          compiler_params=_cparams(("parallel",)))(*([P] * 8), *params)


def _rwkv_prep_bwd(P, params, cts_a, cts_b, tm=128):
    S = P.shape[0]
    slabs, pspecs = _rwkv_prep_specs(tm)
    has_b = [c is not None for c in cts_b]
    n_ct = 7 + sum(has_b)

    def body(*refs):
        first = pl.program_id(0) == 0
        ins = _prep_inputs(refs, first)
        ct_refs = refs[19:19 + n_ct]
        out_refs = refs[19 + n_ct:]
        cts, pos = [], 7
        for i in range(7):
            c = ct_refs[i][...]
            if has_b[i]:
                c = c + ct_refs[pos][...]
                pos += 1
            cts.append(c)
        _, vjp = jax.vjp(_rwkv_prep_fn, *ins)
        grads = vjp(tuple(cts))
        for s in range(4):
            out_refs[s][...] = grads[2 * s]
            out_refs[4 + s][...] = grads[2 * s + 1]
        for i in range(11):
            _acc(out_refs[8 + i], grads[8 + i], first)

    ct_in = list(cts_a) + [c for c in cts_b if c is not None]
    row, lrow = _rows(tm, D), _rows(tm, LORA_PAD)
    f = jax.ShapeDtypeStruct
    zshapes = [f((S, D), F32)] * 3 + [f((S, LORA_PAD), F32)]
    pshapes = [f((1, D), F32)] * 3 + [f((1, LORA_PAD), F32)] + [f((1, D), F32)] * 4 + [f((LORA_W, D), F32), f((LORA_A, D), F32), f((G_PAD, D), F32)]
    return pl.pallas_call(
        body, name="rwkv_prep_bwd", grid=(S // tm,),
        in_specs=slabs + pspecs + [row] * n_ct,
        out_specs=[row, row, row, lrow] * 2 + pspecs,
        out_shape=zshapes * 2 + pshapes,
        compiler_params=_cparams(("arbitrary",)))(*([P] * 8), *params, *ct_in)


def _shift_add(a, b, tm=256):
    S, W = a.shape

    def body(a_ref, b_ref, h_ref, o_ref):
        last = pl.program_id(0) == pl.num_programs(0) - 1
        o_ref[...] = (a_ref[...] + _shift_up(b_ref[...], h_ref[...], 1, last)).astype(BF16)

    return pl.pallas_call(body, name="shift_add", grid=(S // tm,),
                          in_specs=[_rows(tm, W), _rows(tm, W), _next8(tm, W, S)],
                          out_specs=_rows(tm, W), out_shape=jax.ShapeDtypeStruct((S, W), BF16),
                          compiler_params=_cparams(("parallel",)))(a, b, b)


def _rwkv_post_fn(y, r, kmod, v, g, lnx_w, lnx_b, r_k):
    mean = _head_sum(y) * (1.0 / HEAD)
    yc = y - mean
    var = _head_sum(yc * yc) * (1.0 / HEAD)
    yn = yc * lax.rsqrt(var + GN_EPS) * lnx_w + lnx_b
    bonus = _head_sum(r * kmod * r_k) * v
    return (yn + bonus) * g


def _rwkv_post(y, r, kmod, v, g, lnx_w, lnx_b, r_k, tm=256):
    S = y.shape[0]

    def body(y_ref, r_ref, k_ref, v_ref, g_ref, w_ref, b_ref, rk_ref, o_ref):
        o_ref[...] = _rwkv_post_fn(y_ref[...], r_ref[...], k_ref[...], v_ref[...], g_ref[...],
                                   w_ref[...], b_ref[...], rk_ref[...]).astype(BF16)

    row, vec = _rows(tm, D), _full((1, D))
    return pl.pallas_call(body, name="rwkv_post", grid=(S // tm,), in_specs=[row] * 5 + [vec] * 3, out_specs=row,
                          out_shape=jax.ShapeDtypeStruct((S, D), BF16),
                          compiler_params=_cparams(("parallel",)))(y, r, kmod, v, g, lnx_w, lnx_b, r_k)


def _rwkv_post_bwd(drw, y, r, kmod, v, g, lnx_w, lnx_b, r_k, tm=256):
    S = y.shape[0]

    def body(d_ref, y_ref, r_ref, k_ref, v_ref, g_ref, w_ref, b_ref, rk_ref, *out_refs):
        first = pl.program_id(0) == 0
        _, vjp = jax.vjp(_rwkv_post_fn, y_ref[...], r_ref[...], k_ref[...], v_ref[...], g_ref[...],
                         w_ref[...], b_ref[...], rk_ref[...])
        grads = vjp(d_ref[...])
        for i in range(5):
            out_refs[i][...] = grads[i]
        for i in range(5, 8):
            _acc(out_refs[i], grads[i], first)

    row, vec = _rows(tm, D), _full((1, D))
    f = jax.ShapeDtypeStruct
    return pl.pallas_call(body, name="rwkv_post_bwd", grid=(S // tm,), in_specs=[row] * 6 + [vec] * 3,
                          out_specs=[row] * 5 + [vec] * 3, out_shape=[f((S, D), F32)] * 5 + [f((1, D), F32)] * 3,
                          compiler_params=_cparams(("arbitrary",)))(drw, y, r, kmod, v, g, lnx_w, lnx_b, r_k)


def _col(tile, ii, lane_lo):
    a = jnp.broadcast_to(tile[0:HEAD, ii:ii + 1], (HEAD, LANES))
    b = jnp.broadcast_to(tile[HEAD:2 * HEAD, ii:ii + 1], (HEAD, LANES))
    return jnp.where(lane_lo, a, b)


SCAN_NG = SCAN_TB // SUBLANES
N_PIECES = 2


def _scan_sources(lane_refs, mxu_refs, t_ref, p_ref):
    for o, ref in enumerate(lane_refs):
        t_ref[o] = ref[...].T
    for o, ref in enumerate(mxu_refs):
        rest = ref[...]
        for p in range(N_PIECES):
            piece = rest.astype(BF16).astype(F32)
            rest = rest - piece
            p_ref[o, p] = piece


def _gen_tiles(buf_ref, g, n_lane, n_mxu, t_ref, p_ref):
    lane_lo = lax.broadcasted_iota(jnp.int32, (HEAD, LANES), 1) < HEAD
    tiles = [pltpu.roll(t_ref[o], (LANES - SUBLANES * g) % LANES, 1) for o in range(n_lane)]
    for ii in range(SUBLANES):
        for o in range(n_lane):
            buf_ref[ii, o] = _col(tiles[o], ii, lane_lo)
    if n_mxu == 0:
        return
    diag = (lax.broadcasted_iota(jnp.int32, (HEAD, LANES), 1) % HEAD
            == lax.broadcasted_iota(jnp.int32, (HEAD, LANES), 0)).astype(BF16)
    ones = (lax.broadcasted_iota(jnp.int32, (LANES, LANES), 0) // HEAD
            == lax.broadcasted_iota(jnp.int32, (LANES, LANES), 1) // HEAD).astype(BF16)
    start = pl.multiple_of(g * SUBLANES, SUBLANES)
    for o in range(n_mxu):
        cols = None
        for p in range(N_PIECES):
            rows = p_ref[o, p, pl.ds(start, SUBLANES), :].astype(BF16)
            lhs = jnp.concatenate([jnp.broadcast_to(rows[ii:ii + 1], (HEAD, LANES)) * diag for ii in range(SUBLANES)], axis=0)
            part = jnp.dot(lhs, ones, preferred_element_type=F32)
            cols = part if cols is None else cols + part
        for ii in range(SUBLANES):
            buf_ref[ii, n_lane + o] = cols[ii * HEAD:(ii + 1) * HEAD]


def _scan_scratch(n_lane, n_mxu):
    tiles = pltpu.VMEM((SUBLANES, n_lane + n_mxu, HEAD, LANES), F32)
    return [pltpu.VMEM((HEAD, LANES), F32), pltpu.VMEM((max(n_lane, 1), LANES, SCAN_TB), F32),
            pltpu.VMEM((max(n_mxu, 1), N_PIECES, SCAN_TB, LANES), F32), tiles, tiles]


def _scan_fwd(r, w, k, v, a, b):
    S = r.shape[0]
    nblk = S // SCAN_TB
    npair = N_HEADS // 2

    def body(r_ref, w_ref, k_ref, v_ref, a_ref, b_ref, y_ref, sall_ref, s_ref, t_ref, p_ref, buf0, buf1):
        @pl.when(pl.program_id(1) == 0)
        def _():
            s_ref[...] = jnp.zeros_like(s_ref)

        _scan_sources((w_ref,), (r_ref, k_ref, a_ref, b_ref), t_ref, p_ref)
        gen = functools.partial(_gen_tiles, n_lane=1, n_mxu=4, t_ref=t_ref, p_ref=p_ref)

        def steps(buf, g, st):
            for ii in range(SUBLANES):
                t = g * SUBLANES + ii
                wc, rc, kc, ac, bc = [buf[ii, o] for o in range(5)]
                sall_ref[t] = st
                sa = jnp.sum(st * ac, axis=0, keepdims=True)
                st = st * wc + bc * sa + kc * v_ref[pl.ds(t, 1), :]
                y_ref[pl.ds(t, 1), :] = jnp.sum(st * rc, axis=0, keepdims=True)
            return st

        gen(buf0, 0)

        def two_groups(i, st):
            g = 2 * i
            gen(buf1, g + 1)
            st = steps(buf0, g, st)
            gen(buf0, jnp.minimum(g + 2, SCAN_NG - 1))
            return steps(buf1, g + 1, st)

        s_ref[...] = lax.fori_loop(0, SCAN_NG // 2, two_groups, s_ref[...])

    blk = pl.BlockSpec((SCAN_TB, LANES), lambda p, i: (i, p))
    return pl.pallas_call(
        body, name="scan_fwd", grid=(npair, nblk), in_specs=[blk] * 6,
        out_specs=[blk, pl.BlockSpec((SCAN_TB, None, HEAD, LANES), lambda p, i: (i, p, 0, 0))],
        out_shape=[jax.ShapeDtypeStruct((S, D), F32), jax.ShapeDtypeStruct((S, npair, HEAD, LANES), F32)],
        scratch_shapes=_scan_scratch(1, 4),
        compiler_params=_cparams(("parallel", "arbitrary")))(r, w, k, v, a, b)


def _scan_bwd(r, w, k, v, a, b, sall, dy):
    S = r.shape[0]
    nblk = S // SCAN_TB
    npair = N_HEADS // 2
    NG = SCAN_TB // SUBLANES
    nt = (((1,), (1,)), ((), ()))

    def body(r_ref, w_ref, k_ref, v_ref, a_ref, b_ref, sall_ref, dy_ref,
             dr_ref, dw_ref, dk_ref, dv_ref, da_ref, db_ref, ds_ref, t_ref, p_ref, buf0, buf1):
        @pl.when(pl.program_id(1) == 0)
        def _():
            ds_ref[...] = jnp.zeros_like(ds_ref)

        _scan_sources((w_ref, r_ref, k_ref), (a_ref, b_ref), t_ref, p_ref)
        gen = functools.partial(_gen_tiles, n_lane=3, n_mxu=2, t_ref=t_ref, p_ref=p_ref)
        half_sel = (lax.broadcasted_iota(jnp.int32, (SUBLANES, LANES), 0)
                    == lax.broadcasted_iota(jnp.int32, (SUBLANES, LANES), 1) // HEAD).astype(BF16)

        def key_grad(ref, t, z):
            res = lax.dot_general(half_sel, z.astype(BF16), nt, preferred_element_type=F32)
            ref[pl.ds(t, 1), 0:HEAD] = res[0:1]
            ref[pl.ds(t, 1), HEAD:2 * HEAD] = res[1:2]

        def steps(buf, g, dst):
            for ii in reversed(range(SUBLANES)):
                t = g * SUBLANES + ii
                wc, rc, kc, ac, bc = [buf[ii, o] for o in range(5)]
                vrow = v_ref[pl.ds(t, 1), :]
                dyrow = dy_ref[pl.ds(t, 1), :]
                sp = sall_ref[t]
                sa = jnp.sum(sp * ac, axis=0, keepdims=True)
                sn = sp * wc + bc * sa + kc * vrow
                dsn = dst + rc * dyrow
                dv_ref[pl.ds(t, 1), :] = jnp.sum(dsn * kc, axis=0, keepdims=True)
                dsa = jnp.sum(dsn * bc, axis=0, keepdims=True)
                key_grad(dr_ref, t, sn * dyrow)
                key_grad(dw_ref, t, dsn * sp)
                key_grad(dk_ref, t, dsn * vrow)
                key_grad(da_ref, t, sp * dsa)
                key_grad(db_ref, t, dsn * sa)
                dst = dsn * wc + ac * dsa
            return dst

        gen(buf0, SCAN_NG - 1)

        def two_groups(i, dst):
            g = SCAN_NG - 1 - 2 * i
            gen(buf1, g - 1)
            dst = steps(buf0, g, dst)
            gen(buf0, jnp.maximum(g - 2, 0))
            return steps(buf1, g - 1, dst)

        ds_ref[...] = lax.fori_loop(0, SCAN_NG // 2, two_groups, ds_ref[...])

    blk = pl.BlockSpec((SCAN_TB, LANES), lambda p, i: (nblk - 1 - i, p))
    shp = jax.ShapeDtypeStruct((S, D), F32)
    return pl.pallas_call(
        body, name="scan_bwd", grid=(npair, nblk),
        in_specs=[blk] * 6 + [pl.BlockSpec((SCAN_TB, None, HEAD, LANES), lambda p, i: (nblk - 1 - i, p, 0, 0)), blk],
        out_specs=[blk] * 6, out_shape=[shp] * 6,
        scratch_shapes=_scan_scratch(3, 2),
        compiler_params=_cparams(("parallel", "arbitrary")))(r, w, k, v, a, b, sall, dy)


def _ada_fwd(c8, w_ada, b_ada):
    def body(c_ref, w_ref, b_ref, o_ref):
        o_ref[...] = jnp.dot(c_ref[...].astype(BF16), w_ref[...], preferred_element_type=F32) + b_ref[...]

    tn = 1536
    return pl.pallas_call(body, name="ada_fwd", grid=(6 * D // tn,),
                          in_specs=[_full((SUBLANES, D)), pl.BlockSpec((D, tn), lambda j: (0, j)), pl.BlockSpec((1, tn), lambda j: (0, j))],
                          out_specs=pl.BlockSpec((SUBLANES, tn), lambda j: (0, j)),
                          out_shape=jax.ShapeDtypeStruct((SUBLANES, 6 * D), F32),
                          compiler_params=_cparams(("parallel",)))(c8, w_ada, b_ada)


def _outer(col, row):
    N = row.shape[1]
    tn = 1536

    def body(c_ref, r_ref, o_ref):
        o_ref[...] = c_ref[...] * r_ref[...]

    return pl.pallas_call(body, name="ada_wgrad", grid=(N // tn,),
                          in_specs=[_full((D, 1)), pl.BlockSpec((1, tn), lambda j: (0, j))],
                          out_specs=pl.BlockSpec((D, tn), lambda j: (0, j)),
                          out_shape=jax.ShapeDtypeStruct((D, N), F32),
                          compiler_params=_cparams(("parallel",)))(col, row)


def _exchange(srcs, broadcast, name):
    n = len(srcs)
    out_shape = [jax.ShapeDtypeStruct((N_DEV,) + (s.shape if broadcast else s.shape[1:]), s.dtype) for s in srcs]

    def body(*refs):
        src_refs, out_refs = refs[:n], refs[n:2 * n]
        send_sems, recv_sems, local_sems = refs[2 * n:]
        x, y, c = lax.axis_index("x"), lax.axis_index("y"), lax.axis_index("c")
        me = 4 * x + 2 * y + c

        def block(i, j):
            return src_refs[i] if broadcast else src_refs[i].at[j]

        def remote(i, d, src_slot, dst_slot):
            px, py, pc = x ^ (d >> 2), y ^ ((d >> 1) & 1), c ^ (d & 1)
            return pltpu.make_async_remote_copy(
                src_ref=block(i, src_slot), dst_ref=out_refs[i].at[dst_slot], send_sem=send_sems.at[i, d],
                recv_sem=recv_sems.at[i, d], device_id=(px, py, pc), device_id_type=_MESH)

        local = [pltpu.make_async_copy(block(i, me), out_refs[i].at[me], local_sems.at[i]) for i in range(n)]
        for cp in local:
            cp.start()
        sends = [remote(i, d, me ^ d, me) for d in range(1, N_DEV) for i in range(n)]
        for cp in sends:
            cp.start()
        for d in range(1, N_DEV):
            for i in range(n):
                remote(i, d, me, me ^ d).wait_recv()
        for cp in sends:
            cp.wait_send()
        for cp in local:
            cp.wait()

    any_spec = pl.BlockSpec(memory_space=pl.ANY)
    return pl.pallas_call(
        body, name=name, out_shape=out_shape, in_specs=[any_spec] * n, out_specs=[any_spec] * n,
        scratch_shapes=[pltpu.SemaphoreType.DMA((n, N_DEV)), pltpu.SemaphoreType.DMA((n, N_DEV)), pltpu.SemaphoreType.DMA((n,))],
        compiler_params=pltpu.CompilerParams(has_side_effects=True),
    )(*srcs)


def _sum_adam(parts, w, m, v, name):
    _, R, C = parts.shape
    tm = 256 if R % 256 == 0 else R
    c1 = 1.0 / (1.0 - ADAM_B1 ** ADAM_STEP)
    c2 = 1.0 / (1.0 - ADAM_B2 ** ADAM_STEP)

    def body(p_ref, w_ref, m_ref, v_ref, g_ref, d_ref, nm_ref, nv_ref):
        g = p_ref[0].astype(F32)
        for j in range(1, N_DEV):
            g = g + p_ref[j].astype(F32)
        nm = ADAM_B1 * m_ref[...] + (1.0 - ADAM_B1) * g
        nv = ADAM_B2 * v_ref[...] + (1.0 - ADAM_B2) * (g * g)
        g_ref[...] = g
        nm_ref[...] = nm
        nv_ref[...] = nv
        d_ref[...] = -ADAM_LR * ((nm * c1) / (jnp.sqrt(nv * c2) + ADAM_EPS) + ADAM_WD * w_ref[...])

    row = _rows(tm, C)
    shp = jax.ShapeDtypeStruct((R, C), F32)
    return pl.pallas_call(body, name=name, grid=(R // tm,),
                          in_specs=[pl.BlockSpec((N_DEV, tm, C), lambda i: (0, i, 0)), row, row, row],
                          out_specs=[row] * 4, out_shape=[shp] * 4,
                          compiler_params=_cparams(("parallel",)))(parts, w, m, v)


PACK_ALIGN = 16 * LANES
PACK_ROWS = 512 * LANES

SHARDED = (("w_ada", 1), ("w_in", 1), ("w2", 1), ("a2", 1), ("g2", 1), ("w_att_out", 1), ("w_rwkv_out", 0),
           ("w_o", 0), ("w_up", 1), ("conv_w", 1), ("w_down", 0))
REPLICATED = ("b_ada", "norm1_w", "b_gate", "mu_shift", "w0", "a0", "k_k", "k_a", "r_k", "lnx_w", "lnx_b",
              "norm2_w", "conv_b", "norm_f_w")
WEIGHTS = ("w_ada", "b_ada", "norm1_w", "w_in", "b_gate", "mu_shift", "w0", "w2", "a0", "a2", "g2", "k_k", "k_a", "r_k",
           "lnx_w", "lnx_b", "w_att_out", "w_rwkv_out", "w_o", "norm2_w", "w_up", "conv_w", "conv_b", "w_down", "norm_f_w")


def _pack(arrays):
    flat, layout, off = [], [], 0
    for i, a in enumerate(arrays):
        n = a.size
        pad = (-n) % PACK_ALIGN if i + 1 < len(arrays) else (-(off + n)) % PACK_ROWS
        flat.append(a.reshape(-1))
        if pad:
            flat.append(jnp.zeros((pad,), a.dtype))
        layout.append((off, n, a.shape))
        off += n + pad
    return jnp.concatenate(flat).reshape(-1, LANES), layout


def _unpack(buf, layout):
    flat = buf.reshape(-1)
    return [flat[off:off + n].reshape(shape) for off, n, shape in layout]


def _regroup(t, d):
    S = t.shape[0]
    return t.reshape((S // d, d) + t.shape[1:]).swapaxes(0, 1).reshape(t.shape)


def _ungroup(t, d):
    S = t.shape[0]
    return t.reshape((d, S // d) + t.shape[1:]).swapaxes(0, 1).reshape(t.shape)


def _att_in(P):
    S = P.shape[0]
    z = P[:, C_ATT:C_ATT + ATT_IN].astype(BF16).reshape(S, 3, 3, ATT_HEADS, HEAD)
    per_group = [_regroup(z[:, g], d) for g, (_, d) in enumerate(ATT_PATTERNS)]
    z = jnp.stack(per_group, axis=0)
    z = z.transpose(2, 0, 3, 1, 4).reshape(3, 3 * ATT_HEADS, S, HEAD)
    return z[0], z[1], z[2]


def _att_natural(t):
    S = t.shape[1]
    t = t.reshape(3, ATT_HEADS, S, HEAD)
    return jnp.stack([_ungroup(t[g].swapaxes(0, 1), d).swapaxes(0, 1) for g, (_, d) in enumerate(ATT_PATTERNS)], axis=0)


def _att_regrouped(t):
    S = t.shape[2]
    out = jnp.stack([_regroup(t[g].swapaxes(0, 1), d).swapaxes(0, 1) for g, (_, d) in enumerate(ATT_PATTERNS)], axis=0)
    return out.reshape(3 * ATT_HEADS, S, HEAD)


def _att_grad_cols(dq, dk, dv):
    S = dq.shape[1]
    z = jnp.stack([dq, dk, dv], axis=0).astype(BF16).reshape(3, 3, ATT_HEADS, S, HEAD)
    per_group = [_ungroup(z[:, g].transpose(2, 0, 1, 3), d) for g, (_, d) in enumerate(ATT_PATTERNS)]
    return jnp.stack(per_group, axis=1).reshape(S, ATT_IN)


def _pad_w_in(w_in):
    rkv = w_in[:, ATT_IN:ATT_IN + 3 * D]
    lora = w_in[:, ATT_IN + 3 * D:ATT_IN + RWKV_IN]
    gates = w_in[:, ATT_IN + RWKV_IN:]
    att = w_in[:, :ATT_IN]
    lw, la, lg = lora[:, :LORA_W], lora[:, LORA_W:LORA_W + LORA_A], lora[:, LORA_W + LORA_A:]
    zeros = jnp.zeros((w_in.shape[0], LORA_PAD - LANES - LORA_G), w_in.dtype)
    return jnp.concatenate([rkv, gates, att, lw, la, lg, zeros], axis=1)


def _unpad_w_in(g):
    att = g[:, C_ATT:C_ATT + ATT_IN]
    rkv = g[:, C_R:C_R + 3 * D]
    lora = jnp.concatenate([g[:, C_LORA:C_LORA + LORA_W + LORA_A], g[:, C_LORA + LANES:C_LORA + LANES + LORA_G]], axis=1)
    gates = g[:, C_GA:C_GA + 2 * D]
    return jnp.concatenate([att, rkv, lora, gates], axis=1)


def _pad_mu(mu):
    lo = mu[:, 3 * D:]
    mu_l = jnp.concatenate([lo[:, :LORA_W + LORA_A], lo[:, LORA_W + LORA_A:], jnp.zeros((1, LORA_PAD - LANES - LORA_G), mu.dtype)], axis=1)
    return mu[:, :D], mu[:, D:2 * D], mu[:, 2 * D:3 * D], mu_l


def _local_step(x, c, W, target):
    S = x.shape[0]
    G = {}
    c8 = jnp.pad(c, ((0, SUBLANES - 1), (0, 0)))
    ada = _ada_fwd(c8, W["w_ada"], W["b_ada"])[0:1]
    sh1, sc1, gt1, sh2, sc2, gt2 = [ada[:, i * D:(i + 1) * D] for i in range(6)]
    h1, rstd1 = _norm_fwd(x, None, None, W["norm1_w"], sc1, sh1, "norm1_fwd")
    w_in_p = _pad_w_in(W["w_in"])
    P = _mm(h1, w_in_p, "nn", F32, "proj_in")

    q, k, v = _att_in(P)
    o_g, l_g = _att_fwd(q, k, v)
    o_nat = _att_natural(o_g).reshape(3, -1, LANES)
    l_nat = _att_natural(l_g).reshape(3, -1, LANES)
    att_hm = _att_combine_fwd(o_nat, l_nat)
    att = att_hm.reshape(ATT_HEADS, S, HEAD).swapaxes(0, 1).reshape(S, ATT_W).astype(BF16)
    y_att = _mm(att, W["w_att_out"], "nn", F32, "att_out")

    mu_r, mu_k, mu_v, mu_l = _pad_mu(W["mu_shift"])
    g2p = jnp.pad(W["g2"], ((0, G_PAD - LORA_G), (0, 0)))
    prep_params = [mu_r, mu_k, mu_v, mu_l, W["w0"], W["a0"], W["k_k"], W["k_a"], W["w2"], W["a2"], g2p]
    r_, dec, kmod, v_, aa, bb, gg = _rwkv_prep(P, prep_params)
    y_scan, states = _scan_fwd(r_, dec, kmod, v_, aa, bb)
    r_k = W["r_k"].reshape(1, D)
    rw = _rwkv_post(y_scan, r_, kmod, v_, gg, W["lnx_w"], W["lnx_b"], r_k)
    y_rwkv = _mm(rw, W["w_rwkv_out"], "nn", F32, "rwkv_out")

    bga, bgr = W["b_gate"][:, :D], W["b_gate"][:, D:]
    mix = _gate_fwd(P, bga, bgr, y_att, y_rwkv)
    mo = _mm(mix, W["w_o"], "nn", F32, "mix_out")
    x2, h2, rstd2 = _norm_fwd(x, mo, gt1, W["norm2_w"], sc2, sh2, "norm2_fwd")
    u = _mm(h2, W["w_up"], "nn", F32, "ffn_up")
    conv_w8 = jnp.pad(W["conv_w"], ((0, SUBLANES - 3), (0, 0)))
    act = _conv_fwd(u, conv_w8, W["conv_b"])
    f = _mm(act, W["w_down"], "nn", F32, "ffn_down")
    loss_blk, dx3, df, dgt2, G["norm_f_w"] = _final(x2, f, gt2, W["norm_f_w"], target)
    loss = loss_blk[0, 0]

    dact = _mm(df, W["w_down"], "nt", BF16, "ffn_down_dx")
    G["w_down"] = _mm(act, df, "tn", F32, "ffn_down_dw")
    dug, duv, dwg, dwv, dbg, dbv = _conv_bwd_a(dact, u, conv_w8, W["conv_b"])
    G["conv_w"] = jnp.concatenate([dwg[0:3], dwv[0:3]], axis=1)
    G["conv_b"] = jnp.concatenate([dbg, dbv], axis=1)
    du = jnp.concatenate([_conv_bwd_b(dug, conv_w8[:, :D_FF]), _conv_bwd_b(duv, conv_w8[:, D_FF:])], axis=1)
    dh2 = _mm(du, W["w_up"], "nt", F32, "ffn_up_dx")
    G["w_up"] = _mm(h2, du, "tn", F32, "ffn_up_dw")
    dx2, dsh2, dsc2, G["norm2_w"], dmo, dgt1 = _norm_bwd(dh2, x2, rstd2, W["norm2_w"], sc2, dx3, mo, gt1, "norm2_bwd")
    dmix = _mm(dmo, W["w_o"], "nt", F32, "mix_out_dx")
    G["w_o"] = _mm(mix, dmo, "tn", F32, "mix_out_dw")
    dy_att, dy_rwkv, dpga, dpgr, dbga, dbgr = _gate_bwd(dmix, P, bga, bgr, y_att, y_rwkv)
    G["b_gate"] = jnp.concatenate([dbga, dbgr], axis=1)

    datt = _mm(dy_att, W["w_att_out"], "nt", F32, "att_out_dx")
    G["w_att_out"] = _mm(att, dy_att, "tn", F32, "att_out_dw")
    datt_hm = datt.reshape(S, ATT_HEADS, HEAD).swapaxes(0, 1).reshape(-1, LANES)
    do_nat, dl_nat = _att_combine_bwd(datt_hm, o_nat, l_nat)
    do_g = _att_regrouped(do_nat.reshape(3, ATT_HEADS, S, HEAD))
    dl_g = _att_regrouped(dl_nat.reshape(3, ATT_HEADS, S, HEAD))
    dq, dk, dv = _att_bwd(q, k, v, o_g, l_g, do_g, dl_g)
    dp_att = _att_grad_cols(dq, dk, dv)

    drw = _mm(dy_rwkv, W["w_rwkv_out"], "nt", F32, "rwkv_out_dx")
    G["w_rwkv_out"] = _mm(rw, dy_rwkv, "tn", F32, "rwkv_out_dw")
    dy_scan, dr1, dk1, dv1, dgg, G["lnx_w"], G["lnx_b"], drk = _rwkv_post_bwd(drw, y_scan, r_, kmod, v_, gg, W["lnx_w"], W["lnx_b"], r_k)
    G["r_k"] = drk.reshape(W["r_k"].shape)
    dr2, ddec, dk2, dv2, daa, dbb = _scan_bwd(r_, dec, kmod, v_, aa, bb, states, dy_scan)
    pb = _rwkv_prep_bwd(P, prep_params, [dr2, ddec, dk2, dv2, daa, dbb, dgg], [dr1, None, dk1, dv1, None, None, None])
    dz, dzp, dpar = pb[0:4], pb[4:8], pb[8:]
    dp_rkv = [_shift_add(dz[i], dzp[i]) for i in range(3)]
    dp_lora = _shift_add(dz[3], dzp[3])
    dmu_r, dmu_k, dmu_v, dmu_l, G["w0"], G["a0"], G["k_k"], G["k_a"], G["w2"], G["a2"], dg2p = dpar
    G["g2"] = dg2p[0:LORA_G]
    G["mu_shift"] = jnp.concatenate([dmu_r, dmu_k, dmu_v, dmu_l[:, :LORA_W + LORA_A], dmu_l[:, LANES:LANES + LORA_G]], axis=1)

    dP = jnp.concatenate(dp_rkv + [dpga, dpgr, dp_att, dp_lora], axis=1)
    dh1 = _mm(dP, w_in_p, "nt", F32, "proj_in_dx")
    G["w_in"] = _unpad_w_in(_mm(h1, dP, "tn", F32, "proj_in_dw"))
    grad_x, dsh1, dsc1, G["norm1_w"] = _norm_bwd(dh1, x, rstd1, W["norm1_w"], sc1, dx2, None, None, "norm1_bwd")
    dada = jnp.concatenate([dsh1, dsc1, dgt1, dsh2, dsc2, dgt2], axis=1)
    G["b_ada"] = dada
    G["w_ada"] = _outer(c.reshape(D, 1), dada)
    return loss, grad_x, G


def _full_weight(gathered, axis):
    _, rows, cols = gathered.shape
    if axis == 0:
        return gathered.reshape(N_DEV * rows, cols)
    return gathered.transpose(1, 0, 2).reshape(rows, N_DEV * cols)


def _owner_blocks(g, axis):
    rows, cols = g.shape
    g = g.astype(BF16)
    if axis == 0:
        return g.reshape(N_DEV, rows // N_DEV, cols)
    return g.reshape(rows, N_DEV, cols // N_DEV).transpose(1, 0, 2)


def kernel(x, c, w_ada, b_ada, norm1_w, w_in, b_gate, mu_shift, w0, w2, a0, a2, g2, k_k, k_a, r_k, lnx_w, lnx_b, w_att_out, w_rwkv_out, w_o, norm2_w, w_up, conv_w, conv_b, w_down, norm_f_w, loss_target, m_w_ada, m_b_ada, m_norm1_w, m_w_in, m_b_gate, m_mu_shift, m_w0, m_w2, m_a0, m_a2, m_g2, m_k_k, m_k_a, m_r_k, m_lnx_w, m_lnx_b, m_w_att_out, m_w_rwkv_out, m_w_o, m_norm2_w, m_w_up, m_conv_w, m_conv_b, m_w_down, m_norm_f_w, v_w_ada, v_b_ada, v_norm1_w, v_w_in, v_b_gate, v_mu_shift, v_w0, v_w2, v_a0, v_a2, v_g2, v_k_k, v_k_a, v_r_k, v_lnx_w, v_lnx_b, v_w_att_out, v_w_rwkv_out, v_w_o, v_norm2_w, v_w_up, v_conv_w, v_conv_b, v_w_down, v_norm_f_w):
    env = dict(locals())
    w_shard = {n: env[n] for n in WEIGHTS}
    m_shard = {n: env["m_" + n] for n in WEIGHTS}
    v_shard = {n: env["v_" + n] for n in WEIGHTS}

    gathered = _exchange([w_shard[n][0].astype(BF16) for n, _ in SHARDED], True, "gather_weights")
    W = {n: _full_weight(g, axis) for (n, axis), g in zip(SHARDED, gathered)}
    for n in REPLICATED:
        W[n] = w_shard[n].reshape(1, -1) if n != "r_k" else w_shard[n][0]

    loss, grad_x, G = _local_step(x[0], c, W, loss_target[0])
    loss = lax.psum(loss, ("x", "y", "c"))

    parts = _exchange([_owner_blocks(G[n], axis) for n, axis in SHARDED], False, "scatter_grads")
    out = {}
    for (n, _), p in zip(SHARDED, parts):
        res = _sum_adam(p, w_shard[n][0], m_shard[n][0], v_shard[n][0], "adam_" + n)
        for kind, a in zip(("grad", "delta", "new_m", "new_v"), res):
            out[kind, n] = a[None]

    small, slayout = _pack([G[n].reshape(-1) for n in REPLICATED])
    sparts, = _exchange([small], True, "gather_small_grads")
    sw, _ = _pack([w_shard[n].reshape(-1) for n in REPLICATED])
    sm, _ = _pack([m_shard[n].reshape(-1) for n in REPLICATED])
    sv, _ = _pack([v_shard[n].reshape(-1) for n in REPLICATED])
    res = _sum_adam(sparts, sw, sm, sv, "adam_replicated")
    for kind, buf in zip(("grad", "delta", "new_m", "new_v"), res):
        for n, a in zip(REPLICATED, _unpack(buf, slayout)):
            out[kind, n] = a.reshape(w_shard[n].shape)

    return (loss, grad_x[None], *[out[kind, n] for kind in ("grad", "delta", "new_m", "new_v") for n in WEIGHTS])
```

```python
import functools
import math

import jax
import jax.numpy as jnp
from jax import lax
from jax.experimental import pallas as pl
from jax.experimental.pallas import tpu as pltpu

F32 = jnp.float32
BF16 = jnp.bfloat16

D = 1024
HEAD = 64
ATT_PATTERNS = ((128, 1), (512, 4), (2048, 16))
ATT_HEADS = 8
ATT_W = ATT_HEADS * HEAD
ATT_IN = 3 * 3 * ATT_W
QBLK = 128
N_HEADS = D // HEAD
LORA_W, LORA_A, LORA_G = 64, 64, 160
RWKV_IN = 3 * D + LORA_W + LORA_A + LORA_G
N_IN = ATT_IN + RWKV_IN + 2 * D
D_FF = 2816
RMS_EPS = 1e-6
GN_EPS = 64e-5
N_DEV = 8
LANES = 128
SUBLANES = 8

C_R, C_K, C_V, C_GA, C_GR = 0, 1024, 2048, 3072, 4096
C_ATT = 5120
C_LORA = C_ATT + ATT_IN
LORA_PAD = 512
G_PAD = 256
N_PAD = C_LORA + LORA_PAD

ADAM_LR, ADAM_B1, ADAM_B2, ADAM_EPS, ADAM_WD, ADAM_STEP = 0.001, 0.9, 0.999, 1e-08, 0.01, 10

SCAN_TB = 128
VMEM_LIMIT = 56 * 1024 * 1024

_MESH = pl.DeviceIdType.MESH


def _cparams(sem):
    return pltpu.CompilerParams(dimension_semantics=sem, vmem_limit_bytes=VMEM_LIMIT)


def _tile(dim, pref):
    if dim <= pref:
        return dim
    best = None
    for t in range(LANES, pref + 1, LANES):
        if dim % t == 0:
            best = t
    assert best is not None, dim
    return best


MM_TILES = {"nn": (1024, 1408, 1408), "nt": (512, 2048, 1408), "tn": (1408, 1408, 1024)}


def _mm(a, b, mode, out_dtype, name):
    if mode == "nn":
        (M, K), (K2, N) = a.shape, b.shape
    elif mode == "nt":
        (M, K), (N, K2) = a.shape, b.shape
    else:
        (K, M), (K2, N) = a.shape, b.shape
    assert K == K2, (a.shape, b.shape, mode)
    tm, tn, tk = (_tile(dim, pref) for dim, pref in zip((M, N, K), MM_TILES[mode]))
    nk = K // tk
    dims = {"nn": (((1,), (0,)), ((), ())), "nt": (((1,), (1,)), ((), ())), "tn": (((0,), (0,)), ((), ()))}[mode]

    def body(a_ref, b_ref, o_ref, acc_ref):
        k = pl.program_id(2)
        part = lax.dot_general(a_ref[...].astype(BF16), b_ref[...].astype(BF16), dims,
                               preferred_element_type=F32)
        if nk == 1:
            o_ref[...] = part.astype(o_ref.dtype)
            return

        @pl.when(k == 0)
        def _():
            acc_ref[...] = part

        @pl.when(jnp.logical_and(k > 0, k < nk - 1))
        def _():
            acc_ref[...] += part

        @pl.when(k == nk - 1)
        def _():
            o_ref[...] = (acc_ref[...] + part).astype(o_ref.dtype)

    a_spec = pl.BlockSpec((tk, tm), lambda i, j, k: (k, i)) if mode == "tn" else pl.BlockSpec((tm, tk), lambda i, j, k: (i, k))
    b_spec = pl.BlockSpec((tn, tk), lambda i, j, k: (j, k)) if mode == "nt" else pl.BlockSpec((tk, tn), lambda i, j, k: (k, j))
    return pl.pallas_call(
        body, name=name, grid=(M // tm, N // tn, nk),
        in_specs=[a_spec, b_spec],
        out_specs=pl.BlockSpec((tm, tn), lambda i, j, k: (i, j)),
        out_shape=jax.ShapeDtypeStruct((M, N), out_dtype),
        scratch_shapes=[pltpu.VMEM((tm, tn) if nk > 1 else (SUBLANES, LANES), F32)],
        compiler_params=_cparams(("parallel", "parallel", "arbitrary")),
    )(a, b)


def _rows(tm, w, col=0):
    return pl.BlockSpec((tm, w), lambda i: (i, col))


def _full(shape):
    return pl.BlockSpec(shape, lambda i: (0,) * len(shape))


def _prev8(tm, w, col=0):
    return pl.BlockSpec((SUBLANES, w), lambda i: (jnp.maximum(i * (tm // SUBLANES) - 1, 0), col))


def _next8(tm, w, n_rows, col=0):
    last = n_rows // SUBLANES - 1
    return pl.BlockSpec((SUBLANES, w), lambda i: (jnp.minimum((i + 1) * (tm // SUBLANES), last), col))


def _shift_down(x, halo, k, first):
    rolled = pltpu.roll(x, k, 0)
    row = lax.broadcasted_iota(jnp.int32, x.shape, 0)
    out = rolled
    for j in range(k):
        h = jnp.where(first, 0.0, halo[SUBLANES - k + j:SUBLANES - k + j + 1, :])
        out = jnp.where(row == j, h, out)
    return out


def _shift_up(x, halo, k, last):
    n = x.shape[0]
    rolled = pltpu.roll(x, n - k, 0)
    row = lax.broadcasted_iota(jnp.int32, x.shape, 0)
    out = rolled
    for j in range(k):
        h = jnp.where(last, 0.0, halo[j:j + 1, :])
        out = jnp.where(row == n - k + j, h, out)
    return out


def _acc(ref, val, first):
    @pl.when(first)
    def _():
        ref[...] = val

    @pl.when(jnp.logical_not(first))
    def _():
        ref[...] += val


def _colsum(x):
    return jnp.sum(x, axis=0, keepdims=True)


def _norm_fwd(x, mo, gt, nw, sc, sh, name, tm=256):
    S = x.shape[0]
    has_res = mo is not None

    def body(*refs):
        if has_res:
            x_ref, mo_ref, gt_ref, nw_ref, sc_ref, sh_ref, x2_ref, h_ref, rs_ref = refs
            x2 = x_ref[...] + gt_ref[...] * mo_ref[...]
            x2_ref[...] = x2
        else:
            x_ref, nw_ref, sc_ref, sh_ref, h_ref, rs_ref = refs
            x2 = x_ref[...]
        rstd = lax.rsqrt(jnp.mean(x2 * x2, axis=-1, keepdims=True) + RMS_EPS)
        rs_ref[...] = rstd
        h_ref[...] = ((x2 * rstd * nw_ref[...]) * (1.0 + sc_ref[...]) + sh_ref[...]).astype(BF16)

    vec = _full((1, D))
    ins = [x, mo, gt, nw, sc, sh] if has_res else [x, nw, sc, sh]
    in_specs = [_rows(tm, D), _rows(tm, D), vec, vec, vec, vec] if has_res else [_rows(tm, D), vec, vec, vec]
    outs = [jax.ShapeDtypeStruct((S, D), BF16), jax.ShapeDtypeStruct((S, 1), F32)]
    out_specs = [_rows(tm, D), _rows(tm, 1)]
    if has_res:
        outs = [jax.ShapeDtypeStruct((S, D), F32)] + outs
        out_specs = [_rows(tm, D)] + out_specs
    return pl.pallas_call(body, name=name, grid=(S // tm,), in_specs=in_specs, out_specs=out_specs,
                          out_shape=outs, compiler_params=_cparams(("parallel",)))(*ins)


def _norm_bwd(dh, xin, rstd, nw, sc, dres, mo, gt, name, tm=256):
    S = xin.shape[0]
    has_res = mo is not None

    def body(*refs):
        if has_res:
            dh_ref, x_ref, rs_ref, nw_ref, sc_ref, dres_ref, mo_ref, gt_ref, dx_ref, dsh_ref, dsc_ref, dnw_ref, dmo_ref, dgt_ref = refs
        else:
            dh_ref, x_ref, rs_ref, nw_ref, sc_ref, dres_ref, dx_ref, dsh_ref, dsc_ref, dnw_ref = refs
        first = pl.program_id(0) == 0
        dh = dh_ref[...]
        rstd = rs_ref[...]
        n = x_ref[...] * rstd
        w = nw_ref[...]
        _acc(dsh_ref, _colsum(dh), first)
        _acc(dsc_ref, _colsum(dh * (n * w)), first)
        dnw = dh * (1.0 + sc_ref[...])
        _acc(dnw_ref, _colsum(dnw * n), first)
        dn = dnw * w
        dx = dres_ref[...] + rstd * (dn - n * jnp.mean(dn * n, axis=-1, keepdims=True))
        dx_ref[...] = dx
        if has_res:
            dmo_ref[...] = (dx * gt_ref[...]).astype(BF16)
            _acc(dgt_ref, _colsum(dx * mo_ref[...]), first)

    vec = _full((1, D))
    vshape = jax.ShapeDtypeStruct((1, D), F32)
    ins = [dh, xin, rstd, nw, sc, dres] + ([mo, gt] if has_res else [])
    in_specs = [_rows(tm, D), _rows(tm, D), _rows(tm, 1), vec, vec, _rows(tm, D)] + ([_rows(tm, D), vec] if has_res else [])
    outs = [jax.ShapeDtypeStruct((S, D), F32), vshape, vshape, vshape]
    out_specs = [_rows(tm, D), vec, vec, vec]
    if has_res:
        outs += [jax.ShapeDtypeStruct((S, D), BF16), vshape]
        out_specs += [_rows(tm, D), vec]
    return pl.pallas_call(body, name=name, grid=(S // tm,), in_specs=in_specs, out_specs=out_specs,
                          out_shape=outs, compiler_params=_cparams(("arbitrary",)))(*ins)


def _final(x2, f, gt2, nfw, target, tm=256):
    S = x2.shape[0]

    def body(x2_ref, f_ref, gt_ref, w_ref, t_ref, loss_ref, dx_ref, df_ref, dgt_ref, dw_ref):
        first = pl.program_id(0) == 0
        f = f_ref[...]
        gt = gt_ref[...]
        w = w_ref[...]
        x3 = x2_ref[...] + gt * f
        rstd = lax.rsqrt(jnp.mean(x3 * x3, axis=-1, keepdims=True) + RMS_EPS)
        n = x3 * rstd
        e = n * w - t_ref[...]
        part = 0.5 * jnp.sum(jnp.mean(e * e, axis=-1, keepdims=True), axis=0, keepdims=True)
        _acc(loss_ref, jnp.broadcast_to(part, (SUBLANES, LANES)), first)
        dy = e * (1.0 / D)
        _acc(dw_ref, _colsum(dy * n), first)
        dn = dy * w
        dx = rstd * (dn - n * jnp.mean(dn * n, axis=-1, keepdims=True))
        dx_ref[...] = dx
        df_ref[...] = (dx * gt).astype(BF16)
        _acc(dgt_ref, _colsum(dx * f), first)

    vec = _full((1, D))
    vshape = jax.ShapeDtypeStruct((1, D), F32)
    return pl.pallas_call(
        body, name="final_loss", grid=(S // tm,),
        in_specs=[_rows(tm, D), _rows(tm, D), vec, vec, _rows(tm, D)],
        out_specs=[_full((SUBLANES, LANES)), _rows(tm, D), _rows(tm, D), vec, vec],
        out_shape=[jax.ShapeDtypeStruct((SUBLANES, LANES), F32), jax.ShapeDtypeStruct((S, D), F32),
                   jax.ShapeDtypeStruct((S, D), BF16), vshape, vshape],
        compiler_params=_cparams(("arbitrary",)))(x2, f, gt2, nfw, target)


def _gate_fwd(P, bga, bgr, y_att, y_rwkv, tm=256):
    S = P.shape[0]

    def body(pa_ref, pr_ref, ba_ref, br_ref, ya_ref, yr_ref, mix_ref):
        ga = jax.nn.sigmoid(pa_ref[...] + ba_ref[...])
        gr = jax.nn.sigmoid(pr_ref[...] + br_ref[...])
        mix_ref[...] = (ga * ya_ref[...] + gr * yr_ref[...]).astype(BF16)

    vec = _full((1, D))
    return pl.pallas_call(
        body, name="gate_fwd", grid=(S // tm,),
        in_specs=[_rows(tm, D, C_GA // D), _rows(tm, D, C_GR // D), vec, vec, _rows(tm, D), _rows(tm, D)],
        out_specs=_rows(tm, D), out_shape=jax.ShapeDtypeStruct((S, D), BF16),
        compiler_params=_cparams(("parallel",)))(P, P, bga, bgr, y_att, y_rwkv)


def _gate_bwd(dmix, P, bga, bgr, y_att, y_rwkv, tm=256):
    S = P.shape[0]

    def body(dm_ref, pa_ref, pr_ref, ba_ref, br_ref, ya_ref, yr_ref, dya_ref, dyr_ref, dpa_ref, dpr_ref, dba_ref, dbr_ref):
        first = pl.program_id(0) == 0
        dm = dm_ref[...]
        ga = jax.nn.sigmoid(pa_ref[...] + ba_ref[...])
        gr = jax.nn.sigmoid(pr_ref[...] + br_ref[...])
        dya_ref[...] = (dm * ga).astype(BF16)
        dyr_ref[...] = (dm * gr).astype(BF16)
        dpa = dm * ya_ref[...] * ga * (1.0 - ga)
        dpr = dm * yr_ref[...] * gr * (1.0 - gr)
        dpa_ref[...] = dpa.astype(BF16)
        dpr_ref[...] = dpr.astype(BF16)
        _acc(dba_ref, _colsum(dpa), first)
        _acc(dbr_ref, _colsum(dpr), first)

    vec = _full((1, D))
    row = _rows(tm, D)
    rshape = jax.ShapeDtypeStruct((S, D), BF16)
    vshape = jax.ShapeDtypeStruct((1, D), F32)
    return pl.pallas_call(
        body, name="gate_bwd", grid=(S // tm,),
        in_specs=[row, _rows(tm, D, C_GA // D), _rows(tm, D, C_GR // D), vec, vec, row, row],
        out_specs=[row, row, row, row, vec, vec],
        out_shape=[rshape, rshape, rshape, rshape, vshape, vshape],
        compiler_params=_cparams(("arbitrary",)))(dmix, P, P, bga, bgr, y_att, y_rwkv)


def _conv_fwd(u, conv_w8, conv_b, tm=256, tn=256):
    S = u.shape[0]
    nj = D_FF // tn

    def conv(u_ref, h_ref, w_ref, b_ref, first):
        u = u_ref[...]
        h = h_ref[...]
        w = w_ref[...]
        return b_ref[...] + w[0:1] * _shift_down(u, h, 2, first) + w[1:2] * _shift_down(u, h, 1, first) + w[2:3] * u

    def body(ug_ref, hg_ref, uv_ref, hv_ref, wg_ref, wv_ref, bg_ref, bv_ref, act_ref):
        first = pl.program_id(0) == 0
        g = conv(ug_ref, hg_ref, wg_ref, bg_ref, first)
        v = conv(uv_ref, hv_ref, wv_ref, bv_ref, first)
        act_ref[...] = (g * jax.nn.sigmoid(g) * v).astype(BF16)

    blk = lambda off: pl.BlockSpec((tm, tn), lambda i, j: (i, j + off))
    halo = lambda off: pl.BlockSpec((SUBLANES, tn), lambda i, j: (jnp.maximum(i * (tm // SUBLANES) - 1, 0), j + off))
    wsp = lambda off: pl.BlockSpec((SUBLANES, tn), lambda i, j: (0, j + off))
    bsp = lambda off: pl.BlockSpec((1, tn), lambda i, j: (0, j + off))
    return pl.pallas_call(
        body, name="conv_fwd", grid=(S // tm, nj),
        in_specs=[blk(0), halo(0), blk(nj), halo(nj), wsp(0), wsp(nj), bsp(0), bsp(nj)],
        out_specs=pl.BlockSpec((tm, tn), lambda i, j: (i, j)),
        out_shape=jax.ShapeDtypeStruct((S, D_FF), BF16),
        compiler_params=_cparams(("parallel", "parallel")))(u, u, u, u, conv_w8, conv_w8, conv_b, conv_b)


def _conv_bwd_a(dact, u, conv_w8, conv_b, tm=256, tn=256):
    S = u.shape[0]
    nj = D_FF // tn

    def half(u_ref, h_ref, w_ref, b_ref, first):
        u = u_ref[...]
        h = h_ref[...]
        w = w_ref[...]
        u2, u1 = _shift_down(u, h, 2, first), _shift_down(u, h, 1, first)
        return b_ref[...] + w[0:1] * u2 + w[1:2] * u1 + w[2:3] * u, (u2, u1, u)

    def wgrad(d, taps):
        z = jnp.zeros((SUBLANES - 3, d.shape[1]), F32)
        return jnp.concatenate([_colsum(d * taps[0]), _colsum(d * taps[1]), _colsum(d * taps[2]), z], axis=0)

    def body(da_ref, ug_ref, hg_ref, uv_ref, hv_ref, wg_ref, wv_ref, bg_ref, bv_ref,
             d_ref, dwg_ref, dwv_ref, dbg_ref, dbv_ref):
        first = pl.program_id(1) == 0
        g, tg = half(ug_ref, hg_ref, wg_ref, bg_ref, first)
        v, tv = half(uv_ref, hv_ref, wv_ref, bv_ref, first)
        da = da_ref[...].astype(F32)
        sg = jax.nn.sigmoid(g)
        dg = da * v * (sg * (1.0 + g * (1.0 - sg)))
        dv = da * (g * sg)
        d_ref[0] = dg
        d_ref[1] = dv
        _acc(dwg_ref, wgrad(dg, tg), first)
        _acc(dwv_ref, wgrad(dv, tv), first)
        _acc(dbg_ref, _colsum(dg), first)
        _acc(dbv_ref, _colsum(dv), first)

    blk = lambda off: pl.BlockSpec((tm, tn), lambda j, i: (i, j + off))
    halo = lambda off: pl.BlockSpec((SUBLANES, tn), lambda j, i: (jnp.maximum(i * (tm // SUBLANES) - 1, 0), j + off))
    wsp = lambda off: pl.BlockSpec((SUBLANES, tn), lambda j, i: (0, j + off))
    bsp = lambda off: pl.BlockSpec((1, tn), lambda j, i: (0, j + off))
    f = jax.ShapeDtypeStruct
    outs = pl.pallas_call(
        body, name="conv_bwd_a", grid=(nj, S // tm),
        in_specs=[pl.BlockSpec((tm, tn), lambda j, i: (i, j)), blk(0), halo(0), blk(nj), halo(nj), wsp(0), wsp(nj), bsp(0), bsp(nj)],
        out_specs=[pl.BlockSpec((2, tm, tn), lambda j, i: (0, i, j)),
                   pl.BlockSpec((SUBLANES, tn), lambda j, i: (0, j)), pl.BlockSpec((SUBLANES, tn), lambda j, i: (0, j)),
                   pl.BlockSpec((1, tn), lambda j, i: (0, j)), pl.BlockSpec((1, tn), lambda j, i: (0, j))],
        out_shape=[f((2, S, D_FF), F32), f((SUBLANES, D_FF), F32), f((SUBLANES, D_FF), F32),
                   f((1, D_FF), F32), f((1, D_FF), F32)],
        compiler_params=_cparams(("parallel", "arbitrary")))(dact, u, u, u, u, conv_w8, conv_w8, conv_b, conv_b)
    return outs


def _conv_bwd_b(duc, conv_w8, tm=256, tn=256):
    _, S, W = duc.shape
    nj = W // tn
    n_rows = S // tm

    def body(d_ref, h_ref, w_ref, o_ref):
        last = pl.program_id(0) == n_rows - 1
        d = d_ref[...]
        h = h_ref[...]
        w = w_ref[...]
        o_ref[...] = (w[2:3] * d + w[1:2] * _shift_up(d, h, 1, last) + w[0:1] * _shift_up(d, h, 2, last)).astype(BF16)

    last_tile = S // SUBLANES - 1
    return pl.pallas_call(
        body, name="conv_bwd_b", grid=(n_rows, 2 * nj),
        in_specs=[pl.BlockSpec((None, tm, tn), lambda i, j: (j // nj, i, j % nj)),
                  pl.BlockSpec((None, SUBLANES, tn), lambda i, j: (j // nj, jnp.minimum((i + 1) * (tm // SUBLANES), last_tile), j % nj)),
                  pl.BlockSpec((SUBLANES, tn), lambda i, j: (0, j))],
        out_specs=pl.BlockSpec((tm, tn), lambda i, j: (i, j)),
        out_shape=jax.ShapeDtypeStruct((S, 2 * W), BF16),
        compiler_params=_cparams(("parallel", "parallel")))(duc, duc, conv_w8)


ATT_SCALE = HEAD ** -0.5
NEG = -1e30
ATT_PAIRS = ATT_HEADS // 2


def _att_rows(n, d, S):
    per = S // (QBLK * d)
    r, m = n // per, n % per
    cur = pl.ds(m * (QBLK * d) + r, QBLK, stride=d)
    prv = pl.ds(jnp.maximum(m - 1, 0) * (QBLK * d) + r, QBLK, stride=d)
    return cur, prv, m > 0


def _att_slab(g, j):
    return (C_ATT + g * 3 * ATT_W + j * ATT_W) // LANES


def _heads(x):
    return x[:, 0:HEAD], x[:, HEAD:2 * HEAD]


def _att_scores(q, kc, kp, has_prev):
    qi = lax.broadcasted_iota(jnp.int32, (QBLK, QBLK), 0)
    kj = lax.broadcasted_iota(jnp.int32, (QBLK, QBLK), 1)
    nt = (((1,), (1,)), ((), ()))
    s_c = lax.dot_general(q, kc, nt, preferred_element_type=F32) * ATT_SCALE
    s_p = lax.dot_general(q, kp, nt, preferred_element_type=F32) * ATT_SCALE
    s_c = jnp.where(kj <= qi, s_c, NEG)
    s_p = jnp.where(jnp.logical_and(kj >= qi, has_prev), s_p, NEG)
    return s_c, s_p


def _att_fwd(P, g):
    S = P.shape[0]
    d = ATT_PATTERNS[g][1]

    def body(q_ref, k_ref, v_ref, o_ref, l_ref):
        def blk(n, carry):
            cur, prv, has_prev = _att_rows(n, d, S)
            q2, kc2, kp2 = q_ref[cur, :].astype(BF16), k_ref[cur, :].astype(BF16), k_ref[prv, :].astype(BF16)
            vc2, vp2 = v_ref[cur, :].astype(BF16), v_ref[prv, :].astype(BF16)
            outs, lses = [], []
            for q, kc, kp, vc, vp in zip(_heads(q2), _heads(kc2), _heads(kp2), _heads(vc2), _heads(vp2)):
                s_c, s_p = _att_scores(q, kc, kp, has_prev)
                m = jnp.maximum(jnp.max(s_c, axis=1, keepdims=True), jnp.max(s_p, axis=1, keepdims=True))
                p_c = jnp.exp(s_c - m)
                p_p = jnp.exp(s_p - m)
                den = jnp.sum(p_c, axis=1, keepdims=True) + jnp.sum(p_p, axis=1, keepdims=True)
                num = (jnp.dot(p_c.astype(BF16), vc, preferred_element_type=F32)
                       + jnp.dot(p_p.astype(BF16), vp, preferred_element_type=F32))
                outs.append(num / den)
                lses.append(jnp.broadcast_to(m + jnp.log(den), (QBLK, HEAD)))
            o_ref[cur, :] = jnp.concatenate(outs, axis=1)
            l_ref[cur, :] = jnp.concatenate(lses, axis=1)
            return carry

        lax.fori_loop(0, S // QBLK, blk, 0)

    slab = lambda j: pl.BlockSpec((S, LANES), lambda i: (0, _att_slab(g, j) + i))
    out = pl.BlockSpec((S, LANES), lambda i: (0, i))
    shp = jax.ShapeDtypeStruct((S, ATT_W), F32)
    return pl.pallas_call(body, name=f"att_fwd_g{g}", grid=(ATT_PAIRS,), in_specs=[slab(0), slab(1), slab(2)],
                          out_specs=[out, out], out_shape=[shp, shp], compiler_params=_cparams(("parallel",)))(P, P, P)


def _att_bwd(P, o, l, do, dl, g):
    S = P.shape[0]
    d = ATT_PATTERNS[g][1]
    tn = (((0,), (0,)), ((), ()))
    nt = (((1,), (1,)), ((), ()))

    def body(q_ref, k_ref, v_ref, o_ref, l_ref, do_ref, dl_ref, dq_ref, dk_ref, dv_ref, dq_acc, dk_acc, dv_acc):
        dk_acc[...] = jnp.zeros_like(dk_acc)
        dv_acc[...] = jnp.zeros_like(dv_acc)

        def blk(n, carry):
            cur, prv, has_prev = _att_rows(n, d, S)
            q2, kc2, kp2 = q_ref[cur, :].astype(BF16), k_ref[cur, :].astype(BF16), k_ref[prv, :].astype(BF16)
            vc2, vp2 = v_ref[cur, :].astype(BF16), v_ref[prv, :].astype(BF16)
            do2 = do_ref[cur, :]
            dd2 = do2 * o_ref[cur, :] - dl_ref[cur, :]
            l2 = l_ref[cur, :]
            res = []
            for q, kc, kp, vc, vp, dob, dd, lse in zip(_heads(q2), _heads(kc2), _heads(kp2), _heads(vc2), _heads(vp2),
                                                     _heads(do2), _heads(dd2), _heads(l2)):
                s_c, s_p = _att_scores(q, kc, kp, has_prev)
                p_c = jnp.exp(s_c - lse[:, 0:1])
                p_p = jnp.exp(s_p - lse[:, 0:1])
                delta = jnp.sum(dd, axis=1, keepdims=True)
                dob16 = dob.astype(BF16)
                dp_c = lax.dot_general(dob16, vc, nt, preferred_element_type=F32)
                dp_p = lax.dot_general(dob16, vp, nt, preferred_element_type=F32)
                ds_c = (p_c * (dp_c - delta) * ATT_SCALE).astype(BF16)
                ds_p = (p_p * (dp_p - delta) * ATT_SCALE).astype(BF16)
                res.append((
                    jnp.dot(ds_c, kc, preferred_element_type=F32) + jnp.dot(ds_p, kp, preferred_element_type=F32),
                    lax.dot_general(ds_c, q, tn, preferred_element_type=F32),
                    lax.dot_general(ds_p, q, tn, preferred_element_type=F32),
                    lax.dot_general(p_c.astype(BF16), dob16, tn, preferred_element_type=F32),
                    lax.dot_general(p_p.astype(BF16), dob16, tn, preferred_element_type=F32)))
            both = [jnp.concatenate([res[0][i], res[1][i]], axis=1) for i in range(5)]
            dq_acc[cur, :] = both[0]
            dk_acc[cur, :] += both[1]
            dv_acc[cur, :] += both[3]
            dk_acc[prv, :] += both[2]
            dv_acc[prv, :] += both[4]
            return carry

        lax.fori_loop(0, S // QBLK, blk, 0)
        dq_ref[...] = dq_acc[...].astype(BF16)
        dk_ref[...] = dk_acc[...].astype(BF16)
        dv_ref[...] = dv_acc[...].astype(BF16)

    slab = lambda j: pl.BlockSpec((S, LANES), lambda i: (0, _att_slab(g, j) + i))
    blk128 = pl.BlockSpec((S, LANES), lambda i: (0, i))
    shp = jax.ShapeDtypeStruct((S, ATT_W), BF16)
    return pl.pallas_call(body, name=f"att_bwd_g{g}", grid=(ATT_PAIRS,),
                          in_specs=[slab(0), slab(1), slab(2)] + [blk128] * 4, out_specs=[blk128] * 3, out_shape=[shp] * 3,
                          scratch_shapes=[pltpu.VMEM((S, LANES), F32)] * 3,
                          compiler_params=_cparams(("parallel",)))(P, P, P, o, l, do, dl)


def _att_weights(l_refs):
    l0, l1, l2 = [r[...] for r in l_refs]
    m = jnp.maximum(jnp.maximum(l0, l1), l2)
    e = (jnp.exp(l0 - m), jnp.exp(l1 - m), jnp.exp(l2 - m))
    inv = 1.0 / (e[0] + e[1] + e[2])
    return [x * inv for x in e]


def _att_combine_fwd(os, ls, tm=512):
    S = os[0].shape[0]

    def body(o0, o1, o2, l0, l1, l2, a_ref):
        w = _att_weights((l0, l1, l2))
        a_ref[...] = (w[0] * o0[...] + w[1] * o1[...] + w[2] * o2[...]).astype(BF16)

    row = _rows(tm, ATT_W)
    return pl.pallas_call(body, name="att_combine_fwd", grid=(S // tm,), in_specs=[row] * 6, out_specs=row,
                          out_shape=jax.ShapeDtypeStruct((S, ATT_W), BF16),
                          compiler_params=_cparams(("parallel",)))(*os, *ls)


def _att_combine_bwd(da, os, ls, tm=512):
    S = da.shape[0]

    def body(da_ref, o0, o1, o2, l0, l1, l2, *out_refs):
        da = da_ref[...]
        w = _att_weights((l0, l1, l2))
        dw = (da * o0[...], da * o1[...], da * o2[...])
        mean = w[0] * dw[0] + w[1] * dw[1] + w[2] * dw[2]
        for g in range(3):
            out_refs[g][...] = w[g] * da
            out_refs[3 + g][...] = w[g] * (dw[g] - mean)

    row = _rows(tm, ATT_W)
    shp = jax.ShapeDtypeStruct((S, ATT_W), F32)
    return pl.pallas_call(body, name="att_combine_bwd", grid=(S // tm,), in_specs=[row] * 7, out_specs=[row] * 6,
                          out_shape=[shp] * 6, compiler_params=_cparams(("parallel",)))(da, *os, *ls)


@jax.custom_vjp
def _bdot(a, b):
    return jnp.dot(a.astype(BF16), b.astype(BF16), preferred_element_type=F32)


def _bdot_fwd(a, b):
    return _bdot(a, b), (a, b)


def _bdot_bwd(res, ct):
    a, b = res
    ct16 = ct.astype(BF16)
    da = lax.dot_general(ct16, b.astype(BF16), (((1,), (1,)), ((), ())), preferred_element_type=F32)
    db = lax.dot_general(a.astype(BF16), ct16, (((0,), (0,)), ((), ())), preferred_element_type=F32)
    return da, db


_bdot.defvjp(_bdot_fwd, _bdot_bwd)


def _two_piece_dot(x, m):
    hi = x.astype(BF16)
    lo = (x - hi.astype(F32)).astype(BF16)
    return jnp.dot(hi, m, preferred_element_type=F32) + jnp.dot(lo, m, preferred_element_type=F32)


def _head_sum_impl(x):
    sel = (lax.broadcasted_iota(jnp.int32, (D, LANES), 0) // HEAD == lax.broadcasted_iota(jnp.int32, (D, LANES), 1)).astype(BF16)
    sel_t = (lax.broadcasted_iota(jnp.int32, (LANES, D), 1) // HEAD == lax.broadcasted_iota(jnp.int32, (LANES, D), 0)).astype(BF16)
    return _two_piece_dot(_two_piece_dot(x, sel), sel_t)


@jax.custom_vjp
def _head_sum(x):
    return _head_sum_impl(x)


_head_sum.defvjp(lambda x: (_head_sum_impl(x), None), lambda _, ct: (_head_sum_impl(ct),))


def _softplus(z):
    return jnp.maximum(z, 0.0) + jnp.log(1.0 + jnp.exp(-jnp.abs(z)))


def _rwkv_prep_fn(zr, zrp, zk, zkp, zv, zvp, zl, zlp, mu_r, mu_k, mu_v, mu_l, w0, a0, k_k, k_a, w2, a2, g2p):
    r = zr + (zrp - zr) * mu_r
    k = zk + (zkp - zk) * mu_k
    v = zv + (zvp - zv) * mu_v
    lo = zl + (zlp - zl) * mu_l
    w_low, a_low, g_low = lo[:, 0:LORA_W], lo[:, LORA_W:LORA_W + LORA_A], lo[:, LANES:LANES + G_PAD]
    w_log = -_softplus(-(w0 + _bdot(jnp.tanh(w_low), w2))) - 0.5
    decay = jnp.exp(-jnp.exp(w_log))
    a = jax.nn.sigmoid(a0 + _bdot(a_low, a2))
    g = _bdot(jax.nn.sigmoid(g_low), g2p)
    kmod = k * (1.0 + (a - 1.0) * k_a)
    kk = k * k_k
    kk = kk / jnp.maximum(jnp.sqrt(_head_sum(kk * kk)), 1e-12)
    return r, decay, kmod, v, -kk, kk * a, g


def _rwkv_prep_specs(tm):
    vec = _full((1, D))
    slabs = []
    for col in (C_R // D, C_K // D, C_V // D):
        slabs += [_rows(tm, D, col), _prev8(tm, D, col)]
    slabs += [_rows(tm, LORA_PAD, C_LORA // LORA_PAD), _prev8(tm, LORA_PAD, C_LORA // LORA_PAD)]
    params = [vec, vec, vec, _full((1, LORA_PAD)), vec, vec, vec, vec,
              _full((LORA_W, D)), _full((LORA_A, D)), _full((G_PAD, D))]
    return slabs, params


def _prep_inputs(refs, first):
    vals = []
    for s in range(4):
        z = refs[2 * s][...]
        vals += [z, _shift_down(z, refs[2 * s + 1][...], 1, first)]
    return vals + [r[...] for r in refs[8:19]]


def _rwkv_prep(P, params, tm=256):
    S = P.shape[0]
    slabs, pspecs = _rwkv_prep_specs(tm)

    def body(*refs):
        outs = _rwkv_prep_fn(*_prep_inputs(refs, pl.program_id(0) == 0))
        for o_ref, val in zip(refs[19:], outs):
            o_ref[...] = val

    shp = jax.ShapeDtypeStruct((S, D), F32)
    return pl.pallas_call(body, name="rwkv_prep", grid=(S // tm,), in_specs=slabs + pspecs,
                          out_specs=[_rows(tm, D)] * 7, out_shape=[shp] * 7,
                          compiler_params=_cparams(("parallel",)))(*([P] * 8), *params)


def _rwkv_prep_bwd(P, params, cts_a, cts_b, tm=128):
    S = P.shape[0]
    slabs, pspecs = _rwkv_prep_specs(tm)
    has_b = [c is not None for c in cts_b]
    n_ct = 7 + sum(has_b)

    def body(*refs):
        first = pl.program_id(0) == 0
        ins = _prep_inputs(refs, first)
        ct_refs = refs[19:19 + n_ct]
        out_refs = refs[19 + n_ct:]
        cts, pos = [], 7
        for i in range(7):
            c = ct_refs[i][...]
            if has_b[i]:
                c = c + ct_refs[pos][...]
                pos += 1
            cts.append(c)
        _, vjp = jax.vjp(_rwkv_prep_fn, *ins)
        grads = vjp(tuple(cts))
        for s in range(4):
            out_refs[s][...] = grads[2 * s]
            out_refs[4 + s][...] = grads[2 * s + 1]
        for i in range(11):
            _acc(out_refs[8 + i], grads[8 + i], first)

    ct_in = list(cts_a) + [c for c in cts_b if c is not None]
    row, lrow = _rows(tm, D), _rows(tm, LORA_PAD)
    f = jax.ShapeDtypeStruct
    zshapes = [f((S, D), F32)] * 3 + [f((S, LORA_PAD), F32)]
    pshapes = [f((1, D), F32)] * 3 + [f((1, LORA_PAD), F32)] + [f((1, D), F32)] * 4 + [f((LORA_W, D), F32), f((LORA_A, D), F32), f((G_PAD, D), F32)]
    return pl.pallas_call(
        body, name="rwkv_prep_bwd", grid=(S // tm,),
        in_specs=slabs + pspecs + [row] * n_ct,
        out_specs=[row, row, row, lrow] * 2 + pspecs,
        out_shape=zshapes * 2 + pshapes,
        compiler_params=_cparams(("arbitrary",)))(*([P] * 8), *params, *ct_in)


def _shift_add(a, b, tm=256):
    S, W = a.shape

    def body(a_ref, b_ref, h_ref, o_ref):
        last = pl.program_id(0) == pl.num_programs(0) - 1
        o_ref[...] = (a_ref[...] + _shift_up(b_ref[...], h_ref[...], 1, last)).astype(BF16)

    return pl.pallas_call(body, name="shift_add", grid=(S // tm,),
                          in_specs=[_rows(tm, W), _rows(tm, W), _next8(tm, W, S)],
                          out_specs=_rows(tm, W), out_shape=jax.ShapeDtypeStruct((S, W), BF16),
                          compiler_params=_cparams(("parallel",)))(a, b, b)


def _rwkv_post_fn(y, r, kmod, v, g, lnx_w, lnx_b, r_k):
    mean = _head_sum(y) * (1.0 / HEAD)
    yc = y - mean
    var = _head_sum(yc * yc) * (1.0 / HEAD)
    yn = yc * lax.rsqrt(var + GN_EPS) * lnx_w + lnx_b
    bonus = _head_sum(r * kmod * r_k) * v
    return (yn + bonus) * g


def _rwkv_post(y, r, kmod, v, g, lnx_w, lnx_b, r_k, tm=256):
    S = y.shape[0]

    def body(y_ref, r_ref, k_ref, v_ref, g_ref, w_ref, b_ref, rk_ref, o_ref):
        o_ref[...] = _rwkv_post_fn(y_ref[...], r_ref[...], k_ref[...], v_ref[...], g_ref[...],
                                   w_ref[...], b_ref[...], rk_ref[...]).astype(BF16)

    row, vec = _rows(tm, D), _full((1, D))
    return pl.pallas_call(body, name="rwkv_post", grid=(S // tm,), in_specs=[row] * 5 + [vec] * 3, out_specs=row,
                          out_shape=jax.ShapeDtypeStruct((S, D), BF16),
                          compiler_params=_cparams(("parallel",)))(y, r, kmod, v, g, lnx_w, lnx_b, r_k)


def _rwkv_post_bwd(drw, y, r, kmod, v, g, lnx_w, lnx_b, r_k, tm=256):
    S = y.shape[0]

    def body(d_ref, y_ref, r_ref, k_ref, v_ref, g_ref, w_ref, b_ref, rk_ref, *out_refs):
        first = pl.program_id(0) == 0
        _, vjp = jax.vjp(_rwkv_post_fn, y_ref[...], r_ref[...], k_ref[...], v_ref[...], g_ref[...],
                         w_ref[...], b_ref[...], rk_ref[...])
        grads = vjp(d_ref[...])
        for i in range(5):
            out_refs[i][...] = grads[i]
        for i in range(5, 8):
            _acc(out_refs[i], grads[i], first)

    row, vec = _rows(tm, D), _full((1, D))
    f = jax.ShapeDtypeStruct
    return pl.pallas_call(body, name="rwkv_post_bwd", grid=(S // tm,), in_specs=[row] * 6 + [vec] * 3,
                          out_specs=[row] * 5 + [vec] * 3, out_shape=[f((S, D), F32)] * 5 + [f((1, D), F32)] * 3,
                          compiler_params=_cparams(("arbitrary",)))(drw, y, r, kmod, v, g, lnx_w, lnx_b, r_k)


def _col(tile, ii, lane_lo):
    a = jnp.broadcast_to(tile[0:HEAD, ii:ii + 1], (HEAD, LANES))
    b = jnp.broadcast_to(tile[HEAD:2 * HEAD, ii:ii + 1], (HEAD, LANES))
    return jnp.where(lane_lo, a, b)


SCAN_NG = SCAN_TB // SUBLANES
N_PIECES = 2


def _scan_sources(lane_refs, mxu_refs, t_ref, p_ref):
    for o, ref in enumerate(lane_refs):
        t_ref[o] = ref[...].T
    for o, ref in enumerate(mxu_refs):
        rest = ref[...]
        for p in range(N_PIECES):
            piece = rest.astype(BF16).astype(F32)
            rest = rest - piece
            p_ref[o, p] = piece


def _gen_tiles(buf_ref, g, n_lane, n_mxu, t_ref, p_ref):
    lane_lo = lax.broadcasted_iota(jnp.int32, (HEAD, LANES), 1) < HEAD
    tiles = [pltpu.roll(t_ref[o], (LANES - SUBLANES * g) % LANES, 1) for o in range(n_lane)]
    for ii in range(SUBLANES):
        for o in range(n_lane):
            buf_ref[ii, o] = _col(tiles[o], ii, lane_lo)
    if n_mxu == 0:
        return
    diag = (lax.broadcasted_iota(jnp.int32, (HEAD, LANES), 1) % HEAD
            == lax.broadcasted_iota(jnp.int32, (HEAD, LANES), 0)).astype(BF16)
    ones = (lax.broadcasted_iota(jnp.int32, (LANES, LANES), 0) // HEAD
            == lax.broadcasted_iota(jnp.int32, (LANES, LANES), 1) // HEAD).astype(BF16)
    start = pl.multiple_of(g * SUBLANES, SUBLANES)
    for o in range(n_mxu):
        cols = None
        for p in range(N_PIECES):
            rows = p_ref[o, p, pl.ds(start, SUBLANES), :].astype(BF16)
            lhs = jnp.concatenate([jnp.broadcast_to(rows[ii:ii + 1], (HEAD, LANES)) * diag for ii in range(SUBLANES)], axis=0)
            part = jnp.dot(lhs, ones, preferred_element_type=F32)
            cols = part if cols is None else cols + part
        for ii in range(SUBLANES):
            buf_ref[ii, n_lane + o] = cols[ii * HEAD:(ii + 1) * HEAD]


def _scan_scratch(n_lane, n_mxu):
    tiles = pltpu.VMEM((SUBLANES, n_lane + n_mxu, HEAD, LANES), F32)
    return [pltpu.VMEM((HEAD, LANES), F32), pltpu.VMEM((max(n_lane, 1), LANES, SCAN_TB), F32),
            pltpu.VMEM((max(n_mxu, 1), N_PIECES, SCAN_TB, LANES), F32), tiles, tiles]


def _scan_fwd(r, w, k, v, a, b):
    S = r.shape[0]
    nblk = S // SCAN_TB
    npair = N_HEADS // 2

    def body(r_ref, w_ref, k_ref, v_ref, a_ref, b_ref, y_ref, sall_ref, s_ref, t_ref, p_ref, buf0, buf1):
        @pl.when(pl.program_id(1) == 0)
        def _():
            s_ref[...] = jnp.zeros_like(s_ref)

        _scan_sources((w_ref,), (r_ref, k_ref, a_ref, b_ref), t_ref, p_ref)
        gen = functools.partial(_gen_tiles, n_lane=1, n_mxu=4, t_ref=t_ref, p_ref=p_ref)

        def steps(buf, g, st):
            for ii in range(SUBLANES):
                t = g * SUBLANES + ii
                wc, rc, kc, ac, bc = [buf[ii, o] for o in range(5)]
                sall_ref[t] = st
                sa = jnp.sum(st * ac, axis=0, keepdims=True)
                st = st * wc + bc * sa + kc * v_ref[pl.ds(t, 1), :]
                y_ref[pl.ds(t, 1), :] = jnp.sum(st * rc, axis=0, keepdims=True)
            return st

        gen(buf0, 0)

        def two_groups(i, st):
            g = 2 * i
            gen(buf1, g + 1)
            st = steps(buf0, g, st)
            gen(buf0, jnp.minimum(g + 2, SCAN_NG - 1))
            return steps(buf1, g + 1, st)

        s_ref[...] = lax.fori_loop(0, SCAN_NG // 2, two_groups, s_ref[...])

    blk = pl.BlockSpec((SCAN_TB, LANES), lambda p, i: (i, p))
    return pl.pallas_call(
        body, name="scan_fwd", grid=(npair, nblk), in_specs=[blk] * 6,
        out_specs=[blk, pl.BlockSpec((SCAN_TB, None, HEAD, LANES), lambda p, i: (i, p, 0, 0))],
        out_shape=[jax.ShapeDtypeStruct((S, D), F32), jax.ShapeDtypeStruct((S, npair, HEAD, LANES), F32)],
        scratch_shapes=_scan_scratch(1, 4),
        compiler_params=_cparams(("parallel", "arbitrary")))(r, w, k, v, a, b)


def _scan_bwd(r, w, k, v, a, b, sall, dy):
    S = r.shape[0]
    nblk = S // SCAN_TB
    npair = N_HEADS // 2
    NG = SCAN_TB // SUBLANES
    nt = (((1,), (1,)), ((), ()))

    def body(r_ref, w_ref, k_ref, v_ref, a_ref, b_ref, sall_ref, dy_ref,
             dr_ref, dw_ref, dk_ref, dv_ref, da_ref, db_ref, ds_ref, t_ref, p_ref, buf0, buf1):
        @pl.when(pl.program_id(1) == 0)
        def _():
            ds_ref[...] = jnp.zeros_like(ds_ref)

        _scan_sources((w_ref, r_ref, k_ref), (a_ref, b_ref), t_ref, p_ref)
        gen = functools.partial(_gen_tiles, n_lane=3, n_mxu=2, t_ref=t_ref, p_ref=p_ref)
        half_sel = (lax.broadcasted_iota(jnp.int32, (SUBLANES, LANES), 0)
                    == lax.broadcasted_iota(jnp.int32, (SUBLANES, LANES), 1) // HEAD).astype(BF16)

        def key_grad(ref, t, z):
            res = lax.dot_general(half_sel, z.astype(BF16), nt, preferred_element_type=F32)
            ref[pl.ds(t, 1), 0:HEAD] = res[0:1]
            ref[pl.ds(t, 1), HEAD:2 * HEAD] = res[1:2]

        def steps(buf, g, dst):
            for ii in reversed(range(SUBLANES)):
                t = g * SUBLANES + ii
                wc, rc, kc, ac, bc = [buf[ii, o] for o in range(5)]
                vrow = v_ref[pl.ds(t, 1), :]
                dyrow = dy_ref[pl.ds(t, 1), :]
                sp = sall_ref[t]
                sa = jnp.sum(sp * ac, axis=0, keepdims=True)
                sn = sp * wc + bc * sa + kc * vrow
                dsn = dst + rc * dyrow
                dv_ref[pl.ds(t, 1), :] = jnp.sum(dsn * kc, axis=0, keepdims=True)
                dsa = jnp.sum(dsn * bc, axis=0, keepdims=True)
                key_grad(dr_ref, t, sn * dyrow)
                key_grad(dw_ref, t, dsn * sp)
                key_grad(dk_ref, t, dsn * vrow)
                key_grad(da_ref, t, sp * dsa)
                key_grad(db_ref, t, dsn * sa)
                dst = dsn * wc + ac * dsa
            return dst

        gen(buf0, SCAN_NG - 1)

        def two_groups(i, dst):
            g = SCAN_NG - 1 - 2 * i
            gen(buf1, g - 1)
            dst = steps(buf0, g, dst)
            gen(buf0, jnp.maximum(g - 2, 0))
            return steps(buf1, g - 1, dst)

        ds_ref[...] = lax.fori_loop(0, SCAN_NG // 2, two_groups, ds_ref[...])

    blk = pl.BlockSpec((SCAN_TB, LANES), lambda p, i: (nblk - 1 - i, p))
    shp = jax.ShapeDtypeStruct((S, D), F32)
    return pl.pallas_call(
        body, name="scan_bwd", grid=(npair, nblk),
        in_specs=[blk] * 6 + [pl.BlockSpec((SCAN_TB, None, HEAD, LANES), lambda p, i: (nblk - 1 - i, p, 0, 0)), blk],
        out_specs=[blk] * 6, out_shape=[shp] * 6,
        scratch_shapes=_scan_scratch(3, 2),
        compiler_params=_cparams(("parallel", "arbitrary")))(r, w, k, v, a, b, sall, dy)


def _ada_fwd(c8, w_ada, b_ada):
    def body(c_ref, w_ref, b_ref, o_ref):
        o_ref[...] = jnp.dot(c_ref[...].astype(BF16), w_ref[...], preferred_element_type=F32) + b_ref[...]

    tn = 1536
    return pl.pallas_call(body, name="ada_fwd", grid=(6 * D // tn,),
                          in_specs=[_full((SUBLANES, D)), pl.BlockSpec((D, tn), lambda j: (0, j)), pl.BlockSpec((1, tn), lambda j: (0, j))],
                          out_specs=pl.BlockSpec((SUBLANES, tn), lambda j: (0, j)),
                          out_shape=jax.ShapeDtypeStruct((SUBLANES, 6 * D), F32),
                          compiler_params=_cparams(("parallel",)))(c8, w_ada, b_ada)


def _outer(col, row):
    N = row.shape[1]
    tn = 1536

    def body(c_ref, r_ref, o_ref):
        o_ref[...] = c_ref[...] * r_ref[...]

    return pl.pallas_call(body, name="ada_wgrad", grid=(N // tn,),
                          in_specs=[_full((D, 1)), pl.BlockSpec((1, tn), lambda j: (0, j))],
                          out_specs=pl.BlockSpec((D, tn), lambda j: (0, j)),
                          out_shape=jax.ShapeDtypeStruct((D, N), F32),
                          compiler_params=_cparams(("parallel",)))(col, row)


def _exchange(srcs, broadcast, name):
    n = len(srcs)
    out_shape = [jax.ShapeDtypeStruct((N_DEV,) + (s.shape if broadcast else s.shape[1:]), s.dtype) for s in srcs]

    def body(*refs):
        src_refs, out_refs = refs[:n], refs[n:2 * n]
        send_sems, recv_sems, local_sems = refs[2 * n:]
        x, y, c = lax.axis_index("x"), lax.axis_index("y"), lax.axis_index("c")
        me = 4 * x + 2 * y + c

        def block(i, j):
            return src_refs[i] if broadcast else src_refs[i].at[j]

        def remote(i, d, src_slot, dst_slot):
            px, py, pc = x ^ (d >> 2), y ^ ((d >> 1) & 1), c ^ (d & 1)
            return pltpu.make_async_remote_copy(
                src_ref=block(i, src_slot), dst_ref=out_refs[i].at[dst_slot], send_sem=send_sems.at[i, d],
                recv_sem=recv_sems.at[i, d], device_id=(px, py, pc), device_id_type=_MESH)

        local = [pltpu.make_async_copy(block(i, me), out_refs[i].at[me], local_sems.at[i]) for i in range(n)]
        for cp in local:
            cp.start()
        sends = [remote(i, d, me ^ d, me) for d in range(1, N_DEV) for i in range(n)]
        for cp in sends:
            cp.start()
        for d in range(1, N_DEV):
            for i in range(n):
                remote(i, d, me, me ^ d).wait_recv()
        for cp in sends:
            cp.wait_send()
        for cp in local:
            cp.wait()

    any_spec = pl.BlockSpec(memory_space=pl.ANY)
    return pl.pallas_call(
        body, name=name, out_shape=out_shape, in_specs=[any_spec] * n, out_specs=[any_spec] * n,
        scratch_shapes=[pltpu.SemaphoreType.DMA((n, N_DEV)), pltpu.SemaphoreType.DMA((n, N_DEV)), pltpu.SemaphoreType.DMA((n,))],
        compiler_params=pltpu.CompilerParams(has_side_effects=True),
    )(*srcs)


def _sum_adam(parts, w, m, v, name):
    _, R, C = parts.shape
    tm = 256 if R % 256 == 0 else R
    c1 = 1.0 / (1.0 - ADAM_B1 ** ADAM_STEP)
    c2 = 1.0 / (1.0 - ADAM_B2 ** ADAM_STEP)

    def body(p_ref, w_ref, m_ref, v_ref, g_ref, d_ref, nm_ref, nv_ref):
        g = p_ref[0].astype(F32)
        for j in range(1, N_DEV):
            g = g + p_ref[j].astype(F32)
        nm = ADAM_B1 * m_ref[...] + (1.0 - ADAM_B1) * g
        nv = ADAM_B2 * v_ref[...] + (1.0 - ADAM_B2) * (g * g)
        g_ref[...] = g
        nm_ref[...] = nm
        nv_ref[...] = nv
        d_ref[...] = -ADAM_LR * ((nm * c1) / (jnp.sqrt(nv * c2) + ADAM_EPS) + ADAM_WD * w_ref[...])

    row = _rows(tm, C)
    shp = jax.ShapeDtypeStruct((R, C), F32)
    return pl.pallas_call(body, name=name, grid=(R // tm,),
                          in_specs=[pl.BlockSpec((N_DEV, tm, C), lambda i: (0, i, 0)), row, row, row],
                          out_specs=[row] * 4, out_shape=[shp] * 4,
                          compiler_params=_cparams(("parallel",)))(parts, w, m, v)


PACK_ALIGN = 16 * LANES
PACK_ROWS = 512 * LANES

SHARDED = (("w_ada", 1), ("w_in", 1), ("w2", 1), ("a2", 1), ("g2", 1), ("w_att_out", 1), ("w_rwkv_out", 0),
           ("w_o", 0), ("w_up", 1), ("conv_w", 1), ("w_down", 0))
REPLICATED = ("b_ada", "norm1_w", "b_gate", "mu_shift", "w0", "a0", "k_k", "k_a", "r_k", "lnx_w", "lnx_b",
              "norm2_w", "conv_b", "norm_f_w")
WEIGHTS = ("w_ada", "b_ada", "norm1_w", "w_in", "b_gate", "mu_shift", "w0", "w2", "a0", "a2", "g2", "k_k", "k_a", "r_k",
           "lnx_w", "lnx_b", "w_att_out", "w_rwkv_out", "w_o", "norm2_w", "w_up", "conv_w", "conv_b", "w_down", "norm_f_w")


def _pack(arrays):
    flat, layout, off = [], [], 0
    for i, a in enumerate(arrays):
        n = a.size
        pad = (-n) % PACK_ALIGN if i + 1 < len(arrays) else (-(off + n)) % PACK_ROWS
        flat.append(a.reshape(-1))
        if pad:
            flat.append(jnp.zeros((pad,), a.dtype))
        layout.append((off, n, a.shape))
        off += n + pad
    return jnp.concatenate(flat).reshape(-1, LANES), layout


def _unpack(buf, layout):
    flat = buf.reshape(-1)
    return [flat[off:off + n].reshape(shape) for off, n, shape in layout]


def _pad_w_in(w_in):
    rkv = w_in[:, ATT_IN:ATT_IN + 3 * D]
    lora = w_in[:, ATT_IN + 3 * D:ATT_IN + RWKV_IN]
    gates = w_in[:, ATT_IN + RWKV_IN:]
    att = w_in[:, :ATT_IN]
    lw, la, lg = lora[:, :LORA_W], lora[:, LORA_W:LORA_W + LORA_A], lora[:, LORA_W + LORA_A:]
    zeros = jnp.zeros((w_in.shape[0], LORA_PAD - LANES - LORA_G), w_in.dtype)
    return jnp.concatenate([rkv, gates, att, lw, la, lg, zeros], axis=1)


def _unpad_w_in(g):
    att = g[:, C_ATT:C_ATT + ATT_IN]
    rkv = g[:, C_R:C_R + 3 * D]
    lora = jnp.concatenate([g[:, C_LORA:C_LORA + LORA_W + LORA_A], g[:, C_LORA + LANES:C_LORA + LANES + LORA_G]], axis=1)
    gates = g[:, C_GA:C_GA + 2 * D]
    return jnp.concatenate([att, rkv, lora, gates], axis=1)


def _pad_mu(mu):
    lo = mu[:, 3 * D:]
    mu_l = jnp.concatenate([lo[:, :LORA_W + LORA_A], lo[:, LORA_W + LORA_A:], jnp.zeros((1, LORA_PAD - LANES - LORA_G), mu.dtype)], axis=1)
    return mu[:, :D], mu[:, D:2 * D], mu[:, 2 * D:3 * D], mu_l


def _local_step(x, c, W, target):
    S = x.shape[0]
    G = {}
    c8 = jnp.pad(c, ((0, SUBLANES - 1), (0, 0)))
    ada = _ada_fwd(c8, W["w_ada"], W["b_ada"])[0:1]
    sh1, sc1, gt1, sh2, sc2, gt2 = [ada[:, i * D:(i + 1) * D] for i in range(6)]
    h1, rstd1 = _norm_fwd(x, None, None, W["norm1_w"], sc1, sh1, "norm1_fwd")
    w_in_p = _pad_w_in(W["w_in"])
    P = _mm(h1, w_in_p, "nn", F32, "proj_in")

    o_g, l_g = zip(*[_att_fwd(P, g) for g in range(len(ATT_PATTERNS))])
    att = _att_combine_fwd(o_g, l_g)
    y_att = _mm(att, W["w_att_out"], "nn", F32, "att_out")

    mu_r, mu_k, mu_v, mu_l = _pad_mu(W["mu_shift"])
    g2p = jnp.pad(W["g2"], ((0, G_PAD - LORA_G), (0, 0)))
    prep_params = [mu_r, mu_k, mu_v, mu_l, W["w0"], W["a0"], W["k_k"], W["k_a"], W["w2"], W["a2"], g2p]
    r_, dec, kmod, v_, aa, bb, gg = _rwkv_prep(P, prep_params)
    y_scan, states = _scan_fwd(r_, dec, kmod, v_, aa, bb)
    r_k = W["r_k"].reshape(1, D)
    rw = _rwkv_post(y_scan, r_, kmod, v_, gg, W["lnx_w"], W["lnx_b"], r_k)
    y_rwkv = _mm(rw, W["w_rwkv_out"], "nn", F32, "rwkv_out")

    bga, bgr = W["b_gate"][:, :D], W["b_gate"][:, D:]
    mix = _gate_fwd(P, bga, bgr, y_att, y_rwkv)
    mo = _mm(mix, W["w_o"], "nn", F32, "mix_out")
    x2, h2, rstd2 = _norm_fwd(x, mo, gt1, W["norm2_w"], sc2, sh2, "norm2_fwd")
    u = _mm(h2, W["w_up"], "nn", F32, "ffn_up")
    conv_w8 = jnp.pad(W["conv_w"], ((0, SUBLANES - 3), (0, 0)))
    act = _conv_fwd(u, conv_w8, W["conv_b"])
    f = _mm(act, W["w_down"], "nn", F32, "ffn_down")
    loss_blk, dx3, df, dgt2, G["norm_f_w"] = _final(x2, f, gt2, W["norm_f_w"], target)
    loss = loss_blk[0, 0]

    dact = _mm(df, W["w_down"], "nt", BF16, "ffn_down_dx")
    G["w_down"] = _mm(act, df, "tn", F32, "ffn_down_dw")
    duc, dwg, dwv, dbg, dbv = _conv_bwd_a(dact, u, conv_w8, W["conv_b"])
    G["conv_w"] = jnp.concatenate([dwg[0:3], dwv[0:3]], axis=1)
    G["conv_b"] = jnp.concatenate([dbg, dbv], axis=1)
    du = _conv_bwd_b(duc, conv_w8)
    dh2 = _mm(du, W["w_up"], "nt", F32, "ffn_up_dx")
    G["w_up"] = _mm(h2, du, "tn", F32, "ffn_up_dw")
    dx2, dsh2, dsc2, G["norm2_w"], dmo, dgt1 = _norm_bwd(dh2, x2, rstd2, W["norm2_w"], sc2, dx3, mo, gt1, "norm2_bwd")
    dmix = _mm(dmo, W["w_o"], "nt", F32, "mix_out_dx")
    G["w_o"] = _mm(mix, dmo, "tn", F32, "mix_out_dw")
    dy_att, dy_rwkv, dpga, dpgr, dbga, dbgr = _gate_bwd(dmix, P, bga, bgr, y_att, y_rwkv)
    G["b_gate"] = jnp.concatenate([dbga, dbgr], axis=1)

    datt = _mm(dy_att, W["w_att_out"], "nt", F32, "att_out_dx")
    G["w_att_out"] = _mm(att, dy_att, "tn", F32, "att_out_dw")
    dcomb = _att_combine_bwd(datt, o_g, l_g)
    dp_att = []
    for g in range(len(ATT_PATTERNS)):
        dp_att += _att_bwd(P, o_g[g], l_g[g], dcomb[g], dcomb[3 + g], g)

    drw = _mm(dy_rwkv, W["w_rwkv_out"], "nt", F32, "rwkv_out_dx")
    G["w_rwkv_out"] = _mm(rw, dy_rwkv, "tn", F32, "rwkv_out_dw")
    dy_scan, dr1, dk1, dv1, dgg, G["lnx_w"], G["lnx_b"], drk = _rwkv_post_bwd(drw, y_scan, r_, kmod, v_, gg, W["lnx_w"], W["lnx_b"], r_k)
    G["r_k"] = drk.reshape(W["r_k"].shape)
    dr2, ddec, dk2, dv2, daa, dbb = _scan_bwd(r_, dec, kmod, v_, aa, bb, states, dy_scan)
    pb = _rwkv_prep_bwd(P, prep_params, [dr2, ddec, dk2, dv2, daa, dbb, dgg], [dr1, None, dk1, dv1, None, None, None])
    dz, dzp, dpar = pb[0:4], pb[4:8], pb[8:]
    dp_rkv = [_shift_add(dz[i], dzp[i]) for i in range(3)]
    dp_lora = _shift_add(dz[3], dzp[3])
    dmu_r, dmu_k, dmu_v, dmu_l, G["w0"], G["a0"], G["k_k"], G["k_a"], G["w2"], G["a2"], dg2p = dpar
    G["g2"] = dg2p[0:LORA_G]
    G["mu_shift"] = jnp.concatenate([dmu_r, dmu_k, dmu_v, dmu_l[:, :LORA_W + LORA_A], dmu_l[:, LANES:LANES + LORA_G]], axis=1)

    dP = jnp.concatenate(dp_rkv + [dpga, dpgr] + dp_att + [dp_lora], axis=1)
    dh1 = _mm(dP, w_in_p, "nt", F32, "proj_in_dx")
    G["w_in"] = _unpad_w_in(_mm(h1, dP, "tn", F32, "proj_in_dw"))
    grad_x, dsh1, dsc1, G["norm1_w"] = _norm_bwd(dh1, x, rstd1, W["norm1_w"], sc1, dx2, None, None, "norm1_bwd")
    dada = jnp.concatenate([dsh1, dsc1, dgt1, dsh2, dsc2, dgt2], axis=1)
    G["b_ada"] = dada
    G["w_ada"] = _outer(c.reshape(D, 1), dada)
    return loss, grad_x, G


def _full_weight(gathered, axis):
    _, rows, cols = gathered.shape
    if axis == 0:
        return gathered.reshape(N_DEV * rows, cols)
    return gathered.transpose(1, 0, 2).reshape(rows, N_DEV * cols)


def _owner_blocks(g, axis):
    rows, cols = g.shape
    g = g.astype(BF16)
    if axis == 0:
        return g.reshape(N_DEV, rows // N_DEV, cols)
    return g.reshape(rows, N_DEV, cols // N_DEV).transpose(1, 0, 2)


def kernel(x, c, w_ada, b_ada, norm1_w, w_in, b_gate, mu_shift, w0, w2, a0, a2, g2, k_k, k_a, r_k, lnx_w, lnx_b, w_att_out, w_rwkv_out, w_o, norm2_w, w_up, conv_w, conv_b, w_down, norm_f_w, loss_target, m_w_ada, m_b_ada, m_norm1_w, m_w_in, m_b_gate, m_mu_shift, m_w0, m_w2, m_a0, m_a2, m_g2, m_k_k, m_k_a, m_r_k, m_lnx_w, m_lnx_b, m_w_att_out, m_w_rwkv_out, m_w_o, m_norm2_w, m_w_up, m_conv_w, m_conv_b, m_w_down, m_norm_f_w, v_w_ada, v_b_ada, v_norm1_w, v_w_in, v_b_gate, v_mu_shift, v_w0, v_w2, v_a0, v_a2, v_g2, v_k_k, v_k_a, v_r_k, v_lnx_w, v_lnx_b, v_w_att_out, v_w_rwkv_out, v_w_o, v_norm2_w, v_w_up, v_conv_w, v_conv_b, v_w_down, v_norm_f_w):
    env = dict(locals())
    w_shard = {n: env[n] for n in WEIGHTS}
    m_shard = {n: env["m_" + n] for n in WEIGHTS}
    v_shard = {n: env["v_" + n] for n in WEIGHTS}

    gathered = _exchange([w_shard[n][0].astype(BF16) for n, _ in SHARDED], True, "gather_weights")
    W = {n: _full_weight(g, axis) for (n, axis), g in zip(SHARDED, gathered)}
    for n in REPLICATED:
        W[n] = w_shard[n].reshape(1, -1) if n != "r_k" else w_shard[n][0]

    loss, grad_x, G = _local_step(x[0], c, W, loss_target[0])
    loss = lax.psum(loss, ("x", "y", "c"))

    parts = _exchange([_owner_blocks(G[n], axis) for n, axis in SHARDED], False, "scatter_grads")
    out = {}
    for (n, _), p in zip(SHARDED, parts):
        res = _sum_adam(p, w_shard[n][0], m_shard[n][0], v_shard[n][0], "adam_" + n)
        for kind, a in zip(("grad", "delta", "new_m", "new_v"), res):
            out[kind, n] = a[None]

    small, slayout = _pack([G[n].reshape(-1) for n in REPLICATED])
    sparts, = _exchange([small], True, "gather_small_grads")
    sw, _ = _pack([w_shard[n].reshape(-1) for n in REPLICATED])
    sm, _ = _pack([m_shard[n].reshape(-1) for n in REPLICATED])
    sv, _ = _pack([v_shard[n].reshape(-1) for n in REPLICATED])
    res = _sum_adam(sparts, sw, sm, sv, "adam_replicated")
    for kind, buf in zip(("grad", "delta", "new_m", "new_v"), res):
        for n, a in zip(REPLICATED, _unpack(buf, slayout)):
            out[kind, n] = a.reshape(w_shard[n].shape)

    return (loss, grad_x[None], *[out[kind, n] for kind in ("grad", "delta", "new_m", "new_v") for n in WEIGHTS])
```

```python
import functools
import math

import jax
import jax.numpy as jnp
from jax import lax
from jax.experimental import pallas as pl
from jax.experimental.pallas import tpu as pltpu

F32 = jnp.float32
BF16 = jnp.bfloat16

D = 1024
HEAD = 64
ATT_PATTERNS = ((128, 1), (512, 4), (2048, 16))
ATT_HEADS = 8
ATT_W = ATT_HEADS * HEAD
ATT_IN = 3 * 3 * ATT_W
QBLK = 128
N_HEADS = D // HEAD
LORA_W, LORA_A, LORA_G = 64, 64, 160
RWKV_IN = 3 * D + LORA_W + LORA_A + LORA_G
N_IN = ATT_IN + RWKV_IN + 2 * D
D_FF = 2816
RMS_EPS = 1e-6
GN_EPS = 64e-5
N_DEV = 8
LANES = 128
SUBLANES = 8

C_R, C_K, C_V, C_GA, C_GR = 0, 1024, 2048, 3072, 4096
C_ATT = 5120
C_LORA = C_ATT + ATT_IN
LORA_PAD = 512
G_PAD = 256
N_PAD = C_LORA + LORA_PAD

ADAM_LR, ADAM_B1, ADAM_B2, ADAM_EPS, ADAM_WD, ADAM_STEP = 0.001, 0.9, 0.999, 1e-08, 0.01, 10

SCAN_TB = 128
VMEM_LIMIT = 56 * 1024 * 1024

_MESH = pl.DeviceIdType.MESH


def _cparams(sem):
    return pltpu.CompilerParams(dimension_semantics=sem, vmem_limit_bytes=VMEM_LIMIT)


def _tile(dim, pref):
    if dim <= pref:
        return dim
    best = None
    for t in range(LANES, pref + 1, LANES):
        if dim % t == 0:
            best = t
    assert best is not None, dim
    return best


MM_TILES = {"nn": (1024, 1408, 1408), "nt": (512, 2048, 1408), "tn": (1408, 1408, 1024)}


def _mm(a, b, mode, out_dtype, name):
    if mode == "nn":
        (M, K), (K2, N) = a.shape, b.shape
    elif mode == "nt":
        (M, K), (N, K2) = a.shape, b.shape
    else:
        (K, M), (K2, N) = a.shape, b.shape
    assert K == K2, (a.shape, b.shape, mode)
    tm, tn, tk = (_tile(dim, pref) for dim, pref in zip((M, N, K), MM_TILES[mode]))
    nk = K // tk
    dims = {"nn": (((1,), (0,)), ((), ())), "nt": (((1,), (1,)), ((), ())), "tn": (((0,), (0,)), ((), ()))}[mode]

    def body(a_ref, b_ref, o_ref, acc_ref):
        k = pl.program_id(2)
        part = lax.dot_general(a_ref[...].astype(BF16), b_ref[...].astype(BF16), dims,
                               preferred_element_type=F32)
        if nk == 1:
            o_ref[...] = part.astype(o_ref.dtype)
            return

        @pl.when(k == 0)
        def _():
            acc_ref[...] = part

        @pl.when(jnp.logical_and(k > 0, k < nk - 1))
        def _():
            acc_ref[...] += part

        @pl.when(k == nk - 1)
        def _():
            o_ref[...] = (acc_ref[...] + part).astype(o_ref.dtype)

    a_spec = pl.BlockSpec((tk, tm), lambda i, j, k: (k, i)) if mode == "tn" else pl.BlockSpec((tm, tk), lambda i, j, k: (i, k))
    b_spec = pl.BlockSpec((tn, tk), lambda i, j, k: (j, k)) if mode == "nt" else pl.BlockSpec((tk, tn), lambda i, j, k: (k, j))
    return pl.pallas_call(
        body, name=name, grid=(M // tm, N // tn, nk),
        in_specs=[a_spec, b_spec],
        out_specs=pl.BlockSpec((tm, tn), lambda i, j, k: (i, j)),
        out_shape=jax.ShapeDtypeStruct((M, N), out_dtype),
        scratch_shapes=[pltpu.VMEM((tm, tn) if nk > 1 else (SUBLANES, LANES), F32)],
        compiler_params=_cparams(("parallel", "parallel", "arbitrary")),
    )(a, b)


def _rows(tm, w, col=0):
    return pl.BlockSpec((tm, w), lambda i: (i, col))


def _full(shape):
    return pl.BlockSpec(shape, lambda i: (0,) * len(shape))


def _prev8(tm, w, col=0):
    return pl.BlockSpec((SUBLANES, w), lambda i: (jnp.maximum(i * (tm // SUBLANES) - 1, 0), col))


def _next8(tm, w, n_rows, col=0):
    last = n_rows // SUBLANES - 1
    return pl.BlockSpec((SUBLANES, w), lambda i: (jnp.minimum((i + 1) * (tm // SUBLANES), last), col))


def _shift_down(x, halo, k, first):
    rolled = pltpu.roll(x, k, 0)
    row = lax.broadcasted_iota(jnp.int32, x.shape, 0)
    out = rolled
    for j in range(k):
        h = jnp.where(first, 0.0, halo[SUBLANES - k + j:SUBLANES - k + j + 1, :])
        out = jnp.where(row == j, h, out)
    return out


def _shift_up(x, halo, k, last):
    n = x.shape[0]
    rolled = pltpu.roll(x, n - k, 0)
    row = lax.broadcasted_iota(jnp.int32, x.shape, 0)
    out = rolled
    for j in range(k):
        h = jnp.where(last, 0.0, halo[j:j + 1, :])
        out = jnp.where(row == n - k + j, h, out)
    return out


def _acc(ref, val, first):
    @pl.when(first)
    def _():
        ref[...] = val

    @pl.when(jnp.logical_not(first))
    def _():
        ref[...] += val


def _colsum(x):
    return jnp.sum(x, axis=0, keepdims=True)


def _norm_fwd(x, mo, gt, nw, sc, sh, name, tm=256):
    S = x.shape[0]
    has_res = mo is not None

    def body(*refs):
        if has_res:
            x_ref, mo_ref, gt_ref, nw_ref, sc_ref, sh_ref, x2_ref, h_ref, rs_ref = refs
            x2 = x_ref[...] + gt_ref[...] * mo_ref[...]
            x2_ref[...] = x2
        else:
            x_ref, nw_ref, sc_ref, sh_ref, h_ref, rs_ref = refs
            x2 = x_ref[...]
        rstd = lax.rsqrt(jnp.mean(x2 * x2, axis=-1, keepdims=True) + RMS_EPS)
        rs_ref[...] = rstd
        h_ref[...] = ((x2 * rstd * nw_ref[...]) * (1.0 + sc_ref[...]) + sh_ref[...]).astype(BF16)

    vec = _full((1, D))
    ins = [x, mo, gt, nw, sc, sh] if has_res else [x, nw, sc, sh]
    in_specs = [_rows(tm, D), _rows(tm, D), vec, vec, vec, vec] if has_res else [_rows(tm, D), vec, vec, vec]
    outs = [jax.ShapeDtypeStruct((S, D), BF16), jax.ShapeDtypeStruct((S, 1), F32)]
    out_specs = [_rows(tm, D), _rows(tm, 1)]
    if has_res:
        outs = [jax.ShapeDtypeStruct((S, D), F32)] + outs
        out_specs = [_rows(tm, D)] + out_specs
    return pl.pallas_call(body, name=name, grid=(S // tm,), in_specs=in_specs, out_specs=out_specs,
                          out_shape=outs, compiler_params=_cparams(("parallel",)))(*ins)


def _norm_bwd(dh, xin, rstd, nw, sc, dres, mo, gt, name, tm=256):
    S = xin.shape[0]
    has_res = mo is not None

    def body(*refs):
        if has_res:
            dh_ref, x_ref, rs_ref, nw_ref, sc_ref, dres_ref, mo_ref, gt_ref, dx_ref, dsh_ref, dsc_ref, dnw_ref, dmo_ref, dgt_ref = refs
        else:
            dh_ref, x_ref, rs_ref, nw_ref, sc_ref, dres_ref, dx_ref, dsh_ref, dsc_ref, dnw_ref = refs
        first = pl.program_id(0) == 0
        dh = dh_ref[...]
        rstd = rs_ref[...]
        n = x_ref[...] * rstd
        w = nw_ref[...]
        _acc(dsh_ref, _colsum(dh), first)
        _acc(dsc_ref, _colsum(dh * (n * w)), first)
        dnw = dh * (1.0 + sc_ref[...])
        _acc(dnw_ref, _colsum(dnw * n), first)
        dn = dnw * w
        dx = dres_ref[...] + rstd * (dn - n * jnp.mean(dn * n, axis=-1, keepdims=True))
        dx_ref[...] = dx
        if has_res:
            dmo_ref[...] = (dx * gt_ref[...]).astype(BF16)
            _acc(dgt_ref, _colsum(dx * mo_ref[...]), first)

    vec = _full((1, D))
    vshape = jax.ShapeDtypeStruct((1, D), F32)
    ins = [dh, xin, rstd, nw, sc, dres] + ([mo, gt] if has_res else [])
    in_specs = [_rows(tm, D), _rows(tm, D), _rows(tm, 1), vec, vec, _rows(tm, D)] + ([_rows(tm, D), vec] if has_res else [])
    outs = [jax.ShapeDtypeStruct((S, D), F32), vshape, vshape, vshape]
    out_specs = [_rows(tm, D), vec, vec, vec]
    if has_res:
        outs += [jax.ShapeDtypeStruct((S, D), BF16), vshape]
        out_specs += [_rows(tm, D), vec]
    return pl.pallas_call(body, name=name, grid=(S // tm,), in_specs=in_specs, out_specs=out_specs,
                          out_shape=outs, compiler_params=_cparams(("arbitrary",)))(*ins)


def _final(x2, f, gt2, nfw, target, tm=256):
    S = x2.shape[0]

    def body(x2_ref, f_ref, gt_ref, w_ref, t_ref, loss_ref, dx_ref, df_ref, dgt_ref, dw_ref):
        first = pl.program_id(0) == 0
        f = f_ref[...]
        gt = gt_ref[...]
        w = w_ref[...]
        x3 = x2_ref[...] + gt * f
        rstd = lax.rsqrt(jnp.mean(x3 * x3, axis=-1, keepdims=True) + RMS_EPS)
        n = x3 * rstd
        e = n * w - t_ref[...]
        part = 0.5 * jnp.sum(jnp.mean(e * e, axis=-1, keepdims=True), axis=0, keepdims=True)
        _acc(loss_ref, jnp.broadcast_to(part, (SUBLANES, LANES)), first)
        dy = e * (1.0 / D)
        _acc(dw_ref, _colsum(dy * n), first)
        dn = dy * w
        dx = rstd * (dn - n * jnp.mean(dn * n, axis=-1, keepdims=True))
        dx_ref[...] = dx
        df_ref[...] = (dx * gt).astype(BF16)
        _acc(dgt_ref, _colsum(dx * f), first)

    vec = _full((1, D))
    vshape = jax.ShapeDtypeStruct((1, D), F32)
    return pl.pallas_call(
        body, name="final_loss", grid=(S // tm,),
        in_specs=[_rows(tm, D), _rows(tm, D), vec, vec, _rows(tm, D)],
        out_specs=[_full((SUBLANES, LANES)), _rows(tm, D), _rows(tm, D), vec, vec],
        out_shape=[jax.ShapeDtypeStruct((SUBLANES, LANES), F32), jax.ShapeDtypeStruct((S, D), F32),
                   jax.ShapeDtypeStruct((S, D), BF16), vshape, vshape],
        compiler_params=_cparams(("arbitrary",)))(x2, f, gt2, nfw, target)


def _gate_fwd(P, bga, bgr, y_att, y_rwkv, tm=256):
    S = P.shape[0]

    def body(pa_ref, pr_ref, ba_ref, br_ref, ya_ref, yr_ref, mix_ref):
        ga = jax.nn.sigmoid(pa_ref[...] + ba_ref[...])
        gr = jax.nn.sigmoid(pr_ref[...] + br_ref[...])
        mix_ref[...] = (ga * ya_ref[...] + gr * yr_ref[...]).astype(BF16)

    vec = _full((1, D))
    return pl.pallas_call(
        body, name="gate_fwd", grid=(S // tm,),
        in_specs=[_rows(tm, D, C_GA // D), _rows(tm, D, C_GR // D), vec, vec, _rows(tm, D), _rows(tm, D)],
        out_specs=_rows(tm, D), out_shape=jax.ShapeDtypeStruct((S, D), BF16),
        compiler_params=_cparams(("parallel",)))(P, P, bga, bgr, y_att, y_rwkv)


def _gate_bwd(dmix, P, bga, bgr, y_att, y_rwkv, tm=256):
    S = P.shape[0]

    def body(dm_ref, pa_ref, pr_ref, ba_ref, br_ref, ya_ref, yr_ref, dya_ref, dyr_ref, dpa_ref, dpr_ref, dba_ref, dbr_ref):
        first = pl.program_id(0) == 0
        dm = dm_ref[...]
        ga = jax.nn.sigmoid(pa_ref[...] + ba_ref[...])
        gr = jax.nn.sigmoid(pr_ref[...] + br_ref[...])
        dya_ref[...] = (dm * ga).astype(BF16)
        dyr_ref[...] = (dm * gr).astype(BF16)
        dpa = dm * ya_ref[...] * ga * (1.0 - ga)
        dpr = dm * yr_ref[...] * gr * (1.0 - gr)
        dpa_ref[...] = dpa.astype(BF16)
        dpr_ref[...] = dpr.astype(BF16)
        _acc(dba_ref, _colsum(dpa), first)
        _acc(dbr_ref, _colsum(dpr), first)

    vec = _full((1, D))
    row = _rows(tm, D)
    rshape = jax.ShapeDtypeStruct((S, D), BF16)
    vshape = jax.ShapeDtypeStruct((1, D), F32)
    return pl.pallas_call(
        body, name="gate_bwd", grid=(S // tm,),
        in_specs=[row, _rows(tm, D, C_GA // D), _rows(tm, D, C_GR // D), vec, vec, row, row],
        out_specs=[row, row, row, row, vec, vec],
        out_shape=[rshape, rshape, rshape, rshape, vshape, vshape],
        compiler_params=_cparams(("arbitrary",)))(dmix, P, P, bga, bgr, y_att, y_rwkv)


CONV_TN = D_FF // 2


def _conv_fwd(u, conv_w8, conv_b, tm=256, tn=CONV_TN):
    S = u.shape[0]
    nj = D_FF // tn

    def conv(u_ref, h_ref, w_ref, b_ref, first):
        u = u_ref[...]
        h = h_ref[...]
        w = w_ref[...]
        return b_ref[...] + w[0:1] * _shift_down(u, h, 2, first) + w[1:2] * _shift_down(u, h, 1, first) + w[2:3] * u

    def body(ug_ref, hg_ref, uv_ref, hv_ref, wg_ref, wv_ref, bg_ref, bv_ref, act_ref):
        first = pl.program_id(0) == 0
        g = conv(ug_ref, hg_ref, wg_ref, bg_ref, first)
        v = conv(uv_ref, hv_ref, wv_ref, bv_ref, first)
        act_ref[...] = (g * jax.nn.sigmoid(g) * v).astype(BF16)

    blk = lambda off: pl.BlockSpec((tm, tn), lambda i, j: (i, j + off))
    halo = lambda off: pl.BlockSpec((SUBLANES, tn), lambda i, j: (jnp.maximum(i * (tm // SUBLANES) - 1, 0), j + off))
    wsp = lambda off: pl.BlockSpec((SUBLANES, tn), lambda i, j: (0, j + off))
    bsp = lambda off: pl.BlockSpec((1, tn), lambda i, j: (0, j + off))
    return pl.pallas_call(
        body, name="conv_fwd", grid=(S // tm, nj),
        in_specs=[blk(0), halo(0), blk(nj), halo(nj), wsp(0), wsp(nj), bsp(0), bsp(nj)],
        out_specs=pl.BlockSpec((tm, tn), lambda i, j: (i, j)),
        out_shape=jax.ShapeDtypeStruct((S, D_FF), BF16),
        compiler_params=_cparams(("parallel", "parallel")))(u, u, u, u, conv_w8, conv_w8, conv_b, conv_b)


def _conv_bwd_a(dact, u, conv_w8, conv_b, tm=256, tn=CONV_TN):
    S = u.shape[0]
    nj = D_FF // tn

    def half(u_ref, h_ref, w_ref, b_ref, first):
        u = u_ref[...]
        h = h_ref[...]
        w = w_ref[...]
        u2, u1 = _shift_down(u, h, 2, first), _shift_down(u, h, 1, first)
        return b_ref[...] + w[0:1] * u2 + w[1:2] * u1 + w[2:3] * u, (u2, u1, u)

    def wgrad(d, taps):
        z = jnp.zeros((SUBLANES - 3, d.shape[1]), F32)
        return jnp.concatenate([_colsum(d * taps[0]), _colsum(d * taps[1]), _colsum(d * taps[2]), z], axis=0)

    def body(da_ref, ug_ref, hg_ref, uv_ref, hv_ref, wg_ref, wv_ref, bg_ref, bv_ref,
             d_ref, dwg_ref, dwv_ref, dbg_ref, dbv_ref):
        first = pl.program_id(1) == 0
        g, tg = half(ug_ref, hg_ref, wg_ref, bg_ref, first)
        v, tv = half(uv_ref, hv_ref, wv_ref, bv_ref, first)
        da = da_ref[...].astype(F32)
        sg = jax.nn.sigmoid(g)
        dg = da * v * (sg * (1.0 + g * (1.0 - sg)))
        dv = da * (g * sg)
        d_ref[0] = dg
        d_ref[1] = dv
        _acc(dwg_ref, wgrad(dg, tg), first)
        _acc(dwv_ref, wgrad(dv, tv), first)
        _acc(dbg_ref, _colsum(dg), first)
        _acc(dbv_ref, _colsum(dv), first)

    blk = lambda off: pl.BlockSpec((tm, tn), lambda j, i: (i, j + off))
    halo = lambda off: pl.BlockSpec((SUBLANES, tn), lambda j, i: (jnp.maximum(i * (tm // SUBLANES) - 1, 0), j + off))
    wsp = lambda off: pl.BlockSpec((SUBLANES, tn), lambda j, i: (0, j + off))
    bsp = lambda off: pl.BlockSpec((1, tn), lambda j, i: (0, j + off))
    f = jax.ShapeDtypeStruct
    outs = pl.pallas_call(
        body, name="conv_bwd_a", grid=(nj, S // tm),
        in_specs=[pl.BlockSpec((tm, tn), lambda j, i: (i, j)), blk(0), halo(0), blk(nj), halo(nj), wsp(0), wsp(nj), bsp(0), bsp(nj)],
        out_specs=[pl.BlockSpec((2, tm, tn), lambda j, i: (0, i, j)),
                   pl.BlockSpec((SUBLANES, tn), lambda j, i: (0, j)), pl.BlockSpec((SUBLANES, tn), lambda j, i: (0, j)),
                   pl.BlockSpec((1, tn), lambda j, i: (0, j)), pl.BlockSpec((1, tn), lambda j, i: (0, j))],
        out_shape=[f((2, S, D_FF), F32), f((SUBLANES, D_FF), F32), f((SUBLANES, D_FF), F32),
                   f((1, D_FF), F32), f((1, D_FF), F32)],
        compiler_params=_cparams(("parallel", "arbitrary")))(dact, u, u, u, u, conv_w8, conv_w8, conv_b, conv_b)
    return outs


def _conv_bwd_b(duc, conv_w8, tm=256, tn=CONV_TN):
    _, S, W = duc.shape
    nj = W // tn
    n_rows = S // tm

    def body(d_ref, h_ref, w_ref, o_ref):
        last = pl.program_id(0) == n_rows - 1
        d = d_ref[...]
        h = h_ref[...]
        w = w_ref[...]
        o_ref[...] = (w[2:3] * d + w[1:2] * _shift_up(d, h, 1, last) + w[0:1] * _shift_up(d, h, 2, last)).astype(BF16)

    last_tile = S // SUBLANES - 1
    return pl.pallas_call(
        body, name="conv_bwd_b", grid=(n_rows, 2 * nj),
        in_specs=[pl.BlockSpec((None, tm, tn), lambda i, j: (j // nj, i, j % nj)),
                  pl.BlockSpec((None, SUBLANES, tn), lambda i, j: (j // nj, jnp.minimum((i + 1) * (tm // SUBLANES), last_tile), j % nj)),
                  pl.BlockSpec((SUBLANES, tn), lambda i, j: (0, j))],
        out_specs=pl.BlockSpec((tm, tn), lambda i, j: (i, j)),
        out_shape=jax.ShapeDtypeStruct((S, 2 * W), BF16),
        compiler_params=_cparams(("parallel", "parallel")))(duc, duc, conv_w8)


ATT_SCALE = HEAD ** -0.5
NEG = -1e30
ATT_PAIRS = ATT_HEADS // 2


def _att_rows(n, d, S):
    per = S // (QBLK * d)
    r, m = n // per, n % per
    cur = pl.ds(m * (QBLK * d) + r, QBLK, stride=d)
    prv = pl.ds(jnp.maximum(m - 1, 0) * (QBLK * d) + r, QBLK, stride=d)
    return cur, prv, m > 0


def _att_slab(g, j):
    return (C_ATT + g * 3 * ATT_W + j * ATT_W) // LANES


def _heads(x):
    return x[:, 0:HEAD], x[:, HEAD:2 * HEAD]


def _att_scores(q, kc, kp, has_prev):
    qi = lax.broadcasted_iota(jnp.int32, (QBLK, QBLK), 0)
    kj = lax.broadcasted_iota(jnp.int32, (QBLK, QBLK), 1)
    nt = (((1,), (1,)), ((), ()))
    s_c = lax.dot_general(q, kc, nt, preferred_element_type=F32) * ATT_SCALE
    s_p = lax.dot_general(q, kp, nt, preferred_element_type=F32) * ATT_SCALE
    s_c = jnp.where(kj <= qi, s_c, NEG)
    s_p = jnp.where(jnp.logical_and(kj >= qi, has_prev), s_p, NEG)
    return s_c, s_p


def _att_fwd(P, g):
    S = P.shape[0]
    d = ATT_PATTERNS[g][1]

    def body(q_ref, k_ref, v_ref, o_ref, l_ref):
        def blk(n, carry):
            cur, prv, has_prev = _att_rows(n, d, S)
            q2, kc2, kp2 = q_ref[cur, :].astype(BF16), k_ref[cur, :].astype(BF16), k_ref[prv, :].astype(BF16)
            vc2, vp2 = v_ref[cur, :].astype(BF16), v_ref[prv, :].astype(BF16)
            outs, lses = [], []
            for q, kc, kp, vc, vp in zip(_heads(q2), _heads(kc2), _heads(kp2), _heads(vc2), _heads(vp2)):
                s_c, s_p = _att_scores(q, kc, kp, has_prev)
                m = jnp.maximum(jnp.max(s_c, axis=1, keepdims=True), jnp.max(s_p, axis=1, keepdims=True))
                p_c = jnp.exp(s_c - m)
                p_p = jnp.exp(s_p - m)
                den = jnp.sum(p_c, axis=1, keepdims=True) + jnp.sum(p_p, axis=1, keepdims=True)
                num = (jnp.dot(p_c.astype(BF16), vc, preferred_element_type=F32)
                       + jnp.dot(p_p.astype(BF16), vp, preferred_element_type=F32))
                outs.append(num / den)
                lses.append(jnp.broadcast_to(m + jnp.log(den), (QBLK, HEAD)))
            o_ref[cur, :] = jnp.concatenate(outs, axis=1)
            l_ref[cur, :] = jnp.concatenate(lses, axis=1)
            return carry

        lax.fori_loop(0, S // QBLK, blk, 0, unroll=2)

    slab = lambda j: pl.BlockSpec((S, LANES), lambda i: (0, _att_slab(g, j) + i))
    out = pl.BlockSpec((S, LANES), lambda i: (0, i))
    shp = jax.ShapeDtypeStruct((S, ATT_W), F32)
    return pl.pallas_call(body, name=f"att_fwd_g{g}", grid=(ATT_PAIRS,), in_specs=[slab(0), slab(1), slab(2)],
                          out_specs=[out, out], out_shape=[shp, shp], compiler_params=_cparams(("parallel",)))(P, P, P)


def _att_bwd(P, o, l, do, dl, g):
    S = P.shape[0]
    d = ATT_PATTERNS[g][1]
    tn = (((0,), (0,)), ((), ()))
    nt = (((1,), (1,)), ((), ()))

    def body(q_ref, k_ref, v_ref, o_ref, l_ref, do_ref, dl_ref, dq_ref, dk_ref, dv_ref, dq_acc, dk_acc, dv_acc):
        dk_acc[...] = jnp.zeros_like(dk_acc)
        dv_acc[...] = jnp.zeros_like(dv_acc)

        def blk(n, carry):
            cur, prv, has_prev = _att_rows(n, d, S)
            q2, kc2, kp2 = q_ref[cur, :].astype(BF16), k_ref[cur, :].astype(BF16), k_ref[prv, :].astype(BF16)
            vc2, vp2 = v_ref[cur, :].astype(BF16), v_ref[prv, :].astype(BF16)
            do2 = do_ref[cur, :]
            dd2 = do2 * o_ref[cur, :] - dl_ref[cur, :]
            l2 = l_ref[cur, :]
            res = []
            for q, kc, kp, vc, vp, dob, dd, lse in zip(_heads(q2), _heads(kc2), _heads(kp2), _heads(vc2), _heads(vp2),
                                                     _heads(do2), _heads(dd2), _heads(l2)):
                s_c, s_p = _att_scores(q, kc, kp, has_prev)
                p_c = jnp.exp(s_c - lse[:, 0:1])
                p_p = jnp.exp(s_p - lse[:, 0:1])
                delta = jnp.sum(dd, axis=1, keepdims=True)
                dob16 = dob.astype(BF16)
                dp_c = lax.dot_general(dob16, vc, nt, preferred_element_type=F32)
                dp_p = lax.dot_general(dob16, vp, nt, preferred_element_type=F32)
                ds_c = (p_c * (dp_c - delta) * ATT_SCALE).astype(BF16)
                ds_p = (p_p * (dp_p - delta) * ATT_SCALE).astype(BF16)
                res.append((
                    jnp.dot(ds_c, kc, preferred_element_type=F32) + jnp.dot(ds_p, kp, preferred_element_type=F32),
                    lax.dot_general(ds_c, q, tn, preferred_element_type=F32),
                    lax.dot_general(ds_p, q, tn, preferred_element_type=F32),
                    lax.dot_general(p_c.astype(BF16), dob16, tn, preferred_element_type=F32),
                    lax.dot_general(p_p.astype(BF16), dob16, tn, preferred_element_type=F32)))
            both = [jnp.concatenate([res[0][i], res[1][i]], axis=1) for i in range(5)]
            dq_acc[cur, :] = both[0]
            dk_acc[cur, :] += both[1]
            dv_acc[cur, :] += both[3]
            dk_acc[prv, :] += both[2]
            dv_acc[prv, :] += both[4]
            return carry

        lax.fori_loop(0, S // QBLK, blk, 0, unroll=2)
        dq_ref[...] = dq_acc[...].astype(BF16)
        dk_ref[...] = dk_acc[...].astype(BF16)
        dv_ref[...] = dv_acc[...].astype(BF16)

    slab = lambda j: pl.BlockSpec((S, LANES), lambda i: (0, _att_slab(g, j) + i))
    blk128 = pl.BlockSpec((S, LANES), lambda i: (0, i))
    shp = jax.ShapeDtypeStruct((S, ATT_W), BF16)
    return pl.pallas_call(body, name=f"att_bwd_g{g}", grid=(ATT_PAIRS,),
                          in_specs=[slab(0), slab(1), slab(2)] + [blk128] * 4, out_specs=[blk128] * 3, out_shape=[shp] * 3,
                          scratch_shapes=[pltpu.VMEM((S, LANES), F32)] * 3,
                          compiler_params=_cparams(("parallel",)))(P, P, P, o, l, do, dl)


def _att_weights(l_refs):
    l0, l1, l2 = [r[...] for r in l_refs]
    m = jnp.maximum(jnp.maximum(l0, l1), l2)
    e = (jnp.exp(l0 - m), jnp.exp(l1 - m), jnp.exp(l2 - m))
    inv = 1.0 / (e[0] + e[1] + e[2])
    return [x * inv for x in e]


def _att_combine_fwd(os, ls, tm=512):
    S = os[0].shape[0]

    def body(o0, o1, o2, l0, l1, l2, a_ref):
        w = _att_weights((l0, l1, l2))
        a_ref[...] = (w[0] * o0[...] + w[1] * o1[...] + w[2] * o2[...]).astype(BF16)

    row = _rows(tm, ATT_W)
    return pl.pallas_call(body, name="att_combine_fwd", grid=(S // tm,), in_specs=[row] * 6, out_specs=row,
                          out_shape=jax.ShapeDtypeStruct((S, ATT_W), BF16),
                          compiler_params=_cparams(("parallel",)))(*os, *ls)


def _att_combine_bwd(da, os, ls, tm=512):
    S = da.shape[0]

    def body(da_ref, o0, o1, o2, l0, l1, l2, *out_refs):
        da = da_ref[...]
        w = _att_weights((l0, l1, l2))
        dw = (da * o0[...], da * o1[...], da * o2[...])
        mean = w[0] * dw[0] + w[1] * dw[1] + w[2] * dw[2]
        for g in range(3):
            out_refs[g][...] = w[g] * da
            out_refs[3 + g][...] = w[g] * (dw[g] - mean)

    row = _rows(tm, ATT_W)
    shp = jax.ShapeDtypeStruct((S, ATT_W), F32)
    return pl.pallas_call(body, name="att_combine_bwd", grid=(S // tm,), in_specs=[row] * 7, out_specs=[row] * 6,
                          out_shape=[shp] * 6, compiler_params=_cparams(("parallel",)))(da, *os, *ls)


@jax.custom_vjp
def _bdot(a, b):
    return jnp.dot(a.astype(BF16), b.astype(BF16), preferred_element_type=F32)


def _bdot_fwd(a, b):
    return _bdot(a, b), (a, b)


def _bdot_bwd(res, ct):
    a, b = res
    ct16 = ct.astype(BF16)
    da = lax.dot_general(ct16, b.astype(BF16), (((1,), (1,)), ((), ())), preferred_element_type=F32)
    db = lax.dot_general(a.astype(BF16), ct16, (((0,), (0,)), ((), ())), preferred_element_type=F32)
    return da, db


_bdot.defvjp(_bdot_fwd, _bdot_bwd)


def _two_piece_dot(x, m):
    hi = x.astype(BF16)
    lo = (x - hi.astype(F32)).astype(BF16)
    return jnp.dot(hi, m, preferred_element_type=F32) + jnp.dot(lo, m, preferred_element_type=F32)


def _head_sum_impl(x):
    sel = (lax.broadcasted_iota(jnp.int32, (D, LANES), 0) // HEAD == lax.broadcasted_iota(jnp.int32, (D, LANES), 1)).astype(BF16)
    sel_t = (lax.broadcasted_iota(jnp.int32, (LANES, D), 1) // HEAD == lax.broadcasted_iota(jnp.int32, (LANES, D), 0)).astype(BF16)
    return _two_piece_dot(_two_piece_dot(x, sel), sel_t)


@jax.custom_vjp
def _head_sum(x):
    return _head_sum_impl(x)


_head_sum.defvjp(lambda x: (_head_sum_impl(x), None), lambda _, ct: (_head_sum_impl(ct),))


def _softplus(z):
    return jnp.maximum(z, 0.0) + jnp.log(1.0 + jnp.exp(-jnp.abs(z)))


def _rwkv_prep_fn(zr, zrp, zk, zkp, zv, zvp, zl, zlp, mu_r, mu_k, mu_v, mu_l, w0, a0, k_k, k_a, w2, a2, g2p):
    r = zr + (zrp - zr) * mu_r
    k = zk + (zkp - zk) * mu_k
    v = zv + (zvp - zv) * mu_v
    lo = zl + (zlp - zl) * mu_l
    w_low, a_low, g_low = lo[:, 0:LORA_W], lo[:, LORA_W:LORA_W + LORA_A], lo[:, LANES:LANES + G_PAD]
    w_log = -_softplus(-(w0 + _bdot(jnp.tanh(w_low), w2))) - 0.5
    decay = jnp.exp(-jnp.exp(w_log))
    a = jax.nn.sigmoid(a0 + _bdot(a_low, a2))
    g = _bdot(jax.nn.sigmoid(g_low), g2p)
    kmod = k * (1.0 + (a - 1.0) * k_a)
    kk = k * k_k
    kk = kk / jnp.maximum(jnp.sqrt(_head_sum(kk * kk)), 1e-12)
    return r, decay, kmod, v, -kk, kk * a, g


def _rwkv_prep_specs(tm):
    vec = _full((1, D))
    slabs = []
    for col in (C_R // D, C_K // D, C_V // D):
        slabs += [_rows(tm, D, col), _prev8(tm, D, col)]
    slabs += [_rows(tm, LORA_PAD, C_LORA // LORA_PAD), _prev8(tm, LORA_PAD, C_LORA // LORA_PAD)]
    params = [vec, vec, vec, _full((1, LORA_PAD)), vec, vec, vec, vec,
              _full((LORA_W, D)), _full((LORA_A, D)), _full((G_PAD, D))]
    return slabs, params


def _prep_inputs(refs, first):
    vals = []
    for s in range(4):
        z = refs[2 * s][...]
        vals += [z, _shift_down(z, refs[2 * s + 1][...], 1, first)]
    return vals + [r[...] for r in refs[8:19]]


def _rwkv_prep(P, params, tm=256):
    S = P.shape[0]
    slabs, pspecs = _rwkv_prep_specs(tm)

    def body(*refs):
        outs = _rwkv_prep_fn(*_prep_inputs(refs, pl.program_id(0) == 0))
        for o_ref, val in zip(refs[19:], outs):
            o_ref[...] = val

    shp = jax.ShapeDtypeStruct((S, D), F32)
    return pl.pallas_call(body, name="rwkv_prep", grid=(S // tm,), in_specs=slabs + pspecs,
                          out_specs=[_rows(tm, D)] * 7, out_shape=[shp] * 7,
                          compiler_params=_cparams(("parallel",)))(*([P] * 8), *params)


def _rwkv_prep_bwd(P, params, cts_a, cts_b, tm=128):
    S = P.shape[0]
    slabs, pspecs = _rwkv_prep_specs(tm)
    has_b = [c is not None for c in cts_b]
    n_ct = 7 + sum(has_b)

    def body(*refs):
        first = pl.program_id(0) == 0
        ins = _prep_inputs(refs, first)
        ct_refs = refs[19:19 + n_ct]
        out_refs = refs[19 + n_ct:]
        cts, pos = [], 7
        for i in range(7):
            c = ct_refs[i][...]
            if has_b[i]:
                c = c + ct_refs[pos][...]
                pos += 1
            cts.append(c)
        _, vjp = jax.vjp(_rwkv_prep_fn, *ins)
        grads = vjp(tuple(cts))
        for s in range(4):
            out_refs[s][...] = grads[2 * s]
            out_refs[4 + s][...] = grads[2 * s + 1]
        for i in range(11):
            _acc(out_refs[8 + i], grads[8 + i], first)

    ct_in = list(cts_a) + [c for c in cts_b if c is not None]
    row, lrow = _rows(tm, D), _rows(tm, LORA_PAD)
    f = jax.ShapeDtypeStruct
    zshapes = [f((S, D), F32)] * 3 + [f((S, LORA_PAD), F32)]
    pshapes = [f((1, D), F32)] * 3 + [f((1, LORA_PAD), F32)] + [f((1, D), F32)] * 4 + [f((LORA_W, D), F32), f((LORA_A, D), F32), f((G_PAD, D), F32)]
    return pl.pallas_call(
        body, name="rwkv_prep_bwd", grid=(S // tm,),
        in_specs=slabs + pspecs + [row] * n_ct,
        out_specs=[row, row, row, lrow] * 2 + pspecs,
        out_shape=zshapes * 2 + pshapes,
        compiler_params=_cparams(("arbitrary",)))(*([P] * 8), *params, *ct_in)


def _shift_add(a, b, tm=256):
    S, W = a.shape

    def body(a_ref, b_ref, h_ref, o_ref):
        last = pl.program_id(0) == pl.num_programs(0) - 1
        o_ref[...] = (a_ref[...] + _shift_up(b_ref[...], h_ref[...], 1, last)).astype(BF16)

    return pl.pallas_call(body, name="shift_add", grid=(S // tm,),
                          in_specs=[_rows(tm, W), _rows(tm, W), _next8(tm, W, S)],
                          out_specs=_rows(tm, W), out_shape=jax.ShapeDtypeStruct((S, W), BF16),
                          compiler_params=_cparams(("parallel",)))(a, b, b)


def _rwkv_post_fn(y, r, kmod, v, g, lnx_w, lnx_b, r_k):
    mean = _head_sum(y) * (1.0 / HEAD)
    yc = y - mean
    var = _head_sum(yc * yc) * (1.0 / HEAD)
    yn = yc * lax.rsqrt(var + GN_EPS) * lnx_w + lnx_b
    bonus = _head_sum(r * kmod * r_k) * v
    return (yn + bonus) * g


def _rwkv_post(y, r, kmod, v, g, lnx_w, lnx_b, r_k, tm=256):
    S = y.shape[0]

    def body(y_ref, r_ref, k_ref, v_ref, g_ref, w_ref, b_ref, rk_ref, o_ref):
        o_ref[...] = _rwkv_post_fn(y_ref[...], r_ref[...], k_ref[...], v_ref[...], g_ref[...],
                                   w_ref[...], b_ref[...], rk_ref[...]).astype(BF16)

    row, vec = _rows(tm, D), _full((1, D))
    return pl.pallas_call(body, name="rwkv_post", grid=(S // tm,), in_specs=[row] * 5 + [vec] * 3, out_specs=row,
                          out_shape=jax.ShapeDtypeStruct((S, D), BF16),
                          compiler_params=_cparams(("parallel",)))(y, r, kmod, v, g, lnx_w, lnx_b, r_k)


def _rwkv_post_bwd(drw, y, r, kmod, v, g, lnx_w, lnx_b, r_k, tm=256):
    S = y.shape[0]

    def body(d_ref, y_ref, r_ref, k_ref, v_ref, g_ref, w_ref, b_ref, rk_ref, *out_refs):
        first = pl.program_id(0) == 0
        _, vjp = jax.vjp(_rwkv_post_fn, y_ref[...], r_ref[...], k_ref[...], v_ref[...], g_ref[...],
                         w_ref[...], b_ref[...], rk_ref[...])
        grads = vjp(d_ref[...])
        for i in range(5):
            out_refs[i][...] = grads[i]
        for i in range(5, 8):
            _acc(out_refs[i], grads[i], first)

    row, vec = _rows(tm, D), _full((1, D))
    f = jax.ShapeDtypeStruct
    return pl.pallas_call(body, name="rwkv_post_bwd", grid=(S // tm,), in_specs=[row] * 6 + [vec] * 3,
                          out_specs=[row] * 5 + [vec] * 3, out_shape=[f((S, D), F32)] * 5 + [f((1, D), F32)] * 3,
                          compiler_params=_cparams(("arbitrary",)))(drw, y, r, kmod, v, g, lnx_w, lnx_b, r_k)


def _col(tile, ii, lane_lo):
    a = jnp.broadcast_to(tile[0:HEAD, ii:ii + 1], (HEAD, LANES))
    b = jnp.broadcast_to(tile[HEAD:2 * HEAD, ii:ii + 1], (HEAD, LANES))
    return jnp.where(lane_lo, a, b)


SCAN_NG = SCAN_TB // SUBLANES
N_PIECES = 2


def _scan_sources(lane_refs, mxu_refs, t_ref, p_ref):
    for o, ref in enumerate(lane_refs):
        t_ref[o] = ref[...].T
    for o, ref in enumerate(mxu_refs):
        rest = ref[...]
        for p in range(N_PIECES):
            piece = rest.astype(BF16).astype(F32)
            rest = rest - piece
            p_ref[o, p] = piece


def _gen_tiles(buf_ref, g, n_lane, n_mxu, t_ref, p_ref):
    lane_lo = lax.broadcasted_iota(jnp.int32, (HEAD, LANES), 1) < HEAD
    tiles = [pltpu.roll(t_ref[o], (LANES - SUBLANES * g) % LANES, 1) for o in range(n_lane)]
    for ii in range(SUBLANES):
        for o in range(n_lane):
            buf_ref[ii, o] = _col(tiles[o], ii, lane_lo)
    if n_mxu == 0:
        return
    diag = (lax.broadcasted_iota(jnp.int32, (HEAD, LANES), 1) % HEAD
            == lax.broadcasted_iota(jnp.int32, (HEAD, LANES), 0)).astype(BF16)
    ones = (lax.broadcasted_iota(jnp.int32, (LANES, LANES), 0) // HEAD
            == lax.broadcasted_iota(jnp.int32, (LANES, LANES), 1) // HEAD).astype(BF16)
    start = pl.multiple_of(g * SUBLANES, SUBLANES)
    for o in range(n_mxu):
        cols = None
        for p in range(N_PIECES):
            rows = p_ref[o, p, pl.ds(start, SUBLANES), :].astype(BF16)
            lhs = jnp.concatenate([jnp.broadcast_to(rows[ii:ii + 1], (HEAD, LANES)) * diag for ii in range(SUBLANES)], axis=0)
            part = jnp.dot(lhs, ones, preferred_element_type=F32)
            cols = part if cols is None else cols + part
        for ii in range(SUBLANES):
            buf_ref[ii, n_lane + o] = cols[ii * HEAD:(ii + 1) * HEAD]


def _scan_scratch(n_lane, n_mxu):
    tiles = pltpu.VMEM((SUBLANES, n_lane + n_mxu, HEAD, LANES), F32)
    return [pltpu.VMEM((HEAD, LANES), F32), pltpu.VMEM((max(n_lane, 1), LANES, SCAN_TB), F32),
            pltpu.VMEM((max(n_mxu, 1), N_PIECES, SCAN_TB, LANES), F32), tiles, tiles]


def _scan_fwd(r, w, k, v, a, b):
    S = r.shape[0]
    nblk = S // SCAN_TB
    npair = N_HEADS // 2

    def body(r_ref, w_ref, k_ref, v_ref, a_ref, b_ref, y_ref, sall_ref, s_ref, t_ref, p_ref, buf0, buf1):
        @pl.when(pl.program_id(1) == 0)
        def _():
            s_ref[...] = jnp.zeros_like(s_ref)

        _scan_sources((w_ref,), (r_ref, k_ref, a_ref, b_ref), t_ref, p_ref)
        gen = functools.partial(_gen_tiles, n_lane=1, n_mxu=4, t_ref=t_ref, p_ref=p_ref)

        def steps(buf, g, st):
            for ii in range(SUBLANES):
                t = g * SUBLANES + ii
                wc, rc, kc, ac, bc = [buf[ii, o] for o in range(5)]
                sall_ref[t] = st
                sa = jnp.sum(st * ac, axis=0, keepdims=True)
                st = st * wc + bc * sa + kc * v_ref[pl.ds(t, 1), :]
                y_ref[pl.ds(t, 1), :] = jnp.sum(st * rc, axis=0, keepdims=True)
            return st

        gen(buf0, 0)

        def two_groups(i, st):
            g = 2 * i
            gen(buf1, g + 1)
            st = steps(buf0, g, st)
            gen(buf0, jnp.minimum(g + 2, SCAN_NG - 1))
            return steps(buf1, g + 1, st)

        s_ref[...] = lax.fori_loop(0, SCAN_NG // 2, two_groups, s_ref[...])

    blk = pl.BlockSpec((SCAN_TB, LANES), lambda p, i: (i, p))
    return pl.pallas_call(
        body, name="scan_fwd", grid=(npair, nblk), in_specs=[blk] * 6,
        out_specs=[blk, pl.BlockSpec((SCAN_TB, None, HEAD, LANES), lambda p, i: (i, p, 0, 0))],
        out_shape=[jax.ShapeDtypeStruct((S, D), F32), jax.ShapeDtypeStruct((S, npair, HEAD, LANES), F32)],
        scratch_shapes=_scan_scratch(1, 4),
        compiler_params=_cparams(("parallel", "arbitrary")))(r, w, k, v, a, b)


def _scan_bwd(r, w, k, v, a, b, sall, dy):
    S = r.shape[0]
    nblk = S // SCAN_TB
    npair = N_HEADS // 2
    NG = SCAN_TB // SUBLANES
    nt = (((1,), (1,)), ((), ()))

    def body(r_ref, w_ref, k_ref, v_ref, a_ref, b_ref, sall_ref, dy_ref,
             dr_ref, dw_ref, dk_ref, dv_ref, da_ref, db_ref, ds_ref, t_ref, p_ref, buf0, buf1):
        @pl.when(pl.program_id(1) == 0)
        def _():
            ds_ref[...] = jnp.zeros_like(ds_ref)

        _scan_sources((w_ref, r_ref, k_ref), (a_ref, b_ref), t_ref, p_ref)
        gen = functools.partial(_gen_tiles, n_lane=3, n_mxu=2, t_ref=t_ref, p_ref=p_ref)
        half_sel = (lax.broadcasted_iota(jnp.int32, (SUBLANES, LANES), 0)
                    == lax.broadcasted_iota(jnp.int32, (SUBLANES, LANES), 1) // HEAD).astype(BF16)

        def key_grad(ref, t, z):
            res = lax.dot_general(half_sel, z.astype(BF16), nt, preferred_element_type=F32)
            ref[pl.ds(t, 1), 0:HEAD] = res[0:1]
            ref[pl.ds(t, 1), HEAD:2 * HEAD] = res[1:2]

        def steps(buf, g, dst):
            for ii in reversed(range(SUBLANES)):
                t = g * SUBLANES + ii
                wc, rc, kc, ac, bc = [buf[ii, o] for o in range(5)]
                vrow = v_ref[pl.ds(t, 1), :]
                dyrow = dy_ref[pl.ds(t, 1), :]
                sp = sall_ref[t]
                sa = jnp.sum(sp * ac, axis=0, keepdims=True)
                sn = sp * wc + bc * sa + kc * vrow
                dsn = dst + rc * dyrow
                dv_ref[pl.ds(t, 1), :] = jnp.sum(dsn * kc, axis=0, keepdims=True)
                dsa = jnp.sum(dsn * bc, axis=0, keepdims=True)
                key_grad(dr_ref, t, sn * dyrow)
                key_grad(dw_ref, t, dsn * sp)
                key_grad(dk_ref, t, dsn * vrow)
                key_grad(da_ref, t, sp * dsa)
                key_grad(db_ref, t, dsn * sa)
                dst = dsn * wc + ac * dsa
            return dst

        gen(buf0, SCAN_NG - 1)

        def two_groups(i, dst):
            g = SCAN_NG - 1 - 2 * i
            gen(buf1, g - 1)
            dst = steps(buf0, g, dst)
            gen(buf0, jnp.maximum(g - 2, 0))
            return steps(buf1, g - 1, dst)

        ds_ref[...] = lax.fori_loop(0, SCAN_NG // 2, two_groups, ds_ref[...])

    blk = pl.BlockSpec((SCAN_TB, LANES), lambda p, i: (nblk - 1 - i, p))
    shp = jax.ShapeDtypeStruct((S, D), F32)
    return pl.pallas_call(
        body, name="scan_bwd", grid=(npair, nblk),
        in_specs=[blk] * 6 + [pl.BlockSpec((SCAN_TB, None, HEAD, LANES), lambda p, i: (nblk - 1 - i, p, 0, 0)), blk],
        out_specs=[blk] * 6, out_shape=[shp] * 6,
        scratch_shapes=_scan_scratch(3, 2),
        compiler_params=_cparams(("parallel", "arbitrary")))(r, w, k, v, a, b, sall, dy)


def _ada_fwd(c8, w_ada, b_ada):
    def body(c_ref, w_ref, b_ref, o_ref):
        o_ref[...] = jnp.dot(c_ref[...].astype(BF16), w_ref[...], preferred_element_type=F32) + b_ref[...]

    tn = 1536
    return pl.pallas_call(body, name="ada_fwd", grid=(6 * D // tn,),
                          in_specs=[_full((SUBLANES, D)), pl.BlockSpec((D, tn), lambda j: (0, j)), pl.BlockSpec((1, tn), lambda j: (0, j))],
                          out_specs=pl.BlockSpec((SUBLANES, tn), lambda j: (0, j)),
                          out_shape=jax.ShapeDtypeStruct((SUBLANES, 6 * D), F32),
                          compiler_params=_cparams(("parallel",)))(c8, w_ada, b_ada)


def _outer(col, row):
    N = row.shape[1]
    tn = 1536

    def body(c_ref, r_ref, o_ref):
        o_ref[...] = c_ref[...] * r_ref[...]

    return pl.pallas_call(body, name="ada_wgrad", grid=(N // tn,),
                          in_specs=[_full((D, 1)), pl.BlockSpec((1, tn), lambda j: (0, j))],
                          out_specs=pl.BlockSpec((D, tn), lambda j: (0, j)),
                          out_shape=jax.ShapeDtypeStruct((D, N), F32),
                          compiler_params=_cparams(("parallel",)))(col, row)


def _exchange(srcs, broadcast, name):
    n = len(srcs)
    out_shape = [jax.ShapeDtypeStruct((N_DEV,) + (s.shape if broadcast else s.shape[1:]), s.dtype) for s in srcs]

    def body(*refs):
        src_refs, out_refs = refs[:n], refs[n:2 * n]
        send_sems, recv_sems, local_sems = refs[2 * n:]
        x, y, c = lax.axis_index("x"), lax.axis_index("y"), lax.axis_index("c")
        me = 4 * x + 2 * y + c

        def block(i, j):
            return src_refs[i] if broadcast else src_refs[i].at[j]

        def remote(i, d, src_slot, dst_slot):
            px, py, pc = x ^ (d >> 2), y ^ ((d >> 1) & 1), c ^ (d & 1)
            return pltpu.make_async_remote_copy(
                src_ref=block(i, src_slot), dst_ref=out_refs[i].at[dst_slot], send_sem=send_sems.at[i, d],
                recv_sem=recv_sems.at[i, d], device_id=(px, py, pc), device_id_type=_MESH)

        local = [pltpu.make_async_copy(block(i, me), out_refs[i].at[me], local_sems.at[i]) for i in range(n)]
        for cp in local:
            cp.start()
        sends = [remote(i, d, me ^ d, me) for d in range(1, N_DEV) for i in range(n)]
        for cp in sends:
            cp.start()
        for d in range(1, N_DEV):
            for i in range(n):
                remote(i, d, me, me ^ d).wait_recv()
        for cp in sends:
            cp.wait_send()
        for cp in local:
            cp.wait()

    any_spec = pl.BlockSpec(memory_space=pl.ANY)
    return pl.pallas_call(
        body, name=name, out_shape=out_shape, in_specs=[any_spec] * n, out_specs=[any_spec] * n,
        scratch_shapes=[pltpu.SemaphoreType.DMA((n, N_DEV)), pltpu.SemaphoreType.DMA((n, N_DEV)), pltpu.SemaphoreType.DMA((n,))],
        compiler_params=pltpu.CompilerParams(has_side_effects=True),
    )(*srcs)


def _sum_adam(parts, w, m, v, name):
    _, R, C = parts.shape
    tm = 256 if R % 256 == 0 else R
    c1 = 1.0 / (1.0 - ADAM_B1 ** ADAM_STEP)
    c2 = 1.0 / (1.0 - ADAM_B2 ** ADAM_STEP)

    def body(p_ref, w_ref, m_ref, v_ref, g_ref, d_ref, nm_ref, nv_ref):
        g = p_ref[0].astype(F32)
        for j in range(1, N_DEV):
            g = g + p_ref[j].astype(F32)
        nm = ADAM_B1 * m_ref[...] + (1.0 - ADAM_B1) * g
        nv = ADAM_B2 * v_ref[...] + (1.0 - ADAM_B2) * (g * g)
        g_ref[...] = g
        nm_ref[...] = nm
        nv_ref[...] = nv
        d_ref[...] = -ADAM_LR * ((nm * c1) / (jnp.sqrt(nv * c2) + ADAM_EPS) + ADAM_WD * w_ref[...])

    row = _rows(tm, C)
    shp = jax.ShapeDtypeStruct((R, C), F32)
    return pl.pallas_call(body, name=name, grid=(R // tm,),
                          in_specs=[pl.BlockSpec((N_DEV, tm, C), lambda i: (0, i, 0)), row, row, row],
                          out_specs=[row] * 4, out_shape=[shp] * 4,
                          compiler_params=_cparams(("parallel",)))(parts, w, m, v)


PACK_ALIGN = 16 * LANES
PACK_ROWS = 512 * LANES

SHARDED = (("w_ada", 1), ("w_in", 1), ("w2", 1), ("a2", 1), ("g2", 1), ("w_att_out", 1), ("w_rwkv_out", 0),
           ("w_o", 0), ("w_up", 1), ("conv_w", 1), ("w_down", 0))
REPLICATED = ("b_ada", "norm1_w", "b_gate", "mu_shift", "w0", "a0", "k_k", "k_a", "r_k", "lnx_w", "lnx_b",
              "norm2_w", "conv_b", "norm_f_w")
WEIGHTS = ("w_ada", "b_ada", "norm1_w", "w_in", "b_gate", "mu_shift", "w0", "w2", "a0", "a2", "g2", "k_k", "k_a", "r_k",
           "lnx_w", "lnx_b", "w_att_out", "w_rwkv_out", "w_o", "norm2_w", "w_up", "conv_w", "conv_b", "w_down", "norm_f_w")


def _pack(arrays):
    flat, layout, off = [], [], 0
    for i, a in enumerate(arrays):
        n = a.size
        pad = (-n) % PACK_ALIGN if i + 1 < len(arrays) else (-(off + n)) % PACK_ROWS
        flat.append(a.reshape(-1))
        if pad:
            flat.append(jnp.zeros((pad,), a.dtype))
        layout.append((off, n, a.shape))
        off += n + pad
    return jnp.concatenate(flat).reshape(-1, LANES), layout


def _unpack(buf, layout):
    flat = buf.reshape(-1)
    return [flat[off:off + n].reshape(shape) for off, n, shape in layout]


def _pad_w_in(w_in):
    rkv = w_in[:, ATT_IN:ATT_IN + 3 * D]
    lora = w_in[:, ATT_IN + 3 * D:ATT_IN + RWKV_IN]
    gates = w_in[:, ATT_IN + RWKV_IN:]
    att = w_in[:, :ATT_IN]
    lw, la, lg = lora[:, :LORA_W], lora[:, LORA_W:LORA_W + LORA_A], lora[:, LORA_W + LORA_A:]
    zeros = jnp.zeros((w_in.shape[0], LORA_PAD - LANES - LORA_G), w_in.dtype)
    return jnp.concatenate([rkv, gates, att, lw, la, lg, zeros], axis=1)


def _unpad_w_in(g):
    att = g[:, C_ATT:C_ATT + ATT_IN]
    rkv = g[:, C_R:C_R + 3 * D]
    lora = jnp.concatenate([g[:, C_LORA:C_LORA + LORA_W + LORA_A], g[:, C_LORA + LANES:C_LORA + LANES + LORA_G]], axis=1)
    gates = g[:, C_GA:C_GA + 2 * D]
    return jnp.concatenate([att, rkv, lora, gates], axis=1)


def _pad_mu(mu):
    lo = mu[:, 3 * D:]
    mu_l = jnp.concatenate([lo[:, :LORA_W + LORA_A], lo[:, LORA_W + LORA_A:], jnp.zeros((1, LORA_PAD - LANES - LORA_G), mu.dtype)], axis=1)
    return mu[:, :D], mu[:, D:2 * D], mu[:, 2 * D:3 * D], mu_l


def _local_step(x, c, W, target):
    S = x.shape[0]
    G = {}
    c8 = jnp.pad(c, ((0, SUBLANES - 1), (0, 0)))
    ada = _ada_fwd(c8, W["w_ada"], W["b_ada"])[0:1]
    sh1, sc1, gt1, sh2, sc2, gt2 = [ada[:, i * D:(i + 1) * D] for i in range(6)]
    h1, rstd1 = _norm_fwd(x, None, None, W["norm1_w"], sc1, sh1, "norm1_fwd")
    w_in_p = _pad_w_in(W["w_in"])
    P = _mm(h1, w_in_p, "nn", F32, "proj_in")

    o_g, l_g = zip(*[_att_fwd(P, g) for g in range(len(ATT_PATTERNS))])
    att = _att_combine_fwd(o_g, l_g)
    y_att = _mm(att, W["w_att_out"], "nn", F32, "att_out")

    mu_r, mu_k, mu_v, mu_l = _pad_mu(W["mu_shift"])
    g2p = jnp.pad(W["g2"], ((0, G_PAD - LORA_G), (0, 0)))
    prep_params = [mu_r, mu_k, mu_v, mu_l, W["w0"], W["a0"], W["k_k"], W["k_a"], W["w2"], W["a2"], g2p]
    r_, dec, kmod, v_, aa, bb, gg = _rwkv_prep(P, prep_params)
    y_scan, states = _scan_fwd(r_, dec, kmod, v_, aa, bb)
    r_k = W["r_k"].reshape(1, D)
    rw = _rwkv_post(y_scan, r_, kmod, v_, gg, W["lnx_w"], W["lnx_b"], r_k)
    y_rwkv = _mm(rw, W["w_rwkv_out"], "nn", F32, "rwkv_out")

    bga, bgr = W["b_gate"][:, :D], W["b_gate"][:, D:]
    mix = _gate_fwd(P, bga, bgr, y_att, y_rwkv)
    mo = _mm(mix, W["w_o"], "nn", F32, "mix_out")
    x2, h2, rstd2 = _norm_fwd(x, mo, gt1, W["norm2_w"], sc2, sh2, "norm2_fwd")
    u = _mm(h2, W["w_up"], "nn", F32, "ffn_up")
    conv_w8 = jnp.pad(W["conv_w"], ((0, SUBLANES - 3), (0, 0)))
    act = _conv_fwd(u, conv_w8, W["conv_b"])
    f = _mm(act, W["w_down"], "nn", F32, "ffn_down")
    loss_blk, dx3, df, dgt2, G["norm_f_w"] = _final(x2, f, gt2, W["norm_f_w"], target)
    loss = loss_blk[0, 0]

    dact = _mm(df, W["w_down"], "nt", BF16, "ffn_down_dx")
    G["w_down"] = _mm(act, df, "tn", F32, "ffn_down_dw")
    duc, dwg, dwv, dbg, dbv = _conv_bwd_a(dact, u, conv_w8, W["conv_b"])
    G["conv_w"] = jnp.concatenate([dwg[0:3], dwv[0:3]], axis=1)
    G["conv_b"] = jnp.concatenate([dbg, dbv], axis=1)
    du = _conv_bwd_b(duc, conv_w8)
    dh2 = _mm(du, W["w_up"], "nt", F32, "ffn_up_dx")
    G["w_up"] = _mm(h2, du, "tn", F32, "ffn_up_dw")
    dx2, dsh2, dsc2, G["norm2_w"], dmo, dgt1 = _norm_bwd(dh2, x2, rstd2, W["norm2_w"], sc2, dx3, mo, gt1, "norm2_bwd")
    dmix = _mm(dmo, W["w_o"], "nt", F32, "mix_out_dx")
    G["w_o"] = _mm(mix, dmo, "tn", F32, "mix_out_dw")
    dy_att, dy_rwkv, dpga, dpgr, dbga, dbgr = _gate_bwd(dmix, P, bga, bgr, y_att, y_rwkv)
    G["b_gate"] = jnp.concatenate([dbga, dbgr], axis=1)

    datt = _mm(dy_att, W["w_att_out"], "nt", F32, "att_out_dx")
    G["w_att_out"] = _mm(att, dy_att, "tn", F32, "att_out_dw")
    dcomb = _att_combine_bwd(datt, o_g, l_g)
    dp_att = []
    for g in range(len(ATT_PATTERNS)):
        dp_att += _att_bwd(P, o_g[g], l_g[g], dcomb[g], dcomb[3 + g], g)

    drw = _mm(dy_rwkv, W["w_rwkv_out"], "nt", F32, "rwkv_out_dx")
    G["w_rwkv_out"] = _mm(rw, dy_rwkv, "tn", F32, "rwkv_out_dw")
    dy_scan, dr1, dk1, dv1, dgg, G["lnx_w"], G["lnx_b"], drk = _rwkv_post_bwd(drw, y_scan, r_, kmod, v_, gg, W["lnx_w"], W["lnx_b"], r_k)
    G["r_k"] = drk.reshape(W["r_k"].shape)
    dr2, ddec, dk2, dv2, daa, dbb = _scan_bwd(r_, dec, kmod, v_, aa, bb, states, dy_scan)
    pb = _rwkv_prep_bwd(P, prep_params, [dr2, ddec, dk2, dv2, daa, dbb, dgg], [dr1, None, dk1, dv1, None, None, None])
    dz, dzp, dpar = pb[0:4], pb[4:8], pb[8:]
    dp_rkv = [_shift_add(dz[i], dzp[i]) for i in range(3)]
    dp_lora = _shift_add(dz[3], dzp[3])
    dmu_r, dmu_k, dmu_v, dmu_l, G["w0"], G["a0"], G["k_k"], G["k_a"], G["w2"], G["a2"], dg2p = dpar
    G["g2"] = dg2p[0:LORA_G]
    G["mu_shift"] = jnp.concatenate([dmu_r, dmu_k, dmu_v, dmu_l[:, :LORA_W + LORA_A], dmu_l[:, LANES:LANES + LORA_G]], axis=1)

    dP = jnp.concatenate(dp_rkv + [dpga, dpgr] + dp_att + [dp_lora], axis=1)
    dh1 = _mm(dP, w_in_p, "nt", F32, "proj_in_dx")
    G["w_in"] = _unpad_w_in(_mm(h1, dP, "tn", F32, "proj_in_dw"))
    grad_x, dsh1, dsc1, G["norm1_w"] = _norm_bwd(dh1, x, rstd1, W["norm1_w"], sc1, dx2, None, None, "norm1_bwd")
    dada = jnp.concatenate([dsh1, dsc1, dgt1, dsh2, dsc2, dgt2], axis=1)
    G["b_ada"] = dada
    G["w_ada"] = _outer(c.reshape(D, 1), dada)
    return loss, grad_x, G


def _full_weight(gathered, axis):
    _, rows, cols = gathered.shape
    if axis == 0:
        return gathered.reshape(N_DEV * rows, cols)
    return gathered.transpose(1, 0, 2).reshape(rows, N_DEV * cols)


def _owner_blocks(g, axis):
    rows, cols = g.shape
    g = g.astype(BF16)
    if axis == 0:
        return g.reshape(N_DEV, rows // N_DEV, cols)
    return g.reshape(rows, N_DEV, cols // N_DEV).transpose(1, 0, 2)


def kernel(x, c, w_ada, b_ada, norm1_w, w_in, b_gate, mu_shift, w0, w2, a0, a2, g2, k_k, k_a, r_k, lnx_w, lnx_b, w_att_out, w_rwkv_out, w_o, norm2_w, w_up, conv_w, conv_b, w_down, norm_f_w, loss_target, m_w_ada, m_b_ada, m_norm1_w, m_w_in, m_b_gate, m_mu_shift, m_w0, m_w2, m_a0, m_a2, m_g2, m_k_k, m_k_a, m_r_k, m_lnx_w, m_lnx_b, m_w_att_out, m_w_rwkv_out, m_w_o, m_norm2_w, m_w_up, m_conv_w, m_conv_b, m_w_down, m_norm_f_w, v_w_ada, v_b_ada, v_norm1_w, v_w_in, v_b_gate, v_mu_shift, v_w0, v_w2, v_a0, v_a2, v_g2, v_k_k, v_k_a, v_r_k, v_lnx_w, v_lnx_b, v_w_att_out, v_w_rwkv_out, v_w_o, v_norm2_w, v_w_up, v_conv_w, v_conv_b, v_w_down, v_norm_f_w):
    env = dict(locals())
    w_shard = {n: env[n] for n in WEIGHTS}
    m_shard = {n: env["m_" + n] for n in WEIGHTS}
    v_shard = {n: env["v_" + n] for n in WEIGHTS}

    gathered = _exchange([w_shard[n][0].astype(BF16) for n, _ in SHARDED], True, "gather_weights")
    W = {n: _full_weight(g, axis) for (n, axis), g in zip(SHARDED, gathered)}
    for n in REPLICATED:
        W[n] = w_shard[n].reshape(1, -1) if n != "r_k" else w_shard[n][0]

    loss, grad_x, G = _local_step(x[0], c, W, loss_target[0])
    loss = lax.psum(loss, ("x", "y", "c"))

    parts = _exchange([_owner_blocks(G[n], axis) for n, axis in SHARDED], False, "scatter_grads")
    out = {}
    for (n, _), p in zip(SHARDED, parts):
        res = _sum_adam(p, w_shard[n][0], m_shard[n][0], v_shard[n][0], "adam_" + n)
        for kind, a in zip(("grad", "delta", "new_m", "new_v"), res):
            out[kind, n] = a[None]

    small, slayout = _pack([G[n].reshape(-1) for n in REPLICATED])
    sparts, = _exchange([small], True, "gather_small_grads")
    sw, _ = _pack([w_shard[n].reshape(-1) for n in REPLICATED])
    sm, _ = _pack([m_shard[n].reshape(-1) for n in REPLICATED])
    sv, _ = _pack([v_shard[n].reshape(-1) for n in REPLICATED])
    res = _sum_adam(sparts, sw, sm, sv, "adam_replicated")
    for kind, buf in zip(("grad", "delta", "new_m", "new_v"), res):
        for n, a in zip(REPLICATED, _unpack(buf, slayout)):
            out[kind, n] = a.reshape(w_shard[n].shape)

    return (loss, grad_x[None], *[out[kind, n] for kind in ("grad", "delta", "new_m", "new_v") for n in WEIGHTS])
```

```python
import functools
import math

import jax
import jax.numpy as jnp
from jax import lax
from jax.experimental import pallas as pl
from jax.experimental.pallas import tpu as pltpu

F32 = jnp.float32
BF16 = jnp.bfloat16

D = 1024
HEAD = 64
ATT_PATTERNS = ((128, 1), (512, 4), (2048, 16))
ATT_HEADS = 8
ATT_W = ATT_HEADS * HEAD
ATT_IN = 3 * 3 * ATT_W
QBLK = 128
N_HEADS = D // HEAD
LORA_W, LORA_A, LORA_G = 64, 64, 160
RWKV_IN = 3 * D + LORA_W + LORA_A + LORA_G
N_IN = ATT_IN + RWKV_IN + 2 * D
D_FF = 2816
RMS_EPS = 1e-6
GN_EPS = 64e-5
N_DEV = 8
LANES = 128
SUBLANES = 8

C_R, C_K, C_V, C_GA, C_GR = 0, 1024, 2048, 3072, 4096
C_ATT = 5120
C_LORA = C_ATT + ATT_IN
LORA_PAD = 512
G_PAD = 256
N_PAD = C_LORA + LORA_PAD

ADAM_LR, ADAM_B1, ADAM_B2, ADAM_EPS, ADAM_WD, ADAM_STEP = 0.001, 0.9, 0.999, 1e-08, 0.01, 10

VMEM_LIMIT = 56 * 1024 * 1024

_MESH = pl.DeviceIdType.MESH


def _cparams(sem):
    return pltpu.CompilerParams(dimension_semantics=sem, vmem_limit_bytes=VMEM_LIMIT)


def _tile(dim, pref):
    if dim <= pref:
        return dim
    best = None
    for t in range(LANES, pref + 1, LANES):
        if dim % t == 0:
            best = t
    assert best is not None, dim
    return best


MM_TILES = {"nn": (1024, 1408, 1408), "nt": (512, 2048, 1408), "tn": (1408, 1408, 1024)}


def _mm(a, b, mode, out_dtype, name):
    if mode == "nn":
        (M, K), (K2, N) = a.shape, b.shape
    elif mode == "nt":
        (M, K), (N, K2) = a.shape, b.shape
    else:
        (K, M), (K2, N) = a.shape, b.shape
    assert K == K2, (a.shape, b.shape, mode)
    tm, tn, tk = (_tile(dim, pref) for dim, pref in zip((M, N, K), MM_TILES[mode]))
    nk = K // tk
    dims = {"nn": (((1,), (0,)), ((), ())), "nt": (((1,), (1,)), ((), ())), "tn": (((0,), (0,)), ((), ()))}[mode]

    def body(a_ref, b_ref, o_ref, acc_ref):
        k = pl.program_id(2)
        part = lax.dot_general(a_ref[...].astype(BF16), b_ref[...].astype(BF16), dims,
                               preferred_element_type=F32)
        if nk == 1:
            o_ref[...] = part.astype(o_ref.dtype)
            return

        @pl.when(k == 0)
        def _():
            acc_ref[...] = part

        @pl.when(jnp.logical_and(k > 0, k < nk - 1))
        def _():
            acc_ref[...] += part

        @pl.when(k == nk - 1)
        def _():
            o_ref[...] = (acc_ref[...] + part).astype(o_ref.dtype)

    a_spec = pl.BlockSpec((tk, tm), lambda i, j, k: (k, i)) if mode == "tn" else pl.BlockSpec((tm, tk), lambda i, j, k: (i, k))
    b_spec = pl.BlockSpec((tn, tk), lambda i, j, k: (j, k)) if mode == "nt" else pl.BlockSpec((tk, tn), lambda i, j, k: (k, j))
    return pl.pallas_call(
        body, name=name, grid=(M // tm, N // tn, nk),
        in_specs=[a_spec, b_spec],
        out_specs=pl.BlockSpec((tm, tn), lambda i, j, k: (i, j)),
        out_shape=jax.ShapeDtypeStruct((M, N), out_dtype),
        scratch_shapes=[pltpu.VMEM((tm, tn) if nk > 1 else (SUBLANES, LANES), F32)],
        compiler_params=_cparams(("parallel", "parallel", "arbitrary")),
    )(a, b)


def _rows(tm, w, col=0):
    return pl.BlockSpec((tm, w), lambda i: (i, col))


def _full(shape):
    return pl.BlockSpec(shape, lambda i: (0,) * len(shape))


def _prev8(tm, w, col=0):
    return pl.BlockSpec((SUBLANES, w), lambda i: (jnp.maximum(i * (tm // SUBLANES) - 1, 0), col))


def _next8(tm, w, n_rows, col=0):
    last = n_rows // SUBLANES - 1
    return pl.BlockSpec((SUBLANES, w), lambda i: (jnp.minimum((i + 1) * (tm // SUBLANES), last), col))


def _shift_down(x, halo, k, first):
    rolled = pltpu.roll(x, k, 0)
    row = lax.broadcasted_iota(jnp.int32, x.shape, 0)
    out = rolled
    for j in range(k):
        h = jnp.where(first, 0.0, halo[SUBLANES - k + j:SUBLANES - k + j + 1, :])
        out = jnp.where(row == j, h, out)
    return out


def _shift_up(x, halo, k, last):
    n = x.shape[0]
    rolled = pltpu.roll(x, n - k, 0)
    row = lax.broadcasted_iota(jnp.int32, x.shape, 0)
    out = rolled
    for j in range(k):
        h = jnp.where(last, 0.0, halo[j:j + 1, :])
        out = jnp.where(row == n - k + j, h, out)
    return out


def _acc(ref, val, first):
    @pl.when(first)
    def _():
        ref[...] = val

    @pl.when(jnp.logical_not(first))
    def _():
        ref[...] += val


def _colsum(x):
    return jnp.sum(x, axis=0, keepdims=True)


def _norm_fwd(x, mo, gt, nw, sc, sh, name, tm=256):
    S = x.shape[0]
    has_res = mo is not None

    def body(*refs):
        if has_res:
            x_ref, mo_ref, gt_ref, nw_ref, sc_ref, sh_ref, x2_ref, h_ref, rs_ref = refs
            x2 = x_ref[...] + gt_ref[...] * mo_ref[...]
            x2_ref[...] = x2
        else:
            x_ref, nw_ref, sc_ref, sh_ref, h_ref, rs_ref = refs
            x2 = x_ref[...]
        rstd = lax.rsqrt(jnp.mean(x2 * x2, axis=-1, keepdims=True) + RMS_EPS)
        rs_ref[...] = rstd
        h_ref[...] = ((x2 * rstd * nw_ref[...]) * (1.0 + sc_ref[...]) + sh_ref[...]).astype(BF16)

    vec = _full((1, D))
    ins = [x, mo, gt, nw, sc, sh] if has_res else [x, nw, sc, sh]
    in_specs = [_rows(tm, D), _rows(tm, D), vec, vec, vec, vec] if has_res else [_rows(tm, D), vec, vec, vec]
    outs = [jax.ShapeDtypeStruct((S, D), BF16), jax.ShapeDtypeStruct((S, 1), F32)]
    out_specs = [_rows(tm, D), _rows(tm, 1)]
    if has_res:
        outs = [jax.ShapeDtypeStruct((S, D), F32)] + outs
        out_specs = [_rows(tm, D)] + out_specs
    return pl.pallas_call(body, name=name, grid=(S // tm,), in_specs=in_specs, out_specs=out_specs,
                          out_shape=outs, compiler_params=_cparams(("parallel",)))(*ins)


def _norm_bwd(dh, xin, rstd, nw, sc, dres, mo, gt, name, tm=256):
    S = xin.shape[0]
    has_res = mo is not None

    def body(*refs):
        if has_res:
            dh_ref, x_ref, rs_ref, nw_ref, sc_ref, dres_ref, mo_ref, gt_ref, dx_ref, dsh_ref, dsc_ref, dnw_ref, dmo_ref, dgt_ref = refs
        else:
            dh_ref, x_ref, rs_ref, nw_ref, sc_ref, dres_ref, dx_ref, dsh_ref, dsc_ref, dnw_ref = refs
        first = pl.program_id(0) == 0
        dh = dh_ref[...]
        rstd = rs_ref[...]
        n = x_ref[...] * rstd
        w = nw_ref[...]
        _acc(dsh_ref, _colsum(dh), first)
        _acc(dsc_ref, _colsum(dh * (n * w)), first)
        dnw = dh * (1.0 + sc_ref[...])
        _acc(dnw_ref, _colsum(dnw * n), first)
        dn = dnw * w
        dx = dres_ref[...] + rstd * (dn - n * jnp.mean(dn * n, axis=-1, keepdims=True))
        dx_ref[...] = dx
        if has_res:
            dmo_ref[...] = (dx * gt_ref[...]).astype(BF16)
            _acc(dgt_ref, _colsum(dx * mo_ref[...]), first)

    vec = _full((1, D))
    vshape = jax.ShapeDtypeStruct((1, D), F32)
    ins = [dh, xin, rstd, nw, sc, dres] + ([mo, gt] if has_res else [])
    in_specs = [_rows(tm, D), _rows(tm, D), _rows(tm, 1), vec, vec, _rows(tm, D)] + ([_rows(tm, D), vec] if has_res else [])
    outs = [jax.ShapeDtypeStruct((S, D), F32), vshape, vshape, vshape]
    out_specs = [_rows(tm, D), vec, vec, vec]
    if has_res:
        outs += [jax.ShapeDtypeStruct((S, D), BF16), vshape]
        out_specs += [_rows(tm, D), vec]
    return pl.pallas_call(body, name=name, grid=(S // tm,), in_specs=in_specs, out_specs=out_specs,
                          out_shape=outs, compiler_params=_cparams(("arbitrary",)))(*ins)


def _final(x2, f, gt2, nfw, target, tm=256):
    S = x2.shape[0]

    def body(x2_ref, f_ref, gt_ref, w_ref, t_ref, loss_ref, dx_ref, df_ref, dgt_ref, dw_ref):
        first = pl.program_id(0) == 0
        f = f_ref[...]
        gt = gt_ref[...]
        w = w_ref[...]
        x3 = x2_ref[...] + gt * f
        rstd = lax.rsqrt(jnp.mean(x3 * x3, axis=-1, keepdims=True) + RMS_EPS)
        n = x3 * rstd
        e = n * w - t_ref[...]
        part = 0.5 * jnp.sum(jnp.mean(e * e, axis=-1, keepdims=True), axis=0, keepdims=True)
        _acc(loss_ref, jnp.broadcast_to(part, (SUBLANES, LANES)), first)
        dy = e * (1.0 / D)
        _acc(dw_ref, _colsum(dy * n), first)
        dn = dy * w
        dx = rstd * (dn - n * jnp.mean(dn * n, axis=-1, keepdims=True))
        dx_ref[...] = dx
        df_ref[...] = (dx * gt).astype(BF16)
        _acc(dgt_ref, _colsum(dx * f), first)

    vec = _full((1, D))
    vshape = jax.ShapeDtypeStruct((1, D), F32)
    return pl.pallas_call(
        body, name="final_loss", grid=(S // tm,),
        in_specs=[_rows(tm, D), _rows(tm, D), vec, vec, _rows(tm, D)],
        out_specs=[_full((SUBLANES, LANES)), _rows(tm, D), _rows(tm, D), vec, vec],
        out_shape=[jax.ShapeDtypeStruct((SUBLANES, LANES), F32), jax.ShapeDtypeStruct((S, D), F32),
                   jax.ShapeDtypeStruct((S, D), BF16), vshape, vshape],
        compiler_params=_cparams(("arbitrary",)))(x2, f, gt2, nfw, target)


def _gate_fwd(P, bga, bgr, y_att, y_rwkv, tm=256):
    S = P.shape[0]

    def body(pa_ref, pr_ref, ba_ref, br_ref, ya_ref, yr_ref, mix_ref):
        ga = jax.nn.sigmoid(pa_ref[...] + ba_ref[...])
        gr = jax.nn.sigmoid(pr_ref[...] + br_ref[...])
        mix_ref[...] = (ga * ya_ref[...] + gr * yr_ref[...]).astype(BF16)

    vec = _full((1, D))
    return pl.pallas_call(
        body, name="gate_fwd", grid=(S // tm,),
        in_specs=[_rows(tm, D, C_GA // D), _rows(tm, D, C_GR // D), vec, vec, _rows(tm, D), _rows(tm, D)],
        out_specs=_rows(tm, D), out_shape=jax.ShapeDtypeStruct((S, D), BF16),
        compiler_params=_cparams(("parallel",)))(P, P, bga, bgr, y_att, y_rwkv)


def _gate_bwd(dmix, P, bga, bgr, y_att, y_rwkv, tm=256):
    S = P.shape[0]

    def body(dm_ref, pa_ref, pr_ref, ba_ref, br_ref, ya_ref, yr_ref, dya_ref, dyr_ref, dpa_ref, dpr_ref, dba_ref, dbr_ref):
        first = pl.program_id(0) == 0
        dm = dm_ref[...]
        ga = jax.nn.sigmoid(pa_ref[...] + ba_ref[...])
        gr = jax.nn.sigmoid(pr_ref[...] + br_ref[...])
        dya_ref[...] = (dm * ga).astype(BF16)
        dyr_ref[...] = (dm * gr).astype(BF16)
        dpa = dm * ya_ref[...] * ga * (1.0 - ga)
        dpr = dm * yr_ref[...] * gr * (1.0 - gr)
        dpa_ref[...] = dpa.astype(BF16)
        dpr_ref[...] = dpr.astype(BF16)
        _acc(dba_ref, _colsum(dpa), first)
        _acc(dbr_ref, _colsum(dpr), first)

    vec = _full((1, D))
    row = _rows(tm, D)
    rshape = jax.ShapeDtypeStruct((S, D), BF16)
    vshape = jax.ShapeDtypeStruct((1, D), F32)
    return pl.pallas_call(
        body, name="gate_bwd", grid=(S // tm,),
        in_specs=[row, _rows(tm, D, C_GA // D), _rows(tm, D, C_GR // D), vec, vec, row, row],
        out_specs=[row, row, row, row, vec, vec],
        out_shape=[rshape, rshape, rshape, rshape, vshape, vshape],
        compiler_params=_cparams(("arbitrary",)))(dmix, P, P, bga, bgr, y_att, y_rwkv)


CONV_TN = D_FF // 2


def _conv_fwd(u, conv_w8, conv_b, tm=256, tn=CONV_TN):
    S = u.shape[0]
    nj = D_FF // tn

    def conv(u_ref, h_ref, w_ref, b_ref, first):
        u = u_ref[...]
        h = h_ref[...]
        w = w_ref[...]
        return b_ref[...] + w[0:1] * _shift_down(u, h, 2, first) + w[1:2] * _shift_down(u, h, 1, first) + w[2:3] * u

    def body(ug_ref, hg_ref, uv_ref, hv_ref, wg_ref, wv_ref, bg_ref, bv_ref, act_ref):
        first = pl.program_id(0) == 0
        g = conv(ug_ref, hg_ref, wg_ref, bg_ref, first)
        v = conv(uv_ref, hv_ref, wv_ref, bv_ref, first)
        act_ref[...] = (g * jax.nn.sigmoid(g) * v).astype(BF16)

    blk = lambda off: pl.BlockSpec((tm, tn), lambda i, j: (i, j + off))
    halo = lambda off: pl.BlockSpec((SUBLANES, tn), lambda i, j: (jnp.maximum(i * (tm // SUBLANES) - 1, 0), j + off))
    wsp = lambda off: pl.BlockSpec((SUBLANES, tn), lambda i, j: (0, j + off))
    bsp = lambda off: pl.BlockSpec((1, tn), lambda i, j: (0, j + off))
    return pl.pallas_call(
        body, name="conv_fwd", grid=(S // tm, nj),
        in_specs=[blk(0), halo(0), blk(nj), halo(nj), wsp(0), wsp(nj), bsp(0), bsp(nj)],
        out_specs=pl.BlockSpec((tm, tn), lambda i, j: (i, j)),
        out_shape=jax.ShapeDtypeStruct((S, D_FF), BF16),
        compiler_params=_cparams(("parallel", "parallel")))(u, u, u, u, conv_w8, conv_w8, conv_b, conv_b)


def _conv_bwd_a(dact, u, conv_w8, conv_b, tm=256, tn=CONV_TN):
    S = u.shape[0]
    nj = D_FF // tn

    def half(u_ref, h_ref, w_ref, b_ref, first):
        u = u_ref[...]
        h = h_ref[...]
        w = w_ref[...]
        u2, u1 = _shift_down(u, h, 2, first), _shift_down(u, h, 1, first)
        return b_ref[...] + w[0:1] * u2 + w[1:2] * u1 + w[2:3] * u, (u2, u1, u)

    def wgrad(d, taps):
        z = jnp.zeros((SUBLANES - 3, d.shape[1]), F32)
        return jnp.concatenate([_colsum(d * taps[0]), _colsum(d * taps[1]), _colsum(d * taps[2]), z], axis=0)

    def body(da_ref, ug_ref, hg_ref, uv_ref, hv_ref, wg_ref, wv_ref, bg_ref, bv_ref,
             d_ref, dwg_ref, dwv_ref, dbg_ref, dbv_ref):
        first = pl.program_id(1) == 0
        g, tg = half(ug_ref, hg_ref, wg_ref, bg_ref, first)
        v, tv = half(uv_ref, hv_ref, wv_ref, bv_ref, first)
        da = da_ref[...].astype(F32)
        sg = jax.nn.sigmoid(g)
        dg = da * v * (sg * (1.0 + g * (1.0 - sg)))
        dv = da * (g * sg)
        d_ref[0] = dg
        d_ref[1] = dv
        _acc(dwg_ref, wgrad(dg, tg), first)
        _acc(dwv_ref, wgrad(dv, tv), first)
        _acc(dbg_ref, _colsum(dg), first)
        _acc(dbv_ref, _colsum(dv), first)

    blk = lambda off: pl.BlockSpec((tm, tn), lambda j, i: (i, j + off))
    halo = lambda off: pl.BlockSpec((SUBLANES, tn), lambda j, i: (jnp.maximum(i * (tm // SUBLANES) - 1, 0), j + off))
    wsp = lambda off: pl.BlockSpec((SUBLANES, tn), lambda j, i: (0, j + off))
    bsp = lambda off: pl.BlockSpec((1, tn), lambda j, i: (0, j + off))
    f = jax.ShapeDtypeStruct
    outs = pl.pallas_call(
        body, name="conv_bwd_a", grid=(nj, S // tm),
        in_specs=[pl.BlockSpec((tm, tn), lambda j, i: (i, j)), blk(0), halo(0), blk(nj), halo(nj), wsp(0), wsp(nj), bsp(0), bsp(nj)],
        out_specs=[pl.BlockSpec((2, tm, tn), lambda j, i: (0, i, j)),
                   pl.BlockSpec((SUBLANES, tn), lambda j, i: (0, j)), pl.BlockSpec((SUBLANES, tn), lambda j, i: (0, j)),
                   pl.BlockSpec((1, tn), lambda j, i: (0, j)), pl.BlockSpec((1, tn), lambda j, i: (0, j))],
        out_shape=[f((2, S, D_FF), F32), f((SUBLANES, D_FF), F32), f((SUBLANES, D_FF), F32),
                   f((1, D_FF), F32), f((1, D_FF), F32)],
        compiler_params=_cparams(("parallel", "arbitrary")))(dact, u, u, u, u, conv_w8, conv_w8, conv_b, conv_b)
    return outs


def _conv_bwd_b(duc, conv_w8, tm=256, tn=CONV_TN):
    _, S, W = duc.shape
    nj = W // tn
    n_rows = S // tm

    def body(d_ref, h_ref, w_ref, o_ref):
        last = pl.program_id(0) == n_rows - 1
        d = d_ref[...]
        h = h_ref[...]
        w = w_ref[...]
        o_ref[...] = (w[2:3] * d + w[1:2] * _shift_up(d, h, 1, last) + w[0:1] * _shift_up(d, h, 2, last)).astype(BF16)

    last_tile = S // SUBLANES - 1
    return pl.pallas_call(
        body, name="conv_bwd_b", grid=(n_rows, 2 * nj),
        in_specs=[pl.BlockSpec((None, tm, tn), lambda i, j: (j // nj, i, j % nj)),
                  pl.BlockSpec((None, SUBLANES, tn), lambda i, j: (j // nj, jnp.minimum((i + 1) * (tm // SUBLANES), last_tile), j % nj)),
                  pl.BlockSpec((SUBLANES, tn), lambda i, j: (0, j))],
        out_specs=pl.BlockSpec((tm, tn), lambda i, j: (i, j)),
        out_shape=jax.ShapeDtypeStruct((S, 2 * W), BF16),
        compiler_params=_cparams(("parallel", "parallel")))(duc, duc, conv_w8)


ATT_SCALE = HEAD ** -0.5
NEG = -1e30
ATT_PAIRS = ATT_HEADS // 2


def _att_rows(n, d, S):
    per = S // (QBLK * d)
    r, m = n // per, n % per
    cur = pl.ds(m * (QBLK * d) + r, QBLK, stride=d)
    prv = pl.ds(jnp.maximum(m - 1, 0) * (QBLK * d) + r, QBLK, stride=d)
    return cur, prv, m > 0


def _att_slab(g, j):
    return (C_ATT + g * 3 * ATT_W + j * ATT_W) // LANES


def _heads(x):
    return x[:, 0:HEAD], x[:, HEAD:2 * HEAD]


def _att_scores(q, kc, kp, has_prev):
    qi = lax.broadcasted_iota(jnp.int32, (QBLK, QBLK), 0)
    kj = lax.broadcasted_iota(jnp.int32, (QBLK, QBLK), 1)
    nt = (((1,), (1,)), ((), ()))
    s_c = lax.dot_general(q, kc, nt, preferred_element_type=F32) * ATT_SCALE
    s_p = lax.dot_general(q, kp, nt, preferred_element_type=F32) * ATT_SCALE
    s_c = jnp.where(kj <= qi, s_c, NEG)
    s_p = jnp.where(jnp.logical_and(kj >= qi, has_prev), s_p, NEG)
    return s_c, s_p


def _att_fwd(P, g):
    S = P.shape[0]
    d = ATT_PATTERNS[g][1]

    def body(q_ref, k_ref, v_ref, o_ref, l_ref):
        def blk(n, carry):
            cur, prv, has_prev = _att_rows(n, d, S)
            q2, kc2, kp2 = q_ref[cur, :].astype(BF16), k_ref[cur, :].astype(BF16), k_ref[prv, :].astype(BF16)
            vc2, vp2 = v_ref[cur, :].astype(BF16), v_ref[prv, :].astype(BF16)
            outs, lses = [], []
            for q, kc, kp, vc, vp in zip(_heads(q2), _heads(kc2), _heads(kp2), _heads(vc2), _heads(vp2)):
                s_c, s_p = _att_scores(q, kc, kp, has_prev)
                m = jnp.maximum(jnp.max(s_c, axis=1, keepdims=True), jnp.max(s_p, axis=1, keepdims=True))
                p_c = jnp.exp(s_c - m)
                p_p = jnp.exp(s_p - m)
                den = jnp.sum(p_c, axis=1, keepdims=True) + jnp.sum(p_p, axis=1, keepdims=True)
                num = (jnp.dot(p_c.astype(BF16), vc, preferred_element_type=F32)
                       + jnp.dot(p_p.astype(BF16), vp, preferred_element_type=F32))
                outs.append(num / den)
                lses.append(jnp.broadcast_to(m + jnp.log(den), (QBLK, HEAD)))
            o_ref[cur, :] = jnp.concatenate(outs, axis=1)
            l_ref[cur, :] = jnp.concatenate(lses, axis=1)
            return carry

        lax.fori_loop(0, S // QBLK, blk, 0, unroll=2)

    slab = lambda j: pl.BlockSpec((S, LANES), lambda i: (0, _att_slab(g, j) + i))
    out = pl.BlockSpec((S, LANES), lambda i: (0, i))
    shp = jax.ShapeDtypeStruct((S, ATT_W), F32)
    return pl.pallas_call(body, name=f"att_fwd_g{g}", grid=(ATT_PAIRS,), in_specs=[slab(0), slab(1), slab(2)],
                          out_specs=[out, out], out_shape=[shp, shp], compiler_params=_cparams(("parallel",)))(P, P, P)


def _att_bwd(P, o, l, do, dl, g):
    S = P.shape[0]
    d = ATT_PATTERNS[g][1]
    tn = (((0,), (0,)), ((), ()))
    nt = (((1,), (1,)), ((), ()))

    def body(q_ref, k_ref, v_ref, o_ref, l_ref, do_ref, dl_ref, dq_ref, dk_ref, dv_ref, dq_acc, dk_acc, dv_acc):
        dk_acc[...] = jnp.zeros_like(dk_acc)
        dv_acc[...] = jnp.zeros_like(dv_acc)

        def blk(n, carry):
            cur, prv, has_prev = _att_rows(n, d, S)
            q2, kc2, kp2 = q_ref[cur, :].astype(BF16), k_ref[cur, :].astype(BF16), k_ref[prv, :].astype(BF16)
            vc2, vp2 = v_ref[cur, :].astype(BF16), v_ref[prv, :].astype(BF16)
            do2 = do_ref[cur, :]
            dd2 = do2 * o_ref[cur, :] - dl_ref[cur, :]
            l2 = l_ref[cur, :]
            res = []
            for q, kc, kp, vc, vp, dob, dd, lse in zip(_heads(q2), _heads(kc2), _heads(kp2), _heads(vc2), _heads(vp2),
                                                     _heads(do2), _heads(dd2), _heads(l2)):
                s_c, s_p = _att_scores(q, kc, kp, has_prev)
                p_c = jnp.exp(s_c - lse[:, 0:1])
                p_p = jnp.exp(s_p - lse[:, 0:1])
                delta = jnp.sum(dd, axis=1, keepdims=True)
                dob16 = dob.astype(BF16)
                dp_c = lax.dot_general(dob16, vc, nt, preferred_element_type=F32)
                dp_p = lax.dot_general(dob16, vp, nt, preferred_element_type=F32)
                ds_c = (p_c * (dp_c - delta) * ATT_SCALE).astype(BF16)
                ds_p = (p_p * (dp_p - delta) * ATT_SCALE).astype(BF16)
                res.append((
                    jnp.dot(ds_c, kc, preferred_element_type=F32) + jnp.dot(ds_p, kp, preferred_element_type=F32),
                    lax.dot_general(ds_c, q, tn, preferred_element_type=F32),
                    lax.dot_general(ds_p, q, tn, preferred_element_type=F32),
                    lax.dot_general(p_c.astype(BF16), dob16, tn, preferred_element_type=F32),
                    lax.dot_general(p_p.astype(BF16), dob16, tn, preferred_element_type=F32)))
            both = [jnp.concatenate([res[0][i], res[1][i]], axis=1) for i in range(5)]
            dq_acc[cur, :] = both[0]
            dk_acc[cur, :] += both[1]
            dv_acc[cur, :] += both[3]
            dk_acc[prv, :] += both[2]
            dv_acc[prv, :] += both[4]
            return carry

        lax.fori_loop(0, S // QBLK, blk, 0, unroll=2)
        dq_ref[...] = dq_acc[...].astype(BF16)
        dk_ref[...] = dk_acc[...].astype(BF16)
        dv_ref[...] = dv_acc[...].astype(BF16)

    slab = lambda j: pl.BlockSpec((S, LANES), lambda i: (0, _att_slab(g, j) + i))
    blk128 = pl.BlockSpec((S, LANES), lambda i: (0, i))
    shp = jax.ShapeDtypeStruct((S, ATT_W), BF16)
    return pl.pallas_call(body, name=f"att_bwd_g{g}", grid=(ATT_PAIRS,),
                          in_specs=[slab(0), slab(1), slab(2)] + [blk128] * 4, out_specs=[blk128] * 3, out_shape=[shp] * 3,
                          scratch_shapes=[pltpu.VMEM((S, LANES), F32)] * 3,
                          compiler_params=_cparams(("parallel",)))(P, P, P, o, l, do, dl)


def _att_weights(l_refs):
    l0, l1, l2 = [r[...] for r in l_refs]
    m = jnp.maximum(jnp.maximum(l0, l1), l2)
    e = (jnp.exp(l0 - m), jnp.exp(l1 - m), jnp.exp(l2 - m))
    inv = 1.0 / (e[0] + e[1] + e[2])
    return [x * inv for x in e]


def _att_combine_fwd(os, ls, tm=512):
    S = os[0].shape[0]

    def body(o0, o1, o2, l0, l1, l2, a_ref):
        w = _att_weights((l0, l1, l2))
        a_ref[...] = (w[0] * o0[...] + w[1] * o1[...] + w[2] * o2[...]).astype(BF16)

    row = _rows(tm, ATT_W)
    return pl.pallas_call(body, name="att_combine_fwd", grid=(S // tm,), in_specs=[row] * 6, out_specs=row,
                          out_shape=jax.ShapeDtypeStruct((S, ATT_W), BF16),
                          compiler_params=_cparams(("parallel",)))(*os, *ls)


def _att_combine_bwd(da, os, ls, tm=512):
    S = da.shape[0]

    def body(da_ref, o0, o1, o2, l0, l1, l2, *out_refs):
        da = da_ref[...]
        w = _att_weights((l0, l1, l2))
        dw = (da * o0[...], da * o1[...], da * o2[...])
        mean = w[0] * dw[0] + w[1] * dw[1] + w[2] * dw[2]
        for g in range(3):
            out_refs[g][...] = w[g] * da
            out_refs[3 + g][...] = w[g] * (dw[g] - mean)

    row = _rows(tm, ATT_W)
    shp = jax.ShapeDtypeStruct((S, ATT_W), F32)
    return pl.pallas_call(body, name="att_combine_bwd", grid=(S // tm,), in_specs=[row] * 7, out_specs=[row] * 6,
                          out_shape=[shp] * 6, compiler_params=_cparams(("parallel",)))(da, *os, *ls)


@jax.custom_vjp
def _bdot(a, b):
    return jnp.dot(a.astype(BF16), b.astype(BF16), preferred_element_type=F32)


def _bdot_fwd(a, b):
    return _bdot(a, b), (a, b)


def _bdot_bwd(res, ct):
    a, b = res
    ct16 = ct.astype(BF16)
    da = lax.dot_general(ct16, b.astype(BF16), (((1,), (1,)), ((), ())), preferred_element_type=F32)
    db = lax.dot_general(a.astype(BF16), ct16, (((0,), (0,)), ((), ())), preferred_element_type=F32)
    return da, db


_bdot.defvjp(_bdot_fwd, _bdot_bwd)


def _two_piece_dot(x, m):
    hi = x.astype(BF16)
    lo = (x - hi.astype(F32)).astype(BF16)
    return jnp.dot(hi, m, preferred_element_type=F32) + jnp.dot(lo, m, preferred_element_type=F32)


def _head_sum_impl(x):
    sel = (lax.broadcasted_iota(jnp.int32, (D, LANES), 0) // HEAD == lax.broadcasted_iota(jnp.int32, (D, LANES), 1)).astype(BF16)
    sel_t = (lax.broadcasted_iota(jnp.int32, (LANES, D), 1) // HEAD == lax.broadcasted_iota(jnp.int32, (LANES, D), 0)).astype(BF16)
    return _two_piece_dot(_two_piece_dot(x, sel), sel_t)


@jax.custom_vjp
def _head_sum(x):
    return _head_sum_impl(x)


_head_sum.defvjp(lambda x: (_head_sum_impl(x), None), lambda _, ct: (_head_sum_impl(ct),))


def _softplus(z):
    return jnp.maximum(z, 0.0) + jnp.log(1.0 + jnp.exp(-jnp.abs(z)))


def _rwkv_prep_fn(zr, zrp, zk, zkp, zv, zvp, zl, zlp, mu_r, mu_k, mu_v, mu_l, w0, a0, k_k, k_a, w2, a2, g2p):
    r = zr + (zrp - zr) * mu_r
    k = zk + (zkp - zk) * mu_k
    v = zv + (zvp - zv) * mu_v
    lo = zl + (zlp - zl) * mu_l
    w_low, a_low, g_low = lo[:, 0:LORA_W], lo[:, LORA_W:LORA_W + LORA_A], lo[:, LANES:LANES + G_PAD]
    w_log = -_softplus(-(w0 + _bdot(jnp.tanh(w_low), w2))) - 0.5
    decay = -jnp.exp(w_log)
    a = jax.nn.sigmoid(a0 + _bdot(a_low, a2))
    g = _bdot(jax.nn.sigmoid(g_low), g2p)
    kmod = k * (1.0 + (a - 1.0) * k_a)
    kk = k * k_k
    kk = kk / jnp.maximum(jnp.sqrt(_head_sum(kk * kk)), 1e-12)
    return r, decay, kmod, v, -kk, kk * a, g


def _rwkv_prep_specs(tm):
    vec = _full((1, D))
    slabs = []
    for col in (C_R // D, C_K // D, C_V // D):
        slabs += [_rows(tm, D, col), _prev8(tm, D, col)]
    slabs += [_rows(tm, LORA_PAD, C_LORA // LORA_PAD), _prev8(tm, LORA_PAD, C_LORA // LORA_PAD)]
    params = [vec, vec, vec, _full((1, LORA_PAD)), vec, vec, vec, vec,
              _full((LORA_W, D)), _full((LORA_A, D)), _full((G_PAD, D))]
    return slabs, params


def _prep_inputs(refs, first):
    vals = []
    for s in range(4):
        z = refs[2 * s][...]
        vals += [z, _shift_down(z, refs[2 * s + 1][...], 1, first)]
    return vals + [r[...] for r in refs[8:19]]


def _rwkv_prep(P, params, tm=256):
    S = P.shape[0]
    slabs, pspecs = _rwkv_prep_specs(tm)

    def body(*refs):
        outs = _rwkv_prep_fn(*_prep_inputs(refs, pl.program_id(0) == 0))
        for o_ref, val in zip(refs[19:], outs):
            o_ref[...] = val

    shp = jax.ShapeDtypeStruct((S, D), F32)
    return pl.pallas_call(body, name="rwkv_prep", grid=(S // tm,), in_specs=slabs + pspecs,
                          out_specs=[_rows(tm, D)] * 7, out_shape=[shp] * 7,
                          compiler_params=_cparams(("parallel",)))(*([P] * 8), *params)


def _rwkv_prep_bwd(P, params, cts_a, cts_b, tm=128):
    S = P.shape[0]
    slabs, pspecs = _rwkv_prep_specs(tm)
    has_b = [c is not None for c in cts_b]
    n_ct = 7 + sum(has_b)

    def body(*refs):
        first = pl.program_id(0) == 0
        ins = _prep_inputs(refs, first)
        ct_refs = refs[19:19 + n_ct]
        out_refs = refs[19 + n_ct:]
        cts, pos = [], 7
        for i in range(7):
            c = ct_refs[i][...]
            if has_b[i]:
                c = c + ct_refs[pos][...]
                pos += 1
            cts.append(c)
        _, vjp = jax.vjp(_rwkv_prep_fn, *ins)
        grads = vjp(tuple(cts))
        for s in range(4):
            out_refs[s][...] = grads[2 * s]
            out_refs[4 + s][...] = grads[2 * s + 1]
        for i in range(11):
            _acc(out_refs[8 + i], grads[8 + i], first)

    ct_in = list(cts_a) + [c for c in cts_b if c is not None]
    row, lrow = _rows(tm, D), _rows(tm, LORA_PAD)
    f = jax.ShapeDtypeStruct
    zshapes = [f((S, D), F32)] * 3 + [f((S, LORA_PAD), F32)]
    pshapes = [f((1, D), F32)] * 3 + [f((1, LORA_PAD), F32)] + [f((1, D), F32)] * 4 + [f((LORA_W, D), F32), f((LORA_A, D), F32), f((G_PAD, D), F32)]
    return pl.pallas_call(
        body, name="rwkv_prep_bwd", grid=(S // tm,),
        in_specs=slabs + pspecs + [row] * n_ct,
        out_specs=[row, row, row, lrow] * 2 + pspecs,
        out_shape=zshapes * 2 + pshapes,
        compiler_params=_cparams(("arbitrary",)))(*([P] * 8), *params, *ct_in)


def _shift_add(a, b, tm=256):
    S, W = a.shape

    def body(a_ref, b_ref, h_ref, o_ref):
        last = pl.program_id(0) == pl.num_programs(0) - 1
        o_ref[...] = (a_ref[...] + _shift_up(b_ref[...], h_ref[...], 1, last)).astype(BF16)

    return pl.pallas_call(body, name="shift_add", grid=(S // tm,),
                          in_specs=[_rows(tm, W), _rows(tm, W), _next8(tm, W, S)],
                          out_specs=_rows(tm, W), out_shape=jax.ShapeDtypeStruct((S, W), BF16),
                          compiler_params=_cparams(("parallel",)))(a, b, b)


def _rwkv_post_fn(y, r, kmod, v, g, lnx_w, lnx_b, r_k):
    mean = _head_sum(y) * (1.0 / HEAD)
    yc = y - mean
    var = _head_sum(yc * yc) * (1.0 / HEAD)
    yn = yc * lax.rsqrt(var + GN_EPS) * lnx_w + lnx_b
    bonus = _head_sum(r * kmod * r_k) * v
    return (yn + bonus) * g


def _rwkv_post(y, r, kmod, v, g, lnx_w, lnx_b, r_k, tm=256):
    S = y.shape[0]

    def body(y_ref, r_ref, k_ref, v_ref, g_ref, w_ref, b_ref, rk_ref, o_ref):
        o_ref[...] = _rwkv_post_fn(y_ref[...], r_ref[...], k_ref[...], v_ref[...], g_ref[...],
                                   w_ref[...], b_ref[...], rk_ref[...]).astype(BF16)

    row, vec = _rows(tm, D), _full((1, D))
    return pl.pallas_call(body, name="rwkv_post", grid=(S // tm,), in_specs=[row] * 5 + [vec] * 3, out_specs=row,
                          out_shape=jax.ShapeDtypeStruct((S, D), BF16),
                          compiler_params=_cparams(("parallel",)))(y, r, kmod, v, g, lnx_w, lnx_b, r_k)


def _rwkv_post_bwd(drw, y, r, kmod, v, g, lnx_w, lnx_b, r_k, tm=256):
    S = y.shape[0]

    def body(d_ref, y_ref, r_ref, k_ref, v_ref, g_ref, w_ref, b_ref, rk_ref, *out_refs):
        first = pl.program_id(0) == 0
        _, vjp = jax.vjp(_rwkv_post_fn, y_ref[...], r_ref[...], k_ref[...], v_ref[...], g_ref[...],
                         w_ref[...], b_ref[...], rk_ref[...])
        grads = vjp(d_ref[...])
        for i in range(5):
            out_refs[i][...] = grads[i]
        for i in range(5, 8):
            _acc(out_refs[i], grads[i], first)

    row, vec = _rows(tm, D), _full((1, D))
    f = jax.ShapeDtypeStruct
    return pl.pallas_call(body, name="rwkv_post_bwd", grid=(S // tm,), in_specs=[row] * 6 + [vec] * 3,
                          out_specs=[row] * 5 + [vec] * 3, out_shape=[f((S, D), F32)] * 5 + [f((1, D), F32)] * 3,
                          compiler_params=_cparams(("arbitrary",)))(drw, y, r, kmod, v, g, lnx_w, lnx_b, r_k)


CHUNK = 32
CHUNK_TB = 256
_DOT_DIMS = {"nn": (((2,), (1,)), ((0,), (0,))), "nt": (((2,), (2,)), ((0,), (0,))), "tn": (((1,), (1,)), ((0,), (0,)))}


def _dot16(x, y, mode):
    return lax.dot_general(x.astype(BF16), y.astype(BF16), _DOT_DIMS[mode], preferred_element_type=F32)


@functools.partial(jax.custom_vjp, nondiff_argnums=(2,))
def _mm16(x, y, mode):
    return _dot16(x, y, mode)


def _mm16_fwd(x, y, mode):
    return _dot16(x, y, mode), (x, y)


def _mm16_bwd(mode, res, ct):
    x, y = res
    if mode == "nn":
        return _dot16(ct, y, "nt"), _dot16(x, ct, "tn")
    if mode == "nt":
        return _dot16(ct, y, "nn"), _dot16(ct, x, "tn")
    return _dot16(y, ct, "nt"), _dot16(x, ct, "nn")


_mm16.defvjp(_mm16_fwd, _mm16_bwd)


def _tri_sum(x, upper):
    T = x.shape[0]
    i = lax.broadcasted_iota(jnp.int32, (T, T), 0)
    j = lax.broadcasted_iota(jnp.int32, (T, T), 1)
    tri = ((j >= i) if upper else (i >= j)).astype(BF16)
    out, rest = None, x
    for _ in range(3):
        piece = rest.astype(BF16)
        rest = rest - piece.astype(F32)
        part = jnp.dot(tri, piece, preferred_element_type=F32)
        out = part if out is None else out + part
    return out


@jax.custom_vjp
def _cumsum_rows(x):
    return _tri_sum(x, False)


_cumsum_rows.defvjp(lambda x: (_tri_sum(x, False), None), lambda _, ct: (_tri_sum(ct, True),))


def _rows_to_cols(x):
    H, _, K = x.shape
    eye = (lax.broadcasted_iota(jnp.int32, (H, K, K), 1) == lax.broadcasted_iota(jnp.int32, (H, K, K), 2)).astype(F32)
    out = lax.dot_general(eye, jnp.broadcast_to(x, (H, SUBLANES, K)), _DOT_DIMS["nt"],
                          precision=lax.Precision.HIGHEST, preferred_element_type=F32)
    return out[:, :, 0:1]


def _per_head(x):
    return jnp.concatenate([x[:, h * HEAD:(h + 1) * HEAD][None] for h in range(N_HEADS)], axis=0)


def _chunk_fn(st0, r, lw, k, v, a, b):
    T = r.shape[0]
    cl = _cumsum_rows(lw)
    cl_end = cl[T - 1:T, :]
    inv = jnp.exp(-cl)
    to_end = jnp.exp(cl_end - cl)
    ah, rh, bh, kh, be, ke, v3 = [_per_head(x) for x in
                                  (a * jnp.exp(cl - lw), r * jnp.exp(cl), b * inv, k * inv, b * to_end, k * to_end, v)]
    i = lax.broadcasted_iota(jnp.int32, (N_HEADS, T, T), 1)
    j = lax.broadcasted_iota(jnp.int32, (N_HEADS, T, T), 2)
    a_ab = jnp.where(i > j, _mm16(ah, bh, "nt"), 0.0)
    a_ak = jnp.where(i > j, _mm16(ah, kh, "nt"), 0.0)
    m_rb = jnp.where(i >= j, _mm16(rh, bh, "nt"), 0.0)
    m_rk = jnp.where(i >= j, _mm16(rh, kh, "nt"), 0.0)
    rhs = _mm16(ah, st0, "nn") + _mm16(a_ak, v3, "nn")
    power, solve, n = a_ab, (i == j).astype(F32) + a_ab, 1
    while 2 * n < T:
        power = _mm16(power, power, "nn")
        solve = solve + _mm16(solve, power, "nn")
        n *= 2
    sa = _mm16(solve, rhs, "nn")
    y3 = _mm16(rh, st0, "nn") + _mm16(m_rb, sa, "nn") + _mm16(m_rk, v3, "nn")
    st_end = _rows_to_cols(_per_head(jnp.exp(cl_end))) * st0 + _mm16(be, sa, "tn") + _mm16(ke, v3, "tn")
    return jnp.concatenate([y3[h] for h in range(N_HEADS)], axis=1), st_end


def _cscan_fwd(r, lw, k, v, a, b):
    S = r.shape[0]
    per_blk = CHUNK_TB // CHUNK

    def body(r_ref, lw_ref, k_ref, v_ref, a_ref, b_ref, y_ref, ck_ref, st_ref):
        @pl.when(pl.program_id(0) == 0)
        def _():
            st_ref[...] = jnp.zeros_like(st_ref)

        def chunk(c, carry):
            rows = pl.ds(pl.multiple_of(c * CHUNK, CHUNK), CHUNK)
            st0 = st_ref[...]
            ck_ref[c] = st0
            y, st_end = _chunk_fn(st0, r_ref[rows, :], lw_ref[rows, :], k_ref[rows, :],
                                  v_ref[rows, :], a_ref[rows, :], b_ref[rows, :])
            y_ref[rows, :] = y
            st_ref[...] = st_end
            return carry

        lax.fori_loop(0, per_blk, chunk, 0)

    blk = _rows(CHUNK_TB, D)
    return pl.pallas_call(
        body, name="scan_fwd", grid=(S // CHUNK_TB,), in_specs=[blk] * 6,
        out_specs=[blk, pl.BlockSpec((per_blk, N_HEADS, HEAD, HEAD), lambda i: (i, 0, 0, 0))],
        out_shape=[jax.ShapeDtypeStruct((S, D), F32), jax.ShapeDtypeStruct((S // CHUNK, N_HEADS, HEAD, HEAD), F32)],
        scratch_shapes=[pltpu.VMEM((N_HEADS, HEAD, HEAD), F32)],
        compiler_params=_cparams(("arbitrary",)))(r, lw, k, v, a, b)


def _cscan_bwd(r, lw, k, v, a, b, ckpt, dy):
    S = r.shape[0]
    per_blk = CHUNK_TB // CHUNK
    nblk = S // CHUNK_TB

    def body(r_ref, lw_ref, k_ref, v_ref, a_ref, b_ref, ck_ref, dy_ref, *rest):
        out_refs, ds_ref = rest[:6], rest[6]

        @pl.when(pl.program_id(0) == 0)
        def _():
            ds_ref[...] = jnp.zeros_like(ds_ref)

        def chunk(cc, carry):
            c = per_blk - 1 - cc
            rows = pl.ds(pl.multiple_of(c * CHUNK, CHUNK), CHUNK)
            ins = (ck_ref[c], r_ref[rows, :], lw_ref[rows, :], k_ref[rows, :], v_ref[rows, :], a_ref[rows, :], b_ref[rows, :])
            _, vjp = jax.vjp(_chunk_fn, *ins)
            grads = vjp((dy_ref[rows, :], ds_ref[...]))
            ds_ref[...] = grads[0]
            for o_ref, g in zip(out_refs, grads[1:]):
                o_ref[rows, :] = g
            return carry

        lax.fori_loop(0, per_blk, chunk, 0)

    blk = pl.BlockSpec((CHUNK_TB, D), lambda i: (nblk - 1 - i, 0))
    shp = jax.ShapeDtypeStruct((S, D), F32)
    return pl.pallas_call(
        body, name="scan_bwd", grid=(nblk,),
        in_specs=[blk] * 6 + [pl.BlockSpec((per_blk, N_HEADS, HEAD, HEAD), lambda i: (nblk - 1 - i, 0, 0, 0)), blk],
        out_specs=[blk] * 6, out_shape=[shp] * 6,
        scratch_shapes=[pltpu.VMEM((N_HEADS, HEAD, HEAD), F32)],
        compiler_params=_cparams(("arbitrary",)))(r, lw, k, v, a, b, ckpt, dy)


def _ada_fwd(c8, w_ada, b_ada):
    def body(c_ref, w_ref, b_ref, o_ref):
        o_ref[...] = jnp.dot(c_ref[...].astype(BF16), w_ref[...], preferred_element_type=F32) + b_ref[...]

    tn = 1536
    return pl.pallas_call(body, name="ada_fwd", grid=(6 * D // tn,),
                          in_specs=[_full((SUBLANES, D)), pl.BlockSpec((D, tn), lambda j: (0, j)), pl.BlockSpec((1, tn), lambda j: (0, j))],
                          out_specs=pl.BlockSpec((SUBLANES, tn), lambda j: (0, j)),
                          out_shape=jax.ShapeDtypeStruct((SUBLANES, 6 * D), F32),
                          compiler_params=_cparams(("parallel",)))(c8, w_ada, b_ada)


def _outer(col, row):
    N = row.shape[1]
    tn = 1536

    def body(c_ref, r_ref, o_ref):
        o_ref[...] = c_ref[...] * r_ref[...]

    return pl.pallas_call(body, name="ada_wgrad", grid=(N // tn,),
                          in_specs=[_full((D, 1)), pl.BlockSpec((1, tn), lambda j: (0, j))],
                          out_specs=pl.BlockSpec((D, tn), lambda j: (0, j)),
                          out_shape=jax.ShapeDtypeStruct((D, N), F32),
                          compiler_params=_cparams(("parallel",)))(col, row)


def _exchange(srcs, broadcast, name):
    n = len(srcs)
    out_shape = [jax.ShapeDtypeStruct((N_DEV,) + (s.shape if broadcast else s.shape[1:]), s.dtype) for s in srcs]

    def body(*refs):
        src_refs, out_refs = refs[:n], refs[n:2 * n]
        send_sems, recv_sems, local_sems = refs[2 * n:]
        x, y, c = lax.axis_index("x"), lax.axis_index("y"), lax.axis_index("c")
        me = 4 * x + 2 * y + c

        def block(i, j):
            return src_refs[i] if broadcast else src_refs[i].at[j]

        def remote(i, d, src_slot, dst_slot):
            px, py, pc = x ^ (d >> 2), y ^ ((d >> 1) & 1), c ^ (d & 1)
            return pltpu.make_async_remote_copy(
                src_ref=block(i, src_slot), dst_ref=out_refs[i].at[dst_slot], send_sem=send_sems.at[i, d],
                recv_sem=recv_sems.at[i, d], device_id=(px, py, pc), device_id_type=_MESH)

        local = [pltpu.make_async_copy(block(i, me), out_refs[i].at[me], local_sems.at[i]) for i in range(n)]
        for cp in local:
            cp.start()
        sends = [remote(i, d, me ^ d, me) for d in range(1, N_DEV) for i in range(n)]
        for cp in sends:
            cp.start()
        for d in range(1, N_DEV):
            for i in range(n):
                remote(i, d, me, me ^ d).wait_recv()
        for cp in sends:
            cp.wait_send()
        for cp in local:
            cp.wait()

    any_spec = pl.BlockSpec(memory_space=pl.ANY)
    return pl.pallas_call(
        body, name=name, out_shape=out_shape, in_specs=[any_spec] * n, out_specs=[any_spec] * n,
        scratch_shapes=[pltpu.SemaphoreType.DMA((n, N_DEV)), pltpu.SemaphoreType.DMA((n, N_DEV)), pltpu.SemaphoreType.DMA((n,))],
        compiler_params=pltpu.CompilerParams(has_side_effects=True),
    )(*srcs)


def _sum_adam(parts, w, m, v, name):
    _, R, C = parts.shape
    tm = 256 if R % 256 == 0 else R
    c1 = 1.0 / (1.0 - ADAM_B1 ** ADAM_STEP)
    c2 = 1.0 / (1.0 - ADAM_B2 ** ADAM_STEP)

    def body(p_ref, w_ref, m_ref, v_ref, g_ref, d_ref, nm_ref, nv_ref):
        g = p_ref[0].astype(F32)
        for j in range(1, N_DEV):
            g = g + p_ref[j].astype(F32)
        nm = ADAM_B1 * m_ref[...] + (1.0 - ADAM_B1) * g
        nv = ADAM_B2 * v_ref[...] + (1.0 - ADAM_B2) * (g * g)
        g_ref[...] = g
        nm_ref[...] = nm
        nv_ref[...] = nv
        d_ref[...] = -ADAM_LR * ((nm * c1) / (jnp.sqrt(nv * c2) + ADAM_EPS) + ADAM_WD * w_ref[...])

    row = _rows(tm, C)
    shp = jax.ShapeDtypeStruct((R, C), F32)
    return pl.pallas_call(body, name=name, grid=(R // tm,),
                          in_specs=[pl.BlockSpec((N_DEV, tm, C), lambda i: (0, i, 0)), row, row, row],
                          out_specs=[row] * 4, out_shape=[shp] * 4,
                          compiler_params=_cparams(("parallel",)))(parts, w, m, v)


PACK_ALIGN = 16 * LANES
PACK_ROWS = 512 * LANES

SHARDED = (("w_ada", 1), ("w_in", 1), ("w2", 1), ("a2", 1), ("g2", 1), ("w_att_out", 1), ("w_rwkv_out", 0),
           ("w_o", 0), ("w_up", 1), ("conv_w", 1), ("w_down", 0))
REPLICATED = ("b_ada", "norm1_w", "b_gate", "mu_shift", "w0", "a0", "k_k", "k_a", "r_k", "lnx_w", "lnx_b",
              "norm2_w", "conv_b", "norm_f_w")
WEIGHTS = ("w_ada", "b_ada", "norm1_w", "w_in", "b_gate", "mu_shift", "w0", "w2", "a0", "a2", "g2", "k_k", "k_a", "r_k",
           "lnx_w", "lnx_b", "w_att_out", "w_rwkv_out", "w_o", "norm2_w", "w_up", "conv_w", "conv_b", "w_down", "norm_f_w")


def _pack(arrays):
    flat, layout, off = [], [], 0
    for i, a in enumerate(arrays):
        n = a.size
        pad = (-n) % PACK_ALIGN if i + 1 < len(arrays) else (-(off + n)) % PACK_ROWS
        flat.append(a.reshape(-1))
        if pad:
            flat.append(jnp.zeros((pad,), a.dtype))
        layout.append((off, n, a.shape))
        off += n + pad
    return jnp.concatenate(flat).reshape(-1, LANES), layout


def _unpack(buf, layout):
    flat = buf.reshape(-1)
    return [flat[off:off + n].reshape(shape) for off, n, shape in layout]


def _pad_w_in(w_in):
    rkv = w_in[:, ATT_IN:ATT_IN + 3 * D]
    lora = w_in[:, ATT_IN + 3 * D:ATT_IN + RWKV_IN]
    gates = w_in[:, ATT_IN + RWKV_IN:]
    att = w_in[:, :ATT_IN]
    lw, la, lg = lora[:, :LORA_W], lora[:, LORA_W:LORA_W + LORA_A], lora[:, LORA_W + LORA_A:]
    zeros = jnp.zeros((w_in.shape[0], LORA_PAD - LANES - LORA_G), w_in.dtype)
    return jnp.concatenate([rkv, gates, att, lw, la, lg, zeros], axis=1)


def _unpad_w_in(g):
    att = g[:, C_ATT:C_ATT + ATT_IN]
    rkv = g[:, C_R:C_R + 3 * D]
    lora = jnp.concatenate([g[:, C_LORA:C_LORA + LORA_W + LORA_A], g[:, C_LORA + LANES:C_LORA + LANES + LORA_G]], axis=1)
    gates = g[:, C_GA:C_GA + 2 * D]
    return jnp.concatenate([att, rkv, lora, gates], axis=1)


def _pad_mu(mu):
    lo = mu[:, 3 * D:]
    mu_l = jnp.concatenate([lo[:, :LORA_W + LORA_A], lo[:, LORA_W + LORA_A:], jnp.zeros((1, LORA_PAD - LANES - LORA_G), mu.dtype)], axis=1)
    return mu[:, :D], mu[:, D:2 * D], mu[:, 2 * D:3 * D], mu_l


def _local_step(x, c, W, target):
    S = x.shape[0]
    G = {}
    c8 = jnp.pad(c, ((0, SUBLANES - 1), (0, 0)))
    ada = _ada_fwd(c8, W["w_ada"], W["b_ada"])[0:1]
    sh1, sc1, gt1, sh2, sc2, gt2 = [ada[:, i * D:(i + 1) * D] for i in range(6)]
    h1, rstd1 = _norm_fwd(x, None, None, W["norm1_w"], sc1, sh1, "norm1_fwd")
    w_in_p = _pad_w_in(W["w_in"])
    P = _mm(h1, w_in_p, "nn", F32, "proj_in")

    o_g, l_g = zip(*[_att_fwd(P, g) for g in range(len(ATT_PATTERNS))])
    att = _att_combine_fwd(o_g, l_g)
    y_att = _mm(att, W["w_att_out"], "nn", F32, "att_out")

    mu_r, mu_k, mu_v, mu_l = _pad_mu(W["mu_shift"])
    g2p = jnp.pad(W["g2"], ((0, G_PAD - LORA_G), (0, 0)))
    prep_params = [mu_r, mu_k, mu_v, mu_l, W["w0"], W["a0"], W["k_k"], W["k_a"], W["w2"], W["a2"], g2p]
    r_, dec, kmod, v_, aa, bb, gg = _rwkv_prep(P, prep_params)
    y_scan, states = _cscan_fwd(r_, dec, kmod, v_, aa, bb)
    r_k = W["r_k"].reshape(1, D)
    rw = _rwkv_post(y_scan, r_, kmod, v_, gg, W["lnx_w"], W["lnx_b"], r_k)
    y_rwkv = _mm(rw, W["w_rwkv_out"], "nn", F32, "rwkv_out")

    bga, bgr = W["b_gate"][:, :D], W["b_gate"][:, D:]
    mix = _gate_fwd(P, bga, bgr, y_att, y_rwkv)
    mo = _mm(mix, W["w_o"], "nn", F32, "mix_out")
    x2, h2, rstd2 = _norm_fwd(x, mo, gt1, W["norm2_w"], sc2, sh2, "norm2_fwd")
    u = _mm(h2, W["w_up"], "nn", F32, "ffn_up")
    conv_w8 = jnp.pad(W["conv_w"], ((0, SUBLANES - 3), (0, 0)))
    act = _conv_fwd(u, conv_w8, W["conv_b"])
    f = _mm(act, W["w_down"], "nn", F32, "ffn_down")
    loss_blk, dx3, df, dgt2, G["norm_f_w"] = _final(x2, f, gt2, W["norm_f_w"], target)
    loss = loss_blk[0, 0]

    dact = _mm(df, W["w_down"], "nt", BF16, "ffn_down_dx")
    G["w_down"] = _mm(act, df, "tn", F32, "ffn_down_dw")
    duc, dwg, dwv, dbg, dbv = _conv_bwd_a(dact, u, conv_w8, W["conv_b"])
    G["conv_w"] = jnp.concatenate([dwg[0:3], dwv[0:3]], axis=1)
    G["conv_b"] = jnp.concatenate([dbg, dbv], axis=1)
    du = _conv_bwd_b(duc, conv_w8)
    dh2 = _mm(du, W["w_up"], "nt", F32, "ffn_up_dx")
    G["w_up"] = _mm(h2, du, "tn", F32, "ffn_up_dw")
    dx2, dsh2, dsc2, G["norm2_w"], dmo, dgt1 = _norm_bwd(dh2, x2, rstd2, W["norm2_w"], sc2, dx3, mo, gt1, "norm2_bwd")
    dmix = _mm(dmo, W["w_o"], "nt", F32, "mix_out_dx")
    G["w_o"] = _mm(mix, dmo, "tn", F32, "mix_out_dw")
    dy_att, dy_rwkv, dpga, dpgr, dbga, dbgr = _gate_bwd(dmix, P, bga, bgr, y_att, y_rwkv)
    G["b_gate"] = jnp.concatenate([dbga, dbgr], axis=1)

    datt = _mm(dy_att, W["w_att_out"], "nt", F32, "att_out_dx")
    G["w_att_out"] = _mm(att, dy_att, "tn", F32, "att_out_dw")
    dcomb = _att_combine_bwd(datt, o_g, l_g)
    dp_att = []
    for g in range(len(ATT_PATTERNS)):
        dp_att += _att_bwd(P, o_g[g], l_g[g], dcomb[g], dcomb[3 + g], g)

    drw = _mm(dy_rwkv, W["w_rwkv_out"], "nt", F32, "rwkv_out_dx")
    G["w_rwkv_out"] = _mm(rw, dy_rwkv, "tn", F32, "rwkv_out_dw")
    dy_scan, dr1, dk1, dv1, dgg, G["lnx_w"], G["lnx_b"], drk = _rwkv_post_bwd(drw, y_scan, r_, kmod, v_, gg, W["lnx_w"], W["lnx_b"], r_k)
    G["r_k"] = drk.reshape(W["r_k"].shape)
    dr2, ddec, dk2, dv2, daa, dbb = _cscan_bwd(r_, dec, kmod, v_, aa, bb, states, dy_scan)
    pb = _rwkv_prep_bwd(P, prep_params, [dr2, ddec, dk2, dv2, daa, dbb, dgg], [dr1, None, dk1, dv1, None, None, None])
    dz, dzp, dpar = pb[0:4], pb[4:8], pb[8:]
    dp_rkv = [_shift_add(dz[i], dzp[i]) for i in range(3)]
    dp_lora = _shift_add(dz[3], dzp[3])
    dmu_r, dmu_k, dmu_v, dmu_l, G["w0"], G["a0"], G["k_k"], G["k_a"], G["w2"], G["a2"], dg2p = dpar
    G["g2"] = dg2p[0:LORA_G]
    G["mu_shift"] = jnp.concatenate([dmu_r, dmu_k, dmu_v, dmu_l[:, :LORA_W + LORA_A], dmu_l[:, LANES:LANES + LORA_G]], axis=1)

    dP = jnp.concatenate(dp_rkv + [dpga, dpgr] + dp_att + [dp_lora], axis=1)
    dh1 = _mm(dP, w_in_p, "nt", F32, "proj_in_dx")
    G["w_in"] = _unpad_w_in(_mm(h1, dP, "tn", F32, "proj_in_dw"))
    grad_x, dsh1, dsc1, G["norm1_w"] = _norm_bwd(dh1, x, rstd1, W["norm1_w"], sc1, dx2, None, None, "norm1_bwd")
    dada = jnp.concatenate([dsh1, dsc1, dgt1, dsh2, dsc2, dgt2], axis=1)
    G["b_ada"] = dada
    G["w_ada"] = _outer(c.reshape(D, 1), dada)
    return loss, grad_x, G


def _full_weight(gathered, axis):
    _, rows, cols = gathered.shape
    if axis == 0:
        return gathered.reshape(N_DEV * rows, cols)
    return gathered.transpose(1, 0, 2).reshape(rows, N_DEV * cols)


def _owner_blocks(g, axis):
    rows, cols = g.shape
    g = g.astype(BF16)
    if axis == 0:
        return g.reshape(N_DEV, rows // N_DEV, cols)
    return g.reshape(rows, N_DEV, cols // N_DEV).transpose(1, 0, 2)


def kernel(x, c, w_ada, b_ada, norm1_w, w_in, b_gate, mu_shift, w0, w2, a0, a2, g2, k_k, k_a, r_k, lnx_w, lnx_b, w_att_out, w_rwkv_out, w_o, norm2_w, w_up, conv_w, conv_b, w_down, norm_f_w, loss_target, m_w_ada, m_b_ada, m_norm1_w, m_w_in, m_b_gate, m_mu_shift, m_w0, m_w2, m_a0, m_a2, m_g2, m_k_k, m_k_a, m_r_k, m_lnx_w, m_lnx_b, m_w_att_out, m_w_rwkv_out, m_w_o, m_norm2_w, m_w_up, m_conv_w, m_conv_b, m_w_down, m_norm_f_w, v_w_ada, v_b_ada, v_norm1_w, v_w_in, v_b_gate, v_mu_shift, v_w0, v_w2, v_a0, v_a2, v_g2, v_k_k, v_k_a, v_r_k, v_lnx_w, v_lnx_b, v_w_att_out, v_w_rwkv_out, v_w_o, v_norm2_w, v_w_up, v_conv_w, v_conv_b, v_w_down, v_norm_f_w):
    env = dict(locals())
    w_shard = {n: env[n] for n in WEIGHTS}
    m_shard = {n: env["m_" + n] for n in WEIGHTS}
    v_shard = {n: env["v_" + n] for n in WEIGHTS}

    gathered = _exchange([w_shard[n][0].astype(BF16) for n, _ in SHARDED], True, "gather_weights")
    W = {n: _full_weight(g, axis) for (n, axis), g in zip(SHARDED, gathered)}
    for n in REPLICATED:
        W[n] = w_shard[n].reshape(1, -1) if n != "r_k" else w_shard[n][0]

    loss, grad_x, G = _local_step(x[0], c, W, loss_target[0])
    loss = lax.psum(loss, ("x", "y", "c"))

    parts = _exchange([_owner_blocks(G[n], axis) for n, axis in SHARDED], False, "scatter_grads")
    out = {}
    for (n, _), p in zip(SHARDED, parts):
        res = _sum_adam(p, w_shard[n][0], m_shard[n][0], v_shard[n][0], "adam_" + n)
        for kind, a in zip(("grad", "delta", "new_m", "new_v"), res):
            out[kind, n] = a[None]

    small, slayout = _pack([G[n].reshape(-1) for n in REPLICATED])
    sparts, = _exchange([small], True, "gather_small_grads")
    sw, _ = _pack([w_shard[n].reshape(-1) for n in REPLICATED])
    sm, _ = _pack([m_shard[n].reshape(-1) for n in REPLICATED])
    sv, _ = _pack([v_shard[n].reshape(-1) for n in REPLICATED])
    res = _sum_adam(sparts, sw, sm, sv, "adam_replicated")
    for kind, buf in zip(("grad", "delta", "new_m", "new_v"), res):
        for n, a in zip(REPLICATED, _unpack(buf, slayout)):
            out[kind, n] = a.reshape(w_shard[n].shape)

    return (loss, grad_x[None], *[out[kind, n] for kind in ("grad", "delta", "new_m", "new_v") for n in WEIGHTS])
```

```python
import functools
import math

import jax
import jax.numpy as jnp
from jax import lax
from jax.experimental import pallas as pl
from jax.experimental.pallas import tpu as pltpu

F32 = jnp.float32
BF16 = jnp.bfloat16

D = 1024
HEAD = 64
ATT_PATTERNS = ((128, 1), (512, 4), (2048, 16))
ATT_HEADS = 8
ATT_W = ATT_HEADS * HEAD
ATT_IN = 3 * 3 * ATT_W
QBLK = 128
N_HEADS = D // HEAD
LORA_W, LORA_A, LORA_G = 64, 64, 160
RWKV_IN = 3 * D + LORA_W + LORA_A + LORA_G
N_IN = ATT_IN + RWKV_IN + 2 * D
D_FF = 2816
RMS_EPS = 1e-6
GN_EPS = 64e-5
N_DEV = 8
LANES = 128
SUBLANES = 8

C_R, C_K, C_V, C_GA, C_GR = 0, 1024, 2048, 3072, 4096
C_ATT = 5120
C_LORA = C_ATT + ATT_IN
LORA_PAD = 512
G_PAD = 256
N_PAD = C_LORA + LORA_PAD

ADAM_LR, ADAM_B1, ADAM_B2, ADAM_EPS, ADAM_WD, ADAM_STEP = 0.001, 0.9, 0.999, 1e-08, 0.01, 10

VMEM_LIMIT = 56 * 1024 * 1024

_MESH = pl.DeviceIdType.MESH


def _cparams(sem):
    return pltpu.CompilerParams(dimension_semantics=sem, vmem_limit_bytes=VMEM_LIMIT)


def _tile(dim, pref):
    if dim <= pref:
        return dim
    best = None
    for t in range(LANES, pref + 1, LANES):
        if dim % t == 0:
            best = t
    assert best is not None, dim
    return best


MM_TILES = {"nn": (1024, 1408, 1408), "nt": (512, 2048, 1408), "tn": (1408, 1408, 1024)}


def _mm(a, b, mode, out_dtype, name):
    if mode == "nn":
        (M, K), (K2, N) = a.shape, b.shape
    elif mode == "nt":
        (M, K), (N, K2) = a.shape, b.shape
    else:
        (K, M), (K2, N) = a.shape, b.shape
    assert K == K2, (a.shape, b.shape, mode)
    tm, tn, tk = (_tile(dim, pref) for dim, pref in zip((M, N, K), MM_TILES[mode]))
    nk = K // tk
    dims = {"nn": (((1,), (0,)), ((), ())), "nt": (((1,), (1,)), ((), ())), "tn": (((0,), (0,)), ((), ()))}[mode]

    def body(a_ref, b_ref, o_ref, acc_ref):
        k = pl.program_id(2)
        part = lax.dot_general(a_ref[...].astype(BF16), b_ref[...].astype(BF16), dims,
                               preferred_element_type=F32)
        if nk == 1:
            o_ref[...] = part.astype(o_ref.dtype)
            return

        @pl.when(k == 0)
        def _():
            acc_ref[...] = part

        @pl.when(jnp.logical_and(k > 0, k < nk - 1))
        def _():
            acc_ref[...] += part

        @pl.when(k == nk - 1)
        def _():
            o_ref[...] = (acc_ref[...] + part).astype(o_ref.dtype)

    a_spec = pl.BlockSpec((tk, tm), lambda i, j, k: (k, i)) if mode == "tn" else pl.BlockSpec((tm, tk), lambda i, j, k: (i, k))
    b_spec = pl.BlockSpec((tn, tk), lambda i, j, k: (j, k)) if mode == "nt" else pl.BlockSpec((tk, tn), lambda i, j, k: (k, j))
    return pl.pallas_call(
        body, name=name, grid=(M // tm, N // tn, nk),
        in_specs=[a_spec, b_spec],
        out_specs=pl.BlockSpec((tm, tn), lambda i, j, k: (i, j)),
        out_shape=jax.ShapeDtypeStruct((M, N), out_dtype),
        scratch_shapes=[pltpu.VMEM((tm, tn) if nk > 1 else (SUBLANES, LANES), F32)],
        compiler_params=_cparams(("parallel", "parallel", "arbitrary")),
    )(a, b)


def _rows(tm, w, col=0):
    return pl.BlockSpec((tm, w), lambda i: (i, col))


def _full(shape):
    return pl.BlockSpec(shape, lambda i: (0,) * len(shape))


def _prev8(tm, w, col=0):
    return pl.BlockSpec((SUBLANES, w), lambda i: (jnp.maximum(i * (tm // SUBLANES) - 1, 0), col))


def _next8(tm, w, n_rows, col=0):
    last = n_rows // SUBLANES - 1
    return pl.BlockSpec((SUBLANES, w), lambda i: (jnp.minimum((i + 1) * (tm // SUBLANES), last), col))


def _shift_down(x, halo, k, first):
    rolled = pltpu.roll(x, k, 0)
    row = lax.broadcasted_iota(jnp.int32, x.shape, 0)
    out = rolled
    for j in range(k):
        h = jnp.where(first, 0.0, halo[SUBLANES - k + j:SUBLANES - k + j + 1, :])
        out = jnp.where(row == j, h, out)
    return out


def _shift_up(x, halo, k, last):
    n = x.shape[0]
    rolled = pltpu.roll(x, n - k, 0)
    row = lax.broadcasted_iota(jnp.int32, x.shape, 0)
    out = rolled
    for j in range(k):
        h = jnp.where(last, 0.0, halo[j:j + 1, :])
        out = jnp.where(row == n - k + j, h, out)
    return out


def _acc(ref, val, first):
    @pl.when(first)
    def _():
        ref[...] = val

    @pl.when(jnp.logical_not(first))
    def _():
        ref[...] += val


def _colsum(x):
    return jnp.sum(x, axis=0, keepdims=True)


def _norm_fwd(x, mo, gt, nw, sc, sh, name, tm=256):
    S = x.shape[0]
    has_res = mo is not None

    def body(*refs):
        if has_res:
            x_ref, mo_ref, gt_ref, nw_ref, sc_ref, sh_ref, x2_ref, h_ref, rs_ref = refs
            x2 = x_ref[...] + gt_ref[...] * mo_ref[...]
            x2_ref[...] = x2
        else:
            x_ref, nw_ref, sc_ref, sh_ref, h_ref, rs_ref = refs
            x2 = x_ref[...]
        rstd = lax.rsqrt(jnp.mean(x2 * x2, axis=-1, keepdims=True) + RMS_EPS)
        rs_ref[...] = rstd
        h_ref[...] = ((x2 * rstd * nw_ref[...]) * (1.0 + sc_ref[...]) + sh_ref[...]).astype(BF16)

    vec = _full((1, D))
    ins = [x, mo, gt, nw, sc, sh] if has_res else [x, nw, sc, sh]
    in_specs = [_rows(tm, D), _rows(tm, D), vec, vec, vec, vec] if has_res else [_rows(tm, D), vec, vec, vec]
    outs = [jax.ShapeDtypeStruct((S, D), BF16), jax.ShapeDtypeStruct((S, 1), F32)]
    out_specs = [_rows(tm, D), _rows(tm, 1)]
    if has_res:
        outs = [jax.ShapeDtypeStruct((S, D), F32)] + outs
        out_specs = [_rows(tm, D)] + out_specs
    return pl.pallas_call(body, name=name, grid=(S // tm,), in_specs=in_specs, out_specs=out_specs,
                          out_shape=outs, compiler_params=_cparams(("parallel",)))(*ins)


def _norm_bwd(dh, xin, rstd, nw, sc, dres, mo, gt, name, tm=256):
    S = xin.shape[0]
    has_res = mo is not None

    def body(*refs):
        if has_res:
            dh_ref, x_ref, rs_ref, nw_ref, sc_ref, dres_ref, mo_ref, gt_ref, dx_ref, dsh_ref, dsc_ref, dnw_ref, dmo_ref, dgt_ref = refs
        else:
            dh_ref, x_ref, rs_ref, nw_ref, sc_ref, dres_ref, dx_ref, dsh_ref, dsc_ref, dnw_ref = refs
        first = pl.program_id(0) == 0
        dh = dh_ref[...]
        rstd = rs_ref[...]
        n = x_ref[...] * rstd
        w = nw_ref[...]
        _acc(dsh_ref, _colsum(dh), first)
        _acc(dsc_ref, _colsum(dh * (n * w)), first)
        dnw = dh * (1.0 + sc_ref[...])
        _acc(dnw_ref, _colsum(dnw * n), first)
        dn = dnw * w
        dx = dres_ref[...] + rstd * (dn - n * jnp.mean(dn * n, axis=-1, keepdims=True))
        dx_ref[...] = dx
        if has_res:
            dmo_ref[...] = (dx * gt_ref[...]).astype(BF16)
            _acc(dgt_ref, _colsum(dx * mo_ref[...]), first)

    vec = _full((1, D))
    vshape = jax.ShapeDtypeStruct((1, D), F32)
    ins = [dh, xin, rstd, nw, sc, dres] + ([mo, gt] if has_res else [])
    in_specs = [_rows(tm, D), _rows(tm, D), _rows(tm, 1), vec, vec, _rows(tm, D)] + ([_rows(tm, D), vec] if has_res else [])
    outs = [jax.ShapeDtypeStruct((S, D), F32), vshape, vshape, vshape]
    out_specs = [_rows(tm, D), vec, vec, vec]
    if has_res:
        outs += [jax.ShapeDtypeStruct((S, D), BF16), vshape]
        out_specs += [_rows(tm, D), vec]
    return pl.pallas_call(body, name=name, grid=(S // tm,), in_specs=in_specs, out_specs=out_specs,
                          out_shape=outs, compiler_params=_cparams(("arbitrary",)))(*ins)


def _final(x2, f, gt2, nfw, target, tm=256):
    S = x2.shape[0]

    def body(x2_ref, f_ref, gt_ref, w_ref, t_ref, loss_ref, dx_ref, df_ref, dgt_ref, dw_ref):
        first = pl.program_id(0) == 0
        f = f_ref[...]
        gt = gt_ref[...]
        w = w_ref[...]
        x3 = x2_ref[...] + gt * f
        rstd = lax.rsqrt(jnp.mean(x3 * x3, axis=-1, keepdims=True) + RMS_EPS)
        n = x3 * rstd
        e = n * w - t_ref[...]
        part = 0.5 * jnp.sum(jnp.mean(e * e, axis=-1, keepdims=True), axis=0, keepdims=True)
        _acc(loss_ref, jnp.broadcast_to(part, (SUBLANES, LANES)), first)
        dy = e * (1.0 / D)
        _acc(dw_ref, _colsum(dy * n), first)
        dn = dy * w
        dx = rstd * (dn - n * jnp.mean(dn * n, axis=-1, keepdims=True))
        dx_ref[...] = dx
        df_ref[...] = (dx * gt).astype(BF16)
        _acc(dgt_ref, _colsum(dx * f), first)

    vec = _full((1, D))
    vshape = jax.ShapeDtypeStruct((1, D), F32)
    return pl.pallas_call(
        body, name="final_loss", grid=(S // tm,),
        in_specs=[_rows(tm, D), _rows(tm, D), vec, vec, _rows(tm, D)],
        out_specs=[_full((SUBLANES, LANES)), _rows(tm, D), _rows(tm, D), vec, vec],
        out_shape=[jax.ShapeDtypeStruct((SUBLANES, LANES), F32), jax.ShapeDtypeStruct((S, D), F32),
                   jax.ShapeDtypeStruct((S, D), BF16), vshape, vshape],
        compiler_params=_cparams(("arbitrary",)))(x2, f, gt2, nfw, target)


def _gate_fwd(P, bga, bgr, y_att, y_rwkv, tm=256):
    S = P.shape[0]

    def body(pa_ref, pr_ref, ba_ref, br_ref, ya_ref, yr_ref, mix_ref):
        ga = jax.nn.sigmoid(pa_ref[...] + ba_ref[...])
        gr = jax.nn.sigmoid(pr_ref[...] + br_ref[...])
        mix_ref[...] = (ga * ya_ref[...] + gr * yr_ref[...]).astype(BF16)

    vec = _full((1, D))
    return pl.pallas_call(
        body, name="gate_fwd", grid=(S // tm,),
        in_specs=[_rows(tm, D, C_GA // D), _rows(tm, D, C_GR // D), vec, vec, _rows(tm, D), _rows(tm, D)],
        out_specs=_rows(tm, D), out_shape=jax.ShapeDtypeStruct((S, D), BF16),
        compiler_params=_cparams(("parallel",)))(P, P, bga, bgr, y_att, y_rwkv)


def _gate_bwd(dmix, P, bga, bgr, y_att, y_rwkv, tm=256):
    S = P.shape[0]

    def body(dm_ref, pa_ref, pr_ref, ba_ref, br_ref, ya_ref, yr_ref, dya_ref, dyr_ref, dpa_ref, dpr_ref, dba_ref, dbr_ref):
        first = pl.program_id(0) == 0
        dm = dm_ref[...]
        ga = jax.nn.sigmoid(pa_ref[...] + ba_ref[...])
        gr = jax.nn.sigmoid(pr_ref[...] + br_ref[...])
        dya_ref[...] = (dm * ga).astype(BF16)
        dyr_ref[...] = (dm * gr).astype(BF16)
        dpa = dm * ya_ref[...] * ga * (1.0 - ga)
        dpr = dm * yr_ref[...] * gr * (1.0 - gr)
        dpa_ref[...] = dpa.astype(BF16)
        dpr_ref[...] = dpr.astype(BF16)
        _acc(dba_ref, _colsum(dpa), first)
        _acc(dbr_ref, _colsum(dpr), first)

    vec = _full((1, D))
    row = _rows(tm, D)
    rshape = jax.ShapeDtypeStruct((S, D), BF16)
    vshape = jax.ShapeDtypeStruct((1, D), F32)
    return pl.pallas_call(
        body, name="gate_bwd", grid=(S // tm,),
        in_specs=[row, _rows(tm, D, C_GA // D), _rows(tm, D, C_GR // D), vec, vec, row, row],
        out_specs=[row, row, row, row, vec, vec],
        out_shape=[rshape, rshape, rshape, rshape, vshape, vshape],
        compiler_params=_cparams(("arbitrary",)))(dmix, P, P, bga, bgr, y_att, y_rwkv)


CONV_TN = D_FF // 2


def _conv_fwd(u, conv_w8, conv_b, tm=256, tn=CONV_TN):
    S = u.shape[0]
    nj = D_FF // tn

    def conv(u_ref, h_ref, w_ref, b_ref, first):
        u = u_ref[...]
        h = h_ref[...]
        w = w_ref[...]
        return b_ref[...] + w[0:1] * _shift_down(u, h, 2, first) + w[1:2] * _shift_down(u, h, 1, first) + w[2:3] * u

    def body(ug_ref, hg_ref, uv_ref, hv_ref, wg_ref, wv_ref, bg_ref, bv_ref, act_ref):
        first = pl.program_id(0) == 0
        g = conv(ug_ref, hg_ref, wg_ref, bg_ref, first)
        v = conv(uv_ref, hv_ref, wv_ref, bv_ref, first)
        act_ref[...] = (g * jax.nn.sigmoid(g) * v).astype(BF16)

    blk = lambda off: pl.BlockSpec((tm, tn), lambda i, j: (i, j + off))
    halo = lambda off: pl.BlockSpec((SUBLANES, tn), lambda i, j: (jnp.maximum(i * (tm // SUBLANES) - 1, 0), j + off))
    wsp = lambda off: pl.BlockSpec((SUBLANES, tn), lambda i, j: (0, j + off))
    bsp = lambda off: pl.BlockSpec((1, tn), lambda i, j: (0, j + off))
    return pl.pallas_call(
        body, name="conv_fwd", grid=(S // tm, nj),
        in_specs=[blk(0), halo(0), blk(nj), halo(nj), wsp(0), wsp(nj), bsp(0), bsp(nj)],
        out_specs=pl.BlockSpec((tm, tn), lambda i, j: (i, j)),
        out_shape=jax.ShapeDtypeStruct((S, D_FF), BF16),
        compiler_params=_cparams(("parallel", "parallel")))(u, u, u, u, conv_w8, conv_w8, conv_b, conv_b)


def _conv_bwd_a(dact, u, conv_w8, conv_b, tm=256, tn=CONV_TN):
    S = u.shape[0]
    nj = D_FF // tn

    def half(u_ref, h_ref, w_ref, b_ref, first):
        u = u_ref[...]
        h = h_ref[...]
        w = w_ref[...]
        u2, u1 = _shift_down(u, h, 2, first), _shift_down(u, h, 1, first)
        return b_ref[...] + w[0:1] * u2 + w[1:2] * u1 + w[2:3] * u, (u2, u1, u)

    def wgrad(d, taps):
        z = jnp.zeros((SUBLANES - 3, d.shape[1]), F32)
        return jnp.concatenate([_colsum(d * taps[0]), _colsum(d * taps[1]), _colsum(d * taps[2]), z], axis=0)

    def body(da_ref, ug_ref, hg_ref, uv_ref, hv_ref, wg_ref, wv_ref, bg_ref, bv_ref,
             d_ref, dwg_ref, dwv_ref, dbg_ref, dbv_ref):
        first = pl.program_id(1) == 0
        g, tg = half(ug_ref, hg_ref, wg_ref, bg_ref, first)
        v, tv = half(uv_ref, hv_ref, wv_ref, bv_ref, first)
        da = da_ref[...].astype(F32)
        sg = jax.nn.sigmoid(g)
        dg = da * v * (sg * (1.0 + g * (1.0 - sg)))
        dv = da * (g * sg)
        d_ref[0] = dg
        d_ref[1] = dv
        _acc(dwg_ref, wgrad(dg, tg), first)
        _acc(dwv_ref, wgrad(dv, tv), first)
        _acc(dbg_ref, _colsum(dg), first)
        _acc(dbv_ref, _colsum(dv), first)

    blk = lambda off: pl.BlockSpec((tm, tn), lambda j, i: (i, j + off))
    halo = lambda off: pl.BlockSpec((SUBLANES, tn), lambda j, i: (jnp.maximum(i * (tm // SUBLANES) - 1, 0), j + off))
    wsp = lambda off: pl.BlockSpec((SUBLANES, tn), lambda j, i: (0, j + off))
    bsp = lambda off: pl.BlockSpec((1, tn), lambda j, i: (0, j + off))
    f = jax.ShapeDtypeStruct
    outs = pl.pallas_call(
        body, name="conv_bwd_a", grid=(nj, S // tm),
        in_specs=[pl.BlockSpec((tm, tn), lambda j, i: (i, j)), blk(0), halo(0), blk(nj), halo(nj), wsp(0), wsp(nj), bsp(0), bsp(nj)],
        out_specs=[pl.BlockSpec((2, tm, tn), lambda j, i: (0, i, j)),
                   pl.BlockSpec((SUBLANES, tn), lambda j, i: (0, j)), pl.BlockSpec((SUBLANES, tn), lambda j, i: (0, j)),
                   pl.BlockSpec((1, tn), lambda j, i: (0, j)), pl.BlockSpec((1, tn), lambda j, i: (0, j))],
        out_shape=[f((2, S, D_FF), F32), f((SUBLANES, D_FF), F32), f((SUBLANES, D_FF), F32),
                   f((1, D_FF), F32), f((1, D_FF), F32)],
        compiler_params=_cparams(("parallel", "arbitrary")))(dact, u, u, u, u, conv_w8, conv_w8, conv_b, conv_b)
    return outs


def _conv_bwd_b(duc, conv_w8, tm=256, tn=CONV_TN):
    _, S, W = duc.shape
    nj = W // tn
    n_rows = S // tm

    def body(d_ref, h_ref, w_ref, o_ref):
        last = pl.program_id(0) == n_rows - 1
        d = d_ref[...]
        h = h_ref[...]
        w = w_ref[...]
        o_ref[...] = (w[2:3] * d + w[1:2] * _shift_up(d, h, 1, last) + w[0:1] * _shift_up(d, h, 2, last)).astype(BF16)

    last_tile = S // SUBLANES - 1
    return pl.pallas_call(
        body, name="conv_bwd_b", grid=(n_rows, 2 * nj),
        in_specs=[pl.BlockSpec((None, tm, tn), lambda i, j: (j // nj, i, j % nj)),
                  pl.BlockSpec((None, SUBLANES, tn), lambda i, j: (j // nj, jnp.minimum((i + 1) * (tm // SUBLANES), last_tile), j % nj)),
                  pl.BlockSpec((SUBLANES, tn), lambda i, j: (0, j))],
        out_specs=pl.BlockSpec((tm, tn), lambda i, j: (i, j)),
        out_shape=jax.ShapeDtypeStruct((S, 2 * W), BF16),
        compiler_params=_cparams(("parallel", "parallel")))(duc, duc, conv_w8)


ATT_SCALE = HEAD ** -0.5
NEG = -1e30
ATT_PAIRS = ATT_HEADS // 2


def _att_rows(n, d, S):
    per = S // (QBLK * d)
    r, m = n // per, n % per
    cur = pl.ds(m * (QBLK * d) + r, QBLK, stride=d)
    prv = pl.ds(jnp.maximum(m - 1, 0) * (QBLK * d) + r, QBLK, stride=d)
    return cur, prv, m > 0


def _att_slab(g, j):
    return (C_ATT + g * 3 * ATT_W + j * ATT_W) // LANES


def _heads(x):
    return x[:, 0:HEAD], x[:, HEAD:2 * HEAD]


def _att_scores(q, kc, kp, has_prev):
    qi = lax.broadcasted_iota(jnp.int32, (QBLK, QBLK), 0)
    kj = lax.broadcasted_iota(jnp.int32, (QBLK, QBLK), 1)
    nt = (((1,), (1,)), ((), ()))
    s_c = lax.dot_general(q, kc, nt, preferred_element_type=F32) * ATT_SCALE
    s_p = lax.dot_general(q, kp, nt, preferred_element_type=F32) * ATT_SCALE
    s_c = jnp.where(kj <= qi, s_c, NEG)
    s_p = jnp.where(jnp.logical_and(kj >= qi, has_prev), s_p, NEG)
    return s_c, s_p


def _att_fwd(P, g):
    S = P.shape[0]
    d = ATT_PATTERNS[g][1]

    def body(q_ref, k_ref, v_ref, o_ref, l_ref):
        def blk(n, carry):
            cur, prv, has_prev = _att_rows(n, d, S)
            q2, kc2, kp2 = q_ref[cur, :].astype(BF16), k_ref[cur, :].astype(BF16), k_ref[prv, :].astype(BF16)
            vc2, vp2 = v_ref[cur, :].astype(BF16), v_ref[prv, :].astype(BF16)
            outs, lses = [], []
            for q, kc, kp, vc, vp in zip(_heads(q2), _heads(kc2), _heads(kp2), _heads(vc2), _heads(vp2)):
                s_c, s_p = _att_scores(q, kc, kp, has_prev)
                m = jnp.maximum(jnp.max(s_c, axis=1, keepdims=True), jnp.max(s_p, axis=1, keepdims=True))
                p_c = jnp.exp(s_c - m)
                p_p = jnp.exp(s_p - m)
                den = jnp.sum(p_c, axis=1, keepdims=True) + jnp.sum(p_p, axis=1, keepdims=True)
                num = (jnp.dot(p_c.astype(BF16), vc, preferred_element_type=F32)
                       + jnp.dot(p_p.astype(BF16), vp, preferred_element_type=F32))
                outs.append(num / den)
                lses.append(jnp.broadcast_to(m + jnp.log(den), (QBLK, HEAD)))
            o_ref[cur, :] = jnp.concatenate(outs, axis=1)
            l_ref[cur, :] = jnp.concatenate(lses, axis=1)
            return carry

        lax.fori_loop(0, S // QBLK, blk, 0, unroll=2)

    slab = lambda j: pl.BlockSpec((S, LANES), lambda i: (0, _att_slab(g, j) + i))
    out = pl.BlockSpec((S, LANES), lambda i: (0, i))
    shp = jax.ShapeDtypeStruct((S, ATT_W), F32)
    return pl.pallas_call(body, name=f"att_fwd_g{g}", grid=(ATT_PAIRS,), in_specs=[slab(0), slab(1), slab(2)],
                          out_specs=[out, out], out_shape=[shp, shp], compiler_params=_cparams(("parallel",)))(P, P, P)


def _att_bwd(P, o, l, do, dl, g):
    S = P.shape[0]
    d = ATT_PATTERNS[g][1]
    tn = (((0,), (0,)), ((), ()))
    nt = (((1,), (1,)), ((), ()))

    def body(q_ref, k_ref, v_ref, o_ref, l_ref, do_ref, dl_ref, dq_ref, dk_ref, dv_ref, dq_acc, dk_acc, dv_acc):
        dk_acc[...] = jnp.zeros_like(dk_acc)
        dv_acc[...] = jnp.zeros_like(dv_acc)

        def blk(n, carry):
            cur, prv, has_prev = _att_rows(n, d, S)
            q2, kc2, kp2 = q_ref[cur, :].astype(BF16), k_ref[cur, :].astype(BF16), k_ref[prv, :].astype(BF16)
            vc2, vp2 = v_ref[cur, :].astype(BF16), v_ref[prv, :].astype(BF16)
            do2 = do_ref[cur, :]
            dd2 = do2 * o_ref[cur, :] - dl_ref[cur, :]
            l2 = l_ref[cur, :]
            res = []
            for q, kc, kp, vc, vp, dob, dd, lse in zip(_heads(q2), _heads(kc2), _heads(kp2), _heads(vc2), _heads(vp2),
                                                     _heads(do2), _heads(dd2), _heads(l2)):
                s_c, s_p = _att_scores(q, kc, kp, has_prev)
                p_c = jnp.exp(s_c - lse[:, 0:1])
                p_p = jnp.exp(s_p - lse[:, 0:1])
                delta = jnp.sum(dd, axis=1, keepdims=True)
                dob16 = dob.astype(BF16)
                dp_c = lax.dot_general(dob16, vc, nt, preferred_element_type=F32)
                dp_p = lax.dot_general(dob16, vp, nt, preferred_element_type=F32)
                ds_c = (p_c * (dp_c - delta) * ATT_SCALE).astype(BF16)
                ds_p = (p_p * (dp_p - delta) * ATT_SCALE).astype(BF16)
                res.append((
                    jnp.dot(ds_c, kc, preferred_element_type=F32) + jnp.dot(ds_p, kp, preferred_element_type=F32),
                    lax.dot_general(ds_c, q, tn, preferred_element_type=F32),
                    lax.dot_general(ds_p, q, tn, preferred_element_type=F32),
                    lax.dot_general(p_c.astype(BF16), dob16, tn, preferred_element_type=F32),
                    lax.dot_general(p_p.astype(BF16), dob16, tn, preferred_element_type=F32)))
            both = [jnp.concatenate([res[0][i], res[1][i]], axis=1) for i in range(5)]
            dq_acc[cur, :] = both[0]
            dk_acc[cur, :] += both[1]
            dv_acc[cur, :] += both[3]
            dk_acc[prv, :] += both[2]
            dv_acc[prv, :] += both[4]
            return carry

        lax.fori_loop(0, S // QBLK, blk, 0, unroll=2)
        dq_ref[...] = dq_acc[...].astype(BF16)
        dk_ref[...] = dk_acc[...].astype(BF16)
        dv_ref[...] = dv_acc[...].astype(BF16)

    slab = lambda j: pl.BlockSpec((S, LANES), lambda i: (0, _att_slab(g, j) + i))
    blk128 = pl.BlockSpec((S, LANES), lambda i: (0, i))
    shp = jax.ShapeDtypeStruct((S, ATT_W), BF16)
    return pl.pallas_call(body, name=f"att_bwd_g{g}", grid=(ATT_PAIRS,),
                          in_specs=[slab(0), slab(1), slab(2)] + [blk128] * 4, out_specs=[blk128] * 3, out_shape=[shp] * 3,
                          scratch_shapes=[pltpu.VMEM((S, LANES), F32)] * 3,
                          compiler_params=_cparams(("parallel",)))(P, P, P, o, l, do, dl)


def _att_weights(l_refs):
    l0, l1, l2 = [r[...] for r in l_refs]
    m = jnp.maximum(jnp.maximum(l0, l1), l2)
    e = (jnp.exp(l0 - m), jnp.exp(l1 - m), jnp.exp(l2 - m))
    inv = 1.0 / (e[0] + e[1] + e[2])
    return [x * inv for x in e]


def _att_combine_fwd(os, ls, tm=512):
    S = os[0].shape[0]

    def body(o0, o1, o2, l0, l1, l2, a_ref):
        w = _att_weights((l0, l1, l2))
        a_ref[...] = (w[0] * o0[...] + w[1] * o1[...] + w[2] * o2[...]).astype(BF16)

    row = _rows(tm, ATT_W)
    return pl.pallas_call(body, name="att_combine_fwd", grid=(S // tm,), in_specs=[row] * 6, out_specs=row,
                          out_shape=jax.ShapeDtypeStruct((S, ATT_W), BF16),
                          compiler_params=_cparams(("parallel",)))(*os, *ls)


def _att_combine_bwd(da, os, ls, tm=512):
    S = da.shape[0]

    def body(da_ref, o0, o1, o2, l0, l1, l2, *out_refs):
        da = da_ref[...]
        w = _att_weights((l0, l1, l2))
        dw = (da * o0[...], da * o1[...], da * o2[...])
        mean = w[0] * dw[0] + w[1] * dw[1] + w[2] * dw[2]
        for g in range(3):
            out_refs[g][...] = w[g] * da
            out_refs[3 + g][...] = w[g] * (dw[g] - mean)

    row = _rows(tm, ATT_W)
    shp = jax.ShapeDtypeStruct((S, ATT_W), F32)
    return pl.pallas_call(body, name="att_combine_bwd", grid=(S // tm,), in_specs=[row] * 7, out_specs=[row] * 6,
                          out_shape=[shp] * 6, compiler_params=_cparams(("parallel",)))(da, *os, *ls)


@jax.custom_vjp
def _bdot(a, b):
    return jnp.dot(a.astype(BF16), b.astype(BF16), preferred_element_type=F32)


def _bdot_fwd(a, b):
    return _bdot(a, b), (a, b)


def _bdot_bwd(res, ct):
    a, b = res
    ct16 = ct.astype(BF16)
    da = lax.dot_general(ct16, b.astype(BF16), (((1,), (1,)), ((), ())), preferred_element_type=F32)
    db = lax.dot_general(a.astype(BF16), ct16, (((0,), (0,)), ((), ())), preferred_element_type=F32)
    return da, db


_bdot.defvjp(_bdot_fwd, _bdot_bwd)


def _two_piece_dot(x, m):
    hi = x.astype(BF16)
    lo = (x - hi.astype(F32)).astype(BF16)
    return jnp.dot(hi, m, preferred_element_type=F32) + jnp.dot(lo, m, preferred_element_type=F32)


def _head_sum_impl(x):
    sel = (lax.broadcasted_iota(jnp.int32, (D, LANES), 0) // HEAD == lax.broadcasted_iota(jnp.int32, (D, LANES), 1)).astype(BF16)
    sel_t = (lax.broadcasted_iota(jnp.int32, (LANES, D), 1) // HEAD == lax.broadcasted_iota(jnp.int32, (LANES, D), 0)).astype(BF16)
    return _two_piece_dot(_two_piece_dot(x, sel), sel_t)


@jax.custom_vjp
def _head_sum(x):
    return _head_sum_impl(x)


_head_sum.defvjp(lambda x: (_head_sum_impl(x), None), lambda _, ct: (_head_sum_impl(ct),))


def _softplus(z):
    return jnp.maximum(z, 0.0) + jnp.log(1.0 + jnp.exp(-jnp.abs(z)))


def _rwkv_prep_fn(zr, zrp, zk, zkp, zv, zvp, zl, zlp, mu_r, mu_k, mu_v, mu_l, w0, a0, k_k, k_a, w2, a2, g2p):
    r = zr + (zrp - zr) * mu_r
    k = zk + (zkp - zk) * mu_k
    v = zv + (zvp - zv) * mu_v
    lo = zl + (zlp - zl) * mu_l
    w_low, a_low, g_low = lo[:, 0:LORA_W], lo[:, LORA_W:LORA_W + LORA_A], lo[:, LANES:LANES + G_PAD]
    w_log = -_softplus(-(w0 + _bdot(jnp.tanh(w_low), w2))) - 0.5
    decay = -jnp.exp(w_log)
    a = jax.nn.sigmoid(a0 + _bdot(a_low, a2))
    g = _bdot(jax.nn.sigmoid(g_low), g2p)
    kmod = k * (1.0 + (a - 1.0) * k_a)
    kk = k * k_k
    kk = kk / jnp.maximum(jnp.sqrt(_head_sum(kk * kk)), 1e-12)
    return r, decay, kmod, v, -kk, kk * a, g


def _rwkv_prep_specs(tm):
    vec = _full((1, D))
    slabs = []
    for col in (C_R // D, C_K // D, C_V // D):
        slabs += [_rows(tm, D, col), _prev8(tm, D, col)]
    slabs += [_rows(tm, LORA_PAD, C_LORA // LORA_PAD), _prev8(tm, LORA_PAD, C_LORA // LORA_PAD)]
    params = [vec, vec, vec, _full((1, LORA_PAD)), vec, vec, vec, vec,
              _full((LORA_W, D)), _full((LORA_A, D)), _full((G_PAD, D))]
    return slabs, params


def _prep_inputs(refs, first):
    vals = []
    for s in range(4):
        z = refs[2 * s][...]
        vals += [z, _shift_down(z, refs[2 * s + 1][...], 1, first)]
    return vals + [r[...] for r in refs[8:19]]


def _rwkv_prep(P, params, tm=256):
    S = P.shape[0]
    slabs, pspecs = _rwkv_prep_specs(tm)

    def body(*refs):
        outs = _rwkv_prep_fn(*_prep_inputs(refs, pl.program_id(0) == 0))
        for o_ref, val in zip(refs[19:], outs):
            o_ref[...] = val

    shp = jax.ShapeDtypeStruct((S, D), F32)
    return pl.pallas_call(body, name="rwkv_prep", grid=(S // tm,), in_specs=slabs + pspecs,
                          out_specs=[_rows(tm, D)] * 7, out_shape=[shp] * 7,
                          compiler_params=_cparams(("parallel",)))(*([P] * 8), *params)


def _rwkv_prep_bwd(P, params, cts_a, cts_b, tm=128):
    S = P.shape[0]
    slabs, pspecs = _rwkv_prep_specs(tm)
    has_b = [c is not None for c in cts_b]
    n_ct = 7 + sum(has_b)

    def body(*refs):
        first = pl.program_id(0) == 0
        ins = _prep_inputs(refs, first)
        ct_refs = refs[19:19 + n_ct]
        out_refs = refs[19 + n_ct:]
        cts, pos = [], 7
        for i in range(7):
            c = ct_refs[i][...]
            if has_b[i]:
                c = c + ct_refs[pos][...]
                pos += 1
            cts.append(c)
        _, vjp = jax.vjp(_rwkv_prep_fn, *ins)
        grads = vjp(tuple(cts))
        for s in range(4):
            out_refs[s][...] = grads[2 * s]
            out_refs[4 + s][...] = grads[2 * s + 1]
        for i in range(11):
            _acc(out_refs[8 + i], grads[8 + i], first)

    ct_in = list(cts_a) + [c for c in cts_b if c is not None]
    row, lrow = _rows(tm, D), _rows(tm, LORA_PAD)
    f = jax.ShapeDtypeStruct
    zshapes = [f((S, D), F32)] * 3 + [f((S, LORA_PAD), F32)]
    pshapes = [f((1, D), F32)] * 3 + [f((1, LORA_PAD), F32)] + [f((1, D), F32)] * 4 + [f((LORA_W, D), F32), f((LORA_A, D), F32), f((G_PAD, D), F32)]
    return pl.pallas_call(
        body, name="rwkv_prep_bwd", grid=(S // tm,),
        in_specs=slabs + pspecs + [row] * n_ct,
        out_specs=[row, row, row, lrow] * 2 + pspecs,
        out_shape=zshapes * 2 + pshapes,
        compiler_params=_cparams(("arbitrary",)))(*([P] * 8), *params, *ct_in)


def _shift_add(a, b, tm=256):
    S, W = a.shape

    def body(a_ref, b_ref, h_ref, o_ref):
        last = pl.program_id(0) == pl.num_programs(0) - 1
        o_ref[...] = (a_ref[...] + _shift_up(b_ref[...], h_ref[...], 1, last)).astype(BF16)

    return pl.pallas_call(body, name="shift_add", grid=(S // tm,),
                          in_specs=[_rows(tm, W), _rows(tm, W), _next8(tm, W, S)],
                          out_specs=_rows(tm, W), out_shape=jax.ShapeDtypeStruct((S, W), BF16),
                          compiler_params=_cparams(("parallel",)))(a, b, b)


def _rwkv_post_fn(y, r, kmod, v, g, lnx_w, lnx_b, r_k):
    mean = _head_sum(y) * (1.0 / HEAD)
    yc = y - mean
    var = _head_sum(yc * yc) * (1.0 / HEAD)
    yn = yc * lax.rsqrt(var + GN_EPS) * lnx_w + lnx_b
    bonus = _head_sum(r * kmod * r_k) * v
    return (yn + bonus) * g


def _rwkv_post(y, r, kmod, v, g, lnx_w, lnx_b, r_k, tm=256):
    S = y.shape[0]

    def body(y_ref, r_ref, k_ref, v_ref, g_ref, w_ref, b_ref, rk_ref, o_ref):
        o_ref[...] = _rwkv_post_fn(y_ref[...], r_ref[...], k_ref[...], v_ref[...], g_ref[...],
                                   w_ref[...], b_ref[...], rk_ref[...]).astype(BF16)

    row, vec = _rows(tm, D), _full((1, D))
    return pl.pallas_call(body, name="rwkv_post", grid=(S // tm,), in_specs=[row] * 5 + [vec] * 3, out_specs=row,
                          out_shape=jax.ShapeDtypeStruct((S, D), BF16),
                          compiler_params=_cparams(("parallel",)))(y, r, kmod, v, g, lnx_w, lnx_b, r_k)


def _rwkv_post_bwd(drw, y, r, kmod, v, g, lnx_w, lnx_b, r_k, tm=256):
    S = y.shape[0]

    def body(d_ref, y_ref, r_ref, k_ref, v_ref, g_ref, w_ref, b_ref, rk_ref, *out_refs):
        first = pl.program_id(0) == 0
        _, vjp = jax.vjp(_rwkv_post_fn, y_ref[...], r_ref[...], k_ref[...], v_ref[...], g_ref[...],
                         w_ref[...], b_ref[...], rk_ref[...])
        grads = vjp(d_ref[...])
        for i in range(5):
            out_refs[i][...] = grads[i]
        for i in range(5, 8):
            _acc(out_refs[i], grads[i], first)

    row, vec = _rows(tm, D), _full((1, D))
    f = jax.ShapeDtypeStruct
    return pl.pallas_call(body, name="rwkv_post_bwd", grid=(S // tm,), in_specs=[row] * 6 + [vec] * 3,
                          out_specs=[row] * 5 + [vec] * 3, out_shape=[f((S, D), F32)] * 5 + [f((1, D), F32)] * 3,
                          compiler_params=_cparams(("arbitrary",)))(drw, y, r, kmod, v, g, lnx_w, lnx_b, r_k)


CHUNK = 32
CHUNK_TB = 256
_DOT_DIMS = {"nn": (((2,), (1,)), ((0,), (0,))), "nt": (((2,), (2,)), ((0,), (0,))), "tn": (((1,), (1,)), ((0,), (0,)))}


def _dot16(x, y, mode):
    return lax.dot_general(x.astype(BF16), y.astype(BF16), _DOT_DIMS[mode], preferred_element_type=F32)


@functools.partial(jax.custom_vjp, nondiff_argnums=(2,))
def _mm16(x, y, mode):
    return _dot16(x, y, mode)


def _mm16_fwd(x, y, mode):
    return _dot16(x, y, mode), (x, y)


def _mm16_bwd(mode, res, ct):
    x, y = res
    if mode == "nn":
        return _dot16(ct, y, "nt"), _dot16(x, ct, "tn")
    if mode == "nt":
        return _dot16(ct, y, "nn"), _dot16(ct, x, "tn")
    return _dot16(y, ct, "nt"), _dot16(x, ct, "nn")


_mm16.defvjp(_mm16_fwd, _mm16_bwd)


def _tri_sum(x, upper):
    T = x.shape[0]
    i = lax.broadcasted_iota(jnp.int32, (T, T), 0)
    j = lax.broadcasted_iota(jnp.int32, (T, T), 1)
    tri = ((j >= i) if upper else (i >= j)).astype(BF16)
    out, rest = None, x
    for _ in range(3):
        piece = rest.astype(BF16)
        rest = rest - piece.astype(F32)
        part = jnp.dot(tri, piece, preferred_element_type=F32)
        out = part if out is None else out + part
    return out


@jax.custom_vjp
def _cumsum_rows(x):
    return _tri_sum(x, False)


_cumsum_rows.defvjp(lambda x: (_tri_sum(x, False), None), lambda _, ct: (_tri_sum(ct, True),))


def _rows_to_cols(x):
    H, _, K = x.shape
    eye = (lax.broadcasted_iota(jnp.int32, (H, K, K), 1) == lax.broadcasted_iota(jnp.int32, (H, K, K), 2)).astype(F32)
    out = lax.dot_general(eye, jnp.broadcast_to(x, (H, SUBLANES, K)), _DOT_DIMS["nt"],
                          precision=lax.Precision.HIGHEST, preferred_element_type=F32)
    return out[:, :, 0:1]


def _per_head(x):
    return jnp.concatenate([x[:, h * HEAD:(h + 1) * HEAD][None] for h in range(N_HEADS)], axis=0)


def _chunk_fn(st0, r, lw, k, v, a, b):
    T = r.shape[0]
    cl = _cumsum_rows(lw)
    cl_end = cl[T - 1:T, :]
    inv = jnp.exp(-cl)
    to_end = jnp.exp(cl_end - cl)
    ah, rh, bh, kh, be, ke, v3 = [_per_head(x) for x in
                                  (a * jnp.exp(cl - lw), r * jnp.exp(cl), b * inv, k * inv, b * to_end, k * to_end, v)]
    i = lax.broadcasted_iota(jnp.int32, (N_HEADS, T, T), 1)
    j = lax.broadcasted_iota(jnp.int32, (N_HEADS, T, T), 2)
    a_ab = jnp.where(i > j, _mm16(ah, bh, "nt"), 0.0)
    a_ak = jnp.where(i > j, _mm16(ah, kh, "nt"), 0.0)
    m_rb = jnp.where(i >= j, _mm16(rh, bh, "nt"), 0.0)
    m_rk = jnp.where(i >= j, _mm16(rh, kh, "nt"), 0.0)
    rhs = _mm16(ah, st0, "nn") + _mm16(a_ak, v3, "nn")
    power, solve, n = a_ab, (i == j).astype(F32) + a_ab, 1
    while 2 * n < T:
        power = _mm16(power, power, "nn")
        solve = solve + _mm16(solve, power, "nn")
        n *= 2
    sa = _mm16(solve, rhs, "nn")
    y3 = _mm16(rh, st0, "nn") + _mm16(m_rb, sa, "nn") + _mm16(m_rk, v3, "nn")
    st_end = _rows_to_cols(_per_head(jnp.exp(cl_end))) * st0 + _mm16(be, sa, "tn") + _mm16(ke, v3, "tn")
    return jnp.concatenate([y3[h] for h in range(N_HEADS)], axis=1), st_end


def _hosted_exchange(refs, n, broadcast, n_steps):
    if n == 0:
        return lambda: None
    start, wait = _exchange_ops(refs[:n], refs[n:2 * n], *refs[2 * n:], broadcast)
    pl.when(pl.program_id(0) == 0)(start)
    return lambda: pl.when(pl.program_id(0) == n_steps - 1)(wait)


def _cscan_fwd(r, lw, k, v, a, b, gather=()):
    S = r.shape[0]
    per_blk = CHUNK_TB // CHUNK
    n_x = len(gather)
    nblk = S // CHUNK_TB

    def body(*refs):
        r_ref, lw_ref, k_ref, v_ref, a_ref, b_ref = refs[:6]
        y_ref, ck_ref = refs[6 + n_x:8 + n_x]
        st_ref = refs[8 + 2 * n_x]
        finish = _hosted_exchange(refs[6:6 + n_x] + refs[8 + n_x:8 + 2 * n_x] + refs[9 + 2 * n_x:], n_x, True, nblk)

        @pl.when(pl.program_id(0) == 0)
        def _():
            st_ref[...] = jnp.zeros_like(st_ref)

        def chunk(c, carry):
            rows = pl.ds(pl.multiple_of(c * CHUNK, CHUNK), CHUNK)
            st0 = st_ref[...]
            ck_ref[c] = st0
            y, st_end = _chunk_fn(st0, r_ref[rows, :], lw_ref[rows, :], k_ref[rows, :],
                                  v_ref[rows, :], a_ref[rows, :], b_ref[rows, :])
            y_ref[rows, :] = y
            st_ref[...] = st_end
            return carry

        lax.fori_loop(0, per_blk, chunk, 0)
        finish()

    blk = _rows(CHUNK_TB, D)
    any_spec = pl.BlockSpec(memory_space=pl.ANY)
    outs = pl.pallas_call(
        body, name="scan_fwd", grid=(nblk,), in_specs=[blk] * 6 + [any_spec] * n_x,
        out_specs=[blk, pl.BlockSpec((per_blk, N_HEADS, HEAD, HEAD), lambda i: (i, 0, 0, 0))] + [any_spec] * n_x,
        out_shape=[jax.ShapeDtypeStruct((S, D), F32), jax.ShapeDtypeStruct((S // CHUNK, N_HEADS, HEAD, HEAD), F32)]
        + _exchange_shapes(gather, True),
        scratch_shapes=[pltpu.VMEM((N_HEADS, HEAD, HEAD), F32)] + (_exchange_scratch(n_x) if n_x else []),
        compiler_params=_cparams(("arbitrary",)))(r, lw, k, v, a, b, *gather)
    return outs[0], outs[1], outs[2:]


def _cscan_bwd(r, lw, k, v, a, b, ckpt, dy, scatter=()):
    S = r.shape[0]
    per_blk = CHUNK_TB // CHUNK
    nblk = S // CHUNK_TB
    n_x = len(scatter)

    def body(*refs):
        r_ref, lw_ref, k_ref, v_ref, a_ref, b_ref, ck_ref, dy_ref = refs[:8]
        out_refs = refs[8 + n_x:14 + n_x]
        ds_ref = refs[14 + 2 * n_x]
        finish = _hosted_exchange(refs[8:8 + n_x] + refs[14 + n_x:14 + 2 * n_x] + refs[15 + 2 * n_x:], n_x, False, nblk)

        @pl.when(pl.program_id(0) == 0)
        def _():
            ds_ref[...] = jnp.zeros_like(ds_ref)

        def chunk(cc, carry):
            c = per_blk - 1 - cc
            rows = pl.ds(pl.multiple_of(c * CHUNK, CHUNK), CHUNK)
            ins = (ck_ref[c], r_ref[rows, :], lw_ref[rows, :], k_ref[rows, :], v_ref[rows, :], a_ref[rows, :], b_ref[rows, :])
            _, vjp = jax.vjp(_chunk_fn, *ins)
            grads = vjp((dy_ref[rows, :], ds_ref[...]))
            ds_ref[...] = grads[0]
            for o_ref, g in zip(out_refs, grads[1:]):
                o_ref[rows, :] = g
            return carry

        lax.fori_loop(0, per_blk, chunk, 0)
        finish()

    blk = pl.BlockSpec((CHUNK_TB, D), lambda i: (nblk - 1 - i, 0))
    any_spec = pl.BlockSpec(memory_space=pl.ANY)
    shp = jax.ShapeDtypeStruct((S, D), F32)
    outs = pl.pallas_call(
        body, name="scan_bwd", grid=(nblk,),
        in_specs=[blk] * 6 + [pl.BlockSpec((per_blk, N_HEADS, HEAD, HEAD), lambda i: (nblk - 1 - i, 0, 0, 0)), blk]
        + [any_spec] * n_x,
        out_specs=[blk] * 6 + [any_spec] * n_x, out_shape=[shp] * 6 + _exchange_shapes(scatter, False),
        scratch_shapes=[pltpu.VMEM((N_HEADS, HEAD, HEAD), F32)] + (_exchange_scratch(n_x) if n_x else []),
        compiler_params=_cparams(("arbitrary",)))(r, lw, k, v, a, b, ckpt, dy, *scatter)
    return outs[:6], outs[6:]


def _ada_fwd(c8, w_ada, b_ada):
    def body(c_ref, w_ref, b_ref, o_ref):
        o_ref[...] = jnp.dot(c_ref[...].astype(BF16), w_ref[...], preferred_element_type=F32) + b_ref[...]

    tn = 1536
    return pl.pallas_call(body, name="ada_fwd", grid=(6 * D // tn,),
                          in_specs=[_full((SUBLANES, D)), pl.BlockSpec((D, tn), lambda j: (0, j)), pl.BlockSpec((1, tn), lambda j: (0, j))],
                          out_specs=pl.BlockSpec((SUBLANES, tn), lambda j: (0, j)),
                          out_shape=jax.ShapeDtypeStruct((SUBLANES, 6 * D), F32),
                          compiler_params=_cparams(("parallel",)))(c8, w_ada, b_ada)


def _outer(col, row):
    N = row.shape[1]
    tn = 1536

    def body(c_ref, r_ref, o_ref):
        o_ref[...] = c_ref[...] * r_ref[...]

    return pl.pallas_call(body, name="ada_wgrad", grid=(N // tn,),
                          in_specs=[_full((D, 1)), pl.BlockSpec((1, tn), lambda j: (0, j))],
                          out_specs=pl.BlockSpec((D, tn), lambda j: (0, j)),
                          out_shape=jax.ShapeDtypeStruct((D, N), F32),
                          compiler_params=_cparams(("parallel",)))(col, row)


def _exchange(srcs, broadcast, name):
    n = len(srcs)

    def body(*refs):
        start, wait = _exchange_ops(refs[:n], refs[n:2 * n], *refs[2 * n:], broadcast)
        start()
        wait()

    any_spec = pl.BlockSpec(memory_space=pl.ANY)
    return pl.pallas_call(
        body, name=name, out_shape=_exchange_shapes(srcs, broadcast), in_specs=[any_spec] * n, out_specs=[any_spec] * n,
        scratch_shapes=_exchange_scratch(n),
        compiler_params=pltpu.CompilerParams(has_side_effects=True),
    )(*srcs)


def _exchange_shapes(srcs, broadcast):
    return [jax.ShapeDtypeStruct((N_DEV,) + (s.shape if broadcast else s.shape[1:]), s.dtype) for s in srcs]


def _exchange_scratch(n):
    return [pltpu.SemaphoreType.DMA((n, N_DEV)), pltpu.SemaphoreType.DMA((n, N_DEV)), pltpu.SemaphoreType.DMA((n,))]


def _exchange_ops(src_refs, out_refs, send_sems, recv_sems, local_sems, broadcast):
    n = len(src_refs)
    x, y, c = lax.axis_index("x"), lax.axis_index("y"), lax.axis_index("c")
    me = 4 * x + 2 * y + c

    def block(i, j):
        return src_refs[i] if broadcast else src_refs[i].at[j]

    def remote(i, d, src_slot, dst_slot):
        px, py, pc = x ^ (d >> 2), y ^ ((d >> 1) & 1), c ^ (d & 1)
        return pltpu.make_async_remote_copy(
            src_ref=block(i, src_slot), dst_ref=out_refs[i].at[dst_slot], send_sem=send_sems.at[i, d],
            recv_sem=recv_sems.at[i, d], device_id=(px, py, pc), device_id_type=_MESH)

    def local(i):
        return pltpu.make_async_copy(block(i, me), out_refs[i].at[me], local_sems.at[i])

    def start():
        for i in range(n):
            local(i).start()
        for d in range(1, N_DEV):
            for i in range(n):
                remote(i, d, me ^ d, me).start()

    def wait():
        for d in range(1, N_DEV):
            for i in range(n):
                remote(i, d, me, me ^ d).wait_recv()
        for d in range(1, N_DEV):
            for i in range(n):
                remote(i, d, me ^ d, me).wait_send()
        for i in range(n):
            local(i).wait()

    return start, wait


def _sum_adam(parts, w, m, v, name):
    _, R, C = parts.shape
    tm = 256 if R % 256 == 0 else R
    c1 = 1.0 / (1.0 - ADAM_B1 ** ADAM_STEP)
    c2 = 1.0 / (1.0 - ADAM_B2 ** ADAM_STEP)

    def body(p_ref, w_ref, m_ref, v_ref, g_ref, d_ref, nm_ref, nv_ref):
        g = p_ref[0].astype(F32)
        for j in range(1, N_DEV):
            g = g + p_ref[j].astype(F32)
        nm = ADAM_B1 * m_ref[...] + (1.0 - ADAM_B1) * g
        nv = ADAM_B2 * v_ref[...] + (1.0 - ADAM_B2) * (g * g)
        g_ref[...] = g
        nm_ref[...] = nm
        nv_ref[...] = nv
        d_ref[...] = -ADAM_LR * ((nm * c1) / (jnp.sqrt(nv * c2) + ADAM_EPS) + ADAM_WD * w_ref[...])

    row = _rows(tm, C)
    shp = jax.ShapeDtypeStruct((R, C), F32)
    return pl.pallas_call(body, name=name, grid=(R // tm,),
                          in_specs=[pl.BlockSpec((N_DEV, tm, C), lambda i: (0, i, 0)), row, row, row],
                          out_specs=[row] * 4, out_shape=[shp] * 4,
                          compiler_params=_cparams(("parallel",)))(parts, w, m, v)


PACK_ALIGN = 16 * LANES
PACK_ROWS = 512 * LANES

SHARDED = (("w_ada", 1), ("w_in", 1), ("w2", 1), ("a2", 1), ("g2", 1), ("w_att_out", 1), ("w_rwkv_out", 0),
           ("w_o", 0), ("w_up", 1), ("conv_w", 1), ("w_down", 0))
EARLY, LATE = SHARDED[:5], SHARDED[5:]
REPLICATED = ("b_ada", "norm1_w", "b_gate", "mu_shift", "w0", "a0", "k_k", "k_a", "r_k", "lnx_w", "lnx_b",
              "norm2_w", "conv_b", "norm_f_w")
WEIGHTS = ("w_ada", "b_ada", "norm1_w", "w_in", "b_gate", "mu_shift", "w0", "w2", "a0", "a2", "g2", "k_k", "k_a", "r_k",
           "lnx_w", "lnx_b", "w_att_out", "w_rwkv_out", "w_o", "norm2_w", "w_up", "conv_w", "conv_b", "w_down", "norm_f_w")


def _pack(arrays):
    flat, layout, off = [], [], 0
    for i, a in enumerate(arrays):
        n = a.size
        pad = (-n) % PACK_ALIGN if i + 1 < len(arrays) else (-(off + n)) % PACK_ROWS
        flat.append(a.reshape(-1))
        if pad:
            flat.append(jnp.zeros((pad,), a.dtype))
        layout.append((off, n, a.shape))
        off += n + pad
    return jnp.concatenate(flat).reshape(-1, LANES), layout


def _unpack(buf, layout):
    flat = buf.reshape(-1)
    return [flat[off:off + n].reshape(shape) for off, n, shape in layout]


def _pad_w_in(w_in):
    rkv = w_in[:, ATT_IN:ATT_IN + 3 * D]
    lora = w_in[:, ATT_IN + 3 * D:ATT_IN + RWKV_IN]
    gates = w_in[:, ATT_IN + RWKV_IN:]
    att = w_in[:, :ATT_IN]
    lw, la, lg = lora[:, :LORA_W], lora[:, LORA_W:LORA_W + LORA_A], lora[:, LORA_W + LORA_A:]
    zeros = jnp.zeros((w_in.shape[0], LORA_PAD - LANES - LORA_G), w_in.dtype)
    return jnp.concatenate([rkv, gates, att, lw, la, lg, zeros], axis=1)


def _unpad_w_in(g):
    att = g[:, C_ATT:C_ATT + ATT_IN]
    rkv = g[:, C_R:C_R + 3 * D]
    lora = jnp.concatenate([g[:, C_LORA:C_LORA + LORA_W + LORA_A], g[:, C_LORA + LANES:C_LORA + LANES + LORA_G]], axis=1)
    gates = g[:, C_GA:C_GA + 2 * D]
    return jnp.concatenate([att, rkv, lora, gates], axis=1)


def _pad_mu(mu):
    lo = mu[:, 3 * D:]
    mu_l = jnp.concatenate([lo[:, :LORA_W + LORA_A], lo[:, LORA_W + LORA_A:], jnp.zeros((1, LORA_PAD - LANES - LORA_G), mu.dtype)], axis=1)
    return mu[:, :D], mu[:, D:2 * D], mu[:, 2 * D:3 * D], mu_l


def _local_step(x, c, W, late_shards, target):
    S = x.shape[0]
    W = dict(W)
    G = {}
    c8 = jnp.pad(c, ((0, SUBLANES - 1), (0, 0)))
    ada = _ada_fwd(c8, W["w_ada"], W["b_ada"])[0:1]
    sh1, sc1, gt1, sh2, sc2, gt2 = [ada[:, i * D:(i + 1) * D] for i in range(6)]
    h1, rstd1 = _norm_fwd(x, None, None, W["norm1_w"], sc1, sh1, "norm1_fwd")
    w_in_p = _pad_w_in(W["w_in"])
    P = _mm(h1, w_in_p, "nn", F32, "proj_in")

    mu_r, mu_k, mu_v, mu_l = _pad_mu(W["mu_shift"])
    g2p = jnp.pad(W["g2"], ((0, G_PAD - LORA_G), (0, 0)))
    prep_params = [mu_r, mu_k, mu_v, mu_l, W["w0"], W["a0"], W["k_k"], W["k_a"], W["w2"], W["a2"], g2p]
    r_, dec, kmod, v_, aa, bb, gg = _rwkv_prep(P, prep_params)
    y_scan, states, late = _cscan_fwd(r_, dec, kmod, v_, aa, bb, gather=late_shards)
    W.update({n: _full_weight(g, axis) for (n, axis), g in zip(LATE, late)})

    o_g, l_g = zip(*[_att_fwd(P, g) for g in range(len(ATT_PATTERNS))])
    att = _att_combine_fwd(o_g, l_g)
    y_att = _mm(att, W["w_att_out"], "nn", F32, "att_out")
    r_k = W["r_k"].reshape(1, D)
    rw = _rwkv_post(y_scan, r_, kmod, v_, gg, W["lnx_w"], W["lnx_b"], r_k)
    y_rwkv = _mm(rw, W["w_rwkv_out"], "nn", F32, "rwkv_out")

    bga, bgr = W["b_gate"][:, :D], W["b_gate"][:, D:]
    mix = _gate_fwd(P, bga, bgr, y_att, y_rwkv)
    mo = _mm(mix, W["w_o"], "nn", F32, "mix_out")
    x2, h2, rstd2 = _norm_fwd(x, mo, gt1, W["norm2_w"], sc2, sh2, "norm2_fwd")
    u = _mm(h2, W["w_up"], "nn", F32, "ffn_up")
    conv_w8 = jnp.pad(W["conv_w"], ((0, SUBLANES - 3), (0, 0)))
    act = _conv_fwd(u, conv_w8, W["conv_b"])
    f = _mm(act, W["w_down"], "nn", F32, "ffn_down")
    loss_blk, dx3, df, dgt2, G["norm_f_w"] = _final(x2, f, gt2, W["norm_f_w"], target)
    loss = loss_blk[0, 0]

    dact = _mm(df, W["w_down"], "nt", BF16, "ffn_down_dx")
    G["w_down"] = _mm(act, df, "tn", F32, "ffn_down_dw")
    duc, dwg, dwv, dbg, dbv = _conv_bwd_a(dact, u, conv_w8, W["conv_b"])
    G["conv_w"] = jnp.concatenate([dwg[0:3], dwv[0:3]], axis=1)
    G["conv_b"] = jnp.concatenate([dbg, dbv], axis=1)
    du = _conv_bwd_b(duc, conv_w8)
    dh2 = _mm(du, W["w_up"], "nt", F32, "ffn_up_dx")
    G["w_up"] = _mm(h2, du, "tn", F32, "ffn_up_dw")
    dx2, dsh2, dsc2, G["norm2_w"], dmo, dgt1 = _norm_bwd(dh2, x2, rstd2, W["norm2_w"], sc2, dx3, mo, gt1, "norm2_bwd")
    dmix = _mm(dmo, W["w_o"], "nt", F32, "mix_out_dx")
    G["w_o"] = _mm(mix, dmo, "tn", F32, "mix_out_dw")
    dy_att, dy_rwkv, dpga, dpgr, dbga, dbgr = _gate_bwd(dmix, P, bga, bgr, y_att, y_rwkv)
    G["b_gate"] = jnp.concatenate([dbga, dbgr], axis=1)

    datt = _mm(dy_att, W["w_att_out"], "nt", F32, "att_out_dx")
    G["w_att_out"] = _mm(att, dy_att, "tn", F32, "att_out_dw")
    dcomb = _att_combine_bwd(datt, o_g, l_g)
    dp_att = []
    for g in range(len(ATT_PATTERNS)):
        dp_att += _att_bwd(P, o_g[g], l_g[g], dcomb[g], dcomb[3 + g], g)

    drw = _mm(dy_rwkv, W["w_rwkv_out"], "nt", F32, "rwkv_out_dx")
    G["w_rwkv_out"] = _mm(rw, dy_rwkv, "tn", F32, "rwkv_out_dw")
    dy_scan, dr1, dk1, dv1, dgg, G["lnx_w"], G["lnx_b"], drk = _rwkv_post_bwd(drw, y_scan, r_, kmod, v_, gg, W["lnx_w"], W["lnx_b"], r_k)
    G["r_k"] = drk.reshape(W["r_k"].shape)
    late_blocks = [_owner_blocks(G[n], axis) for n, axis in LATE] if late_shards else []
    (dr2, ddec, dk2, dv2, daa, dbb), late_parts = _cscan_bwd(r_, dec, kmod, v_, aa, bb, states, dy_scan, scatter=late_blocks)
    pb = _rwkv_prep_bwd(P, prep_params, [dr2, ddec, dk2, dv2, daa, dbb, dgg], [dr1, None, dk1, dv1, None, None, None])
    dz, dzp, dpar = pb[0:4], pb[4:8], pb[8:]
    dp_rkv = [_shift_add(dz[i], dzp[i]) for i in range(3)]
    dp_lora = _shift_add(dz[3], dzp[3])
    dmu_r, dmu_k, dmu_v, dmu_l, G["w0"], G["a0"], G["k_k"], G["k_a"], G["w2"], G["a2"], dg2p = dpar
    G["g2"] = dg2p[0:LORA_G]
    G["mu_shift"] = jnp.concatenate([dmu_r, dmu_k, dmu_v, dmu_l[:, :LORA_W + LORA_A], dmu_l[:, LANES:LANES + LORA_G]], axis=1)

    dP = jnp.concatenate(dp_rkv + [dpga, dpgr] + dp_att + [dp_lora], axis=1)
    dh1 = _mm(dP, w_in_p, "nt", F32, "proj_in_dx")
    G["w_in"] = _unpad_w_in(_mm(h1, dP, "tn", F32, "proj_in_dw"))
    grad_x, dsh1, dsc1, G["norm1_w"] = _norm_bwd(dh1, x, rstd1, W["norm1_w"], sc1, dx2, None, None, "norm1_bwd")
    dada = jnp.concatenate([dsh1, dsc1, dgt1, dsh2, dsc2, dgt2], axis=1)
    G["b_ada"] = dada
    G["w_ada"] = _outer(c.reshape(D, 1), dada)
    return loss, grad_x, G, late_parts


def _full_weight(gathered, axis):
    _, rows, cols = gathered.shape
    if axis == 0:
        return gathered.reshape(N_DEV * rows, cols)
    return gathered.transpose(1, 0, 2).reshape(rows, N_DEV * cols)


def _owner_blocks(g, axis):
    rows, cols = g.shape
    g = g.astype(BF16)
    if axis == 0:
        return g.reshape(N_DEV, rows // N_DEV, cols)
    return g.reshape(rows, N_DEV, cols // N_DEV).transpose(1, 0, 2)


def kernel(x, c, w_ada, b_ada, norm1_w, w_in, b_gate, mu_shift, w0, w2, a0, a2, g2, k_k, k_a, r_k, lnx_w, lnx_b, w_att_out, w_rwkv_out, w_o, norm2_w, w_up, conv_w, conv_b, w_down, norm_f_w, loss_target, m_w_ada, m_b_ada, m_norm1_w, m_w_in, m_b_gate, m_mu_shift, m_w0, m_w2, m_a0, m_a2, m_g2, m_k_k, m_k_a, m_r_k, m_lnx_w, m_lnx_b, m_w_att_out, m_w_rwkv_out, m_w_o, m_norm2_w, m_w_up, m_conv_w, m_conv_b, m_w_down, m_norm_f_w, v_w_ada, v_b_ada, v_norm1_w, v_w_in, v_b_gate, v_mu_shift, v_w0, v_w2, v_a0, v_a2, v_g2, v_k_k, v_k_a, v_r_k, v_lnx_w, v_lnx_b, v_w_att_out, v_w_rwkv_out, v_w_o, v_norm2_w, v_w_up, v_conv_w, v_conv_b, v_w_down, v_norm_f_w):
    env = dict(locals())
    w_shard = {n: env[n] for n in WEIGHTS}
    m_shard = {n: env["m_" + n] for n in WEIGHTS}
    v_shard = {n: env["v_" + n] for n in WEIGHTS}

    gathered = _exchange([w_shard[n][0].astype(BF16) for n, _ in EARLY], True, "gather_weights")
    W = {n: _full_weight(g, axis) for (n, axis), g in zip(EARLY, gathered)}
    for n in REPLICATED:
        W[n] = w_shard[n].reshape(1, -1) if n != "r_k" else w_shard[n][0]

    late_shards = [w_shard[n][0].astype(BF16) for n, _ in LATE]
    loss, grad_x, G, late_parts = _local_step(x[0], c, W, late_shards, loss_target[0])
    loss = lax.psum(loss, ("x", "y", "c"))

    parts = list(_exchange([_owner_blocks(G[n], axis) for n, axis in EARLY], False, "scatter_grads")) + list(late_parts)
    out = {}
    for (n, _), p in zip(EARLY + LATE, parts):
        res = _sum_adam(p, w_shard[n][0], m_shard[n][0], v_shard[n][0], "adam_" + n)
        for kind, a in zip(("grad", "delta", "new_m", "new_v"), res):
            out[kind, n] = a[None]

    small, slayout = _pack([G[n].reshape(-1) for n in REPLICATED])
    sparts, = _exchange([small], True, "gather_small_grads")
    sw, _ = _pack([w_shard[n].reshape(-1) for n in REPLICATED])
    sm, _ = _pack([m_shard[n].reshape(-1) for n in REPLICATED])
    sv, _ = _pack([v_shard[n].reshape(-1) for n in REPLICATED])
    res = _sum_adam(sparts, sw, sm, sv, "adam_replicated")
    for kind, buf in zip(("grad", "delta", "new_m", "new_v"), res):
        for n, a in zip(REPLICATED, _unpack(buf, slayout)):
            out[kind, n] = a.reshape(w_shard[n].shape)

    return (loss, grad_x[None], *[out[kind, n] for kind in ("grad", "delta", "new_m", "new_v") for n in WEIGHTS])
```

```python
import functools
import math

import jax
import jax.numpy as jnp
from jax import lax
from jax.experimental import pallas as pl
from jax.experimental.pallas import tpu as pltpu

F32 = jnp.float32
BF16 = jnp.bfloat16

D = 1024
HEAD = 64
ATT_PATTERNS = ((128, 1), (512, 4), (2048, 16))
ATT_HEADS = 8
ATT_W = ATT_HEADS * HEAD
ATT_IN = 3 * 3 * ATT_W
QBLK = 128
N_HEADS = D // HEAD
LORA_W, LORA_A, LORA_G = 64, 64, 160
RWKV_IN = 3 * D + LORA_W + LORA_A + LORA_G
N_IN = ATT_IN + RWKV_IN + 2 * D
D_FF = 2816
RMS_EPS = 1e-6
GN_EPS = 64e-5
N_DEV = 8
LANES = 128
SUBLANES = 8

C_R, C_K, C_V, C_GA, C_GR = 0, 1024, 2048, 3072, 4096
C_ATT = 5120
C_LORA = C_ATT + ATT_IN
LORA_PAD = 512
G_PAD = 256
N_PAD = C_LORA + LORA_PAD

ADAM_LR, ADAM_B1, ADAM_B2, ADAM_EPS, ADAM_WD, ADAM_STEP = 0.001, 0.9, 0.999, 1e-08, 0.01, 10

VMEM_LIMIT = 56 * 1024 * 1024

_MESH = pl.DeviceIdType.MESH


def _cparams(sem):
    return pltpu.CompilerParams(dimension_semantics=sem, vmem_limit_bytes=VMEM_LIMIT)


def _tile(dim, pref):
    if dim <= pref:
        return dim
    best = None
    for t in range(LANES, pref + 1, LANES):
        if dim % t == 0:
            best = t
    assert best is not None, dim
    return best


MM_TILES = {"nn": (1024, 1408, 1408), "nt": (512, 2048, 1408), "tn": (1408, 1408, 1024)}


def _mm(a, b, mode, out_dtype, name, scatter=()):
    if mode == "nn":
        (M, K), (K2, N) = a.shape, b.shape
    elif mode == "nt":
        (M, K), (N, K2) = a.shape, b.shape
    else:
        (K, M), (K2, N) = a.shape, b.shape
    assert K == K2, (a.shape, b.shape, mode)
    tm, tn, tk = (_tile(dim, pref) for dim, pref in zip((M, N, K), MM_TILES[mode]))
    nk = K // tk
    grid = (M // tm, N // tn, nk)
    n_x = len(scatter)
    dims = {"nn": (((1,), (0,)), ((), ())), "nt": (((1,), (1,)), ((), ())), "tn": (((0,), (0,)), ((), ()))}[mode]

    def body(*refs):
        a_ref, b_ref = refs[:2]
        o_ref, acc_ref = refs[2 + n_x], refs[3 + 2 * n_x]
        finish = _hosted_exchange(refs[2:2 + n_x] + refs[3 + n_x:3 + 2 * n_x] + refs[4 + 2 * n_x:], n_x, False, grid)
        k = pl.program_id(2)
        part = lax.dot_general(a_ref[...].astype(BF16), b_ref[...].astype(BF16), dims,
                               preferred_element_type=F32)
        if nk == 1:
            o_ref[...] = part.astype(o_ref.dtype)
        else:
            @pl.when(k == 0)
            def _():
                acc_ref[...] = part

            @pl.when(jnp.logical_and(k > 0, k < nk - 1))
            def _():
                acc_ref[...] += part

            @pl.when(k == nk - 1)
            def _():
                o_ref[...] = (acc_ref[...] + part).astype(o_ref.dtype)
        finish()

    a_spec = pl.BlockSpec((tk, tm), lambda i, j, k: (k, i)) if mode == "tn" else pl.BlockSpec((tm, tk), lambda i, j, k: (i, k))
    b_spec = pl.BlockSpec((tn, tk), lambda i, j, k: (j, k)) if mode == "nt" else pl.BlockSpec((tk, tn), lambda i, j, k: (k, j))
    any_spec = pl.BlockSpec(memory_space=pl.ANY)
    outs = pl.pallas_call(
        body, name=name, grid=grid,
        in_specs=[a_spec, b_spec] + [any_spec] * n_x,
        out_specs=[pl.BlockSpec((tm, tn), lambda i, j, k: (i, j))] + [any_spec] * n_x,
        out_shape=[jax.ShapeDtypeStruct((M, N), out_dtype)] + _exchange_shapes(scatter, False),
        scratch_shapes=[pltpu.VMEM((tm, tn) if nk > 1 else (SUBLANES, LANES), F32)] + (_exchange_scratch(n_x) if n_x else []),
        compiler_params=_cparams(("arbitrary",) * 3 if n_x else ("parallel", "parallel", "arbitrary")),
    )(a, b, *scatter)
    return (outs[0], outs[1:]) if n_x else outs[0]


def _rows(tm, w, col=0):
    return pl.BlockSpec((tm, w), lambda i: (i, col))


def _full(shape):
    return pl.BlockSpec(shape, lambda i: (0,) * len(shape))


def _prev8(tm, w, col=0):
    return pl.BlockSpec((SUBLANES, w), lambda i: (jnp.maximum(i * (tm // SUBLANES) - 1, 0), col))


def _next8(tm, w, n_rows, col=0):
    last = n_rows // SUBLANES - 1
    return pl.BlockSpec((SUBLANES, w), lambda i: (jnp.minimum((i + 1) * (tm // SUBLANES), last), col))


def _shift_down(x, halo, k, first):
    rolled = pltpu.roll(x, k, 0)
    row = lax.broadcasted_iota(jnp.int32, x.shape, 0)
    out = rolled
    for j in range(k):
        h = jnp.where(first, 0.0, halo[SUBLANES - k + j:SUBLANES - k + j + 1, :])
        out = jnp.where(row == j, h, out)
    return out


def _shift_up(x, halo, k, last):
    n = x.shape[0]
    rolled = pltpu.roll(x, n - k, 0)
    row = lax.broadcasted_iota(jnp.int32, x.shape, 0)
    out = rolled
    for j in range(k):
        h = jnp.where(last, 0.0, halo[j:j + 1, :])
        out = jnp.where(row == n - k + j, h, out)
    return out


def _acc(ref, val, first):
    @pl.when(first)
    def _():
        ref[...] = val

    @pl.when(jnp.logical_not(first))
    def _():
        ref[...] += val


def _colsum(x):
    return jnp.sum(x, axis=0, keepdims=True)


def _norm_fwd(x, mo, gt, nw, sc, sh, name, tm=256):
    S = x.shape[0]
    has_res = mo is not None

    def body(*refs):
        if has_res:
            x_ref, mo_ref, gt_ref, nw_ref, sc_ref, sh_ref, x2_ref, h_ref, rs_ref = refs
            x2 = x_ref[...] + gt_ref[...] * mo_ref[...]
            x2_ref[...] = x2
        else:
            x_ref, nw_ref, sc_ref, sh_ref, h_ref, rs_ref = refs
            x2 = x_ref[...]
        rstd = lax.rsqrt(jnp.mean(x2 * x2, axis=-1, keepdims=True) + RMS_EPS)
        rs_ref[...] = rstd
        h_ref[...] = ((x2 * rstd * nw_ref[...]) * (1.0 + sc_ref[...]) + sh_ref[...]).astype(BF16)

    vec = _full((1, D))
    ins = [x, mo, gt, nw, sc, sh] if has_res else [x, nw, sc, sh]
    in_specs = [_rows(tm, D), _rows(tm, D), vec, vec, vec, vec] if has_res else [_rows(tm, D), vec, vec, vec]
    outs = [jax.ShapeDtypeStruct((S, D), BF16), jax.ShapeDtypeStruct((S, 1), F32)]
    out_specs = [_rows(tm, D), _rows(tm, 1)]
    if has_res:
        outs = [jax.ShapeDtypeStruct((S, D), F32)] + outs
        out_specs = [_rows(tm, D)] + out_specs
    return pl.pallas_call(body, name=name, grid=(S // tm,), in_specs=in_specs, out_specs=out_specs,
                          out_shape=outs, compiler_params=_cparams(("parallel",)))(*ins)


def _norm_bwd(dh, xin, rstd, nw, sc, dres, mo, gt, name, tm=256):
    S = xin.shape[0]
    has_res = mo is not None

    def body(*refs):
        if has_res:
            dh_ref, x_ref, rs_ref, nw_ref, sc_ref, dres_ref, mo_ref, gt_ref, dx_ref, dsh_ref, dsc_ref, dnw_ref, dmo_ref, dgt_ref = refs
        else:
            dh_ref, x_ref, rs_ref, nw_ref, sc_ref, dres_ref, dx_ref, dsh_ref, dsc_ref, dnw_ref = refs
        first = pl.program_id(0) == 0
        dh = dh_ref[...]
        rstd = rs_ref[...]
        n = x_ref[...] * rstd
        w = nw_ref[...]
        _acc(dsh_ref, _colsum(dh), first)
        _acc(dsc_ref, _colsum(dh * (n * w)), first)
        dnw = dh * (1.0 + sc_ref[...])
        _acc(dnw_ref, _colsum(dnw * n), first)
        dn = dnw * w
        dx = dres_ref[...] + rstd * (dn - n * jnp.mean(dn * n, axis=-1, keepdims=True))
        dx_ref[...] = dx
        if has_res:
            dmo_ref[...] = (dx * gt_ref[...]).astype(BF16)
            _acc(dgt_ref, _colsum(dx * mo_ref[...]), first)

    vec = _full((1, D))
    vshape = jax.ShapeDtypeStruct((1, D), F32)
    ins = [dh, xin, rstd, nw, sc, dres] + ([mo, gt] if has_res else [])
    in_specs = [_rows(tm, D), _rows(tm, D), _rows(tm, 1), vec, vec, _rows(tm, D)] + ([_rows(tm, D), vec] if has_res else [])
    outs = [jax.ShapeDtypeStruct((S, D), F32), vshape, vshape, vshape]
    out_specs = [_rows(tm, D), vec, vec, vec]
    if has_res:
        outs += [jax.ShapeDtypeStruct((S, D), BF16), vshape]
        out_specs += [_rows(tm, D), vec]
    return pl.pallas_call(body, name=name, grid=(S // tm,), in_specs=in_specs, out_specs=out_specs,
                          out_shape=outs, compiler_params=_cparams(("arbitrary",)))(*ins)


def _final(x2, f, gt2, nfw, target, tm=256):
    S = x2.shape[0]

    def body(x2_ref, f_ref, gt_ref, w_ref, t_ref, loss_ref, dx_ref, df_ref, dgt_ref, dw_ref):
        first = pl.program_id(0) == 0
        f = f_ref[...]
        gt = gt_ref[...]
        w = w_ref[...]
        x3 = x2_ref[...] + gt * f
        rstd = lax.rsqrt(jnp.mean(x3 * x3, axis=-1, keepdims=True) + RMS_EPS)
        n = x3 * rstd
        e = n * w - t_ref[...]
        part = 0.5 * jnp.sum(jnp.mean(e * e, axis=-1, keepdims=True), axis=0, keepdims=True)
        _acc(loss_ref, jnp.broadcast_to(part, (SUBLANES, LANES)), first)
        dy = e * (1.0 / D)
        _acc(dw_ref, _colsum(dy * n), first)
        dn = dy * w
        dx = rstd * (dn - n * jnp.mean(dn * n, axis=-1, keepdims=True))
        dx_ref[...] = dx
        df_ref[...] = (dx * gt).astype(BF16)
        _acc(dgt_ref, _colsum(dx * f), first)

    vec = _full((1, D))
    vshape = jax.ShapeDtypeStruct((1, D), F32)
    return pl.pallas_call(
        body, name="final_loss", grid=(S // tm,),
        in_specs=[_rows(tm, D), _rows(tm, D), vec, vec, _rows(tm, D)],
        out_specs=[_full((SUBLANES, LANES)), _rows(tm, D), _rows(tm, D), vec, vec],
        out_shape=[jax.ShapeDtypeStruct((SUBLANES, LANES), F32), jax.ShapeDtypeStruct((S, D), F32),
                   jax.ShapeDtypeStruct((S, D), BF16), vshape, vshape],
        compiler_params=_cparams(("arbitrary",)))(x2, f, gt2, nfw, target)


def _gate_fwd(P, bga, bgr, y_att, y_rwkv, tm=256):
    S = P.shape[0]

    def body(pa_ref, pr_ref, ba_ref, br_ref, ya_ref, yr_ref, mix_ref):
        ga = jax.nn.sigmoid(pa_ref[...] + ba_ref[...])
        gr = jax.nn.sigmoid(pr_ref[...] + br_ref[...])
        mix_ref[...] = (ga * ya_ref[...] + gr * yr_ref[...]).astype(BF16)

    vec = _full((1, D))
    return pl.pallas_call(
        body, name="gate_fwd", grid=(S // tm,),
        in_specs=[_rows(tm, D, C_GA // D), _rows(tm, D, C_GR // D), vec, vec, _rows(tm, D), _rows(tm, D)],
        out_specs=_rows(tm, D), out_shape=jax.ShapeDtypeStruct((S, D), BF16),
        compiler_params=_cparams(("parallel",)))(P, P, bga, bgr, y_att, y_rwkv)


def _gate_bwd(dmix, P, bga, bgr, y_att, y_rwkv, tm=256):
    S = P.shape[0]

    def body(dm_ref, pa_ref, pr_ref, ba_ref, br_ref, ya_ref, yr_ref, dya_ref, dyr_ref, dpa_ref, dpr_ref, dba_ref, dbr_ref):
        first = pl.program_id(0) == 0
        dm = dm_ref[...]
        ga = jax.nn.sigmoid(pa_ref[...] + ba_ref[...])
        gr = jax.nn.sigmoid(pr_ref[...] + br_ref[...])
        dya_ref[...] = (dm * ga).astype(BF16)
        dyr_ref[...] = (dm * gr).astype(BF16)
        dpa = dm * ya_ref[...] * ga * (1.0 - ga)
        dpr = dm * yr_ref[...] * gr * (1.0 - gr)
        dpa_ref[...] = dpa.astype(BF16)
        dpr_ref[...] = dpr.astype(BF16)
        _acc(dba_ref, _colsum(dpa), first)
        _acc(dbr_ref, _colsum(dpr), first)

    vec = _full((1, D))
    row = _rows(tm, D)
    rshape = jax.ShapeDtypeStruct((S, D), BF16)
    vshape = jax.ShapeDtypeStruct((1, D), F32)
    return pl.pallas_call(
        body, name="gate_bwd", grid=(S // tm,),
        in_specs=[row, _rows(tm, D, C_GA // D), _rows(tm, D, C_GR // D), vec, vec, row, row],
        out_specs=[row, row, row, row, vec, vec],
        out_shape=[rshape, rshape, rshape, rshape, vshape, vshape],
        compiler_params=_cparams(("arbitrary",)))(dmix, P, P, bga, bgr, y_att, y_rwkv)


CONV_TN = D_FF // 2


def _conv_fwd(u, conv_w8, conv_b, tm=256, tn=CONV_TN):
    S = u.shape[0]
    nj = D_FF // tn

    def conv(u_ref, h_ref, w_ref, b_ref, first):
        u = u_ref[...]
        h = h_ref[...]
        w = w_ref[...]
        return b_ref[...] + w[0:1] * _shift_down(u, h, 2, first) + w[1:2] * _shift_down(u, h, 1, first) + w[2:3] * u

    def body(ug_ref, hg_ref, uv_ref, hv_ref, wg_ref, wv_ref, bg_ref, bv_ref, act_ref):
        first = pl.program_id(0) == 0
        g = conv(ug_ref, hg_ref, wg_ref, bg_ref, first)
        v = conv(uv_ref, hv_ref, wv_ref, bv_ref, first)
        act_ref[...] = (g * jax.nn.sigmoid(g) * v).astype(BF16)

    blk = lambda off: pl.BlockSpec((tm, tn), lambda i, j: (i, j + off))
    halo = lambda off: pl.BlockSpec((SUBLANES, tn), lambda i, j: (jnp.maximum(i * (tm // SUBLANES) - 1, 0), j + off))
    wsp = lambda off: pl.BlockSpec((SUBLANES, tn), lambda i, j: (0, j + off))
    bsp = lambda off: pl.BlockSpec((1, tn), lambda i, j: (0, j + off))
    return pl.pallas_call(
        body, name="conv_fwd", grid=(S // tm, nj),
        in_specs=[blk(0), halo(0), blk(nj), halo(nj), wsp(0), wsp(nj), bsp(0), bsp(nj)],
        out_specs=pl.BlockSpec((tm, tn), lambda i, j: (i, j)),
        out_shape=jax.ShapeDtypeStruct((S, D_FF), BF16),
        compiler_params=_cparams(("parallel", "parallel")))(u, u, u, u, conv_w8, conv_w8, conv_b, conv_b)


def _conv_bwd_a(dact, u, conv_w8, conv_b, tm=256, tn=CONV_TN):
    S = u.shape[0]
    nj = D_FF // tn

    def half(u_ref, h_ref, w_ref, b_ref, first):
        u = u_ref[...]
        h = h_ref[...]
        w = w_ref[...]
        u2, u1 = _shift_down(u, h, 2, first), _shift_down(u, h, 1, first)
        return b_ref[...] + w[0:1] * u2 + w[1:2] * u1 + w[2:3] * u, (u2, u1, u)

    def wgrad(d, taps):
        z = jnp.zeros((SUBLANES - 3, d.shape[1]), F32)
        return jnp.concatenate([_colsum(d * taps[0]), _colsum(d * taps[1]), _colsum(d * taps[2]), z], axis=0)

    def body(da_ref, ug_ref, hg_ref, uv_ref, hv_ref, wg_ref, wv_ref, bg_ref, bv_ref,
             d_ref, dwg_ref, dwv_ref, dbg_ref, dbv_ref):
        first = pl.program_id(1) == 0
        g, tg = half(ug_ref, hg_ref, wg_ref, bg_ref, first)
        v, tv = half(uv_ref, hv_ref, wv_ref, bv_ref, first)
        da = da_ref[...].astype(F32)
        sg = jax.nn.sigmoid(g)
        dg = da * v * (sg * (1.0 + g * (1.0 - sg)))
        dv = da * (g * sg)
        d_ref[0] = dg
        d_ref[1] = dv
        _acc(dwg_ref, wgrad(dg, tg), first)
        _acc(dwv_ref, wgrad(dv, tv), first)
        _acc(dbg_ref, _colsum(dg), first)
        _acc(dbv_ref, _colsum(dv), first)

    blk = lambda off: pl.BlockSpec((tm, tn), lambda j, i: (i, j + off))
    halo = lambda off: pl.BlockSpec((SUBLANES, tn), lambda j, i: (jnp.maximum(i * (tm // SUBLANES) - 1, 0), j + off))
    wsp = lambda off: pl.BlockSpec((SUBLANES, tn), lambda j, i: (0, j + off))
    bsp = lambda off: pl.BlockSpec((1, tn), lambda j, i: (0, j + off))
    f = jax.ShapeDtypeStruct
    outs = pl.pallas_call(
        body, name="conv_bwd_a", grid=(nj, S // tm),
        in_specs=[pl.BlockSpec((tm, tn), lambda j, i: (i, j)), blk(0), halo(0), blk(nj), halo(nj), wsp(0), wsp(nj), bsp(0), bsp(nj)],
        out_specs=[pl.BlockSpec((2, tm, tn), lambda j, i: (0, i, j)),
                   pl.BlockSpec((SUBLANES, tn), lambda j, i: (0, j)), pl.BlockSpec((SUBLANES, tn), lambda j, i: (0, j)),
                   pl.BlockSpec((1, tn), lambda j, i: (0, j)), pl.BlockSpec((1, tn), lambda j, i: (0, j))],
        out_shape=[f((2, S, D_FF), F32), f((SUBLANES, D_FF), F32), f((SUBLANES, D_FF), F32),
                   f((1, D_FF), F32), f((1, D_FF), F32)],
        compiler_params=_cparams(("parallel", "arbitrary")))(dact, u, u, u, u, conv_w8, conv_w8, conv_b, conv_b)
    return outs


def _conv_bwd_b(duc, conv_w8, tm=256, tn=CONV_TN):
    _, S, W = duc.shape
    nj = W // tn
    n_rows = S // tm

    def body(d_ref, h_ref, w_ref, o_ref):
        last = pl.program_id(0) == n_rows - 1
        d = d_ref[...]
        h = h_ref[...]
        w = w_ref[...]
        o_ref[...] = (w[2:3] * d + w[1:2] * _shift_up(d, h, 1, last) + w[0:1] * _shift_up(d, h, 2, last)).astype(BF16)

    last_tile = S // SUBLANES - 1
    return pl.pallas_call(
        body, name="conv_bwd_b", grid=(n_rows, 2 * nj),
        in_specs=[pl.BlockSpec((None, tm, tn), lambda i, j: (j // nj, i, j % nj)),
                  pl.BlockSpec((None, SUBLANES, tn), lambda i, j: (j // nj, jnp.minimum((i + 1) * (tm // SUBLANES), last_tile), j % nj)),
                  pl.BlockSpec((SUBLANES, tn), lambda i, j: (0, j))],
        out_specs=pl.BlockSpec((tm, tn), lambda i, j: (i, j)),
        out_shape=jax.ShapeDtypeStruct((S, 2 * W), BF16),
        compiler_params=_cparams(("parallel", "parallel")))(duc, duc, conv_w8)


ATT_SCALE = HEAD ** -0.5
NEG = -1e30
ATT_PAIRS = ATT_HEADS // 2


def _att_rows(n, d, S):
    per = S // (QBLK * d)
    r, m = n // per, n % per
    cur = pl.ds(m * (QBLK * d) + r, QBLK, stride=d)
    prv = pl.ds(jnp.maximum(m - 1, 0) * (QBLK * d) + r, QBLK, stride=d)
    return cur, prv, m > 0


def _att_slab(g, j):
    return (C_ATT + g * 3 * ATT_W + j * ATT_W) // LANES


def _heads(x):
    return x[:, 0:HEAD], x[:, HEAD:2 * HEAD]


def _att_scores(q, kc, kp, has_prev):
    qi = lax.broadcasted_iota(jnp.int32, (QBLK, QBLK), 0)
    kj = lax.broadcasted_iota(jnp.int32, (QBLK, QBLK), 1)
    nt = (((1,), (1,)), ((), ()))
    s_c = lax.dot_general(q, kc, nt, preferred_element_type=F32) * ATT_SCALE
    s_p = lax.dot_general(q, kp, nt, preferred_element_type=F32) * ATT_SCALE
    s_c = jnp.where(kj <= qi, s_c, NEG)
    s_p = jnp.where(jnp.logical_and(kj >= qi, has_prev), s_p, NEG)
    return s_c, s_p


def _att_fwd(P, g):
    S = P.shape[0]
    d = ATT_PATTERNS[g][1]

    def body(q_ref, k_ref, v_ref, o_ref, l_ref):
        def blk(n, carry):
            cur, prv, has_prev = _att_rows(n, d, S)
            q2, kc2, kp2 = q_ref[cur, :].astype(BF16), k_ref[cur, :].astype(BF16), k_ref[prv, :].astype(BF16)
            vc2, vp2 = v_ref[cur, :].astype(BF16), v_ref[prv, :].astype(BF16)
            outs, lses = [], []
            for q, kc, kp, vc, vp in zip(_heads(q2), _heads(kc2), _heads(kp2), _heads(vc2), _heads(vp2)):
                s_c, s_p = _att_scores(q, kc, kp, has_prev)
                m = jnp.maximum(jnp.max(s_c, axis=1, keepdims=True), jnp.max(s_p, axis=1, keepdims=True))
                p_c = jnp.exp(s_c - m)
                p_p = jnp.exp(s_p - m)
                den = jnp.sum(p_c, axis=1, keepdims=True) + jnp.sum(p_p, axis=1, keepdims=True)
                num = (jnp.dot(p_c.astype(BF16), vc, preferred_element_type=F32)
                       + jnp.dot(p_p.astype(BF16), vp, preferred_element_type=F32))
                outs.append(num / den)
                lses.append(jnp.broadcast_to(m + jnp.log(den), (QBLK, HEAD)))
            o_ref[cur, :] = jnp.concatenate(outs, axis=1)
            l_ref[cur, :] = jnp.concatenate(lses, axis=1)
            return carry

        lax.fori_loop(0, S // QBLK, blk, 0, unroll=2)

    slab = lambda j: pl.BlockSpec((S, LANES), lambda i: (0, _att_slab(g, j) + i))
    out = pl.BlockSpec((S, LANES), lambda i: (0, i))
    shp = jax.ShapeDtypeStruct((S, ATT_W), F32)
    return pl.pallas_call(body, name=f"att_fwd_g{g}", grid=(ATT_PAIRS,), in_specs=[slab(0), slab(1), slab(2)],
                          out_specs=[out, out], out_shape=[shp, shp], compiler_params=_cparams(("parallel",)))(P, P, P)


def _att_bwd(P, o, l, do, dl, g):
    S = P.shape[0]
    d = ATT_PATTERNS[g][1]
    tn = (((0,), (0,)), ((), ()))
    nt = (((1,), (1,)), ((), ()))

    def body(q_ref, k_ref, v_ref, o_ref, l_ref, do_ref, dl_ref, dq_ref, dk_ref, dv_ref, dq_acc, dk_acc, dv_acc):
        dk_acc[...] = jnp.zeros_like(dk_acc)
        dv_acc[...] = jnp.zeros_like(dv_acc)

        def blk(n, carry):
            cur, prv, has_prev = _att_rows(n, d, S)
            q2, kc2, kp2 = q_ref[cur, :].astype(BF16), k_ref[cur, :].astype(BF16), k_ref[prv, :].astype(BF16)
            vc2, vp2 = v_ref[cur, :].astype(BF16), v_ref[prv, :].astype(BF16)
            do2 = do_ref[cur, :]
            dd2 = do2 * o_ref[cur, :] - dl_ref[cur, :]
            l2 = l_ref[cur, :]
            res = []
            for q, kc, kp, vc, vp, dob, dd, lse in zip(_heads(q2), _heads(kc2), _heads(kp2), _heads(vc2), _heads(vp2),
                                                     _heads(do2), _heads(dd2), _heads(l2)):
                s_c, s_p = _att_scores(q, kc, kp, has_prev)
                p_c = jnp.exp(s_c - lse[:, 0:1])
                p_p = jnp.exp(s_p - lse[:, 0:1])
                delta = jnp.sum(dd, axis=1, keepdims=True)
                dob16 = dob.astype(BF16)
                dp_c = lax.dot_general(dob16, vc, nt, preferred_element_type=F32)
                dp_p = lax.dot_general(dob16, vp, nt, preferred_element_type=F32)
                ds_c = (p_c * (dp_c - delta) * ATT_SCALE).astype(BF16)
                ds_p = (p_p * (dp_p - delta) * ATT_SCALE).astype(BF16)
                res.append((
                    jnp.dot(ds_c, kc, preferred_element_type=F32) + jnp.dot(ds_p, kp, preferred_element_type=F32),
                    lax.dot_general(ds_c, q, tn, preferred_element_type=F32),
                    lax.dot_general(ds_p, q, tn, preferred_element_type=F32),
                    lax.dot_general(p_c.astype(BF16), dob16, tn, preferred_element_type=F32),
                    lax.dot_general(p_p.astype(BF16), dob16, tn, preferred_element_type=F32)))
            both = [jnp.concatenate([res[0][i], res[1][i]], axis=1) for i in range(5)]
            dq_acc[cur, :] = both[0]
            dk_acc[cur, :] += both[1]
            dv_acc[cur, :] += both[3]
            dk_acc[prv, :] += both[2]
            dv_acc[prv, :] += both[4]
            return carry

        lax.fori_loop(0, S // QBLK, blk, 0, unroll=2)
        dq_ref[...] = dq_acc[...].astype(BF16)
        dk_ref[...] = dk_acc[...].astype(BF16)
        dv_ref[...] = dv_acc[...].astype(BF16)

    slab = lambda j: pl.BlockSpec((S, LANES), lambda i: (0, _att_slab(g, j) + i))
    blk128 = pl.BlockSpec((S, LANES), lambda i: (0, i))
    shp = jax.ShapeDtypeStruct((S, ATT_W), BF16)
    return pl.pallas_call(body, name=f"att_bwd_g{g}", grid=(ATT_PAIRS,),
                          in_specs=[slab(0), slab(1), slab(2)] + [blk128] * 4, out_specs=[blk128] * 3, out_shape=[shp] * 3,
                          scratch_shapes=[pltpu.VMEM((S, LANES), F32)] * 3,
                          compiler_params=_cparams(("parallel",)))(P, P, P, o, l, do, dl)


def _att_weights(l_refs):
    l0, l1, l2 = [r[...] for r in l_refs]
    m = jnp.maximum(jnp.maximum(l0, l1), l2)
    e = (jnp.exp(l0 - m), jnp.exp(l1 - m), jnp.exp(l2 - m))
    inv = 1.0 / (e[0] + e[1] + e[2])
    return [x * inv for x in e]


def _att_combine_fwd(os, ls, tm=512):
    S = os[0].shape[0]

    def body(o0, o1, o2, l0, l1, l2, a_ref):
        w = _att_weights((l0, l1, l2))
        a_ref[...] = (w[0] * o0[...] + w[1] * o1[...] + w[2] * o2[...]).astype(BF16)

    row = _rows(tm, ATT_W)
    return pl.pallas_call(body, name="att_combine_fwd", grid=(S // tm,), in_specs=[row] * 6, out_specs=row,
                          out_shape=jax.ShapeDtypeStruct((S, ATT_W), BF16),
                          compiler_params=_cparams(("parallel",)))(*os, *ls)


def _att_combine_bwd(da, os, ls, tm=512):
    S = da.shape[0]

    def body(da_ref, o0, o1, o2, l0, l1, l2, *out_refs):
        da = da_ref[...]
        w = _att_weights((l0, l1, l2))
        dw = (da * o0[...], da * o1[...], da * o2[...])
        mean = w[0] * dw[0] + w[1] * dw[1] + w[2] * dw[2]
        for g in range(3):
            out_refs[g][...] = w[g] * da
            out_refs[3 + g][...] = w[g] * (dw[g] - mean)

    row = _rows(tm, ATT_W)
    shp = jax.ShapeDtypeStruct((S, ATT_W), F32)
    return pl.pallas_call(body, name="att_combine_bwd", grid=(S // tm,), in_specs=[row] * 7, out_specs=[row] * 6,
                          out_shape=[shp] * 6, compiler_params=_cparams(("parallel",)))(da, *os, *ls)


@jax.custom_vjp
def _bdot(a, b):
    return jnp.dot(a.astype(BF16), b.astype(BF16), preferred_element_type=F32)


def _bdot_fwd(a, b):
    return _bdot(a, b), (a, b)


def _bdot_bwd(res, ct):
    a, b = res
    ct16 = ct.astype(BF16)
    da = lax.dot_general(ct16, b.astype(BF16), (((1,), (1,)), ((), ())), preferred_element_type=F32)
    db = lax.dot_general(a.astype(BF16), ct16, (((0,), (0,)), ((), ())), preferred_element_type=F32)
    return da, db


_bdot.defvjp(_bdot_fwd, _bdot_bwd)


def _two_piece_dot(x, m):
    hi = x.astype(BF16)
    lo = (x - hi.astype(F32)).astype(BF16)
    return jnp.dot(hi, m, preferred_element_type=F32) + jnp.dot(lo, m, preferred_element_type=F32)


def _head_sum_impl(x):
    sel = (lax.broadcasted_iota(jnp.int32, (D, LANES), 0) // HEAD == lax.broadcasted_iota(jnp.int32, (D, LANES), 1)).astype(BF16)
    sel_t = (lax.broadcasted_iota(jnp.int32, (LANES, D), 1) // HEAD == lax.broadcasted_iota(jnp.int32, (LANES, D), 0)).astype(BF16)
    return _two_piece_dot(_two_piece_dot(x, sel), sel_t)


@jax.custom_vjp
def _head_sum(x):
    return _head_sum_impl(x)


_head_sum.defvjp(lambda x: (_head_sum_impl(x), None), lambda _, ct: (_head_sum_impl(ct),))


def _softplus(z):
    return jnp.maximum(z, 0.0) + jnp.log(1.0 + jnp.exp(-jnp.abs(z)))


def _rwkv_prep_fn(zr, zrp, zk, zkp, zv, zvp, zl, zlp, mu_r, mu_k, mu_v, mu_l, w0, a0, k_k, k_a, w2, a2, g2p):
    r = zr + (zrp - zr) * mu_r
    k = zk + (zkp - zk) * mu_k
    v = zv + (zvp - zv) * mu_v
    lo = zl + (zlp - zl) * mu_l
    w_low, a_low, g_low = lo[:, 0:LORA_W], lo[:, LORA_W:LORA_W + LORA_A], lo[:, LANES:LANES + G_PAD]
    w_log = -_softplus(-(w0 + _bdot(jnp.tanh(w_low), w2))) - 0.5
    decay = -jnp.exp(w_log)
    a = jax.nn.sigmoid(a0 + _bdot(a_low, a2))
    g = _bdot(jax.nn.sigmoid(g_low), g2p)
    kmod = k * (1.0 + (a - 1.0) * k_a)
    kk = k * k_k
    kk = kk / jnp.maximum(jnp.sqrt(_head_sum(kk * kk)), 1e-12)
    return r, decay, kmod, v, -kk, kk * a, g


def _rwkv_prep_specs(tm):
    vec = _full((1, D))
    slabs = []
    for col in (C_R // D, C_K // D, C_V // D):
        slabs += [_rows(tm, D, col), _prev8(tm, D, col)]
    slabs += [_rows(tm, LORA_PAD, C_LORA // LORA_PAD), _prev8(tm, LORA_PAD, C_LORA // LORA_PAD)]
    params = [vec, vec, vec, _full((1, LORA_PAD)), vec, vec, vec, vec,
              _full((LORA_W, D)), _full((LORA_A, D)), _full((G_PAD, D))]
    return slabs, params


def _prep_inputs(refs, first):
    vals = []
    for s in range(4):
        z = refs[2 * s][...]
        vals += [z, _shift_down(z, refs[2 * s + 1][...], 1, first)]
    return vals + [r[...] for r in refs[8:19]]


def _rwkv_prep(P, params, tm=256):
    S = P.shape[0]
    slabs, pspecs = _rwkv_prep_specs(tm)

    def body(*refs):
        outs = _rwkv_prep_fn(*_prep_inputs(refs, pl.program_id(0) == 0))
        for o_ref, val in zip(refs[19:], outs):
            o_ref[...] = val

    shp = jax.ShapeDtypeStruct((S, D), F32)
    return pl.pallas_call(body, name="rwkv_prep", grid=(S // tm,), in_specs=slabs + pspecs,
                          out_specs=[_rows(tm, D)] * 7, out_shape=[shp] * 7,
                          compiler_params=_cparams(("parallel",)))(*([P] * 8), *params)


def _rwkv_prep_bwd(P, params, cts_a, cts_b, tm=128):
    S = P.shape[0]
    slabs, pspecs = _rwkv_prep_specs(tm)
    has_b = [c is not None for c in cts_b]
    n_ct = 7 + sum(has_b)

    def body(*refs):
        first = pl.program_id(0) == 0
        ins = _prep_inputs(refs, first)
        ct_refs = refs[19:19 + n_ct]
        out_refs = refs[19 + n_ct:]
        cts, pos = [], 7
        for i in range(7):
            c = ct_refs[i][...]
            if has_b[i]:
                c = c + ct_refs[pos][...]
                pos += 1
            cts.append(c)
        _, vjp = jax.vjp(_rwkv_prep_fn, *ins)
        grads = vjp(tuple(cts))
        for s in range(4):
            out_refs[s][...] = grads[2 * s]
            out_refs[4 + s][...] = grads[2 * s + 1]
        for i in range(11):
            _acc(out_refs[8 + i], grads[8 + i], first)

    ct_in = list(cts_a) + [c for c in cts_b if c is not None]
    row, lrow = _rows(tm, D), _rows(tm, LORA_PAD)
    f = jax.ShapeDtypeStruct
    zshapes = [f((S, D), F32)] * 3 + [f((S, LORA_PAD), F32)]
    pshapes = [f((1, D), F32)] * 3 + [f((1, LORA_PAD), F32)] + [f((1, D), F32)] * 4 + [f((LORA_W, D), F32), f((LORA_A, D), F32), f((G_PAD, D), F32)]
    return pl.pallas_call(
        body, name="rwkv_prep_bwd", grid=(S // tm,),
        in_specs=slabs + pspecs + [row] * n_ct,
        out_specs=[row, row, row, lrow] * 2 + pspecs,
        out_shape=zshapes * 2 + pshapes,
        compiler_params=_cparams(("arbitrary",)))(*([P] * 8), *params, *ct_in)


def _shift_add(a, b, tm=256):
    S, W = a.shape

    def body(a_ref, b_ref, h_ref, o_ref):
        last = pl.program_id(0) == pl.num_programs(0) - 1
        o_ref[...] = (a_ref[...] + _shift_up(b_ref[...], h_ref[...], 1, last)).astype(BF16)

    return pl.pallas_call(body, name="shift_add", grid=(S // tm,),
                          in_specs=[_rows(tm, W), _rows(tm, W), _next8(tm, W, S)],
                          out_specs=_rows(tm, W), out_shape=jax.ShapeDtypeStruct((S, W), BF16),
                          compiler_params=_cparams(("parallel",)))(a, b, b)


def _rwkv_post_fn(y, r, kmod, v, g, lnx_w, lnx_b, r_k):
    mean = _head_sum(y) * (1.0 / HEAD)
    yc = y - mean
    var = _head_sum(yc * yc) * (1.0 / HEAD)
    yn = yc * lax.rsqrt(var + GN_EPS) * lnx_w + lnx_b
    bonus = _head_sum(r * kmod * r_k) * v
    return (yn + bonus) * g


def _rwkv_post(y, r, kmod, v, g, lnx_w, lnx_b, r_k, tm=256):
    S = y.shape[0]

    def body(y_ref, r_ref, k_ref, v_ref, g_ref, w_ref, b_ref, rk_ref, o_ref):
        o_ref[...] = _rwkv_post_fn(y_ref[...], r_ref[...], k_ref[...], v_ref[...], g_ref[...],
                                   w_ref[...], b_ref[...], rk_ref[...]).astype(BF16)

    row, vec = _rows(tm, D), _full((1, D))
    return pl.pallas_call(body, name="rwkv_post", grid=(S // tm,), in_specs=[row] * 5 + [vec] * 3, out_specs=row,
                          out_shape=jax.ShapeDtypeStruct((S, D), BF16),
                          compiler_params=_cparams(("parallel",)))(y, r, kmod, v, g, lnx_w, lnx_b, r_k)


def _rwkv_post_bwd(drw, y, r, kmod, v, g, lnx_w, lnx_b, r_k, tm=256):
    S = y.shape[0]

    def body(d_ref, y_ref, r_ref, k_ref, v_ref, g_ref, w_ref, b_ref, rk_ref, *out_refs):
        first = pl.program_id(0) == 0
        _, vjp = jax.vjp(_rwkv_post_fn, y_ref[...], r_ref[...], k_ref[...], v_ref[...], g_ref[...],
                         w_ref[...], b_ref[...], rk_ref[...])
        grads = vjp(d_ref[...])
        for i in range(5):
            out_refs[i][...] = grads[i]
        for i in range(5, 8):
            _acc(out_refs[i], grads[i], first)

    row, vec = _rows(tm, D), _full((1, D))
    f = jax.ShapeDtypeStruct
    return pl.pallas_call(body, name="rwkv_post_bwd", grid=(S // tm,), in_specs=[row] * 6 + [vec] * 3,
                          out_specs=[row] * 5 + [vec] * 3, out_shape=[f((S, D), F32)] * 5 + [f((1, D), F32)] * 3,
                          compiler_params=_cparams(("arbitrary",)))(drw, y, r, kmod, v, g, lnx_w, lnx_b, r_k)


CHUNK = 64
CHUNK_TB = 256
_DOT_DIMS = {"nn": (((2,), (1,)), ((0,), (0,))), "nt": (((2,), (2,)), ((0,), (0,))), "tn": (((1,), (1,)), ((0,), (0,)))}


def _dot16(x, y, mode):
    return lax.dot_general(x.astype(BF16), y.astype(BF16), _DOT_DIMS[mode], preferred_element_type=F32)


@functools.partial(jax.custom_vjp, nondiff_argnums=(2,))
def _mm16(x, y, mode):
    return _dot16(x, y, mode)


def _mm16_fwd(x, y, mode):
    return _dot16(x, y, mode), (x, y)


def _mm16_bwd(mode, res, ct):
    x, y = res
    if mode == "nn":
        return _dot16(ct, y, "nt"), _dot16(x, ct, "tn")
    if mode == "nt":
        return _dot16(ct, y, "nn"), _dot16(ct, x, "tn")
    return _dot16(y, ct, "nt"), _dot16(x, ct, "nn")


_mm16.defvjp(_mm16_fwd, _mm16_bwd)


def _tri_sum(x, upper):
    T = x.shape[0]
    i = lax.broadcasted_iota(jnp.int32, (T, T), 0)
    j = lax.broadcasted_iota(jnp.int32, (T, T), 1)
    tri = ((j >= i) if upper else (i >= j)).astype(BF16)
    out, rest = None, x
    for _ in range(3):
        piece = rest.astype(BF16)
        rest = rest - piece.astype(F32)
        part = jnp.dot(tri, piece, preferred_element_type=F32)
        out = part if out is None else out + part
    return out


@jax.custom_vjp
def _cumsum_rows(x):
    return _tri_sum(x, False)


_cumsum_rows.defvjp(lambda x: (_tri_sum(x, False), None), lambda _, ct: (_tri_sum(ct, True),))


def _rows_to_cols(x):
    H, _, K = x.shape
    eye = (lax.broadcasted_iota(jnp.int32, (H, K, K), 1) == lax.broadcasted_iota(jnp.int32, (H, K, K), 2)).astype(F32)
    out = lax.dot_general(eye, jnp.broadcast_to(x, (H, SUBLANES, K)), _DOT_DIMS["nt"],
                          precision=lax.Precision.HIGHEST, preferred_element_type=F32)
    return out[:, :, 0:1]


def _per_head(x):
    return jnp.concatenate([x[:, h * HEAD:(h + 1) * HEAD][None] for h in range(N_HEADS)], axis=0)


def _chunk_fn(st0, r, lw, k, v, a, b):
    T = r.shape[0]
    cl = _cumsum_rows(lw)
    cl_end = cl[T - 1:T, :]
    inv = jnp.exp(-cl)
    to_end = jnp.exp(cl_end - cl)
    ah, rh, bh, kh, be, ke, v3 = [_per_head(x) for x in
                                  (a * jnp.exp(cl - lw), r * jnp.exp(cl), b * inv, k * inv, b * to_end, k * to_end, v)]
    i = lax.broadcasted_iota(jnp.int32, (N_HEADS, T, T), 1)
    j = lax.broadcasted_iota(jnp.int32, (N_HEADS, T, T), 2)
    a_ab = jnp.where(i > j, _mm16(ah, bh, "nt"), 0.0)
    a_ak = jnp.where(i > j, _mm16(ah, kh, "nt"), 0.0)
    m_rb = jnp.where(i >= j, _mm16(rh, bh, "nt"), 0.0)
    m_rk = jnp.where(i >= j, _mm16(rh, kh, "nt"), 0.0)
    rhs = _mm16(ah, st0, "nn") + _mm16(a_ak, v3, "nn")
    power, solve, n = a_ab, (i == j).astype(F32) + a_ab, 1
    while 2 * n < T:
        power = _mm16(power, power, "nn")
        solve = solve + _mm16(solve, power, "nn")
        n *= 2
    sa = _mm16(solve, rhs, "nn")
    y3 = _mm16(rh, st0, "nn") + _mm16(m_rb, sa, "nn") + _mm16(m_rk, v3, "nn")
    st_end = _rows_to_cols(_per_head(jnp.exp(cl_end))) * st0 + _mm16(be, sa, "tn") + _mm16(ke, v3, "tn")
    return jnp.concatenate([y3[h] for h in range(N_HEADS)], axis=1), st_end


def _hosted_exchange(refs, n, broadcast, grid):
    if n == 0:
        return lambda: None
    start, wait = _exchange_ops(refs[:n], refs[n:2 * n], *refs[2 * n:], broadcast)
    first = functools.reduce(jnp.logical_and, [pl.program_id(a) == 0 for a in range(len(grid))])
    last = functools.reduce(jnp.logical_and, [pl.program_id(a) == g - 1 for a, g in enumerate(grid)])
    pl.when(first)(start)
    return lambda: pl.when(last)(wait)


def _cscan_fwd(r, lw, k, v, a, b, gather=()):
    S = r.shape[0]
    per_blk = CHUNK_TB // CHUNK
    n_x = len(gather)
    nblk = S // CHUNK_TB

    def body(*refs):
        r_ref, lw_ref, k_ref, v_ref, a_ref, b_ref = refs[:6]
        y_ref, ck_ref = refs[6 + n_x:8 + n_x]
        st_ref = refs[8 + 2 * n_x]
        finish = _hosted_exchange(refs[6:6 + n_x] + refs[8 + n_x:8 + 2 * n_x] + refs[9 + 2 * n_x:], n_x, True, (nblk,))

        @pl.when(pl.program_id(0) == 0)
        def _():
            st_ref[...] = jnp.zeros_like(st_ref)

        def chunk(c, carry):
            rows = pl.ds(pl.multiple_of(c * CHUNK, CHUNK), CHUNK)
            st0 = st_ref[...]
            ck_ref[c] = st0
            y, st_end = _chunk_fn(st0, r_ref[rows, :], lw_ref[rows, :], k_ref[rows, :],
                                  v_ref[rows, :], a_ref[rows, :], b_ref[rows, :])
            y_ref[rows, :] = y
            st_ref[...] = st_end
            return carry

        lax.fori_loop(0, per_blk, chunk, 0)
        finish()

    blk = _rows(CHUNK_TB, D)
    any_spec = pl.BlockSpec(memory_space=pl.ANY)
    outs = pl.pallas_call(
        body, name="scan_fwd", grid=(nblk,), in_specs=[blk] * 6 + [any_spec] * n_x,
        out_specs=[blk, pl.BlockSpec((per_blk, N_HEADS, HEAD, HEAD), lambda i: (i, 0, 0, 0))] + [any_spec] * n_x,
        out_shape=[jax.ShapeDtypeStruct((S, D), F32), jax.ShapeDtypeStruct((S // CHUNK, N_HEADS, HEAD, HEAD), F32)]
        + _exchange_shapes(gather, True),
        scratch_shapes=[pltpu.VMEM((N_HEADS, HEAD, HEAD), F32)] + (_exchange_scratch(n_x) if n_x else []),
        compiler_params=_cparams(("arbitrary",)))(r, lw, k, v, a, b, *gather)
    return outs[0], outs[1], outs[2:]


def _cscan_bwd(r, lw, k, v, a, b, ckpt, dy, scatter=()):
    S = r.shape[0]
    per_blk = CHUNK_TB // CHUNK
    nblk = S // CHUNK_TB
    n_x = len(scatter)

    def body(*refs):
        r_ref, lw_ref, k_ref, v_ref, a_ref, b_ref, ck_ref, dy_ref = refs[:8]
        out_refs = refs[8 + n_x:14 + n_x]
        ds_ref = refs[14 + 2 * n_x]
        finish = _hosted_exchange(refs[8:8 + n_x] + refs[14 + n_x:14 + 2 * n_x] + refs[15 + 2 * n_x:], n_x, False, (nblk,))

        @pl.when(pl.program_id(0) == 0)
        def _():
            ds_ref[...] = jnp.zeros_like(ds_ref)

        def chunk(cc, carry):
            c = per_blk - 1 - cc
            rows = pl.ds(pl.multiple_of(c * CHUNK, CHUNK), CHUNK)
            ins = (ck_ref[c], r_ref[rows, :], lw_ref[rows, :], k_ref[rows, :], v_ref[rows, :], a_ref[rows, :], b_ref[rows, :])
            _, vjp = jax.vjp(_chunk_fn, *ins)
            grads = vjp((dy_ref[rows, :], ds_ref[...]))
            ds_ref[...] = grads[0]
            for o_ref, g in zip(out_refs, grads[1:]):
                o_ref[rows, :] = g
            return carry

        lax.fori_loop(0, per_blk, chunk, 0)
        finish()

    blk = pl.BlockSpec((CHUNK_TB, D), lambda i: (nblk - 1 - i, 0))
    any_spec = pl.BlockSpec(memory_space=pl.ANY)
    shp = jax.ShapeDtypeStruct((S, D), F32)
    outs = pl.pallas_call(
        body, name="scan_bwd", grid=(nblk,),
        in_specs=[blk] * 6 + [pl.BlockSpec((per_blk, N_HEADS, HEAD, HEAD), lambda i: (nblk - 1 - i, 0, 0, 0)), blk]
        + [any_spec] * n_x,
        out_specs=[blk] * 6 + [any_spec] * n_x, out_shape=[shp] * 6 + _exchange_shapes(scatter, False),
        scratch_shapes=[pltpu.VMEM((N_HEADS, HEAD, HEAD), F32)] + (_exchange_scratch(n_x) if n_x else []),
        compiler_params=_cparams(("arbitrary",)))(r, lw, k, v, a, b, ckpt, dy, *scatter)
    return outs[:6], outs[6:]


def _ada_fwd(c8, w_ada, b_ada):
    def body(c_ref, w_ref, b_ref, o_ref):
        o_ref[...] = jnp.dot(c_ref[...].astype(BF16), w_ref[...], preferred_element_type=F32) + b_ref[...]

    tn = 1536
    return pl.pallas_call(body, name="ada_fwd", grid=(6 * D // tn,),
                          in_specs=[_full((SUBLANES, D)), pl.BlockSpec((D, tn), lambda j: (0, j)), pl.BlockSpec((1, tn), lambda j: (0, j))],
                          out_specs=pl.BlockSpec((SUBLANES, tn), lambda j: (0, j)),
                          out_shape=jax.ShapeDtypeStruct((SUBLANES, 6 * D), F32),
                          compiler_params=_cparams(("parallel",)))(c8, w_ada, b_ada)


def _outer(col, row):
    N = row.shape[1]
    tn = 1536

    def body(c_ref, r_ref, o_ref):
        o_ref[...] = c_ref[...] * r_ref[...]

    return pl.pallas_call(body, name="ada_wgrad", grid=(N // tn,),
                          in_specs=[_full((D, 1)), pl.BlockSpec((1, tn), lambda j: (0, j))],
                          out_specs=pl.BlockSpec((D, tn), lambda j: (0, j)),
                          out_shape=jax.ShapeDtypeStruct((D, N), F32),
                          compiler_params=_cparams(("parallel",)))(col, row)


def _exchange(srcs, broadcast, name):
    n = len(srcs)

    def body(*refs):
        start, wait = _exchange_ops(refs[:n], refs[n:2 * n], *refs[2 * n:], broadcast)
        start()
        wait()

    any_spec = pl.BlockSpec(memory_space=pl.ANY)
    return pl.pallas_call(
        body, name=name, out_shape=_exchange_shapes(srcs, broadcast), in_specs=[any_spec] * n, out_specs=[any_spec] * n,
        scratch_shapes=_exchange_scratch(n),
        compiler_params=pltpu.CompilerParams(has_side_effects=True),
    )(*srcs)


def _exchange_shapes(srcs, broadcast):
    return [jax.ShapeDtypeStruct((N_DEV,) + (s.shape if broadcast else s.shape[1:]), s.dtype) for s in srcs]


def _exchange_scratch(n):
    return [pltpu.SemaphoreType.DMA((n, N_DEV)), pltpu.SemaphoreType.DMA((n, N_DEV)), pltpu.SemaphoreType.DMA((n,))]


def _exchange_ops(src_refs, out_refs, send_sems, recv_sems, local_sems, broadcast):
    n = len(src_refs)
    x, y, c = lax.axis_index("x"), lax.axis_index("y"), lax.axis_index("c")
    me = 4 * x + 2 * y + c

    def block(i, j):
        return src_refs[i] if broadcast else src_refs[i].at[j]

    def remote(i, d, src_slot, dst_slot):
        px, py, pc = x ^ (d >> 2), y ^ ((d >> 1) & 1), c ^ (d & 1)
        return pltpu.make_async_remote_copy(
            src_ref=block(i, src_slot), dst_ref=out_refs[i].at[dst_slot], send_sem=send_sems.at[i, d],
            recv_sem=recv_sems.at[i, d], device_id=(px, py, pc), device_id_type=_MESH)

    def local(i):
        return pltpu.make_async_copy(block(i, me), out_refs[i].at[me], local_sems.at[i])

    def start():
        for i in range(n):
            local(i).start()
        for d in range(1, N_DEV):
            for i in range(n):
                remote(i, d, me ^ d, me).start()

    def wait():
        for d in range(1, N_DEV):
            for i in range(n):
                remote(i, d, me, me ^ d).wait_recv()
        for d in range(1, N_DEV):
            for i in range(n):
                remote(i, d, me ^ d, me).wait_send()
        for i in range(n):
            local(i).wait()

    return start, wait


def _sum_adam(parts, w, m, v, name):
    _, R, C = parts.shape
    tm = 256 if R % 256 == 0 else R
    c1 = 1.0 / (1.0 - ADAM_B1 ** ADAM_STEP)
    c2 = 1.0 / (1.0 - ADAM_B2 ** ADAM_STEP)

    def body(p_ref, w_ref, m_ref, v_ref, g_ref, d_ref, nm_ref, nv_ref):
        g = p_ref[0].astype(F32)
        for j in range(1, N_DEV):
            g = g + p_ref[j].astype(F32)
        nm = ADAM_B1 * m_ref[...] + (1.0 - ADAM_B1) * g
        nv = ADAM_B2 * v_ref[...] + (1.0 - ADAM_B2) * (g * g)
        g_ref[...] = g
        nm_ref[...] = nm
        nv_ref[...] = nv
        d_ref[...] = -ADAM_LR * ((nm * c1) / (jnp.sqrt(nv * c2) + ADAM_EPS) + ADAM_WD * w_ref[...])

    row = _rows(tm, C)
    shp = jax.ShapeDtypeStruct((R, C), F32)
    return pl.pallas_call(body, name=name, grid=(R // tm,),
                          in_specs=[pl.BlockSpec((N_DEV, tm, C), lambda i: (0, i, 0)), row, row, row],
                          out_specs=[row] * 4, out_shape=[shp] * 4,
                          compiler_params=_cparams(("parallel",)))(parts, w, m, v)


PACK_ALIGN = 16 * LANES
PACK_ROWS = 512 * LANES

SHARDED = (("w_ada", 1), ("w_in", 1), ("w2", 1), ("a2", 1), ("g2", 1), ("w_att_out", 1), ("w_rwkv_out", 0),
           ("w_o", 0), ("w_up", 1), ("conv_w", 1), ("w_down", 0))
EARLY, LATE = SHARDED[:5], SHARDED[5:]
REPLICATED = ("b_ada", "norm1_w", "b_gate", "mu_shift", "w0", "a0", "k_k", "k_a", "r_k", "lnx_w", "lnx_b",
              "norm2_w", "conv_b", "norm_f_w")
WEIGHTS = ("w_ada", "b_ada", "norm1_w", "w_in", "b_gate", "mu_shift", "w0", "w2", "a0", "a2", "g2", "k_k", "k_a", "r_k",
           "lnx_w", "lnx_b", "w_att_out", "w_rwkv_out", "w_o", "norm2_w", "w_up", "conv_w", "conv_b", "w_down", "norm_f_w")


def _pack(arrays):
    flat, layout, off = [], [], 0
    for i, a in enumerate(arrays):
        n = a.size
        pad = (-n) % PACK_ALIGN if i + 1 < len(arrays) else (-(off + n)) % PACK_ROWS
        flat.append(a.reshape(-1))
        if pad:
            flat.append(jnp.zeros((pad,), a.dtype))
        layout.append((off, n, a.shape))
        off += n + pad
    return jnp.concatenate(flat).reshape(-1, LANES), layout


def _unpack(buf, layout):
    flat = buf.reshape(-1)
    return [flat[off:off + n].reshape(shape) for off, n, shape in layout]


def _pad_w_in(w_in):
    rkv = w_in[:, ATT_IN:ATT_IN + 3 * D]
    lora = w_in[:, ATT_IN + 3 * D:ATT_IN + RWKV_IN]
    gates = w_in[:, ATT_IN + RWKV_IN:]
    att = w_in[:, :ATT_IN]
    lw, la, lg = lora[:, :LORA_W], lora[:, LORA_W:LORA_W + LORA_A], lora[:, LORA_W + LORA_A:]
    zeros = jnp.zeros((w_in.shape[0], LORA_PAD - LANES - LORA_G), w_in.dtype)
    return jnp.concatenate([rkv, gates, att, lw, la, lg, zeros], axis=1)


def _unpad_w_in(g):
    att = g[:, C_ATT:C_ATT + ATT_IN]
    rkv = g[:, C_R:C_R + 3 * D]
    lora = jnp.concatenate([g[:, C_LORA:C_LORA + LORA_W + LORA_A], g[:, C_LORA + LANES:C_LORA + LANES + LORA_G]], axis=1)
    gates = g[:, C_GA:C_GA + 2 * D]
    return jnp.concatenate([att, rkv, lora, gates], axis=1)


def _pad_mu(mu):
    lo = mu[:, 3 * D:]
    mu_l = jnp.concatenate([lo[:, :LORA_W + LORA_A], lo[:, LORA_W + LORA_A:], jnp.zeros((1, LORA_PAD - LANES - LORA_G), mu.dtype)], axis=1)
    return mu[:, :D], mu[:, D:2 * D], mu[:, 2 * D:3 * D], mu_l


def _local_step(x, c, W, late_shards, target):
    S = x.shape[0]
    W = dict(W)
    G = {}
    c8 = jnp.pad(c, ((0, SUBLANES - 1), (0, 0)))
    ada = _ada_fwd(c8, W["w_ada"], W["b_ada"])[0:1]
    sh1, sc1, gt1, sh2, sc2, gt2 = [ada[:, i * D:(i + 1) * D] for i in range(6)]
    h1, rstd1 = _norm_fwd(x, None, None, W["norm1_w"], sc1, sh1, "norm1_fwd")
    w_in_p = _pad_w_in(W["w_in"])
    P = _mm(h1, w_in_p, "nn", F32, "proj_in")

    mu_r, mu_k, mu_v, mu_l = _pad_mu(W["mu_shift"])
    g2p = jnp.pad(W["g2"], ((0, G_PAD - LORA_G), (0, 0)))
    prep_params = [mu_r, mu_k, mu_v, mu_l, W["w0"], W["a0"], W["k_k"], W["k_a"], W["w2"], W["a2"], g2p]
    r_, dec, kmod, v_, aa, bb, gg = _rwkv_prep(P, prep_params)
    y_scan, states, late = _cscan_fwd(r_, dec, kmod, v_, aa, bb, gather=late_shards)
    W.update({n: _full_weight(g, axis) for (n, axis), g in zip(LATE, late)})

    o_g, l_g = zip(*[_att_fwd(P, g) for g in range(len(ATT_PATTERNS))])
    att = _att_combine_fwd(o_g, l_g)
    y_att = _mm(att, W["w_att_out"], "nn", F32, "att_out")
    r_k = W["r_k"].reshape(1, D)
    rw = _rwkv_post(y_scan, r_, kmod, v_, gg, W["lnx_w"], W["lnx_b"], r_k)
    y_rwkv = _mm(rw, W["w_rwkv_out"], "nn", F32, "rwkv_out")

    bga, bgr = W["b_gate"][:, :D], W["b_gate"][:, D:]
    mix = _gate_fwd(P, bga, bgr, y_att, y_rwkv)
    mo = _mm(mix, W["w_o"], "nn", F32, "mix_out")
    x2, h2, rstd2 = _norm_fwd(x, mo, gt1, W["norm2_w"], sc2, sh2, "norm2_fwd")
    u = _mm(h2, W["w_up"], "nn", F32, "ffn_up")
    conv_w8 = jnp.pad(W["conv_w"], ((0, SUBLANES - 3), (0, 0)))
    act = _conv_fwd(u, conv_w8, W["conv_b"])
    f = _mm(act, W["w_down"], "nn", F32, "ffn_down")
    loss_blk, dx3, df, dgt2, G["norm_f_w"] = _final(x2, f, gt2, W["norm_f_w"], target)
    loss = loss_blk[0, 0]

    dact = _mm(df, W["w_down"], "nt", BF16, "ffn_down_dx")
    G["w_down"] = _mm(act, df, "tn", F32, "ffn_down_dw")
    duc, dwg, dwv, dbg, dbv = _conv_bwd_a(dact, u, conv_w8, W["conv_b"])
    G["conv_w"] = jnp.concatenate([dwg[0:3], dwv[0:3]], axis=1)
    G["conv_b"] = jnp.concatenate([dbg, dbv], axis=1)
    du = _conv_bwd_b(duc, conv_w8)
    dh2 = _mm(du, W["w_up"], "nt", F32, "ffn_up_dx")
    G["w_up"] = _mm(h2, du, "tn", F32, "ffn_up_dw")
    dx2, dsh2, dsc2, G["norm2_w"], dmo, dgt1 = _norm_bwd(dh2, x2, rstd2, W["norm2_w"], sc2, dx3, mo, gt1, "norm2_bwd")
    dmix = _mm(dmo, W["w_o"], "nt", F32, "mix_out_dx")
    G["w_o"] = _mm(mix, dmo, "tn", F32, "mix_out_dw")
    dy_att, dy_rwkv, dpga, dpgr, dbga, dbgr = _gate_bwd(dmix, P, bga, bgr, y_att, y_rwkv)
    G["b_gate"] = jnp.concatenate([dbga, dbgr], axis=1)

    datt = _mm(dy_att, W["w_att_out"], "nt", F32, "att_out_dx")
    G["w_att_out"] = _mm(att, dy_att, "tn", F32, "att_out_dw")
    dcomb = _att_combine_bwd(datt, o_g, l_g)
    dp_att = []
    for g in range(len(ATT_PATTERNS)):
        dp_att += _att_bwd(P, o_g[g], l_g[g], dcomb[g], dcomb[3 + g], g)

    drw = _mm(dy_rwkv, W["w_rwkv_out"], "nt", F32, "rwkv_out_dx")
    G["w_rwkv_out"] = _mm(rw, dy_rwkv, "tn", F32, "rwkv_out_dw")
    dy_scan, dr1, dk1, dv1, dgg, G["lnx_w"], G["lnx_b"], drk = _rwkv_post_bwd(drw, y_scan, r_, kmod, v_, gg, W["lnx_w"], W["lnx_b"], r_k)
    G["r_k"] = drk.reshape(W["r_k"].shape)
    late_blocks = [_owner_blocks(G[n], axis) for n, axis in LATE] if late_shards else []
    (dr2, ddec, dk2, dv2, daa, dbb), late_parts = _cscan_bwd(r_, dec, kmod, v_, aa, bb, states, dy_scan, scatter=late_blocks)
    pb = _rwkv_prep_bwd(P, prep_params, [dr2, ddec, dk2, dv2, daa, dbb, dgg], [dr1, None, dk1, dv1, None, None, None])
    dz, dzp, dpar = pb[0:4], pb[4:8], pb[8:]
    dp_rkv = [_shift_add(dz[i], dzp[i]) for i in range(3)]
    dp_lora = _shift_add(dz[3], dzp[3])
    dmu_r, dmu_k, dmu_v, dmu_l, G["w0"], G["a0"], G["k_k"], G["k_a"], G["w2"], G["a2"], dg2p = dpar
    G["g2"] = dg2p[0:LORA_G]
    G["mu_shift"] = jnp.concatenate([dmu_r, dmu_k, dmu_v, dmu_l[:, :LORA_W + LORA_A], dmu_l[:, LANES:LANES + LORA_G]], axis=1)

    dP = jnp.concatenate(dp_rkv + [dpga, dpgr] + dp_att + [dp_lora], axis=1)
    G["w_in"] = _unpad_w_in(_mm(h1, dP, "tn", F32, "proj_in_dw"))
    if late_shards:
        dh1, (w_in_parts,) = _mm(dP, w_in_p, "nt", F32, "proj_in_dx", scatter=[_owner_blocks(G["w_in"], 1)])
        done = dict(zip([n for n, _ in LATE] + ["w_in"], list(late_parts) + [w_in_parts]))
    else:
        dh1, done = _mm(dP, w_in_p, "nt", F32, "proj_in_dx"), {}
    grad_x, dsh1, dsc1, G["norm1_w"] = _norm_bwd(dh1, x, rstd1, W["norm1_w"], sc1, dx2, None, None, "norm1_bwd")
    dada = jnp.concatenate([dsh1, dsc1, dgt1, dsh2, dsc2, dgt2], axis=1)
    G["b_ada"] = dada
    G["w_ada"] = _outer(c.reshape(D, 1), dada)
    return loss, grad_x, G, done


def _full_weight(gathered, axis):
    _, rows, cols = gathered.shape
    if axis == 0:
        return gathered.reshape(N_DEV * rows, cols)
    return gathered.transpose(1, 0, 2).reshape(rows, N_DEV * cols)


def _owner_blocks(g, axis):
    rows, cols = g.shape
    g = g.astype(BF16)
    if axis == 0:
        return g.reshape(N_DEV, rows // N_DEV, cols)
    return g.reshape(rows, N_DEV, cols // N_DEV).transpose(1, 0, 2)


def kernel(x, c, w_ada, b_ada, norm1_w, w_in, b_gate, mu_shift, w0, w2, a0, a2, g2, k_k, k_a, r_k, lnx_w, lnx_b, w_att_out, w_rwkv_out, w_o, norm2_w, w_up, conv_w, conv_b, w_down, norm_f_w, loss_target, m_w_ada, m_b_ada, m_norm1_w, m_w_in, m_b_gate, m_mu_shift, m_w0, m_w2, m_a0, m_a2, m_g2, m_k_k, m_k_a, m_r_k, m_lnx_w, m_lnx_b, m_w_att_out, m_w_rwkv_out, m_w_o, m_norm2_w, m_w_up, m_conv_w, m_conv_b, m_w_down, m_norm_f_w, v_w_ada, v_b_ada, v_norm1_w, v_w_in, v_b_gate, v_mu_shift, v_w0, v_w2, v_a0, v_a2, v_g2, v_k_k, v_k_a, v_r_k, v_lnx_w, v_lnx_b, v_w_att_out, v_w_rwkv_out, v_w_o, v_norm2_w, v_w_up, v_conv_w, v_conv_b, v_w_down, v_norm_f_w):
    env = dict(locals())
    w_shard = {n: env[n] for n in WEIGHTS}
    m_shard = {n: env["m_" + n] for n in WEIGHTS}
    v_shard = {n: env["v_" + n] for n in WEIGHTS}

    gathered = _exchange([w_shard[n][0].astype(BF16) for n, _ in EARLY], True, "gather_weights")
    W = {n: _full_weight(g, axis) for (n, axis), g in zip(EARLY, gathered)}
    for n in REPLICATED:
        W[n] = w_shard[n].reshape(1, -1) if n != "r_k" else w_shard[n][0]

    late_shards = [w_shard[n][0].astype(BF16) for n, _ in LATE]
    loss, grad_x, G, parts = _local_step(x[0], c, W, late_shards, loss_target[0])
    loss = lax.psum(loss, ("x", "y", "c"))

    rest = [(n, axis) for n, axis in SHARDED if n not in parts]
    parts.update(zip([n for n, _ in rest], _exchange([_owner_blocks(G[n], axis) for n, axis in rest], False, "scatter_grads")))
    out = {}
    for n, p in parts.items():
        res = _sum_adam(p, w_shard[n][0], m_shard[n][0], v_shard[n][0], "adam_" + n)
        for kind, a in zip(("grad", "delta", "new_m", "new_v"), res):
            out[kind, n] = a[None]

    small, slayout = _pack([G[n].reshape(-1) for n in REPLICATED])
    sparts, = _exchange([small], True, "gather_small_grads")
    sw, _ = _pack([w_shard[n].reshape(-1) for n in REPLICATED])
    sm, _ = _pack([m_shard[n].reshape(-1) for n in REPLICATED])
    sv, _ = _pack([v_shard[n].reshape(-1) for n in REPLICATED])
    res = _sum_adam(sparts, sw, sm, sv, "adam_replicated")
    for kind, buf in zip(("grad", "delta", "new_m", "new_v"), res):
        for n, a in zip(REPLICATED, _unpack(buf, slayout)):
            out[kind, n] = a.reshape(w_shard[n].shape)

    return (loss, grad_x[None], *[out[kind, n] for kind in ("grad", "delta", "new_m", "new_v") for n in WEIGHTS])
```

```python
import functools
import math

import jax
import jax.numpy as jnp
from jax import lax
from jax.experimental import pallas as pl
from jax.experimental.pallas import tpu as pltpu

F32 = jnp.float32
BF16 = jnp.bfloat16

D = 1024
HEAD = 64
ATT_PATTERNS = ((128, 1), (512, 4), (2048, 16))
ATT_HEADS = 8
ATT_W = ATT_HEADS * HEAD
ATT_IN = 3 * 3 * ATT_W
QBLK = 128
N_HEADS = D // HEAD
LORA_W, LORA_A, LORA_G = 64, 64, 160
RWKV_IN = 3 * D + LORA_W + LORA_A + LORA_G
N_IN = ATT_IN + RWKV_IN + 2 * D
D_FF = 2816
RMS_EPS = 1e-6
GN_EPS = 64e-5
N_DEV = 8
LANES = 128
SUBLANES = 8

C_R, C_K, C_V, C_GA, C_GR = 0, 1024, 2048, 3072, 4096
C_ATT = 5120
C_LORA = C_ATT + ATT_IN
LORA_PAD = 512
G_PAD = 256
N_PAD = C_LORA + LORA_PAD

ADAM_LR, ADAM_B1, ADAM_B2, ADAM_EPS, ADAM_WD, ADAM_STEP = 0.001, 0.9, 0.999, 1e-08, 0.01, 10

VMEM_LIMIT = 56 * 1024 * 1024

_MESH = pl.DeviceIdType.MESH


def _cparams(sem):
    return pltpu.CompilerParams(dimension_semantics=sem, vmem_limit_bytes=VMEM_LIMIT)


def _tile(dim, pref):
    if dim <= pref:
        return dim
    best = None
    for t in range(LANES, pref + 1, LANES):
        if dim % t == 0:
            best = t
    assert best is not None, dim
    return best


MM_TILES = {"nn": (1024, 1408, 1408), "nt": (512, 2048, 1408), "tn": (1408, 1408, 1024)}


def _mm(a, b, mode, out_dtype, name, scatter=()):
    if mode == "nn":
        (M, K), (K2, N) = a.shape, b.shape
    elif mode == "nt":
        (M, K), (N, K2) = a.shape, b.shape
    else:
        (K, M), (K2, N) = a.shape, b.shape
    assert K == K2, (a.shape, b.shape, mode)
    tm, tn, tk = (_tile(dim, pref) for dim, pref in zip((M, N, K), MM_TILES[mode]))
    nk = K // tk
    grid = (M // tm, N // tn, nk)
    n_x = len(scatter)
    dims = {"nn": (((1,), (0,)), ((), ())), "nt": (((1,), (1,)), ((), ())), "tn": (((0,), (0,)), ((), ()))}[mode]

    def body(*refs):
        a_ref, b_ref = refs[:2]
        o_ref, acc_ref = refs[2 + n_x], refs[3 + 2 * n_x]
        finish = _hosted_exchange(refs[2:2 + n_x] + refs[3 + n_x:3 + 2 * n_x] + refs[4 + 2 * n_x:], n_x, False, grid)
        k = pl.program_id(2)
        part = lax.dot_general(a_ref[...].astype(BF16), b_ref[...].astype(BF16), dims,
                               preferred_element_type=F32)
        if nk == 1:
            o_ref[...] = part.astype(o_ref.dtype)
        else:
            @pl.when(k == 0)
            def _():
                acc_ref[...] = part

            @pl.when(jnp.logical_and(k > 0, k < nk - 1))
            def _():
                acc_ref[...] += part

            @pl.when(k == nk - 1)
            def _():
                o_ref[...] = (acc_ref[...] + part).astype(o_ref.dtype)
        finish()

    a_spec = pl.BlockSpec((tk, tm), lambda i, j, k: (k, i)) if mode == "tn" else pl.BlockSpec((tm, tk), lambda i, j, k: (i, k))
    b_spec = pl.BlockSpec((tn, tk), lambda i, j, k: (j, k)) if mode == "nt" else pl.BlockSpec((tk, tn), lambda i, j, k: (k, j))
    any_spec = pl.BlockSpec(memory_space=pl.ANY)
    outs = pl.pallas_call(
        body, name=name, grid=grid,
        in_specs=[a_spec, b_spec] + [any_spec] * n_x,
        out_specs=[pl.BlockSpec((tm, tn), lambda i, j, k: (i, j))] + [any_spec] * n_x,
        out_shape=[jax.ShapeDtypeStruct((M, N), out_dtype)] + _exchange_shapes(scatter, False),
        scratch_shapes=[pltpu.VMEM((tm, tn) if nk > 1 else (SUBLANES, LANES), F32)] + (_exchange_scratch(n_x) if n_x else []),
        compiler_params=_cparams(("arbitrary",) * 3 if n_x else ("parallel", "parallel", "arbitrary")),
    )(a, b, *scatter)
    return (outs[0], outs[1:]) if n_x else outs[0]


def _rows(tm, w, col=0):
    return pl.BlockSpec((tm, w), lambda i: (i, col))


def _full(shape):
    return pl.BlockSpec(shape, lambda i: (0,) * len(shape))


def _prev8(tm, w, col=0):
    return pl.BlockSpec((SUBLANES, w), lambda i: (jnp.maximum(i * (tm // SUBLANES) - 1, 0), col))


def _next8(tm, w, n_rows, col=0):
    last = n_rows // SUBLANES - 1
    return pl.BlockSpec((SUBLANES, w), lambda i: (jnp.minimum((i + 1) * (tm // SUBLANES), last), col))


def _shift_down(x, halo, k, first):
    rolled = pltpu.roll(x, k, 0)
    row = lax.broadcasted_iota(jnp.int32, x.shape, 0)
    out = rolled
    for j in range(k):
        h = jnp.where(first, 0.0, halo[SUBLANES - k + j:SUBLANES - k + j + 1, :])
        out = jnp.where(row == j, h, out)
    return out


def _shift_up(x, halo, k, last):
    n = x.shape[0]
    rolled = pltpu.roll(x, n - k, 0)
    row = lax.broadcasted_iota(jnp.int32, x.shape, 0)
    out = rolled
    for j in range(k):
        h = jnp.where(last, 0.0, halo[j:j + 1, :])
        out = jnp.where(row == n - k + j, h, out)
    return out


def _acc(ref, val, first):
    @pl.when(first)
    def _():
        ref[...] = val

    @pl.when(jnp.logical_not(first))
    def _():
        ref[...] += val


def _colsum(x):
    return jnp.sum(x, axis=0, keepdims=True)


def _norm_fwd(x, mo, gt, nw, sc, sh, name, tm=256):
    S = x.shape[0]
    has_res = mo is not None

    def body(*refs):
        if has_res:
            x_ref, mo_ref, gt_ref, nw_ref, sc_ref, sh_ref, x2_ref, h_ref, rs_ref = refs
            x2 = x_ref[...] + gt_ref[...] * mo_ref[...]
            x2_ref[...] = x2
        else:
            x_ref, nw_ref, sc_ref, sh_ref, h_ref, rs_ref = refs
            x2 = x_ref[...]
        rstd = lax.rsqrt(jnp.mean(x2 * x2, axis=-1, keepdims=True) + RMS_EPS)
        rs_ref[...] = rstd
        h_ref[...] = ((x2 * rstd * nw_ref[...]) * (1.0 + sc_ref[...]) + sh_ref[...]).astype(BF16)

    vec = _full((1, D))
    ins = [x, mo, gt, nw, sc, sh] if has_res else [x, nw, sc, sh]
    in_specs = [_rows(tm, D), _rows(tm, D), vec, vec, vec, vec] if has_res else [_rows(tm, D), vec, vec, vec]
    outs = [jax.ShapeDtypeStruct((S, D), BF16), jax.ShapeDtypeStruct((S, 1), F32)]
    out_specs = [_rows(tm, D), _rows(tm, 1)]
    if has_res:
        outs = [jax.ShapeDtypeStruct((S, D), F32)] + outs
        out_specs = [_rows(tm, D)] + out_specs
    return pl.pallas_call(body, name=name, grid=(S // tm,), in_specs=in_specs, out_specs=out_specs,
                          out_shape=outs, compiler_params=_cparams(("parallel",)))(*ins)


def _norm_bwd(dh, xin, rstd, nw, sc, dres, mo, gt, name, tm=256):
    S = xin.shape[0]
    has_res = mo is not None

    def body(*refs):
        if has_res:
            dh_ref, x_ref, rs_ref, nw_ref, sc_ref, dres_ref, mo_ref, gt_ref, dx_ref, dsh_ref, dsc_ref, dnw_ref, dmo_ref, dgt_ref = refs
        else:
            dh_ref, x_ref, rs_ref, nw_ref, sc_ref, dres_ref, dx_ref, dsh_ref, dsc_ref, dnw_ref = refs
        first = pl.program_id(0) == 0
        dh = dh_ref[...]
        rstd = rs_ref[...]
        n = x_ref[...] * rstd
        w = nw_ref[...]
        _acc(dsh_ref, _colsum(dh), first)
        _acc(dsc_ref, _colsum(dh * (n * w)), first)
        dnw = dh * (1.0 + sc_ref[...])
        _acc(dnw_ref, _colsum(dnw * n), first)
        dn = dnw * w
        dx = dres_ref[...] + rstd * (dn - n * jnp.mean(dn * n, axis=-1, keepdims=True))
        dx_ref[...] = dx
        if has_res:
            dmo_ref[...] = (dx * gt_ref[...]).astype(BF16)
            _acc(dgt_ref, _colsum(dx * mo_ref[...]), first)

    vec = _full((1, D))
    vshape = jax.ShapeDtypeStruct((1, D), F32)
    ins = [dh, xin, rstd, nw, sc, dres] + ([mo, gt] if has_res else [])
    in_specs = [_rows(tm, D), _rows(tm, D), _rows(tm, 1), vec, vec, _rows(tm, D)] + ([_rows(tm, D), vec] if has_res else [])
    outs = [jax.ShapeDtypeStruct((S, D), F32), vshape, vshape, vshape]
    out_specs = [_rows(tm, D), vec, vec, vec]
    if has_res:
        outs += [jax.ShapeDtypeStruct((S, D), BF16), vshape]
        out_specs += [_rows(tm, D), vec]
    return pl.pallas_call(body, name=name, grid=(S // tm,), in_specs=in_specs, out_specs=out_specs,
                          out_shape=outs, compiler_params=_cparams(("arbitrary",)))(*ins)


def _final(x2, f, gt2, nfw, target, tm=256):
    S = x2.shape[0]

    def body(x2_ref, f_ref, gt_ref, w_ref, t_ref, loss_ref, dx_ref, df_ref, dgt_ref, dw_ref):
        first = pl.program_id(0) == 0
        f = f_ref[...]
        gt = gt_ref[...]
        w = w_ref[...]
        x3 = x2_ref[...] + gt * f
        rstd = lax.rsqrt(jnp.mean(x3 * x3, axis=-1, keepdims=True) + RMS_EPS)
        n = x3 * rstd
        e = n * w - t_ref[...]
        part = 0.5 * jnp.sum(jnp.mean(e * e, axis=-1, keepdims=True), axis=0, keepdims=True)
        _acc(loss_ref, jnp.broadcast_to(part, (SUBLANES, LANES)), first)
        dy = e * (1.0 / D)
        _acc(dw_ref, _colsum(dy * n), first)
        dn = dy * w
        dx = rstd * (dn - n * jnp.mean(dn * n, axis=-1, keepdims=True))
        dx_ref[...] = dx
        df_ref[...] = (dx * gt).astype(BF16)
        _acc(dgt_ref, _colsum(dx * f), first)

    vec = _full((1, D))
    vshape = jax.ShapeDtypeStruct((1, D), F32)
    return pl.pallas_call(
        body, name="final_loss", grid=(S // tm,),
        in_specs=[_rows(tm, D), _rows(tm, D), vec, vec, _rows(tm, D)],
        out_specs=[_full((SUBLANES, LANES)), _rows(tm, D), _rows(tm, D), vec, vec],
        out_shape=[jax.ShapeDtypeStruct((SUBLANES, LANES), F32), jax.ShapeDtypeStruct((S, D), F32),
                   jax.ShapeDtypeStruct((S, D), BF16), vshape, vshape],
        compiler_params=_cparams(("arbitrary",)))(x2, f, gt2, nfw, target)


def _gate_fwd(P, bga, bgr, y_att, y_rwkv, tm=256):
    S = P.shape[0]

    def body(pa_ref, pr_ref, ba_ref, br_ref, ya_ref, yr_ref, mix_ref):
        ga = jax.nn.sigmoid(pa_ref[...] + ba_ref[...])
        gr = jax.nn.sigmoid(pr_ref[...] + br_ref[...])
        mix_ref[...] = (ga * ya_ref[...] + gr * yr_ref[...]).astype(BF16)

    vec = _full((1, D))
    return pl.pallas_call(
        body, name="gate_fwd", grid=(S // tm,),
        in_specs=[_rows(tm, D, C_GA // D), _rows(tm, D, C_GR // D), vec, vec, _rows(tm, D), _rows(tm, D)],
        out_specs=_rows(tm, D), out_shape=jax.ShapeDtypeStruct((S, D), BF16),
        compiler_params=_cparams(("parallel",)))(P, P, bga, bgr, y_att, y_rwkv)


def _gate_bwd(dmix, P, bga, bgr, y_att, y_rwkv, tm=256):
    S = P.shape[0]

    def body(dm_ref, pa_ref, pr_ref, ba_ref, br_ref, ya_ref, yr_ref, dya_ref, dyr_ref, dpa_ref, dpr_ref, dba_ref, dbr_ref):
        first = pl.program_id(0) == 0
        dm = dm_ref[...]
        ga = jax.nn.sigmoid(pa_ref[...] + ba_ref[...])
        gr = jax.nn.sigmoid(pr_ref[...] + br_ref[...])
        dya_ref[...] = (dm * ga).astype(BF16)
        dyr_ref[...] = (dm * gr).astype(BF16)
        dpa = dm * ya_ref[...] * ga * (1.0 - ga)
        dpr = dm * yr_ref[...] * gr * (1.0 - gr)
        dpa_ref[...] = dpa.astype(BF16)
        dpr_ref[...] = dpr.astype(BF16)
        _acc(dba_ref, _colsum(dpa), first)
        _acc(dbr_ref, _colsum(dpr), first)

    vec = _full((1, D))
    row = _rows(tm, D)
    rshape = jax.ShapeDtypeStruct((S, D), BF16)
    vshape = jax.ShapeDtypeStruct((1, D), F32)
    return pl.pallas_call(
        body, name="gate_bwd", grid=(S // tm,),
        in_specs=[row, _rows(tm, D, C_GA // D), _rows(tm, D, C_GR // D), vec, vec, row, row],
        out_specs=[row, row, row, row, vec, vec],
        out_shape=[rshape, rshape, rshape, rshape, vshape, vshape],
        compiler_params=_cparams(("arbitrary",)))(dmix, P, P, bga, bgr, y_att, y_rwkv)


CONV_TN = D_FF // 2


def _conv_fwd(u, conv_w8, conv_b, tm=256, tn=CONV_TN):
    S = u.shape[0]
    nj = D_FF // tn

    def conv(u_ref, h_ref, w_ref, b_ref, first):
        u = u_ref[...]
        h = h_ref[...]
        w = w_ref[...]
        return b_ref[...] + w[0:1] * _shift_down(u, h, 2, first) + w[1:2] * _shift_down(u, h, 1, first) + w[2:3] * u

    def body(ug_ref, hg_ref, uv_ref, hv_ref, wg_ref, wv_ref, bg_ref, bv_ref, act_ref):
        first = pl.program_id(0) == 0
        g = conv(ug_ref, hg_ref, wg_ref, bg_ref, first)
        v = conv(uv_ref, hv_ref, wv_ref, bv_ref, first)
        act_ref[...] = (g * jax.nn.sigmoid(g) * v).astype(BF16)

    blk = lambda off: pl.BlockSpec((tm, tn), lambda i, j: (i, j + off))
    halo = lambda off: pl.BlockSpec((SUBLANES, tn), lambda i, j: (jnp.maximum(i * (tm // SUBLANES) - 1, 0), j + off))
    wsp = lambda off: pl.BlockSpec((SUBLANES, tn), lambda i, j: (0, j + off))
    bsp = lambda off: pl.BlockSpec((1, tn), lambda i, j: (0, j + off))
    return pl.pallas_call(
        body, name="conv_fwd", grid=(S // tm, nj),
        in_specs=[blk(0), halo(0), blk(nj), halo(nj), wsp(0), wsp(nj), bsp(0), bsp(nj)],
        out_specs=pl.BlockSpec((tm, tn), lambda i, j: (i, j)),
        out_shape=jax.ShapeDtypeStruct((S, D_FF), BF16),
        compiler_params=_cparams(("parallel", "parallel")))(u, u, u, u, conv_w8, conv_w8, conv_b, conv_b)


def _conv_bwd_a(dact, u, conv_w8, conv_b, tm=256, tn=CONV_TN):
    S = u.shape[0]
    nj = D_FF // tn

    def half(u_ref, h_ref, w_ref, b_ref, first):
        u = u_ref[...]
        h = h_ref[...]
        w = w_ref[...]
        u2, u1 = _shift_down(u, h, 2, first), _shift_down(u, h, 1, first)
        return b_ref[...] + w[0:1] * u2 + w[1:2] * u1 + w[2:3] * u, (u2, u1, u)

    def wgrad(d, taps):
        z = jnp.zeros((SUBLANES - 3, d.shape[1]), F32)
        return jnp.concatenate([_colsum(d * taps[0]), _colsum(d * taps[1]), _colsum(d * taps[2]), z], axis=0)

    def body(da_ref, ug_ref, hg_ref, uv_ref, hv_ref, wg_ref, wv_ref, bg_ref, bv_ref,
             d_ref, dwg_ref, dwv_ref, dbg_ref, dbv_ref):
        first = pl.program_id(1) == 0
        g, tg = half(ug_ref, hg_ref, wg_ref, bg_ref, first)
        v, tv = half(uv_ref, hv_ref, wv_ref, bv_ref, first)
        da = da_ref[...].astype(F32)
        sg = jax.nn.sigmoid(g)
        dg = da * v * (sg * (1.0 + g * (1.0 - sg)))
        dv = da * (g * sg)
        d_ref[0] = dg
        d_ref[1] = dv
        _acc(dwg_ref, wgrad(dg, tg), first)
        _acc(dwv_ref, wgrad(dv, tv), first)
        _acc(dbg_ref, _colsum(dg), first)
        _acc(dbv_ref, _colsum(dv), first)

    blk = lambda off: pl.BlockSpec((tm, tn), lambda j, i: (i, j + off))
    halo = lambda off: pl.BlockSpec((SUBLANES, tn), lambda j, i: (jnp.maximum(i * (tm // SUBLANES) - 1, 0), j + off))
    wsp = lambda off: pl.BlockSpec((SUBLANES, tn), lambda j, i: (0, j + off))
    bsp = lambda off: pl.BlockSpec((1, tn), lambda j, i: (0, j + off))
    f = jax.ShapeDtypeStruct
    outs = pl.pallas_call(
        body, name="conv_bwd_a", grid=(nj, S // tm),
        in_specs=[pl.BlockSpec((tm, tn), lambda j, i: (i, j)), blk(0), halo(0), blk(nj), halo(nj), wsp(0), wsp(nj), bsp(0), bsp(nj)],
        out_specs=[pl.BlockSpec((2, tm, tn), lambda j, i: (0, i, j)),
                   pl.BlockSpec((SUBLANES, tn), lambda j, i: (0, j)), pl.BlockSpec((SUBLANES, tn), lambda j, i: (0, j)),
                   pl.BlockSpec((1, tn), lambda j, i: (0, j)), pl.BlockSpec((1, tn), lambda j, i: (0, j))],
        out_shape=[f((2, S, D_FF), F32), f((SUBLANES, D_FF), F32), f((SUBLANES, D_FF), F32),
                   f((1, D_FF), F32), f((1, D_FF), F32)],
        compiler_params=_cparams(("parallel", "arbitrary")))(dact, u, u, u, u, conv_w8, conv_w8, conv_b, conv_b)
    return outs


def _conv_bwd_b(duc, conv_w8, tm=256, tn=CONV_TN):
    _, S, W = duc.shape
    nj = W // tn
    n_rows = S // tm

    def body(d_ref, h_ref, w_ref, o_ref):
        last = pl.program_id(0) == n_rows - 1
        d = d_ref[...]
        h = h_ref[...]
        w = w_ref[...]
        o_ref[...] = (w[2:3] * d + w[1:2] * _shift_up(d, h, 1, last) + w[0:1] * _shift_up(d, h, 2, last)).astype(BF16)

    last_tile = S // SUBLANES - 1
    return pl.pallas_call(
        body, name="conv_bwd_b", grid=(n_rows, 2 * nj),
        in_specs=[pl.BlockSpec((None, tm, tn), lambda i, j: (j // nj, i, j % nj)),
                  pl.BlockSpec((None, SUBLANES, tn), lambda i, j: (j // nj, jnp.minimum((i + 1) * (tm // SUBLANES), last_tile), j % nj)),
                  pl.BlockSpec((SUBLANES, tn), lambda i, j: (0, j))],
        out_specs=pl.BlockSpec((tm, tn), lambda i, j: (i, j)),
        out_shape=jax.ShapeDtypeStruct((S, 2 * W), BF16),
        compiler_params=_cparams(("parallel", "parallel")))(duc, duc, conv_w8)


ATT_SCALE = HEAD ** -0.5
NEG = -1e30
ATT_PAIRS = ATT_HEADS // 2


def _att_rows(n, d, S):
    per = S // (QBLK * d)
    r, m = n // per, n % per
    cur = pl.ds(m * (QBLK * d) + r, QBLK, stride=d)
    prv = pl.ds(jnp.maximum(m - 1, 0) * (QBLK * d) + r, QBLK, stride=d)
    return cur, prv, m > 0


def _att_slab(g, j):
    return (C_ATT + g * 3 * ATT_W + j * ATT_W) // LANES


def _heads(x):
    return x[:, 0:HEAD], x[:, HEAD:2 * HEAD]


def _att_scores(q, kc, kp, has_prev):
    qi = lax.broadcasted_iota(jnp.int32, (QBLK, QBLK), 0)
    kj = lax.broadcasted_iota(jnp.int32, (QBLK, QBLK), 1)
    nt = (((1,), (1,)), ((), ()))
    s_c = lax.dot_general(q, kc, nt, preferred_element_type=F32) * ATT_SCALE
    s_p = lax.dot_general(q, kp, nt, preferred_element_type=F32) * ATT_SCALE
    s_c = jnp.where(kj <= qi, s_c, NEG)
    s_p = jnp.where(jnp.logical_and(kj >= qi, has_prev), s_p, NEG)
    return s_c, s_p


def _att_fwd(P, g):
    S = P.shape[0]
    d = ATT_PATTERNS[g][1]

    def body(q_ref, k_ref, v_ref, o_ref, l_ref):
        def blk(n, carry):
            cur, prv, has_prev = _att_rows(n, d, S)
            q2, kc2, kp2 = q_ref[cur, :].astype(BF16), k_ref[cur, :].astype(BF16), k_ref[prv, :].astype(BF16)
            vc2, vp2 = v_ref[cur, :].astype(BF16), v_ref[prv, :].astype(BF16)
            outs, lses = [], []
            for q, kc, kp, vc, vp in zip(_heads(q2), _heads(kc2), _heads(kp2), _heads(vc2), _heads(vp2)):
                s_c, s_p = _att_scores(q, kc, kp, has_prev)
                m = jnp.maximum(jnp.max(s_c, axis=1, keepdims=True), jnp.max(s_p, axis=1, keepdims=True))
                p_c = jnp.exp(s_c - m)
                p_p = jnp.exp(s_p - m)
                den = jnp.sum(p_c, axis=1, keepdims=True) + jnp.sum(p_p, axis=1, keepdims=True)
                num = (jnp.dot(p_c.astype(BF16), vc, preferred_element_type=F32)
                       + jnp.dot(p_p.astype(BF16), vp, preferred_element_type=F32))
                outs.append(num / den)
                lses.append(jnp.broadcast_to(m + jnp.log(den), (QBLK, HEAD)))
            o_ref[cur, :] = jnp.concatenate(outs, axis=1)
            l_ref[cur, :] = jnp.concatenate(lses, axis=1)
            return carry

        lax.fori_loop(0, S // QBLK, blk, 0, unroll=2)

    slab = lambda j: pl.BlockSpec((S, LANES), lambda i: (0, _att_slab(g, j) + i))
    out = pl.BlockSpec((S, LANES), lambda i: (0, i))
    shp = jax.ShapeDtypeStruct((S, ATT_W), F32)
    return pl.pallas_call(body, name=f"att_fwd_g{g}", grid=(ATT_PAIRS,), in_specs=[slab(0), slab(1), slab(2)],
                          out_specs=[out, out], out_shape=[shp, shp], compiler_params=_cparams(("parallel",)))(P, P, P)


def _att_bwd(P, o, l, do, dl, g):
    S = P.shape[0]
    d = ATT_PATTERNS[g][1]
    tn = (((0,), (0,)), ((), ()))
    nt = (((1,), (1,)), ((), ()))

    def body(q_ref, k_ref, v_ref, o_ref, l_ref, do_ref, dl_ref, dq_ref, dk_ref, dv_ref, dq_acc, dk_acc, dv_acc):
        dk_acc[...] = jnp.zeros_like(dk_acc)
        dv_acc[...] = jnp.zeros_like(dv_acc)

        def blk(n, carry):
            cur, prv, has_prev = _att_rows(n, d, S)
            q2, kc2, kp2 = q_ref[cur, :].astype(BF16), k_ref[cur, :].astype(BF16), k_ref[prv, :].astype(BF16)
            vc2, vp2 = v_ref[cur, :].astype(BF16), v_ref[prv, :].astype(BF16)
            do2 = do_ref[cur, :]
            dd2 = do2 * o_ref[cur, :] - dl_ref[cur, :]
            l2 = l_ref[cur, :]
            res = []
            for q, kc, kp, vc, vp, dob, dd, lse in zip(_heads(q2), _heads(kc2), _heads(kp2), _heads(vc2), _heads(vp2),
                                                     _heads(do2), _heads(dd2), _heads(l2)):
                s_c, s_p = _att_scores(q, kc, kp, has_prev)
                p_c = jnp.exp(s_c - lse[:, 0:1])
                p_p = jnp.exp(s_p - lse[:, 0:1])
                delta = jnp.sum(dd, axis=1, keepdims=True)
                dob16 = dob.astype(BF16)
                dp_c = lax.dot_general(dob16, vc, nt, preferred_element_type=F32)
                dp_p = lax.dot_general(dob16, vp, nt, preferred_element_type=F32)
                ds_c = (p_c * (dp_c - delta) * ATT_SCALE).astype(BF16)
                ds_p = (p_p * (dp_p - delta) * ATT_SCALE).astype(BF16)
                res.append((
                    jnp.dot(ds_c, kc, preferred_element_type=F32) + jnp.dot(ds_p, kp, preferred_element_type=F32),
                    lax.dot_general(ds_c, q, tn, preferred_element_type=F32),
                    lax.dot_general(ds_p, q, tn, preferred_element_type=F32),
                    lax.dot_general(p_c.astype(BF16), dob16, tn, preferred_element_type=F32),
                    lax.dot_general(p_p.astype(BF16), dob16, tn, preferred_element_type=F32)))
            both = [jnp.concatenate([res[0][i], res[1][i]], axis=1) for i in range(5)]
            dq_acc[cur, :] = both[0]
            dk_acc[cur, :] += both[1]
            dv_acc[cur, :] += both[3]
            dk_acc[prv, :] += both[2]
            dv_acc[prv, :] += both[4]
            return carry

        lax.fori_loop(0, S // QBLK, blk, 0, unroll=2)
        dq_ref[...] = dq_acc[...].astype(BF16)
        dk_ref[...] = dk_acc[...].astype(BF16)
        dv_ref[...] = dv_acc[...].astype(BF16)

    slab = lambda j: pl.BlockSpec((S, LANES), lambda i: (0, _att_slab(g, j) + i))
    blk128 = pl.BlockSpec((S, LANES), lambda i: (0, i))
    shp = jax.ShapeDtypeStruct((S, ATT_W), BF16)
    return pl.pallas_call(body, name=f"att_bwd_g{g}", grid=(ATT_PAIRS,),
                          in_specs=[slab(0), slab(1), slab(2)] + [blk128] * 4, out_specs=[blk128] * 3, out_shape=[shp] * 3,
                          scratch_shapes=[pltpu.VMEM((S, LANES), F32)] * 3,
                          compiler_params=_cparams(("parallel",)))(P, P, P, o, l, do, dl)


def _att_weights(l_refs):
    l0, l1, l2 = [r[...] for r in l_refs]
    m = jnp.maximum(jnp.maximum(l0, l1), l2)
    e = (jnp.exp(l0 - m), jnp.exp(l1 - m), jnp.exp(l2 - m))
    inv = 1.0 / (e[0] + e[1] + e[2])
    return [x * inv for x in e]


def _att_combine_fwd(os, ls, tm=512):
    S = os[0].shape[0]

    def body(o0, o1, o2, l0, l1, l2, a_ref):
        w = _att_weights((l0, l1, l2))
        a_ref[...] = (w[0] * o0[...] + w[1] * o1[...] + w[2] * o2[...]).astype(BF16)

    row = _rows(tm, ATT_W)
    return pl.pallas_call(body, name="att_combine_fwd", grid=(S // tm,), in_specs=[row] * 6, out_specs=row,
                          out_shape=jax.ShapeDtypeStruct((S, ATT_W), BF16),
                          compiler_params=_cparams(("parallel",)))(*os, *ls)


def _att_combine_bwd(da, os, ls, tm=512):
    S = da.shape[0]

    def body(da_ref, o0, o1, o2, l0, l1, l2, *out_refs):
        da = da_ref[...]
        w = _att_weights((l0, l1, l2))
        dw = (da * o0[...], da * o1[...], da * o2[...])
        mean = w[0] * dw[0] + w[1] * dw[1] + w[2] * dw[2]
        for g in range(3):
            out_refs[g][...] = w[g] * da
            out_refs[3 + g][...] = w[g] * (dw[g] - mean)

    row = _rows(tm, ATT_W)
    shp = jax.ShapeDtypeStruct((S, ATT_W), F32)
    return pl.pallas_call(body, name="att_combine_bwd", grid=(S // tm,), in_specs=[row] * 7, out_specs=[row] * 6,
                          out_shape=[shp] * 6, compiler_params=_cparams(("parallel",)))(da, *os, *ls)


@jax.custom_vjp
def _bdot(a, b):
    return jnp.dot(a.astype(BF16), b.astype(BF16), preferred_element_type=F32)


def _bdot_fwd(a, b):
    return _bdot(a, b), (a, b)


def _bdot_bwd(res, ct):
    a, b = res
    ct16 = ct.astype(BF16)
    da = lax.dot_general(ct16, b.astype(BF16), (((1,), (1,)), ((), ())), preferred_element_type=F32)
    db = lax.dot_general(a.astype(BF16), ct16, (((0,), (0,)), ((), ())), preferred_element_type=F32)
    return da, db


_bdot.defvjp(_bdot_fwd, _bdot_bwd)


def _two_piece_dot(x, m):
    hi = x.astype(BF16)
    lo = (x - hi.astype(F32)).astype(BF16)
    return jnp.dot(hi, m, preferred_element_type=F32) + jnp.dot(lo, m, preferred_element_type=F32)


def _head_sum_impl(x):
    sel = (lax.broadcasted_iota(jnp.int32, (D, LANES), 0) // HEAD == lax.broadcasted_iota(jnp.int32, (D, LANES), 1)).astype(BF16)
    sel_t = (lax.broadcasted_iota(jnp.int32, (LANES, D), 1) // HEAD == lax.broadcasted_iota(jnp.int32, (LANES, D), 0)).astype(BF16)
    return _two_piece_dot(_two_piece_dot(x, sel), sel_t)


@jax.custom_vjp
def _head_sum(x):
    return _head_sum_impl(x)


_head_sum.defvjp(lambda x: (_head_sum_impl(x), None), lambda _, ct: (_head_sum_impl(ct),))


def _softplus(z):
    return jnp.maximum(z, 0.0) + jnp.log(1.0 + jnp.exp(-jnp.abs(z)))


def _rwkv_prep_fn(zr, zrp, zk, zkp, zv, zvp, zl, zlp, mu_r, mu_k, mu_v, mu_l, w0, a0, k_k, k_a, w2, a2, g2p):
    r = zr + (zrp - zr) * mu_r
    k = zk + (zkp - zk) * mu_k
    v = zv + (zvp - zv) * mu_v
    lo = zl + (zlp - zl) * mu_l
    w_low, a_low, g_low = lo[:, 0:LORA_W], lo[:, LORA_W:LORA_W + LORA_A], lo[:, LANES:LANES + G_PAD]
    w_log = -_softplus(-(w0 + _bdot(jnp.tanh(w_low), w2))) - 0.5
    decay = -jnp.exp(w_log)
    a = jax.nn.sigmoid(a0 + _bdot(a_low, a2))
    g = _bdot(jax.nn.sigmoid(g_low), g2p)
    kmod = k * (1.0 + (a - 1.0) * k_a)
    kk = k * k_k
    kk = kk / jnp.maximum(jnp.sqrt(_head_sum(kk * kk)), 1e-12)
    return r, decay, kmod, v, -kk, kk * a, g


def _rwkv_prep_specs(tm):
    vec = _full((1, D))
    slabs = []
    for col in (C_R // D, C_K // D, C_V // D):
        slabs += [_rows(tm, D, col), _prev8(tm, D, col)]
    slabs += [_rows(tm, LORA_PAD, C_LORA // LORA_PAD), _prev8(tm, LORA_PAD, C_LORA // LORA_PAD)]
    params = [vec, vec, vec, _full((1, LORA_PAD)), vec, vec, vec, vec,
              _full((LORA_W, D)), _full((LORA_A, D)), _full((G_PAD, D))]
    return slabs, params


def _prep_inputs(refs, first):
    vals = []
    for s in range(4):
        z = refs[2 * s][...]
        vals += [z, _shift_down(z, refs[2 * s + 1][...], 1, first)]
    return vals + [r[...] for r in refs[8:19]]


def _rwkv_prep(P, params, tm=256):
    S = P.shape[0]
    slabs, pspecs = _rwkv_prep_specs(tm)

    def body(*refs):
        outs = _rwkv_prep_fn(*_prep_inputs(refs, pl.program_id(0) == 0))
        for o_ref, val in zip(refs[19:], outs):
            o_ref[...] = val

    shp = jax.ShapeDtypeStruct((S, D), F32)
    return pl.pallas_call(body, name="rwkv_prep", grid=(S // tm,), in_specs=slabs + pspecs,
                          out_specs=[_rows(tm, D)] * 7, out_shape=[shp] * 7,
                          compiler_params=_cparams(("parallel",)))(*([P] * 8), *params)


def _rwkv_prep_bwd(P, params, cts_a, cts_b, tm=128):
    S = P.shape[0]
    slabs, pspecs = _rwkv_prep_specs(tm)
    has_b = [c is not None for c in cts_b]
    n_ct = 7 + sum(has_b)

    def body(*refs):
        first = pl.program_id(0) == 0
        ins = _prep_inputs(refs, first)
        ct_refs = refs[19:19 + n_ct]
        out_refs = refs[19 + n_ct:]
        cts, pos = [], 7
        for i in range(7):
            c = ct_refs[i][...]
            if has_b[i]:
                c = c + ct_refs[pos][...]
                pos += 1
            cts.append(c)
        _, vjp = jax.vjp(_rwkv_prep_fn, *ins)
        grads = vjp(tuple(cts))
        for s in range(4):
            out_refs[s][...] = grads[2 * s]
            out_refs[4 + s][...] = grads[2 * s + 1]
        for i in range(11):
            _acc(out_refs[8 + i], grads[8 + i], first)

    ct_in = list(cts_a) + [c for c in cts_b if c is not None]
    row, lrow = _rows(tm, D), _rows(tm, LORA_PAD)
    f = jax.ShapeDtypeStruct
    zshapes = [f((S, D), F32)] * 3 + [f((S, LORA_PAD), F32)]
    pshapes = [f((1, D), F32)] * 3 + [f((1, LORA_PAD), F32)] + [f((1, D), F32)] * 4 + [f((LORA_W, D), F32), f((LORA_A, D), F32), f((G_PAD, D), F32)]
    return pl.pallas_call(
        body, name="rwkv_prep_bwd", grid=(S // tm,),
        in_specs=slabs + pspecs + [row] * n_ct,
        out_specs=[row, row, row, lrow] * 2 + pspecs,
        out_shape=zshapes * 2 + pshapes,
        compiler_params=_cparams(("arbitrary",)))(*([P] * 8), *params, *ct_in)


def _shift_add(a, b, tm=256):
    S, W = a.shape

    def body(a_ref, b_ref, h_ref, o_ref):
        last = pl.program_id(0) == pl.num_programs(0) - 1
        o_ref[...] = (a_ref[...] + _shift_up(b_ref[...], h_ref[...], 1, last)).astype(BF16)

    return pl.pallas_call(body, name="shift_add", grid=(S // tm,),
                          in_specs=[_rows(tm, W), _rows(tm, W), _next8(tm, W, S)],
                          out_specs=_rows(tm, W), out_shape=jax.ShapeDtypeStruct((S, W), BF16),
                          compiler_params=_cparams(("parallel",)))(a, b, b)


def _rwkv_post_fn(y, r, kmod, v, g, lnx_w, lnx_b, r_k):
    mean = _head_sum(y) * (1.0 / HEAD)
    yc = y - mean
    var = _head_sum(yc * yc) * (1.0 / HEAD)
    yn = yc * lax.rsqrt(var + GN_EPS) * lnx_w + lnx_b
    bonus = _head_sum(r * kmod * r_k) * v
    return (yn + bonus) * g


def _rwkv_post(y, r, kmod, v, g, lnx_w, lnx_b, r_k, tm=256):
    S = y.shape[0]

    def body(y_ref, r_ref, k_ref, v_ref, g_ref, w_ref, b_ref, rk_ref, o_ref):
        o_ref[...] = _rwkv_post_fn(y_ref[...], r_ref[...], k_ref[...], v_ref[...], g_ref[...],
                                   w_ref[...], b_ref[...], rk_ref[...]).astype(BF16)

    row, vec = _rows(tm, D), _full((1, D))
    return pl.pallas_call(body, name="rwkv_post", grid=(S // tm,), in_specs=[row] * 5 + [vec] * 3, out_specs=row,
                          out_shape=jax.ShapeDtypeStruct((S, D), BF16),
                          compiler_params=_cparams(("parallel",)))(y, r, kmod, v, g, lnx_w, lnx_b, r_k)


def _rwkv_post_bwd(drw, y, r, kmod, v, g, lnx_w, lnx_b, r_k, tm=256):
    S = y.shape[0]

    def body(d_ref, y_ref, r_ref, k_ref, v_ref, g_ref, w_ref, b_ref, rk_ref, *out_refs):
        first = pl.program_id(0) == 0
        _, vjp = jax.vjp(_rwkv_post_fn, y_ref[...], r_ref[...], k_ref[...], v_ref[...], g_ref[...],
                         w_ref[...], b_ref[...], rk_ref[...])
        grads = vjp(d_ref[...])
        for i in range(5):
            out_refs[i][...] = grads[i]
        for i in range(5, 8):
            _acc(out_refs[i], grads[i], first)

    row, vec = _rows(tm, D), _full((1, D))
    f = jax.ShapeDtypeStruct
    return pl.pallas_call(body, name="rwkv_post_bwd", grid=(S // tm,), in_specs=[row] * 6 + [vec] * 3,
                          out_specs=[row] * 5 + [vec] * 3, out_shape=[f((S, D), F32)] * 5 + [f((1, D), F32)] * 3,
                          compiler_params=_cparams(("arbitrary",)))(drw, y, r, kmod, v, g, lnx_w, lnx_b, r_k)


CHUNK = 64
CHUNK_TB = 256
_DOT_DIMS = {"nn": (((2,), (1,)), ((0,), (0,))), "nt": (((2,), (2,)), ((0,), (0,))), "tn": (((1,), (1,)), ((0,), (0,)))}


def _dot16(x, y, mode):
    return lax.dot_general(x.astype(BF16), y.astype(BF16), _DOT_DIMS[mode], preferred_element_type=F32)


@functools.partial(jax.custom_vjp, nondiff_argnums=(2,))
def _mm16(x, y, mode):
    return _dot16(x, y, mode)


def _mm16_fwd(x, y, mode):
    return _dot16(x, y, mode), (x, y)


def _mm16_bwd(mode, res, ct):
    x, y = res
    if mode == "nn":
        return _dot16(ct, y, "nt"), _dot16(x, ct, "tn")
    if mode == "nt":
        return _dot16(ct, y, "nn"), _dot16(ct, x, "tn")
    return _dot16(y, ct, "nt"), _dot16(x, ct, "nn")


_mm16.defvjp(_mm16_fwd, _mm16_bwd)


def _tri_sum(x, upper):
    T = x.shape[0]
    i = lax.broadcasted_iota(jnp.int32, (T, T), 0)
    j = lax.broadcasted_iota(jnp.int32, (T, T), 1)
    tri = ((j >= i) if upper else (i >= j)).astype(BF16)
    out, rest = None, x
    for _ in range(3):
        piece = rest.astype(BF16)
        rest = rest - piece.astype(F32)
        part = jnp.dot(tri, piece, preferred_element_type=F32)
        out = part if out is None else out + part
    return out


@jax.custom_vjp
def _cumsum_rows(x):
    return _tri_sum(x, False)


_cumsum_rows.defvjp(lambda x: (_tri_sum(x, False), None), lambda _, ct: (_tri_sum(ct, True),))


def _rows_to_cols(x):
    H, _, K = x.shape
    eye = (lax.broadcasted_iota(jnp.int32, (H, K, K), 1) == lax.broadcasted_iota(jnp.int32, (H, K, K), 2)).astype(F32)
    out = lax.dot_general(eye, jnp.broadcast_to(x, (H, SUBLANES, K)), _DOT_DIMS["nt"],
                          precision=lax.Precision.HIGHEST, preferred_element_type=F32)
    return out[:, :, 0:1]


def _per_head(x):
    return jnp.concatenate([x[:, h * HEAD:(h + 1) * HEAD][None] for h in range(N_HEADS)], axis=0)


def _chunk_fn(st0, r, lw, k, v, a, b):
    T = r.shape[0]
    cl = _cumsum_rows(lw)
    cl_end = cl[T - 1:T, :]
    inv = jnp.exp(-cl)
    to_end = jnp.exp(cl_end - cl)
    ah, rh, bh, kh, be, ke, v3 = [_per_head(x) for x in
                                  (a * jnp.exp(cl - lw), r * jnp.exp(cl), b * inv, k * inv, b * to_end, k * to_end, v)]
    i = lax.broadcasted_iota(jnp.int32, (N_HEADS, T, T), 1)
    j = lax.broadcasted_iota(jnp.int32, (N_HEADS, T, T), 2)
    a_ab = jnp.where(i > j, _mm16(ah, bh, "nt"), 0.0)
    a_ak = jnp.where(i > j, _mm16(ah, kh, "nt"), 0.0)
    m_rb = jnp.where(i >= j, _mm16(rh, bh, "nt"), 0.0)
    m_rk = jnp.where(i >= j, _mm16(rh, kh, "nt"), 0.0)
    rhs = _mm16(ah, st0, "nn") + _mm16(a_ak, v3, "nn")
    power, solve, n = a_ab, (i == j).astype(F32) + a_ab, 1
    while 2 * n < T:
        power = _mm16(power, power, "nn")
        solve = solve + _mm16(solve, power, "nn")
        n *= 2
    sa = _mm16(solve, rhs, "nn")
    y3 = _mm16(rh, st0, "nn") + _mm16(m_rb, sa, "nn") + _mm16(m_rk, v3, "nn")
    st_end = _rows_to_cols(_per_head(jnp.exp(cl_end))) * st0 + _mm16(be, sa, "tn") + _mm16(ke, v3, "tn")
    return jnp.concatenate([y3[h] for h in range(N_HEADS)], axis=1), st_end


def _hosted_exchange(refs, n, broadcast, grid):
    if n == 0:
        return lambda: None
    start, wait = _exchange_ops(refs[:n], refs[n:2 * n], *refs[2 * n:], broadcast)
    first = functools.reduce(jnp.logical_and, [pl.program_id(a) == 0 for a in range(len(grid))])
    last = functools.reduce(jnp.logical_and, [pl.program_id(a) == g - 1 for a, g in enumerate(grid)])
    pl.when(first)(start)
    return lambda: pl.when(last)(wait)


def _cscan_fwd(r, lw, k, v, a, b, gather=()):
    S = r.shape[0]
    per_blk = CHUNK_TB // CHUNK
    n_x = len(gather)
    nblk = S // CHUNK_TB

    def body(*refs):
        r_ref, lw_ref, k_ref, v_ref, a_ref, b_ref = refs[:6]
        y_ref, ck_ref = refs[6 + n_x:8 + n_x]
        st_ref = refs[8 + 2 * n_x]
        finish = _hosted_exchange(refs[6:6 + n_x] + refs[8 + n_x:8 + 2 * n_x] + refs[9 + 2 * n_x:], n_x, True, (nblk,))

        @pl.when(pl.program_id(0) == 0)
        def _():
            st_ref[...] = jnp.zeros_like(st_ref)

        def chunk(c, carry):
            rows = pl.ds(pl.multiple_of(c * CHUNK, CHUNK), CHUNK)
            st0 = st_ref[...]
            ck_ref[c] = st0
            y, st_end = _chunk_fn(st0, r_ref[rows, :], lw_ref[rows, :], k_ref[rows, :],
                                  v_ref[rows, :], a_ref[rows, :], b_ref[rows, :])
            y_ref[rows, :] = y
            st_ref[...] = st_end
            return carry

        lax.fori_loop(0, per_blk, chunk, 0)
        finish()

    blk = _rows(CHUNK_TB, D)
    any_spec = pl.BlockSpec(memory_space=pl.ANY)
    outs = pl.pallas_call(
        body, name="scan_fwd", grid=(nblk,), in_specs=[blk] * 6 + [any_spec] * n_x,
        out_specs=[blk, pl.BlockSpec((per_blk, N_HEADS, HEAD, HEAD), lambda i: (i, 0, 0, 0))] + [any_spec] * n_x,
        out_shape=[jax.ShapeDtypeStruct((S, D), F32), jax.ShapeDtypeStruct((S // CHUNK, N_HEADS, HEAD, HEAD), F32)]
        + _exchange_shapes(gather, True),
        scratch_shapes=[pltpu.VMEM((N_HEADS, HEAD, HEAD), F32)] + (_exchange_scratch(n_x) if n_x else []),
        compiler_params=_cparams(("arbitrary",)))(r, lw, k, v, a, b, *gather)
    return outs[0], outs[1], outs[2:]


def _cscan_bwd(r, lw, k, v, a, b, ckpt, dy, scatter=()):
    S = r.shape[0]
    per_blk = CHUNK_TB // CHUNK
    nblk = S // CHUNK_TB
    n_x = len(scatter)

    def body(*refs):
        r_ref, lw_ref, k_ref, v_ref, a_ref, b_ref, ck_ref, dy_ref = refs[:8]
        out_refs = refs[8 + n_x:14 + n_x]
        ds_ref = refs[14 + 2 * n_x]
        finish = _hosted_exchange(refs[8:8 + n_x] + refs[14 + n_x:14 + 2 * n_x] + refs[15 + 2 * n_x:], n_x, False, (nblk,))

        @pl.when(pl.program_id(0) == 0)
        def _():
            ds_ref[...] = jnp.zeros_like(ds_ref)

        def chunk(cc, carry):
            c = per_blk - 1 - cc
            rows = pl.ds(pl.multiple_of(c * CHUNK, CHUNK), CHUNK)
            ins = (ck_ref[c], r_ref[rows, :], lw_ref[rows, :], k_ref[rows, :], v_ref[rows, :], a_ref[rows, :], b_ref[rows, :])
            _, vjp = jax.vjp(_chunk_fn, *ins)
            grads = vjp((dy_ref[rows, :], ds_ref[...]))
            ds_ref[...] = grads[0]
            for o_ref, g in zip(out_refs, grads[1:]):
                o_ref[rows, :] = g
            return carry

        lax.fori_loop(0, per_blk, chunk, 0)
        finish()

    blk = pl.BlockSpec((CHUNK_TB, D), lambda i: (nblk - 1 - i, 0))
    any_spec = pl.BlockSpec(memory_space=pl.ANY)
    shp = jax.ShapeDtypeStruct((S, D), F32)
    outs = pl.pallas_call(
        body, name="scan_bwd", grid=(nblk,),
        in_specs=[blk] * 6 + [pl.BlockSpec((per_blk, N_HEADS, HEAD, HEAD), lambda i: (nblk - 1 - i, 0, 0, 0)), blk]
        + [any_spec] * n_x,
        out_specs=[blk] * 6 + [any_spec] * n_x, out_shape=[shp] * 6 + _exchange_shapes(scatter, False),
        scratch_shapes=[pltpu.VMEM((N_HEADS, HEAD, HEAD), F32)] + (_exchange_scratch(n_x) if n_x else []),
        compiler_params=_cparams(("arbitrary",)))(r, lw, k, v, a, b, ckpt, dy, *scatter)
    return outs[:6], outs[6:]


def _ada_partial(c_all, w_shard):
    def body(c_ref, w_ref, o_ref):
        o_ref[...] = jnp.dot(c_ref[...].astype(BF16), w_ref[...].astype(BF16), preferred_element_type=F32)

    vm = pl.BlockSpec(memory_space=pltpu.VMEM)
    return pl.pallas_call(body, name="ada_partial", in_specs=[vm, vm], out_specs=vm,
                          out_shape=jax.ShapeDtypeStruct((N_DEV, w_shard.shape[1]), F32),
                          compiler_params=pltpu.CompilerParams(vmem_limit_bytes=VMEM_LIMIT))(c_all, w_shard)


def _ada_bias(rows, b_ada):
    def body(r_ref, b_ref, o_ref):
        o_ref[...] = r_ref[...] + b_ref[...]

    vm = pl.BlockSpec(memory_space=pltpu.VMEM)
    return pl.pallas_call(body, name="ada_bias", in_specs=[vm, vm], out_specs=vm,
                          out_shape=jax.ShapeDtypeStruct(rows.shape, F32))(rows, b_ada)


def _ada_wgrad(c_cols, d_all):
    def body(c_ref, d_ref, o_ref):
        acc = c_ref[:, 0:1] * d_ref[0:1, :]
        for j in range(1, N_DEV):
            acc = acc + c_ref[:, j:j + 1] * d_ref[j:j + 1, :]
        o_ref[...] = acc

    vm = pl.BlockSpec(memory_space=pltpu.VMEM)
    return pl.pallas_call(body, name="ada_wgrad", in_specs=[vm, vm], out_specs=vm,
                          out_shape=jax.ShapeDtypeStruct((D, d_all.shape[1]), F32),
                          compiler_params=pltpu.CompilerParams(vmem_limit_bytes=VMEM_LIMIT))(c_cols, d_all)


def _exchange(srcs, broadcast, name):
    n = len(srcs)

    def body(*refs):
        start, wait = _exchange_ops(refs[:n], refs[n:2 * n], *refs[2 * n:], broadcast)
        start()
        wait()

    any_spec = pl.BlockSpec(memory_space=pl.ANY)
    return pl.pallas_call(
        body, name=name, out_shape=_exchange_shapes(srcs, broadcast), in_specs=[any_spec] * n, out_specs=[any_spec] * n,
        scratch_shapes=_exchange_scratch(n),
        compiler_params=pltpu.CompilerParams(has_side_effects=True),
    )(*srcs)


def _flags(broadcast, n):
    return [broadcast] * n if isinstance(broadcast, bool) else list(broadcast)


def _exchange_shapes(srcs, broadcast):
    return [jax.ShapeDtypeStruct((N_DEV,) + (s.shape if bc else s.shape[1:]), s.dtype)
            for s, bc in zip(srcs, _flags(broadcast, len(srcs)))]


def _exchange_scratch(n):
    return [pltpu.SemaphoreType.DMA((n, N_DEV)), pltpu.SemaphoreType.DMA((n, N_DEV)), pltpu.SemaphoreType.DMA((n,))]


def _exchange_ops(src_refs, out_refs, send_sems, recv_sems, local_sems, broadcast):
    n = len(src_refs)
    flags = _flags(broadcast, n)
    x, y, c = lax.axis_index("x"), lax.axis_index("y"), lax.axis_index("c")
    me = 4 * x + 2 * y + c

    def block(i, j):
        return src_refs[i] if flags[i] else src_refs[i].at[j]

    def remote(i, d, src_slot, dst_slot):
        px, py, pc = x ^ (d >> 2), y ^ ((d >> 1) & 1), c ^ (d & 1)
        return pltpu.make_async_remote_copy(
            src_ref=block(i, src_slot), dst_ref=out_refs[i].at[dst_slot], send_sem=send_sems.at[i, d],
            recv_sem=recv_sems.at[i, d], device_id=(px, py, pc), device_id_type=_MESH)

    def local(i):
        return pltpu.make_async_copy(block(i, me), out_refs[i].at[me], local_sems.at[i])

    def start():
        for i in range(n):
            local(i).start()
        for d in range(1, N_DEV):
            for i in range(n):
                remote(i, d, me ^ d, me).start()

    def wait():
        for d in range(1, N_DEV):
            for i in range(n):
                remote(i, d, me, me ^ d).wait_recv()
        for d in range(1, N_DEV):
            for i in range(n):
                remote(i, d, me ^ d, me).wait_send()
        for i in range(n):
            local(i).wait()

    return start, wait


def _sum_adam(parts, w, m, v, name):
    n_parts, R, C = parts.shape
    tm = 256 if R % 256 == 0 else R
    c1 = 1.0 / (1.0 - ADAM_B1 ** ADAM_STEP)
    c2 = 1.0 / (1.0 - ADAM_B2 ** ADAM_STEP)

    def body(p_ref, w_ref, m_ref, v_ref, g_ref, d_ref, nm_ref, nv_ref):
        g = p_ref[0].astype(F32)
        for j in range(1, n_parts):
            g = g + p_ref[j].astype(F32)
        nm = ADAM_B1 * m_ref[...] + (1.0 - ADAM_B1) * g
        nv = ADAM_B2 * v_ref[...] + (1.0 - ADAM_B2) * (g * g)
        g_ref[...] = g
        nm_ref[...] = nm
        nv_ref[...] = nv
        d_ref[...] = -ADAM_LR * ((nm * c1) / (jnp.sqrt(nv * c2) + ADAM_EPS) + ADAM_WD * w_ref[...])

    row = _rows(tm, C)
    shp = jax.ShapeDtypeStruct((R, C), F32)
    return pl.pallas_call(body, name=name, grid=(R // tm,),
                          in_specs=[pl.BlockSpec((n_parts, tm, C), lambda i: (0, i, 0)), row, row, row],
                          out_specs=[row] * 4, out_shape=[shp] * 4,
                          compiler_params=_cparams(("parallel",)))(parts, w, m, v)


PACK_ALIGN = 16 * LANES
PACK_ROWS = 512 * LANES

SHARDED = (("w_ada", 1), ("w_in", 1), ("w2", 1), ("a2", 1), ("g2", 1), ("w_att_out", 1), ("w_rwkv_out", 0),
           ("w_o", 0), ("w_up", 1), ("conv_w", 1), ("w_down", 0))
EARLY, LATE = SHARDED[1:5], SHARDED[5:]
REPLICATED = ("b_ada", "norm1_w", "b_gate", "mu_shift", "w0", "a0", "k_k", "k_a", "r_k", "lnx_w", "lnx_b",
              "norm2_w", "conv_b", "norm_f_w")
WEIGHTS = ("w_ada", "b_ada", "norm1_w", "w_in", "b_gate", "mu_shift", "w0", "w2", "a0", "a2", "g2", "k_k", "k_a", "r_k",
           "lnx_w", "lnx_b", "w_att_out", "w_rwkv_out", "w_o", "norm2_w", "w_up", "conv_w", "conv_b", "w_down", "norm_f_w")


def _pack(arrays):
    flat, layout, off = [], [], 0
    for i, a in enumerate(arrays):
        n = a.size
        pad = (-n) % PACK_ALIGN if i + 1 < len(arrays) else (-(off + n)) % PACK_ROWS
        flat.append(a.reshape(-1))
        if pad:
            flat.append(jnp.zeros((pad,), a.dtype))
        layout.append((off, n, a.shape))
        off += n + pad
    return jnp.concatenate(flat).reshape(-1, LANES), layout


def _unpack(buf, layout):
    flat = buf.reshape(-1)
    return [flat[off:off + n].reshape(shape) for off, n, shape in layout]


def _pad_w_in(w_in):
    rkv = w_in[:, ATT_IN:ATT_IN + 3 * D]
    lora = w_in[:, ATT_IN + 3 * D:ATT_IN + RWKV_IN]
    gates = w_in[:, ATT_IN + RWKV_IN:]
    att = w_in[:, :ATT_IN]
    lw, la, lg = lora[:, :LORA_W], lora[:, LORA_W:LORA_W + LORA_A], lora[:, LORA_W + LORA_A:]
    zeros = jnp.zeros((w_in.shape[0], LORA_PAD - LANES - LORA_G), w_in.dtype)
    return jnp.concatenate([rkv, gates, att, lw, la, lg, zeros], axis=1)


def _unpad_w_in(g):
    att = g[:, C_ATT:C_ATT + ATT_IN]
    rkv = g[:, C_R:C_R + 3 * D]
    lora = jnp.concatenate([g[:, C_LORA:C_LORA + LORA_W + LORA_A], g[:, C_LORA + LANES:C_LORA + LANES + LORA_G]], axis=1)
    gates = g[:, C_GA:C_GA + 2 * D]
    return jnp.concatenate([att, rkv, lora, gates], axis=1)


def _pad_mu(mu):
    lo = mu[:, 3 * D:]
    mu_l = jnp.concatenate([lo[:, :LORA_W + LORA_A], lo[:, LORA_W + LORA_A:], jnp.zeros((1, LORA_PAD - LANES - LORA_G), mu.dtype)], axis=1)
    return mu[:, :D], mu[:, D:2 * D], mu[:, 2 * D:3 * D], mu_l


def _local_step(x, ada, W, late_shards, target):
    S = x.shape[0]
    W = dict(W)
    G = {}
    sh1, sc1, gt1, sh2, sc2, gt2 = [ada[:, i * D:(i + 1) * D] for i in range(6)]
    h1, rstd1 = _norm_fwd(x, None, None, W["norm1_w"], sc1, sh1, "norm1_fwd")
    w_in_p = _pad_w_in(W["w_in"])
    P = _mm(h1, w_in_p, "nn", F32, "proj_in")

    mu_r, mu_k, mu_v, mu_l = _pad_mu(W["mu_shift"])
    g2p = jnp.pad(W["g2"], ((0, G_PAD - LORA_G), (0, 0)))
    prep_params = [mu_r, mu_k, mu_v, mu_l, W["w0"], W["a0"], W["k_k"], W["k_a"], W["w2"], W["a2"], g2p]
    r_, dec, kmod, v_, aa, bb, gg = _rwkv_prep(P, prep_params)
    y_scan, states, late = _cscan_fwd(r_, dec, kmod, v_, aa, bb, gather=late_shards)
    W.update({n: _full_weight(g, axis) for (n, axis), g in zip(LATE, late)})

    o_g, l_g = zip(*[_att_fwd(P, g) for g in range(len(ATT_PATTERNS))])
    att = _att_combine_fwd(o_g, l_g)
    y_att = _mm(att, W["w_att_out"], "nn", F32, "att_out")
    r_k = W["r_k"].reshape(1, D)
    rw = _rwkv_post(y_scan, r_, kmod, v_, gg, W["lnx_w"], W["lnx_b"], r_k)
    y_rwkv = _mm(rw, W["w_rwkv_out"], "nn", F32, "rwkv_out")

    bga, bgr = W["b_gate"][:, :D], W["b_gate"][:, D:]
    mix = _gate_fwd(P, bga, bgr, y_att, y_rwkv)
    mo = _mm(mix, W["w_o"], "nn", F32, "mix_out")
    x2, h2, rstd2 = _norm_fwd(x, mo, gt1, W["norm2_w"], sc2, sh2, "norm2_fwd")
    u = _mm(h2, W["w_up"], "nn", F32, "ffn_up")
    conv_w8 = jnp.pad(W["conv_w"], ((0, SUBLANES - 3), (0, 0)))
    act = _conv_fwd(u, conv_w8, W["conv_b"])
    f = _mm(act, W["w_down"], "nn", F32, "ffn_down")
    loss_blk, dx3, df, dgt2, G["norm_f_w"] = _final(x2, f, gt2, W["norm_f_w"], target)
    loss = loss_blk[0, 0]

    dact = _mm(df, W["w_down"], "nt", BF16, "ffn_down_dx")
    G["w_down"] = _mm(act, df, "tn", F32, "ffn_down_dw")
    duc, dwg, dwv, dbg, dbv = _conv_bwd_a(dact, u, conv_w8, W["conv_b"])
    G["conv_w"] = jnp.concatenate([dwg[0:3], dwv[0:3]], axis=1)
    G["conv_b"] = jnp.concatenate([dbg, dbv], axis=1)
    du = _conv_bwd_b(duc, conv_w8)
    dh2 = _mm(du, W["w_up"], "nt", F32, "ffn_up_dx")
    G["w_up"] = _mm(h2, du, "tn", F32, "ffn_up_dw")
    dx2, dsh2, dsc2, G["norm2_w"], dmo, dgt1 = _norm_bwd(dh2, x2, rstd2, W["norm2_w"], sc2, dx3, mo, gt1, "norm2_bwd")
    dmix = _mm(dmo, W["w_o"], "nt", F32, "mix_out_dx")
    G["w_o"] = _mm(mix, dmo, "tn", F32, "mix_out_dw")
    dy_att, dy_rwkv, dpga, dpgr, dbga, dbgr = _gate_bwd(dmix, P, bga, bgr, y_att, y_rwkv)
    G["b_gate"] = jnp.concatenate([dbga, dbgr], axis=1)

    datt = _mm(dy_att, W["w_att_out"], "nt", F32, "att_out_dx")
    G["w_att_out"] = _mm(att, dy_att, "tn", F32, "att_out_dw")
    dcomb = _att_combine_bwd(datt, o_g, l_g)
    dp_att = []
    for g in range(len(ATT_PATTERNS)):
        dp_att += _att_bwd(P, o_g[g], l_g[g], dcomb[g], dcomb[3 + g], g)

    drw = _mm(dy_rwkv, W["w_rwkv_out"], "nt", F32, "rwkv_out_dx")
    G["w_rwkv_out"] = _mm(rw, dy_rwkv, "tn", F32, "rwkv_out_dw")
    dy_scan, dr1, dk1, dv1, dgg, G["lnx_w"], G["lnx_b"], drk = _rwkv_post_bwd(drw, y_scan, r_, kmod, v_, gg, W["lnx_w"], W["lnx_b"], r_k)
    G["r_k"] = drk.reshape(W["r_k"].shape)
    late_blocks = [_owner_blocks(G[n], axis) for n, axis in LATE] if late_shards else []
    (dr2, ddec, dk2, dv2, daa, dbb), late_parts = _cscan_bwd(r_, dec, kmod, v_, aa, bb, states, dy_scan, scatter=late_blocks)
    pb = _rwkv_prep_bwd(P, prep_params, [dr2, ddec, dk2, dv2, daa, dbb, dgg], [dr1, None, dk1, dv1, None, None, None])
    dz, dzp, dpar = pb[0:4], pb[4:8], pb[8:]
    dp_rkv = [_shift_add(dz[i], dzp[i]) for i in range(3)]
    dp_lora = _shift_add(dz[3], dzp[3])
    dmu_r, dmu_k, dmu_v, dmu_l, G["w0"], G["a0"], G["k_k"], G["k_a"], G["w2"], G["a2"], dg2p = dpar
    G["g2"] = dg2p[0:LORA_G]
    G["mu_shift"] = jnp.concatenate([dmu_r, dmu_k, dmu_v, dmu_l[:, :LORA_W + LORA_A], dmu_l[:, LANES:LANES + LORA_G]], axis=1)

    dP = jnp.concatenate(dp_rkv + [dpga, dpgr] + dp_att + [dp_lora], axis=1)
    G["w_in"] = _unpad_w_in(_mm(h1, dP, "tn", F32, "proj_in_dw"))
    if late_shards:
        dh1, (w_in_parts,) = _mm(dP, w_in_p, "nt", F32, "proj_in_dx", scatter=[_owner_blocks(G["w_in"], 1)])
        done = dict(zip([n for n, _ in LATE] + ["w_in"], list(late_parts) + [w_in_parts]))
    else:
        dh1, done = _mm(dP, w_in_p, "nt", F32, "proj_in_dx"), {}
    grad_x, dsh1, dsc1, G["norm1_w"] = _norm_bwd(dh1, x, rstd1, W["norm1_w"], sc1, dx2, None, None, "norm1_bwd")
    dada = jnp.concatenate([dsh1, dsc1, dgt1, dsh2, dsc2, dgt2], axis=1)
    G["b_ada"] = dada
    return loss, grad_x, G, done


def _full_weight(gathered, axis):
    _, rows, cols = gathered.shape
    if axis == 0:
        return gathered.reshape(N_DEV * rows, cols)
    return gathered.transpose(1, 0, 2).reshape(rows, N_DEV * cols)


def _owner_blocks(g, axis):
    rows, cols = g.shape
    g = g.astype(BF16)
    if axis == 0:
        return g.reshape(N_DEV, rows // N_DEV, cols)
    return g.reshape(rows, N_DEV, cols // N_DEV).transpose(1, 0, 2)


def kernel(x, c, w_ada, b_ada, norm1_w, w_in, b_gate, mu_shift, w0, w2, a0, a2, g2, k_k, k_a, r_k, lnx_w, lnx_b, w_att_out, w_rwkv_out, w_o, norm2_w, w_up, conv_w, conv_b, w_down, norm_f_w, loss_target, m_w_ada, m_b_ada, m_norm1_w, m_w_in, m_b_gate, m_mu_shift, m_w0, m_w2, m_a0, m_a2, m_g2, m_k_k, m_k_a, m_r_k, m_lnx_w, m_lnx_b, m_w_att_out, m_w_rwkv_out, m_w_o, m_norm2_w, m_w_up, m_conv_w, m_conv_b, m_w_down, m_norm_f_w, v_w_ada, v_b_ada, v_norm1_w, v_w_in, v_b_gate, v_mu_shift, v_w0, v_w2, v_a0, v_a2, v_g2, v_k_k, v_k_a, v_r_k, v_lnx_w, v_lnx_b, v_w_att_out, v_w_rwkv_out, v_w_o, v_norm2_w, v_w_up, v_conv_w, v_conv_b, v_w_down, v_norm_f_w):
    env = dict(locals())
    w_shard = {n: env[n] for n in WEIGHTS}
    m_shard = {n: env["m_" + n] for n in WEIGHTS}
    v_shard = {n: env["v_" + n] for n in WEIGHTS}

    c_all, *gathered = _exchange([c] + [w_shard[n][0].astype(BF16) for n, _ in EARLY], True, "gather_weights")
    c_all = c_all.reshape(N_DEV, D)
    W = {n: _full_weight(g, axis) for (n, axis), g in zip(EARLY, gathered)}
    for n in REPLICATED:
        W[n] = w_shard[n].reshape(1, -1) if n != "r_k" else w_shard[n][0]
    ada_cols = _ada_partial(c_all, w_shard["w_ada"][0])
    ada_rows, = _exchange([ada_cols[:, None, :]], False, "ada_rows")
    ada = _ada_bias(ada_rows.reshape(1, -1), w_shard["b_ada"])

    late_shards = [w_shard[n][0].astype(BF16) for n, _ in LATE]
    loss, grad_x, G, parts = _local_step(x[0], ada, W, late_shards, loss_target[0])
    loss = lax.psum(loss, ("x", "y", "c"))

    small, slayout = _pack([G[n].reshape(-1) for n in REPLICATED])
    sparts, dada_all = _exchange([small, G["b_ada"].reshape(N_DEV, 1, -1)], [True, False], "gather_small_grads")
    parts["w_ada"] = _ada_wgrad(c_all.T, dada_all.reshape(N_DEV, -1))[None]

    rest = [(n, axis) for n, axis in SHARDED if n not in parts]
    parts.update(zip([n for n, _ in rest], _exchange([_owner_blocks(G[n], axis) for n, axis in rest], False, "scatter_grads")))
    out = {}
    for n, p in parts.items():
        res = _sum_adam(p, w_shard[n][0], m_shard[n][0], v_shard[n][0], "adam_" + n)
        for kind, a in zip(("grad", "delta", "new_m", "new_v"), res):
            out[kind, n] = a[None]

    sw, _ = _pack([w_shard[n].reshape(-1) for n in REPLICATED])
    sm, _ = _pack([m_shard[n].reshape(-1) for n in REPLICATED])
    sv, _ = _pack([v_shard[n].reshape(-1) for n in REPLICATED])
    res = _sum_adam(sparts, sw, sm, sv, "adam_replicated")
    for kind, buf in zip(("grad", "delta", "new_m", "new_v"), res):
        for n, a in zip(REPLICATED, _unpack(buf, slayout)):
            out[kind, n] = a.reshape(w_shard[n].shape)

    return (loss, grad_x[None], *[out[kind, n] for kind in ("grad", "delta", "new_m", "new_v") for n in WEIGHTS])
```

```python
import functools
import math

import jax
import jax.numpy as jnp
from jax import lax
from jax.experimental import pallas as pl
from jax.experimental.pallas import tpu as pltpu

F32 = jnp.float32
BF16 = jnp.bfloat16

D = 1024
HEAD = 64
ATT_PATTERNS = ((128, 1), (512, 4), (2048, 16))
ATT_HEADS = 8
ATT_W = ATT_HEADS * HEAD
ATT_IN = 3 * 3 * ATT_W
QBLK = 128
N_HEADS = D // HEAD
LORA_W, LORA_A, LORA_G = 64, 64, 160
RWKV_IN = 3 * D + LORA_W + LORA_A + LORA_G
N_IN = ATT_IN + RWKV_IN + 2 * D
D_FF = 2816
RMS_EPS = 1e-6
GN_EPS = 64e-5
N_DEV = 8
LANES = 128
SUBLANES = 8

C_R, C_K, C_V, C_GA, C_GR = 0, 1024, 2048, 3072, 4096
C_ATT = 5120
C_LORA = C_ATT + ATT_IN
LORA_PAD = 512
G_PAD = 256
N_PAD = C_LORA + LORA_PAD

ADAM_LR, ADAM_B1, ADAM_B2, ADAM_EPS, ADAM_WD, ADAM_STEP = 0.001, 0.9, 0.999, 1e-08, 0.01, 10

VMEM_LIMIT = 56 * 1024 * 1024

_MESH = pl.DeviceIdType.MESH


def _cparams(sem):
    return pltpu.CompilerParams(dimension_semantics=sem, vmem_limit_bytes=VMEM_LIMIT)


def _tile(dim, pref):
    if dim <= pref:
        return dim
    best = None
    for t in range(LANES, pref + 1, LANES):
        if dim % t == 0:
            best = t
    assert best is not None, dim
    return best


MM_TILES = {"nn": (1024, 1408, 1408), "nt": (512, 2048, 1408), "tn": (1408, 1408, 1024)}


def _mm(a, b, mode, out_dtype, name, scatter=()):
    if mode == "nn":
        (M, K), (K2, N) = a.shape, b.shape
    elif mode == "nt":
        (M, K), (N, K2) = a.shape, b.shape
    else:
        (K, M), (K2, N) = a.shape, b.shape
    assert K == K2, (a.shape, b.shape, mode)
    tm, tn, tk = (_tile(dim, pref) for dim, pref in zip((M, N, K), MM_TILES[mode]))
    nk = K // tk
    grid = (M // tm, N // tn, nk)
    n_x = len(scatter)
    dims = {"nn": (((1,), (0,)), ((), ())), "nt": (((1,), (1,)), ((), ())), "tn": (((0,), (0,)), ((), ()))}[mode]

    def body(*refs):
        a_ref, b_ref = refs[:2]
        o_ref, acc_ref = refs[2 + n_x], refs[3 + 2 * n_x]
        finish = _hosted_exchange(refs[2:2 + n_x] + refs[3 + n_x:3 + 2 * n_x] + refs[4 + 2 * n_x:], n_x, False, grid)
        k = pl.program_id(2)
        part = lax.dot_general(a_ref[...].astype(BF16), b_ref[...].astype(BF16), dims,
                               preferred_element_type=F32)
        if nk == 1:
            o_ref[...] = part.astype(o_ref.dtype)
        else:
            @pl.when(k == 0)
            def _():
                acc_ref[...] = part

            @pl.when(jnp.logical_and(k > 0, k < nk - 1))
            def _():
                acc_ref[...] += part

            @pl.when(k == nk - 1)
            def _():
                o_ref[...] = (acc_ref[...] + part).astype(o_ref.dtype)
        finish()

    a_spec = pl.BlockSpec((tk, tm), lambda i, j, k: (k, i)) if mode == "tn" else pl.BlockSpec((tm, tk), lambda i, j, k: (i, k))
    b_spec = pl.BlockSpec((tn, tk), lambda i, j, k: (j, k)) if mode == "nt" else pl.BlockSpec((tk, tn), lambda i, j, k: (k, j))
    any_spec = pl.BlockSpec(memory_space=pl.ANY)
    outs = pl.pallas_call(
        body, name=name, grid=grid,
        in_specs=[a_spec, b_spec] + [any_spec] * n_x,
        out_specs=[pl.BlockSpec((tm, tn), lambda i, j, k: (i, j))] + [any_spec] * n_x,
        out_shape=[jax.ShapeDtypeStruct((M, N), out_dtype)] + _exchange_shapes(scatter, False),
        scratch_shapes=[pltpu.VMEM((tm, tn) if nk > 1 else (SUBLANES, LANES), F32)] + (_exchange_scratch(n_x) if n_x else []),
        compiler_params=_cparams(("arbitrary",) * 3 if n_x else ("parallel", "parallel", "arbitrary")),
    )(a, b, *scatter)
    return (outs[0], outs[1:]) if n_x else outs[0]


def _rows(tm, w, col=0):
    return pl.BlockSpec((tm, w), lambda i: (i, col))


def _full(shape):
    return pl.BlockSpec(shape, lambda i: (0,) * len(shape))


def _prev8(tm, w, col=0):
    return pl.BlockSpec((SUBLANES, w), lambda i: (jnp.maximum(i * (tm // SUBLANES) - 1, 0), col))


def _next8(tm, w, n_rows, col=0):
    last = n_rows // SUBLANES - 1
    return pl.BlockSpec((SUBLANES, w), lambda i: (jnp.minimum((i + 1) * (tm // SUBLANES), last), col))


def _shift_down(x, halo, k, first):
    rolled = pltpu.roll(x, k, 0)
    row = lax.broadcasted_iota(jnp.int32, x.shape, 0)
    out = rolled
    for j in range(k):
        h = jnp.where(first, 0.0, halo[SUBLANES - k + j:SUBLANES - k + j + 1, :])
        out = jnp.where(row == j, h, out)
    return out


def _shift_up(x, halo, k, last):
    n = x.shape[0]
    rolled = pltpu.roll(x, n - k, 0)
    row = lax.broadcasted_iota(jnp.int32, x.shape, 0)
    out = rolled
    for j in range(k):
        h = jnp.where(last, 0.0, halo[j:j + 1, :])
        out = jnp.where(row == n - k + j, h, out)
    return out


def _acc(ref, val, first):
    @pl.when(first)
    def _():
        ref[...] = val

    @pl.when(jnp.logical_not(first))
    def _():
        ref[...] += val


def _colsum(x):
    return jnp.sum(x, axis=0, keepdims=True)


def _norm_fwd(x, mo, gt, nw, sc, sh, name, tm=256):
    S = x.shape[0]
    has_res = mo is not None

    def body(*refs):
        if has_res:
            x_ref, mo_ref, gt_ref, nw_ref, sc_ref, sh_ref, x2_ref, h_ref, rs_ref = refs
            x2 = x_ref[...] + gt_ref[...] * mo_ref[...]
            x2_ref[...] = x2
        else:
            x_ref, nw_ref, sc_ref, sh_ref, h_ref, rs_ref = refs
            x2 = x_ref[...]
        rstd = lax.rsqrt(jnp.mean(x2 * x2, axis=-1, keepdims=True) + RMS_EPS)
        rs_ref[...] = rstd
        h_ref[...] = ((x2 * rstd * nw_ref[...]) * (1.0 + sc_ref[...]) + sh_ref[...]).astype(BF16)

    vec = _full((1, D))
    ins = [x, mo, gt, nw, sc, sh] if has_res else [x, nw, sc, sh]
    in_specs = [_rows(tm, D), _rows(tm, D), vec, vec, vec, vec] if has_res else [_rows(tm, D), vec, vec, vec]
    outs = [jax.ShapeDtypeStruct((S, D), BF16), jax.ShapeDtypeStruct((S, 1), F32)]
    out_specs = [_rows(tm, D), _rows(tm, 1)]
    if has_res:
        outs = [jax.ShapeDtypeStruct((S, D), F32)] + outs
        out_specs = [_rows(tm, D)] + out_specs
    return pl.pallas_call(body, name=name, grid=(S // tm,), in_specs=in_specs, out_specs=out_specs,
                          out_shape=outs, compiler_params=_cparams(("parallel",)))(*ins)


def _norm_bwd(dh, xin, rstd, nw, sc, dres, mo, gt, name, tm=256):
    S = xin.shape[0]
    has_res = mo is not None

    def body(*refs):
        if has_res:
            dh_ref, x_ref, rs_ref, nw_ref, sc_ref, dres_ref, mo_ref, gt_ref, dx_ref, dsh_ref, dsc_ref, dnw_ref, dmo_ref, dgt_ref = refs
        else:
            dh_ref, x_ref, rs_ref, nw_ref, sc_ref, dres_ref, dx_ref, dsh_ref, dsc_ref, dnw_ref = refs
        first = pl.program_id(0) == 0
        dh = dh_ref[...]
        rstd = rs_ref[...]
        n = x_ref[...] * rstd
        w = nw_ref[...]
        _acc(dsh_ref, _colsum(dh), first)
        _acc(dsc_ref, _colsum(dh * (n * w)), first)
        dnw = dh * (1.0 + sc_ref[...])
        _acc(dnw_ref, _colsum(dnw * n), first)
        dn = dnw * w
        dx = dres_ref[...] + rstd * (dn - n * jnp.mean(dn * n, axis=-1, keepdims=True))
        dx_ref[...] = dx
        if has_res:
            dmo_ref[...] = (dx * gt_ref[...]).astype(BF16)
            _acc(dgt_ref, _colsum(dx * mo_ref[...]), first)

    vec = _full((1, D))
    vshape = jax.ShapeDtypeStruct((1, D), F32)
    ins = [dh, xin, rstd, nw, sc, dres] + ([mo, gt] if has_res else [])
    in_specs = [_rows(tm, D), _rows(tm, D), _rows(tm, 1), vec, vec, _rows(tm, D)] + ([_rows(tm, D), vec] if has_res else [])
    outs = [jax.ShapeDtypeStruct((S, D), F32), vshape, vshape, vshape]
    out_specs = [_rows(tm, D), vec, vec, vec]
    if has_res:
        outs += [jax.ShapeDtypeStruct((S, D), BF16), vshape]
        out_specs += [_rows(tm, D), vec]
    return pl.pallas_call(body, name=name, grid=(S // tm,), in_specs=in_specs, out_specs=out_specs,
                          out_shape=outs, compiler_params=_cparams(("arbitrary",)))(*ins)


def _final(x2, f, gt2, nfw, target, tm=256):
    S = x2.shape[0]

    def body(x2_ref, f_ref, gt_ref, w_ref, t_ref, loss_ref, dx_ref, df_ref, dgt_ref, dw_ref):
        first = pl.program_id(0) == 0
        f = f_ref[...]
        gt = gt_ref[...]
        w = w_ref[...]
        x3 = x2_ref[...] + gt * f
        rstd = lax.rsqrt(jnp.mean(x3 * x3, axis=-1, keepdims=True) + RMS_EPS)
        n = x3 * rstd
        e = n * w - t_ref[...]
        part = 0.5 * jnp.sum(jnp.mean(e * e, axis=-1, keepdims=True), axis=0, keepdims=True)
        _acc(loss_ref, jnp.broadcast_to(part, (SUBLANES, LANES)), first)
        dy = e * (1.0 / D)
        _acc(dw_ref, _colsum(dy * n), first)
        dn = dy * w
        dx = rstd * (dn - n * jnp.mean(dn * n, axis=-1, keepdims=True))
        dx_ref[...] = dx
        df_ref[...] = (dx * gt).astype(BF16)
        _acc(dgt_ref, _colsum(dx * f), first)

    vec = _full((1, D))
    vshape = jax.ShapeDtypeStruct((1, D), F32)
    return pl.pallas_call(
        body, name="final_loss", grid=(S // tm,),
        in_specs=[_rows(tm, D), _rows(tm, D), vec, vec, _rows(tm, D)],
        out_specs=[_full((SUBLANES, LANES)), _rows(tm, D), _rows(tm, D), vec, vec],
        out_shape=[jax.ShapeDtypeStruct((SUBLANES, LANES), F32), jax.ShapeDtypeStruct((S, D), F32),
                   jax.ShapeDtypeStruct((S, D), BF16), vshape, vshape],
        compiler_params=_cparams(("arbitrary",)))(x2, f, gt2, nfw, target)


def _gate_fwd(P, bga, bgr, y_att, y_rwkv, tm=256):
    S = P.shape[0]

    def body(pa_ref, pr_ref, ba_ref, br_ref, ya_ref, yr_ref, mix_ref):
        ga = jax.nn.sigmoid(pa_ref[...] + ba_ref[...])
        gr = jax.nn.sigmoid(pr_ref[...] + br_ref[...])
        mix_ref[...] = (ga * ya_ref[...] + gr * yr_ref[...]).astype(BF16)

    vec = _full((1, D))
    return pl.pallas_call(
        body, name="gate_fwd", grid=(S // tm,),
        in_specs=[_rows(tm, D, C_GA // D), _rows(tm, D, C_GR // D), vec, vec, _rows(tm, D), _rows(tm, D)],
        out_specs=_rows(tm, D), out_shape=jax.ShapeDtypeStruct((S, D), BF16),
        compiler_params=_cparams(("parallel",)))(P, P, bga, bgr, y_att, y_rwkv)


def _gate_bwd(dmix, P, bga, bgr, y_att, y_rwkv, tm=256):
    S = P.shape[0]

    def body(dm_ref, pa_ref, pr_ref, ba_ref, br_ref, ya_ref, yr_ref, dya_ref, dyr_ref, dpa_ref, dpr_ref, dba_ref, dbr_ref):
        first = pl.program_id(0) == 0
        dm = dm_ref[...]
        ga = jax.nn.sigmoid(pa_ref[...] + ba_ref[...])
        gr = jax.nn.sigmoid(pr_ref[...] + br_ref[...])
        dya_ref[...] = (dm * ga).astype(BF16)
        dyr_ref[...] = (dm * gr).astype(BF16)
        dpa = dm * ya_ref[...] * ga * (1.0 - ga)
        dpr = dm * yr_ref[...] * gr * (1.0 - gr)
        dpa_ref[...] = dpa.astype(BF16)
        dpr_ref[...] = dpr.astype(BF16)
        _acc(dba_ref, _colsum(dpa), first)
        _acc(dbr_ref, _colsum(dpr), first)

    vec = _full((1, D))
    row = _rows(tm, D)
    rshape = jax.ShapeDtypeStruct((S, D), BF16)
    vshape = jax.ShapeDtypeStruct((1, D), F32)
    return pl.pallas_call(
        body, name="gate_bwd", grid=(S // tm,),
        in_specs=[row, _rows(tm, D, C_GA // D), _rows(tm, D, C_GR // D), vec, vec, row, row],
        out_specs=[row, row, row, row, vec, vec],
        out_shape=[rshape, rshape, rshape, rshape, vshape, vshape],
        compiler_params=_cparams(("arbitrary",)))(dmix, P, P, bga, bgr, y_att, y_rwkv)


CONV_TN = D_FF // 2


def _conv_fwd(u, conv_w8, conv_b, tm=256, tn=CONV_TN):
    S = u.shape[0]
    nj = D_FF // tn

    def conv(u_ref, h_ref, w_ref, b_ref, first):
        u = u_ref[...]
        h = h_ref[...]
        w = w_ref[...]
        return b_ref[...] + w[0:1] * _shift_down(u, h, 2, first) + w[1:2] * _shift_down(u, h, 1, first) + w[2:3] * u

    def body(ug_ref, hg_ref, uv_ref, hv_ref, wg_ref, wv_ref, bg_ref, bv_ref, act_ref):
        first = pl.program_id(0) == 0
        g = conv(ug_ref, hg_ref, wg_ref, bg_ref, first)
        v = conv(uv_ref, hv_ref, wv_ref, bv_ref, first)
        act_ref[...] = (g * jax.nn.sigmoid(g) * v).astype(BF16)

    blk = lambda off: pl.BlockSpec((tm, tn), lambda i, j: (i, j + off))
    halo = lambda off: pl.BlockSpec((SUBLANES, tn), lambda i, j: (jnp.maximum(i * (tm // SUBLANES) - 1, 0), j + off))
    wsp = lambda off: pl.BlockSpec((SUBLANES, tn), lambda i, j: (0, j + off))
    bsp = lambda off: pl.BlockSpec((1, tn), lambda i, j: (0, j + off))
    return pl.pallas_call(
        body, name="conv_fwd", grid=(S // tm, nj),
        in_specs=[blk(0), halo(0), blk(nj), halo(nj), wsp(0), wsp(nj), bsp(0), bsp(nj)],
        out_specs=pl.BlockSpec((tm, tn), lambda i, j: (i, j)),
        out_shape=jax.ShapeDtypeStruct((S, D_FF), BF16),
        compiler_params=_cparams(("parallel", "parallel")))(u, u, u, u, conv_w8, conv_w8, conv_b, conv_b)


def _conv_bwd_a(dact, u, conv_w8, conv_b, tm=256, tn=CONV_TN):
    S = u.shape[0]
    nj = D_FF // tn

    def half(u_ref, h_ref, w_ref, b_ref, first):
        u = u_ref[...]
        h = h_ref[...]
        w = w_ref[...]
        u2, u1 = _shift_down(u, h, 2, first), _shift_down(u, h, 1, first)
        return b_ref[...] + w[0:1] * u2 + w[1:2] * u1 + w[2:3] * u, (u2, u1, u)

    def wgrad(d, taps):
        z = jnp.zeros((SUBLANES - 3, d.shape[1]), F32)
        return jnp.concatenate([_colsum(d * taps[0]), _colsum(d * taps[1]), _colsum(d * taps[2]), z], axis=0)

    def body(da_ref, ug_ref, hg_ref, uv_ref, hv_ref, wg_ref, wv_ref, bg_ref, bv_ref,
             d_ref, dwg_ref, dwv_ref, dbg_ref, dbv_ref):
        first = pl.program_id(1) == 0
        g, tg = half(ug_ref, hg_ref, wg_ref, bg_ref, first)
        v, tv = half(uv_ref, hv_ref, wv_ref, bv_ref, first)
        da = da_ref[...].astype(F32)
        sg = jax.nn.sigmoid(g)
        dg = da * v * (sg * (1.0 + g * (1.0 - sg)))
        dv = da * (g * sg)
        d_ref[0] = dg
        d_ref[1] = dv
        _acc(dwg_ref, wgrad(dg, tg), first)
        _acc(dwv_ref, wgrad(dv, tv), first)
        _acc(dbg_ref, _colsum(dg), first)
        _acc(dbv_ref, _colsum(dv), first)

    blk = lambda off: pl.BlockSpec((tm, tn), lambda j, i: (i, j + off))
    halo = lambda off: pl.BlockSpec((SUBLANES, tn), lambda j, i: (jnp.maximum(i * (tm // SUBLANES) - 1, 0), j + off))
    wsp = lambda off: pl.BlockSpec((SUBLANES, tn), lambda j, i: (0, j + off))
    bsp = lambda off: pl.BlockSpec((1, tn), lambda j, i: (0, j + off))
    f = jax.ShapeDtypeStruct
    outs = pl.pallas_call(
        body, name="conv_bwd_a", grid=(nj, S // tm),
        in_specs=[pl.BlockSpec((tm, tn), lambda j, i: (i, j)), blk(0), halo(0), blk(nj), halo(nj), wsp(0), wsp(nj), bsp(0), bsp(nj)],
        out_specs=[pl.BlockSpec((2, tm, tn), lambda j, i: (0, i, j)),
                   pl.BlockSpec((SUBLANES, tn), lambda j, i: (0, j)), pl.BlockSpec((SUBLANES, tn), lambda j, i: (0, j)),
                   pl.BlockSpec((1, tn), lambda j, i: (0, j)), pl.BlockSpec((1, tn), lambda j, i: (0, j))],
        out_shape=[f((2, S, D_FF), F32), f((SUBLANES, D_FF), F32), f((SUBLANES, D_FF), F32),
                   f((1, D_FF), F32), f((1, D_FF), F32)],
        compiler_params=_cparams(("parallel", "arbitrary")))(dact, u, u, u, u, conv_w8, conv_w8, conv_b, conv_b)
    return outs


def _conv_bwd_b(duc, conv_w8, tm=256, tn=CONV_TN):
    _, S, W = duc.shape
    nj = W // tn
    n_rows = S // tm

    def body(d_ref, h_ref, w_ref, o_ref):
        last = pl.program_id(0) == n_rows - 1
        d = d_ref[...]
        h = h_ref[...]
        w = w_ref[...]
        o_ref[...] = (w[2:3] * d + w[1:2] * _shift_up(d, h, 1, last) + w[0:1] * _shift_up(d, h, 2, last)).astype(BF16)

    last_tile = S // SUBLANES - 1
    return pl.pallas_call(
        body, name="conv_bwd_b", grid=(n_rows, 2 * nj),
        in_specs=[pl.BlockSpec((None, tm, tn), lambda i, j: (j // nj, i, j % nj)),
                  pl.BlockSpec((None, SUBLANES, tn), lambda i, j: (j // nj, jnp.minimum((i + 1) * (tm // SUBLANES), last_tile), j % nj)),
                  pl.BlockSpec((SUBLANES, tn), lambda i, j: (0, j))],
        out_specs=pl.BlockSpec((tm, tn), lambda i, j: (i, j)),
        out_shape=jax.ShapeDtypeStruct((S, 2 * W), BF16),
        compiler_params=_cparams(("parallel", "parallel")))(duc, duc, conv_w8)


ATT_SCALE = HEAD ** -0.5
NEG = -1e30
ATT_PAIRS = ATT_HEADS // 2


def _att_rows(n, d, S):
    per = S // (QBLK * d)
    r, m = n // per, n % per
    cur = pl.ds(m * (QBLK * d) + r, QBLK, stride=d)
    prv = pl.ds(jnp.maximum(m - 1, 0) * (QBLK * d) + r, QBLK, stride=d)
    return cur, prv, m > 0


def _att_slab(g, j):
    return (C_ATT + g * 3 * ATT_W + j * ATT_W) // LANES


def _heads(x):
    return x[:, 0:HEAD], x[:, HEAD:2 * HEAD]


ATT_NB = 4


def _stack(tiles):
    return jnp.concatenate([t[None] for t in tiles], axis=0)


def _att_operands(i, d, S, *sources):
    rows, has = [], []
    tiles = [[] for _ in sources]
    for bb in range(ATT_NB):
        cur, prv, has_prev = _att_rows(i * ATT_NB + bb, d, S)
        rows.append((cur, prv))
        has.append(has_prev)
        for t, (ref, use_cur) in zip(tiles, sources):
            t += _heads(ref[cur if use_cur else prv, :].astype(BF16))
    return rows, has, [_stack(t) for t in tiles]


def _att_mask(s_c, s_p, has_prev):
    qi = lax.broadcasted_iota(jnp.int32, (QBLK, QBLK), 0)
    kj = lax.broadcasted_iota(jnp.int32, (QBLK, QBLK), 1)
    s_c = jnp.where(kj <= qi, s_c * ATT_SCALE, NEG)
    s_p = jnp.where(jnp.logical_and(kj >= qi, has_prev), s_p * ATT_SCALE, NEG)
    return s_c, s_p


def _att_fwd(P, g):
    S = P.shape[0]
    d = ATT_PATTERNS[g][1]

    def body(q_ref, k_ref, v_ref, o_ref, l_ref):
        def group(i, carry):
            rows, has, (q, kc, kp, vc, vp) = _att_operands(i, d, S, (q_ref, True), (k_ref, True), (k_ref, False),
                                                           (v_ref, True), (v_ref, False))
            s_c_all, s_p_all = _dot16(q, kc, "nt"), _dot16(q, kp, "nt")
            p_c, p_p, den, lse = [], [], [], []
            for e in range(2 * ATT_NB):
                s_c, s_p = _att_mask(s_c_all[e], s_p_all[e], has[e // 2])
                m = jnp.maximum(jnp.max(s_c, axis=1, keepdims=True), jnp.max(s_p, axis=1, keepdims=True))
                pc, pp = jnp.exp(s_c - m), jnp.exp(s_p - m)
                den.append(jnp.sum(pc, axis=1, keepdims=True) + jnp.sum(pp, axis=1, keepdims=True))
                lse.append(jnp.broadcast_to(m + jnp.log(den[e]), (QBLK, HEAD)))
                p_c.append(pc)
                p_p.append(pp)
            num = _dot16(_stack(p_c), vc, "nn") + _dot16(_stack(p_p), vp, "nn")
            for bb, (cur, _) in enumerate(rows):
                o_ref[cur, :] = jnp.concatenate([num[2 * bb] / den[2 * bb], num[2 * bb + 1] / den[2 * bb + 1]], axis=1)
                l_ref[cur, :] = jnp.concatenate(lse[2 * bb:2 * bb + 2], axis=1)
            return carry

        lax.fori_loop(0, S // QBLK // ATT_NB, group, 0)

    slab = lambda j: pl.BlockSpec((S, LANES), lambda i: (0, _att_slab(g, j) + i))
    out = pl.BlockSpec((S, LANES), lambda i: (0, i))
    shp = jax.ShapeDtypeStruct((S, ATT_W), F32)
    return pl.pallas_call(body, name=f"att_fwd_g{g}", grid=(ATT_PAIRS,), in_specs=[slab(0), slab(1), slab(2)],
                          out_specs=[out, out], out_shape=[shp, shp], compiler_params=_cparams(("parallel",)))(P, P, P)


def _att_bwd(P, o, l, do, dl, g):
    S = P.shape[0]
    d = ATT_PATTERNS[g][1]

    def body(q_ref, k_ref, v_ref, o_ref, l_ref, do_ref, dl_ref, dq_ref, dk_ref, dv_ref, dq_acc, dk_acc, dv_acc):
        dk_acc[...] = jnp.zeros_like(dk_acc)
        dv_acc[...] = jnp.zeros_like(dv_acc)

        def group(i, carry):
            rows, has, (q, kc, kp, vc, vp, dob) = _att_operands(
                i, d, S, (q_ref, True), (k_ref, True), (k_ref, False), (v_ref, True), (v_ref, False), (do_ref, True))
            s_c_all, s_p_all = _dot16(q, kc, "nt"), _dot16(q, kp, "nt")
            dp_c_all, dp_p_all = _dot16(dob, vc, "nt"), _dot16(dob, vp, "nt")
            p_c, p_p, ds_c, ds_p = [], [], [], []
            for bb, (cur, _) in enumerate(rows):
                dd2 = do_ref[cur, :] * o_ref[cur, :] - dl_ref[cur, :]
                for h, (dd, lse) in enumerate(zip(_heads(dd2), _heads(l_ref[cur, :]))):
                    e = 2 * bb + h
                    s_c, s_p = _att_mask(s_c_all[e], s_p_all[e], has[bb])
                    pc, pp = jnp.exp(s_c - lse[:, 0:1]), jnp.exp(s_p - lse[:, 0:1])
                    delta = jnp.sum(dd, axis=1, keepdims=True)
                    p_c.append(pc)
                    p_p.append(pp)
                    ds_c.append(pc * (dp_c_all[e] - delta) * ATT_SCALE)
                    ds_p.append(pp * (dp_p_all[e] - delta) * ATT_SCALE)
            p_c, p_p, ds_c, ds_p = map(_stack, (p_c, p_p, ds_c, ds_p))
            dq = _dot16(ds_c, kc, "nn") + _dot16(ds_p, kp, "nn")
            dk_c, dk_p = _dot16(ds_c, q, "tn"), _dot16(ds_p, q, "tn")
            dv_c, dv_p = _dot16(p_c, dob, "tn"), _dot16(p_p, dob, "tn")
            pair = lambda x, bb: jnp.concatenate([x[2 * bb], x[2 * bb + 1]], axis=1)
            for bb, (cur, prv) in enumerate(rows):
                dq_acc[cur, :] = pair(dq, bb)
                dk_acc[cur, :] += pair(dk_c, bb)
                dv_acc[cur, :] += pair(dv_c, bb)
                dk_acc[prv, :] += pair(dk_p, bb)
                dv_acc[prv, :] += pair(dv_p, bb)
            return carry

        lax.fori_loop(0, S // QBLK // ATT_NB, group, 0)
        dq_ref[...] = dq_acc[...].astype(BF16)
        dk_ref[...] = dk_acc[...].astype(BF16)
        dv_ref[...] = dv_acc[...].astype(BF16)

    slab = lambda j: pl.BlockSpec((S, LANES), lambda i: (0, _att_slab(g, j) + i))
    blk128 = pl.BlockSpec((S, LANES), lambda i: (0, i))
    shp = jax.ShapeDtypeStruct((S, ATT_W), BF16)
    return pl.pallas_call(body, name=f"att_bwd_g{g}", grid=(ATT_PAIRS,),
                          in_specs=[slab(0), slab(1), slab(2)] + [blk128] * 4, out_specs=[blk128] * 3, out_shape=[shp] * 3,
                          scratch_shapes=[pltpu.VMEM((S, LANES), F32)] * 3,
                          compiler_params=_cparams(("parallel",)))(P, P, P, o, l, do, dl)


def _att_weights(l_refs):
    l0, l1, l2 = [r[...] for r in l_refs]
    m = jnp.maximum(jnp.maximum(l0, l1), l2)
    e = (jnp.exp(l0 - m), jnp.exp(l1 - m), jnp.exp(l2 - m))
    inv = 1.0 / (e[0] + e[1] + e[2])
    return [x * inv for x in e]


def _att_combine_fwd(os, ls, tm=512):
    S = os[0].shape[0]

    def body(o0, o1, o2, l0, l1, l2, a_ref):
        w = _att_weights((l0, l1, l2))
        a_ref[...] = (w[0] * o0[...] + w[1] * o1[...] + w[2] * o2[...]).astype(BF16)

    row = _rows(tm, ATT_W)
    return pl.pallas_call(body, name="att_combine_fwd", grid=(S // tm,), in_specs=[row] * 6, out_specs=row,
                          out_shape=jax.ShapeDtypeStruct((S, ATT_W), BF16),
                          compiler_params=_cparams(("parallel",)))(*os, *ls)


def _att_combine_bwd(da, os, ls, tm=512):
    S = da.shape[0]

    def body(da_ref, o0, o1, o2, l0, l1, l2, *out_refs):
        da = da_ref[...]
        w = _att_weights((l0, l1, l2))
        dw = (da * o0[...], da * o1[...], da * o2[...])
        mean = w[0] * dw[0] + w[1] * dw[1] + w[2] * dw[2]
        for g in range(3):
            out_refs[g][...] = w[g] * da
            out_refs[3 + g][...] = w[g] * (dw[g] - mean)

    row = _rows(tm, ATT_W)
    shp = jax.ShapeDtypeStruct((S, ATT_W), F32)
    return pl.pallas_call(body, name="att_combine_bwd", grid=(S // tm,), in_specs=[row] * 7, out_specs=[row] * 6,
                          out_shape=[shp] * 6, compiler_params=_cparams(("parallel",)))(da, *os, *ls)


@jax.custom_vjp
def _bdot(a, b):
    return jnp.dot(a.astype(BF16), b.astype(BF16), preferred_element_type=F32)


def _bdot_fwd(a, b):
    return _bdot(a, b), (a, b)


def _bdot_bwd(res, ct):
    a, b = res
    ct16 = ct.astype(BF16)
    da = lax.dot_general(ct16, b.astype(BF16), (((1,), (1,)), ((), ())), preferred_element_type=F32)
    db = lax.dot_general(a.astype(BF16), ct16, (((0,), (0,)), ((), ())), preferred_element_type=F32)
    return da, db


_bdot.defvjp(_bdot_fwd, _bdot_bwd)


def _two_piece_dot(x, m):
    hi = x.astype(BF16)
    lo = (x - hi.astype(F32)).astype(BF16)
    return jnp.dot(hi, m, preferred_element_type=F32) + jnp.dot(lo, m, preferred_element_type=F32)


def _head_sum_impl(x):
    sel = (lax.broadcasted_iota(jnp.int32, (D, LANES), 0) // HEAD == lax.broadcasted_iota(jnp.int32, (D, LANES), 1)).astype(BF16)
    sel_t = (lax.broadcasted_iota(jnp.int32, (LANES, D), 1) // HEAD == lax.broadcasted_iota(jnp.int32, (LANES, D), 0)).astype(BF16)
    return _two_piece_dot(_two_piece_dot(x, sel), sel_t)


@jax.custom_vjp
def _head_sum(x):
    return _head_sum_impl(x)


_head_sum.defvjp(lambda x: (_head_sum_impl(x), None), lambda _, ct: (_head_sum_impl(ct),))


def _softplus(z):
    return jnp.maximum(z, 0.0) + jnp.log(1.0 + jnp.exp(-jnp.abs(z)))


def _rwkv_prep_fn(zr, zrp, zk, zkp, zv, zvp, zl, zlp, mu_r, mu_k, mu_v, mu_l, w0, a0, k_k, k_a, w2, a2, g2p):
    r = zr + (zrp - zr) * mu_r
    k = zk + (zkp - zk) * mu_k
    v = zv + (zvp - zv) * mu_v
    lo = zl + (zlp - zl) * mu_l
    w_low, a_low, g_low = lo[:, 0:LORA_W], lo[:, LORA_W:LORA_W + LORA_A], lo[:, LANES:LANES + G_PAD]
    w_log = -_softplus(-(w0 + _bdot(jnp.tanh(w_low), w2))) - 0.5
    decay = -jnp.exp(w_log)
    a = jax.nn.sigmoid(a0 + _bdot(a_low, a2))
    g = _bdot(jax.nn.sigmoid(g_low), g2p)
    kmod = k * (1.0 + (a - 1.0) * k_a)
    kk = k * k_k
    kk = kk / jnp.maximum(jnp.sqrt(_head_sum(kk * kk)), 1e-12)
    return r, decay, kmod, v, -kk, kk * a, g


def _rwkv_prep_specs(tm):
    vec = _full((1, D))
    slabs = []
    for col in (C_R // D, C_K // D, C_V // D):
        slabs += [_rows(tm, D, col), _prev8(tm, D, col)]
    slabs += [_rows(tm, LORA_PAD, C_LORA // LORA_PAD), _prev8(tm, LORA_PAD, C_LORA // LORA_PAD)]
    params = [vec, vec, vec, _full((1, LORA_PAD)), vec, vec, vec, vec,
              _full((LORA_W, D)), _full((LORA_A, D)), _full((G_PAD, D))]
    return slabs, params


def _prep_inputs(refs, first):
    vals = []
    for s in range(4):
        z = refs[2 * s][...]
        vals += [z, _shift_down(z, refs[2 * s + 1][...], 1, first)]
    return vals + [r[...] for r in refs[8:19]]


def _rwkv_prep(P, params, tm=256):
    S = P.shape[0]
    slabs, pspecs = _rwkv_prep_specs(tm)

    def body(*refs):
        outs = _rwkv_prep_fn(*_prep_inputs(refs, pl.program_id(0) == 0))
        for o_ref, val in zip(refs[19:], outs):
            o_ref[...] = val

    shp = jax.ShapeDtypeStruct((S, D), F32)
    return pl.pallas_call(body, name="rwkv_prep", grid=(S // tm,), in_specs=slabs + pspecs,
                          out_specs=[_rows(tm, D)] * 7, out_shape=[shp] * 7,
                          compiler_params=_cparams(("parallel",)))(*([P] * 8), *params)


def _rwkv_prep_bwd(P, params, cts_a, cts_b, tm=128):
    S = P.shape[0]
    slabs, pspecs = _rwkv_prep_specs(tm)
    has_b = [c is not None for c in cts_b]
    n_ct = 7 + sum(has_b)

    def body(*refs):
        first = pl.program_id(0) == 0
        ins = _prep_inputs(refs, first)
        ct_refs = refs[19:19 + n_ct]
        out_refs = refs[19 + n_ct:]
        cts, pos = [], 7
        for i in range(7):
            c = ct_refs[i][...]
            if has_b[i]:
                c = c + ct_refs[pos][...]
                pos += 1
            cts.append(c)
        _, vjp = jax.vjp(_rwkv_prep_fn, *ins)
        grads = vjp(tuple(cts))
        for s in range(4):
            out_refs[s][...] = grads[2 * s]
            out_refs[4 + s][...] = grads[2 * s + 1]
        for i in range(11):
            _acc(out_refs[8 + i], grads[8 + i], first)

    ct_in = list(cts_a) + [c for c in cts_b if c is not None]
    row, lrow = _rows(tm, D), _rows(tm, LORA_PAD)
    f = jax.ShapeDtypeStruct
    zshapes = [f((S, D), F32)] * 3 + [f((S, LORA_PAD), F32)]
    pshapes = [f((1, D), F32)] * 3 + [f((1, LORA_PAD), F32)] + [f((1, D), F32)] * 4 + [f((LORA_W, D), F32), f((LORA_A, D), F32), f((G_PAD, D), F32)]
    return pl.pallas_call(
        body, name="rwkv_prep_bwd", grid=(S // tm,),
        in_specs=slabs + pspecs + [row] * n_ct,
        out_specs=[row, row, row, lrow] * 2 + pspecs,
        out_shape=zshapes * 2 + pshapes,
        compiler_params=_cparams(("arbitrary",)))(*([P] * 8), *params, *ct_in)


def _shift_add(a, b, tm=256):
    S, W = a.shape

    def body(a_ref, b_ref, h_ref, o_ref):
        last = pl.program_id(0) == pl.num_programs(0) - 1
        o_ref[...] = (a_ref[...] + _shift_up(b_ref[...], h_ref[...], 1, last)).astype(BF16)

    return pl.pallas_call(body, name="shift_add", grid=(S // tm,),
                          in_specs=[_rows(tm, W), _rows(tm, W), _next8(tm, W, S)],
                          out_specs=_rows(tm, W), out_shape=jax.ShapeDtypeStruct((S, W), BF16),
                          compiler_params=_cparams(("parallel",)))(a, b, b)


def _rwkv_post_fn(y, r, kmod, v, g, lnx_w, lnx_b, r_k):
    mean = _head_sum(y) * (1.0 / HEAD)
    yc = y - mean
    var = _head_sum(yc * yc) * (1.0 / HEAD)
    yn = yc * lax.rsqrt(var + GN_EPS) * lnx_w + lnx_b
    bonus = _head_sum(r * kmod * r_k) * v
    return (yn + bonus) * g


def _rwkv_post(y, r, kmod, v, g, lnx_w, lnx_b, r_k, tm=256):
    S = y.shape[0]

    def body(y_ref, r_ref, k_ref, v_ref, g_ref, w_ref, b_ref, rk_ref, o_ref):
        o_ref[...] = _rwkv_post_fn(y_ref[...], r_ref[...], k_ref[...], v_ref[...], g_ref[...],
                                   w_ref[...], b_ref[...], rk_ref[...]).astype(BF16)

    row, vec = _rows(tm, D), _full((1, D))
    return pl.pallas_call(body, name="rwkv_post", grid=(S // tm,), in_specs=[row] * 5 + [vec] * 3, out_specs=row,
                          out_shape=jax.ShapeDtypeStruct((S, D), BF16),
                          compiler_params=_cparams(("parallel",)))(y, r, kmod, v, g, lnx_w, lnx_b, r_k)


def _rwkv_post_bwd(drw, y, r, kmod, v, g, lnx_w, lnx_b, r_k, tm=256):
    S = y.shape[0]

    def body(d_ref, y_ref, r_ref, k_ref, v_ref, g_ref, w_ref, b_ref, rk_ref, *out_refs):
        first = pl.program_id(0) == 0
        _, vjp = jax.vjp(_rwkv_post_fn, y_ref[...], r_ref[...], k_ref[...], v_ref[...], g_ref[...],
                         w_ref[...], b_ref[...], rk_ref[...])
        grads = vjp(d_ref[...])
        for i in range(5):
            out_refs[i][...] = grads[i]
        for i in range(5, 8):
            _acc(out_refs[i], grads[i], first)

    row, vec = _rows(tm, D), _full((1, D))
    f = jax.ShapeDtypeStruct
    return pl.pallas_call(body, name="rwkv_post_bwd", grid=(S // tm,), in_specs=[row] * 6 + [vec] * 3,
                          out_specs=[row] * 5 + [vec] * 3, out_shape=[f((S, D), F32)] * 5 + [f((1, D), F32)] * 3,
                          compiler_params=_cparams(("arbitrary",)))(drw, y, r, kmod, v, g, lnx_w, lnx_b, r_k)


CHUNK = 64
CHUNK_TB = 256
_DOT_DIMS = {"nn": (((2,), (1,)), ((0,), (0,))), "nt": (((2,), (2,)), ((0,), (0,))), "tn": (((1,), (1,)), ((0,), (0,)))}


def _dot16(x, y, mode):
    return lax.dot_general(x.astype(BF16), y.astype(BF16), _DOT_DIMS[mode], preferred_element_type=F32)


@functools.partial(jax.custom_vjp, nondiff_argnums=(2,))
def _mm16(x, y, mode):
    return _dot16(x, y, mode)


def _mm16_fwd(x, y, mode):
    return _dot16(x, y, mode), (x, y)


def _mm16_bwd(mode, res, ct):
    x, y = res
    if mode == "nn":
        return _dot16(ct, y, "nt"), _dot16(x, ct, "tn")
    if mode == "nt":
        return _dot16(ct, y, "nn"), _dot16(ct, x, "tn")
    return _dot16(y, ct, "nt"), _dot16(x, ct, "nn")


_mm16.defvjp(_mm16_fwd, _mm16_bwd)


def _tri_sum(x, upper):
    T = x.shape[0]
    i = lax.broadcasted_iota(jnp.int32, (T, T), 0)
    j = lax.broadcasted_iota(jnp.int32, (T, T), 1)
    tri = ((j >= i) if upper else (i >= j)).astype(BF16)
    out, rest = None, x
    for _ in range(3):
        piece = rest.astype(BF16)
        rest = rest - piece.astype(F32)
        part = jnp.dot(tri, piece, preferred_element_type=F32)
        out = part if out is None else out + part
    return out


@jax.custom_vjp
def _cumsum_rows(x):
    return _tri_sum(x, False)


_cumsum_rows.defvjp(lambda x: (_tri_sum(x, False), None), lambda _, ct: (_tri_sum(ct, True),))


def _rows_to_cols(x):
    H, _, K = x.shape
    eye = (lax.broadcasted_iota(jnp.int32, (H, K, K), 1) == lax.broadcasted_iota(jnp.int32, (H, K, K), 2)).astype(F32)
    out = lax.dot_general(eye, jnp.broadcast_to(x, (H, SUBLANES, K)), _DOT_DIMS["nt"],
                          precision=lax.Precision.HIGHEST, preferred_element_type=F32)
    return out[:, :, 0:1]


def _per_head(x):
    return jnp.concatenate([x[:, h * HEAD:(h + 1) * HEAD][None] for h in range(N_HEADS)], axis=0)


def _chunk_fn(st0, r, lw, k, v, a, b):
    T = r.shape[0]
    cl = _cumsum_rows(lw)
    cl_end = cl[T - 1:T, :]
    inv = jnp.exp(-cl)
    to_end = jnp.exp(cl_end - cl)
    ah, rh, bh, kh, be, ke, v3 = [_per_head(x) for x in
                                  (a * jnp.exp(cl - lw), r * jnp.exp(cl), b * inv, k * inv, b * to_end, k * to_end, v)]
    i = lax.broadcasted_iota(jnp.int32, (N_HEADS, T, T), 1)
    j = lax.broadcasted_iota(jnp.int32, (N_HEADS, T, T), 2)
    a_ab = jnp.where(i > j, _mm16(ah, bh, "nt"), 0.0)
    a_ak = jnp.where(i > j, _mm16(ah, kh, "nt"), 0.0)
    m_rb = jnp.where(i >= j, _mm16(rh, bh, "nt"), 0.0)
    m_rk = jnp.where(i >= j, _mm16(rh, kh, "nt"), 0.0)
    rhs = _mm16(ah, st0, "nn") + _mm16(a_ak, v3, "nn")
    power, solve, n = a_ab, (i == j).astype(F32) + a_ab, 1
    while 2 * n < T:
        power = _mm16(power, power, "nn")
        solve = solve + _mm16(solve, power, "nn")
        n *= 2
    sa = _mm16(solve, rhs, "nn")
    y3 = _mm16(rh, st0, "nn") + _mm16(m_rb, sa, "nn") + _mm16(m_rk, v3, "nn")
    st_end = _rows_to_cols(_per_head(jnp.exp(cl_end))) * st0 + _mm16(be, sa, "tn") + _mm16(ke, v3, "tn")
    return jnp.concatenate([y3[h] for h in range(N_HEADS)], axis=1), st_end


def _hosted_exchange(refs, n, broadcast, grid):
    if n == 0:
        return lambda: None
    start, wait = _exchange_ops(refs[:n], refs[n:2 * n], *refs[2 * n:], broadcast)
    first = functools.reduce(jnp.logical_and, [pl.program_id(a) == 0 for a in range(len(grid))])
    last = functools.reduce(jnp.logical_and, [pl.program_id(a) == g - 1 for a, g in enumerate(grid)])
    pl.when(first)(start)
    return lambda: pl.when(last)(wait)


def _cscan_fwd(r, lw, k, v, a, b, gather=()):
    S = r.shape[0]
    per_blk = CHUNK_TB // CHUNK
    n_x = len(gather)
    nblk = S // CHUNK_TB

    def body(*refs):
        r_ref, lw_ref, k_ref, v_ref, a_ref, b_ref = refs[:6]
        y_ref, ck_ref = refs[6 + n_x:8 + n_x]
        st_ref = refs[8 + 2 * n_x]
        finish = _hosted_exchange(refs[6:6 + n_x] + refs[8 + n_x:8 + 2 * n_x] + refs[9 + 2 * n_x:], n_x, True, (nblk,))

        @pl.when(pl.program_id(0) == 0)
        def _():
            st_ref[...] = jnp.zeros_like(st_ref)

        def chunk(c, carry):
            rows = pl.ds(pl.multiple_of(c * CHUNK, CHUNK), CHUNK)
            st0 = st_ref[...]
            ck_ref[c] = st0
            y, st_end = _chunk_fn(st0, r_ref[rows, :], lw_ref[rows, :], k_ref[rows, :],
                                  v_ref[rows, :], a_ref[rows, :], b_ref[rows, :])
            y_ref[rows, :] = y
            st_ref[...] = st_end
            return carry

        lax.fori_loop(0, per_blk, chunk, 0)
        finish()

    blk = _rows(CHUNK_TB, D)
    any_spec = pl.BlockSpec(memory_space=pl.ANY)
    outs = pl.pallas_call(
        body, name="scan_fwd", grid=(nblk,), in_specs=[blk] * 6 + [any_spec] * n_x,
        out_specs=[blk, pl.BlockSpec((per_blk, N_HEADS, HEAD, HEAD), lambda i: (i, 0, 0, 0))] + [any_spec] * n_x,
        out_shape=[jax.ShapeDtypeStruct((S, D), F32), jax.ShapeDtypeStruct((S // CHUNK, N_HEADS, HEAD, HEAD), F32)]
        + _exchange_shapes(gather, True),
        scratch_shapes=[pltpu.VMEM((N_HEADS, HEAD, HEAD), F32)] + (_exchange_scratch(n_x) if n_x else []),
        compiler_params=_cparams(("arbitrary",)))(r, lw, k, v, a, b, *gather)
    return outs[0], outs[1], outs[2:]


def _cscan_bwd(r, lw, k, v, a, b, ckpt, dy, scatter=()):
    S = r.shape[0]
    per_blk = CHUNK_TB // CHUNK
    nblk = S // CHUNK_TB
    n_x = len(scatter)

    def body(*refs):
        r_ref, lw_ref, k_ref, v_ref, a_ref, b_ref, ck_ref, dy_ref = refs[:8]
        out_refs = refs[8 + n_x:14 + n_x]
        ds_ref = refs[14 + 2 * n_x]
        finish = _hosted_exchange(refs[8:8 + n_x] + refs[14 + n_x:14 + 2 * n_x] + refs[15 + 2 * n_x:], n_x, False, (nblk,))

        @pl.when(pl.program_id(0) == 0)
        def _():
            ds_ref[...] = jnp.zeros_like(ds_ref)

        def chunk(cc, carry):
            c = per_blk - 1 - cc
            rows = pl.ds(pl.multiple_of(c * CHUNK, CHUNK), CHUNK)
            ins = (ck_ref[c], r_ref[rows, :], lw_ref[rows, :], k_ref[rows, :], v_ref[rows, :], a_ref[rows, :], b_ref[rows, :])
            _, vjp = jax.vjp(_chunk_fn, *ins)
            grads = vjp((dy_ref[rows, :], ds_ref[...]))
            ds_ref[...] = grads[0]
            for o_ref, g in zip(out_refs, grads[1:]):
                o_ref[rows, :] = g
            return carry

        lax.fori_loop(0, per_blk, chunk, 0)
        finish()

    blk = pl.BlockSpec((CHUNK_TB, D), lambda i: (nblk - 1 - i, 0))
    any_spec = pl.BlockSpec(memory_space=pl.ANY)
    shp = jax.ShapeDtypeStruct((S, D), F32)
    outs = pl.pallas_call(
        body, name="scan_bwd", grid=(nblk,),
        in_specs=[blk] * 6 + [pl.BlockSpec((per_blk, N_HEADS, HEAD, HEAD), lambda i: (nblk - 1 - i, 0, 0, 0)), blk]
        + [any_spec] * n_x,
        out_specs=[blk] * 6 + [any_spec] * n_x, out_shape=[shp] * 6 + _exchange_shapes(scatter, False),
        scratch_shapes=[pltpu.VMEM((N_HEADS, HEAD, HEAD), F32)] + (_exchange_scratch(n_x) if n_x else []),
        compiler_params=_cparams(("arbitrary",)))(r, lw, k, v, a, b, ckpt, dy, *scatter)
    return outs[:6], outs[6:]


def _ada_partial(c_all, w_shard):
    def body(c_ref, w_ref, o_ref):
        o_ref[...] = jnp.dot(c_ref[...].astype(BF16), w_ref[...].astype(BF16), preferred_element_type=F32)

    vm = pl.BlockSpec(memory_space=pltpu.VMEM)
    return pl.pallas_call(body, name="ada_partial", in_specs=[vm, vm], out_specs=vm,
                          out_shape=jax.ShapeDtypeStruct((N_DEV, w_shard.shape[1]), F32),
                          compiler_params=pltpu.CompilerParams(vmem_limit_bytes=VMEM_LIMIT))(c_all, w_shard)


def _ada_bias(rows, b_ada):
    def body(r_ref, b_ref, o_ref):
        o_ref[...] = r_ref[...] + b_ref[...]

    vm = pl.BlockSpec(memory_space=pltpu.VMEM)
    return pl.pallas_call(body, name="ada_bias", in_specs=[vm, vm], out_specs=vm,
                          out_shape=jax.ShapeDtypeStruct(rows.shape, F32))(rows, b_ada)


def _ada_wgrad(c_cols, d_all):
    def body(c_ref, d_ref, o_ref):
        acc = c_ref[:, 0:1] * d_ref[0:1, :]
        for j in range(1, N_DEV):
            acc = acc + c_ref[:, j:j + 1] * d_ref[j:j + 1, :]
        o_ref[...] = acc

    vm = pl.BlockSpec(memory_space=pltpu.VMEM)
    return pl.pallas_call(body, name="ada_wgrad", in_specs=[vm, vm], out_specs=vm,
                          out_shape=jax.ShapeDtypeStruct((D, d_all.shape[1]), F32),
                          compiler_params=pltpu.CompilerParams(vmem_limit_bytes=VMEM_LIMIT))(c_cols, d_all)


def _exchange(srcs, broadcast, name):
    n = len(srcs)

    def body(*refs):
        start, wait = _exchange_ops(refs[:n], refs[n:2 * n], *refs[2 * n:], broadcast)
        start()
        wait()

    any_spec = pl.BlockSpec(memory_space=pl.ANY)
    return pl.pallas_call(
        body, name=name, out_shape=_exchange_shapes(srcs, broadcast), in_specs=[any_spec] * n, out_specs=[any_spec] * n,
        scratch_shapes=_exchange_scratch(n),
        compiler_params=pltpu.CompilerParams(has_side_effects=True),
    )(*srcs)


def _flags(broadcast, n):
    return [broadcast] * n if isinstance(broadcast, bool) else list(broadcast)


def _exchange_shapes(srcs, broadcast):
    return [jax.ShapeDtypeStruct((N_DEV,) + (s.shape if bc else s.shape[1:]), s.dtype)
            for s, bc in zip(srcs, _flags(broadcast, len(srcs)))]


def _exchange_scratch(n):
    return [pltpu.SemaphoreType.DMA((n, N_DEV)), pltpu.SemaphoreType.DMA((n, N_DEV)), pltpu.SemaphoreType.DMA((n,))]


def _exchange_ops(src_refs, out_refs, send_sems, recv_sems, local_sems, broadcast):
    n = len(src_refs)
    flags = _flags(broadcast, n)
    x, y, c = lax.axis_index("x"), lax.axis_index("y"), lax.axis_index("c")
    me = 4 * x + 2 * y + c

    def block(i, j):
        return src_refs[i] if flags[i] else src_refs[i].at[j]

    def remote(i, d, src_slot, dst_slot):
        px, py, pc = x ^ (d >> 2), y ^ ((d >> 1) & 1), c ^ (d & 1)
        return pltpu.make_async_remote_copy(
            src_ref=block(i, src_slot), dst_ref=out_refs[i].at[dst_slot], send_sem=send_sems.at[i, d],
            recv_sem=recv_sems.at[i, d], device_id=(px, py, pc), device_id_type=_MESH)

    def local(i):
        return pltpu.make_async_copy(block(i, me), out_refs[i].at[me], local_sems.at[i])

    def start():
        for i in range(n):
            local(i).start()
        for d in range(1, N_DEV):
            for i in range(n):
                remote(i, d, me ^ d, me).start()

    def wait():
        for d in range(1, N_DEV):
            for i in range(n):
                remote(i, d, me, me ^ d).wait_recv()
        for d in range(1, N_DEV):
            for i in range(n):
                remote(i, d, me ^ d, me).wait_send()
        for i in range(n):
            local(i).wait()

    return start, wait


def _sum_adam(parts, w, m, v, name):
    n_parts, R, C = parts.shape
    tm = 256 if R % 256 == 0 else R
    c1 = 1.0 / (1.0 - ADAM_B1 ** ADAM_STEP)
    c2 = 1.0 / (1.0 - ADAM_B2 ** ADAM_STEP)

    def body(p_ref, w_ref, m_ref, v_ref, g_ref, d_ref, nm_ref, nv_ref):
        g = p_ref[0].astype(F32)
        for j in range(1, n_parts):
            g = g + p_ref[j].astype(F32)
        nm = ADAM_B1 * m_ref[...] + (1.0 - ADAM_B1) * g
        nv = ADAM_B2 * v_ref[...] + (1.0 - ADAM_B2) * (g * g)
        g_ref[...] = g
        nm_ref[...] = nm
        nv_ref[...] = nv
        d_ref[...] = -ADAM_LR * ((nm * c1) / (jnp.sqrt(nv * c2) + ADAM_EPS) + ADAM_WD * w_ref[...])

    row = _rows(tm, C)
    shp = jax.ShapeDtypeStruct((R, C), F32)
    return pl.pallas_call(body, name=name, grid=(R // tm,),
                          in_specs=[pl.BlockSpec((n_parts, tm, C), lambda i: (0, i, 0)), row, row, row],
                          out_specs=[row] * 4, out_shape=[shp] * 4,
                          compiler_params=_cparams(("parallel",)))(parts, w, m, v)


PACK_ALIGN = 16 * LANES
PACK_ROWS = 512 * LANES

SHARDED = (("w_ada", 1), ("w_in", 1), ("w2", 1), ("a2", 1), ("g2", 1), ("w_att_out", 1), ("w_rwkv_out", 0),
           ("w_o", 0), ("w_up", 1), ("conv_w", 1), ("w_down", 0))
EARLY, LATE = SHARDED[1:5], SHARDED[5:]
REPLICATED = ("b_ada", "norm1_w", "b_gate", "mu_shift", "w0", "a0", "k_k", "k_a", "r_k", "lnx_w", "lnx_b",
              "norm2_w", "conv_b", "norm_f_w")
WEIGHTS = ("w_ada", "b_ada", "norm1_w", "w_in", "b_gate", "mu_shift", "w0", "w2", "a0", "a2", "g2", "k_k", "k_a", "r_k",
           "lnx_w", "lnx_b", "w_att_out", "w_rwkv_out", "w_o", "norm2_w", "w_up", "conv_w", "conv_b", "w_down", "norm_f_w")


def _pack(arrays):
    flat, layout, off = [], [], 0
    for i, a in enumerate(arrays):
        n = a.size
        pad = (-n) % PACK_ALIGN if i + 1 < len(arrays) else (-(off + n)) % PACK_ROWS
        flat.append(a.reshape(-1))
        if pad:
            flat.append(jnp.zeros((pad,), a.dtype))
        layout.append((off, n, a.shape))
        off += n + pad
    return jnp.concatenate(flat).reshape(-1, LANES), layout


def _unpack(buf, layout):
    flat = buf.reshape(-1)
    return [flat[off:off + n].reshape(shape) for off, n, shape in layout]


def _pad_w_in(w_in):
    rkv = w_in[:, ATT_IN:ATT_IN + 3 * D]
    lora = w_in[:, ATT_IN + 3 * D:ATT_IN + RWKV_IN]
    gates = w_in[:, ATT_IN + RWKV_IN:]
    att = w_in[:, :ATT_IN]
    lw, la, lg = lora[:, :LORA_W], lora[:, LORA_W:LORA_W + LORA_A], lora[:, LORA_W + LORA_A:]
    zeros = jnp.zeros((w_in.shape[0], LORA_PAD - LANES - LORA_G), w_in.dtype)
    return jnp.concatenate([rkv, gates, att, lw, la, lg, zeros], axis=1)


def _unpad_w_in(g):
    att = g[:, C_ATT:C_ATT + ATT_IN]
    rkv = g[:, C_R:C_R + 3 * D]
    lora = jnp.concatenate([g[:, C_LORA:C_LORA + LORA_W + LORA_A], g[:, C_LORA + LANES:C_LORA + LANES + LORA_G]], axis=1)
    gates = g[:, C_GA:C_GA + 2 * D]
    return jnp.concatenate([att, rkv, lora, gates], axis=1)


def _pad_mu(mu):
    lo = mu[:, 3 * D:]
    mu_l = jnp.concatenate([lo[:, :LORA_W + LORA_A], lo[:, LORA_W + LORA_A:], jnp.zeros((1, LORA_PAD - LANES - LORA_G), mu.dtype)], axis=1)
    return mu[:, :D], mu[:, D:2 * D], mu[:, 2 * D:3 * D], mu_l


def _local_step(x, ada, W, late_shards, target):
    S = x.shape[0]
    W = dict(W)
    G = {}
    sh1, sc1, gt1, sh2, sc2, gt2 = [ada[:, i * D:(i + 1) * D] for i in range(6)]
    h1, rstd1 = _norm_fwd(x, None, None, W["norm1_w"], sc1, sh1, "norm1_fwd")
    w_in_p = _pad_w_in(W["w_in"])
    P = _mm(h1, w_in_p, "nn", F32, "proj_in")

    mu_r, mu_k, mu_v, mu_l = _pad_mu(W["mu_shift"])
    g2p = jnp.pad(W["g2"], ((0, G_PAD - LORA_G), (0, 0)))
    prep_params = [mu_r, mu_k, mu_v, mu_l, W["w0"], W["a0"], W["k_k"], W["k_a"], W["w2"], W["a2"], g2p]
    r_, dec, kmod, v_, aa, bb, gg = _rwkv_prep(P, prep_params)
    y_scan, states, late = _cscan_fwd(r_, dec, kmod, v_, aa, bb, gather=late_shards)
    W.update({n: _full_weight(g, axis) for (n, axis), g in zip(LATE, late)})

    o_g, l_g = zip(*[_att_fwd(P, g) for g in range(len(ATT_PATTERNS))])
    att = _att_combine_fwd(o_g, l_g)
    y_att = _mm(att, W["w_att_out"], "nn", F32, "att_out")
    r_k = W["r_k"].reshape(1, D)
    rw = _rwkv_post(y_scan, r_, kmod, v_, gg, W["lnx_w"], W["lnx_b"], r_k)
    y_rwkv = _mm(rw, W["w_rwkv_out"], "nn", F32, "rwkv_out")

    bga, bgr = W["b_gate"][:, :D], W["b_gate"][:, D:]
    mix = _gate_fwd(P, bga, bgr, y_att, y_rwkv)
    mo = _mm(mix, W["w_o"], "nn", F32, "mix_out")
    x2, h2, rstd2 = _norm_fwd(x, mo, gt1, W["norm2_w"], sc2, sh2, "norm2_fwd")
    u = _mm(h2, W["w_up"], "nn", F32, "ffn_up")
    conv_w8 = jnp.pad(W["conv_w"], ((0, SUBLANES - 3), (0, 0)))
    act = _conv_fwd(u, conv_w8, W["conv_b"])
    f = _mm(act, W["w_down"], "nn", F32, "ffn_down")
    loss_blk, dx3, df, dgt2, G["norm_f_w"] = _final(x2, f, gt2, W["norm_f_w"], target)
    loss = loss_blk[0, 0]

    dact = _mm(df, W["w_down"], "nt", BF16, "ffn_down_dx")
    G["w_down"] = _mm(act, df, "tn", F32, "ffn_down_dw")
    duc, dwg, dwv, dbg, dbv = _conv_bwd_a(dact, u, conv_w8, W["conv_b"])
    G["conv_w"] = jnp.concatenate([dwg[0:3], dwv[0:3]], axis=1)
    G["conv_b"] = jnp.concatenate([dbg, dbv], axis=1)
    du = _conv_bwd_b(duc, conv_w8)
    dh2 = _mm(du, W["w_up"], "nt", F32, "ffn_up_dx")
    G["w_up"] = _mm(h2, du, "tn", F32, "ffn_up_dw")
    dx2, dsh2, dsc2, G["norm2_w"], dmo, dgt1 = _norm_bwd(dh2, x2, rstd2, W["norm2_w"], sc2, dx3, mo, gt1, "norm2_bwd")
    dmix = _mm(dmo, W["w_o"], "nt", F32, "mix_out_dx")
    G["w_o"] = _mm(mix, dmo, "tn", F32, "mix_out_dw")
    dy_att, dy_rwkv, dpga, dpgr, dbga, dbgr = _gate_bwd(dmix, P, bga, bgr, y_att, y_rwkv)
    G["b_gate"] = jnp.concatenate([dbga, dbgr], axis=1)

    datt = _mm(dy_att, W["w_att_out"], "nt", F32, "att_out_dx")
    G["w_att_out"] = _mm(att, dy_att, "tn", F32, "att_out_dw")
    dcomb = _att_combine_bwd(datt, o_g, l_g)
    dp_att = []
    for g in range(len(ATT_PATTERNS)):
        dp_att += _att_bwd(P, o_g[g], l_g[g], dcomb[g], dcomb[3 + g], g)

    drw = _mm(dy_rwkv, W["w_rwkv_out"], "nt", F32, "rwkv_out_dx")
    G["w_rwkv_out"] = _mm(rw, dy_rwkv, "tn", F32, "rwkv_out_dw")
    dy_scan, dr1, dk1, dv1, dgg, G["lnx_w"], G["lnx_b"], drk = _rwkv_post_bwd(drw, y_scan, r_, kmod, v_, gg, W["lnx_w"], W["lnx_b"], r_k)
    G["r_k"] = drk.reshape(W["r_k"].shape)
    late_blocks = [_owner_blocks(G[n], axis) for n, axis in LATE] if late_shards else []
    (dr2, ddec, dk2, dv2, daa, dbb), late_parts = _cscan_bwd(r_, dec, kmod, v_, aa, bb, states, dy_scan, scatter=late_blocks)
    pb = _rwkv_prep_bwd(P, prep_params, [dr2, ddec, dk2, dv2, daa, dbb, dgg], [dr1, None, dk1, dv1, None, None, None])
    dz, dzp, dpar = pb[0:4], pb[4:8], pb[8:]
    dp_rkv = [_shift_add(dz[i], dzp[i]) for i in range(3)]
    dp_lora = _shift_add(dz[3], dzp[3])
    dmu_r, dmu_k, dmu_v, dmu_l, G["w0"], G["a0"], G["k_k"], G["k_a"], G["w2"], G["a2"], dg2p = dpar
    G["g2"] = dg2p[0:LORA_G]
    G["mu_shift"] = jnp.concatenate([dmu_r, dmu_k, dmu_v, dmu_l[:, :LORA_W + LORA_A], dmu_l[:, LANES:LANES + LORA_G]], axis=1)

    dP = jnp.concatenate(dp_rkv + [dpga, dpgr] + dp_att + [dp_lora], axis=1)
    G["w_in"] = _unpad_w_in(_mm(h1, dP, "tn", F32, "proj_in_dw"))
    if late_shards:
        dh1, (w_in_parts,) = _mm(dP, w_in_p, "nt", F32, "proj_in_dx", scatter=[_owner_blocks(G["w_in"], 1)])
        done = dict(zip([n for n, _ in LATE] + ["w_in"], list(late_parts) + [w_in_parts]))
    else:
        dh1, done = _mm(dP, w_in_p, "nt", F32, "proj_in_dx"), {}
    grad_x, dsh1, dsc1, G["norm1_w"] = _norm_bwd(dh1, x, rstd1, W["norm1_w"], sc1, dx2, None, None, "norm1_bwd")
    dada = jnp.concatenate([dsh1, dsc1, dgt1, dsh2, dsc2, dgt2], axis=1)
    G["b_ada"] = dada
    return loss, grad_x, G, done


def _full_weight(gathered, axis):
    _, rows, cols = gathered.shape
    if axis == 0:
        return gathered.reshape(N_DEV * rows, cols)
    return gathered.transpose(1, 0, 2).reshape(rows, N_DEV * cols)


def _owner_blocks(g, axis):
    rows, cols = g.shape
    g = g.astype(BF16)
    if axis == 0:
        return g.reshape(N_DEV, rows // N_DEV, cols)
    return g.reshape(rows, N_DEV, cols // N_DEV).transpose(1, 0, 2)


def kernel(x, c, w_ada, b_ada, norm1_w, w_in, b_gate, mu_shift, w0, w2, a0, a2, g2, k_k, k_a, r_k, lnx_w, lnx_b, w_att_out, w_rwkv_out, w_o, norm2_w, w_up, conv_w, conv_b, w_down, norm_f_w, loss_target, m_w_ada, m_b_ada, m_norm1_w, m_w_in, m_b_gate, m_mu_shift, m_w0, m_w2, m_a0, m_a2, m_g2, m_k_k, m_k_a, m_r_k, m_lnx_w, m_lnx_b, m_w_att_out, m_w_rwkv_out, m_w_o, m_norm2_w, m_w_up, m_conv_w, m_conv_b, m_w_down, m_norm_f_w, v_w_ada, v_b_ada, v_norm1_w, v_w_in, v_b_gate, v_mu_shift, v_w0, v_w2, v_a0, v_a2, v_g2, v_k_k, v_k_a, v_r_k, v_lnx_w, v_lnx_b, v_w_att_out, v_w_rwkv_out, v_w_o, v_norm2_w, v_w_up, v_conv_w, v_conv_b, v_w_down, v_norm_f_w):
    env = dict(locals())
    w_shard = {n: env[n] for n in WEIGHTS}
    m_shard = {n: env["m_" + n] for n in WEIGHTS}
    v_shard = {n: env["v_" + n] for n in WEIGHTS}

    c_all, *gathered = _exchange([c] + [w_shard[n][0].astype(BF16) for n, _ in EARLY], True, "gather_weights")
    c_all = c_all.reshape(N_DEV, D)
    W = {n: _full_weight(g, axis) for (n, axis), g in zip(EARLY, gathered)}
    for n in REPLICATED:
        W[n] = w_shard[n].reshape(1, -1) if n != "r_k" else w_shard[n][0]
    ada_cols = _ada_partial(c_all, w_shard["w_ada"][0])
    ada_rows, = _exchange([ada_cols[:, None, :]], False, "ada_rows")
    ada = _ada_bias(ada_rows.reshape(1, -1), w_shard["b_ada"])

    late_shards = [w_shard[n][0].astype(BF16) for n, _ in LATE]
    loss, grad_x, G, parts = _local_step(x[0], ada, W, late_shards, loss_target[0])
    loss = lax.psum(loss, ("x", "y", "c"))

    small, slayout = _pack([G[n].reshape(-1) for n in REPLICATED])
    sparts, dada_all = _exchange([small, G["b_ada"].reshape(N_DEV, 1, -1)], [True, False], "gather_small_grads")
    parts["w_ada"] = _ada_wgrad(c_all.T, dada_all.reshape(N_DEV, -1))[None]

    rest = [(n, axis) for n, axis in SHARDED if n not in parts]
    parts.update(zip([n for n, _ in rest], _exchange([_owner_blocks(G[n], axis) for n, axis in rest], False, "scatter_grads")))
    out = {}
    for n, p in parts.items():
        res = _sum_adam(p, w_shard[n][0], m_shard[n][0], v_shard[n][0], "adam_" + n)
        for kind, a in zip(("grad", "delta", "new_m", "new_v"), res):
            out[kind, n] = a[None]

    sw, _ = _pack([w_shard[n].reshape(-1) for n in REPLICATED])
    sm, _ = _pack([m_shard[n].reshape(-1) for n in REPLICATED])
    sv, _ = _pack([v_shard[n].reshape(-1) for n in REPLICATED])
    res = _sum_adam(sparts, sw, sm, sv, "adam_replicated")
    for kind, buf in zip(("grad", "delta", "new_m", "new_v"), res):
        for n, a in zip(REPLICATED, _unpack(buf, slayout)):
            out[kind, n] = a.reshape(w_shard[n].shape)

    return (loss, grad_x[None], *[out[kind, n] for kind in ("grad", "delta", "new_m", "new_v") for n in WEIGHTS])
```

```python
import functools
import math

import jax
import jax.numpy as jnp
from jax import lax
from jax.experimental import pallas as pl
from jax.experimental.pallas import tpu as pltpu

F32 = jnp.float32
BF16 = jnp.bfloat16

D = 1024
HEAD = 64
ATT_PATTERNS = ((128, 1), (512, 4), (2048, 16))
ATT_HEADS = 8
ATT_W = ATT_HEADS * HEAD
ATT_IN = 3 * 3 * ATT_W
QBLK = 128
N_HEADS = D // HEAD
LORA_W, LORA_A, LORA_G = 64, 64, 160
RWKV_IN = 3 * D + LORA_W + LORA_A + LORA_G
N_IN = ATT_IN + RWKV_IN + 2 * D
D_FF = 2816
RMS_EPS = 1e-6
GN_EPS = 64e-5
N_DEV = 8
LANES = 128
SUBLANES = 8

C_R, C_K, C_V, C_GA, C_GR = 0, 1024, 2048, 3072, 4096
C_ATT = 5120
C_LORA = C_ATT + ATT_IN
LORA_PAD = 512
G_PAD = 256
N_PAD = C_LORA + LORA_PAD

ADAM_LR, ADAM_B1, ADAM_B2, ADAM_EPS, ADAM_WD, ADAM_STEP = 0.001, 0.9, 0.999, 1e-08, 0.01, 10

VMEM_LIMIT = 56 * 1024 * 1024

_MESH = pl.DeviceIdType.MESH


def _cparams(sem):
    return pltpu.CompilerParams(dimension_semantics=sem, vmem_limit_bytes=VMEM_LIMIT)


def _tile(dim, pref):
    if dim <= pref:
        return dim
    best = None
    for t in range(LANES, pref + 1, LANES):
        if dim % t == 0:
            best = t
    assert best is not None, dim
    return best


MM_TILES = {"nn": (1024, 1408, 1408), "nt": (512, 2048, 1408), "tn": (1408, 1408, 1024)}


def _mm(a, b, mode, out_dtype, name, scatter=()):
    if mode == "nn":
        (M, K), (K2, N) = a.shape, b.shape
    elif mode == "nt":
        (M, K), (N, K2) = a.shape, b.shape
    else:
        (K, M), (K2, N) = a.shape, b.shape
    assert K == K2, (a.shape, b.shape, mode)
    tm, tn, tk = (_tile(dim, pref) for dim, pref in zip((M, N, K), MM_TILES[mode]))
    nk = K // tk
    grid = (M // tm, N // tn, nk)
    n_x = len(scatter)
    dims = {"nn": (((1,), (0,)), ((), ())), "nt": (((1,), (1,)), ((), ())), "tn": (((0,), (0,)), ((), ()))}[mode]

    def body(*refs):
        a_ref, b_ref = refs[:2]
        o_ref, acc_ref = refs[2 + n_x], refs[3 + 2 * n_x]
        finish = _hosted_exchange(refs[2:2 + n_x] + refs[3 + n_x:3 + 2 * n_x] + refs[4 + 2 * n_x:], n_x, False, grid)
        k = pl.program_id(2)
        part = lax.dot_general(a_ref[...].astype(BF16), b_ref[...].astype(BF16), dims,
                               preferred_element_type=F32)
        if nk == 1:
            o_ref[...] = part.astype(o_ref.dtype)
        else:
            @pl.when(k == 0)
            def _():
                acc_ref[...] = part

            @pl.when(jnp.logical_and(k > 0, k < nk - 1))
            def _():
                acc_ref[...] += part

            @pl.when(k == nk - 1)
            def _():
                o_ref[...] = (acc_ref[...] + part).astype(o_ref.dtype)
        finish()

    a_spec = pl.BlockSpec((tk, tm), lambda i, j, k: (k, i)) if mode == "tn" else pl.BlockSpec((tm, tk), lambda i, j, k: (i, k))
    b_spec = pl.BlockSpec((tn, tk), lambda i, j, k: (j, k)) if mode == "nt" else pl.BlockSpec((tk, tn), lambda i, j, k: (k, j))
    any_spec = pl.BlockSpec(memory_space=pl.ANY)
    outs = pl.pallas_call(
        body, name=name, grid=grid,
        in_specs=[a_spec, b_spec] + [any_spec] * n_x,
        out_specs=[pl.BlockSpec((tm, tn), lambda i, j, k: (i, j))] + [any_spec] * n_x,
        out_shape=[jax.ShapeDtypeStruct((M, N), out_dtype)] + _exchange_shapes(scatter, False),
        scratch_shapes=[pltpu.VMEM((tm, tn) if nk > 1 else (SUBLANES, LANES), F32)] + (_exchange_scratch(n_x) if n_x else []),
        compiler_params=_cparams(("arbitrary",) * 3 if n_x else ("parallel", "parallel", "arbitrary")),
    )(a, b, *scatter)
    return (outs[0], outs[1:]) if n_x else outs[0]


def _rows(tm, w, col=0):
    return pl.BlockSpec((tm, w), lambda i: (i, col))


def _full(shape):
    return pl.BlockSpec(shape, lambda i: (0,) * len(shape))


def _prev8(tm, w, col=0):
    return pl.BlockSpec((SUBLANES, w), lambda i: (jnp.maximum(i * (tm // SUBLANES) - 1, 0), col))


def _next8(tm, w, n_rows, col=0):
    last = n_rows // SUBLANES - 1
    return pl.BlockSpec((SUBLANES, w), lambda i: (jnp.minimum((i + 1) * (tm // SUBLANES), last), col))


def _shift_down(x, halo, k, first):
    rolled = pltpu.roll(x, k, 0)
    row = lax.broadcasted_iota(jnp.int32, x.shape, 0)
    out = rolled
    for j in range(k):
        h = jnp.where(first, 0.0, halo[SUBLANES - k + j:SUBLANES - k + j + 1, :])
        out = jnp.where(row == j, h, out)
    return out


def _shift_up(x, halo, k, last):
    n = x.shape[0]
    rolled = pltpu.roll(x, n - k, 0)
    row = lax.broadcasted_iota(jnp.int32, x.shape, 0)
    out = rolled
    for j in range(k):
        h = jnp.where(last, 0.0, halo[j:j + 1, :])
        out = jnp.where(row == n - k + j, h, out)
    return out


def _acc(ref, val, first):
    @pl.when(first)
    def _():
        ref[...] = val

    @pl.when(jnp.logical_not(first))
    def _():
        ref[...] += val


def _colsum(x):
    return jnp.sum(x, axis=0, keepdims=True)


def _norm_fwd(x, mo, gt, nw, sc, sh, name, tm=256):
    S = x.shape[0]
    has_res = mo is not None

    def body(*refs):
        if has_res:
            x_ref, mo_ref, gt_ref, nw_ref, sc_ref, sh_ref, x2_ref, h_ref, rs_ref = refs
            x2 = x_ref[...] + gt_ref[...] * mo_ref[...]
            x2_ref[...] = x2
        else:
            x_ref, nw_ref, sc_ref, sh_ref, h_ref, rs_ref = refs
            x2 = x_ref[...]
        rstd = lax.rsqrt(jnp.mean(x2 * x2, axis=-1, keepdims=True) + RMS_EPS)
        rs_ref[...] = rstd
        h_ref[...] = ((x2 * rstd * nw_ref[...]) * (1.0 + sc_ref[...]) + sh_ref[...]).astype(BF16)

    vec = _full((1, D))
    ins = [x, mo, gt, nw, sc, sh] if has_res else [x, nw, sc, sh]
    in_specs = [_rows(tm, D), _rows(tm, D), vec, vec, vec, vec] if has_res else [_rows(tm, D), vec, vec, vec]
    outs = [jax.ShapeDtypeStruct((S, D), BF16), jax.ShapeDtypeStruct((S, 1), F32)]
    out_specs = [_rows(tm, D), _rows(tm, 1)]
    if has_res:
        outs = [jax.ShapeDtypeStruct((S, D), F32)] + outs
        out_specs = [_rows(tm, D)] + out_specs
    return pl.pallas_call(body, name=name, grid=(S // tm,), in_specs=in_specs, out_specs=out_specs,
                          out_shape=outs, compiler_params=_cparams(("parallel",)))(*ins)


def _norm_bwd(dh, xin, rstd, nw, sc, dres, mo, gt, name, tm=256):
    S = xin.shape[0]
    has_res = mo is not None

    def body(*refs):
        if has_res:
            dh_ref, x_ref, rs_ref, nw_ref, sc_ref, dres_ref, mo_ref, gt_ref, dx_ref, dsh_ref, dsc_ref, dnw_ref, dmo_ref, dgt_ref = refs
        else:
            dh_ref, x_ref, rs_ref, nw_ref, sc_ref, dres_ref, dx_ref, dsh_ref, dsc_ref, dnw_ref = refs
        first = pl.program_id(0) == 0
        dh = dh_ref[...]
        rstd = rs_ref[...]
        n = x_ref[...] * rstd
        w = nw_ref[...]
        _acc(dsh_ref, _colsum(dh), first)
        _acc(dsc_ref, _colsum(dh * (n * w)), first)
        dnw = dh * (1.0 + sc_ref[...])
        _acc(dnw_ref, _colsum(dnw * n), first)
        dn = dnw * w
        dx = dres_ref[...] + rstd * (dn - n * jnp.mean(dn * n, axis=-1, keepdims=True))
        dx_ref[...] = dx
        if has_res:
            dmo_ref[...] = (dx * gt_ref[...]).astype(BF16)
            _acc(dgt_ref, _colsum(dx * mo_ref[...]), first)

    vec = _full((1, D))
    vshape = jax.ShapeDtypeStruct((1, D), F32)
    ins = [dh, xin, rstd, nw, sc, dres] + ([mo, gt] if has_res else [])
    in_specs = [_rows(tm, D), _rows(tm, D), _rows(tm, 1), vec, vec, _rows(tm, D)] + ([_rows(tm, D), vec] if has_res else [])
    outs = [jax.ShapeDtypeStruct((S, D), F32), vshape, vshape, vshape]
    out_specs = [_rows(tm, D), vec, vec, vec]
    if has_res:
        outs += [jax.ShapeDtypeStruct((S, D), BF16), vshape]
        out_specs += [_rows(tm, D), vec]
    return pl.pallas_call(body, name=name, grid=(S // tm,), in_specs=in_specs, out_specs=out_specs,
                          out_shape=outs, compiler_params=_cparams(("arbitrary",)))(*ins)


def _final(x2, f, gt2, nfw, target, tm=256):
    S = x2.shape[0]

    def body(x2_ref, f_ref, gt_ref, w_ref, t_ref, loss_ref, dx_ref, df_ref, dgt_ref, dw_ref):
        first = pl.program_id(0) == 0
        f = f_ref[...]
        gt = gt_ref[...]
        w = w_ref[...]
        x3 = x2_ref[...] + gt * f
        rstd = lax.rsqrt(jnp.mean(x3 * x3, axis=-1, keepdims=True) + RMS_EPS)
        n = x3 * rstd
        e = n * w - t_ref[...]
        part = 0.5 * jnp.sum(jnp.mean(e * e, axis=-1, keepdims=True), axis=0, keepdims=True)
        _acc(loss_ref, jnp.broadcast_to(part, (SUBLANES, LANES)), first)
        dy = e * (1.0 / D)
        _acc(dw_ref, _colsum(dy * n), first)
        dn = dy * w
        dx = rstd * (dn - n * jnp.mean(dn * n, axis=-1, keepdims=True))
        dx_ref[...] = dx
        df_ref[...] = (dx * gt).astype(BF16)
        _acc(dgt_ref, _colsum(dx * f), first)

    vec = _full((1, D))
    vshape = jax.ShapeDtypeStruct((1, D), F32)
    return pl.pallas_call(
        body, name="final_loss", grid=(S // tm,),
        in_specs=[_rows(tm, D), _rows(tm, D), vec, vec, _rows(tm, D)],
        out_specs=[_full((SUBLANES, LANES)), _rows(tm, D), _rows(tm, D), vec, vec],
        out_shape=[jax.ShapeDtypeStruct((SUBLANES, LANES), F32), jax.ShapeDtypeStruct((S, D), F32),
                   jax.ShapeDtypeStruct((S, D), BF16), vshape, vshape],
        compiler_params=_cparams(("arbitrary",)))(x2, f, gt2, nfw, target)


def _gate_fwd(P, bga, bgr, y_att, y_rwkv, tm=256):
    S = P.shape[0]

    def body(pa_ref, pr_ref, ba_ref, br_ref, ya_ref, yr_ref, mix_ref):
        ga = jax.nn.sigmoid(pa_ref[...] + ba_ref[...])
        gr = jax.nn.sigmoid(pr_ref[...] + br_ref[...])
        mix_ref[...] = (ga * ya_ref[...] + gr * yr_ref[...]).astype(BF16)

    vec = _full((1, D))
    return pl.pallas_call(
        body, name="gate_fwd", grid=(S // tm,),
        in_specs=[_rows(tm, D, C_GA // D), _rows(tm, D, C_GR // D), vec, vec, _rows(tm, D), _rows(tm, D)],
        out_specs=_rows(tm, D), out_shape=jax.ShapeDtypeStruct((S, D), BF16),
        compiler_params=_cparams(("parallel",)))(P, P, bga, bgr, y_att, y_rwkv)


def _gate_bwd(dmix, P, bga, bgr, y_att, y_rwkv, tm=256):
    S = P.shape[0]

    def body(dm_ref, pa_ref, pr_ref, ba_ref, br_ref, ya_ref, yr_ref, dya_ref, dyr_ref, dpa_ref, dpr_ref, dba_ref, dbr_ref):
        first = pl.program_id(0) == 0
        dm = dm_ref[...]
        ga = jax.nn.sigmoid(pa_ref[...] + ba_ref[...])
        gr = jax.nn.sigmoid(pr_ref[...] + br_ref[...])
        dya_ref[...] = (dm * ga).astype(BF16)
        dyr_ref[...] = (dm * gr).astype(BF16)
        dpa = dm * ya_ref[...] * ga * (1.0 - ga)
        dpr = dm * yr_ref[...] * gr * (1.0 - gr)
        dpa_ref[...] = dpa.astype(BF16)
        dpr_ref[...] = dpr.astype(BF16)
        _acc(dba_ref, _colsum(dpa), first)
        _acc(dbr_ref, _colsum(dpr), first)

    vec = _full((1, D))
    row = _rows(tm, D)
    rshape = jax.ShapeDtypeStruct((S, D), BF16)
    vshape = jax.ShapeDtypeStruct((1, D), F32)
    return pl.pallas_call(
        body, name="gate_bwd", grid=(S // tm,),
        in_specs=[row, _rows(tm, D, C_GA // D), _rows(tm, D, C_GR // D), vec, vec, row, row],
        out_specs=[row, row, row, row, vec, vec],
        out_shape=[rshape, rshape, rshape, rshape, vshape, vshape],
        compiler_params=_cparams(("arbitrary",)))(dmix, P, P, bga, bgr, y_att, y_rwkv)


CONV_TN = D_FF // 2


def _conv_fwd(u, conv_w8, conv_b, tm=256, tn=CONV_TN):
    S = u.shape[0]
    nj = D_FF // tn

    def conv(u_ref, h_ref, w_ref, b_ref, first):
        u = u_ref[...]
        h = h_ref[...]
        w = w_ref[...]
        return b_ref[...] + w[0:1] * _shift_down(u, h, 2, first) + w[1:2] * _shift_down(u, h, 1, first) + w[2:3] * u

    def body(ug_ref, hg_ref, uv_ref, hv_ref, wg_ref, wv_ref, bg_ref, bv_ref, act_ref):
        first = pl.program_id(0) == 0
        g = conv(ug_ref, hg_ref, wg_ref, bg_ref, first)
        v = conv(uv_ref, hv_ref, wv_ref, bv_ref, first)
        act_ref[...] = (g * jax.nn.sigmoid(g) * v).astype(BF16)

    blk = lambda off: pl.BlockSpec((tm, tn), lambda i, j: (i, j + off))
    halo = lambda off: pl.BlockSpec((SUBLANES, tn), lambda i, j: (jnp.maximum(i * (tm // SUBLANES) - 1, 0), j + off))
    wsp = lambda off: pl.BlockSpec((SUBLANES, tn), lambda i, j: (0, j + off))
    bsp = lambda off: pl.BlockSpec((1, tn), lambda i, j: (0, j + off))
    return pl.pallas_call(
        body, name="conv_fwd", grid=(S // tm, nj),
        in_specs=[blk(0), halo(0), blk(nj), halo(nj), wsp(0), wsp(nj), bsp(0), bsp(nj)],
        out_specs=pl.BlockSpec((tm, tn), lambda i, j: (i, j)),
        out_shape=jax.ShapeDtypeStruct((S, D_FF), BF16),
        compiler_params=_cparams(("parallel", "parallel")))(u, u, u, u, conv_w8, conv_w8, conv_b, conv_b)


def _conv_bwd_a(dact, u, conv_w8, conv_b, tm=256, tn=CONV_TN):
    S = u.shape[0]
    nj = D_FF // tn

    def half(u_ref, h_ref, w_ref, b_ref, first):
        u = u_ref[...]
        h = h_ref[...]
        w = w_ref[...]
        u2, u1 = _shift_down(u, h, 2, first), _shift_down(u, h, 1, first)
        return b_ref[...] + w[0:1] * u2 + w[1:2] * u1 + w[2:3] * u, (u2, u1, u)

    def wgrad(d, taps):
        z = jnp.zeros((SUBLANES - 3, d.shape[1]), F32)
        return jnp.concatenate([_colsum(d * taps[0]), _colsum(d * taps[1]), _colsum(d * taps[2]), z], axis=0)

    def body(da_ref, ug_ref, hg_ref, uv_ref, hv_ref, wg_ref, wv_ref, bg_ref, bv_ref,
             d_ref, dwg_ref, dwv_ref, dbg_ref, dbv_ref):
        first = pl.program_id(1) == 0
        g, tg = half(ug_ref, hg_ref, wg_ref, bg_ref, first)
        v, tv = half(uv_ref, hv_ref, wv_ref, bv_ref, first)
        da = da_ref[...].astype(F32)
        sg = jax.nn.sigmoid(g)
        dg = da * v * (sg * (1.0 + g * (1.0 - sg)))
        dv = da * (g * sg)
        d_ref[0] = dg
        d_ref[1] = dv
        _acc(dwg_ref, wgrad(dg, tg), first)
        _acc(dwv_ref, wgrad(dv, tv), first)
        _acc(dbg_ref, _colsum(dg), first)
        _acc(dbv_ref, _colsum(dv), first)

    blk = lambda off: pl.BlockSpec((tm, tn), lambda j, i: (i, j + off))
    halo = lambda off: pl.BlockSpec((SUBLANES, tn), lambda j, i: (jnp.maximum(i * (tm // SUBLANES) - 1, 0), j + off))
    wsp = lambda off: pl.BlockSpec((SUBLANES, tn), lambda j, i: (0, j + off))
    bsp = lambda off: pl.BlockSpec((1, tn), lambda j, i: (0, j + off))
    f = jax.ShapeDtypeStruct
    outs = pl.pallas_call(
        body, name="conv_bwd_a", grid=(nj, S // tm),
        in_specs=[pl.BlockSpec((tm, tn), lambda j, i: (i, j)), blk(0), halo(0), blk(nj), halo(nj), wsp(0), wsp(nj), bsp(0), bsp(nj)],
        out_specs=[pl.BlockSpec((2, tm, tn), lambda j, i: (0, i, j)),
                   pl.BlockSpec((SUBLANES, tn), lambda j, i: (0, j)), pl.BlockSpec((SUBLANES, tn), lambda j, i: (0, j)),
                   pl.BlockSpec((1, tn), lambda j, i: (0, j)), pl.BlockSpec((1, tn), lambda j, i: (0, j))],
        out_shape=[f((2, S, D_FF), F32), f((SUBLANES, D_FF), F32), f((SUBLANES, D_FF), F32),
                   f((1, D_FF), F32), f((1, D_FF), F32)],
        compiler_params=_cparams(("parallel", "arbitrary")))(dact, u, u, u, u, conv_w8, conv_w8, conv_b, conv_b)
    return outs


def _conv_bwd_b(duc, conv_w8, tm=256, tn=CONV_TN):
    _, S, W = duc.shape
    nj = W // tn
    n_rows = S // tm

    def body(d_ref, h_ref, w_ref, o_ref):
        last = pl.program_id(0) == n_rows - 1
        d = d_ref[...]
        h = h_ref[...]
        w = w_ref[...]
        o_ref[...] = (w[2:3] * d + w[1:2] * _shift_up(d, h, 1, last) + w[0:1] * _shift_up(d, h, 2, last)).astype(BF16)

    last_tile = S // SUBLANES - 1
    return pl.pallas_call(
        body, name="conv_bwd_b", grid=(n_rows, 2 * nj),
        in_specs=[pl.BlockSpec((None, tm, tn), lambda i, j: (j // nj, i, j % nj)),
                  pl.BlockSpec((None, SUBLANES, tn), lambda i, j: (j // nj, jnp.minimum((i + 1) * (tm // SUBLANES), last_tile), j % nj)),
                  pl.BlockSpec((SUBLANES, tn), lambda i, j: (0, j))],
        out_specs=pl.BlockSpec((tm, tn), lambda i, j: (i, j)),
        out_shape=jax.ShapeDtypeStruct((S, 2 * W), BF16),
        compiler_params=_cparams(("parallel", "parallel")))(duc, duc, conv_w8)


ATT_SCALE = HEAD ** -0.5
NEG = -1e30
ATT_PAIRS = ATT_HEADS // 2


def _att_rows(n, d, S):
    per = S // (QBLK * d)
    r, m = n // per, n % per
    cur = pl.ds(m * (QBLK * d) + r, QBLK, stride=d)
    prv = pl.ds(jnp.maximum(m - 1, 0) * (QBLK * d) + r, QBLK, stride=d)
    return cur, prv, m > 0


def _att_slab(g, j):
    return (C_ATT + g * 3 * ATT_W + j * ATT_W) // LANES


def _heads(x):
    return x[:, 0:HEAD], x[:, HEAD:2 * HEAD]


ATT_NB = 4


def _stack(tiles):
    return jnp.concatenate([t[None] for t in tiles], axis=0)


def _att_operands(i, d, S, *sources):
    rows, has = [], []
    tiles = [[] for _ in sources]
    for bb in range(ATT_NB):
        cur, prv, has_prev = _att_rows(i * ATT_NB + bb, d, S)
        rows.append((cur, prv))
        has.append(has_prev)
        for t, (ref, use_cur) in zip(tiles, sources):
            t += _heads(ref[cur if use_cur else prv, :].astype(BF16))
    return rows, has, [_stack(t) for t in tiles]


def _att_mask(s_c, s_p, has_prev):
    qi = lax.broadcasted_iota(jnp.int32, (QBLK, QBLK), 0)
    kj = lax.broadcasted_iota(jnp.int32, (QBLK, QBLK), 1)
    s_c = jnp.where(kj <= qi, s_c * ATT_SCALE, NEG)
    s_p = jnp.where(jnp.logical_and(kj >= qi, has_prev), s_p * ATT_SCALE, NEG)
    return s_c, s_p


def _att_fwd(P, g):
    S = P.shape[0]
    d = ATT_PATTERNS[g][1]

    def body(q_ref, k_ref, v_ref, o_ref, l_ref):
        def group(i, carry):
            rows, has, (q, kc, kp, vc, vp) = _att_operands(i, d, S, (q_ref, True), (k_ref, True), (k_ref, False),
                                                           (v_ref, True), (v_ref, False))
            s_c_all, s_p_all = _dot16(q, kc, "nt"), _dot16(q, kp, "nt")
            p_c, p_p, den, lse = [], [], [], []
            for e in range(2 * ATT_NB):
                s_c, s_p = _att_mask(s_c_all[e], s_p_all[e], has[e // 2])
                m = jnp.maximum(jnp.max(s_c, axis=1, keepdims=True), jnp.max(s_p, axis=1, keepdims=True))
                pc, pp = jnp.exp(s_c - m), jnp.exp(s_p - m)
                den.append(jnp.sum(pc, axis=1, keepdims=True) + jnp.sum(pp, axis=1, keepdims=True))
                lse.append(jnp.broadcast_to(m + jnp.log(den[e]), (QBLK, HEAD)))
                p_c.append(pc)
                p_p.append(pp)
            num = _dot16(_stack(p_c), vc, "nn") + _dot16(_stack(p_p), vp, "nn")
            for bb, (cur, _) in enumerate(rows):
                o_ref[cur, :] = jnp.concatenate([num[2 * bb] / den[2 * bb], num[2 * bb + 1] / den[2 * bb + 1]], axis=1)
                l_ref[cur, :] = jnp.concatenate(lse[2 * bb:2 * bb + 2], axis=1)
            return carry

        lax.fori_loop(0, S // QBLK // ATT_NB, group, 0)

    slab = lambda j: pl.BlockSpec((S, LANES), lambda i: (0, _att_slab(g, j) + i))
    out = pl.BlockSpec((S, LANES), lambda i: (0, i))
    shp = jax.ShapeDtypeStruct((S, ATT_W), F32)
    return pl.pallas_call(body, name=f"att_fwd_g{g}", grid=(ATT_PAIRS,), in_specs=[slab(0), slab(1), slab(2)],
                          out_specs=[out, out], out_shape=[shp, shp], compiler_params=_cparams(("parallel",)))(P, P, P)


def _att_bwd(P, o, l, do, dl, g):
    S = P.shape[0]
    d = ATT_PATTERNS[g][1]

    def body(q_ref, k_ref, v_ref, o_ref, l_ref, do_ref, dl_ref, dq_ref, dk_ref, dv_ref, dq_acc, dk_acc, dv_acc):
        dk_acc[...] = jnp.zeros_like(dk_acc)
        dv_acc[...] = jnp.zeros_like(dv_acc)

        def group(i, carry):
            rows, has, (q, kc, kp, vc, vp, dob) = _att_operands(
                i, d, S, (q_ref, True), (k_ref, True), (k_ref, False), (v_ref, True), (v_ref, False), (do_ref, True))
            s_c_all, s_p_all = _dot16(q, kc, "nt"), _dot16(q, kp, "nt")
            dp_c_all, dp_p_all = _dot16(dob, vc, "nt"), _dot16(dob, vp, "nt")
            p_c, p_p, ds_c, ds_p = [], [], [], []
            for bb, (cur, _) in enumerate(rows):
                dd2 = do_ref[cur, :] * o_ref[cur, :] - dl_ref[cur, :]
                for h, (dd, lse) in enumerate(zip(_heads(dd2), _heads(l_ref[cur, :]))):
                    e = 2 * bb + h
                    s_c, s_p = _att_mask(s_c_all[e], s_p_all[e], has[bb])
                    pc, pp = jnp.exp(s_c - lse[:, 0:1]), jnp.exp(s_p - lse[:, 0:1])
                    delta = jnp.sum(dd, axis=1, keepdims=True)
                    p_c.append(pc)
                    p_p.append(pp)
                    ds_c.append(pc * (dp_c_all[e] - delta) * ATT_SCALE)
                    ds_p.append(pp * (dp_p_all[e] - delta) * ATT_SCALE)
            p_c, p_p, ds_c, ds_p = map(_stack, (p_c, p_p, ds_c, ds_p))
            dq = _dot16(ds_c, kc, "nn") + _dot16(ds_p, kp, "nn")
            dk_c, dk_p = _dot16(ds_c, q, "tn"), _dot16(ds_p, q, "tn")
            dv_c, dv_p = _dot16(p_c, dob, "tn"), _dot16(p_p, dob, "tn")
            pair = lambda x, bb: jnp.concatenate([x[2 * bb], x[2 * bb + 1]], axis=1)
            for bb, (cur, prv) in enumerate(rows):
                dq_acc[cur, :] = pair(dq, bb)
                dk_acc[cur, :] += pair(dk_c, bb)
                dv_acc[cur, :] += pair(dv_c, bb)
                dk_acc[prv, :] += pair(dk_p, bb)
                dv_acc[prv, :] += pair(dv_p, bb)
            return carry

        lax.fori_loop(0, S // QBLK // ATT_NB, group, 0)
        dq_ref[...] = dq_acc[...].astype(BF16)
        dk_ref[...] = dk_acc[...].astype(BF16)
        dv_ref[...] = dv_acc[...].astype(BF16)

    slab = lambda j: pl.BlockSpec((S, LANES), lambda i: (0, _att_slab(g, j) + i))
    blk128 = pl.BlockSpec((S, LANES), lambda i: (0, i))
    shp = jax.ShapeDtypeStruct((S, ATT_W), BF16)
    return pl.pallas_call(body, name=f"att_bwd_g{g}", grid=(ATT_PAIRS,),
                          in_specs=[slab(0), slab(1), slab(2)] + [blk128] * 4, out_specs=[blk128] * 3, out_shape=[shp] * 3,
                          scratch_shapes=[pltpu.VMEM((S, LANES), F32)] * 3,
                          compiler_params=_cparams(("parallel",)))(P, P, P, o, l, do, dl)


def _att_weights(l_refs):
    l0, l1, l2 = [r[...] for r in l_refs]
    m = jnp.maximum(jnp.maximum(l0, l1), l2)
    e = (jnp.exp(l0 - m), jnp.exp(l1 - m), jnp.exp(l2 - m))
    inv = 1.0 / (e[0] + e[1] + e[2])
    return [x * inv for x in e]


def _att_combine_fwd(os, ls, tm=512):
    S = os[0].shape[0]

    def body(o0, o1, o2, l0, l1, l2, a_ref):
        w = _att_weights((l0, l1, l2))
        a_ref[...] = (w[0] * o0[...] + w[1] * o1[...] + w[2] * o2[...]).astype(BF16)

    row = _rows(tm, ATT_W)
    return pl.pallas_call(body, name="att_combine_fwd", grid=(S // tm,), in_specs=[row] * 6, out_specs=row,
                          out_shape=jax.ShapeDtypeStruct((S, ATT_W), BF16),
                          compiler_params=_cparams(("parallel",)))(*os, *ls)


def _att_combine_bwd(da, os, ls, tm=512):
    S = da.shape[0]

    def body(da_ref, o0, o1, o2, l0, l1, l2, *out_refs):
        da = da_ref[...]
        w = _att_weights((l0, l1, l2))
        dw = (da * o0[...], da * o1[...], da * o2[...])
        mean = w[0] * dw[0] + w[1] * dw[1] + w[2] * dw[2]
        for g in range(3):
            out_refs[g][...] = w[g] * da
            out_refs[3 + g][...] = w[g] * (dw[g] - mean)

    row = _rows(tm, ATT_W)
    shp = jax.ShapeDtypeStruct((S, ATT_W), F32)
    return pl.pallas_call(body, name="att_combine_bwd", grid=(S // tm,), in_specs=[row] * 7, out_specs=[row] * 6,
                          out_shape=[shp] * 6, compiler_params=_cparams(("parallel",)))(da, *os, *ls)


@jax.custom_vjp
def _bdot(a, b):
    return jnp.dot(a.astype(BF16), b.astype(BF16), preferred_element_type=F32)


def _bdot_fwd(a, b):
    return _bdot(a, b), (a, b)


def _bdot_bwd(res, ct):
    a, b = res
    ct16 = ct.astype(BF16)
    da = lax.dot_general(ct16, b.astype(BF16), (((1,), (1,)), ((), ())), preferred_element_type=F32)
    db = lax.dot_general(a.astype(BF16), ct16, (((0,), (0,)), ((), ())), preferred_element_type=F32)
    return da, db


_bdot.defvjp(_bdot_fwd, _bdot_bwd)


def _two_piece_dot(x, m):
    hi = x.astype(BF16)
    lo = (x - hi.astype(F32)).astype(BF16)
    return jnp.dot(hi, m, preferred_element_type=F32) + jnp.dot(lo, m, preferred_element_type=F32)


def _head_sum_impl(x):
    sel = (lax.broadcasted_iota(jnp.int32, (D, LANES), 0) // HEAD == lax.broadcasted_iota(jnp.int32, (D, LANES), 1)).astype(BF16)
    sel_t = (lax.broadcasted_iota(jnp.int32, (LANES, D), 1) // HEAD == lax.broadcasted_iota(jnp.int32, (LANES, D), 0)).astype(BF16)
    return _two_piece_dot(_two_piece_dot(x, sel), sel_t)


@jax.custom_vjp
def _head_sum(x):
    return _head_sum_impl(x)


_head_sum.defvjp(lambda x: (_head_sum_impl(x), None), lambda _, ct: (_head_sum_impl(ct),))


def _softplus(z):
    return jnp.maximum(z, 0.0) + jnp.log(1.0 + jnp.exp(-jnp.abs(z)))


def _rwkv_prep_fn(zr, zrp, zk, zkp, zv, zvp, zl, zlp, mu_r, mu_k, mu_v, mu_l, w0, a0, k_k, k_a, w2, a2, g2p):
    r = zr + (zrp - zr) * mu_r
    k = zk + (zkp - zk) * mu_k
    v = zv + (zvp - zv) * mu_v
    lo = zl + (zlp - zl) * mu_l
    w_low, a_low, g_low = lo[:, 0:LORA_W], lo[:, LORA_W:LORA_W + LORA_A], lo[:, LANES:LANES + G_PAD]
    w_log = -_softplus(-(w0 + _bdot(jnp.tanh(w_low), w2))) - 0.5
    decay = -jnp.exp(w_log)
    a = jax.nn.sigmoid(a0 + _bdot(a_low, a2))
    g = _bdot(jax.nn.sigmoid(g_low), g2p)
    kmod = k * (1.0 + (a - 1.0) * k_a)
    kk = k * k_k
    kk = kk / jnp.maximum(jnp.sqrt(_head_sum(kk * kk)), 1e-12)
    return r, decay, kmod, v, -kk, kk * a, g


def _rwkv_prep_specs(tm):
    vec = _full((1, D))
    slabs = []
    for col in (C_R // D, C_K // D, C_V // D):
        slabs += [_rows(tm, D, col), _prev8(tm, D, col)]
    slabs += [_rows(tm, LORA_PAD, C_LORA // LORA_PAD), _prev8(tm, LORA_PAD, C_LORA // LORA_PAD)]
    params = [vec, vec, vec, _full((1, LORA_PAD)), vec, vec, vec, vec,
              _full((LORA_W, D)), _full((LORA_A, D)), _full((G_PAD, D))]
    return slabs, params


def _prep_inputs(refs, first):
    vals = []
    for s in range(4):
        z = refs[2 * s][...]
        vals += [z, _shift_down(z, refs[2 * s + 1][...], 1, first)]
    return vals + [r[...] for r in refs[8:19]]


def _rwkv_prep(P, params, tm=256):
    S = P.shape[0]
    slabs, pspecs = _rwkv_prep_specs(tm)

    def body(*refs):
        outs = _rwkv_prep_fn(*_prep_inputs(refs, pl.program_id(0) == 0))
        for o_ref, val in zip(refs[19:], outs):
            o_ref[...] = val

    shp = jax.ShapeDtypeStruct((S, D), F32)
    return pl.pallas_call(body, name="rwkv_prep", grid=(S // tm,), in_specs=slabs + pspecs,
                          out_specs=[_rows(tm, D)] * 7, out_shape=[shp] * 7,
                          compiler_params=_cparams(("parallel",)))(*([P] * 8), *params)


def _rwkv_prep_bwd(P, params, cts_a, cts_b, tm=128):
    S = P.shape[0]
    slabs, pspecs = _rwkv_prep_specs(tm)
    has_b = [c is not None for c in cts_b]
    n_ct = 7 + sum(has_b)

    def body(*refs):
        first = pl.program_id(0) == 0
        ins = _prep_inputs(refs, first)
        ct_refs = refs[19:19 + n_ct]
        out_refs = refs[19 + n_ct:]
        cts, pos = [], 7
        for i in range(7):
            c = ct_refs[i][...]
            if has_b[i]:
                c = c + ct_refs[pos][...]
                pos += 1
            cts.append(c)
        _, vjp = jax.vjp(_rwkv_prep_fn, *ins)
        grads = vjp(tuple(cts))
        for s in range(4):
            out_refs[s][...] = grads[2 * s]
            out_refs[4 + s][...] = grads[2 * s + 1]
        for i in range(11):
            _acc(out_refs[8 + i], grads[8 + i], first)

    ct_in = list(cts_a) + [c for c in cts_b if c is not None]
    row, lrow = _rows(tm, D), _rows(tm, LORA_PAD)
    f = jax.ShapeDtypeStruct
    zshapes = [f((S, D), F32)] * 3 + [f((S, LORA_PAD), F32)]
    pshapes = [f((1, D), F32)] * 3 + [f((1, LORA_PAD), F32)] + [f((1, D), F32)] * 4 + [f((LORA_W, D), F32), f((LORA_A, D), F32), f((G_PAD, D), F32)]
    return pl.pallas_call(
        body, name="rwkv_prep_bwd", grid=(S // tm,),
        in_specs=slabs + pspecs + [row] * n_ct,
        out_specs=[row, row, row, lrow] * 2 + pspecs,
        out_shape=zshapes * 2 + pshapes,
        compiler_params=_cparams(("arbitrary",)))(*([P] * 8), *params, *ct_in)


def _shift_add(a, b, tm=256):
    S, W = a.shape

    def body(a_ref, b_ref, h_ref, o_ref):
        last = pl.program_id(0) == pl.num_programs(0) - 1
        o_ref[...] = (a_ref[...] + _shift_up(b_ref[...], h_ref[...], 1, last)).astype(BF16)

    return pl.pallas_call(body, name="shift_add", grid=(S // tm,),
                          in_specs=[_rows(tm, W), _rows(tm, W), _next8(tm, W, S)],
                          out_specs=_rows(tm, W), out_shape=jax.ShapeDtypeStruct((S, W), BF16),
                          compiler_params=_cparams(("parallel",)))(a, b, b)


def _rwkv_post_fn(y, r, kmod, v, g, lnx_w, lnx_b, r_k):
    mean = _head_sum(y) * (1.0 / HEAD)
    yc = y - mean
    var = _head_sum(yc * yc) * (1.0 / HEAD)
    yn = yc * lax.rsqrt(var + GN_EPS) * lnx_w + lnx_b
    bonus = _head_sum(r * kmod * r_k) * v
    return (yn + bonus) * g


def _rwkv_post(y, r, kmod, v, g, lnx_w, lnx_b, r_k, tm=256):
    S = y.shape[0]

    def body(y_ref, r_ref, k_ref, v_ref, g_ref, w_ref, b_ref, rk_ref, o_ref):
        o_ref[...] = _rwkv_post_fn(y_ref[...], r_ref[...], k_ref[...], v_ref[...], g_ref[...],
                                   w_ref[...], b_ref[...], rk_ref[...]).astype(BF16)

    row, vec = _rows(tm, D), _full((1, D))
    return pl.pallas_call(body, name="rwkv_post", grid=(S // tm,), in_specs=[row] * 5 + [vec] * 3, out_specs=row,
                          out_shape=jax.ShapeDtypeStruct((S, D), BF16),
                          compiler_params=_cparams(("parallel",)))(y, r, kmod, v, g, lnx_w, lnx_b, r_k)


def _rwkv_post_bwd(drw, y, r, kmod, v, g, lnx_w, lnx_b, r_k, tm=256):
    S = y.shape[0]

    def body(d_ref, y_ref, r_ref, k_ref, v_ref, g_ref, w_ref, b_ref, rk_ref, *out_refs):
        first = pl.program_id(0) == 0
        _, vjp = jax.vjp(_rwkv_post_fn, y_ref[...], r_ref[...], k_ref[...], v_ref[...], g_ref[...],
                         w_ref[...], b_ref[...], rk_ref[...])
        grads = vjp(d_ref[...])
        for i in range(5):
            out_refs[i][...] = grads[i]
        for i in range(5, 8):
            _acc(out_refs[i], grads[i], first)

    row, vec = _rows(tm, D), _full((1, D))
    f = jax.ShapeDtypeStruct
    return pl.pallas_call(body, name="rwkv_post_bwd", grid=(S // tm,), in_specs=[row] * 6 + [vec] * 3,
                          out_specs=[row] * 5 + [vec] * 3, out_shape=[f((S, D), F32)] * 5 + [f((1, D), F32)] * 3,
                          compiler_params=_cparams(("arbitrary",)))(drw, y, r, kmod, v, g, lnx_w, lnx_b, r_k)


CHUNK = 64
CHUNK_TB = 256
_DOT_DIMS = {"nn": (((2,), (1,)), ((0,), (0,))), "nt": (((2,), (2,)), ((0,), (0,))), "tn": (((1,), (1,)), ((0,), (0,)))}


def _dot16(x, y, mode):
    return lax.dot_general(x.astype(BF16), y.astype(BF16), _DOT_DIMS[mode], preferred_element_type=F32)


@functools.partial(jax.custom_vjp, nondiff_argnums=(2,))
def _mm16(x, y, mode):
    return _dot16(x, y, mode)


def _mm16_fwd(x, y, mode):
    return _dot16(x, y, mode), (x, y)


def _mm16_bwd(mode, res, ct):
    x, y = res
    if mode == "nn":
        return _dot16(ct, y, "nt"), _dot16(x, ct, "tn")
    if mode == "nt":
        return _dot16(ct, y, "nn"), _dot16(ct, x, "tn")
    return _dot16(y, ct, "nt"), _dot16(x, ct, "nn")


_mm16.defvjp(_mm16_fwd, _mm16_bwd)


def _tri_sum(x, upper):
    T = x.shape[0]
    i = lax.broadcasted_iota(jnp.int32, (T, T), 0)
    j = lax.broadcasted_iota(jnp.int32, (T, T), 1)
    tri = ((j >= i) if upper else (i >= j)).astype(BF16)
    out, rest = None, x
    for _ in range(3):
        piece = rest.astype(BF16)
        rest = rest - piece.astype(F32)
        part = jnp.dot(tri, piece, preferred_element_type=F32)
        out = part if out is None else out + part
    return out


@jax.custom_vjp
def _cumsum_rows(x):
    return _tri_sum(x, False)


_cumsum_rows.defvjp(lambda x: (_tri_sum(x, False), None), lambda _, ct: (_tri_sum(ct, True),))


def _rows_to_cols(x):
    H, _, K = x.shape
    eye = (lax.broadcasted_iota(jnp.int32, (H, K, K), 1) == lax.broadcasted_iota(jnp.int32, (H, K, K), 2)).astype(F32)
    out = lax.dot_general(eye, jnp.broadcast_to(x, (H, SUBLANES, K)), _DOT_DIMS["nt"],
                          precision=lax.Precision.HIGHEST, preferred_element_type=F32)
    return out[:, :, 0:1]


def _per_head(x):
    return jnp.concatenate([x[:, h * HEAD:(h + 1) * HEAD][None] for h in range(N_HEADS)], axis=0)


def _chunk_fn(st0, r, lw, k, v, a, b):
    T = r.shape[0]
    cl = _cumsum_rows(lw)
    cl_end = cl[T - 1:T, :]
    inv = jnp.exp(-cl)
    to_end = jnp.exp(cl_end - cl)
    ah, rh, bh, kh, be, ke, v3 = [_per_head(x) for x in
                                  (a * jnp.exp(cl - lw), r * jnp.exp(cl), b * inv, k * inv, b * to_end, k * to_end, v)]
    i = lax.broadcasted_iota(jnp.int32, (N_HEADS, T, T), 1)
    j = lax.broadcasted_iota(jnp.int32, (N_HEADS, T, T), 2)
    a_ab = jnp.where(i > j, _mm16(ah, bh, "nt"), 0.0)
    a_ak = jnp.where(i > j, _mm16(ah, kh, "nt"), 0.0)
    m_rb = jnp.where(i >= j, _mm16(rh, bh, "nt"), 0.0)
    m_rk = jnp.where(i >= j, _mm16(rh, kh, "nt"), 0.0)
    rhs = _mm16(ah, st0, "nn") + _mm16(a_ak, v3, "nn")
    power, solve, n = a_ab, (i == j).astype(F32) + a_ab, 1
    while 2 * n < T:
        power = _mm16(power, power, "nn")
        solve = solve + _mm16(solve, power, "nn")
        n *= 2
    sa = _mm16(solve, rhs, "nn")
    y3 = _mm16(rh, st0, "nn") + _mm16(m_rb, sa, "nn") + _mm16(m_rk, v3, "nn")
    st_end = _rows_to_cols(_per_head(jnp.exp(cl_end))) * st0 + _mm16(be, sa, "tn") + _mm16(ke, v3, "tn")
    return jnp.concatenate([y3[h] for h in range(N_HEADS)], axis=1), st_end


def _hosted_exchange(refs, n, broadcast, grid):
    if n == 0:
        return lambda: None
    start, wait = _exchange_ops(refs[:n], refs[n:2 * n], *refs[2 * n:], broadcast)
    first = functools.reduce(jnp.logical_and, [pl.program_id(a) == 0 for a in range(len(grid))])
    last = functools.reduce(jnp.logical_and, [pl.program_id(a) == g - 1 for a, g in enumerate(grid)])
    pl.when(first)(start)
    return lambda: pl.when(last)(wait)


def _cscan_fwd(r, lw, k, v, a, b, gather=()):
    S = r.shape[0]
    per_blk = CHUNK_TB // CHUNK
    n_x = len(gather)
    nblk = S // CHUNK_TB

    def body(*refs):
        r_ref, lw_ref, k_ref, v_ref, a_ref, b_ref = refs[:6]
        y_ref, ck_ref = refs[6 + n_x:8 + n_x]
        st_ref = refs[8 + 2 * n_x]
        finish = _hosted_exchange(refs[6:6 + n_x] + refs[8 + n_x:8 + 2 * n_x] + refs[9 + 2 * n_x:], n_x, True, (nblk,))

        @pl.when(pl.program_id(0) == 0)
        def _():
            st_ref[...] = jnp.zeros_like(st_ref)

        def chunk(c, carry):
            rows = pl.ds(pl.multiple_of(c * CHUNK, CHUNK), CHUNK)
            st0 = st_ref[...]
            ck_ref[c] = st0
            y, st_end = _chunk_fn(st0, r_ref[rows, :], lw_ref[rows, :], k_ref[rows, :],
                                  v_ref[rows, :], a_ref[rows, :], b_ref[rows, :])
            y_ref[rows, :] = y
            st_ref[...] = st_end
            return carry

        lax.fori_loop(0, per_blk, chunk, 0)
        finish()

    blk = _rows(CHUNK_TB, D)
    any_spec = pl.BlockSpec(memory_space=pl.ANY)
    outs = pl.pallas_call(
        body, name="scan_fwd", grid=(nblk,), in_specs=[blk] * 6 + [any_spec] * n_x,
        out_specs=[blk, pl.BlockSpec((per_blk, N_HEADS, HEAD, HEAD), lambda i: (i, 0, 0, 0))] + [any_spec] * n_x,
        out_shape=[jax.ShapeDtypeStruct((S, D), F32), jax.ShapeDtypeStruct((S // CHUNK, N_HEADS, HEAD, HEAD), F32)]
        + _exchange_shapes(gather, True),
        scratch_shapes=[pltpu.VMEM((N_HEADS, HEAD, HEAD), F32)] + (_exchange_scratch(n_x) if n_x else []),
        compiler_params=_cparams(("arbitrary",)))(r, lw, k, v, a, b, *gather)
    return outs[0], outs[1], outs[2:]


def _cscan_bwd(r, lw, k, v, a, b, ckpt, dy, scatter=()):
    S = r.shape[0]
    per_blk = CHUNK_TB // CHUNK
    nblk = S // CHUNK_TB
    n_x = len(scatter)

    def body(*refs):
        r_ref, lw_ref, k_ref, v_ref, a_ref, b_ref, ck_ref, dy_ref = refs[:8]
        out_refs = refs[8 + n_x:14 + n_x]
        ds_ref = refs[14 + 2 * n_x]
        finish = _hosted_exchange(refs[8:8 + n_x] + refs[14 + n_x:14 + 2 * n_x] + refs[15 + 2 * n_x:], n_x, False, (nblk,))

        @pl.when(pl.program_id(0) == 0)
        def _():
            ds_ref[...] = jnp.zeros_like(ds_ref)

        def chunk(cc, carry):
            c = per_blk - 1 - cc
            rows = pl.ds(pl.multiple_of(c * CHUNK, CHUNK), CHUNK)
            ins = (ck_ref[c], r_ref[rows, :], lw_ref[rows, :], k_ref[rows, :], v_ref[rows, :], a_ref[rows, :], b_ref[rows, :])
            _, vjp = jax.vjp(_chunk_fn, *ins)
            grads = vjp((dy_ref[rows, :], ds_ref[...]))
            ds_ref[...] = grads[0]
            for o_ref, g in zip(out_refs, grads[1:]):
                o_ref[rows, :] = g
            return carry

        lax.fori_loop(0, per_blk, chunk, 0)
        finish()

    blk = pl.BlockSpec((CHUNK_TB, D), lambda i: (nblk - 1 - i, 0))
    any_spec = pl.BlockSpec(memory_space=pl.ANY)
    shp = jax.ShapeDtypeStruct((S, D), F32)
    outs = pl.pallas_call(
        body, name="scan_bwd", grid=(nblk,),
        in_specs=[blk] * 6 + [pl.BlockSpec((per_blk, N_HEADS, HEAD, HEAD), lambda i: (nblk - 1 - i, 0, 0, 0)), blk]
        + [any_spec] * n_x,
        out_specs=[blk] * 6 + [any_spec] * n_x, out_shape=[shp] * 6 + _exchange_shapes(scatter, False),
        scratch_shapes=[pltpu.VMEM((N_HEADS, HEAD, HEAD), F32)] + (_exchange_scratch(n_x) if n_x else []),
        compiler_params=_cparams(("arbitrary",)))(r, lw, k, v, a, b, ckpt, dy, *scatter)
    return outs[:6], outs[6:]


def _ada_partial(c_all, w_shard):
    def body(c_ref, w_ref, o_ref):
        o_ref[...] = jnp.dot(c_ref[...].astype(BF16), w_ref[...].astype(BF16), preferred_element_type=F32)

    vm = pl.BlockSpec(memory_space=pltpu.VMEM)
    return pl.pallas_call(body, name="ada_partial", in_specs=[vm, vm], out_specs=vm,
                          out_shape=jax.ShapeDtypeStruct((N_DEV, w_shard.shape[1]), F32),
                          compiler_params=pltpu.CompilerParams(vmem_limit_bytes=VMEM_LIMIT))(c_all, w_shard)


def _ada_bias(rows, b_ada):
    def body(r_ref, b_ref, o_ref):
        o_ref[...] = r_ref[...] + b_ref[...]

    vm = pl.BlockSpec(memory_space=pltpu.VMEM)
    return pl.pallas_call(body, name="ada_bias", in_specs=[vm, vm], out_specs=vm,
                          out_shape=jax.ShapeDtypeStruct(rows.shape, F32))(rows, b_ada)


def _ada_wgrad(c_cols, d_all):
    def body(c_ref, d_ref, o_ref):
        acc = c_ref[:, 0:1] * d_ref[0:1, :]
        for j in range(1, N_DEV):
            acc = acc + c_ref[:, j:j + 1] * d_ref[j:j + 1, :]
        o_ref[...] = acc

    vm = pl.BlockSpec(memory_space=pltpu.VMEM)
    return pl.pallas_call(body, name="ada_wgrad", in_specs=[vm, vm], out_specs=vm,
                          out_shape=jax.ShapeDtypeStruct((D, d_all.shape[1]), F32),
                          compiler_params=pltpu.CompilerParams(vmem_limit_bytes=VMEM_LIMIT))(c_cols, d_all)


def _exchange(srcs, broadcast, name):
    n = len(srcs)

    def body(*refs):
        start, wait = _exchange_ops(refs[:n], refs[n:2 * n], *refs[2 * n:], broadcast)
        start()
        wait()

    any_spec = pl.BlockSpec(memory_space=pl.ANY)
    return pl.pallas_call(
        body, name=name, out_shape=_exchange_shapes(srcs, broadcast), in_specs=[any_spec] * n, out_specs=[any_spec] * n,
        scratch_shapes=_exchange_scratch(n),
        compiler_params=pltpu.CompilerParams(has_side_effects=True),
    )(*srcs)


def _gather_via_sibling(srcs, name):
    n = len(srcs)

    def body(*refs):
        src_refs, out_refs = refs[:n], refs[n:2 * n]
        send_sems, recv_sems, local_sems = refs[2 * n:]
        x, y, c = lax.axis_index("x"), lax.axis_index("y"), lax.axis_index("c")
        me, sibling = (x, y, c), (x, y, 1 - c)
        chips = [(1 - x, y), (x, 1 - y), (1 - x, 1 - y)]

        def slot(px, py, pc):
            return 4 * px + 2 * py + pc

        def copy(i, k, block, to, src=None):
            rows = out_refs[i].at[slot(*block)]
            return pltpu.make_async_remote_copy(
                src_ref=rows if src is None else src, dst_ref=rows, send_sem=send_sems.at[i, k],
                recv_sem=recv_sems.at[i, k], device_id=to, device_id_type=_MESH)

        local = [pltpu.make_async_copy(src_refs[i], out_refs[i].at[slot(*me)], local_sems.at[i]) for i in range(n)]
        for cp in local:
            cp.start()
        first = [copy(i, 0, me, sibling, src=src_refs[i]) for i in range(n)]
        first += [copy(i, 1 + j, me, (*chip, c), src=src_refs[i]) for j, chip in enumerate(chips) for i in range(n)]
        for cp in first:
            cp.start()
        passed = []
        for j, chip in enumerate(chips):
            for i in range(n):
                copy(i, 1 + j, (*chip, c), me).wait_recv()
                passed.append(copy(i, 4 + j, (*chip, c), sibling))
                passed[-1].start()
        for i in range(n):
            copy(i, 0, sibling, me).wait_recv()
            for j, chip in enumerate(chips):
                copy(i, 4 + j, (*chip, 1 - c), me).wait_recv()
        for cp in first + passed:
            cp.wait_send()
        for cp in local:
            cp.wait()

    any_spec = pl.BlockSpec(memory_space=pl.ANY)
    return pl.pallas_call(
        body, name=name, out_shape=_exchange_shapes(srcs, True), in_specs=[any_spec] * n, out_specs=[any_spec] * n,
        scratch_shapes=_exchange_scratch(n),
        compiler_params=pltpu.CompilerParams(has_side_effects=True),
    )(*srcs)


def _flags(broadcast, n):
    return [broadcast] * n if isinstance(broadcast, bool) else list(broadcast)


def _exchange_shapes(srcs, broadcast):
    return [jax.ShapeDtypeStruct((N_DEV,) + (s.shape if bc else s.shape[1:]), s.dtype)
            for s, bc in zip(srcs, _flags(broadcast, len(srcs)))]


def _exchange_scratch(n):
    return [pltpu.SemaphoreType.DMA((n, N_DEV)), pltpu.SemaphoreType.DMA((n, N_DEV)), pltpu.SemaphoreType.DMA((n,))]


def _exchange_ops(src_refs, out_refs, send_sems, recv_sems, local_sems, broadcast):
    n = len(src_refs)
    flags = _flags(broadcast, n)
    x, y, c = lax.axis_index("x"), lax.axis_index("y"), lax.axis_index("c")
    me = 4 * x + 2 * y + c

    def block(i, j):
        return src_refs[i] if flags[i] else src_refs[i].at[j]

    def remote(i, d, src_slot, dst_slot):
        px, py, pc = x ^ (d >> 2), y ^ ((d >> 1) & 1), c ^ (d & 1)
        return pltpu.make_async_remote_copy(
            src_ref=block(i, src_slot), dst_ref=out_refs[i].at[dst_slot], send_sem=send_sems.at[i, d],
            recv_sem=recv_sems.at[i, d], device_id=(px, py, pc), device_id_type=_MESH)

    def local(i):
        return pltpu.make_async_copy(block(i, me), out_refs[i].at[me], local_sems.at[i])

    def start():
        for i in range(n):
            local(i).start()
        for d in range(1, N_DEV):
            for i in range(n):
                remote(i, d, me ^ d, me).start()

    def wait():
        for d in range(1, N_DEV):
            for i in range(n):
                remote(i, d, me, me ^ d).wait_recv()
        for d in range(1, N_DEV):
            for i in range(n):
                remote(i, d, me ^ d, me).wait_send()
        for i in range(n):
            local(i).wait()

    return start, wait


def _sum_adam(parts, w, m, v, name):
    n_parts, R, C = parts.shape
    tm = 256 if R % 256 == 0 else R
    c1 = 1.0 / (1.0 - ADAM_B1 ** ADAM_STEP)
    c2 = 1.0 / (1.0 - ADAM_B2 ** ADAM_STEP)

    def body(p_ref, w_ref, m_ref, v_ref, g_ref, d_ref, nm_ref, nv_ref):
        g = p_ref[0].astype(F32)
        for j in range(1, n_parts):
            g = g + p_ref[j].astype(F32)
        nm = ADAM_B1 * m_ref[...] + (1.0 - ADAM_B1) * g
        nv = ADAM_B2 * v_ref[...] + (1.0 - ADAM_B2) * (g * g)
        g_ref[...] = g
        nm_ref[...] = nm
        nv_ref[...] = nv
        d_ref[...] = -ADAM_LR * ((nm * c1) / (jnp.sqrt(nv * c2) + ADAM_EPS) + ADAM_WD * w_ref[...])

    row = _rows(tm, C)
    shp = jax.ShapeDtypeStruct((R, C), F32)
    return pl.pallas_call(body, name=name, grid=(R // tm,),
                          in_specs=[pl.BlockSpec((n_parts, tm, C), lambda i: (0, i, 0)), row, row, row],
                          out_specs=[row] * 4, out_shape=[shp] * 4,
                          compiler_params=_cparams(("parallel",)))(parts, w, m, v)


PACK_ALIGN = 16 * LANES
PACK_ROWS = 512 * LANES

SHARDED = (("w_ada", 1), ("w_in", 1), ("w2", 1), ("a2", 1), ("g2", 1), ("w_att_out", 1), ("w_rwkv_out", 0),
           ("w_o", 0), ("w_up", 1), ("conv_w", 1), ("w_down", 0))
EARLY, LATE = SHARDED[1:5], SHARDED[5:]
REPLICATED = ("b_ada", "norm1_w", "b_gate", "mu_shift", "w0", "a0", "k_k", "k_a", "r_k", "lnx_w", "lnx_b",
              "norm2_w", "conv_b", "norm_f_w")
WEIGHTS = ("w_ada", "b_ada", "norm1_w", "w_in", "b_gate", "mu_shift", "w0", "w2", "a0", "a2", "g2", "k_k", "k_a", "r_k",
           "lnx_w", "lnx_b", "w_att_out", "w_rwkv_out", "w_o", "norm2_w", "w_up", "conv_w", "conv_b", "w_down", "norm_f_w")


def _pack(arrays):
    flat, layout, off = [], [], 0
    for i, a in enumerate(arrays):
        n = a.size
        pad = (-n) % PACK_ALIGN if i + 1 < len(arrays) else (-(off + n)) % PACK_ROWS
        flat.append(a.reshape(-1))
        if pad:
            flat.append(jnp.zeros((pad,), a.dtype))
        layout.append((off, n, a.shape))
        off += n + pad
    return jnp.concatenate(flat).reshape(-1, LANES), layout


def _unpack(buf, layout):
    flat = buf.reshape(-1)
    return [flat[off:off + n].reshape(shape) for off, n, shape in layout]


def _pad_w_in(w_in):
    rkv = w_in[:, ATT_IN:ATT_IN + 3 * D]
    lora = w_in[:, ATT_IN + 3 * D:ATT_IN + RWKV_IN]
    gates = w_in[:, ATT_IN + RWKV_IN:]
    att = w_in[:, :ATT_IN]
    lw, la, lg = lora[:, :LORA_W], lora[:, LORA_W:LORA_W + LORA_A], lora[:, LORA_W + LORA_A:]
    zeros = jnp.zeros((w_in.shape[0], LORA_PAD - LANES - LORA_G), w_in.dtype)
    return jnp.concatenate([rkv, gates, att, lw, la, lg, zeros], axis=1)


def _unpad_w_in(g):
    att = g[:, C_ATT:C_ATT + ATT_IN]
    rkv = g[:, C_R:C_R + 3 * D]
    lora = jnp.concatenate([g[:, C_LORA:C_LORA + LORA_W + LORA_A], g[:, C_LORA + LANES:C_LORA + LANES + LORA_G]], axis=1)
    gates = g[:, C_GA:C_GA + 2 * D]
    return jnp.concatenate([att, rkv, lora, gates], axis=1)


def _pad_mu(mu):
    lo = mu[:, 3 * D:]
    mu_l = jnp.concatenate([lo[:, :LORA_W + LORA_A], lo[:, LORA_W + LORA_A:], jnp.zeros((1, LORA_PAD - LANES - LORA_G), mu.dtype)], axis=1)
    return mu[:, :D], mu[:, D:2 * D], mu[:, 2 * D:3 * D], mu_l


def _local_step(x, ada, W, late_shards, target):
    S = x.shape[0]
    W = dict(W)
    G = {}
    sh1, sc1, gt1, sh2, sc2, gt2 = [ada[:, i * D:(i + 1) * D] for i in range(6)]
    h1, rstd1 = _norm_fwd(x, None, None, W["norm1_w"], sc1, sh1, "norm1_fwd")
    w_in_p = _pad_w_in(W["w_in"])
    P = _mm(h1, w_in_p, "nn", F32, "proj_in")

    mu_r, mu_k, mu_v, mu_l = _pad_mu(W["mu_shift"])
    g2p = jnp.pad(W["g2"], ((0, G_PAD - LORA_G), (0, 0)))
    prep_params = [mu_r, mu_k, mu_v, mu_l, W["w0"], W["a0"], W["k_k"], W["k_a"], W["w2"], W["a2"], g2p]
    r_, dec, kmod, v_, aa, bb, gg = _rwkv_prep(P, prep_params)
    y_scan, states, late = _cscan_fwd(r_, dec, kmod, v_, aa, bb, gather=late_shards)
    W.update({n: _full_weight(g, axis) for (n, axis), g in zip(LATE, late)})

    o_g, l_g = zip(*[_att_fwd(P, g) for g in range(len(ATT_PATTERNS))])
    att = _att_combine_fwd(o_g, l_g)
    y_att = _mm(att, W["w_att_out"], "nn", F32, "att_out")
    r_k = W["r_k"].reshape(1, D)
    rw = _rwkv_post(y_scan, r_, kmod, v_, gg, W["lnx_w"], W["lnx_b"], r_k)
    y_rwkv = _mm(rw, W["w_rwkv_out"], "nn", F32, "rwkv_out")

    bga, bgr = W["b_gate"][:, :D], W["b_gate"][:, D:]
    mix = _gate_fwd(P, bga, bgr, y_att, y_rwkv)
    mo = _mm(mix, W["w_o"], "nn", F32, "mix_out")
    x2, h2, rstd2 = _norm_fwd(x, mo, gt1, W["norm2_w"], sc2, sh2, "norm2_fwd")
    u = _mm(h2, W["w_up"], "nn", F32, "ffn_up")
    conv_w8 = jnp.pad(W["conv_w"], ((0, SUBLANES - 3), (0, 0)))
    act = _conv_fwd(u, conv_w8, W["conv_b"])
    f = _mm(act, W["w_down"], "nn", F32, "ffn_down")
    loss_blk, dx3, df, dgt2, G["norm_f_w"] = _final(x2, f, gt2, W["norm_f_w"], target)
    loss = loss_blk[0, 0]

    dact = _mm(df, W["w_down"], "nt", BF16, "ffn_down_dx")
    G["w_down"] = _mm(act, df, "tn", F32, "ffn_down_dw")
    duc, dwg, dwv, dbg, dbv = _conv_bwd_a(dact, u, conv_w8, W["conv_b"])
    G["conv_w"] = jnp.concatenate([dwg[0:3], dwv[0:3]], axis=1)
    G["conv_b"] = jnp.concatenate([dbg, dbv], axis=1)
    du = _conv_bwd_b(duc, conv_w8)
    dh2 = _mm(du, W["w_up"], "nt", F32, "ffn_up_dx")
    G["w_up"] = _mm(h2, du, "tn", F32, "ffn_up_dw")
    dx2, dsh2, dsc2, G["norm2_w"], dmo, dgt1 = _norm_bwd(dh2, x2, rstd2, W["norm2_w"], sc2, dx3, mo, gt1, "norm2_bwd")
    dmix = _mm(dmo, W["w_o"], "nt", F32, "mix_out_dx")
    G["w_o"] = _mm(mix, dmo, "tn", F32, "mix_out_dw")
    dy_att, dy_rwkv, dpga, dpgr, dbga, dbgr = _gate_bwd(dmix, P, bga, bgr, y_att, y_rwkv)
    G["b_gate"] = jnp.concatenate([dbga, dbgr], axis=1)

    datt = _mm(dy_att, W["w_att_out"], "nt", F32, "att_out_dx")
    G["w_att_out"] = _mm(att, dy_att, "tn", F32, "att_out_dw")
    dcomb = _att_combine_bwd(datt, o_g, l_g)
    dp_att = []
    for g in range(len(ATT_PATTERNS)):
        dp_att += _att_bwd(P, o_g[g], l_g[g], dcomb[g], dcomb[3 + g], g)

    drw = _mm(dy_rwkv, W["w_rwkv_out"], "nt", F32, "rwkv_out_dx")
    G["w_rwkv_out"] = _mm(rw, dy_rwkv, "tn", F32, "rwkv_out_dw")
    dy_scan, dr1, dk1, dv1, dgg, G["lnx_w"], G["lnx_b"], drk = _rwkv_post_bwd(drw, y_scan, r_, kmod, v_, gg, W["lnx_w"], W["lnx_b"], r_k)
    G["r_k"] = drk.reshape(W["r_k"].shape)
    late_blocks = [_owner_blocks(G[n], axis) for n, axis in LATE] if late_shards else []
    (dr2, ddec, dk2, dv2, daa, dbb), late_parts = _cscan_bwd(r_, dec, kmod, v_, aa, bb, states, dy_scan, scatter=late_blocks)
    pb = _rwkv_prep_bwd(P, prep_params, [dr2, ddec, dk2, dv2, daa, dbb, dgg], [dr1, None, dk1, dv1, None, None, None])
    dz, dzp, dpar = pb[0:4], pb[4:8], pb[8:]
    dp_rkv = [_shift_add(dz[i], dzp[i]) for i in range(3)]
    dp_lora = _shift_add(dz[3], dzp[3])
    dmu_r, dmu_k, dmu_v, dmu_l, G["w0"], G["a0"], G["k_k"], G["k_a"], G["w2"], G["a2"], dg2p = dpar
    G["g2"] = dg2p[0:LORA_G]
    G["mu_shift"] = jnp.concatenate([dmu_r, dmu_k, dmu_v, dmu_l[:, :LORA_W + LORA_A], dmu_l[:, LANES:LANES + LORA_G]], axis=1)

    dP = jnp.concatenate(dp_rkv + [dpga, dpgr] + dp_att + [dp_lora], axis=1)
    G["w_in"] = _unpad_w_in(_mm(h1, dP, "tn", F32, "proj_in_dw"))
    if late_shards:
        dh1, (w_in_parts,) = _mm(dP, w_in_p, "nt", F32, "proj_in_dx", scatter=[_owner_blocks(G["w_in"], 1)])
        done = dict(zip([n for n, _ in LATE] + ["w_in"], list(late_parts) + [w_in_parts]))
    else:
        dh1, done = _mm(dP, w_in_p, "nt", F32, "proj_in_dx"), {}
    grad_x, dsh1, dsc1, G["norm1_w"] = _norm_bwd(dh1, x, rstd1, W["norm1_w"], sc1, dx2, None, None, "norm1_bwd")
    dada = jnp.concatenate([dsh1, dsc1, dgt1, dsh2, dsc2, dgt2], axis=1)
    G["b_ada"] = dada
    return loss, grad_x, G, done


def _full_weight(gathered, axis):
    _, rows, cols = gathered.shape
    if axis == 0:
        return gathered.reshape(N_DEV * rows, cols)
    return gathered.transpose(1, 0, 2).reshape(rows, N_DEV * cols)


def _owner_blocks(g, axis):
    rows, cols = g.shape
    g = g.astype(BF16)
    if axis == 0:
        return g.reshape(N_DEV, rows // N_DEV, cols)
    return g.reshape(rows, N_DEV, cols // N_DEV).transpose(1, 0, 2)


def kernel(x, c, w_ada, b_ada, norm1_w, w_in, b_gate, mu_shift, w0, w2, a0, a2, g2, k_k, k_a, r_k, lnx_w, lnx_b, w_att_out, w_rwkv_out, w_o, norm2_w, w_up, conv_w, conv_b, w_down, norm_f_w, loss_target, m_w_ada, m_b_ada, m_norm1_w, m_w_in, m_b_gate, m_mu_shift, m_w0, m_w2, m_a0, m_a2, m_g2, m_k_k, m_k_a, m_r_k, m_lnx_w, m_lnx_b, m_w_att_out, m_w_rwkv_out, m_w_o, m_norm2_w, m_w_up, m_conv_w, m_conv_b, m_w_down, m_norm_f_w, v_w_ada, v_b_ada, v_norm1_w, v_w_in, v_b_gate, v_mu_shift, v_w0, v_w2, v_a0, v_a2, v_g2, v_k_k, v_k_a, v_r_k, v_lnx_w, v_lnx_b, v_w_att_out, v_w_rwkv_out, v_w_o, v_norm2_w, v_w_up, v_conv_w, v_conv_b, v_w_down, v_norm_f_w):
    env = dict(locals())
    w_shard = {n: env[n] for n in WEIGHTS}
    m_shard = {n: env["m_" + n] for n in WEIGHTS}
    v_shard = {n: env["v_" + n] for n in WEIGHTS}

    c_all, *gathered = _gather_via_sibling([c] + [w_shard[n][0].astype(BF16) for n, _ in EARLY], "gather_weights")
    c_all = c_all.reshape(N_DEV, D)
    W = {n: _full_weight(g, axis) for (n, axis), g in zip(EARLY, gathered)}
    for n in REPLICATED:
        W[n] = w_shard[n].reshape(1, -1) if n != "r_k" else w_shard[n][0]
    ada_cols = _ada_partial(c_all, w_shard["w_ada"][0])
    ada_rows, = _exchange([ada_cols[:, None, :]], False, "ada_rows")
    ada = _ada_bias(ada_rows.reshape(1, -1), w_shard["b_ada"])

    late_shards = [w_shard[n][0].astype(BF16) for n, _ in LATE]
    loss, grad_x, G, parts = _local_step(x[0], ada, W, late_shards, loss_target[0])
    loss = lax.psum(loss, ("x", "y", "c"))

    small, slayout = _pack([G[n].reshape(-1) for n in REPLICATED])
    sparts, dada_all = _exchange([small, G["b_ada"].reshape(N_DEV, 1, -1)], [True, False], "gather_small_grads")
    parts["w_ada"] = _ada_wgrad(c_all.T, dada_all.reshape(N_DEV, -1))[None]

    rest = [(n, axis) for n, axis in SHARDED if n not in parts]
    parts.update(zip([n for n, _ in rest], _exchange([_owner_blocks(G[n], axis) for n, axis in rest], False, "scatter_grads")))
    out = {}
    for n, p in parts.items():
        res = _sum_adam(p, w_shard[n][0], m_shard[n][0], v_shard[n][0], "adam_" + n)
        for kind, a in zip(("grad", "delta", "new_m", "new_v"), res):
            out[kind, n] = a[None]

    sw, _ = _pack([w_shard[n].reshape(-1) for n in REPLICATED])
    sm, _ = _pack([m_shard[n].reshape(-1) for n in REPLICATED])
    sv, _ = _pack([v_shard[n].reshape(-1) for n in REPLICATED])
    res = _sum_adam(sparts, sw, sm, sv, "adam_replicated")
    for kind, buf in zip(("grad", "delta", "new_m", "new_v"), res):
        for n, a in zip(REPLICATED, _unpack(buf, slayout)):
            out[kind, n] = a.reshape(w_shard[n].shape)

    return (loss, grad_x[None], *[out[kind, n] for kind in ("grad", "delta", "new_m", "new_v") for n in WEIGHTS])
```

```python
import functools
import math

import jax
import jax.numpy as jnp
from jax import lax
from jax.experimental import pallas as pl
from jax.experimental.pallas import tpu as pltpu

F32 = jnp.float32
BF16 = jnp.bfloat16

D = 1024
HEAD = 64
ATT_PATTERNS = ((128, 1), (512, 4), (2048, 16))
ATT_HEADS = 8
ATT_W = ATT_HEADS * HEAD
ATT_IN = 3 * 3 * ATT_W
QBLK = 128
N_HEADS = D // HEAD
LORA_W, LORA_A, LORA_G = 64, 64, 160
RWKV_IN = 3 * D + LORA_W + LORA_A + LORA_G
N_IN = ATT_IN + RWKV_IN + 2 * D
D_FF = 2816
RMS_EPS = 1e-6
GN_EPS = 64e-5
N_DEV = 8
LANES = 128
SUBLANES = 8

C_R, C_K, C_V, C_GA, C_GR = 0, 1024, 2048, 3072, 4096
C_ATT = 5120
C_LORA = C_ATT + ATT_IN
LORA_PAD = 512
G_PAD = 256
N_PAD = C_LORA + LORA_PAD

ADAM_LR, ADAM_B1, ADAM_B2, ADAM_EPS, ADAM_WD, ADAM_STEP = 0.001, 0.9, 0.999, 1e-08, 0.01, 10

VMEM_LIMIT = 56 * 1024 * 1024

_MESH = pl.DeviceIdType.MESH


def _cparams(sem):
    return pltpu.CompilerParams(dimension_semantics=sem, vmem_limit_bytes=VMEM_LIMIT)


def _tile(dim, pref):
    if dim <= pref:
        return dim
    best = None
    for t in range(LANES, pref + 1, LANES):
        if dim % t == 0:
            best = t
    assert best is not None, dim
    return best


MM_TILES = {"nn": (1024, 1408, 1408), "nt": (512, 2048, 1408), "tn": (1408, 1408, 1024)}


def _mm(a, b, mode, out_dtype, name, scatter=()):
    if mode == "nn":
        (M, K), (K2, N) = a.shape, b.shape
    elif mode == "nt":
        (M, K), (N, K2) = a.shape, b.shape
    else:
        (K, M), (K2, N) = a.shape, b.shape
    assert K == K2, (a.shape, b.shape, mode)
    tm, tn, tk = (_tile(dim, pref) for dim, pref in zip((M, N, K), MM_TILES[mode]))
    nk = K // tk
    grid = (M // tm, N // tn, nk)
    n_x = len(scatter)
    dims = {"nn": (((1,), (0,)), ((), ())), "nt": (((1,), (1,)), ((), ())), "tn": (((0,), (0,)), ((), ()))}[mode]

    def body(*refs):
        a_ref, b_ref = refs[:2]
        o_ref, acc_ref = refs[2 + n_x], refs[3 + 2 * n_x]
        finish = _hosted_exchange(refs[2:2 + n_x] + refs[3 + n_x:3 + 2 * n_x] + refs[4 + 2 * n_x:], n_x, False, grid)
        k = pl.program_id(2)
        part = lax.dot_general(a_ref[...].astype(BF16), b_ref[...].astype(BF16), dims,
                               preferred_element_type=F32)
        if nk == 1:
            o_ref[...] = part.astype(o_ref.dtype)
        else:
            @pl.when(k == 0)
            def _():
                acc_ref[...] = part

            @pl.when(jnp.logical_and(k > 0, k < nk - 1))
            def _():
                acc_ref[...] += part

            @pl.when(k == nk - 1)
            def _():
                o_ref[...] = (acc_ref[...] + part).astype(o_ref.dtype)
        finish()

    a_spec = pl.BlockSpec((tk, tm), lambda i, j, k: (k, i)) if mode == "tn" else pl.BlockSpec((tm, tk), lambda i, j, k: (i, k))
    b_spec = pl.BlockSpec((tn, tk), lambda i, j, k: (j, k)) if mode == "nt" else pl.BlockSpec((tk, tn), lambda i, j, k: (k, j))
    any_spec = pl.BlockSpec(memory_space=pl.ANY)
    outs = pl.pallas_call(
        body, name=name, grid=grid,
        in_specs=[a_spec, b_spec] + [any_spec] * n_x,
        out_specs=[pl.BlockSpec((tm, tn), lambda i, j, k: (i, j))] + [any_spec] * n_x,
        out_shape=[jax.ShapeDtypeStruct((M, N), out_dtype)] + _exchange_shapes(scatter, False),
        scratch_shapes=[pltpu.VMEM((tm, tn) if nk > 1 else (SUBLANES, LANES), F32)] + (_exchange_scratch(n_x) if n_x else []),
        compiler_params=_cparams(("arbitrary",) * 3 if n_x else ("parallel", "parallel", "arbitrary")),
    )(a, b, *scatter)
    return (outs[0], outs[1:]) if n_x else outs[0]


def _rows(tm, w, col=0):
    return pl.BlockSpec((tm, w), lambda i: (i, col))


def _full(shape):
    return pl.BlockSpec(shape, lambda i: (0,) * len(shape))


def _prev8(tm, w, col=0):
    return pl.BlockSpec((SUBLANES, w), lambda i: (jnp.maximum(i * (tm // SUBLANES) - 1, 0), col))


def _next8(tm, w, n_rows, col=0):
    last = n_rows // SUBLANES - 1
    return pl.BlockSpec((SUBLANES, w), lambda i: (jnp.minimum((i + 1) * (tm // SUBLANES), last), col))


def _shift_down(x, halo, k, first):
    rolled = pltpu.roll(x, k, 0)
    row = lax.broadcasted_iota(jnp.int32, x.shape, 0)
    out = rolled
    for j in range(k):
        h = jnp.where(first, 0.0, halo[SUBLANES - k + j:SUBLANES - k + j + 1, :])
        out = jnp.where(row == j, h, out)
    return out


def _shift_up(x, halo, k, last):
    n = x.shape[0]
    rolled = pltpu.roll(x, n - k, 0)
    row = lax.broadcasted_iota(jnp.int32, x.shape, 0)
    out = rolled
    for j in range(k):
        h = jnp.where(last, 0.0, halo[j:j + 1, :])
        out = jnp.where(row == n - k + j, h, out)
    return out


def _acc(ref, val, first):
    @pl.when(first)
    def _():
        ref[...] = val

    @pl.when(jnp.logical_not(first))
    def _():
        ref[...] += val


def _colsum(x):
    return jnp.sum(x, axis=0, keepdims=True)


def _norm_fwd(x, mo, gt, nw, sc, sh, name, tm=256):
    S = x.shape[0]
    has_res = mo is not None

    def body(*refs):
        if has_res:
            x_ref, mo_ref, gt_ref, nw_ref, sc_ref, sh_ref, x2_ref, h_ref, rs_ref = refs
            x2 = x_ref[...] + gt_ref[...] * mo_ref[...]
            x2_ref[...] = x2
        else:
            x_ref, nw_ref, sc_ref, sh_ref, h_ref, rs_ref = refs
            x2 = x_ref[...]
        rstd = lax.rsqrt(jnp.mean(x2 * x2, axis=-1, keepdims=True) + RMS_EPS)
        rs_ref[...] = rstd
        h_ref[...] = ((x2 * rstd * nw_ref[...]) * (1.0 + sc_ref[...]) + sh_ref[...]).astype(BF16)

    vec = _full((1, D))
    ins = [x, mo, gt, nw, sc, sh] if has_res else [x, nw, sc, sh]
    in_specs = [_rows(tm, D), _rows(tm, D), vec, vec, vec, vec] if has_res else [_rows(tm, D), vec, vec, vec]
    outs = [jax.ShapeDtypeStruct((S, D), BF16), jax.ShapeDtypeStruct((S, 1), F32)]
    out_specs = [_rows(tm, D), _rows(tm, 1)]
    if has_res:
        outs = [jax.ShapeDtypeStruct((S, D), F32)] + outs
        out_specs = [_rows(tm, D)] + out_specs
    return pl.pallas_call(body, name=name, grid=(S // tm,), in_specs=in_specs, out_specs=out_specs,
                          out_shape=outs, compiler_params=_cparams(("parallel",)))(*ins)


def _norm_bwd(dh, xin, rstd, nw, sc, dres, mo, gt, name, tm=256):
    S = xin.shape[0]
    has_res = mo is not None

    def body(*refs):
        if has_res:
            dh_ref, x_ref, rs_ref, nw_ref, sc_ref, dres_ref, mo_ref, gt_ref, dx_ref, dsh_ref, dsc_ref, dnw_ref, dmo_ref, dgt_ref = refs
        else:
            dh_ref, x_ref, rs_ref, nw_ref, sc_ref, dres_ref, dx_ref, dsh_ref, dsc_ref, dnw_ref = refs
        first = pl.program_id(0) == 0
        dh = dh_ref[...]
        rstd = rs_ref[...]
        n = x_ref[...] * rstd
        w = nw_ref[...]
        _acc(dsh_ref, _colsum(dh), first)
        _acc(dsc_ref, _colsum(dh * (n * w)), first)
        dnw = dh * (1.0 + sc_ref[...])
        _acc(dnw_ref, _colsum(dnw * n), first)
        dn = dnw * w
        dx = dres_ref[...] + rstd * (dn - n * jnp.mean(dn * n, axis=-1, keepdims=True))
        dx_ref[...] = dx
        if has_res:
            dmo_ref[...] = (dx * gt_ref[...]).astype(BF16)
            _acc(dgt_ref, _colsum(dx * mo_ref[...]), first)

    vec = _full((1, D))
    vshape = jax.ShapeDtypeStruct((1, D), F32)
    ins = [dh, xin, rstd, nw, sc, dres] + ([mo, gt] if has_res else [])
    in_specs = [_rows(tm, D), _rows(tm, D), _rows(tm, 1), vec, vec, _rows(tm, D)] + ([_rows(tm, D), vec] if has_res else [])
    outs = [jax.ShapeDtypeStruct((S, D), F32), vshape, vshape, vshape]
    out_specs = [_rows(tm, D), vec, vec, vec]
    if has_res:
        outs += [jax.ShapeDtypeStruct((S, D), BF16), vshape]
        out_specs += [_rows(tm, D), vec]
    return pl.pallas_call(body, name=name, grid=(S // tm,), in_specs=in_specs, out_specs=out_specs,
                          out_shape=outs, compiler_params=_cparams(("arbitrary",)))(*ins)


def _final(x2, f, gt2, nfw, target, tm=256):
    S = x2.shape[0]

    def body(x2_ref, f_ref, gt_ref, w_ref, t_ref, loss_ref, dx_ref, df_ref, dgt_ref, dw_ref):
        first = pl.program_id(0) == 0
        f = f_ref[...]
        gt = gt_ref[...]
        w = w_ref[...]
        x3 = x2_ref[...] + gt * f
        rstd = lax.rsqrt(jnp.mean(x3 * x3, axis=-1, keepdims=True) + RMS_EPS)
        n = x3 * rstd
        e = n * w - t_ref[...]
        part = 0.5 * jnp.sum(jnp.mean(e * e, axis=-1, keepdims=True), axis=0, keepdims=True)
        _acc(loss_ref, jnp.broadcast_to(part, (SUBLANES, LANES)), first)
        dy = e * (1.0 / D)
        _acc(dw_ref, _colsum(dy * n), first)
        dn = dy * w
        dx = rstd * (dn - n * jnp.mean(dn * n, axis=-1, keepdims=True))
        dx_ref[...] = dx
        df_ref[...] = (dx * gt).astype(BF16)
        _acc(dgt_ref, _colsum(dx * f), first)

    vec = _full((1, D))
    vshape = jax.ShapeDtypeStruct((1, D), F32)
    return pl.pallas_call(
        body, name="final_loss", grid=(S // tm,),
        in_specs=[_rows(tm, D), _rows(tm, D), vec, vec, _rows(tm, D)],
        out_specs=[_full((SUBLANES, LANES)), _rows(tm, D), _rows(tm, D), vec, vec],
        out_shape=[jax.ShapeDtypeStruct((SUBLANES, LANES), F32), jax.ShapeDtypeStruct((S, D), F32),
                   jax.ShapeDtypeStruct((S, D), BF16), vshape, vshape],
        compiler_params=_cparams(("arbitrary",)))(x2, f, gt2, nfw, target)


def _gate_fwd(P, bga, bgr, y_att, y_rwkv, tm=256):
    S = P.shape[0]

    def body(pa_ref, pr_ref, ba_ref, br_ref, ya_ref, yr_ref, mix_ref):
        ga = jax.nn.sigmoid(pa_ref[...] + ba_ref[...])
        gr = jax.nn.sigmoid(pr_ref[...] + br_ref[...])
        mix_ref[...] = (ga * ya_ref[...] + gr * yr_ref[...]).astype(BF16)

    vec = _full((1, D))
    return pl.pallas_call(
        body, name="gate_fwd", grid=(S // tm,),
        in_specs=[_rows(tm, D, C_GA // D), _rows(tm, D, C_GR // D), vec, vec, _rows(tm, D), _rows(tm, D)],
        out_specs=_rows(tm, D), out_shape=jax.ShapeDtypeStruct((S, D), BF16),
        compiler_params=_cparams(("parallel",)))(P, P, bga, bgr, y_att, y_rwkv)


def _gate_bwd(dmix, P, bga, bgr, y_att, y_rwkv, tm=256):
    S = P.shape[0]

    def body(dm_ref, pa_ref, pr_ref, ba_ref, br_ref, ya_ref, yr_ref, dya_ref, dyr_ref, dpa_ref, dpr_ref, dba_ref, dbr_ref):
        first = pl.program_id(0) == 0
        dm = dm_ref[...]
        ga = jax.nn.sigmoid(pa_ref[...] + ba_ref[...])
        gr = jax.nn.sigmoid(pr_ref[...] + br_ref[...])
        dya_ref[...] = (dm * ga).astype(BF16)
        dyr_ref[...] = (dm * gr).astype(BF16)
        dpa = dm * ya_ref[...] * ga * (1.0 - ga)
        dpr = dm * yr_ref[...] * gr * (1.0 - gr)
        dpa_ref[...] = dpa.astype(BF16)
        dpr_ref[...] = dpr.astype(BF16)
        _acc(dba_ref, _colsum(dpa), first)
        _acc(dbr_ref, _colsum(dpr), first)

    vec = _full((1, D))
    row = _rows(tm, D)
    rshape = jax.ShapeDtypeStruct((S, D), BF16)
    vshape = jax.ShapeDtypeStruct((1, D), F32)
    return pl.pallas_call(
        body, name="gate_bwd", grid=(S // tm,),
        in_specs=[row, _rows(tm, D, C_GA // D), _rows(tm, D, C_GR // D), vec, vec, row, row],
        out_specs=[row, row, row, row, vec, vec],
        out_shape=[rshape, rshape, rshape, rshape, vshape, vshape],
        compiler_params=_cparams(("arbitrary",)))(dmix, P, P, bga, bgr, y_att, y_rwkv)


CONV_TN = D_FF // 2


def _conv_fwd(u, conv_w8, conv_b, tm=256, tn=CONV_TN):
    S = u.shape[0]
    nj = D_FF // tn

    def conv(u_ref, h_ref, w_ref, b_ref, first):
        u = u_ref[...]
        h = h_ref[...]
        w = w_ref[...]
        return b_ref[...] + w[0:1] * _shift_down(u, h, 2, first) + w[1:2] * _shift_down(u, h, 1, first) + w[2:3] * u

    def body(ug_ref, hg_ref, uv_ref, hv_ref, wg_ref, wv_ref, bg_ref, bv_ref, act_ref):
        first = pl.program_id(0) == 0
        g = conv(ug_ref, hg_ref, wg_ref, bg_ref, first)
        v = conv(uv_ref, hv_ref, wv_ref, bv_ref, first)
        act_ref[...] = (g * jax.nn.sigmoid(g) * v).astype(BF16)

    blk = lambda off: pl.BlockSpec((tm, tn), lambda i, j: (i, j + off))
    halo = lambda off: pl.BlockSpec((SUBLANES, tn), lambda i, j: (jnp.maximum(i * (tm // SUBLANES) - 1, 0), j + off))
    wsp = lambda off: pl.BlockSpec((SUBLANES, tn), lambda i, j: (0, j + off))
    bsp = lambda off: pl.BlockSpec((1, tn), lambda i, j: (0, j + off))
    return pl.pallas_call(
        body, name="conv_fwd", grid=(S // tm, nj),
        in_specs=[blk(0), halo(0), blk(nj), halo(nj), wsp(0), wsp(nj), bsp(0), bsp(nj)],
        out_specs=pl.BlockSpec((tm, tn), lambda i, j: (i, j)),
        out_shape=jax.ShapeDtypeStruct((S, D_FF), BF16),
        compiler_params=_cparams(("parallel", "parallel")))(u, u, u, u, conv_w8, conv_w8, conv_b, conv_b)


def _conv_bwd_a(dact, u, conv_w8, conv_b, tm=256, tn=CONV_TN):
    S = u.shape[0]
    nj = D_FF // tn

    def half(u_ref, h_ref, w_ref, b_ref, first):
        u = u_ref[...]
        h = h_ref[...]
        w = w_ref[...]
        u2, u1 = _shift_down(u, h, 2, first), _shift_down(u, h, 1, first)
        return b_ref[...] + w[0:1] * u2 + w[1:2] * u1 + w[2:3] * u, (u2, u1, u)

    def wgrad(d, taps):
        z = jnp.zeros((SUBLANES - 3, d.shape[1]), F32)
        return jnp.concatenate([_colsum(d * taps[0]), _colsum(d * taps[1]), _colsum(d * taps[2]), z], axis=0)

    def body(da_ref, ug_ref, hg_ref, uv_ref, hv_ref, wg_ref, wv_ref, bg_ref, bv_ref,
             d_ref, dwg_ref, dwv_ref, dbg_ref, dbv_ref):
        first = pl.program_id(1) == 0
        g, tg = half(ug_ref, hg_ref, wg_ref, bg_ref, first)
        v, tv = half(uv_ref, hv_ref, wv_ref, bv_ref, first)
        da = da_ref[...].astype(F32)
        sg = jax.nn.sigmoid(g)
        dg = da * v * (sg * (1.0 + g * (1.0 - sg)))
        dv = da * (g * sg)
        d_ref[0] = dg
        d_ref[1] = dv
        _acc(dwg_ref, wgrad(dg, tg), first)
        _acc(dwv_ref, wgrad(dv, tv), first)
        _acc(dbg_ref, _colsum(dg), first)
        _acc(dbv_ref, _colsum(dv), first)

    blk = lambda off: pl.BlockSpec((tm, tn), lambda j, i: (i, j + off))
    halo = lambda off: pl.BlockSpec((SUBLANES, tn), lambda j, i: (jnp.maximum(i * (tm // SUBLANES) - 1, 0), j + off))
    wsp = lambda off: pl.BlockSpec((SUBLANES, tn), lambda j, i: (0, j + off))
    bsp = lambda off: pl.BlockSpec((1, tn), lambda j, i: (0, j + off))
    f = jax.ShapeDtypeStruct
    outs = pl.pallas_call(
        body, name="conv_bwd_a", grid=(nj, S // tm),
        in_specs=[pl.BlockSpec((tm, tn), lambda j, i: (i, j)), blk(0), halo(0), blk(nj), halo(nj), wsp(0), wsp(nj), bsp(0), bsp(nj)],
        out_specs=[pl.BlockSpec((2, tm, tn), lambda j, i: (0, i, j)),
                   pl.BlockSpec((SUBLANES, tn), lambda j, i: (0, j)), pl.BlockSpec((SUBLANES, tn), lambda j, i: (0, j)),
                   pl.BlockSpec((1, tn), lambda j, i: (0, j)), pl.BlockSpec((1, tn), lambda j, i: (0, j))],
        out_shape=[f((2, S, D_FF), F32), f((SUBLANES, D_FF), F32), f((SUBLANES, D_FF), F32),
                   f((1, D_FF), F32), f((1, D_FF), F32)],
        compiler_params=_cparams(("parallel", "arbitrary")))(dact, u, u, u, u, conv_w8, conv_w8, conv_b, conv_b)
    return outs


def _conv_bwd_b(duc, conv_w8, tm=256, tn=CONV_TN):
    _, S, W = duc.shape
    nj = W // tn
    n_rows = S // tm

    def body(d_ref, h_ref, w_ref, o_ref):
        last = pl.program_id(0) == n_rows - 1
        d = d_ref[...]
        h = h_ref[...]
        w = w_ref[...]
        o_ref[...] = (w[2:3] * d + w[1:2] * _shift_up(d, h, 1, last) + w[0:1] * _shift_up(d, h, 2, last)).astype(BF16)

    last_tile = S // SUBLANES - 1
    return pl.pallas_call(
        body, name="conv_bwd_b", grid=(n_rows, 2 * nj),
        in_specs=[pl.BlockSpec((None, tm, tn), lambda i, j: (j // nj, i, j % nj)),
                  pl.BlockSpec((None, SUBLANES, tn), lambda i, j: (j // nj, jnp.minimum((i + 1) * (tm // SUBLANES), last_tile), j % nj)),
                  pl.BlockSpec((SUBLANES, tn), lambda i, j: (0, j))],
        out_specs=pl.BlockSpec((tm, tn), lambda i, j: (i, j)),
        out_shape=jax.ShapeDtypeStruct((S, 2 * W), BF16),
        compiler_params=_cparams(("parallel", "parallel")))(duc, duc, conv_w8)


ATT_SCALE = HEAD ** -0.5
NEG = -1e30
ATT_PAIRS = ATT_HEADS // 2


def _att_rows(n, d, S):
    per = S // (QBLK * d)
    r, m = n // per, n % per
    cur = pl.ds(m * (QBLK * d) + r, QBLK, stride=d)
    prv = pl.ds(jnp.maximum(m - 1, 0) * (QBLK * d) + r, QBLK, stride=d)
    return cur, prv, m > 0


def _att_slab(g, j):
    return (C_ATT + g * 3 * ATT_W + j * ATT_W) // LANES


def _heads(x):
    return x[:, 0:HEAD], x[:, HEAD:2 * HEAD]


ATT_NB = 4


def _stack(tiles):
    return jnp.concatenate([t[None] for t in tiles], axis=0)


def _att_operands(i, d, S, *sources):
    rows, has = [], []
    tiles = [[] for _ in sources]
    for bb in range(ATT_NB):
        cur, prv, has_prev = _att_rows(i * ATT_NB + bb, d, S)
        rows.append((cur, prv))
        has.append(has_prev)
        for t, (ref, use_cur) in zip(tiles, sources):
            t += _heads(ref[cur if use_cur else prv, :].astype(BF16))
    return rows, has, [_stack(t) for t in tiles]


def _att_mask(s_c, s_p, has_prev):
    qi = lax.broadcasted_iota(jnp.int32, (QBLK, QBLK), 0)
    kj = lax.broadcasted_iota(jnp.int32, (QBLK, QBLK), 1)
    s_c = jnp.where(kj <= qi, s_c * ATT_SCALE, NEG)
    s_p = jnp.where(jnp.logical_and(kj >= qi, has_prev), s_p * ATT_SCALE, NEG)
    return s_c, s_p


def _att_fwd(P, g):
    S = P.shape[0]
    d = ATT_PATTERNS[g][1]

    def body(q_ref, k_ref, v_ref, o_ref, l_ref):
        def group(i, carry):
            rows, has, (q, kc, kp, vc, vp) = _att_operands(i, d, S, (q_ref, True), (k_ref, True), (k_ref, False),
                                                           (v_ref, True), (v_ref, False))
            s_c_all, s_p_all = _dot16(q, kc, "nt"), _dot16(q, kp, "nt")
            p_c, p_p, den, lse = [], [], [], []
            for e in range(2 * ATT_NB):
                s_c, s_p = _att_mask(s_c_all[e], s_p_all[e], has[e // 2])
                m = jnp.maximum(jnp.max(s_c, axis=1, keepdims=True), jnp.max(s_p, axis=1, keepdims=True))
                pc, pp = jnp.exp(s_c - m), jnp.exp(s_p - m)
                den.append(jnp.sum(pc, axis=1, keepdims=True) + jnp.sum(pp, axis=1, keepdims=True))
                lse.append(jnp.broadcast_to(m + jnp.log(den[e]), (QBLK, HEAD)))
                p_c.append(pc)
                p_p.append(pp)
            num = _dot16(_stack(p_c), vc, "nn") + _dot16(_stack(p_p), vp, "nn")
            for bb, (cur, _) in enumerate(rows):
                o_ref[cur, :] = jnp.concatenate([num[2 * bb] / den[2 * bb], num[2 * bb + 1] / den[2 * bb + 1]], axis=1)
                l_ref[cur, :] = jnp.concatenate(lse[2 * bb:2 * bb + 2], axis=1)
            return carry

        lax.fori_loop(0, S // QBLK // ATT_NB, group, 0)

    slab = lambda j: pl.BlockSpec((S, LANES), lambda i: (0, _att_slab(g, j) + i))
    out = pl.BlockSpec((S, LANES), lambda i: (0, i))
    shp = jax.ShapeDtypeStruct((S, ATT_W), F32)
    return pl.pallas_call(body, name=f"att_fwd_g{g}", grid=(ATT_PAIRS,), in_specs=[slab(0), slab(1), slab(2)],
                          out_specs=[out, out], out_shape=[shp, shp], compiler_params=_cparams(("parallel",)))(P, P, P)


def _att_bwd(P, o, l, do, dl, g):
    S = P.shape[0]
    d = ATT_PATTERNS[g][1]

    def body(q_ref, k_ref, v_ref, o_ref, l_ref, do_ref, dl_ref, dq_ref, dk_ref, dv_ref, dq_acc, dk_acc, dv_acc):
        dk_acc[...] = jnp.zeros_like(dk_acc)
        dv_acc[...] = jnp.zeros_like(dv_acc)

        def group(i, carry):
            rows, has, (q, kc, kp, vc, vp, dob) = _att_operands(
                i, d, S, (q_ref, True), (k_ref, True), (k_ref, False), (v_ref, True), (v_ref, False), (do_ref, True))
            s_c_all, s_p_all = _dot16(q, kc, "nt"), _dot16(q, kp, "nt")
            dp_c_all, dp_p_all = _dot16(dob, vc, "nt"), _dot16(dob, vp, "nt")
            p_c, p_p, ds_c, ds_p = [], [], [], []
            for bb, (cur, _) in enumerate(rows):
                dd2 = do_ref[cur, :] * o_ref[cur, :] - dl_ref[cur, :]
                for h, (dd, lse) in enumerate(zip(_heads(dd2), _heads(l_ref[cur, :]))):
                    e = 2 * bb + h
                    s_c, s_p = _att_mask(s_c_all[e], s_p_all[e], has[bb])
                    pc, pp = jnp.exp(s_c - lse[:, 0:1]), jnp.exp(s_p - lse[:, 0:1])
                    delta = jnp.sum(dd, axis=1, keepdims=True)
                    p_c.append(pc)
                    p_p.append(pp)
                    ds_c.append(pc * (dp_c_all[e] - delta) * ATT_SCALE)
                    ds_p.append(pp * (dp_p_all[e] - delta) * ATT_SCALE)
            p_c, p_p, ds_c, ds_p = map(_stack, (p_c, p_p, ds_c, ds_p))
            dq = _dot16(ds_c, kc, "nn") + _dot16(ds_p, kp, "nn")
            dk_c, dk_p = _dot16(ds_c, q, "tn"), _dot16(ds_p, q, "tn")
            dv_c, dv_p = _dot16(p_c, dob, "tn"), _dot16(p_p, dob, "tn")
            pair = lambda x, bb: jnp.concatenate([x[2 * bb], x[2 * bb + 1]], axis=1)
            for bb, (cur, prv) in enumerate(rows):
                dq_acc[cur, :] = pair(dq, bb)
                dk_acc[cur, :] += pair(dk_c, bb)
                dv_acc[cur, :] += pair(dv_c, bb)
                dk_acc[prv, :] += pair(dk_p, bb)
                dv_acc[prv, :] += pair(dv_p, bb)
            return carry

        lax.fori_loop(0, S // QBLK // ATT_NB, group, 0)
        dq_ref[...] = dq_acc[...].astype(BF16)
        dk_ref[...] = dk_acc[...].astype(BF16)
        dv_ref[...] = dv_acc[...].astype(BF16)

    slab = lambda j: pl.BlockSpec((S, LANES), lambda i: (0, _att_slab(g, j) + i))
    blk128 = pl.BlockSpec((S, LANES), lambda i: (0, i))
    shp = jax.ShapeDtypeStruct((S, ATT_W), BF16)
    return pl.pallas_call(body, name=f"att_bwd_g{g}", grid=(ATT_PAIRS,),
                          in_specs=[slab(0), slab(1), slab(2)] + [blk128] * 4, out_specs=[blk128] * 3, out_shape=[shp] * 3,
                          scratch_shapes=[pltpu.VMEM((S, LANES), F32)] * 3,
                          compiler_params=_cparams(("parallel",)))(P, P, P, o, l, do, dl)


def _att_weights(l_refs):
    l0, l1, l2 = [r[...] for r in l_refs]
    m = jnp.maximum(jnp.maximum(l0, l1), l2)
    e = (jnp.exp(l0 - m), jnp.exp(l1 - m), jnp.exp(l2 - m))
    inv = 1.0 / (e[0] + e[1] + e[2])
    return [x * inv for x in e]


def _att_combine_fwd(os, ls, tm=512):
    S = os[0].shape[0]

    def body(o0, o1, o2, l0, l1, l2, a_ref):
        w = _att_weights((l0, l1, l2))
        a_ref[...] = (w[0] * o0[...] + w[1] * o1[...] + w[2] * o2[...]).astype(BF16)

    row = _rows(tm, ATT_W)
    return pl.pallas_call(body, name="att_combine_fwd", grid=(S // tm,), in_specs=[row] * 6, out_specs=row,
                          out_shape=jax.ShapeDtypeStruct((S, ATT_W), BF16),
                          compiler_params=_cparams(("parallel",)))(*os, *ls)


def _att_combine_bwd(da, os, ls, tm=512):
    S = da.shape[0]

    def body(da_ref, o0, o1, o2, l0, l1, l2, *out_refs):
        da = da_ref[...]
        w = _att_weights((l0, l1, l2))
        dw = (da * o0[...], da * o1[...], da * o2[...])
        mean = w[0] * dw[0] + w[1] * dw[1] + w[2] * dw[2]
        for g in range(3):
            out_refs[g][...] = w[g] * da
            out_refs[3 + g][...] = w[g] * (dw[g] - mean)

    row = _rows(tm, ATT_W)
    shp = jax.ShapeDtypeStruct((S, ATT_W), F32)
    return pl.pallas_call(body, name="att_combine_bwd", grid=(S // tm,), in_specs=[row] * 7, out_specs=[row] * 6,
                          out_shape=[shp] * 6, compiler_params=_cparams(("parallel",)))(da, *os, *ls)


@jax.custom_vjp
def _bdot(a, b):
    return jnp.dot(a.astype(BF16), b.astype(BF16), preferred_element_type=F32)


def _bdot_fwd(a, b):
    return _bdot(a, b), (a, b)


def _bdot_bwd(res, ct):
    a, b = res
    ct16 = ct.astype(BF16)
    da = lax.dot_general(ct16, b.astype(BF16), (((1,), (1,)), ((), ())), preferred_element_type=F32)
    db = lax.dot_general(a.astype(BF16), ct16, (((0,), (0,)), ((), ())), preferred_element_type=F32)
    return da, db


_bdot.defvjp(_bdot_fwd, _bdot_bwd)


def _two_piece_dot(x, m):
    hi = x.astype(BF16)
    lo = (x - hi.astype(F32)).astype(BF16)
    return jnp.dot(hi, m, preferred_element_type=F32) + jnp.dot(lo, m, preferred_element_type=F32)


def _head_sum_impl(x):
    sel = (lax.broadcasted_iota(jnp.int32, (D, LANES), 0) // HEAD == lax.broadcasted_iota(jnp.int32, (D, LANES), 1)).astype(BF16)
    sel_t = (lax.broadcasted_iota(jnp.int32, (LANES, D), 1) // HEAD == lax.broadcasted_iota(jnp.int32, (LANES, D), 0)).astype(BF16)
    return _two_piece_dot(_two_piece_dot(x, sel), sel_t)


@jax.custom_vjp
def _head_sum(x):
    return _head_sum_impl(x)


_head_sum.defvjp(lambda x: (_head_sum_impl(x), None), lambda _, ct: (_head_sum_impl(ct),))


def _softplus(z):
    return jnp.maximum(z, 0.0) + jnp.log(1.0 + jnp.exp(-jnp.abs(z)))


def _rwkv_prep_fn(zr, zrp, zk, zkp, zv, zvp, zl, zlp, mu_r, mu_k, mu_v, mu_l, w0, a0, k_k, k_a, w2, a2, g2p):
    r = zr + (zrp - zr) * mu_r
    k = zk + (zkp - zk) * mu_k
    v = zv + (zvp - zv) * mu_v
    lo = zl + (zlp - zl) * mu_l
    w_low, a_low, g_low = lo[:, 0:LORA_W], lo[:, LORA_W:LORA_W + LORA_A], lo[:, LANES:LANES + G_PAD]
    w_log = -_softplus(-(w0 + _bdot(jnp.tanh(w_low), w2))) - 0.5
    decay = -jnp.exp(w_log)
    a = jax.nn.sigmoid(a0 + _bdot(a_low, a2))
    g = _bdot(jax.nn.sigmoid(g_low), g2p)
    kmod = k * (1.0 + (a - 1.0) * k_a)
    kk = k * k_k
    kk = kk / jnp.maximum(jnp.sqrt(_head_sum(kk * kk)), 1e-12)
    return r, decay, kmod, v, -kk, kk * a, g


def _rwkv_prep_specs(tm):
    vec = _full((1, D))
    slabs = []
    for col in (C_R // D, C_K // D, C_V // D):
        slabs += [_rows(tm, D, col), _prev8(tm, D, col)]
    slabs += [_rows(tm, LORA_PAD, C_LORA // LORA_PAD), _prev8(tm, LORA_PAD, C_LORA // LORA_PAD)]
    params = [vec, vec, vec, _full((1, LORA_PAD)), vec, vec, vec, vec,
              _full((LORA_W, D)), _full((LORA_A, D)), _full((G_PAD, D))]
    return slabs, params


def _prep_inputs(refs, first):
    vals = []
    for s in range(4):
        z = refs[2 * s][...]
        vals += [z, _shift_down(z, refs[2 * s + 1][...], 1, first)]
    return vals + [r[...] for r in refs[8:19]]


def _rwkv_prep(P, params, tm=256):
    S = P.shape[0]
    slabs, pspecs = _rwkv_prep_specs(tm)

    def body(*refs):
        outs = _rwkv_prep_fn(*_prep_inputs(refs, pl.program_id(0) == 0))
        for o_ref, val in zip(refs[19:], outs):
            o_ref[...] = val

    shp = jax.ShapeDtypeStruct((S, D), F32)
    return pl.pallas_call(body, name="rwkv_prep", grid=(S // tm,), in_specs=slabs + pspecs,
                          out_specs=[_rows(tm, D)] * 7, out_shape=[shp] * 7,
                          compiler_params=_cparams(("parallel",)))(*([P] * 8), *params)


def _rwkv_prep_bwd(P, params, cts_a, cts_b, tm=128):
    S = P.shape[0]
    slabs, pspecs = _rwkv_prep_specs(tm)
    has_b = [c is not None for c in cts_b]
    n_ct = 7 + sum(has_b)

    def body(*refs):
        first = pl.program_id(0) == 0
        ins = _prep_inputs(refs, first)
        ct_refs = refs[19:19 + n_ct]
        out_refs = refs[19 + n_ct:]
        cts, pos = [], 7
        for i in range(7):
            c = ct_refs[i][...]
            if has_b[i]:
                c = c + ct_refs[pos][...]
                pos += 1
            cts.append(c)
        _, vjp = jax.vjp(_rwkv_prep_fn, *ins)
        grads = vjp(tuple(cts))
        for s in range(4):
            out_refs[s][...] = grads[2 * s]
            out_refs[4 + s][...] = grads[2 * s + 1]
        for i in range(11):
            _acc(out_refs[8 + i], grads[8 + i], first)

    ct_in = list(cts_a) + [c for c in cts_b if c is not None]
    row, lrow = _rows(tm, D), _rows(tm, LORA_PAD)
    f = jax.ShapeDtypeStruct
    zshapes = [f((S, D), F32)] * 3 + [f((S, LORA_PAD), F32)]
    pshapes = [f((1, D), F32)] * 3 + [f((1, LORA_PAD), F32)] + [f((1, D), F32)] * 4 + [f((LORA_W, D), F32), f((LORA_A, D), F32), f((G_PAD, D), F32)]
    return pl.pallas_call(
        body, name="rwkv_prep_bwd", grid=(S // tm,),
        in_specs=slabs + pspecs + [row] * n_ct,
        out_specs=[row, row, row, lrow] * 2 + pspecs,
        out_shape=zshapes * 2 + pshapes,
        compiler_params=_cparams(("arbitrary",)))(*([P] * 8), *params, *ct_in)


def _shift_add(a, b, tm=256):
    S, W = a.shape

    def body(a_ref, b_ref, h_ref, o_ref):
        last = pl.program_id(0) == pl.num_programs(0) - 1
        o_ref[...] = (a_ref[...] + _shift_up(b_ref[...], h_ref[...], 1, last)).astype(BF16)

    return pl.pallas_call(body, name="shift_add", grid=(S // tm,),
                          in_specs=[_rows(tm, W), _rows(tm, W), _next8(tm, W, S)],
                          out_specs=_rows(tm, W), out_shape=jax.ShapeDtypeStruct((S, W), BF16),
                          compiler_params=_cparams(("parallel",)))(a, b, b)


def _rwkv_post_fn(y, r, kmod, v, g, lnx_w, lnx_b, r_k):
    mean = _head_sum(y) * (1.0 / HEAD)
    yc = y - mean
    var = _head_sum(yc * yc) * (1.0 / HEAD)
    yn = yc * lax.rsqrt(var + GN_EPS) * lnx_w + lnx_b
    bonus = _head_sum(r * kmod * r_k) * v
    return (yn + bonus) * g


def _rwkv_post(y, r, kmod, v, g, lnx_w, lnx_b, r_k, tm=256):
    S = y.shape[0]

    def body(y_ref, r_ref, k_ref, v_ref, g_ref, w_ref, b_ref, rk_ref, o_ref):
        o_ref[...] = _rwkv_post_fn(y_ref[...], r_ref[...], k_ref[...], v_ref[...], g_ref[...],
                                   w_ref[...], b_ref[...], rk_ref[...]).astype(BF16)

    row, vec = _rows(tm, D), _full((1, D))
    return pl.pallas_call(body, name="rwkv_post", grid=(S // tm,), in_specs=[row] * 5 + [vec] * 3, out_specs=row,
                          out_shape=jax.ShapeDtypeStruct((S, D), BF16),
                          compiler_params=_cparams(("parallel",)))(y, r, kmod, v, g, lnx_w, lnx_b, r_k)


def _rwkv_post_bwd(drw, y, r, kmod, v, g, lnx_w, lnx_b, r_k, tm=256):
    S = y.shape[0]

    def body(d_ref, y_ref, r_ref, k_ref, v_ref, g_ref, w_ref, b_ref, rk_ref, *out_refs):
        first = pl.program_id(0) == 0
        _, vjp = jax.vjp(_rwkv_post_fn, y_ref[...], r_ref[...], k_ref[...], v_ref[...], g_ref[...],
                         w_ref[...], b_ref[...], rk_ref[...])
        grads = vjp(d_ref[...])
        for i in range(5):
            out_refs[i][...] = grads[i]
        for i in range(5, 8):
            _acc(out_refs[i], grads[i], first)

    row, vec = _rows(tm, D), _full((1, D))
    f = jax.ShapeDtypeStruct
    return pl.pallas_call(body, name="rwkv_post_bwd", grid=(S // tm,), in_specs=[row] * 6 + [vec] * 3,
                          out_specs=[row] * 5 + [vec] * 3, out_shape=[f((S, D), F32)] * 5 + [f((1, D), F32)] * 3,
                          compiler_params=_cparams(("arbitrary",)))(drw, y, r, kmod, v, g, lnx_w, lnx_b, r_k)


CHUNK = 64
CHUNK_TB = 256
_DOT_DIMS = {"nn": (((2,), (1,)), ((0,), (0,))), "nt": (((2,), (2,)), ((0,), (0,))), "tn": (((1,), (1,)), ((0,), (0,)))}


def _dot16(x, y, mode):
    return lax.dot_general(x.astype(BF16), y.astype(BF16), _DOT_DIMS[mode], preferred_element_type=F32)


@functools.partial(jax.custom_vjp, nondiff_argnums=(2,))
def _mm16(x, y, mode):
    return _dot16(x, y, mode)


def _mm16_fwd(x, y, mode):
    return _dot16(x, y, mode), (x, y)


def _mm16_bwd(mode, res, ct):
    x, y = res
    if mode == "nn":
        return _dot16(ct, y, "nt"), _dot16(x, ct, "tn")
    if mode == "nt":
        return _dot16(ct, y, "nn"), _dot16(ct, x, "tn")
    return _dot16(y, ct, "nt"), _dot16(x, ct, "nn")


_mm16.defvjp(_mm16_fwd, _mm16_bwd)


def _tri_sum(x, upper):
    T = x.shape[0]
    i = lax.broadcasted_iota(jnp.int32, (T, T), 0)
    j = lax.broadcasted_iota(jnp.int32, (T, T), 1)
    tri = ((j >= i) if upper else (i >= j)).astype(BF16)
    out, rest = None, x
    for _ in range(3):
        piece = rest.astype(BF16)
        rest = rest - piece.astype(F32)
        part = jnp.dot(tri, piece, preferred_element_type=F32)
        out = part if out is None else out + part
    return out


@jax.custom_vjp
def _cumsum_rows(x):
    return _tri_sum(x, False)


_cumsum_rows.defvjp(lambda x: (_tri_sum(x, False), None), lambda _, ct: (_tri_sum(ct, True),))


def _rows_to_cols(x):
    H, _, K = x.shape
    eye = (lax.broadcasted_iota(jnp.int32, (H, K, K), 1) == lax.broadcasted_iota(jnp.int32, (H, K, K), 2)).astype(F32)
    out = lax.dot_general(eye, jnp.broadcast_to(x, (H, SUBLANES, K)), _DOT_DIMS["nt"],
                          precision=lax.Precision.HIGHEST, preferred_element_type=F32)
    return out[:, :, 0:1]


def _per_head(x):
    return jnp.concatenate([x[:, h * HEAD:(h + 1) * HEAD][None] for h in range(N_HEADS)], axis=0)


def _chunk_fn(st0, r, lw, k, v, a, b):
    T = r.shape[0]
    cl = _cumsum_rows(lw)
    cl_end = cl[T - 1:T, :]
    inv = jnp.exp(-cl)
    to_end = jnp.exp(cl_end - cl)
    ah, rh, bh, kh, be, ke, v3 = [_per_head(x) for x in
                                  (a * jnp.exp(cl - lw), r * jnp.exp(cl), b * inv, k * inv, b * to_end, k * to_end, v)]
    i = lax.broadcasted_iota(jnp.int32, (N_HEADS, T, T), 1)
    j = lax.broadcasted_iota(jnp.int32, (N_HEADS, T, T), 2)
    a_ab = jnp.where(i > j, _mm16(ah, bh, "nt"), 0.0)
    a_ak = jnp.where(i > j, _mm16(ah, kh, "nt"), 0.0)
    m_rb = jnp.where(i >= j, _mm16(rh, bh, "nt"), 0.0)
    m_rk = jnp.where(i >= j, _mm16(rh, kh, "nt"), 0.0)
    rhs = _mm16(ah, st0, "nn") + _mm16(a_ak, v3, "nn")
    power, solve, n = a_ab, (i == j).astype(F32) + a_ab, 1
    while 2 * n < T:
        power = _mm16(power, power, "nn")
        solve = solve + _mm16(solve, power, "nn")
        n *= 2
    sa = _mm16(solve, rhs, "nn")
    y3 = _mm16(rh, st0, "nn") + _mm16(m_rb, sa, "nn") + _mm16(m_rk, v3, "nn")
    st_end = _rows_to_cols(_per_head(jnp.exp(cl_end))) * st0 + _mm16(be, sa, "tn") + _mm16(ke, v3, "tn")
    return jnp.concatenate([y3[h] for h in range(N_HEADS)], axis=1), st_end


def _hosted_exchange(refs, n, broadcast, grid):
    if n == 0:
        return lambda: None
    start, wait = _exchange_ops(refs[:n], refs[n:2 * n], *refs[2 * n:], broadcast)
    first = functools.reduce(jnp.logical_and, [pl.program_id(a) == 0 for a in range(len(grid))])
    last = functools.reduce(jnp.logical_and, [pl.program_id(a) == g - 1 for a, g in enumerate(grid)])
    pl.when(first)(start)
    return lambda: pl.when(last)(wait)


def _cscan_fwd(r, lw, k, v, a, b, gather=()):
    S = r.shape[0]
    per_blk = CHUNK_TB // CHUNK
    n_x = len(gather)
    nblk = S // CHUNK_TB

    def body(*refs):
        r_ref, lw_ref, k_ref, v_ref, a_ref, b_ref = refs[:6]
        y_ref, ck_ref = refs[6 + n_x:8 + n_x]
        st_ref = refs[8 + 2 * n_x]
        finish = _hosted_exchange(refs[6:6 + n_x] + refs[8 + n_x:8 + 2 * n_x] + refs[9 + 2 * n_x:], n_x, True, (nblk,))

        @pl.when(pl.program_id(0) == 0)
        def _():
            st_ref[...] = jnp.zeros_like(st_ref)

        def chunk(c, carry):
            rows = pl.ds(pl.multiple_of(c * CHUNK, CHUNK), CHUNK)
            st0 = st_ref[...]
            ck_ref[c] = st0
            y, st_end = _chunk_fn(st0, r_ref[rows, :], lw_ref[rows, :], k_ref[rows, :],
                                  v_ref[rows, :], a_ref[rows, :], b_ref[rows, :])
            y_ref[rows, :] = y
            st_ref[...] = st_end
            return carry

        lax.fori_loop(0, per_blk, chunk, 0)
        finish()

    blk = _rows(CHUNK_TB, D)
    any_spec = pl.BlockSpec(memory_space=pl.ANY)
    outs = pl.pallas_call(
        body, name="scan_fwd", grid=(nblk,), in_specs=[blk] * 6 + [any_spec] * n_x,
        out_specs=[blk, pl.BlockSpec((per_blk, N_HEADS, HEAD, HEAD), lambda i: (i, 0, 0, 0))] + [any_spec] * n_x,
        out_shape=[jax.ShapeDtypeStruct((S, D), F32), jax.ShapeDtypeStruct((S // CHUNK, N_HEADS, HEAD, HEAD), F32)]
        + _exchange_shapes(gather, True),
        scratch_shapes=[pltpu.VMEM((N_HEADS, HEAD, HEAD), F32)] + (_exchange_scratch(n_x) if n_x else []),
        compiler_params=_cparams(("arbitrary",)))(r, lw, k, v, a, b, *gather)
    return outs[0], outs[1], outs[2:]


def _cscan_bwd(r, lw, k, v, a, b, ckpt, dy, scatter=()):
    S = r.shape[0]
    per_blk = CHUNK_TB // CHUNK
    nblk = S // CHUNK_TB
    n_x = len(scatter)

    def body(*refs):
        r_ref, lw_ref, k_ref, v_ref, a_ref, b_ref, ck_ref, dy_ref = refs[:8]
        out_refs = refs[8 + n_x:14 + n_x]
        ds_ref = refs[14 + 2 * n_x]
        finish = _hosted_exchange(refs[8:8 + n_x] + refs[14 + n_x:14 + 2 * n_x] + refs[15 + 2 * n_x:], n_x, False, (nblk,))

        @pl.when(pl.program_id(0) == 0)
        def _():
            ds_ref[...] = jnp.zeros_like(ds_ref)

        def chunk(cc, carry):
            c = per_blk - 1 - cc
            rows = pl.ds(pl.multiple_of(c * CHUNK, CHUNK), CHUNK)
            ins = (ck_ref[c], r_ref[rows, :], lw_ref[rows, :], k_ref[rows, :], v_ref[rows, :], a_ref[rows, :], b_ref[rows, :])
            _, vjp = jax.vjp(_chunk_fn, *ins)
            grads = vjp((dy_ref[rows, :], ds_ref[...]))
            ds_ref[...] = grads[0]
            for o_ref, g in zip(out_refs, grads[1:]):
                o_ref[rows, :] = g
            return carry

        lax.fori_loop(0, per_blk, chunk, 0)
        finish()

    blk = pl.BlockSpec((CHUNK_TB, D), lambda i: (nblk - 1 - i, 0))
    any_spec = pl.BlockSpec(memory_space=pl.ANY)
    shp = jax.ShapeDtypeStruct((S, D), F32)
    outs = pl.pallas_call(
        body, name="scan_bwd", grid=(nblk,),
        in_specs=[blk] * 6 + [pl.BlockSpec((per_blk, N_HEADS, HEAD, HEAD), lambda i: (nblk - 1 - i, 0, 0, 0)), blk]
        + [any_spec] * n_x,
        out_specs=[blk] * 6 + [any_spec] * n_x, out_shape=[shp] * 6 + _exchange_shapes(scatter, False),
        scratch_shapes=[pltpu.VMEM((N_HEADS, HEAD, HEAD), F32)] + (_exchange_scratch(n_x) if n_x else []),
        compiler_params=_cparams(("arbitrary",)))(r, lw, k, v, a, b, ckpt, dy, *scatter)
    return outs[:6], outs[6:]


def _ada_partial(c_all, w_shard):
    def body(c_ref, w_ref, o_ref):
        o_ref[...] = jnp.dot(c_ref[...].astype(BF16), w_ref[...].astype(BF16), preferred_element_type=F32)

    vm = pl.BlockSpec(memory_space=pltpu.VMEM)
    return pl.pallas_call(body, name="ada_partial", in_specs=[vm, vm], out_specs=vm,
                          out_shape=jax.ShapeDtypeStruct((N_DEV, w_shard.shape[1]), F32),
                          compiler_params=pltpu.CompilerParams(vmem_limit_bytes=VMEM_LIMIT))(c_all, w_shard)


def _ada_bias(rows, b_ada):
    def body(r_ref, b_ref, o_ref):
        o_ref[...] = r_ref[...] + b_ref[...]

    vm = pl.BlockSpec(memory_space=pltpu.VMEM)
    return pl.pallas_call(body, name="ada_bias", in_specs=[vm, vm], out_specs=vm,
                          out_shape=jax.ShapeDtypeStruct(rows.shape, F32))(rows, b_ada)


def _ada_wgrad(c_cols, d_all):
    def body(c_ref, d_ref, o_ref):
        acc = c_ref[:, 0:1] * d_ref[0:1, :]
        for j in range(1, N_DEV):
            acc = acc + c_ref[:, j:j + 1] * d_ref[j:j + 1, :]
        o_ref[...] = acc

    vm = pl.BlockSpec(memory_space=pltpu.VMEM)
    return pl.pallas_call(body, name="ada_wgrad", in_specs=[vm, vm], out_specs=vm,
                          out_shape=jax.ShapeDtypeStruct((D, d_all.shape[1]), F32),
                          compiler_params=pltpu.CompilerParams(vmem_limit_bytes=VMEM_LIMIT))(c_cols, d_all)


def _exchange(srcs, broadcast, name):
    n = len(srcs)

    def body(*refs):
        start, wait = _exchange_ops(refs[:n], refs[n:2 * n], *refs[2 * n:], broadcast)
        start()
        wait()

    any_spec = pl.BlockSpec(memory_space=pl.ANY)
    return pl.pallas_call(
        body, name=name, out_shape=_exchange_shapes(srcs, broadcast), in_specs=[any_spec] * n, out_specs=[any_spec] * n,
        scratch_shapes=_exchange_scratch(n),
        compiler_params=pltpu.CompilerParams(has_side_effects=True),
    )(*srcs)


def _gather_via_sibling(srcs, name):
    n = len(srcs)

    def body(*refs):
        src_refs, out_refs = refs[:n], refs[n:2 * n]
        send_sems, recv_sems, local_sems = refs[2 * n:]
        x, y, c = lax.axis_index("x"), lax.axis_index("y"), lax.axis_index("c")
        me, sibling = (x, y, c), (x, y, 1 - c)
        chips = [(1 - x, y), (x, 1 - y), (1 - x, 1 - y)]

        def slot(px, py, pc):
            return 4 * px + 2 * py + pc

        def copy(i, k, block, to, src=None):
            rows = out_refs[i].at[slot(*block)]
            return pltpu.make_async_remote_copy(
                src_ref=rows if src is None else src, dst_ref=rows, send_sem=send_sems.at[i, k],
                recv_sem=recv_sems.at[i, k], device_id=to, device_id_type=_MESH)

        local = [pltpu.make_async_copy(src_refs[i], out_refs[i].at[slot(*me)], local_sems.at[i]) for i in range(n)]
        for cp in local:
            cp.start()
        first = [copy(i, 0, me, sibling, src=src_refs[i]) for i in range(n)]
        first += [copy(i, 1 + j, me, (*chip, c), src=src_refs[i]) for j, chip in enumerate(chips) for i in range(n)]
        for cp in first:
            cp.start()
        passed = []
        for j, chip in enumerate(chips):
            for i in range(n):
                copy(i, 1 + j, (*chip, c), me).wait_recv()
                passed.append(copy(i, 4 + j, (*chip, c), sibling))
                passed[-1].start()
        for i in range(n):
            copy(i, 0, sibling, me).wait_recv()
            for j, chip in enumerate(chips):
                copy(i, 4 + j, (*chip, 1 - c), me).wait_recv()
        for cp in first + passed:
            cp.wait_send()
        for cp in local:
            cp.wait()

    any_spec = pl.BlockSpec(memory_space=pl.ANY)
    return pl.pallas_call(
        body, name=name, out_shape=_exchange_shapes(srcs, True), in_specs=[any_spec] * n, out_specs=[any_spec] * n,
        scratch_shapes=_exchange_scratch(n),
        compiler_params=pltpu.CompilerParams(has_side_effects=True),
    )(*srcs)


def _flags(broadcast, n):
    return [broadcast] * n if isinstance(broadcast, bool) else list(broadcast)


def _exchange_shapes(srcs, broadcast):
    return [jax.ShapeDtypeStruct((N_DEV,) + (s.shape if bc else s.shape[1:]), s.dtype)
            for s, bc in zip(srcs, _flags(broadcast, len(srcs)))]


def _exchange_scratch(n):
    return [pltpu.SemaphoreType.DMA((n, N_DEV)), pltpu.SemaphoreType.DMA((n, N_DEV)), pltpu.SemaphoreType.DMA((n,))]


def _exchange_ops(src_refs, out_refs, send_sems, recv_sems, local_sems, broadcast):
    n = len(src_refs)
    flags = _flags(broadcast, n)
    x, y, c = lax.axis_index("x"), lax.axis_index("y"), lax.axis_index("c")
    me = 4 * x + 2 * y + c

    def block(i, j):
        return src_refs[i] if flags[i] else src_refs[i].at[j]

    def remote(i, d, src_slot, dst_slot):
        px, py, pc = x ^ (d >> 2), y ^ ((d >> 1) & 1), c ^ (d & 1)
        return pltpu.make_async_remote_copy(
            src_ref=block(i, src_slot), dst_ref=out_refs[i].at[dst_slot], send_sem=send_sems.at[i, d],
            recv_sem=recv_sems.at[i, d], device_id=(px, py, pc), device_id_type=_MESH)

    def local(i):
        return pltpu.make_async_copy(block(i, me), out_refs[i].at[me], local_sems.at[i])

    def start():
        for i in range(n):
            local(i).start()
        for d in range(1, N_DEV):
            for i in range(n):
                remote(i, d, me ^ d, me).start()

    def wait():
        for d in range(1, N_DEV):
            for i in range(n):
                remote(i, d, me, me ^ d).wait_recv()
        for d in range(1, N_DEV):
            for i in range(n):
                remote(i, d, me ^ d, me).wait_send()
        for i in range(n):
            local(i).wait()

    return start, wait


def _sum_adam(parts, w, m, v, name):
    n_parts, R, C = parts.shape
    fits = [t for t in range(SUBLANES, R + 1, SUBLANES) if R % t == 0 and t * C <= 2504 * LANES]
    tm = max(fits) if fits else R
    c1 = 1.0 / (1.0 - ADAM_B1 ** ADAM_STEP)
    c2 = 1.0 / (1.0 - ADAM_B2 ** ADAM_STEP)

    def body(p_ref, w_ref, m_ref, v_ref, g_ref, d_ref, nm_ref, nv_ref):
        g = p_ref[0].astype(F32)
        for j in range(1, n_parts):
            g = g + p_ref[j].astype(F32)
        nm = ADAM_B1 * m_ref[...] + (1.0 - ADAM_B1) * g
        nv = ADAM_B2 * v_ref[...] + (1.0 - ADAM_B2) * (g * g)
        g_ref[...] = g
        nm_ref[...] = nm
        nv_ref[...] = nv
        d_ref[...] = -ADAM_LR * ((nm * c1) / (jnp.sqrt(nv * c2) + ADAM_EPS) + ADAM_WD * w_ref[...])

    row = _rows(tm, C)
    shp = jax.ShapeDtypeStruct((R, C), F32)
    return pl.pallas_call(body, name=name, grid=(R // tm,),
                          in_specs=[pl.BlockSpec((n_parts, tm, C), lambda i: (0, i, 0)), row, row, row],
                          out_specs=[row] * 4, out_shape=[shp] * 4,
                          compiler_params=_cparams(("parallel",)))(parts, w, m, v)


PACK_ALIGN = 16 * LANES
PACK_ROWS = 512 * LANES

TRANSPOSED = ("w_in", "w_up")
SHARDED = (("w_ada", 1), ("w_in", 0), ("w2", 1), ("a2", 1), ("g2", 1), ("w_att_out", 1), ("w_rwkv_out", 0),
           ("w_o", 0), ("w_up", 0), ("conv_w", 1), ("w_down", 0))
EARLY, LATE = SHARDED[1:5], SHARDED[5:]
REPLICATED = ("b_ada", "norm1_w", "b_gate", "mu_shift", "w0", "a0", "k_k", "k_a", "r_k", "lnx_w", "lnx_b",
              "norm2_w", "conv_b", "norm_f_w")
WEIGHTS = ("w_ada", "b_ada", "norm1_w", "w_in", "b_gate", "mu_shift", "w0", "w2", "a0", "a2", "g2", "k_k", "k_a", "r_k",
           "lnx_w", "lnx_b", "w_att_out", "w_rwkv_out", "w_o", "norm2_w", "w_up", "conv_w", "conv_b", "w_down", "norm_f_w")


def _pack(arrays):
    flat, layout, off = [], [], 0
    for i, a in enumerate(arrays):
        n = a.size
        pad = (-n) % PACK_ALIGN if i + 1 < len(arrays) else (-(off + n)) % PACK_ROWS
        flat.append(a.reshape(-1))
        if pad:
            flat.append(jnp.zeros((pad,), a.dtype))
        layout.append((off, n, a.shape))
        off += n + pad
    return jnp.concatenate(flat).reshape(-1, LANES), layout


def _unpack(buf, layout):
    flat = buf.reshape(-1)
    return [flat[off:off + n].reshape(shape) for off, n, shape in layout]


def _pad_w_in(w_in_t):
    rkv = w_in_t[ATT_IN:ATT_IN + 3 * D]
    lora = w_in_t[ATT_IN + 3 * D:ATT_IN + RWKV_IN]
    gates = w_in_t[ATT_IN + RWKV_IN:]
    att = w_in_t[:ATT_IN]
    lw, la, lg = lora[:LORA_W], lora[LORA_W:LORA_W + LORA_A], lora[LORA_W + LORA_A:]
    zeros = jnp.zeros((LORA_PAD - LANES - LORA_G, w_in_t.shape[1]), w_in_t.dtype)
    return jnp.concatenate([rkv, gates, att, lw, la, lg, zeros], axis=0)


def _unpad_w_in(g):
    att = g[C_ATT:C_ATT + ATT_IN]
    rkv = g[C_R:C_R + 3 * D]
    lora = jnp.concatenate([g[C_LORA:C_LORA + LORA_W + LORA_A], g[C_LORA + LANES:C_LORA + LANES + LORA_G]], axis=0)
    gates = g[C_GA:C_GA + 2 * D]
    return jnp.concatenate([att, rkv, lora, gates], axis=0)


def _pad_mu(mu):
    lo = mu[:, 3 * D:]
    mu_l = jnp.concatenate([lo[:, :LORA_W + LORA_A], lo[:, LORA_W + LORA_A:], jnp.zeros((1, LORA_PAD - LANES - LORA_G), mu.dtype)], axis=1)
    return mu[:, :D], mu[:, D:2 * D], mu[:, 2 * D:3 * D], mu_l


def _local_step(x, ada, W, late_shards, target):
    S = x.shape[0]
    W = dict(W)
    G = {}
    sh1, sc1, gt1, sh2, sc2, gt2 = [ada[:, i * D:(i + 1) * D] for i in range(6)]
    h1, rstd1 = _norm_fwd(x, None, None, W["norm1_w"], sc1, sh1, "norm1_fwd")
    w_in_p = _pad_w_in(W["w_in"])
    P = _mm(h1, w_in_p, "nt", F32, "proj_in")

    mu_r, mu_k, mu_v, mu_l = _pad_mu(W["mu_shift"])
    g2p = jnp.pad(W["g2"], ((0, G_PAD - LORA_G), (0, 0)))
    prep_params = [mu_r, mu_k, mu_v, mu_l, W["w0"], W["a0"], W["k_k"], W["k_a"], W["w2"], W["a2"], g2p]
    r_, dec, kmod, v_, aa, bb, gg = _rwkv_prep(P, prep_params)
    y_scan, states, late = _cscan_fwd(r_, dec, kmod, v_, aa, bb, gather=late_shards)
    W.update({n: _full_weight(g, axis) for (n, axis), g in zip(LATE, late)})

    o_g, l_g = zip(*[_att_fwd(P, g) for g in range(len(ATT_PATTERNS))])
    att = _att_combine_fwd(o_g, l_g)
    y_att = _mm(att, W["w_att_out"], "nn", F32, "att_out")
    r_k = W["r_k"].reshape(1, D)
    rw = _rwkv_post(y_scan, r_, kmod, v_, gg, W["lnx_w"], W["lnx_b"], r_k)
    y_rwkv = _mm(rw, W["w_rwkv_out"], "nn", F32, "rwkv_out")

    bga, bgr = W["b_gate"][:, :D], W["b_gate"][:, D:]
    mix = _gate_fwd(P, bga, bgr, y_att, y_rwkv)
    mo = _mm(mix, W["w_o"], "nn", F32, "mix_out")
    x2, h2, rstd2 = _norm_fwd(x, mo, gt1, W["norm2_w"], sc2, sh2, "norm2_fwd")
    u = _mm(h2, W["w_up"], "nt", F32, "ffn_up")
    conv_w8 = jnp.pad(W["conv_w"], ((0, SUBLANES - 3), (0, 0)))
    act = _conv_fwd(u, conv_w8, W["conv_b"])
    f = _mm(act, W["w_down"], "nn", F32, "ffn_down")
    loss_blk, dx3, df, dgt2, G["norm_f_w"] = _final(x2, f, gt2, W["norm_f_w"], target)
    loss = loss_blk[0, 0]

    dact = _mm(df, W["w_down"], "nt", BF16, "ffn_down_dx")
    G["w_down"] = _mm(act, df, "tn", F32, "ffn_down_dw")
    duc, dwg, dwv, dbg, dbv = _conv_bwd_a(dact, u, conv_w8, W["conv_b"])
    G["conv_w"] = jnp.concatenate([dwg[0:3], dwv[0:3]], axis=1)
    G["conv_b"] = jnp.concatenate([dbg, dbv], axis=1)
    du = _conv_bwd_b(duc, conv_w8)
    dh2 = _mm(du, W["w_up"], "nn", F32, "ffn_up_dx")
    G["w_up"] = _mm(du, h2, "tn", F32, "ffn_up_dw")
    dx2, dsh2, dsc2, G["norm2_w"], dmo, dgt1 = _norm_bwd(dh2, x2, rstd2, W["norm2_w"], sc2, dx3, mo, gt1, "norm2_bwd")
    dmix = _mm(dmo, W["w_o"], "nt", F32, "mix_out_dx")
    G["w_o"] = _mm(mix, dmo, "tn", F32, "mix_out_dw")
    dy_att, dy_rwkv, dpga, dpgr, dbga, dbgr = _gate_bwd(dmix, P, bga, bgr, y_att, y_rwkv)
    G["b_gate"] = jnp.concatenate([dbga, dbgr], axis=1)

    datt = _mm(dy_att, W["w_att_out"], "nt", F32, "att_out_dx")
    G["w_att_out"] = _mm(att, dy_att, "tn", F32, "att_out_dw")
    dcomb = _att_combine_bwd(datt, o_g, l_g)
    dp_att = []
    for g in range(len(ATT_PATTERNS)):
        dp_att += _att_bwd(P, o_g[g], l_g[g], dcomb[g], dcomb[3 + g], g)

    drw = _mm(dy_rwkv, W["w_rwkv_out"], "nt", F32, "rwkv_out_dx")
    G["w_rwkv_out"] = _mm(rw, dy_rwkv, "tn", F32, "rwkv_out_dw")
    dy_scan, dr1, dk1, dv1, dgg, G["lnx_w"], G["lnx_b"], drk = _rwkv_post_bwd(drw, y_scan, r_, kmod, v_, gg, W["lnx_w"], W["lnx_b"], r_k)
    G["r_k"] = drk.reshape(W["r_k"].shape)
    late_blocks = [_owner_blocks(G[n], axis) for n, axis in LATE] if late_shards else []
    (dr2, ddec, dk2, dv2, daa, dbb), late_parts = _cscan_bwd(r_, dec, kmod, v_, aa, bb, states, dy_scan, scatter=late_blocks)
    pb = _rwkv_prep_bwd(P, prep_params, [dr2, ddec, dk2, dv2, daa, dbb, dgg], [dr1, None, dk1, dv1, None, None, None])
    dz, dzp, dpar = pb[0:4], pb[4:8], pb[8:]
    dp_rkv = [_shift_add(dz[i], dzp[i]) for i in range(3)]
    dp_lora = _shift_add(dz[3], dzp[3])
    dmu_r, dmu_k, dmu_v, dmu_l, G["w0"], G["a0"], G["k_k"], G["k_a"], G["w2"], G["a2"], dg2p = dpar
    G["g2"] = dg2p[0:LORA_G]
    G["mu_shift"] = jnp.concatenate([dmu_r, dmu_k, dmu_v, dmu_l[:, :LORA_W + LORA_A], dmu_l[:, LANES:LANES + LORA_G]], axis=1)

    dP = jnp.concatenate(dp_rkv + [dpga, dpgr] + dp_att + [dp_lora], axis=1)
    G["w_in"] = _unpad_w_in(_mm(dP, h1, "tn", F32, "proj_in_dw"))
    if late_shards:
        dh1, (w_in_parts,) = _mm(dP, w_in_p, "nn", F32, "proj_in_dx", scatter=[_owner_blocks(G["w_in"], 0)])
        done = dict(zip([n for n, _ in LATE] + ["w_in"], list(late_parts) + [w_in_parts]))
    else:
        dh1, done = _mm(dP, w_in_p, "nn", F32, "proj_in_dx"), {}
    grad_x, dsh1, dsc1, G["norm1_w"] = _norm_bwd(dh1, x, rstd1, W["norm1_w"], sc1, dx2, None, None, "norm1_bwd")
    dada = jnp.concatenate([dsh1, dsc1, dgt1, dsh2, dsc2, dgt2], axis=1)
    G["b_ada"] = dada
    return loss, grad_x, G, done


def _full_weight(gathered, axis):
    _, rows, cols = gathered.shape
    if axis == 0:
        return gathered.reshape(N_DEV * rows, cols)
    return gathered.transpose(1, 0, 2).reshape(rows, N_DEV * cols)


def _owner_blocks(g, axis):
    rows, cols = g.shape
    g = g.astype(BF16)
    if axis == 0:
        return g.reshape(N_DEV, rows // N_DEV, cols)
    return g.reshape(rows, N_DEV, cols // N_DEV).transpose(1, 0, 2)


def kernel(x, c, w_ada, b_ada, norm1_w, w_in, b_gate, mu_shift, w0, w2, a0, a2, g2, k_k, k_a, r_k, lnx_w, lnx_b, w_att_out, w_rwkv_out, w_o, norm2_w, w_up, conv_w, conv_b, w_down, norm_f_w, loss_target, m_w_ada, m_b_ada, m_norm1_w, m_w_in, m_b_gate, m_mu_shift, m_w0, m_w2, m_a0, m_a2, m_g2, m_k_k, m_k_a, m_r_k, m_lnx_w, m_lnx_b, m_w_att_out, m_w_rwkv_out, m_w_o, m_norm2_w, m_w_up, m_conv_w, m_conv_b, m_w_down, m_norm_f_w, v_w_ada, v_b_ada, v_norm1_w, v_w_in, v_b_gate, v_mu_shift, v_w0, v_w2, v_a0, v_a2, v_g2, v_k_k, v_k_a, v_r_k, v_lnx_w, v_lnx_b, v_w_att_out, v_w_rwkv_out, v_w_o, v_norm2_w, v_w_up, v_conv_w, v_conv_b, v_w_down, v_norm_f_w):
    env = dict(locals())
    w_shard = {n: env[n] for n in WEIGHTS}
    m_shard = {n: env["m_" + n] for n in WEIGHTS}
    v_shard = {n: env["v_" + n] for n in WEIGHTS}

    def mat(shards, n):
        return jnp.swapaxes(shards[n][0], 0, 1) if n in TRANSPOSED else shards[n][0]

    c_all, *gathered = _gather_via_sibling([c] + [mat(w_shard, n).astype(BF16) for n, _ in EARLY], "gather_weights")
    c_all = c_all.reshape(N_DEV, D)
    W = {n: _full_weight(g, axis) for (n, axis), g in zip(EARLY, gathered)}
    for n in REPLICATED:
        W[n] = w_shard[n].reshape(1, -1) if n != "r_k" else w_shard[n][0]
    ada_cols = _ada_partial(c_all, w_shard["w_ada"][0])
    ada_rows, = _exchange([ada_cols[:, None, :]], False, "ada_rows")
    ada = _ada_bias(ada_rows.reshape(1, -1), w_shard["b_ada"])

    late_shards = [mat(w_shard, n).astype(BF16) for n, _ in LATE]
    loss, grad_x, G, parts = _local_step(x[0], ada, W, late_shards, loss_target[0])
    loss = lax.psum(loss, ("x", "y", "c"))

    small, slayout = _pack([G[n].reshape(-1) for n in REPLICATED])
    sparts, dada_all = _exchange([small, G["b_ada"].reshape(N_DEV, 1, -1)], [True, False], "gather_small_grads")
    parts["w_ada"] = _ada_wgrad(c_all.T, dada_all.reshape(N_DEV, -1))[None]

    rest = [(n, axis) for n, axis in SHARDED if n not in parts]
    parts.update(zip([n for n, _ in rest], _exchange([_owner_blocks(G[n], axis) for n, axis in rest], False, "scatter_grads")))
    out = {}
    for n, p in parts.items():
        if n in TRANSPOSED:
            flat = lambda a: a.reshape(-1, LANES)
            res = _sum_adam(p.reshape(p.shape[0], -1, LANES), flat(mat(w_shard, n)), flat(mat(m_shard, n)), flat(mat(v_shard, n)), "adam_" + n)
            res = [jnp.swapaxes(a.reshape(p.shape[1:]), 0, 1) for a in res]
        else:
            res = _sum_adam(p, mat(w_shard, n), mat(m_shard, n), mat(v_shard, n), "adam_" + n)
        for kind, a in zip(("grad", "delta", "new_m", "new_v"), res):
            out[kind, n] = a[None]

    sw, _ = _pack([w_shard[n].reshape(-1) for n in REPLICATED])
    sm, _ = _pack([m_shard[n].reshape(-1) for n in REPLICATED])
    sv, _ = _pack([v_shard[n].reshape(-1) for n in REPLICATED])
    res = _sum_adam(sparts, sw, sm, sv, "adam_replicated")
    for kind, buf in zip(("grad", "delta", "new_m", "new_v"), res):
        for n, a in zip(REPLICATED, _unpack(buf, slayout)):
            out[kind, n] = a.reshape(w_shard[n].shape)

    return (loss, grad_x[None], *[out[kind, n] for kind in ("grad", "delta", "new_m", "new_v") for n in WEIGHTS])
```

```python
import functools
import math

import jax
import jax.numpy as jnp
from jax import lax
from jax.experimental import pallas as pl
from jax.experimental.pallas import tpu as pltpu

F32 = jnp.float32
BF16 = jnp.bfloat16

D = 1024
HEAD = 64
ATT_PATTERNS = ((128, 1), (512, 4), (2048, 16))
ATT_HEADS = 8
ATT_W = ATT_HEADS * HEAD
ATT_IN = 3 * 3 * ATT_W
QBLK = 128
N_HEADS = D // HEAD
LORA_W, LORA_A, LORA_G = 64, 64, 160
RWKV_IN = 3 * D + LORA_W + LORA_A + LORA_G
N_IN = ATT_IN + RWKV_IN + 2 * D
D_FF = 2816
RMS_EPS = 1e-6
GN_EPS = 64e-5
N_DEV = 8
LANES = 128
SUBLANES = 8

C_R, C_K, C_V, C_GA, C_GR = 0, 1024, 2048, 3072, 4096
C_ATT = 5120
C_LORA = C_ATT + ATT_IN
LORA_PAD = 512
G_PAD = 256
N_PAD = C_LORA + LORA_PAD

ADAM_LR, ADAM_B1, ADAM_B2, ADAM_EPS, ADAM_WD, ADAM_STEP = 0.001, 0.9, 0.999, 1e-08, 0.01, 10

VMEM_LIMIT = 56 * 1024 * 1024

_MESH = pl.DeviceIdType.MESH


def _cparams(sem):
    return pltpu.CompilerParams(dimension_semantics=sem, vmem_limit_bytes=VMEM_LIMIT)


def _tile(dim, pref):
    if dim <= pref:
        return dim
    best = None
    for t in range(LANES, pref + 1, LANES):
        if dim % t == 0:
            best = t
    assert best is not None, dim
    return best


MM_TILES = {"nn": (1024, 1408, 1408), "nt": (512, 2048, 1408), "tn": (1408, 1408, 1024)}


def _mm(a, b, mode, out_dtype, name, scatter=()):
    if mode == "nn":
        (M, K), (K2, N) = a.shape, b.shape
    elif mode == "nt":
        (M, K), (N, K2) = a.shape, b.shape
    else:
        (K, M), (K2, N) = a.shape, b.shape
    assert K == K2, (a.shape, b.shape, mode)
    tm, tn, tk = (_tile(dim, pref) for dim, pref in zip((M, N, K), MM_TILES[mode]))
    nk = K // tk
    grid = (M // tm, N // tn, nk)
    n_x = len(scatter)
    dims = {"nn": (((1,), (0,)), ((), ())), "nt": (((1,), (1,)), ((), ())), "tn": (((0,), (0,)), ((), ()))}[mode]

    def body(*refs):
        a_ref, b_ref = refs[:2]
        o_ref, acc_ref = refs[2 + n_x], refs[3 + 2 * n_x]
        finish = _hosted_exchange(refs[2:2 + n_x] + refs[3 + n_x:3 + 2 * n_x] + refs[4 + 2 * n_x:], n_x, False, grid)
        k = pl.program_id(2)
        part = lax.dot_general(a_ref[...].astype(BF16), b_ref[...].astype(BF16), dims,
                               preferred_element_type=F32)
        if nk == 1:
            o_ref[...] = part.astype(o_ref.dtype)
        else:
            @pl.when(k == 0)
            def _():
                acc_ref[...] = part

            @pl.when(jnp.logical_and(k > 0, k < nk - 1))
            def _():
                acc_ref[...] += part

            @pl.when(k == nk - 1)
            def _():
                o_ref[...] = (acc_ref[...] + part).astype(o_ref.dtype)
        finish()

    a_spec = pl.BlockSpec((tk, tm), lambda i, j, k: (k, i)) if mode == "tn" else pl.BlockSpec((tm, tk), lambda i, j, k: (i, k))
    b_spec = pl.BlockSpec((tn, tk), lambda i, j, k: (j, k)) if mode == "nt" else pl.BlockSpec((tk, tn), lambda i, j, k: (k, j))
    any_spec = pl.BlockSpec(memory_space=pl.ANY)
    outs = pl.pallas_call(
        body, name=name, grid=grid,
        in_specs=[a_spec, b_spec] + [any_spec] * n_x,
        out_specs=[pl.BlockSpec((tm, tn), lambda i, j, k: (i, j))] + [any_spec] * n_x,
        out_shape=[jax.ShapeDtypeStruct((M, N), out_dtype)] + _exchange_shapes(scatter, False),
        scratch_shapes=[pltpu.VMEM((tm, tn) if nk > 1 else (SUBLANES, LANES), F32)] + (_exchange_scratch(n_x) if n_x else []),
        compiler_params=_cparams(("arbitrary",) * 3 if n_x else ("parallel", "parallel", "arbitrary")),
    )(a, b, *scatter)
    return (outs[0], outs[1:]) if n_x else outs[0]


def _rows(tm, w, col=0):
    return pl.BlockSpec((tm, w), lambda i: (i, col))


def _full(shape):
    return pl.BlockSpec(shape, lambda i: (0,) * len(shape))


def _prev8(tm, w, col=0):
    return pl.BlockSpec((SUBLANES, w), lambda i: (jnp.maximum(i * (tm // SUBLANES) - 1, 0), col))


def _next8(tm, w, n_rows, col=0):
    last = n_rows // SUBLANES - 1
    return pl.BlockSpec((SUBLANES, w), lambda i: (jnp.minimum((i + 1) * (tm // SUBLANES), last), col))


def _shift_down(x, halo, k, first):
    rolled = pltpu.roll(x, k, 0)
    row = lax.broadcasted_iota(jnp.int32, x.shape, 0)
    out = rolled
    for j in range(k):
        h = jnp.where(first, 0.0, halo[SUBLANES - k + j:SUBLANES - k + j + 1, :])
        out = jnp.where(row == j, h, out)
    return out


def _shift_up(x, halo, k, last):
    n = x.shape[0]
    rolled = pltpu.roll(x, n - k, 0)
    row = lax.broadcasted_iota(jnp.int32, x.shape, 0)
    out = rolled
    for j in range(k):
        h = jnp.where(last, 0.0, halo[j:j + 1, :])
        out = jnp.where(row == n - k + j, h, out)
    return out


def _acc(ref, val, first):
    @pl.when(first)
    def _():
        ref[...] = val

    @pl.when(jnp.logical_not(first))
    def _():
        ref[...] += val


def _colsum(x):
    return jnp.sum(x, axis=0, keepdims=True)


def _norm_fwd(x, mo, gt, nw, sc, sh, name, tm=256):
    S = x.shape[0]
    has_res = mo is not None

    def body(*refs):
        if has_res:
            x_ref, mo_ref, gt_ref, nw_ref, sc_ref, sh_ref, x2_ref, h_ref, rs_ref = refs
            x2 = x_ref[...] + gt_ref[...] * mo_ref[...]
            x2_ref[...] = x2
        else:
            x_ref, nw_ref, sc_ref, sh_ref, h_ref, rs_ref = refs
            x2 = x_ref[...]
        rstd = lax.rsqrt(jnp.mean(x2 * x2, axis=-1, keepdims=True) + RMS_EPS)
        rs_ref[...] = rstd
        h_ref[...] = ((x2 * rstd * nw_ref[...]) * (1.0 + sc_ref[...]) + sh_ref[...]).astype(BF16)

    vec = _full((1, D))
    ins = [x, mo, gt, nw, sc, sh] if has_res else [x, nw, sc, sh]
    in_specs = [_rows(tm, D), _rows(tm, D), vec, vec, vec, vec] if has_res else [_rows(tm, D), vec, vec, vec]
    outs = [jax.ShapeDtypeStruct((S, D), BF16), jax.ShapeDtypeStruct((S, 1), F32)]
    out_specs = [_rows(tm, D), _rows(tm, 1)]
    if has_res:
        outs = [jax.ShapeDtypeStruct((S, D), F32)] + outs
        out_specs = [_rows(tm, D)] + out_specs
    return pl.pallas_call(body, name=name, grid=(S // tm,), in_specs=in_specs, out_specs=out_specs,
                          out_shape=outs, compiler_params=_cparams(("parallel",)))(*ins)


def _norm_bwd(dh, xin, rstd, nw, sc, dres, mo, gt, name, tm=256):
    S = xin.shape[0]
    has_res = mo is not None

    def body(*refs):
        if has_res:
            dh_ref, x_ref, rs_ref, nw_ref, sc_ref, dres_ref, mo_ref, gt_ref, dx_ref, dsh_ref, dsc_ref, dnw_ref, dmo_ref, dgt_ref = refs
        else:
            dh_ref, x_ref, rs_ref, nw_ref, sc_ref, dres_ref, dx_ref, dsh_ref, dsc_ref, dnw_ref = refs
        first = pl.program_id(0) == 0
        dh = dh_ref[...]
        rstd = rs_ref[...]
        n = x_ref[...] * rstd
        w = nw_ref[...]
        _acc(dsh_ref, _colsum(dh), first)
        _acc(dsc_ref, _colsum(dh * (n * w)), first)
        dnw = dh * (1.0 + sc_ref[...])
        _acc(dnw_ref, _colsum(dnw * n), first)
        dn = dnw * w
        dx = dres_ref[...] + rstd * (dn - n * jnp.mean(dn * n, axis=-1, keepdims=True))
        dx_ref[...] = dx
        if has_res:
            dmo_ref[...] = (dx * gt_ref[...]).astype(BF16)
            _acc(dgt_ref, _colsum(dx * mo_ref[...]), first)

    vec = _full((1, D))
    vshape = jax.ShapeDtypeStruct((1, D), F32)
    ins = [dh, xin, rstd, nw, sc, dres] + ([mo, gt] if has_res else [])
    in_specs = [_rows(tm, D), _rows(tm, D), _rows(tm, 1), vec, vec, _rows(tm, D)] + ([_rows(tm, D), vec] if has_res else [])
    outs = [jax.ShapeDtypeStruct((S, D), F32), vshape, vshape, vshape]
    out_specs = [_rows(tm, D), vec, vec, vec]
    if has_res:
        outs += [jax.ShapeDtypeStruct((S, D), BF16), vshape]
        out_specs += [_rows(tm, D), vec]
    return pl.pallas_call(body, name=name, grid=(S // tm,), in_specs=in_specs, out_specs=out_specs,
                          out_shape=outs, compiler_params=_cparams(("arbitrary",)))(*ins)


def _final(x2, f, gt2, nfw, target, tm=256):
    S = x2.shape[0]

    def body(x2_ref, f_ref, gt_ref, w_ref, t_ref, loss_ref, dx_ref, df_ref, dgt_ref, dw_ref):
        first = pl.program_id(0) == 0
        f = f_ref[...]
        gt = gt_ref[...]
        w = w_ref[...]
        x3 = x2_ref[...] + gt * f
        rstd = lax.rsqrt(jnp.mean(x3 * x3, axis=-1, keepdims=True) + RMS_EPS)
        n = x3 * rstd
        e = n * w - t_ref[...]
        part = 0.5 * jnp.sum(jnp.mean(e * e, axis=-1, keepdims=True), axis=0, keepdims=True)
        _acc(loss_ref, jnp.broadcast_to(part, (SUBLANES, LANES)), first)
        dy = e * (1.0 / D)
        _acc(dw_ref, _colsum(dy * n), first)
        dn = dy * w
        dx = rstd * (dn - n * jnp.mean(dn * n, axis=-1, keepdims=True))
        dx_ref[...] = dx
        df_ref[...] = (dx * gt).astype(BF16)
        _acc(dgt_ref, _colsum(dx * f), first)

    vec = _full((1, D))
    vshape = jax.ShapeDtypeStruct((1, D), F32)
    return pl.pallas_call(
        body, name="final_loss", grid=(S // tm,),
        in_specs=[_rows(tm, D), _rows(tm, D), vec, vec, _rows(tm, D)],
        out_specs=[_full((SUBLANES, LANES)), _rows(tm, D), _rows(tm, D), vec, vec],
        out_shape=[jax.ShapeDtypeStruct((SUBLANES, LANES), F32), jax.ShapeDtypeStruct((S, D), F32),
                   jax.ShapeDtypeStruct((S, D), BF16), vshape, vshape],
        compiler_params=_cparams(("arbitrary",)))(x2, f, gt2, nfw, target)


def _gate_fwd(P, bga, bgr, y_att, y_rwkv, tm=256):
    S = P.shape[0]

    def body(pa_ref, pr_ref, ba_ref, br_ref, ya_ref, yr_ref, mix_ref):
        ga = jax.nn.sigmoid(pa_ref[...] + ba_ref[...])
        gr = jax.nn.sigmoid(pr_ref[...] + br_ref[...])
        mix_ref[...] = (ga * ya_ref[...] + gr * yr_ref[...]).astype(BF16)

    vec = _full((1, D))
    return pl.pallas_call(
        body, name="gate_fwd", grid=(S // tm,),
        in_specs=[_rows(tm, D, C_GA // D), _rows(tm, D, C_GR // D), vec, vec, _rows(tm, D), _rows(tm, D)],
        out_specs=_rows(tm, D), out_shape=jax.ShapeDtypeStruct((S, D), BF16),
        compiler_params=_cparams(("parallel",)))(P, P, bga, bgr, y_att, y_rwkv)


def _gate_bwd(dmix, P, bga, bgr, y_att, y_rwkv, tm=256):
    S = P.shape[0]

    def body(dm_ref, pa_ref, pr_ref, ba_ref, br_ref, ya_ref, yr_ref, dya_ref, dyr_ref, dpa_ref, dpr_ref, dba_ref, dbr_ref):
        first = pl.program_id(0) == 0
        dm = dm_ref[...]
        ga = jax.nn.sigmoid(pa_ref[...] + ba_ref[...])
        gr = jax.nn.sigmoid(pr_ref[...] + br_ref[...])
        dya_ref[...] = (dm * ga).astype(BF16)
        dyr_ref[...] = (dm * gr).astype(BF16)
        dpa = dm * ya_ref[...] * ga * (1.0 - ga)
        dpr = dm * yr_ref[...] * gr * (1.0 - gr)
        dpa_ref[...] = dpa.astype(BF16)
        dpr_ref[...] = dpr.astype(BF16)
        _acc(dba_ref, _colsum(dpa), first)
        _acc(dbr_ref, _colsum(dpr), first)

    vec = _full((1, D))
    row = _rows(tm, D)
    rshape = jax.ShapeDtypeStruct((S, D), BF16)
    vshape = jax.ShapeDtypeStruct((1, D), F32)
    return pl.pallas_call(
        body, name="gate_bwd", grid=(S // tm,),
        in_specs=[row, _rows(tm, D, C_GA // D), _rows(tm, D, C_GR // D), vec, vec, row, row],
        out_specs=[row, row, row, row, vec, vec],
        out_shape=[rshape, rshape, rshape, rshape, vshape, vshape],
        compiler_params=_cparams(("arbitrary",)))(dmix, P, P, bga, bgr, y_att, y_rwkv)


CONV_TN = D_FF // 2


def _conv_fwd(u, conv_w8, conv_b, tm=256, tn=CONV_TN):
    S = u.shape[0]
    nj = D_FF // tn

    def conv(u_ref, h_ref, w_ref, b_ref, first):
        u = u_ref[...]
        h = h_ref[...]
        w = w_ref[...]
        return b_ref[...] + w[0:1] * _shift_down(u, h, 2, first) + w[1:2] * _shift_down(u, h, 1, first) + w[2:3] * u

    def body(ug_ref, hg_ref, uv_ref, hv_ref, wg_ref, wv_ref, bg_ref, bv_ref, act_ref):
        first = pl.program_id(0) == 0
        g = conv(ug_ref, hg_ref, wg_ref, bg_ref, first)
        v = conv(uv_ref, hv_ref, wv_ref, bv_ref, first)
        act_ref[...] = (g * jax.nn.sigmoid(g) * v).astype(BF16)

    blk = lambda off: pl.BlockSpec((tm, tn), lambda i, j: (i, j + off))
    halo = lambda off: pl.BlockSpec((SUBLANES, tn), lambda i, j: (jnp.maximum(i * (tm // SUBLANES) - 1, 0), j + off))
    wsp = lambda off: pl.BlockSpec((SUBLANES, tn), lambda i, j: (0, j + off))
    bsp = lambda off: pl.BlockSpec((1, tn), lambda i, j: (0, j + off))
    return pl.pallas_call(
        body, name="conv_fwd", grid=(S // tm, nj),
        in_specs=[blk(0), halo(0), blk(nj), halo(nj), wsp(0), wsp(nj), bsp(0), bsp(nj)],
        out_specs=pl.BlockSpec((tm, tn), lambda i, j: (i, j)),
        out_shape=jax.ShapeDtypeStruct((S, D_FF), BF16),
        compiler_params=_cparams(("parallel", "parallel")))(u, u, u, u, conv_w8, conv_w8, conv_b, conv_b)


def _conv_bwd_a(dact, u, conv_w8, conv_b, tm=256, tn=CONV_TN):
    S = u.shape[0]
    nj = D_FF // tn

    def half(u_ref, h_ref, w_ref, b_ref, first):
        u = u_ref[...]
        h = h_ref[...]
        w = w_ref[...]
        u2, u1 = _shift_down(u, h, 2, first), _shift_down(u, h, 1, first)
        return b_ref[...] + w[0:1] * u2 + w[1:2] * u1 + w[2:3] * u, (u2, u1, u)

    def wgrad(d, taps):
        z = jnp.zeros((SUBLANES - 3, d.shape[1]), F32)
        return jnp.concatenate([_colsum(d * taps[0]), _colsum(d * taps[1]), _colsum(d * taps[2]), z], axis=0)

    def body(da_ref, ug_ref, hg_ref, uv_ref, hv_ref, wg_ref, wv_ref, bg_ref, bv_ref,
             d_ref, dwg_ref, dwv_ref, dbg_ref, dbv_ref):
        first = pl.program_id(1) == 0
        g, tg = half(ug_ref, hg_ref, wg_ref, bg_ref, first)
        v, tv = half(uv_ref, hv_ref, wv_ref, bv_ref, first)
        da = da_ref[...].astype(F32)
        sg = jax.nn.sigmoid(g)
        dg = da * v * (sg * (1.0 + g * (1.0 - sg)))
        dv = da * (g * sg)
        d_ref[0] = dg
        d_ref[1] = dv
        _acc(dwg_ref, wgrad(dg, tg), first)
        _acc(dwv_ref, wgrad(dv, tv), first)
        _acc(dbg_ref, _colsum(dg), first)
        _acc(dbv_ref, _colsum(dv), first)

    blk = lambda off: pl.BlockSpec((tm, tn), lambda j, i: (i, j + off))
    halo = lambda off: pl.BlockSpec((SUBLANES, tn), lambda j, i: (jnp.maximum(i * (tm // SUBLANES) - 1, 0), j + off))
    wsp = lambda off: pl.BlockSpec((SUBLANES, tn), lambda j, i: (0, j + off))
    bsp = lambda off: pl.BlockSpec((1, tn), lambda j, i: (0, j + off))
    f = jax.ShapeDtypeStruct
    outs = pl.pallas_call(
        body, name="conv_bwd_a", grid=(nj, S // tm),
        in_specs=[pl.BlockSpec((tm, tn), lambda j, i: (i, j)), blk(0), halo(0), blk(nj), halo(nj), wsp(0), wsp(nj), bsp(0), bsp(nj)],
        out_specs=[pl.BlockSpec((2, tm, tn), lambda j, i: (0, i, j)),
                   pl.BlockSpec((SUBLANES, tn), lambda j, i: (0, j)), pl.BlockSpec((SUBLANES, tn), lambda j, i: (0, j)),
                   pl.BlockSpec((1, tn), lambda j, i: (0, j)), pl.BlockSpec((1, tn), lambda j, i: (0, j))],
        out_shape=[f((2, S, D_FF), F32), f((SUBLANES, D_FF), F32), f((SUBLANES, D_FF), F32),
                   f((1, D_FF), F32), f((1, D_FF), F32)],
        compiler_params=_cparams(("parallel", "arbitrary")))(dact, u, u, u, u, conv_w8, conv_w8, conv_b, conv_b)
    return outs


def _conv_bwd_b(duc, conv_w8, tm=256, tn=CONV_TN):
    _, S, W = duc.shape
    nj = W // tn
    n_rows = S // tm

    def body(d_ref, h_ref, w_ref, o_ref):
        last = pl.program_id(0) == n_rows - 1
        d = d_ref[...]
        h = h_ref[...]
        w = w_ref[...]
        o_ref[...] = (w[2:3] * d + w[1:2] * _shift_up(d, h, 1, last) + w[0:1] * _shift_up(d, h, 2, last)).astype(BF16)

    last_tile = S // SUBLANES - 1
    return pl.pallas_call(
        body, name="conv_bwd_b", grid=(n_rows, 2 * nj),
        in_specs=[pl.BlockSpec((None, tm, tn), lambda i, j: (j // nj, i, j % nj)),
                  pl.BlockSpec((None, SUBLANES, tn), lambda i, j: (j // nj, jnp.minimum((i + 1) * (tm // SUBLANES), last_tile), j % nj)),
                  pl.BlockSpec((SUBLANES, tn), lambda i, j: (0, j))],
        out_specs=pl.BlockSpec((tm, tn), lambda i, j: (i, j)),
        out_shape=jax.ShapeDtypeStruct((S, 2 * W), BF16),
        compiler_params=_cparams(("parallel", "parallel")))(duc, duc, conv_w8)


ATT_SCALE = HEAD ** -0.5
NEG = -1e30
ATT_PAIRS = ATT_HEADS // 2


def _att_rows(n, d, S):
    per = S // (QBLK * d)
    r, m = n // per, n % per
    cur = pl.ds(m * (QBLK * d) + r, QBLK, stride=d)
    prv = pl.ds(jnp.maximum(m - 1, 0) * (QBLK * d) + r, QBLK, stride=d)
    return cur, prv, m > 0


def _att_slab(g, j):
    return (C_ATT + g * 3 * ATT_W + j * ATT_W) // LANES


def _heads(x):
    return x[:, 0:HEAD], x[:, HEAD:2 * HEAD]


ATT_NB = 4


def _stack(tiles):
    return jnp.concatenate([t[None] for t in tiles], axis=0)


def _att_operands(i, d, S, *sources):
    rows, has = [], []
    tiles = [[] for _ in sources]
    for bb in range(ATT_NB):
        cur, prv, has_prev = _att_rows(i * ATT_NB + bb, d, S)
        rows.append((cur, prv))
        has.append(has_prev)
        for t, (ref, use_cur) in zip(tiles, sources):
            t += _heads(ref[cur if use_cur else prv, :].astype(BF16))
    return rows, has, [_stack(t) for t in tiles]


def _att_mask(s_c, s_p, has_prev):
    qi = lax.broadcasted_iota(jnp.int32, (QBLK, QBLK), 0)
    kj = lax.broadcasted_iota(jnp.int32, (QBLK, QBLK), 1)
    s_c = jnp.where(kj <= qi, s_c * ATT_SCALE, NEG)
    s_p = jnp.where(jnp.logical_and(kj >= qi, has_prev), s_p * ATT_SCALE, NEG)
    return s_c, s_p


def _att_fwd(P, g):
    S = P.shape[0]
    d = ATT_PATTERNS[g][1]

    def body(q_ref, k_ref, v_ref, o_ref, l_ref):
        def group(i, carry):
            rows, has, (q, kc, kp, vc, vp) = _att_operands(i, d, S, (q_ref, True), (k_ref, True), (k_ref, False),
                                                           (v_ref, True), (v_ref, False))
            s_c_all, s_p_all = _dot16(q, kc, "nt"), _dot16(q, kp, "nt")
            p_c, p_p, den, lse = [], [], [], []
            for e in range(2 * ATT_NB):
                s_c, s_p = _att_mask(s_c_all[e], s_p_all[e], has[e // 2])
                m = jnp.maximum(jnp.max(s_c, axis=1, keepdims=True), jnp.max(s_p, axis=1, keepdims=True))
                pc, pp = jnp.exp(s_c - m), jnp.exp(s_p - m)
                den.append(jnp.sum(pc, axis=1, keepdims=True) + jnp.sum(pp, axis=1, keepdims=True))
                lse.append(jnp.broadcast_to(m + jnp.log(den[e]), (QBLK, HEAD)))
                p_c.append(pc)
                p_p.append(pp)
            num = _dot16(_stack(p_c), vc, "nn") + _dot16(_stack(p_p), vp, "nn")
            for bb, (cur, _) in enumerate(rows):
                o_ref[cur, :] = jnp.concatenate([num[2 * bb] / den[2 * bb], num[2 * bb + 1] / den[2 * bb + 1]], axis=1)
                l_ref[cur, :] = jnp.concatenate(lse[2 * bb:2 * bb + 2], axis=1)
            return carry

        lax.fori_loop(0, S // QBLK // ATT_NB, group, 0)

    slab = lambda j: pl.BlockSpec((S, LANES), lambda i: (0, _att_slab(g, j) + i))
    out = pl.BlockSpec((S, LANES), lambda i: (0, i))
    shp = jax.ShapeDtypeStruct((S, ATT_W), F32)
    return pl.pallas_call(body, name=f"att_fwd_g{g}", grid=(ATT_PAIRS,), in_specs=[slab(0), slab(1), slab(2)],
                          out_specs=[out, out], out_shape=[shp, shp], compiler_params=_cparams(("parallel",)))(P, P, P)


def _att_bwd(P, o, l, do, dl, g):
    S = P.shape[0]
    d = ATT_PATTERNS[g][1]

    def body(q_ref, k_ref, v_ref, o_ref, l_ref, do_ref, dl_ref, dq_ref, dk_ref, dv_ref, dq_acc, dk_acc, dv_acc):
        dk_acc[...] = jnp.zeros_like(dk_acc)
        dv_acc[...] = jnp.zeros_like(dv_acc)

        def group(i, carry):
            rows, has, (q, kc, kp, vc, vp, dob) = _att_operands(
                i, d, S, (q_ref, True), (k_ref, True), (k_ref, False), (v_ref, True), (v_ref, False), (do_ref, True))
            s_c_all, s_p_all = _dot16(q, kc, "nt"), _dot16(q, kp, "nt")
            dp_c_all, dp_p_all = _dot16(dob, vc, "nt"), _dot16(dob, vp, "nt")
            p_c, p_p, ds_c, ds_p = [], [], [], []
            for bb, (cur, _) in enumerate(rows):
                dd2 = do_ref[cur, :] * o_ref[cur, :] - dl_ref[cur, :]
                for h, (dd, lse) in enumerate(zip(_heads(dd2), _heads(l_ref[cur, :]))):
                    e = 2 * bb + h
                    s_c, s_p = _att_mask(s_c_all[e], s_p_all[e], has[bb])
                    pc, pp = jnp.exp(s_c - lse[:, 0:1]), jnp.exp(s_p - lse[:, 0:1])
                    delta = jnp.sum(dd, axis=1, keepdims=True)
                    p_c.append(pc)
                    p_p.append(pp)
                    ds_c.append(pc * (dp_c_all[e] - delta) * ATT_SCALE)
                    ds_p.append(pp * (dp_p_all[e] - delta) * ATT_SCALE)
            p_c, p_p, ds_c, ds_p = map(_stack, (p_c, p_p, ds_c, ds_p))
            dq = _dot16(ds_c, kc, "nn") + _dot16(ds_p, kp, "nn")
            dk_c, dk_p = _dot16(ds_c, q, "tn"), _dot16(ds_p, q, "tn")
            dv_c, dv_p = _dot16(p_c, dob, "tn"), _dot16(p_p, dob, "tn")
            pair = lambda x, bb: jnp.concatenate([x[2 * bb], x[2 * bb + 1]], axis=1)
            for bb, (cur, prv) in enumerate(rows):
                dq_acc[cur, :] = pair(dq, bb)
                dk_acc[cur, :] += pair(dk_c, bb)
                dv_acc[cur, :] += pair(dv_c, bb)
                dk_acc[prv, :] += pair(dk_p, bb)
                dv_acc[prv, :] += pair(dv_p, bb)
            return carry

        lax.fori_loop(0, S // QBLK // ATT_NB, group, 0)
        dq_ref[...] = dq_acc[...].astype(BF16)
        dk_ref[...] = dk_acc[...].astype(BF16)
        dv_ref[...] = dv_acc[...].astype(BF16)

    slab = lambda j: pl.BlockSpec((S, LANES), lambda i: (0, _att_slab(g, j) + i))
    blk128 = pl.BlockSpec((S, LANES), lambda i: (0, i))
    shp = jax.ShapeDtypeStruct((S, ATT_W), BF16)
    return pl.pallas_call(body, name=f"att_bwd_g{g}", grid=(ATT_PAIRS,),
                          in_specs=[slab(0), slab(1), slab(2)] + [blk128] * 4, out_specs=[blk128] * 3, out_shape=[shp] * 3,
                          scratch_shapes=[pltpu.VMEM((S, LANES), F32)] * 3,
                          compiler_params=_cparams(("parallel",)))(P, P, P, o, l, do, dl)


def _att_weights(l_refs):
    l0, l1, l2 = [r[...] for r in l_refs]
    m = jnp.maximum(jnp.maximum(l0, l1), l2)
    e = (jnp.exp(l0 - m), jnp.exp(l1 - m), jnp.exp(l2 - m))
    inv = 1.0 / (e[0] + e[1] + e[2])
    return [x * inv for x in e]


def _att_combine_fwd(os, ls, tm=512):
    S = os[0].shape[0]

    def body(o0, o1, o2, l0, l1, l2, a_ref):
        w = _att_weights((l0, l1, l2))
        a_ref[...] = (w[0] * o0[...] + w[1] * o1[...] + w[2] * o2[...]).astype(BF16)

    row = _rows(tm, ATT_W)
    return pl.pallas_call(body, name="att_combine_fwd", grid=(S // tm,), in_specs=[row] * 6, out_specs=row,
                          out_shape=jax.ShapeDtypeStruct((S, ATT_W), BF16),
                          compiler_params=_cparams(("parallel",)))(*os, *ls)


def _att_combine_bwd(da, os, ls, tm=512):
    S = da.shape[0]

    def body(da_ref, o0, o1, o2, l0, l1, l2, *out_refs):
        da = da_ref[...]
        w = _att_weights((l0, l1, l2))
        dw = (da * o0[...], da * o1[...], da * o2[...])
        mean = w[0] * dw[0] + w[1] * dw[1] + w[2] * dw[2]
        for g in range(3):
            out_refs[g][...] = w[g] * da
            out_refs[3 + g][...] = w[g] * (dw[g] - mean)

    row = _rows(tm, ATT_W)
    shp = jax.ShapeDtypeStruct((S, ATT_W), F32)
    return pl.pallas_call(body, name="att_combine_bwd", grid=(S // tm,), in_specs=[row] * 7, out_specs=[row] * 6,
                          out_shape=[shp] * 6, compiler_params=_cparams(("parallel",)))(da, *os, *ls)


@jax.custom_vjp
def _bdot(a, b):
    return jnp.dot(a.astype(BF16), b.astype(BF16), preferred_element_type=F32)


def _bdot_fwd(a, b):
    return _bdot(a, b), (a, b)


def _bdot_bwd(res, ct):
    a, b = res
    ct16 = ct.astype(BF16)
    da = lax.dot_general(ct16, b.astype(BF16), (((1,), (1,)), ((), ())), preferred_element_type=F32)
    db = lax.dot_general(a.astype(BF16), ct16, (((0,), (0,)), ((), ())), preferred_element_type=F32)
    return da, db


_bdot.defvjp(_bdot_fwd, _bdot_bwd)


def _two_piece_dot(x, m):
    hi = x.astype(BF16)
    lo = (x - hi.astype(F32)).astype(BF16)
    return jnp.dot(hi, m, preferred_element_type=F32) + jnp.dot(lo, m, preferred_element_type=F32)


def _head_sum_impl(x):
    sel = (lax.broadcasted_iota(jnp.int32, (D, LANES), 0) // HEAD == lax.broadcasted_iota(jnp.int32, (D, LANES), 1)).astype(BF16)
    sel_t = (lax.broadcasted_iota(jnp.int32, (LANES, D), 1) // HEAD == lax.broadcasted_iota(jnp.int32, (LANES, D), 0)).astype(BF16)
    return _two_piece_dot(_two_piece_dot(x, sel), sel_t)


@jax.custom_vjp
def _head_sum(x):
    return _head_sum_impl(x)


_head_sum.defvjp(lambda x: (_head_sum_impl(x), None), lambda _, ct: (_head_sum_impl(ct),))


def _softplus(z):
    return jnp.maximum(z, 0.0) + jnp.log(1.0 + jnp.exp(-jnp.abs(z)))


def _rwkv_prep_fn(zr, zrp, zk, zkp, zv, zvp, zl, zlp, mu_r, mu_k, mu_v, mu_l, w0, a0, k_k, k_a, w2, a2, g2p):
    r = zr + (zrp - zr) * mu_r
    k = zk + (zkp - zk) * mu_k
    v = zv + (zvp - zv) * mu_v
    lo = zl + (zlp - zl) * mu_l
    w_low, a_low, g_low = lo[:, 0:LORA_W], lo[:, LORA_W:LORA_W + LORA_A], lo[:, LANES:LANES + G_PAD]
    w_log = -_softplus(-(w0 + _bdot(jnp.tanh(w_low), w2))) - 0.5
    decay = -jnp.exp(w_log)
    a = jax.nn.sigmoid(a0 + _bdot(a_low, a2))
    g = _bdot(jax.nn.sigmoid(g_low), g2p)
    kmod = k * (1.0 + (a - 1.0) * k_a)
    kk = k * k_k
    kk = kk / jnp.maximum(jnp.sqrt(_head_sum(kk * kk)), 1e-12)
    return r, decay, kmod, v, -kk, kk * a, g


def _rwkv_prep_specs(tm):
    vec = _full((1, D))
    slabs = []
    for col in (C_R // D, C_K // D, C_V // D):
        slabs += [_rows(tm, D, col), _prev8(tm, D, col)]
    slabs += [_rows(tm, LORA_PAD, C_LORA // LORA_PAD), _prev8(tm, LORA_PAD, C_LORA // LORA_PAD)]
    params = [vec, vec, vec, _full((1, LORA_PAD)), vec, vec, vec, vec,
              _full((LORA_W, D)), _full((LORA_A, D)), _full((G_PAD, D))]
    return slabs, params


def _prep_inputs(refs, first):
    vals = []
    for s in range(4):
        z = refs[2 * s][...]
        vals += [z, _shift_down(z, refs[2 * s + 1][...], 1, first)]
    return vals + [r[...] for r in refs[8:19]]


def _rwkv_prep(P, params, tm=256):
    S = P.shape[0]
    slabs, pspecs = _rwkv_prep_specs(tm)

    def body(*refs):
        outs = _rwkv_prep_fn(*_prep_inputs(refs, pl.program_id(0) == 0))
        for o_ref, val in zip(refs[19:], outs):
            o_ref[...] = val

    shp = jax.ShapeDtypeStruct((S, D), F32)
    return pl.pallas_call(body, name="rwkv_prep", grid=(S // tm,), in_specs=slabs + pspecs,
                          out_specs=[_rows(tm, D)] * 7, out_shape=[shp] * 7,
                          compiler_params=_cparams(("parallel",)))(*([P] * 8), *params)


def _rwkv_prep_bwd(P, params, cts_a, cts_b, tm=128):
    S = P.shape[0]
    slabs, pspecs = _rwkv_prep_specs(tm)
    has_b = [c is not None for c in cts_b]
    n_ct = 7 + sum(has_b)

    def body(*refs):
        first = pl.program_id(0) == 0
        ins = _prep_inputs(refs, first)
        ct_refs = refs[19:19 + n_ct]
        out_refs = refs[19 + n_ct:]
        cts, pos = [], 7
        for i in range(7):
            c = ct_refs[i][...]
            if has_b[i]:
                c = c + ct_refs[pos][...]
                pos += 1
            cts.append(c)
        _, vjp = jax.vjp(_rwkv_prep_fn, *ins)
        grads = vjp(tuple(cts))
        for s in range(4):
            out_refs[s][...] = grads[2 * s]
            out_refs[4 + s][...] = grads[2 * s + 1]
        for i in range(11):
            _acc(out_refs[8 + i], grads[8 + i], first)

    ct_in = list(cts_a) + [c for c in cts_b if c is not None]
    row, lrow = _rows(tm, D), _rows(tm, LORA_PAD)
    f = jax.ShapeDtypeStruct
    zshapes = [f((S, D), F32)] * 3 + [f((S, LORA_PAD), F32)]
    pshapes = [f((1, D), F32)] * 3 + [f((1, LORA_PAD), F32)] + [f((1, D), F32)] * 4 + [f((LORA_W, D), F32), f((LORA_A, D), F32), f((G_PAD, D), F32)]
    return pl.pallas_call(
        body, name="rwkv_prep_bwd", grid=(S // tm,),
        in_specs=slabs + pspecs + [row] * n_ct,
        out_specs=[row, row, row, lrow] * 2 + pspecs,
        out_shape=zshapes * 2 + pshapes,
        compiler_params=_cparams(("arbitrary",)))(*([P] * 8), *params, *ct_in)


def _shift_add(a, b, tm=256):
    S, W = a.shape

    def body(a_ref, b_ref, h_ref, o_ref):
        last = pl.program_id(0) == pl.num_programs(0) - 1
        o_ref[...] = (a_ref[...] + _shift_up(b_ref[...], h_ref[...], 1, last)).astype(BF16)

    return pl.pallas_call(body, name="shift_add", grid=(S // tm,),
                          in_specs=[_rows(tm, W), _rows(tm, W), _next8(tm, W, S)],
                          out_specs=_rows(tm, W), out_shape=jax.ShapeDtypeStruct((S, W), BF16),
                          compiler_params=_cparams(("parallel",)))(a, b, b)


def _rwkv_post_fn(y, r, kmod, v, g, lnx_w, lnx_b, r_k):
    mean = _head_sum(y) * (1.0 / HEAD)
    yc = y - mean
    var = _head_sum(yc * yc) * (1.0 / HEAD)
    yn = yc * lax.rsqrt(var + GN_EPS) * lnx_w + lnx_b
    bonus = _head_sum(r * kmod * r_k) * v
    return (yn + bonus) * g


def _rwkv_post(y, r, kmod, v, g, lnx_w, lnx_b, r_k, tm=256):
    S = y.shape[0]

    def body(y_ref, r_ref, k_ref, v_ref, g_ref, w_ref, b_ref, rk_ref, o_ref):
        o_ref[...] = _rwkv_post_fn(y_ref[...], r_ref[...], k_ref[...], v_ref[...], g_ref[...],
                                   w_ref[...], b_ref[...], rk_ref[...]).astype(BF16)

    row, vec = _rows(tm, D), _full((1, D))
    return pl.pallas_call(body, name="rwkv_post", grid=(S // tm,), in_specs=[row] * 5 + [vec] * 3, out_specs=row,
                          out_shape=jax.ShapeDtypeStruct((S, D), BF16),
                          compiler_params=_cparams(("parallel",)))(y, r, kmod, v, g, lnx_w, lnx_b, r_k)


def _rwkv_post_bwd(drw, y, r, kmod, v, g, lnx_w, lnx_b, r_k, tm=256):
    S = y.shape[0]

    def body(d_ref, y_ref, r_ref, k_ref, v_ref, g_ref, w_ref, b_ref, rk_ref, *out_refs):
        first = pl.program_id(0) == 0
        _, vjp = jax.vjp(_rwkv_post_fn, y_ref[...], r_ref[...], k_ref[...], v_ref[...], g_ref[...],
                         w_ref[...], b_ref[...], rk_ref[...])
        grads = vjp(d_ref[...])
        for i in range(5):
            out_refs[i][...] = grads[i]
        for i in range(5, 8):
            _acc(out_refs[i], grads[i], first)

    row, vec = _rows(tm, D), _full((1, D))
    f = jax.ShapeDtypeStruct
    return pl.pallas_call(body, name="rwkv_post_bwd", grid=(S // tm,), in_specs=[row] * 6 + [vec] * 3,
                          out_specs=[row] * 5 + [vec] * 3, out_shape=[f((S, D), F32)] * 5 + [f((1, D), F32)] * 3,
                          compiler_params=_cparams(("arbitrary",)))(drw, y, r, kmod, v, g, lnx_w, lnx_b, r_k)


CHUNK = 64
CHUNK_TB = 256
_DOT_DIMS = {"nn": (((2,), (1,)), ((0,), (0,))), "nt": (((2,), (2,)), ((0,), (0,))), "tn": (((1,), (1,)), ((0,), (0,)))}


def _dot16(x, y, mode):
    return lax.dot_general(x.astype(BF16), y.astype(BF16), _DOT_DIMS[mode], preferred_element_type=F32)


@functools.partial(jax.custom_vjp, nondiff_argnums=(2,))
def _mm16(x, y, mode):
    return _dot16(x, y, mode)


def _mm16_fwd(x, y, mode):
    return _dot16(x, y, mode), (x, y)


def _mm16_bwd(mode, res, ct):
    x, y = res
    if mode == "nn":
        return _dot16(ct, y, "nt"), _dot16(x, ct, "tn")
    if mode == "nt":
        return _dot16(ct, y, "nn"), _dot16(ct, x, "tn")
    return _dot16(y, ct, "nt"), _dot16(x, ct, "nn")


_mm16.defvjp(_mm16_fwd, _mm16_bwd)


def _tri_sum(x, upper):
    T = x.shape[0]
    i = lax.broadcasted_iota(jnp.int32, (T, T), 0)
    j = lax.broadcasted_iota(jnp.int32, (T, T), 1)
    tri = ((j >= i) if upper else (i >= j)).astype(BF16)
    out, rest = None, x
    for _ in range(3):
        piece = rest.astype(BF16)
        rest = rest - piece.astype(F32)
        part = jnp.dot(tri, piece, preferred_element_type=F32)
        out = part if out is None else out + part
    return out


@jax.custom_vjp
def _cumsum_rows(x):
    return _tri_sum(x, False)


_cumsum_rows.defvjp(lambda x: (_tri_sum(x, False), None), lambda _, ct: (_tri_sum(ct, True),))


def _rows_to_cols(x):
    H, _, K = x.shape
    eye = (lax.broadcasted_iota(jnp.int32, (H, K, K), 1) == lax.broadcasted_iota(jnp.int32, (H, K, K), 2)).astype(F32)
    out = lax.dot_general(eye, jnp.broadcast_to(x, (H, SUBLANES, K)), _DOT_DIMS["nt"],
                          precision=lax.Precision.HIGHEST, preferred_element_type=F32)
    return out[:, :, 0:1]


def _per_head(x):
    return jnp.concatenate([x[:, h * HEAD:(h + 1) * HEAD][None] for h in range(N_HEADS)], axis=0)


def _chunk_fn(st0, r, lw, k, v, a, b):
    T = r.shape[0]
    cl = _cumsum_rows(lw)
    cl_end = cl[T - 1:T, :]
    inv = jnp.exp(-cl)
    to_end = jnp.exp(cl_end - cl)
    ah, rh, bh, kh, be, ke, v3 = [_per_head(x) for x in
                                  (a * jnp.exp(cl - lw), r * jnp.exp(cl), b * inv, k * inv, b * to_end, k * to_end, v)]
    i = lax.broadcasted_iota(jnp.int32, (N_HEADS, T, T), 1)
    j = lax.broadcasted_iota(jnp.int32, (N_HEADS, T, T), 2)
    a_ab = jnp.where(i > j, _mm16(ah, bh, "nt"), 0.0)
    a_ak = jnp.where(i > j, _mm16(ah, kh, "nt"), 0.0)
    m_rb = jnp.where(i >= j, _mm16(rh, bh, "nt"), 0.0)
    m_rk = jnp.where(i >= j, _mm16(rh, kh, "nt"), 0.0)
    rhs = _mm16(ah, st0, "nn") + _mm16(a_ak, v3, "nn")
    power, solve, n = a_ab, (i == j).astype(F32) + a_ab, 1
    while 2 * n < T:
        power = _mm16(power, power, "nn")
        solve = solve + _mm16(solve, power, "nn")
        n *= 2
    sa = _mm16(solve, rhs, "nn")
    y3 = _mm16(rh, st0, "nn") + _mm16(m_rb, sa, "nn") + _mm16(m_rk, v3, "nn")
    st_end = _rows_to_cols(_per_head(jnp.exp(cl_end))) * st0 + _mm16(be, sa, "tn") + _mm16(ke, v3, "tn")
    return jnp.concatenate([y3[h] for h in range(N_HEADS)], axis=1), st_end


def _hosted_exchange(refs, n, broadcast, grid):
    if n == 0:
        return lambda: None
    start, wait = _exchange_ops(refs[:n], refs[n:2 * n], *refs[2 * n:], broadcast)
    first = functools.reduce(jnp.logical_and, [pl.program_id(a) == 0 for a in range(len(grid))])
    last = functools.reduce(jnp.logical_and, [pl.program_id(a) == g - 1 for a, g in enumerate(grid)])
    pl.when(first)(start)
    return lambda: pl.when(last)(wait)


def _cscan_fwd(r, lw, k, v, a, b, gather=()):
    S = r.shape[0]
    per_blk = CHUNK_TB // CHUNK
    n_x = len(gather)
    nblk = S // CHUNK_TB

    def body(*refs):
        r_ref, lw_ref, k_ref, v_ref, a_ref, b_ref = refs[:6]
        y_ref, ck_ref = refs[6 + n_x:8 + n_x]
        st_ref = refs[8 + 2 * n_x]
        finish = _hosted_exchange(refs[6:6 + n_x] + refs[8 + n_x:8 + 2 * n_x] + refs[9 + 2 * n_x:], n_x, True, (nblk,))

        @pl.when(pl.program_id(0) == 0)
        def _():
            st_ref[...] = jnp.zeros_like(st_ref)

        def chunk(c, carry):
            rows = pl.ds(pl.multiple_of(c * CHUNK, CHUNK), CHUNK)
            st0 = st_ref[...]
            ck_ref[c] = st0
            y, st_end = _chunk_fn(st0, r_ref[rows, :], lw_ref[rows, :], k_ref[rows, :],
                                  v_ref[rows, :], a_ref[rows, :], b_ref[rows, :])
            y_ref[rows, :] = y
            st_ref[...] = st_end
            return carry

        lax.fori_loop(0, per_blk, chunk, 0)
        finish()

    blk = _rows(CHUNK_TB, D)
    any_spec = pl.BlockSpec(memory_space=pl.ANY)
    outs = pl.pallas_call(
        body, name="scan_fwd", grid=(nblk,), in_specs=[blk] * 6 + [any_spec] * n_x,
        out_specs=[blk, pl.BlockSpec((per_blk, N_HEADS, HEAD, HEAD), lambda i: (i, 0, 0, 0))] + [any_spec] * n_x,
        out_shape=[jax.ShapeDtypeStruct((S, D), F32), jax.ShapeDtypeStruct((S // CHUNK, N_HEADS, HEAD, HEAD), F32)]
        + _exchange_shapes(gather, True),
        scratch_shapes=[pltpu.VMEM((N_HEADS, HEAD, HEAD), F32)] + (_exchange_scratch(n_x) if n_x else []),
        compiler_params=_cparams(("arbitrary",)))(r, lw, k, v, a, b, *gather)
    return outs[0], outs[1], outs[2:]


def _cscan_bwd(r, lw, k, v, a, b, ckpt, dy, scatter=()):
    S = r.shape[0]
    per_blk = CHUNK_TB // CHUNK
    nblk = S // CHUNK_TB
    n_x = len(scatter)

    def body(*refs):
        r_ref, lw_ref, k_ref, v_ref, a_ref, b_ref, ck_ref, dy_ref = refs[:8]
        out_refs = refs[8 + n_x:14 + n_x]
        ds_ref = refs[14 + 2 * n_x]
        finish = _hosted_exchange(refs[8:8 + n_x] + refs[14 + n_x:14 + 2 * n_x] + refs[15 + 2 * n_x:], n_x, False, (nblk,))

        @pl.when(pl.program_id(0) == 0)
        def _():
            ds_ref[...] = jnp.zeros_like(ds_ref)

        def chunk(cc, carry):
            c = per_blk - 1 - cc
            rows = pl.ds(pl.multiple_of(c * CHUNK, CHUNK), CHUNK)
            ins = (ck_ref[c], r_ref[rows, :], lw_ref[rows, :], k_ref[rows, :], v_ref[rows, :], a_ref[rows, :], b_ref[rows, :])
            _, vjp = jax.vjp(_chunk_fn, *ins)
            grads = vjp((dy_ref[rows, :], ds_ref[...]))
            ds_ref[...] = grads[0]
            for o_ref, g in zip(out_refs, grads[1:]):
                o_ref[rows, :] = g
            return carry

        lax.fori_loop(0, per_blk, chunk, 0)
        finish()

    blk = pl.BlockSpec((CHUNK_TB, D), lambda i: (nblk - 1 - i, 0))
    any_spec = pl.BlockSpec(memory_space=pl.ANY)
    shp = jax.ShapeDtypeStruct((S, D), F32)
    outs = pl.pallas_call(
        body, name="scan_bwd", grid=(nblk,),
        in_specs=[blk] * 6 + [pl.BlockSpec((per_blk, N_HEADS, HEAD, HEAD), lambda i: (nblk - 1 - i, 0, 0, 0)), blk]
        + [any_spec] * n_x,
        out_specs=[blk] * 6 + [any_spec] * n_x, out_shape=[shp] * 6 + _exchange_shapes(scatter, False),
        scratch_shapes=[pltpu.VMEM((N_HEADS, HEAD, HEAD), F32)] + (_exchange_scratch(n_x) if n_x else []),
        compiler_params=_cparams(("arbitrary",)))(r, lw, k, v, a, b, ckpt, dy, *scatter)
    return outs[:6], outs[6:]


def _ada_partial(c_all, w_shard):
    def body(c_ref, w_ref, o_ref):
        o_ref[...] = jnp.dot(c_ref[...].astype(BF16), w_ref[...].astype(BF16), preferred_element_type=F32)

    vm = pl.BlockSpec(memory_space=pltpu.VMEM)
    return pl.pallas_call(body, name="ada_partial", in_specs=[vm, vm], out_specs=vm,
                          out_shape=jax.ShapeDtypeStruct((N_DEV, w_shard.shape[1]), F32),
                          compiler_params=pltpu.CompilerParams(vmem_limit_bytes=VMEM_LIMIT))(c_all, w_shard)


def _ada_bias(rows, b_ada):
    def body(r_ref, b_ref, o_ref):
        o_ref[...] = r_ref[...] + b_ref[...]

    vm = pl.BlockSpec(memory_space=pltpu.VMEM)
    return pl.pallas_call(body, name="ada_bias", in_specs=[vm, vm], out_specs=vm,
                          out_shape=jax.ShapeDtypeStruct(rows.shape, F32))(rows, b_ada)


def _ada_wgrad(c_cols, d_all):
    def body(c_ref, d_ref, o_ref):
        acc = c_ref[:, 0:1] * d_ref[0:1, :]
        for j in range(1, N_DEV):
            acc = acc + c_ref[:, j:j + 1] * d_ref[j:j + 1, :]
        o_ref[...] = acc

    vm = pl.BlockSpec(memory_space=pltpu.VMEM)
    return pl.pallas_call(body, name="ada_wgrad", in_specs=[vm, vm], out_specs=vm,
                          out_shape=jax.ShapeDtypeStruct((D, d_all.shape[1]), F32),
                          compiler_params=pltpu.CompilerParams(vmem_limit_bytes=VMEM_LIMIT))(c_cols, d_all)


def _exchange(srcs, broadcast, name):
    n = len(srcs)

    def body(*refs):
        start, wait = _exchange_ops(refs[:n], refs[n:2 * n], *refs[2 * n:], broadcast)
        start()
        wait()

    any_spec = pl.BlockSpec(memory_space=pl.ANY)
    return pl.pallas_call(
        body, name=name, out_shape=_exchange_shapes(srcs, broadcast), in_specs=[any_spec] * n, out_specs=[any_spec] * n,
        scratch_shapes=_exchange_scratch(n),
        compiler_params=pltpu.CompilerParams(has_side_effects=True),
    )(*srcs)


def _gather_via_sibling(srcs, name):
    n = len(srcs)

    def body(*refs):
        src_refs, out_refs = refs[:n], refs[n:2 * n]
        send_sems, recv_sems, local_sems = refs[2 * n:]
        x, y, c = lax.axis_index("x"), lax.axis_index("y"), lax.axis_index("c")
        me, sibling = (x, y, c), (x, y, 1 - c)
        chips = [(1 - x, y), (x, 1 - y), (1 - x, 1 - y)]

        def slot(px, py, pc):
            return 4 * px + 2 * py + pc

        def copy(i, k, block, to, src=None):
            rows = out_refs[i].at[slot(*block)]
            return pltpu.make_async_remote_copy(
                src_ref=rows if src is None else src, dst_ref=rows, send_sem=send_sems.at[i, k],
                recv_sem=recv_sems.at[i, k], device_id=to, device_id_type=_MESH)

        local = [pltpu.make_async_copy(src_refs[i], out_refs[i].at[slot(*me)], local_sems.at[i]) for i in range(n)]
        for cp in local:
            cp.start()
        first = [copy(i, 0, me, sibling, src=src_refs[i]) for i in range(n)]
        first += [copy(i, 1 + j, me, (*chip, c), src=src_refs[i]) for j, chip in enumerate(chips) for i in range(n)]
        for cp in first:
            cp.start()
        passed = []
        for j, chip in enumerate(chips):
            for i in range(n):
                copy(i, 1 + j, (*chip, c), me).wait_recv()
                passed.append(copy(i, 4 + j, (*chip, c), sibling))
                passed[-1].start()
        for i in range(n):
            copy(i, 0, sibling, me).wait_recv()
            for j, chip in enumerate(chips):
                copy(i, 4 + j, (*chip, 1 - c), me).wait_recv()
        for cp in first + passed:
            cp.wait_send()
        for cp in local:
            cp.wait()

    any_spec = pl.BlockSpec(memory_space=pl.ANY)
    return pl.pallas_call(
        body, name=name, out_shape=_exchange_shapes(srcs, True), in_specs=[any_spec] * n, out_specs=[any_spec] * n,
        scratch_shapes=_exchange_scratch(n),
        compiler_params=pltpu.CompilerParams(has_side_effects=True),
    )(*srcs)


def _flags(broadcast, n):
    return [broadcast] * n if isinstance(broadcast, bool) else list(broadcast)


def _exchange_shapes(srcs, broadcast):
    return [jax.ShapeDtypeStruct((N_DEV,) + (s.shape if bc else s.shape[1:]), s.dtype)
            for s, bc in zip(srcs, _flags(broadcast, len(srcs)))]


def _exchange_scratch(n):
    return [pltpu.SemaphoreType.DMA((n, N_DEV)), pltpu.SemaphoreType.DMA((n, N_DEV)), pltpu.SemaphoreType.DMA((n,))]


def _exchange_ops(src_refs, out_refs, send_sems, recv_sems, local_sems, broadcast):
    n = len(src_refs)
    flags = _flags(broadcast, n)
    x, y, c = lax.axis_index("x"), lax.axis_index("y"), lax.axis_index("c")
    me = 4 * x + 2 * y + c

    def block(i, j):
        return src_refs[i] if flags[i] else src_refs[i].at[j]

    def remote(i, d, src_slot, dst_slot):
        px, py, pc = x ^ (d >> 2), y ^ ((d >> 1) & 1), c ^ (d & 1)
        return pltpu.make_async_remote_copy(
            src_ref=block(i, src_slot), dst_ref=out_refs[i].at[dst_slot], send_sem=send_sems.at[i, d],
            recv_sem=recv_sems.at[i, d], device_id=(px, py, pc), device_id_type=_MESH)

    def local(i):
        return pltpu.make_async_copy(block(i, me), out_refs[i].at[me], local_sems.at[i])

    def start():
        for i in range(n):
            local(i).start()
        for d in range(1, N_DEV):
            for i in range(n):
                remote(i, d, me ^ d, me).start()

    def wait():
        for d in range(1, N_DEV):
            for i in range(n):
                remote(i, d, me, me ^ d).wait_recv()
        for d in range(1, N_DEV):
            for i in range(n):
                remote(i, d, me ^ d, me).wait_send()
        for i in range(n):
            local(i).wait()

    return start, wait


def _sum_adam(parts, w, m, v, name):
    n_parts, R, C = parts.shape
    fits = [t for t in range(16, R + 1, 16) if R % t == 0 and t * C <= 2504 * LANES]
    if fits:
        tm, tc = max(fits), C
    elif C % (2 * LANES) == 0 and R * C > 2504 * LANES:
        tm, tc = R, 2 * LANES
    else:
        tm, tc = R, C
    c1 = 1.0 / (1.0 - ADAM_B1 ** ADAM_STEP)
    c2 = 1.0 / (1.0 - ADAM_B2 ** ADAM_STEP)

    def body(p_ref, w_ref, m_ref, v_ref, g_ref, d_ref, nm_ref, nv_ref):
        g = p_ref[0].astype(F32)
        for j in range(1, n_parts):
            g = g + p_ref[j].astype(F32)
        nm = ADAM_B1 * m_ref[...] + (1.0 - ADAM_B1) * g
        nv = ADAM_B2 * v_ref[...] + (1.0 - ADAM_B2) * (g * g)
        g_ref[...] = g
        nm_ref[...] = nm
        nv_ref[...] = nv
        d_ref[...] = -ADAM_LR * ((nm * c1) / (jnp.sqrt(nv * c2) + ADAM_EPS) + ADAM_WD * w_ref[...])

    blk = pl.BlockSpec((tm, tc), lambda i, j: (i, j))
    shp = jax.ShapeDtypeStruct((R, C), F32)
    return pl.pallas_call(body, name=name, grid=(R // tm, C // tc),
                          in_specs=[pl.BlockSpec((n_parts, tm, tc), lambda i, j: (0, i, j)), blk, blk, blk],
                          out_specs=[blk] * 4, out_shape=[shp] * 4,
                          compiler_params=_cparams(("parallel", "parallel")))(parts, w, m, v)


PACK_ALIGN = 16 * LANES
PACK_ROWS = 512 * LANES

TRANSPOSED = ("w_in", "w_up")
SHARDED = (("w_ada", 1), ("w_in", 0), ("w2", 1), ("a2", 1), ("g2", 1), ("w_att_out", 1), ("w_rwkv_out", 0),
           ("w_o", 0), ("w_up", 0), ("conv_w", 1), ("w_down", 0))
EARLY, LATE = SHARDED[1:5], SHARDED[5:]
REPLICATED = ("b_ada", "norm1_w", "b_gate", "mu_shift", "w0", "a0", "k_k", "k_a", "r_k", "lnx_w", "lnx_b",
              "norm2_w", "conv_b", "norm_f_w")
WEIGHTS = ("w_ada", "b_ada", "norm1_w", "w_in", "b_gate", "mu_shift", "w0", "w2", "a0", "a2", "g2", "k_k", "k_a", "r_k",
           "lnx_w", "lnx_b", "w_att_out", "w_rwkv_out", "w_o", "norm2_w", "w_up", "conv_w", "conv_b", "w_down", "norm_f_w")


def _pack(arrays):
    flat, layout, off = [], [], 0
    for i, a in enumerate(arrays):
        n = a.size
        pad = (-n) % PACK_ALIGN if i + 1 < len(arrays) else (-(off + n)) % PACK_ROWS
        flat.append(a.reshape(-1))
        if pad:
            flat.append(jnp.zeros((pad,), a.dtype))
        layout.append((off, n, a.shape))
        off += n + pad
    return jnp.concatenate(flat).reshape(-1, LANES), layout


def _unpack(buf, layout):
    flat = buf.reshape(-1)
    return [flat[off:off + n].reshape(shape) for off, n, shape in layout]


def _pad_w_in(w_in_t):
    rkv = w_in_t[ATT_IN:ATT_IN + 3 * D]
    lora = w_in_t[ATT_IN + 3 * D:ATT_IN + RWKV_IN]
    gates = w_in_t[ATT_IN + RWKV_IN:]
    att = w_in_t[:ATT_IN]
    lw, la, lg = lora[:LORA_W], lora[LORA_W:LORA_W + LORA_A], lora[LORA_W + LORA_A:]
    zeros = jnp.zeros((LORA_PAD - LANES - LORA_G, w_in_t.shape[1]), w_in_t.dtype)
    return jnp.concatenate([rkv, gates, att, lw, la, lg, zeros], axis=0)


def _unpad_w_in(g):
    att = g[C_ATT:C_ATT + ATT_IN]
    rkv = g[C_R:C_R + 3 * D]
    lora = jnp.concatenate([g[C_LORA:C_LORA + LORA_W + LORA_A], g[C_LORA + LANES:C_LORA + LANES + LORA_G]], axis=0)
    gates = g[C_GA:C_GA + 2 * D]
    return jnp.concatenate([att, rkv, lora, gates], axis=0)


def _pad_mu(mu):
    lo = mu[:, 3 * D:]
    mu_l = jnp.concatenate([lo[:, :LORA_W + LORA_A], lo[:, LORA_W + LORA_A:], jnp.zeros((1, LORA_PAD - LANES - LORA_G), mu.dtype)], axis=1)
    return mu[:, :D], mu[:, D:2 * D], mu[:, 2 * D:3 * D], mu_l


def _local_step(x, ada, W, late_shards, target):
    S = x.shape[0]
    W = dict(W)
    G = {}
    sh1, sc1, gt1, sh2, sc2, gt2 = [ada[:, i * D:(i + 1) * D] for i in range(6)]
    h1, rstd1 = _norm_fwd(x, None, None, W["norm1_w"], sc1, sh1, "norm1_fwd")
    w_in_p = _pad_w_in(W["w_in"])
    P = _mm(h1, w_in_p, "nt", F32, "proj_in")

    mu_r, mu_k, mu_v, mu_l = _pad_mu(W["mu_shift"])
    g2p = jnp.pad(W["g2"], ((0, G_PAD - LORA_G), (0, 0)))
    prep_params = [mu_r, mu_k, mu_v, mu_l, W["w0"], W["a0"], W["k_k"], W["k_a"], W["w2"], W["a2"], g2p]
    r_, dec, kmod, v_, aa, bb, gg = _rwkv_prep(P, prep_params)
    y_scan, states, late = _cscan_fwd(r_, dec, kmod, v_, aa, bb, gather=late_shards)
    W.update({n: _full_weight(g, axis) for (n, axis), g in zip(LATE, late)})

    o_g, l_g = zip(*[_att_fwd(P, g) for g in range(len(ATT_PATTERNS))])
    att = _att_combine_fwd(o_g, l_g)
    y_att = _mm(att, W["w_att_out"], "nn", F32, "att_out")
    r_k = W["r_k"].reshape(1, D)
    rw = _rwkv_post(y_scan, r_, kmod, v_, gg, W["lnx_w"], W["lnx_b"], r_k)
    y_rwkv = _mm(rw, W["w_rwkv_out"], "nn", F32, "rwkv_out")

    bga, bgr = W["b_gate"][:, :D], W["b_gate"][:, D:]
    mix = _gate_fwd(P, bga, bgr, y_att, y_rwkv)
    mo = _mm(mix, W["w_o"], "nn", F32, "mix_out")
    x2, h2, rstd2 = _norm_fwd(x, mo, gt1, W["norm2_w"], sc2, sh2, "norm2_fwd")
    u = _mm(h2, W["w_up"], "nt", F32, "ffn_up")
    conv_w8 = jnp.pad(W["conv_w"], ((0, SUBLANES - 3), (0, 0)))
    act = _conv_fwd(u, conv_w8, W["conv_b"])
    f = _mm(act, W["w_down"], "nn", F32, "ffn_down")
    loss_blk, dx3, df, dgt2, G["norm_f_w"] = _final(x2, f, gt2, W["norm_f_w"], target)
    loss = loss_blk[0, 0]

    dact = _mm(df, W["w_down"], "nt", BF16, "ffn_down_dx")
    G["w_down"] = _mm(act, df, "tn", BF16, "ffn_down_dw")
    duc, dwg, dwv, dbg, dbv = _conv_bwd_a(dact, u, conv_w8, W["conv_b"])
    G["conv_w"] = jnp.concatenate([dwg[0:3], dwv[0:3]], axis=1)
    G["conv_b"] = jnp.concatenate([dbg, dbv], axis=1)
    du = _conv_bwd_b(duc, conv_w8)
    dh2 = _mm(du, W["w_up"], "nn", F32, "ffn_up_dx")
    G["w_up"] = _mm(du, h2, "tn", BF16, "ffn_up_dw")
    dx2, dsh2, dsc2, G["norm2_w"], dmo, dgt1 = _norm_bwd(dh2, x2, rstd2, W["norm2_w"], sc2, dx3, mo, gt1, "norm2_bwd")
    dmix = _mm(dmo, W["w_o"], "nt", F32, "mix_out_dx")
    G["w_o"] = _mm(mix, dmo, "tn", BF16, "mix_out_dw")
    dy_att, dy_rwkv, dpga, dpgr, dbga, dbgr = _gate_bwd(dmix, P, bga, bgr, y_att, y_rwkv)
    G["b_gate"] = jnp.concatenate([dbga, dbgr], axis=1)

    datt = _mm(dy_att, W["w_att_out"], "nt", F32, "att_out_dx")
    G["w_att_out"] = _mm(att, dy_att, "tn", BF16, "att_out_dw")
    dcomb = _att_combine_bwd(datt, o_g, l_g)
    dp_att = []
    for g in range(len(ATT_PATTERNS)):
        dp_att += _att_bwd(P, o_g[g], l_g[g], dcomb[g], dcomb[3 + g], g)

    drw = _mm(dy_rwkv, W["w_rwkv_out"], "nt", F32, "rwkv_out_dx")
    G["w_rwkv_out"] = _mm(rw, dy_rwkv, "tn", BF16, "rwkv_out_dw")
    dy_scan, dr1, dk1, dv1, dgg, G["lnx_w"], G["lnx_b"], drk = _rwkv_post_bwd(drw, y_scan, r_, kmod, v_, gg, W["lnx_w"], W["lnx_b"], r_k)
    G["r_k"] = drk.reshape(W["r_k"].shape)
    late_blocks = [_owner_blocks(G[n], axis) for n, axis in LATE] if late_shards else []
    (dr2, ddec, dk2, dv2, daa, dbb), late_parts = _cscan_bwd(r_, dec, kmod, v_, aa, bb, states, dy_scan, scatter=late_blocks)
    pb = _rwkv_prep_bwd(P, prep_params, [dr2, ddec, dk2, dv2, daa, dbb, dgg], [dr1, None, dk1, dv1, None, None, None])
    dz, dzp, dpar = pb[0:4], pb[4:8], pb[8:]
    dp_rkv = [_shift_add(dz[i], dzp[i]) for i in range(3)]
    dp_lora = _shift_add(dz[3], dzp[3])
    dmu_r, dmu_k, dmu_v, dmu_l, G["w0"], G["a0"], G["k_k"], G["k_a"], G["w2"], G["a2"], dg2p = dpar
    G["g2"] = dg2p[0:LORA_G]
    G["mu_shift"] = jnp.concatenate([dmu_r, dmu_k, dmu_v, dmu_l[:, :LORA_W + LORA_A], dmu_l[:, LANES:LANES + LORA_G]], axis=1)

    dP = jnp.concatenate(dp_rkv + [dpga, dpgr] + dp_att + [dp_lora], axis=1)
    G["w_in"] = _unpad_w_in(_mm(dP, h1, "tn", BF16, "proj_in_dw"))
    if late_shards:
        dh1, (w_in_parts,) = _mm(dP, w_in_p, "nn", F32, "proj_in_dx", scatter=[_owner_blocks(G["w_in"], 0)])
        done = dict(zip([n for n, _ in LATE] + ["w_in"], list(late_parts) + [w_in_parts]))
    else:
        dh1, done = _mm(dP, w_in_p, "nn", F32, "proj_in_dx"), {}
    grad_x, dsh1, dsc1, G["norm1_w"] = _norm_bwd(dh1, x, rstd1, W["norm1_w"], sc1, dx2, None, None, "norm1_bwd")
    dada = jnp.concatenate([dsh1, dsc1, dgt1, dsh2, dsc2, dgt2], axis=1)
    G["b_ada"] = dada
    return loss, grad_x, G, done


def _full_weight(gathered, axis):
    _, rows, cols = gathered.shape
    if axis == 0:
        return gathered.reshape(N_DEV * rows, cols)
    return gathered.transpose(1, 0, 2).reshape(rows, N_DEV * cols)


def _owner_blocks(g, axis):
    rows, cols = g.shape
    g = g.astype(BF16)
    if axis == 0:
        return g.reshape(N_DEV, rows // N_DEV, cols)
    return g.reshape(rows, N_DEV, cols // N_DEV).transpose(1, 0, 2)


def kernel(x, c, w_ada, b_ada, norm1_w, w_in, b_gate, mu_shift, w0, w2, a0, a2, g2, k_k, k_a, r_k, lnx_w, lnx_b, w_att_out, w_rwkv_out, w_o, norm2_w, w_up, conv_w, conv_b, w_down, norm_f_w, loss_target, m_w_ada, m_b_ada, m_norm1_w, m_w_in, m_b_gate, m_mu_shift, m_w0, m_w2, m_a0, m_a2, m_g2, m_k_k, m_k_a, m_r_k, m_lnx_w, m_lnx_b, m_w_att_out, m_w_rwkv_out, m_w_o, m_norm2_w, m_w_up, m_conv_w, m_conv_b, m_w_down, m_norm_f_w, v_w_ada, v_b_ada, v_norm1_w, v_w_in, v_b_gate, v_mu_shift, v_w0, v_w2, v_a0, v_a2, v_g2, v_k_k, v_k_a, v_r_k, v_lnx_w, v_lnx_b, v_w_att_out, v_w_rwkv_out, v_w_o, v_norm2_w, v_w_up, v_conv_w, v_conv_b, v_w_down, v_norm_f_w):
    env = dict(locals())
    w_shard = {n: env[n] for n in WEIGHTS}
    m_shard = {n: env["m_" + n] for n in WEIGHTS}
    v_shard = {n: env["v_" + n] for n in WEIGHTS}

    def mat(shards, n):
        return jnp.swapaxes(shards[n][0], 0, 1) if n in TRANSPOSED else shards[n][0]

    c_all, *gathered = _gather_via_sibling([c] + [mat(w_shard, n).astype(BF16) for n, _ in EARLY], "gather_weights")
    c_all = c_all.reshape(N_DEV, D)
    W = {n: _full_weight(g, axis) for (n, axis), g in zip(EARLY, gathered)}
    for n in REPLICATED:
        W[n] = w_shard[n].reshape(1, -1) if n != "r_k" else w_shard[n][0]
    ada_cols = _ada_partial(c_all, w_shard["w_ada"][0])
    ada_rows, = _exchange([ada_cols[:, None, :]], False, "ada_rows")
    ada = _ada_bias(ada_rows.reshape(1, -1), w_shard["b_ada"])

    late_shards = [mat(w_shard, n).astype(BF16) for n, _ in LATE]
    loss, grad_x, G, parts = _local_step(x[0], ada, W, late_shards, loss_target[0])
    loss = lax.psum(loss, ("x", "y", "c"))

    small, slayout = _pack([G[n].reshape(-1) for n in REPLICATED])
    sparts, dada_all = _exchange([small, G["b_ada"].reshape(N_DEV, 1, -1)], [True, False], "gather_small_grads")
    parts["w_ada"] = _ada_wgrad(c_all.T, dada_all.reshape(N_DEV, -1))[None]

    rest = [(n, axis) for n, axis in SHARDED if n not in parts]
    parts.update(zip([n for n, _ in rest], _exchange([_owner_blocks(G[n], axis) for n, axis in rest], False, "scatter_grads")))
    out = {}
    for n, p in parts.items():
        res = _sum_adam(p, mat(w_shard, n), mat(m_shard, n), mat(v_shard, n), "adam_" + n)
        if n in TRANSPOSED:
            res = [jnp.swapaxes(a, 0, 1) for a in res]
        for kind, a in zip(("grad", "delta", "new_m", "new_v"), res):
            out[kind, n] = a[None]

    sw, _ = _pack([w_shard[n].reshape(-1) for n in REPLICATED])
    sm, _ = _pack([m_shard[n].reshape(-1) for n in REPLICATED])
    sv, _ = _pack([v_shard[n].reshape(-1) for n in REPLICATED])
    res = _sum_adam(sparts, sw, sm, sv, "adam_replicated")
    for kind, buf in zip(("grad", "delta", "new_m", "new_v"), res):
        for n, a in zip(REPLICATED, _unpack(buf, slayout)):
            out[kind, n] = a.reshape(w_shard[n].shape)

    return (loss, grad_x[None], *[out[kind, n] for kind in ("grad", "delta", "new_m", "new_v") for n in WEIGHTS])
```

```python
import functools
import math

import jax
import jax.numpy as jnp
from jax import lax
from jax.experimental import pallas as pl
from jax.experimental.pallas import tpu as pltpu

F32 = jnp.float32
BF16 = jnp.bfloat16

D = 1024
HEAD = 64
ATT_PATTERNS = ((128, 1), (512, 4), (2048, 16))
ATT_HEADS = 8
ATT_W = ATT_HEADS * HEAD
ATT_IN = 3 * 3 * ATT_W
QBLK = 128
N_HEADS = D // HEAD
LORA_W, LORA_A, LORA_G = 64, 64, 160
RWKV_IN = 3 * D + LORA_W + LORA_A + LORA_G
N_IN = ATT_IN + RWKV_IN + 2 * D
D_FF = 2816
RMS_EPS = 1e-6
GN_EPS = 64e-5
N_DEV = 8
LANES = 128
SUBLANES = 8

C_R, C_K, C_V, C_GA, C_GR = 0, 1024, 2048, 3072, 4096
C_ATT = 5120
C_LORA = C_ATT + ATT_IN
LORA_PAD = 512
G_PAD = 256
N_PAD = C_LORA + LORA_PAD

ADAM_LR, ADAM_B1, ADAM_B2, ADAM_EPS, ADAM_WD, ADAM_STEP = 0.001, 0.9, 0.999, 1e-08, 0.01, 10

VMEM_LIMIT = 56 * 1024 * 1024

_MESH = pl.DeviceIdType.MESH


def _cparams(sem):
    return pltpu.CompilerParams(dimension_semantics=sem, vmem_limit_bytes=VMEM_LIMIT)


def _tile(dim, pref):
    if dim <= pref:
        return dim
    best = None
    for t in range(LANES, pref + 1, LANES):
        if dim % t == 0:
            best = t
    assert best is not None, dim
    return best


MM_TILES = {"nn": (1024, 1408, 1408), "nt": (1024, 2048, 1408), "tn": (1408, 1408, 1024)}


def _mm(a, b, mode, out_dtype, name, scatter=()):
    if mode == "nn":
        (M, K), (K2, N) = a.shape, b.shape
    elif mode == "nt":
        (M, K), (N, K2) = a.shape, b.shape
    else:
        (K, M), (K2, N) = a.shape, b.shape
    assert K == K2, (a.shape, b.shape, mode)
    tm, tn, tk = (_tile(dim, pref) for dim, pref in zip((M, N, K), MM_TILES[mode]))
    nk = K // tk
    grid = (M // tm, N // tn, nk)
    n_x = len(scatter)
    dims = {"nn": (((1,), (0,)), ((), ())), "nt": (((1,), (1,)), ((), ())), "tn": (((0,), (0,)), ((), ()))}[mode]

    def body(*refs):
        a_ref, b_ref = refs[:2]
        o_ref, acc_ref = refs[2 + n_x], refs[3 + 2 * n_x]
        finish = _hosted_exchange(refs[2:2 + n_x] + refs[3 + n_x:3 + 2 * n_x] + refs[4 + 2 * n_x:], n_x, False, grid)
        k = pl.program_id(2)
        part = lax.dot_general(a_ref[...].astype(BF16), b_ref[...].astype(BF16), dims,
                               preferred_element_type=F32)
        if nk == 1:
            o_ref[...] = part.astype(o_ref.dtype)
        else:
            @pl.when(k == 0)
            def _():
                acc_ref[...] = part

            @pl.when(jnp.logical_and(k > 0, k < nk - 1))
            def _():
                acc_ref[...] += part

            @pl.when(k == nk - 1)
            def _():
                o_ref[...] = (acc_ref[...] + part).astype(o_ref.dtype)
        finish()

    a_spec = pl.BlockSpec((tk, tm), lambda i, j, k: (k, i)) if mode == "tn" else pl.BlockSpec((tm, tk), lambda i, j, k: (i, k))
    b_spec = pl.BlockSpec((tn, tk), lambda i, j, k: (j, k)) if mode == "nt" else pl.BlockSpec((tk, tn), lambda i, j, k: (k, j))
    any_spec = pl.BlockSpec(memory_space=pl.ANY)
    outs = pl.pallas_call(
        body, name=name, grid=grid,
        in_specs=[a_spec, b_spec] + [any_spec] * n_x,
        out_specs=[pl.BlockSpec((tm, tn), lambda i, j, k: (i, j))] + [any_spec] * n_x,
        out_shape=[jax.ShapeDtypeStruct((M, N), out_dtype)] + _exchange_shapes(scatter, False),
        scratch_shapes=[pltpu.VMEM((tm, tn) if nk > 1 else (SUBLANES, LANES), F32)] + (_exchange_scratch(n_x) if n_x else []),
        compiler_params=_cparams(("arbitrary",) * 3 if n_x else ("parallel", "parallel", "arbitrary")),
    )(a, b, *scatter)
    return (outs[0], outs[1:]) if n_x else outs[0]


def _rows(tm, w, col=0):
    return pl.BlockSpec((tm, w), lambda i: (i, col))


def _full(shape):
    return pl.BlockSpec(shape, lambda i: (0,) * len(shape))


def _prev8(tm, w, col=0):
    return pl.BlockSpec((SUBLANES, w), lambda i: (jnp.maximum(i * (tm // SUBLANES) - 1, 0), col))


def _next8(tm, w, n_rows, col=0):
    last = n_rows // SUBLANES - 1
    return pl.BlockSpec((SUBLANES, w), lambda i: (jnp.minimum((i + 1) * (tm // SUBLANES), last), col))


def _shift_down(x, halo, k, first):
    rolled = pltpu.roll(x, k, 0)
    row = lax.broadcasted_iota(jnp.int32, x.shape, 0)
    out = rolled
    for j in range(k):
        h = jnp.where(first, 0.0, halo[SUBLANES - k + j:SUBLANES - k + j + 1, :])
        out = jnp.where(row == j, h, out)
    return out


def _shift_up(x, halo, k, last):
    n = x.shape[0]
    rolled = pltpu.roll(x, n - k, 0)
    row = lax.broadcasted_iota(jnp.int32, x.shape, 0)
    out = rolled
    for j in range(k):
        h = jnp.where(last, 0.0, halo[j:j + 1, :])
        out = jnp.where(row == n - k + j, h, out)
    return out


def _acc(ref, val, first):
    @pl.when(first)
    def _():
        ref[...] = val

    @pl.when(jnp.logical_not(first))
    def _():
        ref[...] += val


def _colsum(x):
    return jnp.sum(x, axis=0, keepdims=True)


def _norm_fwd(x, mo, gt, nw, sc, sh, name, tm=256):
    S = x.shape[0]
    has_res = mo is not None

    def body(*refs):
        if has_res:
            x_ref, mo_ref, gt_ref, nw_ref, sc_ref, sh_ref, x2_ref, h_ref, rs_ref = refs
            x2 = x_ref[...] + gt_ref[...] * mo_ref[...]
            x2_ref[...] = x2
        else:
            x_ref, nw_ref, sc_ref, sh_ref, h_ref, rs_ref = refs
            x2 = x_ref[...]
        rstd = lax.rsqrt(jnp.mean(x2 * x2, axis=-1, keepdims=True) + RMS_EPS)
        rs_ref[...] = rstd
        h_ref[...] = ((x2 * rstd * nw_ref[...]) * (1.0 + sc_ref[...]) + sh_ref[...]).astype(BF16)

    vec = _full((1, D))
    ins = [x, mo, gt, nw, sc, sh] if has_res else [x, nw, sc, sh]
    in_specs = [_rows(tm, D), _rows(tm, D), vec, vec, vec, vec] if has_res else [_rows(tm, D), vec, vec, vec]
    outs = [jax.ShapeDtypeStruct((S, D), BF16), jax.ShapeDtypeStruct((S, 1), F32)]
    out_specs = [_rows(tm, D), _rows(tm, 1)]
    if has_res:
        outs = [jax.ShapeDtypeStruct((S, D), F32)] + outs
        out_specs = [_rows(tm, D)] + out_specs
    return pl.pallas_call(body, name=name, grid=(S // tm,), in_specs=in_specs, out_specs=out_specs,
                          out_shape=outs, compiler_params=_cparams(("parallel",)))(*ins)


def _norm_bwd(dh, xin, rstd, nw, sc, dres, mo, gt, name, tm=256):
    S = xin.shape[0]
    has_res = mo is not None

    def body(*refs):
        if has_res:
            dh_ref, x_ref, rs_ref, nw_ref, sc_ref, dres_ref, mo_ref, gt_ref, dx_ref, dsh_ref, dsc_ref, dnw_ref, dmo_ref, dgt_ref = refs
        else:
            dh_ref, x_ref, rs_ref, nw_ref, sc_ref, dres_ref, dx_ref, dsh_ref, dsc_ref, dnw_ref = refs
        first = pl.program_id(0) == 0
        dh = dh_ref[...]
        rstd = rs_ref[...]
        n = x_ref[...] * rstd
        w = nw_ref[...]
        _acc(dsh_ref, _colsum(dh), first)
        _acc(dsc_ref, _colsum(dh * (n * w)), first)
        dnw = dh * (1.0 + sc_ref[...])
        _acc(dnw_ref, _colsum(dnw * n), first)
        dn = dnw * w
        dx = dres_ref[...] + rstd * (dn - n * jnp.mean(dn * n, axis=-1, keepdims=True))
        dx_ref[...] = dx
        if has_res:
            dmo_ref[...] = (dx * gt_ref[...]).astype(BF16)
            _acc(dgt_ref, _colsum(dx * mo_ref[...]), first)

    vec = _full((1, D))
    vshape = jax.ShapeDtypeStruct((1, D), F32)
    ins = [dh, xin, rstd, nw, sc, dres] + ([mo, gt] if has_res else [])
    in_specs = [_rows(tm, D), _rows(tm, D), _rows(tm, 1), vec, vec, _rows(tm, D)] + ([_rows(tm, D), vec] if has_res else [])
    outs = [jax.ShapeDtypeStruct((S, D), F32), vshape, vshape, vshape]
    out_specs = [_rows(tm, D), vec, vec, vec]
    if has_res:
        outs += [jax.ShapeDtypeStruct((S, D), BF16), vshape]
        out_specs += [_rows(tm, D), vec]
    return pl.pallas_call(body, name=name, grid=(S // tm,), in_specs=in_specs, out_specs=out_specs,
                          out_shape=outs, compiler_params=_cparams(("arbitrary",)))(*ins)


def _final(x2, f, gt2, nfw, target, tm=256):
    S = x2.shape[0]

    def body(x2_ref, f_ref, gt_ref, w_ref, t_ref, loss_ref, dx_ref, df_ref, dgt_ref, dw_ref):
        first = pl.program_id(0) == 0
        f = f_ref[...]
        gt = gt_ref[...]
        w = w_ref[...]
        x3 = x2_ref[...] + gt * f
        rstd = lax.rsqrt(jnp.mean(x3 * x3, axis=-1, keepdims=True) + RMS_EPS)
        n = x3 * rstd
        e = n * w - t_ref[...]
        part = 0.5 * jnp.sum(jnp.mean(e * e, axis=-1, keepdims=True), axis=0, keepdims=True)
        _acc(loss_ref, jnp.broadcast_to(part, (SUBLANES, LANES)), first)
        dy = e * (1.0 / D)
        _acc(dw_ref, _colsum(dy * n), first)
        dn = dy * w
        dx = rstd * (dn - n * jnp.mean(dn * n, axis=-1, keepdims=True))
        dx_ref[...] = dx
        df_ref[...] = (dx * gt).astype(BF16)
        _acc(dgt_ref, _colsum(dx * f), first)

    vec = _full((1, D))
    vshape = jax.ShapeDtypeStruct((1, D), F32)
    return pl.pallas_call(
        body, name="final_loss", grid=(S // tm,),
        in_specs=[_rows(tm, D), _rows(tm, D), vec, vec, _rows(tm, D)],
        out_specs=[_full((SUBLANES, LANES)), _rows(tm, D), _rows(tm, D), vec, vec],
        out_shape=[jax.ShapeDtypeStruct((SUBLANES, LANES), F32), jax.ShapeDtypeStruct((S, D), F32),
                   jax.ShapeDtypeStruct((S, D), BF16), vshape, vshape],
        compiler_params=_cparams(("arbitrary",)))(x2, f, gt2, nfw, target)


def _gate_fwd(P, bga, bgr, y_att, y_rwkv, tm=256):
    S = P.shape[0]

    def body(pa_ref, pr_ref, ba_ref, br_ref, ya_ref, yr_ref, mix_ref):
        ga = jax.nn.sigmoid(pa_ref[...] + ba_ref[...])
        gr = jax.nn.sigmoid(pr_ref[...] + br_ref[...])
        mix_ref[...] = (ga * ya_ref[...] + gr * yr_ref[...]).astype(BF16)

    vec = _full((1, D))
    return pl.pallas_call(
        body, name="gate_fwd", grid=(S // tm,),
        in_specs=[_rows(tm, D, C_GA // D), _rows(tm, D, C_GR // D), vec, vec, _rows(tm, D), _rows(tm, D)],
        out_specs=_rows(tm, D), out_shape=jax.ShapeDtypeStruct((S, D), BF16),
        compiler_params=_cparams(("parallel",)))(P, P, bga, bgr, y_att, y_rwkv)


def _gate_bwd(dmix, P, bga, bgr, y_att, y_rwkv, tm=256):
    S = P.shape[0]

    def body(dm_ref, pa_ref, pr_ref, ba_ref, br_ref, ya_ref, yr_ref, dya_ref, dyr_ref, dpa_ref, dpr_ref, dba_ref, dbr_ref):
        first = pl.program_id(0) == 0
        dm = dm_ref[...]
        ga = jax.nn.sigmoid(pa_ref[...] + ba_ref[...])
        gr = jax.nn.sigmoid(pr_ref[...] + br_ref[...])
        dya_ref[...] = (dm * ga).astype(BF16)
        dyr_ref[...] = (dm * gr).astype(BF16)
        dpa = dm * ya_ref[...] * ga * (1.0 - ga)
        dpr = dm * yr_ref[...] * gr * (1.0 - gr)
        dpa_ref[...] = dpa.astype(BF16)
        dpr_ref[...] = dpr.astype(BF16)
        _acc(dba_ref, _colsum(dpa), first)
        _acc(dbr_ref, _colsum(dpr), first)

    vec = _full((1, D))
    row = _rows(tm, D)
    rshape = jax.ShapeDtypeStruct((S, D), BF16)
    vshape = jax.ShapeDtypeStruct((1, D), F32)
    return pl.pallas_call(
        body, name="gate_bwd", grid=(S // tm,),
        in_specs=[row, _rows(tm, D, C_GA // D), _rows(tm, D, C_GR // D), vec, vec, row, row],
        out_specs=[row, row, row, row, vec, vec],
        out_shape=[rshape, rshape, rshape, rshape, vshape, vshape],
        compiler_params=_cparams(("arbitrary",)))(dmix, P, P, bga, bgr, y_att, y_rwkv)


CONV_TN = D_FF // 2


def _conv_fwd(u, conv_w8, conv_b, tm=256, tn=CONV_TN):
    S = u.shape[0]
    nj = D_FF // tn

    def conv(u_ref, h_ref, w_ref, b_ref, first):
        u = u_ref[...]
        h = h_ref[...]
        w = w_ref[...]
        return b_ref[...] + w[0:1] * _shift_down(u, h, 2, first) + w[1:2] * _shift_down(u, h, 1, first) + w[2:3] * u

    def body(ug_ref, hg_ref, uv_ref, hv_ref, wg_ref, wv_ref, bg_ref, bv_ref, act_ref):
        first = pl.program_id(0) == 0
        g = conv(ug_ref, hg_ref, wg_ref, bg_ref, first)
        v = conv(uv_ref, hv_ref, wv_ref, bv_ref, first)
        act_ref[...] = (g * jax.nn.sigmoid(g) * v).astype(BF16)

    blk = lambda off: pl.BlockSpec((tm, tn), lambda i, j: (i, j + off))
    halo = lambda off: pl.BlockSpec((SUBLANES, tn), lambda i, j: (jnp.maximum(i * (tm // SUBLANES) - 1, 0), j + off))
    wsp = lambda off: pl.BlockSpec((SUBLANES, tn), lambda i, j: (0, j + off))
    bsp = lambda off: pl.BlockSpec((1, tn), lambda i, j: (0, j + off))
    return pl.pallas_call(
        body, name="conv_fwd", grid=(S // tm, nj),
        in_specs=[blk(0), halo(0), blk(nj), halo(nj), wsp(0), wsp(nj), bsp(0), bsp(nj)],
        out_specs=pl.BlockSpec((tm, tn), lambda i, j: (i, j)),
        out_shape=jax.ShapeDtypeStruct((S, D_FF), BF16),
        compiler_params=_cparams(("parallel", "parallel")))(u, u, u, u, conv_w8, conv_w8, conv_b, conv_b)


def _conv_bwd_a(dact, u, conv_w8, conv_b, tm=256, tn=CONV_TN):
    S = u.shape[0]
    nj = D_FF // tn

    def half(u_ref, h_ref, w_ref, b_ref, first):
        u = u_ref[...]
        h = h_ref[...]
        w = w_ref[...]
        u2, u1 = _shift_down(u, h, 2, first), _shift_down(u, h, 1, first)
        return b_ref[...] + w[0:1] * u2 + w[1:2] * u1 + w[2:3] * u, (u2, u1, u)

    def wgrad(d, taps):
        z = jnp.zeros((SUBLANES - 3, d.shape[1]), F32)
        return jnp.concatenate([_colsum(d * taps[0]), _colsum(d * taps[1]), _colsum(d * taps[2]), z], axis=0)

    def body(da_ref, ug_ref, hg_ref, uv_ref, hv_ref, wg_ref, wv_ref, bg_ref, bv_ref,
             d_ref, dwg_ref, dwv_ref, dbg_ref, dbv_ref):
        first = pl.program_id(1) == 0
        g, tg = half(ug_ref, hg_ref, wg_ref, bg_ref, first)
        v, tv = half(uv_ref, hv_ref, wv_ref, bv_ref, first)
        da = da_ref[...].astype(F32)
        sg = jax.nn.sigmoid(g)
        dg = da * v * (sg * (1.0 + g * (1.0 - sg)))
        dv = da * (g * sg)
        d_ref[0] = dg
        d_ref[1] = dv
        _acc(dwg_ref, wgrad(dg, tg), first)
        _acc(dwv_ref, wgrad(dv, tv), first)
        _acc(dbg_ref, _colsum(dg), first)
        _acc(dbv_ref, _colsum(dv), first)

    blk = lambda off: pl.BlockSpec((tm, tn), lambda j, i: (i, j + off))
    halo = lambda off: pl.BlockSpec((SUBLANES, tn), lambda j, i: (jnp.maximum(i * (tm // SUBLANES) - 1, 0), j + off))
    wsp = lambda off: pl.BlockSpec((SUBLANES, tn), lambda j, i: (0, j + off))
    bsp = lambda off: pl.BlockSpec((1, tn), lambda j, i: (0, j + off))
    f = jax.ShapeDtypeStruct
    outs = pl.pallas_call(
        body, name="conv_bwd_a", grid=(nj, S // tm),
        in_specs=[pl.BlockSpec((tm, tn), lambda j, i: (i, j)), blk(0), halo(0), blk(nj), halo(nj), wsp(0), wsp(nj), bsp(0), bsp(nj)],
        out_specs=[pl.BlockSpec((2, tm, tn), lambda j, i: (0, i, j)),
                   pl.BlockSpec((SUBLANES, tn), lambda j, i: (0, j)), pl.BlockSpec((SUBLANES, tn), lambda j, i: (0, j)),
                   pl.BlockSpec((1, tn), lambda j, i: (0, j)), pl.BlockSpec((1, tn), lambda j, i: (0, j))],
        out_shape=[f((2, S, D_FF), F32), f((SUBLANES, D_FF), F32), f((SUBLANES, D_FF), F32),
                   f((1, D_FF), F32), f((1, D_FF), F32)],
        compiler_params=_cparams(("parallel", "arbitrary")))(dact, u, u, u, u, conv_w8, conv_w8, conv_b, conv_b)
    return outs


def _conv_bwd_b(duc, conv_w8, tm=256, tn=CONV_TN):
    _, S, W = duc.shape
    nj = W // tn
    n_rows = S // tm

    def body(d_ref, h_ref, w_ref, o_ref):
        last = pl.program_id(0) == n_rows - 1
        d = d_ref[...]
        h = h_ref[...]
        w = w_ref[...]
        o_ref[...] = (w[2:3] * d + w[1:2] * _shift_up(d, h, 1, last) + w[0:1] * _shift_up(d, h, 2, last)).astype(BF16)

    last_tile = S // SUBLANES - 1
    return pl.pallas_call(
        body, name="conv_bwd_b", grid=(n_rows, 2 * nj),
        in_specs=[pl.BlockSpec((None, tm, tn), lambda i, j: (j // nj, i, j % nj)),
                  pl.BlockSpec((None, SUBLANES, tn), lambda i, j: (j // nj, jnp.minimum((i + 1) * (tm // SUBLANES), last_tile), j % nj)),
                  pl.BlockSpec((SUBLANES, tn), lambda i, j: (0, j))],
        out_specs=pl.BlockSpec((tm, tn), lambda i, j: (i, j)),
        out_shape=jax.ShapeDtypeStruct((S, 2 * W), BF16),
        compiler_params=_cparams(("parallel", "parallel")))(duc, duc, conv_w8)


ATT_SCALE = HEAD ** -0.5
NEG = -1e30
ATT_PAIRS = ATT_HEADS // 2


def _att_rows(n, d, S):
    per = S // (QBLK * d)
    r, m = n // per, n % per
    cur = pl.ds(m * (QBLK * d) + r, QBLK, stride=d)
    prv = pl.ds(jnp.maximum(m - 1, 0) * (QBLK * d) + r, QBLK, stride=d)
    return cur, prv, m > 0


def _att_slab(g, j):
    return (C_ATT + g * 3 * ATT_W + j * ATT_W) // LANES


def _heads(x):
    return x[:, 0:HEAD], x[:, HEAD:2 * HEAD]


ATT_NB = 4


def _stack(tiles):
    return jnp.concatenate([t[None] for t in tiles], axis=0)


def _att_operands(i, d, S, *sources):
    rows, has = [], []
    tiles = [[] for _ in sources]
    for bb in range(ATT_NB):
        cur, prv, has_prev = _att_rows(i * ATT_NB + bb, d, S)
        rows.append((cur, prv))
        has.append(has_prev)
        for t, (ref, use_cur) in zip(tiles, sources):
            t += _heads(ref[cur if use_cur else prv, :].astype(BF16))
    return rows, has, [_stack(t) for t in tiles]


def _att_mask(s_c, s_p, has_prev):
    qi = lax.broadcasted_iota(jnp.int32, (QBLK, QBLK), 0)
    kj = lax.broadcasted_iota(jnp.int32, (QBLK, QBLK), 1)
    s_c = jnp.where(kj <= qi, s_c * ATT_SCALE, NEG)
    s_p = jnp.where(jnp.logical_and(kj >= qi, has_prev), s_p * ATT_SCALE, NEG)
    return s_c, s_p


def _att_fwd(P, g):
    S = P.shape[0]
    d = ATT_PATTERNS[g][1]

    def body(q_ref, k_ref, v_ref, o_ref, l_ref):
        def group(i, carry):
            rows, has, (q, kc, kp, vc, vp) = _att_operands(i, d, S, (q_ref, True), (k_ref, True), (k_ref, False),
                                                           (v_ref, True), (v_ref, False))
            s_c_all, s_p_all = _dot16(q, kc, "nt"), _dot16(q, kp, "nt")
            p_c, p_p, den, lse = [], [], [], []
            for e in range(2 * ATT_NB):
                s_c, s_p = _att_mask(s_c_all[e], s_p_all[e], has[e // 2])
                m = jnp.maximum(jnp.max(s_c, axis=1, keepdims=True), jnp.max(s_p, axis=1, keepdims=True))
                pc, pp = jnp.exp(s_c - m), jnp.exp(s_p - m)
                den.append(jnp.sum(pc, axis=1, keepdims=True) + jnp.sum(pp, axis=1, keepdims=True))
                lse.append(jnp.broadcast_to(m + jnp.log(den[e]), (QBLK, HEAD)))
                p_c.append(pc)
                p_p.append(pp)
            num = _dot16(_stack(p_c), vc, "nn") + _dot16(_stack(p_p), vp, "nn")
            for bb, (cur, _) in enumerate(rows):
                o_ref[cur, :] = jnp.concatenate([num[2 * bb] / den[2 * bb], num[2 * bb + 1] / den[2 * bb + 1]], axis=1)
                l_ref[cur, :] = jnp.concatenate(lse[2 * bb:2 * bb + 2], axis=1)
            return carry

        lax.fori_loop(0, S // QBLK // ATT_NB, group, 0)

    slab = lambda j: pl.BlockSpec((S, LANES), lambda i: (0, _att_slab(g, j) + i))
    out = pl.BlockSpec((S, LANES), lambda i: (0, i))
    shp = jax.ShapeDtypeStruct((S, ATT_W), F32)
    return pl.pallas_call(body, name=f"att_fwd_g{g}", grid=(ATT_PAIRS,), in_specs=[slab(0), slab(1), slab(2)],
                          out_specs=[out, out], out_shape=[shp, shp], compiler_params=_cparams(("parallel",)))(P, P, P)


def _att_bwd(P, o, l, do, dl, g):
    S = P.shape[0]
    d = ATT_PATTERNS[g][1]

    def body(q_ref, k_ref, v_ref, o_ref, l_ref, do_ref, dl_ref, dq_ref, dk_ref, dv_ref, dq_acc, dk_acc, dv_acc):
        dk_acc[...] = jnp.zeros_like(dk_acc)
        dv_acc[...] = jnp.zeros_like(dv_acc)

        def group(i, carry):
            rows, has, (q, kc, kp, vc, vp, dob) = _att_operands(
                i, d, S, (q_ref, True), (k_ref, True), (k_ref, False), (v_ref, True), (v_ref, False), (do_ref, True))
            s_c_all, s_p_all = _dot16(q, kc, "nt"), _dot16(q, kp, "nt")
            dp_c_all, dp_p_all = _dot16(dob, vc, "nt"), _dot16(dob, vp, "nt")
            p_c, p_p, ds_c, ds_p = [], [], [], []
            for bb, (cur, _) in enumerate(rows):
                dd2 = do_ref[cur, :] * o_ref[cur, :] - dl_ref[cur, :]
                for h, (dd, lse) in enumerate(zip(_heads(dd2), _heads(l_ref[cur, :]))):
                    e = 2 * bb + h
                    s_c, s_p = _att_mask(s_c_all[e], s_p_all[e], has[bb])
                    pc, pp = jnp.exp(s_c - lse[:, 0:1]), jnp.exp(s_p - lse[:, 0:1])
                    delta = jnp.sum(dd, axis=1, keepdims=True)
                    p_c.append(pc)
                    p_p.append(pp)
                    ds_c.append(pc * (dp_c_all[e] - delta) * ATT_SCALE)
                    ds_p.append(pp * (dp_p_all[e] - delta) * ATT_SCALE)
            p_c, p_p, ds_c, ds_p = map(_stack, (p_c, p_p, ds_c, ds_p))
            dq = _dot16(ds_c, kc, "nn") + _dot16(ds_p, kp, "nn")
            dk_c, dk_p = _dot16(ds_c, q, "tn"), _dot16(ds_p, q, "tn")
            dv_c, dv_p = _dot16(p_c, dob, "tn"), _dot16(p_p, dob, "tn")
            pair = lambda x, bb: jnp.concatenate([x[2 * bb], x[2 * bb + 1]], axis=1)
            for bb, (cur, prv) in enumerate(rows):
                dq_acc[cur, :] = pair(dq, bb)
                dk_acc[cur, :] += pair(dk_c, bb)
                dv_acc[cur, :] += pair(dv_c, bb)
                dk_acc[prv, :] += pair(dk_p, bb)
                dv_acc[prv, :] += pair(dv_p, bb)
            return carry

        lax.fori_loop(0, S // QBLK // ATT_NB, group, 0)
        dq_ref[...] = dq_acc[...].astype(BF16)
        dk_ref[...] = dk_acc[...].astype(BF16)
        dv_ref[...] = dv_acc[...].astype(BF16)

    slab = lambda j: pl.BlockSpec((S, LANES), lambda i: (0, _att_slab(g, j) + i))
    blk128 = pl.BlockSpec((S, LANES), lambda i: (0, i))
    shp = jax.ShapeDtypeStruct((S, ATT_W), BF16)
    return pl.pallas_call(body, name=f"att_bwd_g{g}", grid=(ATT_PAIRS,),
                          in_specs=[slab(0), slab(1), slab(2)] + [blk128] * 4, out_specs=[blk128] * 3, out_shape=[shp] * 3,
                          scratch_shapes=[pltpu.VMEM((S, LANES), F32)] * 3,
                          compiler_params=_cparams(("parallel",)))(P, P, P, o, l, do, dl)


def _att_weights(l_refs):
    l0, l1, l2 = [r[...] for r in l_refs]
    m = jnp.maximum(jnp.maximum(l0, l1), l2)
    e = (jnp.exp(l0 - m), jnp.exp(l1 - m), jnp.exp(l2 - m))
    inv = 1.0 / (e[0] + e[1] + e[2])
    return [x * inv for x in e]


def _att_combine_fwd(os, ls, tm=512):
    S = os[0].shape[0]

    def body(o0, o1, o2, l0, l1, l2, a_ref):
        w = _att_weights((l0, l1, l2))
        a_ref[...] = (w[0] * o0[...] + w[1] * o1[...] + w[2] * o2[...]).astype(BF16)

    row = _rows(tm, ATT_W)
    return pl.pallas_call(body, name="att_combine_fwd", grid=(S // tm,), in_specs=[row] * 6, out_specs=row,
                          out_shape=jax.ShapeDtypeStruct((S, ATT_W), BF16),
                          compiler_params=_cparams(("parallel",)))(*os, *ls)


def _att_combine_bwd(da, os, ls, tm=512):
    S = da.shape[0]

    def body(da_ref, o0, o1, o2, l0, l1, l2, *out_refs):
        da = da_ref[...]
        w = _att_weights((l0, l1, l2))
        dw = (da * o0[...], da * o1[...], da * o2[...])
        mean = w[0] * dw[0] + w[1] * dw[1] + w[2] * dw[2]
        for g in range(3):
            out_refs[g][...] = w[g] * da
            out_refs[3 + g][...] = w[g] * (dw[g] - mean)

    row = _rows(tm, ATT_W)
    shp = jax.ShapeDtypeStruct((S, ATT_W), F32)
    return pl.pallas_call(body, name="att_combine_bwd", grid=(S // tm,), in_specs=[row] * 7, out_specs=[row] * 6,
                          out_shape=[shp] * 6, compiler_params=_cparams(("parallel",)))(da, *os, *ls)


@jax.custom_vjp
def _bdot(a, b):
    return jnp.dot(a.astype(BF16), b.astype(BF16), preferred_element_type=F32)


def _bdot_fwd(a, b):
    return _bdot(a, b), (a, b)


def _bdot_bwd(res, ct):
    a, b = res
    ct16 = ct.astype(BF16)
    da = lax.dot_general(ct16, b.astype(BF16), (((1,), (1,)), ((), ())), preferred_element_type=F32)
    db = lax.dot_general(a.astype(BF16), ct16, (((0,), (0,)), ((), ())), preferred_element_type=F32)
    return da, db


_bdot.defvjp(_bdot_fwd, _bdot_bwd)


def _two_piece_dot(x, m):
    hi = x.astype(BF16)
    lo = (x - hi.astype(F32)).astype(BF16)
    return jnp.dot(hi, m, preferred_element_type=F32) + jnp.dot(lo, m, preferred_element_type=F32)


def _head_sum_impl(x):
    sel = (lax.broadcasted_iota(jnp.int32, (D, LANES), 0) // HEAD == lax.broadcasted_iota(jnp.int32, (D, LANES), 1)).astype(BF16)
    sel_t = (lax.broadcasted_iota(jnp.int32, (LANES, D), 1) // HEAD == lax.broadcasted_iota(jnp.int32, (LANES, D), 0)).astype(BF16)
    return _two_piece_dot(_two_piece_dot(x, sel), sel_t)


@jax.custom_vjp
def _head_sum(x):
    return _head_sum_impl(x)


_head_sum.defvjp(lambda x: (_head_sum_impl(x), None), lambda _, ct: (_head_sum_impl(ct),))


def _softplus(z):
    return jnp.maximum(z, 0.0) + jnp.log(1.0 + jnp.exp(-jnp.abs(z)))


def _rwkv_prep_fn(zr, zrp, zk, zkp, zv, zvp, zl, zlp, mu_r, mu_k, mu_v, mu_l, w0, a0, k_k, k_a, w2, a2, g2p):
    r = zr + (zrp - zr) * mu_r
    k = zk + (zkp - zk) * mu_k
    v = zv + (zvp - zv) * mu_v
    lo = zl + (zlp - zl) * mu_l
    w_low, a_low, g_low = lo[:, 0:LORA_W], lo[:, LORA_W:LORA_W + LORA_A], lo[:, LANES:LANES + G_PAD]
    w_log = -_softplus(-(w0 + _bdot(jnp.tanh(w_low), w2))) - 0.5
    decay = -jnp.exp(w_log)
    a = jax.nn.sigmoid(a0 + _bdot(a_low, a2))
    g = _bdot(jax.nn.sigmoid(g_low), g2p)
    kmod = k * (1.0 + (a - 1.0) * k_a)
    kk = k * k_k
    kk = kk / jnp.maximum(jnp.sqrt(_head_sum(kk * kk)), 1e-12)
    return r, decay, kmod, v, -kk, kk * a, g


def _rwkv_prep_specs(tm):
    vec = _full((1, D))
    slabs = []
    for col in (C_R // D, C_K // D, C_V // D):
        slabs += [_rows(tm, D, col), _prev8(tm, D, col)]
    slabs += [_rows(tm, LORA_PAD, C_LORA // LORA_PAD), _prev8(tm, LORA_PAD, C_LORA // LORA_PAD)]
    params = [vec, vec, vec, _full((1, LORA_PAD)), vec, vec, vec, vec,
              _full((LORA_W, D)), _full((LORA_A, D)), _full((G_PAD, D))]
    return slabs, params


def _prep_inputs(refs, first):
    vals = []
    for s in range(4):
        z = refs[2 * s][...]
        vals += [z, _shift_down(z, refs[2 * s + 1][...], 1, first)]
    return vals + [r[...] for r in refs[8:19]]


def _rwkv_prep(P, params, tm=256):
    S = P.shape[0]
    slabs, pspecs = _rwkv_prep_specs(tm)

    def body(*refs):
        outs = _rwkv_prep_fn(*_prep_inputs(refs, pl.program_id(0) == 0))
        for o_ref, val in zip(refs[19:], outs):
            o_ref[...] = val

    shp = jax.ShapeDtypeStruct((S, D), F32)
    return pl.pallas_call(body, name="rwkv_prep", grid=(S // tm,), in_specs=slabs + pspecs,
                          out_specs=[_rows(tm, D)] * 7, out_shape=[shp] * 7,
                          compiler_params=_cparams(("parallel",)))(*([P] * 8), *params)


def _rwkv_prep_bwd(P, params, cts_a, cts_b, tm=128):
    S = P.shape[0]
    slabs, pspecs = _rwkv_prep_specs(tm)
    has_b = [c is not None for c in cts_b]
    n_ct = 7 + sum(has_b)

    def body(*refs):
        first = pl.program_id(0) == 0
        ins = _prep_inputs(refs, first)
        ct_refs = refs[19:19 + n_ct]
        out_refs = refs[19 + n_ct:]
        cts, pos = [], 7
        for i in range(7):
            c = ct_refs[i][...]
            if has_b[i]:
                c = c + ct_refs[pos][...]
                pos += 1
            cts.append(c)
        _, vjp = jax.vjp(_rwkv_prep_fn, *ins)
        grads = vjp(tuple(cts))
        for s in range(4):
            out_refs[s][...] = grads[2 * s]
            out_refs[4 + s][...] = grads[2 * s + 1]
        for i in range(11):
            _acc(out_refs[8 + i], grads[8 + i], first)

    ct_in = list(cts_a) + [c for c in cts_b if c is not None]
    row, lrow = _rows(tm, D), _rows(tm, LORA_PAD)
    f = jax.ShapeDtypeStruct
    zshapes = [f((S, D), F32)] * 3 + [f((S, LORA_PAD), F32)]
    pshapes = [f((1, D), F32)] * 3 + [f((1, LORA_PAD), F32)] + [f((1, D), F32)] * 4 + [f((LORA_W, D), F32), f((LORA_A, D), F32), f((G_PAD, D), F32)]
    return pl.pallas_call(
        body, name="rwkv_prep_bwd", grid=(S // tm,),
        in_specs=slabs + pspecs + [row] * n_ct,
        out_specs=[row, row, row, lrow] * 2 + pspecs,
        out_shape=zshapes * 2 + pshapes,
        compiler_params=_cparams(("arbitrary",)))(*([P] * 8), *params, *ct_in)


def _shift_add(a, b, tm=256):
    S, W = a.shape

    def body(a_ref, b_ref, h_ref, o_ref):
        last = pl.program_id(0) == pl.num_programs(0) - 1
        o_ref[...] = (a_ref[...] + _shift_up(b_ref[...], h_ref[...], 1, last)).astype(BF16)

    return pl.pallas_call(body, name="shift_add", grid=(S // tm,),
                          in_specs=[_rows(tm, W), _rows(tm, W), _next8(tm, W, S)],
                          out_specs=_rows(tm, W), out_shape=jax.ShapeDtypeStruct((S, W), BF16),
                          compiler_params=_cparams(("parallel",)))(a, b, b)


def _rwkv_post_fn(y, r, kmod, v, g, lnx_w, lnx_b, r_k):
    mean = _head_sum(y) * (1.0 / HEAD)
    yc = y - mean
    var = _head_sum(yc * yc) * (1.0 / HEAD)
    yn = yc * lax.rsqrt(var + GN_EPS) * lnx_w + lnx_b
    bonus = _head_sum(r * kmod * r_k) * v
    return (yn + bonus) * g


def _rwkv_post(y, r, kmod, v, g, lnx_w, lnx_b, r_k, tm=256):
    S = y.shape[0]

    def body(y_ref, r_ref, k_ref, v_ref, g_ref, w_ref, b_ref, rk_ref, o_ref):
        o_ref[...] = _rwkv_post_fn(y_ref[...], r_ref[...], k_ref[...], v_ref[...], g_ref[...],
                                   w_ref[...], b_ref[...], rk_ref[...]).astype(BF16)

    row, vec = _rows(tm, D), _full((1, D))
    return pl.pallas_call(body, name="rwkv_post", grid=(S // tm,), in_specs=[row] * 5 + [vec] * 3, out_specs=row,
                          out_shape=jax.ShapeDtypeStruct((S, D), BF16),
                          compiler_params=_cparams(("parallel",)))(y, r, kmod, v, g, lnx_w, lnx_b, r_k)


def _rwkv_post_bwd(drw, y, r, kmod, v, g, lnx_w, lnx_b, r_k, tm=256):
    S = y.shape[0]

    def body(d_ref, y_ref, r_ref, k_ref, v_ref, g_ref, w_ref, b_ref, rk_ref, *out_refs):
        first = pl.program_id(0) == 0
        _, vjp = jax.vjp(_rwkv_post_fn, y_ref[...], r_ref[...], k_ref[...], v_ref[...], g_ref[...],
                         w_ref[...], b_ref[...], rk_ref[...])
        grads = vjp(d_ref[...])
        for i in range(5):
            out_refs[i][...] = grads[i]
        for i in range(5, 8):
            _acc(out_refs[i], grads[i], first)

    row, vec = _rows(tm, D), _full((1, D))
    f = jax.ShapeDtypeStruct
    return pl.pallas_call(body, name="rwkv_post_bwd", grid=(S // tm,), in_specs=[row] * 6 + [vec] * 3,
                          out_specs=[row] * 5 + [vec] * 3, out_shape=[f((S, D), F32)] * 5 + [f((1, D), F32)] * 3,
                          compiler_params=_cparams(("arbitrary",)))(drw, y, r, kmod, v, g, lnx_w, lnx_b, r_k)


CHUNK = 64
CHUNK_TB = 256
_DOT_DIMS = {"nn": (((2,), (1,)), ((0,), (0,))), "nt": (((2,), (2,)), ((0,), (0,))), "tn": (((1,), (1,)), ((0,), (0,)))}


def _dot16(x, y, mode):
    return lax.dot_general(x.astype(BF16), y.astype(BF16), _DOT_DIMS[mode], preferred_element_type=F32)


@functools.partial(jax.custom_vjp, nondiff_argnums=(2,))
def _mm16(x, y, mode):
    return _dot16(x, y, mode)


def _mm16_fwd(x, y, mode):
    return _dot16(x, y, mode), (x, y)


def _mm16_bwd(mode, res, ct):
    x, y = res
    if mode == "nn":
        return _dot16(ct, y, "nt"), _dot16(x, ct, "tn")
    if mode == "nt":
        return _dot16(ct, y, "nn"), _dot16(ct, x, "tn")
    return _dot16(y, ct, "nt"), _dot16(x, ct, "nn")


_mm16.defvjp(_mm16_fwd, _mm16_bwd)


def _tri_sum(x, upper):
    T = x.shape[0]
    i = lax.broadcasted_iota(jnp.int32, (T, T), 0)
    j = lax.broadcasted_iota(jnp.int32, (T, T), 1)
    tri = ((j >= i) if upper else (i >= j)).astype(BF16)
    out, rest = None, x
    for _ in range(3):
        piece = rest.astype(BF16)
        rest = rest - piece.astype(F32)
        part = jnp.dot(tri, piece, preferred_element_type=F32)
        out = part if out is None else out + part
    return out


@jax.custom_vjp
def _cumsum_rows(x):
    return _tri_sum(x, False)


_cumsum_rows.defvjp(lambda x: (_tri_sum(x, False), None), lambda _, ct: (_tri_sum(ct, True),))


def _rows_to_cols(x):
    H, _, K = x.shape
    eye = (lax.broadcasted_iota(jnp.int32, (H, K, K), 1) == lax.broadcasted_iota(jnp.int32, (H, K, K), 2)).astype(F32)
    out = lax.dot_general(eye, jnp.broadcast_to(x, (H, SUBLANES, K)), _DOT_DIMS["nt"],
                          precision=lax.Precision.HIGHEST, preferred_element_type=F32)
    return out[:, :, 0:1]


def _per_head(x):
    return jnp.concatenate([x[:, h * HEAD:(h + 1) * HEAD][None] for h in range(N_HEADS)], axis=0)


def _chunk_fn(st0, r, lw, k, v, a, b):
    T = r.shape[0]
    cl = _cumsum_rows(lw)
    cl_end = cl[T - 1:T, :]
    inv = jnp.exp(-cl)
    to_end = jnp.exp(cl_end - cl)
    ah, rh, bh, kh, be, ke, v3 = [_per_head(x) for x in
                                  (a * jnp.exp(cl - lw), r * jnp.exp(cl), b * inv, k * inv, b * to_end, k * to_end, v)]
    i = lax.broadcasted_iota(jnp.int32, (N_HEADS, T, T), 1)
    j = lax.broadcasted_iota(jnp.int32, (N_HEADS, T, T), 2)
    a_ab = jnp.where(i > j, _mm16(ah, bh, "nt"), 0.0)
    a_ak = jnp.where(i > j, _mm16(ah, kh, "nt"), 0.0)
    m_rb = jnp.where(i >= j, _mm16(rh, bh, "nt"), 0.0)
    m_rk = jnp.where(i >= j, _mm16(rh, kh, "nt"), 0.0)
    rhs = _mm16(ah, st0, "nn") + _mm16(a_ak, v3, "nn")
    power, solve, n = a_ab, (i == j).astype(F32) + a_ab, 1
    while 2 * n < T:
        power = _mm16(power, power, "nn")
        solve = solve + _mm16(solve, power, "nn")
        n *= 2
    sa = _mm16(solve, rhs, "nn")
    y3 = _mm16(rh, st0, "nn") + _mm16(m_rb, sa, "nn") + _mm16(m_rk, v3, "nn")
    st_end = _rows_to_cols(_per_head(jnp.exp(cl_end))) * st0 + _mm16(be, sa, "tn") + _mm16(ke, v3, "tn")
    return jnp.concatenate([y3[h] for h in range(N_HEADS)], axis=1), st_end


def _hosted_exchange(refs, n, broadcast, grid):
    if n == 0:
        return lambda: None
    start, wait = _exchange_ops(refs[:n], refs[n:2 * n], *refs[2 * n:], broadcast)
    first = functools.reduce(jnp.logical_and, [pl.program_id(a) == 0 for a in range(len(grid))])
    last = functools.reduce(jnp.logical_and, [pl.program_id(a) == g - 1 for a, g in enumerate(grid)])
    pl.when(first)(start)
    return lambda: pl.when(last)(wait)


def _cscan_fwd(r, lw, k, v, a, b, gather=()):
    S = r.shape[0]
    per_blk = CHUNK_TB // CHUNK
    n_x = len(gather)
    nblk = S // CHUNK_TB

    def body(*refs):
        r_ref, lw_ref, k_ref, v_ref, a_ref, b_ref = refs[:6]
        y_ref, ck_ref = refs[6 + n_x:8 + n_x]
        st_ref = refs[8 + 2 * n_x]
        finish = _hosted_exchange(refs[6:6 + n_x] + refs[8 + n_x:8 + 2 * n_x] + refs[9 + 2 * n_x:], n_x, True, (nblk,))

        @pl.when(pl.program_id(0) == 0)
        def _():
            st_ref[...] = jnp.zeros_like(st_ref)

        def chunk(c, carry):
            rows = pl.ds(pl.multiple_of(c * CHUNK, CHUNK), CHUNK)
            st0 = st_ref[...]
            ck_ref[c] = st0
            y, st_end = _chunk_fn(st0, r_ref[rows, :], lw_ref[rows, :], k_ref[rows, :],
                                  v_ref[rows, :], a_ref[rows, :], b_ref[rows, :])
            y_ref[rows, :] = y
            st_ref[...] = st_end
            return carry

        lax.fori_loop(0, per_blk, chunk, 0)
        finish()

    blk = _rows(CHUNK_TB, D)
    any_spec = pl.BlockSpec(memory_space=pl.ANY)
    outs = pl.pallas_call(
        body, name="scan_fwd", grid=(nblk,), in_specs=[blk] * 6 + [any_spec] * n_x,
        out_specs=[blk, pl.BlockSpec((per_blk, N_HEADS, HEAD, HEAD), lambda i: (i, 0, 0, 0))] + [any_spec] * n_x,
        out_shape=[jax.ShapeDtypeStruct((S, D), F32), jax.ShapeDtypeStruct((S // CHUNK, N_HEADS, HEAD, HEAD), F32)]
        + _exchange_shapes(gather, True),
        scratch_shapes=[pltpu.VMEM((N_HEADS, HEAD, HEAD), F32)] + (_exchange_scratch(n_x) if n_x else []),
        compiler_params=_cparams(("arbitrary",)))(r, lw, k, v, a, b, *gather)
    return outs[0], outs[1], outs[2:]


def _cscan_bwd(r, lw, k, v, a, b, ckpt, dy, scatter=()):
    S = r.shape[0]
    per_blk = CHUNK_TB // CHUNK
    nblk = S // CHUNK_TB
    n_x = len(scatter)

    def body(*refs):
        r_ref, lw_ref, k_ref, v_ref, a_ref, b_ref, ck_ref, dy_ref = refs[:8]
        out_refs = refs[8 + n_x:14 + n_x]
        ds_ref = refs[14 + 2 * n_x]
        finish = _hosted_exchange(refs[8:8 + n_x] + refs[14 + n_x:14 + 2 * n_x] + refs[15 + 2 * n_x:], n_x, False, (nblk,))

        @pl.when(pl.program_id(0) == 0)
        def _():
            ds_ref[...] = jnp.zeros_like(ds_ref)

        def chunk(cc, carry):
            c = per_blk - 1 - cc
            rows = pl.ds(pl.multiple_of(c * CHUNK, CHUNK), CHUNK)
            ins = (ck_ref[c], r_ref[rows, :], lw_ref[rows, :], k_ref[rows, :], v_ref[rows, :], a_ref[rows, :], b_ref[rows, :])
            _, vjp = jax.vjp(_chunk_fn, *ins)
            grads = vjp((dy_ref[rows, :], ds_ref[...]))
            ds_ref[...] = grads[0]
            for o_ref, g in zip(out_refs, grads[1:]):
                o_ref[rows, :] = g
            return carry

        lax.fori_loop(0, per_blk, chunk, 0)
        finish()

    blk = pl.BlockSpec((CHUNK_TB, D), lambda i: (nblk - 1 - i, 0))
    any_spec = pl.BlockSpec(memory_space=pl.ANY)
    shp = jax.ShapeDtypeStruct((S, D), F32)
    outs = pl.pallas_call(
        body, name="scan_bwd", grid=(nblk,),
        in_specs=[blk] * 6 + [pl.BlockSpec((per_blk, N_HEADS, HEAD, HEAD), lambda i: (nblk - 1 - i, 0, 0, 0)), blk]
        + [any_spec] * n_x,
        out_specs=[blk] * 6 + [any_spec] * n_x, out_shape=[shp] * 6 + _exchange_shapes(scatter, False),
        scratch_shapes=[pltpu.VMEM((N_HEADS, HEAD, HEAD), F32)] + (_exchange_scratch(n_x) if n_x else []),
        compiler_params=_cparams(("arbitrary",)))(r, lw, k, v, a, b, ckpt, dy, *scatter)
    return outs[:6], outs[6:]


def _ada_partial(c_all, w_shard):
    def body(c_ref, w_ref, o_ref):
        o_ref[...] = jnp.dot(c_ref[...].astype(BF16), w_ref[...].astype(BF16), preferred_element_type=F32)

    vm = pl.BlockSpec(memory_space=pltpu.VMEM)
    return pl.pallas_call(body, name="ada_partial", in_specs=[vm, vm], out_specs=vm,
                          out_shape=jax.ShapeDtypeStruct((N_DEV, w_shard.shape[1]), F32),
                          compiler_params=pltpu.CompilerParams(vmem_limit_bytes=VMEM_LIMIT))(c_all, w_shard)


def _ada_bias(rows, b_ada):
    def body(r_ref, b_ref, o_ref):
        o_ref[...] = r_ref[...] + b_ref[...]

    vm = pl.BlockSpec(memory_space=pltpu.VMEM)
    return pl.pallas_call(body, name="ada_bias", in_specs=[vm, vm], out_specs=vm,
                          out_shape=jax.ShapeDtypeStruct(rows.shape, F32))(rows, b_ada)


def _ada_wgrad(c_cols, d_all):
    def body(c_ref, d_ref, o_ref):
        acc = c_ref[:, 0:1] * d_ref[0:1, :]
        for j in range(1, N_DEV):
            acc = acc + c_ref[:, j:j + 1] * d_ref[j:j + 1, :]
        o_ref[...] = acc

    vm = pl.BlockSpec(memory_space=pltpu.VMEM)
    return pl.pallas_call(body, name="ada_wgrad", in_specs=[vm, vm], out_specs=vm,
                          out_shape=jax.ShapeDtypeStruct((D, d_all.shape[1]), F32),
                          compiler_params=pltpu.CompilerParams(vmem_limit_bytes=VMEM_LIMIT))(c_cols, d_all)


def _exchange(srcs, broadcast, name):
    n = len(srcs)

    def body(*refs):
        start, wait = _exchange_ops(refs[:n], refs[n:2 * n], *refs[2 * n:], broadcast)
        start()
        wait()

    any_spec = pl.BlockSpec(memory_space=pl.ANY)
    return pl.pallas_call(
        body, name=name, out_shape=_exchange_shapes(srcs, broadcast), in_specs=[any_spec] * n, out_specs=[any_spec] * n,
        scratch_shapes=_exchange_scratch(n),
        compiler_params=pltpu.CompilerParams(has_side_effects=True),
    )(*srcs)


def _gather_via_sibling(srcs, name):
    n = len(srcs)

    def body(*refs):
        src_refs, out_refs = refs[:n], refs[n:2 * n]
        send_sems, recv_sems, local_sems = refs[2 * n:]
        x, y, c = lax.axis_index("x"), lax.axis_index("y"), lax.axis_index("c")
        me, sibling = (x, y, c), (x, y, 1 - c)
        chips = [(1 - x, y), (x, 1 - y), (1 - x, 1 - y)]

        def slot(px, py, pc):
            return 4 * px + 2 * py + pc

        def copy(i, k, block, to, src=None):
            rows = out_refs[i].at[slot(*block)]
            return pltpu.make_async_remote_copy(
                src_ref=rows if src is None else src, dst_ref=rows, send_sem=send_sems.at[i, k],
                recv_sem=recv_sems.at[i, k], device_id=to, device_id_type=_MESH)

        local = [pltpu.make_async_copy(src_refs[i], out_refs[i].at[slot(*me)], local_sems.at[i]) for i in range(n)]
        for cp in local:
            cp.start()
        first = [copy(i, 0, me, sibling, src=src_refs[i]) for i in range(n)]
        first += [copy(i, 1 + j, me, (*chip, c), src=src_refs[i]) for j, chip in enumerate(chips) for i in range(n)]
        for cp in first:
            cp.start()
        passed = []
        for j, chip in enumerate(chips):
            for i in range(n):
                copy(i, 1 + j, (*chip, c), me).wait_recv()
                passed.append(copy(i, 4 + j, (*chip, c), sibling))
                passed[-1].start()
        for i in range(n):
            copy(i, 0, sibling, me).wait_recv()
            for j, chip in enumerate(chips):
                copy(i, 4 + j, (*chip, 1 - c), me).wait_recv()
        for cp in first + passed:
            cp.wait_send()
        for cp in local:
            cp.wait()

    any_spec = pl.BlockSpec(memory_space=pl.ANY)
    return pl.pallas_call(
        body, name=name, out_shape=_exchange_shapes(srcs, True), in_specs=[any_spec] * n, out_specs=[any_spec] * n,
        scratch_shapes=_exchange_scratch(n),
        compiler_params=pltpu.CompilerParams(has_side_effects=True),
    )(*srcs)


def _flags(broadcast, n):
    return [broadcast] * n if isinstance(broadcast, bool) else list(broadcast)


def _exchange_shapes(srcs, broadcast):
    return [jax.ShapeDtypeStruct((N_DEV,) + (s.shape if bc else s.shape[1:]), s.dtype)
            for s, bc in zip(srcs, _flags(broadcast, len(srcs)))]


def _exchange_scratch(n):
    return [pltpu.SemaphoreType.DMA((n, N_DEV)), pltpu.SemaphoreType.DMA((n, N_DEV)), pltpu.SemaphoreType.DMA((n,))]


def _exchange_ops(src_refs, out_refs, send_sems, recv_sems, local_sems, broadcast):
    n = len(src_refs)
    flags = _flags(broadcast, n)
    x, y, c = lax.axis_index("x"), lax.axis_index("y"), lax.axis_index("c")
    me = 4 * x + 2 * y + c

    def block(i, j):
        return src_refs[i] if flags[i] else src_refs[i].at[j]

    def remote(i, d, src_slot, dst_slot):
        px, py, pc = x ^ (d >> 2), y ^ ((d >> 1) & 1), c ^ (d & 1)
        return pltpu.make_async_remote_copy(
            src_ref=block(i, src_slot), dst_ref=out_refs[i].at[dst_slot], send_sem=send_sems.at[i, d],
            recv_sem=recv_sems.at[i, d], device_id=(px, py, pc), device_id_type=_MESH)

    def local(i):
        return pltpu.make_async_copy(block(i, me), out_refs[i].at[me], local_sems.at[i])

    def start():
        for i in range(n):
            local(i).start()
        for d in range(1, N_DEV):
            for i in range(n):
                remote(i, d, me ^ d, me).start()

    def wait():
        for d in range(1, N_DEV):
            for i in range(n):
                remote(i, d, me, me ^ d).wait_recv()
        for d in range(1, N_DEV):
            for i in range(n):
                remote(i, d, me ^ d, me).wait_send()
        for i in range(n):
            local(i).wait()

    return start, wait


def _adamw(w, g, m, v):
    nm = ADAM_B1 * m + (1.0 - ADAM_B1) * g
    nv = ADAM_B2 * v + (1.0 - ADAM_B2) * (g * g)
    m_hat = nm * (1.0 / (1.0 - ADAM_B1 ** ADAM_STEP))
    v_hat = nv * (1.0 / (1.0 - ADAM_B2 ** ADAM_STEP))
    return -ADAM_LR * (m_hat / (jnp.sqrt(v_hat) + ADAM_EPS) + ADAM_WD * w), nm, nv


def _adam_vectors(parts, ws, ms, vs):
    nv = len(ws)
    sizes = [w.shape[1] for w in ws]

    def body(*refs):
        p_ref = refs[0]
        w_refs, m_refs, v_refs = refs[1:1 + nv], refs[1 + nv:1 + 2 * nv], refs[1 + 2 * nv:1 + 3 * nv]
        out_refs = refs[1 + 3 * nv:]
        g_all = p_ref[0]
        for j in range(1, N_DEV):
            g_all = g_all + p_ref[j]
        off = 0
        for i, n in enumerate(sizes):
            g = g_all[:, off:off + n]
            off += -(-n // LANES) * LANES
            delta, new_m, new_v = _adamw(w_refs[i][...], g, m_refs[i][...], v_refs[i][...])
            for o_ref, val in zip(out_refs[4 * i:4 * i + 4], (g, delta, new_m, new_v)):
                o_ref[...] = val

    vm = pl.BlockSpec(memory_space=pltpu.VMEM)
    outs = pl.pallas_call(body, name="adam_replicated", in_specs=[vm] * (1 + 3 * nv), out_specs=[vm] * (4 * nv),
                          out_shape=[jax.ShapeDtypeStruct((1, n), F32) for n in sizes for _ in range(4)])(parts, *ws, *ms, *vs)
    return [outs[4 * i:4 * i + 4] for i in range(nv)]


def _sum_adam(parts, w, m, v, name):
    n_parts, R, C = parts.shape
    fits = [t for t in range(16, R + 1, 16) if R % t == 0 and t * C <= 2504 * LANES]
    if fits:
        tm, tc = max(fits), C
    elif C % (2 * LANES) == 0 and R * C > 2504 * LANES:
        tm, tc = R, 2 * LANES
    else:
        tm, tc = R, C

    def body(p_ref, w_ref, m_ref, v_ref, g_ref, d_ref, nm_ref, nv_ref):
        g = p_ref[0].astype(F32)
        for j in range(1, n_parts):
            g = g + p_ref[j].astype(F32)
        g_ref[...] = g
        d_ref[...], nm_ref[...], nv_ref[...] = _adamw(w_ref[...], g, m_ref[...], v_ref[...])

    blk = pl.BlockSpec((tm, tc), lambda i, j: (i, j))
    shp = jax.ShapeDtypeStruct((R, C), F32)
    return pl.pallas_call(body, name=name, grid=(R // tm, C // tc),
                          in_specs=[pl.BlockSpec((n_parts, tm, tc), lambda i, j: (0, i, j)), blk, blk, blk],
                          out_specs=[blk] * 4, out_shape=[shp] * 4,
                          compiler_params=_cparams(("parallel", "parallel")))(parts, w, m, v)


TRANSPOSED = ("w_in", "w_up")
SHARDED = (("w_ada", 1), ("w_in", 0), ("w2", 1), ("a2", 1), ("g2", 1), ("w_att_out", 1), ("w_rwkv_out", 0),
           ("w_o", 0), ("w_up", 0), ("conv_w", 1), ("w_down", 0))
EARLY, LATE = SHARDED[1:5], SHARDED[5:]
REPLICATED = ("b_ada", "norm1_w", "b_gate", "mu_shift", "w0", "a0", "k_k", "k_a", "r_k", "lnx_w", "lnx_b",
              "norm2_w", "conv_b", "norm_f_w")
WEIGHTS = ("w_ada", "b_ada", "norm1_w", "w_in", "b_gate", "mu_shift", "w0", "w2", "a0", "a2", "g2", "k_k", "k_a", "r_k",
           "lnx_w", "lnx_b", "w_att_out", "w_rwkv_out", "w_o", "norm2_w", "w_up", "conv_w", "conv_b", "w_down", "norm_f_w")


def _pad_w_in(w_in_t):
    rkv = w_in_t[ATT_IN:ATT_IN + 3 * D]
    lora = w_in_t[ATT_IN + 3 * D:ATT_IN + RWKV_IN]
    gates = w_in_t[ATT_IN + RWKV_IN:]
    att = w_in_t[:ATT_IN]
    lw, la, lg = lora[:LORA_W], lora[LORA_W:LORA_W + LORA_A], lora[LORA_W + LORA_A:]
    zeros = jnp.zeros((LORA_PAD - LANES - LORA_G, w_in_t.shape[1]), w_in_t.dtype)
    return jnp.concatenate([rkv, gates, att, lw, la, lg, zeros], axis=0)


def _unpad_w_in(g):
    att = g[C_ATT:C_ATT + ATT_IN]
    rkv = g[C_R:C_R + 3 * D]
    lora = jnp.concatenate([g[C_LORA:C_LORA + LORA_W + LORA_A], g[C_LORA + LANES:C_LORA + LANES + LORA_G]], axis=0)
    gates = g[C_GA:C_GA + 2 * D]
    return jnp.concatenate([att, rkv, lora, gates], axis=0)


def _pad_mu(mu):
    lo = mu[:, 3 * D:]
    mu_l = jnp.concatenate([lo[:, :LORA_W + LORA_A], lo[:, LORA_W + LORA_A:], jnp.zeros((1, LORA_PAD - LANES - LORA_G), mu.dtype)], axis=1)
    return mu[:, :D], mu[:, D:2 * D], mu[:, 2 * D:3 * D], mu_l


def _local_step(x, ada, W, late_shards, target):
    S = x.shape[0]
    W = dict(W)
    G = {}
    sh1, sc1, gt1, sh2, sc2, gt2 = [ada[:, i * D:(i + 1) * D] for i in range(6)]
    h1, rstd1 = _norm_fwd(x, None, None, W["norm1_w"], sc1, sh1, "norm1_fwd")
    w_in_p = _pad_w_in(W["w_in"])
    P = _mm(h1, w_in_p, "nt", F32, "proj_in")

    mu_r, mu_k, mu_v, mu_l = _pad_mu(W["mu_shift"])
    g2p = jnp.pad(W["g2"], ((0, G_PAD - LORA_G), (0, 0)))
    prep_params = [mu_r, mu_k, mu_v, mu_l, W["w0"], W["a0"], W["k_k"], W["k_a"], W["w2"], W["a2"], g2p]
    r_, dec, kmod, v_, aa, bb, gg = _rwkv_prep(P, prep_params)
    y_scan, states, late = _cscan_fwd(r_, dec, kmod, v_, aa, bb, gather=late_shards)
    W.update({n: _full_weight(g, axis) for (n, axis), g in zip(LATE, late)})

    o_g, l_g = zip(*[_att_fwd(P, g) for g in range(len(ATT_PATTERNS))])
    att = _att_combine_fwd(o_g, l_g)
    y_att = _mm(att, W["w_att_out"], "nn", F32, "att_out")
    r_k = W["r_k"].reshape(1, D)
    rw = _rwkv_post(y_scan, r_, kmod, v_, gg, W["lnx_w"], W["lnx_b"], r_k)
    y_rwkv = _mm(rw, W["w_rwkv_out"], "nn", F32, "rwkv_out")

    bga, bgr = W["b_gate"][:, :D], W["b_gate"][:, D:]
    mix = _gate_fwd(P, bga, bgr, y_att, y_rwkv)
    mo = _mm(mix, W["w_o"], "nn", F32, "mix_out")
    x2, h2, rstd2 = _norm_fwd(x, mo, gt1, W["norm2_w"], sc2, sh2, "norm2_fwd")
    u = _mm(h2, W["w_up"], "nt", F32, "ffn_up")
    conv_w8 = jnp.pad(W["conv_w"], ((0, SUBLANES - 3), (0, 0)))
    act = _conv_fwd(u, conv_w8, W["conv_b"])
    f = _mm(act, W["w_down"], "nn", F32, "ffn_down")
    loss_blk, dx3, df, dgt2, G["norm_f_w"] = _final(x2, f, gt2, W["norm_f_w"], target)
    loss = loss_blk[0, 0]

    dact = _mm(df, W["w_down"], "nt", BF16, "ffn_down_dx")
    G["w_down"] = _mm(act, df, "tn", BF16, "ffn_down_dw")
    duc, dwg, dwv, dbg, dbv = _conv_bwd_a(dact, u, conv_w8, W["conv_b"])
    G["conv_w"] = jnp.concatenate([dwg[0:3], dwv[0:3]], axis=1)
    G["conv_b"] = jnp.concatenate([dbg, dbv], axis=1)
    du = _conv_bwd_b(duc, conv_w8)
    dh2 = _mm(du, W["w_up"], "nn", F32, "ffn_up_dx")
    G["w_up"] = _mm(du, h2, "tn", BF16, "ffn_up_dw")
    dx2, dsh2, dsc2, G["norm2_w"], dmo, dgt1 = _norm_bwd(dh2, x2, rstd2, W["norm2_w"], sc2, dx3, mo, gt1, "norm2_bwd")
    dmix = _mm(dmo, W["w_o"], "nt", F32, "mix_out_dx")
    G["w_o"] = _mm(mix, dmo, "tn", BF16, "mix_out_dw")
    dy_att, dy_rwkv, dpga, dpgr, dbga, dbgr = _gate_bwd(dmix, P, bga, bgr, y_att, y_rwkv)
    G["b_gate"] = jnp.concatenate([dbga, dbgr], axis=1)

    datt = _mm(dy_att, W["w_att_out"], "nt", F32, "att_out_dx")
    G["w_att_out"] = _mm(att, dy_att, "tn", BF16, "att_out_dw")
    dcomb = _att_combine_bwd(datt, o_g, l_g)
    dp_att = []
    for g in range(len(ATT_PATTERNS)):
        dp_att += _att_bwd(P, o_g[g], l_g[g], dcomb[g], dcomb[3 + g], g)

    drw = _mm(dy_rwkv, W["w_rwkv_out"], "nt", F32, "rwkv_out_dx")
    G["w_rwkv_out"] = _mm(rw, dy_rwkv, "tn", BF16, "rwkv_out_dw")
    dy_scan, dr1, dk1, dv1, dgg, G["lnx_w"], G["lnx_b"], drk = _rwkv_post_bwd(drw, y_scan, r_, kmod, v_, gg, W["lnx_w"], W["lnx_b"], r_k)
    G["r_k"] = drk.reshape(W["r_k"].shape)
    late_blocks = [_owner_blocks(G[n], axis) for n, axis in LATE] if late_shards else []
    (dr2, ddec, dk2, dv2, daa, dbb), late_parts = _cscan_bwd(r_, dec, kmod, v_, aa, bb, states, dy_scan, scatter=late_blocks)
    pb = _rwkv_prep_bwd(P, prep_params, [dr2, ddec, dk2, dv2, daa, dbb, dgg], [dr1, None, dk1, dv1, None, None, None])
    dz, dzp, dpar = pb[0:4], pb[4:8], pb[8:]
    dp_rkv = [_shift_add(dz[i], dzp[i]) for i in range(3)]
    dp_lora = _shift_add(dz[3], dzp[3])
    dmu_r, dmu_k, dmu_v, dmu_l, G["w0"], G["a0"], G["k_k"], G["k_a"], G["w2"], G["a2"], dg2p = dpar
    G["g2"] = dg2p[0:LORA_G]
    G["mu_shift"] = jnp.concatenate([dmu_r, dmu_k, dmu_v, dmu_l[:, :LORA_W + LORA_A], dmu_l[:, LANES:LANES + LORA_G]], axis=1)

    dP = jnp.concatenate(dp_rkv + [dpga, dpgr] + dp_att + [dp_lora], axis=1)
    G["w_in"] = _unpad_w_in(_mm(dP, h1, "tn", BF16, "proj_in_dw"))
    if late_shards:
        dh1, (w_in_parts,) = _mm(dP, w_in_p, "nn", F32, "proj_in_dx", scatter=[_owner_blocks(G["w_in"], 0)])
        done = dict(zip([n for n, _ in LATE] + ["w_in"], list(late_parts) + [w_in_parts]))
    else:
        dh1, done = _mm(dP, w_in_p, "nn", F32, "proj_in_dx"), {}
    grad_x, dsh1, dsc1, G["norm1_w"] = _norm_bwd(dh1, x, rstd1, W["norm1_w"], sc1, dx2, None, None, "norm1_bwd")
    dada = jnp.concatenate([dsh1, dsc1, dgt1, dsh2, dsc2, dgt2], axis=1)
    G["b_ada"] = dada
    return loss, grad_x, G, done


def _full_weight(gathered, axis):
    _, rows, cols = gathered.shape
    if axis == 0:
        return gathered.reshape(N_DEV * rows, cols)
    return gathered.transpose(1, 0, 2).reshape(rows, N_DEV * cols)


def _owner_blocks(g, axis):
    rows, cols = g.shape
    g = g.astype(BF16)
    if axis == 0:
        return g.reshape(N_DEV, rows // N_DEV, cols)
    return g.reshape(rows, N_DEV, cols // N_DEV).transpose(1, 0, 2)


def kernel(x, c, w_ada, b_ada, norm1_w, w_in, b_gate, mu_shift, w0, w2, a0, a2, g2, k_k, k_a, r_k, lnx_w, lnx_b, w_att_out, w_rwkv_out, w_o, norm2_w, w_up, conv_w, conv_b, w_down, norm_f_w, loss_target, m_w_ada, m_b_ada, m_norm1_w, m_w_in, m_b_gate, m_mu_shift, m_w0, m_w2, m_a0, m_a2, m_g2, m_k_k, m_k_a, m_r_k, m_lnx_w, m_lnx_b, m_w_att_out, m_w_rwkv_out, m_w_o, m_norm2_w, m_w_up, m_conv_w, m_conv_b, m_w_down, m_norm_f_w, v_w_ada, v_b_ada, v_norm1_w, v_w_in, v_b_gate, v_mu_shift, v_w0, v_w2, v_a0, v_a2, v_g2, v_k_k, v_k_a, v_r_k, v_lnx_w, v_lnx_b, v_w_att_out, v_w_rwkv_out, v_w_o, v_norm2_w, v_w_up, v_conv_w, v_conv_b, v_w_down, v_norm_f_w):
    env = dict(locals())
    w_shard = {n: env[n] for n in WEIGHTS}
    m_shard = {n: env["m_" + n] for n in WEIGHTS}
    v_shard = {n: env["v_" + n] for n in WEIGHTS}

    def mat(shards, n):
        return jnp.swapaxes(shards[n][0], 0, 1) if n in TRANSPOSED else shards[n][0]

    c_all, *gathered = _gather_via_sibling([c] + [mat(w_shard, n).astype(BF16) for n, _ in EARLY], "gather_weights")
    c_all = c_all.reshape(N_DEV, D)
    W = {n: _full_weight(g, axis) for (n, axis), g in zip(EARLY, gathered)}
    for n in REPLICATED:
        W[n] = w_shard[n].reshape(1, -1) if n != "r_k" else w_shard[n][0]
    ada_cols = _ada_partial(c_all, w_shard["w_ada"][0])
    ada_rows, = _exchange([ada_cols[:, None, :]], False, "ada_rows")
    ada = _ada_bias(ada_rows.reshape(1, -1), w_shard["b_ada"])

    late_shards = [mat(w_shard, n).astype(BF16) for n, _ in LATE]
    loss, grad_x, G, parts = _local_step(x[0], ada, W, late_shards, loss_target[0])
    loss = lax.psum(loss, ("x", "y", "c"))

    row = lambda a: a.reshape(1, -1)
    small = jnp.concatenate([jnp.pad(row(G[n]), ((0, 0), (0, (-G[n].size) % LANES))) for n in REPLICATED], axis=1)
    sparts, dada_all = _exchange([small, G["b_ada"].reshape(N_DEV, 1, -1)], [True, False], "gather_small_grads")
    parts["w_ada"] = _ada_wgrad(c_all.T, dada_all.reshape(N_DEV, -1))[None]

    rest = [(n, axis) for n, axis in SHARDED if n not in parts]
    parts.update(zip([n for n, _ in rest], _exchange([_owner_blocks(G[n], axis) for n, axis in rest], False, "scatter_grads")))
    out = {}
    for n, p in parts.items():
        res = _sum_adam(p, mat(w_shard, n), mat(m_shard, n), mat(v_shard, n), "adam_" + n)
        if n in TRANSPOSED:
            res = [jnp.swapaxes(a, 0, 1) for a in res]
        for kind, a in zip(("grad", "delta", "new_m", "new_v"), res):
            out[kind, n] = a[None]

    res = _adam_vectors(sparts, *[[row(s[n]) for n in REPLICATED] for s in (w_shard, m_shard, v_shard)])
    for n, four in zip(REPLICATED, res):
        for kind, a in zip(("grad", "delta", "new_m", "new_v"), four):
            out[kind, n] = a.reshape(w_shard[n].shape)

    return (loss, grad_x[None], *[out[kind, n] for kind in ("grad", "delta", "new_m", "new_v") for n in WEIGHTS])
```

```python
import functools
import math

import jax
import jax.numpy as jnp
from jax import lax
from jax.experimental import pallas as pl
from jax.experimental.pallas import tpu as pltpu

F32 = jnp.float32
BF16 = jnp.bfloat16

D = 1024
HEAD = 64
ATT_PATTERNS = ((128, 1), (512, 4), (2048, 16))
ATT_HEADS = 8
ATT_W = ATT_HEADS * HEAD
ATT_IN = 3 * 3 * ATT_W
QBLK = 128
N_HEADS = D // HEAD
LORA_W, LORA_A, LORA_G = 64, 64, 160
RWKV_IN = 3 * D + LORA_W + LORA_A + LORA_G
N_IN = ATT_IN + RWKV_IN + 2 * D
D_FF = 2816
RMS_EPS = 1e-6
GN_EPS = 64e-5
N_DEV = 8
LANES = 128
SUBLANES = 8

C_R, C_K, C_V, C_GA, C_GR = 0, 1024, 2048, 3072, 4096
C_ATT = 5120
C_LORA = C_ATT + ATT_IN
LORA_PAD = 512
G_PAD = 256
N_PAD = C_LORA + LORA_PAD

ADAM_LR, ADAM_B1, ADAM_B2, ADAM_EPS, ADAM_WD, ADAM_STEP = 0.001, 0.9, 0.999, 1e-08, 0.01, 10

VMEM_LIMIT = 56 * 1024 * 1024

_MESH = pl.DeviceIdType.MESH


def _cparams(sem):
    return pltpu.CompilerParams(dimension_semantics=sem, vmem_limit_bytes=VMEM_LIMIT)


def _tile(dim, pref):
    if dim <= pref:
        return dim
    best = None
    for t in range(LANES, pref + 1, LANES):
        if dim % t == 0:
            best = t
    assert best is not None, dim
    return best


MM_TILES = {"nn": (1024, 1408, 1408), "nt": (1024, 2048, 1408), "tn": (1408, 1408, 1024)}


def _mm(a, b, mode, out_dtype, name, scatter=()):
    if mode == "nn":
        (M, K), (K2, N) = a.shape, b.shape
    elif mode == "nt":
        (M, K), (N, K2) = a.shape, b.shape
    else:
        (K, M), (K2, N) = a.shape, b.shape
    assert K == K2, (a.shape, b.shape, mode)
    tm, tn, tk = (_tile(dim, pref) for dim, pref in zip((M, N, K), MM_TILES[mode]))
    nk = K // tk
    grid = (M // tm, N // tn, nk)
    n_x = len(scatter)
    dims = {"nn": (((1,), (0,)), ((), ())), "nt": (((1,), (1,)), ((), ())), "tn": (((0,), (0,)), ((), ()))}[mode]

    def body(*refs):
        a_ref, b_ref = refs[:2]
        o_ref, acc_ref = refs[2 + n_x], refs[3 + 2 * n_x]
        finish = _hosted_exchange(refs[2:2 + n_x] + refs[3 + n_x:3 + 2 * n_x] + refs[4 + 2 * n_x:], n_x, False, grid)
        k = pl.program_id(2)
        part = lax.dot_general(a_ref[...].astype(BF16), b_ref[...].astype(BF16), dims,
                               preferred_element_type=F32)
        if nk == 1:
            o_ref[...] = part.astype(o_ref.dtype)
        else:
            @pl.when(k == 0)
            def _():
                acc_ref[...] = part

            @pl.when(jnp.logical_and(k > 0, k < nk - 1))
            def _():
                acc_ref[...] += part

            @pl.when(k == nk - 1)
            def _():
                o_ref[...] = (acc_ref[...] + part).astype(o_ref.dtype)
        finish()

    a_spec = pl.BlockSpec((tk, tm), lambda i, j, k: (k, i)) if mode == "tn" else pl.BlockSpec((tm, tk), lambda i, j, k: (i, k))
    b_spec = pl.BlockSpec((tn, tk), lambda i, j, k: (j, k)) if mode == "nt" else pl.BlockSpec((tk, tn), lambda i, j, k: (k, j))
    any_spec = pl.BlockSpec(memory_space=pl.ANY)
    outs = pl.pallas_call(
        body, name=name, grid=grid,
        in_specs=[a_spec, b_spec] + [any_spec] * n_x,
        out_specs=[pl.BlockSpec((tm, tn), lambda i, j, k: (i, j))] + [any_spec] * n_x,
        out_shape=[jax.ShapeDtypeStruct((M, N), out_dtype)] + _exchange_shapes(scatter, False),
        scratch_shapes=[pltpu.VMEM((tm, tn) if nk > 1 else (SUBLANES, LANES), F32)] + (_exchange_scratch(n_x) if n_x else []),
        compiler_params=_cparams(("arbitrary",) * 3 if n_x else ("parallel", "parallel", "arbitrary")),
    )(a, b, *scatter)
    return (outs[0], outs[1:]) if n_x else outs[0]


def _rows(tm, w, col=0):
    return pl.BlockSpec((tm, w), lambda i: (i, col))


def _full(shape):
    return pl.BlockSpec(shape, lambda i: (0,) * len(shape))


def _shift_down(x, halo, k, first):
    rolled = pltpu.roll(x, k, 0)
    row = lax.broadcasted_iota(jnp.int32, x.shape, 0)
    out = rolled
    for j in range(k):
        h = jnp.where(first, 0.0, halo[SUBLANES - k + j:SUBLANES - k + j + 1, :])
        out = jnp.where(row == j, h, out)
    return out


def _shift_up(x, halo, k, last):
    n = x.shape[0]
    rolled = pltpu.roll(x, n - k, 0)
    row = lax.broadcasted_iota(jnp.int32, x.shape, 0)
    out = rolled
    for j in range(k):
        h = jnp.where(last, 0.0, halo[j:j + 1, :])
        out = jnp.where(row == n - k + j, h, out)
    return out


def _acc(ref, val, first):
    @pl.when(first)
    def _():
        ref[...] = val

    @pl.when(jnp.logical_not(first))
    def _():
        ref[...] += val


def _colsum(x):
    return jnp.sum(x, axis=0, keepdims=True)


def _norm_fwd(x, mo, gt, nw, sc, sh, name, tm=256):
    S = x.shape[0]
    has_res = mo is not None

    def body(*refs):
        if has_res:
            x_ref, mo_ref, gt_ref, nw_ref, sc_ref, sh_ref, x2_ref, h_ref, rs_ref = refs
            x2 = x_ref[...] + gt_ref[...] * mo_ref[...]
            x2_ref[...] = x2
        else:
            x_ref, nw_ref, sc_ref, sh_ref, h_ref, rs_ref = refs
            x2 = x_ref[...]
        rstd = lax.rsqrt(jnp.mean(x2 * x2, axis=-1, keepdims=True) + RMS_EPS)
        rs_ref[...] = rstd
        h_ref[...] = ((x2 * rstd * nw_ref[...]) * (1.0 + sc_ref[...]) + sh_ref[...]).astype(BF16)

    vec = _full((1, D))
    ins = [x, mo, gt, nw, sc, sh] if has_res else [x, nw, sc, sh]
    in_specs = [_rows(tm, D), _rows(tm, D), vec, vec, vec, vec] if has_res else [_rows(tm, D), vec, vec, vec]
    outs = [jax.ShapeDtypeStruct((S, D), BF16), jax.ShapeDtypeStruct((S, 1), F32)]
    out_specs = [_rows(tm, D), _rows(tm, 1)]
    if has_res:
        outs = [jax.ShapeDtypeStruct((S, D), F32)] + outs
        out_specs = [_rows(tm, D)] + out_specs
    return pl.pallas_call(body, name=name, grid=(S // tm,), in_specs=in_specs, out_specs=out_specs,
                          out_shape=outs, compiler_params=_cparams(("parallel",)))(*ins)


def _norm_bwd(dh, xin, rstd, nw, sc, dres, mo, gt, name, tm=256):
    S = xin.shape[0]
    has_res = mo is not None

    def body(*refs):
        if has_res:
            dh_ref, x_ref, rs_ref, nw_ref, sc_ref, dres_ref, mo_ref, gt_ref, dx_ref, dsh_ref, dsc_ref, dnw_ref, dmo_ref, dgt_ref = refs
        else:
            dh_ref, x_ref, rs_ref, nw_ref, sc_ref, dres_ref, dx_ref, dsh_ref, dsc_ref, dnw_ref = refs
        first = pl.program_id(0) == 0
        dh = dh_ref[...]
        rstd = rs_ref[...]
        n = x_ref[...] * rstd
        w = nw_ref[...]
        _acc(dsh_ref, _colsum(dh), first)
        _acc(dsc_ref, _colsum(dh * (n * w)), first)
        dnw = dh * (1.0 + sc_ref[...])
        _acc(dnw_ref, _colsum(dnw * n), first)
        dn = dnw * w
        dx = dres_ref[...] + rstd * (dn - n * jnp.mean(dn * n, axis=-1, keepdims=True))
        dx_ref[...] = dx
        if has_res:
            dmo_ref[...] = (dx * gt_ref[...]).astype(BF16)
            _acc(dgt_ref, _colsum(dx * mo_ref[...]), first)

    vec = _full((1, D))
    vshape = jax.ShapeDtypeStruct((1, D), F32)
    ins = [dh, xin, rstd, nw, sc, dres] + ([mo, gt] if has_res else [])
    in_specs = [_rows(tm, D), _rows(tm, D), _rows(tm, 1), vec, vec, _rows(tm, D)] + ([_rows(tm, D), vec] if has_res else [])
    outs = [jax.ShapeDtypeStruct((S, D), F32), vshape, vshape, vshape]
    out_specs = [_rows(tm, D), vec, vec, vec]
    if has_res:
        outs += [jax.ShapeDtypeStruct((S, D), BF16), vshape]
        out_specs += [_rows(tm, D), vec]
    return pl.pallas_call(body, name=name, grid=(S // tm,), in_specs=in_specs, out_specs=out_specs,
                          out_shape=outs, compiler_params=_cparams(("arbitrary",)))(*ins)


def _final(x2, f, gt2, nfw, target, tm=256):
    S = x2.shape[0]

    def body(x2_ref, f_ref, gt_ref, w_ref, t_ref, loss_ref, dx_ref, df_ref, dgt_ref, dw_ref):
        first = pl.program_id(0) == 0
        f = f_ref[...]
        gt = gt_ref[...]
        w = w_ref[...]
        x3 = x2_ref[...] + gt * f
        rstd = lax.rsqrt(jnp.mean(x3 * x3, axis=-1, keepdims=True) + RMS_EPS)
        n = x3 * rstd
        e = n * w - t_ref[...]
        part = 0.5 * jnp.sum(jnp.mean(e * e, axis=-1, keepdims=True), axis=0, keepdims=True)
        _acc(loss_ref, jnp.broadcast_to(part, (SUBLANES, LANES)), first)
        dy = e * (1.0 / D)
        _acc(dw_ref, _colsum(dy * n), first)
        dn = dy * w
        dx = rstd * (dn - n * jnp.mean(dn * n, axis=-1, keepdims=True))
        dx_ref[...] = dx
        df_ref[...] = (dx * gt).astype(BF16)
        _acc(dgt_ref, _colsum(dx * f), first)

    vec = _full((1, D))
    vshape = jax.ShapeDtypeStruct((1, D), F32)
    return pl.pallas_call(
        body, name="final_loss", grid=(S // tm,),
        in_specs=[_rows(tm, D), _rows(tm, D), vec, vec, _rows(tm, D)],
        out_specs=[_full((SUBLANES, LANES)), _rows(tm, D), _rows(tm, D), vec, vec],
        out_shape=[jax.ShapeDtypeStruct((SUBLANES, LANES), F32), jax.ShapeDtypeStruct((S, D), F32),
                   jax.ShapeDtypeStruct((S, D), BF16), vshape, vshape],
        compiler_params=_cparams(("arbitrary",)))(x2, f, gt2, nfw, target)


def _gate_fwd(P, bga, bgr, y_att, y_rwkv, tm=256):
    S = P.shape[0]

    def body(pa_ref, pr_ref, ba_ref, br_ref, ya_ref, yr_ref, mix_ref):
        ga = jax.nn.sigmoid(pa_ref[...] + ba_ref[...])
        gr = jax.nn.sigmoid(pr_ref[...] + br_ref[...])
        mix_ref[...] = (ga * ya_ref[...] + gr * yr_ref[...]).astype(BF16)

    vec = _full((1, D))
    return pl.pallas_call(
        body, name="gate_fwd", grid=(S // tm,),
        in_specs=[_rows(tm, D, C_GA // D), _rows(tm, D, C_GR // D), vec, vec, _rows(tm, D), _rows(tm, D)],
        out_specs=_rows(tm, D), out_shape=jax.ShapeDtypeStruct((S, D), BF16),
        compiler_params=_cparams(("parallel",)))(P, P, bga, bgr, y_att, y_rwkv)


def _gate_bwd(dmix, P, bga, bgr, y_att, y_rwkv, tm=256):
    S = P.shape[0]

    def body(dm_ref, pa_ref, pr_ref, ba_ref, br_ref, ya_ref, yr_ref, dya_ref, dyr_ref, dpa_ref, dpr_ref, dba_ref, dbr_ref):
        first = pl.program_id(0) == 0
        dm = dm_ref[...]
        ga = jax.nn.sigmoid(pa_ref[...] + ba_ref[...])
        gr = jax.nn.sigmoid(pr_ref[...] + br_ref[...])
        dya_ref[...] = (dm * ga).astype(BF16)
        dyr_ref[...] = (dm * gr).astype(BF16)
        dpa = dm * ya_ref[...] * ga * (1.0 - ga)
        dpr = dm * yr_ref[...] * gr * (1.0 - gr)
        dpa_ref[...] = dpa.astype(BF16)
        dpr_ref[...] = dpr.astype(BF16)
        _acc(dba_ref, _colsum(dpa), first)
        _acc(dbr_ref, _colsum(dpr), first)

    vec = _full((1, D))
    row = _rows(tm, D)
    rshape = jax.ShapeDtypeStruct((S, D), BF16)
    vshape = jax.ShapeDtypeStruct((1, D), F32)
    return pl.pallas_call(
        body, name="gate_bwd", grid=(S // tm,),
        in_specs=[row, _rows(tm, D, C_GA // D), _rows(tm, D, C_GR // D), vec, vec, row, row],
        out_specs=[row, row, row, row, vec, vec],
        out_shape=[rshape, rshape, rshape, rshape, vshape, vshape],
        compiler_params=_cparams(("arbitrary",)))(dmix, P, P, bga, bgr, y_att, y_rwkv)


CONV_TN = D_FF // 2


def _conv_fwd(u, conv_w8, conv_b, tm=256, tn=CONV_TN):
    S = u.shape[0]
    nj = D_FF // tn

    def conv(u_ref, h_ref, w_ref, b_ref, first):
        u = u_ref[...]
        h = h_ref[...]
        w = w_ref[...]
        return b_ref[...] + w[0:1] * _shift_down(u, h, 2, first) + w[1:2] * _shift_down(u, h, 1, first) + w[2:3] * u

    def body(ug_ref, hg_ref, uv_ref, hv_ref, wg_ref, wv_ref, bg_ref, bv_ref, act_ref):
        first = pl.program_id(0) == 0
        g = conv(ug_ref, hg_ref, wg_ref, bg_ref, first)
        v = conv(uv_ref, hv_ref, wv_ref, bv_ref, first)
        act_ref[...] = (g * jax.nn.sigmoid(g) * v).astype(BF16)

    blk = lambda off: pl.BlockSpec((tm, tn), lambda i, j: (i, j + off))
    halo = lambda off: pl.BlockSpec((SUBLANES, tn), lambda i, j: (jnp.maximum(i * (tm // SUBLANES) - 1, 0), j + off))
    wsp = lambda off: pl.BlockSpec((SUBLANES, tn), lambda i, j: (0, j + off))
    bsp = lambda off: pl.BlockSpec((1, tn), lambda i, j: (0, j + off))
    return pl.pallas_call(
        body, name="conv_fwd", grid=(S // tm, nj),
        in_specs=[blk(0), halo(0), blk(nj), halo(nj), wsp(0), wsp(nj), bsp(0), bsp(nj)],
        out_specs=pl.BlockSpec((tm, tn), lambda i, j: (i, j)),
        out_shape=jax.ShapeDtypeStruct((S, D_FF), BF16),
        compiler_params=_cparams(("parallel", "parallel")))(u, u, u, u, conv_w8, conv_w8, conv_b, conv_b)


def _conv_bwd_a(dact, u, conv_w8, conv_b, tm=256, tn=CONV_TN):
    S = u.shape[0]
    nj = D_FF // tn

    def half(u_ref, h_ref, w_ref, b_ref, first):
        u = u_ref[...]
        h = h_ref[...]
        w = w_ref[...]
        u2, u1 = _shift_down(u, h, 2, first), _shift_down(u, h, 1, first)
        return b_ref[...] + w[0:1] * u2 + w[1:2] * u1 + w[2:3] * u, (u2, u1, u)

    def wgrad(d, taps):
        z = jnp.zeros((SUBLANES - 3, d.shape[1]), F32)
        return jnp.concatenate([_colsum(d * taps[0]), _colsum(d * taps[1]), _colsum(d * taps[2]), z], axis=0)

    def body(da_ref, ug_ref, hg_ref, uv_ref, hv_ref, wg_ref, wv_ref, bg_ref, bv_ref,
             d_ref, dwg_ref, dwv_ref, dbg_ref, dbv_ref):
        first = pl.program_id(1) == 0
        g, tg = half(ug_ref, hg_ref, wg_ref, bg_ref, first)
        v, tv = half(uv_ref, hv_ref, wv_ref, bv_ref, first)
        da = da_ref[...].astype(F32)
        sg = jax.nn.sigmoid(g)
        dg = da * v * (sg * (1.0 + g * (1.0 - sg)))
        dv = da * (g * sg)
        d_ref[0] = dg
        d_ref[1] = dv
        _acc(dwg_ref, wgrad(dg, tg), first)
        _acc(dwv_ref, wgrad(dv, tv), first)
        _acc(dbg_ref, _colsum(dg), first)
        _acc(dbv_ref, _colsum(dv), first)

    blk = lambda off: pl.BlockSpec((tm, tn), lambda j, i: (i, j + off))
    halo = lambda off: pl.BlockSpec((SUBLANES, tn), lambda j, i: (jnp.maximum(i * (tm // SUBLANES) - 1, 0), j + off))
    wsp = lambda off: pl.BlockSpec((SUBLANES, tn), lambda j, i: (0, j + off))
    bsp = lambda off: pl.BlockSpec((1, tn), lambda j, i: (0, j + off))
    f = jax.ShapeDtypeStruct
    outs = pl.pallas_call(
        body, name="conv_bwd_a", grid=(nj, S // tm),
        in_specs=[pl.BlockSpec((tm, tn), lambda j, i: (i, j)), blk(0), halo(0), blk(nj), halo(nj), wsp(0), wsp(nj), bsp(0), bsp(nj)],
        out_specs=[pl.BlockSpec((2, tm, tn), lambda j, i: (0, i, j)),
                   pl.BlockSpec((SUBLANES, tn), lambda j, i: (0, j)), pl.BlockSpec((SUBLANES, tn), lambda j, i: (0, j)),
                   pl.BlockSpec((1, tn), lambda j, i: (0, j)), pl.BlockSpec((1, tn), lambda j, i: (0, j))],
        out_shape=[f((2, S, D_FF), F32), f((SUBLANES, D_FF), F32), f((SUBLANES, D_FF), F32),
                   f((1, D_FF), F32), f((1, D_FF), F32)],
        compiler_params=_cparams(("parallel", "arbitrary")))(dact, u, u, u, u, conv_w8, conv_w8, conv_b, conv_b)
    return outs


def _conv_bwd_b(duc, conv_w8, tm=256, tn=CONV_TN):
    _, S, W = duc.shape
    nj = W // tn
    n_rows = S // tm

    def body(d_ref, h_ref, w_ref, o_ref):
        last = pl.program_id(0) == n_rows - 1
        d = d_ref[...]
        h = h_ref[...]
        w = w_ref[...]
        o_ref[...] = (w[2:3] * d + w[1:2] * _shift_up(d, h, 1, last) + w[0:1] * _shift_up(d, h, 2, last)).astype(BF16)

    last_tile = S // SUBLANES - 1
    return pl.pallas_call(
        body, name="conv_bwd_b", grid=(n_rows, 2 * nj),
        in_specs=[pl.BlockSpec((None, tm, tn), lambda i, j: (j // nj, i, j % nj)),
                  pl.BlockSpec((None, SUBLANES, tn), lambda i, j: (j // nj, jnp.minimum((i + 1) * (tm // SUBLANES), last_tile), j % nj)),
                  pl.BlockSpec((SUBLANES, tn), lambda i, j: (0, j))],
        out_specs=pl.BlockSpec((tm, tn), lambda i, j: (i, j)),
        out_shape=jax.ShapeDtypeStruct((S, 2 * W), BF16),
        compiler_params=_cparams(("parallel", "parallel")))(duc, duc, conv_w8)


ATT_SCALE = HEAD ** -0.5
NEG = -1e30
ATT_PAIRS = ATT_HEADS // 2


def _att_rows(n, d, S):
    per = S // (QBLK * d)
    r, m = n // per, n % per
    cur = pl.ds(m * (QBLK * d) + r, QBLK, stride=d)
    prv = pl.ds(jnp.maximum(m - 1, 0) * (QBLK * d) + r, QBLK, stride=d)
    return cur, prv, m > 0


def _att_slab(g, j):
    return (C_ATT + g * 3 * ATT_W + j * ATT_W) // LANES


def _heads(x):
    return x[:, 0:HEAD], x[:, HEAD:2 * HEAD]


ATT_NB = 4


def _stack(tiles):
    return jnp.concatenate([t[None] for t in tiles], axis=0)


def _att_operands(i, d, S, *sources):
    rows, has = [], []
    tiles = [[] for _ in sources]
    for bb in range(ATT_NB):
        cur, prv, has_prev = _att_rows(i * ATT_NB + bb, d, S)
        rows.append((cur, prv))
        has.append(has_prev)
        for t, (ref, use_cur) in zip(tiles, sources):
            t += _heads(ref[cur if use_cur else prv, :].astype(BF16))
    return rows, has, [_stack(t) for t in tiles]


def _att_mask(s_c, s_p, has_prev):
    qi = lax.broadcasted_iota(jnp.int32, (QBLK, QBLK), 0)
    kj = lax.broadcasted_iota(jnp.int32, (QBLK, QBLK), 1)
    s_c = jnp.where(kj <= qi, s_c * ATT_SCALE, NEG)
    s_p = jnp.where(jnp.logical_and(kj >= qi, has_prev), s_p * ATT_SCALE, NEG)
    return s_c, s_p


def _att_fwd(P, g):
    S = P.shape[0]
    d = ATT_PATTERNS[g][1]

    def body(q_ref, k_ref, v_ref, o_ref, l_ref):
        def group(i, carry):
            rows, has, (q, kc, kp, vc, vp) = _att_operands(i, d, S, (q_ref, True), (k_ref, True), (k_ref, False),
                                                           (v_ref, True), (v_ref, False))
            s_c_all, s_p_all = _dot16(q, kc, "nt"), _dot16(q, kp, "nt")
            p_c, p_p, den, lse = [], [], [], []
            for e in range(2 * ATT_NB):
                s_c, s_p = _att_mask(s_c_all[e], s_p_all[e], has[e // 2])
                m = jnp.maximum(jnp.max(s_c, axis=1, keepdims=True), jnp.max(s_p, axis=1, keepdims=True))
                pc, pp = jnp.exp(s_c - m), jnp.exp(s_p - m)
                den.append(jnp.sum(pc, axis=1, keepdims=True) + jnp.sum(pp, axis=1, keepdims=True))
                lse.append(jnp.broadcast_to(m + jnp.log(den[e]), (QBLK, HEAD)))
                p_c.append(pc)
                p_p.append(pp)
            num = _dot16(_stack(p_c), vc, "nn") + _dot16(_stack(p_p), vp, "nn")
            for bb, (cur, _) in enumerate(rows):
                o_ref[cur, :] = jnp.concatenate([num[2 * bb] / den[2 * bb], num[2 * bb + 1] / den[2 * bb + 1]], axis=1)
                l_ref[cur, :] = jnp.concatenate(lse[2 * bb:2 * bb + 2], axis=1)
            return carry

        lax.fori_loop(0, S // QBLK // ATT_NB, group, 0)

    slab = lambda j: pl.BlockSpec((S, LANES), lambda i: (0, _att_slab(g, j) + i))
    out = pl.BlockSpec((S, LANES), lambda i: (0, i))
    shp = jax.ShapeDtypeStruct((S, ATT_W), F32)
    return pl.pallas_call(body, name=f"att_fwd_g{g}", grid=(ATT_PAIRS,), in_specs=[slab(0), slab(1), slab(2)],
                          out_specs=[out, out], out_shape=[shp, shp], compiler_params=_cparams(("parallel",)))(P, P, P)


def _att_bwd(P, o, l, do, dl, g):
    S = P.shape[0]
    d = ATT_PATTERNS[g][1]

    def body(q_ref, k_ref, v_ref, o_ref, l_ref, do_ref, dl_ref, dq_ref, dk_ref, dv_ref, dq_acc, dk_acc, dv_acc):
        dk_acc[...] = jnp.zeros_like(dk_acc)
        dv_acc[...] = jnp.zeros_like(dv_acc)

        def group(i, carry):
            rows, has, (q, kc, kp, vc, vp, dob) = _att_operands(
                i, d, S, (q_ref, True), (k_ref, True), (k_ref, False), (v_ref, True), (v_ref, False), (do_ref, True))
            s_c_all, s_p_all = _dot16(q, kc, "nt"), _dot16(q, kp, "nt")
            dp_c_all, dp_p_all = _dot16(dob, vc, "nt"), _dot16(dob, vp, "nt")
            p_c, p_p, ds_c, ds_p = [], [], [], []
            for bb, (cur, _) in enumerate(rows):
                dd2 = do_ref[cur, :] * o_ref[cur, :] - dl_ref[cur, :]
                for h, (dd, lse) in enumerate(zip(_heads(dd2), _heads(l_ref[cur, :]))):
                    e = 2 * bb + h
                    s_c, s_p = _att_mask(s_c_all[e], s_p_all[e], has[bb])
                    pc, pp = jnp.exp(s_c - lse[:, 0:1]), jnp.exp(s_p - lse[:, 0:1])
                    delta = jnp.sum(dd, axis=1, keepdims=True)
                    p_c.append(pc)
                    p_p.append(pp)
                    ds_c.append(pc * (dp_c_all[e] - delta) * ATT_SCALE)
                    ds_p.append(pp * (dp_p_all[e] - delta) * ATT_SCALE)
            p_c, p_p, ds_c, ds_p = map(_stack, (p_c, p_p, ds_c, ds_p))
            dq = _dot16(ds_c, kc, "nn") + _dot16(ds_p, kp, "nn")
            dk_c, dk_p = _dot16(ds_c, q, "tn"), _dot16(ds_p, q, "tn")
            dv_c, dv_p = _dot16(p_c, dob, "tn"), _dot16(p_p, dob, "tn")
            pair = lambda x, bb: jnp.concatenate([x[2 * bb], x[2 * bb + 1]], axis=1)
            for bb, (cur, prv) in enumerate(rows):
                dq_acc[cur, :] = pair(dq, bb)
                dk_acc[cur, :] += pair(dk_c, bb)
                dv_acc[cur, :] += pair(dv_c, bb)
                dk_acc[prv, :] += pair(dk_p, bb)
                dv_acc[prv, :] += pair(dv_p, bb)
            return carry

        lax.fori_loop(0, S // QBLK // ATT_NB, group, 0)
        dq_ref[...] = dq_acc[...].astype(BF16)
        dk_ref[...] = dk_acc[...].astype(BF16)
        dv_ref[...] = dv_acc[...].astype(BF16)

    slab = lambda j: pl.BlockSpec((S, LANES), lambda i: (0, _att_slab(g, j) + i))
    blk128 = pl.BlockSpec((S, LANES), lambda i: (0, i))
    shp = jax.ShapeDtypeStruct((S, ATT_W), BF16)
    return pl.pallas_call(body, name=f"att_bwd_g{g}", grid=(ATT_PAIRS,),
                          in_specs=[slab(0), slab(1), slab(2)] + [blk128] * 4, out_specs=[blk128] * 3, out_shape=[shp] * 3,
                          scratch_shapes=[pltpu.VMEM((S, LANES), F32)] * 3,
                          compiler_params=_cparams(("parallel",)))(P, P, P, o, l, do, dl)


def _att_weights(l_refs):
    l0, l1, l2 = [r[...] for r in l_refs]
    m = jnp.maximum(jnp.maximum(l0, l1), l2)
    e = (jnp.exp(l0 - m), jnp.exp(l1 - m), jnp.exp(l2 - m))
    inv = 1.0 / (e[0] + e[1] + e[2])
    return [x * inv for x in e]


def _att_combine_fwd(os, ls, tm=512):
    S = os[0].shape[0]

    def body(o0, o1, o2, l0, l1, l2, a_ref):
        w = _att_weights((l0, l1, l2))
        a_ref[...] = (w[0] * o0[...] + w[1] * o1[...] + w[2] * o2[...]).astype(BF16)

    row = _rows(tm, ATT_W)
    return pl.pallas_call(body, name="att_combine_fwd", grid=(S // tm,), in_specs=[row] * 6, out_specs=row,
                          out_shape=jax.ShapeDtypeStruct((S, ATT_W), BF16),
                          compiler_params=_cparams(("parallel",)))(*os, *ls)


def _att_combine_bwd(da, os, ls, tm=512):
    S = da.shape[0]

    def body(da_ref, o0, o1, o2, l0, l1, l2, *out_refs):
        da = da_ref[...]
        w = _att_weights((l0, l1, l2))
        dw = (da * o0[...], da * o1[...], da * o2[...])
        mean = w[0] * dw[0] + w[1] * dw[1] + w[2] * dw[2]
        for g in range(3):
            out_refs[g][...] = w[g] * da
            out_refs[3 + g][...] = w[g] * (dw[g] - mean)

    row = _rows(tm, ATT_W)
    shp = jax.ShapeDtypeStruct((S, ATT_W), F32)
    return pl.pallas_call(body, name="att_combine_bwd", grid=(S // tm,), in_specs=[row] * 7, out_specs=[row] * 6,
                          out_shape=[shp] * 6, compiler_params=_cparams(("parallel",)))(da, *os, *ls)


@jax.custom_vjp
def _bdot(a, b):
    return jnp.dot(a.astype(BF16), b.astype(BF16), preferred_element_type=F32)


def _bdot_fwd(a, b):
    return _bdot(a, b), (a, b)


def _bdot_bwd(res, ct):
    a, b = res
    ct16 = ct.astype(BF16)
    da = lax.dot_general(ct16, b.astype(BF16), (((1,), (1,)), ((), ())), preferred_element_type=F32)
    db = lax.dot_general(a.astype(BF16), ct16, (((0,), (0,)), ((), ())), preferred_element_type=F32)
    return da, db


_bdot.defvjp(_bdot_fwd, _bdot_bwd)


def _two_piece_dot(x, m):
    hi = x.astype(BF16)
    lo = (x - hi.astype(F32)).astype(BF16)
    return jnp.dot(hi, m, preferred_element_type=F32) + jnp.dot(lo, m, preferred_element_type=F32)


def _head_sum_impl(x):
    sel = (lax.broadcasted_iota(jnp.int32, (D, LANES), 0) // HEAD == lax.broadcasted_iota(jnp.int32, (D, LANES), 1)).astype(BF16)
    sel_t = (lax.broadcasted_iota(jnp.int32, (LANES, D), 1) // HEAD == lax.broadcasted_iota(jnp.int32, (LANES, D), 0)).astype(BF16)
    return _two_piece_dot(_two_piece_dot(x, sel), sel_t)


@jax.custom_vjp
def _head_sum(x):
    return _head_sum_impl(x)


_head_sum.defvjp(lambda x: (_head_sum_impl(x), None), lambda _, ct: (_head_sum_impl(ct),))


def _softplus(z):
    return jnp.maximum(z, 0.0) + jnp.log(1.0 + jnp.exp(-jnp.abs(z)))


def _rwkv_prep_fn(zr, zrp, zk, zkp, zv, zvp, zl, zlp, mu_r, mu_k, mu_v, mu_l, w0, a0, k_k, k_a, w2, a2, g2p):
    r = zr + (zrp - zr) * mu_r
    k = zk + (zkp - zk) * mu_k
    v = zv + (zvp - zv) * mu_v
    lo = zl + (zlp - zl) * mu_l
    w_low, a_low, g_low = lo[:, 0:LORA_W], lo[:, LORA_W:LORA_W + LORA_A], lo[:, LANES:LANES + G_PAD]
    w_log = -_softplus(-(w0 + _bdot(jnp.tanh(w_low), w2))) - 0.5
    decay = -jnp.exp(w_log)
    a = jax.nn.sigmoid(a0 + _bdot(a_low, a2))
    g = _bdot(jax.nn.sigmoid(g_low), g2p)
    kmod = k * (1.0 + (a - 1.0) * k_a)
    kk = k * k_k
    kk = kk / jnp.maximum(jnp.sqrt(_head_sum(kk * kk)), 1e-12)
    return r, decay, kmod, v, -kk, kk * a, g


def _rwkv_prep_specs(tm, blk=lambda i: i):
    vec = _full((1, D))
    rows = lambda w, col: pl.BlockSpec((tm, w), lambda i: (blk(i), col))
    prev = lambda w, col: pl.BlockSpec((SUBLANES, w), lambda i: (jnp.maximum(blk(i) * (tm // SUBLANES) - 1, 0), col))
    slabs = []
    for col in (C_R // D, C_K // D, C_V // D):
        slabs += [rows(D, col), prev(D, col)]
    slabs += [rows(LORA_PAD, C_LORA // LORA_PAD), prev(LORA_PAD, C_LORA // LORA_PAD)]
    params = [vec, vec, vec, _full((1, LORA_PAD)), vec, vec, vec, vec,
              _full((LORA_W, D)), _full((LORA_A, D)), _full((G_PAD, D))]
    return slabs, params


def _prep_inputs(refs, first):
    vals = []
    for s in range(4):
        z = refs[2 * s][...]
        vals += [z, _shift_down(z, refs[2 * s + 1][...], 1, first)]
    return vals + [r[...] for r in refs[8:19]]


def _rwkv_prep(P, params, tm=256):
    S = P.shape[0]
    slabs, pspecs = _rwkv_prep_specs(tm)

    def body(*refs):
        outs = _rwkv_prep_fn(*_prep_inputs(refs, pl.program_id(0) == 0))
        for o_ref, val in zip(refs[19:], outs):
            o_ref[...] = val

    shp = jax.ShapeDtypeStruct((S, D), F32)
    return pl.pallas_call(body, name="rwkv_prep", grid=(S // tm,), in_specs=slabs + pspecs,
                          out_specs=[_rows(tm, D)] * 7, out_shape=[shp] * 7,
                          compiler_params=_cparams(("parallel",)))(*([P] * 8), *params)


def _rwkv_prep_bwd(P, params, cts_a, cts_b, tm=128):
    S = P.shape[0]
    nblk = S // tm
    blk = lambda i: nblk - 1 - i
    slabs, pspecs = _rwkv_prep_specs(tm, blk)
    has_b = [c is not None for c in cts_b]
    n_ct = 7 + sum(has_b)

    def body(*refs):
        start = pl.program_id(0) == 0
        ins = _prep_inputs(refs, pl.program_id(0) == nblk - 1)
        ct_refs = refs[19:19 + n_ct]
        out_refs = refs[19 + n_ct:19 + n_ct + 15]
        carry_refs = refs[19 + n_ct + 15:]

        @pl.when(start)
        def _():
            for c_ref in carry_refs:
                c_ref[...] = jnp.zeros_like(c_ref)

        cts, pos = [], 7
        for i in range(7):
            c = ct_refs[i][...]
            if has_b[i]:
                c = c + ct_refs[pos][...]
                pos += 1
            cts.append(c)
        _, vjp = jax.vjp(_rwkv_prep_fn, *ins)
        grads = vjp(tuple(cts))
        for s in range(4):
            shifted = grads[2 * s + 1]
            out_refs[s][...] = (grads[2 * s] + _shift_up(shifted, carry_refs[s][...], 1, start)).astype(BF16)
            carry_refs[s][0:1, :] = shifted[0:1, :]
        for i in range(11):
            _acc(out_refs[4 + i], grads[8 + i], start)

    ct_in = list(cts_a) + [c for c in cts_b if c is not None]
    row = lambda w: pl.BlockSpec((tm, w), lambda i: (blk(i), 0))
    f = jax.ShapeDtypeStruct
    zshapes = [f((S, D), BF16)] * 3 + [f((S, LORA_PAD), BF16)]
    pshapes = [f((1, D), F32)] * 3 + [f((1, LORA_PAD), F32)] + [f((1, D), F32)] * 4 + [f((LORA_W, D), F32), f((LORA_A, D), F32), f((G_PAD, D), F32)]
    return pl.pallas_call(
        body, name="rwkv_prep_bwd", grid=(nblk,),
        in_specs=slabs + pspecs + [row(D)] * n_ct,
        out_specs=[row(D), row(D), row(D), row(LORA_PAD)] + pspecs,
        out_shape=zshapes + pshapes,
        scratch_shapes=[pltpu.VMEM((SUBLANES, D), F32)] * 3 + [pltpu.VMEM((SUBLANES, LORA_PAD), F32)],
        compiler_params=_cparams(("arbitrary",)))(*([P] * 8), *params, *ct_in)


def _rwkv_post_fn(y, r, kmod, v, g, lnx_w, lnx_b, r_k):
    mean = _head_sum(y) * (1.0 / HEAD)
    yc = y - mean
    var = _head_sum(yc * yc) * (1.0 / HEAD)
    yn = yc * lax.rsqrt(var + GN_EPS) * lnx_w + lnx_b
    bonus = _head_sum(r * kmod * r_k) * v
    return (yn + bonus) * g


def _rwkv_post(y, r, kmod, v, g, lnx_w, lnx_b, r_k, tm=256):
    S = y.shape[0]

    def body(y_ref, r_ref, k_ref, v_ref, g_ref, w_ref, b_ref, rk_ref, o_ref):
        o_ref[...] = _rwkv_post_fn(y_ref[...], r_ref[...], k_ref[...], v_ref[...], g_ref[...],
                                   w_ref[...], b_ref[...], rk_ref[...]).astype(BF16)

    row, vec = _rows(tm, D), _full((1, D))
    return pl.pallas_call(body, name="rwkv_post", grid=(S // tm,), in_specs=[row] * 5 + [vec] * 3, out_specs=row,
                          out_shape=jax.ShapeDtypeStruct((S, D), BF16),
                          compiler_params=_cparams(("parallel",)))(y, r, kmod, v, g, lnx_w, lnx_b, r_k)


def _rwkv_post_bwd(drw, y, r, kmod, v, g, lnx_w, lnx_b, r_k, tm=256):
    S = y.shape[0]

    def body(d_ref, y_ref, r_ref, k_ref, v_ref, g_ref, w_ref, b_ref, rk_ref, *out_refs):
        first = pl.program_id(0) == 0
        _, vjp = jax.vjp(_rwkv_post_fn, y_ref[...], r_ref[...], k_ref[...], v_ref[...], g_ref[...],
                         w_ref[...], b_ref[...], rk_ref[...])
        grads = vjp(d_ref[...])
        for i in range(5):
            out_refs[i][...] = grads[i]
        for i in range(5, 8):
            _acc(out_refs[i], grads[i], first)

    row, vec = _rows(tm, D), _full((1, D))
    f = jax.ShapeDtypeStruct
    return pl.pallas_call(body, name="rwkv_post_bwd", grid=(S // tm,), in_specs=[row] * 6 + [vec] * 3,
                          out_specs=[row] * 5 + [vec] * 3, out_shape=[f((S, D), F32)] * 5 + [f((1, D), F32)] * 3,
                          compiler_params=_cparams(("arbitrary",)))(drw, y, r, kmod, v, g, lnx_w, lnx_b, r_k)


CHUNK = 64
CHUNK_TB = 256
_DOT_DIMS = {"nn": (((2,), (1,)), ((0,), (0,))), "nt": (((2,), (2,)), ((0,), (0,))), "tn": (((1,), (1,)), ((0,), (0,)))}


def _dot16(x, y, mode):
    return lax.dot_general(x.astype(BF16), y.astype(BF16), _DOT_DIMS[mode], preferred_element_type=F32)


@functools.partial(jax.custom_vjp, nondiff_argnums=(2,))
def _mm16(x, y, mode):
    return _dot16(x, y, mode)


def _mm16_fwd(x, y, mode):
    return _dot16(x, y, mode), (x, y)


def _mm16_bwd(mode, res, ct):
    x, y = res
    if mode == "nn":
        return _dot16(ct, y, "nt"), _dot16(x, ct, "tn")
    if mode == "nt":
        return _dot16(ct, y, "nn"), _dot16(ct, x, "tn")
    return _dot16(y, ct, "nt"), _dot16(x, ct, "nn")


_mm16.defvjp(_mm16_fwd, _mm16_bwd)


def _tri_sum(x, upper):
    T = x.shape[0]
    i = lax.broadcasted_iota(jnp.int32, (T, T), 0)
    j = lax.broadcasted_iota(jnp.int32, (T, T), 1)
    tri = ((j >= i) if upper else (i >= j)).astype(BF16)
    out, rest = None, x
    for _ in range(3):
        piece = rest.astype(BF16)
        rest = rest - piece.astype(F32)
        part = jnp.dot(tri, piece, preferred_element_type=F32)
        out = part if out is None else out + part
    return out


@jax.custom_vjp
def _cumsum_rows(x):
    return _tri_sum(x, False)


_cumsum_rows.defvjp(lambda x: (_tri_sum(x, False), None), lambda _, ct: (_tri_sum(ct, True),))


def _rows_to_cols(x):
    H, _, K = x.shape
    eye = (lax.broadcasted_iota(jnp.int32, (H, K, K), 1) == lax.broadcasted_iota(jnp.int32, (H, K, K), 2)).astype(F32)
    out = lax.dot_general(eye, jnp.broadcast_to(x, (H, SUBLANES, K)), _DOT_DIMS["nt"],
                          precision=lax.Precision.HIGHEST, preferred_element_type=F32)
    return out[:, :, 0:1]


def _per_head(x):
    return jnp.concatenate([x[:, h * HEAD:(h + 1) * HEAD][None] for h in range(N_HEADS)], axis=0)


def _chunk_fn(st0, r, lw, k, v, a, b):
    T = r.shape[0]
    cl = _cumsum_rows(lw)
    cl_end = cl[T - 1:T, :]
    inv = jnp.exp(-cl)
    to_end = jnp.exp(cl_end - cl)
    ah, rh, bh, kh, be, ke, v3 = [_per_head(x) for x in
                                  (a * jnp.exp(cl - lw), r * jnp.exp(cl), b * inv, k * inv, b * to_end, k * to_end, v)]
    i = lax.broadcasted_iota(jnp.int32, (N_HEADS, T, T), 1)
    j = lax.broadcasted_iota(jnp.int32, (N_HEADS, T, T), 2)
    a_ab = jnp.where(i > j, _mm16(ah, bh, "nt"), 0.0)
    a_ak = jnp.where(i > j, _mm16(ah, kh, "nt"), 0.0)
    m_rb = jnp.where(i >= j, _mm16(rh, bh, "nt"), 0.0)
    m_rk = jnp.where(i >= j, _mm16(rh, kh, "nt"), 0.0)
    rhs = _mm16(ah, st0, "nn") + _mm16(a_ak, v3, "nn")
    power, solve, n = a_ab, (i == j).astype(F32) + a_ab, 1
    while 2 * n < T:
        power = _mm16(power, power, "nn")
        solve = solve + _mm16(solve, power, "nn")
        n *= 2
    sa = _mm16(solve, rhs, "nn")
    y3 = _mm16(rh, st0, "nn") + _mm16(m_rb, sa, "nn") + _mm16(m_rk, v3, "nn")
    st_end = _rows_to_cols(_per_head(jnp.exp(cl_end))) * st0 + _mm16(be, sa, "tn") + _mm16(ke, v3, "tn")
    return jnp.concatenate([y3[h] for h in range(N_HEADS)], axis=1), st_end


def _hosted_exchange(refs, n, broadcast, grid):
    if n == 0:
        return lambda: None
    start, wait = _exchange_ops(refs[:n], refs[n:2 * n], *refs[2 * n:], broadcast)
    first = functools.reduce(jnp.logical_and, [pl.program_id(a) == 0 for a in range(len(grid))])
    last = functools.reduce(jnp.logical_and, [pl.program_id(a) == g - 1 for a, g in enumerate(grid)])
    pl.when(first)(start)
    return lambda: pl.when(last)(wait)


def _cscan_fwd(r, lw, k, v, a, b, gather=()):
    S = r.shape[0]
    per_blk = CHUNK_TB // CHUNK
    n_x = len(gather)
    nblk = S // CHUNK_TB

    def body(*refs):
        r_ref, lw_ref, k_ref, v_ref, a_ref, b_ref = refs[:6]
        y_ref, ck_ref = refs[6 + n_x:8 + n_x]
        st_ref = refs[8 + 2 * n_x]
        finish = _hosted_exchange(refs[6:6 + n_x] + refs[8 + n_x:8 + 2 * n_x] + refs[9 + 2 * n_x:], n_x, True, (nblk,))

        @pl.when(pl.program_id(0) == 0)
        def _():
            st_ref[...] = jnp.zeros_like(st_ref)

        def chunk(c, carry):
            rows = pl.ds(pl.multiple_of(c * CHUNK, CHUNK), CHUNK)
            st0 = st_ref[...]
            ck_ref[c] = st0
            y, st_end = _chunk_fn(st0, r_ref[rows, :], lw_ref[rows, :], k_ref[rows, :],
                                  v_ref[rows, :], a_ref[rows, :], b_ref[rows, :])
            y_ref[rows, :] = y
            st_ref[...] = st_end
            return carry

        lax.fori_loop(0, per_blk, chunk, 0)
        finish()

    blk = _rows(CHUNK_TB, D)
    any_spec = pl.BlockSpec(memory_space=pl.ANY)
    outs = pl.pallas_call(
        body, name="scan_fwd", grid=(nblk,), in_specs=[blk] * 6 + [any_spec] * n_x,
        out_specs=[blk, pl.BlockSpec((per_blk, N_HEADS, HEAD, HEAD), lambda i: (i, 0, 0, 0))] + [any_spec] * n_x,
        out_shape=[jax.ShapeDtypeStruct((S, D), F32), jax.ShapeDtypeStruct((S // CHUNK, N_HEADS, HEAD, HEAD), F32)]
        + _exchange_shapes(gather, True),
        scratch_shapes=[pltpu.VMEM((N_HEADS, HEAD, HEAD), F32)] + (_exchange_scratch(n_x) if n_x else []),
        compiler_params=_cparams(("arbitrary",)))(r, lw, k, v, a, b, *gather)
    return outs[0], outs[1], outs[2:]


def _cscan_bwd(r, lw, k, v, a, b, ckpt, dy, scatter=()):
    S = r.shape[0]
    per_blk = CHUNK_TB // CHUNK
    nblk = S // CHUNK_TB
    n_x = len(scatter)

    def body(*refs):
        r_ref, lw_ref, k_ref, v_ref, a_ref, b_ref, ck_ref, dy_ref = refs[:8]
        out_refs = refs[8 + n_x:14 + n_x]
        ds_ref = refs[14 + 2 * n_x]
        finish = _hosted_exchange(refs[8:8 + n_x] + refs[14 + n_x:14 + 2 * n_x] + refs[15 + 2 * n_x:], n_x, False, (nblk,))

        @pl.when(pl.program_id(0) == 0)
        def _():
            ds_ref[...] = jnp.zeros_like(ds_ref)

        def chunk(cc, carry):
            c = per_blk - 1 - cc
            rows = pl.ds(pl.multiple_of(c * CHUNK, CHUNK), CHUNK)
            ins = (ck_ref[c], r_ref[rows, :], lw_ref[rows, :], k_ref[rows, :], v_ref[rows, :], a_ref[rows, :], b_ref[rows, :])
            _, vjp = jax.vjp(_chunk_fn, *ins)
            grads = vjp((dy_ref[rows, :], ds_ref[...]))
            ds_ref[...] = grads[0]
            for o_ref, g in zip(out_refs, grads[1:]):
                o_ref[rows, :] = g
            return carry

        lax.fori_loop(0, per_blk, chunk, 0)
        finish()

    blk = pl.BlockSpec((CHUNK_TB, D), lambda i: (nblk - 1 - i, 0))
    any_spec = pl.BlockSpec(memory_space=pl.ANY)
    shp = jax.ShapeDtypeStruct((S, D), F32)
    outs = pl.pallas_call(
        body, name="scan_bwd", grid=(nblk,),
        in_specs=[blk] * 6 + [pl.BlockSpec((per_blk, N_HEADS, HEAD, HEAD), lambda i: (nblk - 1 - i, 0, 0, 0)), blk]
        + [any_spec] * n_x,
        out_specs=[blk] * 6 + [any_spec] * n_x, out_shape=[shp] * 6 + _exchange_shapes(scatter, False),
        scratch_shapes=[pltpu.VMEM((N_HEADS, HEAD, HEAD), F32)] + (_exchange_scratch(n_x) if n_x else []),
        compiler_params=_cparams(("arbitrary",)))(r, lw, k, v, a, b, ckpt, dy, *scatter)
    return outs[:6], outs[6:]


def _ada_partial(c_all, w_shard):
    def body(c_ref, w_ref, o_ref):
        o_ref[...] = jnp.dot(c_ref[...].astype(BF16), w_ref[...].astype(BF16), preferred_element_type=F32)

    vm = pl.BlockSpec(memory_space=pltpu.VMEM)
    return pl.pallas_call(body, name="ada_partial", in_specs=[vm, vm], out_specs=vm,
                          out_shape=jax.ShapeDtypeStruct((N_DEV, w_shard.shape[1]), F32),
                          compiler_params=pltpu.CompilerParams(vmem_limit_bytes=VMEM_LIMIT))(c_all, w_shard)


def _ada_bias(rows, b_ada):
    def body(r_ref, b_ref, o_ref):
        o_ref[...] = r_ref[...] + b_ref[...]

    vm = pl.BlockSpec(memory_space=pltpu.VMEM)
    return pl.pallas_call(body, name="ada_bias", in_specs=[vm, vm], out_specs=vm,
                          out_shape=jax.ShapeDtypeStruct(rows.shape, F32))(rows, b_ada)


def _ada_wgrad(c_cols, d_all):
    def body(c_ref, d_ref, o_ref):
        acc = c_ref[:, 0:1] * d_ref[0:1, :]
        for j in range(1, N_DEV):
            acc = acc + c_ref[:, j:j + 1] * d_ref[j:j + 1, :]
        o_ref[...] = acc

    vm = pl.BlockSpec(memory_space=pltpu.VMEM)
    return pl.pallas_call(body, name="ada_wgrad", in_specs=[vm, vm], out_specs=vm,
                          out_shape=jax.ShapeDtypeStruct((D, d_all.shape[1]), F32),
                          compiler_params=pltpu.CompilerParams(vmem_limit_bytes=VMEM_LIMIT))(c_cols, d_all)


def _exchange(srcs, broadcast, name):
    n = len(srcs)

    def body(*refs):
        start, wait = _exchange_ops(refs[:n], refs[n:2 * n], *refs[2 * n:], broadcast)
        start()
        wait()

    any_spec = pl.BlockSpec(memory_space=pl.ANY)
    return pl.pallas_call(
        body, name=name, out_shape=_exchange_shapes(srcs, broadcast), in_specs=[any_spec] * n, out_specs=[any_spec] * n,
        scratch_shapes=_exchange_scratch(n),
        compiler_params=pltpu.CompilerParams(has_side_effects=True),
    )(*srcs)


def _gather_via_sibling(srcs, name):
    n = len(srcs)

    def body(*refs):
        src_refs, out_refs = refs[:n], refs[n:2 * n]
        send_sems, recv_sems, local_sems = refs[2 * n:]
        x, y, c = lax.axis_index("x"), lax.axis_index("y"), lax.axis_index("c")
        me, sibling = (x, y, c), (x, y, 1 - c)
        chips = [(1 - x, y), (x, 1 - y), (1 - x, 1 - y)]

        def slot(px, py, pc):
            return 4 * px + 2 * py + pc

        def copy(i, k, block, to, src=None):
            rows = out_refs[i].at[slot(*block)]
            return pltpu.make_async_remote_copy(
                src_ref=rows if src is None else src, dst_ref=rows, send_sem=send_sems.at[i, k],
                recv_sem=recv_sems.at[i, k], device_id=to, device_id_type=_MESH)

        local = [pltpu.make_async_copy(src_refs[i], out_refs[i].at[slot(*me)], local_sems.at[i]) for i in range(n)]
        for cp in local:
            cp.start()
        first = [copy(i, 0, me, sibling, src=src_refs[i]) for i in range(n)]
        first += [copy(i, 1 + j, me, (*chip, c), src=src_refs[i]) for j, chip in enumerate(chips) for i in range(n)]
        for cp in first:
            cp.start()
        passed = []
        for j, chip in enumerate(chips):
            for i in range(n):
                copy(i, 1 + j, (*chip, c), me).wait_recv()
                passed.append(copy(i, 4 + j, (*chip, c), sibling))
                passed[-1].start()
        for i in range(n):
            copy(i, 0, sibling, me).wait_recv()
            for j, chip in enumerate(chips):
                copy(i, 4 + j, (*chip, 1 - c), me).wait_recv()
        for cp in first + passed:
            cp.wait_send()
        for cp in local:
            cp.wait()

    any_spec = pl.BlockSpec(memory_space=pl.ANY)
    return pl.pallas_call(
        body, name=name, out_shape=_exchange_shapes(srcs, True), in_specs=[any_spec] * n, out_specs=[any_spec] * n,
        scratch_shapes=_exchange_scratch(n),
        compiler_params=pltpu.CompilerParams(has_side_effects=True),
    )(*srcs)


def _flags(broadcast, n):
    return [broadcast] * n if isinstance(broadcast, bool) else list(broadcast)


def _exchange_shapes(srcs, broadcast):
    return [jax.ShapeDtypeStruct((N_DEV,) + (s.shape if bc else s.shape[1:]), s.dtype)
            for s, bc in zip(srcs, _flags(broadcast, len(srcs)))]


def _exchange_scratch(n):
    return [pltpu.SemaphoreType.DMA((n, N_DEV)), pltpu.SemaphoreType.DMA((n, N_DEV)), pltpu.SemaphoreType.DMA((n,))]


def _exchange_ops(src_refs, out_refs, send_sems, recv_sems, local_sems, broadcast):
    n = len(src_refs)
    flags = _flags(broadcast, n)
    x, y, c = lax.axis_index("x"), lax.axis_index("y"), lax.axis_index("c")
    me = 4 * x + 2 * y + c

    def block(i, j):
        return src_refs[i] if flags[i] else src_refs[i].at[j]

    def remote(i, d, src_slot, dst_slot):
        px, py, pc = x ^ (d >> 2), y ^ ((d >> 1) & 1), c ^ (d & 1)
        return pltpu.make_async_remote_copy(
            src_ref=block(i, src_slot), dst_ref=out_refs[i].at[dst_slot], send_sem=send_sems.at[i, d],
            recv_sem=recv_sems.at[i, d], device_id=(px, py, pc), device_id_type=_MESH)

    def local(i):
        return pltpu.make_async_copy(block(i, me), out_refs[i].at[me], local_sems.at[i])

    def start():
        for i in range(n):
            local(i).start()
        for d in range(1, N_DEV):
            for i in range(n):
                remote(i, d, me ^ d, me).start()

    def wait():
        for d in range(1, N_DEV):
            for i in range(n):
                remote(i, d, me, me ^ d).wait_recv()
        for d in range(1, N_DEV):
            for i in range(n):
                remote(i, d, me ^ d, me).wait_send()
        for i in range(n):
            local(i).wait()

    return start, wait


def _adamw(w, g, m, v):
    nm = ADAM_B1 * m + (1.0 - ADAM_B1) * g
    nv = ADAM_B2 * v + (1.0 - ADAM_B2) * (g * g)
    m_hat = nm * (1.0 / (1.0 - ADAM_B1 ** ADAM_STEP))
    v_hat = nv * (1.0 / (1.0 - ADAM_B2 ** ADAM_STEP))
    return -ADAM_LR * (m_hat / (jnp.sqrt(v_hat) + ADAM_EPS) + ADAM_WD * w), nm, nv


def _adam_vectors(parts, ws, ms, vs):
    nv = len(ws)
    sizes = [w.shape[1] for w in ws]

    def body(*refs):
        p_ref = refs[0]
        w_refs, m_refs, v_refs = refs[1:1 + nv], refs[1 + nv:1 + 2 * nv], refs[1 + 2 * nv:1 + 3 * nv]
        out_refs = refs[1 + 3 * nv:]
        g_all = p_ref[0]
        for j in range(1, N_DEV):
            g_all = g_all + p_ref[j]
        off = 0
        for i, n in enumerate(sizes):
            g = g_all[:, off:off + n]
            off += -(-n // LANES) * LANES
            delta, new_m, new_v = _adamw(w_refs[i][...], g, m_refs[i][...], v_refs[i][...])
            for o_ref, val in zip(out_refs[4 * i:4 * i + 4], (g, delta, new_m, new_v)):
                o_ref[...] = val

    vm = pl.BlockSpec(memory_space=pltpu.VMEM)
    outs = pl.pallas_call(body, name="adam_replicated", in_specs=[vm] * (1 + 3 * nv), out_specs=[vm] * (4 * nv),
                          out_shape=[jax.ShapeDtypeStruct((1, n), F32) for n in sizes for _ in range(4)])(parts, *ws, *ms, *vs)
    return [outs[4 * i:4 * i + 4] for i in range(nv)]


def _sum_adam(parts, w, m, v, name):
    n_parts, R, C = parts.shape
    fits = [t for t in range(16, R + 1, 16) if R % t == 0 and t * C <= 2504 * LANES]
    if fits:
        tm, tc = max(fits), C
    elif C % (2 * LANES) == 0 and R * C > 2504 * LANES:
        tm, tc = R, 2 * LANES
    else:
        tm, tc = R, C

    def body(p_ref, w_ref, m_ref, v_ref, g_ref, d_ref, nm_ref, nv_ref):
        g = p_ref[0].astype(F32)
        for j in range(1, n_parts):
            g = g + p_ref[j].astype(F32)
        g_ref[...] = g
        d_ref[...], nm_ref[...], nv_ref[...] = _adamw(w_ref[...], g, m_ref[...], v_ref[...])

    blk = pl.BlockSpec((tm, tc), lambda i, j: (i, j))
    shp = jax.ShapeDtypeStruct((R, C), F32)
    return pl.pallas_call(body, name=name, grid=(R // tm, C // tc),
                          in_specs=[pl.BlockSpec((n_parts, tm, tc), lambda i, j: (0, i, j)), blk, blk, blk],
                          out_specs=[blk] * 4, out_shape=[shp] * 4,
                          compiler_params=_cparams(("parallel", "parallel")))(parts, w, m, v)


TRANSPOSED = ("w_in", "w_up")
SHARDED = (("w_ada", 1), ("w_in", 0), ("w2", 1), ("a2", 1), ("g2", 1), ("w_att_out", 1), ("w_rwkv_out", 0),
           ("w_o", 0), ("w_up", 0), ("conv_w", 1), ("w_down", 0))
EARLY, LATE = SHARDED[1:5], SHARDED[5:]
REPLICATED = ("b_ada", "norm1_w", "b_gate", "mu_shift", "w0", "a0", "k_k", "k_a", "r_k", "lnx_w", "lnx_b",
              "norm2_w", "conv_b", "norm_f_w")
WEIGHTS = ("w_ada", "b_ada", "norm1_w", "w_in", "b_gate", "mu_shift", "w0", "w2", "a0", "a2", "g2", "k_k", "k_a", "r_k",
           "lnx_w", "lnx_b", "w_att_out", "w_rwkv_out", "w_o", "norm2_w", "w_up", "conv_w", "conv_b", "w_down", "norm_f_w")


def _pad_w_in(w_in_t):
    rkv = w_in_t[ATT_IN:ATT_IN + 3 * D]
    lora = w_in_t[ATT_IN + 3 * D:ATT_IN + RWKV_IN]
    gates = w_in_t[ATT_IN + RWKV_IN:]
    att = w_in_t[:ATT_IN]
    lw, la, lg = lora[:LORA_W], lora[LORA_W:LORA_W + LORA_A], lora[LORA_W + LORA_A:]
    zeros = jnp.zeros((LORA_PAD - LANES - LORA_G, w_in_t.shape[1]), w_in_t.dtype)
    return jnp.concatenate([rkv, gates, att, lw, la, lg, zeros], axis=0)


def _unpad_w_in(g):
    att = g[C_ATT:C_ATT + ATT_IN]
    rkv = g[C_R:C_R + 3 * D]
    lora = jnp.concatenate([g[C_LORA:C_LORA + LORA_W + LORA_A], g[C_LORA + LANES:C_LORA + LANES + LORA_G]], axis=0)
    gates = g[C_GA:C_GA + 2 * D]
    return jnp.concatenate([att, rkv, lora, gates], axis=0)


def _pad_mu(mu):
    lo = mu[:, 3 * D:]
    mu_l = jnp.concatenate([lo[:, :LORA_W + LORA_A], lo[:, LORA_W + LORA_A:], jnp.zeros((1, LORA_PAD - LANES - LORA_G), mu.dtype)], axis=1)
    return mu[:, :D], mu[:, D:2 * D], mu[:, 2 * D:3 * D], mu_l


def _local_step(x, ada, W, late_shards, target):
    S = x.shape[0]
    W = dict(W)
    G = {}
    sh1, sc1, gt1, sh2, sc2, gt2 = [ada[:, i * D:(i + 1) * D] for i in range(6)]
    h1, rstd1 = _norm_fwd(x, None, None, W["norm1_w"], sc1, sh1, "norm1_fwd")
    w_in_p = _pad_w_in(W["w_in"])
    P = _mm(h1, w_in_p, "nt", F32, "proj_in")

    mu_r, mu_k, mu_v, mu_l = _pad_mu(W["mu_shift"])
    g2p = jnp.pad(W["g2"], ((0, G_PAD - LORA_G), (0, 0)))
    prep_params = [mu_r, mu_k, mu_v, mu_l, W["w0"], W["a0"], W["k_k"], W["k_a"], W["w2"], W["a2"], g2p]
    r_, dec, kmod, v_, aa, bb, gg = _rwkv_prep(P, prep_params)
    y_scan, states, late = _cscan_fwd(r_, dec, kmod, v_, aa, bb, gather=late_shards)
    W.update({n: _full_weight(g, axis) for (n, axis), g in zip(LATE, late)})

    o_g, l_g = zip(*[_att_fwd(P, g) for g in range(len(ATT_PATTERNS))])
    att = _att_combine_fwd(o_g, l_g)
    y_att = _mm(att, W["w_att_out"], "nn", F32, "att_out")
    r_k = W["r_k"].reshape(1, D)
    rw = _rwkv_post(y_scan, r_, kmod, v_, gg, W["lnx_w"], W["lnx_b"], r_k)
    y_rwkv = _mm(rw, W["w_rwkv_out"], "nn", F32, "rwkv_out")

    bga, bgr = W["b_gate"][:, :D], W["b_gate"][:, D:]
    mix = _gate_fwd(P, bga, bgr, y_att, y_rwkv)
    mo = _mm(mix, W["w_o"], "nn", F32, "mix_out")
    x2, h2, rstd2 = _norm_fwd(x, mo, gt1, W["norm2_w"], sc2, sh2, "norm2_fwd")
    u = _mm(h2, W["w_up"], "nt", F32, "ffn_up")
    conv_w8 = jnp.pad(W["conv_w"], ((0, SUBLANES - 3), (0, 0)))
    act = _conv_fwd(u, conv_w8, W["conv_b"])
    f = _mm(act, W["w_down"], "nn", F32, "ffn_down")
    loss_blk, dx3, df, dgt2, G["norm_f_w"] = _final(x2, f, gt2, W["norm_f_w"], target)
    loss = loss_blk[0, 0]

    dact = _mm(df, W["w_down"], "nt", BF16, "ffn_down_dx")
    G["w_down"] = _mm(act, df, "tn", BF16, "ffn_down_dw")
    duc, dwg, dwv, dbg, dbv = _conv_bwd_a(dact, u, conv_w8, W["conv_b"])
    G["conv_w"] = jnp.concatenate([dwg[0:3], dwv[0:3]], axis=1)
    G["conv_b"] = jnp.concatenate([dbg, dbv], axis=1)
    du = _conv_bwd_b(duc, conv_w8)
    dh2 = _mm(du, W["w_up"], "nn", F32, "ffn_up_dx")
    G["w_up"] = _mm(du, h2, "tn", BF16, "ffn_up_dw")
    dx2, dsh2, dsc2, G["norm2_w"], dmo, dgt1 = _norm_bwd(dh2, x2, rstd2, W["norm2_w"], sc2, dx3, mo, gt1, "norm2_bwd")
    dmix = _mm(dmo, W["w_o"], "nt", F32, "mix_out_dx")
    G["w_o"] = _mm(mix, dmo, "tn", BF16, "mix_out_dw")
    dy_att, dy_rwkv, dpga, dpgr, dbga, dbgr = _gate_bwd(dmix, P, bga, bgr, y_att, y_rwkv)
    G["b_gate"] = jnp.concatenate([dbga, dbgr], axis=1)

    datt = _mm(dy_att, W["w_att_out"], "nt", F32, "att_out_dx")
    G["w_att_out"] = _mm(att, dy_att, "tn", BF16, "att_out_dw")
    dcomb = _att_combine_bwd(datt, o_g, l_g)
    dp_att = []
    for g in range(len(ATT_PATTERNS)):
        dp_att += _att_bwd(P, o_g[g], l_g[g], dcomb[g], dcomb[3 + g], g)

    drw = _mm(dy_rwkv, W["w_rwkv_out"], "nt", F32, "rwkv_out_dx")
    G["w_rwkv_out"] = _mm(rw, dy_rwkv, "tn", BF16, "rwkv_out_dw")
    dy_scan, dr1, dk1, dv1, dgg, G["lnx_w"], G["lnx_b"], drk = _rwkv_post_bwd(drw, y_scan, r_, kmod, v_, gg, W["lnx_w"], W["lnx_b"], r_k)
    G["r_k"] = drk.reshape(W["r_k"].shape)
    late_blocks = [_owner_blocks(G[n], axis) for n, axis in LATE] if late_shards else []
    (dr2, ddec, dk2, dv2, daa, dbb), late_parts = _cscan_bwd(r_, dec, kmod, v_, aa, bb, states, dy_scan, scatter=late_blocks)
    pb = _rwkv_prep_bwd(P, prep_params, [dr2, ddec, dk2, dv2, daa, dbb, dgg], [dr1, None, dk1, dv1, None, None, None])
    dp_rkv, dp_lora, dpar = list(pb[0:3]), pb[3], pb[4:]
    dmu_r, dmu_k, dmu_v, dmu_l, G["w0"], G["a0"], G["k_k"], G["k_a"], G["w2"], G["a2"], dg2p = dpar
    G["g2"] = dg2p[0:LORA_G]
    G["mu_shift"] = jnp.concatenate([dmu_r, dmu_k, dmu_v, dmu_l[:, :LORA_W + LORA_A], dmu_l[:, LANES:LANES + LORA_G]], axis=1)

    dP = jnp.concatenate(dp_rkv + [dpga, dpgr] + dp_att + [dp_lora], axis=1)
    G["w_in"] = _unpad_w_in(_mm(dP, h1, "tn", BF16, "proj_in_dw"))
    if late_shards:
        dh1, (w_in_parts,) = _mm(dP, w_in_p, "nn", F32, "proj_in_dx", scatter=[_owner_blocks(G["w_in"], 0)])
        done = dict(zip([n for n, _ in LATE] + ["w_in"], list(late_parts) + [w_in_parts]))
    else:
        dh1, done = _mm(dP, w_in_p, "nn", F32, "proj_in_dx"), {}
    grad_x, dsh1, dsc1, G["norm1_w"] = _norm_bwd(dh1, x, rstd1, W["norm1_w"], sc1, dx2, None, None, "norm1_bwd")
    dada = jnp.concatenate([dsh1, dsc1, dgt1, dsh2, dsc2, dgt2], axis=1)
    G["b_ada"] = dada
    return loss, grad_x, G, done


def _full_weight(gathered, axis):
    _, rows, cols = gathered.shape
    if axis == 0:
        return gathered.reshape(N_DEV * rows, cols)
    return gathered.transpose(1, 0, 2).reshape(rows, N_DEV * cols)


def _owner_blocks(g, axis):
    rows, cols = g.shape
    g = g.astype(BF16)
    if axis == 0:
        return g.reshape(N_DEV, rows // N_DEV, cols)
    return g.reshape(rows, N_DEV, cols // N_DEV).transpose(1, 0, 2)


def kernel(x, c, w_ada, b_ada, norm1_w, w_in, b_gate, mu_shift, w0, w2, a0, a2, g2, k_k, k_a, r_k, lnx_w, lnx_b, w_att_out, w_rwkv_out, w_o, norm2_w, w_up, conv_w, conv_b, w_down, norm_f_w, loss_target, m_w_ada, m_b_ada, m_norm1_w, m_w_in, m_b_gate, m_mu_shift, m_w0, m_w2, m_a0, m_a2, m_g2, m_k_k, m_k_a, m_r_k, m_lnx_w, m_lnx_b, m_w_att_out, m_w_rwkv_out, m_w_o, m_norm2_w, m_w_up, m_conv_w, m_conv_b, m_w_down, m_norm_f_w, v_w_ada, v_b_ada, v_norm1_w, v_w_in, v_b_gate, v_mu_shift, v_w0, v_w2, v_a0, v_a2, v_g2, v_k_k, v_k_a, v_r_k, v_lnx_w, v_lnx_b, v_w_att_out, v_w_rwkv_out, v_w_o, v_norm2_w, v_w_up, v_conv_w, v_conv_b, v_w_down, v_norm_f_w):
    env = dict(locals())
    w_shard = {n: env[n] for n in WEIGHTS}
    m_shard = {n: env["m_" + n] for n in WEIGHTS}
    v_shard = {n: env["v_" + n] for n in WEIGHTS}

    def mat(shards, n):
        return jnp.swapaxes(shards[n][0], 0, 1) if n in TRANSPOSED else shards[n][0]

    c_all, *gathered = _gather_via_sibling([c] + [mat(w_shard, n).astype(BF16) for n, _ in EARLY], "gather_weights")
    c_all = c_all.reshape(N_DEV, D)
    W = {n: _full_weight(g, axis) for (n, axis), g in zip(EARLY, gathered)}
    for n in REPLICATED:
        W[n] = w_shard[n].reshape(1, -1) if n != "r_k" else w_shard[n][0]
    ada_cols = _ada_partial(c_all, w_shard["w_ada"][0])
    ada_rows, = _exchange([ada_cols[:, None, :]], False, "ada_rows")
    ada = _ada_bias(ada_rows.reshape(1, -1), w_shard["b_ada"])

    late_shards = [mat(w_shard, n).astype(BF16) for n, _ in LATE]
    loss, grad_x, G, parts = _local_step(x[0], ada, W, late_shards, loss_target[0])
    loss = lax.psum(loss, ("x", "y", "c"))

    row = lambda a: a.reshape(1, -1)
    small = jnp.concatenate([jnp.pad(row(G[n]), ((0, 0), (0, (-G[n].size) % LANES))) for n in REPLICATED], axis=1)
    sparts, dada_all = _exchange([small, G["b_ada"].reshape(N_DEV, 1, -1)], [True, False], "gather_small_grads")
    parts["w_ada"] = _ada_wgrad(c_all.T, dada_all.reshape(N_DEV, -1))[None]

    rest = [(n, axis) for n, axis in SHARDED if n not in parts]
    parts.update(zip([n for n, _ in rest], _exchange([_owner_blocks(G[n], axis) for n, axis in rest], False, "scatter_grads")))
    out = {}
    for n, p in parts.items():
        res = _sum_adam(p, mat(w_shard, n), mat(m_shard, n), mat(v_shard, n), "adam_" + n)
        if n in TRANSPOSED:
            res = [jnp.swapaxes(a, 0, 1) for a in res]
        for kind, a in zip(("grad", "delta", "new_m", "new_v"), res):
            out[kind, n] = a[None]

    res = _adam_vectors(sparts, *[[row(s[n]) for n in REPLICATED] for s in (w_shard, m_shard, v_shard)])
    for n, four in zip(REPLICATED, res):
        for kind, a in zip(("grad", "delta", "new_m", "new_v"), four):
            out[kind, n] = a.reshape(w_shard[n].shape)

    return (loss, grad_x[None], *[out[kind, n] for kind in ("grad", "delta", "new_m", "new_v") for n in WEIGHTS])
```

```python
import functools
import math

import jax
import jax.numpy as jnp
from jax import lax
from jax.experimental import pallas as pl
from jax.experimental.pallas import tpu as pltpu

F32 = jnp.float32
BF16 = jnp.bfloat16

D = 1024
HEAD = 64
ATT_PATTERNS = ((128, 1), (512, 4), (2048, 16))
ATT_HEADS = 8
ATT_W = ATT_HEADS * HEAD
ATT_IN = 3 * 3 * ATT_W
QBLK = 128
N_HEADS = D // HEAD
LORA_W, LORA_A, LORA_G = 64, 64, 160
RWKV_IN = 3 * D + LORA_W + LORA_A + LORA_G
N_IN = ATT_IN + RWKV_IN + 2 * D
D_FF = 2816
RMS_EPS = 1e-6
GN_EPS = 64e-5
N_DEV = 8
LANES = 128
SUBLANES = 8

C_R, C_K, C_V, C_GA, C_GR = 0, 1024, 2048, 3072, 4096
C_ATT = 5120
C_LORA = C_ATT + ATT_IN
LORA_PAD = 512
G_PAD = 256
N_PAD = C_LORA + LORA_PAD

ADAM_LR, ADAM_B1, ADAM_B2, ADAM_EPS, ADAM_WD, ADAM_STEP = 0.001, 0.9, 0.999, 1e-08, 0.01, 10

VMEM_LIMIT = 56 * 1024 * 1024

_MESH = pl.DeviceIdType.MESH


def _cparams(sem):
    return pltpu.CompilerParams(dimension_semantics=sem, vmem_limit_bytes=VMEM_LIMIT)


def _tile(dim, pref):
    if dim <= pref:
        return dim
    best = None
    for t in range(LANES, pref + 1, LANES):
        if dim % t == 0:
            best = t
    assert best is not None, dim
    return best


MM_TILES = {"nn": (1024, 1408, 1408), "nt": (1024, 2048, 1408), "tn": (1408, 1408, 1024)}


def _mm(a, b, mode, out_dtype, name, scatter=()):
    if mode == "nn":
        (M, K), (K2, N) = a.shape, b.shape
    elif mode == "nt":
        (M, K), (N, K2) = a.shape, b.shape
    else:
        (K, M), (K2, N) = a.shape, b.shape
    assert K == K2, (a.shape, b.shape, mode)
    tm, tn, tk = (_tile(dim, pref) for dim, pref in zip((M, N, K), MM_TILES[mode]))
    nk = K // tk
    grid = (M // tm, N // tn, nk)
    n_x = len(scatter)
    dims = {"nn": (((1,), (0,)), ((), ())), "nt": (((1,), (1,)), ((), ())), "tn": (((0,), (0,)), ((), ()))}[mode]

    def body(*refs):
        a_ref, b_ref = refs[:2]
        o_ref, acc_ref = refs[2 + n_x], refs[3 + 2 * n_x]
        finish = _hosted_exchange(refs[2:2 + n_x] + refs[3 + n_x:3 + 2 * n_x] + refs[4 + 2 * n_x:], n_x, False, grid)
        k = pl.program_id(2)
        part = lax.dot_general(a_ref[...].astype(BF16), b_ref[...].astype(BF16), dims,
                               preferred_element_type=F32)
        if nk == 1:
            o_ref[...] = part.astype(o_ref.dtype)
        else:
            @pl.when(k == 0)
            def _():
                acc_ref[...] = part

            @pl.when(jnp.logical_and(k > 0, k < nk - 1))
            def _():
                acc_ref[...] += part

            @pl.when(k == nk - 1)
            def _():
                o_ref[...] = (acc_ref[...] + part).astype(o_ref.dtype)
        finish()

    a_spec = pl.BlockSpec((tk, tm), lambda i, j, k: (k, i)) if mode == "tn" else pl.BlockSpec((tm, tk), lambda i, j, k: (i, k))
    b_spec = pl.BlockSpec((tn, tk), lambda i, j, k: (j, k)) if mode == "nt" else pl.BlockSpec((tk, tn), lambda i, j, k: (k, j))
    any_spec = pl.BlockSpec(memory_space=pl.ANY)
    outs = pl.pallas_call(
        body, name=name, grid=grid,
        in_specs=[a_spec, b_spec] + [any_spec] * n_x,
        out_specs=[pl.BlockSpec((tm, tn), lambda i, j, k: (i, j))] + [any_spec] * n_x,
        out_shape=[jax.ShapeDtypeStruct((M, N), out_dtype)] + _exchange_shapes(scatter, False),
        scratch_shapes=[pltpu.VMEM((tm, tn) if nk > 1 else (SUBLANES, LANES), F32)] + (_exchange_scratch(n_x) if n_x else []),
        compiler_params=_cparams(("arbitrary",) * 3 if n_x else ("parallel", "parallel", "arbitrary")),
    )(a, b, *scatter)
    return (outs[0], outs[1:]) if n_x else outs[0]


def _rows(tm, w, col=0):
    return pl.BlockSpec((tm, w), lambda i: (i, col))


def _full(shape):
    return pl.BlockSpec(shape, lambda i: (0,) * len(shape))


def _shift_down(x, halo, k, first):
    rolled = pltpu.roll(x, k, 0)
    row = lax.broadcasted_iota(jnp.int32, x.shape, 0)
    out = rolled
    n_halo = halo.shape[0]
    for j in range(k):
        h = jnp.where(first, 0.0, halo[n_halo - k + j:n_halo - k + j + 1, :])
        out = jnp.where(row == j, h, out)
    return out


def _shift_up(x, halo, k, last):
    n = x.shape[0]
    rolled = pltpu.roll(x, n - k, 0)
    row = lax.broadcasted_iota(jnp.int32, x.shape, 0)
    out = rolled
    for j in range(k):
        h = jnp.where(last, 0.0, halo[j:j + 1, :])
        out = jnp.where(row == n - k + j, h, out)
    return out


def _acc(ref, val, first):
    @pl.when(first)
    def _():
        ref[...] = val

    @pl.when(jnp.logical_not(first))
    def _():
        ref[...] += val


def _colsum(x):
    return jnp.sum(x, axis=0, keepdims=True)


def _norm_fwd(x, mo, gt, nw, sc, sh, name, tm=256):
    S = x.shape[0]
    has_res = mo is not None

    def body(*refs):
        if has_res:
            x_ref, mo_ref, gt_ref, nw_ref, sc_ref, sh_ref, x2_ref, h_ref, rs_ref = refs
            x2 = x_ref[...] + gt_ref[...] * mo_ref[...]
            x2_ref[...] = x2
        else:
            x_ref, nw_ref, sc_ref, sh_ref, h_ref, rs_ref = refs
            x2 = x_ref[...]
        rstd = lax.rsqrt(jnp.mean(x2 * x2, axis=-1, keepdims=True) + RMS_EPS)
        rs_ref[...] = rstd
        h_ref[...] = ((x2 * rstd * nw_ref[...]) * (1.0 + sc_ref[...]) + sh_ref[...]).astype(BF16)

    vec = _full((1, D))
    ins = [x, mo, gt, nw, sc, sh] if has_res else [x, nw, sc, sh]
    in_specs = [_rows(tm, D), _rows(tm, D), vec, vec, vec, vec] if has_res else [_rows(tm, D), vec, vec, vec]
    outs = [jax.ShapeDtypeStruct((S, D), BF16), jax.ShapeDtypeStruct((S, 1), F32)]
    out_specs = [_rows(tm, D), _rows(tm, 1)]
    if has_res:
        outs = [jax.ShapeDtypeStruct((S, D), F32)] + outs
        out_specs = [_rows(tm, D)] + out_specs
    return pl.pallas_call(body, name=name, grid=(S // tm,), in_specs=in_specs, out_specs=out_specs,
                          out_shape=outs, compiler_params=_cparams(("parallel",)))(*ins)


def _norm_bwd(dh, xin, rstd, nw, sc, dres, mo, gt, name, tm=256):
    S = xin.shape[0]
    has_res = mo is not None

    def body(*refs):
        if has_res:
            dh_ref, x_ref, rs_ref, nw_ref, sc_ref, dres_ref, mo_ref, gt_ref, dx_ref, dsh_ref, dsc_ref, dnw_ref, dmo_ref, dgt_ref = refs
        else:
            dh_ref, x_ref, rs_ref, nw_ref, sc_ref, dres_ref, dx_ref, dsh_ref, dsc_ref, dnw_ref = refs
        first = pl.program_id(0) == 0
        dh = dh_ref[...]
        rstd = rs_ref[...]
        n = x_ref[...] * rstd
        w = nw_ref[...]
        _acc(dsh_ref, _colsum(dh), first)
        _acc(dsc_ref, _colsum(dh * (n * w)), first)
        dnw = dh * (1.0 + sc_ref[...])
        _acc(dnw_ref, _colsum(dnw * n), first)
        dn = dnw * w
        dx = dres_ref[...] + rstd * (dn - n * jnp.mean(dn * n, axis=-1, keepdims=True))
        dx_ref[...] = dx
        if has_res:
            dmo_ref[...] = (dx * gt_ref[...]).astype(BF16)
            _acc(dgt_ref, _colsum(dx * mo_ref[...]), first)

    vec = _full((1, D))
    vshape = jax.ShapeDtypeStruct((1, D), F32)
    ins = [dh, xin, rstd, nw, sc, dres] + ([mo, gt] if has_res else [])
    in_specs = [_rows(tm, D), _rows(tm, D), _rows(tm, 1), vec, vec, _rows(tm, D)] + ([_rows(tm, D), vec] if has_res else [])
    outs = [jax.ShapeDtypeStruct((S, D), F32), vshape, vshape, vshape]
    out_specs = [_rows(tm, D), vec, vec, vec]
    if has_res:
        outs += [jax.ShapeDtypeStruct((S, D), BF16), vshape]
        out_specs += [_rows(tm, D), vec]
    return pl.pallas_call(body, name=name, grid=(S // tm,), in_specs=in_specs, out_specs=out_specs,
                          out_shape=outs, compiler_params=_cparams(("arbitrary",)))(*ins)


def _final(x2, f, gt2, nfw, target, tm=256):
    S = x2.shape[0]

    def body(x2_ref, f_ref, gt_ref, w_ref, t_ref, loss_ref, dx_ref, df_ref, dgt_ref, dw_ref):
        first = pl.program_id(0) == 0
        f = f_ref[...]
        gt = gt_ref[...]
        w = w_ref[...]
        x3 = x2_ref[...] + gt * f
        rstd = lax.rsqrt(jnp.mean(x3 * x3, axis=-1, keepdims=True) + RMS_EPS)
        n = x3 * rstd
        e = n * w - t_ref[...]
        part = 0.5 * jnp.sum(jnp.mean(e * e, axis=-1, keepdims=True), axis=0, keepdims=True)
        _acc(loss_ref, jnp.broadcast_to(part, (SUBLANES, LANES)), first)
        dy = e * (1.0 / D)
        _acc(dw_ref, _colsum(dy * n), first)
        dn = dy * w
        dx = rstd * (dn - n * jnp.mean(dn * n, axis=-1, keepdims=True))
        dx_ref[...] = dx
        df_ref[...] = (dx * gt).astype(BF16)
        _acc(dgt_ref, _colsum(dx * f), first)

    vec = _full((1, D))
    vshape = jax.ShapeDtypeStruct((1, D), F32)
    return pl.pallas_call(
        body, name="final_loss", grid=(S // tm,),
        in_specs=[_rows(tm, D), _rows(tm, D), vec, vec, _rows(tm, D)],
        out_specs=[_full((SUBLANES, LANES)), _rows(tm, D), _rows(tm, D), vec, vec],
        out_shape=[jax.ShapeDtypeStruct((SUBLANES, LANES), F32), jax.ShapeDtypeStruct((S, D), F32),
                   jax.ShapeDtypeStruct((S, D), BF16), vshape, vshape],
        compiler_params=_cparams(("arbitrary",)))(x2, f, gt2, nfw, target)


def _gate_fwd(P, bga, bgr, y_att, y_rwkv, tm=256):
    S = P.shape[0]

    def body(pa_ref, pr_ref, ba_ref, br_ref, ya_ref, yr_ref, mix_ref):
        ga = jax.nn.sigmoid(pa_ref[...] + ba_ref[...])
        gr = jax.nn.sigmoid(pr_ref[...] + br_ref[...])
        mix_ref[...] = (ga * ya_ref[...] + gr * yr_ref[...]).astype(BF16)

    vec = _full((1, D))
    return pl.pallas_call(
        body, name="gate_fwd", grid=(S // tm,),
        in_specs=[_rows(tm, D, C_GA // D), _rows(tm, D, C_GR // D), vec, vec, _rows(tm, D), _rows(tm, D)],
        out_specs=_rows(tm, D), out_shape=jax.ShapeDtypeStruct((S, D), BF16),
        compiler_params=_cparams(("parallel",)))(P, P, bga, bgr, y_att, y_rwkv)


def _gate_bwd(dmix, P, bga, bgr, y_att, y_rwkv, tm=256):
    S = P.shape[0]

    def body(dm_ref, pa_ref, pr_ref, ba_ref, br_ref, ya_ref, yr_ref, dya_ref, dyr_ref, dpa_ref, dpr_ref, dba_ref, dbr_ref):
        first = pl.program_id(0) == 0
        dm = dm_ref[...]
        ga = jax.nn.sigmoid(pa_ref[...] + ba_ref[...])
        gr = jax.nn.sigmoid(pr_ref[...] + br_ref[...])
        dya_ref[...] = (dm * ga).astype(BF16)
        dyr_ref[...] = (dm * gr).astype(BF16)
        dpa = dm * ya_ref[...] * ga * (1.0 - ga)
        dpr = dm * yr_ref[...] * gr * (1.0 - gr)
        dpa_ref[...] = dpa.astype(BF16)
        dpr_ref[...] = dpr.astype(BF16)
        _acc(dba_ref, _colsum(dpa), first)
        _acc(dbr_ref, _colsum(dpr), first)

    vec = _full((1, D))
    row = _rows(tm, D)
    rshape = jax.ShapeDtypeStruct((S, D), BF16)
    vshape = jax.ShapeDtypeStruct((1, D), F32)
    return pl.pallas_call(
        body, name="gate_bwd", grid=(S // tm,),
        in_specs=[row, _rows(tm, D, C_GA // D), _rows(tm, D, C_GR // D), vec, vec, row, row],
        out_specs=[row, row, row, row, vec, vec],
        out_shape=[rshape, rshape, rshape, rshape, vshape, vshape],
        compiler_params=_cparams(("arbitrary",)))(dmix, P, P, bga, bgr, y_att, y_rwkv)


CONV_TN = D_FF // 2
HALO = 16


def _conv_fwd(u, conv_w8, conv_b, tm=256, tn=CONV_TN):
    S = u.shape[0]
    nj = D_FF // tn

    def conv(u_ref, h_ref, w_ref, b_ref, first):
        u = u_ref[...].astype(F32)
        h = h_ref[...].astype(F32)
        w = w_ref[...]
        return b_ref[...] + w[0:1] * _shift_down(u, h, 2, first) + w[1:2] * _shift_down(u, h, 1, first) + w[2:3] * u

    def body(ug_ref, hg_ref, uv_ref, hv_ref, wg_ref, wv_ref, bg_ref, bv_ref, act_ref):
        first = pl.program_id(0) == 0
        g = conv(ug_ref, hg_ref, wg_ref, bg_ref, first)
        v = conv(uv_ref, hv_ref, wv_ref, bv_ref, first)
        act_ref[...] = (g * jax.nn.sigmoid(g) * v).astype(BF16)

    blk = lambda off: pl.BlockSpec((tm, tn), lambda i, j: (i, j + off))
    halo = lambda off: pl.BlockSpec((HALO, tn), lambda i, j: (jnp.maximum(i * (tm // HALO) - 1, 0), j + off))
    wsp = lambda off: pl.BlockSpec((SUBLANES, tn), lambda i, j: (0, j + off))
    bsp = lambda off: pl.BlockSpec((1, tn), lambda i, j: (0, j + off))
    return pl.pallas_call(
        body, name="conv_fwd", grid=(S // tm, nj),
        in_specs=[blk(0), halo(0), blk(nj), halo(nj), wsp(0), wsp(nj), bsp(0), bsp(nj)],
        out_specs=pl.BlockSpec((tm, tn), lambda i, j: (i, j)),
        out_shape=jax.ShapeDtypeStruct((S, D_FF), BF16),
        compiler_params=_cparams(("parallel", "parallel")))(u, u, u, u, conv_w8, conv_w8, conv_b, conv_b)


def _conv_bwd_a(dact, u, conv_w8, conv_b, tm=256, tn=CONV_TN):
    S = u.shape[0]
    nj = D_FF // tn

    def half(u_ref, h_ref, w_ref, b_ref, first):
        u = u_ref[...].astype(F32)
        h = h_ref[...].astype(F32)
        w = w_ref[...]
        u2, u1 = _shift_down(u, h, 2, first), _shift_down(u, h, 1, first)
        return b_ref[...] + w[0:1] * u2 + w[1:2] * u1 + w[2:3] * u, (u2, u1, u)

    def wgrad(d, taps):
        z = jnp.zeros((SUBLANES - 3, d.shape[1]), F32)
        return jnp.concatenate([_colsum(d * taps[0]), _colsum(d * taps[1]), _colsum(d * taps[2]), z], axis=0)

    def body(da_ref, ug_ref, hg_ref, uv_ref, hv_ref, wg_ref, wv_ref, bg_ref, bv_ref,
             d_ref, dwg_ref, dwv_ref, dbg_ref, dbv_ref):
        first = pl.program_id(1) == 0
        g, tg = half(ug_ref, hg_ref, wg_ref, bg_ref, first)
        v, tv = half(uv_ref, hv_ref, wv_ref, bv_ref, first)
        da = da_ref[...].astype(F32)
        sg = jax.nn.sigmoid(g)
        dg = da * v * (sg * (1.0 + g * (1.0 - sg)))
        dv = da * (g * sg)
        d_ref[0] = dg.astype(BF16)
        d_ref[1] = dv.astype(BF16)
        _acc(dwg_ref, wgrad(dg, tg), first)
        _acc(dwv_ref, wgrad(dv, tv), first)
        _acc(dbg_ref, _colsum(dg), first)
        _acc(dbv_ref, _colsum(dv), first)

    blk = lambda off: pl.BlockSpec((tm, tn), lambda j, i: (i, j + off))
    halo = lambda off: pl.BlockSpec((HALO, tn), lambda j, i: (jnp.maximum(i * (tm // HALO) - 1, 0), j + off))
    wsp = lambda off: pl.BlockSpec((SUBLANES, tn), lambda j, i: (0, j + off))
    bsp = lambda off: pl.BlockSpec((1, tn), lambda j, i: (0, j + off))
    f = jax.ShapeDtypeStruct
    outs = pl.pallas_call(
        body, name="conv_bwd_a", grid=(nj, S // tm),
        in_specs=[pl.BlockSpec((tm, tn), lambda j, i: (i, j)), blk(0), halo(0), blk(nj), halo(nj), wsp(0), wsp(nj), bsp(0), bsp(nj)],
        out_specs=[pl.BlockSpec((2, tm, tn), lambda j, i: (0, i, j)),
                   pl.BlockSpec((SUBLANES, tn), lambda j, i: (0, j)), pl.BlockSpec((SUBLANES, tn), lambda j, i: (0, j)),
                   pl.BlockSpec((1, tn), lambda j, i: (0, j)), pl.BlockSpec((1, tn), lambda j, i: (0, j))],
        out_shape=[f((2, S, D_FF), BF16), f((SUBLANES, D_FF), F32), f((SUBLANES, D_FF), F32),
                   f((1, D_FF), F32), f((1, D_FF), F32)],
        compiler_params=_cparams(("parallel", "arbitrary")))(dact, u, u, u, u, conv_w8, conv_w8, conv_b, conv_b)
    return outs


def _conv_bwd_b(duc, conv_w8, tm=256, tn=CONV_TN):
    _, S, W = duc.shape
    nj = W // tn
    n_rows = S // tm

    def body(d_ref, h_ref, w_ref, o_ref):
        last = pl.program_id(0) == n_rows - 1
        d = d_ref[...].astype(F32)
        h = h_ref[...].astype(F32)
        w = w_ref[...]
        o_ref[...] = (w[2:3] * d + w[1:2] * _shift_up(d, h, 1, last) + w[0:1] * _shift_up(d, h, 2, last)).astype(BF16)

    last_tile = S // HALO - 1
    return pl.pallas_call(
        body, name="conv_bwd_b", grid=(n_rows, 2 * nj),
        in_specs=[pl.BlockSpec((None, tm, tn), lambda i, j: (j // nj, i, j % nj)),
                  pl.BlockSpec((None, HALO, tn), lambda i, j: (j // nj, jnp.minimum((i + 1) * (tm // HALO), last_tile), j % nj)),
                  pl.BlockSpec((SUBLANES, tn), lambda i, j: (0, j))],
        out_specs=pl.BlockSpec((tm, tn), lambda i, j: (i, j)),
        out_shape=jax.ShapeDtypeStruct((S, 2 * W), BF16),
        compiler_params=_cparams(("parallel", "parallel")))(duc, duc, conv_w8)


ATT_SCALE = HEAD ** -0.5
NEG = -1e30
ATT_PAIRS = ATT_HEADS // 2


def _att_rows(n, d, S):
    per = S // (QBLK * d)
    r, m = n // per, n % per
    cur = pl.ds(m * (QBLK * d) + r, QBLK, stride=d)
    prv = pl.ds(jnp.maximum(m - 1, 0) * (QBLK * d) + r, QBLK, stride=d)
    return cur, prv, m > 0


def _att_slab(g, j):
    return (C_ATT + g * 3 * ATT_W + j * ATT_W) // LANES


def _heads(x):
    return x[:, 0:HEAD], x[:, HEAD:2 * HEAD]


ATT_NB = 4


def _stack(tiles):
    return jnp.concatenate([t[None] for t in tiles], axis=0)


def _att_operands(i, d, S, *sources):
    rows, has = [], []
    tiles = [[] for _ in sources]
    for bb in range(ATT_NB):
        cur, prv, has_prev = _att_rows(i * ATT_NB + bb, d, S)
        rows.append((cur, prv))
        has.append(has_prev)
        for t, (ref, use_cur) in zip(tiles, sources):
            t += _heads(ref[cur if use_cur else prv, :].astype(BF16))
    return rows, has, [_stack(t) for t in tiles]


def _att_mask(s_c, s_p, has_prev):
    qi = lax.broadcasted_iota(jnp.int32, (QBLK, QBLK), 0)
    kj = lax.broadcasted_iota(jnp.int32, (QBLK, QBLK), 1)
    s_c = jnp.where(kj <= qi, s_c * ATT_SCALE, NEG)
    s_p = jnp.where(jnp.logical_and(kj >= qi, has_prev), s_p * ATT_SCALE, NEG)
    return s_c, s_p


def _att_fwd(P, g):
    S = P.shape[0]
    d = ATT_PATTERNS[g][1]

    def body(q_ref, k_ref, v_ref, o_ref, l_ref):
        def group(i, carry):
            rows, has, (q, kc, kp, vc, vp) = _att_operands(i, d, S, (q_ref, True), (k_ref, True), (k_ref, False),
                                                           (v_ref, True), (v_ref, False))
            s_c_all, s_p_all = _dot16(q, kc, "nt"), _dot16(q, kp, "nt")
            p_c, p_p, den, lse = [], [], [], []
            for e in range(2 * ATT_NB):
                s_c, s_p = _att_mask(s_c_all[e], s_p_all[e], has[e // 2])
                m = jnp.maximum(jnp.max(s_c, axis=1, keepdims=True), jnp.max(s_p, axis=1, keepdims=True))
                pc, pp = jnp.exp(s_c - m), jnp.exp(s_p - m)
                den.append(jnp.sum(pc, axis=1, keepdims=True) + jnp.sum(pp, axis=1, keepdims=True))
                lse.append(jnp.broadcast_to(m + jnp.log(den[e]), (QBLK, HEAD)))
                p_c.append(pc)
                p_p.append(pp)
            num = _dot16(_stack(p_c), vc, "nn") + _dot16(_stack(p_p), vp, "nn")
            for bb, (cur, _) in enumerate(rows):
                o_ref[cur, :] = jnp.concatenate([num[2 * bb] / den[2 * bb], num[2 * bb + 1] / den[2 * bb + 1]], axis=1)
                l_ref[cur, :] = jnp.concatenate(lse[2 * bb:2 * bb + 2], axis=1)
            return carry

        lax.fori_loop(0, S // QBLK // ATT_NB, group, 0)

    slab = lambda j: pl.BlockSpec((S, LANES), lambda i: (0, _att_slab(g, j) + i))
    out = pl.BlockSpec((S, LANES), lambda i: (0, i))
    shp = jax.ShapeDtypeStruct((S, ATT_W), F32)
    return pl.pallas_call(body, name=f"att_fwd_g{g}", grid=(ATT_PAIRS,), in_specs=[slab(0), slab(1), slab(2)],
                          out_specs=[out, out], out_shape=[shp, shp], compiler_params=_cparams(("parallel",)))(P, P, P)


def _att_bwd(P, o, l, do, dl, g):
    S = P.shape[0]
    d = ATT_PATTERNS[g][1]

    def body(q_ref, k_ref, v_ref, o_ref, l_ref, do_ref, dl_ref, dq_ref, dk_ref, dv_ref, dq_acc, dk_acc, dv_acc):
        dk_acc[...] = jnp.zeros_like(dk_acc)
        dv_acc[...] = jnp.zeros_like(dv_acc)

        def group(i, carry):
            rows, has, (q, kc, kp, vc, vp, dob) = _att_operands(
                i, d, S, (q_ref, True), (k_ref, True), (k_ref, False), (v_ref, True), (v_ref, False), (do_ref, True))
            s_c_all, s_p_all = _dot16(q, kc, "nt"), _dot16(q, kp, "nt")
            dp_c_all, dp_p_all = _dot16(dob, vc, "nt"), _dot16(dob, vp, "nt")
            p_c, p_p, ds_c, ds_p = [], [], [], []
            for bb, (cur, _) in enumerate(rows):
                dd2 = do_ref[cur, :] * o_ref[cur, :] - dl_ref[cur, :]
                for h, (dd, lse) in enumerate(zip(_heads(dd2), _heads(l_ref[cur, :]))):
                    e = 2 * bb + h
                    s_c, s_p = _att_mask(s_c_all[e], s_p_all[e], has[bb])
                    pc, pp = jnp.exp(s_c - lse[:, 0:1]), jnp.exp(s_p - lse[:, 0:1])
                    delta = jnp.sum(dd, axis=1, keepdims=True)
                    p_c.append(pc)
                    p_p.append(pp)
                    ds_c.append(pc * (dp_c_all[e] - delta) * ATT_SCALE)
                    ds_p.append(pp * (dp_p_all[e] - delta) * ATT_SCALE)
            p_c, p_p, ds_c, ds_p = map(_stack, (p_c, p_p, ds_c, ds_p))
            dq = _dot16(ds_c, kc, "nn") + _dot16(ds_p, kp, "nn")
            dk_c, dk_p = _dot16(ds_c, q, "tn"), _dot16(ds_p, q, "tn")
            dv_c, dv_p = _dot16(p_c, dob, "tn"), _dot16(p_p, dob, "tn")
            pair = lambda x, bb: jnp.concatenate([x[2 * bb], x[2 * bb + 1]], axis=1)
            for bb, (cur, prv) in enumerate(rows):
                dq_acc[cur, :] = pair(dq, bb)
                dk_acc[cur, :] += pair(dk_c, bb)
                dv_acc[cur, :] += pair(dv_c, bb)
                dk_acc[prv, :] += pair(dk_p, bb)
                dv_acc[prv, :] += pair(dv_p, bb)
            return carry

        lax.fori_loop(0, S // QBLK // ATT_NB, group, 0)
        dq_ref[...] = dq_acc[...].astype(BF16)
        dk_ref[...] = dk_acc[...].astype(BF16)
        dv_ref[...] = dv_acc[...].astype(BF16)

    slab = lambda j: pl.BlockSpec((S, LANES), lambda i: (0, _att_slab(g, j) + i))
    blk128 = pl.BlockSpec((S, LANES), lambda i: (0, i))
    shp = jax.ShapeDtypeStruct((S, ATT_W), BF16)
    return pl.pallas_call(body, name=f"att_bwd_g{g}", grid=(ATT_PAIRS,),
                          in_specs=[slab(0), slab(1), slab(2)] + [blk128] * 4, out_specs=[blk128] * 3, out_shape=[shp] * 3,
                          scratch_shapes=[pltpu.VMEM((S, LANES), F32)] * 3,
                          compiler_params=_cparams(("parallel",)))(P, P, P, o, l, do, dl)


def _att_weights(l_refs):
    l0, l1, l2 = [r[...] for r in l_refs]
    m = jnp.maximum(jnp.maximum(l0, l1), l2)
    e = (jnp.exp(l0 - m), jnp.exp(l1 - m), jnp.exp(l2 - m))
    inv = 1.0 / (e[0] + e[1] + e[2])
    return [x * inv for x in e]


def _att_combine_fwd(os, ls, tm=512):
    S = os[0].shape[0]

    def body(o0, o1, o2, l0, l1, l2, a_ref):
        w = _att_weights((l0, l1, l2))
        a_ref[...] = (w[0] * o0[...] + w[1] * o1[...] + w[2] * o2[...]).astype(BF16)

    row = _rows(tm, ATT_W)
    return pl.pallas_call(body, name="att_combine_fwd", grid=(S // tm,), in_specs=[row] * 6, out_specs=row,
                          out_shape=jax.ShapeDtypeStruct((S, ATT_W), BF16),
                          compiler_params=_cparams(("parallel",)))(*os, *ls)


def _att_combine_bwd(da, os, ls, tm=512):
    S = da.shape[0]

    def body(da_ref, o0, o1, o2, l0, l1, l2, *out_refs):
        da = da_ref[...]
        w = _att_weights((l0, l1, l2))
        dw = (da * o0[...], da * o1[...], da * o2[...])
        mean = w[0] * dw[0] + w[1] * dw[1] + w[2] * dw[2]
        for g in range(3):
            out_refs[g][...] = w[g] * da
            out_refs[3 + g][...] = w[g] * (dw[g] - mean)

    row = _rows(tm, ATT_W)
    shp = jax.ShapeDtypeStruct((S, ATT_W), F32)
    return pl.pallas_call(body, name="att_combine_bwd", grid=(S // tm,), in_specs=[row] * 7, out_specs=[row] * 6,
                          out_shape=[shp] * 6, compiler_params=_cparams(("parallel",)))(da, *os, *ls)


@jax.custom_vjp
def _bdot(a, b):
    return jnp.dot(a.astype(BF16), b.astype(BF16), preferred_element_type=F32)


def _bdot_fwd(a, b):
    return _bdot(a, b), (a, b)


def _bdot_bwd(res, ct):
    a, b = res
    ct16 = ct.astype(BF16)
    da = lax.dot_general(ct16, b.astype(BF16), (((1,), (1,)), ((), ())), preferred_element_type=F32)
    db = lax.dot_general(a.astype(BF16), ct16, (((0,), (0,)), ((), ())), preferred_element_type=F32)
    return da, db


_bdot.defvjp(_bdot_fwd, _bdot_bwd)


def _two_piece_dot(x, m):
    hi = x.astype(BF16)
    lo = (x - hi.astype(F32)).astype(BF16)
    return jnp.dot(hi, m, preferred_element_type=F32) + jnp.dot(lo, m, preferred_element_type=F32)


def _head_sum_impl(x):
    sel = (lax.broadcasted_iota(jnp.int32, (D, LANES), 0) // HEAD == lax.broadcasted_iota(jnp.int32, (D, LANES), 1)).astype(BF16)
    sel_t = (lax.broadcasted_iota(jnp.int32, (LANES, D), 1) // HEAD == lax.broadcasted_iota(jnp.int32, (LANES, D), 0)).astype(BF16)
    return _two_piece_dot(_two_piece_dot(x, sel), sel_t)


@jax.custom_vjp
def _head_sum(x):
    return _head_sum_impl(x)


_head_sum.defvjp(lambda x: (_head_sum_impl(x), None), lambda _, ct: (_head_sum_impl(ct),))


def _softplus(z):
    return jnp.maximum(z, 0.0) + jnp.log(1.0 + jnp.exp(-jnp.abs(z)))


def _rwkv_prep_fn(zr, zrp, zk, zkp, zv, zvp, zl, zlp, mu_r, mu_k, mu_v, mu_l, w0, a0, k_k, k_a, w2, a2, g2p):
    r = zr + (zrp - zr) * mu_r
    k = zk + (zkp - zk) * mu_k
    v = zv + (zvp - zv) * mu_v
    lo = zl + (zlp - zl) * mu_l
    w_low, a_low, g_low = lo[:, 0:LORA_W], lo[:, LORA_W:LORA_W + LORA_A], lo[:, LANES:LANES + G_PAD]
    w_log = -_softplus(-(w0 + _bdot(jnp.tanh(w_low), w2))) - 0.5
    decay = -jnp.exp(w_log)
    a = jax.nn.sigmoid(a0 + _bdot(a_low, a2))
    g = _bdot(jax.nn.sigmoid(g_low), g2p)
    kmod = k * (1.0 + (a - 1.0) * k_a)
    kk = k * k_k
    kk = kk / jnp.maximum(jnp.sqrt(_head_sum(kk * kk)), 1e-12)
    return r, decay, kmod, v, -kk, kk * a, g


def _rwkv_prep_specs(tm, blk=lambda i: i):
    vec = _full((1, D))
    rows = lambda w, col: pl.BlockSpec((tm, w), lambda i: (blk(i), col))
    prev = lambda w, col: pl.BlockSpec((SUBLANES, w), lambda i: (jnp.maximum(blk(i) * (tm // SUBLANES) - 1, 0), col))
    slabs = []
    for col in (C_R // D, C_K // D, C_V // D):
        slabs += [rows(D, col), prev(D, col)]
    slabs += [rows(LORA_PAD, C_LORA // LORA_PAD), prev(LORA_PAD, C_LORA // LORA_PAD)]
    params = [vec, vec, vec, _full((1, LORA_PAD)), vec, vec, vec, vec,
              _full((LORA_W, D)), _full((LORA_A, D)), _full((G_PAD, D))]
    return slabs, params


def _prep_inputs(refs, first):
    vals = []
    for s in range(4):
        z = refs[2 * s][...]
        vals += [z, _shift_down(z, refs[2 * s + 1][...], 1, first)]
    return vals + [r[...] for r in refs[8:19]]


def _rwkv_prep(P, params, tm=256):
    S = P.shape[0]
    slabs, pspecs = _rwkv_prep_specs(tm)

    def body(*refs):
        outs = _rwkv_prep_fn(*_prep_inputs(refs, pl.program_id(0) == 0))
        for o_ref, val in zip(refs[19:], outs):
            o_ref[...] = val

    shp = jax.ShapeDtypeStruct((S, D), F32)
    return pl.pallas_call(body, name="rwkv_prep", grid=(S // tm,), in_specs=slabs + pspecs,
                          out_specs=[_rows(tm, D)] * 7, out_shape=[shp] * 7,
                          compiler_params=_cparams(("parallel",)))(*([P] * 8), *params)


def _rwkv_prep_bwd(P, params, cts_a, cts_b, tm=128):
    S = P.shape[0]
    nblk = S // tm
    blk = lambda i: nblk - 1 - i
    slabs, pspecs = _rwkv_prep_specs(tm, blk)
    has_b = [c is not None for c in cts_b]
    n_ct = 7 + sum(has_b)

    def body(*refs):
        start = pl.program_id(0) == 0
        ins = _prep_inputs(refs, pl.program_id(0) == nblk - 1)
        ct_refs = refs[19:19 + n_ct]
        out_refs = refs[19 + n_ct:19 + n_ct + 15]
        carry_refs = refs[19 + n_ct + 15:]

        @pl.when(start)
        def _():
            for c_ref in carry_refs:
                c_ref[...] = jnp.zeros_like(c_ref)

        cts, pos = [], 7
        for i in range(7):
            c = ct_refs[i][...]
            if has_b[i]:
                c = c + ct_refs[pos][...]
                pos += 1
            cts.append(c)
        _, vjp = jax.vjp(_rwkv_prep_fn, *ins)
        grads = vjp(tuple(cts))
        for s in range(4):
            shifted = grads[2 * s + 1]
            out_refs[s][...] = (grads[2 * s] + _shift_up(shifted, carry_refs[s][...], 1, start)).astype(BF16)
            carry_refs[s][0:1, :] = shifted[0:1, :]
        for i in range(11):
            _acc(out_refs[4 + i], grads[8 + i], start)

    ct_in = list(cts_a) + [c for c in cts_b if c is not None]
    row = lambda w: pl.BlockSpec((tm, w), lambda i: (blk(i), 0))
    f = jax.ShapeDtypeStruct
    zshapes = [f((S, D), BF16)] * 3 + [f((S, LORA_PAD), BF16)]
    pshapes = [f((1, D), F32)] * 3 + [f((1, LORA_PAD), F32)] + [f((1, D), F32)] * 4 + [f((LORA_W, D), F32), f((LORA_A, D), F32), f((G_PAD, D), F32)]
    return pl.pallas_call(
        body, name="rwkv_prep_bwd", grid=(nblk,),
        in_specs=slabs + pspecs + [row(D)] * n_ct,
        out_specs=[row(D), row(D), row(D), row(LORA_PAD)] + pspecs,
        out_shape=zshapes + pshapes,
        scratch_shapes=[pltpu.VMEM((SUBLANES, D), F32)] * 3 + [pltpu.VMEM((SUBLANES, LORA_PAD), F32)],
        compiler_params=_cparams(("arbitrary",)))(*([P] * 8), *params, *ct_in)


def _rwkv_post_fn(y, r, kmod, v, g, lnx_w, lnx_b, r_k):
    mean = _head_sum(y) * (1.0 / HEAD)
    yc = y - mean
    var = _head_sum(yc * yc) * (1.0 / HEAD)
    yn = yc * lax.rsqrt(var + GN_EPS) * lnx_w + lnx_b
    bonus = _head_sum(r * kmod * r_k) * v
    return (yn + bonus) * g


def _rwkv_post(y, r, kmod, v, g, lnx_w, lnx_b, r_k, tm=256):
    S = y.shape[0]

    def body(y_ref, r_ref, k_ref, v_ref, g_ref, w_ref, b_ref, rk_ref, o_ref):
        o_ref[...] = _rwkv_post_fn(y_ref[...], r_ref[...], k_ref[...], v_ref[...], g_ref[...],
                                   w_ref[...], b_ref[...], rk_ref[...]).astype(BF16)

    row, vec = _rows(tm, D), _full((1, D))
    return pl.pallas_call(body, name="rwkv_post", grid=(S // tm,), in_specs=[row] * 5 + [vec] * 3, out_specs=row,
                          out_shape=jax.ShapeDtypeStruct((S, D), BF16),
                          compiler_params=_cparams(("parallel",)))(y, r, kmod, v, g, lnx_w, lnx_b, r_k)


def _rwkv_post_bwd(drw, y, r, kmod, v, g, lnx_w, lnx_b, r_k, tm=256):
    S = y.shape[0]

    def body(d_ref, y_ref, r_ref, k_ref, v_ref, g_ref, w_ref, b_ref, rk_ref, *out_refs):
        first = pl.program_id(0) == 0
        _, vjp = jax.vjp(_rwkv_post_fn, y_ref[...], r_ref[...], k_ref[...], v_ref[...], g_ref[...],
                         w_ref[...], b_ref[...], rk_ref[...])
        grads = vjp(d_ref[...])
        for i in range(5):
            out_refs[i][...] = grads[i]
        for i in range(5, 8):
            _acc(out_refs[i], grads[i], first)

    row, vec = _rows(tm, D), _full((1, D))
    f = jax.ShapeDtypeStruct
    return pl.pallas_call(body, name="rwkv_post_bwd", grid=(S // tm,), in_specs=[row] * 6 + [vec] * 3,
                          out_specs=[row] * 5 + [vec] * 3, out_shape=[f((S, D), F32)] * 5 + [f((1, D), F32)] * 3,
                          compiler_params=_cparams(("arbitrary",)))(drw, y, r, kmod, v, g, lnx_w, lnx_b, r_k)


CHUNK = 64
CHUNK_TB = 256
_DOT_DIMS = {"nn": (((2,), (1,)), ((0,), (0,))), "nt": (((2,), (2,)), ((0,), (0,))), "tn": (((1,), (1,)), ((0,), (0,)))}


def _dot16(x, y, mode):
    return lax.dot_general(x.astype(BF16), y.astype(BF16), _DOT_DIMS[mode], preferred_element_type=F32)


@functools.partial(jax.custom_vjp, nondiff_argnums=(2,))
def _mm16(x, y, mode):
    return _dot16(x, y, mode)


def _mm16_fwd(x, y, mode):
    return _dot16(x, y, mode), (x, y)


def _mm16_bwd(mode, res, ct):
    x, y = res
    if mode == "nn":
        return _dot16(ct, y, "nt"), _dot16(x, ct, "tn")
    if mode == "nt":
        return _dot16(ct, y, "nn"), _dot16(ct, x, "tn")
    return _dot16(y, ct, "nt"), _dot16(x, ct, "nn")


_mm16.defvjp(_mm16_fwd, _mm16_bwd)


def _tri_sum(x, upper):
    T = x.shape[0]
    i = lax.broadcasted_iota(jnp.int32, (T, T), 0)
    j = lax.broadcasted_iota(jnp.int32, (T, T), 1)
    tri = ((j >= i) if upper else (i >= j)).astype(BF16)
    out, rest = None, x
    for _ in range(3):
        piece = rest.astype(BF16)
        rest = rest - piece.astype(F32)
        part = jnp.dot(tri, piece, preferred_element_type=F32)
        out = part if out is None else out + part
    return out


@jax.custom_vjp
def _cumsum_rows(x):
    return _tri_sum(x, False)


_cumsum_rows.defvjp(lambda x: (_tri_sum(x, False), None), lambda _, ct: (_tri_sum(ct, True),))


def _rows_to_cols(x):
    H, _, K = x.shape
    eye = (lax.broadcasted_iota(jnp.int32, (H, K, K), 1) == lax.broadcasted_iota(jnp.int32, (H, K, K), 2)).astype(F32)
    out = lax.dot_general(eye, jnp.broadcast_to(x, (H, SUBLANES, K)), _DOT_DIMS["nt"],
                          precision=lax.Precision.HIGHEST, preferred_element_type=F32)
    return out[:, :, 0:1]


def _per_head(x):
    return jnp.concatenate([x[:, h * HEAD:(h + 1) * HEAD][None] for h in range(N_HEADS)], axis=0)


def _chunk_fn(st0, r, lw, k, v, a, b):
    T = r.shape[0]
    cl = _cumsum_rows(lw)
    cl_end = cl[T - 1:T, :]
    inv = jnp.exp(-cl)
    to_end = jnp.exp(cl_end - cl)
    ah, rh, bh, kh, be, ke, v3 = [_per_head(x) for x in
                                  (a * jnp.exp(cl - lw), r * jnp.exp(cl), b * inv, k * inv, b * to_end, k * to_end, v)]
    i = lax.broadcasted_iota(jnp.int32, (N_HEADS, T, T), 1)
    j = lax.broadcasted_iota(jnp.int32, (N_HEADS, T, T), 2)
    a_ab = jnp.where(i > j, _mm16(ah, bh, "nt"), 0.0)
    a_ak = jnp.where(i > j, _mm16(ah, kh, "nt"), 0.0)
    m_rb = jnp.where(i >= j, _mm16(rh, bh, "nt"), 0.0)
    m_rk = jnp.where(i >= j, _mm16(rh, kh, "nt"), 0.0)
    rhs = _mm16(ah, st0, "nn") + _mm16(a_ak, v3, "nn")
    power, solve, n = a_ab, (i == j).astype(F32) + a_ab, 1
    while 2 * n < T:
        power = _mm16(power, power, "nn")
        solve = solve + _mm16(solve, power, "nn")
        n *= 2
    sa = _mm16(solve, rhs, "nn")
    y3 = _mm16(rh, st0, "nn") + _mm16(m_rb, sa, "nn") + _mm16(m_rk, v3, "nn")
    st_end = _rows_to_cols(_per_head(jnp.exp(cl_end))) * st0 + _mm16(be, sa, "tn") + _mm16(ke, v3, "tn")
    return jnp.concatenate([y3[h] for h in range(N_HEADS)], axis=1), st_end


def _hosted_exchange(refs, n, broadcast, grid):
    if n == 0:
        return lambda: None
    start, wait = _exchange_ops(refs[:n], refs[n:2 * n], *refs[2 * n:], broadcast)
    first = functools.reduce(jnp.logical_and, [pl.program_id(a) == 0 for a in range(len(grid))])
    last = functools.reduce(jnp.logical_and, [pl.program_id(a) == g - 1 for a, g in enumerate(grid)])
    pl.when(first)(start)
    return lambda: pl.when(last)(wait)


def _cscan_fwd(r, lw, k, v, a, b, gather=()):
    S = r.shape[0]
    per_blk = CHUNK_TB // CHUNK
    n_x = len(gather)
    nblk = S // CHUNK_TB

    def body(*refs):
        r_ref, lw_ref, k_ref, v_ref, a_ref, b_ref = refs[:6]
        y_ref, ck_ref = refs[6 + n_x:8 + n_x]
        st_ref = refs[8 + 2 * n_x]
        finish = _hosted_exchange(refs[6:6 + n_x] + refs[8 + n_x:8 + 2 * n_x] + refs[9 + 2 * n_x:], n_x, True, (nblk,))

        @pl.when(pl.program_id(0) == 0)
        def _():
            st_ref[...] = jnp.zeros_like(st_ref)

        def chunk(c, carry):
            rows = pl.ds(pl.multiple_of(c * CHUNK, CHUNK), CHUNK)
            st0 = st_ref[...]
            ck_ref[c] = st0
            y, st_end = _chunk_fn(st0, r_ref[rows, :], lw_ref[rows, :], k_ref[rows, :],
                                  v_ref[rows, :], a_ref[rows, :], b_ref[rows, :])
            y_ref[rows, :] = y
            st_ref[...] = st_end
            return carry

        lax.fori_loop(0, per_blk, chunk, 0)
        finish()

    blk = _rows(CHUNK_TB, D)
    any_spec = pl.BlockSpec(memory_space=pl.ANY)
    outs = pl.pallas_call(
        body, name="scan_fwd", grid=(nblk,), in_specs=[blk] * 6 + [any_spec] * n_x,
        out_specs=[blk, pl.BlockSpec((per_blk, N_HEADS, HEAD, HEAD), lambda i: (i, 0, 0, 0))] + [any_spec] * n_x,
        out_shape=[jax.ShapeDtypeStruct((S, D), F32), jax.ShapeDtypeStruct((S // CHUNK, N_HEADS, HEAD, HEAD), F32)]
        + _exchange_shapes(gather, True),
        scratch_shapes=[pltpu.VMEM((N_HEADS, HEAD, HEAD), F32)] + (_exchange_scratch(n_x) if n_x else []),
        compiler_params=_cparams(("arbitrary",)))(r, lw, k, v, a, b, *gather)
    return outs[0], outs[1], outs[2:]


def _cscan_bwd(r, lw, k, v, a, b, ckpt, dy, scatter=()):
    S = r.shape[0]
    per_blk = CHUNK_TB // CHUNK
    nblk = S // CHUNK_TB
    n_x = len(scatter)

    def body(*refs):
        r_ref, lw_ref, k_ref, v_ref, a_ref, b_ref, ck_ref, dy_ref = refs[:8]
        out_refs = refs[8 + n_x:14 + n_x]
        ds_ref = refs[14 + 2 * n_x]
        finish = _hosted_exchange(refs[8:8 + n_x] + refs[14 + n_x:14 + 2 * n_x] + refs[15 + 2 * n_x:], n_x, False, (nblk,))

        @pl.when(pl.program_id(0) == 0)
        def _():
            ds_ref[...] = jnp.zeros_like(ds_ref)

        def chunk(cc, carry):
            c = per_blk - 1 - cc
            rows = pl.ds(pl.multiple_of(c * CHUNK, CHUNK), CHUNK)
            ins = (ck_ref[c], r_ref[rows, :], lw_ref[rows, :], k_ref[rows, :], v_ref[rows, :], a_ref[rows, :], b_ref[rows, :])
            _, vjp = jax.vjp(_chunk_fn, *ins)
            grads = vjp((dy_ref[rows, :], ds_ref[...]))
            ds_ref[...] = grads[0]
            for o_ref, g in zip(out_refs, grads[1:]):
                o_ref[rows, :] = g
            return carry

        lax.fori_loop(0, per_blk, chunk, 0)
        finish()

    blk = pl.BlockSpec((CHUNK_TB, D), lambda i: (nblk - 1 - i, 0))
    any_spec = pl.BlockSpec(memory_space=pl.ANY)
    shp = jax.ShapeDtypeStruct((S, D), F32)
    outs = pl.pallas_call(
        body, name="scan_bwd", grid=(nblk,),
        in_specs=[blk] * 6 + [pl.BlockSpec((per_blk, N_HEADS, HEAD, HEAD), lambda i: (nblk - 1 - i, 0, 0, 0)), blk]
        + [any_spec] * n_x,
        out_specs=[blk] * 6 + [any_spec] * n_x, out_shape=[shp] * 6 + _exchange_shapes(scatter, False),
        scratch_shapes=[pltpu.VMEM((N_HEADS, HEAD, HEAD), F32)] + (_exchange_scratch(n_x) if n_x else []),
        compiler_params=_cparams(("arbitrary",)))(r, lw, k, v, a, b, ckpt, dy, *scatter)
    return outs[:6], outs[6:]


def _ada_partial(c_all, w_shard):
    def body(c_ref, w_ref, o_ref):
        o_ref[...] = jnp.dot(c_ref[...].astype(BF16), w_ref[...].astype(BF16), preferred_element_type=F32)

    vm = pl.BlockSpec(memory_space=pltpu.VMEM)
    return pl.pallas_call(body, name="ada_partial", in_specs=[vm, vm], out_specs=vm,
                          out_shape=jax.ShapeDtypeStruct((N_DEV, w_shard.shape[1]), F32),
                          compiler_params=pltpu.CompilerParams(vmem_limit_bytes=VMEM_LIMIT))(c_all, w_shard)


def _ada_bias(rows, b_ada):
    def body(r_ref, b_ref, o_ref):
        o_ref[...] = r_ref[...] + b_ref[...]

    vm = pl.BlockSpec(memory_space=pltpu.VMEM)
    return pl.pallas_call(body, name="ada_bias", in_specs=[vm, vm], out_specs=vm,
                          out_shape=jax.ShapeDtypeStruct(rows.shape, F32))(rows, b_ada)


def _ada_wgrad(c_cols, d_all):
    def body(c_ref, d_ref, o_ref):
        acc = c_ref[:, 0:1] * d_ref[0:1, :]
        for j in range(1, N_DEV):
            acc = acc + c_ref[:, j:j + 1] * d_ref[j:j + 1, :]
        o_ref[...] = acc

    vm = pl.BlockSpec(memory_space=pltpu.VMEM)
    return pl.pallas_call(body, name="ada_wgrad", in_specs=[vm, vm], out_specs=vm,
                          out_shape=jax.ShapeDtypeStruct((D, d_all.shape[1]), F32),
                          compiler_params=pltpu.CompilerParams(vmem_limit_bytes=VMEM_LIMIT))(c_cols, d_all)


def _exchange(srcs, broadcast, name):
    n = len(srcs)

    def body(*refs):
        start, wait = _exchange_ops(refs[:n], refs[n:2 * n], *refs[2 * n:], broadcast)
        start()
        wait()

    any_spec = pl.BlockSpec(memory_space=pl.ANY)
    return pl.pallas_call(
        body, name=name, out_shape=_exchange_shapes(srcs, broadcast), in_specs=[any_spec] * n, out_specs=[any_spec] * n,
        scratch_shapes=_exchange_scratch(n),
        compiler_params=pltpu.CompilerParams(has_side_effects=True),
    )(*srcs)


def _gather_via_sibling(srcs, name):
    n = len(srcs)

    def body(*refs):
        src_refs, out_refs = refs[:n], refs[n:2 * n]
        send_sems, recv_sems, local_sems = refs[2 * n:]
        x, y, c = lax.axis_index("x"), lax.axis_index("y"), lax.axis_index("c")
        me, sibling = (x, y, c), (x, y, 1 - c)
        chips = [(1 - x, y), (x, 1 - y), (1 - x, 1 - y)]

        def slot(px, py, pc):
            return 4 * px + 2 * py + pc

        def copy(i, k, block, to, src=None):
            rows = out_refs[i].at[slot(*block)]
            return pltpu.make_async_remote_copy(
                src_ref=rows if src is None else src, dst_ref=rows, send_sem=send_sems.at[i, k],
                recv_sem=recv_sems.at[i, k], device_id=to, device_id_type=_MESH)

        local = [pltpu.make_async_copy(src_refs[i], out_refs[i].at[slot(*me)], local_sems.at[i]) for i in range(n)]
        for cp in local:
            cp.start()
        first = [copy(i, 0, me, sibling, src=src_refs[i]) for i in range(n)]
        first += [copy(i, 1 + j, me, (*chip, c), src=src_refs[i]) for j, chip in enumerate(chips) for i in range(n)]
        for cp in first:
            cp.start()
        passed = []
        for j, chip in enumerate(chips):
            for i in range(n):
                copy(i, 1 + j, (*chip, c), me).wait_recv()
                passed.append(copy(i, 4 + j, (*chip, c), sibling))
                passed[-1].start()
        for i in range(n):
            copy(i, 0, sibling, me).wait_recv()
            for j, chip in enumerate(chips):
                copy(i, 4 + j, (*chip, 1 - c), me).wait_recv()
        for cp in first + passed:
            cp.wait_send()
        for cp in local:
            cp.wait()

    any_spec = pl.BlockSpec(memory_space=pl.ANY)
    return pl.pallas_call(
        body, name=name, out_shape=_exchange_shapes(srcs, True), in_specs=[any_spec] * n, out_specs=[any_spec] * n,
        scratch_shapes=_exchange_scratch(n),
        compiler_params=pltpu.CompilerParams(has_side_effects=True),
    )(*srcs)


def _flags(broadcast, n):
    return [broadcast] * n if isinstance(broadcast, bool) else list(broadcast)


def _exchange_shapes(srcs, broadcast):
    return [jax.ShapeDtypeStruct((N_DEV,) + (s.shape if bc else s.shape[1:]), s.dtype)
            for s, bc in zip(srcs, _flags(broadcast, len(srcs)))]


def _exchange_scratch(n):
    return [pltpu.SemaphoreType.DMA((n, N_DEV)), pltpu.SemaphoreType.DMA((n, N_DEV)), pltpu.SemaphoreType.DMA((n,))]


def _exchange_ops(src_refs, out_refs, send_sems, recv_sems, local_sems, broadcast):
    n = len(src_refs)
    flags = _flags(broadcast, n)
    x, y, c = lax.axis_index("x"), lax.axis_index("y"), lax.axis_index("c")
    me = 4 * x + 2 * y + c

    def block(i, j):
        return src_refs[i] if flags[i] else src_refs[i].at[j]

    def remote(i, d, src_slot, dst_slot):
        px, py, pc = x ^ (d >> 2), y ^ ((d >> 1) & 1), c ^ (d & 1)
        return pltpu.make_async_remote_copy(
            src_ref=block(i, src_slot), dst_ref=out_refs[i].at[dst_slot], send_sem=send_sems.at[i, d],
            recv_sem=recv_sems.at[i, d], device_id=(px, py, pc), device_id_type=_MESH)

    def local(i):
        return pltpu.make_async_copy(block(i, me), out_refs[i].at[me], local_sems.at[i])

    def start():
        for i in range(n):
            local(i).start()
        for d in range(1, N_DEV):
            for i in range(n):
                remote(i, d, me ^ d, me).start()

    def wait():
        for d in range(1, N_DEV):
            for i in range(n):
                remote(i, d, me, me ^ d).wait_recv()
        for d in range(1, N_DEV):
            for i in range(n):
                remote(i, d, me ^ d, me).wait_send()
        for i in range(n):
            local(i).wait()

    return start, wait


def _adamw(w, g, m, v):
    nm = ADAM_B1 * m + (1.0 - ADAM_B1) * g
    nv = ADAM_B2 * v + (1.0 - ADAM_B2) * (g * g)
    m_hat = nm * (1.0 / (1.0 - ADAM_B1 ** ADAM_STEP))
    v_hat = nv * (1.0 / (1.0 - ADAM_B2 ** ADAM_STEP))
    return -ADAM_LR * (m_hat / (jnp.sqrt(v_hat) + ADAM_EPS) + ADAM_WD * w), nm, nv


def _adam_vectors(parts, ws, ms, vs):
    nv = len(ws)
    sizes = [w.shape[1] for w in ws]

    def body(*refs):
        p_ref = refs[0]
        w_refs, m_refs, v_refs = refs[1:1 + nv], refs[1 + nv:1 + 2 * nv], refs[1 + 2 * nv:1 + 3 * nv]
        out_refs = refs[1 + 3 * nv:]
        g_all = p_ref[0]
        for j in range(1, N_DEV):
            g_all = g_all + p_ref[j]
        off = 0
        for i, n in enumerate(sizes):
            g = g_all[:, off:off + n]
            off += -(-n // LANES) * LANES
            delta, new_m, new_v = _adamw(w_refs[i][...], g, m_refs[i][...], v_refs[i][...])
            for o_ref, val in zip(out_refs[4 * i:4 * i + 4], (g, delta, new_m, new_v)):
                o_ref[...] = val

    vm = pl.BlockSpec(memory_space=pltpu.VMEM)
    outs = pl.pallas_call(body, name="adam_replicated", in_specs=[vm] * (1 + 3 * nv), out_specs=[vm] * (4 * nv),
                          out_shape=[jax.ShapeDtypeStruct((1, n), F32) for n in sizes for _ in range(4)])(parts, *ws, *ms, *vs)
    return [outs[4 * i:4 * i + 4] for i in range(nv)]


def _sum_adam(parts, w, m, v, name):
    n_parts, R, C = parts.shape
    fits = [t for t in range(16, R + 1, 16) if R % t == 0 and t * C <= 2504 * LANES]
    if fits:
        tm, tc = max(fits), C
    elif C % (2 * LANES) == 0 and R * C > 2504 * LANES:
        tm, tc = R, 2 * LANES
    else:
        tm, tc = R, C

    def body(p_ref, w_ref, m_ref, v_ref, g_ref, d_ref, nm_ref, nv_ref):
        g = p_ref[0].astype(F32)
        for j in range(1, n_parts):
            g = g + p_ref[j].astype(F32)
        g_ref[...] = g
        d_ref[...], nm_ref[...], nv_ref[...] = _adamw(w_ref[...], g, m_ref[...], v_ref[...])

    blk = pl.BlockSpec((tm, tc), lambda i, j: (i, j))
    shp = jax.ShapeDtypeStruct((R, C), F32)
    return pl.pallas_call(body, name=name, grid=(R // tm, C // tc),
                          in_specs=[pl.BlockSpec((n_parts, tm, tc), lambda i, j: (0, i, j)), blk, blk, blk],
                          out_specs=[blk] * 4, out_shape=[shp] * 4,
                          compiler_params=_cparams(("parallel", "parallel")))(parts, w, m, v)


TRANSPOSED = ("w_in", "w_up")
SHARDED = (("w_ada", 1), ("w_in", 0), ("w2", 1), ("a2", 1), ("g2", 1), ("w_att_out", 1), ("w_rwkv_out", 0),
           ("w_o", 0), ("w_up", 0), ("conv_w", 1), ("w_down", 0))
EARLY, LATE = SHARDED[1:5], SHARDED[5:]
REPLICATED = ("b_ada", "norm1_w", "b_gate", "mu_shift", "w0", "a0", "k_k", "k_a", "r_k", "lnx_w", "lnx_b",
              "norm2_w", "conv_b", "norm_f_w")
WEIGHTS = ("w_ada", "b_ada", "norm1_w", "w_in", "b_gate", "mu_shift", "w0", "w2", "a0", "a2", "g2", "k_k", "k_a", "r_k",
           "lnx_w", "lnx_b", "w_att_out", "w_rwkv_out", "w_o", "norm2_w", "w_up", "conv_w", "conv_b", "w_down", "norm_f_w")


def _pad_w_in(w_in_t):
    rkv = w_in_t[ATT_IN:ATT_IN + 3 * D]
    lora = w_in_t[ATT_IN + 3 * D:ATT_IN + RWKV_IN]
    gates = w_in_t[ATT_IN + RWKV_IN:]
    att = w_in_t[:ATT_IN]
    lw, la, lg = lora[:LORA_W], lora[LORA_W:LORA_W + LORA_A], lora[LORA_W + LORA_A:]
    zeros = jnp.zeros((LORA_PAD - LANES - LORA_G, w_in_t.shape[1]), w_in_t.dtype)
    return jnp.concatenate([rkv, gates, att, lw, la, lg, zeros], axis=0)


def _unpad_w_in(g):
    att = g[C_ATT:C_ATT + ATT_IN]
    rkv = g[C_R:C_R + 3 * D]
    lora = jnp.concatenate([g[C_LORA:C_LORA + LORA_W + LORA_A], g[C_LORA + LANES:C_LORA + LANES + LORA_G]], axis=0)
    gates = g[C_GA:C_GA + 2 * D]
    return jnp.concatenate([att, rkv, lora, gates], axis=0)


def _pad_mu(mu):
    lo = mu[:, 3 * D:]
    mu_l = jnp.concatenate([lo[:, :LORA_W + LORA_A], lo[:, LORA_W + LORA_A:], jnp.zeros((1, LORA_PAD - LANES - LORA_G), mu.dtype)], axis=1)
    return mu[:, :D], mu[:, D:2 * D], mu[:, 2 * D:3 * D], mu_l


def _local_step(x, ada, W, late_shards, target):
    S = x.shape[0]
    W = dict(W)
    G = {}
    sh1, sc1, gt1, sh2, sc2, gt2 = [ada[:, i * D:(i + 1) * D] for i in range(6)]
    h1, rstd1 = _norm_fwd(x, None, None, W["norm1_w"], sc1, sh1, "norm1_fwd")
    w_in_p = _pad_w_in(W["w_in"])
    P = _mm(h1, w_in_p, "nt", F32, "proj_in")

    mu_r, mu_k, mu_v, mu_l = _pad_mu(W["mu_shift"])
    g2p = jnp.pad(W["g2"], ((0, G_PAD - LORA_G), (0, 0)))
    prep_params = [mu_r, mu_k, mu_v, mu_l, W["w0"], W["a0"], W["k_k"], W["k_a"], W["w2"], W["a2"], g2p]
    r_, dec, kmod, v_, aa, bb, gg = _rwkv_prep(P, prep_params)
    y_scan, states, late = _cscan_fwd(r_, dec, kmod, v_, aa, bb, gather=late_shards)
    W.update({n: _full_weight(g, axis) for (n, axis), g in zip(LATE, late)})

    o_g, l_g = zip(*[_att_fwd(P, g) for g in range(len(ATT_PATTERNS))])
    att = _att_combine_fwd(o_g, l_g)
    y_att = _mm(att, W["w_att_out"], "nn", F32, "att_out")
    r_k = W["r_k"].reshape(1, D)
    rw = _rwkv_post(y_scan, r_, kmod, v_, gg, W["lnx_w"], W["lnx_b"], r_k)
    y_rwkv = _mm(rw, W["w_rwkv_out"], "nn", F32, "rwkv_out")

    bga, bgr = W["b_gate"][:, :D], W["b_gate"][:, D:]
    mix = _gate_fwd(P, bga, bgr, y_att, y_rwkv)
    mo = _mm(mix, W["w_o"], "nn", F32, "mix_out")
    x2, h2, rstd2 = _norm_fwd(x, mo, gt1, W["norm2_w"], sc2, sh2, "norm2_fwd")
    u = _mm(h2, W["w_up"], "nt", BF16, "ffn_up")
    conv_w8 = jnp.pad(W["conv_w"], ((0, SUBLANES - 3), (0, 0)))
    act = _conv_fwd(u, conv_w8, W["conv_b"])
    f = _mm(act, W["w_down"], "nn", F32, "ffn_down")
    loss_blk, dx3, df, dgt2, G["norm_f_w"] = _final(x2, f, gt2, W["norm_f_w"], target)
    loss = loss_blk[0, 0]

    dact = _mm(df, W["w_down"], "nt", BF16, "ffn_down_dx")
    G["w_down"] = _mm(act, df, "tn", BF16, "ffn_down_dw")
    duc, dwg, dwv, dbg, dbv = _conv_bwd_a(dact, u, conv_w8, W["conv_b"])
    G["conv_w"] = jnp.concatenate([dwg[0:3], dwv[0:3]], axis=1)
    G["conv_b"] = jnp.concatenate([dbg, dbv], axis=1)
    du = _conv_bwd_b(duc, conv_w8)
    dh2 = _mm(du, W["w_up"], "nn", F32, "ffn_up_dx")
    G["w_up"] = _mm(du, h2, "tn", BF16, "ffn_up_dw")
    dx2, dsh2, dsc2, G["norm2_w"], dmo, dgt1 = _norm_bwd(dh2, x2, rstd2, W["norm2_w"], sc2, dx3, mo, gt1, "norm2_bwd")
    dmix = _mm(dmo, W["w_o"], "nt", F32, "mix_out_dx")
    G["w_o"] = _mm(mix, dmo, "tn", BF16, "mix_out_dw")
    dy_att, dy_rwkv, dpga, dpgr, dbga, dbgr = _gate_bwd(dmix, P, bga, bgr, y_att, y_rwkv)
    G["b_gate"] = jnp.concatenate([dbga, dbgr], axis=1)

    datt = _mm(dy_att, W["w_att_out"], "nt", F32, "att_out_dx")
    G["w_att_out"] = _mm(att, dy_att, "tn", BF16, "att_out_dw")
    dcomb = _att_combine_bwd(datt, o_g, l_g)
    dp_att = []
    for g in range(len(ATT_PATTERNS)):
        dp_att += _att_bwd(P, o_g[g], l_g[g], dcomb[g], dcomb[3 + g], g)

    drw = _mm(dy_rwkv, W["w_rwkv_out"], "nt", F32, "rwkv_out_dx")
    G["w_rwkv_out"] = _mm(rw, dy_rwkv, "tn", BF16, "rwkv_out_dw")
    dy_scan, dr1, dk1, dv1, dgg, G["lnx_w"], G["lnx_b"], drk = _rwkv_post_bwd(drw, y_scan, r_, kmod, v_, gg, W["lnx_w"], W["lnx_b"], r_k)
    G["r_k"] = drk.reshape(W["r_k"].shape)
    late_blocks = [_owner_blocks(G[n], axis) for n, axis in LATE] if late_shards else []
    (dr2, ddec, dk2, dv2, daa, dbb), late_parts = _cscan_bwd(r_, dec, kmod, v_, aa, bb, states, dy_scan, scatter=late_blocks)
    pb = _rwkv_prep_bwd(P, prep_params, [dr2, ddec, dk2, dv2, daa, dbb, dgg], [dr1, None, dk1, dv1, None, None, None])
    dp_rkv, dp_lora, dpar = list(pb[0:3]), pb[3], pb[4:]
    dmu_r, dmu_k, dmu_v, dmu_l, G["w0"], G["a0"], G["k_k"], G["k_a"], G["w2"], G["a2"], dg2p = dpar
    G["g2"] = dg2p[0:LORA_G]
    G["mu_shift"] = jnp.concatenate([dmu_r, dmu_k, dmu_v, dmu_l[:, :LORA_W + LORA_A], dmu_l[:, LANES:LANES + LORA_G]], axis=1)

    dP = jnp.concatenate(dp_rkv + [dpga, dpgr] + dp_att + [dp_lora], axis=1)
    G["w_in"] = _unpad_w_in(_mm(dP, h1, "tn", BF16, "proj_in_dw"))
    if late_shards:
        dh1, (w_in_parts,) = _mm(dP, w_in_p, "nn", F32, "proj_in_dx", scatter=[_owner_blocks(G["w_in"], 0)])
        done = dict(zip([n for n, _ in LATE] + ["w_in"], list(late_parts) + [w_in_parts]))
    else:
        dh1, done = _mm(dP, w_in_p, "nn", F32, "proj_in_dx"), {}
    grad_x, dsh1, dsc1, G["norm1_w"] = _norm_bwd(dh1, x, rstd1, W["norm1_w"], sc1, dx2, None, None, "norm1_bwd")
    dada = jnp.concatenate([dsh1, dsc1, dgt1, dsh2, dsc2, dgt2], axis=1)
    G["b_ada"] = dada
    return loss, grad_x, G, done


def _full_weight(gathered, axis):
    _, rows, cols = gathered.shape
    if axis == 0:
        return gathered.reshape(N_DEV * rows, cols)
    return gathered.transpose(1, 0, 2).reshape(rows, N_DEV * cols)


def _owner_blocks(g, axis):
    rows, cols = g.shape
    g = g.astype(BF16)
    if axis == 0:
        return g.reshape(N_DEV, rows // N_DEV, cols)
    return g.reshape(rows, N_DEV, cols // N_DEV).transpose(1, 0, 2)


def kernel(x, c, w_ada, b_ada, norm1_w, w_in, b_gate, mu_shift, w0, w2, a0, a2, g2, k_k, k_a, r_k, lnx_w, lnx_b, w_att_out, w_rwkv_out, w_o, norm2_w, w_up, conv_w, conv_b, w_down, norm_f_w, loss_target, m_w_ada, m_b_ada, m_norm1_w, m_w_in, m_b_gate, m_mu_shift, m_w0, m_w2, m_a0, m_a2, m_g2, m_k_k, m_k_a, m_r_k, m_lnx_w, m_lnx_b, m_w_att_out, m_w_rwkv_out, m_w_o, m_norm2_w, m_w_up, m_conv_w, m_conv_b, m_w_down, m_norm_f_w, v_w_ada, v_b_ada, v_norm1_w, v_w_in, v_b_gate, v_mu_shift, v_w0, v_w2, v_a0, v_a2, v_g2, v_k_k, v_k_a, v_r_k, v_lnx_w, v_lnx_b, v_w_att_out, v_w_rwkv_out, v_w_o, v_norm2_w, v_w_up, v_conv_w, v_conv_b, v_w_down, v_norm_f_w):
    env = dict(locals())
    w_shard = {n: env[n] for n in WEIGHTS}
    m_shard = {n: env["m_" + n] for n in WEIGHTS}
    v_shard = {n: env["v_" + n] for n in WEIGHTS}

    def mat(shards, n):
        return jnp.swapaxes(shards[n][0], 0, 1) if n in TRANSPOSED else shards[n][0]

    c_all, *gathered = _gather_via_sibling([c] + [mat(w_shard, n).astype(BF16) for n, _ in EARLY], "gather_weights")
    c_all = c_all.reshape(N_DEV, D)
    W = {n: _full_weight(g, axis) for (n, axis), g in zip(EARLY, gathered)}
    for n in REPLICATED:
        W[n] = w_shard[n].reshape(1, -1) if n != "r_k" else w_shard[n][0]
    ada_cols = _ada_partial(c_all, w_shard["w_ada"][0])
    ada_rows, = _exchange([ada_cols[:, None, :]], False, "ada_rows")
    ada = _ada_bias(ada_rows.reshape(1, -1), w_shard["b_ada"])

    late_shards = [mat(w_shard, n).astype(BF16) for n, _ in LATE]
    loss, grad_x, G, parts = _local_step(x[0], ada, W, late_shards, loss_target[0])
    loss = lax.psum(loss, ("x", "y", "c"))

    row = lambda a: a.reshape(1, -1)
    small = jnp.concatenate([jnp.pad(row(G[n]), ((0, 0), (0, (-G[n].size) % LANES))) for n in REPLICATED], axis=1)
    sparts, dada_all = _exchange([small, G["b_ada"].reshape(N_DEV, 1, -1)], [True, False], "gather_small_grads")
    parts["w_ada"] = _ada_wgrad(c_all.T, dada_all.reshape(N_DEV, -1))[None]

    rest = [(n, axis) for n, axis in SHARDED if n not in parts]
    parts.update(zip([n for n, _ in rest], _exchange([_owner_blocks(G[n], axis) for n, axis in rest], False, "scatter_grads")))
    out = {}
    for n, p in parts.items():
        res = _sum_adam(p, mat(w_shard, n), mat(m_shard, n), mat(v_shard, n), "adam_" + n)
        if n in TRANSPOSED:
            res = [jnp.swapaxes(a, 0, 1) for a in res]
        for kind, a in zip(("grad", "delta", "new_m", "new_v"), res):
            out[kind, n] = a[None]

    res = _adam_vectors(sparts, *[[row(s[n]) for n in REPLICATED] for s in (w_shard, m_shard, v_shard)])
    for n, four in zip(REPLICATED, res):
        for kind, a in zip(("grad", "delta", "new_m", "new_v"), four):
            out[kind, n] = a.reshape(w_shard[n].shape)

    return (loss, grad_x[None], *[out[kind, n] for kind in ("grad", "delta", "new_m", "new_v") for n in WEIGHTS])
```

```python
import functools
import math

import jax
import jax.numpy as jnp
from jax import lax
from jax.experimental import pallas as pl
from jax.experimental.pallas import tpu as pltpu

F32 = jnp.float32
BF16 = jnp.bfloat16

D = 1024
HEAD = 64
ATT_PATTERNS = ((128, 1), (512, 4), (2048, 16))
ATT_HEADS = 8
ATT_W = ATT_HEADS * HEAD
ATT_IN = 3 * 3 * ATT_W
QBLK = 128
N_HEADS = D // HEAD
LORA_W, LORA_A, LORA_G = 64, 64, 160
RWKV_IN = 3 * D + LORA_W + LORA_A + LORA_G
N_IN = ATT_IN + RWKV_IN + 2 * D
D_FF = 2816
RMS_EPS = 1e-6
GN_EPS = 64e-5
N_DEV = 8
LANES = 128
SUBLANES = 8

C_R, C_K, C_V, C_GA, C_GR = 0, 1024, 2048, 3072, 4096
C_ATT = 5120
C_LORA = C_ATT + ATT_IN
LORA_PAD = 512
G_PAD = 256
N_PAD = C_LORA + LORA_PAD

ADAM_LR, ADAM_B1, ADAM_B2, ADAM_EPS, ADAM_WD, ADAM_STEP = 0.001, 0.9, 0.999, 1e-08, 0.01, 10

VMEM_LIMIT = 56 * 1024 * 1024

_MESH = pl.DeviceIdType.MESH


def _cparams(sem):
    return pltpu.CompilerParams(dimension_semantics=sem, vmem_limit_bytes=VMEM_LIMIT)


def _tile(dim, pref):
    if dim <= pref:
        return dim
    best = None
    for t in range(LANES, pref + 1, LANES):
        if dim % t == 0:
            best = t
    assert best is not None, dim
    return best


MM_TILES = {"nn": (1024, 1408, 1408), "nt": (1024, 2048, 1408), "tn": (1408, 1408, 1024)}


def _mm(a, b, mode, out_dtype, name, scatter=()):
    if mode == "nn":
        (M, K), (K2, N) = a.shape, b.shape
    elif mode == "nt":
        (M, K), (N, K2) = a.shape, b.shape
    else:
        (K, M), (K2, N) = a.shape, b.shape
    assert K == K2, (a.shape, b.shape, mode)
    tm, tn, tk = (_tile(dim, pref) for dim, pref in zip((M, N, K), MM_TILES[mode]))
    nk = K // tk
    grid = (M // tm, N // tn, nk)
    n_x = len(scatter)
    dims = {"nn": (((1,), (0,)), ((), ())), "nt": (((1,), (1,)), ((), ())), "tn": (((0,), (0,)), ((), ()))}[mode]

    def body(*refs):
        a_ref, b_ref = refs[:2]
        o_ref, acc_ref = refs[2 + n_x], refs[3 + 2 * n_x]
        finish = _hosted_exchange(refs[2:2 + n_x] + refs[3 + n_x:3 + 2 * n_x] + refs[4 + 2 * n_x:], n_x, False, grid)
        k = pl.program_id(2)
        part = lax.dot_general(a_ref[...].astype(BF16), b_ref[...].astype(BF16), dims,
                               preferred_element_type=F32)
        if nk == 1:
            o_ref[...] = part.astype(o_ref.dtype)
        else:
            @pl.when(k == 0)
            def _():
                acc_ref[...] = part

            @pl.when(jnp.logical_and(k > 0, k < nk - 1))
            def _():
                acc_ref[...] += part

            @pl.when(k == nk - 1)
            def _():
                o_ref[...] = (acc_ref[...] + part).astype(o_ref.dtype)
        finish()

    a_spec = pl.BlockSpec((tk, tm), lambda i, j, k: (k, i)) if mode == "tn" else pl.BlockSpec((tm, tk), lambda i, j, k: (i, k))
    b_spec = pl.BlockSpec((tn, tk), lambda i, j, k: (j, k)) if mode == "nt" else pl.BlockSpec((tk, tn), lambda i, j, k: (k, j))
    any_spec = pl.BlockSpec(memory_space=pl.ANY)
    outs = pl.pallas_call(
        body, name=name, grid=grid,
        in_specs=[a_spec, b_spec] + [any_spec] * n_x,
        out_specs=[pl.BlockSpec((tm, tn), lambda i, j, k: (i, j))] + [any_spec] * n_x,
        out_shape=[jax.ShapeDtypeStruct((M, N), out_dtype)] + _exchange_shapes(scatter, False),
        scratch_shapes=[pltpu.VMEM((tm, tn) if nk > 1 else (SUBLANES, LANES), F32)] + (_exchange_scratch(n_x) if n_x else []),
        compiler_params=_cparams(("arbitrary",) * 3 if n_x else ("parallel", "parallel", "arbitrary")),
    )(a, b, *scatter)
    return (outs[0], outs[1:]) if n_x else outs[0]


def _rows(tm, w, col=0):
    return pl.BlockSpec((tm, w), lambda i: (i, col))


def _full(shape):
    return pl.BlockSpec(shape, lambda i: (0,) * len(shape))


def _shift_down(x, halo, k, first):
    rolled = pltpu.roll(x, k, 0)
    row = lax.broadcasted_iota(jnp.int32, x.shape, 0)
    out = rolled
    n_halo = halo.shape[0]
    for j in range(k):
        h = jnp.where(first, 0.0, halo[n_halo - k + j:n_halo - k + j + 1, :])
        out = jnp.where(row == j, h, out)
    return out


def _shift_up(x, halo, k, last):
    n = x.shape[0]
    rolled = pltpu.roll(x, n - k, 0)
    row = lax.broadcasted_iota(jnp.int32, x.shape, 0)
    out = rolled
    for j in range(k):
        h = jnp.where(last, 0.0, halo[j:j + 1, :])
        out = jnp.where(row == n - k + j, h, out)
    return out


def _acc(ref, val, first):
    @pl.when(first)
    def _():
        ref[...] = val

    @pl.when(jnp.logical_not(first))
    def _():
        ref[...] += val


def _colsum(x):
    return jnp.sum(x, axis=0, keepdims=True)


def _norm_fwd(x, mo, gt, nw, sc, sh, name, tm=256):
    S = x.shape[0]
    has_res = mo is not None

    def body(*refs):
        if has_res:
            x_ref, mo_ref, gt_ref, nw_ref, sc_ref, sh_ref, x2_ref, h_ref, rs_ref = refs
            x2 = x_ref[...] + gt_ref[...] * mo_ref[...]
            x2_ref[...] = x2
        else:
            x_ref, nw_ref, sc_ref, sh_ref, h_ref, rs_ref = refs
            x2 = x_ref[...]
        rstd = lax.rsqrt(jnp.mean(x2 * x2, axis=-1, keepdims=True) + RMS_EPS)
        rs_ref[...] = rstd
        h_ref[...] = ((x2 * rstd * nw_ref[...]) * (1.0 + sc_ref[...]) + sh_ref[...]).astype(BF16)

    vec = _full((1, D))
    ins = [x, mo, gt, nw, sc, sh] if has_res else [x, nw, sc, sh]
    in_specs = [_rows(tm, D), _rows(tm, D), vec, vec, vec, vec] if has_res else [_rows(tm, D), vec, vec, vec]
    outs = [jax.ShapeDtypeStruct((S, D), BF16), jax.ShapeDtypeStruct((S, 1), F32)]
    out_specs = [_rows(tm, D), _rows(tm, 1)]
    if has_res:
        outs = [jax.ShapeDtypeStruct((S, D), F32)] + outs
        out_specs = [_rows(tm, D)] + out_specs
    return pl.pallas_call(body, name=name, grid=(S // tm,), in_specs=in_specs, out_specs=out_specs,
                          out_shape=outs, compiler_params=_cparams(("parallel",)))(*ins)


def _norm_bwd(dh, xin, rstd, nw, sc, dres, mo, gt, name, tm=256):
    S = xin.shape[0]
    has_res = mo is not None

    def body(*refs):
        if has_res:
            dh_ref, x_ref, rs_ref, nw_ref, sc_ref, dres_ref, mo_ref, gt_ref, dx_ref, dsh_ref, dsc_ref, dnw_ref, dmo_ref, dgt_ref = refs
        else:
            dh_ref, x_ref, rs_ref, nw_ref, sc_ref, dres_ref, dx_ref, dsh_ref, dsc_ref, dnw_ref = refs
        first = pl.program_id(0) == 0
        dh = dh_ref[...]
        rstd = rs_ref[...]
        n = x_ref[...] * rstd
        w = nw_ref[...]
        _acc(dsh_ref, _colsum(dh), first)
        _acc(dsc_ref, _colsum(dh * (n * w)), first)
        dnw = dh * (1.0 + sc_ref[...])
        _acc(dnw_ref, _colsum(dnw * n), first)
        dn = dnw * w
        dx = dres_ref[...] + rstd * (dn - n * jnp.mean(dn * n, axis=-1, keepdims=True))
        dx_ref[...] = dx
        if has_res:
            dmo_ref[...] = (dx * gt_ref[...]).astype(BF16)
            _acc(dgt_ref, _colsum(dx * mo_ref[...]), first)

    vec = _full((1, D))
    vshape = jax.ShapeDtypeStruct((1, D), F32)
    ins = [dh, xin, rstd, nw, sc, dres] + ([mo, gt] if has_res else [])
    in_specs = [_rows(tm, D), _rows(tm, D), _rows(tm, 1), vec, vec, _rows(tm, D)] + ([_rows(tm, D), vec] if has_res else [])
    outs = [jax.ShapeDtypeStruct((S, D), F32), vshape, vshape, vshape]
    out_specs = [_rows(tm, D), vec, vec, vec]
    if has_res:
        outs += [jax.ShapeDtypeStruct((S, D), BF16), vshape]
        out_specs += [_rows(tm, D), vec]
    return pl.pallas_call(body, name=name, grid=(S // tm,), in_specs=in_specs, out_specs=out_specs,
                          out_shape=outs, compiler_params=_cparams(("arbitrary",)))(*ins)


def _final(x2, f, gt2, nfw, target, tm=256):
    S = x2.shape[0]

    def body(x2_ref, f_ref, gt_ref, w_ref, t_ref, loss_ref, dx_ref, df_ref, dgt_ref, dw_ref):
        first = pl.program_id(0) == 0
        f = f_ref[...]
        gt = gt_ref[...]
        w = w_ref[...]
        x3 = x2_ref[...] + gt * f
        rstd = lax.rsqrt(jnp.mean(x3 * x3, axis=-1, keepdims=True) + RMS_EPS)
        n = x3 * rstd
        e = n * w - t_ref[...]
        part = 0.5 * jnp.sum(jnp.mean(e * e, axis=-1, keepdims=True), axis=0, keepdims=True)
        _acc(loss_ref, jnp.broadcast_to(part, (SUBLANES, LANES)), first)
        dy = e * (1.0 / D)
        _acc(dw_ref, _colsum(dy * n), first)
        dn = dy * w
        dx = rstd * (dn - n * jnp.mean(dn * n, axis=-1, keepdims=True))
        dx_ref[...] = dx
        df_ref[...] = (dx * gt).astype(BF16)
        _acc(dgt_ref, _colsum(dx * f), first)

    vec = _full((1, D))
    vshape = jax.ShapeDtypeStruct((1, D), F32)
    return pl.pallas_call(
        body, name="final_loss", grid=(S // tm,),
        in_specs=[_rows(tm, D), _rows(tm, D), vec, vec, _rows(tm, D)],
        out_specs=[_full((SUBLANES, LANES)), _rows(tm, D), _rows(tm, D), vec, vec],
        out_shape=[jax.ShapeDtypeStruct((SUBLANES, LANES), F32), jax.ShapeDtypeStruct((S, D), F32),
                   jax.ShapeDtypeStruct((S, D), BF16), vshape, vshape],
        compiler_params=_cparams(("arbitrary",)))(x2, f, gt2, nfw, target)


def _gate_fwd(P, bga, bgr, y_att, y_rwkv, tm=256):
    S = P.shape[0]

    def body(pa_ref, pr_ref, ba_ref, br_ref, ya_ref, yr_ref, mix_ref):
        ga = jax.nn.sigmoid(pa_ref[...] + ba_ref[...])
        gr = jax.nn.sigmoid(pr_ref[...] + br_ref[...])
        mix_ref[...] = (ga * ya_ref[...] + gr * yr_ref[...]).astype(BF16)

    vec = _full((1, D))
    return pl.pallas_call(
        body, name="gate_fwd", grid=(S // tm,),
        in_specs=[_rows(tm, D, C_GA // D), _rows(tm, D, C_GR // D), vec, vec, _rows(tm, D), _rows(tm, D)],
        out_specs=_rows(tm, D), out_shape=jax.ShapeDtypeStruct((S, D), BF16),
        compiler_params=_cparams(("parallel",)))(P, P, bga, bgr, y_att, y_rwkv)


def _gate_bwd(dmix, P, bga, bgr, y_att, y_rwkv, tm=256):
    S = P.shape[0]

    def body(dm_ref, pa_ref, pr_ref, ba_ref, br_ref, ya_ref, yr_ref, dya_ref, dyr_ref, dpa_ref, dpr_ref, dba_ref, dbr_ref):
        first = pl.program_id(0) == 0
        dm = dm_ref[...]
        ga = jax.nn.sigmoid(pa_ref[...] + ba_ref[...])
        gr = jax.nn.sigmoid(pr_ref[...] + br_ref[...])
        dya_ref[...] = (dm * ga).astype(BF16)
        dyr_ref[...] = (dm * gr).astype(BF16)
        dpa = dm * ya_ref[...] * ga * (1.0 - ga)
        dpr = dm * yr_ref[...] * gr * (1.0 - gr)
        dpa_ref[...] = dpa.astype(BF16)
        dpr_ref[...] = dpr.astype(BF16)
        _acc(dba_ref, _colsum(dpa), first)
        _acc(dbr_ref, _colsum(dpr), first)

    vec = _full((1, D))
    row = _rows(tm, D)
    rshape = jax.ShapeDtypeStruct((S, D), BF16)
    vshape = jax.ShapeDtypeStruct((1, D), F32)
    return pl.pallas_call(
        body, name="gate_bwd", grid=(S // tm,),
        in_specs=[row, _rows(tm, D, C_GA // D), _rows(tm, D, C_GR // D), vec, vec, row, row],
        out_specs=[row, row, row, row, vec, vec],
        out_shape=[rshape, rshape, rshape, rshape, vshape, vshape],
        compiler_params=_cparams(("arbitrary",)))(dmix, P, P, bga, bgr, y_att, y_rwkv)


CONV_TN = D_FF // 2
HALO = 16


def _conv_fwd(u, conv_w8, conv_b, tm=256, tn=CONV_TN):
    S = u.shape[0]
    nj = D_FF // tn

    def conv(u_ref, h_ref, w_ref, b_ref, first):
        u = u_ref[...].astype(F32)
        h = h_ref[...].astype(F32)
        w = w_ref[...]
        return b_ref[...] + w[0:1] * _shift_down(u, h, 2, first) + w[1:2] * _shift_down(u, h, 1, first) + w[2:3] * u

    def body(ug_ref, hg_ref, uv_ref, hv_ref, wg_ref, wv_ref, bg_ref, bv_ref, act_ref):
        first = pl.program_id(0) == 0
        g = conv(ug_ref, hg_ref, wg_ref, bg_ref, first)
        v = conv(uv_ref, hv_ref, wv_ref, bv_ref, first)
        act_ref[...] = (g * jax.nn.sigmoid(g) * v).astype(BF16)

    blk = lambda off: pl.BlockSpec((tm, tn), lambda i, j: (i, j + off))
    halo = lambda off: pl.BlockSpec((HALO, tn), lambda i, j: (jnp.maximum(i * (tm // HALO) - 1, 0), j + off))
    wsp = lambda off: pl.BlockSpec((SUBLANES, tn), lambda i, j: (0, j + off))
    bsp = lambda off: pl.BlockSpec((1, tn), lambda i, j: (0, j + off))
    return pl.pallas_call(
        body, name="conv_fwd", grid=(S // tm, nj),
        in_specs=[blk(0), halo(0), blk(nj), halo(nj), wsp(0), wsp(nj), bsp(0), bsp(nj)],
        out_specs=pl.BlockSpec((tm, tn), lambda i, j: (i, j)),
        out_shape=jax.ShapeDtypeStruct((S, D_FF), BF16),
        compiler_params=_cparams(("parallel", "parallel")))(u, u, u, u, conv_w8, conv_w8, conv_b, conv_b)


def _conv_bwd_a(dact, u, conv_w8, conv_b, tm=256, tn=CONV_TN):
    S = u.shape[0]
    nj = D_FF // tn

    def half(u_ref, h_ref, w_ref, b_ref, first):
        u = u_ref[...].astype(F32)
        h = h_ref[...].astype(F32)
        w = w_ref[...]
        u2, u1 = _shift_down(u, h, 2, first), _shift_down(u, h, 1, first)
        return b_ref[...] + w[0:1] * u2 + w[1:2] * u1 + w[2:3] * u, (u2, u1, u)

    def wgrad(d, taps):
        z = jnp.zeros((SUBLANES - 3, d.shape[1]), F32)
        return jnp.concatenate([_colsum(d * taps[0]), _colsum(d * taps[1]), _colsum(d * taps[2]), z], axis=0)

    def body(da_ref, ug_ref, hg_ref, uv_ref, hv_ref, wg_ref, wv_ref, bg_ref, bv_ref,
             d_ref, dwg_ref, dwv_ref, dbg_ref, dbv_ref):
        first = pl.program_id(1) == 0
        g, tg = half(ug_ref, hg_ref, wg_ref, bg_ref, first)
        v, tv = half(uv_ref, hv_ref, wv_ref, bv_ref, first)
        da = da_ref[...].astype(F32)
        sg = jax.nn.sigmoid(g)
        dg = da * v * (sg * (1.0 + g * (1.0 - sg)))
        dv = da * (g * sg)
        d_ref[0] = dg.astype(BF16)
        d_ref[1] = dv.astype(BF16)
        _acc(dwg_ref, wgrad(dg, tg), first)
        _acc(dwv_ref, wgrad(dv, tv), first)
        _acc(dbg_ref, _colsum(dg), first)
        _acc(dbv_ref, _colsum(dv), first)

    blk = lambda off: pl.BlockSpec((tm, tn), lambda j, i: (i, j + off))
    halo = lambda off: pl.BlockSpec((HALO, tn), lambda j, i: (jnp.maximum(i * (tm // HALO) - 1, 0), j + off))
    wsp = lambda off: pl.BlockSpec((SUBLANES, tn), lambda j, i: (0, j + off))
    bsp = lambda off: pl.BlockSpec((1, tn), lambda j, i: (0, j + off))
    f = jax.ShapeDtypeStruct
    outs = pl.pallas_call(
        body, name="conv_bwd_a", grid=(nj, S // tm),
        in_specs=[pl.BlockSpec((tm, tn), lambda j, i: (i, j)), blk(0), halo(0), blk(nj), halo(nj), wsp(0), wsp(nj), bsp(0), bsp(nj)],
        out_specs=[pl.BlockSpec((2, tm, tn), lambda j, i: (0, i, j)),
                   pl.BlockSpec((SUBLANES, tn), lambda j, i: (0, j)), pl.BlockSpec((SUBLANES, tn), lambda j, i: (0, j)),
                   pl.BlockSpec((1, tn), lambda j, i: (0, j)), pl.BlockSpec((1, tn), lambda j, i: (0, j))],
        out_shape=[f((2, S, D_FF), BF16), f((SUBLANES, D_FF), F32), f((SUBLANES, D_FF), F32),
                   f((1, D_FF), F32), f((1, D_FF), F32)],
        compiler_params=_cparams(("parallel", "arbitrary")))(dact, u, u, u, u, conv_w8, conv_w8, conv_b, conv_b)
    return outs


def _conv_bwd_b(duc, conv_w8, tm=256, tn=CONV_TN):
    _, S, W = duc.shape
    nj = W // tn
    n_rows = S // tm

    def body(d_ref, h_ref, w_ref, o_ref):
        last = pl.program_id(0) == n_rows - 1
        d = d_ref[...].astype(F32)
        h = h_ref[...].astype(F32)
        w = w_ref[...]
        o_ref[...] = (w[2:3] * d + w[1:2] * _shift_up(d, h, 1, last) + w[0:1] * _shift_up(d, h, 2, last)).astype(BF16)

    last_tile = S // HALO - 1
    return pl.pallas_call(
        body, name="conv_bwd_b", grid=(n_rows, 2 * nj),
        in_specs=[pl.BlockSpec((None, tm, tn), lambda i, j: (j // nj, i, j % nj)),
                  pl.BlockSpec((None, HALO, tn), lambda i, j: (j // nj, jnp.minimum((i + 1) * (tm // HALO), last_tile), j % nj)),
                  pl.BlockSpec((SUBLANES, tn), lambda i, j: (0, j))],
        out_specs=pl.BlockSpec((tm, tn), lambda i, j: (i, j)),
        out_shape=jax.ShapeDtypeStruct((S, 2 * W), BF16),
        compiler_params=_cparams(("parallel", "parallel")))(duc, duc, conv_w8)


ATT_SCALE = HEAD ** -0.5
NEG = -1e30
ATT_PAIRS = ATT_HEADS // 2


def _att_rows(n, d, S):
    per = S // (QBLK * d)
    r, m = n // per, n % per
    cur = pl.ds(m * (QBLK * d) + r, QBLK, stride=d)
    prv = pl.ds(jnp.maximum(m - 1, 0) * (QBLK * d) + r, QBLK, stride=d)
    return cur, prv, m > 0


def _att_slab(g, j):
    return (C_ATT + g * 3 * ATT_W + j * ATT_W) // LANES


def _heads(x):
    return x[:, 0:HEAD], x[:, HEAD:2 * HEAD]


ATT_NB = 4


def _stack(tiles):
    return jnp.concatenate([t[None] for t in tiles], axis=0)


def _att_operands(i, d, S, *sources):
    rows, has = [], []
    tiles = [[] for _ in sources]
    for bb in range(ATT_NB):
        cur, prv, has_prev = _att_rows(i * ATT_NB + bb, d, S)
        rows.append((cur, prv))
        has.append(has_prev)
        for t, (ref, use_cur) in zip(tiles, sources):
            t += _heads(ref[cur if use_cur else prv, :].astype(BF16))
    return rows, has, [_stack(t) for t in tiles]


def _att_mask(s_c, s_p, has_prev):
    qi = lax.broadcasted_iota(jnp.int32, (QBLK, QBLK), 0)
    kj = lax.broadcasted_iota(jnp.int32, (QBLK, QBLK), 1)
    s_c = jnp.where(kj <= qi, s_c * ATT_SCALE, NEG)
    s_p = jnp.where(jnp.logical_and(kj >= qi, has_prev), s_p * ATT_SCALE, NEG)
    return s_c, s_p


def _att_fwd(P, g):
    S = P.shape[0]
    d = ATT_PATTERNS[g][1]

    def body(q_ref, k_ref, v_ref, o_ref, l_ref):
        def group(i, carry):
            rows, has, (q, kc, kp, vc, vp) = _att_operands(i, d, S, (q_ref, True), (k_ref, True), (k_ref, False),
                                                           (v_ref, True), (v_ref, False))
            s_c_all, s_p_all = _dot16(q, kc, "nt"), _dot16(q, kp, "nt")
            p_c, p_p, den, lse = [], [], [], []
            for e in range(2 * ATT_NB):
                s_c, s_p = _att_mask(s_c_all[e], s_p_all[e], has[e // 2])
                m = jnp.maximum(jnp.max(s_c, axis=1, keepdims=True), jnp.max(s_p, axis=1, keepdims=True))
                pc, pp = jnp.exp(s_c - m), jnp.exp(s_p - m)
                den.append(jnp.sum(pc, axis=1, keepdims=True) + jnp.sum(pp, axis=1, keepdims=True))
                lse.append(jnp.broadcast_to(m + jnp.log(den[e]), (QBLK, HEAD)))
                p_c.append(pc)
                p_p.append(pp)
            num = _dot16(_stack(p_c), vc, "nn") + _dot16(_stack(p_p), vp, "nn")
            for bb, (cur, _) in enumerate(rows):
                o_ref[cur, :] = jnp.concatenate([num[2 * bb] / den[2 * bb], num[2 * bb + 1] / den[2 * bb + 1]], axis=1)
                l_ref[cur, :] = jnp.concatenate(lse[2 * bb:2 * bb + 2], axis=1)
            return carry

        lax.fori_loop(0, S // QBLK // ATT_NB, group, 0)

    slab = lambda j: pl.BlockSpec((S, LANES), lambda i: (0, _att_slab(g, j) + i))
    out = pl.BlockSpec((S, LANES), lambda i: (0, i))
    shp = jax.ShapeDtypeStruct((S, ATT_W), F32)
    return pl.pallas_call(body, name=f"att_fwd_g{g}", grid=(ATT_PAIRS,), in_specs=[slab(0), slab(1), slab(2)],
                          out_specs=[out, out], out_shape=[shp, shp], compiler_params=_cparams(("parallel",)))(P, P, P)


def _att_bwd(P, o, l, do, dl, g):
    S = P.shape[0]
    d = ATT_PATTERNS[g][1]

    def body(q_ref, k_ref, v_ref, o_ref, l_ref, do_ref, dl_ref, dq_ref, dk_ref, dv_ref, dq_acc, dk_acc, dv_acc):
        dk_acc[...] = jnp.zeros_like(dk_acc)
        dv_acc[...] = jnp.zeros_like(dv_acc)

        def group(i, carry):
            rows, has, (q, kc, kp, vc, vp, dob) = _att_operands(
                i, d, S, (q_ref, True), (k_ref, True), (k_ref, False), (v_ref, True), (v_ref, False), (do_ref, True))
            s_c_all, s_p_all = _dot16(q, kc, "nt"), _dot16(q, kp, "nt")
            dp_c_all, dp_p_all = _dot16(dob, vc, "nt"), _dot16(dob, vp, "nt")
            p_c, p_p, ds_c, ds_p = [], [], [], []
            for bb, (cur, _) in enumerate(rows):
                dd2 = do_ref[cur, :] * o_ref[cur, :] - dl_ref[cur, :]
                for h, (dd, lse) in enumerate(zip(_heads(dd2), _heads(l_ref[cur, :]))):
                    e = 2 * bb + h
                    s_c, s_p = _att_mask(s_c_all[e], s_p_all[e], has[bb])
                    pc, pp = jnp.exp(s_c - lse[:, 0:1]), jnp.exp(s_p - lse[:, 0:1])
                    delta = jnp.sum(dd, axis=1, keepdims=True)
                    p_c.append(pc)
                    p_p.append(pp)
                    ds_c.append(pc * (dp_c_all[e] - delta) * ATT_SCALE)
                    ds_p.append(pp * (dp_p_all[e] - delta) * ATT_SCALE)
            p_c, p_p, ds_c, ds_p = map(_stack, (p_c, p_p, ds_c, ds_p))
            dq = _dot16(ds_c, kc, "nn") + _dot16(ds_p, kp, "nn")
            dk_c, dk_p = _dot16(ds_c, q, "tn"), _dot16(ds_p, q, "tn")
            dv_c, dv_p = _dot16(p_c, dob, "tn"), _dot16(p_p, dob, "tn")
            pair = lambda x, bb: jnp.concatenate([x[2 * bb], x[2 * bb + 1]], axis=1)
            for bb, (cur, prv) in enumerate(rows):
                dq_acc[cur, :] = pair(dq, bb)
                dk_acc[cur, :] += pair(dk_c, bb)
                dv_acc[cur, :] += pair(dv_c, bb)
                dk_acc[prv, :] += pair(dk_p, bb)
                dv_acc[prv, :] += pair(dv_p, bb)
            return carry

        lax.fori_loop(0, S // QBLK // ATT_NB, group, 0)
        dq_ref[...] = dq_acc[...].astype(BF16)
        dk_ref[...] = dk_acc[...].astype(BF16)
        dv_ref[...] = dv_acc[...].astype(BF16)

    slab = lambda j: pl.BlockSpec((S, LANES), lambda i: (0, _att_slab(g, j) + i))
    blk128 = pl.BlockSpec((S, LANES), lambda i: (0, i))
    shp = jax.ShapeDtypeStruct((S, ATT_W), BF16)
    return pl.pallas_call(body, name=f"att_bwd_g{g}", grid=(ATT_PAIRS,),
                          in_specs=[slab(0), slab(1), slab(2)] + [blk128] * 4, out_specs=[blk128] * 3, out_shape=[shp] * 3,
                          scratch_shapes=[pltpu.VMEM((S, LANES), F32)] * 3,
                          compiler_params=_cparams(("parallel",)))(P, P, P, o, l, do, dl)


def _att_weights(l_refs):
    l0, l1, l2 = [r[...] for r in l_refs]
    m = jnp.maximum(jnp.maximum(l0, l1), l2)
    e = (jnp.exp(l0 - m), jnp.exp(l1 - m), jnp.exp(l2 - m))
    inv = 1.0 / (e[0] + e[1] + e[2])
    return [x * inv for x in e]


def _att_combine_fwd(os, ls, tm=512):
    S = os[0].shape[0]

    def body(o0, o1, o2, l0, l1, l2, a_ref):
        w = _att_weights((l0, l1, l2))
        a_ref[...] = (w[0] * o0[...] + w[1] * o1[...] + w[2] * o2[...]).astype(BF16)

    row = _rows(tm, ATT_W)
    return pl.pallas_call(body, name="att_combine_fwd", grid=(S // tm,), in_specs=[row] * 6, out_specs=row,
                          out_shape=jax.ShapeDtypeStruct((S, ATT_W), BF16),
                          compiler_params=_cparams(("parallel",)))(*os, *ls)


def _att_combine_bwd(da, os, ls, tm=512):
    S = da.shape[0]

    def body(da_ref, o0, o1, o2, l0, l1, l2, *out_refs):
        da = da_ref[...]
        w = _att_weights((l0, l1, l2))
        dw = (da * o0[...], da * o1[...], da * o2[...])
        mean = w[0] * dw[0] + w[1] * dw[1] + w[2] * dw[2]
        for g in range(3):
            out_refs[g][...] = w[g] * da
            out_refs[3 + g][...] = w[g] * (dw[g] - mean)

    row = _rows(tm, ATT_W)
    shp = jax.ShapeDtypeStruct((S, ATT_W), F32)
    return pl.pallas_call(body, name="att_combine_bwd", grid=(S // tm,), in_specs=[row] * 7, out_specs=[row] * 6,
                          out_shape=[shp] * 6, compiler_params=_cparams(("parallel",)))(da, *os, *ls)


@jax.custom_vjp
def _bdot(a, b):
    return jnp.dot(a.astype(BF16), b.astype(BF16), preferred_element_type=F32)


def _bdot_fwd(a, b):
    return _bdot(a, b), (a, b)


def _bdot_bwd(res, ct):
    a, b = res
    ct16 = ct.astype(BF16)
    da = lax.dot_general(ct16, b.astype(BF16), (((1,), (1,)), ((), ())), preferred_element_type=F32)
    db = lax.dot_general(a.astype(BF16), ct16, (((0,), (0,)), ((), ())), preferred_element_type=F32)
    return da, db


_bdot.defvjp(_bdot_fwd, _bdot_bwd)


def _two_piece_dot(x, m):
    hi = x.astype(BF16)
    lo = (x - hi.astype(F32)).astype(BF16)
    return jnp.dot(hi, m, preferred_element_type=F32) + jnp.dot(lo, m, preferred_element_type=F32)


def _head_sum_impl(x):
    sel = (lax.broadcasted_iota(jnp.int32, (D, LANES), 0) // HEAD == lax.broadcasted_iota(jnp.int32, (D, LANES), 1)).astype(BF16)
    sel_t = (lax.broadcasted_iota(jnp.int32, (LANES, D), 1) // HEAD == lax.broadcasted_iota(jnp.int32, (LANES, D), 0)).astype(BF16)
    return _two_piece_dot(_two_piece_dot(x, sel), sel_t)


@jax.custom_vjp
def _head_sum(x):
    return _head_sum_impl(x)


_head_sum.defvjp(lambda x: (_head_sum_impl(x), None), lambda _, ct: (_head_sum_impl(ct),))


def _softplus(z):
    return jnp.maximum(z, 0.0) + jnp.log(1.0 + jnp.exp(-jnp.abs(z)))


def _rwkv_prep_fn(zr, zrp, zk, zkp, zv, zvp, zl, zlp, mu_r, mu_k, mu_v, mu_l, w0, a0, k_k, k_a, w2, a2, g2p):
    r = zr + (zrp - zr) * mu_r
    k = zk + (zkp - zk) * mu_k
    v = zv + (zvp - zv) * mu_v
    lo = zl + (zlp - zl) * mu_l
    w_low, a_low, g_low = lo[:, 0:LORA_W], lo[:, LORA_W:LORA_W + LORA_A], lo[:, LANES:LANES + G_PAD]
    w_log = -_softplus(-(w0 + _bdot(jnp.tanh(w_low), w2))) - 0.5
    decay = -jnp.exp(w_log)
    a = jax.nn.sigmoid(a0 + _bdot(a_low, a2))
    g = _bdot(jax.nn.sigmoid(g_low), g2p)
    kmod = k * (1.0 + (a - 1.0) * k_a)
    kk = k * k_k
    kk = kk / jnp.maximum(jnp.sqrt(_head_sum(kk * kk)), 1e-12)
    return r, decay, kmod, v, -kk, kk * a, g


def _rwkv_prep_specs(tm, blk=lambda i: i):
    vec = _full((1, D))
    rows = lambda w, col: pl.BlockSpec((tm, w), lambda i: (blk(i), col))
    prev = lambda w, col: pl.BlockSpec((SUBLANES, w), lambda i: (jnp.maximum(blk(i) * (tm // SUBLANES) - 1, 0), col))
    slabs = []
    for col in (C_R // D, C_K // D, C_V // D):
        slabs += [rows(D, col), prev(D, col)]
    slabs += [rows(LORA_PAD, C_LORA // LORA_PAD), prev(LORA_PAD, C_LORA // LORA_PAD)]
    params = [vec, vec, vec, _full((1, LORA_PAD)), vec, vec, vec, vec,
              _full((LORA_W, D)), _full((LORA_A, D)), _full((G_PAD, D))]
    return slabs, params


def _prep_inputs(refs, first):
    vals = []
    for s in range(4):
        z = refs[2 * s][...]
        vals += [z, _shift_down(z, refs[2 * s + 1][...], 1, first)]
    return vals + [r[...] for r in refs[8:19]]


def _rwkv_prep(P, params, tm=256):
    S = P.shape[0]
    slabs, pspecs = _rwkv_prep_specs(tm)

    def body(*refs):
        outs = _rwkv_prep_fn(*_prep_inputs(refs, pl.program_id(0) == 0))
        for o_ref, val in zip(refs[19:], outs):
            o_ref[...] = val

    shp = jax.ShapeDtypeStruct((S, D), F32)
    return pl.pallas_call(body, name="rwkv_prep", grid=(S // tm,), in_specs=slabs + pspecs,
                          out_specs=[_rows(tm, D)] * 7, out_shape=[shp] * 7,
                          compiler_params=_cparams(("parallel",)))(*([P] * 8), *params)


def _rwkv_prep_bwd(P, params, cts_a, cts_b, tm=128):
    S = P.shape[0]
    nblk = S // tm
    blk = lambda i: nblk - 1 - i
    slabs, pspecs = _rwkv_prep_specs(tm, blk)
    has_b = [c is not None for c in cts_b]
    n_ct = 7 + sum(has_b)

    def body(*refs):
        start = pl.program_id(0) == 0
        ins = _prep_inputs(refs, pl.program_id(0) == nblk - 1)
        ct_refs = refs[19:19 + n_ct]
        out_refs = refs[19 + n_ct:19 + n_ct + 15]
        carry_refs = refs[19 + n_ct + 15:]

        @pl.when(start)
        def _():
            for c_ref in carry_refs:
                c_ref[...] = jnp.zeros_like(c_ref)

        cts, pos = [], 7
        for i in range(7):
            c = ct_refs[i][...]
            if has_b[i]:
                c = c + ct_refs[pos][...]
                pos += 1
            cts.append(c)
        _, vjp = jax.vjp(_rwkv_prep_fn, *ins)
        grads = vjp(tuple(cts))
        for s in range(4):
            shifted = grads[2 * s + 1]
            out_refs[s][...] = (grads[2 * s] + _shift_up(shifted, carry_refs[s][...], 1, start)).astype(BF16)
            carry_refs[s][0:1, :] = shifted[0:1, :]
        for i in range(11):
            _acc(out_refs[4 + i], grads[8 + i], start)

    ct_in = list(cts_a) + [c for c in cts_b if c is not None]
    row = lambda w: pl.BlockSpec((tm, w), lambda i: (blk(i), 0))
    f = jax.ShapeDtypeStruct
    zshapes = [f((S, D), BF16)] * 3 + [f((S, LORA_PAD), BF16)]
    pshapes = [f((1, D), F32)] * 3 + [f((1, LORA_PAD), F32)] + [f((1, D), F32)] * 4 + [f((LORA_W, D), F32), f((LORA_A, D), F32), f((G_PAD, D), F32)]
    return pl.pallas_call(
        body, name="rwkv_prep_bwd", grid=(nblk,),
        in_specs=slabs + pspecs + [row(D)] * n_ct,
        out_specs=[row(D), row(D), row(D), row(LORA_PAD)] + pspecs,
        out_shape=zshapes + pshapes,
        scratch_shapes=[pltpu.VMEM((SUBLANES, D), F32)] * 3 + [pltpu.VMEM((SUBLANES, LORA_PAD), F32)],
        compiler_params=_cparams(("arbitrary",)))(*([P] * 8), *params, *ct_in)


def _rwkv_post_fn(y, r, kmod, v, g, lnx_w, lnx_b, r_k):
    mean = _head_sum(y) * (1.0 / HEAD)
    yc = y - mean
    var = _head_sum(yc * yc) * (1.0 / HEAD)
    yn = yc * lax.rsqrt(var + GN_EPS) * lnx_w + lnx_b
    bonus = _head_sum(r * kmod * r_k) * v
    return (yn + bonus) * g


def _rwkv_post(y, r, kmod, v, g, lnx_w, lnx_b, r_k, tm=256):
    S = y.shape[0]

    def body(y_ref, r_ref, k_ref, v_ref, g_ref, w_ref, b_ref, rk_ref, o_ref):
        o_ref[...] = _rwkv_post_fn(y_ref[...], r_ref[...], k_ref[...], v_ref[...], g_ref[...],
                                   w_ref[...], b_ref[...], rk_ref[...]).astype(BF16)

    row, vec = _rows(tm, D), _full((1, D))
    return pl.pallas_call(body, name="rwkv_post", grid=(S // tm,), in_specs=[row] * 5 + [vec] * 3, out_specs=row,
                          out_shape=jax.ShapeDtypeStruct((S, D), BF16),
                          compiler_params=_cparams(("parallel",)))(y, r, kmod, v, g, lnx_w, lnx_b, r_k)


def _rwkv_post_bwd(drw, y, r, kmod, v, g, lnx_w, lnx_b, r_k, tm=256):
    S = y.shape[0]

    def body(d_ref, y_ref, r_ref, k_ref, v_ref, g_ref, w_ref, b_ref, rk_ref, *out_refs):
        first = pl.program_id(0) == 0
        _, vjp = jax.vjp(_rwkv_post_fn, y_ref[...], r_ref[...], k_ref[...], v_ref[...], g_ref[...],
                         w_ref[...], b_ref[...], rk_ref[...])
        grads = vjp(d_ref[...])
        for i in range(5):
            out_refs[i][...] = grads[i]
        for i in range(5, 8):
            _acc(out_refs[i], grads[i], first)

    row, vec = _rows(tm, D), _full((1, D))
    f = jax.ShapeDtypeStruct
    return pl.pallas_call(body, name="rwkv_post_bwd", grid=(S // tm,), in_specs=[row] * 6 + [vec] * 3,
                          out_specs=[row] * 5 + [vec] * 3, out_shape=[f((S, D), F32)] * 5 + [f((1, D), F32)] * 3,
                          compiler_params=_cparams(("arbitrary",)))(drw, y, r, kmod, v, g, lnx_w, lnx_b, r_k)


CHUNK = 64
CHUNK_TB = 256
_DOT_DIMS = {"nn": (((2,), (1,)), ((0,), (0,))), "nt": (((2,), (2,)), ((0,), (0,))), "tn": (((1,), (1,)), ((0,), (0,)))}


def _dot16(x, y, mode):
    return lax.dot_general(x.astype(BF16), y.astype(BF16), _DOT_DIMS[mode], preferred_element_type=F32)


@functools.partial(jax.custom_vjp, nondiff_argnums=(2,))
def _mm16(x, y, mode):
    return _dot16(x, y, mode)


def _mm16_fwd(x, y, mode):
    return _dot16(x, y, mode), (x, y)


def _mm16_bwd(mode, res, ct):
    x, y = res
    if mode == "nn":
        return _dot16(ct, y, "nt"), _dot16(x, ct, "tn")
    if mode == "nt":
        return _dot16(ct, y, "nn"), _dot16(ct, x, "tn")
    return _dot16(y, ct, "nt"), _dot16(x, ct, "nn")


_mm16.defvjp(_mm16_fwd, _mm16_bwd)


def _tri_sum(x, upper):
    T = x.shape[0]
    i = lax.broadcasted_iota(jnp.int32, (T, T), 0)
    j = lax.broadcasted_iota(jnp.int32, (T, T), 1)
    tri = ((j >= i) if upper else (i >= j)).astype(BF16)
    out, rest = None, x
    for _ in range(3):
        piece = rest.astype(BF16)
        rest = rest - piece.astype(F32)
        part = jnp.dot(tri, piece, preferred_element_type=F32)
        out = part if out is None else out + part
    return out


@jax.custom_vjp
def _cumsum_rows(x):
    return _tri_sum(x, False)


_cumsum_rows.defvjp(lambda x: (_tri_sum(x, False), None), lambda _, ct: (_tri_sum(ct, True),))


def _rows_to_cols(x):
    H, _, K = x.shape
    eye = (lax.broadcasted_iota(jnp.int32, (H, K, K), 1) == lax.broadcasted_iota(jnp.int32, (H, K, K), 2)).astype(F32)
    out = lax.dot_general(eye, jnp.broadcast_to(x, (H, SUBLANES, K)), _DOT_DIMS["nt"],
                          precision=lax.Precision.HIGHEST, preferred_element_type=F32)
    return out[:, :, 0:1]


def _per_head(x):
    return jnp.concatenate([x[:, h * HEAD:(h + 1) * HEAD][None] for h in range(N_HEADS)], axis=0)


def _chunk_fn(st0, r, lw, k, v, a, b):
    T = r.shape[0]
    cl = _cumsum_rows(lw)
    cl_end = cl[T - 1:T, :]
    inv = jnp.exp(-cl)
    to_end = jnp.exp(cl_end - cl)
    ah, rh, bh, kh, be, ke, v3 = [_per_head(x) for x in
                                  (a * jnp.exp(cl - lw), r * jnp.exp(cl), b * inv, k * inv, b * to_end, k * to_end, v)]
    i = lax.broadcasted_iota(jnp.int32, (N_HEADS, T, T), 1)
    j = lax.broadcasted_iota(jnp.int32, (N_HEADS, T, T), 2)
    a_ab = jnp.where(i > j, _mm16(ah, bh, "nt"), 0.0)
    a_ak = jnp.where(i > j, _mm16(ah, kh, "nt"), 0.0)
    m_rb = jnp.where(i >= j, _mm16(rh, bh, "nt"), 0.0)
    m_rk = jnp.where(i >= j, _mm16(rh, kh, "nt"), 0.0)
    rhs = _mm16(ah, st0, "nn") + _mm16(a_ak, v3, "nn")
    power, solve, n = a_ab, (i == j).astype(F32) + a_ab, 1
    while 2 * n < T:
        power = _mm16(power, power, "nn")
        solve = solve + _mm16(solve, power, "nn")
        n *= 2
    sa = _mm16(solve, rhs, "nn")
    y3 = _mm16(rh, st0, "nn") + _mm16(m_rb, sa, "nn") + _mm16(m_rk, v3, "nn")
    st_end = _rows_to_cols(_per_head(jnp.exp(cl_end))) * st0 + _mm16(be, sa, "tn") + _mm16(ke, v3, "tn")
    return jnp.concatenate([y3[h] for h in range(N_HEADS)], axis=1), st_end


def _hosted_exchange(refs, n, broadcast, grid):
    if n == 0:
        return lambda: None
    start, wait = _exchange_ops(refs[:n], refs[n:2 * n], *refs[2 * n:], broadcast)
    first = functools.reduce(jnp.logical_and, [pl.program_id(a) == 0 for a in range(len(grid))])
    last = functools.reduce(jnp.logical_and, [pl.program_id(a) == g - 1 for a, g in enumerate(grid)])
    pl.when(first)(start)
    return lambda: pl.when(last)(wait)


def _cscan_fwd(r, lw, k, v, a, b, gather=()):
    S = r.shape[0]
    per_blk = CHUNK_TB // CHUNK
    n_x = len(gather)
    nblk = S // CHUNK_TB

    def body(*refs):
        r_ref, lw_ref, k_ref, v_ref, a_ref, b_ref = refs[:6]
        y_ref, ck_ref = refs[6 + n_x:8 + n_x]
        st_ref = refs[8 + 2 * n_x]
        finish = _hosted_exchange(refs[6:6 + n_x] + refs[8 + n_x:8 + 2 * n_x] + refs[9 + 2 * n_x:], n_x, True, (nblk,))

        @pl.when(pl.program_id(0) == 0)
        def _():
            st_ref[...] = jnp.zeros_like(st_ref)

        def chunk(c, carry):
            rows = pl.ds(pl.multiple_of(c * CHUNK, CHUNK), CHUNK)
            st0 = st_ref[...]
            ck_ref[c] = st0
            y, st_end = _chunk_fn(st0, r_ref[rows, :], lw_ref[rows, :], k_ref[rows, :],
                                  v_ref[rows, :], a_ref[rows, :], b_ref[rows, :])
            y_ref[rows, :] = y
            st_ref[...] = st_end
            return carry

        lax.fori_loop(0, per_blk, chunk, 0)
        finish()

    blk = _rows(CHUNK_TB, D)
    any_spec = pl.BlockSpec(memory_space=pl.ANY)
    outs = pl.pallas_call(
        body, name="scan_fwd", grid=(nblk,), in_specs=[blk] * 6 + [any_spec] * n_x,
        out_specs=[blk, pl.BlockSpec((per_blk, N_HEADS, HEAD, HEAD), lambda i: (i, 0, 0, 0))] + [any_spec] * n_x,
        out_shape=[jax.ShapeDtypeStruct((S, D), F32), jax.ShapeDtypeStruct((S // CHUNK, N_HEADS, HEAD, HEAD), F32)]
        + _exchange_shapes(gather, True),
        scratch_shapes=[pltpu.VMEM((N_HEADS, HEAD, HEAD), F32)] + (_exchange_scratch(n_x) if n_x else []),
        compiler_params=_cparams(("arbitrary",)))(r, lw, k, v, a, b, *gather)
    return outs[0], outs[1], outs[2:]


def _cscan_bwd(r, lw, k, v, a, b, ckpt, dy, scatter=()):
    S = r.shape[0]
    per_blk = CHUNK_TB // CHUNK
    nblk = S // CHUNK_TB
    n_x = len(scatter)

    def body(*refs):
        r_ref, lw_ref, k_ref, v_ref, a_ref, b_ref, ck_ref, dy_ref = refs[:8]
        out_refs = refs[8 + n_x:14 + n_x]
        ds_ref = refs[14 + 2 * n_x]
        finish = _hosted_exchange(refs[8:8 + n_x] + refs[14 + n_x:14 + 2 * n_x] + refs[15 + 2 * n_x:], n_x, False, (nblk,))

        @pl.when(pl.program_id(0) == 0)
        def _():
            ds_ref[...] = jnp.zeros_like(ds_ref)

        def chunk(cc, carry):
            c = per_blk - 1 - cc
            rows = pl.ds(pl.multiple_of(c * CHUNK, CHUNK), CHUNK)
            ins = (ck_ref[c], r_ref[rows, :], lw_ref[rows, :], k_ref[rows, :], v_ref[rows, :], a_ref[rows, :], b_ref[rows, :])
            _, vjp = jax.vjp(_chunk_fn, *ins)
            grads = vjp((dy_ref[rows, :], ds_ref[...]))
            ds_ref[...] = grads[0]
            for o_ref, g in zip(out_refs, grads[1:]):
                o_ref[rows, :] = g
            return carry

        lax.fori_loop(0, per_blk, chunk, 0)
        finish()

    blk = pl.BlockSpec((CHUNK_TB, D), lambda i: (nblk - 1 - i, 0))
    any_spec = pl.BlockSpec(memory_space=pl.ANY)
    shp = jax.ShapeDtypeStruct((S, D), F32)
    outs = pl.pallas_call(
        body, name="scan_bwd", grid=(nblk,),
        in_specs=[blk] * 6 + [pl.BlockSpec((per_blk, N_HEADS, HEAD, HEAD), lambda i: (nblk - 1 - i, 0, 0, 0)), blk]
        + [any_spec] * n_x,
        out_specs=[blk] * 6 + [any_spec] * n_x, out_shape=[shp] * 6 + _exchange_shapes(scatter, False),
        scratch_shapes=[pltpu.VMEM((N_HEADS, HEAD, HEAD), F32)] + (_exchange_scratch(n_x) if n_x else []),
        compiler_params=_cparams(("arbitrary",)))(r, lw, k, v, a, b, ckpt, dy, *scatter)
    return outs[:6], outs[6:]


def _ada_partial(c_all, w_shard):
    def body(c_ref, w_ref, o_ref):
        o_ref[...] = jnp.dot(c_ref[...].astype(BF16), w_ref[...].astype(BF16), preferred_element_type=F32)

    vm = pl.BlockSpec(memory_space=pltpu.VMEM)
    return pl.pallas_call(body, name="ada_partial", in_specs=[vm, vm], out_specs=vm,
                          out_shape=jax.ShapeDtypeStruct((N_DEV, w_shard.shape[1]), F32),
                          compiler_params=pltpu.CompilerParams(vmem_limit_bytes=VMEM_LIMIT))(c_all, w_shard)


def _ada_bias(rows, b_ada):
    def body(r_ref, b_ref, o_ref):
        o_ref[...] = r_ref[...] + b_ref[...]

    vm = pl.BlockSpec(memory_space=pltpu.VMEM)
    return pl.pallas_call(body, name="ada_bias", in_specs=[vm, vm], out_specs=vm,
                          out_shape=jax.ShapeDtypeStruct(rows.shape, F32))(rows, b_ada)


def _ada_wgrad(c_cols, d_all):
    def body(c_ref, d_ref, o_ref):
        acc = c_ref[:, 0:1] * d_ref[0:1, :]
        for j in range(1, N_DEV):
            acc = acc + c_ref[:, j:j + 1] * d_ref[j:j + 1, :]
        o_ref[...] = acc

    vm = pl.BlockSpec(memory_space=pltpu.VMEM)
    return pl.pallas_call(body, name="ada_wgrad", in_specs=[vm, vm], out_specs=vm,
                          out_shape=jax.ShapeDtypeStruct((D, d_all.shape[1]), F32),
                          compiler_params=pltpu.CompilerParams(vmem_limit_bytes=VMEM_LIMIT))(c_cols, d_all)


def _exchange(srcs, broadcast, name):
    n = len(srcs)

    def body(*refs):
        start, wait = _exchange_ops(refs[:n], refs[n:2 * n], *refs[2 * n:], broadcast)
        start()
        wait()

    any_spec = pl.BlockSpec(memory_space=pl.ANY)
    return pl.pallas_call(
        body, name=name, out_shape=_exchange_shapes(srcs, broadcast), in_specs=[any_spec] * n, out_specs=[any_spec] * n,
        scratch_shapes=_exchange_scratch(n),
        compiler_params=pltpu.CompilerParams(has_side_effects=True),
    )(*srcs)


def _gather_via_sibling(srcs, name):
    n = len(srcs)

    def body(*refs):
        src_refs, out_refs = refs[:n], refs[n:2 * n]
        send_sems, recv_sems, local_sems = refs[2 * n:]
        x, y, c = lax.axis_index("x"), lax.axis_index("y"), lax.axis_index("c")
        me, sibling = (x, y, c), (x, y, 1 - c)
        chips = [(1 - x, y), (x, 1 - y), (1 - x, 1 - y)]

        def slot(px, py, pc):
            return 4 * px + 2 * py + pc

        def copy(i, k, block, to, src=None):
            rows = out_refs[i].at[slot(*block)]
            return pltpu.make_async_remote_copy(
                src_ref=rows if src is None else src, dst_ref=rows, send_sem=send_sems.at[i, k],
                recv_sem=recv_sems.at[i, k], device_id=to, device_id_type=_MESH)

        local = [pltpu.make_async_copy(src_refs[i], out_refs[i].at[slot(*me)], local_sems.at[i]) for i in range(n)]
        for cp in local:
            cp.start()
        first = [copy(i, 0, me, sibling, src=src_refs[i]) for i in range(n)]
        first += [copy(i, 1 + j, me, (*chip, c), src=src_refs[i]) for j, chip in enumerate(chips) for i in range(n)]
        for cp in first:
            cp.start()
        passed = []
        for j, chip in enumerate(chips):
            for i in range(n):
                copy(i, 1 + j, (*chip, c), me).wait_recv()
                passed.append(copy(i, 4 + j, (*chip, c), sibling))
                passed[-1].start()
        for i in range(n):
            copy(i, 0, sibling, me).wait_recv()
            for j, chip in enumerate(chips):
                copy(i, 4 + j, (*chip, 1 - c), me).wait_recv()
        for cp in first + passed:
            cp.wait_send()
        for cp in local:
            cp.wait()

    any_spec = pl.BlockSpec(memory_space=pl.ANY)
    return pl.pallas_call(
        body, name=name, out_shape=_exchange_shapes(srcs, True), in_specs=[any_spec] * n, out_specs=[any_spec] * n,
        scratch_shapes=_exchange_scratch(n),
        compiler_params=pltpu.CompilerParams(has_side_effects=True),
    )(*srcs)


def _flags(broadcast, n):
    return [broadcast] * n if isinstance(broadcast, bool) else list(broadcast)


def _exchange_shapes(srcs, broadcast):
    return [jax.ShapeDtypeStruct((N_DEV,) + (s.shape if bc else s.shape[1:]), s.dtype)
            for s, bc in zip(srcs, _flags(broadcast, len(srcs)))]


def _exchange_scratch(n):
    return [pltpu.SemaphoreType.DMA((n, N_DEV)), pltpu.SemaphoreType.DMA((n, N_DEV)), pltpu.SemaphoreType.DMA((n,))]


def _exchange_ops(src_refs, out_refs, send_sems, recv_sems, local_sems, broadcast):
    n = len(src_refs)
    flags = _flags(broadcast, n)
    x, y, c = lax.axis_index("x"), lax.axis_index("y"), lax.axis_index("c")
    me = 4 * x + 2 * y + c

    def block(i, j):
        return src_refs[i] if flags[i] else src_refs[i].at[j]

    def remote(i, d, src_slot, dst_slot):
        px, py, pc = x ^ (d >> 2), y ^ ((d >> 1) & 1), c ^ (d & 1)
        return pltpu.make_async_remote_copy(
            src_ref=block(i, src_slot), dst_ref=out_refs[i].at[dst_slot], send_sem=send_sems.at[i, d],
            recv_sem=recv_sems.at[i, d], device_id=(px, py, pc), device_id_type=_MESH)

    def local(i):
        return pltpu.make_async_copy(block(i, me), out_refs[i].at[me], local_sems.at[i])

    def start():
        for i in range(n):
            local(i).start()
        for d in range(1, N_DEV):
            for i in range(n):
                remote(i, d, me ^ d, me).start()

    def wait():
        for d in range(1, N_DEV):
            for i in range(n):
                remote(i, d, me, me ^ d).wait_recv()
        for d in range(1, N_DEV):
            for i in range(n):
                remote(i, d, me ^ d, me).wait_send()
        for i in range(n):
            local(i).wait()

    return start, wait


def _adamw(w, g, m, v):
    nm = ADAM_B1 * m + (1.0 - ADAM_B1) * g
    nv = ADAM_B2 * v + (1.0 - ADAM_B2) * (g * g)
    m_hat = nm * (1.0 / (1.0 - ADAM_B1 ** ADAM_STEP))
    v_hat = nv * (1.0 / (1.0 - ADAM_B2 ** ADAM_STEP))
    return -ADAM_LR * (m_hat / (jnp.sqrt(v_hat) + ADAM_EPS) + ADAM_WD * w), nm, nv


def _adam_vectors(parts, ws, ms, vs):
    nv = len(ws)
    sizes = [w.shape[1] for w in ws]

    def body(*refs):
        p_ref = refs[0]
        w_refs, m_refs, v_refs = refs[1:1 + nv], refs[1 + nv:1 + 2 * nv], refs[1 + 2 * nv:1 + 3 * nv]
        out_refs = refs[1 + 3 * nv:]
        g_all = p_ref[0]
        for j in range(1, N_DEV):
            g_all = g_all + p_ref[j]
        off = 0
        for i, n in enumerate(sizes):
            g = g_all[:, off:off + n]
            off += -(-n // LANES) * LANES
            delta, new_m, new_v = _adamw(w_refs[i][...], g, m_refs[i][...], v_refs[i][...])
            for o_ref, val in zip(out_refs[4 * i:4 * i + 4], (g, delta, new_m, new_v)):
                o_ref[...] = val

    vm = pl.BlockSpec(memory_space=pltpu.VMEM)
    outs = pl.pallas_call(body, name="adam_replicated", in_specs=[vm] * (1 + 3 * nv), out_specs=[vm] * (4 * nv),
                          out_shape=[jax.ShapeDtypeStruct((1, n), F32) for n in sizes for _ in range(4)])(parts, *ws, *ms, *vs)
    return [outs[4 * i:4 * i + 4] for i in range(nv)]


def _sum_adam(parts, w, m, v, name):
    n_parts, R, C = parts.shape
    fits = [t for t in range(16, R + 1, 16) if R % t == 0 and t * C <= 2504 * LANES]
    if fits:
        tm, tc = max(fits), C
    elif C % (2 * LANES) == 0 and R * C > 2504 * LANES:
        tm, tc = R, 2 * LANES
    else:
        tm, tc = R, C

    def body(p_ref, w_ref, m_ref, v_ref, g_ref, d_ref, nm_ref, nv_ref):
        g = p_ref[0].astype(F32)
        for j in range(1, n_parts):
            g = g + p_ref[j].astype(F32)
        g_ref[...] = g
        d_ref[...], nm_ref[...], nv_ref[...] = _adamw(w_ref[...], g, m_ref[...], v_ref[...])

    blk = pl.BlockSpec((tm, tc), lambda i, j: (i, j))
    shp = jax.ShapeDtypeStruct((R, C), F32)
    return pl.pallas_call(body, name=name, grid=(R // tm, C // tc),
                          in_specs=[pl.BlockSpec((n_parts, tm, tc), lambda i, j: (0, i, j)), blk, blk, blk],
                          out_specs=[blk] * 4, out_shape=[shp] * 4,
                          compiler_params=_cparams(("parallel", "parallel")))(parts, w, m, v)


TRANSPOSED = ("w_in", "w_up")
SHARDED = (("w_ada", 1), ("w_in", 0), ("w2", 1), ("a2", 1), ("g2", 1), ("w_att_out", 1), ("w_rwkv_out", 0),
           ("w_o", 0), ("w_up", 0), ("conv_w", 1), ("w_down", 0))
EARLY, LATE = SHARDED[1:5], SHARDED[5:]
REPLICATED = ("b_ada", "norm1_w", "b_gate", "mu_shift", "w0", "a0", "k_k", "k_a", "r_k", "lnx_w", "lnx_b",
              "norm2_w", "conv_b", "norm_f_w")
WEIGHTS = ("w_ada", "b_ada", "norm1_w", "w_in", "b_gate", "mu_shift", "w0", "w2", "a0", "a2", "g2", "k_k", "k_a", "r_k",
           "lnx_w", "lnx_b", "w_att_out", "w_rwkv_out", "w_o", "norm2_w", "w_up", "conv_w", "conv_b", "w_down", "norm_f_w")


W_IN_RUNS = ((0, C_ATT, ATT_IN), (ATT_IN, C_R, 3 * D), (ATT_IN + 3 * D, C_LORA, LORA_W + LORA_A),
             (ATT_IN + 3 * D + LORA_W + LORA_A, C_LORA + LANES, LORA_G), (ATT_IN + RWKV_IN, C_GA, 2 * D))
W_IN_SHARD = N_IN // N_DEV


def _pad_w_in(shards):
    pieces = []
    for orig, _, count in sorted(W_IN_RUNS, key=lambda run: run[1]):
        for j in range(orig // W_IN_SHARD, (orig + count - 1) // W_IN_SHARD + 1):
            lo, hi = max(orig, j * W_IN_SHARD), min(orig + count, (j + 1) * W_IN_SHARD)
            pieces.append(shards[j, lo - j * W_IN_SHARD:hi - j * W_IN_SHARD])
    pieces.append(jnp.zeros((LORA_PAD - LANES - LORA_G, shards.shape[2]), shards.dtype))
    return jnp.concatenate(pieces, axis=0)


def _w_in_blocks(g):
    blocks = []
    for j in range(N_DEV):
        pieces = []
        for orig, pad, count in W_IN_RUNS:
            lo, hi = max(orig, j * W_IN_SHARD), min(orig + count, (j + 1) * W_IN_SHARD)
            if lo < hi:
                pieces.append(g[pad + lo - orig:pad + hi - orig])
        blocks.append(jnp.concatenate(pieces, axis=0)[None])
    return jnp.concatenate(blocks, axis=0)


def _pad_mu(mu):
    lo = mu[:, 3 * D:]
    mu_l = jnp.concatenate([lo[:, :LORA_W + LORA_A], lo[:, LORA_W + LORA_A:], jnp.zeros((1, LORA_PAD - LANES - LORA_G), mu.dtype)], axis=1)
    return mu[:, :D], mu[:, D:2 * D], mu[:, 2 * D:3 * D], mu_l


def _local_step(x, ada, W, late_shards, target):
    S = x.shape[0]
    W = dict(W)
    G = {}
    sh1, sc1, gt1, sh2, sc2, gt2 = [ada[:, i * D:(i + 1) * D] for i in range(6)]
    h1, rstd1 = _norm_fwd(x, None, None, W["norm1_w"], sc1, sh1, "norm1_fwd")
    w_in_p = _pad_w_in(W["w_in"])
    P = _mm(h1, w_in_p, "nt", F32, "proj_in")

    mu_r, mu_k, mu_v, mu_l = _pad_mu(W["mu_shift"])
    g2p = jnp.pad(W["g2"], ((0, G_PAD - LORA_G), (0, 0)))
    prep_params = [mu_r, mu_k, mu_v, mu_l, W["w0"], W["a0"], W["k_k"], W["k_a"], W["w2"], W["a2"], g2p]
    r_, dec, kmod, v_, aa, bb, gg = _rwkv_prep(P, prep_params)
    y_scan, states, late = _cscan_fwd(r_, dec, kmod, v_, aa, bb, gather=late_shards)
    W.update({n: _full_weight(g, axis) for (n, axis), g in zip(LATE, late)})

    o_g, l_g = zip(*[_att_fwd(P, g) for g in range(len(ATT_PATTERNS))])
    att = _att_combine_fwd(o_g, l_g)
    y_att = _mm(att, W["w_att_out"], "nn", F32, "att_out")
    r_k = W["r_k"].reshape(1, D)
    rw = _rwkv_post(y_scan, r_, kmod, v_, gg, W["lnx_w"], W["lnx_b"], r_k)
    y_rwkv = _mm(rw, W["w_rwkv_out"], "nn", F32, "rwkv_out")

    bga, bgr = W["b_gate"][:, :D], W["b_gate"][:, D:]
    mix = _gate_fwd(P, bga, bgr, y_att, y_rwkv)
    mo = _mm(mix, W["w_o"], "nn", F32, "mix_out")
    x2, h2, rstd2 = _norm_fwd(x, mo, gt1, W["norm2_w"], sc2, sh2, "norm2_fwd")
    u = _mm(h2, W["w_up"], "nt", BF16, "ffn_up")
    conv_w8 = jnp.pad(W["conv_w"], ((0, SUBLANES - 3), (0, 0)))
    act = _conv_fwd(u, conv_w8, W["conv_b"])
    f = _mm(act, W["w_down"], "nn", F32, "ffn_down")
    loss_blk, dx3, df, dgt2, G["norm_f_w"] = _final(x2, f, gt2, W["norm_f_w"], target)
    loss = loss_blk[0, 0]

    dact = _mm(df, W["w_down"], "nt", BF16, "ffn_down_dx")
    G["w_down"] = _mm(act, df, "tn", BF16, "ffn_down_dw")
    duc, dwg, dwv, dbg, dbv = _conv_bwd_a(dact, u, conv_w8, W["conv_b"])
    G["conv_w"] = jnp.concatenate([dwg[0:3], dwv[0:3]], axis=1)
    G["conv_b"] = jnp.concatenate([dbg, dbv], axis=1)
    du = _conv_bwd_b(duc, conv_w8)
    dh2 = _mm(du, W["w_up"], "nn", F32, "ffn_up_dx")
    G["w_up"] = _mm(du, h2, "tn", BF16, "ffn_up_dw")
    dx2, dsh2, dsc2, G["norm2_w"], dmo, dgt1 = _norm_bwd(dh2, x2, rstd2, W["norm2_w"], sc2, dx3, mo, gt1, "norm2_bwd")
    dmix = _mm(dmo, W["w_o"], "nt", F32, "mix_out_dx")
    G["w_o"] = _mm(mix, dmo, "tn", BF16, "mix_out_dw")
    dy_att, dy_rwkv, dpga, dpgr, dbga, dbgr = _gate_bwd(dmix, P, bga, bgr, y_att, y_rwkv)
    G["b_gate"] = jnp.concatenate([dbga, dbgr], axis=1)

    datt = _mm(dy_att, W["w_att_out"], "nt", F32, "att_out_dx")
    G["w_att_out"] = _mm(att, dy_att, "tn", BF16, "att_out_dw")
    dcomb = _att_combine_bwd(datt, o_g, l_g)
    dp_att = []
    for g in range(len(ATT_PATTERNS)):
        dp_att += _att_bwd(P, o_g[g], l_g[g], dcomb[g], dcomb[3 + g], g)

    drw = _mm(dy_rwkv, W["w_rwkv_out"], "nt", F32, "rwkv_out_dx")
    G["w_rwkv_out"] = _mm(rw, dy_rwkv, "tn", BF16, "rwkv_out_dw")
    dy_scan, dr1, dk1, dv1, dgg, G["lnx_w"], G["lnx_b"], drk = _rwkv_post_bwd(drw, y_scan, r_, kmod, v_, gg, W["lnx_w"], W["lnx_b"], r_k)
    G["r_k"] = drk.reshape(W["r_k"].shape)
    late_blocks = [_owner_blocks(G[n], axis) for n, axis in LATE] if late_shards else []
    (dr2, ddec, dk2, dv2, daa, dbb), late_parts = _cscan_bwd(r_, dec, kmod, v_, aa, bb, states, dy_scan, scatter=late_blocks)
    pb = _rwkv_prep_bwd(P, prep_params, [dr2, ddec, dk2, dv2, daa, dbb, dgg], [dr1, None, dk1, dv1, None, None, None])
    dp_rkv, dp_lora, dpar = list(pb[0:3]), pb[3], pb[4:]
    dmu_r, dmu_k, dmu_v, dmu_l, G["w0"], G["a0"], G["k_k"], G["k_a"], G["w2"], G["a2"], dg2p = dpar
    G["g2"] = dg2p[0:LORA_G]
    G["mu_shift"] = jnp.concatenate([dmu_r, dmu_k, dmu_v, dmu_l[:, :LORA_W + LORA_A], dmu_l[:, LANES:LANES + LORA_G]], axis=1)

    dP = jnp.concatenate(dp_rkv + [dpga, dpgr] + dp_att + [dp_lora], axis=1)
    G["w_in"] = _w_in_blocks(_mm(dP, h1, "tn", BF16, "proj_in_dw"))
    if late_shards:
        dh1, (w_in_parts,) = _mm(dP, w_in_p, "nn", F32, "proj_in_dx", scatter=[G["w_in"]])
        done = dict(zip([n for n, _ in LATE] + ["w_in"], list(late_parts) + [w_in_parts]))
    else:
        dh1, done = _mm(dP, w_in_p, "nn", F32, "proj_in_dx"), {}
    grad_x, dsh1, dsc1, G["norm1_w"] = _norm_bwd(dh1, x, rstd1, W["norm1_w"], sc1, dx2, None, None, "norm1_bwd")
    dada = jnp.concatenate([dsh1, dsc1, dgt1, dsh2, dsc2, dgt2], axis=1)
    G["b_ada"] = dada
    return loss, grad_x, G, done


def _full_weight(gathered, axis):
    _, rows, cols = gathered.shape
    if axis == 0:
        return gathered.reshape(N_DEV * rows, cols)
    return gathered.transpose(1, 0, 2).reshape(rows, N_DEV * cols)


def _owner_blocks(g, axis):
    rows, cols = g.shape
    g = g.astype(BF16)
    if axis == 0:
        return g.reshape(N_DEV, rows // N_DEV, cols)
    return g.reshape(rows, N_DEV, cols // N_DEV).transpose(1, 0, 2)


def kernel(x, c, w_ada, b_ada, norm1_w, w_in, b_gate, mu_shift, w0, w2, a0, a2, g2, k_k, k_a, r_k, lnx_w, lnx_b, w_att_out, w_rwkv_out, w_o, norm2_w, w_up, conv_w, conv_b, w_down, norm_f_w, loss_target, m_w_ada, m_b_ada, m_norm1_w, m_w_in, m_b_gate, m_mu_shift, m_w0, m_w2, m_a0, m_a2, m_g2, m_k_k, m_k_a, m_r_k, m_lnx_w, m_lnx_b, m_w_att_out, m_w_rwkv_out, m_w_o, m_norm2_w, m_w_up, m_conv_w, m_conv_b, m_w_down, m_norm_f_w, v_w_ada, v_b_ada, v_norm1_w, v_w_in, v_b_gate, v_mu_shift, v_w0, v_w2, v_a0, v_a2, v_g2, v_k_k, v_k_a, v_r_k, v_lnx_w, v_lnx_b, v_w_att_out, v_w_rwkv_out, v_w_o, v_norm2_w, v_w_up, v_conv_w, v_conv_b, v_w_down, v_norm_f_w):
    env = dict(locals())
    w_shard = {n: env[n] for n in WEIGHTS}
    m_shard = {n: env["m_" + n] for n in WEIGHTS}
    v_shard = {n: env["v_" + n] for n in WEIGHTS}

    def mat(shards, n):
        return jnp.swapaxes(shards[n][0], 0, 1) if n in TRANSPOSED else shards[n][0]

    c_all, *gathered = _gather_via_sibling([c] + [mat(w_shard, n).astype(BF16) for n, _ in EARLY], "gather_weights")
    c_all = c_all.reshape(N_DEV, D)
    W = {n: g if n == "w_in" else _full_weight(g, axis) for (n, axis), g in zip(EARLY, gathered)}
    for n in REPLICATED:
        W[n] = w_shard[n].reshape(1, -1) if n != "r_k" else w_shard[n][0]
    ada_cols = _ada_partial(c_all, w_shard["w_ada"][0])
    ada_rows, = _exchange([ada_cols[:, None, :]], False, "ada_rows")
    ada = _ada_bias(ada_rows.reshape(1, -1), w_shard["b_ada"])

    late_shards = [mat(w_shard, n).astype(BF16) for n, _ in LATE]
    loss, grad_x, G, parts = _local_step(x[0], ada, W, late_shards, loss_target[0])
    loss = lax.psum(loss, ("x", "y", "c"))

    row = lambda a: a.reshape(1, -1)
    small = jnp.concatenate([jnp.pad(row(G[n]), ((0, 0), (0, (-G[n].size) % LANES))) for n in REPLICATED], axis=1)
    sparts, dada_all = _exchange([small, G["b_ada"].reshape(N_DEV, 1, -1)], [True, False], "gather_small_grads")
    parts["w_ada"] = _ada_wgrad(c_all.T, dada_all.reshape(N_DEV, -1))[None]

    rest = [(n, axis) for n, axis in SHARDED if n not in parts]
    parts.update(zip([n for n, _ in rest], _exchange([_owner_blocks(G[n], axis) for n, axis in rest], False, "scatter_grads")))
    out = {}
    for n, p in parts.items():
        res = _sum_adam(p, mat(w_shard, n), mat(m_shard, n), mat(v_shard, n), "adam_" + n)
        if n in TRANSPOSED:
            res = [jnp.swapaxes(a, 0, 1) for a in res]
        for kind, a in zip(("grad", "delta", "new_m", "new_v"), res):
            out[kind, n] = a[None]

    res = _adam_vectors(sparts, *[[row(s[n]) for n in REPLICATED] for s in (w_shard, m_shard, v_shard)])
    for n, four in zip(REPLICATED, res):
        for kind, a in zip(("grad", "delta", "new_m", "new_v"), four):
            out[kind, n] = a.reshape(w_shard[n].shape)

    return (loss, grad_x[None], *[out[kind, n] for kind in ("grad", "delta", "new_m", "new_v") for n in WEIGHTS])
```

```python
import functools

import jax
import jax.numpy as jnp
from jax import lax
from jax.experimental import pallas as pl
from jax.experimental.pallas import tpu as pltpu

F32 = jnp.float32
BF16 = jnp.bfloat16

D = 1024
HEAD = 64
ATT_PATTERNS = ((128, 1), (512, 4), (2048, 16))
ATT_HEADS = 8
ATT_W = ATT_HEADS * HEAD
ATT_IN = 3 * 3 * ATT_W
QBLK = 128
N_HEADS = D // HEAD
LORA_W, LORA_A, LORA_G = 64, 64, 160
RWKV_IN = 3 * D + LORA_W + LORA_A + LORA_G
N_IN = ATT_IN + RWKV_IN + 2 * D
D_FF = 2816
RMS_EPS = 1e-6
GN_EPS = 64e-5
N_DEV = 8
LANES = 128
SUBLANES = 8

C_R, C_K, C_V, C_GA, C_GR = 0, 1024, 2048, 3072, 4096
C_ATT = 5120
C_LORA = C_ATT + ATT_IN
LORA_PAD = 512
G_PAD = 256
N_PAD = C_LORA + LORA_PAD

ADAM_LR, ADAM_B1, ADAM_B2, ADAM_EPS, ADAM_WD, ADAM_STEP = 0.001, 0.9, 0.999, 1e-08, 0.01, 10

VMEM_LIMIT = 56 * 1024 * 1024

_MESH = pl.DeviceIdType.MESH


def _cparams(sem):
    return pltpu.CompilerParams(dimension_semantics=sem, vmem_limit_bytes=VMEM_LIMIT)


def _tile(dim, pref):
    if dim <= pref:
        return dim
    best = None
    for t in range(LANES, pref + 1, LANES):
        if dim % t == 0:
            best = t
    assert best is not None, dim
    return best


MM_TILES = {"nn": (1024, 1408, 1408), "nt": (1024, 2048, 1408), "tn": (1408, 1408, 1024)}


def _mm(a, b, mode, out_dtype, name, scatter=()):
    if mode == "nn":
        (M, K), (K2, N) = a.shape, b.shape
    elif mode == "nt":
        (M, K), (N, K2) = a.shape, b.shape
    else:
        (K, M), (K2, N) = a.shape, b.shape
    assert K == K2, (a.shape, b.shape, mode)
    tm, tn, tk = (_tile(dim, pref) for dim, pref in zip((M, N, K), MM_TILES[mode]))
    nk = K // tk
    grid = (M // tm, N // tn, nk)
    n_x = len(scatter)
    dims = {"nn": (((1,), (0,)), ((), ())), "nt": (((1,), (1,)), ((), ())), "tn": (((0,), (0,)), ((), ()))}[mode]

    def body(*refs):
        a_ref, b_ref = refs[:2]
        o_ref, acc_ref = refs[2 + n_x], refs[3 + 2 * n_x]
        finish = _hosted_exchange(refs[2:2 + n_x] + refs[3 + n_x:3 + 2 * n_x] + refs[4 + 2 * n_x:], n_x, False, grid)
        k = pl.program_id(2)
        part = lax.dot_general(a_ref[...].astype(BF16), b_ref[...].astype(BF16), dims,
                               preferred_element_type=F32)
        if nk == 1:
            o_ref[...] = part.astype(o_ref.dtype)
        else:
            @pl.when(k == 0)
            def _():
                acc_ref[...] = part

            @pl.when(jnp.logical_and(k > 0, k < nk - 1))
            def _():
                acc_ref[...] += part

            @pl.when(k == nk - 1)
            def _():
                o_ref[...] = (acc_ref[...] + part).astype(o_ref.dtype)
        finish()

    a_spec = pl.BlockSpec((tk, tm), lambda i, j, k: (k, i)) if mode == "tn" else pl.BlockSpec((tm, tk), lambda i, j, k: (i, k))
    b_spec = pl.BlockSpec((tn, tk), lambda i, j, k: (j, k)) if mode == "nt" else pl.BlockSpec((tk, tn), lambda i, j, k: (k, j))
    any_spec = pl.BlockSpec(memory_space=pl.ANY)
    outs = pl.pallas_call(
        body, name=name, grid=grid,
        in_specs=[a_spec, b_spec] + [any_spec] * n_x,
        out_specs=[pl.BlockSpec((tm, tn), lambda i, j, k: (i, j))] + [any_spec] * n_x,
        out_shape=[jax.ShapeDtypeStruct((M, N), out_dtype)] + _exchange_shapes(scatter, False),
        scratch_shapes=[pltpu.VMEM((tm, tn) if nk > 1 else (SUBLANES, LANES), F32)] + (_exchange_scratch(n_x) if n_x else []),
        compiler_params=_cparams(("arbitrary",) * 3 if n_x else ("parallel", "parallel", "arbitrary")),
    )(a, b, *scatter)
    return (outs[0], outs[1:]) if n_x else outs[0]


def _rows(tm, w, col=0):
    return pl.BlockSpec((tm, w), lambda i: (i, col))


def _full(shape):
    return pl.BlockSpec(shape, lambda i: (0,) * len(shape))


def _shift_down(x, halo, k, first):
    rolled = pltpu.roll(x, k, 0)
    row = lax.broadcasted_iota(jnp.int32, x.shape, 0)
    out = rolled
    n_halo = halo.shape[0]
    for j in range(k):
        h = jnp.where(first, 0.0, halo[n_halo - k + j:n_halo - k + j + 1, :])
        out = jnp.where(row == j, h, out)
    return out


def _shift_up(x, halo, k, last):
    n = x.shape[0]
    rolled = pltpu.roll(x, n - k, 0)
    row = lax.broadcasted_iota(jnp.int32, x.shape, 0)
    out = rolled
    for j in range(k):
        h = jnp.where(last, 0.0, halo[j:j + 1, :])
        out = jnp.where(row == n - k + j, h, out)
    return out


def _acc(ref, val, first):
    @pl.when(first)
    def _():
        ref[...] = val

    @pl.when(jnp.logical_not(first))
    def _():
        ref[...] += val


def _colsum(x):
    return jnp.sum(x, axis=0, keepdims=True)


def _norm_fwd(x, mo, gt, nw, sc, sh, name, tm=256):
    S = x.shape[0]
    has_res = mo is not None

    def body(*refs):
        if has_res:
            x_ref, mo_ref, gt_ref, nw_ref, sc_ref, sh_ref, x2_ref, h_ref, rs_ref = refs
            x2 = x_ref[...] + gt_ref[...] * mo_ref[...]
            x2_ref[...] = x2
        else:
            x_ref, nw_ref, sc_ref, sh_ref, h_ref, rs_ref = refs
            x2 = x_ref[...]
        rstd = lax.rsqrt(jnp.mean(x2 * x2, axis=-1, keepdims=True) + RMS_EPS)
        rs_ref[...] = rstd
        h_ref[...] = ((x2 * rstd * nw_ref[...]) * (1.0 + sc_ref[...]) + sh_ref[...]).astype(BF16)

    vec = _full((1, D))
    ins = [x, mo, gt, nw, sc, sh] if has_res else [x, nw, sc, sh]
    in_specs = [_rows(tm, D), _rows(tm, D), vec, vec, vec, vec] if has_res else [_rows(tm, D), vec, vec, vec]
    outs = [jax.ShapeDtypeStruct((S, D), BF16), jax.ShapeDtypeStruct((S, 1), F32)]
    out_specs = [_rows(tm, D), _rows(tm, 1)]
    if has_res:
        outs = [jax.ShapeDtypeStruct((S, D), F32)] + outs
        out_specs = [_rows(tm, D)] + out_specs
    return pl.pallas_call(body, name=name, grid=(S // tm,), in_specs=in_specs, out_specs=out_specs,
                          out_shape=outs, compiler_params=_cparams(("parallel",)))(*ins)


def _norm_bwd(dh, xin, rstd, nw, sc, dres, mo, gt, name, tm=256):
    S = xin.shape[0]
    has_res = mo is not None

    def body(*refs):
        if has_res:
            dh_ref, x_ref, rs_ref, nw_ref, sc_ref, dres_ref, mo_ref, gt_ref, dx_ref, dsh_ref, dsc_ref, dnw_ref, dmo_ref, dgt_ref = refs
        else:
            dh_ref, x_ref, rs_ref, nw_ref, sc_ref, dres_ref, dx_ref, dsh_ref, dsc_ref, dnw_ref = refs
        first = pl.program_id(0) == 0
        dh = dh_ref[...]
        rstd = rs_ref[...]
        n = x_ref[...] * rstd
        w = nw_ref[...]
        _acc(dsh_ref, _colsum(dh), first)
        _acc(dsc_ref, _colsum(dh * (n * w)), first)
        dnw = dh * (1.0 + sc_ref[...])
        _acc(dnw_ref, _colsum(dnw * n), first)
        dn = dnw * w
        dx = dres_ref[...] + rstd * (dn - n * jnp.mean(dn * n, axis=-1, keepdims=True))
        dx_ref[...] = dx
        if has_res:
            dmo_ref[...] = (dx * gt_ref[...]).astype(BF16)
            _acc(dgt_ref, _colsum(dx * mo_ref[...]), first)

    vec = _full((1, D))
    vshape = jax.ShapeDtypeStruct((1, D), F32)
    ins = [dh, xin, rstd, nw, sc, dres] + ([mo, gt] if has_res else [])
    in_specs = [_rows(tm, D), _rows(tm, D), _rows(tm, 1), vec, vec, _rows(tm, D)] + ([_rows(tm, D), vec] if has_res else [])
    outs = [jax.ShapeDtypeStruct((S, D), F32), vshape, vshape, vshape]
    out_specs = [_rows(tm, D), vec, vec, vec]
    if has_res:
        outs += [jax.ShapeDtypeStruct((S, D), BF16), vshape]
        out_specs += [_rows(tm, D), vec]
    return pl.pallas_call(body, name=name, grid=(S // tm,), in_specs=in_specs, out_specs=out_specs,
                          out_shape=outs, compiler_params=_cparams(("arbitrary",)))(*ins)


def _final(x2, f, gt2, nfw, target, tm=256):
    S = x2.shape[0]

    def body(x2_ref, f_ref, gt_ref, w_ref, t_ref, loss_ref, dx_ref, df_ref, dgt_ref, dw_ref):
        first = pl.program_id(0) == 0
        f = f_ref[...]
        gt = gt_ref[...]
        w = w_ref[...]
        x3 = x2_ref[...] + gt * f
        rstd = lax.rsqrt(jnp.mean(x3 * x3, axis=-1, keepdims=True) + RMS_EPS)
        n = x3 * rstd
        e = n * w - t_ref[...]
        part = 0.5 * jnp.sum(jnp.mean(e * e, axis=-1, keepdims=True), axis=0, keepdims=True)
        _acc(loss_ref, jnp.broadcast_to(part, (SUBLANES, LANES)), first)
        dy = e * (1.0 / D)
        _acc(dw_ref, _colsum(dy * n), first)
        dn = dy * w
        dx = rstd * (dn - n * jnp.mean(dn * n, axis=-1, keepdims=True))
        dx_ref[...] = dx
        df_ref[...] = (dx * gt).astype(BF16)
        _acc(dgt_ref, _colsum(dx * f), first)

    vec = _full((1, D))
    vshape = jax.ShapeDtypeStruct((1, D), F32)
    return pl.pallas_call(
        body, name="final_loss", grid=(S // tm,),
        in_specs=[_rows(tm, D), _rows(tm, D), vec, vec, _rows(tm, D)],
        out_specs=[_full((SUBLANES, LANES)), _rows(tm, D), _rows(tm, D), vec, vec],
        out_shape=[jax.ShapeDtypeStruct((SUBLANES, LANES), F32), jax.ShapeDtypeStruct((S, D), F32),
                   jax.ShapeDtypeStruct((S, D), BF16), vshape, vshape],
        compiler_params=_cparams(("arbitrary",)))(x2, f, gt2, nfw, target)


def _gate_fwd(P, bga, bgr, y_att, y_rwkv, tm=256):
    S = P.shape[0]

    def body(pa_ref, pr_ref, ba_ref, br_ref, ya_ref, yr_ref, mix_ref):
        ga = jax.nn.sigmoid(pa_ref[...] + ba_ref[...])
        gr = jax.nn.sigmoid(pr_ref[...] + br_ref[...])
        mix_ref[...] = (ga * ya_ref[...] + gr * yr_ref[...]).astype(BF16)

    vec = _full((1, D))
    return pl.pallas_call(
        body, name="gate_fwd", grid=(S // tm,),
        in_specs=[_rows(tm, D, C_GA // D), _rows(tm, D, C_GR // D), vec, vec, _rows(tm, D), _rows(tm, D)],
        out_specs=_rows(tm, D), out_shape=jax.ShapeDtypeStruct((S, D), BF16),
        compiler_params=_cparams(("parallel",)))(P, P, bga, bgr, y_att, y_rwkv)


def _gate_bwd(dmix, P, bga, bgr, y_att, y_rwkv, tm=256):
    S = P.shape[0]

    def body(dm_ref, pa_ref, pr_ref, ba_ref, br_ref, ya_ref, yr_ref, dya_ref, dyr_ref, dpa_ref, dpr_ref, dba_ref, dbr_ref):
        first = pl.program_id(0) == 0
        dm = dm_ref[...]
        ga = jax.nn.sigmoid(pa_ref[...] + ba_ref[...])
        gr = jax.nn.sigmoid(pr_ref[...] + br_ref[...])
        dya_ref[...] = (dm * ga).astype(BF16)
        dyr_ref[...] = (dm * gr).astype(BF16)
        dpa = dm * ya_ref[...] * ga * (1.0 - ga)
        dpr = dm * yr_ref[...] * gr * (1.0 - gr)
        dpa_ref[...] = dpa.astype(BF16)
        dpr_ref[...] = dpr.astype(BF16)
        _acc(dba_ref, _colsum(dpa), first)
        _acc(dbr_ref, _colsum(dpr), first)

    vec = _full((1, D))
    row = _rows(tm, D)
    rshape = jax.ShapeDtypeStruct((S, D), BF16)
    vshape = jax.ShapeDtypeStruct((1, D), F32)
    return pl.pallas_call(
        body, name="gate_bwd", grid=(S // tm,),
        in_specs=[row, _rows(tm, D, C_GA // D), _rows(tm, D, C_GR // D), vec, vec, row, row],
        out_specs=[row, row, row, row, vec, vec],
        out_shape=[rshape, rshape, rshape, rshape, vshape, vshape],
        compiler_params=_cparams(("arbitrary",)))(dmix, P, P, bga, bgr, y_att, y_rwkv)


CONV_TN = D_FF // 2
HALO = 16


def _conv_fwd(u, conv_w8, conv_b, tm=256, tn=CONV_TN):
    S = u.shape[0]
    nj = D_FF // tn

    def conv(u_ref, h_ref, w_ref, b_ref, first):
        u = u_ref[...].astype(F32)
        h = h_ref[...].astype(F32)
        w = w_ref[...]
        return b_ref[...] + w[0:1] * _shift_down(u, h, 2, first) + w[1:2] * _shift_down(u, h, 1, first) + w[2:3] * u

    def body(ug_ref, hg_ref, uv_ref, hv_ref, wg_ref, wv_ref, bg_ref, bv_ref, act_ref):
        first = pl.program_id(0) == 0
        g = conv(ug_ref, hg_ref, wg_ref, bg_ref, first)
        v = conv(uv_ref, hv_ref, wv_ref, bv_ref, first)
        act_ref[...] = (g * jax.nn.sigmoid(g) * v).astype(BF16)

    blk = lambda off: pl.BlockSpec((tm, tn), lambda i, j: (i, j + off))
    halo = lambda off: pl.BlockSpec((HALO, tn), lambda i, j: (jnp.maximum(i * (tm // HALO) - 1, 0), j + off))
    wsp = lambda off: pl.BlockSpec((SUBLANES, tn), lambda i, j: (0, j + off))
    bsp = lambda off: pl.BlockSpec((1, tn), lambda i, j: (0, j + off))
    return pl.pallas_call(
        body, name="conv_fwd", grid=(S // tm, nj),
        in_specs=[blk(0), halo(0), blk(nj), halo(nj), wsp(0), wsp(nj), bsp(0), bsp(nj)],
        out_specs=pl.BlockSpec((tm, tn), lambda i, j: (i, j)),
        out_shape=jax.ShapeDtypeStruct((S, D_FF), BF16),
        compiler_params=_cparams(("parallel", "parallel")))(u, u, u, u, conv_w8, conv_w8, conv_b, conv_b)


def _conv_bwd_a(dact, u, conv_w8, conv_b, tm=256, tn=CONV_TN):
    S = u.shape[0]
    nj = D_FF // tn

    def half(u_ref, h_ref, w_ref, b_ref, first):
        u = u_ref[...].astype(F32)
        h = h_ref[...].astype(F32)
        w = w_ref[...]
        u2, u1 = _shift_down(u, h, 2, first), _shift_down(u, h, 1, first)
        return b_ref[...] + w[0:1] * u2 + w[1:2] * u1 + w[2:3] * u, (u2, u1, u)

    def wgrad(d, taps):
        z = jnp.zeros((SUBLANES - 3, d.shape[1]), F32)
        return jnp.concatenate([_colsum(d * taps[0]), _colsum(d * taps[1]), _colsum(d * taps[2]), z], axis=0)

    def body(da_ref, ug_ref, hg_ref, uv_ref, hv_ref, wg_ref, wv_ref, bg_ref, bv_ref,
             d_ref, dwg_ref, dwv_ref, dbg_ref, dbv_ref):
        first = pl.program_id(1) == 0
        g, tg = half(ug_ref, hg_ref, wg_ref, bg_ref, first)
        v, tv = half(uv_ref, hv_ref, wv_ref, bv_ref, first)
        da = da_ref[...].astype(F32)
        sg = jax.nn.sigmoid(g)
        dg = da * v * (sg * (1.0 + g * (1.0 - sg)))
        dv = da * (g * sg)
        d_ref[0] = dg.astype(BF16)
        d_ref[1] = dv.astype(BF16)
        _acc(dwg_ref, wgrad(dg, tg), first)
        _acc(dwv_ref, wgrad(dv, tv), first)
        _acc(dbg_ref, _colsum(dg), first)
        _acc(dbv_ref, _colsum(dv), first)

    blk = lambda off: pl.BlockSpec((tm, tn), lambda j, i: (i, j + off))
    halo = lambda off: pl.BlockSpec((HALO, tn), lambda j, i: (jnp.maximum(i * (tm // HALO) - 1, 0), j + off))
    wsp = lambda off: pl.BlockSpec((SUBLANES, tn), lambda j, i: (0, j + off))
    bsp = lambda off: pl.BlockSpec((1, tn), lambda j, i: (0, j + off))
    f = jax.ShapeDtypeStruct
    outs = pl.pallas_call(
        body, name="conv_bwd_a", grid=(nj, S // tm),
        in_specs=[pl.BlockSpec((tm, tn), lambda j, i: (i, j)), blk(0), halo(0), blk(nj), halo(nj), wsp(0), wsp(nj), bsp(0), bsp(nj)],
        out_specs=[pl.BlockSpec((2, tm, tn), lambda j, i: (0, i, j)),
                   pl.BlockSpec((SUBLANES, tn), lambda j, i: (0, j)), pl.BlockSpec((SUBLANES, tn), lambda j, i: (0, j)),
                   pl.BlockSpec((1, tn), lambda j, i: (0, j)), pl.BlockSpec((1, tn), lambda j, i: (0, j))],
        out_shape=[f((2, S, D_FF), BF16), f((SUBLANES, D_FF), F32), f((SUBLANES, D_FF), F32),
                   f((1, D_FF), F32), f((1, D_FF), F32)],
        compiler_params=_cparams(("parallel", "arbitrary")))(dact, u, u, u, u, conv_w8, conv_w8, conv_b, conv_b)
    return outs


def _conv_bwd_b(duc, conv_w8, tm=256, tn=CONV_TN):
    _, S, W = duc.shape
    nj = W // tn
    n_rows = S // tm

    def body(d_ref, h_ref, w_ref, o_ref):
        last = pl.program_id(0) == n_rows - 1
        d = d_ref[...].astype(F32)
        h = h_ref[...].astype(F32)
        w = w_ref[...]
        o_ref[...] = (w[2:3] * d + w[1:2] * _shift_up(d, h, 1, last) + w[0:1] * _shift_up(d, h, 2, last)).astype(BF16)

    last_tile = S // HALO - 1
    return pl.pallas_call(
        body, name="conv_bwd_b", grid=(n_rows, 2 * nj),
        in_specs=[pl.BlockSpec((None, tm, tn), lambda i, j: (j // nj, i, j % nj)),
                  pl.BlockSpec((None, HALO, tn), lambda i, j: (j // nj, jnp.minimum((i + 1) * (tm // HALO), last_tile), j % nj)),
                  pl.BlockSpec((SUBLANES, tn), lambda i, j: (0, j))],
        out_specs=pl.BlockSpec((tm, tn), lambda i, j: (i, j)),
        out_shape=jax.ShapeDtypeStruct((S, 2 * W), BF16),
        compiler_params=_cparams(("parallel", "parallel")))(duc, duc, conv_w8)


ATT_SCALE = HEAD ** -0.5
NEG = -1e30
ATT_PAIRS = ATT_HEADS // 2


def _att_rows(n, d, S):
    per = S // (QBLK * d)
    r, m = n // per, n % per
    cur = pl.ds(m * (QBLK * d) + r, QBLK, stride=d)
    prv = pl.ds(jnp.maximum(m - 1, 0) * (QBLK * d) + r, QBLK, stride=d)
    return cur, prv, m > 0


def _att_slab(g, j):
    return (C_ATT + g * 3 * ATT_W + j * ATT_W) // LANES


def _heads(x):
    return x[:, 0:HEAD], x[:, HEAD:2 * HEAD]


ATT_NB = 4


def _stack(tiles):
    return jnp.concatenate([t[None] for t in tiles], axis=0)


def _att_operands(i, d, S, *sources):
    rows, has = [], []
    tiles = [[] for _ in sources]
    for bb in range(ATT_NB):
        cur, prv, has_prev = _att_rows(i * ATT_NB + bb, d, S)
        rows.append((cur, prv))
        has.append(has_prev)
        for t, (ref, use_cur) in zip(tiles, sources):
            t += _heads(ref[cur if use_cur else prv, :].astype(BF16))
    return rows, has, [_stack(t) for t in tiles]


def _att_mask(s_c, s_p, has_prev):
    qi = lax.broadcasted_iota(jnp.int32, (QBLK, QBLK), 0)
    kj = lax.broadcasted_iota(jnp.int32, (QBLK, QBLK), 1)
    s_c = jnp.where(kj <= qi, s_c * ATT_SCALE, NEG)
    s_p = jnp.where(jnp.logical_and(kj >= qi, has_prev), s_p * ATT_SCALE, NEG)
    return s_c, s_p


def _att_fwd(P, g):
    S = P.shape[0]
    d = ATT_PATTERNS[g][1]

    def body(q_ref, k_ref, v_ref, o_ref, l_ref):
        def group(i, carry):
            rows, has, (q, kc, kp, vc, vp) = _att_operands(i, d, S, (q_ref, True), (k_ref, True), (k_ref, False),
                                                           (v_ref, True), (v_ref, False))
            s_c_all, s_p_all = _dot16(q, kc, "nt"), _dot16(q, kp, "nt")
            p_c, p_p, den, lse = [], [], [], []
            for e in range(2 * ATT_NB):
                s_c, s_p = _att_mask(s_c_all[e], s_p_all[e], has[e // 2])
                m = jnp.maximum(jnp.max(s_c, axis=1, keepdims=True), jnp.max(s_p, axis=1, keepdims=True))
                pc, pp = jnp.exp(s_c - m), jnp.exp(s_p - m)
                den.append(jnp.sum(pc, axis=1, keepdims=True) + jnp.sum(pp, axis=1, keepdims=True))
                lse.append(jnp.broadcast_to(m + jnp.log(den[e]), (QBLK, HEAD)))
                p_c.append(pc)
                p_p.append(pp)
            num = _dot16(_stack(p_c), vc, "nn") + _dot16(_stack(p_p), vp, "nn")
            for bb, (cur, _) in enumerate(rows):
                o_ref[cur, :] = jnp.concatenate([num[2 * bb] / den[2 * bb], num[2 * bb + 1] / den[2 * bb + 1]], axis=1)
                l_ref[cur, :] = jnp.concatenate(lse[2 * bb:2 * bb + 2], axis=1)
            return carry

        lax.fori_loop(0, S // QBLK // ATT_NB, group, 0)

    slab = lambda j: pl.BlockSpec((S, LANES), lambda i: (0, _att_slab(g, j) + i))
    out = pl.BlockSpec((S, LANES), lambda i: (0, i))
    shp = jax.ShapeDtypeStruct((S, ATT_W), F32)
    return pl.pallas_call(body, name=f"att_fwd_g{g}", grid=(ATT_PAIRS,), in_specs=[slab(0), slab(1), slab(2)],
                          out_specs=[out, out], out_shape=[shp, shp], compiler_params=_cparams(("parallel",)))(P, P, P)


def _att_bwd(P, o, l, do, dl, g):
    S = P.shape[0]
    d = ATT_PATTERNS[g][1]

    def body(q_ref, k_ref, v_ref, o_ref, l_ref, do_ref, dl_ref, dq_ref, dk_ref, dv_ref, dq_acc, dk_acc, dv_acc):
        dk_acc[...] = jnp.zeros_like(dk_acc)
        dv_acc[...] = jnp.zeros_like(dv_acc)

        def group(i, carry):
            rows, has, (q, kc, kp, vc, vp, dob) = _att_operands(
                i, d, S, (q_ref, True), (k_ref, True), (k_ref, False), (v_ref, True), (v_ref, False), (do_ref, True))
            s_c_all, s_p_all = _dot16(q, kc, "nt"), _dot16(q, kp, "nt")
            dp_c_all, dp_p_all = _dot16(dob, vc, "nt"), _dot16(dob, vp, "nt")
            p_c, p_p, ds_c, ds_p = [], [], [], []
            for bb, (cur, _) in enumerate(rows):
                dd2 = do_ref[cur, :] * o_ref[cur, :] - dl_ref[cur, :]
                for h, (dd, lse) in enumerate(zip(_heads(dd2), _heads(l_ref[cur, :]))):
                    e = 2 * bb + h
                    s_c, s_p = _att_mask(s_c_all[e], s_p_all[e], has[bb])
                    pc, pp = jnp.exp(s_c - lse[:, 0:1]), jnp.exp(s_p - lse[:, 0:1])
                    delta = jnp.sum(dd, axis=1, keepdims=True)
                    p_c.append(pc)
                    p_p.append(pp)
                    ds_c.append(pc * (dp_c_all[e] - delta) * ATT_SCALE)
                    ds_p.append(pp * (dp_p_all[e] - delta) * ATT_SCALE)
            p_c, p_p, ds_c, ds_p = map(_stack, (p_c, p_p, ds_c, ds_p))
            dq = _dot16(ds_c, kc, "nn") + _dot16(ds_p, kp, "nn")
            dk_c, dk_p = _dot16(ds_c, q, "tn"), _dot16(ds_p, q, "tn")
            dv_c, dv_p = _dot16(p_c, dob, "tn"), _dot16(p_p, dob, "tn")
            pair = lambda x, bb: jnp.concatenate([x[2 * bb], x[2 * bb + 1]], axis=1)
            for bb, (cur, prv) in enumerate(rows):
                dq_acc[cur, :] = pair(dq, bb)
                dk_acc[cur, :] += pair(dk_c, bb)
                dv_acc[cur, :] += pair(dv_c, bb)
                dk_acc[prv, :] += pair(dk_p, bb)
                dv_acc[prv, :] += pair(dv_p, bb)
            return carry

        lax.fori_loop(0, S // QBLK // ATT_NB, group, 0)
        dq_ref[...] = dq_acc[...].astype(BF16)
        dk_ref[...] = dk_acc[...].astype(BF16)
        dv_ref[...] = dv_acc[...].astype(BF16)

    slab = lambda j: pl.BlockSpec((S, LANES), lambda i: (0, _att_slab(g, j) + i))
    blk128 = pl.BlockSpec((S, LANES), lambda i: (0, i))
    shp = jax.ShapeDtypeStruct((S, ATT_W), BF16)
    return pl.pallas_call(body, name=f"att_bwd_g{g}", grid=(ATT_PAIRS,),
                          in_specs=[slab(0), slab(1), slab(2)] + [blk128] * 4, out_specs=[blk128] * 3, out_shape=[shp] * 3,
                          scratch_shapes=[pltpu.VMEM((S, LANES), F32)] * 3,
                          compiler_params=_cparams(("parallel",)))(P, P, P, o, l, do, dl)


def _att_weights(l_refs):
    l0, l1, l2 = [r[...] for r in l_refs]
    m = jnp.maximum(jnp.maximum(l0, l1), l2)
    e = (jnp.exp(l0 - m), jnp.exp(l1 - m), jnp.exp(l2 - m))
    inv = 1.0 / (e[0] + e[1] + e[2])
    return [x * inv for x in e]


def _att_combine_fwd(os, ls, tm=512):
    S = os[0].shape[0]

    def body(o0, o1, o2, l0, l1, l2, a_ref):
        w = _att_weights((l0, l1, l2))
        a_ref[...] = (w[0] * o0[...] + w[1] * o1[...] + w[2] * o2[...]).astype(BF16)

    row = _rows(tm, ATT_W)
    return pl.pallas_call(body, name="att_combine_fwd", grid=(S // tm,), in_specs=[row] * 6, out_specs=row,
                          out_shape=jax.ShapeDtypeStruct((S, ATT_W), BF16),
                          compiler_params=_cparams(("parallel",)))(*os, *ls)


def _att_combine_bwd(da, os, ls, tm=512):
    S = da.shape[0]

    def body(da_ref, o0, o1, o2, l0, l1, l2, *out_refs):
        da = da_ref[...]
        w = _att_weights((l0, l1, l2))
        dw = (da * o0[...], da * o1[...], da * o2[...])
        mean = w[0] * dw[0] + w[1] * dw[1] + w[2] * dw[2]
        for g in range(3):
            out_refs[g][...] = w[g] * da
            out_refs[3 + g][...] = w[g] * (dw[g] - mean)

    row = _rows(tm, ATT_W)
    shp = jax.ShapeDtypeStruct((S, ATT_W), F32)
    return pl.pallas_call(body, name="att_combine_bwd", grid=(S // tm,), in_specs=[row] * 7, out_specs=[row] * 6,
                          out_shape=[shp] * 6, compiler_params=_cparams(("parallel",)))(da, *os, *ls)


@jax.custom_vjp
def _bdot(a, b):
    return jnp.dot(a.astype(BF16), b.astype(BF16), preferred_element_type=F32)


def _bdot_fwd(a, b):
    return _bdot(a, b), (a, b)


def _bdot_bwd(res, ct):
    a, b = res
    ct16 = ct.astype(BF16)
    da = lax.dot_general(ct16, b.astype(BF16), (((1,), (1,)), ((), ())), preferred_element_type=F32)
    db = lax.dot_general(a.astype(BF16), ct16, (((0,), (0,)), ((), ())), preferred_element_type=F32)
    return da, db


_bdot.defvjp(_bdot_fwd, _bdot_bwd)


def _two_piece_dot(x, m):
    hi = x.astype(BF16)
    lo = (x - hi.astype(F32)).astype(BF16)
    return jnp.dot(hi, m, preferred_element_type=F32) + jnp.dot(lo, m, preferred_element_type=F32)


def _head_sum_impl(x):
    sel = (lax.broadcasted_iota(jnp.int32, (D, LANES), 0) // HEAD == lax.broadcasted_iota(jnp.int32, (D, LANES), 1)).astype(BF16)
    sel_t = (lax.broadcasted_iota(jnp.int32, (LANES, D), 1) // HEAD == lax.broadcasted_iota(jnp.int32, (LANES, D), 0)).astype(BF16)
    return _two_piece_dot(_two_piece_dot(x, sel), sel_t)


@jax.custom_vjp
def _head_sum(x):
    return _head_sum_impl(x)


_head_sum.defvjp(lambda x: (_head_sum_impl(x), None), lambda _, ct: (_head_sum_impl(ct),))


def _softplus(z):
    return jnp.maximum(z, 0.0) + jnp.log(1.0 + jnp.exp(-jnp.abs(z)))


def _rwkv_prep_fn(zr, zrp, zk, zkp, zv, zvp, zl, zlp, mu_r, mu_k, mu_v, mu_l, w0, a0, k_k, k_a, w2, a2, g2p):
    r = zr + (zrp - zr) * mu_r
    k = zk + (zkp - zk) * mu_k
    v = zv + (zvp - zv) * mu_v
    lo = zl + (zlp - zl) * mu_l
    w_low, a_low, g_low = lo[:, 0:LORA_W], lo[:, LORA_W:LORA_W + LORA_A], lo[:, LANES:LANES + G_PAD]
    w_log = -_softplus(-(w0 + _bdot(jnp.tanh(w_low), w2))) - 0.5
    decay = -jnp.exp(w_log)
    a = jax.nn.sigmoid(a0 + _bdot(a_low, a2))
    g = _bdot(jax.nn.sigmoid(g_low), g2p)
    kmod = k * (1.0 + (a - 1.0) * k_a)
    kk = k * k_k
    kk = kk / jnp.maximum(jnp.sqrt(_head_sum(kk * kk)), 1e-12)
    return r, decay, kmod, v, -kk, kk * a, g


def _rwkv_prep_specs(tm, blk=lambda i: i):
    vec = _full((1, D))
    rows = lambda w, col: pl.BlockSpec((tm, w), lambda i: (blk(i), col))
    prev = lambda w, col: pl.BlockSpec((SUBLANES, w), lambda i: (jnp.maximum(blk(i) * (tm // SUBLANES) - 1, 0), col))
    slabs = []
    for col in (C_R // D, C_K // D, C_V // D):
        slabs += [rows(D, col), prev(D, col)]
    slabs += [rows(LORA_PAD, C_LORA // LORA_PAD), prev(LORA_PAD, C_LORA // LORA_PAD)]
    params = [vec, vec, vec, _full((1, LORA_PAD)), vec, vec, vec, vec,
              _full((LORA_W, D)), _full((LORA_A, D)), _full((G_PAD, D))]
    return slabs, params


def _prep_inputs(refs, first):
    vals = []
    for s in range(4):
        z = refs[2 * s][...]
        vals += [z, _shift_down(z, refs[2 * s + 1][...], 1, first)]
    return vals + [r[...] for r in refs[8:19]]


def _rwkv_prep(P, params, tm=256):
    S = P.shape[0]
    slabs, pspecs = _rwkv_prep_specs(tm)

    def body(*refs):
        outs = _rwkv_prep_fn(*_prep_inputs(refs, pl.program_id(0) == 0))
        for o_ref, val in zip(refs[19:], outs):
            o_ref[...] = val

    shp = jax.ShapeDtypeStruct((S, D), F32)
    return pl.pallas_call(body, name="rwkv_prep", grid=(S // tm,), in_specs=slabs + pspecs,
                          out_specs=[_rows(tm, D)] * 7, out_shape=[shp] * 7,
                          compiler_params=_cparams(("parallel",)))(*([P] * 8), *params)


def _rwkv_prep_bwd(P, params, cts_a, cts_b, tm=128):
    S = P.shape[0]
    nblk = S // tm
    blk = lambda i: nblk - 1 - i
    slabs, pspecs = _rwkv_prep_specs(tm, blk)
    has_b = [c is not None for c in cts_b]
    n_ct = 7 + sum(has_b)

    def body(*refs):
        start = pl.program_id(0) == 0
        ins = _prep_inputs(refs, pl.program_id(0) == nblk - 1)
        ct_refs = refs[19:19 + n_ct]
        out_refs = refs[19 + n_ct:19 + n_ct + 15]
        carry_refs = refs[19 + n_ct + 15:]

        @pl.when(start)
        def _():
            for c_ref in carry_refs:
                c_ref[...] = jnp.zeros_like(c_ref)

        cts, pos = [], 7
        for i in range(7):
            c = ct_refs[i][...]
            if has_b[i]:
                c = c + ct_refs[pos][...]
                pos += 1
            cts.append(c)
        _, vjp = jax.vjp(_rwkv_prep_fn, *ins)
        grads = vjp(tuple(cts))
        for s in range(4):
            shifted = grads[2 * s + 1]
            out_refs[s][...] = (grads[2 * s] + _shift_up(shifted, carry_refs[s][...], 1, start)).astype(BF16)
            carry_refs[s][0:1, :] = shifted[0:1, :]
        for i in range(11):
            _acc(out_refs[4 + i], grads[8 + i], start)

    ct_in = list(cts_a) + [c for c in cts_b if c is not None]
    row = lambda w: pl.BlockSpec((tm, w), lambda i: (blk(i), 0))
    f = jax.ShapeDtypeStruct
    zshapes = [f((S, D), BF16)] * 3 + [f((S, LORA_PAD), BF16)]
    pshapes = [f((1, D), F32)] * 3 + [f((1, LORA_PAD), F32)] + [f((1, D), F32)] * 4 + [f((LORA_W, D), F32), f((LORA_A, D), F32), f((G_PAD, D), F32)]
    return pl.pallas_call(
        body, name="rwkv_prep_bwd", grid=(nblk,),
        in_specs=slabs + pspecs + [row(D)] * n_ct,
        out_specs=[row(D), row(D), row(D), row(LORA_PAD)] + pspecs,
        out_shape=zshapes + pshapes,
        scratch_shapes=[pltpu.VMEM((SUBLANES, D), F32)] * 3 + [pltpu.VMEM((SUBLANES, LORA_PAD), F32)],
        compiler_params=_cparams(("arbitrary",)))(*([P] * 8), *params, *ct_in)


def _rwkv_post_fn(y, r, kmod, v, g, lnx_w, lnx_b, r_k):
    mean = _head_sum(y) * (1.0 / HEAD)
    yc = y - mean
    var = _head_sum(yc * yc) * (1.0 / HEAD)
    yn = yc * lax.rsqrt(var + GN_EPS) * lnx_w + lnx_b
    bonus = _head_sum(r * kmod * r_k) * v
    return (yn + bonus) * g


def _rwkv_post(y, r, kmod, v, g, lnx_w, lnx_b, r_k, tm=256):
    S = y.shape[0]

    def body(y_ref, r_ref, k_ref, v_ref, g_ref, w_ref, b_ref, rk_ref, o_ref):
        o_ref[...] = _rwkv_post_fn(y_ref[...], r_ref[...], k_ref[...], v_ref[...], g_ref[...],
                                   w_ref[...], b_ref[...], rk_ref[...]).astype(BF16)

    row, vec = _rows(tm, D), _full((1, D))
    return pl.pallas_call(body, name="rwkv_post", grid=(S // tm,), in_specs=[row] * 5 + [vec] * 3, out_specs=row,
                          out_shape=jax.ShapeDtypeStruct((S, D), BF16),
                          compiler_params=_cparams(("parallel",)))(y, r, kmod, v, g, lnx_w, lnx_b, r_k)


def _rwkv_post_bwd(drw, y, r, kmod, v, g, lnx_w, lnx_b, r_k, tm=256):
    S = y.shape[0]

    def body(d_ref, y_ref, r_ref, k_ref, v_ref, g_ref, w_ref, b_ref, rk_ref, *out_refs):
        first = pl.program_id(0) == 0
        _, vjp = jax.vjp(_rwkv_post_fn, y_ref[...], r_ref[...], k_ref[...], v_ref[...], g_ref[...],
                         w_ref[...], b_ref[...], rk_ref[...])
        grads = vjp(d_ref[...])
        for i in range(5):
            out_refs[i][...] = grads[i]
        for i in range(5, 8):
            _acc(out_refs[i], grads[i], first)

    row, vec = _rows(tm, D), _full((1, D))
    f = jax.ShapeDtypeStruct
    return pl.pallas_call(body, name="rwkv_post_bwd", grid=(S // tm,), in_specs=[row] * 6 + [vec] * 3,
                          out_specs=[row] * 5 + [vec] * 3, out_shape=[f((S, D), F32)] * 5 + [f((1, D), F32)] * 3,
                          compiler_params=_cparams(("arbitrary",)))(drw, y, r, kmod, v, g, lnx_w, lnx_b, r_k)


CHUNK = 64
CHUNK_TB = 256
_DOT_DIMS = {"nn": (((2,), (1,)), ((0,), (0,))), "nt": (((2,), (2,)), ((0,), (0,))), "tn": (((1,), (1,)), ((0,), (0,)))}


def _dot16(x, y, mode):
    return lax.dot_general(x.astype(BF16), y.astype(BF16), _DOT_DIMS[mode], preferred_element_type=F32)


@functools.partial(jax.custom_vjp, nondiff_argnums=(2,))
def _mm16(x, y, mode):
    return _dot16(x, y, mode)


def _mm16_fwd(x, y, mode):
    return _dot16(x, y, mode), (x, y)


def _mm16_bwd(mode, res, ct):
    x, y = res
    if mode == "nn":
        return _dot16(ct, y, "nt"), _dot16(x, ct, "tn")
    if mode == "nt":
        return _dot16(ct, y, "nn"), _dot16(ct, x, "tn")
    return _dot16(y, ct, "nt"), _dot16(x, ct, "nn")


_mm16.defvjp(_mm16_fwd, _mm16_bwd)


def _tri_sum(x, upper):
    T = x.shape[0]
    i = lax.broadcasted_iota(jnp.int32, (T, T), 0)
    j = lax.broadcasted_iota(jnp.int32, (T, T), 1)
    tri = ((j >= i) if upper else (i >= j)).astype(BF16)
    out, rest = None, x
    for _ in range(3):
        piece = rest.astype(BF16)
        rest = rest - piece.astype(F32)
        part = jnp.dot(tri, piece, preferred_element_type=F32)
        out = part if out is None else out + part
    return out


@jax.custom_vjp
def _cumsum_rows(x):
    return _tri_sum(x, False)


_cumsum_rows.defvjp(lambda x: (_tri_sum(x, False), None), lambda _, ct: (_tri_sum(ct, True),))


def _rows_to_cols(x):
    H, _, K = x.shape
    eye = (lax.broadcasted_iota(jnp.int32, (H, K, K), 1) == lax.broadcasted_iota(jnp.int32, (H, K, K), 2)).astype(F32)
    out = lax.dot_general(eye, jnp.broadcast_to(x, (H, SUBLANES, K)), _DOT_DIMS["nt"],
                          precision=lax.Precision.HIGHEST, preferred_element_type=F32)
    return out[:, :, 0:1]


def _per_head(x):
    return jnp.concatenate([x[:, h * HEAD:(h + 1) * HEAD][None] for h in range(N_HEADS)], axis=0)


def _chunk_fn(st0, r, lw, k, v, a, b):
    T = r.shape[0]
    cl = _cumsum_rows(lw)
    cl_end = cl[T - 1:T, :]
    inv = jnp.exp(-cl)
    to_end = jnp.exp(cl_end - cl)
    ah, rh, bh, kh, be, ke, v3 = [_per_head(x) for x in
                                  (a * jnp.exp(cl - lw), r * jnp.exp(cl), b * inv, k * inv, b * to_end, k * to_end, v)]
    i = lax.broadcasted_iota(jnp.int32, (N_HEADS, T, T), 1)
    j = lax.broadcasted_iota(jnp.int32, (N_HEADS, T, T), 2)
    a_ab = jnp.where(i > j, _mm16(ah, bh, "nt"), 0.0)
    a_ak = jnp.where(i > j, _mm16(ah, kh, "nt"), 0.0)
    m_rb = jnp.where(i >= j, _mm16(rh, bh, "nt"), 0.0)
    m_rk = jnp.where(i >= j, _mm16(rh, kh, "nt"), 0.0)
    rhs = _mm16(ah, st0, "nn") + _mm16(a_ak, v3, "nn")
    power, solve, n = a_ab, (i == j).astype(F32) + a_ab, 1
    while 2 * n < T:
        power = _mm16(power, power, "nn")
        solve = solve + _mm16(solve, power, "nn")
        n *= 2
    sa = _mm16(solve, rhs, "nn")
    y3 = _mm16(rh, st0, "nn") + _mm16(m_rb, sa, "nn") + _mm16(m_rk, v3, "nn")
    st_end = _rows_to_cols(_per_head(jnp.exp(cl_end))) * st0 + _mm16(be, sa, "tn") + _mm16(ke, v3, "tn")
    return jnp.concatenate([y3[h] for h in range(N_HEADS)], axis=1), st_end


def _hosted_exchange(refs, n, broadcast, grid):
    if n == 0:
        return lambda: None
    start, wait = _exchange_ops(refs[:n], refs[n:2 * n], *refs[2 * n:], broadcast)
    first = functools.reduce(jnp.logical_and, [pl.program_id(a) == 0 for a in range(len(grid))])
    last = functools.reduce(jnp.logical_and, [pl.program_id(a) == g - 1 for a, g in enumerate(grid)])
    pl.when(first)(start)
    return lambda: pl.when(last)(wait)


def _cscan_fwd(r, lw, k, v, a, b, gather=()):
    S = r.shape[0]
    per_blk = CHUNK_TB // CHUNK
    n_x = len(gather)
    nblk = S // CHUNK_TB

    def body(*refs):
        r_ref, lw_ref, k_ref, v_ref, a_ref, b_ref = refs[:6]
        y_ref, ck_ref = refs[6 + n_x:8 + n_x]
        st_ref = refs[8 + 2 * n_x]
        finish = _hosted_exchange(refs[6:6 + n_x] + refs[8 + n_x:8 + 2 * n_x] + refs[9 + 2 * n_x:], n_x, True, (nblk,))

        @pl.when(pl.program_id(0) == 0)
        def _():
            st_ref[...] = jnp.zeros_like(st_ref)

        def chunk(c, carry):
            rows = pl.ds(pl.multiple_of(c * CHUNK, CHUNK), CHUNK)
            st0 = st_ref[...]
            ck_ref[c] = st0
            y, st_end = _chunk_fn(st0, r_ref[rows, :], lw_ref[rows, :], k_ref[rows, :],
                                  v_ref[rows, :], a_ref[rows, :], b_ref[rows, :])
            y_ref[rows, :] = y
            st_ref[...] = st_end
            return carry

        lax.fori_loop(0, per_blk, chunk, 0)
        finish()

    blk = _rows(CHUNK_TB, D)
    any_spec = pl.BlockSpec(memory_space=pl.ANY)
    outs = pl.pallas_call(
        body, name="scan_fwd", grid=(nblk,), in_specs=[blk] * 6 + [any_spec] * n_x,
        out_specs=[blk, pl.BlockSpec((per_blk, N_HEADS, HEAD, HEAD), lambda i: (i, 0, 0, 0))] + [any_spec] * n_x,
        out_shape=[jax.ShapeDtypeStruct((S, D), F32), jax.ShapeDtypeStruct((S // CHUNK, N_HEADS, HEAD, HEAD), F32)]
        + _exchange_shapes(gather, True),
        scratch_shapes=[pltpu.VMEM((N_HEADS, HEAD, HEAD), F32)] + (_exchange_scratch(n_x) if n_x else []),
        compiler_params=_cparams(("arbitrary",)))(r, lw, k, v, a, b, *gather)
    return outs[0], outs[1], outs[2:]


def _cscan_bwd(r, lw, k, v, a, b, ckpt, dy, scatter=()):
    S = r.shape[0]
    per_blk = CHUNK_TB // CHUNK
    nblk = S // CHUNK_TB
    n_x = len(scatter)

    def body(*refs):
        r_ref, lw_ref, k_ref, v_ref, a_ref, b_ref, ck_ref, dy_ref = refs[:8]
        out_refs = refs[8 + n_x:14 + n_x]
        ds_ref = refs[14 + 2 * n_x]
        finish = _hosted_exchange(refs[8:8 + n_x] + refs[14 + n_x:14 + 2 * n_x] + refs[15 + 2 * n_x:], n_x, False, (nblk,))

        @pl.when(pl.program_id(0) == 0)
        def _():
            ds_ref[...] = jnp.zeros_like(ds_ref)

        def chunk(cc, carry):
            c = per_blk - 1 - cc
            rows = pl.ds(pl.multiple_of(c * CHUNK, CHUNK), CHUNK)
            ins = (ck_ref[c], r_ref[rows, :], lw_ref[rows, :], k_ref[rows, :], v_ref[rows, :], a_ref[rows, :], b_ref[rows, :])
            _, vjp = jax.vjp(_chunk_fn, *ins)
            grads = vjp((dy_ref[rows, :], ds_ref[...]))
            ds_ref[...] = grads[0]
            for o_ref, g in zip(out_refs, grads[1:]):
                o_ref[rows, :] = g
            return carry

        lax.fori_loop(0, per_blk, chunk, 0)
        finish()

    blk = pl.BlockSpec((CHUNK_TB, D), lambda i: (nblk - 1 - i, 0))
    any_spec = pl.BlockSpec(memory_space=pl.ANY)
    shp = jax.ShapeDtypeStruct((S, D), F32)
    outs = pl.pallas_call(
        body, name="scan_bwd", grid=(nblk,),
        in_specs=[blk] * 6 + [pl.BlockSpec((per_blk, N_HEADS, HEAD, HEAD), lambda i: (nblk - 1 - i, 0, 0, 0)), blk]
        + [any_spec] * n_x,
        out_specs=[blk] * 6 + [any_spec] * n_x, out_shape=[shp] * 6 + _exchange_shapes(scatter, False),
        scratch_shapes=[pltpu.VMEM((N_HEADS, HEAD, HEAD), F32)] + (_exchange_scratch(n_x) if n_x else []),
        compiler_params=_cparams(("arbitrary",)))(r, lw, k, v, a, b, ckpt, dy, *scatter)
    return outs[:6], outs[6:]


def _ada_partial(c_all, w_shard):
    def body(c_ref, w_ref, o_ref):
        o_ref[...] = jnp.dot(c_ref[...].astype(BF16), w_ref[...].astype(BF16), preferred_element_type=F32)

    vm = pl.BlockSpec(memory_space=pltpu.VMEM)
    return pl.pallas_call(body, name="ada_partial", in_specs=[vm, vm], out_specs=vm,
                          out_shape=jax.ShapeDtypeStruct((N_DEV, w_shard.shape[1]), F32),
                          compiler_params=pltpu.CompilerParams(vmem_limit_bytes=VMEM_LIMIT))(c_all, w_shard)


def _ada_bias(rows, b_ada):
    def body(r_ref, b_ref, o_ref):
        o_ref[...] = r_ref[...] + b_ref[...]

    vm = pl.BlockSpec(memory_space=pltpu.VMEM)
    return pl.pallas_call(body, name="ada_bias", in_specs=[vm, vm], out_specs=vm,
                          out_shape=jax.ShapeDtypeStruct(rows.shape, F32))(rows, b_ada)


def _ada_wgrad(c_cols, d_all):
    def body(c_ref, d_ref, o_ref):
        acc = c_ref[:, 0:1] * d_ref[0:1, :]
        for j in range(1, N_DEV):
            acc = acc + c_ref[:, j:j + 1] * d_ref[j:j + 1, :]
        o_ref[...] = acc

    vm = pl.BlockSpec(memory_space=pltpu.VMEM)
    return pl.pallas_call(body, name="ada_wgrad", in_specs=[vm, vm], out_specs=vm,
                          out_shape=jax.ShapeDtypeStruct((D, d_all.shape[1]), F32),
                          compiler_params=pltpu.CompilerParams(vmem_limit_bytes=VMEM_LIMIT))(c_cols, d_all)


def _exchange(srcs, broadcast, name):
    n = len(srcs)

    def body(*refs):
        start, wait = _exchange_ops(refs[:n], refs[n:2 * n], *refs[2 * n:], broadcast)
        start()
        wait()

    any_spec = pl.BlockSpec(memory_space=pl.ANY)
    return pl.pallas_call(
        body, name=name, out_shape=_exchange_shapes(srcs, broadcast), in_specs=[any_spec] * n, out_specs=[any_spec] * n,
        scratch_shapes=_exchange_scratch(n),
        compiler_params=pltpu.CompilerParams(has_side_effects=True),
    )(*srcs)


def _gather_via_sibling(srcs, name):
    n = len(srcs)

    def body(*refs):
        src_refs, out_refs = refs[:n], refs[n:2 * n]
        send_sems, recv_sems, local_sems = refs[2 * n:]
        x, y, c = lax.axis_index("x"), lax.axis_index("y"), lax.axis_index("c")
        me, sibling = (x, y, c), (x, y, 1 - c)
        chips = [(1 - x, y), (x, 1 - y), (1 - x, 1 - y)]

        def slot(px, py, pc):
            return 4 * px + 2 * py + pc

        def copy(i, k, block, to, src=None):
            rows = out_refs[i].at[slot(*block)]
            return pltpu.make_async_remote_copy(
                src_ref=rows if src is None else src, dst_ref=rows, send_sem=send_sems.at[i, k],
                recv_sem=recv_sems.at[i, k], device_id=to, device_id_type=_MESH)

        local = [pltpu.make_async_copy(src_refs[i], out_refs[i].at[slot(*me)], local_sems.at[i]) for i in range(n)]
        for cp in local:
            cp.start()
        first = [copy(i, 0, me, sibling, src=src_refs[i]) for i in range(n)]
        first += [copy(i, 1 + j, me, (*chip, c), src=src_refs[i]) for j, chip in enumerate(chips) for i in range(n)]
        for cp in first:
            cp.start()
        passed = []
        for j, chip in enumerate(chips):
            for i in range(n):
                copy(i, 1 + j, (*chip, c), me).wait_recv()
                passed.append(copy(i, 4 + j, (*chip, c), sibling))
                passed[-1].start()
        for i in range(n):
            copy(i, 0, sibling, me).wait_recv()
            for j, chip in enumerate(chips):
                copy(i, 4 + j, (*chip, 1 - c), me).wait_recv()
        for cp in first + passed:
            cp.wait_send()
        for cp in local:
            cp.wait()

    any_spec = pl.BlockSpec(memory_space=pl.ANY)
    return pl.pallas_call(
        body, name=name, out_shape=_exchange_shapes(srcs, True), in_specs=[any_spec] * n, out_specs=[any_spec] * n,
        scratch_shapes=_exchange_scratch(n),
        compiler_params=pltpu.CompilerParams(has_side_effects=True),
    )(*srcs)


def _flags(broadcast, n):
    return [broadcast] * n if isinstance(broadcast, bool) else list(broadcast)


def _exchange_shapes(srcs, broadcast):
    return [jax.ShapeDtypeStruct((N_DEV,) + (s.shape if bc else s.shape[1:]), s.dtype)
            for s, bc in zip(srcs, _flags(broadcast, len(srcs)))]


def _exchange_scratch(n):
    return [pltpu.SemaphoreType.DMA((n, N_DEV)), pltpu.SemaphoreType.DMA((n, N_DEV)), pltpu.SemaphoreType.DMA((n,))]


def _exchange_ops(src_refs, out_refs, send_sems, recv_sems, local_sems, broadcast):
    n = len(src_refs)
    flags = _flags(broadcast, n)
    x, y, c = lax.axis_index("x"), lax.axis_index("y"), lax.axis_index("c")
    me = 4 * x + 2 * y + c

    def block(i, j):
        return src_refs[i] if flags[i] else src_refs[i].at[j]

    def remote(i, d, src_slot, dst_slot):
        px, py, pc = x ^ (d >> 2), y ^ ((d >> 1) & 1), c ^ (d & 1)
        return pltpu.make_async_remote_copy(
            src_ref=block(i, src_slot), dst_ref=out_refs[i].at[dst_slot], send_sem=send_sems.at[i, d],
            recv_sem=recv_sems.at[i, d], device_id=(px, py, pc), device_id_type=_MESH)

    def local(i):
        return pltpu.make_async_copy(block(i, me), out_refs[i].at[me], local_sems.at[i])

    def start():
        for i in range(n):
            local(i).start()
        for d in range(1, N_DEV):
            for i in range(n):
                remote(i, d, me ^ d, me).start()

    def wait():
        for d in range(1, N_DEV):
            for i in range(n):
                remote(i, d, me, me ^ d).wait_recv()
        for d in range(1, N_DEV):
            for i in range(n):
                remote(i, d, me ^ d, me).wait_send()
        for i in range(n):
            local(i).wait()

    return start, wait


def _adamw(w, g, m, v):
    nm = ADAM_B1 * m + (1.0 - ADAM_B1) * g
    nv = ADAM_B2 * v + (1.0 - ADAM_B2) * (g * g)
    m_hat = nm * (1.0 / (1.0 - ADAM_B1 ** ADAM_STEP))
    v_hat = nv * (1.0 / (1.0 - ADAM_B2 ** ADAM_STEP))
    return -ADAM_LR * (m_hat / (jnp.sqrt(v_hat) + ADAM_EPS) + ADAM_WD * w), nm, nv


def _adam_vectors(parts, ws, ms, vs):
    nv = len(ws)
    sizes = [w.shape[1] for w in ws]

    def body(*refs):
        p_ref = refs[0]
        w_refs, m_refs, v_refs = refs[1:1 + nv], refs[1 + nv:1 + 2 * nv], refs[1 + 2 * nv:1 + 3 * nv]
        out_refs = refs[1 + 3 * nv:]
        g_all = p_ref[0]
        for j in range(1, N_DEV):
            g_all = g_all + p_ref[j]
        off = 0
        for i, n in enumerate(sizes):
            g = g_all[:, off:off + n]
            off += -(-n // LANES) * LANES
            delta, new_m, new_v = _adamw(w_refs[i][...], g, m_refs[i][...], v_refs[i][...])
            for o_ref, val in zip(out_refs[4 * i:4 * i + 4], (g, delta, new_m, new_v)):
                o_ref[...] = val

    vm = pl.BlockSpec(memory_space=pltpu.VMEM)
    outs = pl.pallas_call(body, name="adam_replicated", in_specs=[vm] * (1 + 3 * nv), out_specs=[vm] * (4 * nv),
                          out_shape=[jax.ShapeDtypeStruct((1, n), F32) for n in sizes for _ in range(4)])(parts, *ws, *ms, *vs)
    return [outs[4 * i:4 * i + 4] for i in range(nv)]


def _sum_adam(parts, w, m, v, name):
    n_parts, R, C = parts.shape
    fits = [t for t in range(16, R + 1, 16) if R % t == 0 and t * C <= 2504 * LANES]
    if fits:
        tm, tc = max(fits), C
    elif C % (2 * LANES) == 0 and R * C > 2504 * LANES:
        tm, tc = R, 2 * LANES
    else:
        tm, tc = R, C

    def body(p_ref, w_ref, m_ref, v_ref, g_ref, d_ref, nm_ref, nv_ref):
        g = p_ref[0].astype(F32)
        for j in range(1, n_parts):
            g = g + p_ref[j].astype(F32)
        g_ref[...] = g
        d_ref[...], nm_ref[...], nv_ref[...] = _adamw(w_ref[...], g, m_ref[...], v_ref[...])

    blk = pl.BlockSpec((tm, tc), lambda i, j: (i, j))
    shp = jax.ShapeDtypeStruct((R, C), F32)
    return pl.pallas_call(body, name=name, grid=(R // tm, C // tc),
                          in_specs=[pl.BlockSpec((n_parts, tm, tc), lambda i, j: (0, i, j)), blk, blk, blk],
                          out_specs=[blk] * 4, out_shape=[shp] * 4,
                          compiler_params=_cparams(("parallel", "parallel")))(parts, w, m, v)


TRANSPOSED = ("w_in", "w_up")
SHARDED = (("w_ada", 1), ("w_in", 0), ("w2", 1), ("a2", 1), ("g2", 1), ("w_att_out", 1), ("w_rwkv_out", 0),
           ("w_o", 0), ("w_up", 0), ("conv_w", 1), ("w_down", 0))
EARLY, LATE = SHARDED[1:5], SHARDED[5:]
REPLICATED = ("b_ada", "norm1_w", "b_gate", "mu_shift", "w0", "a0", "k_k", "k_a", "r_k", "lnx_w", "lnx_b",
              "norm2_w", "conv_b", "norm_f_w")
WEIGHTS = ("w_ada", "b_ada", "norm1_w", "w_in", "b_gate", "mu_shift", "w0", "w2", "a0", "a2", "g2", "k_k", "k_a", "r_k",
           "lnx_w", "lnx_b", "w_att_out", "w_rwkv_out", "w_o", "norm2_w", "w_up", "conv_w", "conv_b", "w_down", "norm_f_w")


W_IN_RUNS = ((0, C_ATT, ATT_IN), (ATT_IN, C_R, 3 * D), (ATT_IN + 3 * D, C_LORA, LORA_W + LORA_A),
             (ATT_IN + 3 * D + LORA_W + LORA_A, C_LORA + LANES, LORA_G), (ATT_IN + RWKV_IN, C_GA, 2 * D))
W_IN_SHARD = N_IN // N_DEV


def _pad_w_in(w_in_t):
    pieces = [w_in_t[orig:orig + count] for orig, _, count in sorted(W_IN_RUNS, key=lambda run: run[1])]
    pieces.append(jnp.zeros((LORA_PAD - LANES - LORA_G, w_in_t.shape[1]), w_in_t.dtype))
    return jnp.concatenate(pieces, axis=0)


def _w_in_blocks(g):
    blocks = []
    for j in range(N_DEV):
        pieces = []
        for orig, pad, count in W_IN_RUNS:
            lo, hi = max(orig, j * W_IN_SHARD), min(orig + count, (j + 1) * W_IN_SHARD)
            if lo < hi:
                pieces.append(g[pad + lo - orig:pad + hi - orig])
        blocks.append(jnp.concatenate(pieces, axis=0)[None])
    return jnp.concatenate(blocks, axis=0)


def _pad_mu(mu):
    lo = mu[:, 3 * D:]
    mu_l = jnp.concatenate([lo[:, :LORA_W + LORA_A], lo[:, LORA_W + LORA_A:], jnp.zeros((1, LORA_PAD - LANES - LORA_G), mu.dtype)], axis=1)
    return mu[:, :D], mu[:, D:2 * D], mu[:, 2 * D:3 * D], mu_l


def _local_step(x, ada, W, late_shards, target):
    S = x.shape[0]
    W = dict(W)
    G = {}
    sh1, sc1, gt1, sh2, sc2, gt2 = [ada[:, i * D:(i + 1) * D] for i in range(6)]
    h1, rstd1 = _norm_fwd(x, None, None, W["norm1_w"], sc1, sh1, "norm1_fwd")
    w_in_p = _pad_w_in(W["w_in"])
    P = _mm(h1, w_in_p, "nt", F32, "proj_in")

    mu_r, mu_k, mu_v, mu_l = _pad_mu(W["mu_shift"])
    g2p = jnp.pad(W["g2"], ((0, G_PAD - LORA_G), (0, 0)))
    prep_params = [mu_r, mu_k, mu_v, mu_l, W["w0"], W["a0"], W["k_k"], W["k_a"], W["w2"], W["a2"], g2p]
    r_, dec, kmod, v_, aa, bb, gg = _rwkv_prep(P, prep_params)
    y_scan, states, late = _cscan_fwd(r_, dec, kmod, v_, aa, bb, gather=late_shards)
    W.update({n: _full_weight(g, axis) for (n, axis), g in zip(LATE, late)})

    o_g, l_g = zip(*[_att_fwd(P, g) for g in range(len(ATT_PATTERNS))])
    att = _att_combine_fwd(o_g, l_g)
    y_att = _mm(att, W["w_att_out"], "nn", F32, "att_out")
    r_k = W["r_k"].reshape(1, D)
    rw = _rwkv_post(y_scan, r_, kmod, v_, gg, W["lnx_w"], W["lnx_b"], r_k)
    y_rwkv = _mm(rw, W["w_rwkv_out"], "nn", F32, "rwkv_out")

    bga, bgr = W["b_gate"][:, :D], W["b_gate"][:, D:]
    mix = _gate_fwd(P, bga, bgr, y_att, y_rwkv)
    mo = _mm(mix, W["w_o"], "nn", F32, "mix_out")
    x2, h2, rstd2 = _norm_fwd(x, mo, gt1, W["norm2_w"], sc2, sh2, "norm2_fwd")
    u = _mm(h2, W["w_up"], "nt", BF16, "ffn_up")
    conv_w8 = jnp.pad(W["conv_w"], ((0, SUBLANES - 3), (0, 0)))
    act = _conv_fwd(u, conv_w8, W["conv_b"])
    f = _mm(act, W["w_down"], "nn", F32, "ffn_down")
    loss_blk, dx3, df, dgt2, G["norm_f_w"] = _final(x2, f, gt2, W["norm_f_w"], target)
    loss = loss_blk[0, 0]

    dact = _mm(df, W["w_down"], "nt", BF16, "ffn_down_dx")
    G["w_down"] = _mm(act, df, "tn", BF16, "ffn_down_dw")
    duc, dwg, dwv, dbg, dbv = _conv_bwd_a(dact, u, conv_w8, W["conv_b"])
    G["conv_w"] = jnp.concatenate([dwg[0:3], dwv[0:3]], axis=1)
    G["conv_b"] = jnp.concatenate([dbg, dbv], axis=1)
    du = _conv_bwd_b(duc, conv_w8)
    dh2 = _mm(du, W["w_up"], "nn", F32, "ffn_up_dx")
    G["w_up"] = _mm(du, h2, "tn", BF16, "ffn_up_dw")
    dx2, dsh2, dsc2, G["norm2_w"], dmo, dgt1 = _norm_bwd(dh2, x2, rstd2, W["norm2_w"], sc2, dx3, mo, gt1, "norm2_bwd")
    dmix = _mm(dmo, W["w_o"], "nt", F32, "mix_out_dx")
    G["w_o"] = _mm(mix, dmo, "tn", BF16, "mix_out_dw")
    dy_att, dy_rwkv, dpga, dpgr, dbga, dbgr = _gate_bwd(dmix, P, bga, bgr, y_att, y_rwkv)
    G["b_gate"] = jnp.concatenate([dbga, dbgr], axis=1)

    datt = _mm(dy_att, W["w_att_out"], "nt", F32, "att_out_dx")
    G["w_att_out"] = _mm(att, dy_att, "tn", BF16, "att_out_dw")
    dcomb = _att_combine_bwd(datt, o_g, l_g)
    dp_att = []
    for g in range(len(ATT_PATTERNS)):
        dp_att += _att_bwd(P, o_g[g], l_g[g], dcomb[g], dcomb[3 + g], g)

    drw = _mm(dy_rwkv, W["w_rwkv_out"], "nt", F32, "rwkv_out_dx")
    G["w_rwkv_out"] = _mm(rw, dy_rwkv, "tn", BF16, "rwkv_out_dw")
    dy_scan, dr1, dk1, dv1, dgg, G["lnx_w"], G["lnx_b"], drk = _rwkv_post_bwd(drw, y_scan, r_, kmod, v_, gg, W["lnx_w"], W["lnx_b"], r_k)
    G["r_k"] = drk.reshape(W["r_k"].shape)
    late_blocks = [_owner_blocks(G[n], axis) for n, axis in LATE] if late_shards else []
    (dr2, ddec, dk2, dv2, daa, dbb), late_parts = _cscan_bwd(r_, dec, kmod, v_, aa, bb, states, dy_scan, scatter=late_blocks)
    pb = _rwkv_prep_bwd(P, prep_params, [dr2, ddec, dk2, dv2, daa, dbb, dgg], [dr1, None, dk1, dv1, None, None, None])
    dp_rkv, dp_lora, dpar = list(pb[0:3]), pb[3], pb[4:]
    dmu_r, dmu_k, dmu_v, dmu_l, G["w0"], G["a0"], G["k_k"], G["k_a"], G["w2"], G["a2"], dg2p = dpar
    G["g2"] = dg2p[0:LORA_G]
    G["mu_shift"] = jnp.concatenate([dmu_r, dmu_k, dmu_v, dmu_l[:, :LORA_W + LORA_A], dmu_l[:, LANES:LANES + LORA_G]], axis=1)

    dP = jnp.concatenate(dp_rkv + [dpga, dpgr] + dp_att + [dp_lora], axis=1)
    G["w_in"] = _w_in_blocks(_mm(dP, h1, "tn", BF16, "proj_in_dw"))
    if late_shards:
        dh1, (w_in_parts,) = _mm(dP, w_in_p, "nn", F32, "proj_in_dx", scatter=[G["w_in"]])
        done = dict(zip([n for n, _ in LATE] + ["w_in"], list(late_parts) + [w_in_parts]))
    else:
        dh1, done = _mm(dP, w_in_p, "nn", F32, "proj_in_dx"), {}
    grad_x, dsh1, dsc1, G["norm1_w"] = _norm_bwd(dh1, x, rstd1, W["norm1_w"], sc1, dx2, None, None, "norm1_bwd")
    dada = jnp.concatenate([dsh1, dsc1, dgt1, dsh2, dsc2, dgt2], axis=1)
    G["b_ada"] = dada
    return loss, grad_x, G, done


def _full_weight(gathered, axis):
    _, rows, cols = gathered.shape
    if axis == 0:
        return gathered.reshape(N_DEV * rows, cols)
    return gathered.transpose(1, 0, 2).reshape(rows, N_DEV * cols)


def _owner_blocks(g, axis):
    rows, cols = g.shape
    g = g.astype(BF16)
    if axis == 0:
        return g.reshape(N_DEV, rows // N_DEV, cols)
    return g.reshape(rows, N_DEV, cols // N_DEV).transpose(1, 0, 2)


def kernel(x, c, w_ada, b_ada, norm1_w, w_in, b_gate, mu_shift, w0, w2, a0, a2, g2, k_k, k_a, r_k, lnx_w, lnx_b, w_att_out, w_rwkv_out, w_o, norm2_w, w_up, conv_w, conv_b, w_down, norm_f_w, loss_target, m_w_ada, m_b_ada, m_norm1_w, m_w_in, m_b_gate, m_mu_shift, m_w0, m_w2, m_a0, m_a2, m_g2, m_k_k, m_k_a, m_r_k, m_lnx_w, m_lnx_b, m_w_att_out, m_w_rwkv_out, m_w_o, m_norm2_w, m_w_up, m_conv_w, m_conv_b, m_w_down, m_norm_f_w, v_w_ada, v_b_ada, v_norm1_w, v_w_in, v_b_gate, v_mu_shift, v_w0, v_w2, v_a0, v_a2, v_g2, v_k_k, v_k_a, v_r_k, v_lnx_w, v_lnx_b, v_w_att_out, v_w_rwkv_out, v_w_o, v_norm2_w, v_w_up, v_conv_w, v_conv_b, v_w_down, v_norm_f_w):
    env = dict(locals())
    w_shard = {n: env[n] for n in WEIGHTS}
    m_shard = {n: env["m_" + n] for n in WEIGHTS}
    v_shard = {n: env["v_" + n] for n in WEIGHTS}

    def mat(shards, n):
        return jnp.swapaxes(shards[n][0], 0, 1) if n in TRANSPOSED else shards[n][0]

    c_all, *gathered = _gather_via_sibling([c] + [mat(w_shard, n).astype(BF16) for n, _ in EARLY], "gather_weights")
    c_all = c_all.reshape(N_DEV, D)
    W = {n: _full_weight(g, axis) for (n, axis), g in zip(EARLY, gathered)}
    for n in REPLICATED:
        W[n] = w_shard[n].reshape(1, -1) if n != "r_k" else w_shard[n][0]
    ada_cols = _ada_partial(c_all, w_shard["w_ada"][0])
    ada_rows, = _exchange([ada_cols[:, None, :]], False, "ada_rows")
    ada = _ada_bias(ada_rows.reshape(1, -1), w_shard["b_ada"])

    late_shards = [mat(w_shard, n).astype(BF16) for n, _ in LATE]
    loss, grad_x, G, parts = _local_step(x[0], ada, W, late_shards, loss_target[0])
    loss = lax.psum(loss, ("x", "y", "c"))

    row = lambda a: a.reshape(1, -1)
    small = jnp.concatenate([jnp.pad(row(G[n]), ((0, 0), (0, (-G[n].size) % LANES))) for n in REPLICATED], axis=1)
    sparts, dada_all = _exchange([small, G["b_ada"].reshape(N_DEV, 1, -1)], [True, False], "gather_small_grads")
    parts["w_ada"] = _ada_wgrad(c_all.T, dada_all.reshape(N_DEV, -1))[None]

    rest = [(n, axis) for n, axis in SHARDED if n not in parts]
    parts.update(zip([n for n, _ in rest], _exchange([_owner_blocks(G[n], axis) for n, axis in rest], False, "scatter_grads")))
    out = {}
    for n, p in parts.items():
        res = _sum_adam(p, mat(w_shard, n), mat(m_shard, n), mat(v_shard, n), "adam_" + n)
        if n in TRANSPOSED:
            res = [jnp.swapaxes(a, 0, 1) for a in res]
        for kind, a in zip(("grad", "delta", "new_m", "new_v"), res):
            out[kind, n] = a[None]

    res = _adam_vectors(sparts, *[[row(s[n]) for n in REPLICATED] for s in (w_shard, m_shard, v_shard)])
    for n, four in zip(REPLICATED, res):
        for kind, a in zip(("grad", "delta", "new_m", "new_v"), four):
            out[kind, n] = a.reshape(w_shard[n].shape)

    return (loss, grad_x[None], *[out[kind, n] for kind in ("grad", "delta", "new_m", "new_v") for n in WEIGHTS])
```

```python
import functools

import jax
import jax.numpy as jnp
from jax import lax
from jax.experimental import pallas as pl
from jax.experimental.pallas import tpu as pltpu

F32 = jnp.float32
BF16 = jnp.bfloat16

D = 1024
HEAD = 64
ATT_PATTERNS = ((128, 1), (512, 4), (2048, 16))
ATT_HEADS = 8
ATT_W = ATT_HEADS * HEAD
ATT_IN = 3 * 3 * ATT_W
QBLK = 128
N_HEADS = D // HEAD
LORA_W, LORA_A, LORA_G = 64, 64, 160
RWKV_IN = 3 * D + LORA_W + LORA_A + LORA_G
N_IN = ATT_IN + RWKV_IN + 2 * D
D_FF = 2816
RMS_EPS = 1e-6
GN_EPS = 64e-5
N_DEV = 8
LANES = 128
SUBLANES = 8

C_R, C_K, C_V, C_GA, C_GR = 0, 1024, 2048, 3072, 4096
C_ATT = 5120
C_LORA = C_ATT + ATT_IN
LORA_PAD = 512
G_PAD = 256
N_PAD = C_LORA + LORA_PAD

ADAM_LR, ADAM_B1, ADAM_B2, ADAM_EPS, ADAM_WD, ADAM_STEP = 0.001, 0.9, 0.999, 1e-08, 0.01, 10

VMEM_LIMIT = 56 * 1024 * 1024

_MESH = pl.DeviceIdType.MESH


def _cparams(sem):
    return pltpu.CompilerParams(dimension_semantics=sem, vmem_limit_bytes=VMEM_LIMIT)


def _tile(dim, pref):
    if dim <= pref:
        return dim
    best = None
    for t in range(LANES, pref + 1, LANES):
        if dim % t == 0:
            best = t
    assert best is not None, dim
    return best


MM_TILES = {"nn": (1024, 1408, 1408), "nt": (1024, 2048, 1408), "tn": (1408, 1408, 1024)}


def _mm(a, b, mode, out_dtype, name, scatter=()):
    if mode == "nn":
        (M, K), (K2, N) = a.shape, b.shape
    elif mode == "nt":
        (M, K), (N, K2) = a.shape, b.shape
    else:
        (K, M), (K2, N) = a.shape, b.shape
    assert K == K2, (a.shape, b.shape, mode)
    tm, tn, tk = (_tile(dim, pref) for dim, pref in zip((M, N, K), MM_TILES[mode]))
    nk = K // tk
    grid = (M // tm, N // tn, nk)
    n_x = len(scatter)
    dims = {"nn": (((1,), (0,)), ((), ())), "nt": (((1,), (1,)), ((), ())), "tn": (((0,), (0,)), ((), ()))}[mode]

    def body(*refs):
        a_ref, b_ref = refs[:2]
        o_ref, acc_ref = refs[2 + n_x], refs[3 + 2 * n_x]
        finish = _hosted_exchange(refs[2:2 + n_x] + refs[3 + n_x:3 + 2 * n_x] + refs[4 + 2 * n_x:], n_x, False, grid)
        k = pl.program_id(2)
        part = lax.dot_general(a_ref[...].astype(BF16), b_ref[...].astype(BF16), dims,
                               preferred_element_type=F32)
        if nk == 1:
            o_ref[...] = part.astype(o_ref.dtype)
        else:
            @pl.when(k == 0)
            def _():
                acc_ref[...] = part

            @pl.when(jnp.logical_and(k > 0, k < nk - 1))
            def _():
                acc_ref[...] += part

            @pl.when(k == nk - 1)
            def _():
                o_ref[...] = (acc_ref[...] + part).astype(o_ref.dtype)
        finish()

    a_spec = pl.BlockSpec((tk, tm), lambda i, j, k: (k, i)) if mode == "tn" else pl.BlockSpec((tm, tk), lambda i, j, k: (i, k))
    b_spec = pl.BlockSpec((tn, tk), lambda i, j, k: (j, k)) if mode == "nt" else pl.BlockSpec((tk, tn), lambda i, j, k: (k, j))
    any_spec = pl.BlockSpec(memory_space=pl.ANY)
    outs = pl.pallas_call(
        body, name=name, grid=grid,
        in_specs=[a_spec, b_spec] + [any_spec] * n_x,
        out_specs=[pl.BlockSpec((tm, tn), lambda i, j, k: (i, j))] + [any_spec] * n_x,
        out_shape=[jax.ShapeDtypeStruct((M, N), out_dtype)] + _exchange_shapes(scatter, False),
        scratch_shapes=[pltpu.VMEM((tm, tn) if nk > 1 else (SUBLANES, LANES), F32)] + (_exchange_scratch(n_x) if n_x else []),
        compiler_params=_cparams(("arbitrary",) * 3 if n_x else ("parallel", "parallel", "arbitrary")),
    )(a, b, *scatter)
    return (outs[0], outs[1:]) if n_x else outs[0]


def _rows(tm, w, col=0):
    return pl.BlockSpec((tm, w), lambda i: (i, col))


def _full(shape):
    return pl.BlockSpec(shape, lambda i: (0,) * len(shape))


def _shift_down(x, halo, k, first):
    rolled = pltpu.roll(x, k, 0)
    row = lax.broadcasted_iota(jnp.int32, x.shape, 0)
    out = rolled
    n_halo = halo.shape[0]
    for j in range(k):
        h = jnp.where(first, 0.0, halo[n_halo - k + j:n_halo - k + j + 1, :])
        out = jnp.where(row == j, h, out)
    return out


def _shift_up(x, halo, k, last):
    n = x.shape[0]
    rolled = pltpu.roll(x, n - k, 0)
    row = lax.broadcasted_iota(jnp.int32, x.shape, 0)
    out = rolled
    for j in range(k):
        h = jnp.where(last, 0.0, halo[j:j + 1, :])
        out = jnp.where(row == n - k + j, h, out)
    return out


def _acc(ref, val, first):
    @pl.when(first)
    def _():
        ref[...] = val

    @pl.when(jnp.logical_not(first))
    def _():
        ref[...] += val


def _colsum(x):
    return jnp.sum(x, axis=0, keepdims=True)


def _norm_fwd(x, mo, gt, nw, sc, sh, name, tm=256):
    S = x.shape[0]
    has_res = mo is not None

    def body(*refs):
        if has_res:
            x_ref, mo_ref, gt_ref, nw_ref, sc_ref, sh_ref, x2_ref, h_ref, rs_ref = refs
            x2 = x_ref[...] + gt_ref[...] * mo_ref[...]
            x2_ref[...] = x2
        else:
            x_ref, nw_ref, sc_ref, sh_ref, h_ref, rs_ref = refs
            x2 = x_ref[...]
        rstd = lax.rsqrt(jnp.mean(x2 * x2, axis=-1, keepdims=True) + RMS_EPS)
        rs_ref[...] = rstd
        h_ref[...] = ((x2 * rstd * nw_ref[...]) * (1.0 + sc_ref[...]) + sh_ref[...]).astype(BF16)

    vec = _full((1, D))
    ins = [x, mo, gt, nw, sc, sh] if has_res else [x, nw, sc, sh]
    in_specs = [_rows(tm, D), _rows(tm, D), vec, vec, vec, vec] if has_res else [_rows(tm, D), vec, vec, vec]
    outs = [jax.ShapeDtypeStruct((S, D), BF16), jax.ShapeDtypeStruct((S, 1), F32)]
    out_specs = [_rows(tm, D), _rows(tm, 1)]
    if has_res:
        outs = [jax.ShapeDtypeStruct((S, D), F32)] + outs
        out_specs = [_rows(tm, D)] + out_specs
    return pl.pallas_call(body, name=name, grid=(S // tm,), in_specs=in_specs, out_specs=out_specs,
                          out_shape=outs, compiler_params=_cparams(("parallel",)))(*ins)


def _norm_bwd(dh, xin, rstd, nw, sc, dres, mo, gt, name, tm=256):
    S = xin.shape[0]
    has_res = mo is not None

    def body(*refs):
        if has_res:
            dh_ref, x_ref, rs_ref, nw_ref, sc_ref, dres_ref, mo_ref, gt_ref, dx_ref, dsh_ref, dsc_ref, dnw_ref, dmo_ref, dgt_ref = refs
        else:
            dh_ref, x_ref, rs_ref, nw_ref, sc_ref, dres_ref, dx_ref, dsh_ref, dsc_ref, dnw_ref = refs
        first = pl.program_id(0) == 0
        dh = dh_ref[...]
        rstd = rs_ref[...]
        n = x_ref[...] * rstd
        w = nw_ref[...]
        _acc(dsh_ref, _colsum(dh), first)
        _acc(dsc_ref, _colsum(dh * (n * w)), first)
        dnw = dh * (1.0 + sc_ref[...])
        _acc(dnw_ref, _colsum(dnw * n), first)
        dn = dnw * w
        dx = dres_ref[...] + rstd * (dn - n * jnp.mean(dn * n, axis=-1, keepdims=True))
        dx_ref[...] = dx
        if has_res:
            dmo_ref[...] = (dx * gt_ref[...]).astype(BF16)
            _acc(dgt_ref, _colsum(dx * mo_ref[...]), first)

    vec = _full((1, D))
    vshape = jax.ShapeDtypeStruct((1, D), F32)
    ins = [dh, xin, rstd, nw, sc, dres] + ([mo, gt] if has_res else [])
    in_specs = [_rows(tm, D), _rows(tm, D), _rows(tm, 1), vec, vec, _rows(tm, D)] + ([_rows(tm, D), vec] if has_res else [])
    outs = [jax.ShapeDtypeStruct((S, D), F32), vshape, vshape, vshape]
    out_specs = [_rows(tm, D), vec, vec, vec]
    if has_res:
        outs += [jax.ShapeDtypeStruct((S, D), BF16), vshape]
        out_specs += [_rows(tm, D), vec]
    return pl.pallas_call(body, name=name, grid=(S // tm,), in_specs=in_specs, out_specs=out_specs,
                          out_shape=outs, compiler_params=_cparams(("arbitrary",)))(*ins)


def _final(x2, f, gt2, nfw, target, tm=256):
    S = x2.shape[0]

    def body(x2_ref, f_ref, gt_ref, w_ref, t_ref, loss_ref, dx_ref, df_ref, dgt_ref, dw_ref):
        first = pl.program_id(0) == 0
        f = f_ref[...]
        gt = gt_ref[...]
        w = w_ref[...]
        x3 = x2_ref[...] + gt * f
        rstd = lax.rsqrt(jnp.mean(x3 * x3, axis=-1, keepdims=True) + RMS_EPS)
        n = x3 * rstd
        e = n * w - t_ref[...]
        part = 0.5 * jnp.sum(jnp.mean(e * e, axis=-1, keepdims=True), axis=0, keepdims=True)
        _acc(loss_ref, jnp.broadcast_to(part, (SUBLANES, LANES)), first)
        dy = e * (1.0 / D)
        _acc(dw_ref, _colsum(dy * n), first)
        dn = dy * w
        dx = rstd * (dn - n * jnp.mean(dn * n, axis=-1, keepdims=True))
        dx_ref[...] = dx
        df_ref[...] = (dx * gt).astype(BF16)
        _acc(dgt_ref, _colsum(dx * f), first)

    vec = _full((1, D))
    vshape = jax.ShapeDtypeStruct((1, D), F32)
    return pl.pallas_call(
        body, name="final_loss", grid=(S // tm,),
        in_specs=[_rows(tm, D), _rows(tm, D), vec, vec, _rows(tm, D)],
        out_specs=[_full((SUBLANES, LANES)), _rows(tm, D), _rows(tm, D), vec, vec],
        out_shape=[jax.ShapeDtypeStruct((SUBLANES, LANES), F32), jax.ShapeDtypeStruct((S, D), F32),
                   jax.ShapeDtypeStruct((S, D), BF16), vshape, vshape],
        compiler_params=_cparams(("arbitrary",)))(x2, f, gt2, nfw, target)


def _gate_fwd(P, bga, bgr, y_att, y_rwkv, tm=256):
    S = P.shape[0]

    def body(pa_ref, pr_ref, ba_ref, br_ref, ya_ref, yr_ref, mix_ref):
        ga = jax.nn.sigmoid(pa_ref[...] + ba_ref[...])
        gr = jax.nn.sigmoid(pr_ref[...] + br_ref[...])
        mix_ref[...] = (ga * ya_ref[...] + gr * yr_ref[...]).astype(BF16)

    vec = _full((1, D))
    return pl.pallas_call(
        body, name="gate_fwd", grid=(S // tm,),
        in_specs=[_rows(tm, D, C_GA // D), _rows(tm, D, C_GR // D), vec, vec, _rows(tm, D), _rows(tm, D)],
        out_specs=_rows(tm, D), out_shape=jax.ShapeDtypeStruct((S, D), BF16),
        compiler_params=_cparams(("parallel",)))(P, P, bga, bgr, y_att, y_rwkv)


def _gate_bwd(dmix, P, bga, bgr, y_att, y_rwkv, tm=256):
    S = P.shape[0]

    def body(dm_ref, pa_ref, pr_ref, ba_ref, br_ref, ya_ref, yr_ref, dya_ref, dyr_ref, dpa_ref, dpr_ref, dba_ref, dbr_ref):
        first = pl.program_id(0) == 0
        dm = dm_ref[...]
        ga = jax.nn.sigmoid(pa_ref[...] + ba_ref[...])
        gr = jax.nn.sigmoid(pr_ref[...] + br_ref[...])
        dya_ref[...] = (dm * ga).astype(BF16)
        dyr_ref[...] = (dm * gr).astype(BF16)
        dpa = dm * ya_ref[...] * ga * (1.0 - ga)
        dpr = dm * yr_ref[...] * gr * (1.0 - gr)
        dpa_ref[...] = dpa.astype(BF16)
        dpr_ref[...] = dpr.astype(BF16)
        _acc(dba_ref, _colsum(dpa), first)
        _acc(dbr_ref, _colsum(dpr), first)

    vec = _full((1, D))
    row = _rows(tm, D)
    rshape = jax.ShapeDtypeStruct((S, D), BF16)
    vshape = jax.ShapeDtypeStruct((1, D), F32)
    return pl.pallas_call(
        body, name="gate_bwd", grid=(S // tm,),
        in_specs=[row, _rows(tm, D, C_GA // D), _rows(tm, D, C_GR // D), vec, vec, row, row],
        out_specs=[row, row, row, row, vec, vec],
        out_shape=[rshape, rshape, rshape, rshape, vshape, vshape],
        compiler_params=_cparams(("arbitrary",)))(dmix, P, P, bga, bgr, y_att, y_rwkv)


CONV_TN = D_FF // 2
HALO = 16


def _conv_fwd(u, conv_w8, conv_b, tm=256, tn=CONV_TN):
    S = u.shape[0]
    nj = D_FF // tn

    def conv(u_ref, h_ref, w_ref, b_ref, first):
        u = u_ref[...].astype(F32)
        h = h_ref[...].astype(F32)
        w = w_ref[...]
        return b_ref[...] + w[0:1] * _shift_down(u, h, 2, first) + w[1:2] * _shift_down(u, h, 1, first) + w[2:3] * u

    def body(ug_ref, hg_ref, uv_ref, hv_ref, wg_ref, wv_ref, bg_ref, bv_ref, act_ref):
        first = pl.program_id(0) == 0
        g = conv(ug_ref, hg_ref, wg_ref, bg_ref, first)
        v = conv(uv_ref, hv_ref, wv_ref, bv_ref, first)
        act_ref[...] = (g * jax.nn.sigmoid(g) * v).astype(BF16)

    blk = lambda off: pl.BlockSpec((tm, tn), lambda i, j: (i, j + off))
    halo = lambda off: pl.BlockSpec((HALO, tn), lambda i, j: (jnp.maximum(i * (tm // HALO) - 1, 0), j + off))
    wsp = lambda off: pl.BlockSpec((SUBLANES, tn), lambda i, j: (0, j + off))
    bsp = lambda off: pl.BlockSpec((1, tn), lambda i, j: (0, j + off))
    return pl.pallas_call(
        body, name="conv_fwd", grid=(S // tm, nj),
        in_specs=[blk(0), halo(0), blk(nj), halo(nj), wsp(0), wsp(nj), bsp(0), bsp(nj)],
        out_specs=pl.BlockSpec((tm, tn), lambda i, j: (i, j)),
        out_shape=jax.ShapeDtypeStruct((S, D_FF), BF16),
        compiler_params=_cparams(("parallel", "parallel")))(u, u, u, u, conv_w8, conv_w8, conv_b, conv_b)


def _conv_bwd_a(dact, u, conv_w8, conv_b, tm=256, tn=CONV_TN):
    S = u.shape[0]
    nj = D_FF // tn

    def half(u_ref, h_ref, w_ref, b_ref, first):
        u = u_ref[...].astype(F32)
        h = h_ref[...].astype(F32)
        w = w_ref[...]
        u2, u1 = _shift_down(u, h, 2, first), _shift_down(u, h, 1, first)
        return b_ref[...] + w[0:1] * u2 + w[1:2] * u1 + w[2:3] * u, (u2, u1, u)

    def wgrad(d, taps):
        z = jnp.zeros((SUBLANES - 3, d.shape[1]), F32)
        return jnp.concatenate([_colsum(d * taps[0]), _colsum(d * taps[1]), _colsum(d * taps[2]), z], axis=0)

    def body(da_ref, ug_ref, hg_ref, uv_ref, hv_ref, wg_ref, wv_ref, bg_ref, bv_ref,
             d_ref, dwg_ref, dwv_ref, dbg_ref, dbv_ref):
        first = pl.program_id(1) == 0
        g, tg = half(ug_ref, hg_ref, wg_ref, bg_ref, first)
        v, tv = half(uv_ref, hv_ref, wv_ref, bv_ref, first)
        da = da_ref[...].astype(F32)
        sg = jax.nn.sigmoid(g)
        dg = da * v * (sg * (1.0 + g * (1.0 - sg)))
        dv = da * (g * sg)
        d_ref[0] = dg.astype(BF16)
        d_ref[1] = dv.astype(BF16)
        _acc(dwg_ref, wgrad(dg, tg), first)
        _acc(dwv_ref, wgrad(dv, tv), first)
        _acc(dbg_ref, _colsum(dg), first)
        _acc(dbv_ref, _colsum(dv), first)

    blk = lambda off: pl.BlockSpec((tm, tn), lambda j, i: (i, j + off))
    halo = lambda off: pl.BlockSpec((HALO, tn), lambda j, i: (jnp.maximum(i * (tm // HALO) - 1, 0), j + off))
    wsp = lambda off: pl.BlockSpec((SUBLANES, tn), lambda j, i: (0, j + off))
    bsp = lambda off: pl.BlockSpec((1, tn), lambda j, i: (0, j + off))
    f = jax.ShapeDtypeStruct
    outs = pl.pallas_call(
        body, name="conv_bwd_a", grid=(nj, S // tm),
        in_specs=[pl.BlockSpec((tm, tn), lambda j, i: (i, j)), blk(0), halo(0), blk(nj), halo(nj), wsp(0), wsp(nj), bsp(0), bsp(nj)],
        out_specs=[pl.BlockSpec((2, tm, tn), lambda j, i: (0, i, j)),
                   pl.BlockSpec((SUBLANES, tn), lambda j, i: (0, j)), pl.BlockSpec((SUBLANES, tn), lambda j, i: (0, j)),
                   pl.BlockSpec((1, tn), lambda j, i: (0, j)), pl.BlockSpec((1, tn), lambda j, i: (0, j))],
        out_shape=[f((2, S, D_FF), BF16), f((SUBLANES, D_FF), F32), f((SUBLANES, D_FF), F32),
                   f((1, D_FF), F32), f((1, D_FF), F32)],
        compiler_params=_cparams(("parallel", "arbitrary")))(dact, u, u, u, u, conv_w8, conv_w8, conv_b, conv_b)
    return outs


def _conv_bwd_b(duc, conv_w8, tm=256, tn=CONV_TN):
    _, S, W = duc.shape
    nj = W // tn
    n_rows = S // tm

    def body(d_ref, h_ref, w_ref, o_ref):
        last = pl.program_id(0) == n_rows - 1
        d = d_ref[...].astype(F32)
        h = h_ref[...].astype(F32)
        w = w_ref[...]
        o_ref[...] = (w[2:3] * d + w[1:2] * _shift_up(d, h, 1, last) + w[0:1] * _shift_up(d, h, 2, last)).astype(BF16)

    last_tile = S // HALO - 1
    return pl.pallas_call(
        body, name="conv_bwd_b", grid=(n_rows, 2 * nj),
        in_specs=[pl.BlockSpec((None, tm, tn), lambda i, j: (j // nj, i, j % nj)),
                  pl.BlockSpec((None, HALO, tn), lambda i, j: (j // nj, jnp.minimum((i + 1) * (tm // HALO), last_tile), j % nj)),
                  pl.BlockSpec((SUBLANES, tn), lambda i, j: (0, j))],
        out_specs=pl.BlockSpec((tm, tn), lambda i, j: (i, j)),
        out_shape=jax.ShapeDtypeStruct((S, 2 * W), BF16),
        compiler_params=_cparams(("parallel", "parallel")))(duc, duc, conv_w8)


ATT_SCALE = HEAD ** -0.5
NEG = -1e30
ATT_PAIRS = ATT_HEADS // 2


def _att_rows(n, d, S):
    per = S // (QBLK * d)
    r, m = n // per, n % per
    cur = pl.ds(m * (QBLK * d) + r, QBLK, stride=d)
    prv = pl.ds(jnp.maximum(m - 1, 0) * (QBLK * d) + r, QBLK, stride=d)
    return cur, prv, m > 0


def _att_slab(g, j):
    return (C_ATT + g * 3 * ATT_W + j * ATT_W) // LANES


def _heads(x):
    return x[:, 0:HEAD], x[:, HEAD:2 * HEAD]


ATT_NB = 4


def _stack(tiles):
    return jnp.concatenate([t[None] for t in tiles], axis=0)


def _att_operands(i, d, S, *sources):
    rows, has = [], []
    tiles = [[] for _ in sources]
    for bb in range(ATT_NB):
        cur, prv, has_prev = _att_rows(i * ATT_NB + bb, d, S)
        rows.append((cur, prv))
        has.append(has_prev)
        for t, (ref, use_cur) in zip(tiles, sources):
            t += _heads(ref[cur if use_cur else prv, :].astype(BF16))
    return rows, has, [_stack(t) for t in tiles]


def _att_mask(s_c, s_p, has_prev):
    qi = lax.broadcasted_iota(jnp.int32, (QBLK, QBLK), 0)
    kj = lax.broadcasted_iota(jnp.int32, (QBLK, QBLK), 1)
    s_c = jnp.where(kj <= qi, s_c * ATT_SCALE, NEG)
    s_p = jnp.where(jnp.logical_and(kj >= qi, has_prev), s_p * ATT_SCALE, NEG)
    return s_c, s_p


def _att_fwd(P, g):
    S = P.shape[0]
    d = ATT_PATTERNS[g][1]

    def body(q_ref, k_ref, v_ref, o_ref, l_ref):
        def group(i, carry):
            rows, has, (q, kc, kp, vc, vp) = _att_operands(i, d, S, (q_ref, True), (k_ref, True), (k_ref, False),
                                                           (v_ref, True), (v_ref, False))
            s_c_all, s_p_all = _dot16(q, kc, "nt"), _dot16(q, kp, "nt")
            p_c, p_p, den, lse = [], [], [], []
            for e in range(2 * ATT_NB):
                s_c, s_p = _att_mask(s_c_all[e], s_p_all[e], has[e // 2])
                m = jnp.maximum(jnp.max(s_c, axis=1, keepdims=True), jnp.max(s_p, axis=1, keepdims=True))
                pc, pp = jnp.exp(s_c - m), jnp.exp(s_p - m)
                den.append(jnp.sum(pc, axis=1, keepdims=True) + jnp.sum(pp, axis=1, keepdims=True))
                lse.append(jnp.broadcast_to(m + jnp.log(den[e]), (QBLK, HEAD)))
                p_c.append(pc)
                p_p.append(pp)
            num = _dot16(_stack(p_c), vc, "nn") + _dot16(_stack(p_p), vp, "nn")
            for bb, (cur, _) in enumerate(rows):
                o_ref[cur, :] = jnp.concatenate([num[2 * bb] / den[2 * bb], num[2 * bb + 1] / den[2 * bb + 1]], axis=1)
                l_ref[cur, :] = jnp.concatenate(lse[2 * bb:2 * bb + 2], axis=1)
            return carry

        lax.fori_loop(0, S // QBLK // ATT_NB, group, 0)

    slab = lambda j: pl.BlockSpec((S, LANES), lambda i: (0, _att_slab(g, j) + i))
    out = pl.BlockSpec((S, LANES), lambda i: (0, i))
    shp = jax.ShapeDtypeStruct((S, ATT_W), F32)
    return pl.pallas_call(body, name=f"att_fwd_g{g}", grid=(ATT_PAIRS,), in_specs=[slab(0), slab(1), slab(2)],
                          out_specs=[out, out], out_shape=[shp, shp], compiler_params=_cparams(("parallel",)))(P, P, P)


def _att_bwd(P, o, l, do, dl, g):
    S = P.shape[0]
    d = ATT_PATTERNS[g][1]

    def body(q_ref, k_ref, v_ref, o_ref, l_ref, do_ref, dl_ref, dq_ref, dk_ref, dv_ref, dq_acc, dk_acc, dv_acc):
        dk_acc[...] = jnp.zeros_like(dk_acc)
        dv_acc[...] = jnp.zeros_like(dv_acc)

        def group(i, carry):
            rows, has, (q, kc, kp, vc, vp, dob) = _att_operands(
                i, d, S, (q_ref, True), (k_ref, True), (k_ref, False), (v_ref, True), (v_ref, False), (do_ref, True))
            s_c_all, s_p_all = _dot16(q, kc, "nt"), _dot16(q, kp, "nt")
            dp_c_all, dp_p_all = _dot16(dob, vc, "nt"), _dot16(dob, vp, "nt")
            p_c, p_p, ds_c, ds_p = [], [], [], []
            for bb, (cur, _) in enumerate(rows):
                dd2 = do_ref[cur, :] * o_ref[cur, :] - dl_ref[cur, :]
                for h, (dd, lse) in enumerate(zip(_heads(dd2), _heads(l_ref[cur, :]))):
                    e = 2 * bb + h
                    s_c, s_p = _att_mask(s_c_all[e], s_p_all[e], has[bb])
                    pc, pp = jnp.exp(s_c - lse[:, 0:1]), jnp.exp(s_p - lse[:, 0:1])
                    delta = jnp.sum(dd, axis=1, keepdims=True)
                    p_c.append(pc)
                    p_p.append(pp)
                    ds_c.append(pc * (dp_c_all[e] - delta) * ATT_SCALE)
                    ds_p.append(pp * (dp_p_all[e] - delta) * ATT_SCALE)
            p_c, p_p, ds_c, ds_p = map(_stack, (p_c, p_p, ds_c, ds_p))
            dq = _dot16(ds_c, kc, "nn") + _dot16(ds_p, kp, "nn")
            dk_c, dk_p = _dot16(ds_c, q, "tn"), _dot16(ds_p, q, "tn")
            dv_c, dv_p = _dot16(p_c, dob, "tn"), _dot16(p_p, dob, "tn")
            pair = lambda x, bb: jnp.concatenate([x[2 * bb], x[2 * bb + 1]], axis=1)
            for bb, (cur, prv) in enumerate(rows):
                dq_acc[cur, :] = pair(dq, bb)
                dk_acc[cur, :] += pair(dk_c, bb)
                dv_acc[cur, :] += pair(dv_c, bb)
                dk_acc[prv, :] += pair(dk_p, bb)
                dv_acc[prv, :] += pair(dv_p, bb)
            return carry

        lax.fori_loop(0, S // QBLK // ATT_NB, group, 0)
        dq_ref[...] = dq_acc[...].astype(BF16)
        dk_ref[...] = dk_acc[...].astype(BF16)
        dv_ref[...] = dv_acc[...].astype(BF16)

    slab = lambda j: pl.BlockSpec((S, LANES), lambda i: (0, _att_slab(g, j) + i))
    blk128 = pl.BlockSpec((S, LANES), lambda i: (0, i))
    shp = jax.ShapeDtypeStruct((S, ATT_W), BF16)
    return pl.pallas_call(body, name=f"att_bwd_g{g}", grid=(ATT_PAIRS,),
                          in_specs=[slab(0), slab(1), slab(2)] + [blk128] * 4, out_specs=[blk128] * 3, out_shape=[shp] * 3,
                          scratch_shapes=[pltpu.VMEM((S, LANES), F32)] * 3,
                          compiler_params=_cparams(("parallel",)))(P, P, P, o, l, do, dl)


def _att_weights(l_refs):
    l0, l1, l2 = [r[...] for r in l_refs]
    m = jnp.maximum(jnp.maximum(l0, l1), l2)
    e = (jnp.exp(l0 - m), jnp.exp(l1 - m), jnp.exp(l2 - m))
    inv = 1.0 / (e[0] + e[1] + e[2])
    return [x * inv for x in e]


def _att_combine_fwd(os, ls, tm=512):
    S = os[0].shape[0]

    def body(o0, o1, o2, l0, l1, l2, a_ref):
        w = _att_weights((l0, l1, l2))
        a_ref[...] = (w[0] * o0[...] + w[1] * o1[...] + w[2] * o2[...]).astype(BF16)

    row = _rows(tm, ATT_W)
    return pl.pallas_call(body, name="att_combine_fwd", grid=(S // tm,), in_specs=[row] * 6, out_specs=row,
                          out_shape=jax.ShapeDtypeStruct((S, ATT_W), BF16),
                          compiler_params=_cparams(("parallel",)))(*os, *ls)


def _att_combine_bwd(da, os, ls, tm=512):
    S = da.shape[0]

    def body(da_ref, o0, o1, o2, l0, l1, l2, *out_refs):
        da = da_ref[...]
        w = _att_weights((l0, l1, l2))
        dw = (da * o0[...], da * o1[...], da * o2[...])
        mean = w[0] * dw[0] + w[1] * dw[1] + w[2] * dw[2]
        for g in range(3):
            out_refs[g][...] = w[g] * da
            out_refs[3 + g][...] = w[g] * (dw[g] - mean)

    row = _rows(tm, ATT_W)
    shp = jax.ShapeDtypeStruct((S, ATT_W), F32)
    return pl.pallas_call(body, name="att_combine_bwd", grid=(S // tm,), in_specs=[row] * 7, out_specs=[row] * 6,
                          out_shape=[shp] * 6, compiler_params=_cparams(("parallel",)))(da, *os, *ls)


@jax.custom_vjp
def _bdot(a, b):
    return jnp.dot(a.astype(BF16), b.astype(BF16), preferred_element_type=F32)


def _bdot_fwd(a, b):
    return _bdot(a, b), (a, b)


def _bdot_bwd(res, ct):
    a, b = res
    ct16 = ct.astype(BF16)
    da = lax.dot_general(ct16, b.astype(BF16), (((1,), (1,)), ((), ())), preferred_element_type=F32)
    db = lax.dot_general(a.astype(BF16), ct16, (((0,), (0,)), ((), ())), preferred_element_type=F32)
    return da, db


_bdot.defvjp(_bdot_fwd, _bdot_bwd)


def _two_piece_dot(x, m):
    hi = x.astype(BF16)
    lo = (x - hi.astype(F32)).astype(BF16)
    return jnp.dot(hi, m, preferred_element_type=F32) + jnp.dot(lo, m, preferred_element_type=F32)


def _head_sum_impl(x):
    sel = (lax.broadcasted_iota(jnp.int32, (D, LANES), 0) // HEAD == lax.broadcasted_iota(jnp.int32, (D, LANES), 1)).astype(BF16)
    sel_t = (lax.broadcasted_iota(jnp.int32, (LANES, D), 1) // HEAD == lax.broadcasted_iota(jnp.int32, (LANES, D), 0)).astype(BF16)
    return _two_piece_dot(_two_piece_dot(x, sel), sel_t)


@jax.custom_vjp
def _head_sum(x):
    return _head_sum_impl(x)


_head_sum.defvjp(lambda x: (_head_sum_impl(x), None), lambda _, ct: (_head_sum_impl(ct),))


def _softplus(z):
    return jnp.maximum(z, 0.0) + jnp.log(1.0 + jnp.exp(-jnp.abs(z)))


def _rwkv_prep_fn(zr, zrp, zk, zkp, zv, zvp, zl, zlp, mu_r, mu_k, mu_v, mu_l, w0, a0, k_k, k_a, w2, a2, g2p):
    r = zr + (zrp - zr) * mu_r
    k = zk + (zkp - zk) * mu_k
    v = zv + (zvp - zv) * mu_v
    lo = zl + (zlp - zl) * mu_l
    w_low, a_low, g_low = lo[:, 0:LORA_W], lo[:, LORA_W:LORA_W + LORA_A], lo[:, LANES:LANES + G_PAD]
    w_log = -_softplus(-(w0 + _bdot(jnp.tanh(w_low), w2))) - 0.5
    decay = -jnp.exp(w_log)
    a = jax.nn.sigmoid(a0 + _bdot(a_low, a2))
    g = _bdot(jax.nn.sigmoid(g_low), g2p)
    kmod = k * (1.0 + (a - 1.0) * k_a)
    kk = k * k_k
    kk = kk / jnp.maximum(jnp.sqrt(_head_sum(kk * kk)), 1e-12)
    return r, decay, kmod, v, -kk, kk * a, g


def _rwkv_prep_specs(tm, blk=lambda i: i):
    vec = _full((1, D))
    rows = lambda w, col: pl.BlockSpec((tm, w), lambda i: (blk(i), col))
    prev = lambda w, col: pl.BlockSpec((SUBLANES, w), lambda i: (jnp.maximum(blk(i) * (tm // SUBLANES) - 1, 0), col))
    slabs = []
    for col in (C_R // D, C_K // D, C_V // D):
        slabs += [rows(D, col), prev(D, col)]
    slabs += [rows(LORA_PAD, C_LORA // LORA_PAD), prev(LORA_PAD, C_LORA // LORA_PAD)]
    params = [vec, vec, vec, _full((1, LORA_PAD)), vec, vec, vec, vec,
              _full((LORA_W, D)), _full((LORA_A, D)), _full((G_PAD, D))]
    return slabs, params


def _prep_inputs(refs, first):
    vals = []
    for s in range(4):
        z = refs[2 * s][...]
        vals += [z, _shift_down(z, refs[2 * s + 1][...], 1, first)]
    return vals + [r[...] for r in refs[8:19]]


def _rwkv_prep(P, params, tm=256):
    S = P.shape[0]
    slabs, pspecs = _rwkv_prep_specs(tm)

    def body(*refs):
        outs = _rwkv_prep_fn(*_prep_inputs(refs, pl.program_id(0) == 0))
        for o_ref, val in zip(refs[19:], outs):
            o_ref[...] = val

    shp = jax.ShapeDtypeStruct((S, D), F32)
    return pl.pallas_call(body, name="rwkv_prep", grid=(S // tm,), in_specs=slabs + pspecs,
                          out_specs=[_rows(tm, D)] * 7, out_shape=[shp] * 7,
                          compiler_params=_cparams(("parallel",)))(*([P] * 8), *params)


def _rwkv_prep_bwd(P, params, cts_a, cts_b, tm=128):
    S = P.shape[0]
    nblk = S // tm
    blk = lambda i: nblk - 1 - i
    slabs, pspecs = _rwkv_prep_specs(tm, blk)
    has_b = [c is not None for c in cts_b]
    n_ct = 7 + sum(has_b)

    def body(*refs):
        start = pl.program_id(0) == 0
        ins = _prep_inputs(refs, pl.program_id(0) == nblk - 1)
        ct_refs = refs[19:19 + n_ct]
        out_refs = refs[19 + n_ct:19 + n_ct + 15]
        carry_refs = refs[19 + n_ct + 15:]

        @pl.when(start)
        def _():
            for c_ref in carry_refs:
                c_ref[...] = jnp.zeros_like(c_ref)

        cts, pos = [], 7
        for i in range(7):
            c = ct_refs[i][...]
            if has_b[i]:
                c = c + ct_refs[pos][...]
                pos += 1
            cts.append(c)
        _, vjp = jax.vjp(_rwkv_prep_fn, *ins)
        grads = vjp(tuple(cts))
        for s in range(4):
            shifted = grads[2 * s + 1]
            out_refs[s][...] = (grads[2 * s] + _shift_up(shifted, carry_refs[s][...], 1, start)).astype(BF16)
            carry_refs[s][0:1, :] = shifted[0:1, :]
        for i in range(11):
            _acc(out_refs[4 + i], grads[8 + i], start)

    ct_in = list(cts_a) + [c for c in cts_b if c is not None]
    row = lambda w: pl.BlockSpec((tm, w), lambda i: (blk(i), 0))
    f = jax.ShapeDtypeStruct
    zshapes = [f((S, D), BF16)] * 3 + [f((S, LORA_PAD), BF16)]
    pshapes = [f((1, D), F32)] * 3 + [f((1, LORA_PAD), F32)] + [f((1, D), F32)] * 4 + [f((LORA_W, D), F32), f((LORA_A, D), F32), f((G_PAD, D), F32)]
    return pl.pallas_call(
        body, name="rwkv_prep_bwd", grid=(nblk,),
        in_specs=slabs + pspecs + [row(D)] * n_ct,
        out_specs=[row(D), row(D), row(D), row(LORA_PAD)] + pspecs,
        out_shape=zshapes + pshapes,
        scratch_shapes=[pltpu.VMEM((SUBLANES, D), F32)] * 3 + [pltpu.VMEM((SUBLANES, LORA_PAD), F32)],
        compiler_params=_cparams(("arbitrary",)))(*([P] * 8), *params, *ct_in)


def _rwkv_post_fn(y, r, kmod, v, g, lnx_w, lnx_b, r_k):
    mean = _head_sum(y) * (1.0 / HEAD)
    yc = y - mean
    var = _head_sum(yc * yc) * (1.0 / HEAD)
    yn = yc * lax.rsqrt(var + GN_EPS) * lnx_w + lnx_b
    bonus = _head_sum(r * kmod * r_k) * v
    return (yn + bonus) * g


def _rwkv_post(y, r, kmod, v, g, lnx_w, lnx_b, r_k, tm=256):
    S = y.shape[0]

    def body(y_ref, r_ref, k_ref, v_ref, g_ref, w_ref, b_ref, rk_ref, o_ref):
        o_ref[...] = _rwkv_post_fn(y_ref[...], r_ref[...], k_ref[...], v_ref[...], g_ref[...],
                                   w_ref[...], b_ref[...], rk_ref[...]).astype(BF16)

    row, vec = _rows(tm, D), _full((1, D))
    return pl.pallas_call(body, name="rwkv_post", grid=(S // tm,), in_specs=[row] * 5 + [vec] * 3, out_specs=row,
                          out_shape=jax.ShapeDtypeStruct((S, D), BF16),
                          compiler_params=_cparams(("parallel",)))(y, r, kmod, v, g, lnx_w, lnx_b, r_k)


def _rwkv_post_bwd(drw, y, r, kmod, v, g, lnx_w, lnx_b, r_k, tm=256):
    S = y.shape[0]

    def body(d_ref, y_ref, r_ref, k_ref, v_ref, g_ref, w_ref, b_ref, rk_ref, *out_refs):
        first = pl.program_id(0) == 0
        _, vjp = jax.vjp(_rwkv_post_fn, y_ref[...], r_ref[...], k_ref[...], v_ref[...], g_ref[...],
                         w_ref[...], b_ref[...], rk_ref[...])
        grads = vjp(d_ref[...])
        for i in range(5):
            out_refs[i][...] = grads[i]
        for i in range(5, 8):
            _acc(out_refs[i], grads[i], first)

    row, vec = _rows(tm, D), _full((1, D))
    f = jax.ShapeDtypeStruct
    return pl.pallas_call(body, name="rwkv_post_bwd", grid=(S // tm,), in_specs=[row] * 6 + [vec] * 3,
                          out_specs=[row] * 5 + [vec] * 3, out_shape=[f((S, D), F32)] * 5 + [f((1, D), F32)] * 3,
                          compiler_params=_cparams(("arbitrary",)))(drw, y, r, kmod, v, g, lnx_w, lnx_b, r_k)


CHUNK = 64
CHUNK_TB = 256
_DOT_DIMS = {"nn": (((2,), (1,)), ((0,), (0,))), "nt": (((2,), (2,)), ((0,), (0,))), "tn": (((1,), (1,)), ((0,), (0,)))}


def _dot16(x, y, mode):
    return lax.dot_general(x.astype(BF16), y.astype(BF16), _DOT_DIMS[mode], preferred_element_type=F32)


@functools.partial(jax.custom_vjp, nondiff_argnums=(2,))
def _mm16(x, y, mode):
    return _dot16(x, y, mode)


def _mm16_fwd(x, y, mode):
    return _dot16(x, y, mode), (x, y)


def _mm16_bwd(mode, res, ct):
    x, y = res
    if mode == "nn":
        return _dot16(ct, y, "nt"), _dot16(x, ct, "tn")
    if mode == "nt":
        return _dot16(ct, y, "nn"), _dot16(ct, x, "tn")
    return _dot16(y, ct, "nt"), _dot16(x, ct, "nn")


_mm16.defvjp(_mm16_fwd, _mm16_bwd)


def _tri_sum(x, upper):
    T = x.shape[0]
    i = lax.broadcasted_iota(jnp.int32, (T, T), 0)
    j = lax.broadcasted_iota(jnp.int32, (T, T), 1)
    tri = ((j >= i) if upper else (i >= j)).astype(BF16)
    out, rest = None, x
    for _ in range(3):
        piece = rest.astype(BF16)
        rest = rest - piece.astype(F32)
        part = jnp.dot(tri, piece, preferred_element_type=F32)
        out = part if out is None else out + part
    return out


@jax.custom_vjp
def _cumsum_rows(x):
    return _tri_sum(x, False)


_cumsum_rows.defvjp(lambda x: (_tri_sum(x, False), None), lambda _, ct: (_tri_sum(ct, True),))


def _rows_to_cols(x):
    H, _, K = x.shape
    eye = (lax.broadcasted_iota(jnp.int32, (H, K, K), 1) == lax.broadcasted_iota(jnp.int32, (H, K, K), 2)).astype(F32)
    out = lax.dot_general(eye, jnp.broadcast_to(x, (H, SUBLANES, K)), _DOT_DIMS["nt"],
                          precision=lax.Precision.HIGHEST, preferred_element_type=F32)
    return out[:, :, 0:1]


def _per_head(x):
    return jnp.concatenate([x[:, h * HEAD:(h + 1) * HEAD][None] for h in range(N_HEADS)], axis=0)


def _chunk_fn(st0, r, lw, k, v, a, b):
    T = r.shape[0]
    cl = _cumsum_rows(lw)
    cl_end = cl[T - 1:T, :]
    inv = jnp.exp(-cl)
    to_end = jnp.exp(cl_end - cl)
    ah, rh, bh, kh, be, ke, v3 = [_per_head(x) for x in
                                  (a * jnp.exp(cl - lw), r * jnp.exp(cl), b * inv, k * inv, b * to_end, k * to_end, v)]
    i = lax.broadcasted_iota(jnp.int32, (N_HEADS, T, T), 1)
    j = lax.broadcasted_iota(jnp.int32, (N_HEADS, T, T), 2)
    a_ab = jnp.where(i > j, _mm16(ah, bh, "nt"), 0.0)
    a_ak = jnp.where(i > j, _mm16(ah, kh, "nt"), 0.0)
    m_rb = jnp.where(i >= j, _mm16(rh, bh, "nt"), 0.0)
    m_rk = jnp.where(i >= j, _mm16(rh, kh, "nt"), 0.0)
    rhs = _mm16(ah, st0, "nn") + _mm16(a_ak, v3, "nn")
    power, solve, n = a_ab, (i == j).astype(F32) + a_ab, 1
    while 2 * n < T:
        power = _mm16(power, power, "nn")
        solve = solve + _mm16(solve, power, "nn")
        n *= 2
    sa = _mm16(solve, rhs, "nn")
    y3 = _mm16(rh, st0, "nn") + _mm16(m_rb, sa, "nn") + _mm16(m_rk, v3, "nn")
    st_end = _rows_to_cols(_per_head(jnp.exp(cl_end))) * st0 + _mm16(be, sa, "tn") + _mm16(ke, v3, "tn")
    return jnp.concatenate([y3[h] for h in range(N_HEADS)], axis=1), st_end


def _hosted_exchange(refs, n, broadcast, grid):
    if n == 0:
        return lambda: None
    start, wait = _exchange_ops(refs[:n], refs[n:2 * n], *refs[2 * n:], broadcast)
    first = functools.reduce(jnp.logical_and, [pl.program_id(a) == 0 for a in range(len(grid))])
    last = functools.reduce(jnp.logical_and, [pl.program_id(a) == g - 1 for a, g in enumerate(grid)])
    pl.when(first)(start)
    return lambda: pl.when(last)(wait)


def _cscan_fwd(r, lw, k, v, a, b, gather=()):
    S = r.shape[0]
    per_blk = CHUNK_TB // CHUNK
    n_x = len(gather)
    nblk = S // CHUNK_TB

    def body(*refs):
        r_ref, lw_ref, k_ref, v_ref, a_ref, b_ref = refs[:6]
        y_ref, ck_ref = refs[6 + n_x:8 + n_x]
        st_ref = refs[8 + 2 * n_x]
        finish = _hosted_exchange(refs[6:6 + n_x] + refs[8 + n_x:8 + 2 * n_x] + refs[9 + 2 * n_x:], n_x, True, (nblk,))

        @pl.when(pl.program_id(0) == 0)
        def _():
            st_ref[...] = jnp.zeros_like(st_ref)

        def chunk(c, carry):
            rows = pl.ds(pl.multiple_of(c * CHUNK, CHUNK), CHUNK)
            st0 = st_ref[...]
            ck_ref[c] = st0
            y, st_end = _chunk_fn(st0, r_ref[rows, :], lw_ref[rows, :], k_ref[rows, :],
                                  v_ref[rows, :], a_ref[rows, :], b_ref[rows, :])
            y_ref[rows, :] = y
            st_ref[...] = st_end
            return carry

        lax.fori_loop(0, per_blk, chunk, 0)
        finish()

    blk = _rows(CHUNK_TB, D)
    any_spec = pl.BlockSpec(memory_space=pl.ANY)
    outs = pl.pallas_call(
        body, name="scan_fwd", grid=(nblk,), in_specs=[blk] * 6 + [any_spec] * n_x,
        out_specs=[blk, pl.BlockSpec((per_blk, N_HEADS, HEAD, HEAD), lambda i: (i, 0, 0, 0))] + [any_spec] * n_x,
        out_shape=[jax.ShapeDtypeStruct((S, D), F32), jax.ShapeDtypeStruct((S // CHUNK, N_HEADS, HEAD, HEAD), F32)]
        + _exchange_shapes(gather, True),
        scratch_shapes=[pltpu.VMEM((N_HEADS, HEAD, HEAD), F32)] + (_exchange_scratch(n_x) if n_x else []),
        compiler_params=_cparams(("arbitrary",)))(r, lw, k, v, a, b, *gather)
    return outs[0], outs[1], outs[2:]


def _cscan_bwd(r, lw, k, v, a, b, ckpt, dy, scatter=()):
    S = r.shape[0]
    per_blk = CHUNK_TB // CHUNK
    nblk = S // CHUNK_TB
    n_x = len(scatter)

    def body(*refs):
        r_ref, lw_ref, k_ref, v_ref, a_ref, b_ref, ck_ref, dy_ref = refs[:8]
        out_refs = refs[8 + n_x:14 + n_x]
        ds_ref = refs[14 + 2 * n_x]
        finish = _hosted_exchange(refs[8:8 + n_x] + refs[14 + n_x:14 + 2 * n_x] + refs[15 + 2 * n_x:], n_x, False, (nblk,))

        @pl.when(pl.program_id(0) == 0)
        def _():
            ds_ref[...] = jnp.zeros_like(ds_ref)

        def chunk(cc, carry):
            c = per_blk - 1 - cc
            rows = pl.ds(pl.multiple_of(c * CHUNK, CHUNK), CHUNK)
            ins = (ck_ref[c], r_ref[rows, :], lw_ref[rows, :], k_ref[rows, :], v_ref[rows, :], a_ref[rows, :], b_ref[rows, :])
            _, vjp = jax.vjp(_chunk_fn, *ins)
            grads = vjp((dy_ref[rows, :], ds_ref[...]))
            ds_ref[...] = grads[0]
            for o_ref, g in zip(out_refs, grads[1:]):
                o_ref[rows, :] = g
            return carry

        lax.fori_loop(0, per_blk, chunk, 0)
        finish()

    blk = pl.BlockSpec((CHUNK_TB, D), lambda i: (nblk - 1 - i, 0))
    any_spec = pl.BlockSpec(memory_space=pl.ANY)
    shp = jax.ShapeDtypeStruct((S, D), F32)
    outs = pl.pallas_call(
        body, name="scan_bwd", grid=(nblk,),
        in_specs=[blk] * 6 + [pl.BlockSpec((per_blk, N_HEADS, HEAD, HEAD), lambda i: (nblk - 1 - i, 0, 0, 0)), blk]
        + [any_spec] * n_x,
        out_specs=[blk] * 6 + [any_spec] * n_x, out_shape=[shp] * 6 + _exchange_shapes(scatter, False),
        scratch_shapes=[pltpu.VMEM((N_HEADS, HEAD, HEAD), F32)] + (_exchange_scratch(n_x) if n_x else []),
        compiler_params=_cparams(("arbitrary",)))(r, lw, k, v, a, b, ckpt, dy, *scatter)
    return outs[:6], outs[6:]


def _ada_partial(c_all, w_shard):
    def body(c_ref, w_ref, o_ref):
        o_ref[...] = jnp.dot(c_ref[...].astype(BF16), w_ref[...].astype(BF16), preferred_element_type=F32)

    vm = pl.BlockSpec(memory_space=pltpu.VMEM)
    return pl.pallas_call(body, name="ada_partial", in_specs=[vm, vm], out_specs=vm,
                          out_shape=jax.ShapeDtypeStruct((N_DEV, w_shard.shape[1]), F32),
                          compiler_params=pltpu.CompilerParams(vmem_limit_bytes=VMEM_LIMIT))(c_all, w_shard)


def _ada_bias(rows, b_ada):
    def body(r_ref, b_ref, o_ref):
        o_ref[...] = r_ref[...] + b_ref[...]

    vm = pl.BlockSpec(memory_space=pltpu.VMEM)
    return pl.pallas_call(body, name="ada_bias", in_specs=[vm, vm], out_specs=vm,
                          out_shape=jax.ShapeDtypeStruct(rows.shape, F32))(rows, b_ada)


def _ada_wgrad(c_cols, d_all):
    def body(c_ref, d_ref, o_ref):
        acc = c_ref[:, 0:1] * d_ref[0:1, :]
        for j in range(1, N_DEV):
            acc = acc + c_ref[:, j:j + 1] * d_ref[j:j + 1, :]
        o_ref[...] = acc

    vm = pl.BlockSpec(memory_space=pltpu.VMEM)
    return pl.pallas_call(body, name="ada_wgrad", in_specs=[vm, vm], out_specs=vm,
                          out_shape=jax.ShapeDtypeStruct((D, d_all.shape[1]), F32),
                          compiler_params=pltpu.CompilerParams(vmem_limit_bytes=VMEM_LIMIT))(c_cols, d_all)


def _exchange(srcs, broadcast, name):
    n = len(srcs)

    def body(*refs):
        start, wait = _exchange_ops(refs[:n], refs[n:2 * n], *refs[2 * n:], broadcast)
        start()
        wait()

    any_spec = pl.BlockSpec(memory_space=pl.ANY)
    return pl.pallas_call(
        body, name=name, out_shape=_exchange_shapes(srcs, broadcast), in_specs=[any_spec] * n, out_specs=[any_spec] * n,
        scratch_shapes=_exchange_scratch(n),
        compiler_params=pltpu.CompilerParams(has_side_effects=True),
    )(*srcs)


def _gather_via_sibling(srcs, name):
    n = len(srcs)

    def body(*refs):
        src_refs, out_refs = refs[:n], refs[n:2 * n]
        send_sems, recv_sems, local_sems = refs[2 * n:]
        x, y, c = lax.axis_index("x"), lax.axis_index("y"), lax.axis_index("c")
        me, sibling = (x, y, c), (x, y, 1 - c)
        chips = [(1 - x, y), (x, 1 - y), (1 - x, 1 - y)]

        def slot(px, py, pc):
            return 4 * px + 2 * py + pc

        def copy(i, k, block, to, src=None):
            rows = out_refs[i].at[slot(*block)]
            return pltpu.make_async_remote_copy(
                src_ref=rows if src is None else src, dst_ref=rows, send_sem=send_sems.at[i, k],
                recv_sem=recv_sems.at[i, k], device_id=to, device_id_type=_MESH)

        local = [pltpu.make_async_copy(src_refs[i], out_refs[i].at[slot(*me)], local_sems.at[i]) for i in range(n)]
        for cp in local:
            cp.start()
        first = [copy(i, 0, me, sibling, src=src_refs[i]) for i in range(n)]
        first += [copy(i, 1 + j, me, (*chip, c), src=src_refs[i]) for j, chip in enumerate(chips) for i in range(n)]
        for cp in first:
            cp.start()
        passed = []
        for j, chip in enumerate(chips):
            for i in range(n):
                copy(i, 1 + j, (*chip, c), me).wait_recv()
                passed.append(copy(i, 4 + j, (*chip, c), sibling))
                passed[-1].start()
        for i in range(n):
            copy(i, 0, sibling, me).wait_recv()
            for j, chip in enumerate(chips):
                copy(i, 4 + j, (*chip, 1 - c), me).wait_recv()
        for cp in first + passed:
            cp.wait_send()
        for cp in local:
            cp.wait()

    any_spec = pl.BlockSpec(memory_space=pl.ANY)
    return pl.pallas_call(
        body, name=name, out_shape=_exchange_shapes(srcs, True), in_specs=[any_spec] * n, out_specs=[any_spec] * n,
        scratch_shapes=_exchange_scratch(n),
        compiler_params=pltpu.CompilerParams(has_side_effects=True),
    )(*srcs)


def _flags(broadcast, n):
    return [broadcast] * n if isinstance(broadcast, bool) else list(broadcast)


def _exchange_shapes(srcs, broadcast):
    return [jax.ShapeDtypeStruct((N_DEV,) + (s.shape if bc else s.shape[1:]), s.dtype)
            for s, bc in zip(srcs, _flags(broadcast, len(srcs)))]


def _exchange_scratch(n):
    return [pltpu.SemaphoreType.DMA((n, N_DEV)), pltpu.SemaphoreType.DMA((n, N_DEV)), pltpu.SemaphoreType.DMA((n,))]


def _exchange_ops(src_refs, out_refs, send_sems, recv_sems, local_sems, broadcast):
    n = len(src_refs)
    flags = _flags(broadcast, n)
    x, y, c = lax.axis_index("x"), lax.axis_index("y"), lax.axis_index("c")
    me = 4 * x + 2 * y + c

    def block(i, j):
        return src_refs[i] if flags[i] else src_refs[i].at[j]

    def remote(i, d, src_slot, dst_slot):
        px, py, pc = x ^ (d >> 2), y ^ ((d >> 1) & 1), c ^ (d & 1)
        return pltpu.make_async_remote_copy(
            src_ref=block(i, src_slot), dst_ref=out_refs[i].at[dst_slot], send_sem=send_sems.at[i, d],
            recv_sem=recv_sems.at[i, d], device_id=(px, py, pc), device_id_type=_MESH)

    def local(i):
        return pltpu.make_async_copy(block(i, me), out_refs[i].at[me], local_sems.at[i])

    def start():
        for i in range(n):
            local(i).start()
        for d in range(1, N_DEV):
            for i in range(n):
                remote(i, d, me ^ d, me).start()

    def wait():
        for d in range(1, N_DEV):
            for i in range(n):
                remote(i, d, me, me ^ d).wait_recv()
        for d in range(1, N_DEV):
            for i in range(n):
                remote(i, d, me ^ d, me).wait_send()
        for i in range(n):
            local(i).wait()

    return start, wait


def _adamw(w, g, m, v):
    nm = ADAM_B1 * m + (1.0 - ADAM_B1) * g
    nv = ADAM_B2 * v + (1.0 - ADAM_B2) * (g * g)
    m_hat = nm * (1.0 / (1.0 - ADAM_B1 ** ADAM_STEP))
    v_hat = nv * (1.0 / (1.0 - ADAM_B2 ** ADAM_STEP))
    return -ADAM_LR * (m_hat / (jnp.sqrt(v_hat) + ADAM_EPS) + ADAM_WD * w), nm, nv


def _adam_vectors(parts, ws, ms, vs):
    nv = len(ws)
    sizes = [w.shape[1] for w in ws]

    def body(*refs):
        p_ref = refs[0]
        w_refs, m_refs, v_refs = refs[1:1 + nv], refs[1 + nv:1 + 2 * nv], refs[1 + 2 * nv:1 + 3 * nv]
        out_refs = refs[1 + 3 * nv:]
        g_all = p_ref[0]
        for j in range(1, N_DEV):
            g_all = g_all + p_ref[j]
        off = 0
        for i, n in enumerate(sizes):
            g = g_all[:, off:off + n]
            off += -(-n // LANES) * LANES
            delta, new_m, new_v = _adamw(w_refs[i][...], g, m_refs[i][...], v_refs[i][...])
            for o_ref, val in zip(out_refs[4 * i:4 * i + 4], (g, delta, new_m, new_v)):
                o_ref[...] = val

    vm = pl.BlockSpec(memory_space=pltpu.VMEM)
    outs = pl.pallas_call(body, name="adam_replicated", in_specs=[vm] * (1 + 3 * nv), out_specs=[vm] * (4 * nv),
                          out_shape=[jax.ShapeDtypeStruct((1, n), F32) for n in sizes for _ in range(4)])(parts, *ws, *ms, *vs)
    return [outs[4 * i:4 * i + 4] for i in range(nv)]


def _sum_adam(parts, w, m, v, name):
    n_parts, R, C = parts.shape
    fits = [t for t in range(16, R + 1, 16) if R % t == 0 and t * C <= 2504 * LANES]
    if fits:
        tm, tc = max(fits), C
    elif C % (2 * LANES) == 0 and R * C > 2504 * LANES:
        tm, tc = R, 2 * LANES
    else:
        tm, tc = R, C

    def body(p_ref, w_ref, m_ref, v_ref, g_ref, d_ref, nm_ref, nv_ref):
        g = p_ref[0].astype(F32)
        for j in range(1, n_parts):
            g = g + p_ref[j].astype(F32)
        g_ref[...] = g
        d_ref[...], nm_ref[...], nv_ref[...] = _adamw(w_ref[...], g, m_ref[...], v_ref[...])

    blk = pl.BlockSpec((tm, tc), lambda i, j: (i, j))
    shp = jax.ShapeDtypeStruct((R, C), F32)
    return pl.pallas_call(body, name=name, grid=(R // tm, C // tc),
                          in_specs=[pl.BlockSpec((n_parts, tm, tc), lambda i, j: (0, i, j)), blk, blk, blk],
                          out_specs=[blk] * 4, out_shape=[shp] * 4,
                          compiler_params=_cparams(("parallel", "parallel")))(parts, w, m, v)


TRANSPOSED = ("w_in", "w_up")
SHARDED = (("w_ada", 1), ("w_in", 0), ("w2", 1), ("a2", 1), ("g2", 1), ("w_att_out", 1), ("w_rwkv_out", 0),
           ("w_o", 0), ("w_up", 0), ("conv_w", 1), ("w_down", 0))
EARLY, LATE = SHARDED[1:5], SHARDED[5:]
REPLICATED = ("b_ada", "norm1_w", "b_gate", "mu_shift", "w0", "a0", "k_k", "k_a", "r_k", "lnx_w", "lnx_b",
              "norm2_w", "conv_b", "norm_f_w")
WEIGHTS = ("w_ada", "b_ada", "norm1_w", "w_in", "b_gate", "mu_shift", "w0", "w2", "a0", "a2", "g2", "k_k", "k_a", "r_k",
           "lnx_w", "lnx_b", "w_att_out", "w_rwkv_out", "w_o", "norm2_w", "w_up", "conv_w", "conv_b", "w_down", "norm_f_w")


W_IN_RUNS = ((0, C_ATT, ATT_IN), (ATT_IN, C_R, 3 * D), (ATT_IN + 3 * D, C_LORA, LORA_W + LORA_A),
             (ATT_IN + 3 * D + LORA_W + LORA_A, C_LORA + LANES, LORA_G), (ATT_IN + RWKV_IN, C_GA, 2 * D))
W_IN_SHARD = N_IN // N_DEV


def _pad_w_in(w_in_t):
    pieces = [w_in_t[orig:orig + count] for orig, _, count in sorted(W_IN_RUNS, key=lambda run: run[1])]
    pieces.append(jnp.zeros((LORA_PAD - LANES - LORA_G, w_in_t.shape[1]), w_in_t.dtype))
    return jnp.concatenate(pieces, axis=0)


def _w_in_blocks(g):
    blocks = []
    for j in range(N_DEV):
        pieces = []
        for orig, pad, count in W_IN_RUNS:
            lo, hi = max(orig, j * W_IN_SHARD), min(orig + count, (j + 1) * W_IN_SHARD)
            if lo < hi:
                pieces.append(g[pad + lo - orig:pad + hi - orig])
        blocks.append(jnp.concatenate(pieces, axis=0)[None])
    return jnp.concatenate(blocks, axis=0)


def _pad_mu(mu):
    lo = mu[:, 3 * D:]
    mu_l = jnp.concatenate([lo[:, :LORA_W + LORA_A], lo[:, LORA_W + LORA_A:], jnp.zeros((1, LORA_PAD - LANES - LORA_G), mu.dtype)], axis=1)
    return mu[:, :D], mu[:, D:2 * D], mu[:, 2 * D:3 * D], mu_l


def _local_step(x, ada, W, late_shards, target):
    S = x.shape[0]
    W = dict(W)
    G = {}
    sh1, sc1, gt1, sh2, sc2, gt2 = [ada[:, i * D:(i + 1) * D] for i in range(6)]
    h1, rstd1 = _norm_fwd(x, None, None, W["norm1_w"], sc1, sh1, "norm1_fwd")
    w_in_p = _pad_w_in(W["w_in"])
    P = _mm(h1, w_in_p, "nt", F32, "proj_in")

    mu_r, mu_k, mu_v, mu_l = _pad_mu(W["mu_shift"])
    g2p = jnp.pad(W["g2"], ((0, G_PAD - LORA_G), (0, 0)))
    prep_params = [mu_r, mu_k, mu_v, mu_l, W["w0"], W["a0"], W["k_k"], W["k_a"], W["w2"], W["a2"], g2p]
    r_, dec, kmod, v_, aa, bb, gg = _rwkv_prep(P, prep_params)
    y_scan, states, late = _cscan_fwd(r_, dec, kmod, v_, aa, bb, gather=late_shards)
    W.update({n: _full_weight(g, axis) for (n, axis), g in zip(LATE, late)})

    o_g, l_g = zip(*[_att_fwd(P, g) for g in range(len(ATT_PATTERNS))])
    att = _att_combine_fwd(o_g, l_g)
    y_att = _mm(att, W["w_att_out"], "nn", F32, "att_out")
    r_k = W["r_k"].reshape(1, D)
    rw = _rwkv_post(y_scan, r_, kmod, v_, gg, W["lnx_w"], W["lnx_b"], r_k)
    y_rwkv = _mm(rw, W["w_rwkv_out"], "nn", F32, "rwkv_out")

    bga, bgr = W["b_gate"][:, :D], W["b_gate"][:, D:]
    mix = _gate_fwd(P, bga, bgr, y_att, y_rwkv)
    mo = _mm(mix, W["w_o"], "nn", F32, "mix_out")
    x2, h2, rstd2 = _norm_fwd(x, mo, gt1, W["norm2_w"], sc2, sh2, "norm2_fwd")
    u = _mm(h2, W["w_up"], "nt", BF16, "ffn_up")
    conv_w8 = jnp.pad(W["conv_w"], ((0, SUBLANES - 3), (0, 0)))
    act = _conv_fwd(u, conv_w8, W["conv_b"])
    f = _mm(act, W["w_down"], "nn", F32, "ffn_down")
    loss_blk, dx3, df, dgt2, G["norm_f_w"] = _final(x2, f, gt2, W["norm_f_w"], target)
    loss = loss_blk[0, 0]

    dact = _mm(df, W["w_down"], "nt", BF16, "ffn_down_dx")
    G["w_down"] = _mm(act, df, "tn", BF16, "ffn_down_dw")
    duc, dwg, dwv, dbg, dbv = _conv_bwd_a(dact, u, conv_w8, W["conv_b"])
    G["conv_w"] = jnp.concatenate([dwg[0:3], dwv[0:3]], axis=1)
    G["conv_b"] = jnp.concatenate([dbg, dbv], axis=1)
    du = _conv_bwd_b(duc, conv_w8)
    dh2 = _mm(du, W["w_up"], "nn", F32, "ffn_up_dx")
    G["w_up"] = _mm(du, h2, "tn", BF16, "ffn_up_dw")
    dx2, dsh2, dsc2, G["norm2_w"], dmo, dgt1 = _norm_bwd(dh2, x2, rstd2, W["norm2_w"], sc2, dx3, mo, gt1, "norm2_bwd")
    dmix = _mm(dmo, W["w_o"], "nt", F32, "mix_out_dx")
    G["w_o"] = _mm(mix, dmo, "tn", BF16, "mix_out_dw")
    dy_att, dy_rwkv, dpga, dpgr, dbga, dbgr = _gate_bwd(dmix, P, bga, bgr, y_att, y_rwkv)
    G["b_gate"] = jnp.concatenate([dbga, dbgr], axis=1)

    datt = _mm(dy_att, W["w_att_out"], "nt", F32, "att_out_dx")
    G["w_att_out"] = _mm(att, dy_att, "tn", BF16, "att_out_dw")
    dcomb = _att_combine_bwd(datt, o_g, l_g)
    dp_att = []
    for g in range(len(ATT_PATTERNS)):
        dp_att += _att_bwd(P, o_g[g], l_g[g], dcomb[g], dcomb[3 + g], g)

    drw = _mm(dy_rwkv, W["w_rwkv_out"], "nt", F32, "rwkv_out_dx")
    G["w_rwkv_out"] = _mm(rw, dy_rwkv, "tn", BF16, "rwkv_out_dw")
    dy_scan, dr1, dk1, dv1, dgg, G["lnx_w"], G["lnx_b"], drk = _rwkv_post_bwd(drw, y_scan, r_, kmod, v_, gg, W["lnx_w"], W["lnx_b"], r_k)
    G["r_k"] = drk.reshape(W["r_k"].shape)
    late_blocks = [_owner_blocks(G[n], axis) for n, axis in LATE] if late_shards else []
    (dr2, ddec, dk2, dv2, daa, dbb), late_parts = _cscan_bwd(r_, dec, kmod, v_, aa, bb, states, dy_scan, scatter=late_blocks)
    pb = _rwkv_prep_bwd(P, prep_params, [dr2, ddec, dk2, dv2, daa, dbb, dgg], [dr1, None, dk1, dv1, None, None, None])
    dp_rkv, dp_lora, dpar = list(pb[0:3]), pb[3], pb[4:]
    dmu_r, dmu_k, dmu_v, dmu_l, G["w0"], G["a0"], G["k_k"], G["k_a"], G["w2"], G["a2"], dg2p = dpar
    G["g2"] = dg2p[0:LORA_G]
    G["mu_shift"] = jnp.concatenate([dmu_r, dmu_k, dmu_v, dmu_l[:, :LORA_W + LORA_A], dmu_l[:, LANES:LANES + LORA_G]], axis=1)

    dP = jnp.concatenate(dp_rkv + [dpga, dpgr] + dp_att + [dp_lora], axis=1)
    G["w_in"] = _w_in_blocks(_mm(dP, h1, "tn", BF16, "proj_in_dw"))
    if late_shards:
        dh1, (w_in_parts,) = _mm(dP, w_in_p, "nn", F32, "proj_in_dx", scatter=[G["w_in"]])
        done = dict(zip([n for n, _ in LATE] + ["w_in"], list(late_parts) + [w_in_parts]))
    else:
        dh1, done = _mm(dP, w_in_p, "nn", F32, "proj_in_dx"), {}
    grad_x, dsh1, dsc1, G["norm1_w"] = _norm_bwd(dh1, x, rstd1, W["norm1_w"], sc1, dx2, None, None, "norm1_bwd")
    dada = jnp.concatenate([dsh1, dsc1, dgt1, dsh2, dsc2, dgt2], axis=1)
    G["b_ada"] = dada
    return loss, grad_x, G, done


def _full_weight(gathered, axis):
    _, rows, cols = gathered.shape
    if axis == 0:
        return gathered.reshape(N_DEV * rows, cols)
    return gathered.transpose(1, 0, 2).reshape(rows, N_DEV * cols)


def _owner_blocks(g, axis):
    rows, cols = g.shape
    g = g.astype(BF16)
    if axis == 0:
        return g.reshape(N_DEV, rows // N_DEV, cols)
    return g.reshape(rows, N_DEV, cols // N_DEV).transpose(1, 0, 2)


def kernel(x, c, w_ada, b_ada, norm1_w, w_in, b_gate, mu_shift, w0, w2, a0, a2, g2, k_k, k_a, r_k, lnx_w, lnx_b, w_att_out, w_rwkv_out, w_o, norm2_w, w_up, conv_w, conv_b, w_down, norm_f_w, loss_target, m_w_ada, m_b_ada, m_norm1_w, m_w_in, m_b_gate, m_mu_shift, m_w0, m_w2, m_a0, m_a2, m_g2, m_k_k, m_k_a, m_r_k, m_lnx_w, m_lnx_b, m_w_att_out, m_w_rwkv_out, m_w_o, m_norm2_w, m_w_up, m_conv_w, m_conv_b, m_w_down, m_norm_f_w, v_w_ada, v_b_ada, v_norm1_w, v_w_in, v_b_gate, v_mu_shift, v_w0, v_w2, v_a0, v_a2, v_g2, v_k_k, v_k_a, v_r_k, v_lnx_w, v_lnx_b, v_w_att_out, v_w_rwkv_out, v_w_o, v_norm2_w, v_w_up, v_conv_w, v_conv_b, v_w_down, v_norm_f_w):
    env = dict(locals())
    w_shard = {n: env[n] for n in WEIGHTS}
    m_shard = {n: env["m_" + n] for n in WEIGHTS}
    v_shard = {n: env["v_" + n] for n in WEIGHTS}

    def mat(shards, n):
        return jnp.swapaxes(shards[n][0], 0, 1) if n in TRANSPOSED else shards[n][0]

    c_all, *gathered = _gather_via_sibling([c] + [mat(w_shard, n).astype(BF16) for n, _ in EARLY], "gather_weights")
    c_all = c_all.reshape(N_DEV, D)
    W = {n: _full_weight(g, axis) for (n, axis), g in zip(EARLY, gathered)}
    for n in REPLICATED:
        W[n] = w_shard[n].reshape(1, -1) if n != "r_k" else w_shard[n][0]
    ada_cols = _ada_partial(c_all, w_shard["w_ada"][0])
    ada_rows, = _exchange([ada_cols[:, None, :]], False, "ada_rows")
    ada = _ada_bias(ada_rows.reshape(1, -1), w_shard["b_ada"])

    late_shards = [mat(w_shard, n).astype(BF16) for n, _ in LATE]
    loss, grad_x, G, parts = _local_step(x[0], ada, W, late_shards, loss_target[0])
    loss = lax.psum(loss, ("x", "y", "c"))

    row = lambda a: a.reshape(1, -1)
    small = jnp.concatenate([jnp.pad(row(G[n]), ((0, 0), (0, (-G[n].size) % LANES))) for n in REPLICATED], axis=1)
    rest = [(n, axis) for n, axis in SHARDED if n not in parts and n != "w_ada"]
    sparts, dada_all, *rest_parts = _exchange(
        [small, G["b_ada"].reshape(N_DEV, 1, -1)] + [_owner_blocks(G[n], axis) for n, axis in rest],
        [True] + [False] * (1 + len(rest)), "last_exchange")
    parts["w_ada"] = _ada_wgrad(c_all.T, dada_all.reshape(N_DEV, -1))[None]
    parts.update(zip([n for n, _ in rest], rest_parts))

    out = {}
    for n, p in parts.items():
        res = _sum_adam(p, mat(w_shard, n), mat(m_shard, n), mat(v_shard, n), "adam_" + n)
        if n in TRANSPOSED:
            res = [jnp.swapaxes(a, 0, 1) for a in res]
        for kind, a in zip(("grad", "delta", "new_m", "new_v"), res):
            out[kind, n] = a[None]

    res = _adam_vectors(sparts, *[[row(s[n]) for n in REPLICATED] for s in (w_shard, m_shard, v_shard)])
    for n, four in zip(REPLICATED, res):
        for kind, a in zip(("grad", "delta", "new_m", "new_v"), four):
            out[kind, n] = a.reshape(w_shard[n].shape)

    return (loss, grad_x[None], *[out[kind, n] for kind in ("grad", "delta", "new_m", "new_v") for n in WEIGHTS])
```

```python
import functools

import jax
import jax.numpy as jnp
from jax import lax
from jax.experimental import pallas as pl
from jax.experimental.pallas import tpu as pltpu

F32 = jnp.float32
BF16 = jnp.bfloat16

D = 1024
HEAD = 64
ATT_PATTERNS = ((128, 1), (512, 4), (2048, 16))
ATT_HEADS = 8
ATT_W = ATT_HEADS * HEAD
ATT_IN = 3 * 3 * ATT_W
QBLK = 128
N_HEADS = D // HEAD
LORA_W, LORA_A, LORA_G = 64, 64, 160
RWKV_IN = 3 * D + LORA_W + LORA_A + LORA_G
N_IN = ATT_IN + RWKV_IN + 2 * D
D_FF = 2816
RMS_EPS = 1e-6
GN_EPS = 64e-5
N_DEV = 8
LANES = 128
SUBLANES = 8

C_R, C_K, C_V, C_GA, C_GR = 0, 1024, 2048, 3072, 4096
C_ATT = 5120
C_LORA = C_ATT + ATT_IN
LORA_PAD = 512
G_PAD = 256
N_PAD = C_LORA + LORA_PAD

ADAM_LR, ADAM_B1, ADAM_B2, ADAM_EPS, ADAM_WD, ADAM_STEP = 0.001, 0.9, 0.999, 1e-08, 0.01, 10

VMEM_LIMIT = 56 * 1024 * 1024

_MESH = pl.DeviceIdType.MESH


def _cparams(sem):
    return pltpu.CompilerParams(dimension_semantics=sem, vmem_limit_bytes=VMEM_LIMIT)


def _tile(dim, pref):
    if dim <= pref:
        return dim
    best = None
    for t in range(LANES, pref + 1, LANES):
        if dim % t == 0:
            best = t
    assert best is not None, dim
    return best


MM_TILES = {"nn": (1024, 1408, 1408), "nt": (1024, 2048, 1408), "tn": (1408, 1408, 1024)}


def _mm(a, b, mode, out_dtype, name, scatter=()):
    if mode == "nn":
        (M, K), (K2, N) = a.shape, b.shape
    elif mode == "nt":
        (M, K), (N, K2) = a.shape, b.shape
    else:
        (K, M), (K2, N) = a.shape, b.shape
    assert K == K2, (a.shape, b.shape, mode)
    tm, tn, tk = (_tile(dim, pref) for dim, pref in zip((M, N, K), MM_TILES[mode]))
    nk = K // tk
    grid = (M // tm, N // tn, nk)
    n_x = len(scatter)
    dims = {"nn": (((1,), (0,)), ((), ())), "nt": (((1,), (1,)), ((), ())), "tn": (((0,), (0,)), ((), ()))}[mode]

    def body(*refs):
        a_ref, b_ref = refs[:2]
        o_ref, acc_ref = refs[2 + n_x], refs[3 + 2 * n_x]
        finish = _hosted_exchange(refs[2:2 + n_x] + refs[3 + n_x:3 + 2 * n_x] + refs[4 + 2 * n_x:], n_x, False, grid)
        k = pl.program_id(2)
        part = lax.dot_general(a_ref[...].astype(BF16), b_ref[...].astype(BF16), dims,
                               preferred_element_type=F32)
        if nk == 1:
            o_ref[...] = part.astype(o_ref.dtype)
        else:
            @pl.when(k == 0)
            def _():
                acc_ref[...] = part

            @pl.when(jnp.logical_and(k > 0, k < nk - 1))
            def _():
                acc_ref[...] += part

            @pl.when(k == nk - 1)
            def _():
                o_ref[...] = (acc_ref[...] + part).astype(o_ref.dtype)
        finish()

    a_spec = pl.BlockSpec((tk, tm), lambda i, j, k: (k, i)) if mode == "tn" else pl.BlockSpec((tm, tk), lambda i, j, k: (i, k))
    b_spec = pl.BlockSpec((tn, tk), lambda i, j, k: (j, k)) if mode == "nt" else pl.BlockSpec((tk, tn), lambda i, j, k: (k, j))
    any_spec = pl.BlockSpec(memory_space=pl.ANY)
    outs = pl.pallas_call(
        body, name=name, grid=grid,
        in_specs=[a_spec, b_spec] + [any_spec] * n_x,
        out_specs=[pl.BlockSpec((tm, tn), lambda i, j, k: (i, j))] + [any_spec] * n_x,
        out_shape=[jax.ShapeDtypeStruct((M, N), out_dtype)] + _exchange_shapes(scatter, False),
        scratch_shapes=[pltpu.VMEM((tm, tn) if nk > 1 else (SUBLANES, LANES), F32)] + (_exchange_scratch(n_x) if n_x else []),
        compiler_params=_cparams(("arbitrary",) * 3 if n_x else ("parallel", "parallel", "arbitrary")),
    )(a, b, *scatter)
    return (outs[0], outs[1:]) if n_x else outs[0]


def _rows(tm, w, col=0):
    return pl.BlockSpec((tm, w), lambda i: (i, col))


def _full(shape):
    return pl.BlockSpec(shape, lambda i: (0,) * len(shape))


def _shift_down(x, halo, k, first):
    rolled = pltpu.roll(x, k, 0)
    row = lax.broadcasted_iota(jnp.int32, x.shape, 0)
    out = rolled
    n_halo = halo.shape[0]
    for j in range(k):
        h = jnp.where(first, 0.0, halo[n_halo - k + j:n_halo - k + j + 1, :])
        out = jnp.where(row == j, h, out)
    return out


def _shift_up(x, halo, k, last):
    n = x.shape[0]
    rolled = pltpu.roll(x, n - k, 0)
    row = lax.broadcasted_iota(jnp.int32, x.shape, 0)
    out = rolled
    for j in range(k):
        h = jnp.where(last, 0.0, halo[j:j + 1, :])
        out = jnp.where(row == n - k + j, h, out)
    return out


def _acc(ref, val, first):
    @pl.when(first)
    def _():
        ref[...] = val

    @pl.when(jnp.logical_not(first))
    def _():
        ref[...] += val


def _colsum(x):
    return jnp.sum(x, axis=0, keepdims=True)


def _norm_fwd(x, mo, gt, nw, sc, sh, name, tm=256):
    S = x.shape[0]
    has_res = mo is not None

    def body(*refs):
        if has_res:
            x_ref, mo_ref, gt_ref, nw_ref, sc_ref, sh_ref, x2_ref, h_ref, rs_ref = refs
            x2 = x_ref[...] + gt_ref[...] * mo_ref[...]
            x2_ref[...] = x2
        else:
            x_ref, nw_ref, sc_ref, sh_ref, h_ref, rs_ref = refs
            x2 = x_ref[...]
        rstd = lax.rsqrt(jnp.mean(x2 * x2, axis=-1, keepdims=True) + RMS_EPS)
        rs_ref[...] = rstd
        h_ref[...] = ((x2 * rstd * nw_ref[...]) * (1.0 + sc_ref[...]) + sh_ref[...]).astype(BF16)

    vec = _full((1, D))
    ins = [x, mo, gt, nw, sc, sh] if has_res else [x, nw, sc, sh]
    in_specs = [_rows(tm, D), _rows(tm, D), vec, vec, vec, vec] if has_res else [_rows(tm, D), vec, vec, vec]
    outs = [jax.ShapeDtypeStruct((S, D), BF16), jax.ShapeDtypeStruct((S, 1), F32)]
    out_specs = [_rows(tm, D), _rows(tm, 1)]
    if has_res:
        outs = [jax.ShapeDtypeStruct((S, D), F32)] + outs
        out_specs = [_rows(tm, D)] + out_specs
    return pl.pallas_call(body, name=name, grid=(S // tm,), in_specs=in_specs, out_specs=out_specs,
                          out_shape=outs, compiler_params=_cparams(("parallel",)))(*ins)


def _norm_bwd(dh, xin, rstd, nw, sc, dres, mo, gt, name, tm=256):
    S = xin.shape[0]
    has_res = mo is not None

    def body(*refs):
        if has_res:
            dh_ref, x_ref, rs_ref, nw_ref, sc_ref, dres_ref, mo_ref, gt_ref, dx_ref, dsh_ref, dsc_ref, dnw_ref, dmo_ref, dgt_ref = refs
        else:
            dh_ref, x_ref, rs_ref, nw_ref, sc_ref, dres_ref, dx_ref, dsh_ref, dsc_ref, dnw_ref = refs
        first = pl.program_id(0) == 0
        dh = dh_ref[...]
        rstd = rs_ref[...]
        n = x_ref[...] * rstd
        w = nw_ref[...]
        _acc(dsh_ref, _colsum(dh), first)
        _acc(dsc_ref, _colsum(dh * (n * w)), first)
        dnw = dh * (1.0 + sc_ref[...])
        _acc(dnw_ref, _colsum(dnw * n), first)
        dn = dnw * w
        dx = dres_ref[...] + rstd * (dn - n * jnp.mean(dn * n, axis=-1, keepdims=True))
        dx_ref[...] = dx
        if has_res:
            dmo_ref[...] = (dx * gt_ref[...]).astype(BF16)
            _acc(dgt_ref, _colsum(dx * mo_ref[...]), first)

    vec = _full((1, D))
    vshape = jax.ShapeDtypeStruct((1, D), F32)
    ins = [dh, xin, rstd, nw, sc, dres] + ([mo, gt] if has_res else [])
    in_specs = [_rows(tm, D), _rows(tm, D), _rows(tm, 1), vec, vec, _rows(tm, D)] + ([_rows(tm, D), vec] if has_res else [])
    outs = [jax.ShapeDtypeStruct((S, D), F32), vshape, vshape, vshape]
    out_specs = [_rows(tm, D), vec, vec, vec]
    if has_res:
        outs += [jax.ShapeDtypeStruct((S, D), BF16), vshape]
        out_specs += [_rows(tm, D), vec]
    return pl.pallas_call(body, name=name, grid=(S // tm,), in_specs=in_specs, out_specs=out_specs,
                          out_shape=outs, compiler_params=_cparams(("arbitrary",)))(*ins)


def _final(x2, f, gt2, nfw, target, tm=256):
    S = x2.shape[0]

    def body(x2_ref, f_ref, gt_ref, w_ref, t_ref, loss_ref, dx_ref, df_ref, dgt_ref, dw_ref):
        first = pl.program_id(0) == 0
        f = f_ref[...]
        gt = gt_ref[...]
        w = w_ref[...]
        x3 = x2_ref[...] + gt * f
        rstd = lax.rsqrt(jnp.mean(x3 * x3, axis=-1, keepdims=True) + RMS_EPS)
        n = x3 * rstd
        e = n * w - t_ref[...]
        part = 0.5 * jnp.sum(jnp.mean(e * e, axis=-1, keepdims=True), axis=0, keepdims=True)
        _acc(loss_ref, jnp.broadcast_to(part, (SUBLANES, LANES)), first)
        dy = e * (1.0 / D)
        _acc(dw_ref, _colsum(dy * n), first)
        dn = dy * w
        dx = rstd * (dn - n * jnp.mean(dn * n, axis=-1, keepdims=True))
        dx_ref[...] = dx
        df_ref[...] = (dx * gt).astype(BF16)
        _acc(dgt_ref, _colsum(dx * f), first)

    vec = _full((1, D))
    vshape = jax.ShapeDtypeStruct((1, D), F32)
    return pl.pallas_call(
        body, name="final_loss", grid=(S // tm,),
        in_specs=[_rows(tm, D), _rows(tm, D), vec, vec, _rows(tm, D)],
        out_specs=[_full((SUBLANES, LANES)), _rows(tm, D), _rows(tm, D), vec, vec],
        out_shape=[jax.ShapeDtypeStruct((SUBLANES, LANES), F32), jax.ShapeDtypeStruct((S, D), F32),
                   jax.ShapeDtypeStruct((S, D), BF16), vshape, vshape],
        compiler_params=_cparams(("arbitrary",)))(x2, f, gt2, nfw, target)


def _gate_fwd(P, bga, bgr, y_att, y_rwkv, tm=256):
    S = P.shape[0]

    def body(pa_ref, pr_ref, ba_ref, br_ref, ya_ref, yr_ref, mix_ref):
        ga = jax.nn.sigmoid(pa_ref[...] + ba_ref[...])
        gr = jax.nn.sigmoid(pr_ref[...] + br_ref[...])
        mix_ref[...] = (ga * ya_ref[...] + gr * yr_ref[...]).astype(BF16)

    vec = _full((1, D))
    return pl.pallas_call(
        body, name="gate_fwd", grid=(S // tm,),
        in_specs=[_rows(tm, D, C_GA // D), _rows(tm, D, C_GR // D), vec, vec, _rows(tm, D), _rows(tm, D)],
        out_specs=_rows(tm, D), out_shape=jax.ShapeDtypeStruct((S, D), BF16),
        compiler_params=_cparams(("parallel",)))(P, P, bga, bgr, y_att, y_rwkv)


def _gate_bwd(dmix, P, bga, bgr, y_att, y_rwkv, tm=256):
    S = P.shape[0]

    def body(dm_ref, pa_ref, pr_ref, ba_ref, br_ref, ya_ref, yr_ref, dya_ref, dyr_ref, dpa_ref, dpr_ref, dba_ref, dbr_ref):
        first = pl.program_id(0) == 0
        dm = dm_ref[...]
        ga = jax.nn.sigmoid(pa_ref[...] + ba_ref[...])
        gr = jax.nn.sigmoid(pr_ref[...] + br_ref[...])
        dya_ref[...] = (dm * ga).astype(BF16)
        dyr_ref[...] = (dm * gr).astype(BF16)
        dpa = dm * ya_ref[...] * ga * (1.0 - ga)
        dpr = dm * yr_ref[...] * gr * (1.0 - gr)
        dpa_ref[...] = dpa.astype(BF16)
        dpr_ref[...] = dpr.astype(BF16)
        _acc(dba_ref, _colsum(dpa), first)
        _acc(dbr_ref, _colsum(dpr), first)

    vec = _full((1, D))
    row = _rows(tm, D)
    rshape = jax.ShapeDtypeStruct((S, D), BF16)
    vshape = jax.ShapeDtypeStruct((1, D), F32)
    return pl.pallas_call(
        body, name="gate_bwd", grid=(S // tm,),
        in_specs=[row, _rows(tm, D, C_GA // D), _rows(tm, D, C_GR // D), vec, vec, row, row],
        out_specs=[row, row, row, row, vec, vec],
        out_shape=[rshape, rshape, rshape, rshape, vshape, vshape],
        compiler_params=_cparams(("arbitrary",)))(dmix, P, P, bga, bgr, y_att, y_rwkv)


CONV_TN = D_FF // 2
HALO = 16


def _conv_fwd(u, conv_w8, conv_b, tm=256, tn=CONV_TN):
    S = u.shape[0]
    nj = D_FF // tn

    def conv(u_ref, h_ref, w_ref, b_ref, first):
        u = u_ref[...].astype(F32)
        h = h_ref[...].astype(F32)
        w = w_ref[...]
        return b_ref[...] + w[0:1] * _shift_down(u, h, 2, first) + w[1:2] * _shift_down(u, h, 1, first) + w[2:3] * u

    def body(ug_ref, hg_ref, uv_ref, hv_ref, wg_ref, wv_ref, bg_ref, bv_ref, act_ref):
        first = pl.program_id(0) == 0
        g = conv(ug_ref, hg_ref, wg_ref, bg_ref, first)
        v = conv(uv_ref, hv_ref, wv_ref, bv_ref, first)
        act_ref[...] = (g * jax.nn.sigmoid(g) * v).astype(BF16)

    blk = lambda off: pl.BlockSpec((tm, tn), lambda i, j: (i, j + off))
    halo = lambda off: pl.BlockSpec((HALO, tn), lambda i, j: (jnp.maximum(i * (tm // HALO) - 1, 0), j + off))
    wsp = lambda off: pl.BlockSpec((SUBLANES, tn), lambda i, j: (0, j + off))
    bsp = lambda off: pl.BlockSpec((1, tn), lambda i, j: (0, j + off))
    return pl.pallas_call(
        body, name="conv_fwd", grid=(S // tm, nj),
        in_specs=[blk(0), halo(0), blk(nj), halo(nj), wsp(0), wsp(nj), bsp(0), bsp(nj)],
        out_specs=pl.BlockSpec((tm, tn), lambda i, j: (i, j)),
        out_shape=jax.ShapeDtypeStruct((S, D_FF), BF16),
        compiler_params=_cparams(("parallel", "parallel")))(u, u, u, u, conv_w8, conv_w8, conv_b, conv_b)


def _conv_bwd_a(dact, u, conv_w8, conv_b, tm=256, tn=CONV_TN):
    S = u.shape[0]
    nj = D_FF // tn

    def half(u_ref, h_ref, w_ref, b_ref, first):
        u = u_ref[...].astype(F32)
        h = h_ref[...].astype(F32)
        w = w_ref[...]
        u2, u1 = _shift_down(u, h, 2, first), _shift_down(u, h, 1, first)
        return b_ref[...] + w[0:1] * u2 + w[1:2] * u1 + w[2:3] * u, (u2, u1, u)

    def wgrad(d, taps):
        z = jnp.zeros((SUBLANES - 3, d.shape[1]), F32)
        return jnp.concatenate([_colsum(d * taps[0]), _colsum(d * taps[1]), _colsum(d * taps[2]), z], axis=0)

    def body(da_ref, ug_ref, hg_ref, uv_ref, hv_ref, wg_ref, wv_ref, bg_ref, bv_ref,
             d_ref, dwg_ref, dwv_ref, dbg_ref, dbv_ref):
        first = pl.program_id(1) == 0
        g, tg = half(ug_ref, hg_ref, wg_ref, bg_ref, first)
        v, tv = half(uv_ref, hv_ref, wv_ref, bv_ref, first)
        da = da_ref[...].astype(F32)
        sg = jax.nn.sigmoid(g)
        dg = da * v * (sg * (1.0 + g * (1.0 - sg)))
        dv = da * (g * sg)
        d_ref[0] = dg.astype(BF16)
        d_ref[1] = dv.astype(BF16)
        _acc(dwg_ref, wgrad(dg, tg), first)
        _acc(dwv_ref, wgrad(dv, tv), first)
        _acc(dbg_ref, _colsum(dg), first)
        _acc(dbv_ref, _colsum(dv), first)

    blk = lambda off: pl.BlockSpec((tm, tn), lambda j, i: (i, j + off))
    halo = lambda off: pl.BlockSpec((HALO, tn), lambda j, i: (jnp.maximum(i * (tm // HALO) - 1, 0), j + off))
    wsp = lambda off: pl.BlockSpec((SUBLANES, tn), lambda j, i: (0, j + off))
    bsp = lambda off: pl.BlockSpec((1, tn), lambda j, i: (0, j + off))
    f = jax.ShapeDtypeStruct
    outs = pl.pallas_call(
        body, name="conv_bwd_a", grid=(nj, S // tm),
        in_specs=[pl.BlockSpec((tm, tn), lambda j, i: (i, j)), blk(0), halo(0), blk(nj), halo(nj), wsp(0), wsp(nj), bsp(0), bsp(nj)],
        out_specs=[pl.BlockSpec((2, tm, tn), lambda j, i: (0, i, j)),
                   pl.BlockSpec((SUBLANES, tn), lambda j, i: (0, j)), pl.BlockSpec((SUBLANES, tn), lambda j, i: (0, j)),
                   pl.BlockSpec((1, tn), lambda j, i: (0, j)), pl.BlockSpec((1, tn), lambda j, i: (0, j))],
        out_shape=[f((2, S, D_FF), BF16), f((SUBLANES, D_FF), F32), f((SUBLANES, D_FF), F32),
                   f((1, D_FF), F32), f((1, D_FF), F32)],
        compiler_params=_cparams(("parallel", "arbitrary")))(dact, u, u, u, u, conv_w8, conv_w8, conv_b, conv_b)
    return outs


def _conv_bwd_b(duc, conv_w8, tm=256, tn=CONV_TN):
    _, S, W = duc.shape
    nj = W // tn
    n_rows = S // tm

    def body(d_ref, h_ref, w_ref, o_ref):
        last = pl.program_id(0) == n_rows - 1
        d = d_ref[...].astype(F32)
        h = h_ref[...].astype(F32)
        w = w_ref[...]
        o_ref[...] = (w[2:3] * d + w[1:2] * _shift_up(d, h, 1, last) + w[0:1] * _shift_up(d, h, 2, last)).astype(BF16)

    last_tile = S // HALO - 1
    return pl.pallas_call(
        body, name="conv_bwd_b", grid=(n_rows, 2 * nj),
        in_specs=[pl.BlockSpec((None, tm, tn), lambda i, j: (j // nj, i, j % nj)),
                  pl.BlockSpec((None, HALO, tn), lambda i, j: (j // nj, jnp.minimum((i + 1) * (tm // HALO), last_tile), j % nj)),
                  pl.BlockSpec((SUBLANES, tn), lambda i, j: (0, j))],
        out_specs=pl.BlockSpec((tm, tn), lambda i, j: (i, j)),
        out_shape=jax.ShapeDtypeStruct((S, 2 * W), BF16),
        compiler_params=_cparams(("parallel", "parallel")))(duc, duc, conv_w8)


ATT_SCALE = HEAD ** -0.5
NEG = -1e30
ATT_PAIRS = ATT_HEADS // 2


def _att_rows(n, d, S):
    per = S // (QBLK * d)
    r, m = n // per, n % per
    cur = pl.ds(m * (QBLK * d) + r, QBLK, stride=d)
    prv = pl.ds(jnp.maximum(m - 1, 0) * (QBLK * d) + r, QBLK, stride=d)
    return cur, prv, m > 0


def _att_slab(g, j):
    return (C_ATT + g * 3 * ATT_W + j * ATT_W) // LANES


def _heads(x):
    return x[:, 0:HEAD], x[:, HEAD:2 * HEAD]


ATT_NB = 4


def _stack(tiles):
    return jnp.concatenate([t[None] for t in tiles], axis=0)


def _att_operands(i, d, S, *sources):
    rows, has = [], []
    tiles = [[] for _ in sources]
    for bb in range(ATT_NB):
        cur, prv, has_prev = _att_rows(i * ATT_NB + bb, d, S)
        rows.append((cur, prv))
        has.append(has_prev)
        for t, (ref, use_cur) in zip(tiles, sources):
            t += _heads(ref[cur if use_cur else prv, :].astype(BF16))
    return rows, has, [_stack(t) for t in tiles]


def _att_mask(s_c, s_p, has_prev):
    qi = lax.broadcasted_iota(jnp.int32, (QBLK, QBLK), 0)
    kj = lax.broadcasted_iota(jnp.int32, (QBLK, QBLK), 1)
    s_c = jnp.where(kj <= qi, s_c * ATT_SCALE, NEG)
    s_p = jnp.where(jnp.logical_and(kj >= qi, has_prev), s_p * ATT_SCALE, NEG)
    return s_c, s_p


def _att_fwd(P, g):
    S = P.shape[0]
    d = ATT_PATTERNS[g][1]

    def body(q_ref, k_ref, v_ref, o_ref, l_ref):
        def group(i, carry):
            rows, has, (q, kc, kp, vc, vp) = _att_operands(i, d, S, (q_ref, True), (k_ref, True), (k_ref, False),
                                                           (v_ref, True), (v_ref, False))
            s_c_all, s_p_all = _dot16(q, kc, "nt"), _dot16(q, kp, "nt")
            p_c, p_p, den, lse = [], [], [], []
            for e in range(2 * ATT_NB):
                s_c, s_p = _att_mask(s_c_all[e], s_p_all[e], has[e // 2])
                m = jnp.maximum(jnp.max(s_c, axis=1, keepdims=True), jnp.max(s_p, axis=1, keepdims=True))
                pc, pp = jnp.exp(s_c - m), jnp.exp(s_p - m)
                den.append(jnp.sum(pc, axis=1, keepdims=True) + jnp.sum(pp, axis=1, keepdims=True))
                lse.append(jnp.broadcast_to(m + jnp.log(den[e]), (QBLK, HEAD)))
                p_c.append(pc)
                p_p.append(pp)
            num = _dot16(_stack(p_c), vc, "nn") + _dot16(_stack(p_p), vp, "nn")
            for bb, (cur, _) in enumerate(rows):
                o_ref[cur, :] = jnp.concatenate([num[2 * bb] / den[2 * bb], num[2 * bb + 1] / den[2 * bb + 1]], axis=1)
                l_ref[cur, :] = jnp.concatenate(lse[2 * bb:2 * bb + 2], axis=1)
            return carry

        lax.fori_loop(0, S // QBLK // ATT_NB, group, 0)

    slab = lambda j: pl.BlockSpec((S, LANES), lambda i: (0, _att_slab(g, j) + i))
    out = pl.BlockSpec((S, LANES), lambda i: (0, i))
    shp = jax.ShapeDtypeStruct((S, ATT_W), F32)
    return pl.pallas_call(body, name=f"att_fwd_g{g}", grid=(ATT_PAIRS,), in_specs=[slab(0), slab(1), slab(2)],
                          out_specs=[out, out], out_shape=[shp, shp], compiler_params=_cparams(("parallel",)))(P, P, P)


def _att_bwd(P, o, l, do, dl, g):
    S = P.shape[0]
    d = ATT_PATTERNS[g][1]

    def body(q_ref, k_ref, v_ref, o_ref, l_ref, do_ref, dl_ref, dq_ref, dk_ref, dv_ref, dq_acc, dk_acc, dv_acc):
        dk_acc[...] = jnp.zeros_like(dk_acc)
        dv_acc[...] = jnp.zeros_like(dv_acc)

        def group(i, carry):
            rows, has, (q, kc, kp, vc, vp, dob) = _att_operands(
                i, d, S, (q_ref, True), (k_ref, True), (k_ref, False), (v_ref, True), (v_ref, False), (do_ref, True))
            s_c_all, s_p_all = _dot16(q, kc, "nt"), _dot16(q, kp, "nt")
            dp_c_all, dp_p_all = _dot16(dob, vc, "nt"), _dot16(dob, vp, "nt")
            p_c, p_p, ds_c, ds_p = [], [], [], []
            for bb, (cur, _) in enumerate(rows):
                dd2 = do_ref[cur, :] * o_ref[cur, :] - dl_ref[cur, :]
                for h, (dd, lse) in enumerate(zip(_heads(dd2), _heads(l_ref[cur, :]))):
                    e = 2 * bb + h
                    s_c, s_p = _att_mask(s_c_all[e], s_p_all[e], has[bb])
                    pc, pp = jnp.exp(s_c - lse[:, 0:1]), jnp.exp(s_p - lse[:, 0:1])
                    delta = jnp.sum(dd, axis=1, keepdims=True)
                    p_c.append(pc)
                    p_p.append(pp)
                    ds_c.append(pc * (dp_c_all[e] - delta) * ATT_SCALE)
                    ds_p.append(pp * (dp_p_all[e] - delta) * ATT_SCALE)
            p_c, p_p, ds_c, ds_p = map(_stack, (p_c, p_p, ds_c, ds_p))
            dq = _dot16(ds_c, kc, "nn") + _dot16(ds_p, kp, "nn")
            dk_c, dk_p = _dot16(ds_c, q, "tn"), _dot16(ds_p, q, "tn")
            dv_c, dv_p = _dot16(p_c, dob, "tn"), _dot16(p_p, dob, "tn")
            pair = lambda x, bb: jnp.concatenate([x[2 * bb], x[2 * bb + 1]], axis=1)
            for bb, (cur, prv) in enumerate(rows):
                dq_acc[cur, :] = pair(dq, bb)
                dk_acc[cur, :] += pair(dk_c, bb)
                dv_acc[cur, :] += pair(dv_c, bb)
                dk_acc[prv, :] += pair(dk_p, bb)
                dv_acc[prv, :] += pair(dv_p, bb)
            return carry

        lax.fori_loop(0, S // QBLK // ATT_NB, group, 0)
        dq_ref[...] = dq_acc[...].astype(BF16)
        dk_ref[...] = dk_acc[...].astype(BF16)
        dv_ref[...] = dv_acc[...].astype(BF16)

    slab = lambda j: pl.BlockSpec((S, LANES), lambda i: (0, _att_slab(g, j) + i))
    blk128 = pl.BlockSpec((S, LANES), lambda i: (0, i))
    shp = jax.ShapeDtypeStruct((S, ATT_W), BF16)
    return pl.pallas_call(body, name=f"att_bwd_g{g}", grid=(ATT_PAIRS,),
                          in_specs=[slab(0), slab(1), slab(2)] + [blk128] * 4, out_specs=[blk128] * 3, out_shape=[shp] * 3,
                          scratch_shapes=[pltpu.VMEM((S, LANES), F32)] * 3,
                          compiler_params=_cparams(("parallel",)))(P, P, P, o, l, do, dl)


def _att_weights(l_refs):
    l0, l1, l2 = [r[...] for r in l_refs]
    m = jnp.maximum(jnp.maximum(l0, l1), l2)
    e = (jnp.exp(l0 - m), jnp.exp(l1 - m), jnp.exp(l2 - m))
    inv = 1.0 / (e[0] + e[1] + e[2])
    return [x * inv for x in e]


def _att_combine_fwd(os, ls, tm=512):
    S = os[0].shape[0]

    def body(o0, o1, o2, l0, l1, l2, a_ref):
        w = _att_weights((l0, l1, l2))
        a_ref[...] = (w[0] * o0[...] + w[1] * o1[...] + w[2] * o2[...]).astype(BF16)

    row = _rows(tm, ATT_W)
    return pl.pallas_call(body, name="att_combine_fwd", grid=(S // tm,), in_specs=[row] * 6, out_specs=row,
                          out_shape=jax.ShapeDtypeStruct((S, ATT_W), BF16),
                          compiler_params=_cparams(("parallel",)))(*os, *ls)


def _att_combine_bwd(da, os, ls, tm=512):
    S = da.shape[0]

    def body(da_ref, o0, o1, o2, l0, l1, l2, *out_refs):
        da = da_ref[...]
        w = _att_weights((l0, l1, l2))
        dw = (da * o0[...], da * o1[...], da * o2[...])
        mean = w[0] * dw[0] + w[1] * dw[1] + w[2] * dw[2]
        for g in range(3):
            out_refs[g][...] = w[g] * da
            out_refs[3 + g][...] = w[g] * (dw[g] - mean)

    row = _rows(tm, ATT_W)
    shp = jax.ShapeDtypeStruct((S, ATT_W), F32)
    return pl.pallas_call(body, name="att_combine_bwd", grid=(S // tm,), in_specs=[row] * 7, out_specs=[row] * 6,
                          out_shape=[shp] * 6, compiler_params=_cparams(("parallel",)))(da, *os, *ls)


@jax.custom_vjp
def _bdot(a, b):
    return jnp.dot(a.astype(BF16), b.astype(BF16), preferred_element_type=F32)


def _bdot_fwd(a, b):
    return _bdot(a, b), (a, b)


def _bdot_bwd(res, ct):
    a, b = res
    ct16 = ct.astype(BF16)
    da = lax.dot_general(ct16, b.astype(BF16), (((1,), (1,)), ((), ())), preferred_element_type=F32)
    db = lax.dot_general(a.astype(BF16), ct16, (((0,), (0,)), ((), ())), preferred_element_type=F32)
    return da, db


_bdot.defvjp(_bdot_fwd, _bdot_bwd)


def _two_piece_dot(x, m):
    hi = x.astype(BF16)
    lo = (x - hi.astype(F32)).astype(BF16)
    return jnp.dot(hi, m, preferred_element_type=F32) + jnp.dot(lo, m, preferred_element_type=F32)


def _head_sum_impl(x):
    sel = (lax.broadcasted_iota(jnp.int32, (D, LANES), 0) // HEAD == lax.broadcasted_iota(jnp.int32, (D, LANES), 1)).astype(BF16)
    sel_t = (lax.broadcasted_iota(jnp.int32, (LANES, D), 1) // HEAD == lax.broadcasted_iota(jnp.int32, (LANES, D), 0)).astype(BF16)
    return _two_piece_dot(_two_piece_dot(x, sel), sel_t)


@jax.custom_vjp
def _head_sum(x):
    return _head_sum_impl(x)


_head_sum.defvjp(lambda x: (_head_sum_impl(x), None), lambda _, ct: (_head_sum_impl(ct),))


def _softplus(z):
    return jnp.maximum(z, 0.0) + jnp.log(1.0 + jnp.exp(-jnp.abs(z)))


def _rwkv_prep_fn(zr, zrp, zk, zkp, zv, zvp, zl, zlp, mu_r, mu_k, mu_v, mu_l, w0, a0, k_k, k_a, w2, a2, g2p):
    r = zr + (zrp - zr) * mu_r
    k = zk + (zkp - zk) * mu_k
    v = zv + (zvp - zv) * mu_v
    lo = zl + (zlp - zl) * mu_l
    w_low, a_low, g_low = lo[:, 0:LORA_W], lo[:, LORA_W:LORA_W + LORA_A], lo[:, LANES:LANES + G_PAD]
    w_log = -_softplus(-(w0 + _bdot(jnp.tanh(w_low), w2))) - 0.5
    decay = -jnp.exp(w_log)
    a = jax.nn.sigmoid(a0 + _bdot(a_low, a2))
    g = _bdot(jax.nn.sigmoid(g_low), g2p)
    kmod = k * (1.0 + (a - 1.0) * k_a)
    kk = k * k_k
    kk = kk / jnp.maximum(jnp.sqrt(_head_sum(kk * kk)), 1e-12)
    return r, decay, kmod, v, -kk, kk * a, g


def _rwkv_prep_specs(tm, blk=lambda i: i):
    vec = _full((1, D))
    rows = lambda w, col: pl.BlockSpec((tm, w), lambda i: (blk(i), col))
    prev = lambda w, col: pl.BlockSpec((SUBLANES, w), lambda i: (jnp.maximum(blk(i) * (tm // SUBLANES) - 1, 0), col))
    slabs = []
    for col in (C_R // D, C_K // D, C_V // D):
        slabs += [rows(D, col), prev(D, col)]
    slabs += [rows(LORA_PAD, C_LORA // LORA_PAD), prev(LORA_PAD, C_LORA // LORA_PAD)]
    params = [vec, vec, vec, _full((1, LORA_PAD)), vec, vec, vec, vec,
              _full((LORA_W, D)), _full((LORA_A, D)), _full((G_PAD, D))]
    return slabs, params


def _prep_inputs(refs, first):
    vals = []
    for s in range(4):
        z = refs[2 * s][...]
        vals += [z, _shift_down(z, refs[2 * s + 1][...], 1, first)]
    return vals + [r[...] for r in refs[8:19]]


def _rwkv_prep(P, params, tm=256):
    S = P.shape[0]
    slabs, pspecs = _rwkv_prep_specs(tm)

    def body(*refs):
        outs = _rwkv_prep_fn(*_prep_inputs(refs, pl.program_id(0) == 0))
        for o_ref, val in zip(refs[19:], outs):
            o_ref[...] = val

    shp = jax.ShapeDtypeStruct((S, D), F32)
    return pl.pallas_call(body, name="rwkv_prep", grid=(S // tm,), in_specs=slabs + pspecs,
                          out_specs=[_rows(tm, D)] * 7, out_shape=[shp] * 7,
                          compiler_params=_cparams(("parallel",)))(*([P] * 8), *params)


def _rwkv_prep_bwd(P, params, cts_a, cts_b, tm=128):
    S = P.shape[0]
    nblk = S // tm
    blk = lambda i: nblk - 1 - i
    slabs, pspecs = _rwkv_prep_specs(tm, blk)
    has_b = [c is not None for c in cts_b]
    n_ct = 7 + sum(has_b)

    def body(*refs):
        start = pl.program_id(0) == 0
        ins = _prep_inputs(refs, pl.program_id(0) == nblk - 1)
        ct_refs = refs[19:19 + n_ct]
        out_refs = refs[19 + n_ct:19 + n_ct + 15]
        carry_refs = refs[19 + n_ct + 15:]

        @pl.when(start)
        def _():
            for c_ref in carry_refs:
                c_ref[...] = jnp.zeros_like(c_ref)

        cts, pos = [], 7
        for i in range(7):
            c = ct_refs[i][...]
            if has_b[i]:
                c = c + ct_refs[pos][...]
                pos += 1
            cts.append(c)
        _, vjp = jax.vjp(_rwkv_prep_fn, *ins)
        grads = vjp(tuple(cts))
        for s in range(4):
            shifted = grads[2 * s + 1]
            out_refs[s][...] = (grads[2 * s] + _shift_up(shifted, carry_refs[s][...], 1, start)).astype(BF16)
            carry_refs[s][0:1, :] = shifted[0:1, :]
        for i in range(11):
            _acc(out_refs[4 + i], grads[8 + i], start)

    ct_in = list(cts_a) + [c for c in cts_b if c is not None]
    row = lambda w: pl.BlockSpec((tm, w), lambda i: (blk(i), 0))
    f = jax.ShapeDtypeStruct
    zshapes = [f((S, D), BF16)] * 3 + [f((S, LORA_PAD), BF16)]
    pshapes = [f((1, D), F32)] * 3 + [f((1, LORA_PAD), F32)] + [f((1, D), F32)] * 4 + [f((LORA_W, D), F32), f((LORA_A, D), F32), f((G_PAD, D), F32)]
    return pl.pallas_call(
        body, name="rwkv_prep_bwd", grid=(nblk,),
        in_specs=slabs + pspecs + [row(D)] * n_ct,
        out_specs=[row(D), row(D), row(D), row(LORA_PAD)] + pspecs,
        out_shape=zshapes + pshapes,
        scratch_shapes=[pltpu.VMEM((SUBLANES, D), F32)] * 3 + [pltpu.VMEM((SUBLANES, LORA_PAD), F32)],
        compiler_params=_cparams(("arbitrary",)))(*([P] * 8), *params, *ct_in)


def _rwkv_post_fn(y, r, kmod, v, g, lnx_w, lnx_b, r_k):
    mean = _head_sum(y) * (1.0 / HEAD)
    yc = y - mean
    var = _head_sum(yc * yc) * (1.0 / HEAD)
    yn = yc * lax.rsqrt(var + GN_EPS) * lnx_w + lnx_b
    bonus = _head_sum(r * kmod * r_k) * v
    return (yn + bonus) * g


def _rwkv_post(y, r, kmod, v, g, lnx_w, lnx_b, r_k, tm=256):
    S = y.shape[0]

    def body(y_ref, r_ref, k_ref, v_ref, g_ref, w_ref, b_ref, rk_ref, o_ref):
        o_ref[...] = _rwkv_post_fn(y_ref[...], r_ref[...], k_ref[...], v_ref[...], g_ref[...],
                                   w_ref[...], b_ref[...], rk_ref[...]).astype(BF16)

    row, vec = _rows(tm, D), _full((1, D))
    return pl.pallas_call(body, name="rwkv_post", grid=(S // tm,), in_specs=[row] * 5 + [vec] * 3, out_specs=row,
                          out_shape=jax.ShapeDtypeStruct((S, D), BF16),
                          compiler_params=_cparams(("parallel",)))(y, r, kmod, v, g, lnx_w, lnx_b, r_k)


def _rwkv_post_bwd(drw, y, r, kmod, v, g, lnx_w, lnx_b, r_k, tm=256):
    S = y.shape[0]

    def body(d_ref, y_ref, r_ref, k_ref, v_ref, g_ref, w_ref, b_ref, rk_ref, *out_refs):
        first = pl.program_id(0) == 0
        _, vjp = jax.vjp(_rwkv_post_fn, y_ref[...], r_ref[...], k_ref[...], v_ref[...], g_ref[...],
                         w_ref[...], b_ref[...], rk_ref[...])
        grads = vjp(d_ref[...])
        for i in range(5):
            out_refs[i][...] = grads[i]
        for i in range(5, 8):
            _acc(out_refs[i], grads[i], first)

    row, vec = _rows(tm, D), _full((1, D))
    f = jax.ShapeDtypeStruct
    return pl.pallas_call(body, name="rwkv_post_bwd", grid=(S // tm,), in_specs=[row] * 6 + [vec] * 3,
                          out_specs=[row] * 5 + [vec] * 3, out_shape=[f((S, D), F32)] * 5 + [f((1, D), F32)] * 3,
                          compiler_params=_cparams(("arbitrary",)))(drw, y, r, kmod, v, g, lnx_w, lnx_b, r_k)


CHUNK = 64
CHUNK_TB = 256
_DOT_DIMS = {"nn": (((2,), (1,)), ((0,), (0,))), "nt": (((2,), (2,)), ((0,), (0,))), "tn": (((1,), (1,)), ((0,), (0,)))}


def _dot16(x, y, mode):
    return lax.dot_general(x.astype(BF16), y.astype(BF16), _DOT_DIMS[mode], preferred_element_type=F32)


@functools.partial(jax.custom_vjp, nondiff_argnums=(2,))
def _mm16(x, y, mode):
    return _dot16(x, y, mode)


def _mm16_fwd(x, y, mode):
    return _dot16(x, y, mode), (x, y)


def _mm16_bwd(mode, res, ct):
    x, y = res
    if mode == "nn":
        return _dot16(ct, y, "nt"), _dot16(x, ct, "tn")
    if mode == "nt":
        return _dot16(ct, y, "nn"), _dot16(ct, x, "tn")
    return _dot16(y, ct, "nt"), _dot16(x, ct, "nn")


_mm16.defvjp(_mm16_fwd, _mm16_bwd)


def _tri_sum(x, upper):
    T = x.shape[0]
    i = lax.broadcasted_iota(jnp.int32, (T, T), 0)
    j = lax.broadcasted_iota(jnp.int32, (T, T), 1)
    tri = ((j >= i) if upper else (i >= j)).astype(BF16)
    out, rest = None, x
    for _ in range(3):
        piece = rest.astype(BF16)
        rest = rest - piece.astype(F32)
        part = jnp.dot(tri, piece, preferred_element_type=F32)
        out = part if out is None else out + part
    return out


@jax.custom_vjp
def _cumsum_rows(x):
    return _tri_sum(x, False)


_cumsum_rows.defvjp(lambda x: (_tri_sum(x, False), None), lambda _, ct: (_tri_sum(ct, True),))


def _rows_to_cols(row):
    per_head = jnp.concatenate([row[:, h * HEAD:(h + 1) * HEAD] for h in range(N_HEADS)], axis=0)
    eye = (lax.broadcasted_iota(jnp.int32, (HEAD, HEAD), 0) == lax.broadcasted_iota(jnp.int32, (HEAD, HEAD), 1)).astype(F32)
    cols = lax.dot_general(eye, per_head, (((1,), (1,)), ((), ())), precision=lax.Precision.HIGHEST,
                           preferred_element_type=F32)
    return jnp.concatenate([cols[:, h:h + 1][None] for h in range(N_HEADS)], axis=0)


def _per_head(x):
    return jnp.concatenate([x[:, h * HEAD:(h + 1) * HEAD][None] for h in range(N_HEADS)], axis=0)


def _chunk_fn(st0, r, lw, k, v, a, b):
    T = r.shape[0]
    cl = _cumsum_rows(lw)
    cl_end = cl[T - 1:T, :]
    inv = jnp.exp(-cl)
    to_end = jnp.exp(cl_end - cl)
    ah, rh, bh, kh, be, ke, v3 = [_per_head(x) for x in
                                  (a * jnp.exp(cl - lw), r * jnp.exp(cl), b * inv, k * inv, b * to_end, k * to_end, v)]
    i = lax.broadcasted_iota(jnp.int32, (N_HEADS, T, T), 1)
    j = lax.broadcasted_iota(jnp.int32, (N_HEADS, T, T), 2)
    a_ab = jnp.where(i > j, _mm16(ah, bh, "nt"), 0.0)
    a_ak = jnp.where(i > j, _mm16(ah, kh, "nt"), 0.0)
    m_rb = jnp.where(i >= j, _mm16(rh, bh, "nt"), 0.0)
    m_rk = jnp.where(i >= j, _mm16(rh, kh, "nt"), 0.0)
    rhs = _mm16(ah, st0, "nn") + _mm16(a_ak, v3, "nn")
    power, solve, n = a_ab, (i == j).astype(F32) + a_ab, 1
    while 2 * n < T:
        power = _mm16(power, power, "nn")
        solve = solve + _mm16(solve, power, "nn")
        n *= 2
    sa = _mm16(solve, rhs, "nn")
    y3 = _mm16(rh, st0, "nn") + _mm16(m_rb, sa, "nn") + _mm16(m_rk, v3, "nn")
    st_end = _rows_to_cols(jnp.exp(cl_end)) * st0 + _mm16(be, sa, "tn") + _mm16(ke, v3, "tn")
    return jnp.concatenate([y3[h] for h in range(N_HEADS)], axis=1), st_end


def _hosted_exchange(refs, n, broadcast, grid):
    if n == 0:
        return lambda: None
    start, wait = _exchange_ops(refs[:n], refs[n:2 * n], *refs[2 * n:], broadcast)
    first = functools.reduce(jnp.logical_and, [pl.program_id(a) == 0 for a in range(len(grid))])
    last = functools.reduce(jnp.logical_and, [pl.program_id(a) == g - 1 for a, g in enumerate(grid)])
    pl.when(first)(start)
    return lambda: pl.when(last)(wait)


def _cscan_fwd(r, lw, k, v, a, b, gather=()):
    S = r.shape[0]
    per_blk = CHUNK_TB // CHUNK
    n_x = len(gather)
    nblk = S // CHUNK_TB

    def body(*refs):
        r_ref, lw_ref, k_ref, v_ref, a_ref, b_ref = refs[:6]
        y_ref, ck_ref = refs[6 + n_x:8 + n_x]
        st_ref = refs[8 + 2 * n_x]
        finish = _hosted_exchange(refs[6:6 + n_x] + refs[8 + n_x:8 + 2 * n_x] + refs[9 + 2 * n_x:], n_x, True, (nblk,))

        @pl.when(pl.program_id(0) == 0)
        def _():
            st_ref[...] = jnp.zeros_like(st_ref)

        def chunk(c, carry):
            rows = pl.ds(pl.multiple_of(c * CHUNK, CHUNK), CHUNK)
            st0 = st_ref[...]
            ck_ref[c] = st0
            y, st_end = _chunk_fn(st0, r_ref[rows, :], lw_ref[rows, :], k_ref[rows, :],
                                  v_ref[rows, :], a_ref[rows, :], b_ref[rows, :])
            y_ref[rows, :] = y
            st_ref[...] = st_end
            return carry

        lax.fori_loop(0, per_blk, chunk, 0)
        finish()

    blk = _rows(CHUNK_TB, D)
    any_spec = pl.BlockSpec(memory_space=pl.ANY)
    outs = pl.pallas_call(
        body, name="scan_fwd", grid=(nblk,), in_specs=[blk] * 6 + [any_spec] * n_x,
        out_specs=[blk, pl.BlockSpec((per_blk, N_HEADS, HEAD, HEAD), lambda i: (i, 0, 0, 0))] + [any_spec] * n_x,
        out_shape=[jax.ShapeDtypeStruct((S, D), F32), jax.ShapeDtypeStruct((S // CHUNK, N_HEADS, HEAD, HEAD), F32)]
        + _exchange_shapes(gather, True),
        scratch_shapes=[pltpu.VMEM((N_HEADS, HEAD, HEAD), F32)] + (_exchange_scratch(n_x) if n_x else []),
        compiler_params=_cparams(("arbitrary",)))(r, lw, k, v, a, b, *gather)
    return outs[0], outs[1], outs[2:]


def _cscan_bwd(r, lw, k, v, a, b, ckpt, dy, scatter=()):
    S = r.shape[0]
    per_blk = CHUNK_TB // CHUNK
    nblk = S // CHUNK_TB
    n_x = len(scatter)

    def body(*refs):
        r_ref, lw_ref, k_ref, v_ref, a_ref, b_ref, ck_ref, dy_ref = refs[:8]
        out_refs = refs[8 + n_x:14 + n_x]
        ds_ref = refs[14 + 2 * n_x]
        finish = _hosted_exchange(refs[8:8 + n_x] + refs[14 + n_x:14 + 2 * n_x] + refs[15 + 2 * n_x:], n_x, False, (nblk,))

        @pl.when(pl.program_id(0) == 0)
        def _():
            ds_ref[...] = jnp.zeros_like(ds_ref)

        def chunk(cc, carry):
            c = per_blk - 1 - cc
            rows = pl.ds(pl.multiple_of(c * CHUNK, CHUNK), CHUNK)
            ins = (ck_ref[c], r_ref[rows, :], lw_ref[rows, :], k_ref[rows, :], v_ref[rows, :], a_ref[rows, :], b_ref[rows, :])
            _, vjp = jax.vjp(_chunk_fn, *ins)
            grads = vjp((dy_ref[rows, :], ds_ref[...]))
            ds_ref[...] = grads[0]
            for o_ref, g in zip(out_refs, grads[1:]):
                o_ref[rows, :] = g
            return carry

        lax.fori_loop(0, per_blk, chunk, 0)
        finish()

    blk = pl.BlockSpec((CHUNK_TB, D), lambda i: (nblk - 1 - i, 0))
    any_spec = pl.BlockSpec(memory_space=pl.ANY)
    shp = jax.ShapeDtypeStruct((S, D), F32)
    outs = pl.pallas_call(
        body, name="scan_bwd", grid=(nblk,),
        in_specs=[blk] * 6 + [pl.BlockSpec((per_blk, N_HEADS, HEAD, HEAD), lambda i: (nblk - 1 - i, 0, 0, 0)), blk]
        + [any_spec] * n_x,
        out_specs=[blk] * 6 + [any_spec] * n_x, out_shape=[shp] * 6 + _exchange_shapes(scatter, False),
        scratch_shapes=[pltpu.VMEM((N_HEADS, HEAD, HEAD), F32)] + (_exchange_scratch(n_x) if n_x else []),
        compiler_params=_cparams(("arbitrary",)))(r, lw, k, v, a, b, ckpt, dy, *scatter)
    return outs[:6], outs[6:]


def _ada_partial(c_all, w_shard):
    def body(c_ref, w_ref, o_ref):
        o_ref[...] = jnp.dot(c_ref[...].astype(BF16), w_ref[...].astype(BF16), preferred_element_type=F32)

    vm = pl.BlockSpec(memory_space=pltpu.VMEM)
    return pl.pallas_call(body, name="ada_partial", in_specs=[vm, vm], out_specs=vm,
                          out_shape=jax.ShapeDtypeStruct((N_DEV, w_shard.shape[1]), F32),
                          compiler_params=pltpu.CompilerParams(vmem_limit_bytes=VMEM_LIMIT))(c_all, w_shard)


def _ada_bias(rows, b_ada):
    def body(r_ref, b_ref, o_ref):
        o_ref[...] = r_ref[...] + b_ref[...]

    vm = pl.BlockSpec(memory_space=pltpu.VMEM)
    return pl.pallas_call(body, name="ada_bias", in_specs=[vm, vm], out_specs=vm,
                          out_shape=jax.ShapeDtypeStruct(rows.shape, F32))(rows, b_ada)


def _ada_wgrad(c_cols, d_all):
    def body(c_ref, d_ref, o_ref):
        acc = c_ref[:, 0:1] * d_ref[0:1, :]
        for j in range(1, N_DEV):
            acc = acc + c_ref[:, j:j + 1] * d_ref[j:j + 1, :]
        o_ref[...] = acc

    vm = pl.BlockSpec(memory_space=pltpu.VMEM)
    return pl.pallas_call(body, name="ada_wgrad", in_specs=[vm, vm], out_specs=vm,
                          out_shape=jax.ShapeDtypeStruct((D, d_all.shape[1]), F32),
                          compiler_params=pltpu.CompilerParams(vmem_limit_bytes=VMEM_LIMIT))(c_cols, d_all)


def _exchange(srcs, broadcast, name):
    n = len(srcs)

    def body(*refs):
        start, wait = _exchange_ops(refs[:n], refs[n:2 * n], *refs[2 * n:], broadcast)
        start()
        wait()

    any_spec = pl.BlockSpec(memory_space=pl.ANY)
    return pl.pallas_call(
        body, name=name, out_shape=_exchange_shapes(srcs, broadcast), in_specs=[any_spec] * n, out_specs=[any_spec] * n,
        scratch_shapes=_exchange_scratch(n),
        compiler_params=pltpu.CompilerParams(has_side_effects=True),
    )(*srcs)


def _gather_via_sibling(srcs, name):
    n = len(srcs)

    def body(*refs):
        src_refs, out_refs = refs[:n], refs[n:2 * n]
        send_sems, recv_sems, local_sems = refs[2 * n:]
        x, y, c = lax.axis_index("x"), lax.axis_index("y"), lax.axis_index("c")
        me, sibling = (x, y, c), (x, y, 1 - c)
        chips = [(1 - x, y), (x, 1 - y), (1 - x, 1 - y)]

        def slot(px, py, pc):
            return 4 * px + 2 * py + pc

        def copy(i, k, block, to, src=None):
            rows = out_refs[i].at[slot(*block)]
            return pltpu.make_async_remote_copy(
                src_ref=rows if src is None else src, dst_ref=rows, send_sem=send_sems.at[i, k],
                recv_sem=recv_sems.at[i, k], device_id=to, device_id_type=_MESH)

        local = [pltpu.make_async_copy(src_refs[i], out_refs[i].at[slot(*me)], local_sems.at[i]) for i in range(n)]
        for cp in local:
            cp.start()
        first = [copy(i, 0, me, sibling, src=src_refs[i]) for i in range(n)]
        first += [copy(i, 1 + j, me, (*chip, c), src=src_refs[i]) for j, chip in enumerate(chips) for i in range(n)]
        for cp in first:
            cp.start()
        passed = []
        for j, chip in enumerate(chips):
            for i in range(n):
                copy(i, 1 + j, (*chip, c), me).wait_recv()
                passed.append(copy(i, 4 + j, (*chip, c), sibling))
                passed[-1].start()
        for i in range(n):
            copy(i, 0, sibling, me).wait_recv()
            for j, chip in enumerate(chips):
                copy(i, 4 + j, (*chip, 1 - c), me).wait_recv()
        for cp in first + passed:
            cp.wait_send()
        for cp in local:
            cp.wait()

    any_spec = pl.BlockSpec(memory_space=pl.ANY)
    return pl.pallas_call(
        body, name=name, out_shape=_exchange_shapes(srcs, True), in_specs=[any_spec] * n, out_specs=[any_spec] * n,
        scratch_shapes=_exchange_scratch(n),
        compiler_params=pltpu.CompilerParams(has_side_effects=True),
    )(*srcs)


def _flags(broadcast, n):
    return [broadcast] * n if isinstance(broadcast, bool) else list(broadcast)


def _exchange_shapes(srcs, broadcast):
    return [jax.ShapeDtypeStruct((N_DEV,) + (s.shape if bc else s.shape[1:]), s.dtype)
            for s, bc in zip(srcs, _flags(broadcast, len(srcs)))]


def _exchange_scratch(n):
    return [pltpu.SemaphoreType.DMA((n, N_DEV)), pltpu.SemaphoreType.DMA((n, N_DEV)), pltpu.SemaphoreType.DMA((n,))]


def _exchange_ops(src_refs, out_refs, send_sems, recv_sems, local_sems, broadcast):
    n = len(src_refs)
    flags = _flags(broadcast, n)
    x, y, c = lax.axis_index("x"), lax.axis_index("y"), lax.axis_index("c")
    me = 4 * x + 2 * y + c

    def block(i, j):
        return src_refs[i] if flags[i] else src_refs[i].at[j]

    def remote(i, d, src_slot, dst_slot):
        px, py, pc = x ^ (d >> 2), y ^ ((d >> 1) & 1), c ^ (d & 1)
        return pltpu.make_async_remote_copy(
            src_ref=block(i, src_slot), dst_ref=out_refs[i].at[dst_slot], send_sem=send_sems.at[i, d],
            recv_sem=recv_sems.at[i, d], device_id=(px, py, pc), device_id_type=_MESH)

    def local(i):
        return pltpu.make_async_copy(block(i, me), out_refs[i].at[me], local_sems.at[i])

    def start():
        for i in range(n):
            local(i).start()
        for d in range(1, N_DEV):
            for i in range(n):
                remote(i, d, me ^ d, me).start()

    def wait():
        for d in range(1, N_DEV):
            for i in range(n):
                remote(i, d, me, me ^ d).wait_recv()
        for d in range(1, N_DEV):
            for i in range(n):
                remote(i, d, me ^ d, me).wait_send()
        for i in range(n):
            local(i).wait()

    return start, wait


def _adamw(w, g, m, v):
    nm = ADAM_B1 * m + (1.0 - ADAM_B1) * g
    nv = ADAM_B2 * v + (1.0 - ADAM_B2) * (g * g)
    m_hat = nm * (1.0 / (1.0 - ADAM_B1 ** ADAM_STEP))
    v_hat = nv * (1.0 / (1.0 - ADAM_B2 ** ADAM_STEP))
    return -ADAM_LR * (m_hat / (jnp.sqrt(v_hat) + ADAM_EPS) + ADAM_WD * w), nm, nv


def _adam_vectors(parts, ws, ms, vs):
    nv = len(ws)
    sizes = [w.shape[1] for w in ws]

    def body(*refs):
        p_ref = refs[0]
        w_refs, m_refs, v_refs = refs[1:1 + nv], refs[1 + nv:1 + 2 * nv], refs[1 + 2 * nv:1 + 3 * nv]
        out_refs = refs[1 + 3 * nv:]
        g_all = p_ref[0]
        for j in range(1, N_DEV):
            g_all = g_all + p_ref[j]
        off = 0
        for i, n in enumerate(sizes):
            g = g_all[:, off:off + n]
            off += -(-n // LANES) * LANES
            delta, new_m, new_v = _adamw(w_refs[i][...], g, m_refs[i][...], v_refs[i][...])
            for o_ref, val in zip(out_refs[4 * i:4 * i + 4], (g, delta, new_m, new_v)):
                o_ref[...] = val

    vm = pl.BlockSpec(memory_space=pltpu.VMEM)
    outs = pl.pallas_call(body, name="adam_replicated", in_specs=[vm] * (1 + 3 * nv), out_specs=[vm] * (4 * nv),
                          out_shape=[jax.ShapeDtypeStruct((1, n), F32) for n in sizes for _ in range(4)])(parts, *ws, *ms, *vs)
    return [outs[4 * i:4 * i + 4] for i in range(nv)]


def _sum_adam(parts, w, m, v, name):
    n_parts, R, C = parts.shape
    fits = [t for t in range(16, R + 1, 16) if R % t == 0 and t * C <= 2504 * LANES]
    if fits:
        tm, tc = max(fits), C
    elif C % (2 * LANES) == 0 and R * C > 2504 * LANES:
        tm, tc = R, 2 * LANES
    else:
        tm, tc = R, C

    def body(p_ref, w_ref, m_ref, v_ref, g_ref, d_ref, nm_ref, nv_ref):
        g = p_ref[0].astype(F32)
        for j in range(1, n_parts):
            g = g + p_ref[j].astype(F32)
        g_ref[...] = g
        d_ref[...], nm_ref[...], nv_ref[...] = _adamw(w_ref[...], g, m_ref[...], v_ref[...])

    blk = pl.BlockSpec((tm, tc), lambda i, j: (i, j))
    shp = jax.ShapeDtypeStruct((R, C), F32)
    return pl.pallas_call(body, name=name, grid=(R // tm, C // tc),
                          in_specs=[pl.BlockSpec((n_parts, tm, tc), lambda i, j: (0, i, j)), blk, blk, blk],
                          out_specs=[blk] * 4, out_shape=[shp] * 4,
                          compiler_params=_cparams(("parallel", "parallel")))(parts, w, m, v)


TRANSPOSED = ("w_in", "w_up")
SHARDED = (("w_ada", 1), ("w_in", 0), ("w2", 1), ("a2", 1), ("g2", 1), ("w_att_out", 1), ("w_rwkv_out", 0),
           ("w_o", 0), ("w_up", 0), ("conv_w", 1), ("w_down", 0))
EARLY, LATE = SHARDED[1:5], SHARDED[5:]
REPLICATED = ("b_ada", "norm1_w", "b_gate", "mu_shift", "w0", "a0", "k_k", "k_a", "r_k", "lnx_w", "lnx_b",
              "norm2_w", "conv_b", "norm_f_w")
WEIGHTS = ("w_ada", "b_ada", "norm1_w", "w_in", "b_gate", "mu_shift", "w0", "w2", "a0", "a2", "g2", "k_k", "k_a", "r_k",
           "lnx_w", "lnx_b", "w_att_out", "w_rwkv_out", "w_o", "norm2_w", "w_up", "conv_w", "conv_b", "w_down", "norm_f_w")


W_IN_RUNS = ((0, C_ATT, ATT_IN), (ATT_IN, C_R, 3 * D), (ATT_IN + 3 * D, C_LORA, LORA_W + LORA_A),
             (ATT_IN + 3 * D + LORA_W + LORA_A, C_LORA + LANES, LORA_G), (ATT_IN + RWKV_IN, C_GA, 2 * D))
W_IN_SHARD = N_IN // N_DEV


def _pad_w_in(w_in_t):
    pieces = [w_in_t[orig:orig + count] for orig, _, count in sorted(W_IN_RUNS, key=lambda run: run[1])]
    pieces.append(jnp.zeros((LORA_PAD - LANES - LORA_G, w_in_t.shape[1]), w_in_t.dtype))
    return jnp.concatenate(pieces, axis=0)


def _w_in_blocks(g):
    blocks = []
    for j in range(N_DEV):
        pieces = []
        for orig, pad, count in W_IN_RUNS:
            lo, hi = max(orig, j * W_IN_SHARD), min(orig + count, (j + 1) * W_IN_SHARD)
            if lo < hi:
                pieces.append(g[pad + lo - orig:pad + hi - orig])
        blocks.append(jnp.concatenate(pieces, axis=0)[None])
    return jnp.concatenate(blocks, axis=0)


def _pad_mu(mu):
    lo = mu[:, 3 * D:]
    mu_l = jnp.concatenate([lo[:, :LORA_W + LORA_A], lo[:, LORA_W + LORA_A:], jnp.zeros((1, LORA_PAD - LANES - LORA_G), mu.dtype)], axis=1)
    return mu[:, :D], mu[:, D:2 * D], mu[:, 2 * D:3 * D], mu_l


def _local_step(x, ada, W, late_shards, target):
    S = x.shape[0]
    W = dict(W)
    G = {}
    sh1, sc1, gt1, sh2, sc2, gt2 = [ada[:, i * D:(i + 1) * D] for i in range(6)]
    h1, rstd1 = _norm_fwd(x, None, None, W["norm1_w"], sc1, sh1, "norm1_fwd")
    w_in_p = _pad_w_in(W["w_in"])
    P = _mm(h1, w_in_p, "nt", F32, "proj_in")

    mu_r, mu_k, mu_v, mu_l = _pad_mu(W["mu_shift"])
    g2p = jnp.pad(W["g2"], ((0, G_PAD - LORA_G), (0, 0)))
    prep_params = [mu_r, mu_k, mu_v, mu_l, W["w0"], W["a0"], W["k_k"], W["k_a"], W["w2"], W["a2"], g2p]
    r_, dec, kmod, v_, aa, bb, gg = _rwkv_prep(P, prep_params)
    y_scan, states, late = _cscan_fwd(r_, dec, kmod, v_, aa, bb, gather=late_shards)
    W.update({n: _full_weight(g, axis) for (n, axis), g in zip(LATE, late)})

    o_g, l_g = zip(*[_att_fwd(P, g) for g in range(len(ATT_PATTERNS))])
    att = _att_combine_fwd(o_g, l_g)
    y_att = _mm(att, W["w_att_out"], "nn", F32, "att_out")
    r_k = W["r_k"].reshape(1, D)
    rw = _rwkv_post(y_scan, r_, kmod, v_, gg, W["lnx_w"], W["lnx_b"], r_k)
    y_rwkv = _mm(rw, W["w_rwkv_out"], "nn", F32, "rwkv_out")

    bga, bgr = W["b_gate"][:, :D], W["b_gate"][:, D:]
    mix = _gate_fwd(P, bga, bgr, y_att, y_rwkv)
    mo = _mm(mix, W["w_o"], "nn", F32, "mix_out")
    x2, h2, rstd2 = _norm_fwd(x, mo, gt1, W["norm2_w"], sc2, sh2, "norm2_fwd")
    u = _mm(h2, W["w_up"], "nt", BF16, "ffn_up")
    conv_w8 = jnp.pad(W["conv_w"], ((0, SUBLANES - 3), (0, 0)))
    act = _conv_fwd(u, conv_w8, W["conv_b"])
    f = _mm(act, W["w_down"], "nn", F32, "ffn_down")
    loss_blk, dx3, df, dgt2, G["norm_f_w"] = _final(x2, f, gt2, W["norm_f_w"], target)
    loss = loss_blk[0, 0]

    dact = _mm(df, W["w_down"], "nt", BF16, "ffn_down_dx")
    G["w_down"] = _mm(act, df, "tn", BF16, "ffn_down_dw")
    duc, dwg, dwv, dbg, dbv = _conv_bwd_a(dact, u, conv_w8, W["conv_b"])
    G["conv_w"] = jnp.concatenate([dwg[0:3], dwv[0:3]], axis=1)
    G["conv_b"] = jnp.concatenate([dbg, dbv], axis=1)
    du = _conv_bwd_b(duc, conv_w8)
    dh2 = _mm(du, W["w_up"], "nn", F32, "ffn_up_dx")
    G["w_up"] = _mm(du, h2, "tn", BF16, "ffn_up_dw")
    dx2, dsh2, dsc2, G["norm2_w"], dmo, dgt1 = _norm_bwd(dh2, x2, rstd2, W["norm2_w"], sc2, dx3, mo, gt1, "norm2_bwd")
    dmix = _mm(dmo, W["w_o"], "nt", F32, "mix_out_dx")
    G["w_o"] = _mm(mix, dmo, "tn", BF16, "mix_out_dw")
    dy_att, dy_rwkv, dpga, dpgr, dbga, dbgr = _gate_bwd(dmix, P, bga, bgr, y_att, y_rwkv)
    G["b_gate"] = jnp.concatenate([dbga, dbgr], axis=1)

    datt = _mm(dy_att, W["w_att_out"], "nt", F32, "att_out_dx")
    G["w_att_out"] = _mm(att, dy_att, "tn", BF16, "att_out_dw")
    dcomb = _att_combine_bwd(datt, o_g, l_g)
    dp_att = []
    for g in range(len(ATT_PATTERNS)):
        dp_att += _att_bwd(P, o_g[g], l_g[g], dcomb[g], dcomb[3 + g], g)

    drw = _mm(dy_rwkv, W["w_rwkv_out"], "nt", F32, "rwkv_out_dx")
    G["w_rwkv_out"] = _mm(rw, dy_rwkv, "tn", BF16, "rwkv_out_dw")
    dy_scan, dr1, dk1, dv1, dgg, G["lnx_w"], G["lnx_b"], drk = _rwkv_post_bwd(drw, y_scan, r_, kmod, v_, gg, W["lnx_w"], W["lnx_b"], r_k)
    G["r_k"] = drk.reshape(W["r_k"].shape)
    late_blocks = [_owner_blocks(G[n], axis) for n, axis in LATE] if late_shards else []
    (dr2, ddec, dk2, dv2, daa, dbb), late_parts = _cscan_bwd(r_, dec, kmod, v_, aa, bb, states, dy_scan, scatter=late_blocks)
    pb = _rwkv_prep_bwd(P, prep_params, [dr2, ddec, dk2, dv2, daa, dbb, dgg], [dr1, None, dk1, dv1, None, None, None])
    dp_rkv, dp_lora, dpar = list(pb[0:3]), pb[3], pb[4:]
    dmu_r, dmu_k, dmu_v, dmu_l, G["w0"], G["a0"], G["k_k"], G["k_a"], G["w2"], G["a2"], dg2p = dpar
    G["g2"] = dg2p[0:LORA_G]
    G["mu_shift"] = jnp.concatenate([dmu_r, dmu_k, dmu_v, dmu_l[:, :LORA_W + LORA_A], dmu_l[:, LANES:LANES + LORA_G]], axis=1)

    dP = jnp.concatenate(dp_rkv + [dpga, dpgr] + dp_att + [dp_lora], axis=1)
    G["w_in"] = _w_in_blocks(_mm(dP, h1, "tn", BF16, "proj_in_dw"))
    if late_shards:
        dh1, (w_in_parts,) = _mm(dP, w_in_p, "nn", F32, "proj_in_dx", scatter=[G["w_in"]])
        done = dict(zip([n for n, _ in LATE] + ["w_in"], list(late_parts) + [w_in_parts]))
    else:
        dh1, done = _mm(dP, w_in_p, "nn", F32, "proj_in_dx"), {}
    grad_x, dsh1, dsc1, G["norm1_w"] = _norm_bwd(dh1, x, rstd1, W["norm1_w"], sc1, dx2, None, None, "norm1_bwd")
    dada = jnp.concatenate([dsh1, dsc1, dgt1, dsh2, dsc2, dgt2], axis=1)
    G["b_ada"] = dada
    return loss, grad_x, G, done


def _full_weight(gathered, axis):
    _, rows, cols = gathered.shape
    if axis == 0:
        return gathered.reshape(N_DEV * rows, cols)
    return gathered.transpose(1, 0, 2).reshape(rows, N_DEV * cols)


def _owner_blocks(g, axis):
    rows, cols = g.shape
    g = g.astype(BF16)
    if axis == 0:
        return g.reshape(N_DEV, rows // N_DEV, cols)
    return g.reshape(rows, N_DEV, cols // N_DEV).transpose(1, 0, 2)


def kernel(x, c, w_ada, b_ada, norm1_w, w_in, b_gate, mu_shift, w0, w2, a0, a2, g2, k_k, k_a, r_k, lnx_w, lnx_b, w_att_out, w_rwkv_out, w_o, norm2_w, w_up, conv_w, conv_b, w_down, norm_f_w, loss_target, m_w_ada, m_b_ada, m_norm1_w, m_w_in, m_b_gate, m_mu_shift, m_w0, m_w2, m_a0, m_a2, m_g2, m_k_k, m_k_a, m_r_k, m_lnx_w, m_lnx_b, m_w_att_out, m_w_rwkv_out, m_w_o, m_norm2_w, m_w_up, m_conv_w, m_conv_b, m_w_down, m_norm_f_w, v_w_ada, v_b_ada, v_norm1_w, v_w_in, v_b_gate, v_mu_shift, v_w0, v_w2, v_a0, v_a2, v_g2, v_k_k, v_k_a, v_r_k, v_lnx_w, v_lnx_b, v_w_att_out, v_w_rwkv_out, v_w_o, v_norm2_w, v_w_up, v_conv_w, v_conv_b, v_w_down, v_norm_f_w):
    env = dict(locals())
    w_shard = {n: env[n] for n in WEIGHTS}
    m_shard = {n: env["m_" + n] for n in WEIGHTS}
    v_shard = {n: env["v_" + n] for n in WEIGHTS}

    def mat(shards, n):
        return jnp.swapaxes(shards[n][0], 0, 1) if n in TRANSPOSED else shards[n][0]

    c_all, *gathered = _gather_via_sibling([c] + [mat(w_shard, n).astype(BF16) for n, _ in EARLY], "gather_weights")
    c_all = c_all.reshape(N_DEV, D)
    W = {n: _full_weight(g, axis) for (n, axis), g in zip(EARLY, gathered)}
    for n in REPLICATED:
        W[n] = w_shard[n].reshape(1, -1) if n != "r_k" else w_shard[n][0]
    ada_cols = _ada_partial(c_all, w_shard["w_ada"][0])
    ada_rows, = _exchange([ada_cols[:, None, :]], False, "ada_rows")
    ada = _ada_bias(ada_rows.reshape(1, -1), w_shard["b_ada"])

    late_shards = [mat(w_shard, n).astype(BF16) for n, _ in LATE]
    loss, grad_x, G, parts = _local_step(x[0], ada, W, late_shards, loss_target[0])
    loss = lax.psum(loss, ("x", "y", "c"))

    row = lambda a: a.reshape(1, -1)
    small = jnp.concatenate([jnp.pad(row(G[n]), ((0, 0), (0, (-G[n].size) % LANES))) for n in REPLICATED], axis=1)
    sparts, dada_all = _exchange([small, G["b_ada"].reshape(N_DEV, 1, -1)], [True, False], "gather_small_grads")
    parts["w_ada"] = _ada_wgrad(c_all.T, dada_all.reshape(N_DEV, -1))[None]

    rest = [(n, axis) for n, axis in SHARDED if n not in parts]
    parts.update(zip([n for n, _ in rest], _exchange([_owner_blocks(G[n], axis) for n, axis in rest], False, "scatter_grads")))
    out = {}
    for n, p in parts.items():
        res = _sum_adam(p, mat(w_shard, n), mat(m_shard, n), mat(v_shard, n), "adam_" + n)
        if n in TRANSPOSED:
            res = [jnp.swapaxes(a, 0, 1) for a in res]
        for kind, a in zip(("grad", "delta", "new_m", "new_v"), res):
            out[kind, n] = a[None]

    res = _adam_vectors(sparts, *[[row(s[n]) for n in REPLICATED] for s in (w_shard, m_shard, v_shard)])
    for n, four in zip(REPLICATED, res):
        for kind, a in zip(("grad", "delta", "new_m", "new_v"), four):
            out[kind, n] = a.reshape(w_shard[n].shape)

    return (loss, grad_x[None], *[out[kind, n] for kind in ("grad", "delta", "new_m", "new_v") for n in WEIGHTS])
```

```python
import functools

import jax
import jax.numpy as jnp
from jax import lax
from jax.experimental import pallas as pl
from jax.experimental.pallas import tpu as pltpu

F32 = jnp.float32
BF16 = jnp.bfloat16

D = 1024
HEAD = 64
ATT_PATTERNS = ((128, 1), (512, 4), (2048, 16))
ATT_HEADS = 8
ATT_W = ATT_HEADS * HEAD
ATT_IN = 3 * 3 * ATT_W
QBLK = 128
N_HEADS = D // HEAD
LORA_W, LORA_A, LORA_G = 64, 64, 160
RWKV_IN = 3 * D + LORA_W + LORA_A + LORA_G
N_IN = ATT_IN + RWKV_IN + 2 * D
D_FF = 2816
RMS_EPS = 1e-6
GN_EPS = 64e-5
N_DEV = 8
LANES = 128
SUBLANES = 8

C_R, C_K, C_V, C_GA, C_GR = 0, 1024, 2048, 3072, 4096
C_ATT = 5120
C_LORA = C_ATT + ATT_IN
LORA_PAD = 512
G_PAD = 256
N_PAD = C_LORA + LORA_PAD

ADAM_LR, ADAM_B1, ADAM_B2, ADAM_EPS, ADAM_WD, ADAM_STEP = 0.001, 0.9, 0.999, 1e-08, 0.01, 10

VMEM_LIMIT = 56 * 1024 * 1024

_MESH = pl.DeviceIdType.MESH


def _cparams(sem):
    return pltpu.CompilerParams(dimension_semantics=sem, vmem_limit_bytes=VMEM_LIMIT)


def _tile(dim, pref):
    if dim <= pref:
        return dim
    best = None
    for t in range(LANES, pref + 1, LANES):
        if dim % t == 0:
            best = t
    assert best is not None, dim
    return best


MM_TILES = {"nn": (1024, 1408, 1408), "nt": (1024, 2048, 1408), "tn": (1408, 1408, 4096)}


def _mm(a, b, mode, out_dtype, name, scatter=()):
    if mode == "nn":
        (M, K), (K2, N) = a.shape, b.shape
    elif mode == "nt":
        (M, K), (N, K2) = a.shape, b.shape
    else:
        (K, M), (K2, N) = a.shape, b.shape
    assert K == K2, (a.shape, b.shape, mode)
    tm, tn, tk = (_tile(dim, pref) for dim, pref in zip((M, N, K), MM_TILES[mode]))
    nk = K // tk
    grid = (M // tm, N // tn, nk)
    n_x = len(scatter)
    dims = {"nn": (((1,), (0,)), ((), ())), "nt": (((1,), (1,)), ((), ())), "tn": (((0,), (0,)), ((), ()))}[mode]

    def body(*refs):
        a_ref, b_ref = refs[:2]
        o_ref, acc_ref = refs[2 + n_x], refs[3 + 2 * n_x]
        finish = _hosted_exchange(refs[2:2 + n_x] + refs[3 + n_x:3 + 2 * n_x] + refs[4 + 2 * n_x:], n_x, False, grid)
        k = pl.program_id(2)
        part = lax.dot_general(a_ref[...].astype(BF16), b_ref[...].astype(BF16), dims,
                               preferred_element_type=F32)
        if nk == 1:
            o_ref[...] = part.astype(o_ref.dtype)
        else:
            @pl.when(k == 0)
            def _():
                acc_ref[...] = part

            @pl.when(jnp.logical_and(k > 0, k < nk - 1))
            def _():
                acc_ref[...] += part

            @pl.when(k == nk - 1)
            def _():
                o_ref[...] = (acc_ref[...] + part).astype(o_ref.dtype)
        finish()

    a_spec = pl.BlockSpec((tk, tm), lambda i, j, k: (k, i)) if mode == "tn" else pl.BlockSpec((tm, tk), lambda i, j, k: (i, k))
    b_spec = pl.BlockSpec((tn, tk), lambda i, j, k: (j, k)) if mode == "nt" else pl.BlockSpec((tk, tn), lambda i, j, k: (k, j))
    any_spec = pl.BlockSpec(memory_space=pl.ANY)
    outs = pl.pallas_call(
        body, name=name, grid=grid,
        in_specs=[a_spec, b_spec] + [any_spec] * n_x,
        out_specs=[pl.BlockSpec((tm, tn), lambda i, j, k: (i, j))] + [any_spec] * n_x,
        out_shape=[jax.ShapeDtypeStruct((M, N), out_dtype)] + _exchange_shapes(scatter, False),
        scratch_shapes=[pltpu.VMEM((tm, tn) if nk > 1 else (SUBLANES, LANES), F32)] + (_exchange_scratch(n_x) if n_x else []),
        compiler_params=_cparams(("arbitrary",) * 3 if n_x else ("parallel", "parallel", "arbitrary")),
    )(a, b, *scatter)
    return (outs[0], outs[1:]) if n_x else outs[0]


def _rows(tm, w, col=0):
    return pl.BlockSpec((tm, w), lambda i: (i, col))


def _full(shape):
    return pl.BlockSpec(shape, lambda i: (0,) * len(shape))


def _shift_down(x, halo, k, first):
    rolled = pltpu.roll(x, k, 0)
    row = lax.broadcasted_iota(jnp.int32, x.shape, 0)
    out = rolled
    n_halo = halo.shape[0]
    for j in range(k):
        h = jnp.where(first, 0.0, halo[n_halo - k + j:n_halo - k + j + 1, :])
        out = jnp.where(row == j, h, out)
    return out


def _shift_up(x, halo, k, last):
    n = x.shape[0]
    rolled = pltpu.roll(x, n - k, 0)
    row = lax.broadcasted_iota(jnp.int32, x.shape, 0)
    out = rolled
    for j in range(k):
        h = jnp.where(last, 0.0, halo[j:j + 1, :])
        out = jnp.where(row == n - k + j, h, out)
    return out


def _acc(ref, val, first):
    @pl.when(first)
    def _():
        ref[...] = val

    @pl.when(jnp.logical_not(first))
    def _():
        ref[...] += val


def _colsum(x):
    return jnp.sum(x, axis=0, keepdims=True)


def _norm_fwd(x, mo, gt, nw, sc, sh, name, tm=256):
    S = x.shape[0]
    has_res = mo is not None

    def body(*refs):
        if has_res:
            x_ref, mo_ref, gt_ref, nw_ref, sc_ref, sh_ref, x2_ref, h_ref, rs_ref = refs
            x2 = x_ref[...] + gt_ref[...] * mo_ref[...]
            x2_ref[...] = x2
        else:
            x_ref, nw_ref, sc_ref, sh_ref, h_ref, rs_ref = refs
            x2 = x_ref[...]
        rstd = lax.rsqrt(jnp.mean(x2 * x2, axis=-1, keepdims=True) + RMS_EPS)
        rs_ref[...] = rstd
        h_ref[...] = ((x2 * rstd * nw_ref[...]) * (1.0 + sc_ref[...]) + sh_ref[...]).astype(BF16)

    vec = _full((1, D))
    ins = [x, mo, gt, nw, sc, sh] if has_res else [x, nw, sc, sh]
    in_specs = [_rows(tm, D), _rows(tm, D), vec, vec, vec, vec] if has_res else [_rows(tm, D), vec, vec, vec]
    outs = [jax.ShapeDtypeStruct((S, D), BF16), jax.ShapeDtypeStruct((S, 1), F32)]
    out_specs = [_rows(tm, D), _rows(tm, 1)]
    if has_res:
        outs = [jax.ShapeDtypeStruct((S, D), F32)] + outs
        out_specs = [_rows(tm, D)] + out_specs
    return pl.pallas_call(body, name=name, grid=(S // tm,), in_specs=in_specs, out_specs=out_specs,
                          out_shape=outs, compiler_params=_cparams(("parallel",)))(*ins)


def _norm_bwd(dh, xin, rstd, nw, sc, dres, mo, gt, name, tm=256):
    S = xin.shape[0]
    has_res = mo is not None

    def body(*refs):
        if has_res:
            dh_ref, x_ref, rs_ref, nw_ref, sc_ref, dres_ref, mo_ref, gt_ref, dx_ref, dsh_ref, dsc_ref, dnw_ref, dmo_ref, dgt_ref = refs
        else:
            dh_ref, x_ref, rs_ref, nw_ref, sc_ref, dres_ref, dx_ref, dsh_ref, dsc_ref, dnw_ref = refs
        first = pl.program_id(0) == 0
        dh = dh_ref[...]
        rstd = rs_ref[...]
        n = x_ref[...] * rstd
        w = nw_ref[...]
        _acc(dsh_ref, _colsum(dh), first)
        _acc(dsc_ref, _colsum(dh * (n * w)), first)
        dnw = dh * (1.0 + sc_ref[...])
        _acc(dnw_ref, _colsum(dnw * n), first)
        dn = dnw * w
        dx = dres_ref[...] + rstd * (dn - n * jnp.mean(dn * n, axis=-1, keepdims=True))
        dx_ref[...] = dx
        if has_res:
            dmo_ref[...] = (dx * gt_ref[...]).astype(BF16)
            _acc(dgt_ref, _colsum(dx * mo_ref[...]), first)

    vec = _full((1, D))
    vshape = jax.ShapeDtypeStruct((1, D), F32)
    ins = [dh, xin, rstd, nw, sc, dres] + ([mo, gt] if has_res else [])
    in_specs = [_rows(tm, D), _rows(tm, D), _rows(tm, 1), vec, vec, _rows(tm, D)] + ([_rows(tm, D), vec] if has_res else [])
    outs = [jax.ShapeDtypeStruct((S, D), F32), vshape, vshape, vshape]
    out_specs = [_rows(tm, D), vec, vec, vec]
    if has_res:
        outs += [jax.ShapeDtypeStruct((S, D), BF16), vshape]
        out_specs += [_rows(tm, D), vec]
    return pl.pallas_call(body, name=name, grid=(S // tm,), in_specs=in_specs, out_specs=out_specs,
                          out_shape=outs, compiler_params=_cparams(("arbitrary",)))(*ins)


def _final(x2, f, gt2, nfw, target, tm=256):
    S = x2.shape[0]

    def body(x2_ref, f_ref, gt_ref, w_ref, t_ref, loss_ref, dx_ref, df_ref, dgt_ref, dw_ref):
        first = pl.program_id(0) == 0
        f = f_ref[...]
        gt = gt_ref[...]
        w = w_ref[...]
        x3 = x2_ref[...] + gt * f
        rstd = lax.rsqrt(jnp.mean(x3 * x3, axis=-1, keepdims=True) + RMS_EPS)
        n = x3 * rstd
        e = n * w - t_ref[...]
        part = 0.5 * jnp.sum(jnp.mean(e * e, axis=-1, keepdims=True), axis=0, keepdims=True)
        _acc(loss_ref, jnp.broadcast_to(part, (SUBLANES, LANES)), first)
        dy = e * (1.0 / D)
        _acc(dw_ref, _colsum(dy * n), first)
        dn = dy * w
        dx = rstd * (dn - n * jnp.mean(dn * n, axis=-1, keepdims=True))
        dx_ref[...] = dx
        df_ref[...] = (dx * gt).astype(BF16)
        _acc(dgt_ref, _colsum(dx * f), first)

    vec = _full((1, D))
    vshape = jax.ShapeDtypeStruct((1, D), F32)
    return pl.pallas_call(
        body, name="final_loss", grid=(S // tm,),
        in_specs=[_rows(tm, D), _rows(tm, D), vec, vec, _rows(tm, D)],
        out_specs=[_full((SUBLANES, LANES)), _rows(tm, D), _rows(tm, D), vec, vec],
        out_shape=[jax.ShapeDtypeStruct((SUBLANES, LANES), F32), jax.ShapeDtypeStruct((S, D), F32),
                   jax.ShapeDtypeStruct((S, D), BF16), vshape, vshape],
        compiler_params=_cparams(("arbitrary",)))(x2, f, gt2, nfw, target)


def _gate_fwd(P, bga, bgr, y_att, y_rwkv, tm=256):
    S = P.shape[0]

    def body(pa_ref, pr_ref, ba_ref, br_ref, ya_ref, yr_ref, mix_ref):
        ga = jax.nn.sigmoid(pa_ref[...] + ba_ref[...])
        gr = jax.nn.sigmoid(pr_ref[...] + br_ref[...])
        mix_ref[...] = (ga * ya_ref[...] + gr * yr_ref[...]).astype(BF16)

    vec = _full((1, D))
    return pl.pallas_call(
        body, name="gate_fwd", grid=(S // tm,),
        in_specs=[_rows(tm, D, C_GA // D), _rows(tm, D, C_GR // D), vec, vec, _rows(tm, D), _rows(tm, D)],
        out_specs=_rows(tm, D), out_shape=jax.ShapeDtypeStruct((S, D), BF16),
        compiler_params=_cparams(("parallel",)))(P, P, bga, bgr, y_att, y_rwkv)


def _gate_bwd(dmix, P, bga, bgr, y_att, y_rwkv, tm=256):
    S = P.shape[0]

    def body(dm_ref, pa_ref, pr_ref, ba_ref, br_ref, ya_ref, yr_ref, dya_ref, dyr_ref, dpa_ref, dpr_ref, dba_ref, dbr_ref):
        first = pl.program_id(0) == 0
        dm = dm_ref[...]
        ga = jax.nn.sigmoid(pa_ref[...] + ba_ref[...])
        gr = jax.nn.sigmoid(pr_ref[...] + br_ref[...])
        dya_ref[...] = (dm * ga).astype(BF16)
        dyr_ref[...] = (dm * gr).astype(BF16)
        dpa = dm * ya_ref[...] * ga * (1.0 - ga)
        dpr = dm * yr_ref[...] * gr * (1.0 - gr)
        dpa_ref[...] = dpa.astype(BF16)
        dpr_ref[...] = dpr.astype(BF16)
        _acc(dba_ref, _colsum(dpa), first)
        _acc(dbr_ref, _colsum(dpr), first)

    vec = _full((1, D))
    row = _rows(tm, D)
    rshape = jax.ShapeDtypeStruct((S, D), BF16)
    vshape = jax.ShapeDtypeStruct((1, D), F32)
    return pl.pallas_call(
        body, name="gate_bwd", grid=(S // tm,),
        in_specs=[row, _rows(tm, D, C_GA // D), _rows(tm, D, C_GR // D), vec, vec, row, row],
        out_specs=[row, row, row, row, vec, vec],
        out_shape=[rshape, rshape, rshape, rshape, vshape, vshape],
        compiler_params=_cparams(("arbitrary",)))(dmix, P, P, bga, bgr, y_att, y_rwkv)


CONV_TN = D_FF // 2
HALO = 16


def _conv_fwd(u, conv_w8, conv_b, tm=256, tn=CONV_TN):
    S = u.shape[0]
    nj = D_FF // tn

    def conv(u_ref, h_ref, w_ref, b_ref, first):
        u = u_ref[...].astype(F32)
        h = h_ref[...].astype(F32)
        w = w_ref[...]
        return b_ref[...] + w[0:1] * _shift_down(u, h, 2, first) + w[1:2] * _shift_down(u, h, 1, first) + w[2:3] * u

    def body(ug_ref, hg_ref, uv_ref, hv_ref, wg_ref, wv_ref, bg_ref, bv_ref, act_ref):
        first = pl.program_id(0) == 0
        g = conv(ug_ref, hg_ref, wg_ref, bg_ref, first)
        v = conv(uv_ref, hv_ref, wv_ref, bv_ref, first)
        act_ref[...] = (g * jax.nn.sigmoid(g) * v).astype(BF16)

    blk = lambda off: pl.BlockSpec((tm, tn), lambda i, j: (i, j + off))
    halo = lambda off: pl.BlockSpec((HALO, tn), lambda i, j: (jnp.maximum(i * (tm // HALO) - 1, 0), j + off))
    wsp = lambda off: pl.BlockSpec((SUBLANES, tn), lambda i, j: (0, j + off))
    bsp = lambda off: pl.BlockSpec((1, tn), lambda i, j: (0, j + off))
    return pl.pallas_call(
        body, name="conv_fwd", grid=(S // tm, nj),
        in_specs=[blk(0), halo(0), blk(nj), halo(nj), wsp(0), wsp(nj), bsp(0), bsp(nj)],
        out_specs=pl.BlockSpec((tm, tn), lambda i, j: (i, j)),
        out_shape=jax.ShapeDtypeStruct((S, D_FF), BF16),
        compiler_params=_cparams(("parallel", "parallel")))(u, u, u, u, conv_w8, conv_w8, conv_b, conv_b)


def _conv_bwd_a(dact, u, conv_w8, conv_b, tm=256, tn=CONV_TN):
    S = u.shape[0]
    nj = D_FF // tn

    def half(u_ref, h_ref, w_ref, b_ref, first):
        u = u_ref[...].astype(F32)
        h = h_ref[...].astype(F32)
        w = w_ref[...]
        u2, u1 = _shift_down(u, h, 2, first), _shift_down(u, h, 1, first)
        return b_ref[...] + w[0:1] * u2 + w[1:2] * u1 + w[2:3] * u, (u2, u1, u)

    def wgrad(d, taps):
        z = jnp.zeros((SUBLANES - 3, d.shape[1]), F32)
        return jnp.concatenate([_colsum(d * taps[0]), _colsum(d * taps[1]), _colsum(d * taps[2]), z], axis=0)

    def body(da_ref, ug_ref, hg_ref, uv_ref, hv_ref, wg_ref, wv_ref, bg_ref, bv_ref,
             d_ref, dwg_ref, dwv_ref, dbg_ref, dbv_ref):
        first = pl.program_id(1) == 0
        g, tg = half(ug_ref, hg_ref, wg_ref, bg_ref, first)
        v, tv = half(uv_ref, hv_ref, wv_ref, bv_ref, first)
        da = da_ref[...].astype(F32)
        sg = jax.nn.sigmoid(g)
        dg = da * v * (sg * (1.0 + g * (1.0 - sg)))
        dv = da * (g * sg)
        d_ref[0] = dg.astype(BF16)
        d_ref[1] = dv.astype(BF16)
        _acc(dwg_ref, wgrad(dg, tg), first)
        _acc(dwv_ref, wgrad(dv, tv), first)
        _acc(dbg_ref, _colsum(dg), first)
        _acc(dbv_ref, _colsum(dv), first)

    blk = lambda off: pl.BlockSpec((tm, tn), lambda j, i: (i, j + off))
    halo = lambda off: pl.BlockSpec((HALO, tn), lambda j, i: (jnp.maximum(i * (tm // HALO) - 1, 0), j + off))
    wsp = lambda off: pl.BlockSpec((SUBLANES, tn), lambda j, i: (0, j + off))
    bsp = lambda off: pl.BlockSpec((1, tn), lambda j, i: (0, j + off))
    f = jax.ShapeDtypeStruct
    outs = pl.pallas_call(
        body, name="conv_bwd_a", grid=(nj, S // tm),
        in_specs=[pl.BlockSpec((tm, tn), lambda j, i: (i, j)), blk(0), halo(0), blk(nj), halo(nj), wsp(0), wsp(nj), bsp(0), bsp(nj)],
        out_specs=[pl.BlockSpec((2, tm, tn), lambda j, i: (0, i, j)),
                   pl.BlockSpec((SUBLANES, tn), lambda j, i: (0, j)), pl.BlockSpec((SUBLANES, tn), lambda j, i: (0, j)),
                   pl.BlockSpec((1, tn), lambda j, i: (0, j)), pl.BlockSpec((1, tn), lambda j, i: (0, j))],
        out_shape=[f((2, S, D_FF), BF16), f((SUBLANES, D_FF), F32), f((SUBLANES, D_FF), F32),
                   f((1, D_FF), F32), f((1, D_FF), F32)],
        compiler_params=_cparams(("parallel", "arbitrary")))(dact, u, u, u, u, conv_w8, conv_w8, conv_b, conv_b)
    return outs


def _conv_bwd_b(duc, conv_w8, tm=256, tn=CONV_TN):
    _, S, W = duc.shape
    nj = W // tn
    n_rows = S // tm

    def body(d_ref, h_ref, w_ref, o_ref):
        last = pl.program_id(0) == n_rows - 1
        d = d_ref[...].astype(F32)
        h = h_ref[...].astype(F32)
        w = w_ref[...]
        o_ref[...] = (w[2:3] * d + w[1:2] * _shift_up(d, h, 1, last) + w[0:1] * _shift_up(d, h, 2, last)).astype(BF16)

    last_tile = S // HALO - 1
    return pl.pallas_call(
        body, name="conv_bwd_b", grid=(n_rows, 2 * nj),
        in_specs=[pl.BlockSpec((None, tm, tn), lambda i, j: (j // nj, i, j % nj)),
                  pl.BlockSpec((None, HALO, tn), lambda i, j: (j // nj, jnp.minimum((i + 1) * (tm // HALO), last_tile), j % nj)),
                  pl.BlockSpec((SUBLANES, tn), lambda i, j: (0, j))],
        out_specs=pl.BlockSpec((tm, tn), lambda i, j: (i, j)),
        out_shape=jax.ShapeDtypeStruct((S, 2 * W), BF16),
        compiler_params=_cparams(("parallel", "parallel")))(duc, duc, conv_w8)


ATT_SCALE = HEAD ** -0.5
NEG = -1e30
ATT_PAIRS = ATT_HEADS // 2


def _att_rows(n, d, S):
    per = S // (QBLK * d)
    r, m = n // per, n % per
    cur = pl.ds(m * (QBLK * d) + r, QBLK, stride=d)
    prv = pl.ds(jnp.maximum(m - 1, 0) * (QBLK * d) + r, QBLK, stride=d)
    return cur, prv, m > 0


def _att_slab(g, j):
    return (C_ATT + g * 3 * ATT_W + j * ATT_W) // LANES


def _heads(x):
    return x[:, 0:HEAD], x[:, HEAD:2 * HEAD]


ATT_NB = 4


def _stack(tiles):
    return jnp.concatenate([t[None] for t in tiles], axis=0)


def _att_operands(i, d, S, *sources):
    rows, has = [], []
    tiles = [[] for _ in sources]
    for bb in range(ATT_NB):
        cur, prv, has_prev = _att_rows(i * ATT_NB + bb, d, S)
        rows.append((cur, prv))
        has.append(has_prev)
        for t, (ref, use_cur) in zip(tiles, sources):
            t += _heads(ref[cur if use_cur else prv, :].astype(BF16))
    return rows, has, [_stack(t) for t in tiles]


def _att_mask(s_c, s_p, has_prev):
    qi = lax.broadcasted_iota(jnp.int32, (QBLK, QBLK), 0)
    kj = lax.broadcasted_iota(jnp.int32, (QBLK, QBLK), 1)
    s_c = jnp.where(kj <= qi, s_c * ATT_SCALE, NEG)
    s_p = jnp.where(jnp.logical_and(kj >= qi, has_prev), s_p * ATT_SCALE, NEG)
    return s_c, s_p


def _att_fwd(P, g):
    S = P.shape[0]
    d = ATT_PATTERNS[g][1]

    def body(q_ref, k_ref, v_ref, o_ref, l_ref):
        def group(i, carry):
            rows, has, (q, kc, kp, vc, vp) = _att_operands(i, d, S, (q_ref, True), (k_ref, True), (k_ref, False),
                                                           (v_ref, True), (v_ref, False))
            s_c_all, s_p_all = _dot16(q, kc, "nt"), _dot16(q, kp, "nt")
            p_c, p_p, den, lse = [], [], [], []
            for e in range(2 * ATT_NB):
                s_c, s_p = _att_mask(s_c_all[e], s_p_all[e], has[e // 2])
                m = jnp.maximum(jnp.max(s_c, axis=1, keepdims=True), jnp.max(s_p, axis=1, keepdims=True))
                pc, pp = jnp.exp(s_c - m), jnp.exp(s_p - m)
                den.append(jnp.sum(pc, axis=1, keepdims=True) + jnp.sum(pp, axis=1, keepdims=True))
                lse.append(jnp.broadcast_to(m + jnp.log(den[e]), (QBLK, HEAD)))
                p_c.append(pc)
                p_p.append(pp)
            num = _dot16(_stack(p_c), vc, "nn") + _dot16(_stack(p_p), vp, "nn")
            for bb, (cur, _) in enumerate(rows):
                o_ref[cur, :] = jnp.concatenate([num[2 * bb] / den[2 * bb], num[2 * bb + 1] / den[2 * bb + 1]], axis=1)
                l_ref[cur, :] = jnp.concatenate(lse[2 * bb:2 * bb + 2], axis=1)
            return carry

        lax.fori_loop(0, S // QBLK // ATT_NB, group, 0)

    slab = lambda j: pl.BlockSpec((S, LANES), lambda i: (0, _att_slab(g, j) + i))
    out = pl.BlockSpec((S, LANES), lambda i: (0, i))
    shp = jax.ShapeDtypeStruct((S, ATT_W), F32)
    return pl.pallas_call(body, name=f"att_fwd_g{g}", grid=(ATT_PAIRS,), in_specs=[slab(0), slab(1), slab(2)],
                          out_specs=[out, out], out_shape=[shp, shp], compiler_params=_cparams(("parallel",)))(P, P, P)


def _att_bwd(P, o, l, do, dl, g):
    S = P.shape[0]
    d = ATT_PATTERNS[g][1]

    def body(q_ref, k_ref, v_ref, o_ref, l_ref, do_ref, dl_ref, dq_ref, dk_ref, dv_ref, dq_acc, dk_acc, dv_acc):
        dk_acc[...] = jnp.zeros_like(dk_acc)
        dv_acc[...] = jnp.zeros_like(dv_acc)

        def group(i, carry):
            rows, has, (q, kc, kp, vc, vp, dob) = _att_operands(
                i, d, S, (q_ref, True), (k_ref, True), (k_ref, False), (v_ref, True), (v_ref, False), (do_ref, True))
            s_c_all, s_p_all = _dot16(q, kc, "nt"), _dot16(q, kp, "nt")
            dp_c_all, dp_p_all = _dot16(dob, vc, "nt"), _dot16(dob, vp, "nt")
            p_c, p_p, ds_c, ds_p = [], [], [], []
            for bb, (cur, _) in enumerate(rows):
                dd2 = do_ref[cur, :] * o_ref[cur, :] - dl_ref[cur, :]
                for h, (dd, lse) in enumerate(zip(_heads(dd2), _heads(l_ref[cur, :]))):
                    e = 2 * bb + h
                    s_c, s_p = _att_mask(s_c_all[e], s_p_all[e], has[bb])
                    pc, pp = jnp.exp(s_c - lse[:, 0:1]), jnp.exp(s_p - lse[:, 0:1])
                    delta = jnp.sum(dd, axis=1, keepdims=True)
                    p_c.append(pc)
                    p_p.append(pp)
                    ds_c.append(pc * (dp_c_all[e] - delta) * ATT_SCALE)
                    ds_p.append(pp * (dp_p_all[e] - delta) * ATT_SCALE)
            p_c, p_p, ds_c, ds_p = map(_stack, (p_c, p_p, ds_c, ds_p))
            dq = _dot16(ds_c, kc, "nn") + _dot16(ds_p, kp, "nn")
            dk_c, dk_p = _dot16(ds_c, q, "tn"), _dot16(ds_p, q, "tn")
            dv_c, dv_p = _dot16(p_c, dob, "tn"), _dot16(p_p, dob, "tn")
            pair = lambda x, bb: jnp.concatenate([x[2 * bb], x[2 * bb + 1]], axis=1)
            for bb, (cur, prv) in enumerate(rows):
                dq_acc[cur, :] = pair(dq, bb)
                dk_acc[cur, :] += pair(dk_c, bb)
                dv_acc[cur, :] += pair(dv_c, bb)
                dk_acc[prv, :] += pair(dk_p, bb)
                dv_acc[prv, :] += pair(dv_p, bb)
            return carry

        lax.fori_loop(0, S // QBLK // ATT_NB, group, 0)
        dq_ref[...] = dq_acc[...].astype(BF16)
        dk_ref[...] = dk_acc[...].astype(BF16)
        dv_ref[...] = dv_acc[...].astype(BF16)

    slab = lambda j: pl.BlockSpec((S, LANES), lambda i: (0, _att_slab(g, j) + i))
    blk128 = pl.BlockSpec((S, LANES), lambda i: (0, i))
    shp = jax.ShapeDtypeStruct((S, ATT_W), BF16)
    return pl.pallas_call(body, name=f"att_bwd_g{g}", grid=(ATT_PAIRS,),
                          in_specs=[slab(0), slab(1), slab(2)] + [blk128] * 4, out_specs=[blk128] * 3, out_shape=[shp] * 3,
                          scratch_shapes=[pltpu.VMEM((S, LANES), F32)] * 3,
                          compiler_params=_cparams(("parallel",)))(P, P, P, o, l, do, dl)


def _att_weights(l_refs):
    l0, l1, l2 = [r[...] for r in l_refs]
    m = jnp.maximum(jnp.maximum(l0, l1), l2)
    e = (jnp.exp(l0 - m), jnp.exp(l1 - m), jnp.exp(l2 - m))
    inv = 1.0 / (e[0] + e[1] + e[2])
    return [x * inv for x in e]


def _att_combine_fwd(os, ls, tm=512):
    S = os[0].shape[0]

    def body(o0, o1, o2, l0, l1, l2, a_ref):
        w = _att_weights((l0, l1, l2))
        a_ref[...] = (w[0] * o0[...] + w[1] * o1[...] + w[2] * o2[...]).astype(BF16)

    row = _rows(tm, ATT_W)
    return pl.pallas_call(body, name="att_combine_fwd", grid=(S // tm,), in_specs=[row] * 6, out_specs=row,
                          out_shape=jax.ShapeDtypeStruct((S, ATT_W), BF16),
                          compiler_params=_cparams(("parallel",)))(*os, *ls)


def _att_combine_bwd(da, os, ls, tm=512):
    S = da.shape[0]

    def body(da_ref, o0, o1, o2, l0, l1, l2, *out_refs):
        da = da_ref[...]
        w = _att_weights((l0, l1, l2))
        dw = (da * o0[...], da * o1[...], da * o2[...])
        mean = w[0] * dw[0] + w[1] * dw[1] + w[2] * dw[2]
        for g in range(3):
            out_refs[g][...] = w[g] * da
            out_refs[3 + g][...] = w[g] * (dw[g] - mean)

    row = _rows(tm, ATT_W)
    shp = jax.ShapeDtypeStruct((S, ATT_W), F32)
    return pl.pallas_call(body, name="att_combine_bwd", grid=(S // tm,), in_specs=[row] * 7, out_specs=[row] * 6,
                          out_shape=[shp] * 6, compiler_params=_cparams(("parallel",)))(da, *os, *ls)


@jax.custom_vjp
def _bdot(a, b):
    return jnp.dot(a.astype(BF16), b.astype(BF16), preferred_element_type=F32)


def _bdot_fwd(a, b):
    return _bdot(a, b), (a, b)


def _bdot_bwd(res, ct):
    a, b = res
    ct16 = ct.astype(BF16)
    da = lax.dot_general(ct16, b.astype(BF16), (((1,), (1,)), ((), ())), preferred_element_type=F32)
    db = lax.dot_general(a.astype(BF16), ct16, (((0,), (0,)), ((), ())), preferred_element_type=F32)
    return da, db


_bdot.defvjp(_bdot_fwd, _bdot_bwd)


def _two_piece_dot(x, m):
    hi = x.astype(BF16)
    lo = (x - hi.astype(F32)).astype(BF16)
    return jnp.dot(hi, m, preferred_element_type=F32) + jnp.dot(lo, m, preferred_element_type=F32)


def _head_sum_impl(x):
    sel = (lax.broadcasted_iota(jnp.int32, (D, LANES), 0) // HEAD == lax.broadcasted_iota(jnp.int32, (D, LANES), 1)).astype(BF16)
    sel_t = (lax.broadcasted_iota(jnp.int32, (LANES, D), 1) // HEAD == lax.broadcasted_iota(jnp.int32, (LANES, D), 0)).astype(BF16)
    return _two_piece_dot(_two_piece_dot(x, sel), sel_t)


@jax.custom_vjp
def _head_sum(x):
    return _head_sum_impl(x)


_head_sum.defvjp(lambda x: (_head_sum_impl(x), None), lambda _, ct: (_head_sum_impl(ct),))


def _softplus(z):
    return jnp.maximum(z, 0.0) + jnp.log(1.0 + jnp.exp(-jnp.abs(z)))


def _rwkv_prep_fn(zr, zrp, zk, zkp, zv, zvp, zl, zlp, mu_r, mu_k, mu_v, mu_l, w0, a0, k_k, k_a, w2, a2, g2p):
    r = zr + (zrp - zr) * mu_r
    k = zk + (zkp - zk) * mu_k
    v = zv + (zvp - zv) * mu_v
    lo = zl + (zlp - zl) * mu_l
    w_low, a_low, g_low = lo[:, 0:LORA_W], lo[:, LORA_W:LORA_W + LORA_A], lo[:, LANES:LANES + G_PAD]
    w_log = -_softplus(-(w0 + _bdot(jnp.tanh(w_low), w2))) - 0.5
    decay = -jnp.exp(w_log)
    a = jax.nn.sigmoid(a0 + _bdot(a_low, a2))
    g = _bdot(jax.nn.sigmoid(g_low), g2p)
    kmod = k * (1.0 + (a - 1.0) * k_a)
    kk = k * k_k
    kk = kk / jnp.maximum(jnp.sqrt(_head_sum(kk * kk)), 1e-12)
    return r, decay, kmod, v, -kk, kk * a, g


def _rwkv_prep_specs(tm, blk=lambda i: i):
    vec = _full((1, D))
    rows = lambda w, col: pl.BlockSpec((tm, w), lambda i: (blk(i), col))
    prev = lambda w, col: pl.BlockSpec((SUBLANES, w), lambda i: (jnp.maximum(blk(i) * (tm // SUBLANES) - 1, 0), col))
    slabs = []
    for col in (C_R // D, C_K // D, C_V // D):
        slabs += [rows(D, col), prev(D, col)]
    slabs += [rows(LORA_PAD, C_LORA // LORA_PAD), prev(LORA_PAD, C_LORA // LORA_PAD)]
    params = [vec, vec, vec, _full((1, LORA_PAD)), vec, vec, vec, vec,
              _full((LORA_W, D)), _full((LORA_A, D)), _full((G_PAD, D))]
    return slabs, params


def _prep_inputs(refs, first):
    vals = []
    for s in range(4):
        z = refs[2 * s][...]
        vals += [z, _shift_down(z, refs[2 * s + 1][...], 1, first)]
    return vals + [r[...] for r in refs[8:19]]


def _rwkv_prep(P, params, tm=256):
    S = P.shape[0]
    slabs, pspecs = _rwkv_prep_specs(tm)

    def body(*refs):
        outs = _rwkv_prep_fn(*_prep_inputs(refs, pl.program_id(0) == 0))
        for o_ref, val in zip(refs[19:], outs):
            o_ref[...] = val

    shp = jax.ShapeDtypeStruct((S, D), F32)
    return pl.pallas_call(body, name="rwkv_prep", grid=(S // tm,), in_specs=slabs + pspecs,
                          out_specs=[_rows(tm, D)] * 7, out_shape=[shp] * 7,
                          compiler_params=_cparams(("parallel",)))(*([P] * 8), *params)


def _rwkv_prep_bwd(P, params, cts_a, cts_b, tm=128):
    S = P.shape[0]
    nblk = S // tm
    blk = lambda i: nblk - 1 - i
    slabs, pspecs = _rwkv_prep_specs(tm, blk)
    has_b = [c is not None for c in cts_b]
    n_ct = 7 + sum(has_b)

    def body(*refs):
        start = pl.program_id(0) == 0
        ins = _prep_inputs(refs, pl.program_id(0) == nblk - 1)
        ct_refs = refs[19:19 + n_ct]
        out_refs = refs[19 + n_ct:19 + n_ct + 15]
        carry_refs = refs[19 + n_ct + 15:]

        @pl.when(start)
        def _():
            for c_ref in carry_refs:
                c_ref[...] = jnp.zeros_like(c_ref)

        cts, pos = [], 7
        for i in range(7):
            c = ct_refs[i][...]
            if has_b[i]:
                c = c + ct_refs[pos][...]
                pos += 1
            cts.append(c)
        _, vjp = jax.vjp(_rwkv_prep_fn, *ins)
        grads = vjp(tuple(cts))
        for s in range(4):
            shifted = grads[2 * s + 1]
            out_refs[s][...] = (grads[2 * s] + _shift_up(shifted, carry_refs[s][...], 1, start)).astype(BF16)
            carry_refs[s][0:1, :] = shifted[0:1, :]
        for i in range(11):
            _acc(out_refs[4 + i], grads[8 + i], start)

    ct_in = list(cts_a) + [c for c in cts_b if c is not None]
    row = lambda w: pl.BlockSpec((tm, w), lambda i: (blk(i), 0))
    f = jax.ShapeDtypeStruct
    zshapes = [f((S, D), BF16)] * 3 + [f((S, LORA_PAD), BF16)]
    pshapes = [f((1, D), F32)] * 3 + [f((1, LORA_PAD), F32)] + [f((1, D), F32)] * 4 + [f((LORA_W, D), F32), f((LORA_A, D), F32), f((G_PAD, D), F32)]
    return pl.pallas_call(
        body, name="rwkv_prep_bwd", grid=(nblk,),
        in_specs=slabs + pspecs + [row(D)] * n_ct,
        out_specs=[row(D), row(D), row(D), row(LORA_PAD)] + pspecs,
        out_shape=zshapes + pshapes,
        scratch_shapes=[pltpu.VMEM((SUBLANES, D), F32)] * 3 + [pltpu.VMEM((SUBLANES, LORA_PAD), F32)],
        compiler_params=_cparams(("arbitrary",)))(*([P] * 8), *params, *ct_in)


def _rwkv_post_fn(y, r, kmod, v, g, lnx_w, lnx_b, r_k):
    mean = _head_sum(y) * (1.0 / HEAD)
    yc = y - mean
    var = _head_sum(yc * yc) * (1.0 / HEAD)
    yn = yc * lax.rsqrt(var + GN_EPS) * lnx_w + lnx_b
    bonus = _head_sum(r * kmod * r_k) * v
    return (yn + bonus) * g


def _rwkv_post(y, r, kmod, v, g, lnx_w, lnx_b, r_k, tm=256):
    S = y.shape[0]

    def body(y_ref, r_ref, k_ref, v_ref, g_ref, w_ref, b_ref, rk_ref, o_ref):
        o_ref[...] = _rwkv_post_fn(y_ref[...], r_ref[...], k_ref[...], v_ref[...], g_ref[...],
                                   w_ref[...], b_ref[...], rk_ref[...]).astype(BF16)

    row, vec = _rows(tm, D), _full((1, D))
    return pl.pallas_call(body, name="rwkv_post", grid=(S // tm,), in_specs=[row] * 5 + [vec] * 3, out_specs=row,
                          out_shape=jax.ShapeDtypeStruct((S, D), BF16),
                          compiler_params=_cparams(("parallel",)))(y, r, kmod, v, g, lnx_w, lnx_b, r_k)


def _rwkv_post_bwd(drw, y, r, kmod, v, g, lnx_w, lnx_b, r_k, tm=256):
    S = y.shape[0]

    def body(d_ref, y_ref, r_ref, k_ref, v_ref, g_ref, w_ref, b_ref, rk_ref, *out_refs):
        first = pl.program_id(0) == 0
        _, vjp = jax.vjp(_rwkv_post_fn, y_ref[...], r_ref[...], k_ref[...], v_ref[...], g_ref[...],
                         w_ref[...], b_ref[...], rk_ref[...])
        grads = vjp(d_ref[...])
        for i in range(5):
            out_refs[i][...] = grads[i]
        for i in range(5, 8):
            _acc(out_refs[i], grads[i], first)

    row, vec = _rows(tm, D), _full((1, D))
    f = jax.ShapeDtypeStruct
    return pl.pallas_call(body, name="rwkv_post_bwd", grid=(S // tm,), in_specs=[row] * 6 + [vec] * 3,
                          out_specs=[row] * 5 + [vec] * 3, out_shape=[f((S, D), F32)] * 5 + [f((1, D), F32)] * 3,
                          compiler_params=_cparams(("arbitrary",)))(drw, y, r, kmod, v, g, lnx_w, lnx_b, r_k)


CHUNK = 64
CHUNK_TB = 256
_DOT_DIMS = {"nn": (((2,), (1,)), ((0,), (0,))), "nt": (((2,), (2,)), ((0,), (0,))), "tn": (((1,), (1,)), ((0,), (0,)))}


def _dot16(x, y, mode):
    return lax.dot_general(x.astype(BF16), y.astype(BF16), _DOT_DIMS[mode], preferred_element_type=F32)


@functools.partial(jax.custom_vjp, nondiff_argnums=(2,))
def _mm16(x, y, mode):
    return _dot16(x, y, mode)


def _mm16_fwd(x, y, mode):
    return _dot16(x, y, mode), (x, y)


def _mm16_bwd(mode, res, ct):
    x, y = res
    if mode == "nn":
        return _dot16(ct, y, "nt"), _dot16(x, ct, "tn")
    if mode == "nt":
        return _dot16(ct, y, "nn"), _dot16(ct, x, "tn")
    return _dot16(y, ct, "nt"), _dot16(x, ct, "nn")


_mm16.defvjp(_mm16_fwd, _mm16_bwd)


def _tri_sum(x, upper):
    T = x.shape[0]
    i = lax.broadcasted_iota(jnp.int32, (T, T), 0)
    j = lax.broadcasted_iota(jnp.int32, (T, T), 1)
    tri = ((j >= i) if upper else (i >= j)).astype(BF16)
    out, rest = None, x
    for _ in range(3):
        piece = rest.astype(BF16)
        rest = rest - piece.astype(F32)
        part = jnp.dot(tri, piece, preferred_element_type=F32)
        out = part if out is None else out + part
    return out


@jax.custom_vjp
def _cumsum_rows(x):
    return _tri_sum(x, False)


_cumsum_rows.defvjp(lambda x: (_tri_sum(x, False), None), lambda _, ct: (_tri_sum(ct, True),))


def _rows_to_cols(row):
    per_head = jnp.concatenate([row[:, h * HEAD:(h + 1) * HEAD] for h in range(N_HEADS)], axis=0)
    eye = (lax.broadcasted_iota(jnp.int32, (HEAD, HEAD), 0) == lax.broadcasted_iota(jnp.int32, (HEAD, HEAD), 1)).astype(F32)
    cols = lax.dot_general(eye, per_head, (((1,), (1,)), ((), ())), precision=lax.Precision.HIGHEST,
                           preferred_element_type=F32)
    return jnp.concatenate([cols[:, h:h + 1][None] for h in range(N_HEADS)], axis=0)


def _per_head(x):
    return jnp.concatenate([x[:, h * HEAD:(h + 1) * HEAD][None] for h in range(N_HEADS)], axis=0)


def _chunk_fn(st0, r, lw, k, v, a, b):
    T = r.shape[0]
    cl = _cumsum_rows(lw)
    cl_end = cl[T - 1:T, :]
    inv = jnp.exp(-cl)
    to_end = jnp.exp(cl_end - cl)
    ah, rh, bh, kh, be, ke, v3 = [_per_head(x) for x in
                                  (a * jnp.exp(cl - lw), r * jnp.exp(cl), b * inv, k * inv, b * to_end, k * to_end, v)]
    i = lax.broadcasted_iota(jnp.int32, (N_HEADS, T, T), 1)
    j = lax.broadcasted_iota(jnp.int32, (N_HEADS, T, T), 2)
    a_ab = jnp.where(i > j, _mm16(ah, bh, "nt"), 0.0)
    a_ak = jnp.where(i > j, _mm16(ah, kh, "nt"), 0.0)
    m_rb = jnp.where(i >= j, _mm16(rh, bh, "nt"), 0.0)
    m_rk = jnp.where(i >= j, _mm16(rh, kh, "nt"), 0.0)
    rhs = _mm16(ah, st0, "nn") + _mm16(a_ak, v3, "nn")
    power, solve, n = a_ab, (i == j).astype(F32) + a_ab, 1
    while 2 * n < T:
        power = _mm16(power, power, "nn")
        solve = solve + _mm16(solve, power, "nn")
        n *= 2
    sa = _mm16(solve, rhs, "nn")
    y3 = _mm16(rh, st0, "nn") + _mm16(m_rb, sa, "nn") + _mm16(m_rk, v3, "nn")
    st_end = _rows_to_cols(jnp.exp(cl_end)) * st0 + _mm16(be, sa, "tn") + _mm16(ke, v3, "tn")
    return jnp.concatenate([y3[h] for h in range(N_HEADS)], axis=1), st_end


def _hosted_exchange(refs, n, broadcast, grid):
    if n == 0:
        return lambda: None
    start, wait = _exchange_ops(refs[:n], refs[n:2 * n], *refs[2 * n:], broadcast)
    first = functools.reduce(jnp.logical_and, [pl.program_id(a) == 0 for a in range(len(grid))])
    last = functools.reduce(jnp.logical_and, [pl.program_id(a) == g - 1 for a, g in enumerate(grid)])
    pl.when(first)(start)
    return lambda: pl.when(last)(wait)


def _cscan_fwd(r, lw, k, v, a, b, gather=()):
    S = r.shape[0]
    per_blk = CHUNK_TB // CHUNK
    n_x = len(gather)
    nblk = S // CHUNK_TB

    def body(*refs):
        r_ref, lw_ref, k_ref, v_ref, a_ref, b_ref = refs[:6]
        y_ref, ck_ref = refs[6 + n_x:8 + n_x]
        st_ref = refs[8 + 2 * n_x]
        finish = _hosted_exchange(refs[6:6 + n_x] + refs[8 + n_x:8 + 2 * n_x] + refs[9 + 2 * n_x:], n_x, True, (nblk,))

        @pl.when(pl.program_id(0) == 0)
        def _():
            st_ref[...] = jnp.zeros_like(st_ref)

        def chunk(c, carry):
            rows = pl.ds(pl.multiple_of(c * CHUNK, CHUNK), CHUNK)
            st0 = st_ref[...]
            ck_ref[c] = st0
            y, st_end = _chunk_fn(st0, r_ref[rows, :], lw_ref[rows, :], k_ref[rows, :],
                                  v_ref[rows, :], a_ref[rows, :], b_ref[rows, :])
            y_ref[rows, :] = y
            st_ref[...] = st_end
            return carry

        lax.fori_loop(0, per_blk, chunk, 0)
        finish()

    blk = _rows(CHUNK_TB, D)
    any_spec = pl.BlockSpec(memory_space=pl.ANY)
    outs = pl.pallas_call(
        body, name="scan_fwd", grid=(nblk,), in_specs=[blk] * 6 + [any_spec] * n_x,
        out_specs=[blk, pl.BlockSpec((per_blk, N_HEADS, HEAD, HEAD), lambda i: (i, 0, 0, 0))] + [any_spec] * n_x,
        out_shape=[jax.ShapeDtypeStruct((S, D), F32), jax.ShapeDtypeStruct((S // CHUNK, N_HEADS, HEAD, HEAD), F32)]
        + _exchange_shapes(gather, True),
        scratch_shapes=[pltpu.VMEM((N_HEADS, HEAD, HEAD), F32)] + (_exchange_scratch(n_x) if n_x else []),
        compiler_params=_cparams(("arbitrary",)))(r, lw, k, v, a, b, *gather)
    return outs[0], outs[1], outs[2:]


def _cscan_bwd(r, lw, k, v, a, b, ckpt, dy, scatter=()):
    S = r.shape[0]
    per_blk = CHUNK_TB // CHUNK
    nblk = S // CHUNK_TB
    n_x = len(scatter)

    def body(*refs):
        r_ref, lw_ref, k_ref, v_ref, a_ref, b_ref, ck_ref, dy_ref = refs[:8]
        out_refs = refs[8 + n_x:14 + n_x]
        ds_ref = refs[14 + 2 * n_x]
        finish = _hosted_exchange(refs[8:8 + n_x] + refs[14 + n_x:14 + 2 * n_x] + refs[15 + 2 * n_x:], n_x, False, (nblk,))

        @pl.when(pl.program_id(0) == 0)
        def _():
            ds_ref[...] = jnp.zeros_like(ds_ref)

        def chunk(cc, carry):
            c = per_blk - 1 - cc
            rows = pl.ds(pl.multiple_of(c * CHUNK, CHUNK), CHUNK)
            ins = (ck_ref[c], r_ref[rows, :], lw_ref[rows, :], k_ref[rows, :], v_ref[rows, :], a_ref[rows, :], b_ref[rows, :])
            _, vjp = jax.vjp(_chunk_fn, *ins)
            grads = vjp((dy_ref[rows, :], ds_ref[...]))
            ds_ref[...] = grads[0]
            for o_ref, g in zip(out_refs, grads[1:]):
                o_ref[rows, :] = g
            return carry

        lax.fori_loop(0, per_blk, chunk, 0)
        finish()

    blk = pl.BlockSpec((CHUNK_TB, D), lambda i: (nblk - 1 - i, 0))
    any_spec = pl.BlockSpec(memory_space=pl.ANY)
    shp = jax.ShapeDtypeStruct((S, D), F32)
    outs = pl.pallas_call(
        body, name="scan_bwd", grid=(nblk,),
        in_specs=[blk] * 6 + [pl.BlockSpec((per_blk, N_HEADS, HEAD, HEAD), lambda i: (nblk - 1 - i, 0, 0, 0)), blk]
        + [any_spec] * n_x,
        out_specs=[blk] * 6 + [any_spec] * n_x, out_shape=[shp] * 6 + _exchange_shapes(scatter, False),
        scratch_shapes=[pltpu.VMEM((N_HEADS, HEAD, HEAD), F32)] + (_exchange_scratch(n_x) if n_x else []),
        compiler_params=_cparams(("arbitrary",)))(r, lw, k, v, a, b, ckpt, dy, *scatter)
    return outs[:6], outs[6:]


def _ada_partial(c_all, w_shard):
    def body(c_ref, w_ref, o_ref):
        o_ref[...] = jnp.dot(c_ref[...].astype(BF16), w_ref[...].astype(BF16), preferred_element_type=F32)

    vm = pl.BlockSpec(memory_space=pltpu.VMEM)
    return pl.pallas_call(body, name="ada_partial", in_specs=[vm, vm], out_specs=vm,
                          out_shape=jax.ShapeDtypeStruct((N_DEV, w_shard.shape[1]), F32),
                          compiler_params=pltpu.CompilerParams(vmem_limit_bytes=VMEM_LIMIT))(c_all, w_shard)


def _ada_bias(rows, b_ada):
    def body(r_ref, b_ref, o_ref):
        o_ref[...] = r_ref[...] + b_ref[...]

    vm = pl.BlockSpec(memory_space=pltpu.VMEM)
    return pl.pallas_call(body, name="ada_bias", in_specs=[vm, vm], out_specs=vm,
                          out_shape=jax.ShapeDtypeStruct(rows.shape, F32))(rows, b_ada)


def _ada_wgrad(c_cols, d_all):
    def body(c_ref, d_ref, o_ref):
        acc = c_ref[:, 0:1] * d_ref[0:1, :]
        for j in range(1, N_DEV):
            acc = acc + c_ref[:, j:j + 1] * d_ref[j:j + 1, :]
        o_ref[...] = acc

    vm = pl.BlockSpec(memory_space=pltpu.VMEM)
    return pl.pallas_call(body, name="ada_wgrad", in_specs=[vm, vm], out_specs=vm,
                          out_shape=jax.ShapeDtypeStruct((D, d_all.shape[1]), F32),
                          compiler_params=pltpu.CompilerParams(vmem_limit_bytes=VMEM_LIMIT))(c_cols, d_all)


def _exchange(srcs, broadcast, name):
    n = len(srcs)

    def body(*refs):
        start, wait = _exchange_ops(refs[:n], refs[n:2 * n], *refs[2 * n:], broadcast)
        start()
        wait()

    any_spec = pl.BlockSpec(memory_space=pl.ANY)
    return pl.pallas_call(
        body, name=name, out_shape=_exchange_shapes(srcs, broadcast), in_specs=[any_spec] * n, out_specs=[any_spec] * n,
        scratch_shapes=_exchange_scratch(n),
        compiler_params=pltpu.CompilerParams(has_side_effects=True),
    )(*srcs)


def _gather_via_sibling(srcs, name):
    n = len(srcs)

    def body(*refs):
        src_refs, out_refs = refs[:n], refs[n:2 * n]
        send_sems, recv_sems, local_sems = refs[2 * n:]
        x, y, c = lax.axis_index("x"), lax.axis_index("y"), lax.axis_index("c")
        me, sibling = (x, y, c), (x, y, 1 - c)
        chips = [(1 - x, y), (x, 1 - y), (1 - x, 1 - y)]

        def slot(px, py, pc):
            return 4 * px + 2 * py + pc

        def copy(i, k, block, to, src=None):
            rows = out_refs[i].at[slot(*block)]
            return pltpu.make_async_remote_copy(
                src_ref=rows if src is None else src, dst_ref=rows, send_sem=send_sems.at[i, k],
                recv_sem=recv_sems.at[i, k], device_id=to, device_id_type=_MESH)

        local = [pltpu.make_async_copy(src_refs[i], out_refs[i].at[slot(*me)], local_sems.at[i]) for i in range(n)]
        for cp in local:
            cp.start()
        first = [copy(i, 0, me, sibling, src=src_refs[i]) for i in range(n)]
        first += [copy(i, 1 + j, me, (*chip, c), src=src_refs[i]) for j, chip in enumerate(chips) for i in range(n)]
        for cp in first:
            cp.start()
        passed = []
        for j, chip in enumerate(chips):
            for i in range(n):
                copy(i, 1 + j, (*chip, c), me).wait_recv()
                passed.append(copy(i, 4 + j, (*chip, c), sibling))
                passed[-1].start()
        for i in range(n):
            copy(i, 0, sibling, me).wait_recv()
            for j, chip in enumerate(chips):
                copy(i, 4 + j, (*chip, 1 - c), me).wait_recv()
        for cp in first + passed:
            cp.wait_send()
        for cp in local:
            cp.wait()

    any_spec = pl.BlockSpec(memory_space=pl.ANY)
    return pl.pallas_call(
        body, name=name, out_shape=_exchange_shapes(srcs, True), in_specs=[any_spec] * n, out_specs=[any_spec] * n,
        scratch_shapes=_exchange_scratch(n),
        compiler_params=pltpu.CompilerParams(has_side_effects=True),
    )(*srcs)


def _flags(broadcast, n):
    return [broadcast] * n if isinstance(broadcast, bool) else list(broadcast)


def _exchange_shapes(srcs, broadcast):
    return [jax.ShapeDtypeStruct((N_DEV,) + (s.shape if bc else s.shape[1:]), s.dtype)
            for s, bc in zip(srcs, _flags(broadcast, len(srcs)))]


def _exchange_scratch(n):
    return [pltpu.SemaphoreType.DMA((n, N_DEV)), pltpu.SemaphoreType.DMA((n, N_DEV)), pltpu.SemaphoreType.DMA((n,))]


def _exchange_ops(src_refs, out_refs, send_sems, recv_sems, local_sems, broadcast):
    n = len(src_refs)
    flags = _flags(broadcast, n)
    x, y, c = lax.axis_index("x"), lax.axis_index("y"), lax.axis_index("c")
    me = 4 * x + 2 * y + c

    def block(i, j):
        return src_refs[i] if flags[i] else src_refs[i].at[j]

    def remote(i, d, src_slot, dst_slot):
        px, py, pc = x ^ (d >> 2), y ^ ((d >> 1) & 1), c ^ (d & 1)
        return pltpu.make_async_remote_copy(
            src_ref=block(i, src_slot), dst_ref=out_refs[i].at[dst_slot], send_sem=send_sems.at[i, d],
            recv_sem=recv_sems.at[i, d], device_id=(px, py, pc), device_id_type=_MESH)

    def local(i):
        return pltpu.make_async_copy(block(i, me), out_refs[i].at[me], local_sems.at[i])

    def start():
        for i in range(n):
            local(i).start()
        for d in range(1, N_DEV):
            for i in range(n):
                remote(i, d, me ^ d, me).start()

    def wait():
        for d in range(1, N_DEV):
            for i in range(n):
                remote(i, d, me, me ^ d).wait_recv()
        for d in range(1, N_DEV):
            for i in range(n):
                remote(i, d, me ^ d, me).wait_send()
        for i in range(n):
            local(i).wait()

    return start, wait


def _adamw(w, g, m, v):
    nm = ADAM_B1 * m + (1.0 - ADAM_B1) * g
    nv = ADAM_B2 * v + (1.0 - ADAM_B2) * (g * g)
    m_hat = nm * (1.0 / (1.0 - ADAM_B1 ** ADAM_STEP))
    v_hat = nv * (1.0 / (1.0 - ADAM_B2 ** ADAM_STEP))
    return -ADAM_LR * (m_hat / (jnp.sqrt(v_hat) + ADAM_EPS) + ADAM_WD * w), nm, nv


def _adam_vectors(parts, ws, ms, vs):
    nv = len(ws)
    sizes = [w.shape[1] for w in ws]

    def body(*refs):
        p_ref = refs[0]
        w_refs, m_refs, v_refs = refs[1:1 + nv], refs[1 + nv:1 + 2 * nv], refs[1 + 2 * nv:1 + 3 * nv]
        out_refs = refs[1 + 3 * nv:]
        g_all = p_ref[0]
        for j in range(1, N_DEV):
            g_all = g_all + p_ref[j]
        off = 0
        for i, n in enumerate(sizes):
            g = g_all[:, off:off + n]
            off += -(-n // LANES) * LANES
            delta, new_m, new_v = _adamw(w_refs[i][...], g, m_refs[i][...], v_refs[i][...])
            for o_ref, val in zip(out_refs[4 * i:4 * i + 4], (g, delta, new_m, new_v)):
                o_ref[...] = val

    vm = pl.BlockSpec(memory_space=pltpu.VMEM)
    outs = pl.pallas_call(body, name="adam_replicated", in_specs=[vm] * (1 + 3 * nv), out_specs=[vm] * (4 * nv),
                          out_shape=[jax.ShapeDtypeStruct((1, n), F32) for n in sizes for _ in range(4)])(parts, *ws, *ms, *vs)
    return [outs[4 * i:4 * i + 4] for i in range(nv)]


def _sum_adam(parts, w, m, v, name):
    n_parts, R, C = parts.shape
    fits = [t for t in range(16, R + 1, 16) if R % t == 0 and t * C <= 2504 * LANES]
    if fits:
        tm, tc = max(fits), C
    elif C % (2 * LANES) == 0 and R * C > 2504 * LANES:
        tm, tc = R, 2 * LANES
    else:
        tm, tc = R, C

    def body(p_ref, w_ref, m_ref, v_ref, g_ref, d_ref, nm_ref, nv_ref):
        g = p_ref[0].astype(F32)
        for j in range(1, n_parts):
            g = g + p_ref[j].astype(F32)
        g_ref[...] = g
        d_ref[...], nm_ref[...], nv_ref[...] = _adamw(w_ref[...], g, m_ref[...], v_ref[...])

    blk = pl.BlockSpec((tm, tc), lambda i, j: (i, j))
    shp = jax.ShapeDtypeStruct((R, C), F32)
    return pl.pallas_call(body, name=name, grid=(R // tm, C // tc),
                          in_specs=[pl.BlockSpec((n_parts, tm, tc), lambda i, j: (0, i, j)), blk, blk, blk],
                          out_specs=[blk] * 4, out_shape=[shp] * 4,
                          compiler_params=_cparams(("parallel", "parallel")))(parts, w, m, v)


TRANSPOSED = ("w_in", "w_up")
SHARDED = (("w_ada", 1), ("w_in", 0), ("w2", 1), ("a2", 1), ("g2", 1), ("w_att_out", 1), ("w_rwkv_out", 0),
           ("w_o", 0), ("w_up", 0), ("conv_w", 1), ("w_down", 0))
EARLY, LATE = SHARDED[1:5], SHARDED[5:]
REPLICATED = ("b_ada", "norm1_w", "b_gate", "mu_shift", "w0", "a0", "k_k", "k_a", "r_k", "lnx_w", "lnx_b",
              "norm2_w", "conv_b", "norm_f_w")
WEIGHTS = ("w_ada", "b_ada", "norm1_w", "w_in", "b_gate", "mu_shift", "w0", "w2", "a0", "a2", "g2", "k_k", "k_a", "r_k",
           "lnx_w", "lnx_b", "w_att_out", "w_rwkv_out", "w_o", "norm2_w", "w_up", "conv_w", "conv_b", "w_down", "norm_f_w")


W_IN_RUNS = ((0, C_ATT, ATT_IN), (ATT_IN, C_R, 3 * D), (ATT_IN + 3 * D, C_LORA, LORA_W + LORA_A),
             (ATT_IN + 3 * D + LORA_W + LORA_A, C_LORA + LANES, LORA_G), (ATT_IN + RWKV_IN, C_GA, 2 * D))
W_IN_SHARD = N_IN // N_DEV


def _pad_w_in(w_in_t):
    pieces = [w_in_t[orig:orig + count] for orig, _, count in sorted(W_IN_RUNS, key=lambda run: run[1])]
    pieces.append(jnp.zeros((LORA_PAD - LANES - LORA_G, w_in_t.shape[1]), w_in_t.dtype))
    return jnp.concatenate(pieces, axis=0)


def _w_in_blocks(g):
    blocks = []
    for j in range(N_DEV):
        pieces = []
        for orig, pad, count in W_IN_RUNS:
            lo, hi = max(orig, j * W_IN_SHARD), min(orig + count, (j + 1) * W_IN_SHARD)
            if lo < hi:
                pieces.append(g[pad + lo - orig:pad + hi - orig])
        blocks.append(jnp.concatenate(pieces, axis=0)[None])
    return jnp.concatenate(blocks, axis=0)


def _pad_mu(mu):
    lo = mu[:, 3 * D:]
    mu_l = jnp.concatenate([lo[:, :LORA_W + LORA_A], lo[:, LORA_W + LORA_A:], jnp.zeros((1, LORA_PAD - LANES - LORA_G), mu.dtype)], axis=1)
    return mu[:, :D], mu[:, D:2 * D], mu[:, 2 * D:3 * D], mu_l


def _local_step(x, ada, W, late_shards, target):
    S = x.shape[0]
    W = dict(W)
    G = {}
    sh1, sc1, gt1, sh2, sc2, gt2 = [ada[:, i * D:(i + 1) * D] for i in range(6)]
    h1, rstd1 = _norm_fwd(x, None, None, W["norm1_w"], sc1, sh1, "norm1_fwd")
    w_in_p = _pad_w_in(W["w_in"])
    P = _mm(h1, w_in_p, "nt", F32, "proj_in")

    mu_r, mu_k, mu_v, mu_l = _pad_mu(W["mu_shift"])
    g2p = jnp.pad(W["g2"], ((0, G_PAD - LORA_G), (0, 0)))
    prep_params = [mu_r, mu_k, mu_v, mu_l, W["w0"], W["a0"], W["k_k"], W["k_a"], W["w2"], W["a2"], g2p]
    r_, dec, kmod, v_, aa, bb, gg = _rwkv_prep(P, prep_params)
    y_scan, states, late = _cscan_fwd(r_, dec, kmod, v_, aa, bb, gather=late_shards)
    W.update({n: _full_weight(g, axis) for (n, axis), g in zip(LATE, late)})

    o_g, l_g = zip(*[_att_fwd(P, g) for g in range(len(ATT_PATTERNS))])
    att = _att_combine_fwd(o_g, l_g)
    y_att = _mm(att, W["w_att_out"], "nn", F32, "att_out")
    r_k = W["r_k"].reshape(1, D)
    rw = _rwkv_post(y_scan, r_, kmod, v_, gg, W["lnx_w"], W["lnx_b"], r_k)
    y_rwkv = _mm(rw, W["w_rwkv_out"], "nn", F32, "rwkv_out")

    bga, bgr = W["b_gate"][:, :D], W["b_gate"][:, D:]
    mix = _gate_fwd(P, bga, bgr, y_att, y_rwkv)
    mo = _mm(mix, W["w_o"], "nn", F32, "mix_out")
    x2, h2, rstd2 = _norm_fwd(x, mo, gt1, W["norm2_w"], sc2, sh2, "norm2_fwd")
    u = _mm(h2, W["w_up"], "nt", BF16, "ffn_up")
    conv_w8 = jnp.pad(W["conv_w"], ((0, SUBLANES - 3), (0, 0)))
    act = _conv_fwd(u, conv_w8, W["conv_b"])
    f = _mm(act, W["w_down"], "nn", F32, "ffn_down")
    loss_blk, dx3, df, dgt2, G["norm_f_w"] = _final(x2, f, gt2, W["norm_f_w"], target)
    loss = loss_blk[0, 0]

    dact = _mm(df, W["w_down"], "nt", BF16, "ffn_down_dx")
    G["w_down"] = _mm(act, df, "tn", BF16, "ffn_down_dw")
    duc, dwg, dwv, dbg, dbv = _conv_bwd_a(dact, u, conv_w8, W["conv_b"])
    G["conv_w"] = jnp.concatenate([dwg[0:3], dwv[0:3]], axis=1)
    G["conv_b"] = jnp.concatenate([dbg, dbv], axis=1)
    du = _conv_bwd_b(duc, conv_w8)
    dh2 = _mm(du, W["w_up"], "nn", F32, "ffn_up_dx")
    G["w_up"] = _mm(du, h2, "tn", BF16, "ffn_up_dw")
    dx2, dsh2, dsc2, G["norm2_w"], dmo, dgt1 = _norm_bwd(dh2, x2, rstd2, W["norm2_w"], sc2, dx3, mo, gt1, "norm2_bwd")
    dmix = _mm(dmo, W["w_o"], "nt", F32, "mix_out_dx")
    G["w_o"] = _mm(mix, dmo, "tn", BF16, "mix_out_dw")
    dy_att, dy_rwkv, dpga, dpgr, dbga, dbgr = _gate_bwd(dmix, P, bga, bgr, y_att, y_rwkv)
    G["b_gate"] = jnp.concatenate([dbga, dbgr], axis=1)

    datt = _mm(dy_att, W["w_att_out"], "nt", F32, "att_out_dx")
    G["w_att_out"] = _mm(att, dy_att, "tn", BF16, "att_out_dw")
    dcomb = _att_combine_bwd(datt, o_g, l_g)
    dp_att = []
    for g in range(len(ATT_PATTERNS)):
        dp_att += _att_bwd(P, o_g[g], l_g[g], dcomb[g], dcomb[3 + g], g)

    drw = _mm(dy_rwkv, W["w_rwkv_out"], "nt", F32, "rwkv_out_dx")
    G["w_rwkv_out"] = _mm(rw, dy_rwkv, "tn", BF16, "rwkv_out_dw")
    dy_scan, dr1, dk1, dv1, dgg, G["lnx_w"], G["lnx_b"], drk = _rwkv_post_bwd(drw, y_scan, r_, kmod, v_, gg, W["lnx_w"], W["lnx_b"], r_k)
    G["r_k"] = drk.reshape(W["r_k"].shape)
    late_blocks = [_owner_blocks(G[n], axis) for n, axis in LATE] if late_shards else []
    (dr2, ddec, dk2, dv2, daa, dbb), late_parts = _cscan_bwd(r_, dec, kmod, v_, aa, bb, states, dy_scan, scatter=late_blocks)
    pb = _rwkv_prep_bwd(P, prep_params, [dr2, ddec, dk2, dv2, daa, dbb, dgg], [dr1, None, dk1, dv1, None, None, None])
    dp_rkv, dp_lora, dpar = list(pb[0:3]), pb[3], pb[4:]
    dmu_r, dmu_k, dmu_v, dmu_l, G["w0"], G["a0"], G["k_k"], G["k_a"], G["w2"], G["a2"], dg2p = dpar
    G["g2"] = dg2p[0:LORA_G]
    G["mu_shift"] = jnp.concatenate([dmu_r, dmu_k, dmu_v, dmu_l[:, :LORA_W + LORA_A], dmu_l[:, LANES:LANES + LORA_G]], axis=1)

    dP = jnp.concatenate(dp_rkv + [dpga, dpgr] + dp_att + [dp_lora], axis=1)
    G["w_in"] = _w_in_blocks(_mm(dP, h1, "tn", BF16, "proj_in_dw"))
    if late_shards:
        dh1, (w_in_parts,) = _mm(dP, w_in_p, "nn", F32, "proj_in_dx", scatter=[G["w_in"]])
        done = dict(zip([n for n, _ in LATE] + ["w_in"], list(late_parts) + [w_in_parts]))
    else:
        dh1, done = _mm(dP, w_in_p, "nn", F32, "proj_in_dx"), {}
    grad_x, dsh1, dsc1, G["norm1_w"] = _norm_bwd(dh1, x, rstd1, W["norm1_w"], sc1, dx2, None, None, "norm1_bwd")
    dada = jnp.concatenate([dsh1, dsc1, dgt1, dsh2, dsc2, dgt2], axis=1)
    G["b_ada"] = dada
    return loss, grad_x, G, done


def _full_weight(gathered, axis):
    _, rows, cols = gathered.shape
    if axis == 0:
        return gathered.reshape(N_DEV * rows, cols)
    return gathered.transpose(1, 0, 2).reshape(rows, N_DEV * cols)


def _owner_blocks(g, axis):
    rows, cols = g.shape
    g = g.astype(BF16)
    if axis == 0:
        return g.reshape(N_DEV, rows // N_DEV, cols)
    return g.reshape(rows, N_DEV, cols // N_DEV).transpose(1, 0, 2)


def kernel(x, c, w_ada, b_ada, norm1_w, w_in, b_gate, mu_shift, w0, w2, a0, a2, g2, k_k, k_a, r_k, lnx_w, lnx_b, w_att_out, w_rwkv_out, w_o, norm2_w, w_up, conv_w, conv_b, w_down, norm_f_w, loss_target, m_w_ada, m_b_ada, m_norm1_w, m_w_in, m_b_gate, m_mu_shift, m_w0, m_w2, m_a0, m_a2, m_g2, m_k_k, m_k_a, m_r_k, m_lnx_w, m_lnx_b, m_w_att_out, m_w_rwkv_out, m_w_o, m_norm2_w, m_w_up, m_conv_w, m_conv_b, m_w_down, m_norm_f_w, v_w_ada, v_b_ada, v_norm1_w, v_w_in, v_b_gate, v_mu_shift, v_w0, v_w2, v_a0, v_a2, v_g2, v_k_k, v_k_a, v_r_k, v_lnx_w, v_lnx_b, v_w_att_out, v_w_rwkv_out, v_w_o, v_norm2_w, v_w_up, v_conv_w, v_conv_b, v_w_down, v_norm_f_w):
    env = dict(locals())
    w_shard = {n: env[n] for n in WEIGHTS}
    m_shard = {n: env["m_" + n] for n in WEIGHTS}
    v_shard = {n: env["v_" + n] for n in WEIGHTS}

    def mat(shards, n):
        return jnp.swapaxes(shards[n][0], 0, 1) if n in TRANSPOSED else shards[n][0]

    c_all, *gathered = _gather_via_sibling([c] + [mat(w_shard, n).astype(BF16) for n, _ in EARLY], "gather_weights")
    c_all = c_all.reshape(N_DEV, D)
    W = {n: _full_weight(g, axis) for (n, axis), g in zip(EARLY, gathered)}
    for n in REPLICATED:
        W[n] = w_shard[n].reshape(1, -1) if n != "r_k" else w_shard[n][0]
    ada_cols = _ada_partial(c_all, w_shard["w_ada"][0])
    ada_rows, = _exchange([ada_cols[:, None, :]], False, "ada_rows")
    ada = _ada_bias(ada_rows.reshape(1, -1), w_shard["b_ada"])

    late_shards = [mat(w_shard, n).astype(BF16) for n, _ in LATE]
    loss, grad_x, G, parts = _local_step(x[0], ada, W, late_shards, loss_target[0])
    loss = lax.psum(loss, ("x", "y", "c"))

    row = lambda a: a.reshape(1, -1)
    small = jnp.concatenate([jnp.pad(row(G[n]), ((0, 0), (0, (-G[n].size) % LANES))) for n in REPLICATED], axis=1)
    sparts, dada_all = _exchange([small, G["b_ada"].reshape(N_DEV, 1, -1)], [True, False], "gather_small_grads")
    parts["w_ada"] = _ada_wgrad(c_all.T, dada_all.reshape(N_DEV, -1))[None]

    rest = [(n, axis) for n, axis in SHARDED if n not in parts]
    parts.update(zip([n for n, _ in rest], _exchange([_owner_blocks(G[n], axis) for n, axis in rest], False, "scatter_grads")))
    out = {}
    for n, p in parts.items():
        res = _sum_adam(p, mat(w_shard, n), mat(m_shard, n), mat(v_shard, n), "adam_" + n)
        if n in TRANSPOSED:
            res = [jnp.swapaxes(a, 0, 1) for a in res]
        for kind, a in zip(("grad", "delta", "new_m", "new_v"), res):
            out[kind, n] = a[None]

    res = _adam_vectors(sparts, *[[row(s[n]) for n in REPLICATED] for s in (w_shard, m_shard, v_shard)])
    for n, four in zip(REPLICATED, res):
        for kind, a in zip(("grad", "delta", "new_m", "new_v"), four):
            out[kind, n] = a.reshape(w_shard[n].shape)

    return (loss, grad_x[None], *[out[kind, n] for kind in ("grad", "delta", "new_m", "new_v") for n in WEIGHTS])
```

```python
import functools

import jax
import jax.numpy as jnp
from jax import lax
from jax.experimental import pallas as pl
from jax.experimental.pallas import tpu as pltpu

F32 = jnp.float32
BF16 = jnp.bfloat16

D = 1024
HEAD = 64
ATT_PATTERNS = ((128, 1), (512, 4), (2048, 16))
ATT_HEADS = 8
ATT_W = ATT_HEADS * HEAD
ATT_IN = 3 * 3 * ATT_W
QBLK = 128
N_HEADS = D // HEAD
LORA_W, LORA_A, LORA_G = 64, 64, 160
RWKV_IN = 3 * D + LORA_W + LORA_A + LORA_G
N_IN = ATT_IN + RWKV_IN + 2 * D
D_FF = 2816
RMS_EPS = 1e-6
GN_EPS = 64e-5
N_DEV = 8
LANES = 128
SUBLANES = 8

C_R, C_K, C_V, C_GA, C_GR = 0, 1024, 2048, 3072, 4096
C_ATT = 5120
C_LORA = C_ATT + ATT_IN
LORA_PAD = 512
G_PAD = 256
N_PAD = C_LORA + LORA_PAD

ADAM_LR, ADAM_B1, ADAM_B2, ADAM_EPS, ADAM_WD, ADAM_STEP = 0.001, 0.9, 0.999, 1e-08, 0.01, 10

VMEM_LIMIT = 56 * 1024 * 1024

_MESH = pl.DeviceIdType.MESH


def _cparams(sem):
    return pltpu.CompilerParams(dimension_semantics=sem, vmem_limit_bytes=VMEM_LIMIT)


def _tile(dim, pref):
    if dim <= pref:
        return dim
    best = None
    for t in range(LANES, pref + 1, LANES):
        if dim % t == 0:
            best = t
    assert best is not None, dim
    return best


MM_TILES = {"nn": (1024, 1408, 2816), "nt": (1024, 2048, 1408), "tn": (1408, 1408, 4096)}


def _mm(a, b, mode, out_dtype, name, scatter=()):
    if mode == "nn":
        (M, K), (K2, N) = a.shape, b.shape
    elif mode == "nt":
        (M, K), (N, K2) = a.shape, b.shape
    else:
        (K, M), (K2, N) = a.shape, b.shape
    assert K == K2, (a.shape, b.shape, mode)
    tm, tn, tk = (_tile(dim, pref) for dim, pref in zip((M, N, K), MM_TILES[mode]))
    nk = K // tk
    grid = (M // tm, N // tn, nk)
    n_x = len(scatter)
    dims = {"nn": (((1,), (0,)), ((), ())), "nt": (((1,), (1,)), ((), ())), "tn": (((0,), (0,)), ((), ()))}[mode]

    def body(*refs):
        a_ref, b_ref = refs[:2]
        o_ref, acc_ref = refs[2 + n_x], refs[3 + 2 * n_x]
        finish = _hosted_exchange(refs[2:2 + n_x] + refs[3 + n_x:3 + 2 * n_x] + refs[4 + 2 * n_x:], n_x, False, grid)
        k = pl.program_id(2)
        part = lax.dot_general(a_ref[...].astype(BF16), b_ref[...].astype(BF16), dims,
                               preferred_element_type=F32)
        if nk == 1:
            o_ref[...] = part.astype(o_ref.dtype)
        else:
            @pl.when(k == 0)
            def _():
                acc_ref[...] = part

            @pl.when(jnp.logical_and(k > 0, k < nk - 1))
            def _():
                acc_ref[...] += part

            @pl.when(k == nk - 1)
            def _():
                o_ref[...] = (acc_ref[...] + part).astype(o_ref.dtype)
        finish()

    a_spec = pl.BlockSpec((tk, tm), lambda i, j, k: (k, i)) if mode == "tn" else pl.BlockSpec((tm, tk), lambda i, j, k: (i, k))
    b_spec = pl.BlockSpec((tn, tk), lambda i, j, k: (j, k)) if mode == "nt" else pl.BlockSpec((tk, tn), lambda i, j, k: (k, j))
    any_spec = pl.BlockSpec(memory_space=pl.ANY)
    outs = pl.pallas_call(
        body, name=name, grid=grid,
        in_specs=[a_spec, b_spec] + [any_spec] * n_x,
        out_specs=[pl.BlockSpec((tm, tn), lambda i, j, k: (i, j))] + [any_spec] * n_x,
        out_shape=[jax.ShapeDtypeStruct((M, N), out_dtype)] + _exchange_shapes(scatter, False),
        scratch_shapes=[pltpu.VMEM((tm, tn) if nk > 1 else (SUBLANES, LANES), F32)] + (_exchange_scratch(n_x) if n_x else []),
        compiler_params=_cparams(("arbitrary",) * 3 if n_x else ("parallel", "parallel", "arbitrary")),
    )(a, b, *scatter)
    return (outs[0], outs[1:]) if n_x else outs[0]


def _rows(tm, w, col=0):
    return pl.BlockSpec((tm, w), lambda i: (i, col))


def _full(shape):
    return pl.BlockSpec(shape, lambda i: (0,) * len(shape))


def _shift_down(x, halo, k, first):
    rolled = pltpu.roll(x, k, 0)
    row = lax.broadcasted_iota(jnp.int32, x.shape, 0)
    out = rolled
    n_halo = halo.shape[0]
    for j in range(k):
        h = jnp.where(first, 0.0, halo[n_halo - k + j:n_halo - k + j + 1, :])
        out = jnp.where(row == j, h, out)
    return out


def _shift_up(x, halo, k, last):
    n = x.shape[0]
    rolled = pltpu.roll(x, n - k, 0)
    row = lax.broadcasted_iota(jnp.int32, x.shape, 0)
    out = rolled
    for j in range(k):
        h = jnp.where(last, 0.0, halo[j:j + 1, :])
        out = jnp.where(row == n - k + j, h, out)
    return out


def _acc(ref, val, first):
    @pl.when(first)
    def _():
        ref[...] = val

    @pl.when(jnp.logical_not(first))
    def _():
        ref[...] += val


def _colsum(x):
    return jnp.sum(x, axis=0, keepdims=True)


def _norm_fwd(x, mo, gt, nw, sc, sh, name, tm=256):
    S = x.shape[0]
    has_res = mo is not None

    def body(*refs):
        if has_res:
            x_ref, mo_ref, gt_ref, nw_ref, sc_ref, sh_ref, x2_ref, h_ref, rs_ref = refs
            x2 = x_ref[...] + gt_ref[...] * mo_ref[...]
            x2_ref[...] = x2
        else:
            x_ref, nw_ref, sc_ref, sh_ref, h_ref, rs_ref = refs
            x2 = x_ref[...]
        rstd = lax.rsqrt(jnp.mean(x2 * x2, axis=-1, keepdims=True) + RMS_EPS)
        rs_ref[...] = rstd
        h_ref[...] = ((x2 * rstd * nw_ref[...]) * (1.0 + sc_ref[...]) + sh_ref[...]).astype(BF16)

    vec = _full((1, D))
    ins = [x, mo, gt, nw, sc, sh] if has_res else [x, nw, sc, sh]
    in_specs = [_rows(tm, D), _rows(tm, D), vec, vec, vec, vec] if has_res else [_rows(tm, D), vec, vec, vec]
    outs = [jax.ShapeDtypeStruct((S, D), BF16), jax.ShapeDtypeStruct((S, 1), F32)]
    out_specs = [_rows(tm, D), _rows(tm, 1)]
    if has_res:
        outs = [jax.ShapeDtypeStruct((S, D), F32)] + outs
        out_specs = [_rows(tm, D)] + out_specs
    return pl.pallas_call(body, name=name, grid=(S // tm,), in_specs=in_specs, out_specs=out_specs,
                          out_shape=outs, compiler_params=_cparams(("parallel",)))(*ins)


def _norm_bwd(dh, xin, rstd, nw, sc, dres, mo, gt, name, tm=256):
    S = xin.shape[0]
    has_res = mo is not None

    def body(*refs):
        if has_res:
            dh_ref, x_ref, rs_ref, nw_ref, sc_ref, dres_ref, mo_ref, gt_ref, dx_ref, dsh_ref, dsc_ref, dnw_ref, dmo_ref, dgt_ref = refs
        else:
            dh_ref, x_ref, rs_ref, nw_ref, sc_ref, dres_ref, dx_ref, dsh_ref, dsc_ref, dnw_ref = refs
        first = pl.program_id(0) == 0
        dh = dh_ref[...]
        rstd = rs_ref[...]
        n = x_ref[...] * rstd
        w = nw_ref[...]
        _acc(dsh_ref, _colsum(dh), first)
        _acc(dsc_ref, _colsum(dh * (n * w)), first)
        dnw = dh * (1.0 + sc_ref[...])
        _acc(dnw_ref, _colsum(dnw * n), first)
        dn = dnw * w
        dx = dres_ref[...] + rstd * (dn - n * jnp.mean(dn * n, axis=-1, keepdims=True))
        dx_ref[...] = dx
        if has_res:
            dmo_ref[...] = (dx * gt_ref[...]).astype(BF16)
            _acc(dgt_ref, _colsum(dx * mo_ref[...]), first)

    vec = _full((1, D))
    vshape = jax.ShapeDtypeStruct((1, D), F32)
    ins = [dh, xin, rstd, nw, sc, dres] + ([mo, gt] if has_res else [])
    in_specs = [_rows(tm, D), _rows(tm, D), _rows(tm, 1), vec, vec, _rows(tm, D)] + ([_rows(tm, D), vec] if has_res else [])
    outs = [jax.ShapeDtypeStruct((S, D), F32), vshape, vshape, vshape]
    out_specs = [_rows(tm, D), vec, vec, vec]
    if has_res:
        outs += [jax.ShapeDtypeStruct((S, D), BF16), vshape]
        out_specs += [_rows(tm, D), vec]
    return pl.pallas_call(body, name=name, grid=(S // tm,), in_specs=in_specs, out_specs=out_specs,
                          out_shape=outs, compiler_params=_cparams(("arbitrary",)))(*ins)


def _final(x2, f, gt2, nfw, target, tm=256):
    S = x2.shape[0]

    def body(x2_ref, f_ref, gt_ref, w_ref, t_ref, loss_ref, dx_ref, df_ref, dgt_ref, dw_ref):
        first = pl.program_id(0) == 0
        f = f_ref[...]
        gt = gt_ref[...]
        w = w_ref[...]
        x3 = x2_ref[...] + gt * f
        rstd = lax.rsqrt(jnp.mean(x3 * x3, axis=-1, keepdims=True) + RMS_EPS)
        n = x3 * rstd
        e = n * w - t_ref[...]
        part = 0.5 * jnp.sum(jnp.mean(e * e, axis=-1, keepdims=True), axis=0, keepdims=True)
        _acc(loss_ref, jnp.broadcast_to(part, (SUBLANES, LANES)), first)
        dy = e * (1.0 / D)
        _acc(dw_ref, _colsum(dy * n), first)
        dn = dy * w
        dx = rstd * (dn - n * jnp.mean(dn * n, axis=-1, keepdims=True))
        dx_ref[...] = dx
        df_ref[...] = (dx * gt).astype(BF16)
        _acc(dgt_ref, _colsum(dx * f), first)

    vec = _full((1, D))
    vshape = jax.ShapeDtypeStruct((1, D), F32)
    return pl.pallas_call(
        body, name="final_loss", grid=(S // tm,),
        in_specs=[_rows(tm, D), _rows(tm, D), vec, vec, _rows(tm, D)],
        out_specs=[_full((SUBLANES, LANES)), _rows(tm, D), _rows(tm, D), vec, vec],
        out_shape=[jax.ShapeDtypeStruct((SUBLANES, LANES), F32), jax.ShapeDtypeStruct((S, D), F32),
                   jax.ShapeDtypeStruct((S, D), BF16), vshape, vshape],
        compiler_params=_cparams(("arbitrary",)))(x2, f, gt2, nfw, target)


def _gate_fwd(P, bga, bgr, y_att, y_rwkv, tm=256):
    S = P.shape[0]

    def body(pa_ref, pr_ref, ba_ref, br_ref, ya_ref, yr_ref, mix_ref):
        ga = jax.nn.sigmoid(pa_ref[...] + ba_ref[...])
        gr = jax.nn.sigmoid(pr_ref[...] + br_ref[...])
        mix_ref[...] = (ga * ya_ref[...] + gr * yr_ref[...]).astype(BF16)

    vec = _full((1, D))
    return pl.pallas_call(
        body, name="gate_fwd", grid=(S // tm,),
        in_specs=[_rows(tm, D, C_GA // D), _rows(tm, D, C_GR // D), vec, vec, _rows(tm, D), _rows(tm, D)],
        out_specs=_rows(tm, D), out_shape=jax.ShapeDtypeStruct((S, D), BF16),
        compiler_params=_cparams(("parallel",)))(P, P, bga, bgr, y_att, y_rwkv)


def _gate_bwd(dmix, P, bga, bgr, y_att, y_rwkv, tm=256):
    S = P.shape[0]

    def body(dm_ref, pa_ref, pr_ref, ba_ref, br_ref, ya_ref, yr_ref, dya_ref, dyr_ref, dpa_ref, dpr_ref, dba_ref, dbr_ref):
        first = pl.program_id(0) == 0
        dm = dm_ref[...]
        ga = jax.nn.sigmoid(pa_ref[...] + ba_ref[...])
        gr = jax.nn.sigmoid(pr_ref[...] + br_ref[...])
        dya_ref[...] = (dm * ga).astype(BF16)
        dyr_ref[...] = (dm * gr).astype(BF16)
        dpa = dm * ya_ref[...] * ga * (1.0 - ga)
        dpr = dm * yr_ref[...] * gr * (1.0 - gr)
        dpa_ref[...] = dpa.astype(BF16)
        dpr_ref[...] = dpr.astype(BF16)
        _acc(dba_ref, _colsum(dpa), first)
        _acc(dbr_ref, _colsum(dpr), first)

    vec = _full((1, D))
    row = _rows(tm, D)
    rshape = jax.ShapeDtypeStruct((S, D), BF16)
    vshape = jax.ShapeDtypeStruct((1, D), F32)
    return pl.pallas_call(
        body, name="gate_bwd", grid=(S // tm,),
        in_specs=[row, _rows(tm, D, C_GA // D), _rows(tm, D, C_GR // D), vec, vec, row, row],
        out_specs=[row, row, row, row, vec, vec],
        out_shape=[rshape, rshape, rshape, rshape, vshape, vshape],
        compiler_params=_cparams(("arbitrary",)))(dmix, P, P, bga, bgr, y_att, y_rwkv)


CONV_TN = D_FF // 2
HALO = 16


def _conv_fwd(u, conv_w8, conv_b, tm=256, tn=CONV_TN):
    S = u.shape[0]
    nj = D_FF // tn

    def conv(u_ref, h_ref, w_ref, b_ref, first):
        u = u_ref[...].astype(F32)
        h = h_ref[...].astype(F32)
        w = w_ref[...]
        return b_ref[...] + w[0:1] * _shift_down(u, h, 2, first) + w[1:2] * _shift_down(u, h, 1, first) + w[2:3] * u

    def body(ug_ref, hg_ref, uv_ref, hv_ref, wg_ref, wv_ref, bg_ref, bv_ref, act_ref):
        first = pl.program_id(0) == 0
        g = conv(ug_ref, hg_ref, wg_ref, bg_ref, first)
        v = conv(uv_ref, hv_ref, wv_ref, bv_ref, first)
        act_ref[...] = (g * jax.nn.sigmoid(g) * v).astype(BF16)

    blk = lambda off: pl.BlockSpec((tm, tn), lambda i, j: (i, j + off))
    halo = lambda off: pl.BlockSpec((HALO, tn), lambda i, j: (jnp.maximum(i * (tm // HALO) - 1, 0), j + off))
    wsp = lambda off: pl.BlockSpec((SUBLANES, tn), lambda i, j: (0, j + off))
    bsp = lambda off: pl.BlockSpec((1, tn), lambda i, j: (0, j + off))
    return pl.pallas_call(
        body, name="conv_fwd", grid=(S // tm, nj),
        in_specs=[blk(0), halo(0), blk(nj), halo(nj), wsp(0), wsp(nj), bsp(0), bsp(nj)],
        out_specs=pl.BlockSpec((tm, tn), lambda i, j: (i, j)),
        out_shape=jax.ShapeDtypeStruct((S, D_FF), BF16),
        compiler_params=_cparams(("parallel", "parallel")))(u, u, u, u, conv_w8, conv_w8, conv_b, conv_b)


def _conv_bwd_a(dact, u, conv_w8, conv_b, tm=256, tn=CONV_TN):
    S = u.shape[0]
    nj = D_FF // tn

    def half(u_ref, h_ref, w_ref, b_ref, first):
        u = u_ref[...].astype(F32)
        h = h_ref[...].astype(F32)
        w = w_ref[...]
        u2, u1 = _shift_down(u, h, 2, first), _shift_down(u, h, 1, first)
        return b_ref[...] + w[0:1] * u2 + w[1:2] * u1 + w[2:3] * u, (u2, u1, u)

    def wgrad(d, taps):
        z = jnp.zeros((SUBLANES - 3, d.shape[1]), F32)
        return jnp.concatenate([_colsum(d * taps[0]), _colsum(d * taps[1]), _colsum(d * taps[2]), z], axis=0)

    def body(da_ref, ug_ref, hg_ref, uv_ref, hv_ref, wg_ref, wv_ref, bg_ref, bv_ref,
             d_ref, dwg_ref, dwv_ref, dbg_ref, dbv_ref):
        first = pl.program_id(1) == 0
        g, tg = half(ug_ref, hg_ref, wg_ref, bg_ref, first)
        v, tv = half(uv_ref, hv_ref, wv_ref, bv_ref, first)
        da = da_ref[...].astype(F32)
        sg = jax.nn.sigmoid(g)
        dg = da * v * (sg * (1.0 + g * (1.0 - sg)))
        dv = da * (g * sg)
        d_ref[0] = dg.astype(BF16)
        d_ref[1] = dv.astype(BF16)
        _acc(dwg_ref, wgrad(dg, tg), first)
        _acc(dwv_ref, wgrad(dv, tv), first)
        _acc(dbg_ref, _colsum(dg), first)
        _acc(dbv_ref, _colsum(dv), first)

    blk = lambda off: pl.BlockSpec((tm, tn), lambda j, i: (i, j + off))
    halo = lambda off: pl.BlockSpec((HALO, tn), lambda j, i: (jnp.maximum(i * (tm // HALO) - 1, 0), j + off))
    wsp = lambda off: pl.BlockSpec((SUBLANES, tn), lambda j, i: (0, j + off))
    bsp = lambda off: pl.BlockSpec((1, tn), lambda j, i: (0, j + off))
    f = jax.ShapeDtypeStruct
    outs = pl.pallas_call(
        body, name="conv_bwd_a", grid=(nj, S // tm),
        in_specs=[pl.BlockSpec((tm, tn), lambda j, i: (i, j)), blk(0), halo(0), blk(nj), halo(nj), wsp(0), wsp(nj), bsp(0), bsp(nj)],
        out_specs=[pl.BlockSpec((2, tm, tn), lambda j, i: (0, i, j)),
                   pl.BlockSpec((SUBLANES, tn), lambda j, i: (0, j)), pl.BlockSpec((SUBLANES, tn), lambda j, i: (0, j)),
                   pl.BlockSpec((1, tn), lambda j, i: (0, j)), pl.BlockSpec((1, tn), lambda j, i: (0, j))],
        out_shape=[f((2, S, D_FF), BF16), f((SUBLANES, D_FF), F32), f((SUBLANES, D_FF), F32),
                   f((1, D_FF), F32), f((1, D_FF), F32)],
        compiler_params=_cparams(("parallel", "arbitrary")))(dact, u, u, u, u, conv_w8, conv_w8, conv_b, conv_b)
    return outs


def _conv_bwd_b(duc, conv_w8, tm=256, tn=CONV_TN):
    _, S, W = duc.shape
    nj = W // tn
    n_rows = S // tm

    def body(d_ref, h_ref, w_ref, o_ref):
        last = pl.program_id(0) == n_rows - 1
        d = d_ref[...].astype(F32)
        h = h_ref[...].astype(F32)
        w = w_ref[...]
        o_ref[...] = (w[2:3] * d + w[1:2] * _shift_up(d, h, 1, last) + w[0:1] * _shift_up(d, h, 2, last)).astype(BF16)

    last_tile = S // HALO - 1
    return pl.pallas_call(
        body, name="conv_bwd_b", grid=(n_rows, 2 * nj),
        in_specs=[pl.BlockSpec((None, tm, tn), lambda i, j: (j // nj, i, j % nj)),
                  pl.BlockSpec((None, HALO, tn), lambda i, j: (j // nj, jnp.minimum((i + 1) * (tm // HALO), last_tile), j % nj)),
                  pl.BlockSpec((SUBLANES, tn), lambda i, j: (0, j))],
        out_specs=pl.BlockSpec((tm, tn), lambda i, j: (i, j)),
        out_shape=jax.ShapeDtypeStruct((S, 2 * W), BF16),
        compiler_params=_cparams(("parallel", "parallel")))(duc, duc, conv_w8)


ATT_SCALE = HEAD ** -0.5
NEG = -1e30
ATT_PAIRS = ATT_HEADS // 2


def _att_rows(n, d, S):
    per = S // (QBLK * d)
    r, m = n // per, n % per
    cur = pl.ds(m * (QBLK * d) + r, QBLK, stride=d)
    prv = pl.ds(jnp.maximum(m - 1, 0) * (QBLK * d) + r, QBLK, stride=d)
    return cur, prv, m > 0


def _att_slab(g, j):
    return (C_ATT + g * 3 * ATT_W + j * ATT_W) // LANES


def _heads(x):
    return x[:, 0:HEAD], x[:, HEAD:2 * HEAD]


ATT_NB = 4


def _stack(tiles):
    return jnp.concatenate([t[None] for t in tiles], axis=0)


def _att_operands(i, d, S, *sources):
    rows, has = [], []
    tiles = [[] for _ in sources]
    for bb in range(ATT_NB):
        cur, prv, has_prev = _att_rows(i * ATT_NB + bb, d, S)
        rows.append((cur, prv))
        has.append(has_prev)
        for t, (ref, use_cur) in zip(tiles, sources):
            t += _heads(ref[cur if use_cur else prv, :].astype(BF16))
    return rows, has, [_stack(t) for t in tiles]


def _att_mask(s_c, s_p, has_prev):
    qi = lax.broadcasted_iota(jnp.int32, (QBLK, QBLK), 0)
    kj = lax.broadcasted_iota(jnp.int32, (QBLK, QBLK), 1)
    s_c = jnp.where(kj <= qi, s_c * ATT_SCALE, NEG)
    s_p = jnp.where(jnp.logical_and(kj >= qi, has_prev), s_p * ATT_SCALE, NEG)
    return s_c, s_p


def _att_fwd(P, g):
    S = P.shape[0]
    d = ATT_PATTERNS[g][1]

    def body(q_ref, k_ref, v_ref, o_ref, l_ref):
        def group(i, carry):
            rows, has, (q, kc, kp, vc, vp) = _att_operands(i, d, S, (q_ref, True), (k_ref, True), (k_ref, False),
                                                           (v_ref, True), (v_ref, False))
            s_c_all, s_p_all = _dot16(q, kc, "nt"), _dot16(q, kp, "nt")
            p_c, p_p, den, lse = [], [], [], []
            for e in range(2 * ATT_NB):
                s_c, s_p = _att_mask(s_c_all[e], s_p_all[e], has[e // 2])
                m = jnp.maximum(jnp.max(s_c, axis=1, keepdims=True), jnp.max(s_p, axis=1, keepdims=True))
                pc, pp = jnp.exp(s_c - m), jnp.exp(s_p - m)
                den.append(jnp.sum(pc, axis=1, keepdims=True) + jnp.sum(pp, axis=1, keepdims=True))
                lse.append(jnp.broadcast_to(m + jnp.log(den[e]), (QBLK, HEAD)))
                p_c.append(pc)
                p_p.append(pp)
            num = _dot16(_stack(p_c), vc, "nn") + _dot16(_stack(p_p), vp, "nn")
            for bb, (cur, _) in enumerate(rows):
                o_ref[cur, :] = jnp.concatenate([num[2 * bb] / den[2 * bb], num[2 * bb + 1] / den[2 * bb + 1]], axis=1)
                l_ref[cur, :] = jnp.concatenate(lse[2 * bb:2 * bb + 2], axis=1)
            return carry

        lax.fori_loop(0, S // QBLK // ATT_NB, group, 0)

    slab = lambda j: pl.BlockSpec((S, LANES), lambda i: (0, _att_slab(g, j) + i))
    out = pl.BlockSpec((S, LANES), lambda i: (0, i))
    shp = jax.ShapeDtypeStruct((S, ATT_W), F32)
    return pl.pallas_call(body, name=f"att_fwd_g{g}", grid=(ATT_PAIRS,), in_specs=[slab(0), slab(1), slab(2)],
                          out_specs=[out, out], out_shape=[shp, shp], compiler_params=_cparams(("parallel",)))(P, P, P)


def _att_bwd(P, o, l, do, dl, g):
    S = P.shape[0]
    d = ATT_PATTERNS[g][1]

    def body(q_ref, k_ref, v_ref, o_ref, l_ref, do_ref, dl_ref, dq_ref, dk_ref, dv_ref, dq_acc, dk_acc, dv_acc):
        dk_acc[...] = jnp.zeros_like(dk_acc)
        dv_acc[...] = jnp.zeros_like(dv_acc)

        def group(i, carry):
            rows, has, (q, kc, kp, vc, vp, dob) = _att_operands(
                i, d, S, (q_ref, True), (k_ref, True), (k_ref, False), (v_ref, True), (v_ref, False), (do_ref, True))
            s_c_all, s_p_all = _dot16(q, kc, "nt"), _dot16(q, kp, "nt")
            dp_c_all, dp_p_all = _dot16(dob, vc, "nt"), _dot16(dob, vp, "nt")
            p_c, p_p, ds_c, ds_p = [], [], [], []
            for bb, (cur, _) in enumerate(rows):
                dd2 = do_ref[cur, :] * o_ref[cur, :] - dl_ref[cur, :]
                for h, (dd, lse) in enumerate(zip(_heads(dd2), _heads(l_ref[cur, :]))):
                    e = 2 * bb + h
                    s_c, s_p = _att_mask(s_c_all[e], s_p_all[e], has[bb])
                    pc, pp = jnp.exp(s_c - lse[:, 0:1]), jnp.exp(s_p - lse[:, 0:1])
                    delta = jnp.sum(dd, axis=1, keepdims=True)
                    p_c.append(pc)
                    p_p.append(pp)
                    ds_c.append(pc * (dp_c_all[e] - delta) * ATT_SCALE)
                    ds_p.append(pp * (dp_p_all[e] - delta) * ATT_SCALE)
            p_c, p_p, ds_c, ds_p = map(_stack, (p_c, p_p, ds_c, ds_p))
            dq = _dot16(ds_c, kc, "nn") + _dot16(ds_p, kp, "nn")
            dk_c, dk_p = _dot16(ds_c, q, "tn"), _dot16(ds_p, q, "tn")
            dv_c, dv_p = _dot16(p_c, dob, "tn"), _dot16(p_p, dob, "tn")
            pair = lambda x, bb: jnp.concatenate([x[2 * bb], x[2 * bb + 1]], axis=1)
            for bb, (cur, prv) in enumerate(rows):
                dq_acc[cur, :] = pair(dq, bb)
                dk_acc[cur, :] += pair(dk_c, bb)
                dv_acc[cur, :] += pair(dv_c, bb)
                dk_acc[prv, :] += pair(dk_p, bb)
                dv_acc[prv, :] += pair(dv_p, bb)
            return carry

        lax.fori_loop(0, S // QBLK // ATT_NB, group, 0)
        dq_ref[...] = dq_acc[...].astype(BF16)
        dk_ref[...] = dk_acc[...].astype(BF16)
        dv_ref[...] = dv_acc[...].astype(BF16)

    slab = lambda j: pl.BlockSpec((S, LANES), lambda i: (0, _att_slab(g, j) + i))
    blk128 = pl.BlockSpec((S, LANES), lambda i: (0, i))
    shp = jax.ShapeDtypeStruct((S, ATT_W), BF16)
    return pl.pallas_call(body, name=f"att_bwd_g{g}", grid=(ATT_PAIRS,),
                          in_specs=[slab(0), slab(1), slab(2)] + [blk128] * 4, out_specs=[blk128] * 3, out_shape=[shp] * 3,
                          scratch_shapes=[pltpu.VMEM((S, LANES), F32)] * 3,
                          compiler_params=_cparams(("parallel",)))(P, P, P, o, l, do, dl)


def _att_weights(l_refs):
    l0, l1, l2 = [r[...] for r in l_refs]
    m = jnp.maximum(jnp.maximum(l0, l1), l2)
    e = (jnp.exp(l0 - m), jnp.exp(l1 - m), jnp.exp(l2 - m))
    inv = 1.0 / (e[0] + e[1] + e[2])
    return [x * inv for x in e]


def _att_combine_fwd(os, ls, tm=512):
    S = os[0].shape[0]

    def body(o0, o1, o2, l0, l1, l2, a_ref):
        w = _att_weights((l0, l1, l2))
        a_ref[...] = (w[0] * o0[...] + w[1] * o1[...] + w[2] * o2[...]).astype(BF16)

    row = _rows(tm, ATT_W)
    return pl.pallas_call(body, name="att_combine_fwd", grid=(S // tm,), in_specs=[row] * 6, out_specs=row,
                          out_shape=jax.ShapeDtypeStruct((S, ATT_W), BF16),
                          compiler_params=_cparams(("parallel",)))(*os, *ls)


def _att_combine_bwd(da, os, ls, tm=512):
    S = da.shape[0]

    def body(da_ref, o0, o1, o2, l0, l1, l2, *out_refs):
        da = da_ref[...]
        w = _att_weights((l0, l1, l2))
        dw = (da * o0[...], da * o1[...], da * o2[...])
        mean = w[0] * dw[0] + w[1] * dw[1] + w[2] * dw[2]
        for g in range(3):
            out_refs[g][...] = w[g] * da
            out_refs[3 + g][...] = w[g] * (dw[g] - mean)

    row = _rows(tm, ATT_W)
    shp = jax.ShapeDtypeStruct((S, ATT_W), F32)
    return pl.pallas_call(body, name="att_combine_bwd", grid=(S // tm,), in_specs=[row] * 7, out_specs=[row] * 6,
                          out_shape=[shp] * 6, compiler_params=_cparams(("parallel",)))(da, *os, *ls)


@jax.custom_vjp
def _bdot(a, b):
    return jnp.dot(a.astype(BF16), b.astype(BF16), preferred_element_type=F32)


def _bdot_fwd(a, b):
    return _bdot(a, b), (a, b)


def _bdot_bwd(res, ct):
    a, b = res
    ct16 = ct.astype(BF16)
    da = lax.dot_general(ct16, b.astype(BF16), (((1,), (1,)), ((), ())), preferred_element_type=F32)
    db = lax.dot_general(a.astype(BF16), ct16, (((0,), (0,)), ((), ())), preferred_element_type=F32)
    return da, db


_bdot.defvjp(_bdot_fwd, _bdot_bwd)


def _two_piece_dot(x, m):
    hi = x.astype(BF16)
    lo = (x - hi.astype(F32)).astype(BF16)
    return jnp.dot(hi, m, preferred_element_type=F32) + jnp.dot(lo, m, preferred_element_type=F32)


def _head_sum_impl(x):
    sel = (lax.broadcasted_iota(jnp.int32, (D, LANES), 0) // HEAD == lax.broadcasted_iota(jnp.int32, (D, LANES), 1)).astype(BF16)
    sel_t = (lax.broadcasted_iota(jnp.int32, (LANES, D), 1) // HEAD == lax.broadcasted_iota(jnp.int32, (LANES, D), 0)).astype(BF16)
    return _two_piece_dot(_two_piece_dot(x, sel), sel_t)


@jax.custom_vjp
def _head_sum(x):
    return _head_sum_impl(x)


_head_sum.defvjp(lambda x: (_head_sum_impl(x), None), lambda _, ct: (_head_sum_impl(ct),))


def _softplus(z):
    return jnp.maximum(z, 0.0) + jnp.log(1.0 + jnp.exp(-jnp.abs(z)))


def _rwkv_prep_fn(zr, zrp, zk, zkp, zv, zvp, zl, zlp, mu_r, mu_k, mu_v, mu_l, w0, a0, k_k, k_a, w2, a2, g2p):
    r = zr + (zrp - zr) * mu_r
    k = zk + (zkp - zk) * mu_k
    v = zv + (zvp - zv) * mu_v
    lo = zl + (zlp - zl) * mu_l
    w_low, a_low, g_low = lo[:, 0:LORA_W], lo[:, LORA_W:LORA_W + LORA_A], lo[:, LANES:LANES + G_PAD]
    w_log = -_softplus(-(w0 + _bdot(jnp.tanh(w_low), w2))) - 0.5
    decay = -jnp.exp(w_log)
    a = jax.nn.sigmoid(a0 + _bdot(a_low, a2))
    g = _bdot(jax.nn.sigmoid(g_low), g2p)
    kmod = k * (1.0 + (a - 1.0) * k_a)
    kk = k * k_k
    kk = kk / jnp.maximum(jnp.sqrt(_head_sum(kk * kk)), 1e-12)
    return r, decay, kmod, v, -kk, kk * a, g


def _rwkv_prep_specs(tm, blk=lambda i: i):
    vec = _full((1, D))
    rows = lambda w, col: pl.BlockSpec((tm, w), lambda i: (blk(i), col))
    prev = lambda w, col: pl.BlockSpec((SUBLANES, w), lambda i: (jnp.maximum(blk(i) * (tm // SUBLANES) - 1, 0), col))
    slabs = []
    for col in (C_R // D, C_K // D, C_V // D):
        slabs += [rows(D, col), prev(D, col)]
    slabs += [rows(LORA_PAD, C_LORA // LORA_PAD), prev(LORA_PAD, C_LORA // LORA_PAD)]
    params = [vec, vec, vec, _full((1, LORA_PAD)), vec, vec, vec, vec,
              _full((LORA_W, D)), _full((LORA_A, D)), _full((G_PAD, D))]
    return slabs, params


def _prep_inputs(refs, first):
    vals = []
    for s in range(4):
        z = refs[2 * s][...]
        vals += [z, _shift_down(z, refs[2 * s + 1][...], 1, first)]
    return vals + [r[...] for r in refs[8:19]]


def _rwkv_prep(P, params, tm=256):
    S = P.shape[0]
    slabs, pspecs = _rwkv_prep_specs(tm)

    def body(*refs):
        outs = _rwkv_prep_fn(*_prep_inputs(refs, pl.program_id(0) == 0))
        for o_ref, val in zip(refs[19:], outs):
            o_ref[...] = val

    shp = jax.ShapeDtypeStruct((S, D), F32)
    return pl.pallas_call(body, name="rwkv_prep", grid=(S // tm,), in_specs=slabs + pspecs,
                          out_specs=[_rows(tm, D)] * 7, out_shape=[shp] * 7,
                          compiler_params=_cparams(("parallel",)))(*([P] * 8), *params)


def _rwkv_prep_bwd(P, params, cts_a, cts_b, tm=128):
    S = P.shape[0]
    nblk = S // tm
    blk = lambda i: nblk - 1 - i
    slabs, pspecs = _rwkv_prep_specs(tm, blk)
    has_b = [c is not None for c in cts_b]
    n_ct = 7 + sum(has_b)

    def body(*refs):
        start = pl.program_id(0) == 0
        ins = _prep_inputs(refs, pl.program_id(0) == nblk - 1)
        ct_refs = refs[19:19 + n_ct]
        out_refs = refs[19 + n_ct:19 + n_ct + 15]
        carry_refs = refs[19 + n_ct + 15:]

        @pl.when(start)
        def _():
            for c_ref in carry_refs:
                c_ref[...] = jnp.zeros_like(c_ref)

        cts, pos = [], 7
        for i in range(7):
            c = ct_refs[i][...]
            if has_b[i]:
                c = c + ct_refs[pos][...]
                pos += 1
            cts.append(c)
        _, vjp = jax.vjp(_rwkv_prep_fn, *ins)
        grads = vjp(tuple(cts))
        for s in range(4):
            shifted = grads[2 * s + 1]
            out_refs[s][...] = (grads[2 * s] + _shift_up(shifted, carry_refs[s][...], 1, start)).astype(BF16)
            carry_refs[s][0:1, :] = shifted[0:1, :]
        for i in range(11):
            _acc(out_refs[4 + i], grads[8 + i], start)

    ct_in = list(cts_a) + [c for c in cts_b if c is not None]
    row = lambda w: pl.BlockSpec((tm, w), lambda i: (blk(i), 0))
    f = jax.ShapeDtypeStruct
    zshapes = [f((S, D), BF16)] * 3 + [f((S, LORA_PAD), BF16)]
    pshapes = [f((1, D), F32)] * 3 + [f((1, LORA_PAD), F32)] + [f((1, D), F32)] * 4 + [f((LORA_W, D), F32), f((LORA_A, D), F32), f((G_PAD, D), F32)]
    return pl.pallas_call(
        body, name="rwkv_prep_bwd", grid=(nblk,),
        in_specs=slabs + pspecs + [row(D)] * n_ct,
        out_specs=[row(D), row(D), row(D), row(LORA_PAD)] + pspecs,
        out_shape=zshapes + pshapes,
        scratch_shapes=[pltpu.VMEM((SUBLANES, D), F32)] * 3 + [pltpu.VMEM((SUBLANES, LORA_PAD), F32)],
        compiler_params=_cparams(("arbitrary",)))(*([P] * 8), *params, *ct_in)


def _rwkv_post_fn(y, r, kmod, v, g, lnx_w, lnx_b, r_k):
    mean = _head_sum(y) * (1.0 / HEAD)
    yc = y - mean
    var = _head_sum(yc * yc) * (1.0 / HEAD)
    yn = yc * lax.rsqrt(var + GN_EPS) * lnx_w + lnx_b
    bonus = _head_sum(r * kmod * r_k) * v
    return (yn + bonus) * g


def _rwkv_post(y, r, kmod, v, g, lnx_w, lnx_b, r_k, tm=256):
    S = y.shape[0]

    def body(y_ref, r_ref, k_ref, v_ref, g_ref, w_ref, b_ref, rk_ref, o_ref):
        o_ref[...] = _rwkv_post_fn(y_ref[...], r_ref[...], k_ref[...], v_ref[...], g_ref[...],
                                   w_ref[...], b_ref[...], rk_ref[...]).astype(BF16)

    row, vec = _rows(tm, D), _full((1, D))
    return pl.pallas_call(body, name="rwkv_post", grid=(S // tm,), in_specs=[row] * 5 + [vec] * 3, out_specs=row,
                          out_shape=jax.ShapeDtypeStruct((S, D), BF16),
                          compiler_params=_cparams(("parallel",)))(y, r, kmod, v, g, lnx_w, lnx_b, r_k)


def _rwkv_post_bwd(drw, y, r, kmod, v, g, lnx_w, lnx_b, r_k, tm=256):
    S = y.shape[0]

    def body(d_ref, y_ref, r_ref, k_ref, v_ref, g_ref, w_ref, b_ref, rk_ref, *out_refs):
        first = pl.program_id(0) == 0
        _, vjp = jax.vjp(_rwkv_post_fn, y_ref[...], r_ref[...], k_ref[...], v_ref[...], g_ref[...],
                         w_ref[...], b_ref[...], rk_ref[...])
        grads = vjp(d_ref[...])
        for i in range(5):
            out_refs[i][...] = grads[i]
        for i in range(5, 8):
            _acc(out_refs[i], grads[i], first)

    row, vec = _rows(tm, D), _full((1, D))
    f = jax.ShapeDtypeStruct
    return pl.pallas_call(body, name="rwkv_post_bwd", grid=(S // tm,), in_specs=[row] * 6 + [vec] * 3,
                          out_specs=[row] * 5 + [vec] * 3, out_shape=[f((S, D), F32)] * 5 + [f((1, D), F32)] * 3,
                          compiler_params=_cparams(("arbitrary",)))(drw, y, r, kmod, v, g, lnx_w, lnx_b, r_k)


CHUNK = 64
CHUNK_TB = 256
_DOT_DIMS = {"nn": (((2,), (1,)), ((0,), (0,))), "nt": (((2,), (2,)), ((0,), (0,))), "tn": (((1,), (1,)), ((0,), (0,)))}


def _dot16(x, y, mode):
    return lax.dot_general(x.astype(BF16), y.astype(BF16), _DOT_DIMS[mode], preferred_element_type=F32)


@functools.partial(jax.custom_vjp, nondiff_argnums=(2,))
def _mm16(x, y, mode):
    return _dot16(x, y, mode)


def _mm16_fwd(x, y, mode):
    return _dot16(x, y, mode), (x, y)


def _mm16_bwd(mode, res, ct):
    x, y = res
    if mode == "nn":
        return _dot16(ct, y, "nt"), _dot16(x, ct, "tn")
    if mode == "nt":
        return _dot16(ct, y, "nn"), _dot16(ct, x, "tn")
    return _dot16(y, ct, "nt"), _dot16(x, ct, "nn")


_mm16.defvjp(_mm16_fwd, _mm16_bwd)


def _tri_sum(x, upper):
    T = x.shape[0]
    i = lax.broadcasted_iota(jnp.int32, (T, T), 0)
    j = lax.broadcasted_iota(jnp.int32, (T, T), 1)
    tri = ((j >= i) if upper else (i >= j)).astype(BF16)
    out, rest = None, x
    for _ in range(3):
        piece = rest.astype(BF16)
        rest = rest - piece.astype(F32)
        part = jnp.dot(tri, piece, preferred_element_type=F32)
        out = part if out is None else out + part
    return out


@jax.custom_vjp
def _cumsum_rows(x):
    return _tri_sum(x, False)


_cumsum_rows.defvjp(lambda x: (_tri_sum(x, False), None), lambda _, ct: (_tri_sum(ct, True),))


def _rows_to_cols(row):
    per_head = jnp.concatenate([row[:, h * HEAD:(h + 1) * HEAD] for h in range(N_HEADS)], axis=0)
    eye = (lax.broadcasted_iota(jnp.int32, (HEAD, HEAD), 0) == lax.broadcasted_iota(jnp.int32, (HEAD, HEAD), 1)).astype(F32)
    cols = lax.dot_general(eye, per_head, (((1,), (1,)), ((), ())), precision=lax.Precision.HIGHEST,
                           preferred_element_type=F32)
    return jnp.concatenate([cols[:, h:h + 1][None] for h in range(N_HEADS)], axis=0)


def _per_head(x):
    return jnp.concatenate([x[:, h * HEAD:(h + 1) * HEAD][None] for h in range(N_HEADS)], axis=0)


def _chunk_fn(st0, r, lw, k, v, a, b):
    T = r.shape[0]
    cl = _cumsum_rows(lw)
    cl_end = cl[T - 1:T, :]
    inv = jnp.exp(-cl)
    to_end = jnp.exp(cl_end - cl)
    ah, rh, bh, kh, be, ke, v3 = [_per_head(x) for x in
                                  (a * jnp.exp(cl - lw), r * jnp.exp(cl), b * inv, k * inv, b * to_end, k * to_end, v)]
    i = lax.broadcasted_iota(jnp.int32, (N_HEADS, T, T), 1)
    j = lax.broadcasted_iota(jnp.int32, (N_HEADS, T, T), 2)
    a_ab = jnp.where(i > j, _mm16(ah, bh, "nt"), 0.0)
    a_ak = jnp.where(i > j, _mm16(ah, kh, "nt"), 0.0)
    m_rb = jnp.where(i >= j, _mm16(rh, bh, "nt"), 0.0)
    m_rk = jnp.where(i >= j, _mm16(rh, kh, "nt"), 0.0)
    rhs = _mm16(ah, st0, "nn") + _mm16(a_ak, v3, "nn")
    power, solve, n = a_ab, (i == j).astype(F32) + a_ab, 1
    while 2 * n < T:
        power = _mm16(power, power, "nn")
        solve = solve + _mm16(solve, power, "nn")
        n *= 2
    sa = _mm16(solve, rhs, "nn")
    y3 = _mm16(rh, st0, "nn") + _mm16(m_rb, sa, "nn") + _mm16(m_rk, v3, "nn")
    st_end = _rows_to_cols(jnp.exp(cl_end)) * st0 + _mm16(be, sa, "tn") + _mm16(ke, v3, "tn")
    return jnp.concatenate([y3[h] for h in range(N_HEADS)], axis=1), st_end


def _hosted_exchange(refs, n, broadcast, grid):
    if n == 0:
        return lambda: None
    start, wait = _exchange_ops(refs[:n], refs[n:2 * n], *refs[2 * n:], broadcast)
    first = functools.reduce(jnp.logical_and, [pl.program_id(a) == 0 for a in range(len(grid))])
    last = functools.reduce(jnp.logical_and, [pl.program_id(a) == g - 1 for a, g in enumerate(grid)])
    pl.when(first)(start)
    return lambda: pl.when(last)(wait)


def _cscan_fwd(r, lw, k, v, a, b, gather=()):
    S = r.shape[0]
    per_blk = CHUNK_TB // CHUNK
    n_x = len(gather)
    nblk = S // CHUNK_TB

    def body(*refs):
        r_ref, lw_ref, k_ref, v_ref, a_ref, b_ref = refs[:6]
        y_ref, ck_ref = refs[6 + n_x:8 + n_x]
        st_ref = refs[8 + 2 * n_x]
        finish = _hosted_exchange(refs[6:6 + n_x] + refs[8 + n_x:8 + 2 * n_x] + refs[9 + 2 * n_x:], n_x, True, (nblk,))

        @pl.when(pl.program_id(0) == 0)
        def _():
            st_ref[...] = jnp.zeros_like(st_ref)

        def chunk(c, carry):
            rows = pl.ds(pl.multiple_of(c * CHUNK, CHUNK), CHUNK)
            st0 = st_ref[...]
            ck_ref[c] = st0
            y, st_end = _chunk_fn(st0, r_ref[rows, :], lw_ref[rows, :], k_ref[rows, :],
                                  v_ref[rows, :], a_ref[rows, :], b_ref[rows, :])
            y_ref[rows, :] = y
            st_ref[...] = st_end
            return carry

        lax.fori_loop(0, per_blk, chunk, 0)
        finish()

    blk = _rows(CHUNK_TB, D)
    any_spec = pl.BlockSpec(memory_space=pl.ANY)
    outs = pl.pallas_call(
        body, name="scan_fwd", grid=(nblk,), in_specs=[blk] * 6 + [any_spec] * n_x,
        out_specs=[blk, pl.BlockSpec((per_blk, N_HEADS, HEAD, HEAD), lambda i: (i, 0, 0, 0))] + [any_spec] * n_x,
        out_shape=[jax.ShapeDtypeStruct((S, D), F32), jax.ShapeDtypeStruct((S // CHUNK, N_HEADS, HEAD, HEAD), F32)]
        + _exchange_shapes(gather, True),
        scratch_shapes=[pltpu.VMEM((N_HEADS, HEAD, HEAD), F32)] + (_exchange_scratch(n_x) if n_x else []),
        compiler_params=_cparams(("arbitrary",)))(r, lw, k, v, a, b, *gather)
    return outs[0], outs[1], outs[2:]


def _cscan_bwd(r, lw, k, v, a, b, ckpt, dy, scatter=()):
    S = r.shape[0]
    per_blk = CHUNK_TB // CHUNK
    nblk = S // CHUNK_TB
    n_x = len(scatter)

    def body(*refs):
        r_ref, lw_ref, k_ref, v_ref, a_ref, b_ref, ck_ref, dy_ref = refs[:8]
        out_refs = refs[8 + n_x:14 + n_x]
        ds_ref = refs[14 + 2 * n_x]
        finish = _hosted_exchange(refs[8:8 + n_x] + refs[14 + n_x:14 + 2 * n_x] + refs[15 + 2 * n_x:], n_x, False, (nblk,))

        @pl.when(pl.program_id(0) == 0)
        def _():
            ds_ref[...] = jnp.zeros_like(ds_ref)

        def chunk(cc, carry):
            c = per_blk - 1 - cc
            rows = pl.ds(pl.multiple_of(c * CHUNK, CHUNK), CHUNK)
            ins = (ck_ref[c], r_ref[rows, :], lw_ref[rows, :], k_ref[rows, :], v_ref[rows, :], a_ref[rows, :], b_ref[rows, :])
            _, vjp = jax.vjp(_chunk_fn, *ins)
            grads = vjp((dy_ref[rows, :], ds_ref[...]))
            ds_ref[...] = grads[0]
            for o_ref, g in zip(out_refs, grads[1:]):
                o_ref[rows, :] = g
            return carry

        lax.fori_loop(0, per_blk, chunk, 0)
        finish()

    blk = pl.BlockSpec((CHUNK_TB, D), lambda i: (nblk - 1 - i, 0))
    any_spec = pl.BlockSpec(memory_space=pl.ANY)
    shp = jax.ShapeDtypeStruct((S, D), F32)
    outs = pl.pallas_call(
        body, name="scan_bwd", grid=(nblk,),
        in_specs=[blk] * 6 + [pl.BlockSpec((per_blk, N_HEADS, HEAD, HEAD), lambda i: (nblk - 1 - i, 0, 0, 0)), blk]
        + [any_spec] * n_x,
        out_specs=[blk] * 6 + [any_spec] * n_x, out_shape=[shp] * 6 + _exchange_shapes(scatter, False),
        scratch_shapes=[pltpu.VMEM((N_HEADS, HEAD, HEAD), F32)] + (_exchange_scratch(n_x) if n_x else []),
        compiler_params=_cparams(("arbitrary",)))(r, lw, k, v, a, b, ckpt, dy, *scatter)
    return outs[:6], outs[6:]


def _ada_partial(c_all, w_shard):
    def body(c_ref, w_ref, o_ref):
        o_ref[...] = jnp.dot(c_ref[...].astype(BF16), w_ref[...].astype(BF16), preferred_element_type=F32)

    vm = pl.BlockSpec(memory_space=pltpu.VMEM)
    return pl.pallas_call(body, name="ada_partial", in_specs=[vm, vm], out_specs=vm,
                          out_shape=jax.ShapeDtypeStruct((N_DEV, w_shard.shape[1]), F32),
                          compiler_params=pltpu.CompilerParams(vmem_limit_bytes=VMEM_LIMIT))(c_all, w_shard)


def _ada_bias(rows, b_ada):
    def body(r_ref, b_ref, o_ref):
        o_ref[...] = r_ref[...] + b_ref[...]

    vm = pl.BlockSpec(memory_space=pltpu.VMEM)
    return pl.pallas_call(body, name="ada_bias", in_specs=[vm, vm], out_specs=vm,
                          out_shape=jax.ShapeDtypeStruct(rows.shape, F32))(rows, b_ada)


def _ada_wgrad(c_cols, d_all):
    def body(c_ref, d_ref, o_ref):
        acc = c_ref[:, 0:1] * d_ref[0:1, :]
        for j in range(1, N_DEV):
            acc = acc + c_ref[:, j:j + 1] * d_ref[j:j + 1, :]
        o_ref[...] = acc

    vm = pl.BlockSpec(memory_space=pltpu.VMEM)
    return pl.pallas_call(body, name="ada_wgrad", in_specs=[vm, vm], out_specs=vm,
                          out_shape=jax.ShapeDtypeStruct((D, d_all.shape[1]), F32),
                          compiler_params=pltpu.CompilerParams(vmem_limit_bytes=VMEM_LIMIT))(c_cols, d_all)


def _exchange(srcs, broadcast, name):
    n = len(srcs)

    def body(*refs):
        start, wait = _exchange_ops(refs[:n], refs[n:2 * n], *refs[2 * n:], broadcast)
        start()
        wait()

    any_spec = pl.BlockSpec(memory_space=pl.ANY)
    return pl.pallas_call(
        body, name=name, out_shape=_exchange_shapes(srcs, broadcast), in_specs=[any_spec] * n, out_specs=[any_spec] * n,
        scratch_shapes=_exchange_scratch(n),
        compiler_params=pltpu.CompilerParams(has_side_effects=True),
    )(*srcs)


def _gather_via_sibling(srcs, name):
    n = len(srcs)

    def body(*refs):
        src_refs, out_refs = refs[:n], refs[n:2 * n]
        send_sems, recv_sems, local_sems = refs[2 * n:]
        x, y, c = lax.axis_index("x"), lax.axis_index("y"), lax.axis_index("c")
        me, sibling = (x, y, c), (x, y, 1 - c)
        chips = [(1 - x, y), (x, 1 - y), (1 - x, 1 - y)]

        def slot(px, py, pc):
            return 4 * px + 2 * py + pc

        def copy(i, k, block, to, src=None):
            rows = out_refs[i].at[slot(*block)]
            return pltpu.make_async_remote_copy(
                src_ref=rows if src is None else src, dst_ref=rows, send_sem=send_sems.at[i, k],
                recv_sem=recv_sems.at[i, k], device_id=to, device_id_type=_MESH)

        local = [pltpu.make_async_copy(src_refs[i], out_refs[i].at[slot(*me)], local_sems.at[i]) for i in range(n)]
        for cp in local:
            cp.start()
        first = [copy(i, 0, me, sibling, src=src_refs[i]) for i in range(n)]
        first += [copy(i, 1 + j, me, (*chip, c), src=src_refs[i]) for j, chip in enumerate(chips) for i in range(n)]
        for cp in first:
            cp.start()
        passed = []
        for j, chip in enumerate(chips):
            for i in range(n):
                copy(i, 1 + j, (*chip, c), me).wait_recv()
                passed.append(copy(i, 4 + j, (*chip, c), sibling))
                passed[-1].start()
        for i in range(n):
            copy(i, 0, sibling, me).wait_recv()
            for j, chip in enumerate(chips):
                copy(i, 4 + j, (*chip, 1 - c), me).wait_recv()
        for cp in first + passed:
            cp.wait_send()
        for cp in local:
            cp.wait()

    any_spec = pl.BlockSpec(memory_space=pl.ANY)
    return pl.pallas_call(
        body, name=name, out_shape=_exchange_shapes(srcs, True), in_specs=[any_spec] * n, out_specs=[any_spec] * n,
        scratch_shapes=_exchange_scratch(n),
        compiler_params=pltpu.CompilerParams(has_side_effects=True),
    )(*srcs)


def _flags(broadcast, n):
    return [broadcast] * n if isinstance(broadcast, bool) else list(broadcast)


def _exchange_shapes(srcs, broadcast):
    return [jax.ShapeDtypeStruct((N_DEV,) + (s.shape if bc else s.shape[1:]), s.dtype)
            for s, bc in zip(srcs, _flags(broadcast, len(srcs)))]


def _exchange_scratch(n):
    return [pltpu.SemaphoreType.DMA((n, N_DEV)), pltpu.SemaphoreType.DMA((n, N_DEV)), pltpu.SemaphoreType.DMA((n,))]


def _exchange_ops(src_refs, out_refs, send_sems, recv_sems, local_sems, broadcast):
    n = len(src_refs)
    flags = _flags(broadcast, n)
    x, y, c = lax.axis_index("x"), lax.axis_index("y"), lax.axis_index("c")
    me = 4 * x + 2 * y + c

    def block(i, j):
        return src_refs[i] if flags[i] else src_refs[i].at[j]

    def remote(i, d, src_slot, dst_slot):
        px, py, pc = x ^ (d >> 2), y ^ ((d >> 1) & 1), c ^ (d & 1)
        return pltpu.make_async_remote_copy(
            src_ref=block(i, src_slot), dst_ref=out_refs[i].at[dst_slot], send_sem=send_sems.at[i, d],
            recv_sem=recv_sems.at[i, d], device_id=(px, py, pc), device_id_type=_MESH)

    def local(i):
        return pltpu.make_async_copy(block(i, me), out_refs[i].at[me], local_sems.at[i])

    def start():
        for i in range(n):
            local(i).start()
        for d in range(1, N_DEV):
            for i in range(n):
                remote(i, d, me ^ d, me).start()

    def wait():
        for d in range(1, N_DEV):
            for i in range(n):
                remote(i, d, me, me ^ d).wait_recv()
        for d in range(1, N_DEV):
            for i in range(n):
                remote(i, d, me ^ d, me).wait_send()
        for i in range(n):
            local(i).wait()

    return start, wait


def _adamw(w, g, m, v):
    nm = ADAM_B1 * m + (1.0 - ADAM_B1) * g
    nv = ADAM_B2 * v + (1.0 - ADAM_B2) * (g * g)
    m_hat = nm * (1.0 / (1.0 - ADAM_B1 ** ADAM_STEP))
    v_hat = nv * (1.0 / (1.0 - ADAM_B2 ** ADAM_STEP))
    return -ADAM_LR * (m_hat / (jnp.sqrt(v_hat) + ADAM_EPS) + ADAM_WD * w), nm, nv


def _adam_vectors(parts, ws, ms, vs):
    nv = len(ws)
    sizes = [w.shape[1] for w in ws]

    def body(*refs):
        p_ref = refs[0]
        w_refs, m_refs, v_refs = refs[1:1 + nv], refs[1 + nv:1 + 2 * nv], refs[1 + 2 * nv:1 + 3 * nv]
        out_refs = refs[1 + 3 * nv:]
        g_all = p_ref[0]
        for j in range(1, N_DEV):
            g_all = g_all + p_ref[j]
        off = 0
        for i, n in enumerate(sizes):
            g = g_all[:, off:off + n]
            off += -(-n // LANES) * LANES
            delta, new_m, new_v = _adamw(w_refs[i][...], g, m_refs[i][...], v_refs[i][...])
            for o_ref, val in zip(out_refs[4 * i:4 * i + 4], (g, delta, new_m, new_v)):
                o_ref[...] = val

    vm = pl.BlockSpec(memory_space=pltpu.VMEM)
    outs = pl.pallas_call(body, name="adam_replicated", in_specs=[vm] * (1 + 3 * nv), out_specs=[vm] * (4 * nv),
                          out_shape=[jax.ShapeDtypeStruct((1, n), F32) for n in sizes for _ in range(4)])(parts, *ws, *ms, *vs)
    return [outs[4 * i:4 * i + 4] for i in range(nv)]


def _sum_adam(parts, w, m, v, name):
    n_parts, R, C = parts.shape
    fits = [t for t in range(16, R + 1, 16) if R % t == 0 and t * C <= 2504 * LANES]
    if fits:
        tm, tc = max(fits), C
    elif C % (2 * LANES) == 0 and R * C > 2504 * LANES:
        tm, tc = R, 2 * LANES
    else:
        tm, tc = R, C

    def body(p_ref, w_ref, m_ref, v_ref, g_ref, d_ref, nm_ref, nv_ref):
        g = p_ref[0].astype(F32)
        for j in range(1, n_parts):
            g = g + p_ref[j].astype(F32)
        g_ref[...] = g
        d_ref[...], nm_ref[...], nv_ref[...] = _adamw(w_ref[...], g, m_ref[...], v_ref[...])

    blk = pl.BlockSpec((tm, tc), lambda i, j: (i, j))
    shp = jax.ShapeDtypeStruct((R, C), F32)
    return pl.pallas_call(body, name=name, grid=(R // tm, C // tc),
                          in_specs=[pl.BlockSpec((n_parts, tm, tc), lambda i, j: (0, i, j)), blk, blk, blk],
                          out_specs=[blk] * 4, out_shape=[shp] * 4,
                          compiler_params=_cparams(("parallel", "parallel")))(parts, w, m, v)


TRANSPOSED = ("w_in", "w_up")
SHARDED = (("w_ada", 1), ("w_in", 0), ("w2", 1), ("a2", 1), ("g2", 1), ("w_att_out", 1), ("w_rwkv_out", 0),
           ("w_o", 0), ("w_up", 0), ("conv_w", 1), ("w_down", 0))
EARLY, LATE = SHARDED[1:5], SHARDED[5:]
REPLICATED = ("b_ada", "norm1_w", "b_gate", "mu_shift", "w0", "a0", "k_k", "k_a", "r_k", "lnx_w", "lnx_b",
              "norm2_w", "conv_b", "norm_f_w")
WEIGHTS = ("w_ada", "b_ada", "norm1_w", "w_in", "b_gate", "mu_shift", "w0", "w2", "a0", "a2", "g2", "k_k", "k_a", "r_k",
           "lnx_w", "lnx_b", "w_att_out", "w_rwkv_out", "w_o", "norm2_w", "w_up", "conv_w", "conv_b", "w_down", "norm_f_w")


W_IN_RUNS = ((0, C_ATT, ATT_IN), (ATT_IN, C_R, 3 * D), (ATT_IN + 3 * D, C_LORA, LORA_W + LORA_A),
             (ATT_IN + 3 * D + LORA_W + LORA_A, C_LORA + LANES, LORA_G), (ATT_IN + RWKV_IN, C_GA, 2 * D))
W_IN_SHARD = N_IN // N_DEV


def _pad_w_in(w_in_t):
    pieces = [w_in_t[orig:orig + count] for orig, _, count in sorted(W_IN_RUNS, key=lambda run: run[1])]
    pieces.append(jnp.zeros((LORA_PAD - LANES - LORA_G, w_in_t.shape[1]), w_in_t.dtype))
    return jnp.concatenate(pieces, axis=0)


def _w_in_blocks(g):
    blocks = []
    for j in range(N_DEV):
        pieces = []
        for orig, pad, count in W_IN_RUNS:
            lo, hi = max(orig, j * W_IN_SHARD), min(orig + count, (j + 1) * W_IN_SHARD)
            if lo < hi:
                pieces.append(g[pad + lo - orig:pad + hi - orig])
        blocks.append(jnp.concatenate(pieces, axis=0)[None])
    return jnp.concatenate(blocks, axis=0)


def _pad_mu(mu):
    lo = mu[:, 3 * D:]
    mu_l = jnp.concatenate([lo[:, :LORA_W + LORA_A], lo[:, LORA_W + LORA_A:], jnp.zeros((1, LORA_PAD - LANES - LORA_G), mu.dtype)], axis=1)
    return mu[:, :D], mu[:, D:2 * D], mu[:, 2 * D:3 * D], mu_l


def _local_step(x, ada, W, late_shards, target):
    S = x.shape[0]
    W = dict(W)
    G = {}
    sh1, sc1, gt1, sh2, sc2, gt2 = [ada[:, i * D:(i + 1) * D] for i in range(6)]
    h1, rstd1 = _norm_fwd(x, None, None, W["norm1_w"], sc1, sh1, "norm1_fwd")
    w_in_p = _pad_w_in(W["w_in"])
    P = _mm(h1, w_in_p, "nt", F32, "proj_in")

    mu_r, mu_k, mu_v, mu_l = _pad_mu(W["mu_shift"])
    g2p = jnp.pad(W["g2"], ((0, G_PAD - LORA_G), (0, 0)))
    prep_params = [mu_r, mu_k, mu_v, mu_l, W["w0"], W["a0"], W["k_k"], W["k_a"], W["w2"], W["a2"], g2p]
    r_, dec, kmod, v_, aa, bb, gg = _rwkv_prep(P, prep_params)
    y_scan, states, late = _cscan_fwd(r_, dec, kmod, v_, aa, bb, gather=late_shards)
    W.update({n: _full_weight(g, axis) for (n, axis), g in zip(LATE, late)})

    o_g, l_g = zip(*[_att_fwd(P, g) for g in range(len(ATT_PATTERNS))])
    att = _att_combine_fwd(o_g, l_g)
    y_att = _mm(att, W["w_att_out"], "nn", F32, "att_out")
    r_k = W["r_k"].reshape(1, D)
    rw = _rwkv_post(y_scan, r_, kmod, v_, gg, W["lnx_w"], W["lnx_b"], r_k)
    y_rwkv = _mm(rw, W["w_rwkv_out"], "nn", F32, "rwkv_out")

    bga, bgr = W["b_gate"][:, :D], W["b_gate"][:, D:]
    mix = _gate_fwd(P, bga, bgr, y_att, y_rwkv)
    mo = _mm(mix, W["w_o"], "nn", F32, "mix_out")
    x2, h2, rstd2 = _norm_fwd(x, mo, gt1, W["norm2_w"], sc2, sh2, "norm2_fwd")
    u = _mm(h2, W["w_up"], "nt", BF16, "ffn_up")
    conv_w8 = jnp.pad(W["conv_w"], ((0, SUBLANES - 3), (0, 0)))
    act = _conv_fwd(u, conv_w8, W["conv_b"])
    f = _mm(act, W["w_down"], "nn", F32, "ffn_down")
    loss_blk, dx3, df, dgt2, G["norm_f_w"] = _final(x2, f, gt2, W["norm_f_w"], target)
    loss = loss_blk[0, 0]

    dact = _mm(df, W["w_down"], "nt", BF16, "ffn_down_dx")
    G["w_down"] = _mm(act, df, "tn", BF16, "ffn_down_dw")
    duc, dwg, dwv, dbg, dbv = _conv_bwd_a(dact, u, conv_w8, W["conv_b"])
    G["conv_w"] = jnp.concatenate([dwg[0:3], dwv[0:3]], axis=1)
    G["conv_b"] = jnp.concatenate([dbg, dbv], axis=1)
    du = _conv_bwd_b(duc, conv_w8)
    dh2 = _mm(du, W["w_up"], "nn", F32, "ffn_up_dx")
    G["w_up"] = _mm(du, h2, "tn", BF16, "ffn_up_dw")
    dx2, dsh2, dsc2, G["norm2_w"], dmo, dgt1 = _norm_bwd(dh2, x2, rstd2, W["norm2_w"], sc2, dx3, mo, gt1, "norm2_bwd")
    dmix = _mm(dmo, W["w_o"], "nt", F32, "mix_out_dx")
    G["w_o"] = _mm(mix, dmo, "tn", BF16, "mix_out_dw")
    dy_att, dy_rwkv, dpga, dpgr, dbga, dbgr = _gate_bwd(dmix, P, bga, bgr, y_att, y_rwkv)
    G["b_gate"] = jnp.concatenate([dbga, dbgr], axis=1)

    datt = _mm(dy_att, W["w_att_out"], "nt", F32, "att_out_dx")
    G["w_att_out"] = _mm(att, dy_att, "tn", BF16, "att_out_dw")
    dcomb = _att_combine_bwd(datt, o_g, l_g)
    dp_att = []
    for g in range(len(ATT_PATTERNS)):
        dp_att += _att_bwd(P, o_g[g], l_g[g], dcomb[g], dcomb[3 + g], g)

    drw = _mm(dy_rwkv, W["w_rwkv_out"], "nt", F32, "rwkv_out_dx")
    G["w_rwkv_out"] = _mm(rw, dy_rwkv, "tn", BF16, "rwkv_out_dw")
    dy_scan, dr1, dk1, dv1, dgg, G["lnx_w"], G["lnx_b"], drk = _rwkv_post_bwd(drw, y_scan, r_, kmod, v_, gg, W["lnx_w"], W["lnx_b"], r_k)
    G["r_k"] = drk.reshape(W["r_k"].shape)
    late_blocks = [_owner_blocks(G[n], axis) for n, axis in LATE] if late_shards else []
    (dr2, ddec, dk2, dv2, daa, dbb), late_parts = _cscan_bwd(r_, dec, kmod, v_, aa, bb, states, dy_scan, scatter=late_blocks)
    pb = _rwkv_prep_bwd(P, prep_params, [dr2, ddec, dk2, dv2, daa, dbb, dgg], [dr1, None, dk1, dv1, None, None, None])
    dp_rkv, dp_lora, dpar = list(pb[0:3]), pb[3], pb[4:]
    dmu_r, dmu_k, dmu_v, dmu_l, G["w0"], G["a0"], G["k_k"], G["k_a"], G["w2"], G["a2"], dg2p = dpar
    G["g2"] = dg2p[0:LORA_G]
    G["mu_shift"] = jnp.concatenate([dmu_r, dmu_k, dmu_v, dmu_l[:, :LORA_W + LORA_A], dmu_l[:, LANES:LANES + LORA_G]], axis=1)

    dP = jnp.concatenate(dp_rkv + [dpga, dpgr] + dp_att + [dp_lora], axis=1)
    G["w_in"] = _w_in_blocks(_mm(dP, h1, "tn", BF16, "proj_in_dw"))
    if late_shards:
        dh1, (w_in_parts,) = _mm(dP, w_in_p, "nn", F32, "proj_in_dx", scatter=[G["w_in"]])
        done = dict(zip([n for n, _ in LATE] + ["w_in"], list(late_parts) + [w_in_parts]))
    else:
        dh1, done = _mm(dP, w_in_p, "nn", F32, "proj_in_dx"), {}
    grad_x, dsh1, dsc1, G["norm1_w"] = _norm_bwd(dh1, x, rstd1, W["norm1_w"], sc1, dx2, None, None, "norm1_bwd")
    dada = jnp.concatenate([dsh1, dsc1, dgt1, dsh2, dsc2, dgt2], axis=1)
    G["b_ada"] = dada
    return loss, grad_x, G, done


def _full_weight(gathered, axis):
    _, rows, cols = gathered.shape
    if axis == 0:
        return gathered.reshape(N_DEV * rows, cols)
    return gathered.transpose(1, 0, 2).reshape(rows, N_DEV * cols)


def _owner_blocks(g, axis):
    rows, cols = g.shape
    g = g.astype(BF16)
    if axis == 0:
        return g.reshape(N_DEV, rows // N_DEV, cols)
    return g.reshape(rows, N_DEV, cols // N_DEV).transpose(1, 0, 2)


def kernel(x, c, w_ada, b_ada, norm1_w, w_in, b_gate, mu_shift, w0, w2, a0, a2, g2, k_k, k_a, r_k, lnx_w, lnx_b, w_att_out, w_rwkv_out, w_o, norm2_w, w_up, conv_w, conv_b, w_down, norm_f_w, loss_target, m_w_ada, m_b_ada, m_norm1_w, m_w_in, m_b_gate, m_mu_shift, m_w0, m_w2, m_a0, m_a2, m_g2, m_k_k, m_k_a, m_r_k, m_lnx_w, m_lnx_b, m_w_att_out, m_w_rwkv_out, m_w_o, m_norm2_w, m_w_up, m_conv_w, m_conv_b, m_w_down, m_norm_f_w, v_w_ada, v_b_ada, v_norm1_w, v_w_in, v_b_gate, v_mu_shift, v_w0, v_w2, v_a0, v_a2, v_g2, v_k_k, v_k_a, v_r_k, v_lnx_w, v_lnx_b, v_w_att_out, v_w_rwkv_out, v_w_o, v_norm2_w, v_w_up, v_conv_w, v_conv_b, v_w_down, v_norm_f_w):
    env = dict(locals())
    w_shard = {n: env[n] for n in WEIGHTS}
    m_shard = {n: env["m_" + n] for n in WEIGHTS}
    v_shard = {n: env["v_" + n] for n in WEIGHTS}

    def mat(shards, n):
        return jnp.swapaxes(shards[n][0], 0, 1) if n in TRANSPOSED else shards[n][0]

    c_all, *gathered = _gather_via_sibling([c] + [mat(w_shard, n).astype(BF16) for n, _ in EARLY], "gather_weights")
    c_all = c_all.reshape(N_DEV, D)
    W = {n: _full_weight(g, axis) for (n, axis), g in zip(EARLY, gathered)}
    for n in REPLICATED:
        W[n] = w_shard[n].reshape(1, -1) if n != "r_k" else w_shard[n][0]
    ada_cols = _ada_partial(c_all, w_shard["w_ada"][0])
    ada_rows, = _exchange([ada_cols[:, None, :]], False, "ada_rows")
    ada = _ada_bias(ada_rows.reshape(1, -1), w_shard["b_ada"])

    late_shards = [mat(w_shard, n).astype(BF16) for n, _ in LATE]
    loss, grad_x, G, parts = _local_step(x[0], ada, W, late_shards, loss_target[0])
    loss = lax.psum(loss, ("x", "y", "c"))

    row = lambda a: a.reshape(1, -1)
    small = jnp.concatenate([jnp.pad(row(G[n]), ((0, 0), (0, (-G[n].size) % LANES))) for n in REPLICATED], axis=1)
    sparts, dada_all = _exchange([small, G["b_ada"].reshape(N_DEV, 1, -1)], [True, False], "gather_small_grads")
    parts["w_ada"] = _ada_wgrad(c_all.T, dada_all.reshape(N_DEV, -1))[None]

    rest = [(n, axis) for n, axis in SHARDED if n not in parts]
    parts.update(zip([n for n, _ in rest], _exchange([_owner_blocks(G[n], axis) for n, axis in rest], False, "scatter_grads")))
    out = {}
    for n, p in parts.items():
        res = _sum_adam(p, mat(w_shard, n), mat(m_shard, n), mat(v_shard, n), "adam_" + n)
        if n in TRANSPOSED:
            res = [jnp.swapaxes(a, 0, 1) for a in res]
        for kind, a in zip(("grad", "delta", "new_m", "new_v"), res):
            out[kind, n] = a[None]

    res = _adam_vectors(sparts, *[[row(s[n]) for n in REPLICATED] for s in (w_shard, m_shard, v_shard)])
    for n, four in zip(REPLICATED, res):
        for kind, a in zip(("grad", "delta", "new_m", "new_v"), four):
            out[kind, n] = a.reshape(w_shard[n].shape)

    return (loss, grad_x[None], *[out[kind, n] for kind in ("grad", "delta", "new_m", "new_v") for n in WEIGHTS])
```

```python
import functools

import jax
import jax.numpy as jnp
from jax import lax
from jax.experimental import pallas as pl
from jax.experimental.pallas import tpu as pltpu

F32 = jnp.float32
BF16 = jnp.bfloat16

D = 1024
HEAD = 64
ATT_PATTERNS = ((128, 1), (512, 4), (2048, 16))
ATT_HEADS = 8
ATT_W = ATT_HEADS * HEAD
ATT_IN = 3 * 3 * ATT_W
QBLK = 128
N_HEADS = D // HEAD
LORA_W, LORA_A, LORA_G = 64, 64, 160
RWKV_IN = 3 * D + LORA_W + LORA_A + LORA_G
N_IN = ATT_IN + RWKV_IN + 2 * D
D_FF = 2816
RMS_EPS = 1e-6
GN_EPS = 64e-5
N_DEV = 8
LANES = 128
SUBLANES = 8

C_R, C_K, C_V, C_GA, C_GR = 0, 1024, 2048, 3072, 4096
C_ATT = 5120
C_LORA = C_ATT + ATT_IN
LORA_PAD = 512
G_PAD = 256
N_PAD = C_LORA + LORA_PAD

ADAM_LR, ADAM_B1, ADAM_B2, ADAM_EPS, ADAM_WD, ADAM_STEP = 0.001, 0.9, 0.999, 1e-08, 0.01, 10

VMEM_LIMIT = 56 * 1024 * 1024

_MESH = pl.DeviceIdType.MESH


def _cparams(sem):
    return pltpu.CompilerParams(dimension_semantics=sem, vmem_limit_bytes=VMEM_LIMIT)


def _tile(dim, pref):
    if dim <= pref:
        return dim
    best = None
    for t in range(LANES, pref + 1, LANES):
        if dim % t == 0:
            best = t
    assert best is not None, dim
    return best


MM_TILES = {"nn": (1024, 1408, 2816), "nt": (1024, 2048, 1408), "tn": (1408, 1408, 4096)}


def _mm(a, b, mode, out_dtype, name, scatter=()):
    if mode == "nn":
        (M, K), (K2, N) = a.shape, b.shape
    elif mode == "nt":
        (M, K), (N, K2) = a.shape, b.shape
    else:
        (K, M), (K2, N) = a.shape, b.shape
    assert K == K2, (a.shape, b.shape, mode)
    tm, tn, tk = (_tile(dim, pref) for dim, pref in zip((M, N, K), MM_TILES[mode]))
    nk = K // tk
    grid = (M // tm, N // tn, nk)
    n_x = len(scatter)
    dims = {"nn": (((1,), (0,)), ((), ())), "nt": (((1,), (1,)), ((), ())), "tn": (((0,), (0,)), ((), ()))}[mode]

    def body(*refs):
        a_ref, b_ref = refs[:2]
        o_ref, acc_ref = refs[2 + n_x], refs[3 + 2 * n_x]
        finish = _hosted_exchange(refs[2:2 + n_x] + refs[3 + n_x:3 + 2 * n_x] + refs[4 + 2 * n_x:], n_x, False, grid)
        k = pl.program_id(2)
        part = lax.dot_general(a_ref[...].astype(BF16), b_ref[...].astype(BF16), dims,
                               preferred_element_type=F32)
        if nk == 1:
            o_ref[...] = part.astype(o_ref.dtype)
        else:
            @pl.when(k == 0)
            def _():
                acc_ref[...] = part

            @pl.when(jnp.logical_and(k > 0, k < nk - 1))
            def _():
                acc_ref[...] += part

            @pl.when(k == nk - 1)
            def _():
                o_ref[...] = (acc_ref[...] + part).astype(o_ref.dtype)
        finish()

    a_spec = pl.BlockSpec((tk, tm), lambda i, j, k: (k, i)) if mode == "tn" else pl.BlockSpec((tm, tk), lambda i, j, k: (i, k))
    b_spec = pl.BlockSpec((tn, tk), lambda i, j, k: (j, k)) if mode == "nt" else pl.BlockSpec((tk, tn), lambda i, j, k: (k, j))
    any_spec = pl.BlockSpec(memory_space=pl.ANY)
    outs = pl.pallas_call(
        body, name=name, grid=grid,
        in_specs=[a_spec, b_spec] + [any_spec] * n_x,
        out_specs=[pl.BlockSpec((tm, tn), lambda i, j, k: (i, j))] + [any_spec] * n_x,
        out_shape=[jax.ShapeDtypeStruct((M, N), out_dtype)] + _exchange_shapes(scatter, False),
        scratch_shapes=[pltpu.VMEM((tm, tn) if nk > 1 else (SUBLANES, LANES), F32)] + (_exchange_scratch(n_x) if n_x else []),
        compiler_params=_cparams(("arbitrary",) * 3 if n_x else ("parallel", "parallel", "arbitrary")),
    )(a, b, *scatter)
    return (outs[0], outs[1:]) if n_x else outs[0]


def _rows(tm, w, col=0):
    return pl.BlockSpec((tm, w), lambda i: (i, col))


def _full(shape):
    return pl.BlockSpec(shape, lambda i: (0,) * len(shape))


def _shift_down(x, halo, k, first):
    rolled = pltpu.roll(x, k, 0)
    row = lax.broadcasted_iota(jnp.int32, x.shape, 0)
    out = rolled
    n_halo = halo.shape[0]
    for j in range(k):
        h = jnp.where(first, 0.0, halo[n_halo - k + j:n_halo - k + j + 1, :])
        out = jnp.where(row == j, h, out)
    return out


def _shift_up(x, halo, k, last):
    n = x.shape[0]
    rolled = pltpu.roll(x, n - k, 0)
    row = lax.broadcasted_iota(jnp.int32, x.shape, 0)
    out = rolled
    for j in range(k):
        h = jnp.where(last, 0.0, halo[j:j + 1, :])
        out = jnp.where(row == n - k + j, h, out)
    return out


def _acc(ref, val, first):
    @pl.when(first)
    def _():
        ref[...] = val

    @pl.when(jnp.logical_not(first))
    def _():
        ref[...] += val


def _colsum(x):
    return jnp.sum(x, axis=0, keepdims=True)


def _norm_fwd(x, mo, gt, nw, sc, sh, name, tm=512):
    S = x.shape[0]
    has_res = mo is not None

    def body(*refs):
        if has_res:
            x_ref, mo_ref, gt_ref, nw_ref, sc_ref, sh_ref, x2_ref, h_ref, rs_ref = refs
            x2 = x_ref[...] + gt_ref[...] * mo_ref[...]
            x2_ref[...] = x2
        else:
            x_ref, nw_ref, sc_ref, sh_ref, h_ref, rs_ref = refs
            x2 = x_ref[...]
        rstd = lax.rsqrt(jnp.mean(x2 * x2, axis=-1, keepdims=True) + RMS_EPS)
        rs_ref[...] = rstd
        h_ref[...] = ((x2 * rstd * nw_ref[...]) * (1.0 + sc_ref[...]) + sh_ref[...]).astype(BF16)

    vec = _full((1, D))
    ins = [x, mo, gt, nw, sc, sh] if has_res else [x, nw, sc, sh]
    in_specs = [_rows(tm, D), _rows(tm, D), vec, vec, vec, vec] if has_res else [_rows(tm, D), vec, vec, vec]
    outs = [jax.ShapeDtypeStruct((S, D), BF16), jax.ShapeDtypeStruct((S, 1), F32)]
    out_specs = [_rows(tm, D), _rows(tm, 1)]
    if has_res:
        outs = [jax.ShapeDtypeStruct((S, D), F32)] + outs
        out_specs = [_rows(tm, D)] + out_specs
    return pl.pallas_call(body, name=name, grid=(S // tm,), in_specs=in_specs, out_specs=out_specs,
                          out_shape=outs, compiler_params=_cparams(("parallel",)))(*ins)


def _norm_bwd(dh, xin, rstd, nw, sc, dres, mo, gt, name, tm=512):
    S = xin.shape[0]
    has_res = mo is not None

    def body(*refs):
        if has_res:
            dh_ref, x_ref, rs_ref, nw_ref, sc_ref, dres_ref, mo_ref, gt_ref, dx_ref, dsh_ref, dsc_ref, dnw_ref, dmo_ref, dgt_ref = refs
        else:
            dh_ref, x_ref, rs_ref, nw_ref, sc_ref, dres_ref, dx_ref, dsh_ref, dsc_ref, dnw_ref = refs
        first = pl.program_id(0) == 0
        dh = dh_ref[...]
        rstd = rs_ref[...]
        n = x_ref[...] * rstd
        w = nw_ref[...]
        _acc(dsh_ref, _colsum(dh), first)
        _acc(dsc_ref, _colsum(dh * (n * w)), first)
        dnw = dh * (1.0 + sc_ref[...])
        _acc(dnw_ref, _colsum(dnw * n), first)
        dn = dnw * w
        dx = dres_ref[...] + rstd * (dn - n * jnp.mean(dn * n, axis=-1, keepdims=True))
        dx_ref[...] = dx
        if has_res:
            dmo_ref[...] = (dx * gt_ref[...]).astype(BF16)
            _acc(dgt_ref, _colsum(dx * mo_ref[...]), first)

    vec = _full((1, D))
    vshape = jax.ShapeDtypeStruct((1, D), F32)
    ins = [dh, xin, rstd, nw, sc, dres] + ([mo, gt] if has_res else [])
    in_specs = [_rows(tm, D), _rows(tm, D), _rows(tm, 1), vec, vec, _rows(tm, D)] + ([_rows(tm, D), vec] if has_res else [])
    outs = [jax.ShapeDtypeStruct((S, D), F32), vshape, vshape, vshape]
    out_specs = [_rows(tm, D), vec, vec, vec]
    if has_res:
        outs += [jax.ShapeDtypeStruct((S, D), BF16), vshape]
        out_specs += [_rows(tm, D), vec]
    return pl.pallas_call(body, name=name, grid=(S // tm,), in_specs=in_specs, out_specs=out_specs,
                          out_shape=outs, compiler_params=_cparams(("arbitrary",)))(*ins)


def _final(x2, f, gt2, nfw, target, tm=512):
    S = x2.shape[0]

    def body(x2_ref, f_ref, gt_ref, w_ref, t_ref, loss_ref, dx_ref, df_ref, dgt_ref, dw_ref):
        first = pl.program_id(0) == 0
        f = f_ref[...]
        gt = gt_ref[...]
        w = w_ref[...]
        x3 = x2_ref[...] + gt * f
        rstd = lax.rsqrt(jnp.mean(x3 * x3, axis=-1, keepdims=True) + RMS_EPS)
        n = x3 * rstd
        e = n * w - t_ref[...]
        part = 0.5 * jnp.sum(jnp.mean(e * e, axis=-1, keepdims=True), axis=0, keepdims=True)
        _acc(loss_ref, jnp.broadcast_to(part, (SUBLANES, LANES)), first)
        dy = e * (1.0 / D)
        _acc(dw_ref, _colsum(dy * n), first)
        dn = dy * w
        dx = rstd * (dn - n * jnp.mean(dn * n, axis=-1, keepdims=True))
        dx_ref[...] = dx
        df_ref[...] = (dx * gt).astype(BF16)
        _acc(dgt_ref, _colsum(dx * f), first)

    vec = _full((1, D))
    vshape = jax.ShapeDtypeStruct((1, D), F32)
    return pl.pallas_call(
        body, name="final_loss", grid=(S // tm,),
        in_specs=[_rows(tm, D), _rows(tm, D), vec, vec, _rows(tm, D)],
        out_specs=[_full((SUBLANES, LANES)), _rows(tm, D), _rows(tm, D), vec, vec],
        out_shape=[jax.ShapeDtypeStruct((SUBLANES, LANES), F32), jax.ShapeDtypeStruct((S, D), F32),
                   jax.ShapeDtypeStruct((S, D), BF16), vshape, vshape],
        compiler_params=_cparams(("arbitrary",)))(x2, f, gt2, nfw, target)


def _gate_fwd(P, bga, bgr, y_att, y_rwkv, tm=512):
    S = P.shape[0]

    def body(pa_ref, pr_ref, ba_ref, br_ref, ya_ref, yr_ref, mix_ref):
        ga = jax.nn.sigmoid(pa_ref[...] + ba_ref[...])
        gr = jax.nn.sigmoid(pr_ref[...] + br_ref[...])
        mix_ref[...] = (ga * ya_ref[...] + gr * yr_ref[...]).astype(BF16)

    vec = _full((1, D))
    return pl.pallas_call(
        body, name="gate_fwd", grid=(S // tm,),
        in_specs=[_rows(tm, D, C_GA // D), _rows(tm, D, C_GR // D), vec, vec, _rows(tm, D), _rows(tm, D)],
        out_specs=_rows(tm, D), out_shape=jax.ShapeDtypeStruct((S, D), BF16),
        compiler_params=_cparams(("parallel",)))(P, P, bga, bgr, y_att, y_rwkv)


def _gate_bwd(dmix, P, bga, bgr, y_att, y_rwkv, tm=512):
    S = P.shape[0]

    def body(dm_ref, pa_ref, pr_ref, ba_ref, br_ref, ya_ref, yr_ref, dya_ref, dyr_ref, dpa_ref, dpr_ref, dba_ref, dbr_ref):
        first = pl.program_id(0) == 0
        dm = dm_ref[...]
        ga = jax.nn.sigmoid(pa_ref[...] + ba_ref[...])
        gr = jax.nn.sigmoid(pr_ref[...] + br_ref[...])
        dya_ref[...] = (dm * ga).astype(BF16)
        dyr_ref[...] = (dm * gr).astype(BF16)
        dpa = dm * ya_ref[...] * ga * (1.0 - ga)
        dpr = dm * yr_ref[...] * gr * (1.0 - gr)
        dpa_ref[...] = dpa.astype(BF16)
        dpr_ref[...] = dpr.astype(BF16)
        _acc(dba_ref, _colsum(dpa), first)
        _acc(dbr_ref, _colsum(dpr), first)

    vec = _full((1, D))
    row = _rows(tm, D)
    rshape = jax.ShapeDtypeStruct((S, D), BF16)
    vshape = jax.ShapeDtypeStruct((1, D), F32)
    return pl.pallas_call(
        body, name="gate_bwd", grid=(S // tm,),
        in_specs=[row, _rows(tm, D, C_GA // D), _rows(tm, D, C_GR // D), vec, vec, row, row],
        out_specs=[row, row, row, row, vec, vec],
        out_shape=[rshape, rshape, rshape, rshape, vshape, vshape],
        compiler_params=_cparams(("arbitrary",)))(dmix, P, P, bga, bgr, y_att, y_rwkv)


CONV_TN = D_FF // 2
HALO = 16


def _conv_fwd(u, conv_w8, conv_b, tm=256, tn=CONV_TN):
    S = u.shape[0]
    nj = D_FF // tn

    def conv(u_ref, h_ref, w_ref, b_ref, first):
        u = u_ref[...].astype(F32)
        h = h_ref[...].astype(F32)
        w = w_ref[...]
        return b_ref[...] + w[0:1] * _shift_down(u, h, 2, first) + w[1:2] * _shift_down(u, h, 1, first) + w[2:3] * u

    def body(ug_ref, hg_ref, uv_ref, hv_ref, wg_ref, wv_ref, bg_ref, bv_ref, act_ref):
        first = pl.program_id(0) == 0
        g = conv(ug_ref, hg_ref, wg_ref, bg_ref, first)
        v = conv(uv_ref, hv_ref, wv_ref, bv_ref, first)
        act_ref[...] = (g * jax.nn.sigmoid(g) * v).astype(BF16)

    blk = lambda off: pl.BlockSpec((tm, tn), lambda i, j: (i, j + off))
    halo = lambda off: pl.BlockSpec((HALO, tn), lambda i, j: (jnp.maximum(i * (tm // HALO) - 1, 0), j + off))
    wsp = lambda off: pl.BlockSpec((SUBLANES, tn), lambda i, j: (0, j + off))
    bsp = lambda off: pl.BlockSpec((1, tn), lambda i, j: (0, j + off))
    return pl.pallas_call(
        body, name="conv_fwd", grid=(S // tm, nj),
        in_specs=[blk(0), halo(0), blk(nj), halo(nj), wsp(0), wsp(nj), bsp(0), bsp(nj)],
        out_specs=pl.BlockSpec((tm, tn), lambda i, j: (i, j)),
        out_shape=jax.ShapeDtypeStruct((S, D_FF), BF16),
        compiler_params=_cparams(("parallel", "parallel")))(u, u, u, u, conv_w8, conv_w8, conv_b, conv_b)


def _conv_bwd_a(dact, u, conv_w8, conv_b, tm=256, tn=CONV_TN):
    S = u.shape[0]
    nj = D_FF // tn

    def half(u_ref, h_ref, w_ref, b_ref, first):
        u = u_ref[...].astype(F32)
        h = h_ref[...].astype(F32)
        w = w_ref[...]
        u2, u1 = _shift_down(u, h, 2, first), _shift_down(u, h, 1, first)
        return b_ref[...] + w[0:1] * u2 + w[1:2] * u1 + w[2:3] * u, (u2, u1, u)

    def wgrad(d, taps):
        z = jnp.zeros((SUBLANES - 3, d.shape[1]), F32)
        return jnp.concatenate([_colsum(d * taps[0]), _colsum(d * taps[1]), _colsum(d * taps[2]), z], axis=0)

    def body(da_ref, ug_ref, hg_ref, uv_ref, hv_ref, wg_ref, wv_ref, bg_ref, bv_ref,
             d_ref, dwg_ref, dwv_ref, dbg_ref, dbv_ref):
        first = pl.program_id(1) == 0
        g, tg = half(ug_ref, hg_ref, wg_ref, bg_ref, first)
        v, tv = half(uv_ref, hv_ref, wv_ref, bv_ref, first)
        da = da_ref[...].astype(F32)
        sg = jax.nn.sigmoid(g)
        dg = da * v * (sg * (1.0 + g * (1.0 - sg)))
        dv = da * (g * sg)
        d_ref[0] = dg.astype(BF16)
        d_ref[1] = dv.astype(BF16)
        _acc(dwg_ref, wgrad(dg, tg), first)
        _acc(dwv_ref, wgrad(dv, tv), first)
        _acc(dbg_ref, _colsum(dg), first)
        _acc(dbv_ref, _colsum(dv), first)

    blk = lambda off: pl.BlockSpec((tm, tn), lambda j, i: (i, j + off))
    halo = lambda off: pl.BlockSpec((HALO, tn), lambda j, i: (jnp.maximum(i * (tm // HALO) - 1, 0), j + off))
    wsp = lambda off: pl.BlockSpec((SUBLANES, tn), lambda j, i: (0, j + off))
    bsp = lambda off: pl.BlockSpec((1, tn), lambda j, i: (0, j + off))
    f = jax.ShapeDtypeStruct
    outs = pl.pallas_call(
        body, name="conv_bwd_a", grid=(nj, S // tm),
        in_specs=[pl.BlockSpec((tm, tn), lambda j, i: (i, j)), blk(0), halo(0), blk(nj), halo(nj), wsp(0), wsp(nj), bsp(0), bsp(nj)],
        out_specs=[pl.BlockSpec((2, tm, tn), lambda j, i: (0, i, j)),
                   pl.BlockSpec((SUBLANES, tn), lambda j, i: (0, j)), pl.BlockSpec((SUBLANES, tn), lambda j, i: (0, j)),
                   pl.BlockSpec((1, tn), lambda j, i: (0, j)), pl.BlockSpec((1, tn), lambda j, i: (0, j))],
        out_shape=[f((2, S, D_FF), BF16), f((SUBLANES, D_FF), F32), f((SUBLANES, D_FF), F32),
                   f((1, D_FF), F32), f((1, D_FF), F32)],
        compiler_params=_cparams(("parallel", "arbitrary")))(dact, u, u, u, u, conv_w8, conv_w8, conv_b, conv_b)
    return outs


def _conv_bwd_b(duc, conv_w8, tm=256, tn=CONV_TN):
    _, S, W = duc.shape
    nj = W // tn
    n_rows = S // tm

    def body(d_ref, h_ref, w_ref, o_ref):
        last = pl.program_id(0) == n_rows - 1
        d = d_ref[...].astype(F32)
        h = h_ref[...].astype(F32)
        w = w_ref[...]
        o_ref[...] = (w[2:3] * d + w[1:2] * _shift_up(d, h, 1, last) + w[0:1] * _shift_up(d, h, 2, last)).astype(BF16)

    last_tile = S // HALO - 1
    return pl.pallas_call(
        body, name="conv_bwd_b", grid=(n_rows, 2 * nj),
        in_specs=[pl.BlockSpec((None, tm, tn), lambda i, j: (j // nj, i, j % nj)),
                  pl.BlockSpec((None, HALO, tn), lambda i, j: (j // nj, jnp.minimum((i + 1) * (tm // HALO), last_tile), j % nj)),
                  pl.BlockSpec((SUBLANES, tn), lambda i, j: (0, j))],
        out_specs=pl.BlockSpec((tm, tn), lambda i, j: (i, j)),
        out_shape=jax.ShapeDtypeStruct((S, 2 * W), BF16),
        compiler_params=_cparams(("parallel", "parallel")))(duc, duc, conv_w8)


ATT_SCALE = HEAD ** -0.5
NEG = -1e30
ATT_PAIRS = ATT_HEADS // 2


def _att_rows(n, d, S):
    per = S // (QBLK * d)
    r, m = n // per, n % per
    cur = pl.ds(m * (QBLK * d) + r, QBLK, stride=d)
    prv = pl.ds(jnp.maximum(m - 1, 0) * (QBLK * d) + r, QBLK, stride=d)
    return cur, prv, m > 0


def _att_slab(g, j):
    return (C_ATT + g * 3 * ATT_W + j * ATT_W) // LANES


def _heads(x):
    return x[:, 0:HEAD], x[:, HEAD:2 * HEAD]


ATT_NB = 4


def _stack(tiles):
    return jnp.concatenate([t[None] for t in tiles], axis=0)


def _att_operands(i, d, S, *sources):
    rows, has = [], []
    tiles = [[] for _ in sources]
    for bb in range(ATT_NB):
        cur, prv, has_prev = _att_rows(i * ATT_NB + bb, d, S)
        rows.append((cur, prv))
        has.append(has_prev)
        for t, (ref, use_cur) in zip(tiles, sources):
            t += _heads(ref[cur if use_cur else prv, :].astype(BF16))
    return rows, has, [_stack(t) for t in tiles]


def _att_mask(s_c, s_p, has_prev):
    qi = lax.broadcasted_iota(jnp.int32, (QBLK, QBLK), 0)
    kj = lax.broadcasted_iota(jnp.int32, (QBLK, QBLK), 1)
    s_c = jnp.where(kj <= qi, s_c * ATT_SCALE, NEG)
    s_p = jnp.where(jnp.logical_and(kj >= qi, has_prev), s_p * ATT_SCALE, NEG)
    return s_c, s_p


def _att_fwd(P, g):
    S = P.shape[0]
    d = ATT_PATTERNS[g][1]

    def body(q_ref, k_ref, v_ref, o_ref, l_ref):
        def group(i, carry):
            rows, has, (q, kc, kp, vc, vp) = _att_operands(i, d, S, (q_ref, True), (k_ref, True), (k_ref, False),
                                                           (v_ref, True), (v_ref, False))
            s_c_all, s_p_all = _dot16(q, kc, "nt"), _dot16(q, kp, "nt")
            p_c, p_p, den, lse = [], [], [], []
            for e in range(2 * ATT_NB):
                s_c, s_p = _att_mask(s_c_all[e], s_p_all[e], has[e // 2])
                m = jnp.maximum(jnp.max(s_c, axis=1, keepdims=True), jnp.max(s_p, axis=1, keepdims=True))
                pc, pp = jnp.exp(s_c - m), jnp.exp(s_p - m)
                den.append(jnp.sum(pc, axis=1, keepdims=True) + jnp.sum(pp, axis=1, keepdims=True))
                lse.append(jnp.broadcast_to(m + jnp.log(den[e]), (QBLK, HEAD)))
                p_c.append(pc)
                p_p.append(pp)
            num = _dot16(_stack(p_c), vc, "nn") + _dot16(_stack(p_p), vp, "nn")
            for bb, (cur, _) in enumerate(rows):
                o_ref[cur, :] = jnp.concatenate([num[2 * bb] / den[2 * bb], num[2 * bb + 1] / den[2 * bb + 1]], axis=1)
                l_ref[cur, :] = jnp.concatenate(lse[2 * bb:2 * bb + 2], axis=1)
            return carry

        lax.fori_loop(0, S // QBLK // ATT_NB, group, 0)

    slab = lambda j: pl.BlockSpec((S, LANES), lambda i: (0, _att_slab(g, j) + i))
    out = pl.BlockSpec((S, LANES), lambda i: (0, i))
    shp = jax.ShapeDtypeStruct((S, ATT_W), F32)
    return pl.pallas_call(body, name=f"att_fwd_g{g}", grid=(ATT_PAIRS,), in_specs=[slab(0), slab(1), slab(2)],
                          out_specs=[out, out], out_shape=[shp, shp], compiler_params=_cparams(("parallel",)))(P, P, P)


def _att_bwd(P, o, l, do, dl, g):
    S = P.shape[0]
    d = ATT_PATTERNS[g][1]

    def body(q_ref, k_ref, v_ref, o_ref, l_ref, do_ref, dl_ref, dq_ref, dk_ref, dv_ref, dq_acc, dk_acc, dv_acc):
        dk_acc[...] = jnp.zeros_like(dk_acc)
        dv_acc[...] = jnp.zeros_like(dv_acc)

        def group(i, carry):
            rows, has, (q, kc, kp, vc, vp, dob) = _att_operands(
                i, d, S, (q_ref, True), (k_ref, True), (k_ref, False), (v_ref, True), (v_ref, False), (do_ref, True))
            s_c_all, s_p_all = _dot16(q, kc, "nt"), _dot16(q, kp, "nt")
            dp_c_all, dp_p_all = _dot16(dob, vc, "nt"), _dot16(dob, vp, "nt")
            p_c, p_p, ds_c, ds_p = [], [], [], []
            for bb, (cur, _) in enumerate(rows):
                dd2 = do_ref[cur, :] * o_ref[cur, :] - dl_ref[cur, :]
                for h, (dd, lse) in enumerate(zip(_heads(dd2), _heads(l_ref[cur, :]))):
                    e = 2 * bb + h
                    s_c, s_p = _att_mask(s_c_all[e], s_p_all[e], has[bb])
                    pc, pp = jnp.exp(s_c - lse[:, 0:1]), jnp.exp(s_p - lse[:, 0:1])
                    delta = jnp.sum(dd, axis=1, keepdims=True)
                    p_c.append(pc)
                    p_p.append(pp)
                    ds_c.append(pc * (dp_c_all[e] - delta) * ATT_SCALE)
                    ds_p.append(pp * (dp_p_all[e] - delta) * ATT_SCALE)
            p_c, p_p, ds_c, ds_p = map(_stack, (p_c, p_p, ds_c, ds_p))
            dq = _dot16(ds_c, kc, "nn") + _dot16(ds_p, kp, "nn")
            dk_c, dk_p = _dot16(ds_c, q, "tn"), _dot16(ds_p, q, "tn")
            dv_c, dv_p = _dot16(p_c, dob, "tn"), _dot16(p_p, dob, "tn")
            pair = lambda x, bb: jnp.concatenate([x[2 * bb], x[2 * bb + 1]], axis=1)
            for bb, (cur, prv) in enumerate(rows):
                dq_acc[cur, :] = pair(dq, bb)
                dk_acc[cur, :] += pair(dk_c, bb)
                dv_acc[cur, :] += pair(dv_c, bb)
                dk_acc[prv, :] += pair(dk_p, bb)
                dv_acc[prv, :] += pair(dv_p, bb)
            return carry

        lax.fori_loop(0, S // QBLK // ATT_NB, group, 0)
        dq_ref[...] = dq_acc[...].astype(BF16)
        dk_ref[...] = dk_acc[...].astype(BF16)
        dv_ref[...] = dv_acc[...].astype(BF16)

    slab = lambda j: pl.BlockSpec((S, LANES), lambda i: (0, _att_slab(g, j) + i))
    blk128 = pl.BlockSpec((S, LANES), lambda i: (0, i))
    shp = jax.ShapeDtypeStruct((S, ATT_W), BF16)
    return pl.pallas_call(body, name=f"att_bwd_g{g}", grid=(ATT_PAIRS,),
                          in_specs=[slab(0), slab(1), slab(2)] + [blk128] * 4, out_specs=[blk128] * 3, out_shape=[shp] * 3,
                          scratch_shapes=[pltpu.VMEM((S, LANES), F32)] * 3,
                          compiler_params=_cparams(("parallel",)))(P, P, P, o, l, do, dl)


def _att_weights(l_refs):
    l0, l1, l2 = [r[...] for r in l_refs]
    m = jnp.maximum(jnp.maximum(l0, l1), l2)
    e = (jnp.exp(l0 - m), jnp.exp(l1 - m), jnp.exp(l2 - m))
    inv = 1.0 / (e[0] + e[1] + e[2])
    return [x * inv for x in e]


def _att_combine_fwd(os, ls, tm=512):
    S = os[0].shape[0]

    def body(o0, o1, o2, l0, l1, l2, a_ref):
        w = _att_weights((l0, l1, l2))
        a_ref[...] = (w[0] * o0[...] + w[1] * o1[...] + w[2] * o2[...]).astype(BF16)

    row = _rows(tm, ATT_W)
    return pl.pallas_call(body, name="att_combine_fwd", grid=(S // tm,), in_specs=[row] * 6, out_specs=row,
                          out_shape=jax.ShapeDtypeStruct((S, ATT_W), BF16),
                          compiler_params=_cparams(("parallel",)))(*os, *ls)


def _att_combine_bwd(da, os, ls, tm=512):
    S = da.shape[0]

    def body(da_ref, o0, o1, o2, l0, l1, l2, *out_refs):
        da = da_ref[...]
        w = _att_weights((l0, l1, l2))
        dw = (da * o0[...], da * o1[...], da * o2[...])
        mean = w[0] * dw[0] + w[1] * dw[1] + w[2] * dw[2]
        for g in range(3):
            out_refs[g][...] = w[g] * da
            out_refs[3 + g][...] = w[g] * (dw[g] - mean)

    row = _rows(tm, ATT_W)
    shp = jax.ShapeDtypeStruct((S, ATT_W), F32)
    return pl.pallas_call(body, name="att_combine_bwd", grid=(S // tm,), in_specs=[row] * 7, out_specs=[row] * 6,
                          out_shape=[shp] * 6, compiler_params=_cparams(("parallel",)))(da, *os, *ls)


@jax.custom_vjp
def _bdot(a, b):
    return jnp.dot(a.astype(BF16), b.astype(BF16), preferred_element_type=F32)


def _bdot_fwd(a, b):
    return _bdot(a, b), (a, b)


def _bdot_bwd(res, ct):
    a, b = res
    ct16 = ct.astype(BF16)
    da = lax.dot_general(ct16, b.astype(BF16), (((1,), (1,)), ((), ())), preferred_element_type=F32)
    db = lax.dot_general(a.astype(BF16), ct16, (((0,), (0,)), ((), ())), preferred_element_type=F32)
    return da, db


_bdot.defvjp(_bdot_fwd, _bdot_bwd)


def _two_piece_dot(x, m):
    hi = x.astype(BF16)
    lo = (x - hi.astype(F32)).astype(BF16)
    return jnp.dot(hi, m, preferred_element_type=F32) + jnp.dot(lo, m, preferred_element_type=F32)


def _head_sum_impl(x):
    sel = (lax.broadcasted_iota(jnp.int32, (D, LANES), 0) // HEAD == lax.broadcasted_iota(jnp.int32, (D, LANES), 1)).astype(BF16)
    sel_t = (lax.broadcasted_iota(jnp.int32, (LANES, D), 1) // HEAD == lax.broadcasted_iota(jnp.int32, (LANES, D), 0)).astype(BF16)
    return _two_piece_dot(_two_piece_dot(x, sel), sel_t)


@jax.custom_vjp
def _head_sum(x):
    return _head_sum_impl(x)


_head_sum.defvjp(lambda x: (_head_sum_impl(x), None), lambda _, ct: (_head_sum_impl(ct),))


def _softplus(z):
    return jnp.maximum(z, 0.0) + jnp.log(1.0 + jnp.exp(-jnp.abs(z)))


def _rwkv_prep_fn(zr, zrp, zk, zkp, zv, zvp, zl, zlp, mu_r, mu_k, mu_v, mu_l, w0, a0, k_k, k_a, w2, a2, g2p):
    r = zr + (zrp - zr) * mu_r
    k = zk + (zkp - zk) * mu_k
    v = zv + (zvp - zv) * mu_v
    lo = zl + (zlp - zl) * mu_l
    w_low, a_low, g_low = lo[:, 0:LORA_W], lo[:, LORA_W:LORA_W + LORA_A], lo[:, LANES:LANES + G_PAD]
    w_log = -_softplus(-(w0 + _bdot(jnp.tanh(w_low), w2))) - 0.5
    decay = -jnp.exp(w_log)
    a = jax.nn.sigmoid(a0 + _bdot(a_low, a2))
    g = _bdot(jax.nn.sigmoid(g_low), g2p)
    kmod = k * (1.0 + (a - 1.0) * k_a)
    kk = k * k_k
    kk = kk / jnp.maximum(jnp.sqrt(_head_sum(kk * kk)), 1e-12)
    return r, decay, kmod, v, -kk, kk * a, g


def _rwkv_prep_specs(tm, blk=lambda i: i):
    vec = _full((1, D))
    rows = lambda w, col: pl.BlockSpec((tm, w), lambda i: (blk(i), col))
    prev = lambda w, col: pl.BlockSpec((SUBLANES, w), lambda i: (jnp.maximum(blk(i) * (tm // SUBLANES) - 1, 0), col))
    slabs = []
    for col in (C_R // D, C_K // D, C_V // D):
        slabs += [rows(D, col), prev(D, col)]
    slabs += [rows(LORA_PAD, C_LORA // LORA_PAD), prev(LORA_PAD, C_LORA // LORA_PAD)]
    params = [vec, vec, vec, _full((1, LORA_PAD)), vec, vec, vec, vec,
              _full((LORA_W, D)), _full((LORA_A, D)), _full((G_PAD, D))]
    return slabs, params


def _prep_inputs(refs, first):
    vals = []
    for s in range(4):
        z = refs[2 * s][...]
        vals += [z, _shift_down(z, refs[2 * s + 1][...], 1, first)]
    return vals + [r[...] for r in refs[8:19]]


def _rwkv_prep(P, params, tm=256):
    S = P.shape[0]
    slabs, pspecs = _rwkv_prep_specs(tm)

    def body(*refs):
        outs = _rwkv_prep_fn(*_prep_inputs(refs, pl.program_id(0) == 0))
        for o_ref, val in zip(refs[19:], outs):
            o_ref[...] = val

    shp = jax.ShapeDtypeStruct((S, D), F32)
    return pl.pallas_call(body, name="rwkv_prep", grid=(S // tm,), in_specs=slabs + pspecs,
                          out_specs=[_rows(tm, D)] * 7, out_shape=[shp] * 7,
                          compiler_params=_cparams(("parallel",)))(*([P] * 8), *params)


def _rwkv_prep_bwd(P, params, cts_a, cts_b, tm=128):
    S = P.shape[0]
    nblk = S // tm
    blk = lambda i: nblk - 1 - i
    slabs, pspecs = _rwkv_prep_specs(tm, blk)
    has_b = [c is not None for c in cts_b]
    n_ct = 7 + sum(has_b)

    def body(*refs):
        start = pl.program_id(0) == 0
        ins = _prep_inputs(refs, pl.program_id(0) == nblk - 1)
        ct_refs = refs[19:19 + n_ct]
        out_refs = refs[19 + n_ct:19 + n_ct + 15]
        carry_refs = refs[19 + n_ct + 15:]

        @pl.when(start)
        def _():
            for c_ref in carry_refs:
                c_ref[...] = jnp.zeros_like(c_ref)

        cts, pos = [], 7
        for i in range(7):
            c = ct_refs[i][...]
            if has_b[i]:
                c = c + ct_refs[pos][...]
                pos += 1
            cts.append(c)
        _, vjp = jax.vjp(_rwkv_prep_fn, *ins)
        grads = vjp(tuple(cts))
        for s in range(4):
            shifted = grads[2 * s + 1]
            out_refs[s][...] = (grads[2 * s] + _shift_up(shifted, carry_refs[s][...], 1, start)).astype(BF16)
            carry_refs[s][0:1, :] = shifted[0:1, :]
        for i in range(11):
            _acc(out_refs[4 + i], grads[8 + i], start)

    ct_in = list(cts_a) + [c for c in cts_b if c is not None]
    row = lambda w: pl.BlockSpec((tm, w), lambda i: (blk(i), 0))
    f = jax.ShapeDtypeStruct
    zshapes = [f((S, D), BF16)] * 3 + [f((S, LORA_PAD), BF16)]
    pshapes = [f((1, D), F32)] * 3 + [f((1, LORA_PAD), F32)] + [f((1, D), F32)] * 4 + [f((LORA_W, D), F32), f((LORA_A, D), F32), f((G_PAD, D), F32)]
    return pl.pallas_call(
        body, name="rwkv_prep_bwd", grid=(nblk,),
        in_specs=slabs + pspecs + [row(D)] * n_ct,
        out_specs=[row(D), row(D), row(D), row(LORA_PAD)] + pspecs,
        out_shape=zshapes + pshapes,
        scratch_shapes=[pltpu.VMEM((SUBLANES, D), F32)] * 3 + [pltpu.VMEM((SUBLANES, LORA_PAD), F32)],
        compiler_params=_cparams(("arbitrary",)))(*([P] * 8), *params, *ct_in)


def _rwkv_post_fn(y, r, kmod, v, g, lnx_w, lnx_b, r_k):
    mean = _head_sum(y) * (1.0 / HEAD)
    yc = y - mean
    var = _head_sum(yc * yc) * (1.0 / HEAD)
    yn = yc * lax.rsqrt(var + GN_EPS) * lnx_w + lnx_b
    bonus = _head_sum(r * kmod * r_k) * v
    return (yn + bonus) * g


def _rwkv_post(y, r, kmod, v, g, lnx_w, lnx_b, r_k, tm=256):
    S = y.shape[0]

    def body(y_ref, r_ref, k_ref, v_ref, g_ref, w_ref, b_ref, rk_ref, o_ref):
        o_ref[...] = _rwkv_post_fn(y_ref[...], r_ref[...], k_ref[...], v_ref[...], g_ref[...],
                                   w_ref[...], b_ref[...], rk_ref[...]).astype(BF16)

    row, vec = _rows(tm, D), _full((1, D))
    return pl.pallas_call(body, name="rwkv_post", grid=(S // tm,), in_specs=[row] * 5 + [vec] * 3, out_specs=row,
                          out_shape=jax.ShapeDtypeStruct((S, D), BF16),
                          compiler_params=_cparams(("parallel",)))(y, r, kmod, v, g, lnx_w, lnx_b, r_k)


def _rwkv_post_bwd(drw, y, r, kmod, v, g, lnx_w, lnx_b, r_k, tm=256):
    S = y.shape[0]

    def body(d_ref, y_ref, r_ref, k_ref, v_ref, g_ref, w_ref, b_ref, rk_ref, *out_refs):
        first = pl.program_id(0) == 0
        _, vjp = jax.vjp(_rwkv_post_fn, y_ref[...], r_ref[...], k_ref[...], v_ref[...], g_ref[...],
                         w_ref[...], b_ref[...], rk_ref[...])
        grads = vjp(d_ref[...])
        for i in range(5):
            out_refs[i][...] = grads[i]
        for i in range(5, 8):
            _acc(out_refs[i], grads[i], first)

    row, vec = _rows(tm, D), _full((1, D))
    f = jax.ShapeDtypeStruct
    return pl.pallas_call(body, name="rwkv_post_bwd", grid=(S // tm,), in_specs=[row] * 6 + [vec] * 3,
                          out_specs=[row] * 5 + [vec] * 3, out_shape=[f((S, D), F32)] * 5 + [f((1, D), F32)] * 3,
                          compiler_params=_cparams(("arbitrary",)))(drw, y, r, kmod, v, g, lnx_w, lnx_b, r_k)


CHUNK = 64
CHUNK_TB = 256
_DOT_DIMS = {"nn": (((2,), (1,)), ((0,), (0,))), "nt": (((2,), (2,)), ((0,), (0,))), "tn": (((1,), (1,)), ((0,), (0,)))}


def _dot16(x, y, mode):
    return lax.dot_general(x.astype(BF16), y.astype(BF16), _DOT_DIMS[mode], preferred_element_type=F32)


@functools.partial(jax.custom_vjp, nondiff_argnums=(2,))
def _mm16(x, y, mode):
    return _dot16(x, y, mode)


def _mm16_fwd(x, y, mode):
    return _dot16(x, y, mode), (x, y)


def _mm16_bwd(mode, res, ct):
    x, y = res
    if mode == "nn":
        return _dot16(ct, y, "nt"), _dot16(x, ct, "tn")
    if mode == "nt":
        return _dot16(ct, y, "nn"), _dot16(ct, x, "tn")
    return _dot16(y, ct, "nt"), _dot16(x, ct, "nn")


_mm16.defvjp(_mm16_fwd, _mm16_bwd)


def _tri_sum(x, upper):
    T = x.shape[0]
    i = lax.broadcasted_iota(jnp.int32, (T, T), 0)
    j = lax.broadcasted_iota(jnp.int32, (T, T), 1)
    tri = ((j >= i) if upper else (i >= j)).astype(BF16)
    out, rest = None, x
    for _ in range(3):
        piece = rest.astype(BF16)
        rest = rest - piece.astype(F32)
        part = jnp.dot(tri, piece, preferred_element_type=F32)
        out = part if out is None else out + part
    return out


@jax.custom_vjp
def _cumsum_rows(x):
    return _tri_sum(x, False)


_cumsum_rows.defvjp(lambda x: (_tri_sum(x, False), None), lambda _, ct: (_tri_sum(ct, True),))


def _rows_to_cols(row):
    per_head = jnp.concatenate([row[:, h * HEAD:(h + 1) * HEAD] for h in range(N_HEADS)], axis=0)
    eye = (lax.broadcasted_iota(jnp.int32, (HEAD, HEAD), 0) == lax.broadcasted_iota(jnp.int32, (HEAD, HEAD), 1)).astype(F32)
    cols = lax.dot_general(eye, per_head, (((1,), (1,)), ((), ())), precision=lax.Precision.HIGHEST,
                           preferred_element_type=F32)
    return jnp.concatenate([cols[:, h:h + 1][None] for h in range(N_HEADS)], axis=0)


def _per_head(x):
    return jnp.concatenate([x[:, h * HEAD:(h + 1) * HEAD][None] for h in range(N_HEADS)], axis=0)


def _chunk_fn(st0, r, lw, k, v, a, b):
    T = r.shape[0]
    cl = _cumsum_rows(lw)
    cl_end = cl[T - 1:T, :]
    inv = jnp.exp(-cl)
    to_end = jnp.exp(cl_end - cl)
    ah, rh, bh, kh, be, ke, v3 = [_per_head(x) for x in
                                  (a * jnp.exp(cl - lw), r * jnp.exp(cl), b * inv, k * inv, b * to_end, k * to_end, v)]
    i = lax.broadcasted_iota(jnp.int32, (N_HEADS, T, T), 1)
    j = lax.broadcasted_iota(jnp.int32, (N_HEADS, T, T), 2)
    a_ab = jnp.where(i > j, _mm16(ah, bh, "nt"), 0.0)
    a_ak = jnp.where(i > j, _mm16(ah, kh, "nt"), 0.0)
    m_rb = jnp.where(i >= j, _mm16(rh, bh, "nt"), 0.0)
    m_rk = jnp.where(i >= j, _mm16(rh, kh, "nt"), 0.0)
    rhs = _mm16(ah, st0, "nn") + _mm16(a_ak, v3, "nn")
    power, solve, n = a_ab, (i == j).astype(F32) + a_ab, 1
    while 2 * n < T:
        power = _mm16(power, power, "nn")
        solve = solve + _mm16(solve, power, "nn")
        n *= 2
    sa = _mm16(solve, rhs, "nn")
    y3 = _mm16(rh, st0, "nn") + _mm16(m_rb, sa, "nn") + _mm16(m_rk, v3, "nn")
    st_end = _rows_to_cols(jnp.exp(cl_end)) * st0 + _mm16(be, sa, "tn") + _mm16(ke, v3, "tn")
    return jnp.concatenate([y3[h] for h in range(N_HEADS)], axis=1), st_end


def _hosted_exchange(refs, n, broadcast, grid):
    if n == 0:
        return lambda: None
    start, wait = _exchange_ops(refs[:n], refs[n:2 * n], *refs[2 * n:], broadcast)
    first = functools.reduce(jnp.logical_and, [pl.program_id(a) == 0 for a in range(len(grid))])
    last = functools.reduce(jnp.logical_and, [pl.program_id(a) == g - 1 for a, g in enumerate(grid)])
    pl.when(first)(start)
    return lambda: pl.when(last)(wait)


def _cscan_fwd(r, lw, k, v, a, b, gather=()):
    S = r.shape[0]
    per_blk = CHUNK_TB // CHUNK
    n_x = len(gather)
    nblk = S // CHUNK_TB

    def body(*refs):
        r_ref, lw_ref, k_ref, v_ref, a_ref, b_ref = refs[:6]
        y_ref, ck_ref = refs[6 + n_x:8 + n_x]
        st_ref = refs[8 + 2 * n_x]
        finish = _hosted_exchange(refs[6:6 + n_x] + refs[8 + n_x:8 + 2 * n_x] + refs[9 + 2 * n_x:], n_x, True, (nblk,))

        @pl.when(pl.program_id(0) == 0)
        def _():
            st_ref[...] = jnp.zeros_like(st_ref)

        def chunk(c, carry):
            rows = pl.ds(pl.multiple_of(c * CHUNK, CHUNK), CHUNK)
            st0 = st_ref[...]
            ck_ref[c] = st0
            y, st_end = _chunk_fn(st0, r_ref[rows, :], lw_ref[rows, :], k_ref[rows, :],
                                  v_ref[rows, :], a_ref[rows, :], b_ref[rows, :])
            y_ref[rows, :] = y
            st_ref[...] = st_end
            return carry

        lax.fori_loop(0, per_blk, chunk, 0)
        finish()

    blk = _rows(CHUNK_TB, D)
    any_spec = pl.BlockSpec(memory_space=pl.ANY)
    outs = pl.pallas_call(
        body, name="scan_fwd", grid=(nblk,), in_specs=[blk] * 6 + [any_spec] * n_x,
        out_specs=[blk, pl.BlockSpec((per_blk, N_HEADS, HEAD, HEAD), lambda i: (i, 0, 0, 0))] + [any_spec] * n_x,
        out_shape=[jax.ShapeDtypeStruct((S, D), F32), jax.ShapeDtypeStruct((S // CHUNK, N_HEADS, HEAD, HEAD), F32)]
        + _exchange_shapes(gather, True),
        scratch_shapes=[pltpu.VMEM((N_HEADS, HEAD, HEAD), F32)] + (_exchange_scratch(n_x) if n_x else []),
        compiler_params=_cparams(("arbitrary",)))(r, lw, k, v, a, b, *gather)
    return outs[0], outs[1], outs[2:]


def _cscan_bwd(r, lw, k, v, a, b, ckpt, dy, scatter=()):
    S = r.shape[0]
    per_blk = CHUNK_TB // CHUNK
    nblk = S // CHUNK_TB
    n_x = len(scatter)

    def body(*refs):
        r_ref, lw_ref, k_ref, v_ref, a_ref, b_ref, ck_ref, dy_ref = refs[:8]
        out_refs = refs[8 + n_x:14 + n_x]
        ds_ref = refs[14 + 2 * n_x]
        finish = _hosted_exchange(refs[8:8 + n_x] + refs[14 + n_x:14 + 2 * n_x] + refs[15 + 2 * n_x:], n_x, False, (nblk,))

        @pl.when(pl.program_id(0) == 0)
        def _():
            ds_ref[...] = jnp.zeros_like(ds_ref)

        def chunk(cc, carry):
            c = per_blk - 1 - cc
            rows = pl.ds(pl.multiple_of(c * CHUNK, CHUNK), CHUNK)
            ins = (ck_ref[c], r_ref[rows, :], lw_ref[rows, :], k_ref[rows, :], v_ref[rows, :], a_ref[rows, :], b_ref[rows, :])
            _, vjp = jax.vjp(_chunk_fn, *ins)
            grads = vjp((dy_ref[rows, :], ds_ref[...]))
            ds_ref[...] = grads[0]
            for o_ref, g in zip(out_refs, grads[1:]):
                o_ref[rows, :] = g
            return carry

        lax.fori_loop(0, per_blk, chunk, 0)
        finish()

    blk = pl.BlockSpec((CHUNK_TB, D), lambda i: (nblk - 1 - i, 0))
    any_spec = pl.BlockSpec(memory_space=pl.ANY)
    shp = jax.ShapeDtypeStruct((S, D), F32)
    outs = pl.pallas_call(
        body, name="scan_bwd", grid=(nblk,),
        in_specs=[blk] * 6 + [pl.BlockSpec((per_blk, N_HEADS, HEAD, HEAD), lambda i: (nblk - 1 - i, 0, 0, 0)), blk]
        + [any_spec] * n_x,
        out_specs=[blk] * 6 + [any_spec] * n_x, out_shape=[shp] * 6 + _exchange_shapes(scatter, False),
        scratch_shapes=[pltpu.VMEM((N_HEADS, HEAD, HEAD), F32)] + (_exchange_scratch(n_x) if n_x else []),
        compiler_params=_cparams(("arbitrary",)))(r, lw, k, v, a, b, ckpt, dy, *scatter)
    return outs[:6], outs[6:]


def _ada_partial(c_all, w_shard):
    def body(c_ref, w_ref, o_ref):
        o_ref[...] = jnp.dot(c_ref[...].astype(BF16), w_ref[...].astype(BF16), preferred_element_type=F32)

    vm = pl.BlockSpec(memory_space=pltpu.VMEM)
    return pl.pallas_call(body, name="ada_partial", in_specs=[vm, vm], out_specs=vm,
                          out_shape=jax.ShapeDtypeStruct((N_DEV, w_shard.shape[1]), F32),
                          compiler_params=pltpu.CompilerParams(vmem_limit_bytes=VMEM_LIMIT))(c_all, w_shard)


def _ada_bias(rows, b_ada):
    def body(r_ref, b_ref, o_ref):
        o_ref[...] = r_ref[...] + b_ref[...]

    vm = pl.BlockSpec(memory_space=pltpu.VMEM)
    return pl.pallas_call(body, name="ada_bias", in_specs=[vm, vm], out_specs=vm,
                          out_shape=jax.ShapeDtypeStruct(rows.shape, F32))(rows, b_ada)


def _ada_wgrad(c_cols, d_all):
    def body(c_ref, d_ref, o_ref):
        acc = c_ref[:, 0:1] * d_ref[0:1, :]
        for j in range(1, N_DEV):
            acc = acc + c_ref[:, j:j + 1] * d_ref[j:j + 1, :]
        o_ref[...] = acc

    vm = pl.BlockSpec(memory_space=pltpu.VMEM)
    return pl.pallas_call(body, name="ada_wgrad", in_specs=[vm, vm], out_specs=vm,
                          out_shape=jax.ShapeDtypeStruct((D, d_all.shape[1]), F32),
                          compiler_params=pltpu.CompilerParams(vmem_limit_bytes=VMEM_LIMIT))(c_cols, d_all)


def _exchange(srcs, broadcast, name):
    n = len(srcs)

    def body(*refs):
        start, wait = _exchange_ops(refs[:n], refs[n:2 * n], *refs[2 * n:], broadcast)
        start()
        wait()

    any_spec = pl.BlockSpec(memory_space=pl.ANY)
    return pl.pallas_call(
        body, name=name, out_shape=_exchange_shapes(srcs, broadcast), in_specs=[any_spec] * n, out_specs=[any_spec] * n,
        scratch_shapes=_exchange_scratch(n),
        compiler_params=pltpu.CompilerParams(has_side_effects=True),
    )(*srcs)


def _gather_via_sibling(srcs, name):
    n = len(srcs)

    def body(*refs):
        src_refs, out_refs = refs[:n], refs[n:2 * n]
        send_sems, recv_sems, local_sems = refs[2 * n:]
        x, y, c = lax.axis_index("x"), lax.axis_index("y"), lax.axis_index("c")
        me, sibling = (x, y, c), (x, y, 1 - c)
        chips = [(1 - x, y), (x, 1 - y), (1 - x, 1 - y)]

        def slot(px, py, pc):
            return 4 * px + 2 * py + pc

        def copy(i, k, block, to, src=None):
            rows = out_refs[i].at[slot(*block)]
            return pltpu.make_async_remote_copy(
                src_ref=rows if src is None else src, dst_ref=rows, send_sem=send_sems.at[i, k],
                recv_sem=recv_sems.at[i, k], device_id=to, device_id_type=_MESH)

        local = [pltpu.make_async_copy(src_refs[i], out_refs[i].at[slot(*me)], local_sems.at[i]) for i in range(n)]
        for cp in local:
            cp.start()
        first = [copy(i, 0, me, sibling, src=src_refs[i]) for i in range(n)]
        first += [copy(i, 1 + j, me, (*chip, c), src=src_refs[i]) for j, chip in enumerate(chips) for i in range(n)]
        for cp in first:
            cp.start()
        passed = []
        for j, chip in enumerate(chips):
            for i in range(n):
                copy(i, 1 + j, (*chip, c), me).wait_recv()
                passed.append(copy(i, 4 + j, (*chip, c), sibling))
                passed[-1].start()
        for i in range(n):
            copy(i, 0, sibling, me).wait_recv()
            for j, chip in enumerate(chips):
                copy(i, 4 + j, (*chip, 1 - c), me).wait_recv()
        for cp in first + passed:
            cp.wait_send()
        for cp in local:
            cp.wait()

    any_spec = pl.BlockSpec(memory_space=pl.ANY)
    return pl.pallas_call(
        body, name=name, out_shape=_exchange_shapes(srcs, True), in_specs=[any_spec] * n, out_specs=[any_spec] * n,
        scratch_shapes=_exchange_scratch(n),
        compiler_params=pltpu.CompilerParams(has_side_effects=True),
    )(*srcs)


def _flags(broadcast, n):
    return [broadcast] * n if isinstance(broadcast, bool) else list(broadcast)


def _exchange_shapes(srcs, broadcast):
    return [jax.ShapeDtypeStruct((N_DEV,) + (s.shape if bc else s.shape[1:]), s.dtype)
            for s, bc in zip(srcs, _flags(broadcast, len(srcs)))]


def _exchange_scratch(n):
    return [pltpu.SemaphoreType.DMA((n, N_DEV)), pltpu.SemaphoreType.DMA((n, N_DEV)), pltpu.SemaphoreType.DMA((n,))]


def _exchange_ops(src_refs, out_refs, send_sems, recv_sems, local_sems, broadcast):
    n = len(src_refs)
    flags = _flags(broadcast, n)
    x, y, c = lax.axis_index("x"), lax.axis_index("y"), lax.axis_index("c")
    me = 4 * x + 2 * y + c

    def block(i, j):
        return src_refs[i] if flags[i] else src_refs[i].at[j]

    def remote(i, d, src_slot, dst_slot):
        px, py, pc = x ^ (d >> 2), y ^ ((d >> 1) & 1), c ^ (d & 1)
        return pltpu.make_async_remote_copy(
            src_ref=block(i, src_slot), dst_ref=out_refs[i].at[dst_slot], send_sem=send_sems.at[i, d],
            recv_sem=recv_sems.at[i, d], device_id=(px, py, pc), device_id_type=_MESH)

    def local(i):
        return pltpu.make_async_copy(block(i, me), out_refs[i].at[me], local_sems.at[i])

    def start():
        for i in range(n):
            local(i).start()
        for d in range(1, N_DEV):
            for i in range(n):
                remote(i, d, me ^ d, me).start()

    def wait():
        for d in range(1, N_DEV):
            for i in range(n):
                remote(i, d, me, me ^ d).wait_recv()
        for d in range(1, N_DEV):
            for i in range(n):
                remote(i, d, me ^ d, me).wait_send()
        for i in range(n):
            local(i).wait()

    return start, wait


def _adamw(w, g, m, v):
    nm = ADAM_B1 * m + (1.0 - ADAM_B1) * g
    nv = ADAM_B2 * v + (1.0 - ADAM_B2) * (g * g)
    m_hat = nm * (1.0 / (1.0 - ADAM_B1 ** ADAM_STEP))
    v_hat = nv * (1.0 / (1.0 - ADAM_B2 ** ADAM_STEP))
    return -ADAM_LR * (m_hat / (jnp.sqrt(v_hat) + ADAM_EPS) + ADAM_WD * w), nm, nv


def _adam_vectors(parts, ws, ms, vs):
    nv = len(ws)
    sizes = [w.shape[1] for w in ws]

    def body(*refs):
        p_ref = refs[0]
        w_refs, m_refs, v_refs = refs[1:1 + nv], refs[1 + nv:1 + 2 * nv], refs[1 + 2 * nv:1 + 3 * nv]
        out_refs = refs[1 + 3 * nv:]
        g_all = p_ref[0]
        for j in range(1, N_DEV):
            g_all = g_all + p_ref[j]
        off = 0
        for i, n in enumerate(sizes):
            g = g_all[:, off:off + n]
            off += -(-n // LANES) * LANES
            delta, new_m, new_v = _adamw(w_refs[i][...], g, m_refs[i][...], v_refs[i][...])
            for o_ref, val in zip(out_refs[4 * i:4 * i + 4], (g, delta, new_m, new_v)):
                o_ref[...] = val

    vm = pl.BlockSpec(memory_space=pltpu.VMEM)
    outs = pl.pallas_call(body, name="adam_replicated", in_specs=[vm] * (1 + 3 * nv), out_specs=[vm] * (4 * nv),
                          out_shape=[jax.ShapeDtypeStruct((1, n), F32) for n in sizes for _ in range(4)])(parts, *ws, *ms, *vs)
    return [outs[4 * i:4 * i + 4] for i in range(nv)]


def _sum_adam(parts, w, m, v, name):
    n_parts, R, C = parts.shape
    fits = [t for t in range(16, R + 1, 16) if R % t == 0 and t * C <= 2504 * LANES]
    if fits:
        tm, tc = max(fits), C
    elif C % (2 * LANES) == 0 and R * C > 2504 * LANES:
        tm, tc = R, 2 * LANES
    else:
        tm, tc = R, C

    def body(p_ref, w_ref, m_ref, v_ref, g_ref, d_ref, nm_ref, nv_ref):
        g = p_ref[0].astype(F32)
        for j in range(1, n_parts):
            g = g + p_ref[j].astype(F32)
        g_ref[...] = g
        d_ref[...], nm_ref[...], nv_ref[...] = _adamw(w_ref[...], g, m_ref[...], v_ref[...])

    blk = pl.BlockSpec((tm, tc), lambda i, j: (i, j))
    shp = jax.ShapeDtypeStruct((R, C), F32)
    return pl.pallas_call(body, name=name, grid=(R // tm, C // tc),
                          in_specs=[pl.BlockSpec((n_parts, tm, tc), lambda i, j: (0, i, j)), blk, blk, blk],
                          out_specs=[blk] * 4, out_shape=[shp] * 4,
                          compiler_params=_cparams(("parallel", "parallel")))(parts, w, m, v)


TRANSPOSED = ("w_in", "w_up")
SHARDED = (("w_ada", 1), ("w_in", 0), ("w2", 1), ("a2", 1), ("g2", 1), ("w_att_out", 1), ("w_rwkv_out", 0),
           ("w_o", 0), ("w_up", 0), ("conv_w", 1), ("w_down", 0))
EARLY, LATE = SHARDED[1:5], SHARDED[5:]
REPLICATED = ("b_ada", "norm1_w", "b_gate", "mu_shift", "w0", "a0", "k_k", "k_a", "r_k", "lnx_w", "lnx_b",
              "norm2_w", "conv_b", "norm_f_w")
WEIGHTS = ("w_ada", "b_ada", "norm1_w", "w_in", "b_gate", "mu_shift", "w0", "w2", "a0", "a2", "g2", "k_k", "k_a", "r_k",
           "lnx_w", "lnx_b", "w_att_out", "w_rwkv_out", "w_o", "norm2_w", "w_up", "conv_w", "conv_b", "w_down", "norm_f_w")


W_IN_RUNS = ((0, C_ATT, ATT_IN), (ATT_IN, C_R, 3 * D), (ATT_IN + 3 * D, C_LORA, LORA_W + LORA_A),
             (ATT_IN + 3 * D + LORA_W + LORA_A, C_LORA + LANES, LORA_G), (ATT_IN + RWKV_IN, C_GA, 2 * D))
W_IN_SHARD = N_IN // N_DEV


def _pad_w_in(w_in_t):
    pieces = [w_in_t[orig:orig + count] for orig, _, count in sorted(W_IN_RUNS, key=lambda run: run[1])]
    pieces.append(jnp.zeros((LORA_PAD - LANES - LORA_G, w_in_t.shape[1]), w_in_t.dtype))
    return jnp.concatenate(pieces, axis=0)


def _w_in_blocks(g):
    blocks = []
    for j in range(N_DEV):
        pieces = []
        for orig, pad, count in W_IN_RUNS:
            lo, hi = max(orig, j * W_IN_SHARD), min(orig + count, (j + 1) * W_IN_SHARD)
            if lo < hi:
                pieces.append(g[pad + lo - orig:pad + hi - orig])
        blocks.append(jnp.concatenate(pieces, axis=0)[None])
    return jnp.concatenate(blocks, axis=0)


def _pad_mu(mu):
    lo = mu[:, 3 * D:]
    mu_l = jnp.concatenate([lo[:, :LORA_W + LORA_A], lo[:, LORA_W + LORA_A:], jnp.zeros((1, LORA_PAD - LANES - LORA_G), mu.dtype)], axis=1)
    return mu[:, :D], mu[:, D:2 * D], mu[:, 2 * D:3 * D], mu_l


def _local_step(x, ada, W, late_shards, target):
    S = x.shape[0]
    W = dict(W)
    G = {}
    sh1, sc1, gt1, sh2, sc2, gt2 = [ada[:, i * D:(i + 1) * D] for i in range(6)]
    h1, rstd1 = _norm_fwd(x, None, None, W["norm1_w"], sc1, sh1, "norm1_fwd")
    w_in_p = _pad_w_in(W["w_in"])
    P = _mm(h1, w_in_p, "nt", F32, "proj_in")

    mu_r, mu_k, mu_v, mu_l = _pad_mu(W["mu_shift"])
    g2p = jnp.pad(W["g2"], ((0, G_PAD - LORA_G), (0, 0)))
    prep_params = [mu_r, mu_k, mu_v, mu_l, W["w0"], W["a0"], W["k_k"], W["k_a"], W["w2"], W["a2"], g2p]
    r_, dec, kmod, v_, aa, bb, gg = _rwkv_prep(P, prep_params)
    y_scan, states, late = _cscan_fwd(r_, dec, kmod, v_, aa, bb, gather=late_shards)
    W.update({n: _full_weight(g, axis) for (n, axis), g in zip(LATE, late)})

    o_g, l_g = zip(*[_att_fwd(P, g) for g in range(len(ATT_PATTERNS))])
    att = _att_combine_fwd(o_g, l_g)
    y_att = _mm(att, W["w_att_out"], "nn", F32, "att_out")
    r_k = W["r_k"].reshape(1, D)
    rw = _rwkv_post(y_scan, r_, kmod, v_, gg, W["lnx_w"], W["lnx_b"], r_k)
    y_rwkv = _mm(rw, W["w_rwkv_out"], "nn", F32, "rwkv_out")

    bga, bgr = W["b_gate"][:, :D], W["b_gate"][:, D:]
    mix = _gate_fwd(P, bga, bgr, y_att, y_rwkv)
    mo = _mm(mix, W["w_o"], "nn", F32, "mix_out")
    x2, h2, rstd2 = _norm_fwd(x, mo, gt1, W["norm2_w"], sc2, sh2, "norm2_fwd")
    u = _mm(h2, W["w_up"], "nt", BF16, "ffn_up")
    conv_w8 = jnp.pad(W["conv_w"], ((0, SUBLANES - 3), (0, 0)))
    act = _conv_fwd(u, conv_w8, W["conv_b"])
    f = _mm(act, W["w_down"], "nn", F32, "ffn_down")
    loss_blk, dx3, df, dgt2, G["norm_f_w"] = _final(x2, f, gt2, W["norm_f_w"], target)
    loss = loss_blk[0, 0]

    dact = _mm(df, W["w_down"], "nt", BF16, "ffn_down_dx")
    G["w_down"] = _mm(act, df, "tn", BF16, "ffn_down_dw")
    duc, dwg, dwv, dbg, dbv = _conv_bwd_a(dact, u, conv_w8, W["conv_b"])
    G["conv_w"] = jnp.concatenate([dwg[0:3], dwv[0:3]], axis=1)
    G["conv_b"] = jnp.concatenate([dbg, dbv], axis=1)
    du = _conv_bwd_b(duc, conv_w8)
    dh2 = _mm(du, W["w_up"], "nn", F32, "ffn_up_dx")
    G["w_up"] = _mm(du, h2, "tn", BF16, "ffn_up_dw")
    dx2, dsh2, dsc2, G["norm2_w"], dmo, dgt1 = _norm_bwd(dh2, x2, rstd2, W["norm2_w"], sc2, dx3, mo, gt1, "norm2_bwd")
    dmix = _mm(dmo, W["w_o"], "nt", F32, "mix_out_dx")
    G["w_o"] = _mm(mix, dmo, "tn", BF16, "mix_out_dw")
    dy_att, dy_rwkv, dpga, dpgr, dbga, dbgr = _gate_bwd(dmix, P, bga, bgr, y_att, y_rwkv)
    G["b_gate"] = jnp.concatenate([dbga, dbgr], axis=1)

    datt = _mm(dy_att, W["w_att_out"], "nt", F32, "att_out_dx")
    G["w_att_out"] = _mm(att, dy_att, "tn", BF16, "att_out_dw")
    dcomb = _att_combine_bwd(datt, o_g, l_g)
    dp_att = []
    for g in range(len(ATT_PATTERNS)):
        dp_att += _att_bwd(P, o_g[g], l_g[g], dcomb[g], dcomb[3 + g], g)

    drw = _mm(dy_rwkv, W["w_rwkv_out"], "nt", F32, "rwkv_out_dx")
    G["w_rwkv_out"] = _mm(rw, dy_rwkv, "tn", BF16, "rwkv_out_dw")
    dy_scan, dr1, dk1, dv1, dgg, G["lnx_w"], G["lnx_b"], drk = _rwkv_post_bwd(drw, y_scan, r_, kmod, v_, gg, W["lnx_w"], W["lnx_b"], r_k)
    G["r_k"] = drk.reshape(W["r_k"].shape)
    late_blocks = [_owner_blocks(G[n], axis) for n, axis in LATE] if late_shards else []
    (dr2, ddec, dk2, dv2, daa, dbb), late_parts = _cscan_bwd(r_, dec, kmod, v_, aa, bb, states, dy_scan, scatter=late_blocks)
    pb = _rwkv_prep_bwd(P, prep_params, [dr2, ddec, dk2, dv2, daa, dbb, dgg], [dr1, None, dk1, dv1, None, None, None])
    dp_rkv, dp_lora, dpar = list(pb[0:3]), pb[3], pb[4:]
    dmu_r, dmu_k, dmu_v, dmu_l, G["w0"], G["a0"], G["k_k"], G["k_a"], G["w2"], G["a2"], dg2p = dpar
    G["g2"] = dg2p[0:LORA_G]
    G["mu_shift"] = jnp.concatenate([dmu_r, dmu_k, dmu_v, dmu_l[:, :LORA_W + LORA_A], dmu_l[:, LANES:LANES + LORA_G]], axis=1)

    dP = jnp.concatenate(dp_rkv + [dpga, dpgr] + dp_att + [dp_lora], axis=1)
    G["w_in"] = _w_in_blocks(_mm(dP, h1, "tn", BF16, "proj_in_dw"))
    if late_shards:
        dh1, (w_in_parts,) = _mm(dP, w_in_p, "nn", F32, "proj_in_dx", scatter=[G["w_in"]])
        done = dict(zip([n for n, _ in LATE] + ["w_in"], list(late_parts) + [w_in_parts]))
    else:
        dh1, done = _mm(dP, w_in_p, "nn", F32, "proj_in_dx"), {}
    grad_x, dsh1, dsc1, G["norm1_w"] = _norm_bwd(dh1, x, rstd1, W["norm1_w"], sc1, dx2, None, None, "norm1_bwd")
    dada = jnp.concatenate([dsh1, dsc1, dgt1, dsh2, dsc2, dgt2], axis=1)
    G["b_ada"] = dada
    return loss, grad_x, G, done


def _full_weight(gathered, axis):
    _, rows, cols = gathered.shape
    if axis == 0:
        return gathered.reshape(N_DEV * rows, cols)
    return gathered.transpose(1, 0, 2).reshape(rows, N_DEV * cols)


def _owner_blocks(g, axis):
    rows, cols = g.shape
    g = g.astype(BF16)
    if axis == 0:
        return g.reshape(N_DEV, rows // N_DEV, cols)
    return g.reshape(rows, N_DEV, cols // N_DEV).transpose(1, 0, 2)


def kernel(x, c, w_ada, b_ada, norm1_w, w_in, b_gate, mu_shift, w0, w2, a0, a2, g2, k_k, k_a, r_k, lnx_w, lnx_b, w_att_out, w_rwkv_out, w_o, norm2_w, w_up, conv_w, conv_b, w_down, norm_f_w, loss_target, m_w_ada, m_b_ada, m_norm1_w, m_w_in, m_b_gate, m_mu_shift, m_w0, m_w2, m_a0, m_a2, m_g2, m_k_k, m_k_a, m_r_k, m_lnx_w, m_lnx_b, m_w_att_out, m_w_rwkv_out, m_w_o, m_norm2_w, m_w_up, m_conv_w, m_conv_b, m_w_down, m_norm_f_w, v_w_ada, v_b_ada, v_norm1_w, v_w_in, v_b_gate, v_mu_shift, v_w0, v_w2, v_a0, v_a2, v_g2, v_k_k, v_k_a, v_r_k, v_lnx_w, v_lnx_b, v_w_att_out, v_w_rwkv_out, v_w_o, v_norm2_w, v_w_up, v_conv_w, v_conv_b, v_w_down, v_norm_f_w):
    env = dict(locals())
    w_shard = {n: env[n] for n in WEIGHTS}
    m_shard = {n: env["m_" + n] for n in WEIGHTS}
    v_shard = {n: env["v_" + n] for n in WEIGHTS}

    def mat(shards, n):
        return jnp.swapaxes(shards[n][0], 0, 1) if n in TRANSPOSED else shards[n][0]

    c_all, *gathered = _gather_via_sibling([c] + [mat(w_shard, n).astype(BF16) for n, _ in EARLY], "gather_weights")
    c_all = c_all.reshape(N_DEV, D)
    W = {n: _full_weight(g, axis) for (n, axis), g in zip(EARLY, gathered)}
    for n in REPLICATED:
        W[n] = w_shard[n].reshape(1, -1) if n != "r_k" else w_shard[n][0]
    ada_cols = _ada_partial(c_all, w_shard["w_ada"][0])
    ada_rows, = _exchange([ada_cols[:, None, :]], False, "ada_rows")
    ada = _ada_bias(ada_rows.reshape(1, -1), w_shard["b_ada"])

    late_shards = [mat(w_shard, n).astype(BF16) for n, _ in LATE]
    loss, grad_x, G, parts = _local_step(x[0], ada, W, late_shards, loss_target[0])
    loss = lax.psum(loss, ("x", "y", "c"))

    row = lambda a: a.reshape(1, -1)
    small = jnp.concatenate([jnp.pad(row(G[n]), ((0, 0), (0, (-G[n].size) % LANES))) for n in REPLICATED], axis=1)
    sparts, dada_all = _exchange([small, G["b_ada"].reshape(N_DEV, 1, -1)], [True, False], "gather_small_grads")
    parts["w_ada"] = _ada_wgrad(c_all.T, dada_all.reshape(N_DEV, -1))[None]

    rest = [(n, axis) for n, axis in SHARDED if n not in parts]
    parts.update(zip([n for n, _ in rest], _exchange([_owner_blocks(G[n], axis) for n, axis in rest], False, "scatter_grads")))
    out = {}
    for n, p in parts.items():
        res = _sum_adam(p, mat(w_shard, n), mat(m_shard, n), mat(v_shard, n), "adam_" + n)
        if n in TRANSPOSED:
            res = [jnp.swapaxes(a, 0, 1) for a in res]
        for kind, a in zip(("grad", "delta", "new_m", "new_v"), res):
            out[kind, n] = a[None]

    res = _adam_vectors(sparts, *[[row(s[n]) for n in REPLICATED] for s in (w_shard, m_shard, v_shard)])
    for n, four in zip(REPLICATED, res):
        for kind, a in zip(("grad", "delta", "new_m", "new_v"), four):
            out[kind, n] = a.reshape(w_shard[n].shape)

    return (loss, grad_x[None], *[out[kind, n] for kind in ("grad", "delta", "new_m", "new_v") for n in WEIGHTS])
```

```python
import functools

import jax
import jax.numpy as jnp
from jax import lax
from jax.experimental import pallas as pl
from jax.experimental.pallas import tpu as pltpu

F32 = jnp.float32
BF16 = jnp.bfloat16

D = 1024
HEAD = 64
ATT_PATTERNS = ((128, 1), (512, 4), (2048, 16))
ATT_HEADS = 8
ATT_W = ATT_HEADS * HEAD
ATT_IN = 3 * 3 * ATT_W
QBLK = 128
N_HEADS = D // HEAD
LORA_W, LORA_A, LORA_G = 64, 64, 160
RWKV_IN = 3 * D + LORA_W + LORA_A + LORA_G
N_IN = ATT_IN + RWKV_IN + 2 * D
D_FF = 2816
RMS_EPS = 1e-6
GN_EPS = 64e-5
N_DEV = 8
LANES = 128
SUBLANES = 8

C_R, C_K, C_V, C_GA, C_GR = 0, 1024, 2048, 3072, 4096
C_ATT = 5120
C_LORA = C_ATT + ATT_IN
LORA_PAD = 512
G_PAD = 256
N_PAD = C_LORA + LORA_PAD

ADAM_LR, ADAM_B1, ADAM_B2, ADAM_EPS, ADAM_WD, ADAM_STEP = 0.001, 0.9, 0.999, 1e-08, 0.01, 10

VMEM_LIMIT = 56 * 1024 * 1024

_MESH = pl.DeviceIdType.MESH


def _cparams(sem):
    return pltpu.CompilerParams(dimension_semantics=sem, vmem_limit_bytes=VMEM_LIMIT)


def _tile(dim, pref):
    if dim <= pref:
        return dim
    best = None
    for t in range(LANES, pref + 1, LANES):
        if dim % t == 0:
            best = t
    assert best is not None, dim
    return best


MM_TILES = {"nn": (1024, 1408, 2816), "nt": (1024, 2048, 1408), "tn": (1408, 1408, 4096)}


def _mm(a, b, mode, out_dtype, name, scatter=()):
    if mode == "nn":
        (M, K), (K2, N) = a.shape, b.shape
    elif mode == "nt":
        (M, K), (N, K2) = a.shape, b.shape
    else:
        (K, M), (K2, N) = a.shape, b.shape
    assert K == K2, (a.shape, b.shape, mode)
    tm, tn, tk = (_tile(dim, pref) for dim, pref in zip((M, N, K), MM_TILES[mode]))
    nk = K // tk
    grid = (M // tm, N // tn, nk)
    n_x = len(scatter)
    dims = {"nn": (((1,), (0,)), ((), ())), "nt": (((1,), (1,)), ((), ())), "tn": (((0,), (0,)), ((), ()))}[mode]

    def body(*refs):
        a_ref, b_ref = refs[:2]
        o_ref, acc_ref = refs[2 + n_x], refs[3 + 2 * n_x]
        finish = _hosted_exchange(refs[2:2 + n_x] + refs[3 + n_x:3 + 2 * n_x] + refs[4 + 2 * n_x:], n_x, False, grid)
        k = pl.program_id(2)
        part = lax.dot_general(a_ref[...].astype(BF16), b_ref[...].astype(BF16), dims,
                               preferred_element_type=F32)
        if nk == 1:
            o_ref[...] = part.astype(o_ref.dtype)
        else:
            @pl.when(k == 0)
            def _():
                acc_ref[...] = part

            @pl.when(jnp.logical_and(k > 0, k < nk - 1))
            def _():
                acc_ref[...] += part

            @pl.when(k == nk - 1)
            def _():
                o_ref[...] = (acc_ref[...] + part).astype(o_ref.dtype)
        finish()

    a_spec = pl.BlockSpec((tk, tm), lambda i, j, k: (k, i)) if mode == "tn" else pl.BlockSpec((tm, tk), lambda i, j, k: (i, k))
    b_spec = pl.BlockSpec((tn, tk), lambda i, j, k: (j, k)) if mode == "nt" else pl.BlockSpec((tk, tn), lambda i, j, k: (k, j))
    any_spec = pl.BlockSpec(memory_space=pl.ANY)
    outs = pl.pallas_call(
        body, name=name, grid=grid,
        in_specs=[a_spec, b_spec] + [any_spec] * n_x,
        out_specs=[pl.BlockSpec((tm, tn), lambda i, j, k: (i, j))] + [any_spec] * n_x,
        out_shape=[jax.ShapeDtypeStruct((M, N), out_dtype)] + _exchange_shapes(scatter, False),
        scratch_shapes=[pltpu.VMEM((tm, tn) if nk > 1 else (SUBLANES, LANES), F32)] + (_exchange_scratch(n_x) if n_x else []),
        compiler_params=_cparams(("arbitrary",) * 3 if n_x else ("parallel", "parallel", "arbitrary")),
    )(a, b, *scatter)
    return (outs[0], outs[1:]) if n_x else outs[0]


def _rows(tm, w, col=0):
    return pl.BlockSpec((tm, w), lambda i: (i, col))


def _full(shape):
    return pl.BlockSpec(shape, lambda i: (0,) * len(shape))


def _shift_down(x, halo, k, first):
    rolled = pltpu.roll(x, k, 0)
    row = lax.broadcasted_iota(jnp.int32, x.shape, 0)
    out = rolled
    n_halo = halo.shape[0]
    for j in range(k):
        h = jnp.where(first, 0.0, halo[n_halo - k + j:n_halo - k + j + 1, :])
        out = jnp.where(row == j, h, out)
    return out


def _shift_up(x, halo, k, last):
    n = x.shape[0]
    rolled = pltpu.roll(x, n - k, 0)
    row = lax.broadcasted_iota(jnp.int32, x.shape, 0)
    out = rolled
    for j in range(k):
        h = jnp.where(last, 0.0, halo[j:j + 1, :])
        out = jnp.where(row == n - k + j, h, out)
    return out


def _acc(ref, val, first):
    @pl.when(first)
    def _():
        ref[...] = val

    @pl.when(jnp.logical_not(first))
    def _():
        ref[...] += val


def _colsum(x):
    return jnp.sum(x, axis=0, keepdims=True)


def _norm_fwd(x, mo, gt, nw, sc, sh, name, tm=512):
    S = x.shape[0]
    has_res = mo is not None

    def body(*refs):
        if has_res:
            x_ref, mo_ref, gt_ref, nw_ref, sc_ref, sh_ref, x2_ref, h_ref, rs_ref = refs
            x2 = x_ref[...] + gt_ref[...] * mo_ref[...]
            x2_ref[...] = x2
        else:
            x_ref, nw_ref, sc_ref, sh_ref, h_ref, rs_ref = refs
            x2 = x_ref[...]
        rstd = lax.rsqrt(jnp.mean(x2 * x2, axis=-1, keepdims=True) + RMS_EPS)
        rs_ref[...] = rstd
        h_ref[...] = ((x2 * rstd * nw_ref[...]) * (1.0 + sc_ref[...]) + sh_ref[...]).astype(BF16)

    vec = _full((1, D))
    ins = [x, mo, gt, nw, sc, sh] if has_res else [x, nw, sc, sh]
    in_specs = [_rows(tm, D), _rows(tm, D), vec, vec, vec, vec] if has_res else [_rows(tm, D), vec, vec, vec]
    outs = [jax.ShapeDtypeStruct((S, D), BF16), jax.ShapeDtypeStruct((S, 1), F32)]
    out_specs = [_rows(tm, D), _rows(tm, 1)]
    if has_res:
        outs = [jax.ShapeDtypeStruct((S, D), F32)] + outs
        out_specs = [_rows(tm, D)] + out_specs
    return pl.pallas_call(body, name=name, grid=(S // tm,), in_specs=in_specs, out_specs=out_specs,
                          out_shape=outs, compiler_params=_cparams(("parallel",)))(*ins)


def _norm_bwd(dh, xin, rstd, nw, sc, dres, mo, gt, name, tm=512):
    S = xin.shape[0]
    has_res = mo is not None

    def body(*refs):
        if has_res:
            dh_ref, x_ref, rs_ref, nw_ref, sc_ref, dres_ref, mo_ref, gt_ref, dx_ref, dsh_ref, dsc_ref, dnw_ref, dmo_ref, dgt_ref = refs
        else:
            dh_ref, x_ref, rs_ref, nw_ref, sc_ref, dres_ref, dx_ref, dsh_ref, dsc_ref, dnw_ref = refs
        first = pl.program_id(0) == 0
        dh = dh_ref[...]
        rstd = rs_ref[...]
        n = x_ref[...] * rstd
        w = nw_ref[...]
        _acc(dsh_ref, _colsum(dh), first)
        _acc(dsc_ref, _colsum(dh * (n * w)), first)
        dnw = dh * (1.0 + sc_ref[...])
        _acc(dnw_ref, _colsum(dnw * n), first)
        dn = dnw * w
        dx = dres_ref[...] + rstd * (dn - n * jnp.mean(dn * n, axis=-1, keepdims=True))
        dx_ref[...] = dx
        if has_res:
            dmo_ref[...] = (dx * gt_ref[...]).astype(BF16)
            _acc(dgt_ref, _colsum(dx * mo_ref[...]), first)

    vec = _full((1, D))
    vshape = jax.ShapeDtypeStruct((1, D), F32)
    ins = [dh, xin, rstd, nw, sc, dres] + ([mo, gt] if has_res else [])
    in_specs = [_rows(tm, D), _rows(tm, D), _rows(tm, 1), vec, vec, _rows(tm, D)] + ([_rows(tm, D), vec] if has_res else [])
    outs = [jax.ShapeDtypeStruct((S, D), F32), vshape, vshape, vshape]
    out_specs = [_rows(tm, D), vec, vec, vec]
    if has_res:
        outs += [jax.ShapeDtypeStruct((S, D), BF16), vshape]
        out_specs += [_rows(tm, D), vec]
    return pl.pallas_call(body, name=name, grid=(S // tm,), in_specs=in_specs, out_specs=out_specs,
                          out_shape=outs, compiler_params=_cparams(("arbitrary",)))(*ins)


def _final(x2, f, gt2, nfw, target, tm=512):
    S = x2.shape[0]

    def body(x2_ref, f_ref, gt_ref, w_ref, t_ref, loss_ref, dx_ref, df_ref, dgt_ref, dw_ref):
        first = pl.program_id(0) == 0
        f = f_ref[...]
        gt = gt_ref[...]
        w = w_ref[...]
        x3 = x2_ref[...] + gt * f
        rstd = lax.rsqrt(jnp.mean(x3 * x3, axis=-1, keepdims=True) + RMS_EPS)
        n = x3 * rstd
        e = n * w - t_ref[...]
        part = 0.5 * jnp.sum(jnp.mean(e * e, axis=-1, keepdims=True), axis=0, keepdims=True)
        _acc(loss_ref, jnp.broadcast_to(part, (SUBLANES, LANES)), first)
        dy = e * (1.0 / D)
        _acc(dw_ref, _colsum(dy * n), first)
        dn = dy * w
        dx = rstd * (dn - n * jnp.mean(dn * n, axis=-1, keepdims=True))
        dx_ref[...] = dx
        df_ref[...] = (dx * gt).astype(BF16)
        _acc(dgt_ref, _colsum(dx * f), first)

    vec = _full((1, D))
    vshape = jax.ShapeDtypeStruct((1, D), F32)
    return pl.pallas_call(
        body, name="final_loss", grid=(S // tm,),
        in_specs=[_rows(tm, D), _rows(tm, D), vec, vec, _rows(tm, D)],
        out_specs=[_full((SUBLANES, LANES)), _rows(tm, D), _rows(tm, D), vec, vec],
        out_shape=[jax.ShapeDtypeStruct((SUBLANES, LANES), F32), jax.ShapeDtypeStruct((S, D), F32),
                   jax.ShapeDtypeStruct((S, D), BF16), vshape, vshape],
        compiler_params=_cparams(("arbitrary",)))(x2, f, gt2, nfw, target)


def _gate_fwd(P, bga, bgr, y_att, y_rwkv, tm=512):
    S = P.shape[0]

    def body(pa_ref, pr_ref, ba_ref, br_ref, ya_ref, yr_ref, mix_ref):
        ga = jax.nn.sigmoid(pa_ref[...] + ba_ref[...])
        gr = jax.nn.sigmoid(pr_ref[...] + br_ref[...])
        mix_ref[...] = (ga * ya_ref[...] + gr * yr_ref[...]).astype(BF16)

    vec = _full((1, D))
    return pl.pallas_call(
        body, name="gate_fwd", grid=(S // tm,),
        in_specs=[_rows(tm, D, C_GA // D), _rows(tm, D, C_GR // D), vec, vec, _rows(tm, D), _rows(tm, D)],
        out_specs=_rows(tm, D), out_shape=jax.ShapeDtypeStruct((S, D), BF16),
        compiler_params=_cparams(("parallel",)))(P, P, bga, bgr, y_att, y_rwkv)


def _gate_bwd(dmix, P, bga, bgr, y_att, y_rwkv, tm=512):
    S = P.shape[0]

    def body(dm_ref, pa_ref, pr_ref, ba_ref, br_ref, ya_ref, yr_ref, dya_ref, dyr_ref, dpa_ref, dpr_ref, dba_ref, dbr_ref):
        first = pl.program_id(0) == 0
        dm = dm_ref[...]
        ga = jax.nn.sigmoid(pa_ref[...] + ba_ref[...])
        gr = jax.nn.sigmoid(pr_ref[...] + br_ref[...])
        dya_ref[...] = (dm * ga).astype(BF16)
        dyr_ref[...] = (dm * gr).astype(BF16)
        dpa = dm * ya_ref[...] * ga * (1.0 - ga)
        dpr = dm * yr_ref[...] * gr * (1.0 - gr)
        dpa_ref[...] = dpa.astype(BF16)
        dpr_ref[...] = dpr.astype(BF16)
        _acc(dba_ref, _colsum(dpa), first)
        _acc(dbr_ref, _colsum(dpr), first)

    vec = _full((1, D))
    row = _rows(tm, D)
    rshape = jax.ShapeDtypeStruct((S, D), BF16)
    vshape = jax.ShapeDtypeStruct((1, D), F32)
    return pl.pallas_call(
        body, name="gate_bwd", grid=(S // tm,),
        in_specs=[row, _rows(tm, D, C_GA // D), _rows(tm, D, C_GR // D), vec, vec, row, row],
        out_specs=[row, row, row, row, vec, vec],
        out_shape=[rshape, rshape, rshape, rshape, vshape, vshape],
        compiler_params=_cparams(("arbitrary",)))(dmix, P, P, bga, bgr, y_att, y_rwkv)


CONV_TN = D_FF // 2
HALO = 16


def _conv_fwd(u, conv_w8, conv_b, tm=512, tn=CONV_TN):
    S = u.shape[0]
    nj = D_FF // tn

    def conv(u_ref, h_ref, w_ref, b_ref, first):
        u = u_ref[...].astype(F32)
        h = h_ref[...].astype(F32)
        w = w_ref[...]
        return b_ref[...] + w[0:1] * _shift_down(u, h, 2, first) + w[1:2] * _shift_down(u, h, 1, first) + w[2:3] * u

    def body(ug_ref, hg_ref, uv_ref, hv_ref, wg_ref, wv_ref, bg_ref, bv_ref, act_ref):
        first = pl.program_id(0) == 0
        g = conv(ug_ref, hg_ref, wg_ref, bg_ref, first)
        v = conv(uv_ref, hv_ref, wv_ref, bv_ref, first)
        act_ref[...] = (g * jax.nn.sigmoid(g) * v).astype(BF16)

    blk = lambda off: pl.BlockSpec((tm, tn), lambda i, j: (i, j + off))
    halo = lambda off: pl.BlockSpec((HALO, tn), lambda i, j: (jnp.maximum(i * (tm // HALO) - 1, 0), j + off))
    wsp = lambda off: pl.BlockSpec((SUBLANES, tn), lambda i, j: (0, j + off))
    bsp = lambda off: pl.BlockSpec((1, tn), lambda i, j: (0, j + off))
    return pl.pallas_call(
        body, name="conv_fwd", grid=(S // tm, nj),
        in_specs=[blk(0), halo(0), blk(nj), halo(nj), wsp(0), wsp(nj), bsp(0), bsp(nj)],
        out_specs=pl.BlockSpec((tm, tn), lambda i, j: (i, j)),
        out_shape=jax.ShapeDtypeStruct((S, D_FF), BF16),
        compiler_params=_cparams(("parallel", "parallel")))(u, u, u, u, conv_w8, conv_w8, conv_b, conv_b)


def _conv_bwd_a(dact, u, conv_w8, conv_b, tm=256, tn=CONV_TN):
    S = u.shape[0]
    nj = D_FF // tn

    def half(u_ref, h_ref, w_ref, b_ref, first):
        u = u_ref[...].astype(F32)
        h = h_ref[...].astype(F32)
        w = w_ref[...]
        u2, u1 = _shift_down(u, h, 2, first), _shift_down(u, h, 1, first)
        return b_ref[...] + w[0:1] * u2 + w[1:2] * u1 + w[2:3] * u, (u2, u1, u)

    def wgrad(d, taps):
        z = jnp.zeros((SUBLANES - 3, d.shape[1]), F32)
        return jnp.concatenate([_colsum(d * taps[0]), _colsum(d * taps[1]), _colsum(d * taps[2]), z], axis=0)

    def body(da_ref, ug_ref, hg_ref, uv_ref, hv_ref, wg_ref, wv_ref, bg_ref, bv_ref,
             d_ref, dwg_ref, dwv_ref, dbg_ref, dbv_ref):
        first = pl.program_id(1) == 0
        g, tg = half(ug_ref, hg_ref, wg_ref, bg_ref, first)
        v, tv = half(uv_ref, hv_ref, wv_ref, bv_ref, first)
        da = da_ref[...].astype(F32)
        sg = jax.nn.sigmoid(g)
        dg = da * v * (sg * (1.0 + g * (1.0 - sg)))
        dv = da * (g * sg)
        d_ref[0] = dg.astype(BF16)
        d_ref[1] = dv.astype(BF16)
        _acc(dwg_ref, wgrad(dg, tg), first)
        _acc(dwv_ref, wgrad(dv, tv), first)
        _acc(dbg_ref, _colsum(dg), first)
        _acc(dbv_ref, _colsum(dv), first)

    blk = lambda off: pl.BlockSpec((tm, tn), lambda j, i: (i, j + off))
    halo = lambda off: pl.BlockSpec((HALO, tn), lambda j, i: (jnp.maximum(i * (tm // HALO) - 1, 0), j + off))
    wsp = lambda off: pl.BlockSpec((SUBLANES, tn), lambda j, i: (0, j + off))
    bsp = lambda off: pl.BlockSpec((1, tn), lambda j, i: (0, j + off))
    f = jax.ShapeDtypeStruct
    outs = pl.pallas_call(
        body, name="conv_bwd_a", grid=(nj, S // tm),
        in_specs=[pl.BlockSpec((tm, tn), lambda j, i: (i, j)), blk(0), halo(0), blk(nj), halo(nj), wsp(0), wsp(nj), bsp(0), bsp(nj)],
        out_specs=[pl.BlockSpec((2, tm, tn), lambda j, i: (0, i, j)),
                   pl.BlockSpec((SUBLANES, tn), lambda j, i: (0, j)), pl.BlockSpec((SUBLANES, tn), lambda j, i: (0, j)),
                   pl.BlockSpec((1, tn), lambda j, i: (0, j)), pl.BlockSpec((1, tn), lambda j, i: (0, j))],
        out_shape=[f((2, S, D_FF), BF16), f((SUBLANES, D_FF), F32), f((SUBLANES, D_FF), F32),
                   f((1, D_FF), F32), f((1, D_FF), F32)],
        compiler_params=_cparams(("parallel", "arbitrary")))(dact, u, u, u, u, conv_w8, conv_w8, conv_b, conv_b)
    return outs


def _conv_bwd_b(duc, conv_w8, tm=512, tn=CONV_TN):
    _, S, W = duc.shape
    nj = W // tn
    n_rows = S // tm

    def body(d_ref, h_ref, w_ref, o_ref):
        last = pl.program_id(0) == n_rows - 1
        d = d_ref[...].astype(F32)
        h = h_ref[...].astype(F32)
        w = w_ref[...]
        o_ref[...] = (w[2:3] * d + w[1:2] * _shift_up(d, h, 1, last) + w[0:1] * _shift_up(d, h, 2, last)).astype(BF16)

    last_tile = S // HALO - 1
    return pl.pallas_call(
        body, name="conv_bwd_b", grid=(n_rows, 2 * nj),
        in_specs=[pl.BlockSpec((None, tm, tn), lambda i, j: (j // nj, i, j % nj)),
                  pl.BlockSpec((None, HALO, tn), lambda i, j: (j // nj, jnp.minimum((i + 1) * (tm // HALO), last_tile), j % nj)),
                  pl.BlockSpec((SUBLANES, tn), lambda i, j: (0, j))],
        out_specs=pl.BlockSpec((tm, tn), lambda i, j: (i, j)),
        out_shape=jax.ShapeDtypeStruct((S, 2 * W), BF16),
        compiler_params=_cparams(("parallel", "parallel")))(duc, duc, conv_w8)


ATT_SCALE = HEAD ** -0.5
NEG = -1e30
ATT_PAIRS = ATT_HEADS // 2


def _att_rows(n, d, S):
    per = S // (QBLK * d)
    r, m = n // per, n % per
    cur = pl.ds(m * (QBLK * d) + r, QBLK, stride=d)
    prv = pl.ds(jnp.maximum(m - 1, 0) * (QBLK * d) + r, QBLK, stride=d)
    return cur, prv, m > 0


def _att_slab(g, j):
    return (C_ATT + g * 3 * ATT_W + j * ATT_W) // LANES


def _heads(x):
    return x[:, 0:HEAD], x[:, HEAD:2 * HEAD]


ATT_NB = 4


def _stack(tiles):
    return jnp.concatenate([t[None] for t in tiles], axis=0)


def _att_operands(i, d, S, *sources):
    rows, has = [], []
    tiles = [[] for _ in sources]
    for bb in range(ATT_NB):
        cur, prv, has_prev = _att_rows(i * ATT_NB + bb, d, S)
        rows.append((cur, prv))
        has.append(has_prev)
        for t, (ref, use_cur) in zip(tiles, sources):
            t += _heads(ref[cur if use_cur else prv, :].astype(BF16))
    return rows, has, [_stack(t) for t in tiles]


def _att_mask(s_c, s_p, has_prev):
    qi = lax.broadcasted_iota(jnp.int32, (QBLK, QBLK), 0)
    kj = lax.broadcasted_iota(jnp.int32, (QBLK, QBLK), 1)
    s_c = jnp.where(kj <= qi, s_c * ATT_SCALE, NEG)
    s_p = jnp.where(jnp.logical_and(kj >= qi, has_prev), s_p * ATT_SCALE, NEG)
    return s_c, s_p


def _att_fwd(P, g):
    S = P.shape[0]
    d = ATT_PATTERNS[g][1]

    def body(q_ref, k_ref, v_ref, o_ref, l_ref):
        def group(i, carry):
            rows, has, (q, kc, kp, vc, vp) = _att_operands(i, d, S, (q_ref, True), (k_ref, True), (k_ref, False),
                                                           (v_ref, True), (v_ref, False))
            s_c_all, s_p_all = _dot16(q, kc, "nt"), _dot16(q, kp, "nt")
            p_c, p_p, den, lse = [], [], [], []
            for e in range(2 * ATT_NB):
                s_c, s_p = _att_mask(s_c_all[e], s_p_all[e], has[e // 2])
                m = jnp.maximum(jnp.max(s_c, axis=1, keepdims=True), jnp.max(s_p, axis=1, keepdims=True))
                pc, pp = jnp.exp(s_c - m), jnp.exp(s_p - m)
                den.append(jnp.sum(pc, axis=1, keepdims=True) + jnp.sum(pp, axis=1, keepdims=True))
                lse.append(jnp.broadcast_to(m + jnp.log(den[e]), (QBLK, HEAD)))
                p_c.append(pc)
                p_p.append(pp)
            num = _dot16(_stack(p_c), vc, "nn") + _dot16(_stack(p_p), vp, "nn")
            for bb, (cur, _) in enumerate(rows):
                o_ref[cur, :] = jnp.concatenate([num[2 * bb] / den[2 * bb], num[2 * bb + 1] / den[2 * bb + 1]], axis=1)
                l_ref[cur, :] = jnp.concatenate(lse[2 * bb:2 * bb + 2], axis=1)
            return carry

        lax.fori_loop(0, S // QBLK // ATT_NB, group, 0)

    slab = lambda j: pl.BlockSpec((S, LANES), lambda i: (0, _att_slab(g, j) + i))
    out = pl.BlockSpec((S, LANES), lambda i: (0, i))
    shp = jax.ShapeDtypeStruct((S, ATT_W), F32)
    return pl.pallas_call(body, name=f"att_fwd_g{g}", grid=(ATT_PAIRS,), in_specs=[slab(0), slab(1), slab(2)],
                          out_specs=[out, out], out_shape=[shp, shp], compiler_params=_cparams(("parallel",)))(P, P, P)


def _att_bwd(P, o, l, do, dl, g):
    S = P.shape[0]
    d = ATT_PATTERNS[g][1]

    def body(q_ref, k_ref, v_ref, o_ref, l_ref, do_ref, dl_ref, dq_ref, dk_ref, dv_ref, dq_acc, dk_acc, dv_acc):
        dk_acc[...] = jnp.zeros_like(dk_acc)
        dv_acc[...] = jnp.zeros_like(dv_acc)

        def group(i, carry):
            rows, has, (q, kc, kp, vc, vp, dob) = _att_operands(
                i, d, S, (q_ref, True), (k_ref, True), (k_ref, False), (v_ref, True), (v_ref, False), (do_ref, True))
            s_c_all, s_p_all = _dot16(q, kc, "nt"), _dot16(q, kp, "nt")
            dp_c_all, dp_p_all = _dot16(dob, vc, "nt"), _dot16(dob, vp, "nt")
            p_c, p_p, ds_c, ds_p = [], [], [], []
            for bb, (cur, _) in enumerate(rows):
                dd2 = do_ref[cur, :] * o_ref[cur, :] - dl_ref[cur, :]
                for h, (dd, lse) in enumerate(zip(_heads(dd2), _heads(l_ref[cur, :]))):
                    e = 2 * bb + h
                    s_c, s_p = _att_mask(s_c_all[e], s_p_all[e], has[bb])
                    pc, pp = jnp.exp(s_c - lse[:, 0:1]), jnp.exp(s_p - lse[:, 0:1])
                    delta = jnp.sum(dd, axis=1, keepdims=True)
                    p_c.append(pc)
                    p_p.append(pp)
                    ds_c.append(pc * (dp_c_all[e] - delta) * ATT_SCALE)
                    ds_p.append(pp * (dp_p_all[e] - delta) * ATT_SCALE)
            p_c, p_p, ds_c, ds_p = map(_stack, (p_c, p_p, ds_c, ds_p))
            dq = _dot16(ds_c, kc, "nn") + _dot16(ds_p, kp, "nn")
            dk_c, dk_p = _dot16(ds_c, q, "tn"), _dot16(ds_p, q, "tn")
            dv_c, dv_p = _dot16(p_c, dob, "tn"), _dot16(p_p, dob, "tn")
            pair = lambda x, bb: jnp.concatenate([x[2 * bb], x[2 * bb + 1]], axis=1)
            for bb, (cur, prv) in enumerate(rows):
                dq_acc[cur, :] = pair(dq, bb)
                dk_acc[cur, :] += pair(dk_c, bb)
                dv_acc[cur, :] += pair(dv_c, bb)
                dk_acc[prv, :] += pair(dk_p, bb)
                dv_acc[prv, :] += pair(dv_p, bb)
            return carry

        lax.fori_loop(0, S // QBLK // ATT_NB, group, 0)
        dq_ref[...] = dq_acc[...].astype(BF16)
        dk_ref[...] = dk_acc[...].astype(BF16)
        dv_ref[...] = dv_acc[...].astype(BF16)

    slab = lambda j: pl.BlockSpec((S, LANES), lambda i: (0, _att_slab(g, j) + i))
    blk128 = pl.BlockSpec((S, LANES), lambda i: (0, i))
    shp = jax.ShapeDtypeStruct((S, ATT_W), BF16)
    return pl.pallas_call(body, name=f"att_bwd_g{g}", grid=(ATT_PAIRS,),
                          in_specs=[slab(0), slab(1), slab(2)] + [blk128] * 4, out_specs=[blk128] * 3, out_shape=[shp] * 3,
                          scratch_shapes=[pltpu.VMEM((S, LANES), F32)] * 3,
                          compiler_params=_cparams(("parallel",)))(P, P, P, o, l, do, dl)


def _att_weights(l_refs):
    l0, l1, l2 = [r[...] for r in l_refs]
    m = jnp.maximum(jnp.maximum(l0, l1), l2)
    e = (jnp.exp(l0 - m), jnp.exp(l1 - m), jnp.exp(l2 - m))
    inv = 1.0 / (e[0] + e[1] + e[2])
    return [x * inv for x in e]


def _att_combine_fwd(os, ls, tm=512):
    S = os[0].shape[0]

    def body(o0, o1, o2, l0, l1, l2, a_ref):
        w = _att_weights((l0, l1, l2))
        a_ref[...] = (w[0] * o0[...] + w[1] * o1[...] + w[2] * o2[...]).astype(BF16)

    row = _rows(tm, ATT_W)
    return pl.pallas_call(body, name="att_combine_fwd", grid=(S // tm,), in_specs=[row] * 6, out_specs=row,
                          out_shape=jax.ShapeDtypeStruct((S, ATT_W), BF16),
                          compiler_params=_cparams(("parallel",)))(*os, *ls)


def _att_combine_bwd(da, os, ls, tm=512):
    S = da.shape[0]

    def body(da_ref, o0, o1, o2, l0, l1, l2, *out_refs):
        da = da_ref[...]
        w = _att_weights((l0, l1, l2))
        dw = (da * o0[...], da * o1[...], da * o2[...])
        mean = w[0] * dw[0] + w[1] * dw[1] + w[2] * dw[2]
        for g in range(3):
            out_refs[g][...] = w[g] * da
            out_refs[3 + g][...] = w[g] * (dw[g] - mean)

    row = _rows(tm, ATT_W)
    shp = jax.ShapeDtypeStruct((S, ATT_W), F32)
    return pl.pallas_call(body, name="att_combine_bwd", grid=(S // tm,), in_specs=[row] * 7, out_specs=[row] * 6,
                          out_shape=[shp] * 6, compiler_params=_cparams(("parallel",)))(da, *os, *ls)


@jax.custom_vjp
def _bdot(a, b):
    return jnp.dot(a.astype(BF16), b.astype(BF16), preferred_element_type=F32)


def _bdot_fwd(a, b):
    return _bdot(a, b), (a, b)


def _bdot_bwd(res, ct):
    a, b = res
    ct16 = ct.astype(BF16)
    da = lax.dot_general(ct16, b.astype(BF16), (((1,), (1,)), ((), ())), preferred_element_type=F32)
    db = lax.dot_general(a.astype(BF16), ct16, (((0,), (0,)), ((), ())), preferred_element_type=F32)
    return da, db


_bdot.defvjp(_bdot_fwd, _bdot_bwd)


def _two_piece_dot(x, m):
    hi = x.astype(BF16)
    lo = (x - hi.astype(F32)).astype(BF16)
    return jnp.dot(hi, m, preferred_element_type=F32) + jnp.dot(lo, m, preferred_element_type=F32)


def _head_sum_impl(x):
    sel = (lax.broadcasted_iota(jnp.int32, (D, LANES), 0) // HEAD == lax.broadcasted_iota(jnp.int32, (D, LANES), 1)).astype(BF16)
    sel_t = (lax.broadcasted_iota(jnp.int32, (LANES, D), 1) // HEAD == lax.broadcasted_iota(jnp.int32, (LANES, D), 0)).astype(BF16)
    return _two_piece_dot(_two_piece_dot(x, sel), sel_t)


@jax.custom_vjp
def _head_sum(x):
    return _head_sum_impl(x)


_head_sum.defvjp(lambda x: (_head_sum_impl(x), None), lambda _, ct: (_head_sum_impl(ct),))


def _softplus(z):
    return jnp.maximum(z, 0.0) + jnp.log(1.0 + jnp.exp(-jnp.abs(z)))


def _rwkv_prep_fn(zr, zrp, zk, zkp, zv, zvp, zl, zlp, mu_r, mu_k, mu_v, mu_l, w0, a0, k_k, k_a, w2, a2, g2p):
    r = zr + (zrp - zr) * mu_r
    k = zk + (zkp - zk) * mu_k
    v = zv + (zvp - zv) * mu_v
    lo = zl + (zlp - zl) * mu_l
    w_low, a_low, g_low = lo[:, 0:LORA_W], lo[:, LORA_W:LORA_W + LORA_A], lo[:, LANES:LANES + G_PAD]
    w_log = -_softplus(-(w0 + _bdot(jnp.tanh(w_low), w2))) - 0.5
    decay = -jnp.exp(w_log)
    a = jax.nn.sigmoid(a0 + _bdot(a_low, a2))
    g = _bdot(jax.nn.sigmoid(g_low), g2p)
    kmod = k * (1.0 + (a - 1.0) * k_a)
    kk = k * k_k
    kk = kk / jnp.maximum(jnp.sqrt(_head_sum(kk * kk)), 1e-12)
    return r, decay, kmod, v, -kk, kk * a, g


def _rwkv_prep_specs(tm, blk=lambda i: i):
    vec = _full((1, D))
    rows = lambda w, col: pl.BlockSpec((tm, w), lambda i: (blk(i), col))
    prev = lambda w, col: pl.BlockSpec((SUBLANES, w), lambda i: (jnp.maximum(blk(i) * (tm // SUBLANES) - 1, 0), col))
    slabs = []
    for col in (C_R // D, C_K // D, C_V // D):
        slabs += [rows(D, col), prev(D, col)]
    slabs += [rows(LORA_PAD, C_LORA // LORA_PAD), prev(LORA_PAD, C_LORA // LORA_PAD)]
    params = [vec, vec, vec, _full((1, LORA_PAD)), vec, vec, vec, vec,
              _full((LORA_W, D)), _full((LORA_A, D)), _full((G_PAD, D))]
    return slabs, params


def _prep_inputs(refs, first):
    vals = []
    for s in range(4):
        z = refs[2 * s][...]
        vals += [z, _shift_down(z, refs[2 * s + 1][...], 1, first)]
    return vals + [r[...] for r in refs[8:19]]


def _rwkv_prep(P, params, tm=256):
    S = P.shape[0]
    slabs, pspecs = _rwkv_prep_specs(tm)

    def body(*refs):
        outs = _rwkv_prep_fn(*_prep_inputs(refs, pl.program_id(0) == 0))
        for o_ref, val in zip(refs[19:], outs):
            o_ref[...] = val

    shp = jax.ShapeDtypeStruct((S, D), F32)
    return pl.pallas_call(body, name="rwkv_prep", grid=(S // tm,), in_specs=slabs + pspecs,
                          out_specs=[_rows(tm, D)] * 7, out_shape=[shp] * 7,
                          compiler_params=_cparams(("parallel",)))(*([P] * 8), *params)


def _rwkv_prep_bwd(P, params, cts_a, cts_b, tm=128):
    S = P.shape[0]
    nblk = S // tm
    blk = lambda i: nblk - 1 - i
    slabs, pspecs = _rwkv_prep_specs(tm, blk)
    has_b = [c is not None for c in cts_b]
    n_ct = 7 + sum(has_b)

    def body(*refs):
        start = pl.program_id(0) == 0
        ins = _prep_inputs(refs, pl.program_id(0) == nblk - 1)
        ct_refs = refs[19:19 + n_ct]
        out_refs = refs[19 + n_ct:19 + n_ct + 15]
        carry_refs = refs[19 + n_ct + 15:]

        @pl.when(start)
        def _():
            for c_ref in carry_refs:
                c_ref[...] = jnp.zeros_like(c_ref)

        cts, pos = [], 7
        for i in range(7):
            c = ct_refs[i][...]
            if has_b[i]:
                c = c + ct_refs[pos][...]
                pos += 1
            cts.append(c)
        _, vjp = jax.vjp(_rwkv_prep_fn, *ins)
        grads = vjp(tuple(cts))
        for s in range(4):
            shifted = grads[2 * s + 1]
            out_refs[s][...] = (grads[2 * s] + _shift_up(shifted, carry_refs[s][...], 1, start)).astype(BF16)
            carry_refs[s][0:1, :] = shifted[0:1, :]
        for i in range(11):
            _acc(out_refs[4 + i], grads[8 + i], start)

    ct_in = list(cts_a) + [c for c in cts_b if c is not None]
    row = lambda w: pl.BlockSpec((tm, w), lambda i: (blk(i), 0))
    f = jax.ShapeDtypeStruct
    zshapes = [f((S, D), BF16)] * 3 + [f((S, LORA_PAD), BF16)]
    pshapes = [f((1, D), F32)] * 3 + [f((1, LORA_PAD), F32)] + [f((1, D), F32)] * 4 + [f((LORA_W, D), F32), f((LORA_A, D), F32), f((G_PAD, D), F32)]
    return pl.pallas_call(
        body, name="rwkv_prep_bwd", grid=(nblk,),
        in_specs=slabs + pspecs + [row(D)] * n_ct,
        out_specs=[row(D), row(D), row(D), row(LORA_PAD)] + pspecs,
        out_shape=zshapes + pshapes,
        scratch_shapes=[pltpu.VMEM((SUBLANES, D), F32)] * 3 + [pltpu.VMEM((SUBLANES, LORA_PAD), F32)],
        compiler_params=_cparams(("arbitrary",)))(*([P] * 8), *params, *ct_in)


def _rwkv_post_fn(y, r, kmod, v, g, lnx_w, lnx_b, r_k):
    mean = _head_sum(y) * (1.0 / HEAD)
    yc = y - mean
    var = _head_sum(yc * yc) * (1.0 / HEAD)
    yn = yc * lax.rsqrt(var + GN_EPS) * lnx_w + lnx_b
    bonus = _head_sum(r * kmod * r_k) * v
    return (yn + bonus) * g


def _rwkv_post(y, r, kmod, v, g, lnx_w, lnx_b, r_k, tm=512):
    S = y.shape[0]

    def body(y_ref, r_ref, k_ref, v_ref, g_ref, w_ref, b_ref, rk_ref, o_ref):
        o_ref[...] = _rwkv_post_fn(y_ref[...], r_ref[...], k_ref[...], v_ref[...], g_ref[...],
                                   w_ref[...], b_ref[...], rk_ref[...]).astype(BF16)

    row, vec = _rows(tm, D), _full((1, D))
    return pl.pallas_call(body, name="rwkv_post", grid=(S // tm,), in_specs=[row] * 5 + [vec] * 3, out_specs=row,
                          out_shape=jax.ShapeDtypeStruct((S, D), BF16),
                          compiler_params=_cparams(("parallel",)))(y, r, kmod, v, g, lnx_w, lnx_b, r_k)


def _rwkv_post_bwd(drw, y, r, kmod, v, g, lnx_w, lnx_b, r_k, tm=256):
    S = y.shape[0]

    def body(d_ref, y_ref, r_ref, k_ref, v_ref, g_ref, w_ref, b_ref, rk_ref, *out_refs):
        first = pl.program_id(0) == 0
        _, vjp = jax.vjp(_rwkv_post_fn, y_ref[...], r_ref[...], k_ref[...], v_ref[...], g_ref[...],
                         w_ref[...], b_ref[...], rk_ref[...])
        grads = vjp(d_ref[...])
        for i in range(5):
            out_refs[i][...] = grads[i]
        for i in range(5, 8):
            _acc(out_refs[i], grads[i], first)

    row, vec = _rows(tm, D), _full((1, D))
    f = jax.ShapeDtypeStruct
    return pl.pallas_call(body, name="rwkv_post_bwd", grid=(S // tm,), in_specs=[row] * 6 + [vec] * 3,
                          out_specs=[row] * 5 + [vec] * 3, out_shape=[f((S, D), F32)] * 5 + [f((1, D), F32)] * 3,
                          compiler_params=_cparams(("arbitrary",)))(drw, y, r, kmod, v, g, lnx_w, lnx_b, r_k)


CHUNK = 64
CHUNK_TB = 256
_DOT_DIMS = {"nn": (((2,), (1,)), ((0,), (0,))), "nt": (((2,), (2,)), ((0,), (0,))), "tn": (((1,), (1,)), ((0,), (0,)))}


def _dot16(x, y, mode):
    return lax.dot_general(x.astype(BF16), y.astype(BF16), _DOT_DIMS[mode], preferred_element_type=F32)


@functools.partial(jax.custom_vjp, nondiff_argnums=(2,))
def _mm16(x, y, mode):
    return _dot16(x, y, mode)


def _mm16_fwd(x, y, mode):
    return _dot16(x, y, mode), (x, y)


def _mm16_bwd(mode, res, ct):
    x, y = res
    if mode == "nn":
        return _dot16(ct, y, "nt"), _dot16(x, ct, "tn")
    if mode == "nt":
        return _dot16(ct, y, "nn"), _dot16(ct, x, "tn")
    return _dot16(y, ct, "nt"), _dot16(x, ct, "nn")


_mm16.defvjp(_mm16_fwd, _mm16_bwd)


def _tri_sum(x, upper):
    T = x.shape[0]
    i = lax.broadcasted_iota(jnp.int32, (T, T), 0)
    j = lax.broadcasted_iota(jnp.int32, (T, T), 1)
    tri = ((j >= i) if upper else (i >= j)).astype(BF16)
    out, rest = None, x
    for _ in range(3):
        piece = rest.astype(BF16)
        rest = rest - piece.astype(F32)
        part = jnp.dot(tri, piece, preferred_element_type=F32)
        out = part if out is None else out + part
    return out


@jax.custom_vjp
def _cumsum_rows(x):
    return _tri_sum(x, False)


_cumsum_rows.defvjp(lambda x: (_tri_sum(x, False), None), lambda _, ct: (_tri_sum(ct, True),))


def _rows_to_cols(row):
    per_head = jnp.concatenate([row[:, h * HEAD:(h + 1) * HEAD] for h in range(N_HEADS)], axis=0)
    eye = (lax.broadcasted_iota(jnp.int32, (HEAD, HEAD), 0) == lax.broadcasted_iota(jnp.int32, (HEAD, HEAD), 1)).astype(F32)
    cols = lax.dot_general(eye, per_head, (((1,), (1,)), ((), ())), precision=lax.Precision.HIGHEST,
                           preferred_element_type=F32)
    return jnp.concatenate([cols[:, h:h + 1][None] for h in range(N_HEADS)], axis=0)


def _per_head(x):
    return jnp.concatenate([x[:, h * HEAD:(h + 1) * HEAD][None] for h in range(N_HEADS)], axis=0)


def _chunk_fn(st0, r, lw, k, v, a, b):
    T = r.shape[0]
    cl = _cumsum_rows(lw)
    cl_end = cl[T - 1:T, :]
    inv = jnp.exp(-cl)
    to_end = jnp.exp(cl_end - cl)
    ah, rh, bh, kh, be, ke, v3 = [_per_head(x) for x in
                                  (a * jnp.exp(cl - lw), r * jnp.exp(cl), b * inv, k * inv, b * to_end, k * to_end, v)]
    i = lax.broadcasted_iota(jnp.int32, (N_HEADS, T, T), 1)
    j = lax.broadcasted_iota(jnp.int32, (N_HEADS, T, T), 2)
    a_ab = jnp.where(i > j, _mm16(ah, bh, "nt"), 0.0)
    a_ak = jnp.where(i > j, _mm16(ah, kh, "nt"), 0.0)
    m_rb = jnp.where(i >= j, _mm16(rh, bh, "nt"), 0.0)
    m_rk = jnp.where(i >= j, _mm16(rh, kh, "nt"), 0.0)
    rhs = _mm16(ah, st0, "nn") + _mm16(a_ak, v3, "nn")
    power, solve, n = a_ab, (i == j).astype(F32) + a_ab, 1
    while 2 * n < T:
        power = _mm16(power, power, "nn")
        solve = solve + _mm16(solve, power, "nn")
        n *= 2
    sa = _mm16(solve, rhs, "nn")
    y3 = _mm16(rh, st0, "nn") + _mm16(m_rb, sa, "nn") + _mm16(m_rk, v3, "nn")
    st_end = _rows_to_cols(jnp.exp(cl_end)) * st0 + _mm16(be, sa, "tn") + _mm16(ke, v3, "tn")
    return jnp.concatenate([y3[h] for h in range(N_HEADS)], axis=1), st_end


def _hosted_exchange(refs, n, broadcast, grid):
    if n == 0:
        return lambda: None
    start, wait = _exchange_ops(refs[:n], refs[n:2 * n], *refs[2 * n:], broadcast)
    first = functools.reduce(jnp.logical_and, [pl.program_id(a) == 0 for a in range(len(grid))])
    last = functools.reduce(jnp.logical_and, [pl.program_id(a) == g - 1 for a, g in enumerate(grid)])
    pl.when(first)(start)
    return lambda: pl.when(last)(wait)


def _cscan_fwd(r, lw, k, v, a, b, gather=()):
    S = r.shape[0]
    per_blk = CHUNK_TB // CHUNK
    n_x = len(gather)
    nblk = S // CHUNK_TB

    def body(*refs):
        r_ref, lw_ref, k_ref, v_ref, a_ref, b_ref = refs[:6]
        y_ref, ck_ref = refs[6 + n_x:8 + n_x]
        st_ref = refs[8 + 2 * n_x]
        finish = _hosted_exchange(refs[6:6 + n_x] + refs[8 + n_x:8 + 2 * n_x] + refs[9 + 2 * n_x:], n_x, True, (nblk,))

        @pl.when(pl.program_id(0) == 0)
        def _():
            st_ref[...] = jnp.zeros_like(st_ref)

        def chunk(c, carry):
            rows = pl.ds(pl.multiple_of(c * CHUNK, CHUNK), CHUNK)
            st0 = st_ref[...]
            ck_ref[c] = st0
            y, st_end = _chunk_fn(st0, r_ref[rows, :], lw_ref[rows, :], k_ref[rows, :],
                                  v_ref[rows, :], a_ref[rows, :], b_ref[rows, :])
            y_ref[rows, :] = y
            st_ref[...] = st_end
            return carry

        lax.fori_loop(0, per_blk, chunk, 0)
        finish()

    blk = _rows(CHUNK_TB, D)
    any_spec = pl.BlockSpec(memory_space=pl.ANY)
    outs = pl.pallas_call(
        body, name="scan_fwd", grid=(nblk,), in_specs=[blk] * 6 + [any_spec] * n_x,
        out_specs=[blk, pl.BlockSpec((per_blk, N_HEADS, HEAD, HEAD), lambda i: (i, 0, 0, 0))] + [any_spec] * n_x,
        out_shape=[jax.ShapeDtypeStruct((S, D), F32), jax.ShapeDtypeStruct((S // CHUNK, N_HEADS, HEAD, HEAD), F32)]
        + _exchange_shapes(gather, True),
        scratch_shapes=[pltpu.VMEM((N_HEADS, HEAD, HEAD), F32)] + (_exchange_scratch(n_x) if n_x else []),
        compiler_params=_cparams(("arbitrary",)))(r, lw, k, v, a, b, *gather)
    return outs[0], outs[1], outs[2:]


def _cscan_bwd(r, lw, k, v, a, b, ckpt, dy, scatter=()):
    S = r.shape[0]
    per_blk = CHUNK_TB // CHUNK
    nblk = S // CHUNK_TB
    n_x = len(scatter)

    def body(*refs):
        r_ref, lw_ref, k_ref, v_ref, a_ref, b_ref, ck_ref, dy_ref = refs[:8]
        out_refs = refs[8 + n_x:14 + n_x]
        ds_ref = refs[14 + 2 * n_x]
        finish = _hosted_exchange(refs[8:8 + n_x] + refs[14 + n_x:14 + 2 * n_x] + refs[15 + 2 * n_x:], n_x, False, (nblk,))

        @pl.when(pl.program_id(0) == 0)
        def _():
            ds_ref[...] = jnp.zeros_like(ds_ref)

        def chunk(cc, carry):
            c = per_blk - 1 - cc
            rows = pl.ds(pl.multiple_of(c * CHUNK, CHUNK), CHUNK)
            ins = (ck_ref[c], r_ref[rows, :], lw_ref[rows, :], k_ref[rows, :], v_ref[rows, :], a_ref[rows, :], b_ref[rows, :])
            _, vjp = jax.vjp(_chunk_fn, *ins)
            grads = vjp((dy_ref[rows, :], ds_ref[...]))
            ds_ref[...] = grads[0]
            for o_ref, g in zip(out_refs, grads[1:]):
                o_ref[rows, :] = g
            return carry

        lax.fori_loop(0, per_blk, chunk, 0)
        finish()

    blk = pl.BlockSpec((CHUNK_TB, D), lambda i: (nblk - 1 - i, 0))
    any_spec = pl.BlockSpec(memory_space=pl.ANY)
    shp = jax.ShapeDtypeStruct((S, D), F32)
    outs = pl.pallas_call(
        body, name="scan_bwd", grid=(nblk,),
        in_specs=[blk] * 6 + [pl.BlockSpec((per_blk, N_HEADS, HEAD, HEAD), lambda i: (nblk - 1 - i, 0, 0, 0)), blk]
        + [any_spec] * n_x,
        out_specs=[blk] * 6 + [any_spec] * n_x, out_shape=[shp] * 6 + _exchange_shapes(scatter, False),
        scratch_shapes=[pltpu.VMEM((N_HEADS, HEAD, HEAD), F32)] + (_exchange_scratch(n_x) if n_x else []),
        compiler_params=_cparams(("arbitrary",)))(r, lw, k, v, a, b, ckpt, dy, *scatter)
    return outs[:6], outs[6:]


def _ada_partial(c_all, w_shard):
    def body(c_ref, w_ref, o_ref):
        o_ref[...] = jnp.dot(c_ref[...].astype(BF16), w_ref[...].astype(BF16), preferred_element_type=F32)

    vm = pl.BlockSpec(memory_space=pltpu.VMEM)
    return pl.pallas_call(body, name="ada_partial", in_specs=[vm, vm], out_specs=vm,
                          out_shape=jax.ShapeDtypeStruct((N_DEV, w_shard.shape[1]), F32),
                          compiler_params=pltpu.CompilerParams(vmem_limit_bytes=VMEM_LIMIT))(c_all, w_shard)


def _ada_bias(rows, b_ada):
    def body(r_ref, b_ref, o_ref):
        o_ref[...] = r_ref[...] + b_ref[...]

    vm = pl.BlockSpec(memory_space=pltpu.VMEM)
    return pl.pallas_call(body, name="ada_bias", in_specs=[vm, vm], out_specs=vm,
                          out_shape=jax.ShapeDtypeStruct(rows.shape, F32))(rows, b_ada)


def _ada_wgrad(c_cols, d_all):
    def body(c_ref, d_ref, o_ref):
        acc = c_ref[:, 0:1] * d_ref[0:1, :]
        for j in range(1, N_DEV):
            acc = acc + c_ref[:, j:j + 1] * d_ref[j:j + 1, :]
        o_ref[...] = acc

    vm = pl.BlockSpec(memory_space=pltpu.VMEM)
    return pl.pallas_call(body, name="ada_wgrad", in_specs=[vm, vm], out_specs=vm,
                          out_shape=jax.ShapeDtypeStruct((D, d_all.shape[1]), F32),
                          compiler_params=pltpu.CompilerParams(vmem_limit_bytes=VMEM_LIMIT))(c_cols, d_all)


def _exchange(srcs, broadcast, name):
    n = len(srcs)

    def body(*refs):
        start, wait = _exchange_ops(refs[:n], refs[n:2 * n], *refs[2 * n:], broadcast)
        start()
        wait()

    any_spec = pl.BlockSpec(memory_space=pl.ANY)
    return pl.pallas_call(
        body, name=name, out_shape=_exchange_shapes(srcs, broadcast), in_specs=[any_spec] * n, out_specs=[any_spec] * n,
        scratch_shapes=_exchange_scratch(n),
        compiler_params=pltpu.CompilerParams(has_side_effects=True),
    )(*srcs)


def _gather_via_sibling(srcs, name):
    n = len(srcs)

    def body(*refs):
        src_refs, out_refs = refs[:n], refs[n:2 * n]
        send_sems, recv_sems, local_sems = refs[2 * n:]
        x, y, c = lax.axis_index("x"), lax.axis_index("y"), lax.axis_index("c")
        me, sibling = (x, y, c), (x, y, 1 - c)
        chips = [(1 - x, y), (x, 1 - y), (1 - x, 1 - y)]

        def slot(px, py, pc):
            return 4 * px + 2 * py + pc

        def copy(i, k, block, to, src=None):
            rows = out_refs[i].at[slot(*block)]
            return pltpu.make_async_remote_copy(
                src_ref=rows if src is None else src, dst_ref=rows, send_sem=send_sems.at[i, k],
                recv_sem=recv_sems.at[i, k], device_id=to, device_id_type=_MESH)

        local = [pltpu.make_async_copy(src_refs[i], out_refs[i].at[slot(*me)], local_sems.at[i]) for i in range(n)]
        for cp in local:
            cp.start()
        first = [copy(i, 0, me, sibling, src=src_refs[i]) for i in range(n)]
        first += [copy(i, 1 + j, me, (*chip, c), src=src_refs[i]) for j, chip in enumerate(chips) for i in range(n)]
        for cp in first:
            cp.start()
        passed = []
        for j, chip in enumerate(chips):
            for i in range(n):
                copy(i, 1 + j, (*chip, c), me).wait_recv()
                passed.append(copy(i, 4 + j, (*chip, c), sibling))
                passed[-1].start()
        for i in range(n):
            copy(i, 0, sibling, me).wait_recv()
            for j, chip in enumerate(chips):
                copy(i, 4 + j, (*chip, 1 - c), me).wait_recv()
        for cp in first + passed:
            cp.wait_send()
        for cp in local:
            cp.wait()

    any_spec = pl.BlockSpec(memory_space=pl.ANY)
    return pl.pallas_call(
        body, name=name, out_shape=_exchange_shapes(srcs, True), in_specs=[any_spec] * n, out_specs=[any_spec] * n,
        scratch_shapes=_exchange_scratch(n),
        compiler_params=pltpu.CompilerParams(has_side_effects=True),
    )(*srcs)


def _flags(broadcast, n):
    return [broadcast] * n if isinstance(broadcast, bool) else list(broadcast)


def _exchange_shapes(srcs, broadcast):
    return [jax.ShapeDtypeStruct((N_DEV,) + (s.shape if bc else s.shape[1:]), s.dtype)
            for s, bc in zip(srcs, _flags(broadcast, len(srcs)))]


def _exchange_scratch(n):
    return [pltpu.SemaphoreType.DMA((n, N_DEV)), pltpu.SemaphoreType.DMA((n, N_DEV)), pltpu.SemaphoreType.DMA((n,))]


def _exchange_ops(src_refs, out_refs, send_sems, recv_sems, local_sems, broadcast):
    n = len(src_refs)
    flags = _flags(broadcast, n)
    x, y, c = lax.axis_index("x"), lax.axis_index("y"), lax.axis_index("c")
    me = 4 * x + 2 * y + c

    def block(i, j):
        return src_refs[i] if flags[i] else src_refs[i].at[j]

    def remote(i, d, src_slot, dst_slot):
        px, py, pc = x ^ (d >> 2), y ^ ((d >> 1) & 1), c ^ (d & 1)
        return pltpu.make_async_remote_copy(
            src_ref=block(i, src_slot), dst_ref=out_refs[i].at[dst_slot], send_sem=send_sems.at[i, d],
            recv_sem=recv_sems.at[i, d], device_id=(px, py, pc), device_id_type=_MESH)

    def local(i):
        return pltpu.make_async_copy(block(i, me), out_refs[i].at[me], local_sems.at[i])

    def start():
        for i in range(n):
            local(i).start()
        for d in range(1, N_DEV):
            for i in range(n):
                remote(i, d, me ^ d, me).start()

    def wait():
        for d in range(1, N_DEV):
            for i in range(n):
                remote(i, d, me, me ^ d).wait_recv()
        for d in range(1, N_DEV):
            for i in range(n):
                remote(i, d, me ^ d, me).wait_send()
        for i in range(n):
            local(i).wait()

    return start, wait


def _adamw(w, g, m, v):
    nm = ADAM_B1 * m + (1.0 - ADAM_B1) * g
    nv = ADAM_B2 * v + (1.0 - ADAM_B2) * (g * g)
    m_hat = nm * (1.0 / (1.0 - ADAM_B1 ** ADAM_STEP))
    v_hat = nv * (1.0 / (1.0 - ADAM_B2 ** ADAM_STEP))
    return -ADAM_LR * (m_hat / (jnp.sqrt(v_hat) + ADAM_EPS) + ADAM_WD * w), nm, nv


def _adam_vectors(parts, ws, ms, vs):
    nv = len(ws)
    sizes = [w.shape[1] for w in ws]

    def body(*refs):
        p_ref = refs[0]
        w_refs, m_refs, v_refs = refs[1:1 + nv], refs[1 + nv:1 + 2 * nv], refs[1 + 2 * nv:1 + 3 * nv]
        out_refs = refs[1 + 3 * nv:]
        g_all = p_ref[0]
        for j in range(1, N_DEV):
            g_all = g_all + p_ref[j]
        off = 0
        for i, n in enumerate(sizes):
            g = g_all[:, off:off + n]
            off += -(-n // LANES) * LANES
            delta, new_m, new_v = _adamw(w_refs[i][...], g, m_refs[i][...], v_refs[i][...])
            for o_ref, val in zip(out_refs[4 * i:4 * i + 4], (g, delta, new_m, new_v)):
                o_ref[...] = val

    vm = pl.BlockSpec(memory_space=pltpu.VMEM)
    outs = pl.pallas_call(body, name="adam_replicated", in_specs=[vm] * (1 + 3 * nv), out_specs=[vm] * (4 * nv),
                          out_shape=[jax.ShapeDtypeStruct((1, n), F32) for n in sizes for _ in range(4)])(parts, *ws, *ms, *vs)
    return [outs[4 * i:4 * i + 4] for i in range(nv)]


def _sum_adam(parts, w, m, v, name):
    n_parts, R, C = parts.shape
    fits = [t for t in range(16, R + 1, 16) if R % t == 0 and t * C <= 2504 * LANES]
    if fits:
        tm, tc = max(fits), C
    elif C % (2 * LANES) == 0 and R * C > 2504 * LANES:
        tm, tc = R, 2 * LANES
    else:
        tm, tc = R, C

    def body(p_ref, w_ref, m_ref, v_ref, g_ref, d_ref, nm_ref, nv_ref):
        g = p_ref[0].astype(F32)
        for j in range(1, n_parts):
            g = g + p_ref[j].astype(F32)
        g_ref[...] = g
        d_ref[...], nm_ref[...], nv_ref[...] = _adamw(w_ref[...], g, m_ref[...], v_ref[...])

    blk = pl.BlockSpec((tm, tc), lambda i, j: (i, j))
    shp = jax.ShapeDtypeStruct((R, C), F32)
    return pl.pallas_call(body, name=name, grid=(R // tm, C // tc),
                          in_specs=[pl.BlockSpec((n_parts, tm, tc), lambda i, j: (0, i, j)), blk, blk, blk],
                          out_specs=[blk] * 4, out_shape=[shp] * 4,
                          compiler_params=_cparams(("parallel", "parallel")))(parts, w, m, v)


TRANSPOSED = ("w_in", "w_up")
SHARDED = (("w_ada", 1), ("w_in", 0), ("w2", 1), ("a2", 1), ("g2", 1), ("w_att_out", 1), ("w_rwkv_out", 0),
           ("w_o", 0), ("w_up", 0), ("conv_w", 1), ("w_down", 0))
EARLY, LATE = SHARDED[1:5], SHARDED[5:]
REPLICATED = ("b_ada", "norm1_w", "b_gate", "mu_shift", "w0", "a0", "k_k", "k_a", "r_k", "lnx_w", "lnx_b",
              "norm2_w", "conv_b", "norm_f_w")
WEIGHTS = ("w_ada", "b_ada", "norm1_w", "w_in", "b_gate", "mu_shift", "w0", "w2", "a0", "a2", "g2", "k_k", "k_a", "r_k",
           "lnx_w", "lnx_b", "w_att_out", "w_rwkv_out", "w_o", "norm2_w", "w_up", "conv_w", "conv_b", "w_down", "norm_f_w")


W_IN_RUNS = ((0, C_ATT, ATT_IN), (ATT_IN, C_R, 3 * D), (ATT_IN + 3 * D, C_LORA, LORA_W + LORA_A),
             (ATT_IN + 3 * D + LORA_W + LORA_A, C_LORA + LANES, LORA_G), (ATT_IN + RWKV_IN, C_GA, 2 * D))
W_IN_SHARD = N_IN // N_DEV


def _pad_w_in(w_in_t):
    pieces = [w_in_t[orig:orig + count] for orig, _, count in sorted(W_IN_RUNS, key=lambda run: run[1])]
    pieces.append(jnp.zeros((LORA_PAD - LANES - LORA_G, w_in_t.shape[1]), w_in_t.dtype))
    return jnp.concatenate(pieces, axis=0)


def _w_in_blocks(g):
    blocks = []
    for j in range(N_DEV):
        pieces = []
        for orig, pad, count in W_IN_RUNS:
            lo, hi = max(orig, j * W_IN_SHARD), min(orig + count, (j + 1) * W_IN_SHARD)
            if lo < hi:
                pieces.append(g[pad + lo - orig:pad + hi - orig])
        blocks.append(jnp.concatenate(pieces, axis=0)[None])
    return jnp.concatenate(blocks, axis=0)


def _pad_mu(mu):
    lo = mu[:, 3 * D:]
    mu_l = jnp.concatenate([lo[:, :LORA_W + LORA_A], lo[:, LORA_W + LORA_A:], jnp.zeros((1, LORA_PAD - LANES - LORA_G), mu.dtype)], axis=1)
    return mu[:, :D], mu[:, D:2 * D], mu[:, 2 * D:3 * D], mu_l


def _local_step(x, ada, W, late_shards, target):
    S = x.shape[0]
    W = dict(W)
    G = {}
    sh1, sc1, gt1, sh2, sc2, gt2 = [ada[:, i * D:(i + 1) * D] for i in range(6)]
    h1, rstd1 = _norm_fwd(x, None, None, W["norm1_w"], sc1, sh1, "norm1_fwd")
    w_in_p = _pad_w_in(W["w_in"])
    P = _mm(h1, w_in_p, "nt", F32, "proj_in")

    mu_r, mu_k, mu_v, mu_l = _pad_mu(W["mu_shift"])
    g2p = jnp.pad(W["g2"], ((0, G_PAD - LORA_G), (0, 0)))
    prep_params = [mu_r, mu_k, mu_v, mu_l, W["w0"], W["a0"], W["k_k"], W["k_a"], W["w2"], W["a2"], g2p]
    r_, dec, kmod, v_, aa, bb, gg = _rwkv_prep(P, prep_params)
    y_scan, states, late = _cscan_fwd(r_, dec, kmod, v_, aa, bb, gather=late_shards)
    W.update({n: _full_weight(g, axis) for (n, axis), g in zip(LATE, late)})

    o_g, l_g = zip(*[_att_fwd(P, g) for g in range(len(ATT_PATTERNS))])
    att = _att_combine_fwd(o_g, l_g)
    y_att = _mm(att, W["w_att_out"], "nn", F32, "att_out")
    r_k = W["r_k"].reshape(1, D)
    rw = _rwkv_post(y_scan, r_, kmod, v_, gg, W["lnx_w"], W["lnx_b"], r_k)
    y_rwkv = _mm(rw, W["w_rwkv_out"], "nn", F32, "rwkv_out")

    bga, bgr = W["b_gate"][:, :D], W["b_gate"][:, D:]
    mix = _gate_fwd(P, bga, bgr, y_att, y_rwkv)
    mo = _mm(mix, W["w_o"], "nn", F32, "mix_out")
    x2, h2, rstd2 = _norm_fwd(x, mo, gt1, W["norm2_w"], sc2, sh2, "norm2_fwd")
    u = _mm(h2, W["w_up"], "nt", BF16, "ffn_up")
    conv_w8 = jnp.pad(W["conv_w"], ((0, SUBLANES - 3), (0, 0)))
    act = _conv_fwd(u, conv_w8, W["conv_b"])
    f = _mm(act, W["w_down"], "nn", F32, "ffn_down")
    loss_blk, dx3, df, dgt2, G["norm_f_w"] = _final(x2, f, gt2, W["norm_f_w"], target)
    loss = loss_blk[0, 0]

    dact = _mm(df, W["w_down"], "nt", BF16, "ffn_down_dx")
    G["w_down"] = _mm(act, df, "tn", BF16, "ffn_down_dw")
    duc, dwg, dwv, dbg, dbv = _conv_bwd_a(dact, u, conv_w8, W["conv_b"])
    G["conv_w"] = jnp.concatenate([dwg[0:3], dwv[0:3]], axis=1)
    G["conv_b"] = jnp.concatenate([dbg, dbv], axis=1)
    du = _conv_bwd_b(duc, conv_w8)
    dh2 = _mm(du, W["w_up"], "nn", F32, "ffn_up_dx")
    G["w_up"] = _mm(du, h2, "tn", BF16, "ffn_up_dw")
    dx2, dsh2, dsc2, G["norm2_w"], dmo, dgt1 = _norm_bwd(dh2, x2, rstd2, W["norm2_w"], sc2, dx3, mo, gt1, "norm2_bwd")
    dmix = _mm(dmo, W["w_o"], "nt", F32, "mix_out_dx")
    G["w_o"] = _mm(mix, dmo, "tn", BF16, "mix_out_dw")
    dy_att, dy_rwkv, dpga, dpgr, dbga, dbgr = _gate_bwd(dmix, P, bga, bgr, y_att, y_rwkv)
    G["b_gate"] = jnp.concatenate([dbga, dbgr], axis=1)

    datt = _mm(dy_att, W["w_att_out"], "nt", F32, "att_out_dx")
    G["w_att_out"] = _mm(att, dy_att, "tn", BF16, "att_out_dw")
    dcomb = _att_combine_bwd(datt, o_g, l_g)
    dp_att = []
    for g in range(len(ATT_PATTERNS)):
        dp_att += _att_bwd(P, o_g[g], l_g[g], dcomb[g], dcomb[3 + g], g)

    drw = _mm(dy_rwkv, W["w_rwkv_out"], "nt", F32, "rwkv_out_dx")
    G["w_rwkv_out"] = _mm(rw, dy_rwkv, "tn", BF16, "rwkv_out_dw")
    dy_scan, dr1, dk1, dv1, dgg, G["lnx_w"], G["lnx_b"], drk = _rwkv_post_bwd(drw, y_scan, r_, kmod, v_, gg, W["lnx_w"], W["lnx_b"], r_k)
    G["r_k"] = drk.reshape(W["r_k"].shape)
    late_blocks = [_owner_blocks(G[n], axis) for n, axis in LATE] if late_shards else []
    (dr2, ddec, dk2, dv2, daa, dbb), late_parts = _cscan_bwd(r_, dec, kmod, v_, aa, bb, states, dy_scan, scatter=late_blocks)
    pb = _rwkv_prep_bwd(P, prep_params, [dr2, ddec, dk2, dv2, daa, dbb, dgg], [dr1, None, dk1, dv1, None, None, None])
    dp_rkv, dp_lora, dpar = list(pb[0:3]), pb[3], pb[4:]
    dmu_r, dmu_k, dmu_v, dmu_l, G["w0"], G["a0"], G["k_k"], G["k_a"], G["w2"], G["a2"], dg2p = dpar
    G["g2"] = dg2p[0:LORA_G]
    G["mu_shift"] = jnp.concatenate([dmu_r, dmu_k, dmu_v, dmu_l[:, :LORA_W + LORA_A], dmu_l[:, LANES:LANES + LORA_G]], axis=1)

    dP = jnp.concatenate(dp_rkv + [dpga, dpgr] + dp_att + [dp_lora], axis=1)
    G["w_in"] = _w_in_blocks(_mm(dP, h1, "tn", BF16, "proj_in_dw"))
    if late_shards:
        dh1, (w_in_parts,) = _mm(dP, w_in_p, "nn", F32, "proj_in_dx", scatter=[G["w_in"]])
        done = dict(zip([n for n, _ in LATE] + ["w_in"], list(late_parts) + [w_in_parts]))
    else:
        dh1, done = _mm(dP, w_in_p, "nn", F32, "proj_in_dx"), {}
    grad_x, dsh1, dsc1, G["norm1_w"] = _norm_bwd(dh1, x, rstd1, W["norm1_w"], sc1, dx2, None, None, "norm1_bwd")
    dada = jnp.concatenate([dsh1, dsc1, dgt1, dsh2, dsc2, dgt2], axis=1)
    G["b_ada"] = dada
    return loss, grad_x, G, done


def _full_weight(gathered, axis):
    _, rows, cols = gathered.shape
    if axis == 0:
        return gathered.reshape(N_DEV * rows, cols)
    return gathered.transpose(1, 0, 2).reshape(rows, N_DEV * cols)


def _owner_blocks(g, axis):
    rows, cols = g.shape
    g = g.astype(BF16)
    if axis == 0:
        return g.reshape(N_DEV, rows // N_DEV, cols)
    return g.reshape(rows, N_DEV, cols // N_DEV).transpose(1, 0, 2)


def kernel(x, c, w_ada, b_ada, norm1_w, w_in, b_gate, mu_shift, w0, w2, a0, a2, g2, k_k, k_a, r_k, lnx_w, lnx_b, w_att_out, w_rwkv_out, w_o, norm2_w, w_up, conv_w, conv_b, w_down, norm_f_w, loss_target, m_w_ada, m_b_ada, m_norm1_w, m_w_in, m_b_gate, m_mu_shift, m_w0, m_w2, m_a0, m_a2, m_g2, m_k_k, m_k_a, m_r_k, m_lnx_w, m_lnx_b, m_w_att_out, m_w_rwkv_out, m_w_o, m_norm2_w, m_w_up, m_conv_w, m_conv_b, m_w_down, m_norm_f_w, v_w_ada, v_b_ada, v_norm1_w, v_w_in, v_b_gate, v_mu_shift, v_w0, v_w2, v_a0, v_a2, v_g2, v_k_k, v_k_a, v_r_k, v_lnx_w, v_lnx_b, v_w_att_out, v_w_rwkv_out, v_w_o, v_norm2_w, v_w_up, v_conv_w, v_conv_b, v_w_down, v_norm_f_w):
    env = dict(locals())
    w_shard = {n: env[n] for n in WEIGHTS}
    m_shard = {n: env["m_" + n] for n in WEIGHTS}
    v_shard = {n: env["v_" + n] for n in WEIGHTS}

    def mat(shards, n):
        return jnp.swapaxes(shards[n][0], 0, 1) if n in TRANSPOSED else shards[n][0]

    c_all, *gathered = _gather_via_sibling([c] + [mat(w_shard, n).astype(BF16) for n, _ in EARLY], "gather_weights")
    c_all = c_all.reshape(N_DEV, D)
    W = {n: _full_weight(g, axis) for (n, axis), g in zip(EARLY, gathered)}
    for n in REPLICATED:
        W[n] = w_shard[n].reshape(1, -1) if n != "r_k" else w_shard[n][0]
    ada_cols = _ada_partial(c_all, w_shard["w_ada"][0])
    ada_rows, = _exchange([ada_cols[:, None, :]], False, "ada_rows")
    ada = _ada_bias(ada_rows.reshape(1, -1), w_shard["b_ada"])

    late_shards = [mat(w_shard, n).astype(BF16) for n, _ in LATE]
    loss, grad_x, G, parts = _local_step(x[0], ada, W, late_shards, loss_target[0])
    loss = lax.psum(loss, ("x", "y", "c"))

    row = lambda a: a.reshape(1, -1)
    small = jnp.concatenate([jnp.pad(row(G[n]), ((0, 0), (0, (-G[n].size) % LANES))) for n in REPLICATED], axis=1)
    sparts, dada_all = _exchange([small, G["b_ada"].reshape(N_DEV, 1, -1)], [True, False], "gather_small_grads")
    parts["w_ada"] = _ada_wgrad(c_all.T, dada_all.reshape(N_DEV, -1))[None]

    rest = [(n, axis) for n, axis in SHARDED if n not in parts]
    parts.update(zip([n for n, _ in rest], _exchange([_owner_blocks(G[n], axis) for n, axis in rest], False, "scatter_grads")))
    out = {}
    for n, p in parts.items():
        res = _sum_adam(p, mat(w_shard, n), mat(m_shard, n), mat(v_shard, n), "adam_" + n)
        if n in TRANSPOSED:
            res = [jnp.swapaxes(a, 0, 1) for a in res]
        for kind, a in zip(("grad", "delta", "new_m", "new_v"), res):
            out[kind, n] = a[None]

    res = _adam_vectors(sparts, *[[row(s[n]) for n in REPLICATED] for s in (w_shard, m_shard, v_shard)])
    for n, four in zip(REPLICATED, res):
        for kind, a in zip(("grad", "delta", "new_m", "new_v"), four):
            out[kind, n] = a.reshape(w_shard[n].shape)

    return (loss, grad_x[None], *[out[kind, n] for kind in ("grad", "delta", "new_m", "new_v") for n in WEIGHTS])
```

```python
import functools

import jax
import jax.numpy as jnp
from jax import lax
from jax.experimental import pallas as pl
from jax.experimental.pallas import tpu as pltpu

F32 = jnp.float32
BF16 = jnp.bfloat16

D = 1024
HEAD = 64
ATT_PATTERNS = ((128, 1), (512, 4), (2048, 16))
ATT_HEADS = 8
ATT_W = ATT_HEADS * HEAD
ATT_IN = 3 * 3 * ATT_W
QBLK = 128
N_HEADS = D // HEAD
LORA_W, LORA_A, LORA_G = 64, 64, 160
RWKV_IN = 3 * D + LORA_W + LORA_A + LORA_G
N_IN = ATT_IN + RWKV_IN + 2 * D
D_FF = 2816
RMS_EPS = 1e-6
GN_EPS = 64e-5
N_DEV = 8
LANES = 128
SUBLANES = 8

C_R, C_K, C_V, C_GA, C_GR = 0, 1024, 2048, 3072, 4096
C_ATT = 5120
C_LORA = C_ATT + ATT_IN
LORA_PAD = 512
G_PAD = 256
N_PAD = C_LORA + LORA_PAD

ADAM_LR, ADAM_B1, ADAM_B2, ADAM_EPS, ADAM_WD, ADAM_STEP = 0.001, 0.9, 0.999, 1e-08, 0.01, 10

VMEM_LIMIT = 56 * 1024 * 1024

_MESH = pl.DeviceIdType.MESH


def _cparams(sem):
    return pltpu.CompilerParams(dimension_semantics=sem, vmem_limit_bytes=VMEM_LIMIT)


def _tile(dim, pref):
    if dim <= pref:
        return dim
    best = None
    for t in range(LANES, pref + 1, LANES):
        if dim % t == 0:
            best = t
    assert best is not None, dim
    return best


MM_TILES = {"nn": (1024, 1408, 2816), "nt": (1024, 2048, 1408), "tn": (1408, 1408, 4096)}


def _mm(a, b, mode, out_dtype, name, scatter=()):
    if mode == "nn":
        (M, K), (K2, N) = a.shape, b.shape
    elif mode == "nt":
        (M, K), (N, K2) = a.shape, b.shape
    else:
        (K, M), (K2, N) = a.shape, b.shape
    assert K == K2, (a.shape, b.shape, mode)
    tm, tn, tk = (_tile(dim, pref) for dim, pref in zip((M, N, K), MM_TILES[mode]))
    nk = K // tk
    grid = (M // tm, N // tn, nk)
    n_x = len(scatter)
    dims = {"nn": (((1,), (0,)), ((), ())), "nt": (((1,), (1,)), ((), ())), "tn": (((0,), (0,)), ((), ()))}[mode]

    def body(*refs):
        a_ref, b_ref = refs[:2]
        o_ref, acc_ref = refs[2 + n_x], refs[3 + 2 * n_x]
        finish = _hosted_exchange(refs[2:2 + n_x] + refs[3 + n_x:3 + 2 * n_x] + refs[4 + 2 * n_x:], n_x, False, grid)
        k = pl.program_id(2)
        part = lax.dot_general(a_ref[...].astype(BF16), b_ref[...].astype(BF16), dims,
                               preferred_element_type=F32)
        if nk == 1:
            o_ref[...] = part.astype(o_ref.dtype)
        else:
            @pl.when(k == 0)
            def _():
                acc_ref[...] = part

            @pl.when(jnp.logical_and(k > 0, k < nk - 1))
            def _():
                acc_ref[...] += part

            @pl.when(k == nk - 1)
            def _():
                o_ref[...] = (acc_ref[...] + part).astype(o_ref.dtype)
        finish()

    a_spec = pl.BlockSpec((tk, tm), lambda i, j, k: (k, i)) if mode == "tn" else pl.BlockSpec((tm, tk), lambda i, j, k: (i, k))
    b_spec = pl.BlockSpec((tn, tk), lambda i, j, k: (j, k)) if mode == "nt" else pl.BlockSpec((tk, tn), lambda i, j, k: (k, j))
    any_spec = pl.BlockSpec(memory_space=pl.ANY)
    outs = pl.pallas_call(
        body, name=name, grid=grid,
        in_specs=[a_spec, b_spec] + [any_spec] * n_x,
        out_specs=[pl.BlockSpec((tm, tn), lambda i, j, k: (i, j))] + [any_spec] * n_x,
        out_shape=[jax.ShapeDtypeStruct((M, N), out_dtype)] + _exchange_shapes(scatter, False),
        scratch_shapes=[pltpu.VMEM((tm, tn) if nk > 1 else (SUBLANES, LANES), F32)] + (_exchange_scratch(n_x) if n_x else []),
        compiler_params=_cparams(("arbitrary",) * 3 if n_x else ("parallel", "parallel", "arbitrary")),
    )(a, b, *scatter)
    return (outs[0], outs[1:]) if n_x else outs[0]


def _rows(tm, w, col=0):
    return pl.BlockSpec((tm, w), lambda i: (i, col))


def _full(shape):
    return pl.BlockSpec(shape, lambda i: (0,) * len(shape))


def _shift_down(x, halo, k, first):
    rolled = pltpu.roll(x, k, 0)
    row = lax.broadcasted_iota(jnp.int32, x.shape, 0)
    out = rolled
    n_halo = halo.shape[0]
    for j in range(k):
        h = jnp.where(first, 0.0, halo[n_halo - k + j:n_halo - k + j + 1, :])
        out = jnp.where(row == j, h, out)
    return out


def _shift_up(x, halo, k, last):
    n = x.shape[0]
    rolled = pltpu.roll(x, n - k, 0)
    row = lax.broadcasted_iota(jnp.int32, x.shape, 0)
    out = rolled
    for j in range(k):
        h = jnp.where(last, 0.0, halo[j:j + 1, :])
        out = jnp.where(row == n - k + j, h, out)
    return out


def _acc(ref, val, first):
    @pl.when(first)
    def _():
        ref[...] = val

    @pl.when(jnp.logical_not(first))
    def _():
        ref[...] += val


def _colsum(x):
    return jnp.sum(x, axis=0, keepdims=True)


def _norm_fwd(x, mo, gt, nw, sc, sh, name, tm=512):
    S = x.shape[0]
    has_res = mo is not None

    def body(*refs):
        if has_res:
            x_ref, mo_ref, gt_ref, nw_ref, sc_ref, sh_ref, x2_ref, h_ref, rs_ref = refs
            x2 = x_ref[...] + gt_ref[...] * mo_ref[...]
            x2_ref[...] = x2
        else:
            x_ref, nw_ref, sc_ref, sh_ref, h_ref, rs_ref = refs
            x2 = x_ref[...]
        rstd = lax.rsqrt(jnp.mean(x2 * x2, axis=-1, keepdims=True) + RMS_EPS)
        rs_ref[...] = rstd
        h_ref[...] = ((x2 * rstd * nw_ref[...]) * (1.0 + sc_ref[...]) + sh_ref[...]).astype(BF16)

    vec = _full((1, D))
    ins = [x, mo, gt, nw, sc, sh] if has_res else [x, nw, sc, sh]
    in_specs = [_rows(tm, D), _rows(tm, D), vec, vec, vec, vec] if has_res else [_rows(tm, D), vec, vec, vec]
    outs = [jax.ShapeDtypeStruct((S, D), BF16), jax.ShapeDtypeStruct((S, 1), F32)]
    out_specs = [_rows(tm, D), _rows(tm, 1)]
    if has_res:
        outs = [jax.ShapeDtypeStruct((S, D), F32)] + outs
        out_specs = [_rows(tm, D)] + out_specs
    return pl.pallas_call(body, name=name, grid=(S // tm,), in_specs=in_specs, out_specs=out_specs,
                          out_shape=outs, compiler_params=_cparams(("parallel",)))(*ins)


def _norm_bwd(dh, xin, rstd, nw, sc, dres, mo, gt, name, tm=512):
    S = xin.shape[0]
    has_res = mo is not None

    def body(*refs):
        if has_res:
            dh_ref, x_ref, rs_ref, nw_ref, sc_ref, dres_ref, mo_ref, gt_ref, dx_ref, dsh_ref, dsc_ref, dnw_ref, dmo_ref, dgt_ref = refs
        else:
            dh_ref, x_ref, rs_ref, nw_ref, sc_ref, dres_ref, dx_ref, dsh_ref, dsc_ref, dnw_ref = refs
        first = pl.program_id(0) == 0
        dh = dh_ref[...]
        rstd = rs_ref[...]
        n = x_ref[...] * rstd
        w = nw_ref[...]
        _acc(dsh_ref, _colsum(dh), first)
        _acc(dsc_ref, _colsum(dh * (n * w)), first)
        dnw = dh * (1.0 + sc_ref[...])
        _acc(dnw_ref, _colsum(dnw * n), first)
        dn = dnw * w
        dx = dres_ref[...] + rstd * (dn - n * jnp.mean(dn * n, axis=-1, keepdims=True))
        dx_ref[...] = dx
        if has_res:
            dmo_ref[...] = (dx * gt_ref[...]).astype(BF16)
            _acc(dgt_ref, _colsum(dx * mo_ref[...]), first)

    vec = _full((1, D))
    vshape = jax.ShapeDtypeStruct((1, D), F32)
    ins = [dh, xin, rstd, nw, sc, dres] + ([mo, gt] if has_res else [])
    in_specs = [_rows(tm, D), _rows(tm, D), _rows(tm, 1), vec, vec, _rows(tm, D)] + ([_rows(tm, D), vec] if has_res else [])
    outs = [jax.ShapeDtypeStruct((S, D), F32), vshape, vshape, vshape]
    out_specs = [_rows(tm, D), vec, vec, vec]
    if has_res:
        outs += [jax.ShapeDtypeStruct((S, D), BF16), vshape]
        out_specs += [_rows(tm, D), vec]
    return pl.pallas_call(body, name=name, grid=(S // tm,), in_specs=in_specs, out_specs=out_specs,
                          out_shape=outs, compiler_params=_cparams(("arbitrary",)))(*ins)


def _final(x2, f, gt2, nfw, target, tm=512):
    S = x2.shape[0]

    def body(x2_ref, f_ref, gt_ref, w_ref, t_ref, loss_ref, dx_ref, df_ref, dgt_ref, dw_ref):
        first = pl.program_id(0) == 0
        f = f_ref[...]
        gt = gt_ref[...]
        w = w_ref[...]
        x3 = x2_ref[...] + gt * f
        rstd = lax.rsqrt(jnp.mean(x3 * x3, axis=-1, keepdims=True) + RMS_EPS)
        n = x3 * rstd
        e = n * w - t_ref[...]
        part = 0.5 * jnp.sum(jnp.mean(e * e, axis=-1, keepdims=True), axis=0, keepdims=True)
        _acc(loss_ref, jnp.broadcast_to(part, (SUBLANES, LANES)), first)
        dy = e * (1.0 / D)
        _acc(dw_ref, _colsum(dy * n), first)
        dn = dy * w
        dx = rstd * (dn - n * jnp.mean(dn * n, axis=-1, keepdims=True))
        dx_ref[...] = dx
        df_ref[...] = (dx * gt).astype(BF16)
        _acc(dgt_ref, _colsum(dx * f), first)

    vec = _full((1, D))
    vshape = jax.ShapeDtypeStruct((1, D), F32)
    return pl.pallas_call(
        body, name="final_loss", grid=(S // tm,),
        in_specs=[_rows(tm, D), _rows(tm, D), vec, vec, _rows(tm, D)],
        out_specs=[_full((SUBLANES, LANES)), _rows(tm, D), _rows(tm, D), vec, vec],
        out_shape=[jax.ShapeDtypeStruct((SUBLANES, LANES), F32), jax.ShapeDtypeStruct((S, D), F32),
                   jax.ShapeDtypeStruct((S, D), BF16), vshape, vshape],
        compiler_params=_cparams(("arbitrary",)))(x2, f, gt2, nfw, target)


def _gate_fwd(P, bga, bgr, y_att, y_rwkv, tm=512):
    S = P.shape[0]

    def body(pa_ref, pr_ref, ba_ref, br_ref, ya_ref, yr_ref, mix_ref):
        ga = jax.nn.sigmoid(pa_ref[...] + ba_ref[...])
        gr = jax.nn.sigmoid(pr_ref[...] + br_ref[...])
        mix_ref[...] = (ga * ya_ref[...] + gr * yr_ref[...]).astype(BF16)

    vec = _full((1, D))
    return pl.pallas_call(
        body, name="gate_fwd", grid=(S // tm,),
        in_specs=[_rows(tm, D, C_GA // D), _rows(tm, D, C_GR // D), vec, vec, _rows(tm, D), _rows(tm, D)],
        out_specs=_rows(tm, D), out_shape=jax.ShapeDtypeStruct((S, D), BF16),
        compiler_params=_cparams(("parallel",)))(P, P, bga, bgr, y_att, y_rwkv)


def _gate_bwd(dmix, P, bga, bgr, y_att, y_rwkv, tm=512):
    S = P.shape[0]

    def body(dm_ref, pa_ref, pr_ref, ba_ref, br_ref, ya_ref, yr_ref, dya_ref, dyr_ref, dpa_ref, dpr_ref, dba_ref, dbr_ref):
        first = pl.program_id(0) == 0
        dm = dm_ref[...]
        ga = jax.nn.sigmoid(pa_ref[...] + ba_ref[...])
        gr = jax.nn.sigmoid(pr_ref[...] + br_ref[...])
        dya_ref[...] = (dm * ga).astype(BF16)
        dyr_ref[...] = (dm * gr).astype(BF16)
        dpa = dm * ya_ref[...] * ga * (1.0 - ga)
        dpr = dm * yr_ref[...] * gr * (1.0 - gr)
        dpa_ref[...] = dpa.astype(BF16)
        dpr_ref[...] = dpr.astype(BF16)
        _acc(dba_ref, _colsum(dpa), first)
        _acc(dbr_ref, _colsum(dpr), first)

    vec = _full((1, D))
    row = _rows(tm, D)
    rshape = jax.ShapeDtypeStruct((S, D), BF16)
    vshape = jax.ShapeDtypeStruct((1, D), F32)
    return pl.pallas_call(
        body, name="gate_bwd", grid=(S // tm,),
        in_specs=[row, _rows(tm, D, C_GA // D), _rows(tm, D, C_GR // D), vec, vec, row, row],
        out_specs=[row, row, row, row, vec, vec],
        out_shape=[rshape, rshape, rshape, rshape, vshape, vshape],
        compiler_params=_cparams(("arbitrary",)))(dmix, P, P, bga, bgr, y_att, y_rwkv)


CONV_TN = D_FF // 2
HALO = 16


def _conv_fwd(u, conv_w8, conv_b, tm=512, tn=CONV_TN):
    S = u.shape[0]
    nj = D_FF // tn

    def conv(u_ref, h_ref, w_ref, b_ref, first):
        u = u_ref[...].astype(F32)
        h = h_ref[...].astype(F32)
        w = w_ref[...]
        return b_ref[...] + w[0:1] * _shift_down(u, h, 2, first) + w[1:2] * _shift_down(u, h, 1, first) + w[2:3] * u

    def body(ug_ref, hg_ref, uv_ref, hv_ref, wg_ref, wv_ref, bg_ref, bv_ref, act_ref):
        first = pl.program_id(0) == 0
        g = conv(ug_ref, hg_ref, wg_ref, bg_ref, first)
        v = conv(uv_ref, hv_ref, wv_ref, bv_ref, first)
        act_ref[...] = (g * jax.nn.sigmoid(g) * v).astype(BF16)

    blk = lambda off: pl.BlockSpec((tm, tn), lambda i, j: (i, j + off))
    halo = lambda off: pl.BlockSpec((HALO, tn), lambda i, j: (jnp.maximum(i * (tm // HALO) - 1, 0), j + off))
    wsp = lambda off: pl.BlockSpec((SUBLANES, tn), lambda i, j: (0, j + off))
    bsp = lambda off: pl.BlockSpec((1, tn), lambda i, j: (0, j + off))
    return pl.pallas_call(
        body, name="conv_fwd", grid=(S // tm, nj),
        in_specs=[blk(0), halo(0), blk(nj), halo(nj), wsp(0), wsp(nj), bsp(0), bsp(nj)],
        out_specs=pl.BlockSpec((tm, tn), lambda i, j: (i, j)),
        out_shape=jax.ShapeDtypeStruct((S, D_FF), BF16),
        compiler_params=_cparams(("parallel", "parallel")))(u, u, u, u, conv_w8, conv_w8, conv_b, conv_b)


def _conv_bwd_a(dact, u, conv_w8, conv_b, tm=256, tn=CONV_TN):
    S = u.shape[0]
    nj = D_FF // tn

    def half(u_ref, h_ref, w_ref, b_ref, first):
        u = u_ref[...].astype(F32)
        h = h_ref[...].astype(F32)
        w = w_ref[...]
        u2, u1 = _shift_down(u, h, 2, first), _shift_down(u, h, 1, first)
        return b_ref[...] + w[0:1] * u2 + w[1:2] * u1 + w[2:3] * u, (u2, u1, u)

    def wgrad(d, taps):
        z = jnp.zeros((SUBLANES - 3, d.shape[1]), F32)
        return jnp.concatenate([_colsum(d * taps[0]), _colsum(d * taps[1]), _colsum(d * taps[2]), z], axis=0)

    def body(da_ref, ug_ref, hg_ref, uv_ref, hv_ref, wg_ref, wv_ref, bg_ref, bv_ref,
             d_ref, dwg_ref, dwv_ref, dbg_ref, dbv_ref):
        first = pl.program_id(1) == 0
        g, tg = half(ug_ref, hg_ref, wg_ref, bg_ref, first)
        v, tv = half(uv_ref, hv_ref, wv_ref, bv_ref, first)
        da = da_ref[...].astype(F32)
        sg = jax.nn.sigmoid(g)
        dg = da * v * (sg * (1.0 + g * (1.0 - sg)))
        dv = da * (g * sg)
        d_ref[0] = dg.astype(BF16)
        d_ref[1] = dv.astype(BF16)
        _acc(dwg_ref, wgrad(dg, tg), first)
        _acc(dwv_ref, wgrad(dv, tv), first)
        _acc(dbg_ref, _colsum(dg), first)
        _acc(dbv_ref, _colsum(dv), first)

    blk = lambda off: pl.BlockSpec((tm, tn), lambda j, i: (i, j + off))
    halo = lambda off: pl.BlockSpec((HALO, tn), lambda j, i: (jnp.maximum(i * (tm // HALO) - 1, 0), j + off))
    wsp = lambda off: pl.BlockSpec((SUBLANES, tn), lambda j, i: (0, j + off))
    bsp = lambda off: pl.BlockSpec((1, tn), lambda j, i: (0, j + off))
    f = jax.ShapeDtypeStruct
    outs = pl.pallas_call(
        body, name="conv_bwd_a", grid=(nj, S // tm),
        in_specs=[pl.BlockSpec((tm, tn), lambda j, i: (i, j)), blk(0), halo(0), blk(nj), halo(nj), wsp(0), wsp(nj), bsp(0), bsp(nj)],
        out_specs=[pl.BlockSpec((2, tm, tn), lambda j, i: (0, i, j)),
                   pl.BlockSpec((SUBLANES, tn), lambda j, i: (0, j)), pl.BlockSpec((SUBLANES, tn), lambda j, i: (0, j)),
                   pl.BlockSpec((1, tn), lambda j, i: (0, j)), pl.BlockSpec((1, tn), lambda j, i: (0, j))],
        out_shape=[f((2, S, D_FF), BF16), f((SUBLANES, D_FF), F32), f((SUBLANES, D_FF), F32),
                   f((1, D_FF), F32), f((1, D_FF), F32)],
        compiler_params=_cparams(("parallel", "arbitrary")))(dact, u, u, u, u, conv_w8, conv_w8, conv_b, conv_b)
    return outs


def _conv_bwd_b(duc, conv_w8, tm=512, tn=CONV_TN):
    _, S, W = duc.shape
    nj = W // tn
    n_rows = S // tm

    def body(d_ref, h_ref, w_ref, o_ref):
        last = pl.program_id(0) == n_rows - 1
        d = d_ref[...].astype(F32)
        h = h_ref[...].astype(F32)
        w = w_ref[...]
        o_ref[...] = (w[2:3] * d + w[1:2] * _shift_up(d, h, 1, last) + w[0:1] * _shift_up(d, h, 2, last)).astype(BF16)

    last_tile = S // HALO - 1
    return pl.pallas_call(
        body, name="conv_bwd_b", grid=(n_rows, 2 * nj),
        in_specs=[pl.BlockSpec((None, tm, tn), lambda i, j: (j // nj, i, j % nj)),
                  pl.BlockSpec((None, HALO, tn), lambda i, j: (j // nj, jnp.minimum((i + 1) * (tm // HALO), last_tile), j % nj)),
                  pl.BlockSpec((SUBLANES, tn), lambda i, j: (0, j))],
        out_specs=pl.BlockSpec((tm, tn), lambda i, j: (i, j)),
        out_shape=jax.ShapeDtypeStruct((S, 2 * W), BF16),
        compiler_params=_cparams(("parallel", "parallel")))(duc, duc, conv_w8)


ATT_SCALE = HEAD ** -0.5
NEG = -1e30
ATT_PAIRS = ATT_HEADS // 2


def _att_rows(n, d, S):
    per = S // (QBLK * d)
    r, m = n // per, n % per
    cur = pl.ds(m * (QBLK * d) + r, QBLK, stride=d)
    prv = pl.ds(jnp.maximum(m - 1, 0) * (QBLK * d) + r, QBLK, stride=d)
    return cur, prv, m > 0


def _att_slab(g, j):
    return (C_ATT + g * 3 * ATT_W + j * ATT_W) // LANES


def _heads(x):
    return x[:, 0:HEAD], x[:, HEAD:2 * HEAD]


ATT_NB = 4


def _stack(tiles):
    return jnp.concatenate([t[None] for t in tiles], axis=0)


def _att_operands(i, d, S, *sources):
    rows, has = [], []
    tiles = [[] for _ in sources]
    for bb in range(ATT_NB):
        cur, prv, has_prev = _att_rows(i * ATT_NB + bb, d, S)
        rows.append((cur, prv))
        has.append(has_prev)
        for t, (ref, use_cur) in zip(tiles, sources):
            t += _heads(ref[cur if use_cur else prv, :].astype(BF16))
    return rows, has, [_stack(t) for t in tiles]


def _att_mask(s_c, s_p, has_prev):
    qi = lax.broadcasted_iota(jnp.int32, (QBLK, QBLK), 0)
    kj = lax.broadcasted_iota(jnp.int32, (QBLK, QBLK), 1)
    s_c = jnp.where(kj <= qi, s_c * ATT_SCALE, NEG)
    s_p = jnp.where(jnp.logical_and(kj >= qi, has_prev), s_p * ATT_SCALE, NEG)
    return s_c, s_p


def _att_fwd(P, g):
    S = P.shape[0]
    d = ATT_PATTERNS[g][1]

    def body(q_ref, k_ref, v_ref, o_ref, l_ref):
        def group(i, carry):
            rows, has, (q, kc, kp, vc, vp) = _att_operands(i, d, S, (q_ref, True), (k_ref, True), (k_ref, False),
                                                           (v_ref, True), (v_ref, False))
            s_c_all, s_p_all = _dot16(q, kc, "nt"), _dot16(q, kp, "nt")
            p_c, p_p, den, lse = [], [], [], []
            for e in range(2 * ATT_NB):
                s_c, s_p = _att_mask(s_c_all[e], s_p_all[e], has[e // 2])
                m = jnp.maximum(jnp.max(s_c, axis=1, keepdims=True), jnp.max(s_p, axis=1, keepdims=True))
                pc, pp = jnp.exp(s_c - m), jnp.exp(s_p - m)
                den.append(jnp.sum(pc, axis=1, keepdims=True) + jnp.sum(pp, axis=1, keepdims=True))
                lse.append(jnp.broadcast_to(m + jnp.log(den[e]), (QBLK, HEAD)))
                p_c.append(pc)
                p_p.append(pp)
            num = _dot16(_stack(p_c), vc, "nn") + _dot16(_stack(p_p), vp, "nn")
            for bb, (cur, _) in enumerate(rows):
                o_ref[cur, :] = jnp.concatenate([num[2 * bb] / den[2 * bb], num[2 * bb + 1] / den[2 * bb + 1]], axis=1)
                l_ref[cur, :] = jnp.concatenate(lse[2 * bb:2 * bb + 2], axis=1)
            return carry

        lax.fori_loop(0, S // QBLK // ATT_NB, group, 0)

    slab = lambda j: pl.BlockSpec((S, LANES), lambda i: (0, _att_slab(g, j) + i))
    out = pl.BlockSpec((S, LANES), lambda i: (0, i))
    shp = jax.ShapeDtypeStruct((S, ATT_W), F32)
    return pl.pallas_call(body, name=f"att_fwd_g{g}", grid=(ATT_PAIRS,), in_specs=[slab(0), slab(1), slab(2)],
                          out_specs=[out, out], out_shape=[shp, shp], compiler_params=_cparams(("parallel",)))(P, P, P)


def _att_bwd(P, o, l, do, dl, g):
    S = P.shape[0]
    d = ATT_PATTERNS[g][1]

    def body(q_ref, k_ref, v_ref, o_ref, l_ref, do_ref, dl_ref, dq_ref, dk_ref, dv_ref, dq_acc, dk_acc, dv_acc):
        dk_acc[...] = jnp.zeros_like(dk_acc)
        dv_acc[...] = jnp.zeros_like(dv_acc)

        def group(i, carry):
            rows, has, (q, kc, kp, vc, vp, dob) = _att_operands(
                i, d, S, (q_ref, True), (k_ref, True), (k_ref, False), (v_ref, True), (v_ref, False), (do_ref, True))
            s_c_all, s_p_all = _dot16(q, kc, "nt"), _dot16(q, kp, "nt")
            dp_c_all, dp_p_all = _dot16(dob, vc, "nt"), _dot16(dob, vp, "nt")
            p_c, p_p, ds_c, ds_p = [], [], [], []
            for bb, (cur, _) in enumerate(rows):
                dd2 = do_ref[cur, :] * o_ref[cur, :] - dl_ref[cur, :]
                for h, (dd, lse) in enumerate(zip(_heads(dd2), _heads(l_ref[cur, :]))):
                    e = 2 * bb + h
                    s_c, s_p = _att_mask(s_c_all[e], s_p_all[e], has[bb])
                    pc, pp = jnp.exp(s_c - lse[:, 0:1]), jnp.exp(s_p - lse[:, 0:1])
                    delta = jnp.sum(dd, axis=1, keepdims=True)
                    p_c.append(pc)
                    p_p.append(pp)
                    ds_c.append(pc * (dp_c_all[e] - delta) * ATT_SCALE)
                    ds_p.append(pp * (dp_p_all[e] - delta) * ATT_SCALE)
            p_c, p_p, ds_c, ds_p = map(_stack, (p_c, p_p, ds_c, ds_p))
            dq = _dot16(ds_c, kc, "nn") + _dot16(ds_p, kp, "nn")
            dk_c, dk_p = _dot16(ds_c, q, "tn"), _dot16(ds_p, q, "tn")
            dv_c, dv_p = _dot16(p_c, dob, "tn"), _dot16(p_p, dob, "tn")
            pair = lambda x, bb: jnp.concatenate([x[2 * bb], x[2 * bb + 1]], axis=1)
            for bb, (cur, prv) in enumerate(rows):
                dq_acc[cur, :] = pair(dq, bb)
                dk_acc[cur, :] += pair(dk_c, bb)
                dv_acc[cur, :] += pair(dv_c, bb)
                dk_acc[prv, :] += pair(dk_p, bb)
                dv_acc[prv, :] += pair(dv_p, bb)
            return carry

        lax.fori_loop(0, S // QBLK // ATT_NB, group, 0)
        dq_ref[...] = dq_acc[...].astype(BF16)
        dk_ref[...] = dk_acc[...].astype(BF16)
        dv_ref[...] = dv_acc[...].astype(BF16)

    slab = lambda j: pl.BlockSpec((S, LANES), lambda i: (0, _att_slab(g, j) + i))
    blk128 = pl.BlockSpec((S, LANES), lambda i: (0, i))
    shp = jax.ShapeDtypeStruct((S, ATT_W), BF16)
    return pl.pallas_call(body, name=f"att_bwd_g{g}", grid=(ATT_PAIRS,),
                          in_specs=[slab(0), slab(1), slab(2)] + [blk128] * 4, out_specs=[blk128] * 3, out_shape=[shp] * 3,
                          scratch_shapes=[pltpu.VMEM((S, LANES), F32)] * 3,
                          compiler_params=_cparams(("parallel",)))(P, P, P, o, l, do, dl)


def _att_weights(l_refs):
    l0, l1, l2 = [r[...] for r in l_refs]
    m = jnp.maximum(jnp.maximum(l0, l1), l2)
    e = (jnp.exp(l0 - m), jnp.exp(l1 - m), jnp.exp(l2 - m))
    inv = 1.0 / (e[0] + e[1] + e[2])
    return [x * inv for x in e]


def _att_combine_fwd(os, ls, tm=512):
    S = os[0].shape[0]

    def body(o0, o1, o2, l0, l1, l2, a_ref):
        w = _att_weights((l0, l1, l2))
        a_ref[...] = (w[0] * o0[...] + w[1] * o1[...] + w[2] * o2[...]).astype(BF16)

    row = _rows(tm, ATT_W)
    return pl.pallas_call(body, name="att_combine_fwd", grid=(S // tm,), in_specs=[row] * 6, out_specs=row,
                          out_shape=jax.ShapeDtypeStruct((S, ATT_W), BF16),
                          compiler_params=_cparams(("parallel",)))(*os, *ls)


def _att_combine_bwd(da, os, ls, tm=512):
    S = da.shape[0]

    def body(da_ref, o0, o1, o2, l0, l1, l2, *out_refs):
        da = da_ref[...]
        w = _att_weights((l0, l1, l2))
        dw = (da * o0[...], da * o1[...], da * o2[...])
        mean = w[0] * dw[0] + w[1] * dw[1] + w[2] * dw[2]
        for g in range(3):
            out_refs[g][...] = w[g] * da
            out_refs[3 + g][...] = w[g] * (dw[g] - mean)

    row = _rows(tm, ATT_W)
    shp = jax.ShapeDtypeStruct((S, ATT_W), F32)
    return pl.pallas_call(body, name="att_combine_bwd", grid=(S // tm,), in_specs=[row] * 7, out_specs=[row] * 6,
                          out_shape=[shp] * 6, compiler_params=_cparams(("parallel",)))(da, *os, *ls)


@jax.custom_vjp
def _bdot(a, b):
    return jnp.dot(a.astype(BF16), b.astype(BF16), preferred_element_type=F32)


def _bdot_fwd(a, b):
    return _bdot(a, b), (a, b)


def _bdot_bwd(res, ct):
    a, b = res
    ct16 = ct.astype(BF16)
    da = lax.dot_general(ct16, b.astype(BF16), (((1,), (1,)), ((), ())), preferred_element_type=F32)
    db = lax.dot_general(a.astype(BF16), ct16, (((0,), (0,)), ((), ())), preferred_element_type=F32)
    return da, db


_bdot.defvjp(_bdot_fwd, _bdot_bwd)


def _two_piece_dot(x, m):
    hi = x.astype(BF16)
    lo = (x - hi.astype(F32)).astype(BF16)
    return jnp.dot(hi, m, preferred_element_type=F32) + jnp.dot(lo, m, preferred_element_type=F32)


def _head_sum_impl(x):
    sel = (lax.broadcasted_iota(jnp.int32, (D, LANES), 0) // HEAD == lax.broadcasted_iota(jnp.int32, (D, LANES), 1)).astype(BF16)
    sel_t = (lax.broadcasted_iota(jnp.int32, (LANES, D), 1) // HEAD == lax.broadcasted_iota(jnp.int32, (LANES, D), 0)).astype(BF16)
    return _two_piece_dot(_two_piece_dot(x, sel), sel_t)


@jax.custom_vjp
def _head_sum(x):
    return _head_sum_impl(x)


_head_sum.defvjp(lambda x: (_head_sum_impl(x), None), lambda _, ct: (_head_sum_impl(ct),))


def _softplus(z):
    return jnp.maximum(z, 0.0) + jnp.log(1.0 + jnp.exp(-jnp.abs(z)))


def _rwkv_prep_fn(zr, zrp, zk, zkp, zv, zvp, zl, zlp, mu_r, mu_k, mu_v, mu_l, w0, a0, k_k, k_a, w2, a2, g2p):
    r = zr + (zrp - zr) * mu_r
    k = zk + (zkp - zk) * mu_k
    v = zv + (zvp - zv) * mu_v
    lo = zl + (zlp - zl) * mu_l
    w_low, a_low, g_low = lo[:, 0:LORA_W], lo[:, LORA_W:LORA_W + LORA_A], lo[:, LANES:LANES + G_PAD]
    w_log = -_softplus(-(w0 + _bdot(jnp.tanh(w_low), w2))) - 0.5
    decay = -jnp.exp(w_log)
    a = jax.nn.sigmoid(a0 + _bdot(a_low, a2))
    g = _bdot(jax.nn.sigmoid(g_low), g2p)
    kmod = k * (1.0 + (a - 1.0) * k_a)
    kk = k * k_k
    kk = kk / jnp.maximum(jnp.sqrt(_head_sum(kk * kk)), 1e-12)
    return r, decay, kmod, v, -kk, kk * a, g


def _rwkv_prep_specs(tm, blk=lambda i: i):
    vec = _full((1, D))
    rows = lambda w, col: pl.BlockSpec((tm, w), lambda i: (blk(i), col))
    prev = lambda w, col: pl.BlockSpec((SUBLANES, w), lambda i: (jnp.maximum(blk(i) * (tm // SUBLANES) - 1, 0), col))
    slabs = []
    for col in (C_R // D, C_K // D, C_V // D):
        slabs += [rows(D, col), prev(D, col)]
    slabs += [rows(LORA_PAD, C_LORA // LORA_PAD), prev(LORA_PAD, C_LORA // LORA_PAD)]
    params = [vec, vec, vec, _full((1, LORA_PAD)), vec, vec, vec, vec,
              _full((LORA_W, D)), _full((LORA_A, D)), _full((G_PAD, D))]
    return slabs, params


def _prep_inputs(refs, first):
    vals = []
    for s in range(4):
        z = refs[2 * s][...]
        vals += [z, _shift_down(z, refs[2 * s + 1][...], 1, first)]
    return vals + [r[...] for r in refs[8:19]]


def _rwkv_prep(P, params, tm=256):
    S = P.shape[0]
    slabs, pspecs = _rwkv_prep_specs(tm)

    def body(*refs):
        outs = _rwkv_prep_fn(*_prep_inputs(refs, pl.program_id(0) == 0))
        for o_ref, val in zip(refs[19:], outs):
            o_ref[...] = val

    shp = jax.ShapeDtypeStruct((S, D), F32)
    return pl.pallas_call(body, name="rwkv_prep", grid=(S // tm,), in_specs=slabs + pspecs,
                          out_specs=[_rows(tm, D)] * 7, out_shape=[shp] * 7,
                          compiler_params=_cparams(("parallel",)))(*([P] * 8), *params)


def _rwkv_prep_bwd(P, params, cts_a, cts_b, tm=128):
    S = P.shape[0]
    nblk = S // tm
    blk = lambda i: nblk - 1 - i
    slabs, pspecs = _rwkv_prep_specs(tm, blk)
    has_b = [c is not None for c in cts_b]
    n_ct = 7 + sum(has_b)

    def body(*refs):
        start = pl.program_id(0) == 0
        ins = _prep_inputs(refs, pl.program_id(0) == nblk - 1)
        ct_refs = refs[19:19 + n_ct]
        out_refs = refs[19 + n_ct:19 + n_ct + 15]
        carry_refs = refs[19 + n_ct + 15:]

        @pl.when(start)
        def _():
            for c_ref in carry_refs:
                c_ref[...] = jnp.zeros_like(c_ref)

        cts, pos = [], 7
        for i in range(7):
            c = ct_refs[i][...]
            if has_b[i]:
                c = c + ct_refs[pos][...]
                pos += 1
            cts.append(c)
        _, vjp = jax.vjp(_rwkv_prep_fn, *ins)
        grads = vjp(tuple(cts))
        for s in range(4):
            shifted = grads[2 * s + 1]
            out_refs[s][...] = (grads[2 * s] + _shift_up(shifted, carry_refs[s][...], 1, start)).astype(BF16)
            carry_refs[s][0:1, :] = shifted[0:1, :]
        for i in range(11):
            _acc(out_refs[4 + i], grads[8 + i], start)

    ct_in = list(cts_a) + [c for c in cts_b if c is not None]
    row = lambda w: pl.BlockSpec((tm, w), lambda i: (blk(i), 0))
    f = jax.ShapeDtypeStruct
    zshapes = [f((S, D), BF16)] * 3 + [f((S, LORA_PAD), BF16)]
    pshapes = [f((1, D), F32)] * 3 + [f((1, LORA_PAD), F32)] + [f((1, D), F32)] * 4 + [f((LORA_W, D), F32), f((LORA_A, D), F32), f((G_PAD, D), F32)]
    return pl.pallas_call(
        body, name="rwkv_prep_bwd", grid=(nblk,),
        in_specs=slabs + pspecs + [row(D)] * n_ct,
        out_specs=[row(D), row(D), row(D), row(LORA_PAD)] + pspecs,
        out_shape=zshapes + pshapes,
        scratch_shapes=[pltpu.VMEM((SUBLANES, D), F32)] * 3 + [pltpu.VMEM((SUBLANES, LORA_PAD), F32)],
        compiler_params=_cparams(("arbitrary",)))(*([P] * 8), *params, *ct_in)


def _rwkv_post_fn(y, r, kmod, v, g, lnx_w, lnx_b, r_k):
    mean = _head_sum(y) * (1.0 / HEAD)
    yc = y - mean
    var = _head_sum(yc * yc) * (1.0 / HEAD)
    yn = yc * lax.rsqrt(var + GN_EPS) * lnx_w + lnx_b
    bonus = _head_sum(r * kmod * r_k) * v
    return (yn + bonus) * g


def _rwkv_post(y, r, kmod, v, g, lnx_w, lnx_b, r_k, tm=512):
    S = y.shape[0]

    def body(y_ref, r_ref, k_ref, v_ref, g_ref, w_ref, b_ref, rk_ref, o_ref):
        o_ref[...] = _rwkv_post_fn(y_ref[...], r_ref[...], k_ref[...], v_ref[...], g_ref[...],
                                   w_ref[...], b_ref[...], rk_ref[...]).astype(BF16)

    row, vec = _rows(tm, D), _full((1, D))
    return pl.pallas_call(body, name="rwkv_post", grid=(S // tm,), in_specs=[row] * 5 + [vec] * 3, out_specs=row,
                          out_shape=jax.ShapeDtypeStruct((S, D), BF16),
                          compiler_params=_cparams(("parallel",)))(y, r, kmod, v, g, lnx_w, lnx_b, r_k)


def _rwkv_post_bwd(drw, y, r, kmod, v, g, lnx_w, lnx_b, r_k, tm=256):
    S = y.shape[0]

    def body(d_ref, y_ref, r_ref, k_ref, v_ref, g_ref, w_ref, b_ref, rk_ref, *out_refs):
        first = pl.program_id(0) == 0
        _, vjp = jax.vjp(_rwkv_post_fn, y_ref[...], r_ref[...], k_ref[...], v_ref[...], g_ref[...],
                         w_ref[...], b_ref[...], rk_ref[...])
        grads = vjp(d_ref[...])
        for i in range(5):
            out_refs[i][...] = grads[i]
        for i in range(5, 8):
            _acc(out_refs[i], grads[i], first)

    row, vec = _rows(tm, D), _full((1, D))
    f = jax.ShapeDtypeStruct
    return pl.pallas_call(body, name="rwkv_post_bwd", grid=(S // tm,), in_specs=[row] * 6 + [vec] * 3,
                          out_specs=[row] * 5 + [vec] * 3, out_shape=[f((S, D), F32)] * 5 + [f((1, D), F32)] * 3,
                          compiler_params=_cparams(("arbitrary",)))(drw, y, r, kmod, v, g, lnx_w, lnx_b, r_k)


CHUNK = 64
CHUNK_TB = 256
_DOT_DIMS = {"nn": (((2,), (1,)), ((0,), (0,))), "nt": (((2,), (2,)), ((0,), (0,))), "tn": (((1,), (1,)), ((0,), (0,)))}


def _dot16(x, y, mode):
    return lax.dot_general(x.astype(BF16), y.astype(BF16), _DOT_DIMS[mode], preferred_element_type=F32)


@functools.partial(jax.custom_vjp, nondiff_argnums=(2,))
def _mm16(x, y, mode):
    return _dot16(x, y, mode)


def _mm16_fwd(x, y, mode):
    return _dot16(x, y, mode), (x, y)


def _mm16_bwd(mode, res, ct):
    x, y = res
    if mode == "nn":
        return _dot16(ct, y, "nt"), _dot16(x, ct, "tn")
    if mode == "nt":
        return _dot16(ct, y, "nn"), _dot16(ct, x, "tn")
    return _dot16(y, ct, "nt"), _dot16(x, ct, "nn")


_mm16.defvjp(_mm16_fwd, _mm16_bwd)


def _tri_sum(x, upper):
    T = x.shape[0]
    i = lax.broadcasted_iota(jnp.int32, (T, T), 0)
    j = lax.broadcasted_iota(jnp.int32, (T, T), 1)
    tri = ((j >= i) if upper else (i >= j)).astype(BF16)
    out, rest = None, x
    for _ in range(3):
        piece = rest.astype(BF16)
        rest = rest - piece.astype(F32)
        part = jnp.dot(tri, piece, preferred_element_type=F32)
        out = part if out is None else out + part
    return out


@jax.custom_vjp
def _cumsum_rows(x):
    return _tri_sum(x, False)


_cumsum_rows.defvjp(lambda x: (_tri_sum(x, False), None), lambda _, ct: (_tri_sum(ct, True),))


def _rows_to_cols(row):
    per_head = jnp.concatenate([row[:, h * HEAD:(h + 1) * HEAD] for h in range(N_HEADS)], axis=0)
    eye = (lax.broadcasted_iota(jnp.int32, (HEAD, HEAD), 0) == lax.broadcasted_iota(jnp.int32, (HEAD, HEAD), 1)).astype(F32)
    cols = lax.dot_general(eye, per_head, (((1,), (1,)), ((), ())), precision=lax.Precision.HIGHEST,
                           preferred_element_type=F32)
    return jnp.concatenate([cols[:, h:h + 1][None] for h in range(N_HEADS)], axis=0)


def _per_head(x):
    return jnp.concatenate([x[:, h * HEAD:(h + 1) * HEAD][None] for h in range(N_HEADS)], axis=0)


def _chunk_fn(st0, r, lw, k, v, a, b):
    T = r.shape[0]
    cl = _cumsum_rows(lw)
    cl_end = cl[T - 1:T, :]
    inv = jnp.exp(-cl)
    to_end = jnp.exp(cl_end - cl)
    ah, rh, bh, kh, be, ke, v3 = [_per_head(x) for x in
                                  (a * jnp.exp(cl - lw), r * jnp.exp(cl), b * inv, k * inv, b * to_end, k * to_end, v)]
    i = lax.broadcasted_iota(jnp.int32, (N_HEADS, T, T), 1)
    j = lax.broadcasted_iota(jnp.int32, (N_HEADS, T, T), 2)
    a_ab = jnp.where(i > j, _mm16(ah, bh, "nt"), 0.0)
    a_ak = jnp.where(i > j, _mm16(ah, kh, "nt"), 0.0)
    m_rb = jnp.where(i >= j, _mm16(rh, bh, "nt"), 0.0)
    m_rk = jnp.where(i >= j, _mm16(rh, kh, "nt"), 0.0)
    rhs = _mm16(ah, st0, "nn") + _mm16(a_ak, v3, "nn")
    power, solve, n = a_ab, (i == j).astype(F32) + a_ab, 1
    while 2 * n < T:
        power = _mm16(power, power, "nn")
        solve = solve + _mm16(solve, power, "nn")
        n *= 2
    sa = _mm16(solve, rhs, "nn")
    y3 = _mm16(rh, st0, "nn") + _mm16(m_rb, sa, "nn") + _mm16(m_rk, v3, "nn")
    st_end = _rows_to_cols(jnp.exp(cl_end)) * st0 + _mm16(be, sa, "tn") + _mm16(ke, v3, "tn")
    return jnp.concatenate([y3[h] for h in range(N_HEADS)], axis=1), st_end


def _hosted_exchange(refs, n, broadcast, grid):
    if n == 0:
        return lambda: None
    start, wait = _exchange_ops(refs[:n], refs[n:2 * n], *refs[2 * n:], broadcast)
    first = functools.reduce(jnp.logical_and, [pl.program_id(a) == 0 for a in range(len(grid))])
    last = functools.reduce(jnp.logical_and, [pl.program_id(a) == g - 1 for a, g in enumerate(grid)])
    pl.when(first)(start)
    return lambda: pl.when(last)(wait)


def _cscan_fwd(r, lw, k, v, a, b, gather=()):
    S = r.shape[0]
    per_blk = CHUNK_TB // CHUNK
    n_x = len(gather)
    nblk = S // CHUNK_TB

    def body(*refs):
        r_ref, lw_ref, k_ref, v_ref, a_ref, b_ref = refs[:6]
        y_ref, ck_ref = refs[6 + n_x:8 + n_x]
        st_ref = refs[8 + 2 * n_x]
        if n_x:
            g_start, g_forward, g_finish = _sibling_gather_ops(refs[6:6 + n_x], refs[8 + n_x:8 + 2 * n_x], *refs[9 + 2 * n_x:])
            pl.when(pl.program_id(0) == 0)(g_start)
            finish = lambda: pl.when(pl.program_id(0) == nblk - 1)(g_finish)
        else:
            g_forward, finish = None, lambda: None

        @pl.when(pl.program_id(0) == 0)
        def _():
            st_ref[...] = jnp.zeros_like(st_ref)

        def chunk(c, carry):
            rows = pl.ds(pl.multiple_of(c * CHUNK, CHUNK), CHUNK)
            st0 = st_ref[...]
            ck_ref[c] = st0
            y, st_end = _chunk_fn(st0, r_ref[rows, :], lw_ref[rows, :], k_ref[rows, :],
                                  v_ref[rows, :], a_ref[rows, :], b_ref[rows, :])
            y_ref[rows, :] = y
            st_ref[...] = st_end
            return carry

        lax.fori_loop(0, per_blk, chunk, 0)
        if g_forward is not None:
            pl.when(pl.program_id(0) == nblk // 2)(g_forward)
        finish()

    blk = _rows(CHUNK_TB, D)
    any_spec = pl.BlockSpec(memory_space=pl.ANY)
    outs = pl.pallas_call(
        body, name="scan_fwd", grid=(nblk,), in_specs=[blk] * 6 + [any_spec] * n_x,
        out_specs=[blk, pl.BlockSpec((per_blk, N_HEADS, HEAD, HEAD), lambda i: (i, 0, 0, 0))] + [any_spec] * n_x,
        out_shape=[jax.ShapeDtypeStruct((S, D), F32), jax.ShapeDtypeStruct((S // CHUNK, N_HEADS, HEAD, HEAD), F32)]
        + _exchange_shapes(gather, True),
        scratch_shapes=[pltpu.VMEM((N_HEADS, HEAD, HEAD), F32)] + (_exchange_scratch(n_x) if n_x else []),
        compiler_params=_cparams(("arbitrary",)))(r, lw, k, v, a, b, *gather)
    return outs[0], outs[1], outs[2:]


def _cscan_bwd(r, lw, k, v, a, b, ckpt, dy, scatter=()):
    S = r.shape[0]
    per_blk = CHUNK_TB // CHUNK
    nblk = S // CHUNK_TB
    n_x = len(scatter)

    def body(*refs):
        r_ref, lw_ref, k_ref, v_ref, a_ref, b_ref, ck_ref, dy_ref = refs[:8]
        out_refs = refs[8 + n_x:14 + n_x]
        ds_ref = refs[14 + 2 * n_x]
        finish = _hosted_exchange(refs[8:8 + n_x] + refs[14 + n_x:14 + 2 * n_x] + refs[15 + 2 * n_x:], n_x, False, (nblk,))

        @pl.when(pl.program_id(0) == 0)
        def _():
            ds_ref[...] = jnp.zeros_like(ds_ref)

        def chunk(cc, carry):
            c = per_blk - 1 - cc
            rows = pl.ds(pl.multiple_of(c * CHUNK, CHUNK), CHUNK)
            ins = (ck_ref[c], r_ref[rows, :], lw_ref[rows, :], k_ref[rows, :], v_ref[rows, :], a_ref[rows, :], b_ref[rows, :])
            _, vjp = jax.vjp(_chunk_fn, *ins)
            grads = vjp((dy_ref[rows, :], ds_ref[...]))
            ds_ref[...] = grads[0]
            for o_ref, g in zip(out_refs, grads[1:]):
                o_ref[rows, :] = g
            return carry

        lax.fori_loop(0, per_blk, chunk, 0)
        finish()

    blk = pl.BlockSpec((CHUNK_TB, D), lambda i: (nblk - 1 - i, 0))
    any_spec = pl.BlockSpec(memory_space=pl.ANY)
    shp = jax.ShapeDtypeStruct((S, D), F32)
    outs = pl.pallas_call(
        body, name="scan_bwd", grid=(nblk,),
        in_specs=[blk] * 6 + [pl.BlockSpec((per_blk, N_HEADS, HEAD, HEAD), lambda i: (nblk - 1 - i, 0, 0, 0)), blk]
        + [any_spec] * n_x,
        out_specs=[blk] * 6 + [any_spec] * n_x, out_shape=[shp] * 6 + _exchange_shapes(scatter, False),
        scratch_shapes=[pltpu.VMEM((N_HEADS, HEAD, HEAD), F32)] + (_exchange_scratch(n_x) if n_x else []),
        compiler_params=_cparams(("arbitrary",)))(r, lw, k, v, a, b, ckpt, dy, *scatter)
    return outs[:6], outs[6:]


def _ada_partial(c_all, w_shard):
    def body(c_ref, w_ref, o_ref):
        o_ref[...] = jnp.dot(c_ref[...].astype(BF16), w_ref[...].astype(BF16), preferred_element_type=F32)

    vm = pl.BlockSpec(memory_space=pltpu.VMEM)
    return pl.pallas_call(body, name="ada_partial", in_specs=[vm, vm], out_specs=vm,
                          out_shape=jax.ShapeDtypeStruct((N_DEV, w_shard.shape[1]), F32),
                          compiler_params=pltpu.CompilerParams(vmem_limit_bytes=VMEM_LIMIT))(c_all, w_shard)


def _ada_bias(rows, b_ada):
    def body(r_ref, b_ref, o_ref):
        o_ref[...] = r_ref[...] + b_ref[...]

    vm = pl.BlockSpec(memory_space=pltpu.VMEM)
    return pl.pallas_call(body, name="ada_bias", in_specs=[vm, vm], out_specs=vm,
                          out_shape=jax.ShapeDtypeStruct(rows.shape, F32))(rows, b_ada)


def _ada_wgrad(c_cols, d_all):
    def body(c_ref, d_ref, o_ref):
        acc = c_ref[:, 0:1] * d_ref[0:1, :]
        for j in range(1, N_DEV):
            acc = acc + c_ref[:, j:j + 1] * d_ref[j:j + 1, :]
        o_ref[...] = acc

    vm = pl.BlockSpec(memory_space=pltpu.VMEM)
    return pl.pallas_call(body, name="ada_wgrad", in_specs=[vm, vm], out_specs=vm,
                          out_shape=jax.ShapeDtypeStruct((D, d_all.shape[1]), F32),
                          compiler_params=pltpu.CompilerParams(vmem_limit_bytes=VMEM_LIMIT))(c_cols, d_all)


def _exchange(srcs, broadcast, name):
    n = len(srcs)

    def body(*refs):
        start, wait = _exchange_ops(refs[:n], refs[n:2 * n], *refs[2 * n:], broadcast)
        start()
        wait()

    any_spec = pl.BlockSpec(memory_space=pl.ANY)
    return pl.pallas_call(
        body, name=name, out_shape=_exchange_shapes(srcs, broadcast), in_specs=[any_spec] * n, out_specs=[any_spec] * n,
        scratch_shapes=_exchange_scratch(n),
        compiler_params=pltpu.CompilerParams(has_side_effects=True),
    )(*srcs)


def _gather_via_sibling(srcs, name):
    n = len(srcs)

    def body(*refs):
        start, forward, finish = _sibling_gather_ops(refs[:n], refs[n:2 * n], *refs[2 * n:])
        start()
        forward()
        finish()

    any_spec = pl.BlockSpec(memory_space=pl.ANY)
    return pl.pallas_call(
        body, name=name, out_shape=_exchange_shapes(srcs, True), in_specs=[any_spec] * n, out_specs=[any_spec] * n,
        scratch_shapes=_exchange_scratch(n),
        compiler_params=pltpu.CompilerParams(has_side_effects=True),
    )(*srcs)


def _sibling_gather_ops(src_refs, out_refs, send_sems, recv_sems, local_sems):
    n = len(src_refs)
    x, y, c = lax.axis_index("x"), lax.axis_index("y"), lax.axis_index("c")
    me, sibling = (x, y, c), (x, y, 1 - c)
    chips = [(1 - x, y), (x, 1 - y), (1 - x, 1 - y)]

    def slot(px, py, pc):
        return 4 * px + 2 * py + pc

    def copy(i, k, block, to, src=None):
        rows = out_refs[i].at[slot(*block)]
        return pltpu.make_async_remote_copy(
            src_ref=rows if src is None else src, dst_ref=rows, send_sem=send_sems.at[i, k],
            recv_sem=recv_sems.at[i, k], device_id=to, device_id_type=_MESH)

    def local(i):
        return pltpu.make_async_copy(src_refs[i], out_refs[i].at[slot(*me)], local_sems.at[i])

    def first():
        cps = [copy(i, 0, me, sibling, src=src_refs[i]) for i in range(n)]
        return cps + [copy(i, 1 + j, me, (*chip, c), src=src_refs[i]) for j, chip in enumerate(chips) for i in range(n)]

    def passed():
        return [copy(i, 4 + j, (*chip, c), sibling) for j, chip in enumerate(chips) for i in range(n)]

    def start():
        for i in range(n):
            local(i).start()
        for cp in first():
            cp.start()

    def forward():
        for j, chip in enumerate(chips):
            for i in range(n):
                copy(i, 1 + j, (*chip, c), me).wait_recv()
                copy(i, 4 + j, (*chip, c), sibling).start()

    def finish():
        for i in range(n):
            copy(i, 0, sibling, me).wait_recv()
            for j, chip in enumerate(chips):
                copy(i, 4 + j, (*chip, 1 - c), me).wait_recv()
        for cp in first() + passed():
            cp.wait_send()
        for i in range(n):
            local(i).wait()

    return start, forward, finish


def _flags(broadcast, n):
    return [broadcast] * n if isinstance(broadcast, bool) else list(broadcast)


def _exchange_shapes(srcs, broadcast):
    return [jax.ShapeDtypeStruct((N_DEV,) + (s.shape if bc else s.shape[1:]), s.dtype)
            for s, bc in zip(srcs, _flags(broadcast, len(srcs)))]


def _exchange_scratch(n):
    return [pltpu.SemaphoreType.DMA((n, N_DEV)), pltpu.SemaphoreType.DMA((n, N_DEV)), pltpu.SemaphoreType.DMA((n,))]


def _exchange_ops(src_refs, out_refs, send_sems, recv_sems, local_sems, broadcast):
    n = len(src_refs)
    flags = _flags(broadcast, n)
    x, y, c = lax.axis_index("x"), lax.axis_index("y"), lax.axis_index("c")
    me = 4 * x + 2 * y + c

    def block(i, j):
        return src_refs[i] if flags[i] else src_refs[i].at[j]

    def remote(i, d, src_slot, dst_slot):
        px, py, pc = x ^ (d >> 2), y ^ ((d >> 1) & 1), c ^ (d & 1)
        return pltpu.make_async_remote_copy(
            src_ref=block(i, src_slot), dst_ref=out_refs[i].at[dst_slot], send_sem=send_sems.at[i, d],
            recv_sem=recv_sems.at[i, d], device_id=(px, py, pc), device_id_type=_MESH)

    def local(i):
        return pltpu.make_async_copy(block(i, me), out_refs[i].at[me], local_sems.at[i])

    def start():
        for i in range(n):
            local(i).start()
        for d in range(1, N_DEV):
            for i in range(n):
                remote(i, d, me ^ d, me).start()

    def wait():
        for d in range(1, N_DEV):
            for i in range(n):
                remote(i, d, me, me ^ d).wait_recv()
        for d in range(1, N_DEV):
            for i in range(n):
                remote(i, d, me ^ d, me).wait_send()
        for i in range(n):
            local(i).wait()

    return start, wait


def _adamw(w, g, m, v):
    nm = ADAM_B1 * m + (1.0 - ADAM_B1) * g
    nv = ADAM_B2 * v + (1.0 - ADAM_B2) * (g * g)
    m_hat = nm * (1.0 / (1.0 - ADAM_B1 ** ADAM_STEP))
    v_hat = nv * (1.0 / (1.0 - ADAM_B2 ** ADAM_STEP))
    return -ADAM_LR * (m_hat / (jnp.sqrt(v_hat) + ADAM_EPS) + ADAM_WD * w), nm, nv


def _adam_vectors(parts, ws, ms, vs):
    nv = len(ws)
    sizes = [w.shape[1] for w in ws]

    def body(*refs):
        p_ref = refs[0]
        w_refs, m_refs, v_refs = refs[1:1 + nv], refs[1 + nv:1 + 2 * nv], refs[1 + 2 * nv:1 + 3 * nv]
        out_refs = refs[1 + 3 * nv:]
        g_all = p_ref[0]
        for j in range(1, N_DEV):
            g_all = g_all + p_ref[j]
        off = 0
        for i, n in enumerate(sizes):
            g = g_all[:, off:off + n]
            off += -(-n // LANES) * LANES
            delta, new_m, new_v = _adamw(w_refs[i][...], g, m_refs[i][...], v_refs[i][...])
            for o_ref, val in zip(out_refs[4 * i:4 * i + 4], (g, delta, new_m, new_v)):
                o_ref[...] = val

    vm = pl.BlockSpec(memory_space=pltpu.VMEM)
    outs = pl.pallas_call(body, name="adam_replicated", in_specs=[vm] * (1 + 3 * nv), out_specs=[vm] * (4 * nv),
                          out_shape=[jax.ShapeDtypeStruct((1, n), F32) for n in sizes for _ in range(4)])(parts, *ws, *ms, *vs)
    return [outs[4 * i:4 * i + 4] for i in range(nv)]


def _sum_adam(parts, w, m, v, name):
    n_parts, R, C = parts.shape
    fits = [t for t in range(16, R + 1, 16) if R % t == 0 and t * C <= 2504 * LANES]
    if fits:
        tm, tc = max(fits), C
    elif C % (2 * LANES) == 0 and R * C > 2504 * LANES:
        tm, tc = R, 2 * LANES
    else:
        tm, tc = R, C

    def body(p_ref, w_ref, m_ref, v_ref, g_ref, d_ref, nm_ref, nv_ref):
        g = p_ref[0].astype(F32)
        for j in range(1, n_parts):
            g = g + p_ref[j].astype(F32)
        g_ref[...] = g
        d_ref[...], nm_ref[...], nv_ref[...] = _adamw(w_ref[...], g, m_ref[...], v_ref[...])

    blk = pl.BlockSpec((tm, tc), lambda i, j: (i, j))
    shp = jax.ShapeDtypeStruct((R, C), F32)
    return pl.pallas_call(body, name=name, grid=(R // tm, C // tc),
                          in_specs=[pl.BlockSpec((n_parts, tm, tc), lambda i, j: (0, i, j)), blk, blk, blk],
                          out_specs=[blk] * 4, out_shape=[shp] * 4,
                          compiler_params=_cparams(("parallel", "parallel")))(parts, w, m, v)


TRANSPOSED = ("w_in", "w_up")
SHARDED = (("w_ada", 1), ("w_in", 0), ("w2", 1), ("a2", 1), ("g2", 1), ("w_att_out", 1), ("w_rwkv_out", 0),
           ("w_o", 0), ("w_up", 0), ("conv_w", 1), ("w_down", 0))
EARLY, LATE = SHARDED[1:5], SHARDED[5:]
REPLICATED = ("b_ada", "norm1_w", "b_gate", "mu_shift", "w0", "a0", "k_k", "k_a", "r_k", "lnx_w", "lnx_b",
              "norm2_w", "conv_b", "norm_f_w")
WEIGHTS = ("w_ada", "b_ada", "norm1_w", "w_in", "b_gate", "mu_shift", "w0", "w2", "a0", "a2", "g2", "k_k", "k_a", "r_k",
           "lnx_w", "lnx_b", "w_att_out", "w_rwkv_out", "w_o", "norm2_w", "w_up", "conv_w", "conv_b", "w_down", "norm_f_w")


W_IN_RUNS = ((0, C_ATT, ATT_IN), (ATT_IN, C_R, 3 * D), (ATT_IN + 3 * D, C_LORA, LORA_W + LORA_A),
             (ATT_IN + 3 * D + LORA_W + LORA_A, C_LORA + LANES, LORA_G), (ATT_IN + RWKV_IN, C_GA, 2 * D))
W_IN_SHARD = N_IN // N_DEV


def _pad_w_in(w_in_t):
    pieces = [w_in_t[orig:orig + count] for orig, _, count in sorted(W_IN_RUNS, key=lambda run: run[1])]
    pieces.append(jnp.zeros((LORA_PAD - LANES - LORA_G, w_in_t.shape[1]), w_in_t.dtype))
    return jnp.concatenate(pieces, axis=0)


def _w_in_blocks(g):
    blocks = []
    for j in range(N_DEV):
        pieces = []
        for orig, pad, count in W_IN_RUNS:
            lo, hi = max(orig, j * W_IN_SHARD), min(orig + count, (j + 1) * W_IN_SHARD)
            if lo < hi:
                pieces.append(g[pad + lo - orig:pad + hi - orig])
        blocks.append(jnp.concatenate(pieces, axis=0)[None])
    return jnp.concatenate(blocks, axis=0)


def _pad_mu(mu):
    lo = mu[:, 3 * D:]
    mu_l = jnp.concatenate([lo[:, :LORA_W + LORA_A], lo[:, LORA_W + LORA_A:], jnp.zeros((1, LORA_PAD - LANES - LORA_G), mu.dtype)], axis=1)
    return mu[:, :D], mu[:, D:2 * D], mu[:, 2 * D:3 * D], mu_l


def _local_step(x, ada, W, late_shards, target):
    S = x.shape[0]
    W = dict(W)
    G = {}
    sh1, sc1, gt1, sh2, sc2, gt2 = [ada[:, i * D:(i + 1) * D] for i in range(6)]
    h1, rstd1 = _norm_fwd(x, None, None, W["norm1_w"], sc1, sh1, "norm1_fwd")
    w_in_p = _pad_w_in(W["w_in"])
    P = _mm(h1, w_in_p, "nt", F32, "proj_in")

    mu_r, mu_k, mu_v, mu_l = _pad_mu(W["mu_shift"])
    g2p = jnp.pad(W["g2"], ((0, G_PAD - LORA_G), (0, 0)))
    prep_params = [mu_r, mu_k, mu_v, mu_l, W["w0"], W["a0"], W["k_k"], W["k_a"], W["w2"], W["a2"], g2p]
    r_, dec, kmod, v_, aa, bb, gg = _rwkv_prep(P, prep_params)
    y_scan, states, late = _cscan_fwd(r_, dec, kmod, v_, aa, bb, gather=late_shards)
    W.update({n: _full_weight(g, axis) for (n, axis), g in zip(LATE, late)})

    o_g, l_g = zip(*[_att_fwd(P, g) for g in range(len(ATT_PATTERNS))])
    att = _att_combine_fwd(o_g, l_g)
    y_att = _mm(att, W["w_att_out"], "nn", F32, "att_out")
    r_k = W["r_k"].reshape(1, D)
    rw = _rwkv_post(y_scan, r_, kmod, v_, gg, W["lnx_w"], W["lnx_b"], r_k)
    y_rwkv = _mm(rw, W["w_rwkv_out"], "nn", F32, "rwkv_out")

    bga, bgr = W["b_gate"][:, :D], W["b_gate"][:, D:]
    mix = _gate_fwd(P, bga, bgr, y_att, y_rwkv)
    mo = _mm(mix, W["w_o"], "nn", F32, "mix_out")
    x2, h2, rstd2 = _norm_fwd(x, mo, gt1, W["norm2_w"], sc2, sh2, "norm2_fwd")
    u = _mm(h2, W["w_up"], "nt", BF16, "ffn_up")
    conv_w8 = jnp.pad(W["conv_w"], ((0, SUBLANES - 3), (0, 0)))
    act = _conv_fwd(u, conv_w8, W["conv_b"])
    f = _mm(act, W["w_down"], "nn", F32, "ffn_down")
    loss_blk, dx3, df, dgt2, G["norm_f_w"] = _final(x2, f, gt2, W["norm_f_w"], target)
    loss = loss_blk[0, 0]

    dact = _mm(df, W["w_down"], "nt", BF16, "ffn_down_dx")
    G["w_down"] = _mm(act, df, "tn", BF16, "ffn_down_dw")
    duc, dwg, dwv, dbg, dbv = _conv_bwd_a(dact, u, conv_w8, W["conv_b"])
    G["conv_w"] = jnp.concatenate([dwg[0:3], dwv[0:3]], axis=1)
    G["conv_b"] = jnp.concatenate([dbg, dbv], axis=1)
    du = _conv_bwd_b(duc, conv_w8)
    dh2 = _mm(du, W["w_up"], "nn", F32, "ffn_up_dx")
    G["w_up"] = _mm(du, h2, "tn", BF16, "ffn_up_dw")
    dx2, dsh2, dsc2, G["norm2_w"], dmo, dgt1 = _norm_bwd(dh2, x2, rstd2, W["norm2_w"], sc2, dx3, mo, gt1, "norm2_bwd")
    dmix = _mm(dmo, W["w_o"], "nt", F32, "mix_out_dx")
    G["w_o"] = _mm(mix, dmo, "tn", BF16, "mix_out_dw")
    dy_att, dy_rwkv, dpga, dpgr, dbga, dbgr = _gate_bwd(dmix, P, bga, bgr, y_att, y_rwkv)
    G["b_gate"] = jnp.concatenate([dbga, dbgr], axis=1)

    datt = _mm(dy_att, W["w_att_out"], "nt", F32, "att_out_dx")
    G["w_att_out"] = _mm(att, dy_att, "tn", BF16, "att_out_dw")
    dcomb = _att_combine_bwd(datt, o_g, l_g)
    dp_att = []
    for g in range(len(ATT_PATTERNS)):
        dp_att += _att_bwd(P, o_g[g], l_g[g], dcomb[g], dcomb[3 + g], g)

    drw = _mm(dy_rwkv, W["w_rwkv_out"], "nt", F32, "rwkv_out_dx")
    G["w_rwkv_out"] = _mm(rw, dy_rwkv, "tn", BF16, "rwkv_out_dw")
    dy_scan, dr1, dk1, dv1, dgg, G["lnx_w"], G["lnx_b"], drk = _rwkv_post_bwd(drw, y_scan, r_, kmod, v_, gg, W["lnx_w"], W["lnx_b"], r_k)
    G["r_k"] = drk.reshape(W["r_k"].shape)
    late_blocks = [_owner_blocks(G[n], axis) for n, axis in LATE] if late_shards else []
    (dr2, ddec, dk2, dv2, daa, dbb), late_parts = _cscan_bwd(r_, dec, kmod, v_, aa, bb, states, dy_scan, scatter=late_blocks)
    pb = _rwkv_prep_bwd(P, prep_params, [dr2, ddec, dk2, dv2, daa, dbb, dgg], [dr1, None, dk1, dv1, None, None, None])
    dp_rkv, dp_lora, dpar = list(pb[0:3]), pb[3], pb[4:]
    dmu_r, dmu_k, dmu_v, dmu_l, G["w0"], G["a0"], G["k_k"], G["k_a"], G["w2"], G["a2"], dg2p = dpar
    G["g2"] = dg2p[0:LORA_G]
    G["mu_shift"] = jnp.concatenate([dmu_r, dmu_k, dmu_v, dmu_l[:, :LORA_W + LORA_A], dmu_l[:, LANES:LANES + LORA_G]], axis=1)

    dP = jnp.concatenate(dp_rkv + [dpga, dpgr] + dp_att + [dp_lora], axis=1)
    G["w_in"] = _w_in_blocks(_mm(dP, h1, "tn", BF16, "proj_in_dw"))
    if late_shards:
        dh1, (w_in_parts,) = _mm(dP, w_in_p, "nn", F32, "proj_in_dx", scatter=[G["w_in"]])
        done = dict(zip([n for n, _ in LATE] + ["w_in"], list(late_parts) + [w_in_parts]))
    else:
        dh1, done = _mm(dP, w_in_p, "nn", F32, "proj_in_dx"), {}
    grad_x, dsh1, dsc1, G["norm1_w"] = _norm_bwd(dh1, x, rstd1, W["norm1_w"], sc1, dx2, None, None, "norm1_bwd")
    dada = jnp.concatenate([dsh1, dsc1, dgt1, dsh2, dsc2, dgt2], axis=1)
    G["b_ada"] = dada
    return loss, grad_x, G, done


def _full_weight(gathered, axis):
    _, rows, cols = gathered.shape
    if axis == 0:
        return gathered.reshape(N_DEV * rows, cols)
    return gathered.transpose(1, 0, 2).reshape(rows, N_DEV * cols)


def _owner_blocks(g, axis):
    rows, cols = g.shape
    g = g.astype(BF16)
    if axis == 0:
        return g.reshape(N_DEV, rows // N_DEV, cols)
    return g.reshape(rows, N_DEV, cols // N_DEV).transpose(1, 0, 2)


def kernel(x, c, w_ada, b_ada, norm1_w, w_in, b_gate, mu_shift, w0, w2, a0, a2, g2, k_k, k_a, r_k, lnx_w, lnx_b, w_att_out, w_rwkv_out, w_o, norm2_w, w_up, conv_w, conv_b, w_down, norm_f_w, loss_target, m_w_ada, m_b_ada, m_norm1_w, m_w_in, m_b_gate, m_mu_shift, m_w0, m_w2, m_a0, m_a2, m_g2, m_k_k, m_k_a, m_r_k, m_lnx_w, m_lnx_b, m_w_att_out, m_w_rwkv_out, m_w_o, m_norm2_w, m_w_up, m_conv_w, m_conv_b, m_w_down, m_norm_f_w, v_w_ada, v_b_ada, v_norm1_w, v_w_in, v_b_gate, v_mu_shift, v_w0, v_w2, v_a0, v_a2, v_g2, v_k_k, v_k_a, v_r_k, v_lnx_w, v_lnx_b, v_w_att_out, v_w_rwkv_out, v_w_o, v_norm2_w, v_w_up, v_conv_w, v_conv_b, v_w_down, v_norm_f_w):
    env = dict(locals())
    w_shard = {n: env[n] for n in WEIGHTS}
    m_shard = {n: env["m_" + n] for n in WEIGHTS}
    v_shard = {n: env["v_" + n] for n in WEIGHTS}

    def mat(shards, n):
        return jnp.swapaxes(shards[n][0], 0, 1) if n in TRANSPOSED else shards[n][0]

    c_all, *gathered = _gather_via_sibling([c] + [mat(w_shard, n).astype(BF16) for n, _ in EARLY], "gather_weights")
    c_all = c_all.reshape(N_DEV, D)
    W = {n: _full_weight(g, axis) for (n, axis), g in zip(EARLY, gathered)}
    for n in REPLICATED:
        W[n] = w_shard[n].reshape(1, -1) if n != "r_k" else w_shard[n][0]
    ada_cols = _ada_partial(c_all, w_shard["w_ada"][0])
    ada_rows, = _exchange([ada_cols[:, None, :]], False, "ada_rows")
    ada = _ada_bias(ada_rows.reshape(1, -1), w_shard["b_ada"])

    late_shards = [mat(w_shard, n).astype(BF16) for n, _ in LATE]
    loss, grad_x, G, parts = _local_step(x[0], ada, W, late_shards, loss_target[0])
    loss = lax.psum(loss, ("x", "y", "c"))

    row = lambda a: a.reshape(1, -1)
    small = jnp.concatenate([jnp.pad(row(G[n]), ((0, 0), (0, (-G[n].size) % LANES))) for n in REPLICATED], axis=1)
    sparts, dada_all = _exchange([small, G["b_ada"].reshape(N_DEV, 1, -1)], [True, False], "gather_small_grads")
    parts["w_ada"] = _ada_wgrad(c_all.T, dada_all.reshape(N_DEV, -1))[None]

    rest = [(n, axis) for n, axis in SHARDED if n not in parts]
    parts.update(zip([n for n, _ in rest], _exchange([_owner_blocks(G[n], axis) for n, axis in rest], False, "scatter_grads")))
    out = {}
    for n, p in parts.items():
        res = _sum_adam(p, mat(w_shard, n), mat(m_shard, n), mat(v_shard, n), "adam_" + n)
        if n in TRANSPOSED:
            res = [jnp.swapaxes(a, 0, 1) for a in res]
        for kind, a in zip(("grad", "delta", "new_m", "new_v"), res):
            out[kind, n] = a[None]

    res = _adam_vectors(sparts, *[[row(s[n]) for n in REPLICATED] for s in (w_shard, m_shard, v_shard)])
    for n, four in zip(REPLICATED, res):
        for kind, a in zip(("grad", "delta", "new_m", "new_v"), four):
            out[kind, n] = a.reshape(w_shard[n].shape)

    return (loss, grad_x[None], *[out[kind, n] for kind in ("grad", "delta", "new_m", "new_v") for n in WEIGHTS])
```
